```python
import jax, jax.numpy as jnp
from jax import lax
import numpy as np

D_MODEL = 2048
BATCH = 8
SEQ = 4096
DEPTH = 2

D_MIX = D_MODEL
GROUP_W = D_MIX // 4
EPS = 1e-6
ROPE_THETA = 10000.0

RET_HEADS = 4
RET_HD = GROUP_W // RET_HEADS
RET_CHUNK = 128

GLA_HEADS = 4
GLA_DK = GROUP_W // 2 // GLA_HEADS
GLA_DV = GROUP_W // GLA_HEADS
GLA_GATE_RANK = 16
GLA_TAU = 16.0
GLA_CHUNK = 64

POOL_GROUPS = 4
POOL_GW = GROUP_W // POOL_GROUPS
POOL_WINDOWS = (2, 4, 8, 16)

MLA_HEADS = 4
MLA_NOPE = 128
MLA_ROPE = 64
MLA_V = GROUP_W // MLA_HEADS
MLA_Q_RANK = D_MODEL // 4
MLA_KV_RANK = D_MODEL // 8
MLA_QBLOCK = 128

SPLIT_SIZES = (
    GROUP_W, GROUP_W, GROUP_W, GROUP_W,
    GLA_HEADS * GLA_DK, GLA_HEADS * GLA_DK, GROUP_W, GROUP_W,
    GLA_GATE_RANK, GLA_GATE_RANK,
    GROUP_W, GROUP_W,
    MLA_Q_RANK, MLA_KV_RANK, MLA_ROPE, GROUP_W,
)
IN_COLS = sum(SPLIT_SIZES)

kernel_name = "bidir_hybrid_headgroup_block"


def _rmsnorm(x, g):
    xf = x.astype(jnp.float32)
    y = xf * lax.rsqrt(jnp.mean(xf * xf, axis=-1, keepdims=True) + EPS)
    return y * g.astype(jnp.float32)


def _rope_tables(dim, s):
    inv = 1.0 / (ROPE_THETA ** (jnp.arange(0, dim, 2, dtype=jnp.float32) / dim))
    ang = jnp.arange(s, dtype=jnp.float32)[:, None] * inv[None, :]
    return jnp.cos(ang), jnp.sin(ang)


def _rotate(x, cos, sin):
    half = x.shape[-1] // 2
    x1, x2 = x[..., :half], x[..., half:]
    return jnp.concatenate([x1 * cos - x2 * sin, x2 * cos + x1 * sin], axis=-1)


def _heads(t, h, d):
    b, s, _ = t.shape
    return t.reshape(b, s, h, d).transpose(0, 2, 1, 3)


def _head_rmsnorm(o, g):
    b, h, s, d = o.shape
    t = _rmsnorm(o.transpose(0, 2, 1, 3), g.reshape(h, d))
    return t.reshape(b, s, h * d)


def _retention_scan(q, k, v, gamma):
    b, h, s, dk = q.shape
    dv = v.shape[-1]
    c = RET_CHUNK
    n = s // c
    q = q.reshape(b, h, n, c, dk)
    k = k.reshape(b, h, n, c, dk)
    v = v.reshape(b, h, n, c, dv)
    log_g = jnp.log(gamma)
    idx = jnp.arange(c, dtype=jnp.float32)
    diff = idx[:, None] - idx[None, :]
    dmat = jnp.where(diff >= 0, jnp.exp(jnp.maximum(diff, 0.0)[None] * log_g[:, None, None]), 0.0)
    scores = jnp.einsum('bhnid,bhnjd->bhnij', q, k) * dmat[None, :, None]
    o_inner = jnp.einsum('bhnij,bhnje->bhnie', scores, v)
    k_dec = k * jnp.exp((c - 1 - idx)[None, :] * log_g[:, None])[None, :, None, :, None]
    kv = jnp.einsum('bhncd,bhnce->bhnde', k_dec, v)
    chunk_decay = jnp.exp(c * log_g)[:, None, None]

    def step(state, kv_n):
        return chunk_decay * state + kv_n, state

    _, s_before = lax.scan(step, jnp.zeros((b, h, dk, dv), jnp.float32), jnp.moveaxis(kv, 2, 0))
    s_before = jnp.moveaxis(s_before, 0, 2)
    q_dec = q * jnp.exp((idx + 1.0)[None, :] * log_g[:, None])[None, :, None, :, None]
    o_cross = jnp.einsum('bhncd,bhnde->bhnce', q_dec, s_before)
    return (o_inner + o_cross).reshape(b, h, s, dv)


def _gla_scan(q, k, v, log_a):
    b, h, s, dk = q.shape
    dv = v.shape[-1]
    c = GLA_CHUNK
    n = s // c
    q = q.reshape(b, h, n, c, dk)
    k = k.reshape(b, h, n, c, dk)
    v = v.reshape(b, h, n, c, dv)
    cum = jnp.cumsum(log_a.reshape(b, h, n, c, dk), axis=3)
    q_t = q * jnp.exp(cum)
    k_t = k * jnp.exp(-cum)
    mask = jnp.tril(jnp.ones((c, c), dtype=bool))
    scores = jnp.where(mask, jnp.einsum('bhnid,bhnjd->bhnij', q_t, k_t), 0.0)
    o_inner = jnp.einsum('bhnij,bhnje->bhnie', scores, v)
    last = cum[..., -1:, :]
    kv = jnp.einsum('bhncd,bhnce->bhnde', k * jnp.exp(last - cum), v)
    decay = jnp.exp(last[..., 0, :])

    def step(state, inp):
        kv_n, d_n = inp
        return d_n[..., None] * state + kv_n, state

    _, s_before = lax.scan(step, jnp.zeros((b, h, dk, dv), jnp.float32),
                           (jnp.moveaxis(kv, 2, 0), jnp.moveaxis(decay, 2, 0)))
    s_before = jnp.moveaxis(s_before, 0, 2)
    o_cross = jnp.einsum('bhncd,bhnde->bhnce', q_t, s_before)
    return (o_inner + o_cross).reshape(b, h, s, dv)


def _retention_branch(q, k, v, gate, norm_g, cos, sin):
    qh = _rotate(_heads(q, RET_HEADS, RET_HD), cos, sin)
    kh = _rotate(_heads(k, RET_HEADS, RET_HD), cos, sin) * (RET_HD ** -0.5)
    vh = _heads(v, RET_HEADS, RET_HD)
    gamma_f = 1.0 - 2.0 ** (-5.0 - jnp.arange(RET_HEADS, dtype=jnp.float32))
    gamma_b = gamma_f[::-1]
    o_f = _retention_scan(qh, kh, vh, gamma_f)
    o_b = jnp.flip(_retention_scan(jnp.flip(qh, 2), jnp.flip(kh, 2), jnp.flip(vh, 2), gamma_b), 2)
    return jax.nn.silu(gate) * _head_rmsnorm(o_f + o_b, norm_g)


def _gla_branch(q, k, v, gate, a_f, a_b, wa2_f, ba_f, wa2_b, ba_b, norm_g):
    qh = _heads(q, GLA_HEADS, GLA_DK) * (GLA_DK ** -0.5)
    kh = _heads(k, GLA_HEADS, GLA_DK)
    vh = _heads(v, GLA_HEADS, GLA_DV)
    la_f = _heads(jax.nn.log_sigmoid(a_f @ wa2_f + ba_f) / GLA_TAU, GLA_HEADS, GLA_DK)
    la_b = _heads(jax.nn.log_sigmoid(a_b @ wa2_b + ba_b) / GLA_TAU, GLA_HEADS, GLA_DK)
    o_f = _gla_scan(qh, kh, vh, la_f)
    o_b = jnp.flip(_gla_scan(jnp.flip(qh, 2), jnp.flip(kh, 2), jnp.flip(vh, 2), jnp.flip(la_b, 2)), 2)
    return jax.nn.silu(gate) * _head_rmsnorm(o_f + o_b, norm_g)


def _pool_branch(u, gate, pool_w, pool_scale):
    b, s, _ = u.shape
    ug = u.reshape(b, s, POOL_GROUPS, POOL_GW)
    cs = jnp.concatenate([jnp.zeros((b, 1, POOL_GROUPS, POOL_GW), jnp.float32),
                          jnp.cumsum(ug, axis=1)], axis=1)
    t = jnp.arange(s)
    outs = []
    for g, w_len in enumerate(POOL_WINDOWS):
        lo = jnp.clip(t - w_len // 2, 0, s)
        hi = jnp.clip(t + w_len // 2, 0, s)
        csg = cs[:, :, g]
        win_sum = jnp.take(csg, hi, axis=1) - jnp.take(csg, lo, axis=1)
        cnt = (hi - lo).astype(jnp.float32)[None, :, None]
        outs.append(win_sum / cnt - ug[:, :, g])
    pooled = jnp.stack(outs, axis=2)
    mixed = jnp.einsum('bsgc,gcd->bsgd', pooled, pool_w).reshape(b, s, GROUP_W)
    return jax.nn.silu(gate) * (mixed * pool_scale)


def _mla_branch(q_lat, kv_lat, k_pe, gate, q_norm_g, wq_b, kv_norm_g, wkv_b, qn_g, kn_g, cos, sin):
    b, s, _ = q_lat.shape
    q = (_rmsnorm(q_lat, q_norm_g) @ wq_b).reshape(b, s, MLA_HEADS, MLA_NOPE + MLA_ROPE)
    kv = (_rmsnorm(kv_lat, kv_norm_g) @ wkv_b).reshape(b, s, MLA_HEADS, MLA_NOPE + MLA_V)
    k_nope, v = kv[..., :MLA_NOPE], kv[..., MLA_NOPE:]
    k = jnp.concatenate([k_nope, jnp.broadcast_to(k_pe[:, :, None, :], (b, s, MLA_HEADS, MLA_ROPE))], axis=-1)
    q = _rmsnorm(q, qn_g)
    k = _rmsnorm(k, kn_g)
    c, sn = cos[:, None, :], sin[:, None, :]
    q = jnp.concatenate([q[..., :MLA_NOPE], _rotate(q[..., MLA_NOPE:], c, sn)], axis=-1)
    k = jnp.concatenate([k[..., :MLA_NOPE], _rotate(k[..., MLA_NOPE:], c, sn)], axis=-1)
    q = q.transpose(0, 2, 1, 3)
    k = k.transpose(0, 2, 1, 3)
    v = v.transpose(0, 2, 1, 3)
    scale = (MLA_NOPE + MLA_ROPE) ** -0.5
    nb = s // MLA_QBLOCK
    qb = q.reshape(b, MLA_HEADS, nb, MLA_QBLOCK, MLA_NOPE + MLA_ROPE).transpose(2, 0, 1, 3, 4)

    def attend(q_blk):
        sc = jnp.einsum('bhqd,bhkd->bhqk', q_blk, k).astype(jnp.float32) * scale
        p = jax.nn.softmax(sc, axis=-1)
        return jnp.einsum('bhqk,bhkd->bhqd', p, v)

    o = lax.map(attend, qb)
    o = o.transpose(1, 0, 3, 2, 4).reshape(b, s, MLA_HEADS * MLA_V)
    return jax.nn.silu(gate) * o


def _fwd_setup_inputs(seed: int = 0) -> dict:
    key = jax.random.key(seed)
    ks = jax.random.split(key, 20)
    f32 = jnp.float32

    def nrm(k, shape, scale):
        return jax.random.normal(k, shape, f32) * scale

    def gain(k, shape):
        return 1.0 + 0.02 * jax.random.normal(k, shape, f32)

    return {
        "x": jax.random.normal(ks[0], (BATCH, SEQ, D_MODEL), f32),
        "norm_g": gain(ks[1], (DEPTH, D_MODEL)),
        "w_in": nrm(ks[2], (DEPTH, D_MODEL, IN_COLS), D_MODEL ** -0.5),
        "ret_norm_g": gain(ks[3], (DEPTH, GROUP_W)),
        "gla_wa2_f": nrm(ks[4], (DEPTH, GLA_GATE_RANK, GLA_HEADS * GLA_DK), GLA_GATE_RANK ** -0.5),
        "gla_ba_f": nrm(ks[5], (DEPTH, GLA_HEADS * GLA_DK), 0.1),
        "gla_wa2_b": nrm(ks[6], (DEPTH, GLA_GATE_RANK, GLA_HEADS * GLA_DK), GLA_GATE_RANK ** -0.5),
        "gla_ba_b": nrm(ks[7], (DEPTH, GLA_HEADS * GLA_DK), 0.1),
        "gla_norm_g": gain(ks[8], (DEPTH, GROUP_W)),
        "pool_w": nrm(ks[9], (DEPTH, POOL_GROUPS, POOL_GW, POOL_GW), POOL_GW ** -0.5),
        "pool_scale": gain(ks[10], (DEPTH, GROUP_W)),
        "mla_q_norm_g": gain(ks[11], (DEPTH, MLA_Q_RANK)),
        "mla_wq_b": nrm(ks[12], (DEPTH, MLA_Q_RANK, MLA_HEADS * (MLA_NOPE + MLA_ROPE)), MLA_Q_RANK ** -0.5),
        "mla_kv_norm_g": gain(ks[13], (DEPTH, MLA_KV_RANK)),
        "mla_wkv_b": nrm(ks[14], (DEPTH, MLA_KV_RANK, MLA_HEADS * (MLA_NOPE + MLA_V)), MLA_KV_RANK ** -0.5),
        "mla_qk_norm_q": gain(ks[15], (DEPTH, MLA_NOPE + MLA_ROPE)),
        "mla_qk_norm_k": gain(ks[16], (DEPTH, MLA_NOPE + MLA_ROPE)),
        "w_out": nrm(ks[17], (DEPTH, D_MIX, D_MODEL), D_MIX ** -0.5),
    }


def _fwd_reference(x, norm_g, w_in, ret_norm_g, gla_wa2_f, gla_ba_f, gla_wa2_b, gla_ba_b, gla_norm_g,
              pool_w, pool_scale, mla_q_norm_g, mla_wq_b, mla_kv_norm_g, mla_wkv_b,
              mla_qk_norm_q, mla_qk_norm_k, w_out):
    s = x.shape[1]
    cos_r, sin_r = _rope_tables(RET_HD, s)
    cos_m, sin_m = _rope_tables(MLA_ROPE, s)
    points = [int(p) for p in np.cumsum(SPLIT_SIZES)[:-1]]
    for l in range(DEPTH):
        h = _rmsnorm(x, norm_g[l]).astype(x.dtype)
        z = (h @ w_in[l]).astype(jnp.float32)
        (rq, rk, rv, rg, gq, gk, gv, gg, ga_f, ga_b, pv, pg, mq, mkv, mkr, mg) = jnp.split(z, points, axis=-1)
        y_a = _retention_branch(rq, rk, rv, rg, ret_norm_g[l], cos_r, sin_r)
        y_b = _gla_branch(gq, gk, gv, gg, ga_f, ga_b, gla_wa2_f[l], gla_ba_f[l],
                          gla_wa2_b[l], gla_ba_b[l], gla_norm_g[l])
        y_c = _pool_branch(pv, pg, pool_w[l], pool_scale[l])
        y_d = _mla_branch(mq, mkv, mkr, mg, mla_q_norm_g[l], mla_wq_b[l], mla_kv_norm_g[l], mla_wkv_b[l],
                          mla_qk_norm_q[l], mla_qk_norm_k[l], cos_m, sin_m)
        y = jnp.concatenate([y_a, y_b, y_c, y_d], axis=-1).astype(x.dtype)
        x = x + y @ w_out[l]
    return x


import jax as _jax
import jax.numpy as _jnp

TWIN_FORMAT = 'train_step'
FWD_PARAMS = ['x', 'norm_g', 'w_in', 'ret_norm_g', 'gla_wa2_f', 'gla_ba_f', 'gla_wa2_b', 'gla_ba_b', 'gla_norm_g', 'pool_w', 'pool_scale', 'mla_q_norm_g', 'mla_wq_b', 'mla_kv_norm_g', 'mla_wkv_b', 'mla_qk_norm_q', 'mla_qk_norm_k', 'w_out']
TWIN_WEIGHTS = ['norm_g', 'w_in', 'ret_norm_g', 'gla_wa2_f', 'gla_ba_f', 'gla_wa2_b', 'gla_ba_b', 'gla_norm_g', 'pool_w', 'pool_scale', 'mla_q_norm_g', 'mla_wq_b', 'mla_kv_norm_g', 'mla_wkv_b', 'mla_qk_norm_q', 'mla_qk_norm_k', 'w_out']
TWIN_DIFF_INPUT = 'x'
TWIN_INPUTS = ['x', 'norm_g', 'w_in', 'ret_norm_g', 'gla_wa2_f', 'gla_ba_f', 'gla_wa2_b', 'gla_ba_b', 'gla_norm_g', 'pool_w', 'pool_scale', 'mla_q_norm_g', 'mla_wq_b', 'mla_kv_norm_g', 'mla_wkv_b', 'mla_qk_norm_q', 'mla_qk_norm_k', 'w_out', 'loss_target', 'm_norm_g', 'm_w_in', 'm_ret_norm_g', 'm_gla_wa2_f', 'm_gla_ba_f', 'm_gla_wa2_b', 'm_gla_ba_b', 'm_gla_norm_g', 'm_pool_w', 'm_pool_scale', 'm_mla_q_norm_g', 'm_mla_wq_b', 'm_mla_kv_norm_g', 'm_mla_wkv_b', 'm_mla_qk_norm_q', 'm_mla_qk_norm_k', 'm_w_out', 'v_norm_g', 'v_w_in', 'v_ret_norm_g', 'v_gla_wa2_f', 'v_gla_ba_f', 'v_gla_wa2_b', 'v_gla_ba_b', 'v_gla_norm_g', 'v_pool_w', 'v_pool_scale', 'v_mla_q_norm_g', 'v_mla_wq_b', 'v_mla_kv_norm_g', 'v_mla_wkv_b', 'v_mla_qk_norm_q', 'v_mla_qk_norm_k', 'v_w_out']
TWIN_OUTPUTS = ['loss', 'grad_x', 'grad_norm_g', 'grad_w_in', 'grad_ret_norm_g', 'grad_gla_wa2_f', 'grad_gla_ba_f', 'grad_gla_wa2_b', 'grad_gla_ba_b', 'grad_gla_norm_g', 'grad_pool_w', 'grad_pool_scale', 'grad_mla_q_norm_g', 'grad_mla_wq_b', 'grad_mla_kv_norm_g', 'grad_mla_wkv_b', 'grad_mla_qk_norm_q', 'grad_mla_qk_norm_k', 'grad_w_out', 'delta_norm_g', 'delta_w_in', 'delta_ret_norm_g', 'delta_gla_wa2_f', 'delta_gla_ba_f', 'delta_gla_wa2_b', 'delta_gla_ba_b', 'delta_gla_norm_g', 'delta_pool_w', 'delta_pool_scale', 'delta_mla_q_norm_g', 'delta_mla_wq_b', 'delta_mla_kv_norm_g', 'delta_mla_wkv_b', 'delta_mla_qk_norm_q', 'delta_mla_qk_norm_k', 'delta_w_out', 'new_m_norm_g', 'new_m_w_in', 'new_m_ret_norm_g', 'new_m_gla_wa2_f', 'new_m_gla_ba_f', 'new_m_gla_wa2_b', 'new_m_gla_ba_b', 'new_m_gla_norm_g', 'new_m_pool_w', 'new_m_pool_scale', 'new_m_mla_q_norm_g', 'new_m_mla_wq_b', 'new_m_mla_kv_norm_g', 'new_m_mla_wkv_b', 'new_m_mla_qk_norm_q', 'new_m_mla_qk_norm_k', 'new_m_w_out', 'new_v_norm_g', 'new_v_w_in', 'new_v_ret_norm_g', 'new_v_gla_wa2_f', 'new_v_gla_ba_f', 'new_v_gla_wa2_b', 'new_v_gla_ba_b', 'new_v_gla_norm_g', 'new_v_pool_w', 'new_v_pool_scale', 'new_v_mla_q_norm_g', 'new_v_mla_wq_b', 'new_v_mla_kv_norm_g', 'new_v_mla_wkv_b', 'new_v_mla_qk_norm_q', 'new_v_mla_qk_norm_k', 'new_v_w_out']
TWIN_LEAF_KINDS = {'loss': 'loss', 'grad_x': 'grad_x', 'grad_norm_g': 'grad_w', 'grad_w_in': 'grad_w', 'grad_ret_norm_g': 'grad_w', 'grad_gla_wa2_f': 'grad_w', 'grad_gla_ba_f': 'grad_w', 'grad_gla_wa2_b': 'grad_w', 'grad_gla_ba_b': 'grad_w', 'grad_gla_norm_g': 'grad_w', 'grad_pool_w': 'grad_w', 'grad_pool_scale': 'grad_w', 'grad_mla_q_norm_g': 'grad_w', 'grad_mla_wq_b': 'grad_w', 'grad_mla_kv_norm_g': 'grad_w', 'grad_mla_wkv_b': 'grad_w', 'grad_mla_qk_norm_q': 'grad_w', 'grad_mla_qk_norm_k': 'grad_w', 'grad_w_out': 'grad_w', 'delta_norm_g': 'delta_w', 'delta_w_in': 'delta_w', 'delta_ret_norm_g': 'delta_w', 'delta_gla_wa2_f': 'delta_w', 'delta_gla_ba_f': 'delta_w', 'delta_gla_wa2_b': 'delta_w', 'delta_gla_ba_b': 'delta_w', 'delta_gla_norm_g': 'delta_w', 'delta_pool_w': 'delta_w', 'delta_pool_scale': 'delta_w', 'delta_mla_q_norm_g': 'delta_w', 'delta_mla_wq_b': 'delta_w', 'delta_mla_kv_norm_g': 'delta_w', 'delta_mla_wkv_b': 'delta_w', 'delta_mla_qk_norm_q': 'delta_w', 'delta_mla_qk_norm_k': 'delta_w', 'delta_w_out': 'delta_w', 'new_m_norm_g': 'new_m', 'new_m_w_in': 'new_m', 'new_m_ret_norm_g': 'new_m', 'new_m_gla_wa2_f': 'new_m', 'new_m_gla_ba_f': 'new_m', 'new_m_gla_wa2_b': 'new_m', 'new_m_gla_ba_b': 'new_m', 'new_m_gla_norm_g': 'new_m', 'new_m_pool_w': 'new_m', 'new_m_pool_scale': 'new_m', 'new_m_mla_q_norm_g': 'new_m', 'new_m_mla_wq_b': 'new_m', 'new_m_mla_kv_norm_g': 'new_m', 'new_m_mla_wkv_b': 'new_m', 'new_m_mla_qk_norm_q': 'new_m', 'new_m_mla_qk_norm_k': 'new_m', 'new_m_w_out': 'new_m', 'new_v_norm_g': 'new_v', 'new_v_w_in': 'new_v', 'new_v_ret_norm_g': 'new_v', 'new_v_gla_wa2_f': 'new_v', 'new_v_gla_ba_f': 'new_v', 'new_v_gla_wa2_b': 'new_v', 'new_v_gla_ba_b': 'new_v', 'new_v_gla_norm_g': 'new_v', 'new_v_pool_w': 'new_v', 'new_v_pool_scale': 'new_v', 'new_v_mla_q_norm_g': 'new_v', 'new_v_mla_wq_b': 'new_v', 'new_v_mla_kv_norm_g': 'new_v', 'new_v_mla_wkv_b': 'new_v', 'new_v_mla_qk_norm_q': 'new_v', 'new_v_mla_qk_norm_k': 'new_v', 'new_v_w_out': 'new_v'}


def _forward(args):
    return _fwd_reference(*[args[k] for k in FWD_PARAMS])


def _output_shape():
    def fwd():
        inp = _fwd_setup_inputs(0)
        return _fwd_reference(*[inp[k] for k in FWD_PARAMS])
    out = _jax.eval_shape(fwd)
    return out.shape, out.dtype

N_MICROBATCH = 1
ADAM_LR = 0.001
ADAM_B1 = 0.9
ADAM_B2 = 0.999
ADAM_EPS = 1e-08
ADAM_WD = 0.01
ADAM_STEP = 10
PER_EXAMPLE_BATCH_AXIS = {'x': 0, 'loss_target': 0}
SHARED_INPUTS = []
_WEIGHT_DTYPES = {'norm_g': _jnp.float32, 'w_in': _jnp.float32, 'ret_norm_g': _jnp.float32, 'gla_wa2_f': _jnp.float32, 'gla_ba_f': _jnp.float32, 'gla_wa2_b': _jnp.float32, 'gla_ba_b': _jnp.float32, 'gla_norm_g': _jnp.float32, 'pool_w': _jnp.float32, 'pool_scale': _jnp.float32, 'mla_q_norm_g': _jnp.float32, 'mla_wq_b': _jnp.float32, 'mla_kv_norm_g': _jnp.float32, 'mla_wkv_b': _jnp.float32, 'mla_qk_norm_q': _jnp.float32, 'mla_qk_norm_k': _jnp.float32, 'w_out': _jnp.float32}
MOMENT_SCALE = {'norm_g': 5.706011e+00, 'w_in': 1.373548e-01, 'ret_norm_g': 5.727190e+00, 'gla_wa2_f': 1.417448e-02, 'gla_ba_f': 5.571581e-02, 'gla_wa2_b': 1.438194e-02, 'gla_ba_b': 5.536903e-02, 'gla_norm_g': 5.716476e+00, 'pool_w': 3.639638e-01, 'pool_scale': 4.511520e+00, 'mla_q_norm_g': 1.169776e-02, 'mla_wq_b': 9.997516e-03, 'mla_kv_norm_g': 4.860389e-02, 'mla_wkv_b': 1.187004e-02, 'mla_qk_norm_q': 4.846678e-02, 'mla_qk_norm_k': 4.890838e-02, 'w_out': 1.188877e-01}


def _to_microbatches(a, axis):
    t = _jnp.moveaxis(a, axis, 0)
    t = t.reshape((N_MICROBATCH, t.shape[0] // N_MICROBATCH) + t.shape[1:])
    return _jnp.moveaxis(t, 1, axis + 1)


def setup_inputs(seed: int = 0) -> dict:
    inp = _fwd_setup_inputs(seed)
    key = _jax.random.fold_in(_jax.random.key(seed), 7919)
    shape, _ = _output_shape()
    out = dict(inp)
    out["loss_target"] = _jax.random.normal(_jax.random.fold_in(key, 0), shape, _jnp.float32)
    for i, name in enumerate(TWIN_WEIGHTS):
        w = inp[name].astype(_jnp.float32)
        if MOMENT_SCALE is None:
            s = _jnp.sqrt(_jnp.mean(_jnp.square(w)) + 1e-30)
        else:
            s = MOMENT_SCALE[name]
        km, kv = _jax.random.split(_jax.random.fold_in(key, i + 1))
        out[name] = w
        out["m_" + name] = s * _jax.random.normal(km, w.shape, _jnp.float32)
        out["v_" + name] = (s * s) * _jax.random.uniform(kv, w.shape, _jnp.float32, 0.5, 1.5)
    if N_MICROBATCH > 1:
        for name, axis in PER_EXAMPLE_BATCH_AXIS.items():
            out[name] = _to_microbatches(out[name], axis)
    return {'x': out['x'], 'norm_g': out['norm_g'], 'w_in': out['w_in'], 'ret_norm_g': out['ret_norm_g'], 'gla_wa2_f': out['gla_wa2_f'], 'gla_ba_f': out['gla_ba_f'], 'gla_wa2_b': out['gla_wa2_b'], 'gla_ba_b': out['gla_ba_b'], 'gla_norm_g': out['gla_norm_g'], 'pool_w': out['pool_w'], 'pool_scale': out['pool_scale'], 'mla_q_norm_g': out['mla_q_norm_g'], 'mla_wq_b': out['mla_wq_b'], 'mla_kv_norm_g': out['mla_kv_norm_g'], 'mla_wkv_b': out['mla_wkv_b'], 'mla_qk_norm_q': out['mla_qk_norm_q'], 'mla_qk_norm_k': out['mla_qk_norm_k'], 'w_out': out['w_out'], 'loss_target': out['loss_target'], 'm_norm_g': out['m_norm_g'], 'm_w_in': out['m_w_in'], 'm_ret_norm_g': out['m_ret_norm_g'], 'm_gla_wa2_f': out['m_gla_wa2_f'], 'm_gla_ba_f': out['m_gla_ba_f'], 'm_gla_wa2_b': out['m_gla_wa2_b'], 'm_gla_ba_b': out['m_gla_ba_b'], 'm_gla_norm_g': out['m_gla_norm_g'], 'm_pool_w': out['m_pool_w'], 'm_pool_scale': out['m_pool_scale'], 'm_mla_q_norm_g': out['m_mla_q_norm_g'], 'm_mla_wq_b': out['m_mla_wq_b'], 'm_mla_kv_norm_g': out['m_mla_kv_norm_g'], 'm_mla_wkv_b': out['m_mla_wkv_b'], 'm_mla_qk_norm_q': out['m_mla_qk_norm_q'], 'm_mla_qk_norm_k': out['m_mla_qk_norm_k'], 'm_w_out': out['m_w_out'], 'v_norm_g': out['v_norm_g'], 'v_w_in': out['v_w_in'], 'v_ret_norm_g': out['v_ret_norm_g'], 'v_gla_wa2_f': out['v_gla_wa2_f'], 'v_gla_ba_f': out['v_gla_ba_f'], 'v_gla_wa2_b': out['v_gla_wa2_b'], 'v_gla_ba_b': out['v_gla_ba_b'], 'v_gla_norm_g': out['v_gla_norm_g'], 'v_pool_w': out['v_pool_w'], 'v_pool_scale': out['v_pool_scale'], 'v_mla_q_norm_g': out['v_mla_q_norm_g'], 'v_mla_wq_b': out['v_mla_wq_b'], 'v_mla_kv_norm_g': out['v_mla_kv_norm_g'], 'v_mla_wkv_b': out['v_mla_wkv_b'], 'v_mla_qk_norm_q': out['v_mla_qk_norm_q'], 'v_mla_qk_norm_k': out['v_mla_qk_norm_k'], 'v_w_out': out['v_w_out']}


def _loss(weights, diff, rest, loss_target):
    with _jax.named_scope("forward"):
        args = {**rest, TWIN_DIFF_INPUT: diff, **{k: w.astype(_WEIGHT_DTYPES[k]) for k, w in weights.items()}}
        y = _forward(args)
    with _jax.named_scope("loss_head"):
        err = _jnp.square(y.astype(_jnp.float32) - loss_target)
        return 0.5 * _jnp.sum(_jnp.mean(err, axis=-1)) if err.ndim else 0.5 * err


def _adamw(w, g, m, v):
    m = ADAM_B1 * m + (1.0 - ADAM_B1) * g
    v = ADAM_B2 * v + (1.0 - ADAM_B2) * _jnp.square(g)
    m_hat = m / (1.0 - ADAM_B1 ** ADAM_STEP)
    v_hat = v / (1.0 - ADAM_B2 ** ADAM_STEP)
    delta = -ADAM_LR * (m_hat / (_jnp.sqrt(v_hat) + ADAM_EPS) + ADAM_WD * w)
    return delta, m, v


def reference(x, norm_g, w_in, ret_norm_g, gla_wa2_f, gla_ba_f, gla_wa2_b, gla_ba_b, gla_norm_g, pool_w, pool_scale, mla_q_norm_g, mla_wq_b, mla_kv_norm_g, mla_wkv_b, mla_qk_norm_q, mla_qk_norm_k, w_out, loss_target, m_norm_g, m_w_in, m_ret_norm_g, m_gla_wa2_f, m_gla_ba_f, m_gla_wa2_b, m_gla_ba_b, m_gla_norm_g, m_pool_w, m_pool_scale, m_mla_q_norm_g, m_mla_wq_b, m_mla_kv_norm_g, m_mla_wkv_b, m_mla_qk_norm_q, m_mla_qk_norm_k, m_w_out, v_norm_g, v_w_in, v_ret_norm_g, v_gla_wa2_f, v_gla_ba_f, v_gla_wa2_b, v_gla_ba_b, v_gla_norm_g, v_pool_w, v_pool_scale, v_mla_q_norm_g, v_mla_wq_b, v_mla_kv_norm_g, v_mla_wkv_b, v_mla_qk_norm_q, v_mla_qk_norm_k, v_w_out):
    given = dict(x=x, norm_g=norm_g, w_in=w_in, ret_norm_g=ret_norm_g, gla_wa2_f=gla_wa2_f, gla_ba_f=gla_ba_f, gla_wa2_b=gla_wa2_b, gla_ba_b=gla_ba_b, gla_norm_g=gla_norm_g, pool_w=pool_w, pool_scale=pool_scale, mla_q_norm_g=mla_q_norm_g, mla_wq_b=mla_wq_b, mla_kv_norm_g=mla_kv_norm_g, mla_wkv_b=mla_wkv_b, mla_qk_norm_q=mla_qk_norm_q, mla_qk_norm_k=mla_qk_norm_k, w_out=w_out, loss_target=loss_target, m_norm_g=m_norm_g, m_w_in=m_w_in, m_ret_norm_g=m_ret_norm_g, m_gla_wa2_f=m_gla_wa2_f, m_gla_ba_f=m_gla_ba_f, m_gla_wa2_b=m_gla_wa2_b, m_gla_ba_b=m_gla_ba_b, m_gla_norm_g=m_gla_norm_g, m_pool_w=m_pool_w, m_pool_scale=m_pool_scale, m_mla_q_norm_g=m_mla_q_norm_g, m_mla_wq_b=m_mla_wq_b, m_mla_kv_norm_g=m_mla_kv_norm_g, m_mla_wkv_b=m_mla_wkv_b, m_mla_qk_norm_q=m_mla_qk_norm_q, m_mla_qk_norm_k=m_mla_qk_norm_k, m_w_out=m_w_out, v_norm_g=v_norm_g, v_w_in=v_w_in, v_ret_norm_g=v_ret_norm_g, v_gla_wa2_f=v_gla_wa2_f, v_gla_ba_f=v_gla_ba_f, v_gla_wa2_b=v_gla_wa2_b, v_gla_ba_b=v_gla_ba_b, v_gla_norm_g=v_gla_norm_g, v_pool_w=v_pool_w, v_pool_scale=v_pool_scale, v_mla_q_norm_g=v_mla_q_norm_g, v_mla_wq_b=v_mla_wq_b, v_mla_kv_norm_g=v_mla_kv_norm_g, v_mla_wkv_b=v_mla_wkv_b, v_mla_qk_norm_q=v_mla_qk_norm_q, v_mla_qk_norm_k=v_mla_qk_norm_k, v_w_out=v_w_out)
    weights = {n: given[n] for n in TWIN_WEIGHTS}
    shared = {n: given[n] for n in SHARED_INPUTS}
    per_example = {n: given[n] for n in ['x']}
    grad_fn = _jax.value_and_grad(_loss, argnums=(0, 1))

    def one_microbatch(ex, loss_target):
        ex = dict(ex)
        diff = ex.pop(TWIN_DIFF_INPUT)
        return grad_fn(weights, diff, {**shared, **ex}, loss_target)

    if N_MICROBATCH == 1:
        loss, (grad_w, grad_x) = one_microbatch(per_example, given["loss_target"])
    else:
        def body(carry, xs):
            loss_sum, grad_sum = carry
            l_k, (gw_k, gx_k) = one_microbatch(xs[0], xs[1])
            with _jax.named_scope("update"):
                return (loss_sum + l_k, _jax.tree.map(_jnp.add, grad_sum, gw_k)), gx_k

        init = (_jnp.zeros((), _jnp.float32), _jax.tree.map(_jnp.zeros_like, weights))
        (loss, grad_w), grad_x = _jax.lax.scan(body, init, (per_example, given["loss_target"]))
    with _jax.named_scope("update"):
        delta_w, new_m, new_v = {}, {}, {}
        for n in TWIN_WEIGHTS:
            delta_w[n], new_m[n], new_v[n] = _adamw(weights[n], grad_w[n], given["m_" + n], given["v_" + n])
    return (loss, grad_x, *[grad_w[n] for n in TWIN_WEIGHTS], *[delta_w[n] for n in TWIN_WEIGHTS],
            *[new_m[n] for n in TWIN_WEIGHTS], *[new_v[n] for n in TWIN_WEIGHTS])
```

```python
import functools
import math

import jax
import jax.numpy as jnp
from jax import lax
from jax.experimental import pallas as pl
from jax.experimental.pallas import tpu as pltpu

F32 = jnp.float32
BF16 = jnp.bfloat16
MESH = pl.DeviceIdType.MESH

EPS = 1e-6
ROPE_THETA = 10000.0
DEPTH = 2
N_DEV = 8
N_CHIP = 4

GROUP_W = 512
RET_HEADS = 4
RET_HD = 128
RET_CHUNK = 128
GLA_HEADS = 4
GLA_DK = 64
GLA_DV = 128
GLA_RANK = 16
GLA_TAU = 16.0
GLA_CHUNK = 64
POOL_GROUPS = 4
POOL_GW = 128
POOL_HALO = 8
POOL_TILE = 256
MLA_HEADS = 4
MLA_NOPE = 128
MLA_ROPE = 64
MLA_QK = MLA_NOPE + MLA_ROPE
MLA_QKP = 256
MLA_V = 128
MLA_Q_RANK = 512
MLA_KV_RANK = 256
MLA_SCALE = MLA_QK ** -0.5

ADAM_LR = 0.001
ADAM_B1 = 0.9
ADAM_B2 = 0.999
ADAM_EPS = 1e-08
ADAM_WD = 0.01
ADAM_STEP = 10

VMEM_LIMIT = 56 * 1024 * 1024

SEG = {
    "rq": (0, 512, 0, 512), "rk": (512, 512, 512, 512), "rv": (1024, 512, 1024, 512), "rg": (1536, 512, 1536, 512),
    "gv": (2048, 512, 2560, 512), "gg": (2560, 512, 3072, 512),
    "pv": (3072, 512, 3616, 512), "pg": (3584, 512, 4128, 512),
    "mq": (4096, 512, 4640, 512), "mg": (4608, 512, 5472, 512),
    "gq": (5120, 256, 2048, 256), "gk": (5376, 256, 2304, 256), "mkv": (5632, 256, 5152, 256),
    "ga": (5888, 128, 3584, 32), "mkr": (6016, 128, 5408, 64),
}
SEG_ORDER = ["rq", "rk", "rv", "rg", "gv", "gg", "pv", "pg", "mq", "mg", "gq", "gk", "mkv", "ga", "mkr"]
IN_COLS = 5984
IN_PAD = 6144
ORIG_ORDER = ["rq", "rk", "rv", "rg", "gq", "gk", "gv", "gg", "ga", "pv", "pg", "mq", "mkv", "mkr", "mg"]


def _cparams(*sem):
    return pltpu.CompilerParams(dimension_semantics=tuple(sem), vmem_limit_bytes=VMEM_LIMIT)


def _bf(v):
    return v.astype(BF16)


def _dot(a, b, ca=1, cb=0):
    return lax.dot_general(_bf(a), _bf(b), (((ca,), (cb,)), ((), ())), preferred_element_type=F32)


def _split_dot(a01, x, ca=1, cb=0):
    hi = _bf(x)
    r1 = x - hi.astype(F32)
    mid = _bf(r1)
    lo = _bf(r1 - mid.astype(F32))
    dn = (((ca,), (cb,)), ((), ()))
    a = _bf(a01)
    return (lax.dot_general(a, hi, dn, preferred_element_type=F32)
            + lax.dot_general(a, mid, dn, preferred_element_type=F32)
            + lax.dot_general(a, lo, dn, preferred_element_type=F32))


def _sigmoid(x):
    return 1.0 / (1.0 + jnp.exp(-x))


def _silu_parts(g):
    sg = _sigmoid(g)
    return g * sg, sg * (1.0 + g * (1.0 - sg))


def _matmul(a, b, *, ta=False, tb=False, out_dtype=F32, tm=512, tn=1024, tk=None, add=None, n_outer=True, name):
    m, kdim = (a.shape[1], a.shape[0]) if ta else a.shape
    n = b.shape[0] if tb else b.shape[1]
    tm, tn = min(tm, m), min(tn, n)
    tk = kdim if tk is None else min(tk, kdim)
    assert m % tm == 0 and n % tn == 0 and kdim % tk == 0
    nk = kdim // tk
    ca, cb = (0 if ta else 1), (1 if tb else 0)

    def body(*refs):
        if add is None:
            a_ref, b_ref, o_ref = refs[:3]
            add_ref = None
        else:
            a_ref, b_ref, add_ref, o_ref = refs[:4]
        p = _dot(a_ref[...], b_ref[...], ca, cb)

        def finish(r):
            if add_ref is not None:
                r = r + add_ref[...]
            o_ref[...] = r.astype(out_dtype)

        if nk == 1:
            finish(p)
        else:
            acc = refs[-1]
            k = pl.program_id(2)

            @pl.when(k == 0)
            def _():
                acc[...] = p

            @pl.when(k > 0)
            def _():
                acc[...] += p

            @pl.when(k == nk - 1)
            def _():
                finish(acc[...])

    def ij(g0, g1):
        return (g1, g0) if n_outer else (g0, g1)

    a_spec = (pl.BlockSpec((tk, tm), lambda g0, g1, k: (k, ij(g0, g1)[0])) if ta
              else pl.BlockSpec((tm, tk), lambda g0, g1, k: (ij(g0, g1)[0], k)))
    b_spec = (pl.BlockSpec((tn, tk), lambda g0, g1, k: (ij(g0, g1)[1], k)) if tb
              else pl.BlockSpec((tk, tn), lambda g0, g1, k: (k, ij(g0, g1)[1])))
    o_spec = pl.BlockSpec((tm, tn), lambda g0, g1, k: ij(g0, g1))
    in_specs = [a_spec, b_spec] + ([o_spec] if add is not None else [])
    args = (a, b) + ((add,) if add is not None else ())
    grid = (n // tn, m // tm, nk) if n_outer else (m // tm, n // tn, nk)
    return pl.pallas_call(
        body, name=name, grid=grid, in_specs=in_specs, out_specs=o_spec,
        out_shape=jax.ShapeDtypeStruct((m, n), out_dtype),
        scratch_shapes=[] if nk == 1 else [pltpu.VMEM((tm, tn), F32)],
        compiler_params=_cparams("parallel", "parallel", "arbitrary"),
    )(*args)


def _rmsnorm_fwd(x, g, *, name, tm=256):
    s, d = x.shape
    tm = min(tm, s)

    def body(x_ref, g_ref, h_ref):
        xv = x_ref[...]
        r = lax.rsqrt(jnp.mean(xv * xv, axis=-1, keepdims=True) + EPS)
        h_ref[...] = _bf(xv * r * g_ref[...])

    return pl.pallas_call(
        body, name=name, grid=(s // tm,),
        in_specs=[pl.BlockSpec((tm, d), lambda i: (i, 0)), pl.BlockSpec((1, d), lambda i: (0, 0))],
        out_specs=pl.BlockSpec((tm, d), lambda i: (i, 0)),
        out_shape=jax.ShapeDtypeStruct((s, d), BF16),
        compiler_params=_cparams("parallel"),
    )(x, g)


def _rmsnorm_bwd(x, dh, g, dres, *, name, tm=256):
    s, d = x.shape
    tm = min(tm, s)

    def body(x_ref, dh_ref, g_ref, dres_ref, dx_ref, dg_ref):
        i = pl.program_id(0)
        xv = x_ref[...]
        r = lax.rsqrt(jnp.mean(xv * xv, axis=-1, keepdims=True) + EPS)
        xn = xv * r
        dv = dh_ref[...]
        part = jnp.sum(dv * xn, axis=0, keepdims=True)

        @pl.when(i == 0)
        def _():
            dg_ref[...] = part

        @pl.when(i > 0)
        def _():
            dg_ref[...] += part

        dxn = dv * g_ref[...]
        dx_ref[...] = dres_ref[...] + r * (dxn - xn * jnp.mean(dxn * xn, axis=-1, keepdims=True))

    row = pl.BlockSpec((tm, d), lambda i: (i, 0))
    vec = pl.BlockSpec((1, d), lambda i: (0, 0))
    return pl.pallas_call(
        body, name=name, grid=(s // tm,), in_specs=[row, row, vec, row], out_specs=[row, vec],
        out_shape=[jax.ShapeDtypeStruct((s, d), F32), jax.ShapeDtypeStruct((1, d), F32)],
        compiler_params=_cparams("arbitrary"),
    )(x, dh, g, dres)


def _loss_head(xf, target, *, name, tm=256):
    s, d = xf.shape
    tm = min(tm, s)

    def body(x_ref, t_ref, dx_ref, l_ref):
        i = pl.program_id(0)
        e = x_ref[...] - t_ref[...]
        dx_ref[...] = e * (1.0 / d)
        rows = jnp.mean(e * e, axis=-1, keepdims=True)
        part = 0.5 * jnp.sum(rows, axis=0, keepdims=True)

        @pl.when(i == 0)
        def _():
            l_ref[...] = part

        @pl.when(i > 0)
        def _():
            l_ref[...] += part

    row = pl.BlockSpec((tm, d), lambda i: (i, 0))
    return pl.pallas_call(
        body, name=name, grid=(s // tm,), in_specs=[row, row],
        out_specs=[row, pl.BlockSpec((1, 1), lambda i: (0, 0))],
        out_shape=[jax.ShapeDtypeStruct((s, d), F32), jax.ShapeDtypeStruct((1, 1), F32)],
        compiler_params=_cparams("arbitrary"),
    )(xf, target)


def _rope_tables(s):
    pos = jnp.arange(s, dtype=F32)[:, None]
    inv_r = 1.0 / (ROPE_THETA ** (jnp.arange(0, RET_HD, 2, dtype=F32) / RET_HD))
    ang = pos * inv_r[None, :]
    ret_cos = jnp.concatenate([jnp.cos(ang), jnp.cos(ang)], axis=1)
    ret_sin = jnp.concatenate([-jnp.sin(ang), jnp.sin(ang)], axis=1)
    inv_m = 1.0 / (ROPE_THETA ** (jnp.arange(0, MLA_ROPE, 2, dtype=F32) / MLA_ROPE))
    am = pos * inv_m[None, :]
    z32, z64 = jnp.zeros((s, 32), F32), jnp.zeros((s, 64), F32)
    mla_cos = jnp.concatenate([jnp.cos(am), jnp.cos(am), z64], axis=1)
    mla_sp = jnp.concatenate([z32, jnp.sin(am), z64], axis=1)
    mla_sn = jnp.concatenate([-jnp.sin(am), z32, z64], axis=1)
    return ret_cos, ret_sin, mla_cos, mla_sp, mla_sn


def _rope128(x, c, sg):
    return x * c + pltpu.roll(x, 64, 1) * sg


def _unrope128(d, c, sg):
    return d * c + pltpu.roll(d * sg, 64, 1)


def _rope64(t, c, sp, sn):
    return t * c + pltpu.roll(t, 96, 1) * sn + pltpu.roll(t, 32, 1) * sp


def _unrope64(d, c, sp, sn):
    return d * c + pltpu.roll(d * sn, 32, 1) + pltpu.roll(d * sp, 96, 1)


def _ret_pre(z, cos, sin, *, name, tm=256):
    s = z.shape[0]
    tm = min(tm, s)
    scale = RET_HD ** -0.5

    def body(q_ref, k_ref, c_ref, s_ref, qo_ref, ko_ref):
        c, sg = c_ref[...], s_ref[...]
        for h in range(RET_HEADS):
            sl = slice(h * RET_HD, (h + 1) * RET_HD)
            qo_ref[:, sl] = _rope128(q_ref[:, sl], c, sg)
            ko_ref[:, sl] = _rope128(k_ref[:, sl], c, sg) * scale

    seg = lambda j: pl.BlockSpec((tm, GROUP_W), lambda i: (i, j))
    tab = pl.BlockSpec((tm, RET_HD), lambda i: (i, 0))
    return pl.pallas_call(
        body, name=name, grid=(s // tm,), in_specs=[seg(0), seg(1), tab, tab],
        out_specs=[seg(0), seg(0)],
        out_shape=[jax.ShapeDtypeStruct((s, GROUP_W), F32)] * 2,
        compiler_params=_cparams("parallel"),
    )(z, z, cos, sin)


def _ret_pre_bwd(dqr, dkr, cos, sin, *, name, tm=256):
    s = dqr.shape[1]
    tm = min(tm, s)
    scale = RET_HD ** -0.5

    def body(dq_ref, dk_ref, c_ref, s_ref, qo_ref, ko_ref):
        c, sg = c_ref[...], s_ref[...]
        for h in range(RET_HEADS):
            sl = slice(h * RET_HD, (h + 1) * RET_HD)
            qo_ref[:, sl] = _bf(_unrope128(dq_ref[0, :, sl] + dq_ref[1, :, sl], c, sg))
            ko_ref[:, sl] = _bf(_unrope128(dk_ref[0, :, sl] + dk_ref[1, :, sl], c, sg) * scale)

    two = pl.BlockSpec((2, tm, GROUP_W), lambda i: (0, i, 0))
    row = pl.BlockSpec((tm, GROUP_W), lambda i: (i, 0))
    tab = pl.BlockSpec((tm, RET_HD), lambda i: (i, 0))
    return pl.pallas_call(
        body, name=name, grid=(s // tm,), in_specs=[two, two, tab, tab], out_specs=[row, row],
        out_shape=[jax.ShapeDtypeStruct((s, GROUP_W), BF16)] * 2,
        compiler_params=_cparams("parallel"),
    )(dqr, dkr, cos, sin)


def _bla(a, b, c, lg, cols, *, name):
    s = a.shape[0]
    ch = min(RET_CHUNK, s)
    n = s // ch
    hd = RET_HD

    def body(lg_ref, a_ref, b_ref, c_ref, o_ref, st):
        d, h, t = pl.program_id(0), pl.program_id(1), pl.program_id(2)
        g = lg_ref[d, h]
        fwd = d == 0

        @pl.when(t == 0)
        def _():
            st[...] = jnp.zeros_like(st)

        av, bv, cv = a_ref[...], b_ref[...], c_ref[...]
        ii = lax.broadcasted_iota(jnp.int32, (ch, ch), 0)
        jj = lax.broadcasted_iota(jnp.int32, (ch, ch), 1)
        diff = jnp.where(fwd, ii - jj, jj - ii).astype(F32)
        dmat = jnp.where(diff >= 0, jnp.exp(jnp.maximum(diff, 0.0) * g), 0.0)
        sc = _dot(av, bv, 1, 1) * dmat
        inner = _dot(sc, cv)
        idx = lax.broadcasted_iota(jnp.int32, (ch, 1), 0).astype(F32)
        pq = jnp.where(fwd, idx + 1.0, ch - idx)
        pk = jnp.where(fwd, ch - 1.0 - idx, idx)
        cross = _dot(av * jnp.exp(pq * g), st[...])
        o_ref[0] = inner + cross
        st[...] = jnp.exp(ch * g) * st[...] + _dot(bv * jnp.exp(pk * g), cv, 0, 0)

    def rows(j):
        return pl.BlockSpec((ch, hd), lambda d, h, t: (t + d * (n - 1 - 2 * t), j * RET_HEADS + h))

    return pl.pallas_call(
        body, name=name, grid=(2, RET_HEADS, n),
        in_specs=[pl.BlockSpec(memory_space=pltpu.SMEM), rows(cols[0]), rows(cols[1]), rows(cols[2])],
        out_specs=pl.BlockSpec((1, ch, hd), lambda d, h, t: (d, t + d * (n - 1 - 2 * t), h)),
        out_shape=jax.ShapeDtypeStruct((2, s, GROUP_W), F32),
        scratch_shapes=[pltpu.VMEM((hd, hd), F32)],
        compiler_params=_cparams("arbitrary", "arbitrary", "arbitrary"),
    )(lg, a, b, c)


def _post(o2, zg, gcol, g, *, norm, name, tm=256):
    s = zg.shape[0]
    tm = min(tm, s)
    nd = o2.shape[0]

    def body(o_ref, gt_ref, g_ref, y_ref):
        silu, _ = _silu_parts(gt_ref[...])
        for h in range(4):
            sl = slice(h * 128, (h + 1) * 128)
            o = o_ref[0, :, sl]
            for k in range(1, nd):
                o = o + o_ref[k, :, sl]
            if norm:
                r = lax.rsqrt(jnp.mean(o * o, axis=-1, keepdims=True) + EPS)
                o = o * r * g_ref[:, sl]
            y_ref[:, sl] = _bf(silu[:, sl] * o)

    return pl.pallas_call(
        body, name=name, grid=(s // tm,),
        in_specs=[pl.BlockSpec((nd, tm, GROUP_W), lambda i: (0, i, 0)),
                  pl.BlockSpec((tm, GROUP_W), lambda i: (i, gcol)),
                  pl.BlockSpec((1, GROUP_W), lambda i: (0, 0))],
        out_specs=pl.BlockSpec((tm, GROUP_W), lambda i: (i, 0)),
        out_shape=jax.ShapeDtypeStruct((s, GROUP_W), BF16),
        compiler_params=_cparams("parallel"),
    )(o2, zg, g)


def _post_bwd(dy, ycol, o2, zg, gcol, g, *, norm, name, tm=256):
    s = zg.shape[0]
    tm = min(tm, s)
    nd = o2.shape[0]

    def body(dy_ref, o_ref, gt_ref, g_ref, dgt_ref, do_ref, dg_ref):
        i = pl.program_id(0)
        silu, dsilu = _silu_parts(gt_ref[...])
        dyv = dy_ref[...]
        parts = []
        for h in range(4):
            sl = slice(h * 128, (h + 1) * 128)
            o = o_ref[0, :, sl]
            for k in range(1, nd):
                o = o + o_ref[k, :, sl]
            dn = dyv[:, sl] * silu[:, sl]
            if norm:
                r = lax.rsqrt(jnp.mean(o * o, axis=-1, keepdims=True) + EPS)
                xn = o * r
                gh = g_ref[:, sl]
                dgt_ref[:, sl] = _bf(dyv[:, sl] * (xn * gh) * dsilu[:, sl])
                parts.append(jnp.sum(dn * xn, axis=0, keepdims=True))
                dxn = dn * gh
                do_ref[:, sl] = r * (dxn - xn * jnp.mean(dxn * xn, axis=-1, keepdims=True))
            else:
                dgt_ref[:, sl] = _bf(dyv[:, sl] * o * dsilu[:, sl])
                parts.append(jnp.zeros((1, 128), F32))
                do_ref[:, sl] = dn
        part = jnp.concatenate(parts, axis=1)

        @pl.when(i == 0)
        def _():
            dg_ref[...] = part

        @pl.when(i > 0)
        def _():
            dg_ref[...] += part

    row = pl.BlockSpec((tm, GROUP_W), lambda i: (i, 0))
    vec = pl.BlockSpec((1, GROUP_W), lambda i: (0, 0))
    return pl.pallas_call(
        body, name=name, grid=(s // tm,),
        in_specs=[pl.BlockSpec((tm, GROUP_W), lambda i: (i, ycol)),
                  pl.BlockSpec((nd, tm, GROUP_W), lambda i: (0, i, 0)),
                  pl.BlockSpec((tm, GROUP_W), lambda i: (i, gcol)), vec],
        out_specs=[row, row, vec],
        out_shape=[jax.ShapeDtypeStruct((s, GROUP_W), BF16), jax.ShapeDtypeStruct((s, GROUP_W), F32),
                   jax.ShapeDtypeStruct((1, GROUP_W), F32)],
        compiler_params=_cparams("arbitrary"),
    )(dy, o2, zg, g)


def _ret_log_gamma(swap):
    gf = 1.0 - 2.0 ** (-5.0 - jnp.arange(RET_HEADS, dtype=F32))
    lf, lb = jnp.log(gf), jnp.log(gf[::-1])
    return jnp.stack([lb, lf] if swap else [lf, lb])


def _log_sigmoid(x):
    return jnp.minimum(x, 0.0) - jnp.log(1.0 + jnp.exp(-jnp.abs(x)))


def _gla_gate(z, wa, ba, *, name, tm=256):
    s = z.shape[0]
    tm = min(tm, s)
    col = SEG["ga"][0] // 128

    def body(ga_ref, wa_ref, ba_ref, la_ref):
        pre = _dot(ga_ref[...], wa_ref[...]) + ba_ref[...]
        la_ref[...] = _log_sigmoid(pre) / GLA_TAU

    return pl.pallas_call(
        body, name=name, grid=(s // tm,),
        in_specs=[pl.BlockSpec((tm, 128), lambda i: (i, col)), pl.BlockSpec((128, 512), lambda i: (0, 0)),
                  pl.BlockSpec((1, 512), lambda i: (0, 0))],
        out_specs=pl.BlockSpec((tm, 512), lambda i: (i, 0)),
        out_shape=jax.ShapeDtypeStruct((s, 512), F32),
        compiler_params=_cparams("parallel"),
    )(z, wa, ba)


def _gla_gate_bwd(dla, z, wa, ba, *, name, tm=256):
    s = z.shape[0]
    tm = min(tm, s)
    col = SEG["ga"][0] // 128

    def body(dla_ref, ga_ref, wa_ref, ba_ref, dga_ref, dwa_ref, dba_ref):
        i = pl.program_id(0)
        gav = ga_ref[...]
        pre = _dot(gav, wa_ref[...]) + ba_ref[...]
        dpre = dla_ref[...] * (1.0 - _sigmoid(pre)) * (1.0 / GLA_TAU)
        dga_ref[...] = _bf(_dot(dpre, wa_ref[...], 1, 1))
        pw = _dot(gav, dpre, 0, 0)
        pb = jnp.sum(dpre, axis=0, keepdims=True)

        @pl.when(i == 0)
        def _():
            dwa_ref[...] = pw
            dba_ref[...] = pb

        @pl.when(i > 0)
        def _():
            dwa_ref[...] += pw
            dba_ref[...] += pb

    return pl.pallas_call(
        body, name=name, grid=(s // tm,),
        in_specs=[pl.BlockSpec((tm, 512), lambda i: (i, 0)), pl.BlockSpec((tm, 128), lambda i: (i, col)),
                  pl.BlockSpec((128, 512), lambda i: (0, 0)), pl.BlockSpec((1, 512), lambda i: (0, 0))],
        out_specs=[pl.BlockSpec((tm, 128), lambda i: (i, 0)), pl.BlockSpec((128, 512), lambda i: (0, 0)),
                   pl.BlockSpec((1, 512), lambda i: (0, 0))],
        out_shape=[jax.ShapeDtypeStruct((s, 128), BF16), jax.ShapeDtypeStruct((128, 512), F32),
                   jax.ShapeDtypeStruct((1, 512), F32)],
        compiler_params=_cparams("arbitrary"),
    )(dla, z, wa, ba)


def _gla_chunk(d, q_ref, k_ref, la_ref, ch):
    fwd = d == 0
    ii = lax.broadcasted_iota(jnp.int32, (ch, ch), 0)
    tt = lax.broadcasted_iota(jnp.int32, (ch, ch), 1)
    tmat = jnp.where(jnp.where(fwd, ii - tt, tt - ii) >= 0, 1.0, 0.0)
    c = _split_dot(tmat, la_ref[0, 0])
    big_l = jnp.where(fwd, c[ch - 1:ch, :], c[0:1, :])
    qv = q_ref[0] * (GLA_DK ** -0.5)
    kv = k_ref[0]
    qt = qv * jnp.exp(c)
    kt = kv * jnp.exp(-c)
    kh = kv * jnp.exp(big_l - c)
    return tmat, c, big_l, qv, kv, qt, kt, kh


def _gla_fwd(qh, kh_, z, la, *, name):
    s = z.shape[0]
    ch = min(GLA_CHUNK, s)
    n = s // ch
    vcol = SEG["gv"][0] // 128

    def body(q_ref, k_ref, v_ref, la_ref, o_ref, zs_ref, st):
        d, t = pl.program_id(0), pl.program_id(2)

        @pl.when(t == 0)
        def _():
            st[...] = jnp.zeros_like(st)

        tmat, c, big_l, qv, kv, qt, kt, kh = _gla_chunk(d, q_ref, k_ref, la_ref, ch)
        vv = v_ref[...]
        p = _dot(qt, kt, 1, 1) * tmat
        zst = st[...]
        o_ref[0] = _dot(p, vv) + _dot(qt, zst, 1, 1)
        zs_ref[0, 0, 0] = zst
        st[...] = zst * jnp.exp(big_l) + _dot(vv, kh, 0, 0)

    cidx = lambda d, t: t + d * (n - 1 - 2 * t)
    hs = pl.BlockSpec((1, ch, GLA_DK), lambda d, h, t: (h, cidx(d, t), 0))
    return pl.pallas_call(
        body, name=name, grid=(2, GLA_HEADS, n),
        in_specs=[hs, hs, pl.BlockSpec((ch, GLA_DV), lambda d, h, t: (cidx(d, t), vcol + h)),
                  pl.BlockSpec((1, 1, ch, GLA_DK), lambda d, h, t: (d, h, cidx(d, t), 0))],
        out_specs=[pl.BlockSpec((1, ch, GLA_DV), lambda d, h, t: (d, cidx(d, t), h)),
                   pl.BlockSpec((1, 1, 1, GLA_DV, GLA_DK), lambda d, h, t: (d, h, cidx(d, t), 0, 0))],
        out_shape=[jax.ShapeDtypeStruct((2, s, GROUP_W), F32),
                   jax.ShapeDtypeStruct((2, GLA_HEADS, n, GLA_DV, GLA_DK), F32)],
        scratch_shapes=[pltpu.VMEM((GLA_DV, GLA_DK), F32)],
        compiler_params=_cparams("arbitrary", "arbitrary", "arbitrary"),
    )(qh, kh_, z, la)


def _gla_bwd(qh, kh_, z, la, do, zs, *, name):
    s = z.shape[0]
    ch = min(GLA_CHUNK, s)
    n = s // ch
    vcol = SEG["gv"][0] // 128

    def body(q_ref, k_ref, v_ref, la_ref, do_ref, zs_ref, dq_ref, dk_ref, dla_ref, dv_ref, gz):
        d, t = pl.program_id(0), pl.program_id(2)

        @pl.when(t == 0)
        def _():
            gz[...] = jnp.zeros_like(gz)

        tmat, c, big_l, qv, kv, qt, kt, kh = _gla_chunk(d, q_ref, k_ref, la_ref, ch)
        vv, dov, zst, gzv = v_ref[...], do_ref[...], zs_ref[0, 0, 0], gz[...]
        p = _dot(qt, kt, 1, 1) * tmat
        dp = _dot(dov, vv, 1, 1) * tmat
        dqt = _dot(dp, kt) + _dot(dov, zst)
        dkt = _dot(dp, qt, 0, 0)
        dkh = _dot(vv, gzv)
        dv_ref[0] = _dot(p, dov, 0, 0) + _dot(kh, gzv, 1, 1)
        ec = jnp.exp(c)
        dq_ref[0, 0] = dqt * ec * (GLA_DK ** -0.5)
        dk_ref[0, 0] = dkt * jnp.exp(-c) + dkh * jnp.exp(big_l - c)
        e_l = jnp.exp(big_l)
        d_l = jnp.sum(dkh * kh, axis=0, keepdims=True) + e_l * jnp.sum(zst * gzv, axis=0, keepdims=True)
        rows = lax.broadcasted_iota(jnp.int32, (ch, 1), 0)
        end = jnp.where(d == 0, ch - 1, 0)
        dc = dqt * qt - dkt * kt - dkh * kh + jnp.where(rows == end, d_l, 0.0)
        dla_ref[0, 0] = _split_dot(tmat, dc, 0, 0)
        gz[...] = gzv * e_l + _dot(dov, qt, 0, 0)

    cidx = lambda d, t: (n - 1 - t) + d * (2 * t - (n - 1))
    hs = pl.BlockSpec((1, ch, GLA_DK), lambda d, h, t: (h, cidx(d, t), 0))
    dhs = pl.BlockSpec((1, 1, ch, GLA_DK), lambda d, h, t: (d, h, cidx(d, t), 0))
    return pl.pallas_call(
        body, name=name, grid=(2, GLA_HEADS, n),
        in_specs=[hs, hs, pl.BlockSpec((ch, GLA_DV), lambda d, h, t: (cidx(d, t), vcol + h)), dhs,
                  pl.BlockSpec((ch, GLA_DV), lambda d, h, t: (cidx(d, t), h)),
                  pl.BlockSpec((1, 1, 1, GLA_DV, GLA_DK), lambda d, h, t: (d, h, cidx(d, t), 0, 0))],
        out_specs=[dhs, dhs, dhs, pl.BlockSpec((1, ch, GLA_DV), lambda d, h, t: (d, cidx(d, t), h))],
        out_shape=[jax.ShapeDtypeStruct((2, GLA_HEADS, s, GLA_DK), F32)] * 3
        + [jax.ShapeDtypeStruct((2, s, GROUP_W), F32)],
        scratch_shapes=[pltpu.VMEM((GLA_DV, GLA_DK), F32)],
        compiler_params=_cparams("arbitrary", "arbitrary", "arbitrary"),
    )(qh, kh_, z, la, do, zs)


def _band(lo, hi, rows, width):
    r = lax.broadcasted_iota(jnp.int32, (rows, width), 0)
    j = lax.broadcasted_iota(jnp.int32, (rows, width), 1)
    k = j - POOL_HALO - r
    return jnp.where((k >= lo) & (k <= hi), 1.0, 0.0)


def _pool_cnt(t0, half, rows, s):
    t = t0 + lax.broadcasted_iota(jnp.int32, (rows, 1), 0)
    return (jnp.minimum(t + half, s) - jnp.maximum(t - half, 0)).astype(F32)


def _pool_fwd(z, pw, scale, *, name):
    s = z.shape[0]
    tl = min(POOL_TILE, s)
    nt = s // tl
    ucol, gcol = SEG["pv"][0] // 128, SEG["pg"][0] // 128

    def body(u_ref, gt_ref, pw_ref, sc_ref, y_ref, pad):
        g = pl.program_id(0)
        half = jnp.left_shift(1, g)
        pad[0:POOL_HALO, :] = jnp.zeros((POOL_HALO, POOL_GW), F32)
        pad[POOL_HALO + s:POOL_HALO + s + POOL_HALO, :] = jnp.zeros((POOL_HALO, POOL_GW), F32)
        pad[POOL_HALO:POOL_HALO + s, :] = u_ref[...]
        band = _band(-half, half - 1, tl, tl + 2 * POOL_HALO)
        pwv, scv = pw_ref[0], sc_ref[...]

        def tile(i, carry):
            t0 = pl.multiple_of(i * tl, tl)
            win = pad[pl.ds(t0, tl + 2 * POOL_HALO), :]
            u = win[POOL_HALO:POOL_HALO + tl, :]
            pooled = _split_dot(band, win) / _pool_cnt(t0, half, tl, s) - u
            mixed = _dot(pooled, pwv)
            silu, _ = _silu_parts(gt_ref[pl.ds(t0, tl), :])
            y_ref[pl.ds(t0, tl), :] = _bf(silu * (mixed * scv))
            return carry

        lax.fori_loop(0, nt, tile, 0)

    return pl.pallas_call(
        body, name=name, grid=(POOL_GROUPS,),
        in_specs=[pl.BlockSpec((s, POOL_GW), lambda g: (0, ucol + g)),
                  pl.BlockSpec((s, POOL_GW), lambda g: (0, gcol + g)),
                  pl.BlockSpec((1, POOL_GW, POOL_GW), lambda g: (g, 0, 0)),
                  pl.BlockSpec((1, POOL_GW), lambda g: (0, g))],
        out_specs=pl.BlockSpec((s, POOL_GW), lambda g: (0, g)),
        out_shape=jax.ShapeDtypeStruct((s, GROUP_W), BF16),
        scratch_shapes=[pltpu.VMEM((s + 2 * POOL_HALO, POOL_GW), F32)],
        compiler_params=_cparams("parallel"),
    )(z, z, pw, scale)


def _pool_bwd(dy, z, pw, scale, *, name):
    s = z.shape[0]
    tl = min(POOL_TILE, s)
    nt = s // tl
    ucol, gcol, ycol = SEG["pv"][0] // 128, SEG["pg"][0] // 128, 2 * GROUP_W // 128

    def body(dy_ref, u_ref, gt_ref, pw_ref, sc_ref, du_ref, dgt_ref, dpw_ref, dsc_ref, pad, epad, dpo):
        g = pl.program_id(0)
        half = jnp.left_shift(1, g)
        zeros = jnp.zeros((POOL_HALO, POOL_GW), F32)
        for buf in (pad, epad):
            buf[0:POOL_HALO, :] = zeros
            buf[POOL_HALO + s:POOL_HALO + s + POOL_HALO, :] = zeros
        pad[POOL_HALO:POOL_HALO + s, :] = u_ref[...]
        band = _band(-half, half - 1, tl, tl + 2 * POOL_HALO)
        band_t = _band(1 - half, half, tl, tl + 2 * POOL_HALO)
        pwv, scv = pw_ref[0], sc_ref[...]
        dpw_ref[0] = jnp.zeros((POOL_GW, POOL_GW), F32)
        dsc_ref[...] = jnp.zeros((1, POOL_GW), F32)

        def tile(i, carry):
            t0 = pl.multiple_of(i * tl, tl)
            win = pad[pl.ds(t0, tl + 2 * POOL_HALO), :]
            u = win[POOL_HALO:POOL_HALO + tl, :]
            cnt = _pool_cnt(t0, half, tl, s)
            pooled = _split_dot(band, win) / cnt - u
            mixed = _dot(pooled, pwv)
            silu, dsilu = _silu_parts(gt_ref[pl.ds(t0, tl), :])
            dyv = dy_ref[pl.ds(t0, tl), :]
            dgt_ref[pl.ds(t0, tl), :] = _bf(dyv * (mixed * scv) * dsilu)
            dsc_ref[...] += jnp.sum(dyv * silu * mixed, axis=0, keepdims=True)
            dm = dyv * silu * scv
            dpw_ref[0] += _dot(pooled, dm, 0, 0)
            dpooled = _dot(dm, pwv, 1, 1)
            dpo[pl.ds(t0, tl), :] = dpooled
            epad[pl.ds(POOL_HALO + t0, tl), :] = dpooled / cnt
            return carry

        lax.fori_loop(0, nt, tile, 0)

        def tile2(i, carry):
            t0 = pl.multiple_of(i * tl, tl)
            ewin = epad[pl.ds(t0, tl + 2 * POOL_HALO), :]
            du_ref[pl.ds(t0, tl), :] = _bf(_split_dot(band_t, ewin) - dpo[pl.ds(t0, tl), :])
            return carry

        lax.fori_loop(0, nt, tile2, 0)

    col = lambda c0: pl.BlockSpec((s, POOL_GW), lambda g: (0, c0 + g))
    return pl.pallas_call(
        body, name=name, grid=(POOL_GROUPS,),
        in_specs=[col(ycol), col(ucol), col(gcol), pl.BlockSpec((1, POOL_GW, POOL_GW), lambda g: (g, 0, 0)),
                  pl.BlockSpec((1, POOL_GW), lambda g: (0, g))],
        out_specs=[col(0), col(0), pl.BlockSpec((1, POOL_GW, POOL_GW), lambda g: (g, 0, 0)),
                   pl.BlockSpec((1, POOL_GW), lambda g: (0, g))],
        out_shape=[jax.ShapeDtypeStruct((s, GROUP_W), BF16), jax.ShapeDtypeStruct((s, GROUP_W), BF16),
                   jax.ShapeDtypeStruct((POOL_GROUPS, POOL_GW, POOL_GW), F32),
                   jax.ShapeDtypeStruct((1, GROUP_W), F32)],
        scratch_shapes=[pltpu.VMEM((s + 2 * POOL_HALO, POOL_GW), F32), pltpu.VMEM((s + 2 * POOL_HALO, POOL_GW), F32),
                        pltpu.VMEM((s, POOL_GW), F32)],
        compiler_params=_cparams("parallel"),
    )(dy, z, z, pw, scale)


def _mla_specs(tm):
    zq = pl.BlockSpec((tm, 512), lambda i: (i, SEG["mq"][0] // 512))
    zkv = pl.BlockSpec((tm, 256), lambda i: (i, SEG["mkv"][0] // 256))
    zkr = pl.BlockSpec((tm, 128), lambda i: (i, SEG["mkr"][0] // 128))
    full = lambda r, c: pl.BlockSpec((r, c), lambda i: (0, 0))
    tab = pl.BlockSpec((tm, 128), lambda i: (i, 0))
    weights = [full(1, 512), full(512, 1024), full(1, 256), full(256, 1024), full(1, 256), full(1, 256)]
    return [zq, zkv, zkr] + weights + [tab, tab, tab]


def _mla_project(xq_ref, xkv_ref, qg_ref, wq_ref, kvg_ref, wkv_ref):
    xq = xq_ref[...]
    r1 = lax.rsqrt(jnp.mean(xq * xq, axis=-1, keepdims=True) + EPS)
    xn1 = xq * r1
    qn = _bf(xn1 * qg_ref[...])
    qraw = _dot(qn, wq_ref[...])
    xkv = xkv_ref[...]
    r2 = lax.rsqrt(jnp.mean(xkv * xkv, axis=-1, keepdims=True) + EPS)
    xn2 = xkv * r2
    kvn = _bf(xn2 * kvg_ref[...])
    kvraw = _dot(kvn, wkv_ref[...])
    return r1, xn1, qn, qraw, r2, xn2, kvn, kvraw


def _mla_pre(z, qg, wq, kvg, wkv, qng, kng, cos, sp, sn, *, name, tm=256):
    s = z.shape[0]
    tm = min(tm, s)

    def body(xq_ref, xkv_ref, pe_ref, qg_ref, wq_ref, kvg_ref, wkv_ref, qng_ref, kng_ref, c_ref, sp_ref, sn_ref,
             q_ref, k_ref, v_ref):
        _, _, _, qraw, _, _, _, kvraw = _mla_project(xq_ref, xkv_ref, qg_ref, wq_ref, kvg_ref, wkv_ref)
        c, spv, snv = c_ref[...], sp_ref[...], sn_ref[...]
        pe = pe_ref[...]
        pe_ss = jnp.sum(pe * pe, axis=-1, keepdims=True)
        qngv, kngv = qng_ref[...], kng_ref[...]
        for h in range(MLA_HEADS):
            b = h * MLA_QKP
            qh = qraw[:, b:b + MLA_QKP]
            r = lax.rsqrt(jnp.sum(qh * qh, axis=-1, keepdims=True) * (1.0 / MLA_QK) + EPS)
            qn_h = qh * r * qngv
            q_ref[:, b:b + 128] = _bf(qn_h[:, :128] * MLA_SCALE)
            q_ref[:, b + 128:b + 256] = _bf(_rope64(qn_h[:, 128:], c, spv, snv) * MLA_SCALE)
            kn = kvraw[:, b:b + 128]
            rk = lax.rsqrt((jnp.sum(kn * kn, axis=-1, keepdims=True) + pe_ss) * (1.0 / MLA_QK) + EPS)
            k_ref[:, b:b + 128] = _bf(kn * rk * kngv[:, :128])
            k_ref[:, b + 128:b + 256] = _bf(_rope64(pe * rk * kngv[:, 128:], c, spv, snv))
            v_ref[:, h * MLA_V:(h + 1) * MLA_V] = _bf(kvraw[:, b + 128:b + 256])

    row = lambda w: pl.BlockSpec((tm, w), lambda i: (i, 0))
    return pl.pallas_call(
        body, name=name, grid=(s // tm,), in_specs=_mla_specs(tm),
        out_specs=[row(1024), row(1024), row(512)],
        out_shape=[jax.ShapeDtypeStruct((s, 1024), BF16), jax.ShapeDtypeStruct((s, 1024), BF16),
                   jax.ShapeDtypeStruct((s, 512), BF16)],
        compiler_params=_cparams("parallel"),
    )(z, z, z, qg, wq, kvg, wkv, qng, kng, cos, sp, sn)


def _mla_pre_bwd(dq, dk, dv, z, qg, wq, kvg, wkv, qng, kng, cos, sp, sn, *, name, tm=256):
    s = z.shape[0]
    tm = min(tm, s)

    def body(dq_ref, dk_ref, dv_ref, xq_ref, xkv_ref, pe_ref, qg_ref, wq_ref, kvg_ref, wkv_ref, qng_ref, kng_ref,
             c_ref, sp_ref, sn_ref, dxq_ref, dxkv_ref, dpe_ref, dwq_ref, dwkv_ref, dqg_ref, dkvg_ref, dqng_ref,
             dkng_ref, dqraw, dkvraw):
        i = pl.program_id(0)
        r1, xn1, qn, qraw, r2, xn2, kvn, kvraw = _mla_project(xq_ref, xkv_ref, qg_ref, wq_ref, kvg_ref, wkv_ref)
        c, spv, snv = c_ref[...], sp_ref[...], sn_ref[...]
        pe = pe_ref[...]
        pe_ss = jnp.sum(pe * pe, axis=-1, keepdims=True)
        qngv, kngv = qng_ref[...], kng_ref[...]
        dqng = jnp.zeros((1, MLA_QKP), F32)
        dkng = jnp.zeros((1, MLA_QKP), F32)
        dpe = jnp.zeros_like(pe)
        for h in range(MLA_HEADS):
            b = h * MLA_QKP
            qh = qraw[:, b:b + MLA_QKP]
            r = lax.rsqrt(jnp.sum(qh * qh, axis=-1, keepdims=True) * (1.0 / MLA_QK) + EPS)
            xn = qh * r
            d_n = jnp.concatenate(
                [dq_ref[:, b:b + 128], _unrope64(dq_ref[:, b + 128:b + 256], c, spv, snv)], axis=1) * MLA_SCALE
            dqng = dqng + jnp.sum(d_n * xn, axis=0, keepdims=True)
            dxn = d_n * qngv
            dqraw[:, b:b + MLA_QKP] = _bf(r * (dxn - xn * (jnp.sum(dxn * xn, axis=-1, keepdims=True) * (1.0 / MLA_QK))))
            kn = kvraw[:, b:b + 128]
            rk = lax.rsqrt((jnp.sum(kn * kn, axis=-1, keepdims=True) + pe_ss) * (1.0 / MLA_QK) + EPS)
            xk = jnp.concatenate([kn, pe], axis=1) * rk
            d_k = jnp.concatenate(
                [dk_ref[:, b:b + 128], _unrope64(dk_ref[:, b + 128:b + 256], c, spv, snv)], axis=1)
            dkng = dkng + jnp.sum(d_k * xk, axis=0, keepdims=True)
            dxk = d_k * kngv
            dfull = rk * (dxk - xk * (jnp.sum(dxk * xk, axis=-1, keepdims=True) * (1.0 / MLA_QK)))
            dkvraw[:, b:b + 128] = _bf(dfull[:, :128])
            dkvraw[:, b + 128:b + 256] = _bf(dv_ref[:, h * MLA_V:(h + 1) * MLA_V])
            dpe = dpe + dfull[:, 128:]
        dpe_ref[...] = _bf(dpe)
        dqr, dkvr = dqraw[...], dkvraw[...]
        dqn = _dot(dqr, wq_ref[...], 1, 1)
        dxn1 = dqn * qg_ref[...]
        dxq_ref[...] = _bf(r1 * (dxn1 - xn1 * jnp.mean(dxn1 * xn1, axis=-1, keepdims=True)))
        dkvn = _dot(dkvr, wkv_ref[...], 1, 1)
        dxn2 = dkvn * kvg_ref[...]
        dxkv_ref[...] = _bf(r2 * (dxn2 - xn2 * jnp.mean(dxn2 * xn2, axis=-1, keepdims=True)))
        parts = (_dot(qn, dqr, 0, 0), _dot(kvn, dkvr, 0, 0), jnp.sum(dqn * xn1, axis=0, keepdims=True),
                 jnp.sum(dkvn * xn2, axis=0, keepdims=True), dqng, dkng)
        accs = (dwq_ref, dwkv_ref, dqg_ref, dkvg_ref, dqng_ref, dkng_ref)

        @pl.when(i == 0)
        def _():
            for a, p in zip(accs, parts):
                a[...] = p

        @pl.when(i > 0)
        def _():
            for a, p in zip(accs, parts):
                a[...] += p

    row = lambda w: pl.BlockSpec((tm, w), lambda i: (i, 0))
    full = lambda r, c: pl.BlockSpec((r, c), lambda i: (0, 0))
    return pl.pallas_call(
        body, name=name, grid=(s // tm,),
        in_specs=[row(1024), row(1024), row(512)] + _mla_specs(tm),
        out_specs=[row(512), row(256), row(128), full(512, 1024), full(256, 1024), full(1, 512), full(1, 256),
                   full(1, 256), full(1, 256)],
        out_shape=[jax.ShapeDtypeStruct((s, 512), BF16), jax.ShapeDtypeStruct((s, 256), BF16),
                   jax.ShapeDtypeStruct((s, 128), BF16), jax.ShapeDtypeStruct((512, 1024), F32),
                   jax.ShapeDtypeStruct((256, 1024), F32), jax.ShapeDtypeStruct((1, 512), F32),
                   jax.ShapeDtypeStruct((1, 256), F32), jax.ShapeDtypeStruct((1, 256), F32),
                   jax.ShapeDtypeStruct((1, 256), F32)],
        scratch_shapes=[pltpu.VMEM((tm, 1024), BF16), pltpu.VMEM((tm, 1024), BF16)],
        compiler_params=_cparams("arbitrary"),
    )(dq, dk, dv, z, z, z, qg, wq, kvg, wkv, qng, kng, cos, sp, sn)


def _flash_fwd(q, k, v, *, name, tq=512, tk=512):
    s = q.shape[0]
    tq, tk = min(tq, s), min(tk, s)
    nk = s // tk

    def body(q_ref, k_ref, v_ref, o_ref, lse_ref, m_s, l_s, acc):
        j = pl.program_id(2)

        @pl.when(j == 0)
        def _():
            m_s[...] = jnp.full_like(m_s, -jnp.inf)
            l_s[...] = jnp.zeros_like(l_s)
            acc[...] = jnp.zeros_like(acc)

        sc = _dot(q_ref[...], k_ref[...], 1, 1)
        m_prev = m_s[...]
        m_new = jnp.maximum(m_prev, jnp.max(sc, axis=-1, keepdims=True))
        p = jnp.exp(sc - m_new[:, 0:1])
        alpha = jnp.exp(m_prev - m_new)
        l_s[...] = alpha * l_s[...] + jnp.sum(p, axis=-1, keepdims=True)
        acc[...] = alpha * acc[...] + _dot(p, v_ref[...])
        m_s[...] = m_new

        @pl.when(j == nk - 1)
        def _():
            o_ref[0] = acc[...] / l_s[...]
            lse_ref[...] = m_s[...] + jnp.log(l_s[...])

    return pl.pallas_call(
        body, name=name, grid=(MLA_HEADS, s // tq, nk),
        in_specs=[pl.BlockSpec((tq, MLA_QKP), lambda h, i, j: (i, h)),
                  pl.BlockSpec((tk, MLA_QKP), lambda h, i, j: (j, h)),
                  pl.BlockSpec((tk, MLA_V), lambda h, i, j: (j, h))],
        out_specs=[pl.BlockSpec((1, tq, MLA_V), lambda h, i, j: (0, i, h)),
                   pl.BlockSpec((tq, MLA_V), lambda h, i, j: (i, h))],
        out_shape=[jax.ShapeDtypeStruct((1, s, GROUP_W), F32), jax.ShapeDtypeStruct((s, GROUP_W), F32)],
        scratch_shapes=[pltpu.VMEM((tq, MLA_V), F32), pltpu.VMEM((tq, MLA_V), F32), pltpu.VMEM((tq, MLA_V), F32)],
        compiler_params=_cparams("parallel", "parallel", "arbitrary"),
    )(q, k, v)


def _flash_bwd_dq(q, k, v, do, o, lse, *, name, tq=512, tk=512):
    s = q.shape[0]
    tq, tk = min(tq, s), min(tk, s)
    nk = s // tk

    def body(q_ref, k_ref, v_ref, do_ref, o_ref, lse_ref, dq_ref, acc):
        j = pl.program_id(2)
        dov = do_ref[...]
        delta = jnp.sum(dov * o_ref[0], axis=-1, keepdims=True)
        p = jnp.exp(_dot(q_ref[...], k_ref[...], 1, 1) - lse_ref[:, 0:1])
        ds = p * (_dot(dov, v_ref[...], 1, 1) - delta)
        part = _dot(ds, k_ref[...])

        @pl.when(j == 0)
        def _():
            acc[...] = part

        @pl.when(j > 0)
        def _():
            acc[...] += part

        @pl.when(j == nk - 1)
        def _():
            dq_ref[...] = acc[...]

    qb = pl.BlockSpec((tq, MLA_QKP), lambda h, i, j: (i, h))
    ob = pl.BlockSpec((tq, MLA_V), lambda h, i, j: (i, h))
    return pl.pallas_call(
        body, name=name, grid=(MLA_HEADS, s // tq, nk),
        in_specs=[qb, pl.BlockSpec((tk, MLA_QKP), lambda h, i, j: (j, h)),
                  pl.BlockSpec((tk, MLA_V), lambda h, i, j: (j, h)), ob,
                  pl.BlockSpec((1, tq, MLA_V), lambda h, i, j: (0, i, h)), ob],
        out_specs=qb,
        out_shape=jax.ShapeDtypeStruct((s, MLA_HEADS * MLA_QKP), F32),
        scratch_shapes=[pltpu.VMEM((tq, MLA_QKP), F32)],
        compiler_params=_cparams("parallel", "parallel", "arbitrary"),
    )(q, k, v, do, o, lse)


def _flash_bwd_dkv(q, k, v, do, o, lse, *, name, tq=512, tk=512):
    s = q.shape[0]
    tq, tk = min(tq, s), min(tk, s)
    nq = s // tq

    def body(q_ref, k_ref, v_ref, do_ref, o_ref, lse_ref, dk_ref, dv_ref, dk_acc, dv_acc):
        i = pl.program_id(2)
        dov = do_ref[...]
        delta = jnp.sum(dov * o_ref[0], axis=-1, keepdims=True)
        p = jnp.exp(_dot(q_ref[...], k_ref[...], 1, 1) - lse_ref[:, 0:1])
        ds = p * (_dot(dov, v_ref[...], 1, 1) - delta)
        pv = _dot(p, dov, 0, 0)
        pk = _dot(ds, q_ref[...], 0, 0)

        @pl.when(i == 0)
        def _():
            dv_acc[...] = pv
            dk_acc[...] = pk

        @pl.when(i > 0)
        def _():
            dv_acc[...] += pv
            dk_acc[...] += pk

        @pl.when(i == nq - 1)
        def _():
            dk_ref[...] = dk_acc[...]
            dv_ref[...] = dv_acc[...]

    kb = pl.BlockSpec((tk, MLA_QKP), lambda h, j, i: (j, h))
    vb = pl.BlockSpec((tk, MLA_V), lambda h, j, i: (j, h))
    ob = pl.BlockSpec((tq, MLA_V), lambda h, j, i: (i, h))
    return pl.pallas_call(
        body, name=name, grid=(MLA_HEADS, s // tk, nq),
        in_specs=[pl.BlockSpec((tq, MLA_QKP), lambda h, j, i: (i, h)), kb, vb, ob,
                  pl.BlockSpec((1, tq, MLA_V), lambda h, j, i: (0, i, h)), ob],
        out_specs=[kb, vb],
        out_shape=[jax.ShapeDtypeStruct((s, MLA_HEADS * MLA_QKP), F32), jax.ShapeDtypeStruct((s, GROUP_W), F32)],
        scratch_shapes=[pltpu.VMEM((tk, MLA_QKP), F32), pltpu.VMEM((tk, MLA_V), F32)],
        compiler_params=_cparams("parallel", "parallel", "arbitrary"),
    )(q, k, v, do, o, lse)


def _rows_tile(r, c, itemsize=4, budget=2 * 1024 * 1024):
    if r * c * itemsize <= budget:
        return r
    best = None
    for t in range(8, r, 8):
        if r % t == 0 and t * c * itemsize <= budget:
            best = t
    return best if best is not None else r


def _add_n(arrs, *, out_dtype=F32, name):
    shape = arrs[0].shape
    c = shape[-1]
    flat = [a.reshape(-1, c) for a in arrs]
    r = flat[0].shape[0]
    t = _rows_tile(r, c)

    def body(*refs):
        acc = refs[0][...].astype(F32)
        for ref in refs[1:-1]:
            acc = acc + ref[...].astype(F32)
        refs[-1][...] = acc.astype(out_dtype)

    blk = pl.BlockSpec((t, c), lambda i: (i, 0))
    out = pl.pallas_call(
        body, name=name, grid=(r // t,), in_specs=[blk] * len(flat), out_specs=blk,
        out_shape=jax.ShapeDtypeStruct((r, c), out_dtype), compiler_params=_cparams("parallel"),
    )(*flat)
    return out.reshape(shape)


def _adamw(w, g, m, v, *, name):
    shape = w.shape
    c = shape[-1]
    flat = [a.reshape(-1, c) for a in (w, g, m, v)]
    r = flat[0].shape[0]
    t = _rows_tile(r, c, budget=1024 * 1024)

    def body(w_ref, g_ref, m_ref, v_ref, d_ref, mo_ref, vo_ref):
        gv = g_ref[...]
        m2 = ADAM_B1 * m_ref[...] + (1.0 - ADAM_B1) * gv
        v2 = ADAM_B2 * v_ref[...] + (1.0 - ADAM_B2) * (gv * gv)
        m_hat = m2 / (1.0 - ADAM_B1 ** ADAM_STEP)
        v_hat = v2 / (1.0 - ADAM_B2 ** ADAM_STEP)
        d_ref[...] = -ADAM_LR * (m_hat / (jnp.sqrt(v_hat) + ADAM_EPS) + ADAM_WD * w_ref[...])
        mo_ref[...] = m2
        vo_ref[...] = v2

    blk = pl.BlockSpec((t, c), lambda i: (i, 0))
    outs = pl.pallas_call(
        body, name=name, grid=(r // t,), in_specs=[blk] * 4, out_specs=[blk] * 3,
        out_shape=[jax.ShapeDtypeStruct((r, c), F32)] * 3, compiler_params=_cparams("parallel"),
    )(*flat)
    return tuple(o.reshape(shape) for o in outs)


def _place():
    x, y, c = lax.axis_index("x"), lax.axis_index("y"), lax.axis_index("c")
    chips = [(1 - x, y), (x, 1 - y), (1 - x, 1 - y)]
    return x, y, c, chips


ANY = pl.BlockSpec(memory_space=pl.ANY)


def _gather_shards(shards, *, name):
    nt = len(shards)

    def body(*refs):
        src, dst = refs[:nt], refs[nt:2 * nt]
        send, recv, fsend, frecv, lsem = refs[2 * nt:]
        x, y, c, chips = _place()
        me = 2 * x + y
        local = [pltpu.make_async_copy(src[t], dst[t].at[me], lsem.at[t]) for t in range(nt)]
        for cp in local:
            cp.start()

        def half(t, slot, hc):
            hr = src[t].shape[0] // 2
            return dst[t].at[slot, pl.ds(hc * hr, hr)]

        def first(t, k):
            hr = src[t].shape[0] // 2
            return pltpu.make_async_remote_copy(
                src_ref=src[t].at[pl.ds(c * hr, hr)], dst_ref=half(t, me, c),
                send_sem=send.at[t, k], recv_sem=recv.at[t, k],
                device_id=(chips[k][0], chips[k][1], c), device_id_type=MESH)

        def landed(t, k):
            slot = 2 * chips[k][0] + chips[k][1]
            return pltpu.make_async_remote_copy(
                src_ref=half(t, slot, c), dst_ref=half(t, slot, c),
                send_sem=send.at[t, k], recv_sem=recv.at[t, k],
                device_id=(chips[k][0], chips[k][1], c), device_id_type=MESH)

        def forward(t, k, hc):
            slot = 2 * chips[k][0] + chips[k][1]
            return pltpu.make_async_remote_copy(
                src_ref=half(t, slot, hc), dst_ref=half(t, slot, hc),
                send_sem=fsend.at[t, k], recv_sem=frecv.at[t, k],
                device_id=(x, y, 1 - c), device_id_type=MESH)

        for t in range(nt):
            for k in range(3):
                first(t, k).start()
        for t in range(nt):
            for k in range(3):
                landed(t, k).wait_recv()
                forward(t, k, c).start()
        for t in range(nt):
            for k in range(3):
                forward(t, k, 1 - c).wait_recv()
        for t in range(nt):
            for k in range(3):
                first(t, k).wait_send()
                forward(t, k, c).wait_send()
        for cp in local:
            cp.wait()

    return pl.pallas_call(
        body, name=name, in_specs=[ANY] * nt, out_specs=[ANY] * nt,
        out_shape=[jax.ShapeDtypeStruct((N_CHIP,) + a.shape, a.dtype) for a in shards],
        scratch_shapes=[pltpu.SemaphoreType.DMA((nt, 3)), pltpu.SemaphoreType.DMA((nt, 3)),
                        pltpu.SemaphoreType.DMA((nt, 3)), pltpu.SemaphoreType.DMA((nt, 3)),
                        pltpu.SemaphoreType.DMA((nt,))],
    )(*shards)


def _pair_exchange(grads, *, name):
    nt = len(grads)

    def body(*refs):
        src, mine, theirs = refs[:nt], refs[nt:2 * nt], refs[2 * nt:3 * nt]
        send, recv, lsem = refs[3 * nt:]
        x, y, c, _ = _place()
        local, remote = [], []
        for t in range(nt):
            hr = src[t].shape[1] // 2
            local.append(pltpu.make_async_copy(src[t].at[:, pl.ds(c * hr, hr)], mine[t], lsem.at[t]))
            remote.append(pltpu.make_async_remote_copy(
                src_ref=src[t].at[:, pl.ds((1 - c) * hr, hr)], dst_ref=theirs[t],
                send_sem=send.at[t], recv_sem=recv.at[t], device_id=(x, y, 1 - c), device_id_type=MESH))
        for cp in local + remote:
            cp.start()
        for cp in remote:
            cp.wait_recv()
        for cp in remote:
            cp.wait_send()
        for cp in local:
            cp.wait()

    halves = [jax.ShapeDtypeStruct((a.shape[0], a.shape[1] // 2) + a.shape[2:], a.dtype) for a in grads]
    outs = pl.pallas_call(
        body, name=name, in_specs=[ANY] * nt, out_specs=[ANY] * (2 * nt), out_shape=halves + halves,
        scratch_shapes=[pltpu.SemaphoreType.DMA((nt,)), pltpu.SemaphoreType.DMA((nt,)),
                        pltpu.SemaphoreType.DMA((nt,))],
    )(*grads)
    return outs[:nt], outs[nt:]


def _chip_exchange(parts, *, name):
    nt = len(parts)

    def body(*refs):
        src, own, got = refs[:nt], refs[nt:2 * nt], refs[2 * nt:3 * nt]
        send, recv, lsem = refs[3 * nt:]
        x, y, c, chips = _place()
        me = 2 * x + y
        local, remote = [], []
        for t in range(nt):
            local.append(pltpu.make_async_copy(src[t].at[me], own[t], lsem.at[t]))
            for k in range(3):
                remote.append(pltpu.make_async_remote_copy(
                    src_ref=src[t].at[2 * chips[k][0] + chips[k][1]], dst_ref=got[t].at[k],
                    send_sem=send.at[t, k], recv_sem=recv.at[t, k],
                    device_id=(chips[k][0], chips[k][1], c), device_id_type=MESH))
        for cp in local + remote:
            cp.start()
        for cp in remote:
            cp.wait_recv()
        for cp in remote:
            cp.wait_send()
        for cp in local:
            cp.wait()

    one = [jax.ShapeDtypeStruct(a.shape[1:], a.dtype) for a in parts]
    three = [jax.ShapeDtypeStruct((3,) + a.shape[1:], a.dtype) for a in parts]
    outs = pl.pallas_call(
        body, name=name, in_specs=[ANY] * nt, out_specs=[ANY] * (2 * nt), out_shape=one + three,
        scratch_shapes=[pltpu.SemaphoreType.DMA((nt, 3)), pltpu.SemaphoreType.DMA((nt, 3)),
                        pltpu.SemaphoreType.DMA((nt,))],
    )(*parts)
    return outs[:nt], outs[nt:]


def _pair_join(halves, *, name):
    nt = len(halves)

    def body(*refs):
        src, dst = refs[:nt], refs[nt:2 * nt]
        send, recv, lsem = refs[2 * nt:]
        x, y, c, _ = _place()
        local, remote = [], []
        for t in range(nt):
            hr = src[t].shape[0]
            rows = dst[t].at[pl.ds(c * hr, hr)]
            local.append(pltpu.make_async_copy(src[t], rows, lsem.at[t]))
            remote.append(pltpu.make_async_remote_copy(
                src_ref=src[t], dst_ref=rows, send_sem=send.at[t], recv_sem=recv.at[t],
                device_id=(x, y, 1 - c), device_id_type=MESH))
        for cp in local + remote:
            cp.start()
        for cp in remote:
            cp.wait_recv()
        for cp in remote:
            cp.wait_send()
        for cp in local:
            cp.wait()

    return pl.pallas_call(
        body, name=name, in_specs=[ANY] * nt, out_specs=[ANY] * nt,
        out_shape=[jax.ShapeDtypeStruct((2 * a.shape[0],) + a.shape[1:], a.dtype) for a in halves],
        scratch_shapes=[pltpu.SemaphoreType.DMA((nt,)), pltpu.SemaphoreType.DMA((nt,)),
                        pltpu.SemaphoreType.DMA((nt,))],
    )(*halves)


def _gather_all(block, *, name):
    m_per, n = block.shape

    def body(x_ref, out_ref, send_sems, recv_sems, local_sem):
        x, y, c, chips = _place()
        me, sibling = (x, y, c), (x, y, 1 - c)

        def rows(px, py, pc):
            return out_ref.at[4 * px + 2 * py + pc]

        def copy(k, blk, to, src=None):
            return pltpu.make_async_remote_copy(
                src_ref=rows(*blk) if src is None else src, dst_ref=rows(*blk),
                send_sem=send_sems.at[k], recv_sem=recv_sems.at[k], device_id=to, device_id_type=MESH)

        mine = pltpu.make_async_copy(x_ref, rows(*me), local_sem)
        mine.start()
        first = [copy(0, me, sibling, src=x_ref)]
        first += [copy(1 + j, me, (*chip, c), src=x_ref) for j, chip in enumerate(chips)]
        for cp in first:
            cp.start()
        passed = [copy(4 + j, (*chip, c), sibling) for j, chip in enumerate(chips)]
        for j, chip in enumerate(chips):
            copy(1 + j, (*chip, c), me).wait_recv()
            passed[j].start()
        copy(0, sibling, me).wait_recv()
        for j, chip in enumerate(chips):
            copy(4 + j, (*chip, 1 - c), me).wait_recv()
        for cp in first + passed:
            cp.wait_send()
        mine.wait()

    return pl.pallas_call(
        body, name=name,
        out_shape=jax.ShapeDtypeStruct((N_DEV, m_per, n), block.dtype),
        in_specs=[pl.BlockSpec(memory_space=pltpu.VMEM)], out_specs=pl.BlockSpec(memory_space=pltpu.VMEM),
        scratch_shapes=[pltpu.SemaphoreType.DMA((7,)), pltpu.SemaphoreType.DMA((7,)), pltpu.SemaphoreType.DMA],
        compiler_params=pltpu.CompilerParams(vmem_limit_bytes=VMEM_LIMIT),
    )(block)


def _sum_slots(slots, *, name):
    n, m, c = slots.shape
    t = _rows_tile(m, c * n)

    def body(s_ref, o_ref):
        acc = s_ref[0]
        for k in range(1, n):
            acc = acc + s_ref[k]
        o_ref[...] = acc

    return pl.pallas_call(
        body, name=name, grid=(m // t,), in_specs=[pl.BlockSpec((n, t, c), lambda i: (0, i, 0))],
        out_specs=pl.BlockSpec((t, c), lambda i: (i, 0)), out_shape=jax.ShapeDtypeStruct((m, c), F32),
        compiler_params=_cparams("parallel"),
    )(slots)


def _pad_cols(a, width):
    return a if a.shape[1] == width else jnp.pad(a, ((0, 0), (0, width - a.shape[1])))


def _w_in_padded(shards):
    full = jnp.concatenate([shards[j] for j in range(N_CHIP)], axis=1)
    return jnp.concatenate([_pad_cols(full[:, SEG[n][2]:SEG[n][2] + SEG[n][3]], SEG[n][1]) for n in SEG_ORDER], axis=1)


def _w_in_unpadded(gp):
    full = jnp.concatenate([gp[:, SEG[n][0]:SEG[n][0] + SEG[n][3]] for n in ORIG_ORDER], axis=1)
    w = IN_COLS // N_CHIP
    return jnp.stack([full[:, j * w:(j + 1) * w] for j in range(N_CHIP)])


def _pad_heads(w, true_w, pad_w):
    r = w.shape[0]
    h = w.shape[1] // true_w
    return jnp.pad(w.reshape(r, h, true_w), ((0, 0), (0, 0), (0, pad_w - true_w))).reshape(r, h * pad_w)


def _unpad_heads(w, true_w, pad_w):
    r = w.shape[0]
    h = w.shape[1] // pad_w
    return w.reshape(r, h, pad_w)[:, :, :true_w].reshape(r, h * true_w)


def _cols_to_slots(a):
    w = a.shape[1] // N_CHIP
    return jnp.stack([a[:, j * w:(j + 1) * w] for j in range(N_CHIP)])


def _slots_to_cols(a):
    return jnp.concatenate([a[j] for j in range(N_CHIP)], axis=1)


def _to_heads(a, h, d):
    return a.reshape(a.shape[0], h, d).transpose(1, 0, 2)


def _from_heads(a):
    return a.transpose(1, 0, 2).reshape(a.shape[1], -1)


SMALL = [("norm_g", 2048), ("ret_norm_g", 512), ("gla_ba_f", 256), ("gla_ba_b", 256), ("gla_norm_g", 512),
         ("pool_w", 4 * 128 * 128), ("pool_scale", 512), ("mla_q_norm_g", 512), ("mla_kv_norm_g", 256),
         ("mla_qk_norm_q", 192), ("mla_qk_norm_k", 192)]


def _pack_small(vals):
    parts = []
    for name, n in SMALL:
        v = vals[name].reshape(-1)
        parts.append(jnp.pad(v, (0, (-v.shape[0]) % 1024)))
    parts.append(jnp.pad(vals["loss"].reshape(-1), (0, 1023)))
    return jnp.concatenate(parts).reshape(-1, 128)


def _unpack_small(block):
    flat = block.reshape(-1)
    out, off = {}, 0
    for name, n in SMALL:
        out[name] = flat[off:off + DEPTH * n]
        off += DEPTH * n + (-(DEPTH * n)) % 1024
    out["loss"] = flat[off]
    return out


def _layer_weights(l, p, g):
    wa = jnp.zeros((128, 512), F32)
    wa = wa.at[0:GLA_RANK, 0:256].set(_slots_to_cols(g["gla_wa2_f"][:, l]))
    wa = wa.at[GLA_RANK:2 * GLA_RANK, 256:512].set(_slots_to_cols(g["gla_wa2_b"][:, l]))
    return dict(
        norm_g=p["norm_g"][l][None, :],
        w_in=_w_in_padded(g["w_in"][:, l]),
        w_out=g["w_out"][:, l].reshape(4 * g["w_out"].shape[2], -1),
        ret_norm_g=p["ret_norm_g"][l][None, :],
        wa=_bf(wa),
        ba=jnp.concatenate([p["gla_ba_f"][l], p["gla_ba_b"][l]])[None, :],
        gla_norm_g=p["gla_norm_g"][l][None, :],
        pool_w=_bf(p["pool_w"][l]),
        pool_scale=p["pool_scale"][l][None, :],
        qg=p["mla_q_norm_g"][l][None, :],
        wq=_pad_heads(_slots_to_cols(g["mla_wq_b"][:, l]), MLA_QK, MLA_QKP),
        kvg=p["mla_kv_norm_g"][l][None, :],
        wkv=_slots_to_cols(g["mla_wkv_b"][:, l]),
        qng=jnp.pad(p["mla_qk_norm_q"][l], (0, MLA_QKP - MLA_QK))[None, :],
        kng=jnp.pad(p["mla_qk_norm_k"][l], (0, MLA_QKP - MLA_QK))[None, :],
    )


def _layer_fwd(l, x, w, tabs):
    ret_cos, ret_sin, mla_cos, mla_sp, mla_sn = tabs
    nm = lambda s: f"l{l}_{s}"
    h = _rmsnorm_fwd(x, w["norm_g"], name=nm("norm"))
    z = _matmul(h, w["w_in"], name=nm("in_proj"))
    qr, kr = _ret_pre(z, ret_cos, ret_sin, name=nm("ret_pre"))
    ret_o = _bla(qr, kr, z, _ret_log_gamma(False), (0, 0, SEG["rv"][0] // 512), name=nm("ret_scan"))
    y_a = _post(ret_o, z, SEG["rg"][0] // 512, w["ret_norm_g"], norm=True, name=nm("ret_post"))
    la = _gla_gate(z, w["wa"], w["ba"], name=nm("gla_gate"))
    la_h = jnp.stack([_to_heads(la[:, :256], GLA_HEADS, GLA_DK), _to_heads(la[:, 256:], GLA_HEADS, GLA_DK)])
    gq = _to_heads(z[:, SEG["gq"][0]:SEG["gq"][0] + 256], GLA_HEADS, GLA_DK)
    gk = _to_heads(z[:, SEG["gk"][0]:SEG["gk"][0] + 256], GLA_HEADS, GLA_DK)
    gla_o, gla_st = _gla_fwd(gq, gk, z, la_h, name=nm("gla_scan"))
    y_b = _post(gla_o, z, SEG["gg"][0] // 512, w["gla_norm_g"], norm=True, name=nm("gla_post"))
    y_c = _pool_fwd(z, w["pool_w"], w["pool_scale"], name=nm("pool"))
    q, k, v = _mla_pre(z, w["qg"], w["wq"], w["kvg"], w["wkv"], w["qng"], w["kng"], mla_cos, mla_sp, mla_sn,
                       name=nm("mla_pre"))
    att_o, lse = _flash_fwd(q, k, v, name=nm("attn"))
    y_d = _post(att_o, z, SEG["mg"][0] // 512, w["qg"], norm=False, name=nm("mla_post"))
    y = jnp.concatenate([y_a, y_b, y_c, y_d], axis=1)
    x_next = _matmul(y, w["w_out"], add=x, name=nm("out_proj"))
    saved = dict(x=x, h=h, z=z, y=y, qr=qr, kr=kr, ret_o=ret_o, la_h=la_h, gq=gq, gk=gk, gla_o=gla_o, gla_st=gla_st,
                 q=q, k=k, v=v, att_o=att_o, lse=lse)
    return x_next, saved


def _layer_bwd(l, dx_next, w, sv, tabs):
    ret_cos, ret_sin, mla_cos, mla_sp, mla_sn = tabs
    nm = lambda s: f"l{l}_{s}"
    z = sv["z"]
    dy = _matmul(dx_next, w["w_out"], tb=True, tk=1024, name=nm("out_proj_dy"))
    d_w_out = _matmul(sv["y"], dx_next, ta=True, tk=1024, name=nm("out_proj_dw"))
    d_rg, d_ret_o, d_ret_g = _post_bwd(dy, 0, sv["ret_o"], z, SEG["rg"][0] // 512, w["ret_norm_g"], norm=True,
                                       name=nm("ret_post_bwd"))
    vcol = SEG["rv"][0] // 512
    dqr = _bla(d_ret_o, z, sv["kr"], _ret_log_gamma(False), (0, vcol, 0), name=nm("ret_scan_dq"))
    dkr = _bla(z, d_ret_o, sv["qr"], _ret_log_gamma(True), (vcol, 0, 0), name=nm("ret_scan_dk"))
    drv = _bla(sv["kr"], sv["qr"], d_ret_o, _ret_log_gamma(True), (0, 0, 0), name=nm("ret_scan_dv"))
    d_rq, d_rk = _ret_pre_bwd(dqr, dkr, ret_cos, ret_sin, name=nm("ret_pre_bwd"))
    d_rv = _add_n([drv[0], drv[1]], out_dtype=BF16, name=nm("ret_dv_sum"))
    d_gg, d_gla_o, d_gla_g = _post_bwd(dy, 1, sv["gla_o"], z, SEG["gg"][0] // 512, w["gla_norm_g"], norm=True,
                                       name=nm("gla_post_bwd"))
    dq2, dk2, dla2, dv2 = _gla_bwd(sv["gq"], sv["gk"], z, sv["la_h"], d_gla_o, sv["gla_st"], name=nm("gla_scan_bwd"))
    d_gq = _bf(_from_heads(dq2[0] + dq2[1]))
    d_gk = _bf(_from_heads(dk2[0] + dk2[1]))
    d_gv = _add_n([dv2[0], dv2[1]], out_dtype=BF16, name=nm("gla_dv_sum"))
    dla = jnp.concatenate([_from_heads(dla2[0]), _from_heads(dla2[1])], axis=1)
    d_ga, d_wa, d_ba = _gla_gate_bwd(dla, z, w["wa"], w["ba"], name=nm("gla_gate_bwd"))
    d_pv, d_pg, d_pool_w, d_pool_scale = _pool_bwd(dy, z, w["pool_w"], w["pool_scale"], name=nm("pool_bwd"))
    d_mg, d_att_o, _ = _post_bwd(dy, 3, sv["att_o"], z, SEG["mg"][0] // 512, w["qg"], norm=False,
                                 name=nm("mla_post_bwd"))
    dq = _flash_bwd_dq(sv["q"], sv["k"], sv["v"], d_att_o, sv["att_o"], sv["lse"], name=nm("attn_dq"))
    dk, dv = _flash_bwd_dkv(sv["q"], sv["k"], sv["v"], d_att_o, sv["att_o"], sv["lse"], name=nm("attn_dkv"))
    d_mq, d_mkv, d_mkr, d_wq, d_wkv, d_qg, d_kvg, d_qng, d_kng = _mla_pre_bwd(
        dq, dk, dv, z, w["qg"], w["wq"], w["kvg"], w["wkv"], w["qng"], w["kng"], mla_cos, mla_sp, mla_sn,
        name=nm("mla_pre_bwd"))
    segs = dict(rq=d_rq, rk=d_rk, rv=d_rv, rg=d_rg, gv=d_gv, gg=d_gg, pv=d_pv, pg=d_pg, mq=d_mq, mg=d_mg,
                gq=d_gq, gk=d_gk, mkv=d_mkv, ga=d_ga, mkr=d_mkr)
    dz = jnp.concatenate([segs[n] for n in SEG_ORDER], axis=1)
    dh = _matmul(dz, w["w_in"], tb=True, tk=2048, name=nm("in_proj_dh"))
    d_w_in = _matmul(sv["h"], dz, ta=True, tk=1024, name=nm("in_proj_dw"))
    dx, d_norm_g = _rmsnorm_bwd(sv["x"], dh, w["norm_g"], dx_next, name=nm("norm_bwd"))
    sharded = dict(
        w_in=_w_in_unpadded(d_w_in),
        w_out=d_w_out.reshape(N_CHIP, d_w_out.shape[0] // N_CHIP, d_w_out.shape[1]),
        mla_wq_b=_cols_to_slots(_unpad_heads(d_wq, MLA_QK, MLA_QKP)),
        mla_wkv_b=_cols_to_slots(d_wkv),
        gla_wa2_f=_cols_to_slots(d_wa[0:GLA_RANK, 0:256]),
        gla_wa2_b=_cols_to_slots(d_wa[GLA_RANK:2 * GLA_RANK, 256:512]),
    )
    small = dict(
        norm_g=d_norm_g[0], ret_norm_g=d_ret_g[0], gla_ba_f=d_ba[0, :256], gla_ba_b=d_ba[0, 256:],
        gla_norm_g=d_gla_g[0], pool_w=d_pool_w.reshape(-1), pool_scale=d_pool_scale[0], mla_q_norm_g=d_qg[0],
        mla_kv_norm_g=d_kvg[0], mla_qk_norm_q=d_qng[0, :MLA_QK], mla_qk_norm_k=d_kng[0, :MLA_QK],
    )
    return dx, sharded, small


SHARDED = ["w_in", "w_out", "mla_wq_b", "mla_wkv_b", "gla_wa2_f", "gla_wa2_b"]
WEIGHTS = ["norm_g", "w_in", "ret_norm_g", "gla_wa2_f", "gla_ba_f", "gla_wa2_b", "gla_ba_b", "gla_norm_g", "pool_w",
           "pool_scale", "mla_q_norm_g", "mla_wq_b", "mla_kv_norm_g", "mla_wkv_b", "mla_qk_norm_q", "mla_qk_norm_k",
           "w_out"]


def _local_step(p, gathered):
    x = p["x"][0]
    tabs = _rope_tables(x.shape[0])
    ws, saved = [], []
    for l in range(DEPTH):
        w = _layer_weights(l, p, gathered)
        x, sv = _layer_fwd(l, x, w, tabs)
        ws.append(w)
        saved.append(sv)
    dx, loss = _loss_head(x, p["loss_target"][0], name="loss_head")
    sharded, small = [None] * DEPTH, [None] * DEPTH
    for l in reversed(range(DEPTH)):
        dx, sharded[l], small[l] = _layer_bwd(l, dx, ws[l], saved[l], tabs)
    sharded = {n: jnp.stack([sharded[l][n] for l in range(DEPTH)], axis=1) for n in SHARDED}
    small = {n: jnp.stack([small[l][n] for l in range(DEPTH)]) for n, _ in SMALL}
    small["loss"] = loss
    return dx[None], sharded, small


def kernel(x, norm_g, w_in, ret_norm_g, gla_wa2_f, gla_ba_f, gla_wa2_b, gla_ba_b, gla_norm_g, pool_w, pool_scale, mla_q_norm_g, mla_wq_b, mla_kv_norm_g, mla_wkv_b, mla_qk_norm_q, mla_qk_norm_k, w_out, loss_target, m_norm_g, m_w_in, m_ret_norm_g, m_gla_wa2_f, m_gla_ba_f, m_gla_wa2_b, m_gla_ba_b, m_gla_norm_g, m_pool_w, m_pool_scale, m_mla_q_norm_g, m_mla_wq_b, m_mla_kv_norm_g, m_mla_wkv_b, m_mla_qk_norm_q, m_mla_qk_norm_k, m_w_out, v_norm_g, v_w_in, v_ret_norm_g, v_gla_wa2_f, v_gla_ba_f, v_gla_wa2_b, v_gla_ba_b, v_gla_norm_g, v_pool_w, v_pool_scale, v_mla_q_norm_g, v_mla_wq_b, v_mla_kv_norm_g, v_mla_wkv_b, v_mla_qk_norm_q, v_mla_qk_norm_k, v_w_out):
    p = dict(x=x, norm_g=norm_g, w_in=w_in, ret_norm_g=ret_norm_g, gla_wa2_f=gla_wa2_f, gla_ba_f=gla_ba_f,
             gla_wa2_b=gla_wa2_b, gla_ba_b=gla_ba_b, gla_norm_g=gla_norm_g, pool_w=pool_w, pool_scale=pool_scale,
             mla_q_norm_g=mla_q_norm_g, mla_wq_b=mla_wq_b, mla_kv_norm_g=mla_kv_norm_g, mla_wkv_b=mla_wkv_b,
             mla_qk_norm_q=mla_qk_norm_q, mla_qk_norm_k=mla_qk_norm_k, w_out=w_out, loss_target=loss_target)
    moments = dict(
        m=dict(norm_g=m_norm_g, w_in=m_w_in, ret_norm_g=m_ret_norm_g, gla_wa2_f=m_gla_wa2_f, gla_ba_f=m_gla_ba_f,
               gla_wa2_b=m_gla_wa2_b, gla_ba_b=m_gla_ba_b, gla_norm_g=m_gla_norm_g, pool_w=m_pool_w,
               pool_scale=m_pool_scale, mla_q_norm_g=m_mla_q_norm_g, mla_wq_b=m_mla_wq_b,
               mla_kv_norm_g=m_mla_kv_norm_g, mla_wkv_b=m_mla_wkv_b, mla_qk_norm_q=m_mla_qk_norm_q,
               mla_qk_norm_k=m_mla_qk_norm_k, w_out=m_w_out),
        v=dict(norm_g=v_norm_g, w_in=v_w_in, ret_norm_g=v_ret_norm_g, gla_wa2_f=v_gla_wa2_f, gla_ba_f=v_gla_ba_f,
               gla_wa2_b=v_gla_wa2_b, gla_ba_b=v_gla_ba_b, gla_norm_g=v_gla_norm_g, pool_w=v_pool_w,
               pool_scale=v_pool_scale, mla_q_norm_g=v_mla_q_norm_g, mla_wq_b=v_mla_wq_b,
               mla_kv_norm_g=v_mla_kv_norm_g, mla_wkv_b=v_mla_wkv_b, mla_qk_norm_q=v_mla_qk_norm_q,
               mla_qk_norm_k=v_mla_qk_norm_k, w_out=v_w_out))

    def as_rows(name, dtype):
        a = p[name].astype(dtype)
        return a.reshape(a.shape[0] * a.shape[1], a.shape[2])

    shards = [as_rows("w_in", BF16), as_rows("w_out", BF16), as_rows("mla_wq_b", BF16), as_rows("mla_wkv_b", BF16),
              as_rows("gla_wa2_f", F32), as_rows("gla_wa2_b", F32)]
    got = _gather_shards(shards, name="gather_weights")
    gathered = {n: a.reshape((N_CHIP, DEPTH, a.shape[1] // DEPTH, a.shape[2])) for n, a in zip(SHARDED, got)}

    grad_x, sharded, small = _local_step(p, gathered)

    flat = [sharded[n].reshape(N_CHIP, -1, sharded[n].shape[-1]) for n in SHARDED]
    mine, theirs = _pair_exchange(flat, name="grad_pair_exchange")
    pair = [_add_n([a, b], name=f"grad_pair_sum_{n}") for n, a, b in zip(SHARDED, mine, theirs)]
    own, others = _chip_exchange(pair, name="grad_chip_exchange")
    halves = [_add_n([a, b[0], b[1], b[2]], name=f"grad_chip_sum_{n}") for n, a, b in zip(SHARDED, own, others)]
    joined = _pair_join(halves, name="grad_pair_join")
    grads = {n: a.reshape(p[n].shape) for n, a in zip(SHARDED, joined)}

    slots = _gather_all(_pack_small(small), name="gather_small")
    total = _unpack_small(_sum_slots(slots, name="sum_small"))
    for n, _ in SMALL:
        grads[n] = total[n].reshape(p[n].shape)
    loss = total["loss"]

    delta, new_m, new_v = {}, {}, {}
    for n in WEIGHTS:
        delta[n], new_m[n], new_v[n] = _adamw(p[n], grads[n], moments["m"][n], moments["v"][n], name=f"adamw_{n}")
    return (loss, grad_x, *[grads[n] for n in WEIGHTS], *[delta[n] for n in WEIGHTS],
            *[new_m[n] for n in WEIGHTS], *[new_v[n] for n in WEIGHTS])
```

```python
import functools
import math

import jax
import jax.numpy as jnp
from jax import lax
from jax.experimental import pallas as pl
from jax.experimental.pallas import tpu as pltpu

F32 = jnp.float32
BF16 = jnp.bfloat16
MESH = pl.DeviceIdType.MESH

EPS = 1e-6
ROPE_THETA = 10000.0
DEPTH = 2
N_DEV = 8
N_CHIP = 4

GROUP_W = 512
RET_HEADS = 4
RET_HD = 128
RET_CHUNK = 128
GLA_HEADS = 4
GLA_DK = 64
GLA_DV = 128
GLA_RANK = 16
GLA_TAU = 16.0
GLA_CHUNK = 64
POOL_GROUPS = 4
POOL_GW = 128
POOL_HALO = 8
POOL_TILE = 256
MLA_HEADS = 4
MLA_NOPE = 128
MLA_ROPE = 64
MLA_QK = MLA_NOPE + MLA_ROPE
MLA_QKP = 256
MLA_V = 128
MLA_Q_RANK = 512
MLA_KV_RANK = 256
MLA_SCALE = MLA_QK ** -0.5

ADAM_LR = 0.001
ADAM_B1 = 0.9
ADAM_B2 = 0.999
ADAM_EPS = 1e-08
ADAM_WD = 0.01
ADAM_STEP = 10

VMEM_LIMIT = 56 * 1024 * 1024

SEG = {
    "rq": (0, 512, 0, 512), "rk": (512, 512, 512, 512), "rv": (1024, 512, 1024, 512), "rg": (1536, 512, 1536, 512),
    "gv": (2048, 512, 2560, 512), "gg": (2560, 512, 3072, 512),
    "pv": (3072, 512, 3616, 512), "pg": (3584, 512, 4128, 512),
    "mq": (4096, 512, 4640, 512), "mg": (4608, 512, 5472, 512),
    "gq": (5120, 256, 2048, 256), "gk": (5376, 256, 2304, 256), "mkv": (5632, 256, 5152, 256),
    "ga": (5888, 128, 3584, 32), "mkr": (6016, 128, 5408, 64),
}
SEG_ORDER = ["rq", "rk", "rv", "rg", "gv", "gg", "pv", "pg", "mq", "mg", "gq", "gk", "mkv", "ga", "mkr"]
IN_COLS = 5984
IN_PAD = 6144
ORIG_ORDER = ["rq", "rk", "rv", "rg", "gq", "gk", "gv", "gg", "ga", "pv", "pg", "mq", "mkv", "mkr", "mg"]


def _cparams(*sem):
    return pltpu.CompilerParams(dimension_semantics=tuple(sem), vmem_limit_bytes=VMEM_LIMIT)


def _bf(v):
    return v.astype(BF16)


def _dot(a, b, ca=1, cb=0):
    return lax.dot_general(_bf(a), _bf(b), (((ca,), (cb,)), ((), ())), preferred_element_type=F32)


def _split_dot(a01, x, ca=1, cb=0):
    hi = _bf(x)
    r1 = x - hi.astype(F32)
    mid = _bf(r1)
    lo = _bf(r1 - mid.astype(F32))
    dn = (((ca,), (cb,)), ((), ()))
    a = _bf(a01)
    return (lax.dot_general(a, hi, dn, preferred_element_type=F32)
            + lax.dot_general(a, mid, dn, preferred_element_type=F32)
            + lax.dot_general(a, lo, dn, preferred_element_type=F32))


def _sigmoid(x):
    return 1.0 / (1.0 + jnp.exp(-x))


def _silu_parts(g):
    sg = _sigmoid(g)
    return g * sg, sg * (1.0 + g * (1.0 - sg))


def _matmul(a, b, *, ta=False, tb=False, out_dtype=F32, tm=512, tn=1024, tk=None, add=None, n_outer=True, name):
    m, kdim = (a.shape[1], a.shape[0]) if ta else a.shape
    n = b.shape[0] if tb else b.shape[1]
    tm, tn = min(tm, m), min(tn, n)
    tk = kdim if tk is None else min(tk, kdim)
    assert m % tm == 0 and n % tn == 0 and kdim % tk == 0
    nk = kdim // tk
    ca, cb = (0 if ta else 1), (1 if tb else 0)

    def body(*refs):
        if add is None:
            a_ref, b_ref, o_ref = refs[:3]
            add_ref = None
        else:
            a_ref, b_ref, add_ref, o_ref = refs[:4]
        p = _dot(a_ref[...], b_ref[...], ca, cb)

        def finish(r):
            if add_ref is not None:
                r = r + add_ref[...]
            o_ref[...] = r.astype(out_dtype)

        if nk == 1:
            finish(p)
        else:
            acc = refs[-1]
            k = pl.program_id(2)

            @pl.when(k == 0)
            def _():
                acc[...] = p

            @pl.when(k > 0)
            def _():
                acc[...] += p

            @pl.when(k == nk - 1)
            def _():
                finish(acc[...])

    def ij(g0, g1):
        return (g1, g0) if n_outer else (g0, g1)

    a_spec = (pl.BlockSpec((tk, tm), lambda g0, g1, k: (k, ij(g0, g1)[0])) if ta
              else pl.BlockSpec((tm, tk), lambda g0, g1, k: (ij(g0, g1)[0], k)))
    b_spec = (pl.BlockSpec((tn, tk), lambda g0, g1, k: (ij(g0, g1)[1], k)) if tb
              else pl.BlockSpec((tk, tn), lambda g0, g1, k: (k, ij(g0, g1)[1])))
    o_spec = pl.BlockSpec((tm, tn), lambda g0, g1, k: ij(g0, g1))
    in_specs = [a_spec, b_spec] + ([o_spec] if add is not None else [])
    args = (a, b) + ((add,) if add is not None else ())
    grid = (n // tn, m // tm, nk) if n_outer else (m // tm, n // tn, nk)
    return pl.pallas_call(
        body, name=name, grid=grid, in_specs=in_specs, out_specs=o_spec,
        out_shape=jax.ShapeDtypeStruct((m, n), out_dtype),
        scratch_shapes=[] if nk == 1 else [pltpu.VMEM((tm, tn), F32)],
        compiler_params=_cparams("parallel", "parallel", "arbitrary"),
    )(*args)


def _rmsnorm_fwd(x, g, *, name, tm=256):
    s, d = x.shape
    tm = min(tm, s)

    def body(x_ref, g_ref, h_ref):
        xv = x_ref[...]
        r = lax.rsqrt(jnp.mean(xv * xv, axis=-1, keepdims=True) + EPS)
        h_ref[...] = _bf(xv * r * g_ref[...])

    return pl.pallas_call(
        body, name=name, grid=(s // tm,),
        in_specs=[pl.BlockSpec((tm, d), lambda i: (i, 0)), pl.BlockSpec((1, d), lambda i: (0, 0))],
        out_specs=pl.BlockSpec((tm, d), lambda i: (i, 0)),
        out_shape=jax.ShapeDtypeStruct((s, d), BF16),
        compiler_params=_cparams("parallel"),
    )(x, g)


def _rmsnorm_bwd(x, dh, g, dres, *, name, tm=256):
    s, d = x.shape
    tm = min(tm, s)

    def body(x_ref, dh_ref, g_ref, dres_ref, dx_ref, dg_ref):
        i = pl.program_id(0)
        xv = x_ref[...]
        r = lax.rsqrt(jnp.mean(xv * xv, axis=-1, keepdims=True) + EPS)
        xn = xv * r
        dv = dh_ref[...]
        part = jnp.sum(dv * xn, axis=0, keepdims=True)

        @pl.when(i == 0)
        def _():
            dg_ref[...] = part

        @pl.when(i > 0)
        def _():
            dg_ref[...] += part

        dxn = dv * g_ref[...]
        dx_ref[...] = dres_ref[...] + r * (dxn - xn * jnp.mean(dxn * xn, axis=-1, keepdims=True))

    row = pl.BlockSpec((tm, d), lambda i: (i, 0))
    vec = pl.BlockSpec((1, d), lambda i: (0, 0))
    return pl.pallas_call(
        body, name=name, grid=(s // tm,), in_specs=[row, row, vec, row], out_specs=[row, vec],
        out_shape=[jax.ShapeDtypeStruct((s, d), F32), jax.ShapeDtypeStruct((1, d), F32)],
        compiler_params=_cparams("arbitrary"),
    )(x, dh, g, dres)


def _loss_head(xf, target, *, name, tm=256):
    s, d = xf.shape
    tm = min(tm, s)

    def body(x_ref, t_ref, dx_ref, l_ref):
        i = pl.program_id(0)
        e = x_ref[...] - t_ref[...]
        dx_ref[...] = e * (1.0 / d)
        rows = jnp.mean(e * e, axis=-1, keepdims=True)
        part = 0.5 * jnp.sum(rows, axis=0, keepdims=True)

        @pl.when(i == 0)
        def _():
            l_ref[...] = part

        @pl.when(i > 0)
        def _():
            l_ref[...] += part

    row = pl.BlockSpec((tm, d), lambda i: (i, 0))
    return pl.pallas_call(
        body, name=name, grid=(s // tm,), in_specs=[row, row],
        out_specs=[row, pl.BlockSpec((1, 1), lambda i: (0, 0))],
        out_shape=[jax.ShapeDtypeStruct((s, d), F32), jax.ShapeDtypeStruct((1, 1), F32)],
        compiler_params=_cparams("arbitrary"),
    )(xf, target)


def _rope_tables(s):
    pos = jnp.arange(s, dtype=F32)[:, None]
    inv_r = 1.0 / (ROPE_THETA ** (jnp.arange(0, RET_HD, 2, dtype=F32) / RET_HD))
    ang = pos * inv_r[None, :]
    ret_cos = jnp.concatenate([jnp.cos(ang), jnp.cos(ang)], axis=1)
    ret_sin = jnp.concatenate([-jnp.sin(ang), jnp.sin(ang)], axis=1)
    inv_m = 1.0 / (ROPE_THETA ** (jnp.arange(0, MLA_ROPE, 2, dtype=F32) / MLA_ROPE))
    am = pos * inv_m[None, :]
    z32, z64 = jnp.zeros((s, 32), F32), jnp.zeros((s, 64), F32)
    mla_cos = jnp.concatenate([jnp.cos(am), jnp.cos(am), z64], axis=1)
    mla_sp = jnp.concatenate([z32, jnp.sin(am), z64], axis=1)
    mla_sn = jnp.concatenate([-jnp.sin(am), z32, z64], axis=1)
    return ret_cos, ret_sin, mla_cos, mla_sp, mla_sn


def _rope128(x, c, sg):
    return x * c + pltpu.roll(x, 64, 1) * sg


def _unrope128(d, c, sg):
    return d * c + pltpu.roll(d * sg, 64, 1)


def _rope64(t, c, sp, sn):
    return t * c + pltpu.roll(t, 96, 1) * sn + pltpu.roll(t, 32, 1) * sp


def _unrope64(d, c, sp, sn):
    return d * c + pltpu.roll(d * sn, 32, 1) + pltpu.roll(d * sp, 96, 1)


def _ret_pre(z, cos, sin, *, name, tm=256):
    s = z.shape[0]
    tm = min(tm, s)
    scale = RET_HD ** -0.5

    def body(q_ref, k_ref, c_ref, s_ref, qo_ref, ko_ref):
        c, sg = c_ref[...], s_ref[...]
        for h in range(RET_HEADS):
            sl = slice(h * RET_HD, (h + 1) * RET_HD)
            qo_ref[:, sl] = _rope128(q_ref[:, sl], c, sg)
            ko_ref[:, sl] = _rope128(k_ref[:, sl], c, sg) * scale

    seg = lambda j: pl.BlockSpec((tm, GROUP_W), lambda i: (i, j))
    tab = pl.BlockSpec((tm, RET_HD), lambda i: (i, 0))
    return pl.pallas_call(
        body, name=name, grid=(s // tm,), in_specs=[seg(0), seg(1), tab, tab],
        out_specs=[seg(0), seg(0)],
        out_shape=[jax.ShapeDtypeStruct((s, GROUP_W), F32)] * 2,
        compiler_params=_cparams("parallel"),
    )(z, z, cos, sin)


def _ret_pre_bwd(dqr, dkr, cos, sin, *, name, tm=256):
    s = dqr.shape[1]
    tm = min(tm, s)
    scale = RET_HD ** -0.5

    def body(dq_ref, dk_ref, c_ref, s_ref, qo_ref, ko_ref):
        c, sg = c_ref[...], s_ref[...]
        for h in range(RET_HEADS):
            sl = slice(h * RET_HD, (h + 1) * RET_HD)
            qo_ref[:, sl] = _bf(_unrope128(dq_ref[0, :, sl] + dq_ref[1, :, sl], c, sg))
            ko_ref[:, sl] = _bf(_unrope128(dk_ref[0, :, sl] + dk_ref[1, :, sl], c, sg) * scale)

    two = pl.BlockSpec((2, tm, GROUP_W), lambda i: (0, i, 0))
    row = pl.BlockSpec((tm, GROUP_W), lambda i: (i, 0))
    tab = pl.BlockSpec((tm, RET_HD), lambda i: (i, 0))
    return pl.pallas_call(
        body, name=name, grid=(s // tm,), in_specs=[two, two, tab, tab], out_specs=[row, row],
        out_shape=[jax.ShapeDtypeStruct((s, GROUP_W), BF16)] * 2,
        compiler_params=_cparams("parallel"),
    )(dqr, dkr, cos, sin)


def _bla(a, b, c, lg, cols, *, name):
    s = a.shape[0]
    ch = min(RET_CHUNK, s)
    n = s // ch
    hd = RET_HD

    def body(lg_ref, a_ref, b_ref, c_ref, o_ref, st):
        d, h, t = pl.program_id(0), pl.program_id(1), pl.program_id(2)
        g = lg_ref[d, h]
        fwd = d == 0

        @pl.when(t == 0)
        def _():
            st[...] = jnp.zeros_like(st)

        av, bv, cv = a_ref[...], b_ref[...], c_ref[...]
        ii = lax.broadcasted_iota(jnp.int32, (ch, ch), 0)
        jj = lax.broadcasted_iota(jnp.int32, (ch, ch), 1)
        diff = jnp.where(fwd, ii - jj, jj - ii).astype(F32)
        dmat = jnp.where(diff >= 0, jnp.exp(jnp.maximum(diff, 0.0) * g), 0.0)
        sc = _dot(av, bv, 1, 1) * dmat
        inner = _dot(sc, cv)
        idx = lax.broadcasted_iota(jnp.int32, (ch, 1), 0).astype(F32)
        pq = jnp.where(fwd, idx + 1.0, ch - idx)
        pk = jnp.where(fwd, ch - 1.0 - idx, idx)
        cross = _dot(av * jnp.exp(pq * g), st[...])
        o_ref[0] = inner + cross
        st[...] = jnp.exp(ch * g) * st[...] + _dot(bv * jnp.exp(pk * g), cv, 0, 0)

    def rows(j):
        return pl.BlockSpec((ch, hd), lambda d, h, t: (t + d * (n - 1 - 2 * t), j * RET_HEADS + h))

    return pl.pallas_call(
        body, name=name, grid=(2, RET_HEADS, n),
        in_specs=[pl.BlockSpec(memory_space=pltpu.SMEM), rows(cols[0]), rows(cols[1]), rows(cols[2])],
        out_specs=pl.BlockSpec((1, ch, hd), lambda d, h, t: (d, t + d * (n - 1 - 2 * t), h)),
        out_shape=jax.ShapeDtypeStruct((2, s, GROUP_W), F32),
        scratch_shapes=[pltpu.VMEM((hd, hd), F32)],
        compiler_params=_cparams("arbitrary", "arbitrary", "arbitrary"),
    )(lg, a, b, c)


def _post(o2, zg, gcol, g, *, norm, name, tm=256):
    s = zg.shape[0]
    tm = min(tm, s)
    nd = o2.shape[0]

    def body(o_ref, gt_ref, g_ref, y_ref):
        silu, _ = _silu_parts(gt_ref[...])
        for h in range(4):
            sl = slice(h * 128, (h + 1) * 128)
            o = o_ref[0, :, sl]
            for k in range(1, nd):
                o = o + o_ref[k, :, sl]
            if norm:
                r = lax.rsqrt(jnp.mean(o * o, axis=-1, keepdims=True) + EPS)
                o = o * r * g_ref[:, sl]
            y_ref[:, sl] = _bf(silu[:, sl] * o)

    return pl.pallas_call(
        body, name=name, grid=(s // tm,),
        in_specs=[pl.BlockSpec((nd, tm, GROUP_W), lambda i: (0, i, 0)),
                  pl.BlockSpec((tm, GROUP_W), lambda i: (i, gcol)),
                  pl.BlockSpec((1, GROUP_W), lambda i: (0, 0))],
        out_specs=pl.BlockSpec((tm, GROUP_W), lambda i: (i, 0)),
        out_shape=jax.ShapeDtypeStruct((s, GROUP_W), BF16),
        compiler_params=_cparams("parallel"),
    )(o2, zg, g)


def _post_bwd(dy, ycol, o2, zg, gcol, g, *, norm, name, tm=256):
    s = zg.shape[0]
    tm = min(tm, s)
    nd = o2.shape[0]

    def body(dy_ref, o_ref, gt_ref, g_ref, dgt_ref, do_ref, dg_ref):
        i = pl.program_id(0)
        silu, dsilu = _silu_parts(gt_ref[...])
        dyv = dy_ref[...]
        parts = []
        for h in range(4):
            sl = slice(h * 128, (h + 1) * 128)
            o = o_ref[0, :, sl]
            for k in range(1, nd):
                o = o + o_ref[k, :, sl]
            dn = dyv[:, sl] * silu[:, sl]
            if norm:
                r = lax.rsqrt(jnp.mean(o * o, axis=-1, keepdims=True) + EPS)
                xn = o * r
                gh = g_ref[:, sl]
                dgt_ref[:, sl] = _bf(dyv[:, sl] * (xn * gh) * dsilu[:, sl])
                parts.append(jnp.sum(dn * xn, axis=0, keepdims=True))
                dxn = dn * gh
                do_ref[:, sl] = r * (dxn - xn * jnp.mean(dxn * xn, axis=-1, keepdims=True))
            else:
                dgt_ref[:, sl] = _bf(dyv[:, sl] * o * dsilu[:, sl])
                parts.append(jnp.zeros((1, 128), F32))
                do_ref[:, sl] = dn
        part = jnp.concatenate(parts, axis=1)

        @pl.when(i == 0)
        def _():
            dg_ref[...] = part

        @pl.when(i > 0)
        def _():
            dg_ref[...] += part

    row = pl.BlockSpec((tm, GROUP_W), lambda i: (i, 0))
    vec = pl.BlockSpec((1, GROUP_W), lambda i: (0, 0))
    return pl.pallas_call(
        body, name=name, grid=(s // tm,),
        in_specs=[pl.BlockSpec((tm, GROUP_W), lambda i: (i, ycol)),
                  pl.BlockSpec((nd, tm, GROUP_W), lambda i: (0, i, 0)),
                  pl.BlockSpec((tm, GROUP_W), lambda i: (i, gcol)), vec],
        out_specs=[row, row, vec],
        out_shape=[jax.ShapeDtypeStruct((s, GROUP_W), BF16), jax.ShapeDtypeStruct((s, GROUP_W), F32),
                   jax.ShapeDtypeStruct((1, GROUP_W), F32)],
        compiler_params=_cparams("arbitrary"),
    )(dy, o2, zg, g)


def _ret_log_gamma(swap):
    gf = 1.0 - 2.0 ** (-5.0 - jnp.arange(RET_HEADS, dtype=F32))
    lf, lb = jnp.log(gf), jnp.log(gf[::-1])
    return jnp.stack([lb, lf] if swap else [lf, lb])


def _log_sigmoid(x):
    return jnp.minimum(x, 0.0) - jnp.log(1.0 + jnp.exp(-jnp.abs(x)))


def _gla_gate(z, wa, ba, *, name, tm=256):
    s = z.shape[0]
    tm = min(tm, s)
    col = SEG["ga"][0] // 128

    def body(ga_ref, wa_ref, ba_ref, la_ref):
        pre = _dot(ga_ref[...], wa_ref[...]) + ba_ref[...]
        la_ref[...] = _log_sigmoid(pre) / GLA_TAU

    return pl.pallas_call(
        body, name=name, grid=(s // tm,),
        in_specs=[pl.BlockSpec((tm, 128), lambda i: (i, col)), pl.BlockSpec((128, 512), lambda i: (0, 0)),
                  pl.BlockSpec((1, 512), lambda i: (0, 0))],
        out_specs=pl.BlockSpec((tm, 512), lambda i: (i, 0)),
        out_shape=jax.ShapeDtypeStruct((s, 512), F32),
        compiler_params=_cparams("parallel"),
    )(z, wa, ba)


def _gla_gate_bwd(dla, z, wa, ba, *, name, tm=256):
    s = z.shape[0]
    tm = min(tm, s)
    col = SEG["ga"][0] // 128

    def body(dla_ref, ga_ref, wa_ref, ba_ref, dga_ref, dwa_ref, dba_ref):
        i = pl.program_id(0)
        gav = ga_ref[...]
        pre = _dot(gav, wa_ref[...]) + ba_ref[...]
        dpre = dla_ref[...] * (1.0 - _sigmoid(pre)) * (1.0 / GLA_TAU)
        dga_ref[...] = _bf(_dot(dpre, wa_ref[...], 1, 1))
        pw = _dot(gav, dpre, 0, 0)
        pb = jnp.sum(dpre, axis=0, keepdims=True)

        @pl.when(i == 0)
        def _():
            dwa_ref[...] = pw
            dba_ref[...] = pb

        @pl.when(i > 0)
        def _():
            dwa_ref[...] += pw
            dba_ref[...] += pb

    return pl.pallas_call(
        body, name=name, grid=(s // tm,),
        in_specs=[pl.BlockSpec((tm, 512), lambda i: (i, 0)), pl.BlockSpec((tm, 128), lambda i: (i, col)),
                  pl.BlockSpec((128, 512), lambda i: (0, 0)), pl.BlockSpec((1, 512), lambda i: (0, 0))],
        out_specs=[pl.BlockSpec((tm, 128), lambda i: (i, 0)), pl.BlockSpec((128, 512), lambda i: (0, 0)),
                   pl.BlockSpec((1, 512), lambda i: (0, 0))],
        out_shape=[jax.ShapeDtypeStruct((s, 128), BF16), jax.ShapeDtypeStruct((128, 512), F32),
                   jax.ShapeDtypeStruct((1, 512), F32)],
        compiler_params=_cparams("arbitrary"),
    )(dla, z, wa, ba)


def _gla_chunk(d, q_ref, k_ref, la_ref, ch):
    fwd = d == 0
    ii = lax.broadcasted_iota(jnp.int32, (ch, ch), 0)
    tt = lax.broadcasted_iota(jnp.int32, (ch, ch), 1)
    tmat = jnp.where(jnp.where(fwd, ii - tt, tt - ii) >= 0, 1.0, 0.0)
    c = _split_dot(tmat, la_ref[0, 0])
    big_l = jnp.where(fwd, c[ch - 1:ch, :], c[0:1, :])
    qv = q_ref[0] * (GLA_DK ** -0.5)
    kv = k_ref[0]
    qt = qv * jnp.exp(c)
    kt = kv * jnp.exp(-c)
    kh = kv * jnp.exp(big_l - c)
    return tmat, c, big_l, qv, kv, qt, kt, kh


def _gla_fwd(qh, kh_, z, la, *, name):
    s = z.shape[0]
    ch = min(GLA_CHUNK, s)
    n = s // ch
    vcol = SEG["gv"][0] // 128

    def body(q_ref, k_ref, v_ref, la_ref, o_ref, zs_ref, st):
        d, t = pl.program_id(0), pl.program_id(2)

        @pl.when(t == 0)
        def _():
            st[...] = jnp.zeros_like(st)

        tmat, c, big_l, qv, kv, qt, kt, kh = _gla_chunk(d, q_ref, k_ref, la_ref, ch)
        vv = v_ref[...]
        p = _dot(qt, kt, 1, 1) * tmat
        zst = st[...]
        o_ref[0] = _dot(p, vv) + _dot(qt, zst, 1, 1)
        zs_ref[0, 0, 0] = zst
        st[...] = zst * jnp.exp(big_l) + _dot(vv, kh, 0, 0)

    cidx = lambda d, t: t + d * (n - 1 - 2 * t)
    hs = pl.BlockSpec((1, ch, GLA_DK), lambda d, h, t: (h, cidx(d, t), 0))
    return pl.pallas_call(
        body, name=name, grid=(2, GLA_HEADS, n),
        in_specs=[hs, hs, pl.BlockSpec((ch, GLA_DV), lambda d, h, t: (cidx(d, t), vcol + h)),
                  pl.BlockSpec((1, 1, ch, GLA_DK), lambda d, h, t: (d, h, cidx(d, t), 0))],
        out_specs=[pl.BlockSpec((1, ch, GLA_DV), lambda d, h, t: (d, cidx(d, t), h)),
                   pl.BlockSpec((1, 1, 1, GLA_DV, GLA_DK), lambda d, h, t: (d, h, cidx(d, t), 0, 0))],
        out_shape=[jax.ShapeDtypeStruct((2, s, GROUP_W), F32),
                   jax.ShapeDtypeStruct((2, GLA_HEADS, n, GLA_DV, GLA_DK), F32)],
        scratch_shapes=[pltpu.VMEM((GLA_DV, GLA_DK), F32)],
        compiler_params=_cparams("arbitrary", "arbitrary", "arbitrary"),
    )(qh, kh_, z, la)


def _gla_bwd(qh, kh_, z, la, do, zs, *, name):
    s = z.shape[0]
    ch = min(GLA_CHUNK, s)
    n = s // ch
    vcol = SEG["gv"][0] // 128

    def body(q_ref, k_ref, v_ref, la_ref, do_ref, zs_ref, dq_ref, dk_ref, dla_ref, dv_ref, gz):
        d, t = pl.program_id(0), pl.program_id(2)

        @pl.when(t == 0)
        def _():
            gz[...] = jnp.zeros_like(gz)

        tmat, c, big_l, qv, kv, qt, kt, kh = _gla_chunk(d, q_ref, k_ref, la_ref, ch)
        vv, dov, zst, gzv = v_ref[...], do_ref[...], zs_ref[0, 0, 0], gz[...]
        p = _dot(qt, kt, 1, 1) * tmat
        dp = _dot(dov, vv, 1, 1) * tmat
        dqt = _dot(dp, kt) + _dot(dov, zst)
        dkt = _dot(dp, qt, 0, 0)
        dkh = _dot(vv, gzv)
        dv_ref[0] = _dot(p, dov, 0, 0) + _dot(kh, gzv, 1, 1)
        ec = jnp.exp(c)
        dq_ref[0, 0] = dqt * ec * (GLA_DK ** -0.5)
        dk_ref[0, 0] = dkt * jnp.exp(-c) + dkh * jnp.exp(big_l - c)
        e_l = jnp.exp(big_l)
        d_l = jnp.sum(dkh * kh, axis=0, keepdims=True) + e_l * jnp.sum(zst * gzv, axis=0, keepdims=True)
        rows = lax.broadcasted_iota(jnp.int32, (ch, 1), 0)
        end = jnp.where(d == 0, ch - 1, 0)
        dc = dqt * qt - dkt * kt - dkh * kh + jnp.where(rows == end, d_l, 0.0)
        dla_ref[0, 0] = _split_dot(tmat, dc, 0, 0)
        gz[...] = gzv * e_l + _dot(dov, qt, 0, 0)

    cidx = lambda d, t: (n - 1 - t) + d * (2 * t - (n - 1))
    hs = pl.BlockSpec((1, ch, GLA_DK), lambda d, h, t: (h, cidx(d, t), 0))
    dhs = pl.BlockSpec((1, 1, ch, GLA_DK), lambda d, h, t: (d, h, cidx(d, t), 0))
    return pl.pallas_call(
        body, name=name, grid=(2, GLA_HEADS, n),
        in_specs=[hs, hs, pl.BlockSpec((ch, GLA_DV), lambda d, h, t: (cidx(d, t), vcol + h)), dhs,
                  pl.BlockSpec((ch, GLA_DV), lambda d, h, t: (cidx(d, t), h)),
                  pl.BlockSpec((1, 1, 1, GLA_DV, GLA_DK), lambda d, h, t: (d, h, cidx(d, t), 0, 0))],
        out_specs=[dhs, dhs, dhs, pl.BlockSpec((1, ch, GLA_DV), lambda d, h, t: (d, cidx(d, t), h))],
        out_shape=[jax.ShapeDtypeStruct((2, GLA_HEADS, s, GLA_DK), F32)] * 3
        + [jax.ShapeDtypeStruct((2, s, GROUP_W), F32)],
        scratch_shapes=[pltpu.VMEM((GLA_DV, GLA_DK), F32)],
        compiler_params=_cparams("arbitrary", "arbitrary", "arbitrary"),
    )(qh, kh_, z, la, do, zs)


def _band(lo, hi, rows, width):
    r = lax.broadcasted_iota(jnp.int32, (rows, width), 0)
    j = lax.broadcasted_iota(jnp.int32, (rows, width), 1)
    k = j - POOL_HALO - r
    return jnp.where((k >= lo) & (k <= hi), 1.0, 0.0)


def _pool_cnt(t0, half, rows, s):
    t = t0 + lax.broadcasted_iota(jnp.int32, (rows, 1), 0)
    return (jnp.minimum(t + half, s) - jnp.maximum(t - half, 0)).astype(F32)


def _pool_fwd(z, pw, scale, *, name):
    s = z.shape[0]
    tl = min(POOL_TILE, s)
    nt = s // tl
    ucol, gcol = SEG["pv"][0] // 128, SEG["pg"][0] // 128

    def body(u_ref, gt_ref, pw_ref, sc_ref, y_ref, pad):
        g = pl.program_id(0)
        half = jnp.left_shift(1, g)
        pad[0:POOL_HALO, :] = jnp.zeros((POOL_HALO, POOL_GW), F32)
        pad[POOL_HALO + s:POOL_HALO + s + POOL_HALO, :] = jnp.zeros((POOL_HALO, POOL_GW), F32)
        pad[POOL_HALO:POOL_HALO + s, :] = u_ref[...]
        band = _band(-half, half - 1, tl, tl + 2 * POOL_HALO)
        pwv, scv = pw_ref[0], sc_ref[...]

        def tile(i, carry):
            t0 = pl.multiple_of(i * tl, tl)
            win = pad[pl.ds(t0, tl + 2 * POOL_HALO), :]
            u = win[POOL_HALO:POOL_HALO + tl, :]
            pooled = _split_dot(band, win) / _pool_cnt(t0, half, tl, s) - u
            mixed = _dot(pooled, pwv)
            silu, _ = _silu_parts(gt_ref[pl.ds(t0, tl), :])
            y_ref[pl.ds(t0, tl), :] = _bf(silu * (mixed * scv))
            return carry

        lax.fori_loop(0, nt, tile, 0)

    return pl.pallas_call(
        body, name=name, grid=(POOL_GROUPS,),
        in_specs=[pl.BlockSpec((s, POOL_GW), lambda g: (0, ucol + g)),
                  pl.BlockSpec((s, POOL_GW), lambda g: (0, gcol + g)),
                  pl.BlockSpec((1, POOL_GW, POOL_GW), lambda g: (g, 0, 0)),
                  pl.BlockSpec((1, POOL_GW), lambda g: (0, g))],
        out_specs=pl.BlockSpec((s, POOL_GW), lambda g: (0, g)),
        out_shape=jax.ShapeDtypeStruct((s, GROUP_W), BF16),
        scratch_shapes=[pltpu.VMEM((s + 2 * POOL_HALO, POOL_GW), F32)],
        compiler_params=_cparams("parallel"),
    )(z, z, pw, scale)


def _pool_bwd(dy, z, pw, scale, *, name):
    s = z.shape[0]
    tl = min(POOL_TILE, s)
    nt = s // tl
    ucol, gcol, ycol = SEG["pv"][0] // 128, SEG["pg"][0] // 128, 2 * GROUP_W // 128

    def body(dy_ref, u_ref, gt_ref, pw_ref, sc_ref, du_ref, dgt_ref, dpw_ref, dsc_ref, pad, epad, dpo):
        g = pl.program_id(0)
        half = jnp.left_shift(1, g)
        zeros = jnp.zeros((POOL_HALO, POOL_GW), F32)
        for buf in (pad, epad):
            buf[0:POOL_HALO, :] = zeros
            buf[POOL_HALO + s:POOL_HALO + s + POOL_HALO, :] = zeros
        pad[POOL_HALO:POOL_HALO + s, :] = u_ref[...]
        band = _band(-half, half - 1, tl, tl + 2 * POOL_HALO)
        band_t = _band(1 - half, half, tl, tl + 2 * POOL_HALO)
        pwv, scv = pw_ref[0], sc_ref[...]
        dpw_ref[0] = jnp.zeros((POOL_GW, POOL_GW), F32)
        dsc_ref[...] = jnp.zeros((1, POOL_GW), F32)

        def tile(i, carry):
            t0 = pl.multiple_of(i * tl, tl)
            win = pad[pl.ds(t0, tl + 2 * POOL_HALO), :]
            u = win[POOL_HALO:POOL_HALO + tl, :]
            cnt = _pool_cnt(t0, half, tl, s)
            pooled = _split_dot(band, win) / cnt - u
            mixed = _dot(pooled, pwv)
            silu, dsilu = _silu_parts(gt_ref[pl.ds(t0, tl), :])
            dyv = dy_ref[pl.ds(t0, tl), :]
            dgt_ref[pl.ds(t0, tl), :] = _bf(dyv * (mixed * scv) * dsilu)
            dsc_ref[...] += jnp.sum(dyv * silu * mixed, axis=0, keepdims=True)
            dm = dyv * silu * scv
            dpw_ref[0] += _dot(pooled, dm, 0, 0)
            dpooled = _dot(dm, pwv, 1, 1)
            dpo[pl.ds(t0, tl), :] = dpooled
            epad[pl.ds(POOL_HALO + t0, tl), :] = dpooled / cnt
            return carry

        lax.fori_loop(0, nt, tile, 0)

        def tile2(i, carry):
            t0 = pl.multiple_of(i * tl, tl)
            ewin = epad[pl.ds(t0, tl + 2 * POOL_HALO), :]
            du_ref[pl.ds(t0, tl), :] = _bf(_split_dot(band_t, ewin) - dpo[pl.ds(t0, tl), :])
            return carry

        lax.fori_loop(0, nt, tile2, 0)

    col = lambda c0: pl.BlockSpec((s, POOL_GW), lambda g: (0, c0 + g))
    return pl.pallas_call(
        body, name=name, grid=(POOL_GROUPS,),
        in_specs=[col(ycol), col(ucol), col(gcol), pl.BlockSpec((1, POOL_GW, POOL_GW), lambda g: (g, 0, 0)),
                  pl.BlockSpec((1, POOL_GW), lambda g: (0, g))],
        out_specs=[col(0), col(0), pl.BlockSpec((1, POOL_GW, POOL_GW), lambda g: (g, 0, 0)),
                   pl.BlockSpec((1, POOL_GW), lambda g: (0, g))],
        out_shape=[jax.ShapeDtypeStruct((s, GROUP_W), BF16), jax.ShapeDtypeStruct((s, GROUP_W), BF16),
                   jax.ShapeDtypeStruct((POOL_GROUPS, POOL_GW, POOL_GW), F32),
                   jax.ShapeDtypeStruct((1, GROUP_W), F32)],
        scratch_shapes=[pltpu.VMEM((s + 2 * POOL_HALO, POOL_GW), F32), pltpu.VMEM((s + 2 * POOL_HALO, POOL_GW), F32),
                        pltpu.VMEM((s, POOL_GW), F32)],
        compiler_params=_cparams("parallel"),
    )(dy, z, z, pw, scale)


def _mla_specs(tm):
    zq = pl.BlockSpec((tm, 512), lambda i: (i, SEG["mq"][0] // 512))
    zkv = pl.BlockSpec((tm, 256), lambda i: (i, SEG["mkv"][0] // 256))
    zkr = pl.BlockSpec((tm, 128), lambda i: (i, SEG["mkr"][0] // 128))
    full = lambda r, c: pl.BlockSpec((r, c), lambda i: (0, 0))
    tab = pl.BlockSpec((tm, 128), lambda i: (i, 0))
    weights = [full(1, 512), full(512, 1024), full(1, 256), full(256, 1024), full(1, 256), full(1, 256)]
    return [zq, zkv, zkr] + weights + [tab, tab, tab]


def _mla_project(xq_ref, xkv_ref, qg_ref, wq_ref, kvg_ref, wkv_ref):
    xq = xq_ref[...]
    r1 = lax.rsqrt(jnp.mean(xq * xq, axis=-1, keepdims=True) + EPS)
    xn1 = xq * r1
    qn = _bf(xn1 * qg_ref[...])
    qraw = _dot(qn, wq_ref[...])
    xkv = xkv_ref[...]
    r2 = lax.rsqrt(jnp.mean(xkv * xkv, axis=-1, keepdims=True) + EPS)
    xn2 = xkv * r2
    kvn = _bf(xn2 * kvg_ref[...])
    kvraw = _dot(kvn, wkv_ref[...])
    return r1, xn1, qn, qraw, r2, xn2, kvn, kvraw


def _mla_pre(z, qg, wq, kvg, wkv, qng, kng, cos, sp, sn, *, name, tm=256):
    s = z.shape[0]
    tm = min(tm, s)

    def body(xq_ref, xkv_ref, pe_ref, qg_ref, wq_ref, kvg_ref, wkv_ref, qng_ref, kng_ref, c_ref, sp_ref, sn_ref,
             q_ref, k_ref, v_ref):
        _, _, _, qraw, _, _, _, kvraw = _mla_project(xq_ref, xkv_ref, qg_ref, wq_ref, kvg_ref, wkv_ref)
        c, spv, snv = c_ref[...], sp_ref[...], sn_ref[...]
        pe = pe_ref[...]
        pe_ss = jnp.sum(pe * pe, axis=-1, keepdims=True)
        qngv, kngv = qng_ref[...], kng_ref[...]
        for h in range(MLA_HEADS):
            b = h * MLA_QKP
            qh = qraw[:, b:b + MLA_QKP]
            r = lax.rsqrt(jnp.sum(qh * qh, axis=-1, keepdims=True) * (1.0 / MLA_QK) + EPS)
            qn_h = qh * r * qngv
            q_ref[:, b:b + 128] = _bf(qn_h[:, :128] * MLA_SCALE)
            q_ref[:, b + 128:b + 256] = _bf(_rope64(qn_h[:, 128:], c, spv, snv) * MLA_SCALE)
            kn = kvraw[:, b:b + 128]
            rk = lax.rsqrt((jnp.sum(kn * kn, axis=-1, keepdims=True) + pe_ss) * (1.0 / MLA_QK) + EPS)
            k_ref[:, b:b + 128] = _bf(kn * rk * kngv[:, :128])
            k_ref[:, b + 128:b + 256] = _bf(_rope64(pe * rk * kngv[:, 128:], c, spv, snv))
            v_ref[:, h * MLA_V:(h + 1) * MLA_V] = _bf(kvraw[:, b + 128:b + 256])

    row = lambda w: pl.BlockSpec((tm, w), lambda i: (i, 0))
    return pl.pallas_call(
        body, name=name, grid=(s // tm,), in_specs=_mla_specs(tm),
        out_specs=[row(1024), row(1024), row(512)],
        out_shape=[jax.ShapeDtypeStruct((s, 1024), BF16), jax.ShapeDtypeStruct((s, 1024), BF16),
                   jax.ShapeDtypeStruct((s, 512), BF16)],
        compiler_params=_cparams("parallel"),
    )(z, z, z, qg, wq, kvg, wkv, qng, kng, cos, sp, sn)


def _mla_pre_bwd(dq, dk, dv, z, qg, wq, kvg, wkv, qng, kng, cos, sp, sn, *, name, tm=256):
    s = z.shape[0]
    tm = min(tm, s)

    def body(dq_ref, dk_ref, dv_ref, xq_ref, xkv_ref, pe_ref, qg_ref, wq_ref, kvg_ref, wkv_ref, qng_ref, kng_ref,
             c_ref, sp_ref, sn_ref, dxq_ref, dxkv_ref, dpe_ref, dwq_ref, dwkv_ref, dqg_ref, dkvg_ref, dqng_ref,
             dkng_ref, dqraw, dkvraw):
        i = pl.program_id(0)
        r1, xn1, qn, qraw, r2, xn2, kvn, kvraw = _mla_project(xq_ref, xkv_ref, qg_ref, wq_ref, kvg_ref, wkv_ref)
        c, spv, snv = c_ref[...], sp_ref[...], sn_ref[...]
        pe = pe_ref[...]
        pe_ss = jnp.sum(pe * pe, axis=-1, keepdims=True)
        qngv, kngv = qng_ref[...], kng_ref[...]
        dqng = jnp.zeros((1, MLA_QKP), F32)
        dkng = jnp.zeros((1, MLA_QKP), F32)
        dpe = jnp.zeros_like(pe)
        for h in range(MLA_HEADS):
            b = h * MLA_QKP
            qh = qraw[:, b:b + MLA_QKP]
            r = lax.rsqrt(jnp.sum(qh * qh, axis=-1, keepdims=True) * (1.0 / MLA_QK) + EPS)
            xn = qh * r
            d_n = jnp.concatenate(
                [dq_ref[:, b:b + 128], _unrope64(dq_ref[:, b + 128:b + 256], c, spv, snv)], axis=1) * MLA_SCALE
            dqng = dqng + jnp.sum(d_n * xn, axis=0, keepdims=True)
            dxn = d_n * qngv
            dqraw[:, b:b + MLA_QKP] = _bf(r * (dxn - xn * (jnp.sum(dxn * xn, axis=-1, keepdims=True) * (1.0 / MLA_QK))))
            kn = kvraw[:, b:b + 128]
            rk = lax.rsqrt((jnp.sum(kn * kn, axis=-1, keepdims=True) + pe_ss) * (1.0 / MLA_QK) + EPS)
            xk = jnp.concatenate([kn, pe], axis=1) * rk
            d_k = jnp.concatenate(
                [dk_ref[:, b:b + 128], _unrope64(dk_ref[:, b + 128:b + 256], c, spv, snv)], axis=1)
            dkng = dkng + jnp.sum(d_k * xk, axis=0, keepdims=True)
            dxk = d_k * kngv
            dfull = rk * (dxk - xk * (jnp.sum(dxk * xk, axis=-1, keepdims=True) * (1.0 / MLA_QK)))
            dkvraw[:, b:b + 128] = _bf(dfull[:, :128])
            dkvraw[:, b + 128:b + 256] = _bf(dv_ref[:, h * MLA_V:(h + 1) * MLA_V])
            dpe = dpe + dfull[:, 128:]
        dpe_ref[...] = _bf(dpe)
        dqr, dkvr = dqraw[...], dkvraw[...]
        dqn = _dot(dqr, wq_ref[...], 1, 1)
        dxn1 = dqn * qg_ref[...]
        dxq_ref[...] = _bf(r1 * (dxn1 - xn1 * jnp.mean(dxn1 * xn1, axis=-1, keepdims=True)))
        dkvn = _dot(dkvr, wkv_ref[...], 1, 1)
        dxn2 = dkvn * kvg_ref[...]
        dxkv_ref[...] = _bf(r2 * (dxn2 - xn2 * jnp.mean(dxn2 * xn2, axis=-1, keepdims=True)))
        parts = (_dot(qn, dqr, 0, 0), _dot(kvn, dkvr, 0, 0), jnp.sum(dqn * xn1, axis=0, keepdims=True),
                 jnp.sum(dkvn * xn2, axis=0, keepdims=True), dqng, dkng)
        accs = (dwq_ref, dwkv_ref, dqg_ref, dkvg_ref, dqng_ref, dkng_ref)

        @pl.when(i == 0)
        def _():
            for a, p in zip(accs, parts):
                a[...] = p

        @pl.when(i > 0)
        def _():
            for a, p in zip(accs, parts):
                a[...] += p

    row = lambda w: pl.BlockSpec((tm, w), lambda i: (i, 0))
    full = lambda r, c: pl.BlockSpec((r, c), lambda i: (0, 0))
    return pl.pallas_call(
        body, name=name, grid=(s // tm,),
        in_specs=[row(1024), row(1024), row(512)] + _mla_specs(tm),
        out_specs=[row(512), row(256), row(128), full(512, 1024), full(256, 1024), full(1, 512), full(1, 256),
                   full(1, 256), full(1, 256)],
        out_shape=[jax.ShapeDtypeStruct((s, 512), BF16), jax.ShapeDtypeStruct((s, 256), BF16),
                   jax.ShapeDtypeStruct((s, 128), BF16), jax.ShapeDtypeStruct((512, 1024), F32),
                   jax.ShapeDtypeStruct((256, 1024), F32), jax.ShapeDtypeStruct((1, 512), F32),
                   jax.ShapeDtypeStruct((1, 256), F32), jax.ShapeDtypeStruct((1, 256), F32),
                   jax.ShapeDtypeStruct((1, 256), F32)],
        scratch_shapes=[pltpu.VMEM((tm, 1024), BF16), pltpu.VMEM((tm, 1024), BF16)],
        compiler_params=_cparams("arbitrary"),
    )(dq, dk, dv, z, z, z, qg, wq, kvg, wkv, qng, kng, cos, sp, sn)


def _flash_fwd(q, k, v, *, name, tq=512, tk=512):
    s = q.shape[0]
    tq, tk = min(tq, s), min(tk, s)
    nk = s // tk

    def body(q_ref, k_ref, v_ref, o_ref, lse_ref, m_s, l_s, acc):
        j = pl.program_id(2)

        @pl.when(j == 0)
        def _():
            m_s[...] = jnp.full_like(m_s, -jnp.inf)
            l_s[...] = jnp.zeros_like(l_s)
            acc[...] = jnp.zeros_like(acc)

        sc = _dot(q_ref[...], k_ref[...], 1, 1)
        m_prev = m_s[...]
        m_new = jnp.maximum(m_prev, jnp.max(sc, axis=-1, keepdims=True))
        p = jnp.exp(sc - m_new[:, 0:1])
        alpha = jnp.exp(m_prev - m_new)
        l_s[...] = alpha * l_s[...] + jnp.sum(p, axis=-1, keepdims=True)
        acc[...] = alpha * acc[...] + _dot(p, v_ref[...])
        m_s[...] = m_new

        @pl.when(j == nk - 1)
        def _():
            o_ref[0] = acc[...] / l_s[...]
            lse_ref[...] = m_s[...] + jnp.log(l_s[...])

    return pl.pallas_call(
        body, name=name, grid=(MLA_HEADS, s // tq, nk),
        in_specs=[pl.BlockSpec((tq, MLA_QKP), lambda h, i, j: (i, h)),
                  pl.BlockSpec((tk, MLA_QKP), lambda h, i, j: (j, h)),
                  pl.BlockSpec((tk, MLA_V), lambda h, i, j: (j, h))],
        out_specs=[pl.BlockSpec((1, tq, MLA_V), lambda h, i, j: (0, i, h)),
                   pl.BlockSpec((tq, MLA_V), lambda h, i, j: (i, h))],
        out_shape=[jax.ShapeDtypeStruct((1, s, GROUP_W), F32), jax.ShapeDtypeStruct((s, GROUP_W), F32)],
        scratch_shapes=[pltpu.VMEM((tq, MLA_V), F32), pltpu.VMEM((tq, MLA_V), F32), pltpu.VMEM((tq, MLA_V), F32)],
        compiler_params=_cparams("parallel", "parallel", "arbitrary"),
    )(q, k, v)


def _flash_bwd_dq(q, k, v, do, o, lse, *, name, tq=512, tk=512):
    s = q.shape[0]
    tq, tk = min(tq, s), min(tk, s)
    nk = s // tk

    def body(q_ref, k_ref, v_ref, do_ref, o_ref, lse_ref, dq_ref, acc):
        j = pl.program_id(2)
        dov = do_ref[...]
        delta = jnp.sum(dov * o_ref[0], axis=-1, keepdims=True)
        p = jnp.exp(_dot(q_ref[...], k_ref[...], 1, 1) - lse_ref[:, 0:1])
        ds = p * (_dot(dov, v_ref[...], 1, 1) - delta)
        part = _dot(ds, k_ref[...])

        @pl.when(j == 0)
        def _():
            acc[...] = part

        @pl.when(j > 0)
        def _():
            acc[...] += part

        @pl.when(j == nk - 1)
        def _():
            dq_ref[...] = acc[...]

    qb = pl.BlockSpec((tq, MLA_QKP), lambda h, i, j: (i, h))
    ob = pl.BlockSpec((tq, MLA_V), lambda h, i, j: (i, h))
    return pl.pallas_call(
        body, name=name, grid=(MLA_HEADS, s // tq, nk),
        in_specs=[qb, pl.BlockSpec((tk, MLA_QKP), lambda h, i, j: (j, h)),
                  pl.BlockSpec((tk, MLA_V), lambda h, i, j: (j, h)), ob,
                  pl.BlockSpec((1, tq, MLA_V), lambda h, i, j: (0, i, h)), ob],
        out_specs=qb,
        out_shape=jax.ShapeDtypeStruct((s, MLA_HEADS * MLA_QKP), F32),
        scratch_shapes=[pltpu.VMEM((tq, MLA_QKP), F32)],
        compiler_params=_cparams("parallel", "parallel", "arbitrary"),
    )(q, k, v, do, o, lse)


def _flash_bwd_dkv(q, k, v, do, o, lse, *, name, tq=512, tk=512):
    s = q.shape[0]
    tq, tk = min(tq, s), min(tk, s)
    nq = s // tq

    def body(q_ref, k_ref, v_ref, do_ref, o_ref, lse_ref, dk_ref, dv_ref, dk_acc, dv_acc):
        i = pl.program_id(2)
        dov = do_ref[...]
        delta = jnp.sum(dov * o_ref[0], axis=-1, keepdims=True)
        p = jnp.exp(_dot(q_ref[...], k_ref[...], 1, 1) - lse_ref[:, 0:1])
        ds = p * (_dot(dov, v_ref[...], 1, 1) - delta)
        pv = _dot(p, dov, 0, 0)
        pk = _dot(ds, q_ref[...], 0, 0)

        @pl.when(i == 0)
        def _():
            dv_acc[...] = pv
            dk_acc[...] = pk

        @pl.when(i > 0)
        def _():
            dv_acc[...] += pv
            dk_acc[...] += pk

        @pl.when(i == nq - 1)
        def _():
            dk_ref[...] = dk_acc[...]
            dv_ref[...] = dv_acc[...]

    kb = pl.BlockSpec((tk, MLA_QKP), lambda h, j, i: (j, h))
    vb = pl.BlockSpec((tk, MLA_V), lambda h, j, i: (j, h))
    ob = pl.BlockSpec((tq, MLA_V), lambda h, j, i: (i, h))
    return pl.pallas_call(
        body, name=name, grid=(MLA_HEADS, s // tk, nq),
        in_specs=[pl.BlockSpec((tq, MLA_QKP), lambda h, j, i: (i, h)), kb, vb, ob,
                  pl.BlockSpec((1, tq, MLA_V), lambda h, j, i: (0, i, h)), ob],
        out_specs=[kb, vb],
        out_shape=[jax.ShapeDtypeStruct((s, MLA_HEADS * MLA_QKP), F32), jax.ShapeDtypeStruct((s, GROUP_W), F32)],
        scratch_shapes=[pltpu.VMEM((tk, MLA_QKP), F32), pltpu.VMEM((tk, MLA_V), F32)],
        compiler_params=_cparams("parallel", "parallel", "arbitrary"),
    )(q, k, v, do, o, lse)


def _rows_tile(r, c, itemsize=4, budget=2 * 1024 * 1024):
    if r * c * itemsize <= budget:
        return r
    best = None
    for t in range(8, r, 8):
        if r % t == 0 and t * c * itemsize <= budget:
            best = t
    return best if best is not None else r


def _add_n(arrs, *, out_dtype=F32, name):
    shape = arrs[0].shape
    c = shape[-1]
    flat = [a.reshape(-1, c) for a in arrs]
    r = flat[0].shape[0]
    t = _rows_tile(r, c)

    def body(*refs):
        acc = refs[0][...].astype(F32)
        for ref in refs[1:-1]:
            acc = acc + ref[...].astype(F32)
        refs[-1][...] = acc.astype(out_dtype)

    blk = pl.BlockSpec((t, c), lambda i: (i, 0))
    out = pl.pallas_call(
        body, name=name, grid=(r // t,), in_specs=[blk] * len(flat), out_specs=blk,
        out_shape=jax.ShapeDtypeStruct((r, c), out_dtype), compiler_params=_cparams("parallel"),
    )(*flat)
    return out.reshape(shape)


def _adamw(w, g, m, v, *, name):
    shape = w.shape
    c = shape[-1]
    flat = [a.reshape(-1, c) for a in (w, g, m, v)]
    r = flat[0].shape[0]
    t = _rows_tile(r, c, budget=1024 * 1024)

    def body(w_ref, g_ref, m_ref, v_ref, d_ref, mo_ref, vo_ref):
        gv = g_ref[...]
        m2 = ADAM_B1 * m_ref[...] + (1.0 - ADAM_B1) * gv
        v2 = ADAM_B2 * v_ref[...] + (1.0 - ADAM_B2) * (gv * gv)
        m_hat = m2 / (1.0 - ADAM_B1 ** ADAM_STEP)
        v_hat = v2 / (1.0 - ADAM_B2 ** ADAM_STEP)
        d_ref[...] = -ADAM_LR * (m_hat / (jnp.sqrt(v_hat) + ADAM_EPS) + ADAM_WD * w_ref[...])
        mo_ref[...] = m2
        vo_ref[...] = v2

    blk = pl.BlockSpec((t, c), lambda i: (i, 0))
    outs = pl.pallas_call(
        body, name=name, grid=(r // t,), in_specs=[blk] * 4, out_specs=[blk] * 3,
        out_shape=[jax.ShapeDtypeStruct((r, c), F32)] * 3, compiler_params=_cparams("parallel"),
    )(*flat)
    return tuple(o.reshape(shape) for o in outs)


def _place():
    x, y, c = lax.axis_index("x"), lax.axis_index("y"), lax.axis_index("c")
    chips = [(1 - x, y), (x, 1 - y), (1 - x, 1 - y)]
    return x, y, c, chips


ANY = pl.BlockSpec(memory_space=pl.ANY)


def _gather_shards(shards, *, name):
    nt = len(shards)

    def body(*refs):
        src, dst = refs[:nt], refs[nt:2 * nt]
        send, recv, fsend, frecv, lsem = refs[2 * nt:]
        x, y, c, chips = _place()
        me = 2 * x + y
        local = [pltpu.make_async_copy(src[t], dst[t].at[me], lsem.at[t]) for t in range(nt)]
        for cp in local:
            cp.start()

        def half(t, slot, hc):
            hr = src[t].shape[0] // 2
            return dst[t].at[slot, pl.ds(hc * hr, hr)]

        def first(t, k):
            hr = src[t].shape[0] // 2
            return pltpu.make_async_remote_copy(
                src_ref=src[t].at[pl.ds(c * hr, hr)], dst_ref=half(t, me, c),
                send_sem=send.at[t, k], recv_sem=recv.at[t, k],
                device_id=(chips[k][0], chips[k][1], c), device_id_type=MESH)

        def landed(t, k):
            slot = 2 * chips[k][0] + chips[k][1]
            return pltpu.make_async_remote_copy(
                src_ref=half(t, slot, c), dst_ref=half(t, slot, c),
                send_sem=send.at[t, k], recv_sem=recv.at[t, k],
                device_id=(chips[k][0], chips[k][1], c), device_id_type=MESH)

        def forward(t, k, hc):
            slot = 2 * chips[k][0] + chips[k][1]
            return pltpu.make_async_remote_copy(
                src_ref=half(t, slot, hc), dst_ref=half(t, slot, hc),
                send_sem=fsend.at[t, k], recv_sem=frecv.at[t, k],
                device_id=(x, y, 1 - c), device_id_type=MESH)

        for t in range(nt):
            for k in range(3):
                first(t, k).start()
        for t in range(nt):
            for k in range(3):
                landed(t, k).wait_recv()
                forward(t, k, c).start()
        for t in range(nt):
            for k in range(3):
                forward(t, k, 1 - c).wait_recv()
        for t in range(nt):
            for k in range(3):
                first(t, k).wait_send()
                forward(t, k, c).wait_send()
        for cp in local:
            cp.wait()

    return pl.pallas_call(
        body, name=name, in_specs=[ANY] * nt, out_specs=[ANY] * nt,
        out_shape=[jax.ShapeDtypeStruct((N_CHIP,) + a.shape, a.dtype) for a in shards],
        scratch_shapes=[pltpu.SemaphoreType.DMA((nt, 3)), pltpu.SemaphoreType.DMA((nt, 3)),
                        pltpu.SemaphoreType.DMA((nt, 3)), pltpu.SemaphoreType.DMA((nt, 3)),
                        pltpu.SemaphoreType.DMA((nt,))],
    )(*shards)


def _comm_rows(hr, c, budget=2 * 1024 * 1024):
    if hr * c * 4 <= budget:
        return hr
    best = None
    for t in range(16, hr, 16):
        if hr % t == 0 and t * c * 4 <= budget:
            best = t
    return best if best is not None else hr


def _pair_reduce(g, where, *, out_dtype, name):
    n_slot, r, cdim = g.shape
    hr = r // 2
    rc = _comm_rows(hr, cdim)
    nr = hr // rc
    steps = n_slot * nr
    g4 = g.reshape(n_slot, 2, hr, cdim)

    def body(w_ref, a_ref, b_ref, o_ref, land, send, recv, credit):
        x, y, c, _ = _place()
        sib = (x, y, 1 - c)
        i = pl.program_id(0) * nr + pl.program_id(1)
        s = lax.rem(i, 2)

        @pl.when(i >= 2)
        def _():
            pl.semaphore_wait(credit.at[s], 1)

        cp = pltpu.make_async_remote_copy(src_ref=b_ref.at[0, 0], dst_ref=land.at[s], send_sem=send.at[s],
                                          recv_sem=recv.at[s], device_id=sib, device_id_type=MESH)
        cp.start()
        cp.wait_recv()
        o_ref[0] = (a_ref[0, 0] + land[s]).astype(out_dtype)
        cp.wait_send()

        @pl.when(i + 2 < steps)
        def _():
            pl.semaphore_signal(credit.at[s], inc=1, device_id=sib, device_id_type=MESH)

    blk = lambda half: pl.BlockSpec((1, 1, rc, cdim), lambda j, t, w: (j, half(w), t, 0))
    grid_spec = pltpu.PrefetchScalarGridSpec(
        num_scalar_prefetch=1, grid=(n_slot, nr),
        in_specs=[blk(lambda w: w[0]), blk(lambda w: 1 - w[0])],
        out_specs=pl.BlockSpec((1, rc, cdim), lambda j, t, w: (j, t, 0)),
        scratch_shapes=[pltpu.VMEM((2, rc, cdim), F32), pltpu.SemaphoreType.DMA((2,)), pltpu.SemaphoreType.DMA((2,)),
                        pltpu.SemaphoreType.REGULAR((2,))])
    return pl.pallas_call(
        body, name=name, grid_spec=grid_spec, out_shape=jax.ShapeDtypeStruct((n_slot, hr, cdim), out_dtype),
        compiler_params=_cparams("arbitrary", "arbitrary"),
    )(where, g4, g4)


def _chip_exchange(parts, *, name):
    nt = len(parts)

    def body(*refs):
        src, got = refs[:nt], refs[nt:2 * nt]
        send, recv = refs[2 * nt:]
        x, y, c, chips = _place()
        remote = []
        for t in range(nt):
            for k in range(3):
                remote.append(pltpu.make_async_remote_copy(
                    src_ref=src[t].at[2 * chips[k][0] + chips[k][1]], dst_ref=got[t].at[k],
                    send_sem=send.at[t, k], recv_sem=recv.at[t, k],
                    device_id=(chips[k][0], chips[k][1], c), device_id_type=MESH))
        for cp in remote:
            cp.start()
        for cp in remote:
            cp.wait_recv()
        for cp in remote:
            cp.wait_send()

    return pl.pallas_call(
        body, name=name, in_specs=[ANY] * nt, out_specs=[ANY] * nt,
        out_shape=[jax.ShapeDtypeStruct((3,) + a.shape[1:], a.dtype) for a in parts],
        scratch_shapes=[pltpu.SemaphoreType.DMA((nt, 3)), pltpu.SemaphoreType.DMA((nt, 3))],
    )(*parts)


def _sum_join(p, got, where, *, name):
    _, hr, cdim = p.shape
    rc = _comm_rows(hr, cdim)
    n = hr // rc

    def body(w_ref, p_ref, g_ref, out, buf, lsem, ssem, rsem):
        x, y, c, _ = _place()
        sib = (x, y, 1 - c)
        r = pl.program_id(0)

        def copies(step, slot):
            rows = out.at[pl.ds(pl.multiple_of(c * hr + step * rc, 8), rc)]
            return (pltpu.make_async_copy(buf.at[slot], rows, lsem.at[slot]),
                    pltpu.make_async_remote_copy(src_ref=buf.at[slot], dst_ref=rows, send_sem=ssem.at[slot],
                                                 recv_sem=rsem, device_id=sib, device_id_type=MESH))

        s = lax.rem(r, 2)

        @pl.when(r >= 2)
        def _():
            lc, rm = copies(r - 2, s)
            lc.wait()
            rm.wait_send()

        buf[s] = p_ref[0].astype(F32) + g_ref[0].astype(F32) + g_ref[1].astype(F32) + g_ref[2].astype(F32)
        lc, rm = copies(r, s)
        lc.start()
        rm.start()

        @pl.when(r == n - 1)
        def _():
            for step in range(max(0, n - 2), n):
                lc, rm = copies(step, step % 2)
                lc.wait()
                rm.wait_send()
            whole = out.at[pl.ds(0, hr)]
            pltpu.make_async_remote_copy(src_ref=whole, dst_ref=whole, send_sem=ssem.at[0], recv_sem=rsem,
                                         device_id=sib, device_id_type=MESH).wait_recv()

    grid_spec = pltpu.PrefetchScalarGridSpec(
        num_scalar_prefetch=1, grid=(n,),
        in_specs=[pl.BlockSpec((1, rc, cdim), lambda t, w: (w[1], t, 0)),
                  pl.BlockSpec((3, rc, cdim), lambda t, w: (0, t, 0))],
        out_specs=ANY,
        scratch_shapes=[pltpu.VMEM((2, rc, cdim), F32), pltpu.SemaphoreType.DMA((2,)), pltpu.SemaphoreType.DMA((2,)),
                        pltpu.SemaphoreType.DMA])
    return pl.pallas_call(
        body, name=name, grid_spec=grid_spec, out_shape=jax.ShapeDtypeStruct((2 * hr, cdim), F32),
        compiler_params=_cparams("arbitrary"),
    )(where, p, got)


def _gather_all(block, *, name):
    m_per, n = block.shape

    def body(x_ref, out_ref, send_sems, recv_sems, local_sem):
        x, y, c, chips = _place()
        me, sibling = (x, y, c), (x, y, 1 - c)

        def rows(px, py, pc):
            return out_ref.at[4 * px + 2 * py + pc]

        def copy(k, blk, to, src=None):
            return pltpu.make_async_remote_copy(
                src_ref=rows(*blk) if src is None else src, dst_ref=rows(*blk),
                send_sem=send_sems.at[k], recv_sem=recv_sems.at[k], device_id=to, device_id_type=MESH)

        mine = pltpu.make_async_copy(x_ref, rows(*me), local_sem)
        mine.start()
        first = [copy(0, me, sibling, src=x_ref)]
        first += [copy(1 + j, me, (*chip, c), src=x_ref) for j, chip in enumerate(chips)]
        for cp in first:
            cp.start()
        passed = [copy(4 + j, (*chip, c), sibling) for j, chip in enumerate(chips)]
        for j, chip in enumerate(chips):
            copy(1 + j, (*chip, c), me).wait_recv()
            passed[j].start()
        copy(0, sibling, me).wait_recv()
        for j, chip in enumerate(chips):
            copy(4 + j, (*chip, 1 - c), me).wait_recv()
        for cp in first + passed:
            cp.wait_send()
        mine.wait()

    return pl.pallas_call(
        body, name=name,
        out_shape=jax.ShapeDtypeStruct((N_DEV, m_per, n), block.dtype),
        in_specs=[pl.BlockSpec(memory_space=pltpu.VMEM)], out_specs=pl.BlockSpec(memory_space=pltpu.VMEM),
        scratch_shapes=[pltpu.SemaphoreType.DMA((7,)), pltpu.SemaphoreType.DMA((7,)), pltpu.SemaphoreType.DMA],
        compiler_params=pltpu.CompilerParams(vmem_limit_bytes=VMEM_LIMIT),
    )(block)


def _sum_slots(slots, *, name):
    n, m, c = slots.shape
    t = _rows_tile(m, c * n)

    def body(s_ref, o_ref):
        acc = s_ref[0]
        for k in range(1, n):
            acc = acc + s_ref[k]
        o_ref[...] = acc

    return pl.pallas_call(
        body, name=name, grid=(m // t,), in_specs=[pl.BlockSpec((n, t, c), lambda i: (0, i, 0))],
        out_specs=pl.BlockSpec((t, c), lambda i: (i, 0)), out_shape=jax.ShapeDtypeStruct((m, c), F32),
        compiler_params=_cparams("parallel"),
    )(slots)


def _pad_cols(a, width):
    return a if a.shape[1] == width else jnp.pad(a, ((0, 0), (0, width - a.shape[1])))


def _w_in_padded(shards):
    full = jnp.concatenate([shards[j] for j in range(N_CHIP)], axis=1)
    return jnp.concatenate([_pad_cols(full[:, SEG[n][2]:SEG[n][2] + SEG[n][3]], SEG[n][1]) for n in SEG_ORDER], axis=1)


def _w_in_unpadded(gp):
    full = jnp.concatenate([gp[:, SEG[n][0]:SEG[n][0] + SEG[n][3]] for n in ORIG_ORDER], axis=1)
    w = IN_COLS // N_CHIP
    return jnp.stack([full[:, j * w:(j + 1) * w] for j in range(N_CHIP)])


def _pad_heads(w, true_w, pad_w):
    r = w.shape[0]
    h = w.shape[1] // true_w
    return jnp.pad(w.reshape(r, h, true_w), ((0, 0), (0, 0), (0, pad_w - true_w))).reshape(r, h * pad_w)


def _unpad_heads(w, true_w, pad_w):
    r = w.shape[0]
    h = w.shape[1] // pad_w
    return w.reshape(r, h, pad_w)[:, :, :true_w].reshape(r, h * true_w)


def _cols_to_slots(a):
    w = a.shape[1] // N_CHIP
    return jnp.stack([a[:, j * w:(j + 1) * w] for j in range(N_CHIP)])


def _slots_to_cols(a):
    return jnp.concatenate([a[j] for j in range(N_CHIP)], axis=1)


def _to_heads(a, h, d):
    return a.reshape(a.shape[0], h, d).transpose(1, 0, 2)


def _from_heads(a):
    return a.transpose(1, 0, 2).reshape(a.shape[1], -1)


SMALL = [("norm_g", 2048), ("ret_norm_g", 512), ("gla_ba_f", 256), ("gla_ba_b", 256), ("gla_norm_g", 512),
         ("pool_w", 4 * 128 * 128), ("pool_scale", 512), ("mla_q_norm_g", 512), ("mla_kv_norm_g", 256),
         ("mla_qk_norm_q", 192), ("mla_qk_norm_k", 192)]


def _pack_small(vals):
    parts = []
    for name, n in SMALL:
        v = vals[name].reshape(-1)
        parts.append(jnp.pad(v, (0, (-v.shape[0]) % 1024)))
    parts.append(jnp.pad(vals["loss"].reshape(-1), (0, 1023)))
    return jnp.concatenate(parts).reshape(-1, 128)


def _unpack_small(block):
    flat = block.reshape(-1)
    out, off = {}, 0
    for name, n in SMALL:
        out[name] = flat[off:off + DEPTH * n]
        off += DEPTH * n + (-(DEPTH * n)) % 1024
    out["loss"] = flat[off]
    return out


def _layer_weights(l, p, g):
    wa = jnp.zeros((128, 512), F32)
    wa = wa.at[0:GLA_RANK, 0:256].set(_slots_to_cols(g["gla_wa2_f"][:, l]))
    wa = wa.at[GLA_RANK:2 * GLA_RANK, 256:512].set(_slots_to_cols(g["gla_wa2_b"][:, l]))
    return dict(
        norm_g=p["norm_g"][l][None, :],
        w_in=_w_in_padded(g["w_in"][:, l]),
        w_out=g["w_out"][:, l].reshape(4 * g["w_out"].shape[2], -1),
        ret_norm_g=p["ret_norm_g"][l][None, :],
        wa=_bf(wa),
        ba=jnp.concatenate([p["gla_ba_f"][l], p["gla_ba_b"][l]])[None, :],
        gla_norm_g=p["gla_norm_g"][l][None, :],
        pool_w=_bf(p["pool_w"][l]),
        pool_scale=p["pool_scale"][l][None, :],
        qg=p["mla_q_norm_g"][l][None, :],
        wq=_pad_heads(_slots_to_cols(g["mla_wq_b"][:, l]), MLA_QK, MLA_QKP),
        kvg=p["mla_kv_norm_g"][l][None, :],
        wkv=_slots_to_cols(g["mla_wkv_b"][:, l]),
        qng=jnp.pad(p["mla_qk_norm_q"][l], (0, MLA_QKP - MLA_QK))[None, :],
        kng=jnp.pad(p["mla_qk_norm_k"][l], (0, MLA_QKP - MLA_QK))[None, :],
    )


def _layer_fwd(l, x, w, tabs):
    ret_cos, ret_sin, mla_cos, mla_sp, mla_sn = tabs
    nm = lambda s: f"l{l}_{s}"
    h = _rmsnorm_fwd(x, w["norm_g"], name=nm("norm"))
    z = _matmul(h, w["w_in"], name=nm("in_proj"))
    qr, kr = _ret_pre(z, ret_cos, ret_sin, name=nm("ret_pre"))
    ret_o = _bla(qr, kr, z, _ret_log_gamma(False), (0, 0, SEG["rv"][0] // 512), name=nm("ret_scan"))
    y_a = _post(ret_o, z, SEG["rg"][0] // 512, w["ret_norm_g"], norm=True, name=nm("ret_post"))
    la = _gla_gate(z, w["wa"], w["ba"], name=nm("gla_gate"))
    la_h = jnp.stack([_to_heads(la[:, :256], GLA_HEADS, GLA_DK), _to_heads(la[:, 256:], GLA_HEADS, GLA_DK)])
    gq = _to_heads(z[:, SEG["gq"][0]:SEG["gq"][0] + 256], GLA_HEADS, GLA_DK)
    gk = _to_heads(z[:, SEG["gk"][0]:SEG["gk"][0] + 256], GLA_HEADS, GLA_DK)
    gla_o, gla_st = _gla_fwd(gq, gk, z, la_h, name=nm("gla_scan"))
    y_b = _post(gla_o, z, SEG["gg"][0] // 512, w["gla_norm_g"], norm=True, name=nm("gla_post"))
    y_c = _pool_fwd(z, w["pool_w"], w["pool_scale"], name=nm("pool"))
    q, k, v = _mla_pre(z, w["qg"], w["wq"], w["kvg"], w["wkv"], w["qng"], w["kng"], mla_cos, mla_sp, mla_sn,
                       name=nm("mla_pre"))
    att_o, lse = _flash_fwd(q, k, v, name=nm("attn"))
    y_d = _post(att_o, z, SEG["mg"][0] // 512, w["qg"], norm=False, name=nm("mla_post"))
    y = jnp.concatenate([y_a, y_b, y_c, y_d], axis=1)
    x_next = _matmul(y, w["w_out"], add=x, name=nm("out_proj"))
    saved = dict(x=x, h=h, z=z, y=y, qr=qr, kr=kr, ret_o=ret_o, la_h=la_h, gq=gq, gk=gk, gla_o=gla_o, gla_st=gla_st,
                 q=q, k=k, v=v, att_o=att_o, lse=lse)
    return x_next, saved


def _layer_bwd(l, dx_next, w, sv, tabs):
    ret_cos, ret_sin, mla_cos, mla_sp, mla_sn = tabs
    nm = lambda s: f"l{l}_{s}"
    z = sv["z"]
    dy = _matmul(dx_next, w["w_out"], tb=True, tk=1024, name=nm("out_proj_dy"))
    d_w_out = _matmul(sv["y"], dx_next, ta=True, tk=1024, name=nm("out_proj_dw"))
    d_rg, d_ret_o, d_ret_g = _post_bwd(dy, 0, sv["ret_o"], z, SEG["rg"][0] // 512, w["ret_norm_g"], norm=True,
                                       name=nm("ret_post_bwd"))
    vcol = SEG["rv"][0] // 512
    dqr = _bla(d_ret_o, z, sv["kr"], _ret_log_gamma(False), (0, vcol, 0), name=nm("ret_scan_dq"))
    dkr = _bla(z, d_ret_o, sv["qr"], _ret_log_gamma(True), (vcol, 0, 0), name=nm("ret_scan_dk"))
    drv = _bla(sv["kr"], sv["qr"], d_ret_o, _ret_log_gamma(True), (0, 0, 0), name=nm("ret_scan_dv"))
    d_rq, d_rk = _ret_pre_bwd(dqr, dkr, ret_cos, ret_sin, name=nm("ret_pre_bwd"))
    d_rv = _add_n([drv[0], drv[1]], out_dtype=BF16, name=nm("ret_dv_sum"))
    d_gg, d_gla_o, d_gla_g = _post_bwd(dy, 1, sv["gla_o"], z, SEG["gg"][0] // 512, w["gla_norm_g"], norm=True,
                                       name=nm("gla_post_bwd"))
    dq2, dk2, dla2, dv2 = _gla_bwd(sv["gq"], sv["gk"], z, sv["la_h"], d_gla_o, sv["gla_st"], name=nm("gla_scan_bwd"))
    d_gq = _bf(_from_heads(dq2[0] + dq2[1]))
    d_gk = _bf(_from_heads(dk2[0] + dk2[1]))
    d_gv = _add_n([dv2[0], dv2[1]], out_dtype=BF16, name=nm("gla_dv_sum"))
    dla = jnp.concatenate([_from_heads(dla2[0]), _from_heads(dla2[1])], axis=1)
    d_ga, d_wa, d_ba = _gla_gate_bwd(dla, z, w["wa"], w["ba"], name=nm("gla_gate_bwd"))
    d_pv, d_pg, d_pool_w, d_pool_scale = _pool_bwd(dy, z, w["pool_w"], w["pool_scale"], name=nm("pool_bwd"))
    d_mg, d_att_o, _ = _post_bwd(dy, 3, sv["att_o"], z, SEG["mg"][0] // 512, w["qg"], norm=False,
                                 name=nm("mla_post_bwd"))
    dq = _flash_bwd_dq(sv["q"], sv["k"], sv["v"], d_att_o, sv["att_o"], sv["lse"], name=nm("attn_dq"))
    dk, dv = _flash_bwd_dkv(sv["q"], sv["k"], sv["v"], d_att_o, sv["att_o"], sv["lse"], name=nm("attn_dkv"))
    d_mq, d_mkv, d_mkr, d_wq, d_wkv, d_qg, d_kvg, d_qng, d_kng = _mla_pre_bwd(
        dq, dk, dv, z, w["qg"], w["wq"], w["kvg"], w["wkv"], w["qng"], w["kng"], mla_cos, mla_sp, mla_sn,
        name=nm("mla_pre_bwd"))
    segs = dict(rq=d_rq, rk=d_rk, rv=d_rv, rg=d_rg, gv=d_gv, gg=d_gg, pv=d_pv, pg=d_pg, mq=d_mq, mg=d_mg,
                gq=d_gq, gk=d_gk, mkv=d_mkv, ga=d_ga, mkr=d_mkr)
    dz = jnp.concatenate([segs[n] for n in SEG_ORDER], axis=1)
    dh = _matmul(dz, w["w_in"], tb=True, tk=2048, name=nm("in_proj_dh"))
    d_w_in = _matmul(sv["h"], dz, ta=True, tk=1024, name=nm("in_proj_dw"))
    dx, d_norm_g = _rmsnorm_bwd(sv["x"], dh, w["norm_g"], dx_next, name=nm("norm_bwd"))
    sharded = dict(
        w_in=_w_in_unpadded(d_w_in),
        w_out=d_w_out.reshape(N_CHIP, d_w_out.shape[0] // N_CHIP, d_w_out.shape[1]),
        mla_wq_b=_cols_to_slots(_unpad_heads(d_wq, MLA_QK, MLA_QKP)),
        mla_wkv_b=_cols_to_slots(d_wkv),
        gla_wa2_f=_cols_to_slots(d_wa[0:GLA_RANK, 0:256]),
        gla_wa2_b=_cols_to_slots(d_wa[GLA_RANK:2 * GLA_RANK, 256:512]),
    )
    small = dict(
        norm_g=d_norm_g[0], ret_norm_g=d_ret_g[0], gla_ba_f=d_ba[0, :256], gla_ba_b=d_ba[0, 256:],
        gla_norm_g=d_gla_g[0], pool_w=d_pool_w.reshape(-1), pool_scale=d_pool_scale[0], mla_q_norm_g=d_qg[0],
        mla_kv_norm_g=d_kvg[0], mla_qk_norm_q=d_qng[0, :MLA_QK], mla_qk_norm_k=d_kng[0, :MLA_QK],
    )
    return dx, sharded, small


SHARDED = ["w_in", "w_out", "mla_wq_b", "mla_wkv_b", "gla_wa2_f", "gla_wa2_b"]
WEIGHTS = ["norm_g", "w_in", "ret_norm_g", "gla_wa2_f", "gla_ba_f", "gla_wa2_b", "gla_ba_b", "gla_norm_g", "pool_w",
           "pool_scale", "mla_q_norm_g", "mla_wq_b", "mla_kv_norm_g", "mla_wkv_b", "mla_qk_norm_q", "mla_qk_norm_k",
           "w_out"]


def _local_step(p, gathered):
    x = p["x"][0]
    tabs = _rope_tables(x.shape[0])
    ws, saved = [], []
    for l in range(DEPTH):
        w = _layer_weights(l, p, gathered)
        x, sv = _layer_fwd(l, x, w, tabs)
        ws.append(w)
        saved.append(sv)
    dx, loss = _loss_head(x, p["loss_target"][0], name="loss_head")
    sharded, small = [None] * DEPTH, [None] * DEPTH
    for l in reversed(range(DEPTH)):
        dx, sharded[l], small[l] = _layer_bwd(l, dx, ws[l], saved[l], tabs)
    sharded = {n: jnp.stack([sharded[l][n] for l in range(DEPTH)], axis=1) for n in SHARDED}
    small = {n: jnp.stack([small[l][n] for l in range(DEPTH)]) for n, _ in SMALL}
    small["loss"] = loss
    return dx[None], sharded, small


def kernel(x, norm_g, w_in, ret_norm_g, gla_wa2_f, gla_ba_f, gla_wa2_b, gla_ba_b, gla_norm_g, pool_w, pool_scale, mla_q_norm_g, mla_wq_b, mla_kv_norm_g, mla_wkv_b, mla_qk_norm_q, mla_qk_norm_k, w_out, loss_target, m_norm_g, m_w_in, m_ret_norm_g, m_gla_wa2_f, m_gla_ba_f, m_gla_wa2_b, m_gla_ba_b, m_gla_norm_g, m_pool_w, m_pool_scale, m_mla_q_norm_g, m_mla_wq_b, m_mla_kv_norm_g, m_mla_wkv_b, m_mla_qk_norm_q, m_mla_qk_norm_k, m_w_out, v_norm_g, v_w_in, v_ret_norm_g, v_gla_wa2_f, v_gla_ba_f, v_gla_wa2_b, v_gla_ba_b, v_gla_norm_g, v_pool_w, v_pool_scale, v_mla_q_norm_g, v_mla_wq_b, v_mla_kv_norm_g, v_mla_wkv_b, v_mla_qk_norm_q, v_mla_qk_norm_k, v_w_out):
    p = dict(x=x, norm_g=norm_g, w_in=w_in, ret_norm_g=ret_norm_g, gla_wa2_f=gla_wa2_f, gla_ba_f=gla_ba_f,
             gla_wa2_b=gla_wa2_b, gla_ba_b=gla_ba_b, gla_norm_g=gla_norm_g, pool_w=pool_w, pool_scale=pool_scale,
             mla_q_norm_g=mla_q_norm_g, mla_wq_b=mla_wq_b, mla_kv_norm_g=mla_kv_norm_g, mla_wkv_b=mla_wkv_b,
             mla_qk_norm_q=mla_qk_norm_q, mla_qk_norm_k=mla_qk_norm_k, w_out=w_out, loss_target=loss_target)
    moments = dict(
        m=dict(norm_g=m_norm_g, w_in=m_w_in, ret_norm_g=m_ret_norm_g, gla_wa2_f=m_gla_wa2_f, gla_ba_f=m_gla_ba_f,
               gla_wa2_b=m_gla_wa2_b, gla_ba_b=m_gla_ba_b, gla_norm_g=m_gla_norm_g, pool_w=m_pool_w,
               pool_scale=m_pool_scale, mla_q_norm_g=m_mla_q_norm_g, mla_wq_b=m_mla_wq_b,
               mla_kv_norm_g=m_mla_kv_norm_g, mla_wkv_b=m_mla_wkv_b, mla_qk_norm_q=m_mla_qk_norm_q,
               mla_qk_norm_k=m_mla_qk_norm_k, w_out=m_w_out),
        v=dict(norm_g=v_norm_g, w_in=v_w_in, ret_norm_g=v_ret_norm_g, gla_wa2_f=v_gla_wa2_f, gla_ba_f=v_gla_ba_f,
               gla_wa2_b=v_gla_wa2_b, gla_ba_b=v_gla_ba_b, gla_norm_g=v_gla_norm_g, pool_w=v_pool_w,
               pool_scale=v_pool_scale, mla_q_norm_g=v_mla_q_norm_g, mla_wq_b=v_mla_wq_b,
               mla_kv_norm_g=v_mla_kv_norm_g, mla_wkv_b=v_mla_wkv_b, mla_qk_norm_q=v_mla_qk_norm_q,
               mla_qk_norm_k=v_mla_qk_norm_k, w_out=v_w_out))

    def as_rows(name, dtype):
        a = p[name].astype(dtype)
        return a.reshape(a.shape[0] * a.shape[1], a.shape[2])

    shards = [as_rows("w_in", BF16), as_rows("w_out", BF16), as_rows("mla_wq_b", BF16), as_rows("mla_wkv_b", BF16),
              as_rows("gla_wa2_f", F32), as_rows("gla_wa2_b", F32)]
    got = _gather_shards(shards, name="gather_weights")
    gathered = {n: a.reshape((N_CHIP, DEPTH, a.shape[1] // DEPTH, a.shape[2])) for n, a in zip(SHARDED, got)}

    grad_x, sharded, small = _local_step(p, gathered)

    where = jnp.stack([lax.axis_index("c"), 2 * lax.axis_index("x") + lax.axis_index("y")]).astype(jnp.int32)
    flat = [sharded[n].reshape(N_CHIP, -1, sharded[n].shape[-1]) for n in SHARDED]
    pair = [_pair_reduce(a, where, out_dtype=BF16, name=f"grad_pair_reduce_{n}") for n, a in zip(SHARDED, flat)]
    others = _chip_exchange(pair, name="grad_chip_exchange")
    joined = [_sum_join(a, b, where, name=f"grad_sum_join_{n}") for n, a, b in zip(SHARDED, pair, others)]
    grads = {n: a.reshape(p[n].shape) for n, a in zip(SHARDED, joined)}

    slots = _gather_all(_pack_small(small), name="gather_small")
    total = _unpack_small(_sum_slots(slots, name="sum_small"))
    for n, _ in SMALL:
        grads[n] = total[n].reshape(p[n].shape)
    loss = total["loss"]

    delta, new_m, new_v = {}, {}, {}
    for n in WEIGHTS:
        delta[n], new_m[n], new_v[n] = _adamw(p[n], grads[n], moments["m"][n], moments["v"][n], name=f"adamw_{n}")
    return (loss, grad_x, *[grads[n] for n in WEIGHTS], *[delta[n] for n in WEIGHTS],
            *[new_m[n] for n in WEIGHTS], *[new_v[n] for n in WEIGHTS])
```

```python
import functools
import math

import jax
import jax.numpy as jnp
from jax import lax
from jax.experimental import pallas as pl
from jax.experimental.pallas import tpu as pltpu

F32 = jnp.float32
BF16 = jnp.bfloat16
MESH = pl.DeviceIdType.MESH

EPS = 1e-6
ROPE_THETA = 10000.0
DEPTH = 2
N_DEV = 8
N_CHIP = 4

GROUP_W = 512
RET_HEADS = 4
RET_HD = 128
RET_CHUNK = 128
GLA_HEADS = 4
GLA_DK = 64
GLA_DV = 128
GLA_RANK = 16
GLA_TAU = 16.0
GLA_CHUNK = 64
POOL_GROUPS = 4
POOL_GW = 128
POOL_HALO = 8
POOL_TILE = 256
MLA_HEADS = 4
MLA_NOPE = 128
MLA_ROPE = 64
MLA_QK = MLA_NOPE + MLA_ROPE
MLA_QKP = 256
MLA_V = 128
MLA_Q_RANK = 512
MLA_KV_RANK = 256
MLA_SCALE = MLA_QK ** -0.5

ADAM_LR = 0.001
ADAM_B1 = 0.9
ADAM_B2 = 0.999
ADAM_EPS = 1e-08
ADAM_WD = 0.01
ADAM_STEP = 10

VMEM_LIMIT = 56 * 1024 * 1024

SEG = {
    "rq": (0, 512, 0, 512), "rk": (512, 512, 512, 512), "rv": (1024, 512, 1024, 512), "rg": (1536, 512, 1536, 512),
    "gv": (2048, 512, 2560, 512), "gg": (2560, 512, 3072, 512),
    "pv": (3072, 512, 3616, 512), "pg": (3584, 512, 4128, 512),
    "mq": (4096, 512, 4640, 512), "mg": (4608, 512, 5472, 512),
    "gq": (5120, 256, 2048, 256), "gk": (5376, 256, 2304, 256), "mkv": (5632, 256, 5152, 256),
    "ga": (5888, 128, 3584, 32), "mkr": (6016, 128, 5408, 64),
}
SEG_ORDER = ["rq", "rk", "rv", "rg", "gv", "gg", "pv", "pg", "mq", "mg", "gq", "gk", "mkv", "ga", "mkr"]
IN_COLS = 5984
IN_PAD = 6144
ORIG_ORDER = ["rq", "rk", "rv", "rg", "gq", "gk", "gv", "gg", "ga", "pv", "pg", "mq", "mkv", "mkr", "mg"]


def _cparams(*sem):
    return pltpu.CompilerParams(dimension_semantics=tuple(sem), vmem_limit_bytes=VMEM_LIMIT)


def _bf(v):
    return v.astype(BF16)


def _dot(a, b, ca=1, cb=0):
    return lax.dot_general(_bf(a), _bf(b), (((ca,), (cb,)), ((), ())), preferred_element_type=F32)


def _split_dot(a01, x, ca=1, cb=0):
    hi = _bf(x)
    r1 = x - hi.astype(F32)
    mid = _bf(r1)
    lo = _bf(r1 - mid.astype(F32))
    dn = (((ca,), (cb,)), ((), ()))
    a = _bf(a01)
    return (lax.dot_general(a, hi, dn, preferred_element_type=F32)
            + lax.dot_general(a, mid, dn, preferred_element_type=F32)
            + lax.dot_general(a, lo, dn, preferred_element_type=F32))


def _sigmoid(x):
    return 1.0 / (1.0 + jnp.exp(-x))


def _silu_parts(g):
    sg = _sigmoid(g)
    return g * sg, sg * (1.0 + g * (1.0 - sg))


def _matmul(a, b, *, ta=False, tb=False, out_dtype=F32, tm=512, tn=1024, tk=None, add=None, n_outer=True, name):
    m, kdim = (a.shape[1], a.shape[0]) if ta else a.shape
    n = b.shape[0] if tb else b.shape[1]
    tm, tn = min(tm, m), min(tn, n)
    tk = kdim if tk is None else min(tk, kdim)
    assert m % tm == 0 and n % tn == 0 and kdim % tk == 0
    nk = kdim // tk
    ca, cb = (0 if ta else 1), (1 if tb else 0)

    def body(*refs):
        if add is None:
            a_ref, b_ref, o_ref = refs[:3]
            add_ref = None
        else:
            a_ref, b_ref, add_ref, o_ref = refs[:4]
        p = _dot(a_ref[...], b_ref[...], ca, cb)

        def finish(r):
            if add_ref is not None:
                r = r + add_ref[...]
            o_ref[...] = r.astype(out_dtype)

        if nk == 1:
            finish(p)
        else:
            acc = refs[-1]
            k = pl.program_id(2)

            @pl.when(k == 0)
            def _():
                acc[...] = p

            @pl.when(k > 0)
            def _():
                acc[...] += p

            @pl.when(k == nk - 1)
            def _():
                finish(acc[...])

    def ij(g0, g1):
        return (g1, g0) if n_outer else (g0, g1)

    a_spec = (pl.BlockSpec((tk, tm), lambda g0, g1, k: (k, ij(g0, g1)[0])) if ta
              else pl.BlockSpec((tm, tk), lambda g0, g1, k: (ij(g0, g1)[0], k)))
    b_spec = (pl.BlockSpec((tn, tk), lambda g0, g1, k: (ij(g0, g1)[1], k)) if tb
              else pl.BlockSpec((tk, tn), lambda g0, g1, k: (k, ij(g0, g1)[1])))
    o_spec = pl.BlockSpec((tm, tn), lambda g0, g1, k: ij(g0, g1))
    in_specs = [a_spec, b_spec] + ([o_spec] if add is not None else [])
    args = (a, b) + ((add,) if add is not None else ())
    grid = (n // tn, m // tm, nk) if n_outer else (m // tm, n // tn, nk)
    return pl.pallas_call(
        body, name=name, grid=grid, in_specs=in_specs, out_specs=o_spec,
        out_shape=jax.ShapeDtypeStruct((m, n), out_dtype),
        scratch_shapes=[] if nk == 1 else [pltpu.VMEM((tm, tn), F32)],
        compiler_params=_cparams("parallel", "parallel", "arbitrary"),
    )(*args)


def _rmsnorm_fwd(x, g, *, name, tm=256):
    s, d = x.shape
    tm = min(tm, s)

    def body(x_ref, g_ref, h_ref):
        xv = x_ref[...]
        r = lax.rsqrt(jnp.mean(xv * xv, axis=-1, keepdims=True) + EPS)
        h_ref[...] = _bf(xv * r * g_ref[...])

    return pl.pallas_call(
        body, name=name, grid=(s // tm,),
        in_specs=[pl.BlockSpec((tm, d), lambda i: (i, 0)), pl.BlockSpec((1, d), lambda i: (0, 0))],
        out_specs=pl.BlockSpec((tm, d), lambda i: (i, 0)),
        out_shape=jax.ShapeDtypeStruct((s, d), BF16),
        compiler_params=_cparams("parallel"),
    )(x, g)


def _rmsnorm_bwd(x, dh, g, dres, *, name, tm=256):
    s, d = x.shape
    tm = min(tm, s)

    def body(x_ref, dh_ref, g_ref, dres_ref, dx_ref, dg_ref):
        i = pl.program_id(0)
        xv = x_ref[...]
        r = lax.rsqrt(jnp.mean(xv * xv, axis=-1, keepdims=True) + EPS)
        xn = xv * r
        dv = dh_ref[...]
        part = jnp.sum(dv * xn, axis=0, keepdims=True)

        @pl.when(i == 0)
        def _():
            dg_ref[...] = part

        @pl.when(i > 0)
        def _():
            dg_ref[...] += part

        dxn = dv * g_ref[...]
        dx_ref[...] = dres_ref[...] + r * (dxn - xn * jnp.mean(dxn * xn, axis=-1, keepdims=True))

    row = pl.BlockSpec((tm, d), lambda i: (i, 0))
    vec = pl.BlockSpec((1, d), lambda i: (0, 0))
    return pl.pallas_call(
        body, name=name, grid=(s // tm,), in_specs=[row, row, vec, row], out_specs=[row, vec],
        out_shape=[jax.ShapeDtypeStruct((s, d), F32), jax.ShapeDtypeStruct((1, d), F32)],
        compiler_params=_cparams("arbitrary"),
    )(x, dh, g, dres)


def _loss_head(xf, target, *, name, tm=256):
    s, d = xf.shape
    tm = min(tm, s)

    def body(x_ref, t_ref, dx_ref, l_ref):
        i = pl.program_id(0)
        e = x_ref[...] - t_ref[...]
        dx_ref[...] = e * (1.0 / d)
        rows = jnp.mean(e * e, axis=-1, keepdims=True)
        part = 0.5 * jnp.sum(rows, axis=0, keepdims=True)

        @pl.when(i == 0)
        def _():
            l_ref[...] = part

        @pl.when(i > 0)
        def _():
            l_ref[...] += part

    row = pl.BlockSpec((tm, d), lambda i: (i, 0))
    return pl.pallas_call(
        body, name=name, grid=(s // tm,), in_specs=[row, row],
        out_specs=[row, pl.BlockSpec((1, 1), lambda i: (0, 0))],
        out_shape=[jax.ShapeDtypeStruct((s, d), F32), jax.ShapeDtypeStruct((1, 1), F32)],
        compiler_params=_cparams("arbitrary"),
    )(xf, target)


def _rope_tables(s):
    pos = jnp.arange(s, dtype=F32)[:, None]
    inv_r = 1.0 / (ROPE_THETA ** (jnp.arange(0, RET_HD, 2, dtype=F32) / RET_HD))
    ang = pos * inv_r[None, :]
    ret_cos = jnp.concatenate([jnp.cos(ang), jnp.cos(ang)], axis=1)
    ret_sin = jnp.concatenate([-jnp.sin(ang), jnp.sin(ang)], axis=1)
    inv_m = 1.0 / (ROPE_THETA ** (jnp.arange(0, MLA_ROPE, 2, dtype=F32) / MLA_ROPE))
    am = pos * inv_m[None, :]
    z32, z64 = jnp.zeros((s, 32), F32), jnp.zeros((s, 64), F32)
    mla_cos = jnp.concatenate([jnp.cos(am), jnp.cos(am), z64], axis=1)
    mla_sp = jnp.concatenate([z32, jnp.sin(am), z64], axis=1)
    mla_sn = jnp.concatenate([-jnp.sin(am), z32, z64], axis=1)
    return ret_cos, ret_sin, mla_cos, mla_sp, mla_sn


def _rope128(x, c, sg):
    return x * c + pltpu.roll(x, 64, 1) * sg


def _unrope128(d, c, sg):
    return d * c + pltpu.roll(d * sg, 64, 1)


def _rope64(t, c, sp, sn):
    return t * c + pltpu.roll(t, 96, 1) * sn + pltpu.roll(t, 32, 1) * sp


def _unrope64(d, c, sp, sn):
    return d * c + pltpu.roll(d * sn, 32, 1) + pltpu.roll(d * sp, 96, 1)


def _ret_pre(z, cos, sin, *, name, tm=256):
    s = z.shape[0]
    tm = min(tm, s)
    scale = RET_HD ** -0.5

    def body(q_ref, k_ref, c_ref, s_ref, qo_ref, ko_ref):
        c, sg = c_ref[...], s_ref[...]
        for h in range(RET_HEADS):
            sl = slice(h * RET_HD, (h + 1) * RET_HD)
            qo_ref[:, sl] = _rope128(q_ref[:, sl], c, sg)
            ko_ref[:, sl] = _rope128(k_ref[:, sl], c, sg) * scale

    seg = lambda j: pl.BlockSpec((tm, GROUP_W), lambda i: (i, j))
    tab = pl.BlockSpec((tm, RET_HD), lambda i: (i, 0))
    return pl.pallas_call(
        body, name=name, grid=(s // tm,), in_specs=[seg(0), seg(1), tab, tab],
        out_specs=[seg(0), seg(0)],
        out_shape=[jax.ShapeDtypeStruct((s, GROUP_W), F32)] * 2,
        compiler_params=_cparams("parallel"),
    )(z, z, cos, sin)


def _ret_pre_bwd(dqr, dkr, cos, sin, *, name, tm=256):
    s = dqr[0].shape[0]
    tm = min(tm, s)
    scale = RET_HD ** -0.5

    def body(dq0_ref, dq1_ref, dk0_ref, dk1_ref, c_ref, s_ref, qo_ref, ko_ref):
        c, sg = c_ref[...], s_ref[...]
        for h in range(RET_HEADS):
            sl = slice(h * RET_HD, (h + 1) * RET_HD)
            qo_ref[:, sl] = _bf(_unrope128(dq0_ref[:, sl] + dq1_ref[:, sl], c, sg))
            ko_ref[:, sl] = _bf(_unrope128(dk0_ref[:, sl] + dk1_ref[:, sl], c, sg) * scale)

    row = pl.BlockSpec((tm, GROUP_W), lambda i: (i, 0))
    tab = pl.BlockSpec((tm, RET_HD), lambda i: (i, 0))
    return pl.pallas_call(
        body, name=name, grid=(s // tm,), in_specs=[row, row, row, row, tab, tab], out_specs=[row, row],
        out_shape=[jax.ShapeDtypeStruct((s, GROUP_W), BF16)] * 2,
        compiler_params=_cparams("parallel"),
    )(dqr[0], dqr[1], dkr[0], dkr[1], cos, sin)


def _bla(a, b, c, lg, cols, *, name):
    s = a.shape[0]
    ch = min(RET_CHUNK, s)
    n = s // ch
    hd = RET_HD

    def body(lg_ref, a0, b0, c0, a1, b1, c1, o0, o1, st):
        t = pl.program_id(0)

        @pl.when(t == 0)
        def _():
            st[...] = jnp.zeros_like(st)

        ii = lax.broadcasted_iota(jnp.int32, (ch, ch), 0)
        jj = lax.broadcasted_iota(jnp.int32, (ch, ch), 1)
        idx = lax.broadcasted_iota(jnp.int32, (ch, 1), 0).astype(F32)
        for d, (a_ref, b_ref, c_ref, o_ref) in enumerate(((a0, b0, c0, o0), (a1, b1, c1, o1))):
            diff = ((ii - jj) if d == 0 else (jj - ii)).astype(F32)
            keep = diff >= 0
            dpos = jnp.maximum(diff, 0.0)
            pq = (idx + 1.0) if d == 0 else (ch - idx)
            pk = (ch - 1.0 - idx) if d == 0 else idx
            for h in range(RET_HEADS):
                g = lg_ref[d, h]
                sl = slice(h * hd, (h + 1) * hd)
                av, bv, cv = a_ref[:, sl], b_ref[:, sl], c_ref[:, sl]
                sc = _dot(av, bv, 1, 1) * jnp.where(keep, jnp.exp(dpos * g), 0.0)
                stv = st[d, h]
                o_ref[:, sl] = _dot(sc, cv) + _dot(av * jnp.exp(pq * g), stv)
                st[d, h] = jnp.exp(ch * g) * stv + _dot(bv * jnp.exp(pk * g), cv, 0, 0)

    fwd = lambda j: pl.BlockSpec((ch, GROUP_W), lambda t: (t, j))
    bwd = lambda j: pl.BlockSpec((ch, GROUP_W), lambda t: (n - 1 - t, j))
    return pl.pallas_call(
        body, name=name, grid=(n,),
        in_specs=[pl.BlockSpec(memory_space=pltpu.SMEM), fwd(cols[0]), fwd(cols[1]), fwd(cols[2]),
                  bwd(cols[0]), bwd(cols[1]), bwd(cols[2])],
        out_specs=[fwd(0), bwd(0)],
        out_shape=[jax.ShapeDtypeStruct((s, GROUP_W), F32)] * 2,
        scratch_shapes=[pltpu.VMEM((2, RET_HEADS, hd, hd), F32)],
        compiler_params=_cparams("arbitrary"),
    )(lg, a, b, c, a, b, c)


def _post(os_, zg, gcol, g, *, norm, name, tm=256):
    s = zg.shape[0]
    tm = min(tm, s)
    nd = len(os_)

    def body(*refs):
        o_refs, (gt_ref, g_ref, y_ref) = refs[:nd], refs[nd:]
        silu, _ = _silu_parts(gt_ref[...])
        for h in range(4):
            sl = slice(h * 128, (h + 1) * 128)
            o = o_refs[0][:, sl]
            for k in range(1, nd):
                o = o + o_refs[k][:, sl]
            if norm:
                r = lax.rsqrt(jnp.mean(o * o, axis=-1, keepdims=True) + EPS)
                o = o * r * g_ref[:, sl]
            y_ref[:, sl] = _bf(silu[:, sl] * o)

    row = pl.BlockSpec((tm, GROUP_W), lambda i: (i, 0))
    return pl.pallas_call(
        body, name=name, grid=(s // tm,),
        in_specs=[row] * nd + [pl.BlockSpec((tm, GROUP_W), lambda i: (i, gcol)),
                               pl.BlockSpec((1, GROUP_W), lambda i: (0, 0))],
        out_specs=row,
        out_shape=jax.ShapeDtypeStruct((s, GROUP_W), BF16),
        compiler_params=_cparams("parallel"),
    )(*os_, zg, g)


def _post_bwd(dy, ycol, os_, zg, gcol, g, *, norm, name, tm=256):
    s = zg.shape[0]
    tm = min(tm, s)
    nd = len(os_)

    def body(*refs):
        dy_ref, o_refs = refs[0], refs[1:1 + nd]
        gt_ref, g_ref, dgt_ref, do_ref, dg_ref = refs[1 + nd:]
        i = pl.program_id(0)
        silu, dsilu = _silu_parts(gt_ref[...])
        dyv = dy_ref[...]
        parts = []
        for h in range(4):
            sl = slice(h * 128, (h + 1) * 128)
            o = o_refs[0][:, sl]
            for k in range(1, nd):
                o = o + o_refs[k][:, sl]
            dn = dyv[:, sl] * silu[:, sl]
            if norm:
                r = lax.rsqrt(jnp.mean(o * o, axis=-1, keepdims=True) + EPS)
                xn = o * r
                gh = g_ref[:, sl]
                dgt_ref[:, sl] = _bf(dyv[:, sl] * (xn * gh) * dsilu[:, sl])
                parts.append(jnp.sum(dn * xn, axis=0, keepdims=True))
                dxn = dn * gh
                do_ref[:, sl] = r * (dxn - xn * jnp.mean(dxn * xn, axis=-1, keepdims=True))
            else:
                dgt_ref[:, sl] = _bf(dyv[:, sl] * o * dsilu[:, sl])
                parts.append(jnp.zeros((1, 128), F32))
                do_ref[:, sl] = dn
        part = jnp.concatenate(parts, axis=1)

        @pl.when(i == 0)
        def _():
            dg_ref[...] = part

        @pl.when(i > 0)
        def _():
            dg_ref[...] += part

    row = pl.BlockSpec((tm, GROUP_W), lambda i: (i, 0))
    vec = pl.BlockSpec((1, GROUP_W), lambda i: (0, 0))
    return pl.pallas_call(
        body, name=name, grid=(s // tm,),
        in_specs=[pl.BlockSpec((tm, GROUP_W), lambda i: (i, ycol))] + [row] * nd
        + [pl.BlockSpec((tm, GROUP_W), lambda i: (i, gcol)), vec],
        out_specs=[row, row, vec],
        out_shape=[jax.ShapeDtypeStruct((s, GROUP_W), BF16), jax.ShapeDtypeStruct((s, GROUP_W), F32),
                   jax.ShapeDtypeStruct((1, GROUP_W), F32)],
        compiler_params=_cparams("arbitrary"),
    )(dy, *os_, zg, g)


def _ret_log_gamma(swap):
    gf = 1.0 - 2.0 ** (-5.0 - jnp.arange(RET_HEADS, dtype=F32))
    lf, lb = jnp.log(gf), jnp.log(gf[::-1])
    return jnp.stack([lb, lf] if swap else [lf, lb])


def _log_sigmoid(x):
    return jnp.minimum(x, 0.0) - jnp.log(1.0 + jnp.exp(-jnp.abs(x)))


def _gla_gate(z, wa, ba, *, name, tm=256):
    s = z.shape[0]
    tm = min(tm, s)
    col = SEG["ga"][0] // 128

    def body(ga_ref, wa_ref, ba_ref, la_ref):
        pre = _dot(ga_ref[...], wa_ref[...]) + ba_ref[...]
        la_ref[...] = _log_sigmoid(pre) / GLA_TAU

    return pl.pallas_call(
        body, name=name, grid=(s // tm,),
        in_specs=[pl.BlockSpec((tm, 128), lambda i: (i, col)), pl.BlockSpec((128, 512), lambda i: (0, 0)),
                  pl.BlockSpec((1, 512), lambda i: (0, 0))],
        out_specs=pl.BlockSpec((tm, 512), lambda i: (i, 0)),
        out_shape=jax.ShapeDtypeStruct((s, 512), F32),
        compiler_params=_cparams("parallel"),
    )(z, wa, ba)


def _gla_gate_bwd(dla, z, wa, ba, *, name, tm=256):
    s = z.shape[0]
    tm = min(tm, s)
    col = SEG["ga"][0] // 128

    def body(dla_ref, ga_ref, wa_ref, ba_ref, dga_ref, dwa_ref, dba_ref):
        i = pl.program_id(0)
        gav = ga_ref[...]
        pre = _dot(gav, wa_ref[...]) + ba_ref[...]
        dpre = dla_ref[...] * (1.0 - _sigmoid(pre)) * (1.0 / GLA_TAU)
        dga_ref[...] = _bf(_dot(dpre, wa_ref[...], 1, 1))
        pw = _dot(gav, dpre, 0, 0)
        pb = jnp.sum(dpre, axis=0, keepdims=True)

        @pl.when(i == 0)
        def _():
            dwa_ref[...] = pw
            dba_ref[...] = pb

        @pl.when(i > 0)
        def _():
            dwa_ref[...] += pw
            dba_ref[...] += pb

    return pl.pallas_call(
        body, name=name, grid=(s // tm,),
        in_specs=[pl.BlockSpec((tm, 512), lambda i: (i, 0)), pl.BlockSpec((tm, 128), lambda i: (i, col)),
                  pl.BlockSpec((128, 512), lambda i: (0, 0)), pl.BlockSpec((1, 512), lambda i: (0, 0))],
        out_specs=[pl.BlockSpec((tm, 128), lambda i: (i, 0)), pl.BlockSpec((128, 512), lambda i: (0, 0)),
                   pl.BlockSpec((1, 512), lambda i: (0, 0))],
        out_shape=[jax.ShapeDtypeStruct((s, 128), BF16), jax.ShapeDtypeStruct((128, 512), F32),
                   jax.ShapeDtypeStruct((1, 512), F32)],
        compiler_params=_cparams("arbitrary"),
    )(dla, z, wa, ba)


def _gla_masks(ch):
    ii = lax.broadcasted_iota(jnp.int32, (ch, ch), 0)
    tt = lax.broadcasted_iota(jnp.int32, (ch, ch), 1)
    return jnp.where(tt <= ii, 1.0, 0.0), jnp.where(tt >= ii, 1.0, 0.0)


def _gla_chunk(d, tmat, qv, kv, lav, ch):
    c = _split_dot(tmat, lav)
    big_l = c[ch - 1:ch, :] if d == 0 else c[0:1, :]
    qt = qv * (GLA_DK ** -0.5) * jnp.exp(c)
    kt = kv * jnp.exp(-c)
    kh = kv * jnp.exp(big_l - c)
    return c, big_l, qt, kt, kh


def _gla_fwd(qh, kh_, z, la, *, name):
    s = z.shape[0]
    ch = min(GLA_CHUNK, s)
    n = s // ch
    vcol = SEG["gv"][0] // GROUP_W

    def body(q0, k0, v0, la0, q1, k1, v1, la1, o0, o1, zs0, zs1, st):
        t = pl.program_id(0)

        @pl.when(t == 0)
        def _():
            st[...] = jnp.zeros_like(st)

        masks = _gla_masks(ch)
        for d, (q_ref, k_ref, v_ref, la_ref, o_ref, zs_ref) in enumerate(
                ((q0, k0, v0, la0, o0, zs0), (q1, k1, v1, la1, o1, zs1))):
            for h in range(GLA_HEADS):
                c, big_l, qt, kt, kh = _gla_chunk(d, masks[d], q_ref[h], k_ref[h], la_ref[0, h], ch)
                vv = v_ref[:, h * GLA_DV:(h + 1) * GLA_DV]
                p = _dot(qt, kt, 1, 1) * masks[d]
                zst = st[d, h]
                o_ref[:, h * GLA_DV:(h + 1) * GLA_DV] = _dot(p, vv) + _dot(qt, zst, 1, 1)
                zs_ref[h, 0] = zst
                st[d, h] = zst * jnp.exp(big_l) + _dot(vv, kh, 0, 0)

    cidx = (lambda t: t), (lambda t: n - 1 - t)
    hs = lambda d: pl.BlockSpec((GLA_HEADS, ch, GLA_DK), lambda t: (0, cidx[d](t), 0))
    vs = lambda d: pl.BlockSpec((ch, GROUP_W), lambda t: (cidx[d](t), vcol))
    las = lambda d: pl.BlockSpec((1, GLA_HEADS, ch, GLA_DK), lambda t: (d, 0, cidx[d](t), 0))
    os_ = lambda d: pl.BlockSpec((ch, GROUP_W), lambda t: (cidx[d](t), 0))
    zss = lambda d: pl.BlockSpec((GLA_HEADS, 1, GLA_DV, GLA_DK), lambda t: (0, cidx[d](t), 0, 0))
    o0, o1, zs0, zs1 = pl.pallas_call(
        body, name=name, grid=(n,),
        in_specs=[hs(0), hs(0), vs(0), las(0), hs(1), hs(1), vs(1), las(1)],
        out_specs=[os_(0), os_(1), zss(0), zss(1)],
        out_shape=[jax.ShapeDtypeStruct((s, GROUP_W), F32)] * 2
        + [jax.ShapeDtypeStruct((GLA_HEADS, n, GLA_DV, GLA_DK), F32)] * 2,
        scratch_shapes=[pltpu.VMEM((2, GLA_HEADS, GLA_DV, GLA_DK), F32)],
        compiler_params=_cparams("arbitrary"),
    )(qh, kh_, z, la, qh, kh_, z, la)
    return (o0, o1), (zs0, zs1)


def _gla_bwd(qh, kh_, z, la, do, zs, *, name):
    s = z.shape[0]
    ch = min(GLA_CHUNK, s)
    n = s // ch
    vcol = SEG["gv"][0] // GROUP_W

    def body(q0, k0, v0, la0, do0, zs0, q1, k1, v1, la1, do1, zs1,
             dq0, dk0, dla0, dv0, dq1, dk1, dla1, dv1, gz):
        t = pl.program_id(0)

        @pl.when(t == 0)
        def _():
            gz[...] = jnp.zeros_like(gz)

        masks = _gla_masks(ch)
        rows = lax.broadcasted_iota(jnp.int32, (ch, 1), 0)
        for d, (q_ref, k_ref, v_ref, la_ref, do_ref, zs_ref, dq_ref, dk_ref, dla_ref, dv_ref) in enumerate(
                ((q0, k0, v0, la0, do0, zs0, dq0, dk0, dla0, dv0), (q1, k1, v1, la1, do1, zs1, dq1, dk1, dla1, dv1))):
            tmat = masks[d]
            end = ch - 1 if d == 0 else 0
            for h in range(GLA_HEADS):
                c, big_l, qt, kt, kh = _gla_chunk(d, tmat, q_ref[h], k_ref[h], la_ref[0, h], ch)
                vsl = slice(h * GLA_DV, (h + 1) * GLA_DV)
                vv, dov, zst, gzv = v_ref[:, vsl], do_ref[:, vsl], zs_ref[h, 0], gz[d, h]
                p = _dot(qt, kt, 1, 1) * tmat
                dp = _dot(dov, vv, 1, 1) * tmat
                dqt = _dot(dp, kt) + _dot(dov, zst)
                dkt = _dot(dp, qt, 0, 0)
                dkh = _dot(vv, gzv)
                dv_ref[:, vsl] = _dot(p, dov, 0, 0) + _dot(kh, gzv, 1, 1)
                dq_ref[h] = dqt * jnp.exp(c) * (GLA_DK ** -0.5)
                dk_ref[h] = dkt * jnp.exp(-c) + dkh * jnp.exp(big_l - c)
                e_l = jnp.exp(big_l)
                d_l = jnp.sum(dkh * kh, axis=0, keepdims=True) + e_l * jnp.sum(zst * gzv, axis=0, keepdims=True)
                dc = dqt * qt - dkt * kt - dkh * kh + jnp.where(rows == end, d_l, 0.0)
                dla_ref[h] = _split_dot(tmat, dc, 0, 0)
                gz[d, h] = gzv * e_l + _dot(dov, qt, 0, 0)

    cidx = (lambda t: n - 1 - t), (lambda t: t)
    hs = lambda d: pl.BlockSpec((GLA_HEADS, ch, GLA_DK), lambda t: (0, cidx[d](t), 0))
    vs = lambda d: pl.BlockSpec((ch, GROUP_W), lambda t: (cidx[d](t), vcol))
    las = lambda d: pl.BlockSpec((1, GLA_HEADS, ch, GLA_DK), lambda t: (d, 0, cidx[d](t), 0))
    row = lambda d: pl.BlockSpec((ch, GROUP_W), lambda t: (cidx[d](t), 0))
    zss = lambda d: pl.BlockSpec((GLA_HEADS, 1, GLA_DV, GLA_DK), lambda t: (0, cidx[d](t), 0, 0))
    hshape = jax.ShapeDtypeStruct((GLA_HEADS, s, GLA_DK), F32)
    wide = jax.ShapeDtypeStruct((s, GROUP_W), F32)
    outs = pl.pallas_call(
        body, name=name, grid=(n,),
        in_specs=[hs(0), hs(0), vs(0), las(0), row(0), zss(0), hs(1), hs(1), vs(1), las(1), row(1), zss(1)],
        out_specs=[hs(0), hs(0), hs(0), row(0), hs(1), hs(1), hs(1), row(1)],
        out_shape=[hshape, hshape, hshape, wide, hshape, hshape, hshape, wide],
        scratch_shapes=[pltpu.VMEM((2, GLA_HEADS, GLA_DV, GLA_DK), F32)],
        compiler_params=_cparams("arbitrary"),
    )(qh, kh_, z, la, do, zs[0], qh, kh_, z, la, do, zs[1])
    dq0, dk0, dla0, dv0, dq1, dk1, dla1, dv1 = outs
    return (dq0, dq1), (dk0, dk1), (dla0, dla1), (dv0, dv1)


def _band(lo, hi, rows, width):
    r = lax.broadcasted_iota(jnp.int32, (rows, width), 0)
    j = lax.broadcasted_iota(jnp.int32, (rows, width), 1)
    k = j - POOL_HALO - r
    return jnp.where((k >= lo) & (k <= hi), 1.0, 0.0)


def _pool_cnt(t0, half, rows, s):
    t = t0 + lax.broadcasted_iota(jnp.int32, (rows, 1), 0)
    return (jnp.minimum(t + half, s) - jnp.maximum(t - half, 0)).astype(F32)


def _pool_fwd(z, pw, scale, *, name):
    s = z.shape[0]
    tl = min(POOL_TILE, s)
    nt = s // tl
    ucol, gcol = SEG["pv"][0] // 128, SEG["pg"][0] // 128

    def body(u_ref, gt_ref, pw_ref, sc_ref, y_ref, pad):
        g = pl.program_id(0)
        half = jnp.left_shift(1, g)
        pad[0:POOL_HALO, :] = jnp.zeros((POOL_HALO, POOL_GW), F32)
        pad[POOL_HALO + s:POOL_HALO + s + POOL_HALO, :] = jnp.zeros((POOL_HALO, POOL_GW), F32)
        pad[POOL_HALO:POOL_HALO + s, :] = u_ref[...]
        band = _band(-half, half - 1, tl, tl + 2 * POOL_HALO)
        pwv, scv = pw_ref[0], sc_ref[...]

        def tile(i, carry):
            t0 = pl.multiple_of(i * tl, tl)
            win = pad[pl.ds(t0, tl + 2 * POOL_HALO), :]
            u = win[POOL_HALO:POOL_HALO + tl, :]
            pooled = _split_dot(band, win) / _pool_cnt(t0, half, tl, s) - u
            mixed = _dot(pooled, pwv)
            silu, _ = _silu_parts(gt_ref[pl.ds(t0, tl), :])
            y_ref[pl.ds(t0, tl), :] = _bf(silu * (mixed * scv))
            return carry

        lax.fori_loop(0, nt, tile, 0)

    return pl.pallas_call(
        body, name=name, grid=(POOL_GROUPS,),
        in_specs=[pl.BlockSpec((s, POOL_GW), lambda g: (0, ucol + g)),
                  pl.BlockSpec((s, POOL_GW), lambda g: (0, gcol + g)),
                  pl.BlockSpec((1, POOL_GW, POOL_GW), lambda g: (g, 0, 0)),
                  pl.BlockSpec((1, POOL_GW), lambda g: (0, g))],
        out_specs=pl.BlockSpec((s, POOL_GW), lambda g: (0, g)),
        out_shape=jax.ShapeDtypeStruct((s, GROUP_W), BF16),
        scratch_shapes=[pltpu.VMEM((s + 2 * POOL_HALO, POOL_GW), F32)],
        compiler_params=_cparams("parallel"),
    )(z, z, pw, scale)


def _pool_bwd(dy, z, pw, scale, *, name):
    s = z.shape[0]
    tl = min(POOL_TILE, s)
    nt = s // tl
    ucol, gcol, ycol = SEG["pv"][0] // 128, SEG["pg"][0] // 128, 2 * GROUP_W // 128

    def body(dy_ref, u_ref, gt_ref, pw_ref, sc_ref, du_ref, dgt_ref, dpw_ref, dsc_ref, pad, epad, dpo):
        g = pl.program_id(0)
        half = jnp.left_shift(1, g)
        zeros = jnp.zeros((POOL_HALO, POOL_GW), F32)
        for buf in (pad, epad):
            buf[0:POOL_HALO, :] = zeros
            buf[POOL_HALO + s:POOL_HALO + s + POOL_HALO, :] = zeros
        pad[POOL_HALO:POOL_HALO + s, :] = u_ref[...]
        band = _band(-half, half - 1, tl, tl + 2 * POOL_HALO)
        band_t = _band(1 - half, half, tl, tl + 2 * POOL_HALO)
        pwv, scv = pw_ref[0], sc_ref[...]
        dpw_ref[0] = jnp.zeros((POOL_GW, POOL_GW), F32)
        dsc_ref[...] = jnp.zeros((1, POOL_GW), F32)

        def tile(i, carry):
            t0 = pl.multiple_of(i * tl, tl)
            win = pad[pl.ds(t0, tl + 2 * POOL_HALO), :]
            u = win[POOL_HALO:POOL_HALO + tl, :]
            cnt = _pool_cnt(t0, half, tl, s)
            pooled = _split_dot(band, win) / cnt - u
            mixed = _dot(pooled, pwv)
            silu, dsilu = _silu_parts(gt_ref[pl.ds(t0, tl), :])
            dyv = dy_ref[pl.ds(t0, tl), :]
            dgt_ref[pl.ds(t0, tl), :] = _bf(dyv * (mixed * scv) * dsilu)
            dsc_ref[...] += jnp.sum(dyv * silu * mixed, axis=0, keepdims=True)
            dm = dyv * silu * scv
            dpw_ref[0] += _dot(pooled, dm, 0, 0)
            dpooled = _dot(dm, pwv, 1, 1)
            dpo[pl.ds(t0, tl), :] = dpooled
            epad[pl.ds(POOL_HALO + t0, tl), :] = dpooled / cnt
            return carry

        lax.fori_loop(0, nt, tile, 0)

        def tile2(i, carry):
            t0 = pl.multiple_of(i * tl, tl)
            ewin = epad[pl.ds(t0, tl + 2 * POOL_HALO), :]
            du_ref[pl.ds(t0, tl), :] = _bf(_split_dot(band_t, ewin) - dpo[pl.ds(t0, tl), :])
            return carry

        lax.fori_loop(0, nt, tile2, 0)

    col = lambda c0: pl.BlockSpec((s, POOL_GW), lambda g: (0, c0 + g))
    return pl.pallas_call(
        body, name=name, grid=(POOL_GROUPS,),
        in_specs=[col(ycol), col(ucol), col(gcol), pl.BlockSpec((1, POOL_GW, POOL_GW), lambda g: (g, 0, 0)),
                  pl.BlockSpec((1, POOL_GW), lambda g: (0, g))],
        out_specs=[col(0), col(0), pl.BlockSpec((1, POOL_GW, POOL_GW), lambda g: (g, 0, 0)),
                   pl.BlockSpec((1, POOL_GW), lambda g: (0, g))],
        out_shape=[jax.ShapeDtypeStruct((s, GROUP_W), BF16), jax.ShapeDtypeStruct((s, GROUP_W), BF16),
                   jax.ShapeDtypeStruct((POOL_GROUPS, POOL_GW, POOL_GW), F32),
                   jax.ShapeDtypeStruct((1, GROUP_W), F32)],
        scratch_shapes=[pltpu.VMEM((s + 2 * POOL_HALO, POOL_GW), F32), pltpu.VMEM((s + 2 * POOL_HALO, POOL_GW), F32),
                        pltpu.VMEM((s, POOL_GW), F32)],
        compiler_params=_cparams("parallel"),
    )(dy, z, z, pw, scale)


def _mla_specs(tm):
    zq = pl.BlockSpec((tm, 512), lambda i: (i, SEG["mq"][0] // 512))
    zkv = pl.BlockSpec((tm, 256), lambda i: (i, SEG["mkv"][0] // 256))
    zkr = pl.BlockSpec((tm, 128), lambda i: (i, SEG["mkr"][0] // 128))
    full = lambda r, c: pl.BlockSpec((r, c), lambda i: (0, 0))
    tab = pl.BlockSpec((tm, 128), lambda i: (i, 0))
    weights = [full(1, 512), full(512, 1024), full(1, 256), full(256, 1024), full(1, 256), full(1, 256)]
    return [zq, zkv, zkr] + weights + [tab, tab, tab]


def _mla_project(xq_ref, xkv_ref, qg_ref, wq_ref, kvg_ref, wkv_ref):
    xq = xq_ref[...]
    r1 = lax.rsqrt(jnp.mean(xq * xq, axis=-1, keepdims=True) + EPS)
    xn1 = xq * r1
    qn = _bf(xn1 * qg_ref[...])
    qraw = _dot(qn, wq_ref[...])
    xkv = xkv_ref[...]
    r2 = lax.rsqrt(jnp.mean(xkv * xkv, axis=-1, keepdims=True) + EPS)
    xn2 = xkv * r2
    kvn = _bf(xn2 * kvg_ref[...])
    kvraw = _dot(kvn, wkv_ref[...])
    return r1, xn1, qn, qraw, r2, xn2, kvn, kvraw


def _mla_pre(z, qg, wq, kvg, wkv, qng, kng, cos, sp, sn, *, name, tm=256):
    s = z.shape[0]
    tm = min(tm, s)

    def body(xq_ref, xkv_ref, pe_ref, qg_ref, wq_ref, kvg_ref, wkv_ref, qng_ref, kng_ref, c_ref, sp_ref, sn_ref,
             q_ref, k_ref, v_ref):
        _, _, _, qraw, _, _, _, kvraw = _mla_project(xq_ref, xkv_ref, qg_ref, wq_ref, kvg_ref, wkv_ref)
        c, spv, snv = c_ref[...], sp_ref[...], sn_ref[...]
        pe = pe_ref[...]
        pe_ss = jnp.sum(pe * pe, axis=-1, keepdims=True)
        qngv, kngv = qng_ref[...], kng_ref[...]
        for h in range(MLA_HEADS):
            b = h * MLA_QKP
            qh = qraw[:, b:b + MLA_QKP]
            r = lax.rsqrt(jnp.sum(qh * qh, axis=-1, keepdims=True) * (1.0 / MLA_QK) + EPS)
            qn_h = qh * r * qngv
            q_ref[:, b:b + 128] = _bf(qn_h[:, :128] * MLA_SCALE)
            q_ref[:, b + 128:b + 256] = _bf(_rope64(qn_h[:, 128:], c, spv, snv) * MLA_SCALE)
            kn = kvraw[:, b:b + 128]
            rk = lax.rsqrt((jnp.sum(kn * kn, axis=-1, keepdims=True) + pe_ss) * (1.0 / MLA_QK) + EPS)
            k_ref[:, b:b + 128] = _bf(kn * rk * kngv[:, :128])
            k_ref[:, b + 128:b + 256] = _bf(_rope64(pe * rk * kngv[:, 128:], c, spv, snv))
            v_ref[:, h * MLA_V:(h + 1) * MLA_V] = _bf(kvraw[:, b + 128:b + 256])

    row = lambda w: pl.BlockSpec((tm, w), lambda i: (i, 0))
    return pl.pallas_call(
        body, name=name, grid=(s // tm,), in_specs=_mla_specs(tm),
        out_specs=[row(1024), row(1024), row(512)],
        out_shape=[jax.ShapeDtypeStruct((s, 1024), BF16), jax.ShapeDtypeStruct((s, 1024), BF16),
                   jax.ShapeDtypeStruct((s, 512), BF16)],
        compiler_params=_cparams("parallel"),
    )(z, z, z, qg, wq, kvg, wkv, qng, kng, cos, sp, sn)


def _mla_pre_bwd(dq, dk, dv, z, qg, wq, kvg, wkv, qng, kng, cos, sp, sn, *, name, tm=256):
    s = z.shape[0]
    tm = min(tm, s)

    def body(dq_ref, dk_ref, dv_ref, xq_ref, xkv_ref, pe_ref, qg_ref, wq_ref, kvg_ref, wkv_ref, qng_ref, kng_ref,
             c_ref, sp_ref, sn_ref, dxq_ref, dxkv_ref, dpe_ref, dwq_ref, dwkv_ref, dqg_ref, dkvg_ref, dqng_ref,
             dkng_ref, dqraw, dkvraw):
        i = pl.program_id(0)
        r1, xn1, qn, qraw, r2, xn2, kvn, kvraw = _mla_project(xq_ref, xkv_ref, qg_ref, wq_ref, kvg_ref, wkv_ref)
        c, spv, snv = c_ref[...], sp_ref[...], sn_ref[...]
        pe = pe_ref[...]
        pe_ss = jnp.sum(pe * pe, axis=-1, keepdims=True)
        qngv, kngv = qng_ref[...], kng_ref[...]
        dqng = jnp.zeros((1, MLA_QKP), F32)
        dkng = jnp.zeros((1, MLA_QKP), F32)
        dpe = jnp.zeros_like(pe)
        for h in range(MLA_HEADS):
            b = h * MLA_QKP
            qh = qraw[:, b:b + MLA_QKP]
            r = lax.rsqrt(jnp.sum(qh * qh, axis=-1, keepdims=True) * (1.0 / MLA_QK) + EPS)
            xn = qh * r
            d_n = jnp.concatenate(
                [dq_ref[:, b:b + 128], _unrope64(dq_ref[:, b + 128:b + 256], c, spv, snv)], axis=1) * MLA_SCALE
            dqng = dqng + jnp.sum(d_n * xn, axis=0, keepdims=True)
            dxn = d_n * qngv
            dqraw[:, b:b + MLA_QKP] = _bf(r * (dxn - xn * (jnp.sum(dxn * xn, axis=-1, keepdims=True) * (1.0 / MLA_QK))))
            kn = kvraw[:, b:b + 128]
            rk = lax.rsqrt((jnp.sum(kn * kn, axis=-1, keepdims=True) + pe_ss) * (1.0 / MLA_QK) + EPS)
            xk = jnp.concatenate([kn, pe], axis=1) * rk
            d_k = jnp.concatenate(
                [dk_ref[:, b:b + 128], _unrope64(dk_ref[:, b + 128:b + 256], c, spv, snv)], axis=1)
            dkng = dkng + jnp.sum(d_k * xk, axis=0, keepdims=True)
            dxk = d_k * kngv
            dfull = rk * (dxk - xk * (jnp.sum(dxk * xk, axis=-1, keepdims=True) * (1.0 / MLA_QK)))
            dkvraw[:, b:b + 128] = _bf(dfull[:, :128])
            dkvraw[:, b + 128:b + 256] = _bf(dv_ref[:, h * MLA_V:(h + 1) * MLA_V])
            dpe = dpe + dfull[:, 128:]
        dpe_ref[...] = _bf(dpe)
        dqr, dkvr = dqraw[...], dkvraw[...]
        dqn = _dot(dqr, wq_ref[...], 1, 1)
        dxn1 = dqn * qg_ref[...]
        dxq_ref[...] = _bf(r1 * (dxn1 - xn1 * jnp.mean(dxn1 * xn1, axis=-1, keepdims=True)))
        dkvn = _dot(dkvr, wkv_ref[...], 1, 1)
        dxn2 = dkvn * kvg_ref[...]
        dxkv_ref[...] = _bf(r2 * (dxn2 - xn2 * jnp.mean(dxn2 * xn2, axis=-1, keepdims=True)))
        parts = (_dot(qn, dqr, 0, 0), _dot(kvn, dkvr, 0, 0), jnp.sum(dqn * xn1, axis=0, keepdims=True),
                 jnp.sum(dkvn * xn2, axis=0, keepdims=True), dqng, dkng)
        accs = (dwq_ref, dwkv_ref, dqg_ref, dkvg_ref, dqng_ref, dkng_ref)

        @pl.when(i == 0)
        def _():
            for a, p in zip(accs, parts):
                a[...] = p

        @pl.when(i > 0)
        def _():
            for a, p in zip(accs, parts):
                a[...] += p

    row = lambda w: pl.BlockSpec((tm, w), lambda i: (i, 0))
    full = lambda r, c: pl.BlockSpec((r, c), lambda i: (0, 0))
    return pl.pallas_call(
        body, name=name, grid=(s // tm,),
        in_specs=[row(1024), row(1024), row(512)] + _mla_specs(tm),
        out_specs=[row(512), row(256), row(128), full(512, 1024), full(256, 1024), full(1, 512), full(1, 256),
                   full(1, 256), full(1, 256)],
        out_shape=[jax.ShapeDtypeStruct((s, 512), BF16), jax.ShapeDtypeStruct((s, 256), BF16),
                   jax.ShapeDtypeStruct((s, 128), BF16), jax.ShapeDtypeStruct((512, 1024), F32),
                   jax.ShapeDtypeStruct((256, 1024), F32), jax.ShapeDtypeStruct((1, 512), F32),
                   jax.ShapeDtypeStruct((1, 256), F32), jax.ShapeDtypeStruct((1, 256), F32),
                   jax.ShapeDtypeStruct((1, 256), F32)],
        scratch_shapes=[pltpu.VMEM((tm, 1024), BF16), pltpu.VMEM((tm, 1024), BF16)],
        compiler_params=_cparams("arbitrary"),
    )(dq, dk, dv, z, z, z, qg, wq, kvg, wkv, qng, kng, cos, sp, sn)


def _flash_fwd(q, k, v, *, name, tq=1024, tk=1024):
    s = q.shape[0]
    tq, tk = min(tq, s), min(tk, s)
    nk = s // tk

    def body(q_ref, k_ref, v_ref, o_ref, lse_ref, m_s, l_s, acc):
        j = pl.program_id(2)

        @pl.when(j == 0)
        def _():
            m_s[...] = jnp.full_like(m_s, -jnp.inf)
            l_s[...] = jnp.zeros_like(l_s)
            acc[...] = jnp.zeros_like(acc)

        sc = _dot(q_ref[...], k_ref[...], 1, 1)
        m_prev = m_s[...]
        m_new = jnp.maximum(m_prev, jnp.max(sc, axis=-1, keepdims=True))
        p = jnp.exp(sc - m_new[:, 0:1])
        alpha = jnp.exp(m_prev - m_new)
        l_s[...] = alpha * l_s[...] + jnp.sum(p, axis=-1, keepdims=True)
        acc[...] = alpha * acc[...] + _dot(p, v_ref[...])
        m_s[...] = m_new

        @pl.when(j == nk - 1)
        def _():
            o_ref[...] = acc[...] / l_s[...]
            lse_ref[...] = m_s[...] + jnp.log(l_s[...])

    return pl.pallas_call(
        body, name=name, grid=(MLA_HEADS, s // tq, nk),
        in_specs=[pl.BlockSpec((tq, MLA_QKP), lambda h, i, j: (i, h)),
                  pl.BlockSpec((tk, MLA_QKP), lambda h, i, j: (j, h)),
                  pl.BlockSpec((tk, MLA_V), lambda h, i, j: (j, h))],
        out_specs=[pl.BlockSpec((tq, MLA_V), lambda h, i, j: (i, h))] * 2,
        out_shape=[jax.ShapeDtypeStruct((s, GROUP_W), F32)] * 2,
        scratch_shapes=[pltpu.VMEM((tq, MLA_V), F32), pltpu.VMEM((tq, MLA_V), F32), pltpu.VMEM((tq, MLA_V), F32)],
        compiler_params=_cparams("parallel", "parallel", "arbitrary"),
    )(q, k, v)


def _flash_bwd_dq(q, k, v, do, o, lse, *, name, tq=1024, tk=1024):
    s = q.shape[0]
    tq, tk = min(tq, s), min(tk, s)
    nk = s // tk

    def body(q_ref, k_ref, v_ref, do_ref, o_ref, lse_ref, dq_ref, acc):
        j = pl.program_id(2)
        dov = do_ref[...]
        delta = jnp.sum(dov * o_ref[...], axis=-1, keepdims=True)
        p = jnp.exp(_dot(q_ref[...], k_ref[...], 1, 1) - lse_ref[:, 0:1])
        ds = p * (_dot(dov, v_ref[...], 1, 1) - delta)
        part = _dot(ds, k_ref[...])

        @pl.when(j == 0)
        def _():
            acc[...] = part

        @pl.when(j > 0)
        def _():
            acc[...] += part

        @pl.when(j == nk - 1)
        def _():
            dq_ref[...] = acc[...]

    qb = pl.BlockSpec((tq, MLA_QKP), lambda h, i, j: (i, h))
    ob = pl.BlockSpec((tq, MLA_V), lambda h, i, j: (i, h))
    return pl.pallas_call(
        body, name=name, grid=(MLA_HEADS, s // tq, nk),
        in_specs=[qb, pl.BlockSpec((tk, MLA_QKP), lambda h, i, j: (j, h)),
                  pl.BlockSpec((tk, MLA_V), lambda h, i, j: (j, h)), ob,
                  ob, ob],
        out_specs=qb,
        out_shape=jax.ShapeDtypeStruct((s, MLA_HEADS * MLA_QKP), F32),
        scratch_shapes=[pltpu.VMEM((tq, MLA_QKP), F32)],
        compiler_params=_cparams("parallel", "parallel", "arbitrary"),
    )(q, k, v, do, o, lse)


def _flash_bwd_dkv(q, k, v, do, o, lse, *, name, tq=1024, tk=1024):
    s = q.shape[0]
    tq, tk = min(tq, s), min(tk, s)
    nq = s // tq

    def body(q_ref, k_ref, v_ref, do_ref, o_ref, lse_ref, dk_ref, dv_ref, dk_acc, dv_acc):
        i = pl.program_id(2)
        dov = do_ref[...]
        delta = jnp.sum(dov * o_ref[...], axis=-1, keepdims=True)
        p = jnp.exp(_dot(q_ref[...], k_ref[...], 1, 1) - lse_ref[:, 0:1])
        ds = p * (_dot(dov, v_ref[...], 1, 1) - delta)
        pv = _dot(p, dov, 0, 0)
        pk = _dot(ds, q_ref[...], 0, 0)

        @pl.when(i == 0)
        def _():
            dv_acc[...] = pv
            dk_acc[...] = pk

        @pl.when(i > 0)
        def _():
            dv_acc[...] += pv
            dk_acc[...] += pk

        @pl.when(i == nq - 1)
        def _():
            dk_ref[...] = dk_acc[...]
            dv_ref[...] = dv_acc[...]

    kb = pl.BlockSpec((tk, MLA_QKP), lambda h, j, i: (j, h))
    vb = pl.BlockSpec((tk, MLA_V), lambda h, j, i: (j, h))
    ob = pl.BlockSpec((tq, MLA_V), lambda h, j, i: (i, h))
    return pl.pallas_call(
        body, name=name, grid=(MLA_HEADS, s // tk, nq),
        in_specs=[pl.BlockSpec((tq, MLA_QKP), lambda h, j, i: (i, h)), kb, vb, ob,
                  ob, ob],
        out_specs=[kb, vb],
        out_shape=[jax.ShapeDtypeStruct((s, MLA_HEADS * MLA_QKP), F32), jax.ShapeDtypeStruct((s, GROUP_W), F32)],
        scratch_shapes=[pltpu.VMEM((tk, MLA_QKP), F32), pltpu.VMEM((tk, MLA_V), F32)],
        compiler_params=_cparams("parallel", "parallel", "arbitrary"),
    )(q, k, v, do, o, lse)


def _rows_tile(r, c, itemsize=4, budget=2 * 1024 * 1024):
    if r * c * itemsize <= budget:
        return r
    best = None
    for t in range(8, r, 8):
        if r % t == 0 and t * c * itemsize <= budget:
            best = t
    return best if best is not None else r


def _add_n(arrs, *, out_dtype=F32, name):
    shape = arrs[0].shape
    c = shape[-1]
    flat = [a.reshape(-1, c) for a in arrs]
    r = flat[0].shape[0]
    t = _rows_tile(r, c)

    def body(*refs):
        acc = refs[0][...].astype(F32)
        for ref in refs[1:-1]:
            acc = acc + ref[...].astype(F32)
        refs[-1][...] = acc.astype(out_dtype)

    blk = pl.BlockSpec((t, c), lambda i: (i, 0))
    out = pl.pallas_call(
        body, name=name, grid=(r // t,), in_specs=[blk] * len(flat), out_specs=blk,
        out_shape=jax.ShapeDtypeStruct((r, c), out_dtype), compiler_params=_cparams("parallel"),
    )(*flat)
    return out.reshape(shape)


def _adamw(w, g, m, v, *, name):
    shape = w.shape
    c = shape[-1]
    flat = [a.reshape(-1, c) for a in (w, g, m, v)]
    r = flat[0].shape[0]
    t = _rows_tile(r, c, budget=1024 * 1024)

    def body(w_ref, g_ref, m_ref, v_ref, d_ref, mo_ref, vo_ref):
        gv = g_ref[...]
        m2 = ADAM_B1 * m_ref[...] + (1.0 - ADAM_B1) * gv
        v2 = ADAM_B2 * v_ref[...] + (1.0 - ADAM_B2) * (gv * gv)
        m_hat = m2 / (1.0 - ADAM_B1 ** ADAM_STEP)
        v_hat = v2 / (1.0 - ADAM_B2 ** ADAM_STEP)
        d_ref[...] = -ADAM_LR * (m_hat / (jnp.sqrt(v_hat) + ADAM_EPS) + ADAM_WD * w_ref[...])
        mo_ref[...] = m2
        vo_ref[...] = v2

    blk = pl.BlockSpec((t, c), lambda i: (i, 0))
    outs = pl.pallas_call(
        body, name=name, grid=(r // t,), in_specs=[blk] * 4, out_specs=[blk] * 3,
        out_shape=[jax.ShapeDtypeStruct((r, c), F32)] * 3, compiler_params=_cparams("parallel"),
    )(*flat)
    return tuple(o.reshape(shape) for o in outs)


def _place():
    x, y, c = lax.axis_index("x"), lax.axis_index("y"), lax.axis_index("c")
    chips = [(1 - x, y), (x, 1 - y), (1 - x, 1 - y)]
    return x, y, c, chips


ANY = pl.BlockSpec(memory_space=pl.ANY)


def _gather_shards(shards, *, name):
    nt = len(shards)

    def body(*refs):
        src, dst = refs[:nt], refs[nt:2 * nt]
        send, recv, fsend, frecv, lsem = refs[2 * nt:]
        x, y, c, chips = _place()
        me = 2 * x + y
        local = [pltpu.make_async_copy(src[t], dst[t].at[me], lsem.at[t]) for t in range(nt)]
        for cp in local:
            cp.start()

        def half(t, slot, hc):
            hr = src[t].shape[0] // 2
            return dst[t].at[slot, pl.ds(hc * hr, hr)]

        def first(t, k):
            hr = src[t].shape[0] // 2
            return pltpu.make_async_remote_copy(
                src_ref=src[t].at[pl.ds(c * hr, hr)], dst_ref=half(t, me, c),
                send_sem=send.at[t, k], recv_sem=recv.at[t, k],
                device_id=(chips[k][0], chips[k][1], c), device_id_type=MESH)

        def landed(t, k):
            slot = 2 * chips[k][0] + chips[k][1]
            return pltpu.make_async_remote_copy(
                src_ref=half(t, slot, c), dst_ref=half(t, slot, c),
                send_sem=send.at[t, k], recv_sem=recv.at[t, k],
                device_id=(chips[k][0], chips[k][1], c), device_id_type=MESH)

        def forward(t, k, hc):
            slot = 2 * chips[k][0] + chips[k][1]
            return pltpu.make_async_remote_copy(
                src_ref=half(t, slot, hc), dst_ref=half(t, slot, hc),
                send_sem=fsend.at[t, k], recv_sem=frecv.at[t, k],
                device_id=(x, y, 1 - c), device_id_type=MESH)

        for t in range(nt):
            for k in range(3):
                first(t, k).start()
        for t in range(nt):
            for k in range(3):
                landed(t, k).wait_recv()
                forward(t, k, c).start()
        for t in range(nt):
            for k in range(3):
                forward(t, k, 1 - c).wait_recv()
        for t in range(nt):
            for k in range(3):
                first(t, k).wait_send()
                forward(t, k, c).wait_send()
        for cp in local:
            cp.wait()

    return pl.pallas_call(
        body, name=name, in_specs=[ANY] * nt, out_specs=[ANY] * nt,
        out_shape=[jax.ShapeDtypeStruct((N_CHIP,) + a.shape, a.dtype) for a in shards],
        scratch_shapes=[pltpu.SemaphoreType.DMA((nt, 3)), pltpu.SemaphoreType.DMA((nt, 3)),
                        pltpu.SemaphoreType.DMA((nt, 3)), pltpu.SemaphoreType.DMA((nt, 3)),
                        pltpu.SemaphoreType.DMA((nt,))],
    )(*shards)


def _comm_rows(hr, c, budget=2 * 1024 * 1024):
    if hr * c * 4 <= budget:
        return hr
    best = None
    for t in range(16, hr, 16):
        if hr % t == 0 and t * c * 4 <= budget:
            best = t
    return best if best is not None else hr


def _pair_reduce(g, where, *, out_dtype, name):
    n_slot, r, cdim = g.shape
    hr = r // 2
    rc = _comm_rows(hr, cdim)
    nr = hr // rc
    steps = n_slot * nr
    g4 = g.reshape(n_slot, 2, hr, cdim)

    def body(w_ref, a_ref, b_ref, o_ref, land, send, recv, credit):
        x, y, c, _ = _place()
        sib = (x, y, 1 - c)
        i = pl.program_id(0) * nr + pl.program_id(1)
        s = lax.rem(i, 2)

        @pl.when(i >= 2)
        def _():
            pl.semaphore_wait(credit.at[s], 1)

        cp = pltpu.make_async_remote_copy(src_ref=b_ref.at[0, 0], dst_ref=land.at[s], send_sem=send.at[s],
                                          recv_sem=recv.at[s], device_id=sib, device_id_type=MESH)
        cp.start()
        cp.wait_recv()
        o_ref[0] = (a_ref[0, 0] + land[s]).astype(out_dtype)
        cp.wait_send()

        @pl.when(i + 2 < steps)
        def _():
            pl.semaphore_signal(credit.at[s], inc=1, device_id=sib, device_id_type=MESH)

    blk = lambda half: pl.BlockSpec((1, 1, rc, cdim), lambda j, t, w: (j, half(w), t, 0))
    grid_spec = pltpu.PrefetchScalarGridSpec(
        num_scalar_prefetch=1, grid=(n_slot, nr),
        in_specs=[blk(lambda w: w[0]), blk(lambda w: 1 - w[0])],
        out_specs=pl.BlockSpec((1, rc, cdim), lambda j, t, w: (j, t, 0)),
        scratch_shapes=[pltpu.VMEM((2, rc, cdim), F32), pltpu.SemaphoreType.DMA((2,)), pltpu.SemaphoreType.DMA((2,)),
                        pltpu.SemaphoreType.REGULAR((2,))])
    return pl.pallas_call(
        body, name=name, grid_spec=grid_spec, out_shape=jax.ShapeDtypeStruct((n_slot, hr, cdim), out_dtype),
        compiler_params=_cparams("arbitrary", "arbitrary"),
    )(where, g4, g4)


def _chip_exchange(parts, *, name):
    nt = len(parts)

    def body(*refs):
        src, got = refs[:nt], refs[nt:2 * nt]
        send, recv = refs[2 * nt:]
        x, y, c, chips = _place()
        remote = []
        for t in range(nt):
            for k in range(3):
                remote.append(pltpu.make_async_remote_copy(
                    src_ref=src[t].at[2 * chips[k][0] + chips[k][1]], dst_ref=got[t].at[k],
                    send_sem=send.at[t, k], recv_sem=recv.at[t, k],
                    device_id=(chips[k][0], chips[k][1], c), device_id_type=MESH))
        for cp in remote:
            cp.start()
        for cp in remote:
            cp.wait_recv()
        for cp in remote:
            cp.wait_send()

    return pl.pallas_call(
        body, name=name, in_specs=[ANY] * nt, out_specs=[ANY] * nt,
        out_shape=[jax.ShapeDtypeStruct((3,) + a.shape[1:], a.dtype) for a in parts],
        scratch_shapes=[pltpu.SemaphoreType.DMA((nt, 3)), pltpu.SemaphoreType.DMA((nt, 3))],
    )(*parts)


def _sum_join(p, got, where, *, name):
    _, hr, cdim = p.shape
    rc = _comm_rows(hr, cdim)
    n = hr // rc

    def body(w_ref, p_ref, g_ref, out, buf, lsem, ssem, rsem):
        x, y, c, _ = _place()
        sib = (x, y, 1 - c)
        r = pl.program_id(0)

        def copies(step, slot):
            rows = out.at[pl.ds(pl.multiple_of(c * hr + step * rc, 8), rc)]
            return (pltpu.make_async_copy(buf.at[slot], rows, lsem.at[slot]),
                    pltpu.make_async_remote_copy(src_ref=buf.at[slot], dst_ref=rows, send_sem=ssem.at[slot],
                                                 recv_sem=rsem, device_id=sib, device_id_type=MESH))

        s = lax.rem(r, 2)

        @pl.when(r >= 2)
        def _():
            lc, rm = copies(r - 2, s)
            lc.wait()
            rm.wait_send()

        buf[s] = p_ref[0].astype(F32) + g_ref[0].astype(F32) + g_ref[1].astype(F32) + g_ref[2].astype(F32)
        lc, rm = copies(r, s)
        lc.start()
        rm.start()

        @pl.when(r == n - 1)
        def _():
            for step in range(max(0, n - 2), n):
                lc, rm = copies(step, step % 2)
                lc.wait()
                rm.wait_send()
            whole = out.at[pl.ds(0, hr)]
            pltpu.make_async_remote_copy(src_ref=whole, dst_ref=whole, send_sem=ssem.at[0], recv_sem=rsem,
                                         device_id=sib, device_id_type=MESH).wait_recv()

    grid_spec = pltpu.PrefetchScalarGridSpec(
        num_scalar_prefetch=1, grid=(n,),
        in_specs=[pl.BlockSpec((1, rc, cdim), lambda t, w: (w[1], t, 0)),
                  pl.BlockSpec((3, rc, cdim), lambda t, w: (0, t, 0))],
        out_specs=ANY,
        scratch_shapes=[pltpu.VMEM((2, rc, cdim), F32), pltpu.SemaphoreType.DMA((2,)), pltpu.SemaphoreType.DMA((2,)),
                        pltpu.SemaphoreType.DMA])
    return pl.pallas_call(
        body, name=name, grid_spec=grid_spec, out_shape=jax.ShapeDtypeStruct((2 * hr, cdim), F32),
        compiler_params=_cparams("arbitrary"),
    )(where, p, got)


def _gather_all(block, *, name):
    m_per, n = block.shape

    def body(x_ref, out_ref, send_sems, recv_sems, local_sem):
        x, y, c, chips = _place()
        me, sibling = (x, y, c), (x, y, 1 - c)

        def rows(px, py, pc):
            return out_ref.at[4 * px + 2 * py + pc]

        def copy(k, blk, to, src=None):
            return pltpu.make_async_remote_copy(
                src_ref=rows(*blk) if src is None else src, dst_ref=rows(*blk),
                send_sem=send_sems.at[k], recv_sem=recv_sems.at[k], device_id=to, device_id_type=MESH)

        mine = pltpu.make_async_copy(x_ref, rows(*me), local_sem)
        mine.start()
        first = [copy(0, me, sibling, src=x_ref)]
        first += [copy(1 + j, me, (*chip, c), src=x_ref) for j, chip in enumerate(chips)]
        for cp in first:
            cp.start()
        passed = [copy(4 + j, (*chip, c), sibling) for j, chip in enumerate(chips)]
        for j, chip in enumerate(chips):
            copy(1 + j, (*chip, c), me).wait_recv()
            passed[j].start()
        copy(0, sibling, me).wait_recv()
        for j, chip in enumerate(chips):
            copy(4 + j, (*chip, 1 - c), me).wait_recv()
        for cp in first + passed:
            cp.wait_send()
        mine.wait()

    return pl.pallas_call(
        body, name=name,
        out_shape=jax.ShapeDtypeStruct((N_DEV, m_per, n), block.dtype),
        in_specs=[pl.BlockSpec(memory_space=pltpu.VMEM)], out_specs=pl.BlockSpec(memory_space=pltpu.VMEM),
        scratch_shapes=[pltpu.SemaphoreType.DMA((7,)), pltpu.SemaphoreType.DMA((7,)), pltpu.SemaphoreType.DMA],
        compiler_params=pltpu.CompilerParams(vmem_limit_bytes=VMEM_LIMIT),
    )(block)


def _sum_slots(slots, *, name):
    n, m, c = slots.shape
    t = _rows_tile(m, c * n)

    def body(s_ref, o_ref):
        acc = s_ref[0]
        for k in range(1, n):
            acc = acc + s_ref[k]
        o_ref[...] = acc

    return pl.pallas_call(
        body, name=name, grid=(m // t,), in_specs=[pl.BlockSpec((n, t, c), lambda i: (0, i, 0))],
        out_specs=pl.BlockSpec((t, c), lambda i: (i, 0)), out_shape=jax.ShapeDtypeStruct((m, c), F32),
        compiler_params=_cparams("parallel"),
    )(slots)


def _pad_cols(a, width):
    return a if a.shape[1] == width else jnp.pad(a, ((0, 0), (0, width - a.shape[1])))


def _w_in_padded(shards):
    full = jnp.concatenate([shards[j] for j in range(N_CHIP)], axis=1)
    return jnp.concatenate([_pad_cols(full[:, SEG[n][2]:SEG[n][2] + SEG[n][3]], SEG[n][1]) for n in SEG_ORDER], axis=1)


def _w_in_unpadded(gp):
    full = jnp.concatenate([gp[:, SEG[n][0]:SEG[n][0] + SEG[n][3]] for n in ORIG_ORDER], axis=1)
    w = IN_COLS // N_CHIP
    return jnp.stack([full[:, j * w:(j + 1) * w] for j in range(N_CHIP)])


def _pad_heads(w, true_w, pad_w):
    r = w.shape[0]
    h = w.shape[1] // true_w
    return jnp.pad(w.reshape(r, h, true_w), ((0, 0), (0, 0), (0, pad_w - true_w))).reshape(r, h * pad_w)


def _unpad_heads(w, true_w, pad_w):
    r = w.shape[0]
    h = w.shape[1] // pad_w
    return w.reshape(r, h, pad_w)[:, :, :true_w].reshape(r, h * true_w)


def _cols_to_slots(a):
    w = a.shape[1] // N_CHIP
    return jnp.stack([a[:, j * w:(j + 1) * w] for j in range(N_CHIP)])


def _slots_to_cols(a):
    return jnp.concatenate([a[j] for j in range(N_CHIP)], axis=1)


def _to_heads(a, h, d):
    return a.reshape(a.shape[0], h, d).transpose(1, 0, 2)


def _from_heads(a):
    return a.transpose(1, 0, 2).reshape(a.shape[1], -1)


SMALL = [("norm_g", 2048), ("ret_norm_g", 512), ("gla_ba_f", 256), ("gla_ba_b", 256), ("gla_norm_g", 512),
         ("pool_w", 4 * 128 * 128), ("pool_scale", 512), ("mla_q_norm_g", 512), ("mla_kv_norm_g", 256),
         ("mla_qk_norm_q", 192), ("mla_qk_norm_k", 192)]


def _pack_small(vals):
    parts = []
    for name, n in SMALL:
        v = vals[name].reshape(-1)
        parts.append(jnp.pad(v, (0, (-v.shape[0]) % 1024)))
    parts.append(jnp.pad(vals["loss"].reshape(-1), (0, 1023)))
    return jnp.concatenate(parts).reshape(-1, 128)


def _unpack_small(block):
    flat = block.reshape(-1)
    out, off = {}, 0
    for name, n in SMALL:
        out[name] = flat[off:off + DEPTH * n]
        off += DEPTH * n + (-(DEPTH * n)) % 1024
    out["loss"] = flat[off]
    return out


def _layer_weights(l, p, g):
    wa = jnp.zeros((128, 512), F32)
    wa = wa.at[0:GLA_RANK, 0:256].set(_slots_to_cols(g["gla_wa2_f"][:, l]))
    wa = wa.at[GLA_RANK:2 * GLA_RANK, 256:512].set(_slots_to_cols(g["gla_wa2_b"][:, l]))
    return dict(
        norm_g=p["norm_g"][l][None, :],
        w_in=_w_in_padded(g["w_in"][:, l]),
        w_out=g["w_out"][:, l].reshape(4 * g["w_out"].shape[2], -1),
        ret_norm_g=p["ret_norm_g"][l][None, :],
        wa=_bf(wa),
        ba=jnp.concatenate([p["gla_ba_f"][l], p["gla_ba_b"][l]])[None, :],
        gla_norm_g=p["gla_norm_g"][l][None, :],
        pool_w=_bf(p["pool_w"][l]),
        pool_scale=p["pool_scale"][l][None, :],
        qg=p["mla_q_norm_g"][l][None, :],
        wq=_pad_heads(_slots_to_cols(g["mla_wq_b"][:, l]), MLA_QK, MLA_QKP),
        kvg=p["mla_kv_norm_g"][l][None, :],
        wkv=_slots_to_cols(g["mla_wkv_b"][:, l]),
        qng=jnp.pad(p["mla_qk_norm_q"][l], (0, MLA_QKP - MLA_QK))[None, :],
        kng=jnp.pad(p["mla_qk_norm_k"][l], (0, MLA_QKP - MLA_QK))[None, :],
    )


def _layer_fwd(l, x, w, tabs):
    ret_cos, ret_sin, mla_cos, mla_sp, mla_sn = tabs
    nm = lambda s: f"l{l}_{s}"
    h = _rmsnorm_fwd(x, w["norm_g"], name=nm("norm"))
    z = _matmul(h, w["w_in"], name=nm("in_proj"))
    qr, kr = _ret_pre(z, ret_cos, ret_sin, name=nm("ret_pre"))
    ret_o = _bla(qr, kr, z, _ret_log_gamma(False), (0, 0, SEG["rv"][0] // 512), name=nm("ret_scan"))
    y_a = _post(ret_o, z, SEG["rg"][0] // 512, w["ret_norm_g"], norm=True, name=nm("ret_post"))
    la = _gla_gate(z, w["wa"], w["ba"], name=nm("gla_gate"))
    la_h = jnp.stack([_to_heads(la[:, :256], GLA_HEADS, GLA_DK), _to_heads(la[:, 256:], GLA_HEADS, GLA_DK)])
    gq = _to_heads(z[:, SEG["gq"][0]:SEG["gq"][0] + 256], GLA_HEADS, GLA_DK)
    gk = _to_heads(z[:, SEG["gk"][0]:SEG["gk"][0] + 256], GLA_HEADS, GLA_DK)
    gla_o, gla_st = _gla_fwd(gq, gk, z, la_h, name=nm("gla_scan"))
    y_b = _post(gla_o, z, SEG["gg"][0] // 512, w["gla_norm_g"], norm=True, name=nm("gla_post"))
    y_c = _pool_fwd(z, w["pool_w"], w["pool_scale"], name=nm("pool"))
    q, k, v = _mla_pre(z, w["qg"], w["wq"], w["kvg"], w["wkv"], w["qng"], w["kng"], mla_cos, mla_sp, mla_sn,
                       name=nm("mla_pre"))
    att_o, lse = _flash_fwd(q, k, v, name=nm("attn"))
    y_d = _post([att_o], z, SEG["mg"][0] // 512, w["qg"], norm=False, name=nm("mla_post"))
    y = jnp.concatenate([y_a, y_b, y_c, y_d], axis=1)
    x_next = _matmul(y, w["w_out"], add=x, name=nm("out_proj"))
    saved = dict(x=x, h=h, z=z, y=y, qr=qr, kr=kr, ret_o=ret_o, la_h=la_h, gq=gq, gk=gk, gla_o=gla_o, gla_st=gla_st,
                 q=q, k=k, v=v, att_o=att_o, lse=lse)
    return x_next, saved


def _layer_bwd(l, dx_next, w, sv, tabs):
    ret_cos, ret_sin, mla_cos, mla_sp, mla_sn = tabs
    nm = lambda s: f"l{l}_{s}"
    z = sv["z"]
    dy = _matmul(dx_next, w["w_out"], tb=True, tk=1024, name=nm("out_proj_dy"))
    d_w_out = _matmul(sv["y"], dx_next, ta=True, tk=1024, name=nm("out_proj_dw"))
    d_rg, d_ret_o, d_ret_g = _post_bwd(dy, 0, sv["ret_o"], z, SEG["rg"][0] // 512, w["ret_norm_g"], norm=True,
                                       name=nm("ret_post_bwd"))
    vcol = SEG["rv"][0] // 512
    dqr = _bla(d_ret_o, z, sv["kr"], _ret_log_gamma(False), (0, vcol, 0), name=nm("ret_scan_dq"))
    dkr = _bla(z, d_ret_o, sv["qr"], _ret_log_gamma(True), (vcol, 0, 0), name=nm("ret_scan_dk"))
    drv = _bla(sv["kr"], sv["qr"], d_ret_o, _ret_log_gamma(True), (0, 0, 0), name=nm("ret_scan_dv"))
    d_rq, d_rk = _ret_pre_bwd(dqr, dkr, ret_cos, ret_sin, name=nm("ret_pre_bwd"))
    d_rv = _add_n([drv[0], drv[1]], out_dtype=BF16, name=nm("ret_dv_sum"))
    d_gg, d_gla_o, d_gla_g = _post_bwd(dy, 1, sv["gla_o"], z, SEG["gg"][0] // 512, w["gla_norm_g"], norm=True,
                                       name=nm("gla_post_bwd"))
    dq2, dk2, dla2, dv2 = _gla_bwd(sv["gq"], sv["gk"], z, sv["la_h"], d_gla_o, sv["gla_st"], name=nm("gla_scan_bwd"))
    d_gq = _bf(_from_heads(dq2[0] + dq2[1]))
    d_gk = _bf(_from_heads(dk2[0] + dk2[1]))
    d_gv = _add_n([dv2[0], dv2[1]], out_dtype=BF16, name=nm("gla_dv_sum"))
    dla = jnp.concatenate([_from_heads(dla2[0]), _from_heads(dla2[1])], axis=1)
    d_ga, d_wa, d_ba = _gla_gate_bwd(dla, z, w["wa"], w["ba"], name=nm("gla_gate_bwd"))
    d_pv, d_pg, d_pool_w, d_pool_scale = _pool_bwd(dy, z, w["pool_w"], w["pool_scale"], name=nm("pool_bwd"))
    d_mg, d_att_o, _ = _post_bwd(dy, 3, [sv["att_o"]], z, SEG["mg"][0] // 512, w["qg"], norm=False,
                                 name=nm("mla_post_bwd"))
    dq = _flash_bwd_dq(sv["q"], sv["k"], sv["v"], d_att_o, sv["att_o"], sv["lse"], name=nm("attn_dq"))
    dk, dv = _flash_bwd_dkv(sv["q"], sv["k"], sv["v"], d_att_o, sv["att_o"], sv["lse"], name=nm("attn_dkv"))
    d_mq, d_mkv, d_mkr, d_wq, d_wkv, d_qg, d_kvg, d_qng, d_kng = _mla_pre_bwd(
        dq, dk, dv, z, w["qg"], w["wq"], w["kvg"], w["wkv"], w["qng"], w["kng"], mla_cos, mla_sp, mla_sn,
        name=nm("mla_pre_bwd"))
    segs = dict(rq=d_rq, rk=d_rk, rv=d_rv, rg=d_rg, gv=d_gv, gg=d_gg, pv=d_pv, pg=d_pg, mq=d_mq, mg=d_mg,
                gq=d_gq, gk=d_gk, mkv=d_mkv, ga=d_ga, mkr=d_mkr)
    dz = jnp.concatenate([segs[n] for n in SEG_ORDER], axis=1)
    dh = _matmul(dz, w["w_in"], tb=True, tk=2048, name=nm("in_proj_dh"))
    d_w_in = _matmul(sv["h"], dz, ta=True, tk=1024, name=nm("in_proj_dw"))
    dx, d_norm_g = _rmsnorm_bwd(sv["x"], dh, w["norm_g"], dx_next, name=nm("norm_bwd"))
    sharded = dict(
        w_in=_w_in_unpadded(d_w_in),
        w_out=d_w_out.reshape(N_CHIP, d_w_out.shape[0] // N_CHIP, d_w_out.shape[1]),
        mla_wq_b=_cols_to_slots(_unpad_heads(d_wq, MLA_QK, MLA_QKP)),
        mla_wkv_b=_cols_to_slots(d_wkv),
        gla_wa2_f=_cols_to_slots(d_wa[0:GLA_RANK, 0:256]),
        gla_wa2_b=_cols_to_slots(d_wa[GLA_RANK:2 * GLA_RANK, 256:512]),
    )
    small = dict(
        norm_g=d_norm_g[0], ret_norm_g=d_ret_g[0], gla_ba_f=d_ba[0, :256], gla_ba_b=d_ba[0, 256:],
        gla_norm_g=d_gla_g[0], pool_w=d_pool_w.reshape(-1), pool_scale=d_pool_scale[0], mla_q_norm_g=d_qg[0],
        mla_kv_norm_g=d_kvg[0], mla_qk_norm_q=d_qng[0, :MLA_QK], mla_qk_norm_k=d_kng[0, :MLA_QK],
    )
    return dx, sharded, small


SHARDED = ["w_in", "w_out", "mla_wq_b", "mla_wkv_b", "gla_wa2_f", "gla_wa2_b"]
WEIGHTS = ["norm_g", "w_in", "ret_norm_g", "gla_wa2_f", "gla_ba_f", "gla_wa2_b", "gla_ba_b", "gla_norm_g", "pool_w",
           "pool_scale", "mla_q_norm_g", "mla_wq_b", "mla_kv_norm_g", "mla_wkv_b", "mla_qk_norm_q", "mla_qk_norm_k",
           "w_out"]


def _local_step(p, gathered):
    x = p["x"][0]
    tabs = _rope_tables(x.shape[0])
    ws, saved = [], []
    for l in range(DEPTH):
        w = _layer_weights(l, p, gathered)
        x, sv = _layer_fwd(l, x, w, tabs)
        ws.append(w)
        saved.append(sv)
    dx, loss = _loss_head(x, p["loss_target"][0], name="loss_head")
    sharded, small = [None] * DEPTH, [None] * DEPTH
    for l in reversed(range(DEPTH)):
        dx, sharded[l], small[l] = _layer_bwd(l, dx, ws[l], saved[l], tabs)
    sharded = {n: jnp.stack([sharded[l][n] for l in range(DEPTH)], axis=1) for n in SHARDED}
    small = {n: jnp.stack([small[l][n] for l in range(DEPTH)]) for n, _ in SMALL}
    small["loss"] = loss
    return dx[None], sharded, small


def kernel(x, norm_g, w_in, ret_norm_g, gla_wa2_f, gla_ba_f, gla_wa2_b, gla_ba_b, gla_norm_g, pool_w, pool_scale, mla_q_norm_g, mla_wq_b, mla_kv_norm_g, mla_wkv_b, mla_qk_norm_q, mla_qk_norm_k, w_out, loss_target, m_norm_g, m_w_in, m_ret_norm_g, m_gla_wa2_f, m_gla_ba_f, m_gla_wa2_b, m_gla_ba_b, m_gla_norm_g, m_pool_w, m_pool_scale, m_mla_q_norm_g, m_mla_wq_b, m_mla_kv_norm_g, m_mla_wkv_b, m_mla_qk_norm_q, m_mla_qk_norm_k, m_w_out, v_norm_g, v_w_in, v_ret_norm_g, v_gla_wa2_f, v_gla_ba_f, v_gla_wa2_b, v_gla_ba_b, v_gla_norm_g, v_pool_w, v_pool_scale, v_mla_q_norm_g, v_mla_wq_b, v_mla_kv_norm_g, v_mla_wkv_b, v_mla_qk_norm_q, v_mla_qk_norm_k, v_w_out):
    p = dict(x=x, norm_g=norm_g, w_in=w_in, ret_norm_g=ret_norm_g, gla_wa2_f=gla_wa2_f, gla_ba_f=gla_ba_f,
             gla_wa2_b=gla_wa2_b, gla_ba_b=gla_ba_b, gla_norm_g=gla_norm_g, pool_w=pool_w, pool_scale=pool_scale,
             mla_q_norm_g=mla_q_norm_g, mla_wq_b=mla_wq_b, mla_kv_norm_g=mla_kv_norm_g, mla_wkv_b=mla_wkv_b,
             mla_qk_norm_q=mla_qk_norm_q, mla_qk_norm_k=mla_qk_norm_k, w_out=w_out, loss_target=loss_target)
    moments = dict(
        m=dict(norm_g=m_norm_g, w_in=m_w_in, ret_norm_g=m_ret_norm_g, gla_wa2_f=m_gla_wa2_f, gla_ba_f=m_gla_ba_f,
               gla_wa2_b=m_gla_wa2_b, gla_ba_b=m_gla_ba_b, gla_norm_g=m_gla_norm_g, pool_w=m_pool_w,
               pool_scale=m_pool_scale, mla_q_norm_g=m_mla_q_norm_g, mla_wq_b=m_mla_wq_b,
               mla_kv_norm_g=m_mla_kv_norm_g, mla_wkv_b=m_mla_wkv_b, mla_qk_norm_q=m_mla_qk_norm_q,
               mla_qk_norm_k=m_mla_qk_norm_k, w_out=m_w_out),
        v=dict(norm_g=v_norm_g, w_in=v_w_in, ret_norm_g=v_ret_norm_g, gla_wa2_f=v_gla_wa2_f, gla_ba_f=v_gla_ba_f,
               gla_wa2_b=v_gla_wa2_b, gla_ba_b=v_gla_ba_b, gla_norm_g=v_gla_norm_g, pool_w=v_pool_w,
               pool_scale=v_pool_scale, mla_q_norm_g=v_mla_q_norm_g, mla_wq_b=v_mla_wq_b,
               mla_kv_norm_g=v_mla_kv_norm_g, mla_wkv_b=v_mla_wkv_b, mla_qk_norm_q=v_mla_qk_norm_q,
               mla_qk_norm_k=v_mla_qk_norm_k, w_out=v_w_out))

    def as_rows(name, dtype):
        a = p[name].astype(dtype)
        return a.reshape(a.shape[0] * a.shape[1], a.shape[2])

    shards = [as_rows("w_in", BF16), as_rows("w_out", BF16), as_rows("mla_wq_b", BF16), as_rows("mla_wkv_b", BF16),
              as_rows("gla_wa2_f", F32), as_rows("gla_wa2_b", F32)]
    got = _gather_shards(shards, name="gather_weights")
    gathered = {n: a.reshape((N_CHIP, DEPTH, a.shape[1] // DEPTH, a.shape[2])) for n, a in zip(SHARDED, got)}

    grad_x, sharded, small = _local_step(p, gathered)

    where = jnp.stack([lax.axis_index("c"), 2 * lax.axis_index("x") + lax.axis_index("y")]).astype(jnp.int32)
    flat = [sharded[n].reshape(N_CHIP, -1, sharded[n].shape[-1]) for n in SHARDED]
    pair = [_pair_reduce(a, where, out_dtype=BF16, name=f"grad_pair_reduce_{n}") for n, a in zip(SHARDED, flat)]
    others = _chip_exchange(pair, name="grad_chip_exchange")
    joined = [_sum_join(a, b, where, name=f"grad_sum_join_{n}") for n, a, b in zip(SHARDED, pair, others)]
    grads = {n: a.reshape(p[n].shape) for n, a in zip(SHARDED, joined)}

    slots = _gather_all(_pack_small(small), name="gather_small")
    total = _unpack_small(_sum_slots(slots, name="sum_small"))
    for n, _ in SMALL:
        grads[n] = total[n].reshape(p[n].shape)
    loss = total["loss"]

    delta, new_m, new_v = {}, {}, {}
    for n in WEIGHTS:
        delta[n], new_m[n], new_v[n] = _adamw(p[n], grads[n], moments["m"][n], moments["v"][n], name=f"adamw_{n}")
    return (loss, grad_x, *[grads[n] for n in WEIGHTS], *[delta[n] for n in WEIGHTS],
            *[new_m[n] for n in WEIGHTS], *[new_v[n] for n in WEIGHTS])
```

```python
import functools
import math

import jax
import jax.numpy as jnp
from jax import lax
from jax.experimental import pallas as pl
from jax.experimental.pallas import tpu as pltpu

F32 = jnp.float32
BF16 = jnp.bfloat16
MESH = pl.DeviceIdType.MESH

EPS = 1e-6
ROPE_THETA = 10000.0
DEPTH = 2
N_DEV = 8
N_CHIP = 4

GROUP_W = 512
RET_HEADS = 4
RET_HD = 128
RET_CHUNK = 128
GLA_HEADS = 4
GLA_DK = 64
GLA_DV = 128
GLA_RANK = 16
GLA_TAU = 16.0
GLA_CHUNK = 64
POOL_GROUPS = 4
POOL_GW = 128
POOL_HALO = 8
POOL_TILE = 256
MLA_HEADS = 4
MLA_NOPE = 128
MLA_ROPE = 64
MLA_QK = MLA_NOPE + MLA_ROPE
MLA_QKP = 256
MLA_V = 128
MLA_Q_RANK = 512
MLA_KV_RANK = 256
MLA_SCALE = MLA_QK ** -0.5

ADAM_LR = 0.001
ADAM_B1 = 0.9
ADAM_B2 = 0.999
ADAM_EPS = 1e-08
ADAM_WD = 0.01
ADAM_STEP = 10

VMEM_LIMIT = 56 * 1024 * 1024

SEG = {
    "rq": (0, 512, 0, 512), "rk": (512, 512, 512, 512), "rv": (1024, 512, 1024, 512), "rg": (1536, 512, 1536, 512),
    "gv": (2048, 512, 2560, 512), "gg": (2560, 512, 3072, 512),
    "pv": (3072, 512, 3616, 512), "pg": (3584, 512, 4128, 512),
    "mq": (4096, 512, 4640, 512), "mg": (4608, 512, 5472, 512),
    "gq": (5120, 256, 2048, 256), "gk": (5376, 256, 2304, 256), "mkv": (5632, 256, 5152, 256),
    "ga": (5888, 128, 3584, 32), "mkr": (6016, 128, 5408, 64),
}
SEG_ORDER = ["rq", "rk", "rv", "rg", "gv", "gg", "pv", "pg", "mq", "mg", "gq", "gk", "mkv", "ga", "mkr"]
IN_COLS = 5984
IN_PAD = 6144
ORIG_ORDER = ["rq", "rk", "rv", "rg", "gq", "gk", "gv", "gg", "ga", "pv", "pg", "mq", "mkv", "mkr", "mg"]


def _cparams(*sem):
    return pltpu.CompilerParams(dimension_semantics=tuple(sem), vmem_limit_bytes=VMEM_LIMIT)


def _bf(v):
    return v.astype(BF16)


def _dot(a, b, ca=1, cb=0):
    return lax.dot_general(_bf(a), _bf(b), (((ca,), (cb,)), ((), ())), preferred_element_type=F32)


def _split_dot(a01, x, ca=1, cb=0):
    hi = _bf(x)
    r1 = x - hi.astype(F32)
    mid = _bf(r1)
    lo = _bf(r1 - mid.astype(F32))
    dn = (((ca,), (cb,)), ((), ()))
    a = _bf(a01)
    return (lax.dot_general(a, hi, dn, preferred_element_type=F32)
            + lax.dot_general(a, mid, dn, preferred_element_type=F32)
            + lax.dot_general(a, lo, dn, preferred_element_type=F32))


def _sigmoid(x):
    return 1.0 / (1.0 + jnp.exp(-x))


def _silu_parts(g):
    sg = _sigmoid(g)
    return g * sg, sg * (1.0 + g * (1.0 - sg))


def _matmul(a, b, *, ta=False, tb=False, out_dtype=F32, tm=512, tn=1024, tk=None, add=None, n_outer=True, name):
    m, kdim = (a.shape[1], a.shape[0]) if ta else a.shape
    n = b.shape[0] if tb else b.shape[1]
    tm, tn = min(tm, m), min(tn, n)
    tk = kdim if tk is None else min(tk, kdim)
    assert m % tm == 0 and n % tn == 0 and kdim % tk == 0
    nk = kdim // tk
    ca, cb = (0 if ta else 1), (1 if tb else 0)

    def body(*refs):
        if add is None:
            a_ref, b_ref, o_ref = refs[:3]
            add_ref = None
        else:
            a_ref, b_ref, add_ref, o_ref = refs[:4]
        p = _dot(a_ref[...], b_ref[...], ca, cb)

        def finish(r):
            if add_ref is not None:
                r = r + add_ref[...]
            o_ref[...] = r.astype(out_dtype)

        if nk == 1:
            finish(p)
        else:
            acc = refs[-1]
            k = pl.program_id(2)

            @pl.when(k == 0)
            def _():
                acc[...] = p

            @pl.when(k > 0)
            def _():
                acc[...] += p

            @pl.when(k == nk - 1)
            def _():
                finish(acc[...])

    def ij(g0, g1):
        return (g1, g0) if n_outer else (g0, g1)

    a_spec = (pl.BlockSpec((tk, tm), lambda g0, g1, k: (k, ij(g0, g1)[0])) if ta
              else pl.BlockSpec((tm, tk), lambda g0, g1, k: (ij(g0, g1)[0], k)))
    b_spec = (pl.BlockSpec((tn, tk), lambda g0, g1, k: (ij(g0, g1)[1], k)) if tb
              else pl.BlockSpec((tk, tn), lambda g0, g1, k: (k, ij(g0, g1)[1])))
    o_spec = pl.BlockSpec((tm, tn), lambda g0, g1, k: ij(g0, g1))
    in_specs = [a_spec, b_spec] + ([o_spec] if add is not None else [])
    args = (a, b) + ((add,) if add is not None else ())
    grid = (n // tn, m // tm, nk) if n_outer else (m // tm, n // tn, nk)
    return pl.pallas_call(
        body, name=name, grid=grid, in_specs=in_specs, out_specs=o_spec,
        out_shape=jax.ShapeDtypeStruct((m, n), out_dtype),
        scratch_shapes=[] if nk == 1 else [pltpu.VMEM((tm, tn), F32)],
        compiler_params=_cparams("parallel", "parallel", "arbitrary"),
    )(*args)


def _rmsnorm_fwd(x, g, *, name, tm=256):
    s, d = x.shape
    tm = min(tm, s)

    def body(x_ref, g_ref, h_ref):
        xv = x_ref[...]
        r = lax.rsqrt(jnp.mean(xv * xv, axis=-1, keepdims=True) + EPS)
        h_ref[...] = _bf(xv * r * g_ref[...])

    return pl.pallas_call(
        body, name=name, grid=(s // tm,),
        in_specs=[pl.BlockSpec((tm, d), lambda i: (i, 0)), pl.BlockSpec((1, d), lambda i: (0, 0))],
        out_specs=pl.BlockSpec((tm, d), lambda i: (i, 0)),
        out_shape=jax.ShapeDtypeStruct((s, d), BF16),
        compiler_params=_cparams("parallel"),
    )(x, g)


def _rmsnorm_bwd(x, dh, g, dres, *, name, tm=256):
    s, d = x.shape
    tm = min(tm, s)

    def body(x_ref, dh_ref, g_ref, dres_ref, dx_ref, dg_ref):
        i = pl.program_id(0)
        xv = x_ref[...]
        r = lax.rsqrt(jnp.mean(xv * xv, axis=-1, keepdims=True) + EPS)
        xn = xv * r
        dv = dh_ref[...]
        part = jnp.sum(dv * xn, axis=0, keepdims=True)

        @pl.when(i == 0)
        def _():
            dg_ref[...] = part

        @pl.when(i > 0)
        def _():
            dg_ref[...] += part

        dxn = dv * g_ref[...]
        dx_ref[...] = dres_ref[...] + r * (dxn - xn * jnp.mean(dxn * xn, axis=-1, keepdims=True))

    row = pl.BlockSpec((tm, d), lambda i: (i, 0))
    vec = pl.BlockSpec((1, d), lambda i: (0, 0))
    return pl.pallas_call(
        body, name=name, grid=(s // tm,), in_specs=[row, row, vec, row], out_specs=[row, vec],
        out_shape=[jax.ShapeDtypeStruct((s, d), F32), jax.ShapeDtypeStruct((1, d), F32)],
        compiler_params=_cparams("arbitrary"),
    )(x, dh, g, dres)


def _loss_head(xf, target, *, name, tm=256):
    s, d = xf.shape
    tm = min(tm, s)

    def body(x_ref, t_ref, dx_ref, l_ref):
        i = pl.program_id(0)
        e = x_ref[...] - t_ref[...]
        dx_ref[...] = e * (1.0 / d)
        rows = jnp.mean(e * e, axis=-1, keepdims=True)
        part = 0.5 * jnp.sum(rows, axis=0, keepdims=True)

        @pl.when(i == 0)
        def _():
            l_ref[...] = part

        @pl.when(i > 0)
        def _():
            l_ref[...] += part

    row = pl.BlockSpec((tm, d), lambda i: (i, 0))
    return pl.pallas_call(
        body, name=name, grid=(s // tm,), in_specs=[row, row],
        out_specs=[row, pl.BlockSpec((1, 1), lambda i: (0, 0))],
        out_shape=[jax.ShapeDtypeStruct((s, d), F32), jax.ShapeDtypeStruct((1, 1), F32)],
        compiler_params=_cparams("arbitrary"),
    )(xf, target)


def _rope_tables(s):
    pos = jnp.arange(s, dtype=F32)[:, None]
    inv_r = 1.0 / (ROPE_THETA ** (jnp.arange(0, RET_HD, 2, dtype=F32) / RET_HD))
    ang = pos * inv_r[None, :]
    ret_cos = jnp.concatenate([jnp.cos(ang), jnp.cos(ang)], axis=1)
    ret_sin = jnp.concatenate([-jnp.sin(ang), jnp.sin(ang)], axis=1)
    inv_m = 1.0 / (ROPE_THETA ** (jnp.arange(0, MLA_ROPE, 2, dtype=F32) / MLA_ROPE))
    am = pos * inv_m[None, :]
    z32, z64 = jnp.zeros((s, 32), F32), jnp.zeros((s, 64), F32)
    mla_cos = jnp.concatenate([jnp.cos(am), jnp.cos(am), z64], axis=1)
    mla_sp = jnp.concatenate([z32, jnp.sin(am), z64], axis=1)
    mla_sn = jnp.concatenate([-jnp.sin(am), z32, z64], axis=1)
    return ret_cos, ret_sin, mla_cos, mla_sp, mla_sn


def _rope128(x, c, sg):
    return x * c + pltpu.roll(x, 64, 1) * sg


def _unrope128(d, c, sg):
    return d * c + pltpu.roll(d * sg, 64, 1)


def _rope64(t, c, sp, sn):
    return t * c + pltpu.roll(t, 96, 1) * sn + pltpu.roll(t, 32, 1) * sp


def _unrope64(d, c, sp, sn):
    return d * c + pltpu.roll(d * sn, 32, 1) + pltpu.roll(d * sp, 96, 1)


def _ret_pre(z, cos, sin, *, name, tm=256):
    s = z.shape[0]
    tm = min(tm, s)
    scale = RET_HD ** -0.5

    def body(q_ref, k_ref, c_ref, s_ref, qo_ref, ko_ref):
        c, sg = c_ref[...], s_ref[...]
        for h in range(RET_HEADS):
            sl = slice(h * RET_HD, (h + 1) * RET_HD)
            qo_ref[:, sl] = _rope128(q_ref[:, sl], c, sg)
            ko_ref[:, sl] = _rope128(k_ref[:, sl], c, sg) * scale

    seg = lambda j: pl.BlockSpec((tm, GROUP_W), lambda i: (i, j))
    tab = pl.BlockSpec((tm, RET_HD), lambda i: (i, 0))
    return pl.pallas_call(
        body, name=name, grid=(s // tm,), in_specs=[seg(0), seg(1), tab, tab],
        out_specs=[seg(0), seg(0)],
        out_shape=[jax.ShapeDtypeStruct((s, GROUP_W), F32)] * 2,
        compiler_params=_cparams("parallel"),
    )(z, z, cos, sin)


def _ret_pre_bwd(dqr, dkr, cos, sin, *, name, tm=256):
    s = dqr[0].shape[0]
    tm = min(tm, s)
    scale = RET_HD ** -0.5

    def body(dq0_ref, dq1_ref, dk0_ref, dk1_ref, c_ref, s_ref, qo_ref, ko_ref):
        c, sg = c_ref[...], s_ref[...]
        for h in range(RET_HEADS):
            sl = slice(h * RET_HD, (h + 1) * RET_HD)
            qo_ref[:, sl] = _bf(_unrope128(dq0_ref[:, sl] + dq1_ref[:, sl], c, sg))
            ko_ref[:, sl] = _bf(_unrope128(dk0_ref[:, sl] + dk1_ref[:, sl], c, sg) * scale)

    row = pl.BlockSpec((tm, GROUP_W), lambda i: (i, 0))
    tab = pl.BlockSpec((tm, RET_HD), lambda i: (i, 0))
    return pl.pallas_call(
        body, name=name, grid=(s // tm,), in_specs=[row, row, row, row, tab, tab], out_specs=[row, row],
        out_shape=[jax.ShapeDtypeStruct((s, GROUP_W), BF16)] * 2,
        compiler_params=_cparams("parallel"),
    )(dqr[0], dqr[1], dkr[0], dkr[1], cos, sin)


def _bla(a, b, c, lg, cols, *, name):
    s = a.shape[0]
    ch = min(RET_CHUNK, s)
    n = s // ch
    hd = RET_HD

    def body(lg_ref, a0, b0, c0, a1, b1, c1, o0, o1, st):
        t = pl.program_id(0)

        @pl.when(t == 0)
        def _():
            st[...] = jnp.zeros_like(st)

        ii = lax.broadcasted_iota(jnp.int32, (ch, ch), 0)
        jj = lax.broadcasted_iota(jnp.int32, (ch, ch), 1)
        idx = lax.broadcasted_iota(jnp.int32, (ch, 1), 0).astype(F32)
        for d, (a_ref, b_ref, c_ref, o_ref) in enumerate(((a0, b0, c0, o0), (a1, b1, c1, o1))):
            diff = ((ii - jj) if d == 0 else (jj - ii)).astype(F32)
            keep = diff >= 0
            dpos = jnp.maximum(diff, 0.0)
            pq = (idx + 1.0) if d == 0 else (ch - idx)
            pk = (ch - 1.0 - idx) if d == 0 else idx
            for h in range(RET_HEADS):
                g = lg_ref[d, h]
                sl = slice(h * hd, (h + 1) * hd)
                av, bv, cv = a_ref[:, sl], b_ref[:, sl], c_ref[:, sl]
                sc = _dot(av, bv, 1, 1) * jnp.where(keep, jnp.exp(dpos * g), 0.0)
                stv = st[d, h]
                o_ref[:, sl] = _dot(sc, cv) + _dot(av * jnp.exp(pq * g), stv)
                st[d, h] = jnp.exp(ch * g) * stv + _dot(bv * jnp.exp(pk * g), cv, 0, 0)

    fwd = lambda j: pl.BlockSpec((ch, GROUP_W), lambda t: (t, j))
    bwd = lambda j: pl.BlockSpec((ch, GROUP_W), lambda t: (n - 1 - t, j))
    return pl.pallas_call(
        body, name=name, grid=(n,),
        in_specs=[pl.BlockSpec(memory_space=pltpu.SMEM), fwd(cols[0]), fwd(cols[1]), fwd(cols[2]),
                  bwd(cols[0]), bwd(cols[1]), bwd(cols[2])],
        out_specs=[fwd(0), bwd(0)],
        out_shape=[jax.ShapeDtypeStruct((s, GROUP_W), F32)] * 2,
        scratch_shapes=[pltpu.VMEM((2, RET_HEADS, hd, hd), F32)],
        compiler_params=_cparams("arbitrary"),
    )(lg, a, b, c, a, b, c)


def _post(os_, zg, gcol, g, *, norm, name, tm=256):
    s = zg.shape[0]
    tm = min(tm, s)
    nd = len(os_)

    def body(*refs):
        o_refs, (gt_ref, g_ref, y_ref) = refs[:nd], refs[nd:]
        silu, _ = _silu_parts(gt_ref[...])
        for h in range(4):
            sl = slice(h * 128, (h + 1) * 128)
            o = o_refs[0][:, sl]
            for k in range(1, nd):
                o = o + o_refs[k][:, sl]
            if norm:
                r = lax.rsqrt(jnp.mean(o * o, axis=-1, keepdims=True) + EPS)
                o = o * r * g_ref[:, sl]
            y_ref[:, sl] = _bf(silu[:, sl] * o)

    row = pl.BlockSpec((tm, GROUP_W), lambda i: (i, 0))
    return pl.pallas_call(
        body, name=name, grid=(s // tm,),
        in_specs=[row] * nd + [pl.BlockSpec((tm, GROUP_W), lambda i: (i, gcol)),
                               pl.BlockSpec((1, GROUP_W), lambda i: (0, 0))],
        out_specs=row,
        out_shape=jax.ShapeDtypeStruct((s, GROUP_W), BF16),
        compiler_params=_cparams("parallel"),
    )(*os_, zg, g)


def _post_bwd(dy, ycol, os_, zg, gcol, g, *, norm, name, tm=256):
    s = zg.shape[0]
    tm = min(tm, s)
    nd = len(os_)

    def body(*refs):
        dy_ref, o_refs = refs[0], refs[1:1 + nd]
        gt_ref, g_ref, dgt_ref, do_ref, dg_ref = refs[1 + nd:]
        i = pl.program_id(0)
        silu, dsilu = _silu_parts(gt_ref[...])
        dyv = dy_ref[...]
        parts = []
        for h in range(4):
            sl = slice(h * 128, (h + 1) * 128)
            o = o_refs[0][:, sl]
            for k in range(1, nd):
                o = o + o_refs[k][:, sl]
            dn = dyv[:, sl] * silu[:, sl]
            if norm:
                r = lax.rsqrt(jnp.mean(o * o, axis=-1, keepdims=True) + EPS)
                xn = o * r
                gh = g_ref[:, sl]
                dgt_ref[:, sl] = _bf(dyv[:, sl] * (xn * gh) * dsilu[:, sl])
                parts.append(jnp.sum(dn * xn, axis=0, keepdims=True))
                dxn = dn * gh
                do_ref[:, sl] = r * (dxn - xn * jnp.mean(dxn * xn, axis=-1, keepdims=True))
            else:
                dgt_ref[:, sl] = _bf(dyv[:, sl] * o * dsilu[:, sl])
                parts.append(jnp.zeros((1, 128), F32))
                do_ref[:, sl] = dn
        part = jnp.concatenate(parts, axis=1)

        @pl.when(i == 0)
        def _():
            dg_ref[...] = part

        @pl.when(i > 0)
        def _():
            dg_ref[...] += part

    row = pl.BlockSpec((tm, GROUP_W), lambda i: (i, 0))
    vec = pl.BlockSpec((1, GROUP_W), lambda i: (0, 0))
    return pl.pallas_call(
        body, name=name, grid=(s // tm,),
        in_specs=[pl.BlockSpec((tm, GROUP_W), lambda i: (i, ycol))] + [row] * nd
        + [pl.BlockSpec((tm, GROUP_W), lambda i: (i, gcol)), vec],
        out_specs=[row, row, vec],
        out_shape=[jax.ShapeDtypeStruct((s, GROUP_W), BF16), jax.ShapeDtypeStruct((s, GROUP_W), F32),
                   jax.ShapeDtypeStruct((1, GROUP_W), F32)],
        compiler_params=_cparams("arbitrary"),
    )(dy, *os_, zg, g)


def _ret_log_gamma(swap):
    gf = 1.0 - 2.0 ** (-5.0 - jnp.arange(RET_HEADS, dtype=F32))
    lf, lb = jnp.log(gf), jnp.log(gf[::-1])
    return jnp.stack([lb, lf] if swap else [lf, lb])


def _log_sigmoid(x):
    return jnp.minimum(x, 0.0) - jnp.log(1.0 + jnp.exp(-jnp.abs(x)))


def _gla_gate(z, wa, ba, *, name, tm=256):
    s = z.shape[0]
    tm = min(tm, s)
    col = SEG["ga"][0] // 128

    def body(ga_ref, wa_ref, ba_ref, la_ref):
        pre = _dot(ga_ref[...], wa_ref[...]) + ba_ref[...]
        la_ref[...] = _log_sigmoid(pre) / GLA_TAU

    return pl.pallas_call(
        body, name=name, grid=(s // tm,),
        in_specs=[pl.BlockSpec((tm, 128), lambda i: (i, col)), pl.BlockSpec((128, 512), lambda i: (0, 0)),
                  pl.BlockSpec((1, 512), lambda i: (0, 0))],
        out_specs=pl.BlockSpec((tm, 512), lambda i: (i, 0)),
        out_shape=jax.ShapeDtypeStruct((s, 512), F32),
        compiler_params=_cparams("parallel"),
    )(z, wa, ba)


def _gla_gate_bwd(dla, z, wa, ba, *, name, tm=256):
    s = z.shape[0]
    tm = min(tm, s)
    col = SEG["ga"][0] // 128

    def body(dla_ref, ga_ref, wa_ref, ba_ref, dga_ref, dwa_ref, dba_ref):
        i = pl.program_id(0)
        gav = ga_ref[...]
        pre = _dot(gav, wa_ref[...]) + ba_ref[...]
        dpre = dla_ref[...] * (1.0 - _sigmoid(pre)) * (1.0 / GLA_TAU)
        dga_ref[...] = _bf(_dot(dpre, wa_ref[...], 1, 1))
        pw = _dot(gav, dpre, 0, 0)
        pb = jnp.sum(dpre, axis=0, keepdims=True)

        @pl.when(i == 0)
        def _():
            dwa_ref[...] = pw
            dba_ref[...] = pb

        @pl.when(i > 0)
        def _():
            dwa_ref[...] += pw
            dba_ref[...] += pb

    return pl.pallas_call(
        body, name=name, grid=(s // tm,),
        in_specs=[pl.BlockSpec((tm, 512), lambda i: (i, 0)), pl.BlockSpec((tm, 128), lambda i: (i, col)),
                  pl.BlockSpec((128, 512), lambda i: (0, 0)), pl.BlockSpec((1, 512), lambda i: (0, 0))],
        out_specs=[pl.BlockSpec((tm, 128), lambda i: (i, 0)), pl.BlockSpec((128, 512), lambda i: (0, 0)),
                   pl.BlockSpec((1, 512), lambda i: (0, 0))],
        out_shape=[jax.ShapeDtypeStruct((s, 128), BF16), jax.ShapeDtypeStruct((128, 512), F32),
                   jax.ShapeDtypeStruct((1, 512), F32)],
        compiler_params=_cparams("arbitrary"),
    )(dla, z, wa, ba)


def _gla_masks(ch):
    ii = lax.broadcasted_iota(jnp.int32, (ch, ch), 0)
    tt = lax.broadcasted_iota(jnp.int32, (ch, ch), 1)
    return jnp.where(tt <= ii, 1.0, 0.0), jnp.where(tt >= ii, 1.0, 0.0)


def _gla_chunk(d, tmat, qv, kv, lav, ch):
    c = _split_dot(tmat, lav)
    big_l = c[ch - 1:ch, :] if d == 0 else c[0:1, :]
    qt = qv * (GLA_DK ** -0.5) * jnp.exp(c)
    kt = kv * jnp.exp(-c)
    kh = kv * jnp.exp(big_l - c)
    return c, big_l, qt, kt, kh


def _gla_fwd(qh, kh_, z, la, *, name):
    s = z.shape[0]
    ch = min(GLA_CHUNK, s)
    n = s // ch
    vcol = SEG["gv"][0] // GROUP_W

    def body(q0, k0, v0, la0, q1, k1, v1, la1, o0, o1, zs0, zs1, st):
        t = pl.program_id(0)

        @pl.when(t == 0)
        def _():
            st[...] = jnp.zeros_like(st)

        masks = _gla_masks(ch)
        for d, (q_ref, k_ref, v_ref, la_ref, o_ref, zs_ref) in enumerate(
                ((q0, k0, v0, la0, o0, zs0), (q1, k1, v1, la1, o1, zs1))):
            for h in range(GLA_HEADS):
                c, big_l, qt, kt, kh = _gla_chunk(d, masks[d], q_ref[h], k_ref[h], la_ref[0, h], ch)
                vv = v_ref[:, h * GLA_DV:(h + 1) * GLA_DV]
                p = _dot(qt, kt, 1, 1) * masks[d]
                zst = st[d, h]
                o_ref[:, h * GLA_DV:(h + 1) * GLA_DV] = _dot(p, vv) + _dot(qt, zst, 1, 1)
                zs_ref[h, 0] = zst
                st[d, h] = zst * jnp.exp(big_l) + _dot(vv, kh, 0, 0)

    cidx = (lambda t: t), (lambda t: n - 1 - t)
    hs = lambda d: pl.BlockSpec((GLA_HEADS, ch, GLA_DK), lambda t: (0, cidx[d](t), 0))
    vs = lambda d: pl.BlockSpec((ch, GROUP_W), lambda t: (cidx[d](t), vcol))
    las = lambda d: pl.BlockSpec((1, GLA_HEADS, ch, GLA_DK), lambda t: (d, 0, cidx[d](t), 0))
    os_ = lambda d: pl.BlockSpec((ch, GROUP_W), lambda t: (cidx[d](t), 0))
    zss = lambda d: pl.BlockSpec((GLA_HEADS, 1, GLA_DV, GLA_DK), lambda t: (0, cidx[d](t), 0, 0))
    o0, o1, zs0, zs1 = pl.pallas_call(
        body, name=name, grid=(n,),
        in_specs=[hs(0), hs(0), vs(0), las(0), hs(1), hs(1), vs(1), las(1)],
        out_specs=[os_(0), os_(1), zss(0), zss(1)],
        out_shape=[jax.ShapeDtypeStruct((s, GROUP_W), F32)] * 2
        + [jax.ShapeDtypeStruct((GLA_HEADS, n, GLA_DV, GLA_DK), F32)] * 2,
        scratch_shapes=[pltpu.VMEM((2, GLA_HEADS, GLA_DV, GLA_DK), F32)],
        compiler_params=_cparams("arbitrary"),
    )(qh, kh_, z, la, qh, kh_, z, la)
    return (o0, o1), (zs0, zs1)


def _gla_bwd(qh, kh_, z, la, do, zs, *, name):
    s = z.shape[0]
    ch = min(GLA_CHUNK, s)
    n = s // ch
    vcol = SEG["gv"][0] // GROUP_W

    def body(q0, k0, v0, la0, do0, zs0, q1, k1, v1, la1, do1, zs1,
             dq0, dk0, dla0, dv0, dq1, dk1, dla1, dv1, gz):
        t = pl.program_id(0)

        @pl.when(t == 0)
        def _():
            gz[...] = jnp.zeros_like(gz)

        masks = _gla_masks(ch)
        rows = lax.broadcasted_iota(jnp.int32, (ch, 1), 0)
        for d, (q_ref, k_ref, v_ref, la_ref, do_ref, zs_ref, dq_ref, dk_ref, dla_ref, dv_ref) in enumerate(
                ((q0, k0, v0, la0, do0, zs0, dq0, dk0, dla0, dv0), (q1, k1, v1, la1, do1, zs1, dq1, dk1, dla1, dv1))):
            tmat = masks[d]
            end = ch - 1 if d == 0 else 0
            for h in range(GLA_HEADS):
                c, big_l, qt, kt, kh = _gla_chunk(d, tmat, q_ref[h], k_ref[h], la_ref[0, h], ch)
                vsl = slice(h * GLA_DV, (h + 1) * GLA_DV)
                vv, dov, zst, gzv = v_ref[:, vsl], do_ref[:, vsl], zs_ref[h, 0], gz[d, h]
                p = _dot(qt, kt, 1, 1) * tmat
                dp = _dot(dov, vv, 1, 1) * tmat
                dqt = _dot(dp, kt) + _dot(dov, zst)
                dkt = _dot(dp, qt, 0, 0)
                dkh = _dot(vv, gzv)
                dv_ref[:, vsl] = _dot(p, dov, 0, 0) + _dot(kh, gzv, 1, 1)
                dq_ref[h] = dqt * jnp.exp(c) * (GLA_DK ** -0.5)
                dk_ref[h] = dkt * jnp.exp(-c) + dkh * jnp.exp(big_l - c)
                e_l = jnp.exp(big_l)
                d_l = jnp.sum(dkh * kh, axis=0, keepdims=True) + e_l * jnp.sum(zst * gzv, axis=0, keepdims=True)
                dc = dqt * qt - dkt * kt - dkh * kh + jnp.where(rows == end, d_l, 0.0)
                dla_ref[h] = _split_dot(tmat, dc, 0, 0)
                gz[d, h] = gzv * e_l + _dot(dov, qt, 0, 0)

    cidx = (lambda t: n - 1 - t), (lambda t: t)
    hs = lambda d: pl.BlockSpec((GLA_HEADS, ch, GLA_DK), lambda t: (0, cidx[d](t), 0))
    vs = lambda d: pl.BlockSpec((ch, GROUP_W), lambda t: (cidx[d](t), vcol))
    las = lambda d: pl.BlockSpec((1, GLA_HEADS, ch, GLA_DK), lambda t: (d, 0, cidx[d](t), 0))
    row = lambda d: pl.BlockSpec((ch, GROUP_W), lambda t: (cidx[d](t), 0))
    zss = lambda d: pl.BlockSpec((GLA_HEADS, 1, GLA_DV, GLA_DK), lambda t: (0, cidx[d](t), 0, 0))
    hshape = jax.ShapeDtypeStruct((GLA_HEADS, s, GLA_DK), F32)
    wide = jax.ShapeDtypeStruct((s, GROUP_W), F32)
    outs = pl.pallas_call(
        body, name=name, grid=(n,),
        in_specs=[hs(0), hs(0), vs(0), las(0), row(0), zss(0), hs(1), hs(1), vs(1), las(1), row(1), zss(1)],
        out_specs=[hs(0), hs(0), hs(0), row(0), hs(1), hs(1), hs(1), row(1)],
        out_shape=[hshape, hshape, hshape, wide, hshape, hshape, hshape, wide],
        scratch_shapes=[pltpu.VMEM((2, GLA_HEADS, GLA_DV, GLA_DK), F32)],
        compiler_params=_cparams("arbitrary"),
    )(qh, kh_, z, la, do, zs[0], qh, kh_, z, la, do, zs[1])
    dq0, dk0, dla0, dv0, dq1, dk1, dla1, dv1 = outs
    return (dq0, dq1), (dk0, dk1), (dla0, dla1), (dv0, dv1)


def _band(lo, hi, rows, width):
    r = lax.broadcasted_iota(jnp.int32, (rows, width), 0)
    j = lax.broadcasted_iota(jnp.int32, (rows, width), 1)
    k = j - POOL_HALO - r
    return jnp.where((k >= lo) & (k <= hi), 1.0, 0.0)


def _pool_cnt(t0, half, rows, s):
    t = t0 + lax.broadcasted_iota(jnp.int32, (rows, 1), 0)
    return (jnp.minimum(t + half, s) - jnp.maximum(t - half, 0)).astype(F32)


def _pool_fwd(z, pw, scale, *, name):
    s = z.shape[0]
    tl = min(POOL_TILE, s)
    nt = s // tl
    ucol, gcol = SEG["pv"][0] // 128, SEG["pg"][0] // 128

    def body(u_ref, gt_ref, pw_ref, sc_ref, y_ref, pad):
        g = pl.program_id(0)
        half = jnp.left_shift(1, g)
        pad[0:POOL_HALO, :] = jnp.zeros((POOL_HALO, POOL_GW), F32)
        pad[POOL_HALO + s:POOL_HALO + s + POOL_HALO, :] = jnp.zeros((POOL_HALO, POOL_GW), F32)
        pad[POOL_HALO:POOL_HALO + s, :] = u_ref[...]
        band = _band(-half, half - 1, tl, tl + 2 * POOL_HALO)
        pwv, scv = pw_ref[0], sc_ref[...]

        def tile(i, carry):
            t0 = pl.multiple_of(i * tl, tl)
            win = pad[pl.ds(t0, tl + 2 * POOL_HALO), :]
            u = win[POOL_HALO:POOL_HALO + tl, :]
            pooled = _split_dot(band, win) / _pool_cnt(t0, half, tl, s) - u
            mixed = _dot(pooled, pwv)
            silu, _ = _silu_parts(gt_ref[pl.ds(t0, tl), :])
            y_ref[pl.ds(t0, tl), :] = _bf(silu * (mixed * scv))
            return carry

        lax.fori_loop(0, nt, tile, 0)

    return pl.pallas_call(
        body, name=name, grid=(POOL_GROUPS,),
        in_specs=[pl.BlockSpec((s, POOL_GW), lambda g: (0, ucol + g)),
                  pl.BlockSpec((s, POOL_GW), lambda g: (0, gcol + g)),
                  pl.BlockSpec((1, POOL_GW, POOL_GW), lambda g: (g, 0, 0)),
                  pl.BlockSpec((1, POOL_GW), lambda g: (0, g))],
        out_specs=pl.BlockSpec((s, POOL_GW), lambda g: (0, g)),
        out_shape=jax.ShapeDtypeStruct((s, GROUP_W), BF16),
        scratch_shapes=[pltpu.VMEM((s + 2 * POOL_HALO, POOL_GW), F32)],
        compiler_params=_cparams("parallel"),
    )(z, z, pw, scale)


def _pool_bwd(dy, z, pw, scale, *, name):
    s = z.shape[0]
    tl = min(POOL_TILE, s)
    nt = s // tl
    ucol, gcol, ycol = SEG["pv"][0] // 128, SEG["pg"][0] // 128, 2 * GROUP_W // 128

    def body(dy_ref, u_ref, gt_ref, pw_ref, sc_ref, du_ref, dgt_ref, dpw_ref, dsc_ref, pad, epad, dpo):
        g = pl.program_id(0)
        half = jnp.left_shift(1, g)
        zeros = jnp.zeros((POOL_HALO, POOL_GW), F32)
        for buf in (pad, epad):
            buf[0:POOL_HALO, :] = zeros
            buf[POOL_HALO + s:POOL_HALO + s + POOL_HALO, :] = zeros
        pad[POOL_HALO:POOL_HALO + s, :] = u_ref[...]
        band = _band(-half, half - 1, tl, tl + 2 * POOL_HALO)
        band_t = _band(1 - half, half, tl, tl + 2 * POOL_HALO)
        pwv, scv = pw_ref[0], sc_ref[...]
        dpw_ref[0] = jnp.zeros((POOL_GW, POOL_GW), F32)
        dsc_ref[...] = jnp.zeros((1, POOL_GW), F32)

        def tile(i, carry):
            t0 = pl.multiple_of(i * tl, tl)
            win = pad[pl.ds(t0, tl + 2 * POOL_HALO), :]
            u = win[POOL_HALO:POOL_HALO + tl, :]
            cnt = _pool_cnt(t0, half, tl, s)
            pooled = _split_dot(band, win) / cnt - u
            mixed = _dot(pooled, pwv)
            silu, dsilu = _silu_parts(gt_ref[pl.ds(t0, tl), :])
            dyv = dy_ref[pl.ds(t0, tl), :]
            dgt_ref[pl.ds(t0, tl), :] = _bf(dyv * (mixed * scv) * dsilu)
            dsc_ref[...] += jnp.sum(dyv * silu * mixed, axis=0, keepdims=True)
            dm = dyv * silu * scv
            dpw_ref[0] += _dot(pooled, dm, 0, 0)
            dpooled = _dot(dm, pwv, 1, 1)
            dpo[pl.ds(t0, tl), :] = dpooled
            epad[pl.ds(POOL_HALO + t0, tl), :] = dpooled / cnt
            return carry

        lax.fori_loop(0, nt, tile, 0)

        def tile2(i, carry):
            t0 = pl.multiple_of(i * tl, tl)
            ewin = epad[pl.ds(t0, tl + 2 * POOL_HALO), :]
            du_ref[pl.ds(t0, tl), :] = _bf(_split_dot(band_t, ewin) - dpo[pl.ds(t0, tl), :])
            return carry

        lax.fori_loop(0, nt, tile2, 0)

    col = lambda c0: pl.BlockSpec((s, POOL_GW), lambda g: (0, c0 + g))
    return pl.pallas_call(
        body, name=name, grid=(POOL_GROUPS,),
        in_specs=[col(ycol), col(ucol), col(gcol), pl.BlockSpec((1, POOL_GW, POOL_GW), lambda g: (g, 0, 0)),
                  pl.BlockSpec((1, POOL_GW), lambda g: (0, g))],
        out_specs=[col(0), col(0), pl.BlockSpec((1, POOL_GW, POOL_GW), lambda g: (g, 0, 0)),
                   pl.BlockSpec((1, POOL_GW), lambda g: (0, g))],
        out_shape=[jax.ShapeDtypeStruct((s, GROUP_W), BF16), jax.ShapeDtypeStruct((s, GROUP_W), BF16),
                   jax.ShapeDtypeStruct((POOL_GROUPS, POOL_GW, POOL_GW), F32),
                   jax.ShapeDtypeStruct((1, GROUP_W), F32)],
        scratch_shapes=[pltpu.VMEM((s + 2 * POOL_HALO, POOL_GW), F32), pltpu.VMEM((s + 2 * POOL_HALO, POOL_GW), F32),
                        pltpu.VMEM((s, POOL_GW), F32)],
        compiler_params=_cparams("parallel"),
    )(dy, z, z, pw, scale)


def _mla_specs(tm):
    zq = pl.BlockSpec((tm, 512), lambda i: (i, SEG["mq"][0] // 512))
    zkv = pl.BlockSpec((tm, 256), lambda i: (i, SEG["mkv"][0] // 256))
    zkr = pl.BlockSpec((tm, 128), lambda i: (i, SEG["mkr"][0] // 128))
    full = lambda r, c: pl.BlockSpec((r, c), lambda i: (0, 0))
    tab = pl.BlockSpec((tm, 128), lambda i: (i, 0))
    weights = [full(1, 512), full(512, 1024), full(1, 256), full(256, 1024), full(1, 256), full(1, 256)]
    return [zq, zkv, zkr] + weights + [tab, tab, tab]


def _mla_project(xq_ref, xkv_ref, qg_ref, wq_ref, kvg_ref, wkv_ref):
    xq = xq_ref[...]
    r1 = lax.rsqrt(jnp.mean(xq * xq, axis=-1, keepdims=True) + EPS)
    xn1 = xq * r1
    qn = _bf(xn1 * qg_ref[...])
    qraw = _dot(qn, wq_ref[...])
    xkv = xkv_ref[...]
    r2 = lax.rsqrt(jnp.mean(xkv * xkv, axis=-1, keepdims=True) + EPS)
    xn2 = xkv * r2
    kvn = _bf(xn2 * kvg_ref[...])
    kvraw = _dot(kvn, wkv_ref[...])
    return r1, xn1, qn, qraw, r2, xn2, kvn, kvraw


def _mla_pre(z, qg, wq, kvg, wkv, qng, kng, cos, sp, sn, *, name, tm=256):
    s = z.shape[0]
    tm = min(tm, s)

    def body(xq_ref, xkv_ref, pe_ref, qg_ref, wq_ref, kvg_ref, wkv_ref, qng_ref, kng_ref, c_ref, sp_ref, sn_ref,
             q_ref, k_ref, v_ref):
        _, _, _, qraw, _, _, _, kvraw = _mla_project(xq_ref, xkv_ref, qg_ref, wq_ref, kvg_ref, wkv_ref)
        c, spv, snv = c_ref[...], sp_ref[...], sn_ref[...]
        pe = pe_ref[...]
        pe_ss = jnp.sum(pe * pe, axis=-1, keepdims=True)
        qngv, kngv = qng_ref[...], kng_ref[...]
        for h in range(MLA_HEADS):
            b = h * MLA_QKP
            qh = qraw[:, b:b + MLA_QKP]
            r = lax.rsqrt(jnp.sum(qh * qh, axis=-1, keepdims=True) * (1.0 / MLA_QK) + EPS)
            qn_h = qh * r * qngv
            q_ref[:, b:b + 128] = _bf(qn_h[:, :128] * MLA_SCALE)
            q_ref[:, b + 128:b + 256] = _bf(_rope64(qn_h[:, 128:], c, spv, snv) * MLA_SCALE)
            kn = kvraw[:, b:b + 128]
            rk = lax.rsqrt((jnp.sum(kn * kn, axis=-1, keepdims=True) + pe_ss) * (1.0 / MLA_QK) + EPS)
            k_ref[:, b:b + 128] = _bf(kn * rk * kngv[:, :128])
            k_ref[:, b + 128:b + 256] = _bf(_rope64(pe * rk * kngv[:, 128:], c, spv, snv))
            v_ref[:, h * MLA_V:(h + 1) * MLA_V] = _bf(kvraw[:, b + 128:b + 256])

    row = lambda w: pl.BlockSpec((tm, w), lambda i: (i, 0))
    return pl.pallas_call(
        body, name=name, grid=(s // tm,), in_specs=_mla_specs(tm),
        out_specs=[row(1024), row(1024), row(512)],
        out_shape=[jax.ShapeDtypeStruct((s, 1024), BF16), jax.ShapeDtypeStruct((s, 1024), BF16),
                   jax.ShapeDtypeStruct((s, 512), BF16)],
        compiler_params=_cparams("parallel"),
    )(z, z, z, qg, wq, kvg, wkv, qng, kng, cos, sp, sn)


def _mla_pre_bwd(dq, dk, dv, z, qg, wq, kvg, wkv, qng, kng, cos, sp, sn, *, name, tm=256):
    s = z.shape[0]
    tm = min(tm, s)

    def body(dq_ref, dk_ref, dv_ref, xq_ref, xkv_ref, pe_ref, qg_ref, wq_ref, kvg_ref, wkv_ref, qng_ref, kng_ref,
             c_ref, sp_ref, sn_ref, dxq_ref, dxkv_ref, dpe_ref, dwq_ref, dwkv_ref, dqg_ref, dkvg_ref, dqng_ref,
             dkng_ref, dqraw, dkvraw):
        i = pl.program_id(0)
        r1, xn1, qn, qraw, r2, xn2, kvn, kvraw = _mla_project(xq_ref, xkv_ref, qg_ref, wq_ref, kvg_ref, wkv_ref)
        c, spv, snv = c_ref[...], sp_ref[...], sn_ref[...]
        pe = pe_ref[...]
        pe_ss = jnp.sum(pe * pe, axis=-1, keepdims=True)
        qngv, kngv = qng_ref[...], kng_ref[...]
        dqng = jnp.zeros((1, MLA_QKP), F32)
        dkng = jnp.zeros((1, MLA_QKP), F32)
        dpe = jnp.zeros_like(pe)
        for h in range(MLA_HEADS):
            b = h * MLA_QKP
            qh = qraw[:, b:b + MLA_QKP]
            r = lax.rsqrt(jnp.sum(qh * qh, axis=-1, keepdims=True) * (1.0 / MLA_QK) + EPS)
            xn = qh * r
            d_n = jnp.concatenate(
                [dq_ref[:, b:b + 128], _unrope64(dq_ref[:, b + 128:b + 256], c, spv, snv)], axis=1) * MLA_SCALE
            dqng = dqng + jnp.sum(d_n * xn, axis=0, keepdims=True)
            dxn = d_n * qngv
            dqraw[:, b:b + MLA_QKP] = _bf(r * (dxn - xn * (jnp.sum(dxn * xn, axis=-1, keepdims=True) * (1.0 / MLA_QK))))
            kn = kvraw[:, b:b + 128]
            rk = lax.rsqrt((jnp.sum(kn * kn, axis=-1, keepdims=True) + pe_ss) * (1.0 / MLA_QK) + EPS)
            xk = jnp.concatenate([kn, pe], axis=1) * rk
            d_k = jnp.concatenate(
                [dk_ref[:, b:b + 128], _unrope64(dk_ref[:, b + 128:b + 256], c, spv, snv)], axis=1)
            dkng = dkng + jnp.sum(d_k * xk, axis=0, keepdims=True)
            dxk = d_k * kngv
            dfull = rk * (dxk - xk * (jnp.sum(dxk * xk, axis=-1, keepdims=True) * (1.0 / MLA_QK)))
            dkvraw[:, b:b + 128] = _bf(dfull[:, :128])
            dkvraw[:, b + 128:b + 256] = _bf(dv_ref[:, h * MLA_V:(h + 1) * MLA_V])
            dpe = dpe + dfull[:, 128:]
        dpe_ref[...] = _bf(dpe)
        dqr, dkvr = dqraw[...], dkvraw[...]
        dqn = _dot(dqr, wq_ref[...], 1, 1)
        dxn1 = dqn * qg_ref[...]
        dxq_ref[...] = _bf(r1 * (dxn1 - xn1 * jnp.mean(dxn1 * xn1, axis=-1, keepdims=True)))
        dkvn = _dot(dkvr, wkv_ref[...], 1, 1)
        dxn2 = dkvn * kvg_ref[...]
        dxkv_ref[...] = _bf(r2 * (dxn2 - xn2 * jnp.mean(dxn2 * xn2, axis=-1, keepdims=True)))
        parts = (_dot(qn, dqr, 0, 0), _dot(kvn, dkvr, 0, 0), jnp.sum(dqn * xn1, axis=0, keepdims=True),
                 jnp.sum(dkvn * xn2, axis=0, keepdims=True), dqng, dkng)
        accs = (dwq_ref, dwkv_ref, dqg_ref, dkvg_ref, dqng_ref, dkng_ref)

        @pl.when(i == 0)
        def _():
            for a, p in zip(accs, parts):
                a[...] = p

        @pl.when(i > 0)
        def _():
            for a, p in zip(accs, parts):
                a[...] += p

    row = lambda w: pl.BlockSpec((tm, w), lambda i: (i, 0))
    full = lambda r, c: pl.BlockSpec((r, c), lambda i: (0, 0))
    return pl.pallas_call(
        body, name=name, grid=(s // tm,),
        in_specs=[row(1024), row(1024), row(512)] + _mla_specs(tm),
        out_specs=[row(512), row(256), row(128), full(512, 1024), full(256, 1024), full(1, 512), full(1, 256),
                   full(1, 256), full(1, 256)],
        out_shape=[jax.ShapeDtypeStruct((s, 512), BF16), jax.ShapeDtypeStruct((s, 256), BF16),
                   jax.ShapeDtypeStruct((s, 128), BF16), jax.ShapeDtypeStruct((512, 1024), F32),
                   jax.ShapeDtypeStruct((256, 1024), F32), jax.ShapeDtypeStruct((1, 512), F32),
                   jax.ShapeDtypeStruct((1, 256), F32), jax.ShapeDtypeStruct((1, 256), F32),
                   jax.ShapeDtypeStruct((1, 256), F32)],
        scratch_shapes=[pltpu.VMEM((tm, 1024), BF16), pltpu.VMEM((tm, 1024), BF16)],
        compiler_params=_cparams("arbitrary"),
    )(dq, dk, dv, z, z, z, qg, wq, kvg, wkv, qng, kng, cos, sp, sn)


def _flash_fwd(q, k, v, *, name, tq=1024, tk=1024):
    s = q.shape[0]
    tq, tk = min(tq, s), min(tk, s)
    nk = s // tk

    def body(q_ref, k_ref, v_ref, o_ref, lse_ref, m_s, l_s, acc):
        j = pl.program_id(2)

        @pl.when(j == 0)
        def _():
            m_s[...] = jnp.full_like(m_s, -jnp.inf)
            l_s[...] = jnp.zeros_like(l_s)
            acc[...] = jnp.zeros_like(acc)

        sc = _dot(q_ref[...], k_ref[...], 1, 1)
        m_prev = m_s[...]
        m_new = jnp.maximum(m_prev, jnp.max(sc, axis=-1, keepdims=True))
        p = jnp.exp(sc - m_new[:, 0:1])
        alpha = jnp.exp(m_prev - m_new)
        l_s[...] = alpha * l_s[...] + jnp.sum(p, axis=-1, keepdims=True)
        acc[...] = alpha * acc[...] + _dot(p, v_ref[...])
        m_s[...] = m_new

        @pl.when(j == nk - 1)
        def _():
            o_ref[...] = acc[...] / l_s[...]
            lse_ref[...] = m_s[...] + jnp.log(l_s[...])

    return pl.pallas_call(
        body, name=name, grid=(MLA_HEADS, s // tq, nk),
        in_specs=[pl.BlockSpec((tq, MLA_QKP), lambda h, i, j: (i, h)),
                  pl.BlockSpec((tk, MLA_QKP), lambda h, i, j: (j, h)),
                  pl.BlockSpec((tk, MLA_V), lambda h, i, j: (j, h))],
        out_specs=[pl.BlockSpec((tq, MLA_V), lambda h, i, j: (i, h))] * 2,
        out_shape=[jax.ShapeDtypeStruct((s, GROUP_W), F32)] * 2,
        scratch_shapes=[pltpu.VMEM((tq, MLA_V), F32), pltpu.VMEM((tq, MLA_V), F32), pltpu.VMEM((tq, MLA_V), F32)],
        compiler_params=_cparams("parallel", "parallel", "arbitrary"),
    )(q, k, v)


def _flash_bwd(q, k, v, do, o, lse, *, name, tq=1024, tk=1024):
    s = q.shape[0]
    tq, tk = min(tq, s), min(tk, s)
    nq, nk = s // tq, s // tk

    def body(q_ref, k_ref, v_ref, do_ref, o_ref, lse_ref, dq_ref, dk_ref, dv_ref, dk_acc, dv_acc):
        j, i = pl.program_id(1), pl.program_id(2)
        dov = do_ref[...]
        delta = jnp.sum(dov * o_ref[...], axis=-1, keepdims=True)
        p = jnp.exp(_dot(q_ref[...], k_ref[...], 1, 1) - lse_ref[:, 0:1])
        ds = p * (_dot(dov, v_ref[...], 1, 1) - delta)
        pv = _dot(p, dov, 0, 0)
        pk = _dot(ds, q_ref[...], 0, 0)
        pq = _dot(ds, k_ref[...])
        rows = pl.ds(pl.multiple_of(i * tq, tq), tq)

        @pl.when(j == 0)
        def _():
            dq_ref[rows, :] = pq

        @pl.when(j > 0)
        def _():
            dq_ref[rows, :] += pq

        @pl.when(i == 0)
        def _():
            dv_acc[...] = pv
            dk_acc[...] = pk

        @pl.when(i > 0)
        def _():
            dv_acc[...] += pv
            dk_acc[...] += pk

        @pl.when(i == nq - 1)
        def _():
            dk_ref[...] = dk_acc[...]
            dv_ref[...] = dv_acc[...]

    qb = pl.BlockSpec((tq, MLA_QKP), lambda h, j, i: (i, h))
    kb = pl.BlockSpec((tk, MLA_QKP), lambda h, j, i: (j, h))
    vb = pl.BlockSpec((tk, MLA_V), lambda h, j, i: (j, h))
    ob = pl.BlockSpec((tq, MLA_V), lambda h, j, i: (i, h))
    return pl.pallas_call(
        body, name=name, grid=(MLA_HEADS, nk, nq),
        in_specs=[qb, kb, vb, ob, ob, ob],
        out_specs=[pl.BlockSpec((s, MLA_QKP), lambda h, j, i: (0, h)), kb, vb],
        out_shape=[jax.ShapeDtypeStruct((s, MLA_HEADS * MLA_QKP), F32),
                   jax.ShapeDtypeStruct((s, MLA_HEADS * MLA_QKP), F32), jax.ShapeDtypeStruct((s, GROUP_W), F32)],
        scratch_shapes=[pltpu.VMEM((tk, MLA_QKP), F32), pltpu.VMEM((tk, MLA_V), F32)],
        compiler_params=_cparams("arbitrary", "arbitrary", "arbitrary"),
    )(q, k, v, do, o, lse)


def _rows_tile(r, c, itemsize=4, budget=2 * 1024 * 1024):
    if r * c * itemsize <= budget:
        return r
    best = None
    for t in range(8, r, 8):
        if r % t == 0 and t * c * itemsize <= budget:
            best = t
    return best if best is not None else r


def _add_n(arrs, *, out_dtype=F32, name):
    shape = arrs[0].shape
    c = shape[-1]
    flat = [a.reshape(-1, c) for a in arrs]
    r = flat[0].shape[0]
    t = _rows_tile(r, c)

    def body(*refs):
        acc = refs[0][...].astype(F32)
        for ref in refs[1:-1]:
            acc = acc + ref[...].astype(F32)
        refs[-1][...] = acc.astype(out_dtype)

    blk = pl.BlockSpec((t, c), lambda i: (i, 0))
    out = pl.pallas_call(
        body, name=name, grid=(r // t,), in_specs=[blk] * len(flat), out_specs=blk,
        out_shape=jax.ShapeDtypeStruct((r, c), out_dtype), compiler_params=_cparams("parallel"),
    )(*flat)
    return out.reshape(shape)


def _adamw(w, g, m, v, *, name):
    shape = w.shape
    c = shape[-1]
    flat = [a.reshape(-1, c) for a in (w, g, m, v)]
    r = flat[0].shape[0]
    t = _rows_tile(r, c, budget=1024 * 1024)

    def body(w_ref, g_ref, m_ref, v_ref, d_ref, mo_ref, vo_ref):
        gv = g_ref[...]
        m2 = ADAM_B1 * m_ref[...] + (1.0 - ADAM_B1) * gv
        v2 = ADAM_B2 * v_ref[...] + (1.0 - ADAM_B2) * (gv * gv)
        m_hat = m2 / (1.0 - ADAM_B1 ** ADAM_STEP)
        v_hat = v2 / (1.0 - ADAM_B2 ** ADAM_STEP)
        d_ref[...] = -ADAM_LR * (m_hat / (jnp.sqrt(v_hat) + ADAM_EPS) + ADAM_WD * w_ref[...])
        mo_ref[...] = m2
        vo_ref[...] = v2

    blk = pl.BlockSpec((t, c), lambda i: (i, 0))
    outs = pl.pallas_call(
        body, name=name, grid=(r // t,), in_specs=[blk] * 4, out_specs=[blk] * 3,
        out_shape=[jax.ShapeDtypeStruct((r, c), F32)] * 3, compiler_params=_cparams("parallel"),
    )(*flat)
    return tuple(o.reshape(shape) for o in outs)


def _place():
    x, y, c = lax.axis_index("x"), lax.axis_index("y"), lax.axis_index("c")
    chips = [(1 - x, y), (x, 1 - y), (1 - x, 1 - y)]
    return x, y, c, chips


ANY = pl.BlockSpec(memory_space=pl.ANY)


def _gather_shards(shards, *, name):
    nt = len(shards)

    def body(*refs):
        src, dst = refs[:nt], refs[nt:2 * nt]
        send, recv, fsend, frecv, lsem = refs[2 * nt:]
        x, y, c, chips = _place()
        me = 2 * x + y
        local = [pltpu.make_async_copy(src[t], dst[t].at[me], lsem.at[t]) for t in range(nt)]
        for cp in local:
            cp.start()

        def half(t, slot, hc):
            hr = src[t].shape[0] // 2
            return dst[t].at[slot, pl.ds(hc * hr, hr)]

        def first(t, k):
            hr = src[t].shape[0] // 2
            return pltpu.make_async_remote_copy(
                src_ref=src[t].at[pl.ds(c * hr, hr)], dst_ref=half(t, me, c),
                send_sem=send.at[t, k], recv_sem=recv.at[t, k],
                device_id=(chips[k][0], chips[k][1], c), device_id_type=MESH)

        def landed(t, k):
            slot = 2 * chips[k][0] + chips[k][1]
            return pltpu.make_async_remote_copy(
                src_ref=half(t, slot, c), dst_ref=half(t, slot, c),
                send_sem=send.at[t, k], recv_sem=recv.at[t, k],
                device_id=(chips[k][0], chips[k][1], c), device_id_type=MESH)

        def forward(t, k, hc):
            slot = 2 * chips[k][0] + chips[k][1]
            return pltpu.make_async_remote_copy(
                src_ref=half(t, slot, hc), dst_ref=half(t, slot, hc),
                send_sem=fsend.at[t, k], recv_sem=frecv.at[t, k],
                device_id=(x, y, 1 - c), device_id_type=MESH)

        for t in range(nt):
            for k in range(3):
                first(t, k).start()
        for t in range(nt):
            for k in range(3):
                landed(t, k).wait_recv()
                forward(t, k, c).start()
        for t in range(nt):
            for k in range(3):
                forward(t, k, 1 - c).wait_recv()
        for t in range(nt):
            for k in range(3):
                first(t, k).wait_send()
                forward(t, k, c).wait_send()
        for cp in local:
            cp.wait()

    return pl.pallas_call(
        body, name=name, in_specs=[ANY] * nt, out_specs=[ANY] * nt,
        out_shape=[jax.ShapeDtypeStruct((N_CHIP,) + a.shape, a.dtype) for a in shards],
        scratch_shapes=[pltpu.SemaphoreType.DMA((nt, 3)), pltpu.SemaphoreType.DMA((nt, 3)),
                        pltpu.SemaphoreType.DMA((nt, 3)), pltpu.SemaphoreType.DMA((nt, 3)),
                        pltpu.SemaphoreType.DMA((nt,))],
    )(*shards)


def _comm_rows(hr, c, budget=2 * 1024 * 1024):
    if hr * c * 4 <= budget:
        return hr
    best = None
    for t in range(16, hr, 16):
        if hr % t == 0 and t * c * 4 <= budget:
            best = t
    return best if best is not None else hr


def _pair_reduce(g, where, *, out_dtype, name):
    n_slot, r, cdim = g.shape
    hr = r // 2
    rc = _comm_rows(hr, cdim)
    nr = hr // rc
    steps = n_slot * nr
    g4 = g.reshape(n_slot, 2, hr, cdim)

    def body(w_ref, a_ref, b_ref, o_ref, land, send, recv, credit):
        x, y, c, _ = _place()
        sib = (x, y, 1 - c)
        i = pl.program_id(0) * nr + pl.program_id(1)
        s = lax.rem(i, 2)

        @pl.when(i >= 2)
        def _():
            pl.semaphore_wait(credit.at[s], 1)

        cp = pltpu.make_async_remote_copy(src_ref=b_ref.at[0, 0], dst_ref=land.at[s], send_sem=send.at[s],
                                          recv_sem=recv.at[s], device_id=sib, device_id_type=MESH)
        cp.start()
        cp.wait_recv()
        o_ref[0] = (a_ref[0, 0] + land[s]).astype(out_dtype)
        cp.wait_send()

        @pl.when(i + 2 < steps)
        def _():
            pl.semaphore_signal(credit.at[s], inc=1, device_id=sib, device_id_type=MESH)

    blk = lambda half: pl.BlockSpec((1, 1, rc, cdim), lambda j, t, w: (j, half(w), t, 0))
    grid_spec = pltpu.PrefetchScalarGridSpec(
        num_scalar_prefetch=1, grid=(n_slot, nr),
        in_specs=[blk(lambda w: w[0]), blk(lambda w: 1 - w[0])],
        out_specs=pl.BlockSpec((1, rc, cdim), lambda j, t, w: (j, t, 0)),
        scratch_shapes=[pltpu.VMEM((2, rc, cdim), F32), pltpu.SemaphoreType.DMA((2,)), pltpu.SemaphoreType.DMA((2,)),
                        pltpu.SemaphoreType.REGULAR((2,))])
    return pl.pallas_call(
        body, name=name, grid_spec=grid_spec, out_shape=jax.ShapeDtypeStruct((n_slot, hr, cdim), out_dtype),
        compiler_params=_cparams("arbitrary", "arbitrary"),
    )(where, g4, g4)


def _chip_exchange(parts, *, name):
    nt = len(parts)

    def body(*refs):
        src, got = refs[:nt], refs[nt:2 * nt]
        send, recv = refs[2 * nt:]
        x, y, c, chips = _place()
        remote = []
        for t in range(nt):
            for k in range(3):
                remote.append(pltpu.make_async_remote_copy(
                    src_ref=src[t].at[2 * chips[k][0] + chips[k][1]], dst_ref=got[t].at[k],
                    send_sem=send.at[t, k], recv_sem=recv.at[t, k],
                    device_id=(chips[k][0], chips[k][1], c), device_id_type=MESH))
        for cp in remote:
            cp.start()
        for cp in remote:
            cp.wait_recv()
        for cp in remote:
            cp.wait_send()

    return pl.pallas_call(
        body, name=name, in_specs=[ANY] * nt, out_specs=[ANY] * nt,
        out_shape=[jax.ShapeDtypeStruct((3,) + a.shape[1:], a.dtype) for a in parts],
        scratch_shapes=[pltpu.SemaphoreType.DMA((nt, 3)), pltpu.SemaphoreType.DMA((nt, 3))],
    )(*parts)


def _sum_join(p, got, where, *, name):
    _, hr, cdim = p.shape
    rc = _comm_rows(hr, cdim)
    n = hr // rc

    def body(w_ref, p_ref, g_ref, out, buf, lsem, ssem, rsem):
        x, y, c, _ = _place()
        sib = (x, y, 1 - c)
        r = pl.program_id(0)

        def copies(step, slot):
            rows = out.at[pl.ds(pl.multiple_of(c * hr + step * rc, 8), rc)]
            return (pltpu.make_async_copy(buf.at[slot], rows, lsem.at[slot]),
                    pltpu.make_async_remote_copy(src_ref=buf.at[slot], dst_ref=rows, send_sem=ssem.at[slot],
                                                 recv_sem=rsem, device_id=sib, device_id_type=MESH))

        s = lax.rem(r, 2)

        @pl.when(r >= 2)
        def _():
            lc, rm = copies(r - 2, s)
            lc.wait()
            rm.wait_send()

        buf[s] = p_ref[0].astype(F32) + g_ref[0].astype(F32) + g_ref[1].astype(F32) + g_ref[2].astype(F32)
        lc, rm = copies(r, s)
        lc.start()
        rm.start()

        @pl.when(r == n - 1)
        def _():
            for step in range(max(0, n - 2), n):
                lc, rm = copies(step, step % 2)
                lc.wait()
                rm.wait_send()
            whole = out.at[pl.ds(0, hr)]
            pltpu.make_async_remote_copy(src_ref=whole, dst_ref=whole, send_sem=ssem.at[0], recv_sem=rsem,
                                         device_id=sib, device_id_type=MESH).wait_recv()

    grid_spec = pltpu.PrefetchScalarGridSpec(
        num_scalar_prefetch=1, grid=(n,),
        in_specs=[pl.BlockSpec((1, rc, cdim), lambda t, w: (w[1], t, 0)),
                  pl.BlockSpec((3, rc, cdim), lambda t, w: (0, t, 0))],
        out_specs=ANY,
        scratch_shapes=[pltpu.VMEM((2, rc, cdim), F32), pltpu.SemaphoreType.DMA((2,)), pltpu.SemaphoreType.DMA((2,)),
                        pltpu.SemaphoreType.DMA])
    return pl.pallas_call(
        body, name=name, grid_spec=grid_spec, out_shape=jax.ShapeDtypeStruct((2 * hr, cdim), F32),
        compiler_params=_cparams("arbitrary"),
    )(where, p, got)


def _gather_all(block, *, name):
    m_per, n = block.shape

    def body(x_ref, out_ref, send_sems, recv_sems, local_sem):
        x, y, c, chips = _place()
        me, sibling = (x, y, c), (x, y, 1 - c)

        def rows(px, py, pc):
            return out_ref.at[4 * px + 2 * py + pc]

        def copy(k, blk, to, src=None):
            return pltpu.make_async_remote_copy(
                src_ref=rows(*blk) if src is None else src, dst_ref=rows(*blk),
                send_sem=send_sems.at[k], recv_sem=recv_sems.at[k], device_id=to, device_id_type=MESH)

        mine = pltpu.make_async_copy(x_ref, rows(*me), local_sem)
        mine.start()
        first = [copy(0, me, sibling, src=x_ref)]
        first += [copy(1 + j, me, (*chip, c), src=x_ref) for j, chip in enumerate(chips)]
        for cp in first:
            cp.start()
        passed = [copy(4 + j, (*chip, c), sibling) for j, chip in enumerate(chips)]
        for j, chip in enumerate(chips):
            copy(1 + j, (*chip, c), me).wait_recv()
            passed[j].start()
        copy(0, sibling, me).wait_recv()
        for j, chip in enumerate(chips):
            copy(4 + j, (*chip, 1 - c), me).wait_recv()
        for cp in first + passed:
            cp.wait_send()
        mine.wait()

    return pl.pallas_call(
        body, name=name,
        out_shape=jax.ShapeDtypeStruct((N_DEV, m_per, n), block.dtype),
        in_specs=[pl.BlockSpec(memory_space=pltpu.VMEM)], out_specs=pl.BlockSpec(memory_space=pltpu.VMEM),
        scratch_shapes=[pltpu.SemaphoreType.DMA((7,)), pltpu.SemaphoreType.DMA((7,)), pltpu.SemaphoreType.DMA],
        compiler_params=pltpu.CompilerParams(vmem_limit_bytes=VMEM_LIMIT),
    )(block)


def _sum_slots(slots, *, name):
    n, m, c = slots.shape
    t = _rows_tile(m, c * n)

    def body(s_ref, o_ref):
        acc = s_ref[0]
        for k in range(1, n):
            acc = acc + s_ref[k]
        o_ref[...] = acc

    return pl.pallas_call(
        body, name=name, grid=(m // t,), in_specs=[pl.BlockSpec((n, t, c), lambda i: (0, i, 0))],
        out_specs=pl.BlockSpec((t, c), lambda i: (i, 0)), out_shape=jax.ShapeDtypeStruct((m, c), F32),
        compiler_params=_cparams("parallel"),
    )(slots)


def _pad_cols(a, width):
    return a if a.shape[1] == width else jnp.pad(a, ((0, 0), (0, width - a.shape[1])))


def _w_in_padded(shards):
    full = jnp.concatenate([shards[j] for j in range(N_CHIP)], axis=1)
    return jnp.concatenate([_pad_cols(full[:, SEG[n][2]:SEG[n][2] + SEG[n][3]], SEG[n][1]) for n in SEG_ORDER], axis=1)


def _w_in_unpadded(gp):
    full = jnp.concatenate([gp[:, SEG[n][0]:SEG[n][0] + SEG[n][3]] for n in ORIG_ORDER], axis=1)
    w = IN_COLS // N_CHIP
    return jnp.stack([full[:, j * w:(j + 1) * w] for j in range(N_CHIP)])


def _pad_heads(w, true_w, pad_w):
    r = w.shape[0]
    h = w.shape[1] // true_w
    return jnp.pad(w.reshape(r, h, true_w), ((0, 0), (0, 0), (0, pad_w - true_w))).reshape(r, h * pad_w)


def _unpad_heads(w, true_w, pad_w):
    r = w.shape[0]
    h = w.shape[1] // pad_w
    return w.reshape(r, h, pad_w)[:, :, :true_w].reshape(r, h * true_w)


def _cols_to_slots(a):
    w = a.shape[1] // N_CHIP
    return jnp.stack([a[:, j * w:(j + 1) * w] for j in range(N_CHIP)])


def _slots_to_cols(a):
    return jnp.concatenate([a[j] for j in range(N_CHIP)], axis=1)


def _to_heads(a, h, d):
    return a.reshape(a.shape[0], h, d).transpose(1, 0, 2)


def _from_heads(a):
    return a.transpose(1, 0, 2).reshape(a.shape[1], -1)


SMALL = [("norm_g", 2048), ("ret_norm_g", 512), ("gla_ba_f", 256), ("gla_ba_b", 256), ("gla_norm_g", 512),
         ("pool_w", 4 * 128 * 128), ("pool_scale", 512), ("mla_q_norm_g", 512), ("mla_kv_norm_g", 256),
         ("mla_qk_norm_q", 192), ("mla_qk_norm_k", 192)]


def _pack_small(vals):
    parts = []
    for name, n in SMALL:
        v = vals[name].reshape(-1)
        parts.append(jnp.pad(v, (0, (-v.shape[0]) % 1024)))
    parts.append(jnp.pad(vals["loss"].reshape(-1), (0, 1023)))
    return jnp.concatenate(parts).reshape(-1, 128)


def _unpack_small(block):
    flat = block.reshape(-1)
    out, off = {}, 0
    for name, n in SMALL:
        out[name] = flat[off:off + DEPTH * n]
        off += DEPTH * n + (-(DEPTH * n)) % 1024
    out["loss"] = flat[off]
    return out


def _layer_weights(l, p, g):
    wa = jnp.zeros((128, 512), F32)
    wa = wa.at[0:GLA_RANK, 0:256].set(_slots_to_cols(g["gla_wa2_f"][:, l]))
    wa = wa.at[GLA_RANK:2 * GLA_RANK, 256:512].set(_slots_to_cols(g["gla_wa2_b"][:, l]))
    return dict(
        norm_g=p["norm_g"][l][None, :],
        w_in=_w_in_padded(g["w_in"][:, l]),
        w_out=g["w_out"][:, l].reshape(4 * g["w_out"].shape[2], -1),
        ret_norm_g=p["ret_norm_g"][l][None, :],
        wa=_bf(wa),
        ba=jnp.concatenate([p["gla_ba_f"][l], p["gla_ba_b"][l]])[None, :],
        gla_norm_g=p["gla_norm_g"][l][None, :],
        pool_w=_bf(p["pool_w"][l]),
        pool_scale=p["pool_scale"][l][None, :],
        qg=p["mla_q_norm_g"][l][None, :],
        wq=_pad_heads(_slots_to_cols(g["mla_wq_b"][:, l]), MLA_QK, MLA_QKP),
        kvg=p["mla_kv_norm_g"][l][None, :],
        wkv=_slots_to_cols(g["mla_wkv_b"][:, l]),
        qng=jnp.pad(p["mla_qk_norm_q"][l], (0, MLA_QKP - MLA_QK))[None, :],
        kng=jnp.pad(p["mla_qk_norm_k"][l], (0, MLA_QKP - MLA_QK))[None, :],
    )


def _layer_fwd(l, x, w, tabs):
    ret_cos, ret_sin, mla_cos, mla_sp, mla_sn = tabs
    nm = lambda s: f"l{l}_{s}"
    h = _rmsnorm_fwd(x, w["norm_g"], name=nm("norm"))
    z = _matmul(h, w["w_in"], name=nm("in_proj"))
    qr, kr = _ret_pre(z, ret_cos, ret_sin, name=nm("ret_pre"))
    ret_o = _bla(qr, kr, z, _ret_log_gamma(False), (0, 0, SEG["rv"][0] // 512), name=nm("ret_scan"))
    y_a = _post(ret_o, z, SEG["rg"][0] // 512, w["ret_norm_g"], norm=True, name=nm("ret_post"))
    la = _gla_gate(z, w["wa"], w["ba"], name=nm("gla_gate"))
    la_h = jnp.stack([_to_heads(la[:, :256], GLA_HEADS, GLA_DK), _to_heads(la[:, 256:], GLA_HEADS, GLA_DK)])
    gq = _to_heads(z[:, SEG["gq"][0]:SEG["gq"][0] + 256], GLA_HEADS, GLA_DK)
    gk = _to_heads(z[:, SEG["gk"][0]:SEG["gk"][0] + 256], GLA_HEADS, GLA_DK)
    gla_o, gla_st = _gla_fwd(gq, gk, z, la_h, name=nm("gla_scan"))
    y_b = _post(gla_o, z, SEG["gg"][0] // 512, w["gla_norm_g"], norm=True, name=nm("gla_post"))
    y_c = _pool_fwd(z, w["pool_w"], w["pool_scale"], name=nm("pool"))
    q, k, v = _mla_pre(z, w["qg"], w["wq"], w["kvg"], w["wkv"], w["qng"], w["kng"], mla_cos, mla_sp, mla_sn,
                       name=nm("mla_pre"))
    att_o, lse = _flash_fwd(q, k, v, name=nm("attn"))
    y_d = _post([att_o], z, SEG["mg"][0] // 512, w["qg"], norm=False, name=nm("mla_post"))
    y = jnp.concatenate([y_a, y_b, y_c, y_d], axis=1)
    x_next = _matmul(y, w["w_out"], add=x, name=nm("out_proj"))
    saved = dict(x=x, h=h, z=z, y=y, qr=qr, kr=kr, ret_o=ret_o, la_h=la_h, gq=gq, gk=gk, gla_o=gla_o, gla_st=gla_st,
                 q=q, k=k, v=v, att_o=att_o, lse=lse)
    return x_next, saved


def _layer_bwd(l, dx_next, w, sv, tabs):
    ret_cos, ret_sin, mla_cos, mla_sp, mla_sn = tabs
    nm = lambda s: f"l{l}_{s}"
    z = sv["z"]
    dy = _matmul(dx_next, w["w_out"], tb=True, name=nm("out_proj_dy"))
    d_w_out = _matmul(sv["y"].T, dx_next, tn=512, name=nm("out_proj_dw"))
    d_rg, d_ret_o, d_ret_g = _post_bwd(dy, 0, sv["ret_o"], z, SEG["rg"][0] // 512, w["ret_norm_g"], norm=True,
                                       name=nm("ret_post_bwd"))
    vcol = SEG["rv"][0] // 512
    dqr = _bla(d_ret_o, z, sv["kr"], _ret_log_gamma(False), (0, vcol, 0), name=nm("ret_scan_dq"))
    dkr = _bla(z, d_ret_o, sv["qr"], _ret_log_gamma(True), (vcol, 0, 0), name=nm("ret_scan_dk"))
    drv = _bla(sv["kr"], sv["qr"], d_ret_o, _ret_log_gamma(True), (0, 0, 0), name=nm("ret_scan_dv"))
    d_rq, d_rk = _ret_pre_bwd(dqr, dkr, ret_cos, ret_sin, name=nm("ret_pre_bwd"))
    d_rv = _add_n([drv[0], drv[1]], out_dtype=BF16, name=nm("ret_dv_sum"))
    d_gg, d_gla_o, d_gla_g = _post_bwd(dy, 1, sv["gla_o"], z, SEG["gg"][0] // 512, w["gla_norm_g"], norm=True,
                                       name=nm("gla_post_bwd"))
    dq2, dk2, dla2, dv2 = _gla_bwd(sv["gq"], sv["gk"], z, sv["la_h"], d_gla_o, sv["gla_st"], name=nm("gla_scan_bwd"))
    d_gq = _bf(_from_heads(dq2[0] + dq2[1]))
    d_gk = _bf(_from_heads(dk2[0] + dk2[1]))
    d_gv = _add_n([dv2[0], dv2[1]], out_dtype=BF16, name=nm("gla_dv_sum"))
    dla = jnp.concatenate([_from_heads(dla2[0]), _from_heads(dla2[1])], axis=1)
    d_ga, d_wa, d_ba = _gla_gate_bwd(dla, z, w["wa"], w["ba"], name=nm("gla_gate_bwd"))
    d_pv, d_pg, d_pool_w, d_pool_scale = _pool_bwd(dy, z, w["pool_w"], w["pool_scale"], name=nm("pool_bwd"))
    d_mg, d_att_o, _ = _post_bwd(dy, 3, [sv["att_o"]], z, SEG["mg"][0] // 512, w["qg"], norm=False,
                                 name=nm("mla_post_bwd"))
    dq, dk, dv = _flash_bwd(sv["q"], sv["k"], sv["v"], d_att_o, sv["att_o"], sv["lse"], name=nm("attn_bwd"))
    d_mq, d_mkv, d_mkr, d_wq, d_wkv, d_qg, d_kvg, d_qng, d_kng = _mla_pre_bwd(
        dq, dk, dv, z, w["qg"], w["wq"], w["kvg"], w["wkv"], w["qng"], w["kng"], mla_cos, mla_sp, mla_sn,
        name=nm("mla_pre_bwd"))
    segs = dict(rq=d_rq, rk=d_rk, rv=d_rv, rg=d_rg, gv=d_gv, gg=d_gg, pv=d_pv, pg=d_pg, mq=d_mq, mg=d_mg,
                gq=d_gq, gk=d_gk, mkv=d_mkv, ga=d_ga, mkr=d_mkr)
    dz = jnp.concatenate([segs[n] for n in SEG_ORDER], axis=1)
    dh = _matmul(dz, w["w_in"], tb=True, tn=512, name=nm("in_proj_dh"))
    d_w_in = _matmul(sv["h"].T, dz, name=nm("in_proj_dw"))
    dx, d_norm_g = _rmsnorm_bwd(sv["x"], dh, w["norm_g"], dx_next, name=nm("norm_bwd"))
    sharded = dict(
        w_in=_w_in_unpadded(d_w_in),
        w_out=d_w_out.reshape(N_CHIP, d_w_out.shape[0] // N_CHIP, d_w_out.shape[1]),
        mla_wq_b=_cols_to_slots(_unpad_heads(d_wq, MLA_QK, MLA_QKP)),
        mla_wkv_b=_cols_to_slots(d_wkv),
        gla_wa2_f=_cols_to_slots(d_wa[0:GLA_RANK, 0:256]),
        gla_wa2_b=_cols_to_slots(d_wa[GLA_RANK:2 * GLA_RANK, 256:512]),
    )
    small = dict(
        norm_g=d_norm_g[0], ret_norm_g=d_ret_g[0], gla_ba_f=d_ba[0, :256], gla_ba_b=d_ba[0, 256:],
        gla_norm_g=d_gla_g[0], pool_w=d_pool_w.reshape(-1), pool_scale=d_pool_scale[0], mla_q_norm_g=d_qg[0],
        mla_kv_norm_g=d_kvg[0], mla_qk_norm_q=d_qng[0, :MLA_QK], mla_qk_norm_k=d_kng[0, :MLA_QK],
    )
    return dx, sharded, small


SHARDED = ["w_in", "w_out", "mla_wq_b", "mla_wkv_b", "gla_wa2_f", "gla_wa2_b"]
WEIGHTS = ["norm_g", "w_in", "ret_norm_g", "gla_wa2_f", "gla_ba_f", "gla_wa2_b", "gla_ba_b", "gla_norm_g", "pool_w",
           "pool_scale", "mla_q_norm_g", "mla_wq_b", "mla_kv_norm_g", "mla_wkv_b", "mla_qk_norm_q", "mla_qk_norm_k",
           "w_out"]


def _local_step(p, gathered):
    x = p["x"][0]
    tabs = _rope_tables(x.shape[0])
    ws, saved = [], []
    for l in range(DEPTH):
        w = _layer_weights(l, p, gathered)
        x, sv = _layer_fwd(l, x, w, tabs)
        ws.append(w)
        saved.append(sv)
    dx, loss = _loss_head(x, p["loss_target"][0], name="loss_head")
    sharded, small = [None] * DEPTH, [None] * DEPTH
    for l in reversed(range(DEPTH)):
        dx, sharded[l], small[l] = _layer_bwd(l, dx, ws[l], saved[l], tabs)
    sharded = {n: jnp.stack([sharded[l][n] for l in range(DEPTH)], axis=1) for n in SHARDED}
    small = {n: jnp.stack([small[l][n] for l in range(DEPTH)]) for n, _ in SMALL}
    small["loss"] = loss
    return dx[None], sharded, small


def kernel(x, norm_g, w_in, ret_norm_g, gla_wa2_f, gla_ba_f, gla_wa2_b, gla_ba_b, gla_norm_g, pool_w, pool_scale, mla_q_norm_g, mla_wq_b, mla_kv_norm_g, mla_wkv_b, mla_qk_norm_q, mla_qk_norm_k, w_out, loss_target, m_norm_g, m_w_in, m_ret_norm_g, m_gla_wa2_f, m_gla_ba_f, m_gla_wa2_b, m_gla_ba_b, m_gla_norm_g, m_pool_w, m_pool_scale, m_mla_q_norm_g, m_mla_wq_b, m_mla_kv_norm_g, m_mla_wkv_b, m_mla_qk_norm_q, m_mla_qk_norm_k, m_w_out, v_norm_g, v_w_in, v_ret_norm_g, v_gla_wa2_f, v_gla_ba_f, v_gla_wa2_b, v_gla_ba_b, v_gla_norm_g, v_pool_w, v_pool_scale, v_mla_q_norm_g, v_mla_wq_b, v_mla_kv_norm_g, v_mla_wkv_b, v_mla_qk_norm_q, v_mla_qk_norm_k, v_w_out):
    p = dict(x=x, norm_g=norm_g, w_in=w_in, ret_norm_g=ret_norm_g, gla_wa2_f=gla_wa2_f, gla_ba_f=gla_ba_f,
             gla_wa2_b=gla_wa2_b, gla_ba_b=gla_ba_b, gla_norm_g=gla_norm_g, pool_w=pool_w, pool_scale=pool_scale,
             mla_q_norm_g=mla_q_norm_g, mla_wq_b=mla_wq_b, mla_kv_norm_g=mla_kv_norm_g, mla_wkv_b=mla_wkv_b,
             mla_qk_norm_q=mla_qk_norm_q, mla_qk_norm_k=mla_qk_norm_k, w_out=w_out, loss_target=loss_target)
    moments = dict(
        m=dict(norm_g=m_norm_g, w_in=m_w_in, ret_norm_g=m_ret_norm_g, gla_wa2_f=m_gla_wa2_f, gla_ba_f=m_gla_ba_f,
               gla_wa2_b=m_gla_wa2_b, gla_ba_b=m_gla_ba_b, gla_norm_g=m_gla_norm_g, pool_w=m_pool_w,
               pool_scale=m_pool_scale, mla_q_norm_g=m_mla_q_norm_g, mla_wq_b=m_mla_wq_b,
               mla_kv_norm_g=m_mla_kv_norm_g, mla_wkv_b=m_mla_wkv_b, mla_qk_norm_q=m_mla_qk_norm_q,
               mla_qk_norm_k=m_mla_qk_norm_k, w_out=m_w_out),
        v=dict(norm_g=v_norm_g, w_in=v_w_in, ret_norm_g=v_ret_norm_g, gla_wa2_f=v_gla_wa2_f, gla_ba_f=v_gla_ba_f,
               gla_wa2_b=v_gla_wa2_b, gla_ba_b=v_gla_ba_b, gla_norm_g=v_gla_norm_g, pool_w=v_pool_w,
               pool_scale=v_pool_scale, mla_q_norm_g=v_mla_q_norm_g, mla_wq_b=v_mla_wq_b,
               mla_kv_norm_g=v_mla_kv_norm_g, mla_wkv_b=v_mla_wkv_b, mla_qk_norm_q=v_mla_qk_norm_q,
               mla_qk_norm_k=v_mla_qk_norm_k, w_out=v_w_out))

    def as_rows(name, dtype):
        a = p[name].astype(dtype)
        return a.reshape(a.shape[0] * a.shape[1], a.shape[2])

    shards = [as_rows("w_in", BF16), as_rows("w_out", BF16), as_rows("mla_wq_b", BF16), as_rows("mla_wkv_b", BF16),
              as_rows("gla_wa2_f", F32), as_rows("gla_wa2_b", F32)]
    got = _gather_shards(shards, name="gather_weights")
    gathered = {n: a.reshape((N_CHIP, DEPTH, a.shape[1] // DEPTH, a.shape[2])) for n, a in zip(SHARDED, got)}

    grad_x, sharded, small = _local_step(p, gathered)

    where = jnp.stack([lax.axis_index("c"), 2 * lax.axis_index("x") + lax.axis_index("y")]).astype(jnp.int32)
    flat = [sharded[n].reshape(N_CHIP, -1, sharded[n].shape[-1]) for n in SHARDED]
    pair = [_pair_reduce(a, where, out_dtype=BF16, name=f"grad_pair_reduce_{n}") for n, a in zip(SHARDED, flat)]
    others = _chip_exchange(pair, name="grad_chip_exchange")
    joined = [_sum_join(a, b, where, name=f"grad_sum_join_{n}") for n, a, b in zip(SHARDED, pair, others)]
    grads = {n: a.reshape(p[n].shape) for n, a in zip(SHARDED, joined)}

    slots = _gather_all(_pack_small(small), name="gather_small")
    total = _unpack_small(_sum_slots(slots, name="sum_small"))
    for n, _ in SMALL:
        grads[n] = total[n].reshape(p[n].shape)
    loss = total["loss"]

    delta, new_m, new_v = {}, {}, {}
    for n in WEIGHTS:
        delta[n], new_m[n], new_v[n] = _adamw(p[n], grads[n], moments["m"][n], moments["v"][n], name=f"adamw_{n}")
    return (loss, grad_x, *[grads[n] for n in WEIGHTS], *[delta[n] for n in WEIGHTS],
            *[new_m[n] for n in WEIGHTS], *[new_v[n] for n in WEIGHTS])
```

```python
import functools
import math

import jax
import jax.numpy as jnp
from jax import lax
from jax.experimental import pallas as pl
from jax.experimental.pallas import tpu as pltpu

F32 = jnp.float32
BF16 = jnp.bfloat16
MESH = pl.DeviceIdType.MESH

EPS = 1e-6
ROPE_THETA = 10000.0
DEPTH = 2
N_DEV = 8
N_CHIP = 4

GROUP_W = 512
RET_HEADS = 4
RET_HD = 128
RET_CHUNK = 128
GLA_HEADS = 4
GLA_DK = 64
GLA_DV = 128
GLA_RANK = 16
GLA_TAU = 16.0
GLA_CHUNK = 64
POOL_GROUPS = 4
POOL_GW = 128
POOL_HALO = 8
POOL_TILE = 256
MLA_HEADS = 4
MLA_NOPE = 128
MLA_ROPE = 64
MLA_QK = MLA_NOPE + MLA_ROPE
MLA_QKP = 256
MLA_V = 128
MLA_Q_RANK = 512
MLA_KV_RANK = 256
MLA_SCALE = MLA_QK ** -0.5

ADAM_LR = 0.001
ADAM_B1 = 0.9
ADAM_B2 = 0.999
ADAM_EPS = 1e-08
ADAM_WD = 0.01
ADAM_STEP = 10

VMEM_LIMIT = 56 * 1024 * 1024

SEG = {
    "rq": (0, 512, 0, 512), "rk": (512, 512, 512, 512), "rv": (1024, 512, 1024, 512), "rg": (1536, 512, 1536, 512),
    "gv": (2048, 512, 2560, 512), "gg": (2560, 512, 3072, 512),
    "pv": (3072, 512, 3616, 512), "pg": (3584, 512, 4128, 512),
    "mq": (4096, 512, 4640, 512), "mg": (4608, 512, 5472, 512),
    "gq": (5120, 256, 2048, 256), "gk": (5376, 256, 2304, 256), "mkv": (5632, 256, 5152, 256),
    "ga": (5888, 128, 3584, 32), "mkr": (6016, 128, 5408, 64),
}
SEG_ORDER = ["rq", "rk", "rv", "rg", "gv", "gg", "pv", "pg", "mq", "mg", "gq", "gk", "mkv", "ga", "mkr"]
IN_COLS = 5984
IN_PAD = 6144
ORIG_ORDER = ["rq", "rk", "rv", "rg", "gq", "gk", "gv", "gg", "ga", "pv", "pg", "mq", "mkv", "mkr", "mg"]


def _cparams(*sem):
    return pltpu.CompilerParams(dimension_semantics=tuple(sem), vmem_limit_bytes=VMEM_LIMIT)


def _bf(v):
    return v.astype(BF16)


def _dot(a, b, ca=1, cb=0):
    return lax.dot_general(_bf(a), _bf(b), (((ca,), (cb,)), ((), ())), preferred_element_type=F32)


def _split_dot(a01, x, ca=1, cb=0):
    hi = _bf(x)
    r1 = x - hi.astype(F32)
    mid = _bf(r1)
    lo = _bf(r1 - mid.astype(F32))
    dn = (((ca,), (cb,)), ((), ()))
    a = _bf(a01)
    return (lax.dot_general(a, hi, dn, preferred_element_type=F32)
            + lax.dot_general(a, mid, dn, preferred_element_type=F32)
            + lax.dot_general(a, lo, dn, preferred_element_type=F32))


def _sigmoid(x):
    return 1.0 / (1.0 + jnp.exp(-x))


def _silu_parts(g):
    sg = _sigmoid(g)
    return g * sg, sg * (1.0 + g * (1.0 - sg))


class _Rider:
    def __init__(self, ins, outs, sems, start, finish, aliases=None):
        self.ins, self.outs, self.sems, self.start, self.finish = list(ins), list(outs), list(sems), start, finish
        self.aliases = dict(aliases or {})


def _ride(body, rider, n_in, n_out, grid):
    if rider is None:
        return body
    ri, ro, rs = len(rider.ins), len(rider.outs), len(rider.sems)

    def wrapped(*refs):
        ins, refs = refs[:n_in], refs[n_in:]
        rin, refs = refs[:ri], refs[ri:]
        outs, refs = refs[:n_out], refs[n_out:]
        rout, refs = refs[:ro], refs[ro:]
        scratch, sems = refs[:len(refs) - rs], refs[len(refs) - rs:]
        first = pl.program_id(0) == 0
        last = pl.program_id(0) == grid[0] - 1
        for ax in range(1, len(grid)):
            first = jnp.logical_and(first, pl.program_id(ax) == 0)
            last = jnp.logical_and(last, pl.program_id(ax) == grid[ax] - 1)

        @pl.when(first)
        def _():
            rider.start(rin, rout, sems)

        body(*ins, *outs, *scratch)

        @pl.when(last)
        def _():
            rider.finish(rin, rout, sems)

    return wrapped


def _ride_call(body, rider, *, name, grid, in_specs, out_specs, out_shape, scratch_shapes, args, sem):
    n_in, n_out = len(in_specs), len(out_specs)
    if rider is None:
        return pl.pallas_call(body, name=name, grid=grid, in_specs=in_specs, out_specs=out_specs, out_shape=out_shape,
                              scratch_shapes=scratch_shapes, compiler_params=_cparams(*sem))(*args), []
    outs = pl.pallas_call(
        _ride(body, rider, n_in, n_out, grid), name=name, grid=grid,
        in_specs=list(in_specs) + [ANY] * len(rider.ins), out_specs=list(out_specs) + [ANY] * len(rider.outs),
        out_shape=list(out_shape) + rider.outs, scratch_shapes=list(scratch_shapes) + rider.sems,
        input_output_aliases={n_in + i: n_out + o for i, o in rider.aliases.items()},
        compiler_params=_cparams(*(["arbitrary"] * len(grid))),
    )(*args, *rider.ins)
    return outs[:n_out], outs[n_out:]


def _matmul(a, b, *, ta=False, tb=False, out_dtype=F32, tm=512, tn=1024, tk=None, add=None, n_outer=True, rider=None,
            name):
    m, kdim = (a.shape[1], a.shape[0]) if ta else a.shape
    n = b.shape[0] if tb else b.shape[1]
    tm, tn = min(tm, m), min(tn, n)
    tk = kdim if tk is None else min(tk, kdim)
    assert m % tm == 0 and n % tn == 0 and kdim % tk == 0
    nk = kdim // tk
    ca, cb = (0 if ta else 1), (1 if tb else 0)

    def body(*refs):
        if add is None:
            a_ref, b_ref, o_ref = refs[:3]
            add_ref = None
        else:
            a_ref, b_ref, add_ref, o_ref = refs[:4]
        p = _dot(a_ref[...], b_ref[...], ca, cb)

        def finish(r):
            if add_ref is not None:
                r = r + add_ref[...]
            o_ref[...] = r.astype(out_dtype)

        if nk == 1:
            finish(p)
        else:
            acc = refs[-1]
            k = pl.program_id(2)

            @pl.when(k == 0)
            def _():
                acc[...] = p

            @pl.when(k > 0)
            def _():
                acc[...] += p

            @pl.when(k == nk - 1)
            def _():
                finish(acc[...])

    def ij(g0, g1):
        return (g1, g0) if n_outer else (g0, g1)

    a_spec = (pl.BlockSpec((tk, tm), lambda g0, g1, k: (k, ij(g0, g1)[0])) if ta
              else pl.BlockSpec((tm, tk), lambda g0, g1, k: (ij(g0, g1)[0], k)))
    b_spec = (pl.BlockSpec((tn, tk), lambda g0, g1, k: (ij(g0, g1)[1], k)) if tb
              else pl.BlockSpec((tk, tn), lambda g0, g1, k: (k, ij(g0, g1)[1])))
    o_spec = pl.BlockSpec((tm, tn), lambda g0, g1, k: ij(g0, g1))
    in_specs = [a_spec, b_spec] + ([o_spec] if add is not None else [])
    args = (a, b) + ((add,) if add is not None else ())
    grid = (n // tn, m // tm, nk) if n_outer else (m // tm, n // tn, nk)
    (out,), rode = _ride_call(
        body, rider, name=name, grid=grid, in_specs=in_specs, out_specs=[o_spec],
        out_shape=[jax.ShapeDtypeStruct((m, n), out_dtype)],
        scratch_shapes=[] if nk == 1 else [pltpu.VMEM((tm, tn), F32)], args=args,
        sem=("parallel", "parallel", "arbitrary"))
    return out if rider is None else (out, rode)


def _rmsnorm_fwd(x, g, *, name, tm=256):
    s, d = x.shape
    tm = min(tm, s)

    def body(x_ref, g_ref, h_ref):
        xv = x_ref[...]
        r = lax.rsqrt(jnp.mean(xv * xv, axis=-1, keepdims=True) + EPS)
        h_ref[...] = _bf(xv * r * g_ref[...])

    return pl.pallas_call(
        body, name=name, grid=(s // tm,),
        in_specs=[pl.BlockSpec((tm, d), lambda i: (i, 0)), pl.BlockSpec((1, d), lambda i: (0, 0))],
        out_specs=pl.BlockSpec((tm, d), lambda i: (i, 0)),
        out_shape=jax.ShapeDtypeStruct((s, d), BF16),
        compiler_params=_cparams("parallel"),
    )(x, g)


def _rmsnorm_bwd(x, dh, g, dres, *, name, tm=256):
    s, d = x.shape
    tm = min(tm, s)

    def body(x_ref, dh_ref, g_ref, dres_ref, dx_ref, dg_ref):
        i = pl.program_id(0)
        xv = x_ref[...]
        r = lax.rsqrt(jnp.mean(xv * xv, axis=-1, keepdims=True) + EPS)
        xn = xv * r
        dv = dh_ref[...]
        part = jnp.sum(dv * xn, axis=0, keepdims=True)

        @pl.when(i == 0)
        def _():
            dg_ref[...] = part

        @pl.when(i > 0)
        def _():
            dg_ref[...] += part

        dxn = dv * g_ref[...]
        dx_ref[...] = dres_ref[...] + r * (dxn - xn * jnp.mean(dxn * xn, axis=-1, keepdims=True))

    row = pl.BlockSpec((tm, d), lambda i: (i, 0))
    vec = pl.BlockSpec((1, d), lambda i: (0, 0))
    return pl.pallas_call(
        body, name=name, grid=(s // tm,), in_specs=[row, row, vec, row], out_specs=[row, vec],
        out_shape=[jax.ShapeDtypeStruct((s, d), F32), jax.ShapeDtypeStruct((1, d), F32)],
        compiler_params=_cparams("arbitrary"),
    )(x, dh, g, dres)


def _loss_head(xf, target, *, name, tm=256):
    s, d = xf.shape
    tm = min(tm, s)

    def body(x_ref, t_ref, dx_ref, l_ref):
        i = pl.program_id(0)
        e = x_ref[...] - t_ref[...]
        dx_ref[...] = e * (1.0 / d)
        rows = jnp.mean(e * e, axis=-1, keepdims=True)
        part = 0.5 * jnp.sum(rows, axis=0, keepdims=True)

        @pl.when(i == 0)
        def _():
            l_ref[...] = part

        @pl.when(i > 0)
        def _():
            l_ref[...] += part

    row = pl.BlockSpec((tm, d), lambda i: (i, 0))
    return pl.pallas_call(
        body, name=name, grid=(s // tm,), in_specs=[row, row],
        out_specs=[row, pl.BlockSpec((1, 1), lambda i: (0, 0))],
        out_shape=[jax.ShapeDtypeStruct((s, d), F32), jax.ShapeDtypeStruct((1, 1), F32)],
        compiler_params=_cparams("arbitrary"),
    )(xf, target)


def _rope_tables(s):
    pos = jnp.arange(s, dtype=F32)[:, None]
    inv_r = 1.0 / (ROPE_THETA ** (jnp.arange(0, RET_HD, 2, dtype=F32) / RET_HD))
    ang = pos * inv_r[None, :]
    ret_cos = jnp.concatenate([jnp.cos(ang), jnp.cos(ang)], axis=1)
    ret_sin = jnp.concatenate([-jnp.sin(ang), jnp.sin(ang)], axis=1)
    inv_m = 1.0 / (ROPE_THETA ** (jnp.arange(0, MLA_ROPE, 2, dtype=F32) / MLA_ROPE))
    am = pos * inv_m[None, :]
    z32, z64 = jnp.zeros((s, 32), F32), jnp.zeros((s, 64), F32)
    mla_cos = jnp.concatenate([jnp.cos(am), jnp.cos(am), z64], axis=1)
    mla_sp = jnp.concatenate([z32, jnp.sin(am), z64], axis=1)
    mla_sn = jnp.concatenate([-jnp.sin(am), z32, z64], axis=1)
    return ret_cos, ret_sin, mla_cos, mla_sp, mla_sn


def _rope128(x, c, sg):
    return x * c + pltpu.roll(x, 64, 1) * sg


def _unrope128(d, c, sg):
    return d * c + pltpu.roll(d * sg, 64, 1)


def _rope64(t, c, sp, sn):
    return t * c + pltpu.roll(t, 96, 1) * sn + pltpu.roll(t, 32, 1) * sp


def _unrope64(d, c, sp, sn):
    return d * c + pltpu.roll(d * sn, 32, 1) + pltpu.roll(d * sp, 96, 1)


def _ret_pre(z, cos, sin, *, name, tm=256):
    s = z.shape[0]
    tm = min(tm, s)
    scale = RET_HD ** -0.5

    def body(q_ref, k_ref, c_ref, s_ref, qo_ref, ko_ref):
        c, sg = c_ref[...], s_ref[...]
        for h in range(RET_HEADS):
            sl = slice(h * RET_HD, (h + 1) * RET_HD)
            qo_ref[:, sl] = _rope128(q_ref[:, sl], c, sg)
            ko_ref[:, sl] = _rope128(k_ref[:, sl], c, sg) * scale

    seg = lambda j: pl.BlockSpec((tm, GROUP_W), lambda i: (i, j))
    tab = pl.BlockSpec((tm, RET_HD), lambda i: (i, 0))
    return pl.pallas_call(
        body, name=name, grid=(s // tm,), in_specs=[seg(0), seg(1), tab, tab],
        out_specs=[seg(0), seg(0)],
        out_shape=[jax.ShapeDtypeStruct((s, GROUP_W), F32)] * 2,
        compiler_params=_cparams("parallel"),
    )(z, z, cos, sin)


def _ret_pre_bwd(dqr, dkr, cos, sin, *, name, tm=256):
    s = dqr[0].shape[0]
    tm = min(tm, s)
    scale = RET_HD ** -0.5

    def body(dq0_ref, dq1_ref, dk0_ref, dk1_ref, c_ref, s_ref, qo_ref, ko_ref):
        c, sg = c_ref[...], s_ref[...]
        for h in range(RET_HEADS):
            sl = slice(h * RET_HD, (h + 1) * RET_HD)
            qo_ref[:, sl] = _bf(_unrope128(dq0_ref[:, sl] + dq1_ref[:, sl], c, sg))
            ko_ref[:, sl] = _bf(_unrope128(dk0_ref[:, sl] + dk1_ref[:, sl], c, sg) * scale)

    row = pl.BlockSpec((tm, GROUP_W), lambda i: (i, 0))
    tab = pl.BlockSpec((tm, RET_HD), lambda i: (i, 0))
    return pl.pallas_call(
        body, name=name, grid=(s // tm,), in_specs=[row, row, row, row, tab, tab], out_specs=[row, row],
        out_shape=[jax.ShapeDtypeStruct((s, GROUP_W), BF16)] * 2,
        compiler_params=_cparams("parallel"),
    )(dqr[0], dqr[1], dkr[0], dkr[1], cos, sin)


def _bla(a, b, c, lg, cols, *, name):
    s = a.shape[0]
    ch = min(RET_CHUNK, s)
    n = s // ch
    hd = RET_HD

    def body(lg_ref, a0, b0, c0, a1, b1, c1, o0, o1, st):
        t = pl.program_id(0)

        @pl.when(t == 0)
        def _():
            st[...] = jnp.zeros_like(st)

        ii = lax.broadcasted_iota(jnp.int32, (ch, ch), 0)
        jj = lax.broadcasted_iota(jnp.int32, (ch, ch), 1)
        idx = lax.broadcasted_iota(jnp.int32, (ch, 1), 0).astype(F32)
        for d, (a_ref, b_ref, c_ref, o_ref) in enumerate(((a0, b0, c0, o0), (a1, b1, c1, o1))):
            diff = ((ii - jj) if d == 0 else (jj - ii)).astype(F32)
            keep = diff >= 0
            dpos = jnp.maximum(diff, 0.0)
            pq = (idx + 1.0) if d == 0 else (ch - idx)
            pk = (ch - 1.0 - idx) if d == 0 else idx
            for h in range(RET_HEADS):
                g = lg_ref[d, h]
                sl = slice(h * hd, (h + 1) * hd)
                av, bv, cv = a_ref[:, sl], b_ref[:, sl], c_ref[:, sl]
                sc = _dot(av, bv, 1, 1) * jnp.where(keep, jnp.exp(dpos * g), 0.0)
                stv = st[d, h]
                o_ref[:, sl] = _dot(sc, cv) + _dot(av * jnp.exp(pq * g), stv)
                st[d, h] = jnp.exp(ch * g) * stv + _dot(bv * jnp.exp(pk * g), cv, 0, 0)

    fwd = lambda j: pl.BlockSpec((ch, GROUP_W), lambda t: (t, j))
    bwd = lambda j: pl.BlockSpec((ch, GROUP_W), lambda t: (n - 1 - t, j))
    return pl.pallas_call(
        body, name=name, grid=(n,),
        in_specs=[pl.BlockSpec(memory_space=pltpu.SMEM), fwd(cols[0]), fwd(cols[1]), fwd(cols[2]),
                  bwd(cols[0]), bwd(cols[1]), bwd(cols[2])],
        out_specs=[fwd(0), bwd(0)],
        out_shape=[jax.ShapeDtypeStruct((s, GROUP_W), F32)] * 2,
        scratch_shapes=[pltpu.VMEM((2, RET_HEADS, hd, hd), F32)],
        compiler_params=_cparams("arbitrary"),
    )(lg, a, b, c, a, b, c)


def _post(os_, zg, gcol, g, *, norm, name, tm=256):
    s = zg.shape[0]
    tm = min(tm, s)
    nd = len(os_)

    def body(*refs):
        o_refs, (gt_ref, g_ref, y_ref) = refs[:nd], refs[nd:]
        silu, _ = _silu_parts(gt_ref[...])
        for h in range(4):
            sl = slice(h * 128, (h + 1) * 128)
            o = o_refs[0][:, sl]
            for k in range(1, nd):
                o = o + o_refs[k][:, sl]
            if norm:
                r = lax.rsqrt(jnp.mean(o * o, axis=-1, keepdims=True) + EPS)
                o = o * r * g_ref[:, sl]
            y_ref[:, sl] = _bf(silu[:, sl] * o)

    row = pl.BlockSpec((tm, GROUP_W), lambda i: (i, 0))
    return pl.pallas_call(
        body, name=name, grid=(s // tm,),
        in_specs=[row] * nd + [pl.BlockSpec((tm, GROUP_W), lambda i: (i, gcol)),
                               pl.BlockSpec((1, GROUP_W), lambda i: (0, 0))],
        out_specs=row,
        out_shape=jax.ShapeDtypeStruct((s, GROUP_W), BF16),
        compiler_params=_cparams("parallel"),
    )(*os_, zg, g)


def _post_bwd(dy, ycol, os_, zg, gcol, g, *, norm, name, tm=256):
    s = zg.shape[0]
    tm = min(tm, s)
    nd = len(os_)

    def body(*refs):
        dy_ref, o_refs = refs[0], refs[1:1 + nd]
        gt_ref, g_ref, dgt_ref, do_ref, dg_ref = refs[1 + nd:]
        i = pl.program_id(0)
        silu, dsilu = _silu_parts(gt_ref[...])
        dyv = dy_ref[...]
        parts = []
        for h in range(4):
            sl = slice(h * 128, (h + 1) * 128)
            o = o_refs[0][:, sl]
            for k in range(1, nd):
                o = o + o_refs[k][:, sl]
            dn = dyv[:, sl] * silu[:, sl]
            if norm:
                r = lax.rsqrt(jnp.mean(o * o, axis=-1, keepdims=True) + EPS)
                xn = o * r
                gh = g_ref[:, sl]
                dgt_ref[:, sl] = _bf(dyv[:, sl] * (xn * gh) * dsilu[:, sl])
                parts.append(jnp.sum(dn * xn, axis=0, keepdims=True))
                dxn = dn * gh
                do_ref[:, sl] = r * (dxn - xn * jnp.mean(dxn * xn, axis=-1, keepdims=True))
            else:
                dgt_ref[:, sl] = _bf(dyv[:, sl] * o * dsilu[:, sl])
                parts.append(jnp.zeros((1, 128), F32))
                do_ref[:, sl] = dn
        part = jnp.concatenate(parts, axis=1)

        @pl.when(i == 0)
        def _():
            dg_ref[...] = part

        @pl.when(i > 0)
        def _():
            dg_ref[...] += part

    row = pl.BlockSpec((tm, GROUP_W), lambda i: (i, 0))
    vec = pl.BlockSpec((1, GROUP_W), lambda i: (0, 0))
    return pl.pallas_call(
        body, name=name, grid=(s // tm,),
        in_specs=[pl.BlockSpec((tm, GROUP_W), lambda i: (i, ycol))] + [row] * nd
        + [pl.BlockSpec((tm, GROUP_W), lambda i: (i, gcol)), vec],
        out_specs=[row, row, vec],
        out_shape=[jax.ShapeDtypeStruct((s, GROUP_W), BF16), jax.ShapeDtypeStruct((s, GROUP_W), F32),
                   jax.ShapeDtypeStruct((1, GROUP_W), F32)],
        compiler_params=_cparams("arbitrary"),
    )(dy, *os_, zg, g)


def _ret_log_gamma(swap):
    gf = 1.0 - 2.0 ** (-5.0 - jnp.arange(RET_HEADS, dtype=F32))
    lf, lb = jnp.log(gf), jnp.log(gf[::-1])
    return jnp.stack([lb, lf] if swap else [lf, lb])


def _log_sigmoid(x):
    return jnp.minimum(x, 0.0) - jnp.log(1.0 + jnp.exp(-jnp.abs(x)))


def _gla_gate(z, wa, ba, *, name, tm=256):
    s = z.shape[0]
    tm = min(tm, s)
    col = SEG["ga"][0] // 128

    def body(ga_ref, wa_ref, ba_ref, la_ref):
        pre = _dot(ga_ref[...], wa_ref[...]) + ba_ref[...]
        la_ref[...] = _log_sigmoid(pre) / GLA_TAU

    return pl.pallas_call(
        body, name=name, grid=(s // tm,),
        in_specs=[pl.BlockSpec((tm, 128), lambda i: (i, col)), pl.BlockSpec((128, 512), lambda i: (0, 0)),
                  pl.BlockSpec((1, 512), lambda i: (0, 0))],
        out_specs=pl.BlockSpec((tm, 512), lambda i: (i, 0)),
        out_shape=jax.ShapeDtypeStruct((s, 512), F32),
        compiler_params=_cparams("parallel"),
    )(z, wa, ba)


def _gla_gate_bwd(dla, z, wa, ba, *, name, tm=256):
    s = z.shape[0]
    tm = min(tm, s)
    col = SEG["ga"][0] // 128

    def body(dla_ref, ga_ref, wa_ref, ba_ref, dga_ref, dwa_ref, dba_ref):
        i = pl.program_id(0)
        gav = ga_ref[...]
        pre = _dot(gav, wa_ref[...]) + ba_ref[...]
        dpre = dla_ref[...] * (1.0 - _sigmoid(pre)) * (1.0 / GLA_TAU)
        dga_ref[...] = _bf(_dot(dpre, wa_ref[...], 1, 1))
        pw = _dot(gav, dpre, 0, 0)
        pb = jnp.sum(dpre, axis=0, keepdims=True)

        @pl.when(i == 0)
        def _():
            dwa_ref[...] = pw
            dba_ref[...] = pb

        @pl.when(i > 0)
        def _():
            dwa_ref[...] += pw
            dba_ref[...] += pb

    return pl.pallas_call(
        body, name=name, grid=(s // tm,),
        in_specs=[pl.BlockSpec((tm, 512), lambda i: (i, 0)), pl.BlockSpec((tm, 128), lambda i: (i, col)),
                  pl.BlockSpec((128, 512), lambda i: (0, 0)), pl.BlockSpec((1, 512), lambda i: (0, 0))],
        out_specs=[pl.BlockSpec((tm, 128), lambda i: (i, 0)), pl.BlockSpec((128, 512), lambda i: (0, 0)),
                   pl.BlockSpec((1, 512), lambda i: (0, 0))],
        out_shape=[jax.ShapeDtypeStruct((s, 128), BF16), jax.ShapeDtypeStruct((128, 512), F32),
                   jax.ShapeDtypeStruct((1, 512), F32)],
        compiler_params=_cparams("arbitrary"),
    )(dla, z, wa, ba)


def _gla_masks(ch):
    ii = lax.broadcasted_iota(jnp.int32, (ch, ch), 0)
    tt = lax.broadcasted_iota(jnp.int32, (ch, ch), 1)
    return jnp.where(tt <= ii, 1.0, 0.0), jnp.where(tt >= ii, 1.0, 0.0)


def _gla_chunk(d, tmat, qv, kv, lav, ch):
    c = _split_dot(tmat, lav)
    big_l = c[ch - 1:ch, :] if d == 0 else c[0:1, :]
    qt = qv * (GLA_DK ** -0.5) * jnp.exp(c)
    kt = kv * jnp.exp(-c)
    kh = kv * jnp.exp(big_l - c)
    return c, big_l, qt, kt, kh


def _gla_fwd(qh, kh_, z, la, *, name):
    s = z.shape[0]
    ch = min(GLA_CHUNK, s)
    n = s // ch
    vcol = SEG["gv"][0] // GROUP_W

    def body(q0, k0, v0, la0, q1, k1, v1, la1, o0, o1, zs0, zs1, st):
        t = pl.program_id(0)

        @pl.when(t == 0)
        def _():
            st[...] = jnp.zeros_like(st)

        masks = _gla_masks(ch)
        for d, (q_ref, k_ref, v_ref, la_ref, o_ref, zs_ref) in enumerate(
                ((q0, k0, v0, la0, o0, zs0), (q1, k1, v1, la1, o1, zs1))):
            for h in range(GLA_HEADS):
                c, big_l, qt, kt, kh = _gla_chunk(d, masks[d], q_ref[h], k_ref[h], la_ref[0, h], ch)
                vv = v_ref[:, h * GLA_DV:(h + 1) * GLA_DV]
                p = _dot(qt, kt, 1, 1) * masks[d]
                zst = st[d, h]
                o_ref[:, h * GLA_DV:(h + 1) * GLA_DV] = _dot(p, vv) + _dot(qt, zst, 1, 1)
                zs_ref[h, 0] = zst
                st[d, h] = zst * jnp.exp(big_l) + _dot(vv, kh, 0, 0)

    cidx = (lambda t: t), (lambda t: n - 1 - t)
    hs = lambda d: pl.BlockSpec((GLA_HEADS, ch, GLA_DK), lambda t: (0, cidx[d](t), 0))
    vs = lambda d: pl.BlockSpec((ch, GROUP_W), lambda t: (cidx[d](t), vcol))
    las = lambda d: pl.BlockSpec((1, GLA_HEADS, ch, GLA_DK), lambda t: (d, 0, cidx[d](t), 0))
    os_ = lambda d: pl.BlockSpec((ch, GROUP_W), lambda t: (cidx[d](t), 0))
    zss = lambda d: pl.BlockSpec((GLA_HEADS, 1, GLA_DV, GLA_DK), lambda t: (0, cidx[d](t), 0, 0))
    o0, o1, zs0, zs1 = pl.pallas_call(
        body, name=name, grid=(n,),
        in_specs=[hs(0), hs(0), vs(0), las(0), hs(1), hs(1), vs(1), las(1)],
        out_specs=[os_(0), os_(1), zss(0), zss(1)],
        out_shape=[jax.ShapeDtypeStruct((s, GROUP_W), F32)] * 2
        + [jax.ShapeDtypeStruct((GLA_HEADS, n, GLA_DV, GLA_DK), F32)] * 2,
        scratch_shapes=[pltpu.VMEM((2, GLA_HEADS, GLA_DV, GLA_DK), F32)],
        compiler_params=_cparams("arbitrary"),
    )(qh, kh_, z, la, qh, kh_, z, la)
    return (o0, o1), (zs0, zs1)


def _gla_bwd(qh, kh_, z, la, do, zs, *, name):
    s = z.shape[0]
    ch = min(GLA_CHUNK, s)
    n = s // ch
    vcol = SEG["gv"][0] // GROUP_W

    def body(q0, k0, v0, la0, do0, zs0, q1, k1, v1, la1, do1, zs1,
             dq0, dk0, dla0, dv0, dq1, dk1, dla1, dv1, gz):
        t = pl.program_id(0)

        @pl.when(t == 0)
        def _():
            gz[...] = jnp.zeros_like(gz)

        masks = _gla_masks(ch)
        rows = lax.broadcasted_iota(jnp.int32, (ch, 1), 0)
        for d, (q_ref, k_ref, v_ref, la_ref, do_ref, zs_ref, dq_ref, dk_ref, dla_ref, dv_ref) in enumerate(
                ((q0, k0, v0, la0, do0, zs0, dq0, dk0, dla0, dv0), (q1, k1, v1, la1, do1, zs1, dq1, dk1, dla1, dv1))):
            tmat = masks[d]
            end = ch - 1 if d == 0 else 0
            for h in range(GLA_HEADS):
                c, big_l, qt, kt, kh = _gla_chunk(d, tmat, q_ref[h], k_ref[h], la_ref[0, h], ch)
                vsl = slice(h * GLA_DV, (h + 1) * GLA_DV)
                vv, dov, zst, gzv = v_ref[:, vsl], do_ref[:, vsl], zs_ref[h, 0], gz[d, h]
                p = _dot(qt, kt, 1, 1) * tmat
                dp = _dot(dov, vv, 1, 1) * tmat
                dqt = _dot(dp, kt) + _dot(dov, zst)
                dkt = _dot(dp, qt, 0, 0)
                dkh = _dot(vv, gzv)
                dv_ref[:, vsl] = _dot(p, dov, 0, 0) + _dot(kh, gzv, 1, 1)
                dq_ref[h] = dqt * jnp.exp(c) * (GLA_DK ** -0.5)
                dk_ref[h] = dkt * jnp.exp(-c) + dkh * jnp.exp(big_l - c)
                e_l = jnp.exp(big_l)
                d_l = jnp.sum(dkh * kh, axis=0, keepdims=True) + e_l * jnp.sum(zst * gzv, axis=0, keepdims=True)
                dc = dqt * qt - dkt * kt - dkh * kh + jnp.where(rows == end, d_l, 0.0)
                dla_ref[h] = _split_dot(tmat, dc, 0, 0)
                gz[d, h] = gzv * e_l + _dot(dov, qt, 0, 0)

    cidx = (lambda t: n - 1 - t), (lambda t: t)
    hs = lambda d: pl.BlockSpec((GLA_HEADS, ch, GLA_DK), lambda t: (0, cidx[d](t), 0))
    vs = lambda d: pl.BlockSpec((ch, GROUP_W), lambda t: (cidx[d](t), vcol))
    las = lambda d: pl.BlockSpec((1, GLA_HEADS, ch, GLA_DK), lambda t: (d, 0, cidx[d](t), 0))
    row = lambda d: pl.BlockSpec((ch, GROUP_W), lambda t: (cidx[d](t), 0))
    zss = lambda d: pl.BlockSpec((GLA_HEADS, 1, GLA_DV, GLA_DK), lambda t: (0, cidx[d](t), 0, 0))
    hshape = jax.ShapeDtypeStruct((GLA_HEADS, s, GLA_DK), F32)
    wide = jax.ShapeDtypeStruct((s, GROUP_W), F32)
    outs = pl.pallas_call(
        body, name=name, grid=(n,),
        in_specs=[hs(0), hs(0), vs(0), las(0), row(0), zss(0), hs(1), hs(1), vs(1), las(1), row(1), zss(1)],
        out_specs=[hs(0), hs(0), hs(0), row(0), hs(1), hs(1), hs(1), row(1)],
        out_shape=[hshape, hshape, hshape, wide, hshape, hshape, hshape, wide],
        scratch_shapes=[pltpu.VMEM((2, GLA_HEADS, GLA_DV, GLA_DK), F32)],
        compiler_params=_cparams("arbitrary"),
    )(qh, kh_, z, la, do, zs[0], qh, kh_, z, la, do, zs[1])
    dq0, dk0, dla0, dv0, dq1, dk1, dla1, dv1 = outs
    return (dq0, dq1), (dk0, dk1), (dla0, dla1), (dv0, dv1)


def _band(lo, hi, rows, width):
    r = lax.broadcasted_iota(jnp.int32, (rows, width), 0)
    j = lax.broadcasted_iota(jnp.int32, (rows, width), 1)
    k = j - POOL_HALO - r
    return jnp.where((k >= lo) & (k <= hi), 1.0, 0.0)


def _pool_cnt(t0, half, rows, s):
    t = t0 + lax.broadcasted_iota(jnp.int32, (rows, 1), 0)
    return (jnp.minimum(t + half, s) - jnp.maximum(t - half, 0)).astype(F32)


def _pool_fwd(z, pw, scale, *, name):
    s = z.shape[0]
    tl = min(POOL_TILE, s)
    nt = s // tl
    ucol, gcol = SEG["pv"][0] // 128, SEG["pg"][0] // 128

    def body(u_ref, gt_ref, pw_ref, sc_ref, y_ref, pad):
        g = pl.program_id(0)
        half = jnp.left_shift(1, g)
        pad[0:POOL_HALO, :] = jnp.zeros((POOL_HALO, POOL_GW), F32)
        pad[POOL_HALO + s:POOL_HALO + s + POOL_HALO, :] = jnp.zeros((POOL_HALO, POOL_GW), F32)
        pad[POOL_HALO:POOL_HALO + s, :] = u_ref[...]
        band = _band(-half, half - 1, tl, tl + 2 * POOL_HALO)
        pwv, scv = pw_ref[0], sc_ref[...]

        def tile(i, carry):
            t0 = pl.multiple_of(i * tl, tl)
            win = pad[pl.ds(t0, tl + 2 * POOL_HALO), :]
            u = win[POOL_HALO:POOL_HALO + tl, :]
            pooled = _split_dot(band, win) / _pool_cnt(t0, half, tl, s) - u
            mixed = _dot(pooled, pwv)
            silu, _ = _silu_parts(gt_ref[pl.ds(t0, tl), :])
            y_ref[pl.ds(t0, tl), :] = _bf(silu * (mixed * scv))
            return carry

        lax.fori_loop(0, nt, tile, 0)

    return pl.pallas_call(
        body, name=name, grid=(POOL_GROUPS,),
        in_specs=[pl.BlockSpec((s, POOL_GW), lambda g: (0, ucol + g)),
                  pl.BlockSpec((s, POOL_GW), lambda g: (0, gcol + g)),
                  pl.BlockSpec((1, POOL_GW, POOL_GW), lambda g: (g, 0, 0)),
                  pl.BlockSpec((1, POOL_GW), lambda g: (0, g))],
        out_specs=pl.BlockSpec((s, POOL_GW), lambda g: (0, g)),
        out_shape=jax.ShapeDtypeStruct((s, GROUP_W), BF16),
        scratch_shapes=[pltpu.VMEM((s + 2 * POOL_HALO, POOL_GW), F32)],
        compiler_params=_cparams("parallel"),
    )(z, z, pw, scale)


def _pool_bwd(dy, z, pw, scale, *, name):
    s = z.shape[0]
    tl = min(POOL_TILE, s)
    nt = s // tl
    ucol, gcol, ycol = SEG["pv"][0] // 128, SEG["pg"][0] // 128, 2 * GROUP_W // 128

    def body(dy_ref, u_ref, gt_ref, pw_ref, sc_ref, du_ref, dgt_ref, dpw_ref, dsc_ref, pad, epad, dpo):
        g = pl.program_id(0)
        half = jnp.left_shift(1, g)
        zeros = jnp.zeros((POOL_HALO, POOL_GW), F32)
        for buf in (pad, epad):
            buf[0:POOL_HALO, :] = zeros
            buf[POOL_HALO + s:POOL_HALO + s + POOL_HALO, :] = zeros
        pad[POOL_HALO:POOL_HALO + s, :] = u_ref[...]
        band = _band(-half, half - 1, tl, tl + 2 * POOL_HALO)
        band_t = _band(1 - half, half, tl, tl + 2 * POOL_HALO)
        pwv, scv = pw_ref[0], sc_ref[...]
        dpw_ref[0] = jnp.zeros((POOL_GW, POOL_GW), F32)
        dsc_ref[...] = jnp.zeros((1, POOL_GW), F32)

        def tile(i, carry):
            t0 = pl.multiple_of(i * tl, tl)
            win = pad[pl.ds(t0, tl + 2 * POOL_HALO), :]
            u = win[POOL_HALO:POOL_HALO + tl, :]
            cnt = _pool_cnt(t0, half, tl, s)
            pooled = _split_dot(band, win) / cnt - u
            mixed = _dot(pooled, pwv)
            silu, dsilu = _silu_parts(gt_ref[pl.ds(t0, tl), :])
            dyv = dy_ref[pl.ds(t0, tl), :]
            dgt_ref[pl.ds(t0, tl), :] = _bf(dyv * (mixed * scv) * dsilu)
            dsc_ref[...] += jnp.sum(dyv * silu * mixed, axis=0, keepdims=True)
            dm = dyv * silu * scv
            dpw_ref[0] += _dot(pooled, dm, 0, 0)
            dpooled = _dot(dm, pwv, 1, 1)
            dpo[pl.ds(t0, tl), :] = dpooled
            epad[pl.ds(POOL_HALO + t0, tl), :] = dpooled / cnt
            return carry

        lax.fori_loop(0, nt, tile, 0)

        def tile2(i, carry):
            t0 = pl.multiple_of(i * tl, tl)
            ewin = epad[pl.ds(t0, tl + 2 * POOL_HALO), :]
            du_ref[pl.ds(t0, tl), :] = _bf(_split_dot(band_t, ewin) - dpo[pl.ds(t0, tl), :])
            return carry

        lax.fori_loop(0, nt, tile2, 0)

    col = lambda c0: pl.BlockSpec((s, POOL_GW), lambda g: (0, c0 + g))
    return pl.pallas_call(
        body, name=name, grid=(POOL_GROUPS,),
        in_specs=[col(ycol), col(ucol), col(gcol), pl.BlockSpec((1, POOL_GW, POOL_GW), lambda g: (g, 0, 0)),
                  pl.BlockSpec((1, POOL_GW), lambda g: (0, g))],
        out_specs=[col(0), col(0), pl.BlockSpec((1, POOL_GW, POOL_GW), lambda g: (g, 0, 0)),
                   pl.BlockSpec((1, POOL_GW), lambda g: (0, g))],
        out_shape=[jax.ShapeDtypeStruct((s, GROUP_W), BF16), jax.ShapeDtypeStruct((s, GROUP_W), BF16),
                   jax.ShapeDtypeStruct((POOL_GROUPS, POOL_GW, POOL_GW), F32),
                   jax.ShapeDtypeStruct((1, GROUP_W), F32)],
        scratch_shapes=[pltpu.VMEM((s + 2 * POOL_HALO, POOL_GW), F32), pltpu.VMEM((s + 2 * POOL_HALO, POOL_GW), F32),
                        pltpu.VMEM((s, POOL_GW), F32)],
        compiler_params=_cparams("parallel"),
    )(dy, z, z, pw, scale)


def _mla_specs(tm):
    zq = pl.BlockSpec((tm, 512), lambda i: (i, SEG["mq"][0] // 512))
    zkv = pl.BlockSpec((tm, 256), lambda i: (i, SEG["mkv"][0] // 256))
    zkr = pl.BlockSpec((tm, 128), lambda i: (i, SEG["mkr"][0] // 128))
    full = lambda r, c: pl.BlockSpec((r, c), lambda i: (0, 0))
    tab = pl.BlockSpec((tm, 128), lambda i: (i, 0))
    weights = [full(1, 512), full(512, 1024), full(1, 256), full(256, 1024), full(1, 256), full(1, 256)]
    return [zq, zkv, zkr] + weights + [tab, tab, tab]


def _mla_project(xq_ref, xkv_ref, qg_ref, wq_ref, kvg_ref, wkv_ref):
    xq = xq_ref[...]
    r1 = lax.rsqrt(jnp.mean(xq * xq, axis=-1, keepdims=True) + EPS)
    xn1 = xq * r1
    qn = _bf(xn1 * qg_ref[...])
    qraw = _dot(qn, wq_ref[...])
    xkv = xkv_ref[...]
    r2 = lax.rsqrt(jnp.mean(xkv * xkv, axis=-1, keepdims=True) + EPS)
    xn2 = xkv * r2
    kvn = _bf(xn2 * kvg_ref[...])
    kvraw = _dot(kvn, wkv_ref[...])
    return r1, xn1, qn, qraw, r2, xn2, kvn, kvraw


def _mla_pre(z, qg, wq, kvg, wkv, qng, kng, cos, sp, sn, *, name, tm=256):
    s = z.shape[0]
    tm = min(tm, s)

    def body(xq_ref, xkv_ref, pe_ref, qg_ref, wq_ref, kvg_ref, wkv_ref, qng_ref, kng_ref, c_ref, sp_ref, sn_ref,
             q_ref, k_ref, v_ref):
        _, _, _, qraw, _, _, _, kvraw = _mla_project(xq_ref, xkv_ref, qg_ref, wq_ref, kvg_ref, wkv_ref)
        c, spv, snv = c_ref[...], sp_ref[...], sn_ref[...]
        pe = pe_ref[...]
        pe_ss = jnp.sum(pe * pe, axis=-1, keepdims=True)
        qngv, kngv = qng_ref[...], kng_ref[...]
        for h in range(MLA_HEADS):
            b = h * MLA_QKP
            qh = qraw[:, b:b + MLA_QKP]
            r = lax.rsqrt(jnp.sum(qh * qh, axis=-1, keepdims=True) * (1.0 / MLA_QK) + EPS)
            qn_h = qh * r * qngv
            q_ref[:, b:b + 128] = _bf(qn_h[:, :128] * MLA_SCALE)
            q_ref[:, b + 128:b + 256] = _bf(_rope64(qn_h[:, 128:], c, spv, snv) * MLA_SCALE)
            kn = kvraw[:, b:b + 128]
            rk = lax.rsqrt((jnp.sum(kn * kn, axis=-1, keepdims=True) + pe_ss) * (1.0 / MLA_QK) + EPS)
            k_ref[:, b:b + 128] = _bf(kn * rk * kngv[:, :128])
            k_ref[:, b + 128:b + 256] = _bf(_rope64(pe * rk * kngv[:, 128:], c, spv, snv))
            v_ref[:, h * MLA_V:(h + 1) * MLA_V] = _bf(kvraw[:, b + 128:b + 256])

    row = lambda w: pl.BlockSpec((tm, w), lambda i: (i, 0))
    return pl.pallas_call(
        body, name=name, grid=(s // tm,), in_specs=_mla_specs(tm),
        out_specs=[row(1024), row(1024), row(512)],
        out_shape=[jax.ShapeDtypeStruct((s, 1024), BF16), jax.ShapeDtypeStruct((s, 1024), BF16),
                   jax.ShapeDtypeStruct((s, 512), BF16)],
        compiler_params=_cparams("parallel"),
    )(z, z, z, qg, wq, kvg, wkv, qng, kng, cos, sp, sn)


def _mla_pre_bwd(dq, dk, dv, z, qg, wq, kvg, wkv, qng, kng, cos, sp, sn, *, name, tm=256):
    s = z.shape[0]
    tm = min(tm, s)

    def body(dq_ref, dk_ref, dv_ref, xq_ref, xkv_ref, pe_ref, qg_ref, wq_ref, kvg_ref, wkv_ref, qng_ref, kng_ref,
             c_ref, sp_ref, sn_ref, dxq_ref, dxkv_ref, dpe_ref, dwq_ref, dwkv_ref, dqg_ref, dkvg_ref, dqng_ref,
             dkng_ref, dqraw, dkvraw):
        i = pl.program_id(0)
        r1, xn1, qn, qraw, r2, xn2, kvn, kvraw = _mla_project(xq_ref, xkv_ref, qg_ref, wq_ref, kvg_ref, wkv_ref)
        c, spv, snv = c_ref[...], sp_ref[...], sn_ref[...]
        pe = pe_ref[...]
        pe_ss = jnp.sum(pe * pe, axis=-1, keepdims=True)
        qngv, kngv = qng_ref[...], kng_ref[...]
        dqng = jnp.zeros((1, MLA_QKP), F32)
        dkng = jnp.zeros((1, MLA_QKP), F32)
        dpe = jnp.zeros_like(pe)
        for h in range(MLA_HEADS):
            b = h * MLA_QKP
            qh = qraw[:, b:b + MLA_QKP]
            r = lax.rsqrt(jnp.sum(qh * qh, axis=-1, keepdims=True) * (1.0 / MLA_QK) + EPS)
            xn = qh * r
            d_n = jnp.concatenate(
                [dq_ref[:, b:b + 128], _unrope64(dq_ref[:, b + 128:b + 256], c, spv, snv)], axis=1) * MLA_SCALE
            dqng = dqng + jnp.sum(d_n * xn, axis=0, keepdims=True)
            dxn = d_n * qngv
            dqraw[:, b:b + MLA_QKP] = _bf(r * (dxn - xn * (jnp.sum(dxn * xn, axis=-1, keepdims=True) * (1.0 / MLA_QK))))
            kn = kvraw[:, b:b + 128]
            rk = lax.rsqrt((jnp.sum(kn * kn, axis=-1, keepdims=True) + pe_ss) * (1.0 / MLA_QK) + EPS)
            xk = jnp.concatenate([kn, pe], axis=1) * rk
            d_k = jnp.concatenate(
                [dk_ref[:, b:b + 128], _unrope64(dk_ref[:, b + 128:b + 256], c, spv, snv)], axis=1)
            dkng = dkng + jnp.sum(d_k * xk, axis=0, keepdims=True)
            dxk = d_k * kngv
            dfull = rk * (dxk - xk * (jnp.sum(dxk * xk, axis=-1, keepdims=True) * (1.0 / MLA_QK)))
            dkvraw[:, b:b + 128] = _bf(dfull[:, :128])
            dkvraw[:, b + 128:b + 256] = _bf(dv_ref[:, h * MLA_V:(h + 1) * MLA_V])
            dpe = dpe + dfull[:, 128:]
        dpe_ref[...] = _bf(dpe)
        dqr, dkvr = dqraw[...], dkvraw[...]
        dqn = _dot(dqr, wq_ref[...], 1, 1)
        dxn1 = dqn * qg_ref[...]
        dxq_ref[...] = _bf(r1 * (dxn1 - xn1 * jnp.mean(dxn1 * xn1, axis=-1, keepdims=True)))
        dkvn = _dot(dkvr, wkv_ref[...], 1, 1)
        dxn2 = dkvn * kvg_ref[...]
        dxkv_ref[...] = _bf(r2 * (dxn2 - xn2 * jnp.mean(dxn2 * xn2, axis=-1, keepdims=True)))
        parts = (_dot(qn, dqr, 0, 0), _dot(kvn, dkvr, 0, 0), jnp.sum(dqn * xn1, axis=0, keepdims=True),
                 jnp.sum(dkvn * xn2, axis=0, keepdims=True), dqng, dkng)
        accs = (dwq_ref, dwkv_ref, dqg_ref, dkvg_ref, dqng_ref, dkng_ref)

        @pl.when(i == 0)
        def _():
            for a, p in zip(accs, parts):
                a[...] = p

        @pl.when(i > 0)
        def _():
            for a, p in zip(accs, parts):
                a[...] += p

    row = lambda w: pl.BlockSpec((tm, w), lambda i: (i, 0))
    full = lambda r, c: pl.BlockSpec((r, c), lambda i: (0, 0))
    return pl.pallas_call(
        body, name=name, grid=(s // tm,),
        in_specs=[row(1024), row(1024), row(512)] + _mla_specs(tm),
        out_specs=[row(512), row(256), row(128), full(512, 1024), full(256, 1024), full(1, 512), full(1, 256),
                   full(1, 256), full(1, 256)],
        out_shape=[jax.ShapeDtypeStruct((s, 512), BF16), jax.ShapeDtypeStruct((s, 256), BF16),
                   jax.ShapeDtypeStruct((s, 128), BF16), jax.ShapeDtypeStruct((512, 1024), F32),
                   jax.ShapeDtypeStruct((256, 1024), F32), jax.ShapeDtypeStruct((1, 512), F32),
                   jax.ShapeDtypeStruct((1, 256), F32), jax.ShapeDtypeStruct((1, 256), F32),
                   jax.ShapeDtypeStruct((1, 256), F32)],
        scratch_shapes=[pltpu.VMEM((tm, 1024), BF16), pltpu.VMEM((tm, 1024), BF16)],
        compiler_params=_cparams("arbitrary"),
    )(dq, dk, dv, z, z, z, qg, wq, kvg, wkv, qng, kng, cos, sp, sn)


def _flash_fwd(q, k, v, *, name, tq=1024, tk=1024, rider=None):
    s = q.shape[0]
    tq, tk = min(tq, s), min(tk, s)
    nk = s // tk

    def body(q_ref, k_ref, v_ref, o_ref, lse_ref, m_s, l_s, acc):
        j = pl.program_id(2)

        @pl.when(j == 0)
        def _():
            m_s[...] = jnp.full_like(m_s, -jnp.inf)
            l_s[...] = jnp.zeros_like(l_s)
            acc[...] = jnp.zeros_like(acc)

        sc = _dot(q_ref[...], k_ref[...], 1, 1)
        m_prev = m_s[...]
        m_new = jnp.maximum(m_prev, jnp.max(sc, axis=-1, keepdims=True))
        p = jnp.exp(sc - m_new[:, 0:1])
        alpha = jnp.exp(m_prev - m_new)
        l_s[...] = alpha * l_s[...] + jnp.sum(p, axis=-1, keepdims=True)
        acc[...] = alpha * acc[...] + _dot(p, v_ref[...])
        m_s[...] = m_new

        @pl.when(j == nk - 1)
        def _():
            o_ref[...] = acc[...] / l_s[...]
            lse_ref[...] = m_s[...] + jnp.log(l_s[...])

    (o, lse), rode = _ride_call(
        body, rider, name=name, grid=(MLA_HEADS, s // tq, nk),
        in_specs=[pl.BlockSpec((tq, MLA_QKP), lambda h, i, j: (i, h)),
                  pl.BlockSpec((tk, MLA_QKP), lambda h, i, j: (j, h)),
                  pl.BlockSpec((tk, MLA_V), lambda h, i, j: (j, h))],
        out_specs=[pl.BlockSpec((tq, MLA_V), lambda h, i, j: (i, h))] * 2,
        out_shape=[jax.ShapeDtypeStruct((s, GROUP_W), F32)] * 2,
        scratch_shapes=[pltpu.VMEM((tq, MLA_V), F32), pltpu.VMEM((tq, MLA_V), F32), pltpu.VMEM((tq, MLA_V), F32)],
        args=(q, k, v), sem=("parallel", "parallel", "arbitrary"))
    return (o, lse) if rider is None else (o, lse, rode)


def _flash_bwd(q, k, v, do, o, lse, *, name, tq=1024, tk=1024, rider=None):
    s = q.shape[0]
    tq, tk = min(tq, s), min(tk, s)
    nq, nk = s // tq, s // tk

    def body(q_ref, k_ref, v_ref, do_ref, o_ref, lse_ref, dq_ref, dk_ref, dv_ref, dk_acc, dv_acc):
        j, i = pl.program_id(1), pl.program_id(2)
        dov = do_ref[...]
        delta = jnp.sum(dov * o_ref[...], axis=-1, keepdims=True)
        p = jnp.exp(_dot(q_ref[...], k_ref[...], 1, 1) - lse_ref[:, 0:1])
        ds = p * (_dot(dov, v_ref[...], 1, 1) - delta)
        pv = _dot(p, dov, 0, 0)
        pk = _dot(ds, q_ref[...], 0, 0)
        pq = _dot(ds, k_ref[...])
        rows = pl.ds(pl.multiple_of(i * tq, tq), tq)

        @pl.when(j == 0)
        def _():
            dq_ref[rows, :] = pq

        @pl.when(j > 0)
        def _():
            dq_ref[rows, :] += pq

        @pl.when(i == 0)
        def _():
            dv_acc[...] = pv
            dk_acc[...] = pk

        @pl.when(i > 0)
        def _():
            dv_acc[...] += pv
            dk_acc[...] += pk

        @pl.when(i == nq - 1)
        def _():
            dk_ref[...] = dk_acc[...]
            dv_ref[...] = dv_acc[...]

    qb = pl.BlockSpec((tq, MLA_QKP), lambda h, j, i: (i, h))
    kb = pl.BlockSpec((tk, MLA_QKP), lambda h, j, i: (j, h))
    vb = pl.BlockSpec((tk, MLA_V), lambda h, j, i: (j, h))
    ob = pl.BlockSpec((tq, MLA_V), lambda h, j, i: (i, h))
    (dq, dk, dv), rode = _ride_call(
        body, rider, name=name, grid=(MLA_HEADS, nk, nq),
        in_specs=[qb, kb, vb, ob, ob, ob],
        out_specs=[pl.BlockSpec((s, MLA_QKP), lambda h, j, i: (0, h)), kb, vb],
        out_shape=[jax.ShapeDtypeStruct((s, MLA_HEADS * MLA_QKP), F32),
                   jax.ShapeDtypeStruct((s, MLA_HEADS * MLA_QKP), F32), jax.ShapeDtypeStruct((s, GROUP_W), F32)],
        scratch_shapes=[pltpu.VMEM((tk, MLA_QKP), F32), pltpu.VMEM((tk, MLA_V), F32)],
        args=(q, k, v, do, o, lse), sem=("arbitrary", "arbitrary", "arbitrary"))
    return (dq, dk, dv) if rider is None else (dq, dk, dv, rode)


def _rows_tile(r, c, itemsize=4, budget=2 * 1024 * 1024):
    if r * c * itemsize <= budget:
        return r
    best = None
    for t in range(8, r, 8):
        if r % t == 0 and t * c * itemsize <= budget:
            best = t
    return best if best is not None else r


def _add_n(arrs, *, out_dtype=F32, name):
    shape = arrs[0].shape
    c = shape[-1]
    flat = [a.reshape(-1, c) for a in arrs]
    r = flat[0].shape[0]
    t = _rows_tile(r, c)

    def body(*refs):
        acc = refs[0][...].astype(F32)
        for ref in refs[1:-1]:
            acc = acc + ref[...].astype(F32)
        refs[-1][...] = acc.astype(out_dtype)

    blk = pl.BlockSpec((t, c), lambda i: (i, 0))
    out = pl.pallas_call(
        body, name=name, grid=(r // t,), in_specs=[blk] * len(flat), out_specs=blk,
        out_shape=jax.ShapeDtypeStruct((r, c), out_dtype), compiler_params=_cparams("parallel"),
    )(*flat)
    return out.reshape(shape)


def _adamw(w, g, m, v, *, name):
    shape = w.shape
    c = shape[-1]
    flat = [a.reshape(-1, c) for a in (w, g, m, v)]
    r = flat[0].shape[0]
    t = _rows_tile(r, c, budget=1024 * 1024)

    def body(w_ref, g_ref, m_ref, v_ref, d_ref, mo_ref, vo_ref):
        gv = g_ref[...]
        m2 = ADAM_B1 * m_ref[...] + (1.0 - ADAM_B1) * gv
        v2 = ADAM_B2 * v_ref[...] + (1.0 - ADAM_B2) * (gv * gv)
        m_hat = m2 / (1.0 - ADAM_B1 ** ADAM_STEP)
        v_hat = v2 / (1.0 - ADAM_B2 ** ADAM_STEP)
        d_ref[...] = -ADAM_LR * (m_hat / (jnp.sqrt(v_hat) + ADAM_EPS) + ADAM_WD * w_ref[...])
        mo_ref[...] = m2
        vo_ref[...] = v2

    blk = pl.BlockSpec((t, c), lambda i: (i, 0))
    outs = pl.pallas_call(
        body, name=name, grid=(r // t,), in_specs=[blk] * 4, out_specs=[blk] * 3,
        out_shape=[jax.ShapeDtypeStruct((r, c), F32)] * 3, compiler_params=_cparams("parallel"),
    )(*flat)
    return tuple(o.reshape(shape) for o in outs)


def _place():
    x, y, c = lax.axis_index("x"), lax.axis_index("y"), lax.axis_index("c")
    chips = [(1 - x, y), (x, 1 - y), (1 - x, 1 - y)]
    return x, y, c, chips


ANY = pl.BlockSpec(memory_space=pl.ANY)


def _gather_shards(shards, *, name):
    nt = len(shards)

    def body(*refs):
        src, dst = refs[:nt], refs[nt:2 * nt]
        send, recv, fsend, frecv, lsem = refs[2 * nt:]
        x, y, c, chips = _place()
        me = 2 * x + y
        local = [pltpu.make_async_copy(src[t], dst[t].at[me], lsem.at[t]) for t in range(nt)]
        for cp in local:
            cp.start()

        def half(t, slot, hc):
            hr = src[t].shape[0] // 2
            return dst[t].at[slot, pl.ds(hc * hr, hr)]

        def first(t, k):
            hr = src[t].shape[0] // 2
            return pltpu.make_async_remote_copy(
                src_ref=src[t].at[pl.ds(c * hr, hr)], dst_ref=half(t, me, c),
                send_sem=send.at[t, k], recv_sem=recv.at[t, k],
                device_id=(chips[k][0], chips[k][1], c), device_id_type=MESH)

        def landed(t, k):
            slot = 2 * chips[k][0] + chips[k][1]
            return pltpu.make_async_remote_copy(
                src_ref=half(t, slot, c), dst_ref=half(t, slot, c),
                send_sem=send.at[t, k], recv_sem=recv.at[t, k],
                device_id=(chips[k][0], chips[k][1], c), device_id_type=MESH)

        def forward(t, k, hc):
            slot = 2 * chips[k][0] + chips[k][1]
            return pltpu.make_async_remote_copy(
                src_ref=half(t, slot, hc), dst_ref=half(t, slot, hc),
                send_sem=fsend.at[t, k], recv_sem=frecv.at[t, k],
                device_id=(x, y, 1 - c), device_id_type=MESH)

        for t in range(nt):
            for k in range(3):
                first(t, k).start()
        for t in range(nt):
            for k in range(3):
                landed(t, k).wait_recv()
                forward(t, k, c).start()
        for t in range(nt):
            for k in range(3):
                forward(t, k, 1 - c).wait_recv()
        for t in range(nt):
            for k in range(3):
                first(t, k).wait_send()
                forward(t, k, c).wait_send()
        for cp in local:
            cp.wait()

    return pl.pallas_call(
        body, name=name, in_specs=[ANY] * nt, out_specs=[ANY] * nt,
        out_shape=[jax.ShapeDtypeStruct((N_CHIP,) + a.shape, a.dtype) for a in shards],
        scratch_shapes=[pltpu.SemaphoreType.DMA((nt, 3)), pltpu.SemaphoreType.DMA((nt, 3)),
                        pltpu.SemaphoreType.DMA((nt, 3)), pltpu.SemaphoreType.DMA((nt, 3)),
                        pltpu.SemaphoreType.DMA((nt,))],
    )(*shards)


def _comm_rows(hr, c, budget=2 * 1024 * 1024):
    if hr * c * 4 <= budget:
        return hr
    best = None
    for t in range(16, hr, 16):
        if hr % t == 0 and t * c * 4 <= budget:
            best = t
    return best if best is not None else hr


def _pair_reduce(g, where, *, out_dtype, name):
    n_slot, r, cdim = g.shape
    hr = r // 2
    rc = _comm_rows(hr, cdim)
    nr = hr // rc
    steps = n_slot * nr
    g4 = g.reshape(n_slot, 2, hr, cdim)

    def body(w_ref, a_ref, b_ref, o_ref, land, send, recv, credit):
        x, y, c, _ = _place()
        sib = (x, y, 1 - c)
        i = pl.program_id(0) * nr + pl.program_id(1)
        s = lax.rem(i, 2)

        @pl.when(i >= 2)
        def _():
            pl.semaphore_wait(credit.at[s], 1)

        cp = pltpu.make_async_remote_copy(src_ref=b_ref.at[0, 0], dst_ref=land.at[s], send_sem=send.at[s],
                                          recv_sem=recv.at[s], device_id=sib, device_id_type=MESH)
        cp.start()
        cp.wait_recv()
        o_ref[0] = (a_ref[0, 0] + land[s]).astype(out_dtype)
        cp.wait_send()

        @pl.when(i + 2 < steps)
        def _():
            pl.semaphore_signal(credit.at[s], inc=1, device_id=sib, device_id_type=MESH)

    blk = lambda half: pl.BlockSpec((1, 1, rc, cdim), lambda j, t, w: (j, half(w), t, 0))
    grid_spec = pltpu.PrefetchScalarGridSpec(
        num_scalar_prefetch=1, grid=(n_slot, nr),
        in_specs=[blk(lambda w: w[0]), blk(lambda w: 1 - w[0])],
        out_specs=pl.BlockSpec((1, rc, cdim), lambda j, t, w: (j, t, 0)),
        scratch_shapes=[pltpu.VMEM((2, rc, cdim), F32), pltpu.SemaphoreType.DMA((2,)), pltpu.SemaphoreType.DMA((2,)),
                        pltpu.SemaphoreType.REGULAR((2,))])
    return pl.pallas_call(
        body, name=name, grid_spec=grid_spec, out_shape=jax.ShapeDtypeStruct((n_slot, hr, cdim), out_dtype),
        compiler_params=_cparams("arbitrary", "arbitrary"),
    )(where, g4, g4)


def _chip_exchange(parts, *, name):
    nt = len(parts)

    def body(*refs):
        src, got = refs[:nt], refs[nt:2 * nt]
        send, recv = refs[2 * nt:]
        x, y, c, chips = _place()
        remote = []
        for t in range(nt):
            for k in range(3):
                remote.append(pltpu.make_async_remote_copy(
                    src_ref=src[t].at[2 * chips[k][0] + chips[k][1]], dst_ref=got[t].at[k],
                    send_sem=send.at[t, k], recv_sem=recv.at[t, k],
                    device_id=(chips[k][0], chips[k][1], c), device_id_type=MESH))
        for cp in remote:
            cp.start()
        for cp in remote:
            cp.wait_recv()
        for cp in remote:
            cp.wait_send()

    return pl.pallas_call(
        body, name=name, in_specs=[ANY] * nt, out_specs=[ANY] * nt,
        out_shape=[jax.ShapeDtypeStruct((3,) + a.shape[1:], a.dtype) for a in parts],
        scratch_shapes=[pltpu.SemaphoreType.DMA((nt, 3)), pltpu.SemaphoreType.DMA((nt, 3))],
    )(*parts)


def _sum_join(p, got, where, *, name):
    _, hr, cdim = p.shape
    rc = _comm_rows(hr, cdim)
    n = hr // rc

    def body(w_ref, p_ref, g_ref, out, buf, lsem, ssem, rsem):
        x, y, c, _ = _place()
        sib = (x, y, 1 - c)
        r = pl.program_id(0)

        def copies(step, slot):
            rows = out.at[pl.ds(pl.multiple_of(c * hr + step * rc, 8), rc)]
            return (pltpu.make_async_copy(buf.at[slot], rows, lsem.at[slot]),
                    pltpu.make_async_remote_copy(src_ref=buf.at[slot], dst_ref=rows, send_sem=ssem.at[slot],
                                                 recv_sem=rsem, device_id=sib, device_id_type=MESH))

        s = lax.rem(r, 2)

        @pl.when(r >= 2)
        def _():
            lc, rm = copies(r - 2, s)
            lc.wait()
            rm.wait_send()

        buf[s] = p_ref[0].astype(F32) + g_ref[0].astype(F32) + g_ref[1].astype(F32) + g_ref[2].astype(F32)
        lc, rm = copies(r, s)
        lc.start()
        rm.start()

        @pl.when(r == n - 1)
        def _():
            for step in range(max(0, n - 2), n):
                lc, rm = copies(step, step % 2)
                lc.wait()
                rm.wait_send()
            whole = out.at[pl.ds(0, hr)]
            pltpu.make_async_remote_copy(src_ref=whole, dst_ref=whole, send_sem=ssem.at[0], recv_sem=rsem,
                                         device_id=sib, device_id_type=MESH).wait_recv()

    grid_spec = pltpu.PrefetchScalarGridSpec(
        num_scalar_prefetch=1, grid=(n,),
        in_specs=[pl.BlockSpec((1, rc, cdim), lambda t, w: (w[1], t, 0)),
                  pl.BlockSpec((3, rc, cdim), lambda t, w: (0, t, 0))],
        out_specs=ANY,
        scratch_shapes=[pltpu.VMEM((2, rc, cdim), F32), pltpu.SemaphoreType.DMA((2,)), pltpu.SemaphoreType.DMA((2,)),
                        pltpu.SemaphoreType.DMA])
    return pl.pallas_call(
        body, name=name, grid_spec=grid_spec, out_shape=jax.ShapeDtypeStruct((2 * hr, cdim), F32),
        compiler_params=_cparams("arbitrary"),
    )(where, p, got)


def _rider_gather_send(shards):
    nt = len(shards)

    def copies(src, dst, send, recv, lsem):
        x, y, c, chips = _place()
        me = 2 * x + y
        local = [pltpu.make_async_copy(src[t], dst[t].at[me], lsem.at[t]) for t in range(nt)]
        out, landed = [], []
        for t in range(nt):
            hr = src[t].shape[0] // 2
            for k in range(3):
                peer = (chips[k][0], chips[k][1], c)
                out.append(pltpu.make_async_remote_copy(
                    src_ref=src[t].at[pl.ds(c * hr, hr)], dst_ref=dst[t].at[me, pl.ds(c * hr, hr)],
                    send_sem=send.at[t, k], recv_sem=recv.at[t, k], device_id=peer, device_id_type=MESH))
                theirs = dst[t].at[2 * chips[k][0] + chips[k][1], pl.ds(c * hr, hr)]
                landed.append(pltpu.make_async_remote_copy(
                    src_ref=theirs, dst_ref=theirs, send_sem=send.at[t, k], recv_sem=recv.at[t, k],
                    device_id=peer, device_id_type=MESH))
        return local, out, landed

    def start(src, dst, sems):
        local, out, _ = copies(src, dst, *sems)
        for cp in local + out:
            cp.start()

    def finish(src, dst, sems):
        local, out, landed = copies(src, dst, *sems)
        for cp in landed:
            cp.wait_recv()
        for cp in out:
            cp.wait_send()
        for cp in local:
            cp.wait()

    return _Rider(shards, [jax.ShapeDtypeStruct((N_CHIP,) + a.shape, a.dtype) for a in shards],
                  [pltpu.SemaphoreType.DMA((nt, 3)), pltpu.SemaphoreType.DMA((nt, 3)), pltpu.SemaphoreType.DMA((nt,))],
                  start, finish)


def _rider_gather_forward(bufs):
    nt = len(bufs)

    def copies(src, dst, send, recv):
        x, y, c, chips = _place()
        mine, theirs = [], []
        for t in range(nt):
            hr = src[t].shape[1] // 2
            for k in range(3):
                slot = 2 * chips[k][0] + chips[k][1]
                for hc, into in ((c, mine), (1 - c, theirs)):
                    into.append(pltpu.make_async_remote_copy(
                        src_ref=src[t].at[slot, pl.ds(hc * hr, hr)], dst_ref=dst[t].at[slot, pl.ds(hc * hr, hr)],
                        send_sem=send.at[t, k], recv_sem=recv.at[t, k], device_id=(x, y, 1 - c), device_id_type=MESH))
        return mine, theirs

    def start(src, dst, sems):
        for cp in copies(src, dst, *sems)[0]:
            cp.start()

    def finish(src, dst, sems):
        mine, theirs = copies(src, dst, *sems)
        for cp in theirs:
            cp.wait_recv()
        for cp in mine:
            cp.wait_send()

    return _Rider(bufs, [jax.ShapeDtypeStruct(a.shape, a.dtype) for a in bufs],
                  [pltpu.SemaphoreType.DMA((nt, 3)), pltpu.SemaphoreType.DMA((nt, 3))], start, finish,
                  aliases={t: t for t in range(nt)})


def _rider_chip_exchange(parts):
    nt = len(parts)

    def copies(src, got, send, recv):
        x, y, c, chips = _place()
        return [pltpu.make_async_remote_copy(
            src_ref=src[t].at[2 * chips[k][0] + chips[k][1]], dst_ref=got[t].at[k], send_sem=send.at[t, k],
            recv_sem=recv.at[t, k], device_id=(chips[k][0], chips[k][1], c), device_id_type=MESH)
            for t in range(nt) for k in range(3)]

    def start(src, got, sems):
        for cp in copies(src, got, *sems):
            cp.start()

    def finish(src, got, sems):
        remote = copies(src, got, *sems)
        for cp in remote:
            cp.wait_recv()
        for cp in remote:
            cp.wait_send()

    return _Rider(parts, [jax.ShapeDtypeStruct((3,) + a.shape[1:], a.dtype) for a in parts],
                  [pltpu.SemaphoreType.DMA((nt, 3)), pltpu.SemaphoreType.DMA((nt, 3))], start, finish)


def _gather_all(block, *, name):
    m_per, n = block.shape

    def body(x_ref, out_ref, send_sems, recv_sems, local_sem):
        x, y, c, chips = _place()
        me, sibling = (x, y, c), (x, y, 1 - c)

        def rows(px, py, pc):
            return out_ref.at[4 * px + 2 * py + pc]

        def copy(k, blk, to, src=None):
            return pltpu.make_async_remote_copy(
                src_ref=rows(*blk) if src is None else src, dst_ref=rows(*blk),
                send_sem=send_sems.at[k], recv_sem=recv_sems.at[k], device_id=to, device_id_type=MESH)

        mine = pltpu.make_async_copy(x_ref, rows(*me), local_sem)
        mine.start()
        first = [copy(0, me, sibling, src=x_ref)]
        first += [copy(1 + j, me, (*chip, c), src=x_ref) for j, chip in enumerate(chips)]
        for cp in first:
            cp.start()
        passed = [copy(4 + j, (*chip, c), sibling) for j, chip in enumerate(chips)]
        for j, chip in enumerate(chips):
            copy(1 + j, (*chip, c), me).wait_recv()
            passed[j].start()
        copy(0, sibling, me).wait_recv()
        for j, chip in enumerate(chips):
            copy(4 + j, (*chip, 1 - c), me).wait_recv()
        for cp in first + passed:
            cp.wait_send()
        mine.wait()

    return pl.pallas_call(
        body, name=name,
        out_shape=jax.ShapeDtypeStruct((N_DEV, m_per, n), block.dtype),
        in_specs=[pl.BlockSpec(memory_space=pltpu.VMEM)], out_specs=pl.BlockSpec(memory_space=pltpu.VMEM),
        scratch_shapes=[pltpu.SemaphoreType.DMA((7,)), pltpu.SemaphoreType.DMA((7,)), pltpu.SemaphoreType.DMA],
        compiler_params=pltpu.CompilerParams(vmem_limit_bytes=VMEM_LIMIT),
    )(block)


def _sum_slots(slots, *, name):
    n, m, c = slots.shape
    t = _rows_tile(m, c * n)

    def body(s_ref, o_ref):
        acc = s_ref[0]
        for k in range(1, n):
            acc = acc + s_ref[k]
        o_ref[...] = acc

    return pl.pallas_call(
        body, name=name, grid=(m // t,), in_specs=[pl.BlockSpec((n, t, c), lambda i: (0, i, 0))],
        out_specs=pl.BlockSpec((t, c), lambda i: (i, 0)), out_shape=jax.ShapeDtypeStruct((m, c), F32),
        compiler_params=_cparams("parallel"),
    )(slots)


def _pad_cols(a, width):
    return a if a.shape[1] == width else jnp.pad(a, ((0, 0), (0, width - a.shape[1])))


def _w_in_padded(shards):
    full = jnp.concatenate([shards[j] for j in range(N_CHIP)], axis=1)
    return jnp.concatenate([_pad_cols(full[:, SEG[n][2]:SEG[n][2] + SEG[n][3]], SEG[n][1]) for n in SEG_ORDER], axis=1)


def _w_in_unpadded(gp):
    full = jnp.concatenate([gp[:, SEG[n][0]:SEG[n][0] + SEG[n][3]] for n in ORIG_ORDER], axis=1)
    w = IN_COLS // N_CHIP
    return jnp.stack([full[:, j * w:(j + 1) * w] for j in range(N_CHIP)])


def _pad_heads(w, true_w, pad_w):
    r = w.shape[0]
    h = w.shape[1] // true_w
    return jnp.pad(w.reshape(r, h, true_w), ((0, 0), (0, 0), (0, pad_w - true_w))).reshape(r, h * pad_w)


def _unpad_heads(w, true_w, pad_w):
    r = w.shape[0]
    h = w.shape[1] // pad_w
    return w.reshape(r, h, pad_w)[:, :, :true_w].reshape(r, h * true_w)


def _cols_to_slots(a):
    w = a.shape[1] // N_CHIP
    return jnp.stack([a[:, j * w:(j + 1) * w] for j in range(N_CHIP)])


def _slots_to_cols(a):
    return jnp.concatenate([a[j] for j in range(N_CHIP)], axis=1)


def _to_heads(a, h, d):
    return a.reshape(a.shape[0], h, d).transpose(1, 0, 2)


def _from_heads(a):
    return a.transpose(1, 0, 2).reshape(a.shape[1], -1)


SMALL = [("norm_g", 2048), ("ret_norm_g", 512), ("gla_ba_f", 256), ("gla_ba_b", 256), ("gla_norm_g", 512),
         ("pool_w", 4 * 128 * 128), ("pool_scale", 512), ("mla_q_norm_g", 512), ("mla_kv_norm_g", 256),
         ("mla_qk_norm_q", 192), ("mla_qk_norm_k", 192)]


def _pack_small(vals):
    parts = []
    for name, n in SMALL:
        v = vals[name].reshape(-1)
        parts.append(jnp.pad(v, (0, (-v.shape[0]) % 1024)))
    parts.append(jnp.pad(vals["loss"].reshape(-1), (0, 1023)))
    return jnp.concatenate(parts).reshape(-1, 128)


def _unpack_small(block):
    flat = block.reshape(-1)
    out, off = {}, 0
    for name, n in SMALL:
        out[name] = flat[off:off + DEPTH * n]
        off += DEPTH * n + (-(DEPTH * n)) % 1024
    out["loss"] = flat[off]
    return out


def _layer_weights(l, p, g):
    wa = jnp.zeros((128, 512), F32)
    wa = wa.at[0:GLA_RANK, 0:256].set(_slots_to_cols(g["gla_wa2_f"]))
    wa = wa.at[GLA_RANK:2 * GLA_RANK, 256:512].set(_slots_to_cols(g["gla_wa2_b"]))
    return dict(
        norm_g=p["norm_g"][l][None, :],
        w_in=_w_in_padded(g["w_in"]),
        w_out=g["w_out"].reshape(4 * g["w_out"].shape[1], -1),
        ret_norm_g=p["ret_norm_g"][l][None, :],
        wa=_bf(wa),
        ba=jnp.concatenate([p["gla_ba_f"][l], p["gla_ba_b"][l]])[None, :],
        gla_norm_g=p["gla_norm_g"][l][None, :],
        pool_w=_bf(p["pool_w"][l]),
        pool_scale=p["pool_scale"][l][None, :],
        qg=p["mla_q_norm_g"][l][None, :],
        wq=_pad_heads(_slots_to_cols(g["mla_wq_b"]), MLA_QK, MLA_QKP),
        kvg=p["mla_kv_norm_g"][l][None, :],
        wkv=_slots_to_cols(g["mla_wkv_b"]),
        qng=jnp.pad(p["mla_qk_norm_q"][l], (0, MLA_QKP - MLA_QK))[None, :],
        kng=jnp.pad(p["mla_qk_norm_k"][l], (0, MLA_QKP - MLA_QK))[None, :],
    )


def _layer_fwd(l, x, w, tabs, next_shards=None):
    ret_cos, ret_sin, mla_cos, mla_sp, mla_sn = tabs
    nm = lambda s: f"l{l}_{s}"
    h = _rmsnorm_fwd(x, w["norm_g"], name=nm("norm"))
    if next_shards is None:
        z = _matmul(h, w["w_in"], name=nm("in_proj"))
    else:
        z, landed = _matmul(h, w["w_in"], rider=_rider_gather_send(next_shards), name=nm("in_proj"))
    qr, kr = _ret_pre(z, ret_cos, ret_sin, name=nm("ret_pre"))
    ret_o = _bla(qr, kr, z, _ret_log_gamma(False), (0, 0, SEG["rv"][0] // 512), name=nm("ret_scan"))
    y_a = _post(ret_o, z, SEG["rg"][0] // 512, w["ret_norm_g"], norm=True, name=nm("ret_post"))
    la = _gla_gate(z, w["wa"], w["ba"], name=nm("gla_gate"))
    la_h = jnp.stack([_to_heads(la[:, :256], GLA_HEADS, GLA_DK), _to_heads(la[:, 256:], GLA_HEADS, GLA_DK)])
    gq = _to_heads(z[:, SEG["gq"][0]:SEG["gq"][0] + 256], GLA_HEADS, GLA_DK)
    gk = _to_heads(z[:, SEG["gk"][0]:SEG["gk"][0] + 256], GLA_HEADS, GLA_DK)
    gla_o, gla_st = _gla_fwd(gq, gk, z, la_h, name=nm("gla_scan"))
    y_b = _post(gla_o, z, SEG["gg"][0] // 512, w["gla_norm_g"], norm=True, name=nm("gla_post"))
    y_c = _pool_fwd(z, w["pool_w"], w["pool_scale"], name=nm("pool"))
    q, k, v = _mla_pre(z, w["qg"], w["wq"], w["kvg"], w["wkv"], w["qng"], w["kng"], mla_cos, mla_sp, mla_sn,
                       name=nm("mla_pre"))
    if next_shards is None:
        (att_o, lse), gathered = _flash_fwd(q, k, v, name=nm("attn")), None
    else:
        att_o, lse, gathered = _flash_fwd(q, k, v, rider=_rider_gather_forward(landed), name=nm("attn"))
    y_d = _post([att_o], z, SEG["mg"][0] // 512, w["qg"], norm=False, name=nm("mla_post"))
    y = jnp.concatenate([y_a, y_b, y_c, y_d], axis=1)
    x_next = _matmul(y, w["w_out"], add=x, name=nm("out_proj"))
    saved = dict(x=x, h=h, z=z, y=y, qr=qr, kr=kr, ret_o=ret_o, la_h=la_h, gq=gq, gk=gk, gla_o=gla_o, gla_st=gla_st,
                 q=q, k=k, v=v, att_o=att_o, lse=lse)
    return x_next, saved, gathered


def _layer_bwd(l, dx_next, w, sv, tabs, riding_parts=None):
    ret_cos, ret_sin, mla_cos, mla_sp, mla_sn = tabs
    nm = lambda s: f"l{l}_{s}"
    z = sv["z"]
    dy = _matmul(dx_next, w["w_out"], tb=True, name=nm("out_proj_dy"))
    d_w_out = _matmul(sv["y"].T, dx_next, tn=512, name=nm("out_proj_dw"))
    d_rg, d_ret_o, d_ret_g = _post_bwd(dy, 0, sv["ret_o"], z, SEG["rg"][0] // 512, w["ret_norm_g"], norm=True,
                                       name=nm("ret_post_bwd"))
    vcol = SEG["rv"][0] // 512
    dqr = _bla(d_ret_o, z, sv["kr"], _ret_log_gamma(False), (0, vcol, 0), name=nm("ret_scan_dq"))
    dkr = _bla(z, d_ret_o, sv["qr"], _ret_log_gamma(True), (vcol, 0, 0), name=nm("ret_scan_dk"))
    drv = _bla(sv["kr"], sv["qr"], d_ret_o, _ret_log_gamma(True), (0, 0, 0), name=nm("ret_scan_dv"))
    d_rq, d_rk = _ret_pre_bwd(dqr, dkr, ret_cos, ret_sin, name=nm("ret_pre_bwd"))
    d_rv = _add_n([drv[0], drv[1]], out_dtype=BF16, name=nm("ret_dv_sum"))
    d_gg, d_gla_o, d_gla_g = _post_bwd(dy, 1, sv["gla_o"], z, SEG["gg"][0] // 512, w["gla_norm_g"], norm=True,
                                       name=nm("gla_post_bwd"))
    dq2, dk2, dla2, dv2 = _gla_bwd(sv["gq"], sv["gk"], z, sv["la_h"], d_gla_o, sv["gla_st"], name=nm("gla_scan_bwd"))
    d_gq = _bf(_from_heads(dq2[0] + dq2[1]))
    d_gk = _bf(_from_heads(dk2[0] + dk2[1]))
    d_gv = _add_n([dv2[0], dv2[1]], out_dtype=BF16, name=nm("gla_dv_sum"))
    dla = jnp.concatenate([_from_heads(dla2[0]), _from_heads(dla2[1])], axis=1)
    d_ga, d_wa, d_ba = _gla_gate_bwd(dla, z, w["wa"], w["ba"], name=nm("gla_gate_bwd"))
    d_pv, d_pg, d_pool_w, d_pool_scale = _pool_bwd(dy, z, w["pool_w"], w["pool_scale"], name=nm("pool_bwd"))
    d_mg, d_att_o, _ = _post_bwd(dy, 3, [sv["att_o"]], z, SEG["mg"][0] // 512, w["qg"], norm=False,
                                 name=nm("mla_post_bwd"))
    if riding_parts is None:
        (dq, dk, dv), rode = _flash_bwd(sv["q"], sv["k"], sv["v"], d_att_o, sv["att_o"], sv["lse"],
                                        name=nm("attn_bwd")), None
    else:
        dq, dk, dv, rode = _flash_bwd(sv["q"], sv["k"], sv["v"], d_att_o, sv["att_o"], sv["lse"],
                                      rider=_rider_chip_exchange(riding_parts), name=nm("attn_bwd"))
    d_mq, d_mkv, d_mkr, d_wq, d_wkv, d_qg, d_kvg, d_qng, d_kng = _mla_pre_bwd(
        dq, dk, dv, z, w["qg"], w["wq"], w["kvg"], w["wkv"], w["qng"], w["kng"], mla_cos, mla_sp, mla_sn,
        name=nm("mla_pre_bwd"))
    segs = dict(rq=d_rq, rk=d_rk, rv=d_rv, rg=d_rg, gv=d_gv, gg=d_gg, pv=d_pv, pg=d_pg, mq=d_mq, mg=d_mg,
                gq=d_gq, gk=d_gk, mkv=d_mkv, ga=d_ga, mkr=d_mkr)
    dz = jnp.concatenate([segs[n] for n in SEG_ORDER], axis=1)
    dh = _matmul(dz, w["w_in"], tb=True, tn=512, name=nm("in_proj_dh"))
    d_w_in = _matmul(sv["h"].T, dz, name=nm("in_proj_dw"))
    dx, d_norm_g = _rmsnorm_bwd(sv["x"], dh, w["norm_g"], dx_next, name=nm("norm_bwd"))
    sharded = dict(
        w_in=_w_in_unpadded(d_w_in),
        w_out=d_w_out.reshape(N_CHIP, d_w_out.shape[0] // N_CHIP, d_w_out.shape[1]),
        mla_wq_b=_cols_to_slots(_unpad_heads(d_wq, MLA_QK, MLA_QKP)),
        mla_wkv_b=_cols_to_slots(d_wkv),
        gla_wa2_f=_cols_to_slots(d_wa[0:GLA_RANK, 0:256]),
        gla_wa2_b=_cols_to_slots(d_wa[GLA_RANK:2 * GLA_RANK, 256:512]),
    )
    small = dict(
        norm_g=d_norm_g[0], ret_norm_g=d_ret_g[0], gla_ba_f=d_ba[0, :256], gla_ba_b=d_ba[0, 256:],
        gla_norm_g=d_gla_g[0], pool_w=d_pool_w.reshape(-1), pool_scale=d_pool_scale[0], mla_q_norm_g=d_qg[0],
        mla_kv_norm_g=d_kvg[0], mla_qk_norm_q=d_qng[0, :MLA_QK], mla_qk_norm_k=d_kng[0, :MLA_QK],
    )
    return dx, sharded, small, rode


SHARDED = ["w_in", "w_out", "mla_wq_b", "mla_wkv_b", "gla_wa2_f", "gla_wa2_b"]
WEIGHTS = ["norm_g", "w_in", "ret_norm_g", "gla_wa2_f", "gla_ba_f", "gla_wa2_b", "gla_ba_b", "gla_norm_g", "pool_w",
           "pool_scale", "mla_q_norm_g", "mla_wq_b", "mla_kv_norm_g", "mla_wkv_b", "mla_qk_norm_q", "mla_qk_norm_k",
           "w_out"]


def _layer_shards(p, l):
    return [p["w_in"][l].astype(BF16), p["w_out"][l].astype(BF16), p["mla_wq_b"][l].astype(BF16),
            p["mla_wkv_b"][l].astype(BF16), p["gla_wa2_f"][l], p["gla_wa2_b"][l]]


def _step(p, where):
    x = p["x"][0]
    tabs = _rope_tables(x.shape[0])
    got0 = _gather_shards(_layer_shards(p, 0), name="l0_gather_weights")
    w0 = _layer_weights(0, p, dict(zip(SHARDED, got0)))
    x1, sv0, got1 = _layer_fwd(0, x, w0, tabs, next_shards=_layer_shards(p, 1))
    w1 = _layer_weights(1, p, dict(zip(SHARDED, got1)))
    x2, sv1, _ = _layer_fwd(1, x1, w1, tabs)
    dx, loss = _loss_head(x2, p["loss_target"][0], name="loss_head")

    def pair_sums(l, sharded):
        return [_pair_reduce(sharded[n], where, out_dtype=BF16, name=f"l{l}_pair_reduce_{n}") for n in SHARDED]

    def joined(l, pair, others):
        return [_sum_join(a, b, where, name=f"l{l}_sum_join_{n}") for n, a, b in zip(SHARDED, pair, others)]

    dx, sharded1, small1, _ = _layer_bwd(1, dx, w1, sv1, tabs)
    pair1 = pair_sums(1, sharded1)
    dx, sharded0, small0, others1 = _layer_bwd(0, dx, w0, sv0, tabs, riding_parts=pair1)
    grads1 = joined(1, pair1, others1)
    pair0 = pair_sums(0, sharded0)
    grads0 = joined(0, pair0, _chip_exchange(pair0, name="l0_chip_exchange"))
    grads = {n: jnp.stack([g0, g1]) for n, g0, g1 in zip(SHARDED, grads0, grads1)}
    small = {n: jnp.stack([small0[n], small1[n]]) for n, _ in SMALL}
    small["loss"] = loss
    return dx[None], grads, small


def kernel(x, norm_g, w_in, ret_norm_g, gla_wa2_f, gla_ba_f, gla_wa2_b, gla_ba_b, gla_norm_g, pool_w, pool_scale, mla_q_norm_g, mla_wq_b, mla_kv_norm_g, mla_wkv_b, mla_qk_norm_q, mla_qk_norm_k, w_out, loss_target, m_norm_g, m_w_in, m_ret_norm_g, m_gla_wa2_f, m_gla_ba_f, m_gla_wa2_b, m_gla_ba_b, m_gla_norm_g, m_pool_w, m_pool_scale, m_mla_q_norm_g, m_mla_wq_b, m_mla_kv_norm_g, m_mla_wkv_b, m_mla_qk_norm_q, m_mla_qk_norm_k, m_w_out, v_norm_g, v_w_in, v_ret_norm_g, v_gla_wa2_f, v_gla_ba_f, v_gla_wa2_b, v_gla_ba_b, v_gla_norm_g, v_pool_w, v_pool_scale, v_mla_q_norm_g, v_mla_wq_b, v_mla_kv_norm_g, v_mla_wkv_b, v_mla_qk_norm_q, v_mla_qk_norm_k, v_w_out):
    p = dict(x=x, norm_g=norm_g, w_in=w_in, ret_norm_g=ret_norm_g, gla_wa2_f=gla_wa2_f, gla_ba_f=gla_ba_f,
             gla_wa2_b=gla_wa2_b, gla_ba_b=gla_ba_b, gla_norm_g=gla_norm_g, pool_w=pool_w, pool_scale=pool_scale,
             mla_q_norm_g=mla_q_norm_g, mla_wq_b=mla_wq_b, mla_kv_norm_g=mla_kv_norm_g, mla_wkv_b=mla_wkv_b,
             mla_qk_norm_q=mla_qk_norm_q, mla_qk_norm_k=mla_qk_norm_k, w_out=w_out, loss_target=loss_target)
    moments = dict(
        m=dict(norm_g=m_norm_g, w_in=m_w_in, ret_norm_g=m_ret_norm_g, gla_wa2_f=m_gla_wa2_f, gla_ba_f=m_gla_ba_f,
               gla_wa2_b=m_gla_wa2_b, gla_ba_b=m_gla_ba_b, gla_norm_g=m_gla_norm_g, pool_w=m_pool_w,
               pool_scale=m_pool_scale, mla_q_norm_g=m_mla_q_norm_g, mla_wq_b=m_mla_wq_b,
               mla_kv_norm_g=m_mla_kv_norm_g, mla_wkv_b=m_mla_wkv_b, mla_qk_norm_q=m_mla_qk_norm_q,
               mla_qk_norm_k=m_mla_qk_norm_k, w_out=m_w_out),
        v=dict(norm_g=v_norm_g, w_in=v_w_in, ret_norm_g=v_ret_norm_g, gla_wa2_f=v_gla_wa2_f, gla_ba_f=v_gla_ba_f,
               gla_wa2_b=v_gla_wa2_b, gla_ba_b=v_gla_ba_b, gla_norm_g=v_gla_norm_g, pool_w=v_pool_w,
               pool_scale=v_pool_scale, mla_q_norm_g=v_mla_q_norm_g, mla_wq_b=v_mla_wq_b,
               mla_kv_norm_g=v_mla_kv_norm_g, mla_wkv_b=v_mla_wkv_b, mla_qk_norm_q=v_mla_qk_norm_q,
               mla_qk_norm_k=v_mla_qk_norm_k, w_out=v_w_out))

    where = jnp.stack([lax.axis_index("c"), 2 * lax.axis_index("x") + lax.axis_index("y")]).astype(jnp.int32)
    grad_x, grads, small = _step(p, where)

    slots = _gather_all(_pack_small(small), name="gather_small")
    total = _unpack_small(_sum_slots(slots, name="sum_small"))
    for n, _ in SMALL:
        grads[n] = total[n].reshape(p[n].shape)
    loss = total["loss"]

    delta, new_m, new_v = {}, {}, {}
    for n in WEIGHTS:
        delta[n], new_m[n], new_v[n] = _adamw(p[n], grads[n], moments["m"][n], moments["v"][n], name=f"adamw_{n}")
    return (loss, grad_x, *[grads[n] for n in WEIGHTS], *[delta[n] for n in WEIGHTS],
            *[new_m[n] for n in WEIGHTS], *[new_v[n] for n in WEIGHTS])
```

```python
import functools
import math

import jax
import jax.numpy as jnp
from jax import lax
from jax.experimental import pallas as pl
from jax.experimental.pallas import tpu as pltpu

F32 = jnp.float32
BF16 = jnp.bfloat16
MESH = pl.DeviceIdType.MESH

EPS = 1e-6
ROPE_THETA = 10000.0
DEPTH = 2
N_DEV = 8
N_CHIP = 4

GROUP_W = 512
RET_HEADS = 4
RET_HD = 128
RET_CHUNK = 128
GLA_HEADS = 4
GLA_DK = 64
GLA_DV = 128
GLA_RANK = 16
GLA_TAU = 16.0
GLA_CHUNK = 64
POOL_GROUPS = 4
POOL_GW = 128
POOL_HALO = 8
POOL_TILE = 256
MLA_HEADS = 4
MLA_NOPE = 128
MLA_ROPE = 64
MLA_QK = MLA_NOPE + MLA_ROPE
MLA_QKP = 256
MLA_V = 128
MLA_Q_RANK = 512
MLA_KV_RANK = 256
MLA_SCALE = MLA_QK ** -0.5

ADAM_LR = 0.001
ADAM_B1 = 0.9
ADAM_B2 = 0.999
ADAM_EPS = 1e-08
ADAM_WD = 0.01
ADAM_STEP = 10

VMEM_LIMIT = 56 * 1024 * 1024

SEG = {
    "rq": (0, 512, 0, 512), "rk": (512, 512, 512, 512), "rv": (1024, 512, 1024, 512), "rg": (1536, 512, 1536, 512),
    "gv": (2048, 512, 2560, 512), "gg": (2560, 512, 3072, 512),
    "pv": (3072, 512, 3616, 512), "pg": (3584, 512, 4128, 512),
    "mq": (4096, 512, 4640, 512), "mg": (4608, 512, 5472, 512),
    "gq": (5120, 256, 2048, 256), "gk": (5376, 256, 2304, 256), "mkv": (5632, 256, 5152, 256),
    "ga": (5888, 128, 3584, 32), "mkr": (6016, 128, 5408, 64),
}
SEG_ORDER = ["rq", "rk", "rv", "rg", "gv", "gg", "pv", "pg", "mq", "mg", "gq", "gk", "mkv", "ga", "mkr"]
IN_COLS = 5984
IN_PAD = 6144
ORIG_ORDER = ["rq", "rk", "rv", "rg", "gq", "gk", "gv", "gg", "ga", "pv", "pg", "mq", "mkv", "mkr", "mg"]


def _cparams(*sem):
    return pltpu.CompilerParams(dimension_semantics=tuple(sem), vmem_limit_bytes=VMEM_LIMIT)


def _bf(v):
    return v.astype(BF16)


def _dot(a, b, ca=1, cb=0):
    return lax.dot_general(_bf(a), _bf(b), (((ca,), (cb,)), ((), ())), preferred_element_type=F32)


def _split_dot(a01, x, ca=1, cb=0):
    hi = _bf(x)
    r1 = x - hi.astype(F32)
    mid = _bf(r1)
    lo = _bf(r1 - mid.astype(F32))
    dn = (((ca,), (cb,)), ((), ()))
    a = _bf(a01)
    return (lax.dot_general(a, hi, dn, preferred_element_type=F32)
            + lax.dot_general(a, mid, dn, preferred_element_type=F32)
            + lax.dot_general(a, lo, dn, preferred_element_type=F32))


def _sigmoid(x):
    return 1.0 / (1.0 + jnp.exp(-x))


def _silu_parts(g):
    sg = _sigmoid(g)
    return g * sg, sg * (1.0 + g * (1.0 - sg))


class _Rider:
    def __init__(self, ins, outs, sems, start, finish, aliases=None):
        self.ins, self.outs, self.sems, self.start, self.finish = list(ins), list(outs), list(sems), start, finish
        self.aliases = dict(aliases or {})


def _ride(body, rider, n_in, n_out, grid):
    if rider is None:
        return body
    ri, ro, rs = len(rider.ins), len(rider.outs), len(rider.sems)

    def wrapped(*refs):
        ins, refs = refs[:n_in], refs[n_in:]
        rin, refs = refs[:ri], refs[ri:]
        outs, refs = refs[:n_out], refs[n_out:]
        rout, refs = refs[:ro], refs[ro:]
        scratch, sems = refs[:len(refs) - rs], refs[len(refs) - rs:]
        first = pl.program_id(0) == 0
        last = pl.program_id(0) == grid[0] - 1
        for ax in range(1, len(grid)):
            first = jnp.logical_and(first, pl.program_id(ax) == 0)
            last = jnp.logical_and(last, pl.program_id(ax) == grid[ax] - 1)

        @pl.when(first)
        def _():
            rider.start(rin, rout, sems)

        body(*ins, *outs, *scratch)

        @pl.when(last)
        def _():
            rider.finish(rin, rout, sems)

    return wrapped


def _ride_call(body, rider, *, name, grid, in_specs, out_specs, out_shape, scratch_shapes, args, sem):
    n_in, n_out = len(in_specs), len(out_specs)
    if rider is None:
        return pl.pallas_call(body, name=name, grid=grid, in_specs=in_specs, out_specs=out_specs, out_shape=out_shape,
                              scratch_shapes=scratch_shapes, compiler_params=_cparams(*sem))(*args), []
    outs = pl.pallas_call(
        _ride(body, rider, n_in, n_out, grid), name=name, grid=grid,
        in_specs=list(in_specs) + [ANY] * len(rider.ins), out_specs=list(out_specs) + [ANY] * len(rider.outs),
        out_shape=list(out_shape) + rider.outs, scratch_shapes=list(scratch_shapes) + rider.sems,
        input_output_aliases={n_in + i: n_out + o for i, o in rider.aliases.items()},
        compiler_params=_cparams(*(["arbitrary"] * len(grid))),
    )(*args, *rider.ins)
    return outs[:n_out], outs[n_out:]


def _matmul(a, b, *, ta=False, tb=False, out_dtype=F32, tm=512, tn=1024, tk=None, add=None, n_outer=True, rider=None,
            name):
    m, kdim = (a.shape[1], a.shape[0]) if ta else a.shape
    n = b.shape[0] if tb else b.shape[1]
    tm, tn = min(tm, m), min(tn, n)
    tk = kdim if tk is None else min(tk, kdim)
    assert m % tm == 0 and n % tn == 0 and kdim % tk == 0
    nk = kdim // tk
    ca, cb = (0 if ta else 1), (1 if tb else 0)

    def body(*refs):
        if add is None:
            a_ref, b_ref, o_ref = refs[:3]
            add_ref = None
        else:
            a_ref, b_ref, add_ref, o_ref = refs[:4]
        p = _dot(a_ref[...], b_ref[...], ca, cb)

        def finish(r):
            if add_ref is not None:
                r = r + add_ref[...]
            o_ref[...] = r.astype(out_dtype)

        if nk == 1:
            finish(p)
        else:
            acc = refs[-1]
            k = pl.program_id(2)

            @pl.when(k == 0)
            def _():
                acc[...] = p

            @pl.when(k > 0)
            def _():
                acc[...] += p

            @pl.when(k == nk - 1)
            def _():
                finish(acc[...])

    def ij(g0, g1):
        return (g1, g0) if n_outer else (g0, g1)

    a_spec = (pl.BlockSpec((tk, tm), lambda g0, g1, k: (k, ij(g0, g1)[0])) if ta
              else pl.BlockSpec((tm, tk), lambda g0, g1, k: (ij(g0, g1)[0], k)))
    b_spec = (pl.BlockSpec((tn, tk), lambda g0, g1, k: (ij(g0, g1)[1], k)) if tb
              else pl.BlockSpec((tk, tn), lambda g0, g1, k: (k, ij(g0, g1)[1])))
    o_spec = pl.BlockSpec((tm, tn), lambda g0, g1, k: ij(g0, g1))
    in_specs = [a_spec, b_spec] + ([o_spec] if add is not None else [])
    args = (a, b) + ((add,) if add is not None else ())
    grid = (n // tn, m // tm, nk) if n_outer else (m // tm, n // tn, nk)
    (out,), rode = _ride_call(
        body, rider, name=name, grid=grid, in_specs=in_specs, out_specs=[o_spec],
        out_shape=[jax.ShapeDtypeStruct((m, n), out_dtype)],
        scratch_shapes=[] if nk == 1 else [pltpu.VMEM((tm, tn), F32)], args=args,
        sem=("parallel", "parallel", "arbitrary"))
    return out if rider is None else (out, rode)


def _rmsnorm_fwd(x, g, *, name, tm=256):
    s, d = x.shape
    tm = min(tm, s)

    def body(x_ref, g_ref, h_ref):
        xv = x_ref[...]
        r = lax.rsqrt(jnp.mean(xv * xv, axis=-1, keepdims=True) + EPS)
        h_ref[...] = _bf(xv * r * g_ref[...])

    return pl.pallas_call(
        body, name=name, grid=(s // tm,),
        in_specs=[pl.BlockSpec((tm, d), lambda i: (i, 0)), pl.BlockSpec((1, d), lambda i: (0, 0))],
        out_specs=pl.BlockSpec((tm, d), lambda i: (i, 0)),
        out_shape=jax.ShapeDtypeStruct((s, d), BF16),
        compiler_params=_cparams("parallel"),
    )(x, g)


def _rmsnorm_bwd(x, dh, g, dres, *, name, tm=256):
    s, d = x.shape
    tm = min(tm, s)

    def body(x_ref, dh_ref, g_ref, dres_ref, dx_ref, dg_ref):
        i = pl.program_id(0)
        xv = x_ref[...]
        r = lax.rsqrt(jnp.mean(xv * xv, axis=-1, keepdims=True) + EPS)
        xn = xv * r
        dv = dh_ref[...]
        part = jnp.sum(dv * xn, axis=0, keepdims=True)

        @pl.when(i == 0)
        def _():
            dg_ref[...] = part

        @pl.when(i > 0)
        def _():
            dg_ref[...] += part

        dxn = dv * g_ref[...]
        dx_ref[...] = dres_ref[...] + r * (dxn - xn * jnp.mean(dxn * xn, axis=-1, keepdims=True))

    row = pl.BlockSpec((tm, d), lambda i: (i, 0))
    vec = pl.BlockSpec((1, d), lambda i: (0, 0))
    return pl.pallas_call(
        body, name=name, grid=(s // tm,), in_specs=[row, row, vec, row], out_specs=[row, vec],
        out_shape=[jax.ShapeDtypeStruct((s, d), F32), jax.ShapeDtypeStruct((1, d), F32)],
        compiler_params=_cparams("arbitrary"),
    )(x, dh, g, dres)


def _loss_head(xf, target, *, name, tm=256):
    s, d = xf.shape
    tm = min(tm, s)

    def body(x_ref, t_ref, dx_ref, l_ref):
        i = pl.program_id(0)
        e = x_ref[...] - t_ref[...]
        dx_ref[...] = e * (1.0 / d)
        rows = jnp.mean(e * e, axis=-1, keepdims=True)
        part = 0.5 * jnp.sum(rows, axis=0, keepdims=True)

        @pl.when(i == 0)
        def _():
            l_ref[...] = part

        @pl.when(i > 0)
        def _():
            l_ref[...] += part

    row = pl.BlockSpec((tm, d), lambda i: (i, 0))
    return pl.pallas_call(
        body, name=name, grid=(s // tm,), in_specs=[row, row],
        out_specs=[row, pl.BlockSpec((1, 1), lambda i: (0, 0))],
        out_shape=[jax.ShapeDtypeStruct((s, d), F32), jax.ShapeDtypeStruct((1, 1), F32)],
        compiler_params=_cparams("arbitrary"),
    )(xf, target)


def _rope_tables(s):
    pos = jnp.arange(s, dtype=F32)[:, None]
    inv_r = 1.0 / (ROPE_THETA ** (jnp.arange(0, RET_HD, 2, dtype=F32) / RET_HD))
    ang = pos * inv_r[None, :]
    ret_cos = jnp.concatenate([jnp.cos(ang), jnp.cos(ang)], axis=1)
    ret_sin = jnp.concatenate([-jnp.sin(ang), jnp.sin(ang)], axis=1)
    inv_m = 1.0 / (ROPE_THETA ** (jnp.arange(0, MLA_ROPE, 2, dtype=F32) / MLA_ROPE))
    am = pos * inv_m[None, :]
    z32, z64 = jnp.zeros((s, 32), F32), jnp.zeros((s, 64), F32)
    mla_cos = jnp.concatenate([jnp.cos(am), jnp.cos(am), z64], axis=1)
    mla_sp = jnp.concatenate([z32, jnp.sin(am), z64], axis=1)
    mla_sn = jnp.concatenate([-jnp.sin(am), z32, z64], axis=1)
    return ret_cos, ret_sin, mla_cos, mla_sp, mla_sn


def _rope128(x, c, sg):
    return x * c + pltpu.roll(x, 64, 1) * sg


def _unrope128(d, c, sg):
    return d * c + pltpu.roll(d * sg, 64, 1)


def _rope64(t, c, sp, sn):
    return t * c + pltpu.roll(t, 96, 1) * sn + pltpu.roll(t, 32, 1) * sp


def _unrope64(d, c, sp, sn):
    return d * c + pltpu.roll(d * sn, 32, 1) + pltpu.roll(d * sp, 96, 1)


def _ret_pre(z, cos, sin, *, name, tm=256):
    s = z.shape[0]
    tm = min(tm, s)
    scale = RET_HD ** -0.5

    def body(q_ref, k_ref, c_ref, s_ref, qo_ref, ko_ref):
        c, sg = c_ref[...], s_ref[...]
        for h in range(RET_HEADS):
            sl = slice(h * RET_HD, (h + 1) * RET_HD)
            qo_ref[:, sl] = _rope128(q_ref[:, sl], c, sg)
            ko_ref[:, sl] = _rope128(k_ref[:, sl], c, sg) * scale

    seg = lambda j: pl.BlockSpec((tm, GROUP_W), lambda i: (i, j))
    tab = pl.BlockSpec((tm, RET_HD), lambda i: (i, 0))
    return pl.pallas_call(
        body, name=name, grid=(s // tm,), in_specs=[seg(0), seg(1), tab, tab],
        out_specs=[seg(0), seg(0)],
        out_shape=[jax.ShapeDtypeStruct((s, GROUP_W), F32)] * 2,
        compiler_params=_cparams("parallel"),
    )(z, z, cos, sin)


def _ret_pre_bwd(dqr, dkr, cos, sin, *, name, tm=256):
    s = dqr[0].shape[0]
    tm = min(tm, s)
    scale = RET_HD ** -0.5

    def body(dq0_ref, dq1_ref, dk0_ref, dk1_ref, c_ref, s_ref, qo_ref, ko_ref):
        c, sg = c_ref[...], s_ref[...]
        for h in range(RET_HEADS):
            sl = slice(h * RET_HD, (h + 1) * RET_HD)
            qo_ref[:, sl] = _bf(_unrope128(dq0_ref[:, sl] + dq1_ref[:, sl], c, sg))
            ko_ref[:, sl] = _bf(_unrope128(dk0_ref[:, sl] + dk1_ref[:, sl], c, sg) * scale)

    row = pl.BlockSpec((tm, GROUP_W), lambda i: (i, 0))
    tab = pl.BlockSpec((tm, RET_HD), lambda i: (i, 0))
    return pl.pallas_call(
        body, name=name, grid=(s // tm,), in_specs=[row, row, row, row, tab, tab], out_specs=[row, row],
        out_shape=[jax.ShapeDtypeStruct((s, GROUP_W), BF16)] * 2,
        compiler_params=_cparams("parallel"),
    )(dqr[0], dqr[1], dkr[0], dkr[1], cos, sin)


def _bla(a, b, c, lg, cols, *, name):
    s = a.shape[0]
    ch = min(RET_CHUNK, s)
    n = s // ch
    hd = RET_HD

    def body(lg_ref, a0, b0, c0, a1, b1, c1, o0, o1, st):
        t = pl.program_id(0)

        @pl.when(t == 0)
        def _():
            st[...] = jnp.zeros_like(st)

        ii = lax.broadcasted_iota(jnp.int32, (ch, ch), 0)
        jj = lax.broadcasted_iota(jnp.int32, (ch, ch), 1)
        idx = lax.broadcasted_iota(jnp.int32, (ch, 1), 0).astype(F32)
        for d, (a_ref, b_ref, c_ref, o_ref) in enumerate(((a0, b0, c0, o0), (a1, b1, c1, o1))):
            diff = ((ii - jj) if d == 0 else (jj - ii)).astype(F32)
            keep = diff >= 0
            dpos = jnp.maximum(diff, 0.0)
            pq = (idx + 1.0) if d == 0 else (ch - idx)
            pk = (ch - 1.0 - idx) if d == 0 else idx
            for h in range(RET_HEADS):
                g = lg_ref[d, h]
                sl = slice(h * hd, (h + 1) * hd)
                av, bv, cv = a_ref[:, sl], b_ref[:, sl], c_ref[:, sl]
                sc = _dot(av, bv, 1, 1) * jnp.where(keep, jnp.exp(dpos * g), 0.0)
                stv = st[d, h]
                o_ref[:, sl] = _dot(sc, cv) + _dot(av * jnp.exp(pq * g), stv)
                st[d, h] = jnp.exp(ch * g) * stv + _dot(bv * jnp.exp(pk * g), cv, 0, 0)

    fwd = lambda j: pl.BlockSpec((ch, GROUP_W), lambda t: (t, j))
    bwd = lambda j: pl.BlockSpec((ch, GROUP_W), lambda t: (n - 1 - t, j))
    return pl.pallas_call(
        body, name=name, grid=(n,),
        in_specs=[pl.BlockSpec(memory_space=pltpu.SMEM), fwd(cols[0]), fwd(cols[1]), fwd(cols[2]),
                  bwd(cols[0]), bwd(cols[1]), bwd(cols[2])],
        out_specs=[fwd(0), bwd(0)],
        out_shape=[jax.ShapeDtypeStruct((s, GROUP_W), F32)] * 2,
        scratch_shapes=[pltpu.VMEM((2, RET_HEADS, hd, hd), F32)],
        compiler_params=_cparams("arbitrary"),
    )(lg, a, b, c, a, b, c)


def _post(os_, zg, gcol, g, *, norm, name, tm=256):
    s = zg.shape[0]
    tm = min(tm, s)
    nd = len(os_)

    def body(*refs):
        o_refs, (gt_ref, g_ref, y_ref) = refs[:nd], refs[nd:]
        silu, _ = _silu_parts(gt_ref[...])
        for h in range(4):
            sl = slice(h * 128, (h + 1) * 128)
            o = o_refs[0][:, sl]
            for k in range(1, nd):
                o = o + o_refs[k][:, sl]
            if norm:
                r = lax.rsqrt(jnp.mean(o * o, axis=-1, keepdims=True) + EPS)
                o = o * r * g_ref[:, sl]
            y_ref[:, sl] = _bf(silu[:, sl] * o)

    row = pl.BlockSpec((tm, GROUP_W), lambda i: (i, 0))
    return pl.pallas_call(
        body, name=name, grid=(s // tm,),
        in_specs=[row] * nd + [pl.BlockSpec((tm, GROUP_W), lambda i: (i, gcol)),
                               pl.BlockSpec((1, GROUP_W), lambda i: (0, 0))],
        out_specs=row,
        out_shape=jax.ShapeDtypeStruct((s, GROUP_W), BF16),
        compiler_params=_cparams("parallel"),
    )(*os_, zg, g)


def _post_bwd(dy, ycol, os_, zg, gcol, g, *, norm, name, tm=256):
    s = zg.shape[0]
    tm = min(tm, s)
    nd = len(os_)

    def body(*refs):
        dy_ref, o_refs = refs[0], refs[1:1 + nd]
        gt_ref, g_ref, dgt_ref, do_ref, dg_ref = refs[1 + nd:]
        i = pl.program_id(0)
        silu, dsilu = _silu_parts(gt_ref[...])
        dyv = dy_ref[...]
        parts = []
        for h in range(4):
            sl = slice(h * 128, (h + 1) * 128)
            o = o_refs[0][:, sl]
            for k in range(1, nd):
                o = o + o_refs[k][:, sl]
            dn = dyv[:, sl] * silu[:, sl]
            if norm:
                r = lax.rsqrt(jnp.mean(o * o, axis=-1, keepdims=True) + EPS)
                xn = o * r
                gh = g_ref[:, sl]
                dgt_ref[:, sl] = _bf(dyv[:, sl] * (xn * gh) * dsilu[:, sl])
                parts.append(jnp.sum(dn * xn, axis=0, keepdims=True))
                dxn = dn * gh
                do_ref[:, sl] = r * (dxn - xn * jnp.mean(dxn * xn, axis=-1, keepdims=True))
            else:
                dgt_ref[:, sl] = _bf(dyv[:, sl] * o * dsilu[:, sl])
                parts.append(jnp.zeros((1, 128), F32))
                do_ref[:, sl] = dn
        part = jnp.concatenate(parts, axis=1)

        @pl.when(i == 0)
        def _():
            dg_ref[...] = part

        @pl.when(i > 0)
        def _():
            dg_ref[...] += part

    row = pl.BlockSpec((tm, GROUP_W), lambda i: (i, 0))
    vec = pl.BlockSpec((1, GROUP_W), lambda i: (0, 0))
    return pl.pallas_call(
        body, name=name, grid=(s // tm,),
        in_specs=[pl.BlockSpec((tm, GROUP_W), lambda i: (i, ycol))] + [row] * nd
        + [pl.BlockSpec((tm, GROUP_W), lambda i: (i, gcol)), vec],
        out_specs=[row, row, vec],
        out_shape=[jax.ShapeDtypeStruct((s, GROUP_W), BF16), jax.ShapeDtypeStruct((s, GROUP_W), F32),
                   jax.ShapeDtypeStruct((1, GROUP_W), F32)],
        compiler_params=_cparams("arbitrary"),
    )(dy, *os_, zg, g)


def _ret_log_gamma(swap):
    gf = 1.0 - 2.0 ** (-5.0 - jnp.arange(RET_HEADS, dtype=F32))
    lf, lb = jnp.log(gf), jnp.log(gf[::-1])
    return jnp.stack([lb, lf] if swap else [lf, lb])


def _log_sigmoid(x):
    return jnp.minimum(x, 0.0) - jnp.log(1.0 + jnp.exp(-jnp.abs(x)))


def _gla_gate(z, wa, ba, *, name, tm=256):
    s = z.shape[0]
    tm = min(tm, s)
    col = SEG["ga"][0] // 128

    def body(ga_ref, wa_ref, ba_ref, la_ref):
        pre = _dot(ga_ref[...], wa_ref[...]) + ba_ref[...]
        la_ref[...] = _log_sigmoid(pre) / GLA_TAU

    return pl.pallas_call(
        body, name=name, grid=(s // tm,),
        in_specs=[pl.BlockSpec((tm, 128), lambda i: (i, col)), pl.BlockSpec((128, 512), lambda i: (0, 0)),
                  pl.BlockSpec((1, 512), lambda i: (0, 0))],
        out_specs=pl.BlockSpec((tm, 512), lambda i: (i, 0)),
        out_shape=jax.ShapeDtypeStruct((s, 512), F32),
        compiler_params=_cparams("parallel"),
    )(z, wa, ba)


def _gla_gate_bwd(dla, z, wa, ba, *, name, tm=256):
    s = z.shape[0]
    tm = min(tm, s)
    col = SEG["ga"][0] // 128

    def body(dla_ref, ga_ref, wa_ref, ba_ref, dga_ref, dwa_ref, dba_ref):
        i = pl.program_id(0)
        gav = ga_ref[...]
        pre = _dot(gav, wa_ref[...]) + ba_ref[...]
        dpre = dla_ref[...] * (1.0 - _sigmoid(pre)) * (1.0 / GLA_TAU)
        dga_ref[...] = _bf(_dot(dpre, wa_ref[...], 1, 1))
        pw = _dot(gav, dpre, 0, 0)
        pb = jnp.sum(dpre, axis=0, keepdims=True)

        @pl.when(i == 0)
        def _():
            dwa_ref[...] = pw
            dba_ref[...] = pb

        @pl.when(i > 0)
        def _():
            dwa_ref[...] += pw
            dba_ref[...] += pb

    return pl.pallas_call(
        body, name=name, grid=(s // tm,),
        in_specs=[pl.BlockSpec((tm, 512), lambda i: (i, 0)), pl.BlockSpec((tm, 128), lambda i: (i, col)),
                  pl.BlockSpec((128, 512), lambda i: (0, 0)), pl.BlockSpec((1, 512), lambda i: (0, 0))],
        out_specs=[pl.BlockSpec((tm, 128), lambda i: (i, 0)), pl.BlockSpec((128, 512), lambda i: (0, 0)),
                   pl.BlockSpec((1, 512), lambda i: (0, 0))],
        out_shape=[jax.ShapeDtypeStruct((s, 128), BF16), jax.ShapeDtypeStruct((128, 512), F32),
                   jax.ShapeDtypeStruct((1, 512), F32)],
        compiler_params=_cparams("arbitrary"),
    )(dla, z, wa, ba)


def _gla_masks(ch):
    ii = lax.broadcasted_iota(jnp.int32, (ch, ch), 0)
    tt = lax.broadcasted_iota(jnp.int32, (ch, ch), 1)
    return jnp.where(tt <= ii, 1.0, 0.0), jnp.where(tt >= ii, 1.0, 0.0)


def _gla_chunk(d, tmat, qv, kv, lav, ch):
    c = _split_dot(tmat, lav)
    big_l = c[ch - 1:ch, :] if d == 0 else c[0:1, :]
    qt = qv * (GLA_DK ** -0.5) * jnp.exp(c)
    kt = kv * jnp.exp(-c)
    kh = kv * jnp.exp(big_l - c)
    return c, big_l, qt, kt, kh


def _gla_fwd(qh, kh_, z, la, *, name):
    s = z.shape[0]
    ch = min(GLA_CHUNK, s)
    n = s // ch
    vcol = SEG["gv"][0] // GROUP_W

    def body(q0, k0, v0, la0, q1, k1, v1, la1, o0, o1, zs0, zs1, st):
        t = pl.program_id(0)

        @pl.when(t == 0)
        def _():
            st[...] = jnp.zeros_like(st)

        masks = _gla_masks(ch)
        for d, (q_ref, k_ref, v_ref, la_ref, o_ref, zs_ref) in enumerate(
                ((q0, k0, v0, la0, o0, zs0), (q1, k1, v1, la1, o1, zs1))):
            for h in range(GLA_HEADS):
                c, big_l, qt, kt, kh = _gla_chunk(d, masks[d], q_ref[h], k_ref[h], la_ref[0, h], ch)
                vv = v_ref[:, h * GLA_DV:(h + 1) * GLA_DV]
                p = _dot(qt, kt, 1, 1) * masks[d]
                zst = st[d, h]
                o_ref[:, h * GLA_DV:(h + 1) * GLA_DV] = _dot(p, vv) + _dot(qt, zst, 1, 1)
                zs_ref[h, 0] = zst
                st[d, h] = zst * jnp.exp(big_l) + _dot(vv, kh, 0, 0)

    cidx = (lambda t: t), (lambda t: n - 1 - t)
    hs = lambda d: pl.BlockSpec((GLA_HEADS, ch, GLA_DK), lambda t: (0, cidx[d](t), 0))
    vs = lambda d: pl.BlockSpec((ch, GROUP_W), lambda t: (cidx[d](t), vcol))
    las = lambda d: pl.BlockSpec((1, GLA_HEADS, ch, GLA_DK), lambda t: (d, 0, cidx[d](t), 0))
    os_ = lambda d: pl.BlockSpec((ch, GROUP_W), lambda t: (cidx[d](t), 0))
    zss = lambda d: pl.BlockSpec((GLA_HEADS, 1, GLA_DV, GLA_DK), lambda t: (0, cidx[d](t), 0, 0))
    o0, o1, zs0, zs1 = pl.pallas_call(
        body, name=name, grid=(n,),
        in_specs=[hs(0), hs(0), vs(0), las(0), hs(1), hs(1), vs(1), las(1)],
        out_specs=[os_(0), os_(1), zss(0), zss(1)],
        out_shape=[jax.ShapeDtypeStruct((s, GROUP_W), F32)] * 2
        + [jax.ShapeDtypeStruct((GLA_HEADS, n, GLA_DV, GLA_DK), F32)] * 2,
        scratch_shapes=[pltpu.VMEM((2, GLA_HEADS, GLA_DV, GLA_DK), F32)],
        compiler_params=_cparams("arbitrary"),
    )(qh, kh_, z, la, qh, kh_, z, la)
    return (o0, o1), (zs0, zs1)


def _gla_bwd(qh, kh_, z, la, do, zs, *, name):
    s = z.shape[0]
    ch = min(GLA_CHUNK, s)
    n = s // ch
    vcol = SEG["gv"][0] // GROUP_W

    def body(q0, k0, v0, la0, do0, zs0, q1, k1, v1, la1, do1, zs1,
             dq0, dk0, dla0, dv0, dq1, dk1, dla1, dv1, gz):
        t = pl.program_id(0)

        @pl.when(t == 0)
        def _():
            gz[...] = jnp.zeros_like(gz)

        masks = _gla_masks(ch)
        rows = lax.broadcasted_iota(jnp.int32, (ch, 1), 0)
        for d, (q_ref, k_ref, v_ref, la_ref, do_ref, zs_ref, dq_ref, dk_ref, dla_ref, dv_ref) in enumerate(
                ((q0, k0, v0, la0, do0, zs0, dq0, dk0, dla0, dv0), (q1, k1, v1, la1, do1, zs1, dq1, dk1, dla1, dv1))):
            tmat = masks[d]
            end = ch - 1 if d == 0 else 0
            for h in range(GLA_HEADS):
                c, big_l, qt, kt, kh = _gla_chunk(d, tmat, q_ref[h], k_ref[h], la_ref[0, h], ch)
                vsl = slice(h * GLA_DV, (h + 1) * GLA_DV)
                vv, dov, zst, gzv = v_ref[:, vsl], do_ref[:, vsl], zs_ref[h, 0], gz[d, h]
                p = _dot(qt, kt, 1, 1) * tmat
                dp = _dot(dov, vv, 1, 1) * tmat
                dqt = _dot(dp, kt) + _dot(dov, zst)
                dkt = _dot(dp, qt, 0, 0)
                dkh = _dot(vv, gzv)
                dv_ref[:, vsl] = _dot(p, dov, 0, 0) + _dot(kh, gzv, 1, 1)
                dq_ref[h] = dqt * jnp.exp(c) * (GLA_DK ** -0.5)
                dk_ref[h] = dkt * jnp.exp(-c) + dkh * jnp.exp(big_l - c)
                e_l = jnp.exp(big_l)
                d_l = jnp.sum(dkh * kh, axis=0, keepdims=True) + e_l * jnp.sum(zst * gzv, axis=0, keepdims=True)
                dc = dqt * qt - dkt * kt - dkh * kh + jnp.where(rows == end, d_l, 0.0)
                dla_ref[h] = _split_dot(tmat, dc, 0, 0)
                gz[d, h] = gzv * e_l + _dot(dov, qt, 0, 0)

    cidx = (lambda t: n - 1 - t), (lambda t: t)
    hs = lambda d: pl.BlockSpec((GLA_HEADS, ch, GLA_DK), lambda t: (0, cidx[d](t), 0))
    vs = lambda d: pl.BlockSpec((ch, GROUP_W), lambda t: (cidx[d](t), vcol))
    las = lambda d: pl.BlockSpec((1, GLA_HEADS, ch, GLA_DK), lambda t: (d, 0, cidx[d](t), 0))
    row = lambda d: pl.BlockSpec((ch, GROUP_W), lambda t: (cidx[d](t), 0))
    zss = lambda d: pl.BlockSpec((GLA_HEADS, 1, GLA_DV, GLA_DK), lambda t: (0, cidx[d](t), 0, 0))
    hshape = jax.ShapeDtypeStruct((GLA_HEADS, s, GLA_DK), F32)
    wide = jax.ShapeDtypeStruct((s, GROUP_W), F32)
    outs = pl.pallas_call(
        body, name=name, grid=(n,),
        in_specs=[hs(0), hs(0), vs(0), las(0), row(0), zss(0), hs(1), hs(1), vs(1), las(1), row(1), zss(1)],
        out_specs=[hs(0), hs(0), hs(0), row(0), hs(1), hs(1), hs(1), row(1)],
        out_shape=[hshape, hshape, hshape, wide, hshape, hshape, hshape, wide],
        scratch_shapes=[pltpu.VMEM((2, GLA_HEADS, GLA_DV, GLA_DK), F32)],
        compiler_params=_cparams("arbitrary"),
    )(qh, kh_, z, la, do, zs[0], qh, kh_, z, la, do, zs[1])
    dq0, dk0, dla0, dv0, dq1, dk1, dla1, dv1 = outs
    return (dq0, dq1), (dk0, dk1), (dla0, dla1), (dv0, dv1)


def _band(lo, hi, rows, width):
    r = lax.broadcasted_iota(jnp.int32, (rows, width), 0)
    j = lax.broadcasted_iota(jnp.int32, (rows, width), 1)
    k = j - POOL_HALO - r
    return jnp.where((k >= lo) & (k <= hi), 1.0, 0.0)


def _pool_cnt(t0, half, rows, s):
    t = t0 + lax.broadcasted_iota(jnp.int32, (rows, 1), 0)
    return (jnp.minimum(t + half, s) - jnp.maximum(t - half, 0)).astype(F32)


def _pool_fwd(z, pw, scale, *, name):
    s = z.shape[0]
    tl = min(POOL_TILE, s)
    nt = s // tl
    ucol, gcol = SEG["pv"][0] // 128, SEG["pg"][0] // 128

    def body(u_ref, gt_ref, pw_ref, sc_ref, y_ref, pad):
        g = pl.program_id(0)
        half = jnp.left_shift(1, g)
        pad[0:POOL_HALO, :] = jnp.zeros((POOL_HALO, POOL_GW), F32)
        pad[POOL_HALO + s:POOL_HALO + s + POOL_HALO, :] = jnp.zeros((POOL_HALO, POOL_GW), F32)
        pad[POOL_HALO:POOL_HALO + s, :] = u_ref[...]
        band = _band(-half, half - 1, tl, tl + 2 * POOL_HALO)
        pwv, scv = pw_ref[0], sc_ref[...]

        def tile(i, carry):
            t0 = pl.multiple_of(i * tl, tl)
            win = pad[pl.ds(t0, tl + 2 * POOL_HALO), :]
            u = win[POOL_HALO:POOL_HALO + tl, :]
            pooled = _split_dot(band, win) / _pool_cnt(t0, half, tl, s) - u
            mixed = _dot(pooled, pwv)
            silu, _ = _silu_parts(gt_ref[pl.ds(t0, tl), :])
            y_ref[pl.ds(t0, tl), :] = _bf(silu * (mixed * scv))
            return carry

        lax.fori_loop(0, nt, tile, 0)

    return pl.pallas_call(
        body, name=name, grid=(POOL_GROUPS,),
        in_specs=[pl.BlockSpec((s, POOL_GW), lambda g: (0, ucol + g)),
                  pl.BlockSpec((s, POOL_GW), lambda g: (0, gcol + g)),
                  pl.BlockSpec((1, POOL_GW, POOL_GW), lambda g: (g, 0, 0)),
                  pl.BlockSpec((1, POOL_GW), lambda g: (0, g))],
        out_specs=pl.BlockSpec((s, POOL_GW), lambda g: (0, g)),
        out_shape=jax.ShapeDtypeStruct((s, GROUP_W), BF16),
        scratch_shapes=[pltpu.VMEM((s + 2 * POOL_HALO, POOL_GW), F32)],
        compiler_params=_cparams("parallel"),
    )(z, z, pw, scale)


def _pool_bwd(dy, z, pw, scale, *, name):
    s = z.shape[0]
    tl = min(POOL_TILE, s)
    nt = s // tl
    ucol, gcol, ycol = SEG["pv"][0] // 128, SEG["pg"][0] // 128, 2 * GROUP_W // 128

    def body(dy_ref, u_ref, gt_ref, pw_ref, sc_ref, du_ref, dgt_ref, dpw_ref, dsc_ref, pad, epad, dpo):
        g = pl.program_id(0)
        half = jnp.left_shift(1, g)
        zeros = jnp.zeros((POOL_HALO, POOL_GW), F32)
        for buf in (pad, epad):
            buf[0:POOL_HALO, :] = zeros
            buf[POOL_HALO + s:POOL_HALO + s + POOL_HALO, :] = zeros
        pad[POOL_HALO:POOL_HALO + s, :] = u_ref[...]
        band = _band(-half, half - 1, tl, tl + 2 * POOL_HALO)
        band_t = _band(1 - half, half, tl, tl + 2 * POOL_HALO)
        pwv, scv = pw_ref[0], sc_ref[...]
        dpw_ref[0] = jnp.zeros((POOL_GW, POOL_GW), F32)
        dsc_ref[...] = jnp.zeros((1, POOL_GW), F32)

        def tile(i, carry):
            t0 = pl.multiple_of(i * tl, tl)
            win = pad[pl.ds(t0, tl + 2 * POOL_HALO), :]
            u = win[POOL_HALO:POOL_HALO + tl, :]
            cnt = _pool_cnt(t0, half, tl, s)
            pooled = _split_dot(band, win) / cnt - u
            mixed = _dot(pooled, pwv)
            silu, dsilu = _silu_parts(gt_ref[pl.ds(t0, tl), :])
            dyv = dy_ref[pl.ds(t0, tl), :]
            dgt_ref[pl.ds(t0, tl), :] = _bf(dyv * (mixed * scv) * dsilu)
            dsc_ref[...] += jnp.sum(dyv * silu * mixed, axis=0, keepdims=True)
            dm = dyv * silu * scv
            dpw_ref[0] += _dot(pooled, dm, 0, 0)
            dpooled = _dot(dm, pwv, 1, 1)
            dpo[pl.ds(t0, tl), :] = dpooled
            epad[pl.ds(POOL_HALO + t0, tl), :] = dpooled / cnt
            return carry

        lax.fori_loop(0, nt, tile, 0)

        def tile2(i, carry):
            t0 = pl.multiple_of(i * tl, tl)
            ewin = epad[pl.ds(t0, tl + 2 * POOL_HALO), :]
            du_ref[pl.ds(t0, tl), :] = _bf(_split_dot(band_t, ewin) - dpo[pl.ds(t0, tl), :])
            return carry

        lax.fori_loop(0, nt, tile2, 0)

    col = lambda c0: pl.BlockSpec((s, POOL_GW), lambda g: (0, c0 + g))
    return pl.pallas_call(
        body, name=name, grid=(POOL_GROUPS,),
        in_specs=[col(ycol), col(ucol), col(gcol), pl.BlockSpec((1, POOL_GW, POOL_GW), lambda g: (g, 0, 0)),
                  pl.BlockSpec((1, POOL_GW), lambda g: (0, g))],
        out_specs=[col(0), col(0), pl.BlockSpec((1, POOL_GW, POOL_GW), lambda g: (g, 0, 0)),
                   pl.BlockSpec((1, POOL_GW), lambda g: (0, g))],
        out_shape=[jax.ShapeDtypeStruct((s, GROUP_W), BF16), jax.ShapeDtypeStruct((s, GROUP_W), BF16),
                   jax.ShapeDtypeStruct((POOL_GROUPS, POOL_GW, POOL_GW), F32),
                   jax.ShapeDtypeStruct((1, GROUP_W), F32)],
        scratch_shapes=[pltpu.VMEM((s + 2 * POOL_HALO, POOL_GW), F32), pltpu.VMEM((s + 2 * POOL_HALO, POOL_GW), F32),
                        pltpu.VMEM((s, POOL_GW), F32)],
        compiler_params=_cparams("parallel"),
    )(dy, z, z, pw, scale)


def _mla_specs(tm):
    zq = pl.BlockSpec((tm, 512), lambda i: (i, SEG["mq"][0] // 512))
    zkv = pl.BlockSpec((tm, 256), lambda i: (i, SEG["mkv"][0] // 256))
    zkr = pl.BlockSpec((tm, 128), lambda i: (i, SEG["mkr"][0] // 128))
    full = lambda r, c: pl.BlockSpec((r, c), lambda i: (0, 0))
    tab = pl.BlockSpec((tm, 128), lambda i: (i, 0))
    weights = [full(1, 512), full(512, 1024), full(1, 256), full(256, 1024), full(1, 256), full(1, 256)]
    return [zq, zkv, zkr] + weights + [tab, tab, tab]


def _mla_project(xq_ref, xkv_ref, qg_ref, wq_ref, kvg_ref, wkv_ref):
    xq = xq_ref[...]
    r1 = lax.rsqrt(jnp.mean(xq * xq, axis=-1, keepdims=True) + EPS)
    xn1 = xq * r1
    qn = _bf(xn1 * qg_ref[...])
    qraw = _dot(qn, wq_ref[...])
    xkv = xkv_ref[...]
    r2 = lax.rsqrt(jnp.mean(xkv * xkv, axis=-1, keepdims=True) + EPS)
    xn2 = xkv * r2
    kvn = _bf(xn2 * kvg_ref[...])
    kvraw = _dot(kvn, wkv_ref[...])
    return r1, xn1, qn, qraw, r2, xn2, kvn, kvraw


def _mla_pre(z, qg, wq, kvg, wkv, qng, kng, cos, sp, sn, *, name, tm=256):
    s = z.shape[0]
    tm = min(tm, s)

    def body(xq_ref, xkv_ref, pe_ref, qg_ref, wq_ref, kvg_ref, wkv_ref, qng_ref, kng_ref, c_ref, sp_ref, sn_ref,
             q_ref, k_ref, v_ref):
        _, _, _, qraw, _, _, _, kvraw = _mla_project(xq_ref, xkv_ref, qg_ref, wq_ref, kvg_ref, wkv_ref)
        c, spv, snv = c_ref[...], sp_ref[...], sn_ref[...]
        pe = pe_ref[...]
        pe_ss = jnp.sum(pe * pe, axis=-1, keepdims=True)
        qngv, kngv = qng_ref[...], kng_ref[...]
        for h in range(MLA_HEADS):
            b = h * MLA_QKP
            qh = qraw[:, b:b + MLA_QKP]
            r = lax.rsqrt(jnp.sum(qh * qh, axis=-1, keepdims=True) * (1.0 / MLA_QK) + EPS)
            qn_h = qh * r * qngv
            q_ref[:, b:b + 128] = _bf(qn_h[:, :128] * MLA_SCALE)
            q_ref[:, b + 128:b + 256] = _bf(_rope64(qn_h[:, 128:], c, spv, snv) * MLA_SCALE)
            kn = kvraw[:, b:b + 128]
            rk = lax.rsqrt((jnp.sum(kn * kn, axis=-1, keepdims=True) + pe_ss) * (1.0 / MLA_QK) + EPS)
            k_ref[:, b:b + 128] = _bf(kn * rk * kngv[:, :128])
            k_ref[:, b + 128:b + 256] = _bf(_rope64(pe * rk * kngv[:, 128:], c, spv, snv))
            v_ref[:, h * MLA_V:(h + 1) * MLA_V] = _bf(kvraw[:, b + 128:b + 256])

    row = lambda w: pl.BlockSpec((tm, w), lambda i: (i, 0))
    return pl.pallas_call(
        body, name=name, grid=(s // tm,), in_specs=_mla_specs(tm),
        out_specs=[row(1024), row(1024), row(512)],
        out_shape=[jax.ShapeDtypeStruct((s, 1024), BF16), jax.ShapeDtypeStruct((s, 1024), BF16),
                   jax.ShapeDtypeStruct((s, 512), BF16)],
        compiler_params=_cparams("parallel"),
    )(z, z, z, qg, wq, kvg, wkv, qng, kng, cos, sp, sn)


def _mla_pre_bwd(dq, dk, dv, z, qg, wq, kvg, wkv, qng, kng, cos, sp, sn, *, name, tm=256):
    s = z.shape[0]
    tm = min(tm, s)

    def body(dq_ref, dk_ref, dv_ref, xq_ref, xkv_ref, pe_ref, qg_ref, wq_ref, kvg_ref, wkv_ref, qng_ref, kng_ref,
             c_ref, sp_ref, sn_ref, dxq_ref, dxkv_ref, dpe_ref, dwq_ref, dwkv_ref, dqg_ref, dkvg_ref, dqng_ref,
             dkng_ref, dqraw, dkvraw):
        i = pl.program_id(0)
        r1, xn1, qn, qraw, r2, xn2, kvn, kvraw = _mla_project(xq_ref, xkv_ref, qg_ref, wq_ref, kvg_ref, wkv_ref)
        c, spv, snv = c_ref[...], sp_ref[...], sn_ref[...]
        pe = pe_ref[...]
        pe_ss = jnp.sum(pe * pe, axis=-1, keepdims=True)
        qngv, kngv = qng_ref[...], kng_ref[...]
        dqng = jnp.zeros((1, MLA_QKP), F32)
        dkng = jnp.zeros((1, MLA_QKP), F32)
        dpe = jnp.zeros_like(pe)
        for h in range(MLA_HEADS):
            b = h * MLA_QKP
            qh = qraw[:, b:b + MLA_QKP]
            r = lax.rsqrt(jnp.sum(qh * qh, axis=-1, keepdims=True) * (1.0 / MLA_QK) + EPS)
            xn = qh * r
            d_n = jnp.concatenate(
                [dq_ref[:, b:b + 128], _unrope64(dq_ref[:, b + 128:b + 256], c, spv, snv)], axis=1) * MLA_SCALE
            dqng = dqng + jnp.sum(d_n * xn, axis=0, keepdims=True)
            dxn = d_n * qngv
            dqraw[:, b:b + MLA_QKP] = _bf(r * (dxn - xn * (jnp.sum(dxn * xn, axis=-1, keepdims=True) * (1.0 / MLA_QK))))
            kn = kvraw[:, b:b + 128]
            rk = lax.rsqrt((jnp.sum(kn * kn, axis=-1, keepdims=True) + pe_ss) * (1.0 / MLA_QK) + EPS)
            xk = jnp.concatenate([kn, pe], axis=1) * rk
            d_k = jnp.concatenate(
                [dk_ref[:, b:b + 128], _unrope64(dk_ref[:, b + 128:b + 256], c, spv, snv)], axis=1)
            dkng = dkng + jnp.sum(d_k * xk, axis=0, keepdims=True)
            dxk = d_k * kngv
            dfull = rk * (dxk - xk * (jnp.sum(dxk * xk, axis=-1, keepdims=True) * (1.0 / MLA_QK)))
            dkvraw[:, b:b + 128] = _bf(dfull[:, :128])
            dkvraw[:, b + 128:b + 256] = _bf(dv_ref[:, h * MLA_V:(h + 1) * MLA_V])
            dpe = dpe + dfull[:, 128:]
        dpe_ref[...] = _bf(dpe)
        dqr, dkvr = dqraw[...], dkvraw[...]
        dqn = _dot(dqr, wq_ref[...], 1, 1)
        dxn1 = dqn * qg_ref[...]
        dxq_ref[...] = _bf(r1 * (dxn1 - xn1 * jnp.mean(dxn1 * xn1, axis=-1, keepdims=True)))
        dkvn = _dot(dkvr, wkv_ref[...], 1, 1)
        dxn2 = dkvn * kvg_ref[...]
        dxkv_ref[...] = _bf(r2 * (dxn2 - xn2 * jnp.mean(dxn2 * xn2, axis=-1, keepdims=True)))
        parts = (_dot(qn, dqr, 0, 0), _dot(kvn, dkvr, 0, 0), jnp.sum(dqn * xn1, axis=0, keepdims=True),
                 jnp.sum(dkvn * xn2, axis=0, keepdims=True), dqng, dkng)
        accs = (dwq_ref, dwkv_ref, dqg_ref, dkvg_ref, dqng_ref, dkng_ref)

        @pl.when(i == 0)
        def _():
            for a, p in zip(accs, parts):
                a[...] = p

        @pl.when(i > 0)
        def _():
            for a, p in zip(accs, parts):
                a[...] += p

    row = lambda w: pl.BlockSpec((tm, w), lambda i: (i, 0))
    full = lambda r, c: pl.BlockSpec((r, c), lambda i: (0, 0))
    return pl.pallas_call(
        body, name=name, grid=(s // tm,),
        in_specs=[row(1024), row(1024), row(512)] + _mla_specs(tm),
        out_specs=[row(512), row(256), row(128), full(512, 1024), full(256, 1024), full(1, 512), full(1, 256),
                   full(1, 256), full(1, 256)],
        out_shape=[jax.ShapeDtypeStruct((s, 512), BF16), jax.ShapeDtypeStruct((s, 256), BF16),
                   jax.ShapeDtypeStruct((s, 128), BF16), jax.ShapeDtypeStruct((512, 1024), F32),
                   jax.ShapeDtypeStruct((256, 1024), F32), jax.ShapeDtypeStruct((1, 512), F32),
                   jax.ShapeDtypeStruct((1, 256), F32), jax.ShapeDtypeStruct((1, 256), F32),
                   jax.ShapeDtypeStruct((1, 256), F32)],
        scratch_shapes=[pltpu.VMEM((tm, 1024), BF16), pltpu.VMEM((tm, 1024), BF16)],
        compiler_params=_cparams("arbitrary"),
    )(dq, dk, dv, z, z, z, qg, wq, kvg, wkv, qng, kng, cos, sp, sn)


def _flash_fwd(q, k, v, *, name, tq=1024, tk=1024, rider=None):
    s = q.shape[0]
    tq, tk = min(tq, s), min(tk, s)
    nk = s // tk

    def body(q_ref, k_ref, v_ref, o_ref, lse_ref, m_s, l_s, acc):
        j = pl.program_id(2)

        @pl.when(j == 0)
        def _():
            m_s[...] = jnp.full_like(m_s, -jnp.inf)
            l_s[...] = jnp.zeros_like(l_s)
            acc[...] = jnp.zeros_like(acc)

        sc = _dot(q_ref[...], k_ref[...], 1, 1)
        m_prev = m_s[...]
        m_new = jnp.maximum(m_prev, jnp.max(sc, axis=-1, keepdims=True))
        p = jnp.exp(sc - m_new[:, 0:1])
        alpha = jnp.exp(m_prev - m_new)
        l_s[...] = alpha * l_s[...] + jnp.sum(p, axis=-1, keepdims=True)
        acc[...] = alpha * acc[...] + _dot(p, v_ref[...])
        m_s[...] = m_new

        @pl.when(j == nk - 1)
        def _():
            o_ref[...] = acc[...] / l_s[...]
            lse_ref[...] = m_s[...] + jnp.log(l_s[...])

    (o, lse), rode = _ride_call(
        body, rider, name=name, grid=(MLA_HEADS, s // tq, nk),
        in_specs=[pl.BlockSpec((tq, MLA_QKP), lambda h, i, j: (i, h)),
                  pl.BlockSpec((tk, MLA_QKP), lambda h, i, j: (j, h)),
                  pl.BlockSpec((tk, MLA_V), lambda h, i, j: (j, h))],
        out_specs=[pl.BlockSpec((tq, MLA_V), lambda h, i, j: (i, h))] * 2,
        out_shape=[jax.ShapeDtypeStruct((s, GROUP_W), F32)] * 2,
        scratch_shapes=[pltpu.VMEM((tq, MLA_V), F32), pltpu.VMEM((tq, MLA_V), F32), pltpu.VMEM((tq, MLA_V), F32)],
        args=(q, k, v), sem=("parallel", "parallel", "arbitrary"))
    return (o, lse) if rider is None else (o, lse, rode)


def _flash_bwd(q, k, v, do, o, lse, *, name, tq=1024, tk=1024, rider=None):
    s = q.shape[0]
    tq, tk = min(tq, s), min(tk, s)
    nq, nk = s // tq, s // tk

    def body(q_ref, k_ref, v_ref, do_ref, o_ref, lse_ref, dq_ref, dk_ref, dv_ref, dk_acc, dv_acc):
        j, i = pl.program_id(1), pl.program_id(2)
        dov = do_ref[...]
        delta = jnp.sum(dov * o_ref[...], axis=-1, keepdims=True)
        p = jnp.exp(_dot(q_ref[...], k_ref[...], 1, 1) - lse_ref[:, 0:1])
        ds = p * (_dot(dov, v_ref[...], 1, 1) - delta)
        pv = _dot(p, dov, 0, 0)
        pk = _dot(ds, q_ref[...], 0, 0)
        pq = _dot(ds, k_ref[...])
        rows = pl.ds(pl.multiple_of(i * tq, tq), tq)

        @pl.when(j == 0)
        def _():
            dq_ref[rows, :] = pq

        @pl.when(j > 0)
        def _():
            dq_ref[rows, :] += pq

        @pl.when(i == 0)
        def _():
            dv_acc[...] = pv
            dk_acc[...] = pk

        @pl.when(i > 0)
        def _():
            dv_acc[...] += pv
            dk_acc[...] += pk

        @pl.when(i == nq - 1)
        def _():
            dk_ref[...] = dk_acc[...]
            dv_ref[...] = dv_acc[...]

    qb = pl.BlockSpec((tq, MLA_QKP), lambda h, j, i: (i, h))
    kb = pl.BlockSpec((tk, MLA_QKP), lambda h, j, i: (j, h))
    vb = pl.BlockSpec((tk, MLA_V), lambda h, j, i: (j, h))
    ob = pl.BlockSpec((tq, MLA_V), lambda h, j, i: (i, h))
    (dq, dk, dv), rode = _ride_call(
        body, rider, name=name, grid=(MLA_HEADS, nk, nq),
        in_specs=[qb, kb, vb, ob, ob, ob],
        out_specs=[pl.BlockSpec((s, MLA_QKP), lambda h, j, i: (0, h)), kb, vb],
        out_shape=[jax.ShapeDtypeStruct((s, MLA_HEADS * MLA_QKP), F32),
                   jax.ShapeDtypeStruct((s, MLA_HEADS * MLA_QKP), F32), jax.ShapeDtypeStruct((s, GROUP_W), F32)],
        scratch_shapes=[pltpu.VMEM((tk, MLA_QKP), F32), pltpu.VMEM((tk, MLA_V), F32)],
        args=(q, k, v, do, o, lse), sem=("arbitrary", "arbitrary", "arbitrary"))
    return (dq, dk, dv) if rider is None else (dq, dk, dv, rode)


def _rows_tile(r, c, itemsize=4, budget=2 * 1024 * 1024):
    if r * c * itemsize <= budget:
        return r
    best = None
    for t in range(8, r, 8):
        if r % t == 0 and t * c * itemsize <= budget:
            best = t
    return best if best is not None else r


def _add_n(arrs, *, out_dtype=F32, name):
    shape = arrs[0].shape
    c = shape[-1]
    flat = [a.reshape(-1, c) for a in arrs]
    r = flat[0].shape[0]
    t = _rows_tile(r, c)

    def body(*refs):
        acc = refs[0][...].astype(F32)
        for ref in refs[1:-1]:
            acc = acc + ref[...].astype(F32)
        refs[-1][...] = acc.astype(out_dtype)

    blk = pl.BlockSpec((t, c), lambda i: (i, 0))
    out = pl.pallas_call(
        body, name=name, grid=(r // t,), in_specs=[blk] * len(flat), out_specs=blk,
        out_shape=jax.ShapeDtypeStruct((r, c), out_dtype), compiler_params=_cparams("parallel"),
    )(*flat)
    return out.reshape(shape)


def _adamw(w, g, m, v, *, name):
    shape = w.shape
    c = shape[-1]
    flat = [a.reshape(-1, c) for a in (w, g, m, v)]
    r = flat[0].shape[0]
    t = _rows_tile(r, c, budget=1024 * 1024)

    def body(w_ref, g_ref, m_ref, v_ref, d_ref, mo_ref, vo_ref):
        gv = g_ref[...]
        m2 = ADAM_B1 * m_ref[...] + (1.0 - ADAM_B1) * gv
        v2 = ADAM_B2 * v_ref[...] + (1.0 - ADAM_B2) * (gv * gv)
        m_hat = m2 / (1.0 - ADAM_B1 ** ADAM_STEP)
        v_hat = v2 / (1.0 - ADAM_B2 ** ADAM_STEP)
        d_ref[...] = -ADAM_LR * (m_hat / (jnp.sqrt(v_hat) + ADAM_EPS) + ADAM_WD * w_ref[...])
        mo_ref[...] = m2
        vo_ref[...] = v2

    blk = pl.BlockSpec((t, c), lambda i: (i, 0))
    outs = pl.pallas_call(
        body, name=name, grid=(r // t,), in_specs=[blk] * 4, out_specs=[blk] * 3,
        out_shape=[jax.ShapeDtypeStruct((r, c), F32)] * 3, compiler_params=_cparams("parallel"),
    )(*flat)
    return tuple(o.reshape(shape) for o in outs)


def _place():
    x, y, c = lax.axis_index("x"), lax.axis_index("y"), lax.axis_index("c")
    chips = [(1 - x, y), (x, 1 - y), (1 - x, 1 - y)]
    return x, y, c, chips


ANY = pl.BlockSpec(memory_space=pl.ANY)


def _half(ref, axis, hc, lead=()):
    n = ref.shape[len(lead) + axis] // 2
    return ref.at[tuple(lead) + (slice(None),) * axis + (pl.ds(hc * n, n),)]


def _gather_shards(shards, axes, *, name):
    nt = len(shards)

    def body(*refs):
        src, dst = refs[:nt], refs[nt:2 * nt]
        send, recv, fsend, frecv, lsem = refs[2 * nt:]
        x, y, c, chips = _place()
        me = 2 * x + y
        local = [pltpu.make_async_copy(src[t], dst[t].at[me], lsem.at[t]) for t in range(nt)]
        for cp in local:
            cp.start()

        def half(t, slot, hc):
            return _half(dst[t], axes[t], hc, lead=(slot,))

        def first(t, k):
            return pltpu.make_async_remote_copy(
                src_ref=_half(src[t], axes[t], c), dst_ref=half(t, me, c),
                send_sem=send.at[t, k], recv_sem=recv.at[t, k],
                device_id=(chips[k][0], chips[k][1], c), device_id_type=MESH)

        def landed(t, k):
            slot = 2 * chips[k][0] + chips[k][1]
            return pltpu.make_async_remote_copy(
                src_ref=half(t, slot, c), dst_ref=half(t, slot, c),
                send_sem=send.at[t, k], recv_sem=recv.at[t, k],
                device_id=(chips[k][0], chips[k][1], c), device_id_type=MESH)

        def forward(t, k, hc):
            slot = 2 * chips[k][0] + chips[k][1]
            return pltpu.make_async_remote_copy(
                src_ref=half(t, slot, hc), dst_ref=half(t, slot, hc),
                send_sem=fsend.at[t, k], recv_sem=frecv.at[t, k],
                device_id=(x, y, 1 - c), device_id_type=MESH)

        for t in range(nt):
            for k in range(3):
                first(t, k).start()
        for t in range(nt):
            for k in range(3):
                landed(t, k).wait_recv()
                forward(t, k, c).start()
        for t in range(nt):
            for k in range(3):
                forward(t, k, 1 - c).wait_recv()
        for t in range(nt):
            for k in range(3):
                first(t, k).wait_send()
                forward(t, k, c).wait_send()
        for cp in local:
            cp.wait()

    return pl.pallas_call(
        body, name=name, in_specs=[ANY] * nt, out_specs=[ANY] * nt,
        out_shape=[jax.ShapeDtypeStruct((N_CHIP,) + a.shape, a.dtype) for a in shards],
        scratch_shapes=[pltpu.SemaphoreType.DMA((nt, 3)), pltpu.SemaphoreType.DMA((nt, 3)),
                        pltpu.SemaphoreType.DMA((nt, 3)), pltpu.SemaphoreType.DMA((nt, 3)),
                        pltpu.SemaphoreType.DMA((nt,))],
    )(*shards)


def _comm_rows(hr, c, budget=2 * 1024 * 1024):
    if hr * c * 4 <= budget:
        return hr
    best = None
    for t in range(16, hr, 16):
        if hr % t == 0 and t * c * 4 <= budget:
            best = t
    return best if best is not None else hr


def _comm_cols(r, hc, budget=2 * 1024 * 1024):
    best = 128
    for t in range(128, hc + 1, 128):
        if hc % t == 0 and r * t * 4 <= budget:
            best = t
    return best


def _comm_chunks(shape, axis):
    r, cdim = shape
    if axis == 0:
        rc = _comm_rows(r // 2, cdim)
        nt = (r // 2) // rc
        return (rc, cdim), nt, (lambda h, t: (h * nt + t, 0))
    cc = _comm_cols(r, cdim // 2)
    nt = (cdim // 2) // cc
    return (r, cc), nt, (lambda h, t: (0, h * nt + t))


def _pair_reduce(g, where, axis, *, out_dtype, name):
    n_slot, r, cdim = g.shape
    blk_shape, nr, at = _comm_chunks((r, cdim), axis)
    steps = n_slot * nr
    half_shape = (r // 2, cdim) if axis == 0 else (r, cdim // 2)

    def body(w_ref, a_ref, b_ref, o_ref, land, send, recv, credit):
        x, y, c, _ = _place()
        sib = (x, y, 1 - c)
        i = pl.program_id(0) * nr + pl.program_id(1)
        s = lax.rem(i, 2)

        @pl.when(i >= 2)
        def _():
            pl.semaphore_wait(credit.at[s], 1)

        cp = pltpu.make_async_remote_copy(src_ref=b_ref.at[0], dst_ref=land.at[s], send_sem=send.at[s],
                                          recv_sem=recv.at[s], device_id=sib, device_id_type=MESH)
        cp.start()
        cp.wait_recv()
        o_ref[0] = (a_ref[0] + land[s]).astype(out_dtype)
        cp.wait_send()

        @pl.when(i + 2 < steps)
        def _():
            pl.semaphore_signal(credit.at[s], inc=1, device_id=sib, device_id_type=MESH)

    blk = lambda half: pl.BlockSpec((1,) + blk_shape, lambda j, t, w: (j,) + at(half(w), t))
    grid_spec = pltpu.PrefetchScalarGridSpec(
        num_scalar_prefetch=1, grid=(n_slot, nr),
        in_specs=[blk(lambda w: w[0]), blk(lambda w: 1 - w[0])],
        out_specs=pl.BlockSpec((1,) + blk_shape, lambda j, t, w: (j,) + at(0, t)),
        scratch_shapes=[pltpu.VMEM((2,) + blk_shape, F32), pltpu.SemaphoreType.DMA((2,)),
                        pltpu.SemaphoreType.DMA((2,)), pltpu.SemaphoreType.REGULAR((2,))])
    return pl.pallas_call(
        body, name=name, grid_spec=grid_spec, out_shape=jax.ShapeDtypeStruct((n_slot,) + half_shape, out_dtype),
        compiler_params=_cparams("arbitrary", "arbitrary"),
    )(where, g, g)


def _chip_exchange(parts, *, name):
    nt = len(parts)

    def body(*refs):
        src, got = refs[:nt], refs[nt:2 * nt]
        send, recv = refs[2 * nt:]
        x, y, c, chips = _place()
        remote = []
        for t in range(nt):
            for k in range(3):
                remote.append(pltpu.make_async_remote_copy(
                    src_ref=src[t].at[2 * chips[k][0] + chips[k][1]], dst_ref=got[t].at[k],
                    send_sem=send.at[t, k], recv_sem=recv.at[t, k],
                    device_id=(chips[k][0], chips[k][1], c), device_id_type=MESH))
        for cp in remote:
            cp.start()
        for cp in remote:
            cp.wait_recv()
        for cp in remote:
            cp.wait_send()

    return pl.pallas_call(
        body, name=name, in_specs=[ANY] * nt, out_specs=[ANY] * nt,
        out_shape=[jax.ShapeDtypeStruct((3,) + a.shape[1:], a.dtype) for a in parts],
        scratch_shapes=[pltpu.SemaphoreType.DMA((nt, 3)), pltpu.SemaphoreType.DMA((nt, 3))],
    )(*parts)


def _sum_join(p, got, where, axis, *, name):
    _, hr, cdim = p.shape
    full = (2 * hr, cdim) if axis == 0 else (hr, 2 * cdim)
    blk_shape, n, at = _comm_chunks(full, axis)
    step_len = blk_shape[axis]
    half_len = full[axis] // 2

    def body(w_ref, p_ref, g_ref, out, buf, lsem, ssem, rsem):
        x, y, c, _ = _place()
        sib = (x, y, 1 - c)
        r = pl.program_id(0)

        def part(start, size):
            return out.at[(slice(None),) * axis + (pl.ds(start, size),)]

        def copies(step, slot):
            rows = part(pl.multiple_of(c * half_len + step * step_len, 8 if axis == 0 else 128), step_len)
            return (pltpu.make_async_copy(buf.at[slot], rows, lsem.at[slot]),
                    pltpu.make_async_remote_copy(src_ref=buf.at[slot], dst_ref=rows, send_sem=ssem.at[slot],
                                                 recv_sem=rsem, device_id=sib, device_id_type=MESH))

        s = lax.rem(r, 2)

        @pl.when(r >= 2)
        def _():
            lc, rm = copies(r - 2, s)
            lc.wait()
            rm.wait_send()

        buf[s] = p_ref[0].astype(F32) + g_ref[0].astype(F32) + g_ref[1].astype(F32) + g_ref[2].astype(F32)
        lc, rm = copies(r, s)
        lc.start()
        rm.start()

        @pl.when(r == n - 1)
        def _():
            for step in range(max(0, n - 2), n):
                lc, rm = copies(step, step % 2)
                lc.wait()
                rm.wait_send()
            whole = part(0, half_len)
            pltpu.make_async_remote_copy(src_ref=whole, dst_ref=whole, send_sem=ssem.at[0], recv_sem=rsem,
                                         device_id=sib, device_id_type=MESH).wait_recv()

    grid_spec = pltpu.PrefetchScalarGridSpec(
        num_scalar_prefetch=1, grid=(n,),
        in_specs=[pl.BlockSpec((1,) + blk_shape, lambda t, w: (w[1],) + at(0, t)),
                  pl.BlockSpec((3,) + blk_shape, lambda t, w: (0,) + at(0, t))],
        out_specs=ANY,
        scratch_shapes=[pltpu.VMEM((2,) + blk_shape, F32), pltpu.SemaphoreType.DMA((2,)),
                        pltpu.SemaphoreType.DMA((2,)), pltpu.SemaphoreType.DMA])
    return pl.pallas_call(
        body, name=name, grid_spec=grid_spec, out_shape=jax.ShapeDtypeStruct(full, F32),
        compiler_params=_cparams("arbitrary"),
    )(where, p, got)


def _rider_gather_send(shards, axes):
    nt = len(shards)

    def copies(src, dst, send, recv, lsem):
        x, y, c, chips = _place()
        me = 2 * x + y
        local = [pltpu.make_async_copy(src[t], dst[t].at[me], lsem.at[t]) for t in range(nt)]
        out, landed = [], []
        for t in range(nt):
            for k in range(3):
                peer = (chips[k][0], chips[k][1], c)
                out.append(pltpu.make_async_remote_copy(
                    src_ref=_half(src[t], axes[t], c), dst_ref=_half(dst[t], axes[t], c, lead=(me,)),
                    send_sem=send.at[t, k], recv_sem=recv.at[t, k], device_id=peer, device_id_type=MESH))
                theirs = _half(dst[t], axes[t], c, lead=(2 * chips[k][0] + chips[k][1],))
                landed.append(pltpu.make_async_remote_copy(
                    src_ref=theirs, dst_ref=theirs, send_sem=send.at[t, k], recv_sem=recv.at[t, k],
                    device_id=peer, device_id_type=MESH))
        return local, out, landed

    def start(src, dst, sems):
        local, out, _ = copies(src, dst, *sems)
        for cp in local + out:
            cp.start()

    def finish(src, dst, sems):
        local, out, landed = copies(src, dst, *sems)
        for cp in landed:
            cp.wait_recv()
        for cp in out:
            cp.wait_send()
        for cp in local:
            cp.wait()

    return _Rider(shards, [jax.ShapeDtypeStruct((N_CHIP,) + a.shape, a.dtype) for a in shards],
                  [pltpu.SemaphoreType.DMA((nt, 3)), pltpu.SemaphoreType.DMA((nt, 3)), pltpu.SemaphoreType.DMA((nt,))],
                  start, finish)


def _rider_gather_forward(bufs, axes):
    nt = len(bufs)

    def copies(src, dst, send, recv):
        x, y, c, chips = _place()
        mine, theirs = [], []
        for t in range(nt):
            for k in range(3):
                slot = 2 * chips[k][0] + chips[k][1]
                for hc, into in ((c, mine), (1 - c, theirs)):
                    into.append(pltpu.make_async_remote_copy(
                        src_ref=_half(src[t], axes[t], hc, lead=(slot,)),
                        dst_ref=_half(dst[t], axes[t], hc, lead=(slot,)),
                        send_sem=send.at[t, k], recv_sem=recv.at[t, k], device_id=(x, y, 1 - c), device_id_type=MESH))
        return mine, theirs

    def start(src, dst, sems):
        for cp in copies(src, dst, *sems)[0]:
            cp.start()

    def finish(src, dst, sems):
        mine, theirs = copies(src, dst, *sems)
        for cp in theirs:
            cp.wait_recv()
        for cp in mine:
            cp.wait_send()

    return _Rider(bufs, [jax.ShapeDtypeStruct(a.shape, a.dtype) for a in bufs],
                  [pltpu.SemaphoreType.DMA((nt, 3)), pltpu.SemaphoreType.DMA((nt, 3))], start, finish,
                  aliases={t: t for t in range(nt)})


def _rider_chip_exchange(parts):
    nt = len(parts)

    def copies(src, got, send, recv):
        x, y, c, chips = _place()
        return [pltpu.make_async_remote_copy(
            src_ref=src[t].at[2 * chips[k][0] + chips[k][1]], dst_ref=got[t].at[k], send_sem=send.at[t, k],
            recv_sem=recv.at[t, k], device_id=(chips[k][0], chips[k][1], c), device_id_type=MESH)
            for t in range(nt) for k in range(3)]

    def start(src, got, sems):
        for cp in copies(src, got, *sems):
            cp.start()

    def finish(src, got, sems):
        remote = copies(src, got, *sems)
        for cp in remote:
            cp.wait_recv()
        for cp in remote:
            cp.wait_send()

    return _Rider(parts, [jax.ShapeDtypeStruct((3,) + a.shape[1:], a.dtype) for a in parts],
                  [pltpu.SemaphoreType.DMA((nt, 3)), pltpu.SemaphoreType.DMA((nt, 3))], start, finish)


def _gather_all(block, *, name):
    m_per, n = block.shape

    def body(x_ref, out_ref, send_sems, recv_sems, local_sem):
        x, y, c, chips = _place()
        me, sibling = (x, y, c), (x, y, 1 - c)

        def rows(px, py, pc):
            return out_ref.at[4 * px + 2 * py + pc]

        def copy(k, blk, to, src=None):
            return pltpu.make_async_remote_copy(
                src_ref=rows(*blk) if src is None else src, dst_ref=rows(*blk),
                send_sem=send_sems.at[k], recv_sem=recv_sems.at[k], device_id=to, device_id_type=MESH)

        mine = pltpu.make_async_copy(x_ref, rows(*me), local_sem)
        mine.start()
        first = [copy(0, me, sibling, src=x_ref)]
        first += [copy(1 + j, me, (*chip, c), src=x_ref) for j, chip in enumerate(chips)]
        for cp in first:
            cp.start()
        passed = [copy(4 + j, (*chip, c), sibling) for j, chip in enumerate(chips)]
        for j, chip in enumerate(chips):
            copy(1 + j, (*chip, c), me).wait_recv()
            passed[j].start()
        copy(0, sibling, me).wait_recv()
        for j, chip in enumerate(chips):
            copy(4 + j, (*chip, 1 - c), me).wait_recv()
        for cp in first + passed:
            cp.wait_send()
        mine.wait()

    return pl.pallas_call(
        body, name=name,
        out_shape=jax.ShapeDtypeStruct((N_DEV, m_per, n), block.dtype),
        in_specs=[pl.BlockSpec(memory_space=pltpu.VMEM)], out_specs=pl.BlockSpec(memory_space=pltpu.VMEM),
        scratch_shapes=[pltpu.SemaphoreType.DMA((7,)), pltpu.SemaphoreType.DMA((7,)), pltpu.SemaphoreType.DMA],
        compiler_params=pltpu.CompilerParams(vmem_limit_bytes=VMEM_LIMIT),
    )(block)


def _sum_slots(slots, *, name):
    n, m, c = slots.shape
    t = _rows_tile(m, c * n)

    def body(s_ref, o_ref):
        acc = s_ref[0]
        for k in range(1, n):
            acc = acc + s_ref[k]
        o_ref[...] = acc

    return pl.pallas_call(
        body, name=name, grid=(m // t,), in_specs=[pl.BlockSpec((n, t, c), lambda i: (0, i, 0))],
        out_specs=pl.BlockSpec((t, c), lambda i: (i, 0)), out_shape=jax.ShapeDtypeStruct((m, c), F32),
        compiler_params=_cparams("parallel"),
    )(slots)


def _pad_rows(a, rows):
    return a if a.shape[0] == rows else jnp.pad(a, ((0, rows - a.shape[0]), (0, 0)))


def _w_in_padded(shards):
    full = shards.reshape(IN_COLS, shards.shape[2])
    return jnp.concatenate([_pad_rows(full[SEG[n][2]:SEG[n][2] + SEG[n][3]], SEG[n][1]) for n in SEG_ORDER], axis=0)


def _w_in_unpadded(gp):
    full = jnp.concatenate([gp[SEG[n][0]:SEG[n][0] + SEG[n][3]] for n in ORIG_ORDER], axis=0)
    return full.reshape(N_CHIP, IN_COLS // N_CHIP, gp.shape[1])


def _pad_heads(w, true_w, pad_w):
    r = w.shape[0]
    h = w.shape[1] // true_w
    return jnp.pad(w.reshape(r, h, true_w), ((0, 0), (0, 0), (0, pad_w - true_w))).reshape(r, h * pad_w)


def _unpad_heads(w, true_w, pad_w):
    r = w.shape[0]
    h = w.shape[1] // pad_w
    return w.reshape(r, h, pad_w)[:, :, :true_w].reshape(r, h * true_w)


def _cols_to_slots(a):
    w = a.shape[1] // N_CHIP
    return jnp.stack([a[:, j * w:(j + 1) * w] for j in range(N_CHIP)])


def _slots_to_cols(a):
    return jnp.concatenate([a[j] for j in range(N_CHIP)], axis=1)


def _to_heads(a, h, d):
    return a.reshape(a.shape[0], h, d).transpose(1, 0, 2)


def _from_heads(a):
    return a.transpose(1, 0, 2).reshape(a.shape[1], -1)


SMALL = [("norm_g", 2048), ("ret_norm_g", 512), ("gla_ba_f", 256), ("gla_ba_b", 256), ("gla_norm_g", 512),
         ("pool_w", 4 * 128 * 128), ("pool_scale", 512), ("mla_q_norm_g", 512), ("mla_kv_norm_g", 256),
         ("mla_qk_norm_q", 192), ("mla_qk_norm_k", 192)]


def _pack_small(vals):
    parts = []
    for name, n in SMALL:
        v = vals[name].reshape(-1)
        parts.append(jnp.pad(v, (0, (-v.shape[0]) % 1024)))
    parts.append(jnp.pad(vals["loss"].reshape(-1), (0, 1023)))
    return jnp.concatenate(parts).reshape(-1, 128)


def _unpack_small(block):
    flat = block.reshape(-1)
    out, off = {}, 0
    for name, n in SMALL:
        out[name] = flat[off:off + DEPTH * n]
        off += DEPTH * n + (-(DEPTH * n)) % 1024
    out["loss"] = flat[off]
    return out


def _layer_weights(l, p, g):
    wa = jnp.zeros((128, 512), F32)
    wa = wa.at[0:GLA_RANK, 0:256].set(_slots_to_cols(g["gla_wa2_f"]))
    wa = wa.at[GLA_RANK:2 * GLA_RANK, 256:512].set(_slots_to_cols(g["gla_wa2_b"]))
    return dict(
        norm_g=p["norm_g"][l][None, :],
        w_in=_w_in_padded(g["w_in"]),
        w_out=g["w_out"].reshape(4 * g["w_out"].shape[1], -1),
        ret_norm_g=p["ret_norm_g"][l][None, :],
        wa=_bf(wa),
        ba=jnp.concatenate([p["gla_ba_f"][l], p["gla_ba_b"][l]])[None, :],
        gla_norm_g=p["gla_norm_g"][l][None, :],
        pool_w=_bf(p["pool_w"][l]),
        pool_scale=p["pool_scale"][l][None, :],
        qg=p["mla_q_norm_g"][l][None, :],
        wq=_pad_heads(_slots_to_cols(g["mla_wq_b"]), MLA_QK, MLA_QKP),
        kvg=p["mla_kv_norm_g"][l][None, :],
        wkv=_slots_to_cols(g["mla_wkv_b"]),
        qng=jnp.pad(p["mla_qk_norm_q"][l], (0, MLA_QKP - MLA_QK))[None, :],
        kng=jnp.pad(p["mla_qk_norm_k"][l], (0, MLA_QKP - MLA_QK))[None, :],
    )


def _layer_fwd(l, x, w, tabs, next_shards=None):
    ret_cos, ret_sin, mla_cos, mla_sp, mla_sn = tabs
    nm = lambda s: f"l{l}_{s}"
    h = _rmsnorm_fwd(x, w["norm_g"], name=nm("norm"))
    if next_shards is None:
        z = _matmul(h, w["w_in"], tb=True, name=nm("in_proj"))
    else:
        z, landed = _matmul(h, w["w_in"], tb=True, rider=_rider_gather_send(next_shards, SHARD_AXES),
                            name=nm("in_proj"))
    qr, kr = _ret_pre(z, ret_cos, ret_sin, name=nm("ret_pre"))
    ret_o = _bla(qr, kr, z, _ret_log_gamma(False), (0, 0, SEG["rv"][0] // 512), name=nm("ret_scan"))
    y_a = _post(ret_o, z, SEG["rg"][0] // 512, w["ret_norm_g"], norm=True, name=nm("ret_post"))
    la = _gla_gate(z, w["wa"], w["ba"], name=nm("gla_gate"))
    la_h = jnp.stack([_to_heads(la[:, :256], GLA_HEADS, GLA_DK), _to_heads(la[:, 256:], GLA_HEADS, GLA_DK)])
    gq = _to_heads(z[:, SEG["gq"][0]:SEG["gq"][0] + 256], GLA_HEADS, GLA_DK)
    gk = _to_heads(z[:, SEG["gk"][0]:SEG["gk"][0] + 256], GLA_HEADS, GLA_DK)
    gla_o, gla_st = _gla_fwd(gq, gk, z, la_h, name=nm("gla_scan"))
    y_b = _post(gla_o, z, SEG["gg"][0] // 512, w["gla_norm_g"], norm=True, name=nm("gla_post"))
    y_c = _pool_fwd(z, w["pool_w"], w["pool_scale"], name=nm("pool"))
    q, k, v = _mla_pre(z, w["qg"], w["wq"], w["kvg"], w["wkv"], w["qng"], w["kng"], mla_cos, mla_sp, mla_sn,
                       name=nm("mla_pre"))
    if next_shards is None:
        (att_o, lse), gathered = _flash_fwd(q, k, v, name=nm("attn")), None
    else:
        att_o, lse, gathered = _flash_fwd(q, k, v, rider=_rider_gather_forward(landed, SHARD_AXES), name=nm("attn"))
    y_d = _post([att_o], z, SEG["mg"][0] // 512, w["qg"], norm=False, name=nm("mla_post"))
    y = jnp.concatenate([y_a, y_b, y_c, y_d], axis=1)
    x_next = _matmul(y, w["w_out"], add=x, name=nm("out_proj"))
    saved = dict(x=x, h=h, z=z, y=y, qr=qr, kr=kr, ret_o=ret_o, la_h=la_h, gq=gq, gk=gk, gla_o=gla_o, gla_st=gla_st,
                 q=q, k=k, v=v, att_o=att_o, lse=lse)
    return x_next, saved, gathered


def _layer_bwd(l, dx_next, w, sv, tabs, riding_parts=None):
    ret_cos, ret_sin, mla_cos, mla_sp, mla_sn = tabs
    nm = lambda s: f"l{l}_{s}"
    z = sv["z"]
    dy = _matmul(dx_next, w["w_out"], tb=True, name=nm("out_proj_dy"))
    d_w_out = _matmul(sv["y"].T, dx_next, tn=512, name=nm("out_proj_dw"))
    d_rg, d_ret_o, d_ret_g = _post_bwd(dy, 0, sv["ret_o"], z, SEG["rg"][0] // 512, w["ret_norm_g"], norm=True,
                                       name=nm("ret_post_bwd"))
    vcol = SEG["rv"][0] // 512
    dqr = _bla(d_ret_o, z, sv["kr"], _ret_log_gamma(False), (0, vcol, 0), name=nm("ret_scan_dq"))
    dkr = _bla(z, d_ret_o, sv["qr"], _ret_log_gamma(True), (vcol, 0, 0), name=nm("ret_scan_dk"))
    drv = _bla(sv["kr"], sv["qr"], d_ret_o, _ret_log_gamma(True), (0, 0, 0), name=nm("ret_scan_dv"))
    d_rq, d_rk = _ret_pre_bwd(dqr, dkr, ret_cos, ret_sin, name=nm("ret_pre_bwd"))
    d_rv = _add_n([drv[0], drv[1]], out_dtype=BF16, name=nm("ret_dv_sum"))
    d_gg, d_gla_o, d_gla_g = _post_bwd(dy, 1, sv["gla_o"], z, SEG["gg"][0] // 512, w["gla_norm_g"], norm=True,
                                       name=nm("gla_post_bwd"))
    dq2, dk2, dla2, dv2 = _gla_bwd(sv["gq"], sv["gk"], z, sv["la_h"], d_gla_o, sv["gla_st"], name=nm("gla_scan_bwd"))
    d_gq = _bf(_from_heads(dq2[0] + dq2[1]))
    d_gk = _bf(_from_heads(dk2[0] + dk2[1]))
    d_gv = _add_n([dv2[0], dv2[1]], out_dtype=BF16, name=nm("gla_dv_sum"))
    dla = jnp.concatenate([_from_heads(dla2[0]), _from_heads(dla2[1])], axis=1)
    d_ga, d_wa, d_ba = _gla_gate_bwd(dla, z, w["wa"], w["ba"], name=nm("gla_gate_bwd"))
    d_pv, d_pg, d_pool_w, d_pool_scale = _pool_bwd(dy, z, w["pool_w"], w["pool_scale"], name=nm("pool_bwd"))
    d_mg, d_att_o, _ = _post_bwd(dy, 3, [sv["att_o"]], z, SEG["mg"][0] // 512, w["qg"], norm=False,
                                 name=nm("mla_post_bwd"))
    if riding_parts is None:
        (dq, dk, dv), rode = _flash_bwd(sv["q"], sv["k"], sv["v"], d_att_o, sv["att_o"], sv["lse"],
                                        name=nm("attn_bwd")), None
    else:
        dq, dk, dv, rode = _flash_bwd(sv["q"], sv["k"], sv["v"], d_att_o, sv["att_o"], sv["lse"],
                                      rider=_rider_chip_exchange(riding_parts), name=nm("attn_bwd"))
    d_mq, d_mkv, d_mkr, d_wq, d_wkv, d_qg, d_kvg, d_qng, d_kng = _mla_pre_bwd(
        dq, dk, dv, z, w["qg"], w["wq"], w["kvg"], w["wkv"], w["qng"], w["kng"], mla_cos, mla_sp, mla_sn,
        name=nm("mla_pre_bwd"))
    segs = dict(rq=d_rq, rk=d_rk, rv=d_rv, rg=d_rg, gv=d_gv, gg=d_gg, pv=d_pv, pg=d_pg, mq=d_mq, mg=d_mg,
                gq=d_gq, gk=d_gk, mkv=d_mkv, ga=d_ga, mkr=d_mkr)
    dz = jnp.concatenate([segs[n] for n in SEG_ORDER], axis=1)
    dh = _matmul(dz, w["w_in"], tn=512, name=nm("in_proj_dh"))
    d_w_in = _matmul(dz.T, sv["h"], name=nm("in_proj_dw"))
    dx, d_norm_g = _rmsnorm_bwd(sv["x"], dh, w["norm_g"], dx_next, name=nm("norm_bwd"))
    sharded = dict(
        w_in=_w_in_unpadded(d_w_in),
        w_out=d_w_out.reshape(N_CHIP, d_w_out.shape[0] // N_CHIP, d_w_out.shape[1]),
        mla_wq_b=_cols_to_slots(_unpad_heads(d_wq, MLA_QK, MLA_QKP)),
        mla_wkv_b=_cols_to_slots(d_wkv),
        gla_wa2_f=_cols_to_slots(d_wa[0:GLA_RANK, 0:256]),
        gla_wa2_b=_cols_to_slots(d_wa[GLA_RANK:2 * GLA_RANK, 256:512]),
    )
    small = dict(
        norm_g=d_norm_g[0], ret_norm_g=d_ret_g[0], gla_ba_f=d_ba[0, :256], gla_ba_b=d_ba[0, 256:],
        gla_norm_g=d_gla_g[0], pool_w=d_pool_w.reshape(-1), pool_scale=d_pool_scale[0], mla_q_norm_g=d_qg[0],
        mla_kv_norm_g=d_kvg[0], mla_qk_norm_q=d_qng[0, :MLA_QK], mla_qk_norm_k=d_kng[0, :MLA_QK],
    )
    return dx, sharded, small, rode


SHARDED = ["w_in", "w_out", "mla_wq_b", "mla_wkv_b", "gla_wa2_f", "gla_wa2_b"]
WEIGHTS = ["norm_g", "w_in", "ret_norm_g", "gla_wa2_f", "gla_ba_f", "gla_wa2_b", "gla_ba_b", "gla_norm_g", "pool_w",
           "pool_scale", "mla_q_norm_g", "mla_wq_b", "mla_kv_norm_g", "mla_wkv_b", "mla_qk_norm_q", "mla_qk_norm_k",
           "w_out"]


SHARD_AXES = [1, 0, 0, 0, 0, 0]


def _layer_shards(p, l):
    return [jnp.swapaxes(p["w_in"], 1, 2)[l].astype(BF16), p["w_out"][l].astype(BF16), p["mla_wq_b"][l].astype(BF16),
            p["mla_wkv_b"][l].astype(BF16), p["gla_wa2_f"][l], p["gla_wa2_b"][l]]


def _step(p, where):
    x = p["x"][0]
    tabs = _rope_tables(x.shape[0])
    got0 = _gather_shards(_layer_shards(p, 0), SHARD_AXES, name="l0_gather_weights")
    w0 = _layer_weights(0, p, dict(zip(SHARDED, got0)))
    x1, sv0, got1 = _layer_fwd(0, x, w0, tabs, next_shards=_layer_shards(p, 1))
    w1 = _layer_weights(1, p, dict(zip(SHARDED, got1)))
    x2, sv1, _ = _layer_fwd(1, x1, w1, tabs)
    dx, loss = _loss_head(x2, p["loss_target"][0], name="loss_head")

    def pair_sums(l, sharded):
        return [_pair_reduce(sharded[n], where, ax, out_dtype=BF16, name=f"l{l}_pair_reduce_{n}")
                for n, ax in zip(SHARDED, SHARD_AXES)]

    def joined(l, pair, others):
        return [_sum_join(a, b, where, ax, name=f"l{l}_sum_join_{n}")
                for n, a, b, ax in zip(SHARDED, pair, others, SHARD_AXES)]

    dx, sharded1, small1, _ = _layer_bwd(1, dx, w1, sv1, tabs)
    pair1 = pair_sums(1, sharded1)
    dx, sharded0, small0, others1 = _layer_bwd(0, dx, w0, sv0, tabs, riding_parts=pair1)
    grads1 = joined(1, pair1, others1)
    pair0 = pair_sums(0, sharded0)
    grads0 = joined(0, pair0, _chip_exchange(pair0, name="l0_chip_exchange"))
    grads = {n: jnp.stack([g0, g1]) for n, g0, g1 in zip(SHARDED, grads0, grads1)}
    small = {n: jnp.stack([small0[n], small1[n]]) for n, _ in SMALL}
    small["loss"] = loss
    return dx[None], grads, small


def kernel(x, norm_g, w_in, ret_norm_g, gla_wa2_f, gla_ba_f, gla_wa2_b, gla_ba_b, gla_norm_g, pool_w, pool_scale, mla_q_norm_g, mla_wq_b, mla_kv_norm_g, mla_wkv_b, mla_qk_norm_q, mla_qk_norm_k, w_out, loss_target, m_norm_g, m_w_in, m_ret_norm_g, m_gla_wa2_f, m_gla_ba_f, m_gla_wa2_b, m_gla_ba_b, m_gla_norm_g, m_pool_w, m_pool_scale, m_mla_q_norm_g, m_mla_wq_b, m_mla_kv_norm_g, m_mla_wkv_b, m_mla_qk_norm_q, m_mla_qk_norm_k, m_w_out, v_norm_g, v_w_in, v_ret_norm_g, v_gla_wa2_f, v_gla_ba_f, v_gla_wa2_b, v_gla_ba_b, v_gla_norm_g, v_pool_w, v_pool_scale, v_mla_q_norm_g, v_mla_wq_b, v_mla_kv_norm_g, v_mla_wkv_b, v_mla_qk_norm_q, v_mla_qk_norm_k, v_w_out):
    p = dict(x=x, norm_g=norm_g, w_in=w_in, ret_norm_g=ret_norm_g, gla_wa2_f=gla_wa2_f, gla_ba_f=gla_ba_f,
             gla_wa2_b=gla_wa2_b, gla_ba_b=gla_ba_b, gla_norm_g=gla_norm_g, pool_w=pool_w, pool_scale=pool_scale,
             mla_q_norm_g=mla_q_norm_g, mla_wq_b=mla_wq_b, mla_kv_norm_g=mla_kv_norm_g, mla_wkv_b=mla_wkv_b,
             mla_qk_norm_q=mla_qk_norm_q, mla_qk_norm_k=mla_qk_norm_k, w_out=w_out, loss_target=loss_target)
    moments = dict(
        m=dict(norm_g=m_norm_g, w_in=m_w_in, ret_norm_g=m_ret_norm_g, gla_wa2_f=m_gla_wa2_f, gla_ba_f=m_gla_ba_f,
               gla_wa2_b=m_gla_wa2_b, gla_ba_b=m_gla_ba_b, gla_norm_g=m_gla_norm_g, pool_w=m_pool_w,
               pool_scale=m_pool_scale, mla_q_norm_g=m_mla_q_norm_g, mla_wq_b=m_mla_wq_b,
               mla_kv_norm_g=m_mla_kv_norm_g, mla_wkv_b=m_mla_wkv_b, mla_qk_norm_q=m_mla_qk_norm_q,
               mla_qk_norm_k=m_mla_qk_norm_k, w_out=m_w_out),
        v=dict(norm_g=v_norm_g, w_in=v_w_in, ret_norm_g=v_ret_norm_g, gla_wa2_f=v_gla_wa2_f, gla_ba_f=v_gla_ba_f,
               gla_wa2_b=v_gla_wa2_b, gla_ba_b=v_gla_ba_b, gla_norm_g=v_gla_norm_g, pool_w=v_pool_w,
               pool_scale=v_pool_scale, mla_q_norm_g=v_mla_q_norm_g, mla_wq_b=v_mla_wq_b,
               mla_kv_norm_g=v_mla_kv_norm_g, mla_wkv_b=v_mla_wkv_b, mla_qk_norm_q=v_mla_qk_norm_q,
               mla_qk_norm_k=v_mla_qk_norm_k, w_out=v_w_out))

    where = jnp.stack([lax.axis_index("c"), 2 * lax.axis_index("x") + lax.axis_index("y")]).astype(jnp.int32)
    grad_x, grads, small = _step(p, where)

    slots = _gather_all(_pack_small(small), name="gather_small")
    total = _unpack_small(_sum_slots(slots, name="sum_small"))
    for n, _ in SMALL:
        grads[n] = total[n].reshape(p[n].shape)
    loss = total["loss"]

    delta, new_m, new_v = {}, {}, {}
    for n in WEIGHTS:
        turn = (lambda a: jnp.swapaxes(a, 1, 2)) if n == "w_in" else (lambda a: a)
        outs = _adamw(turn(p[n]), grads[n], turn(moments["m"][n]), turn(moments["v"][n]), name=f"adamw_{n}")
        grads[n] = turn(grads[n])
        delta[n], new_m[n], new_v[n] = (turn(o) for o in outs)
    return (loss, grad_x, *[grads[n] for n in WEIGHTS], *[delta[n] for n in WEIGHTS],
            *[new_m[n] for n in WEIGHTS], *[new_v[n] for n in WEIGHTS])
```

```python
import functools
import math

import jax
import jax.numpy as jnp
from jax import lax
from jax.experimental import pallas as pl
from jax.experimental.pallas import tpu as pltpu

F32 = jnp.float32
BF16 = jnp.bfloat16
MESH = pl.DeviceIdType.MESH

EPS = 1e-6
ROPE_THETA = 10000.0
DEPTH = 2
N_DEV = 8
N_CHIP = 4

GROUP_W = 512
RET_HEADS = 4
RET_HD = 128
RET_CHUNK = 128
GLA_HEADS = 4
GLA_DK = 64
GLA_DV = 128
GLA_RANK = 16
GLA_TAU = 16.0
GLA_CHUNK = 64
POOL_GROUPS = 4
POOL_GW = 128
POOL_HALO = 8
POOL_TILE = 256
MLA_HEADS = 4
MLA_NOPE = 128
MLA_ROPE = 64
MLA_QK = MLA_NOPE + MLA_ROPE
MLA_QKP = 256
MLA_V = 128
MLA_Q_RANK = 512
MLA_KV_RANK = 256
MLA_SCALE = MLA_QK ** -0.5

ADAM_LR = 0.001
ADAM_B1 = 0.9
ADAM_B2 = 0.999
ADAM_EPS = 1e-08
ADAM_WD = 0.01
ADAM_STEP = 10

VMEM_LIMIT = 56 * 1024 * 1024

SEG = {
    "rq": (0, 512, 0, 512), "rk": (512, 512, 512, 512), "rv": (1024, 512, 1024, 512), "rg": (1536, 512, 1536, 512),
    "gv": (2048, 512, 2560, 512), "gg": (2560, 512, 3072, 512),
    "pv": (3072, 512, 3616, 512), "pg": (3584, 512, 4128, 512),
    "mq": (4096, 512, 4640, 512), "mg": (4608, 512, 5472, 512),
    "gq": (5120, 256, 2048, 256), "gk": (5376, 256, 2304, 256), "mkv": (5632, 256, 5152, 256),
    "ga": (5888, 128, 3584, 32), "mkr": (6016, 128, 5408, 64),
}
SEG_ORDER = ["rq", "rk", "rv", "rg", "gv", "gg", "pv", "pg", "mq", "mg", "gq", "gk", "mkv", "ga", "mkr"]
IN_COLS = 5984
IN_PAD = 6144
ORIG_ORDER = ["rq", "rk", "rv", "rg", "gq", "gk", "gv", "gg", "ga", "pv", "pg", "mq", "mkv", "mkr", "mg"]


def _cparams(*sem):
    return pltpu.CompilerParams(dimension_semantics=tuple(sem), vmem_limit_bytes=VMEM_LIMIT)


def _bf(v):
    return v.astype(BF16)


def _dot(a, b, ca=1, cb=0):
    return lax.dot_general(_bf(a), _bf(b), (((ca,), (cb,)), ((), ())), preferred_element_type=F32)


def _split_dot(a01, x, ca=1, cb=0):
    hi = _bf(x)
    r1 = x - hi.astype(F32)
    mid = _bf(r1)
    lo = _bf(r1 - mid.astype(F32))
    dn = (((ca,), (cb,)), ((), ()))
    a = _bf(a01)
    return (lax.dot_general(a, hi, dn, preferred_element_type=F32)
            + lax.dot_general(a, mid, dn, preferred_element_type=F32)
            + lax.dot_general(a, lo, dn, preferred_element_type=F32))


def _sigmoid(x):
    return 1.0 / (1.0 + jnp.exp(-x))


def _silu_parts(g):
    sg = _sigmoid(g)
    return g * sg, sg * (1.0 + g * (1.0 - sg))


class _Rider:
    def __init__(self, ins, outs, sems, start, finish, aliases=None):
        self.ins, self.outs, self.sems, self.start, self.finish = list(ins), list(outs), list(sems), start, finish
        self.aliases = dict(aliases or {})


def _ride(body, rider, n_in, n_out, grid):
    if rider is None:
        return body
    ri, ro, rs = len(rider.ins), len(rider.outs), len(rider.sems)

    def wrapped(*refs):
        ins, refs = refs[:n_in], refs[n_in:]
        rin, refs = refs[:ri], refs[ri:]
        outs, refs = refs[:n_out], refs[n_out:]
        rout, refs = refs[:ro], refs[ro:]
        scratch, sems = refs[:len(refs) - rs], refs[len(refs) - rs:]
        first = pl.program_id(0) == 0
        last = pl.program_id(0) == grid[0] - 1
        for ax in range(1, len(grid)):
            first = jnp.logical_and(first, pl.program_id(ax) == 0)
            last = jnp.logical_and(last, pl.program_id(ax) == grid[ax] - 1)

        @pl.when(first)
        def _():
            rider.start(rin, rout, sems)

        body(*ins, *outs, *scratch)

        @pl.when(last)
        def _():
            rider.finish(rin, rout, sems)

    return wrapped


def _ride_call(body, rider, *, name, grid, in_specs, out_specs, out_shape, scratch_shapes, args, sem):
    n_in, n_out = len(in_specs), len(out_specs)
    if rider is None:
        return pl.pallas_call(body, name=name, grid=grid, in_specs=in_specs, out_specs=out_specs, out_shape=out_shape,
                              scratch_shapes=scratch_shapes, compiler_params=_cparams(*sem))(*args), []
    outs = pl.pallas_call(
        _ride(body, rider, n_in, n_out, grid), name=name, grid=grid,
        in_specs=list(in_specs) + [ANY] * len(rider.ins), out_specs=list(out_specs) + [ANY] * len(rider.outs),
        out_shape=list(out_shape) + rider.outs, scratch_shapes=list(scratch_shapes) + rider.sems,
        input_output_aliases={n_in + i: n_out + o for i, o in rider.aliases.items()},
        compiler_params=_cparams(*(["arbitrary"] * len(grid))),
    )(*args, *rider.ins)
    return outs[:n_out], outs[n_out:]


def _matmul(a, b, *, ta=False, tb=False, out_dtype=F32, tm=512, tn=1024, tk=None, add=None, n_outer=True, rider=None,
            name):
    m, kdim = (a.shape[1], a.shape[0]) if ta else a.shape
    n = b.shape[0] if tb else b.shape[1]
    tm, tn = min(tm, m), min(tn, n)
    tk = kdim if tk is None else min(tk, kdim)
    assert m % tm == 0 and n % tn == 0 and kdim % tk == 0
    nk = kdim // tk
    ca, cb = (0 if ta else 1), (1 if tb else 0)

    def body(*refs):
        if add is None:
            a_ref, b_ref, o_ref = refs[:3]
            add_ref = None
        else:
            a_ref, b_ref, add_ref, o_ref = refs[:4]
        p = _dot(a_ref[...], b_ref[...], ca, cb)

        def finish(r):
            if add_ref is not None:
                r = r + add_ref[...]
            o_ref[...] = r.astype(out_dtype)

        if nk == 1:
            finish(p)
        else:
            acc = refs[-1]
            k = pl.program_id(2)

            @pl.when(k == 0)
            def _():
                acc[...] = p

            @pl.when(k > 0)
            def _():
                acc[...] += p

            @pl.when(k == nk - 1)
            def _():
                finish(acc[...])

    def ij(g0, g1):
        return (g1, g0) if n_outer else (g0, g1)

    a_spec = (pl.BlockSpec((tk, tm), lambda g0, g1, k: (k, ij(g0, g1)[0])) if ta
              else pl.BlockSpec((tm, tk), lambda g0, g1, k: (ij(g0, g1)[0], k)))
    b_spec = (pl.BlockSpec((tn, tk), lambda g0, g1, k: (ij(g0, g1)[1], k)) if tb
              else pl.BlockSpec((tk, tn), lambda g0, g1, k: (k, ij(g0, g1)[1])))
    o_spec = pl.BlockSpec((tm, tn), lambda g0, g1, k: ij(g0, g1))
    in_specs = [a_spec, b_spec] + ([o_spec] if add is not None else [])
    args = (a, b) + ((add,) if add is not None else ())
    grid = (n // tn, m // tm, nk) if n_outer else (m // tm, n // tn, nk)
    (out,), rode = _ride_call(
        body, rider, name=name, grid=grid, in_specs=in_specs, out_specs=[o_spec],
        out_shape=[jax.ShapeDtypeStruct((m, n), out_dtype)],
        scratch_shapes=[] if nk == 1 else [pltpu.VMEM((tm, tn), F32)], args=args,
        sem=("parallel", "parallel", "arbitrary"))
    return out if rider is None else (out, rode)


def _rmsnorm_fwd(x, g, *, name, tm=256):
    s, d = x.shape
    tm = min(tm, s)

    def body(x_ref, g_ref, h_ref):
        xv = x_ref[...]
        r = lax.rsqrt(jnp.mean(xv * xv, axis=-1, keepdims=True) + EPS)
        h_ref[...] = _bf(xv * r * g_ref[...])

    return pl.pallas_call(
        body, name=name, grid=(s // tm,),
        in_specs=[pl.BlockSpec((tm, d), lambda i: (i, 0)), pl.BlockSpec((1, d), lambda i: (0, 0))],
        out_specs=pl.BlockSpec((tm, d), lambda i: (i, 0)),
        out_shape=jax.ShapeDtypeStruct((s, d), BF16),
        compiler_params=_cparams("parallel"),
    )(x, g)


def _rmsnorm_bwd(x, dh, g, dres, *, name, tm=256):
    s, d = x.shape
    tm = min(tm, s)

    def body(x_ref, dh_ref, g_ref, dres_ref, dx_ref, dg_ref):
        i = pl.program_id(0)
        xv = x_ref[...]
        r = lax.rsqrt(jnp.mean(xv * xv, axis=-1, keepdims=True) + EPS)
        xn = xv * r
        dv = dh_ref[...]
        part = jnp.sum(dv * xn, axis=0, keepdims=True)

        @pl.when(i == 0)
        def _():
            dg_ref[...] = part

        @pl.when(i > 0)
        def _():
            dg_ref[...] += part

        dxn = dv * g_ref[...]
        dx_ref[...] = dres_ref[...] + r * (dxn - xn * jnp.mean(dxn * xn, axis=-1, keepdims=True))

    row = pl.BlockSpec((tm, d), lambda i: (i, 0))
    vec = pl.BlockSpec((1, d), lambda i: (0, 0))
    return pl.pallas_call(
        body, name=name, grid=(s // tm,), in_specs=[row, row, vec, row], out_specs=[row, vec],
        out_shape=[jax.ShapeDtypeStruct((s, d), F32), jax.ShapeDtypeStruct((1, d), F32)],
        compiler_params=_cparams("arbitrary"),
    )(x, dh, g, dres)


def _loss_head(xf, target, *, name, tm=256):
    s, d = xf.shape
    tm = min(tm, s)

    def body(x_ref, t_ref, dx_ref, l_ref):
        i = pl.program_id(0)
        e = x_ref[...] - t_ref[...]
        dx_ref[...] = e * (1.0 / d)
        rows = jnp.mean(e * e, axis=-1, keepdims=True)
        part = 0.5 * jnp.sum(rows, axis=0, keepdims=True)

        @pl.when(i == 0)
        def _():
            l_ref[...] = part

        @pl.when(i > 0)
        def _():
            l_ref[...] += part

    row = pl.BlockSpec((tm, d), lambda i: (i, 0))
    return pl.pallas_call(
        body, name=name, grid=(s // tm,), in_specs=[row, row],
        out_specs=[row, pl.BlockSpec((1, 1), lambda i: (0, 0))],
        out_shape=[jax.ShapeDtypeStruct((s, d), F32), jax.ShapeDtypeStruct((1, 1), F32)],
        compiler_params=_cparams("arbitrary"),
    )(xf, target)


def _rope_tables(s):
    pos = jnp.arange(s, dtype=F32)[:, None]
    inv_r = 1.0 / (ROPE_THETA ** (jnp.arange(0, RET_HD, 2, dtype=F32) / RET_HD))
    ang = pos * inv_r[None, :]
    ret_cos = jnp.concatenate([jnp.cos(ang), jnp.cos(ang)], axis=1)
    ret_sin = jnp.concatenate([-jnp.sin(ang), jnp.sin(ang)], axis=1)
    inv_m = 1.0 / (ROPE_THETA ** (jnp.arange(0, MLA_ROPE, 2, dtype=F32) / MLA_ROPE))
    am = pos * inv_m[None, :]
    z32, z64 = jnp.zeros((s, 32), F32), jnp.zeros((s, 64), F32)
    mla_cos = jnp.concatenate([jnp.cos(am), jnp.cos(am), z64], axis=1)
    mla_sp = jnp.concatenate([z32, jnp.sin(am), z64], axis=1)
    mla_sn = jnp.concatenate([-jnp.sin(am), z32, z64], axis=1)
    return ret_cos, ret_sin, mla_cos, mla_sp, mla_sn


def _rope128(x, c, sg):
    return x * c + pltpu.roll(x, 64, 1) * sg


def _unrope128(d, c, sg):
    return d * c + pltpu.roll(d * sg, 64, 1)


def _rope64(t, c, sp, sn):
    return t * c + pltpu.roll(t, 96, 1) * sn + pltpu.roll(t, 32, 1) * sp


def _unrope64(d, c, sp, sn):
    return d * c + pltpu.roll(d * sn, 32, 1) + pltpu.roll(d * sp, 96, 1)


def _ret_pre(z, cos, sin, *, name, tm=256):
    s = z.shape[0]
    tm = min(tm, s)
    scale = RET_HD ** -0.5

    def body(q_ref, k_ref, c_ref, s_ref, qo_ref, ko_ref):
        c, sg = c_ref[...], s_ref[...]
        for h in range(RET_HEADS):
            sl = slice(h * RET_HD, (h + 1) * RET_HD)
            qo_ref[:, sl] = _rope128(q_ref[:, sl], c, sg)
            ko_ref[:, sl] = _rope128(k_ref[:, sl], c, sg) * scale

    seg = lambda j: pl.BlockSpec((tm, GROUP_W), lambda i: (i, j))
    tab = pl.BlockSpec((tm, RET_HD), lambda i: (i, 0))
    return pl.pallas_call(
        body, name=name, grid=(s // tm,), in_specs=[seg(0), seg(1), tab, tab],
        out_specs=[seg(0), seg(0)],
        out_shape=[jax.ShapeDtypeStruct((s, GROUP_W), F32)] * 2,
        compiler_params=_cparams("parallel"),
    )(z, z, cos, sin)


def _ret_pre_bwd(dqr, dkr, cos, sin, *, name, tm=256):
    s = dqr[0].shape[0]
    tm = min(tm, s)
    scale = RET_HD ** -0.5

    def body(dq0_ref, dq1_ref, dk0_ref, dk1_ref, c_ref, s_ref, qo_ref, ko_ref):
        c, sg = c_ref[...], s_ref[...]
        for h in range(RET_HEADS):
            sl = slice(h * RET_HD, (h + 1) * RET_HD)
            qo_ref[:, sl] = _bf(_unrope128(dq0_ref[:, sl] + dq1_ref[:, sl], c, sg))
            ko_ref[:, sl] = _bf(_unrope128(dk0_ref[:, sl] + dk1_ref[:, sl], c, sg) * scale)

    row = pl.BlockSpec((tm, GROUP_W), lambda i: (i, 0))
    tab = pl.BlockSpec((tm, RET_HD), lambda i: (i, 0))
    return pl.pallas_call(
        body, name=name, grid=(s // tm,), in_specs=[row, row, row, row, tab, tab], out_specs=[row, row],
        out_shape=[jax.ShapeDtypeStruct((s, GROUP_W), BF16)] * 2,
        compiler_params=_cparams("parallel"),
    )(dqr[0], dqr[1], dkr[0], dkr[1], cos, sin)


def _bla(a, b, c, lg, cols, *, name):
    s = a.shape[0]
    ch = min(RET_CHUNK, s)
    n = s // ch
    hd = RET_HD

    def body(lg_ref, a0, b0, c0, a1, b1, c1, o0, o1, st):
        t = pl.program_id(0)

        @pl.when(t == 0)
        def _():
            st[...] = jnp.zeros_like(st)

        ii = lax.broadcasted_iota(jnp.int32, (ch, ch), 0)
        jj = lax.broadcasted_iota(jnp.int32, (ch, ch), 1)
        idx = lax.broadcasted_iota(jnp.int32, (ch, 1), 0).astype(F32)
        for d, (a_ref, b_ref, c_ref, o_ref) in enumerate(((a0, b0, c0, o0), (a1, b1, c1, o1))):
            diff = ((ii - jj) if d == 0 else (jj - ii)).astype(F32)
            keep = diff >= 0
            dpos = jnp.maximum(diff, 0.0)
            pq = (idx + 1.0) if d == 0 else (ch - idx)
            pk = (ch - 1.0 - idx) if d == 0 else idx
            for h in range(RET_HEADS):
                g = lg_ref[d, h]
                sl = slice(h * hd, (h + 1) * hd)
                av, bv, cv = a_ref[:, sl], b_ref[:, sl], c_ref[:, sl]
                sc = _dot(av, bv, 1, 1) * jnp.where(keep, jnp.exp(dpos * g), 0.0)
                stv = st[d, h]
                o_ref[:, sl] = _dot(sc, cv) + _dot(av * jnp.exp(pq * g), stv)
                st[d, h] = jnp.exp(ch * g) * stv + _dot(bv * jnp.exp(pk * g), cv, 0, 0)

    fwd = lambda j: pl.BlockSpec((ch, GROUP_W), lambda t: (t, j))
    bwd = lambda j: pl.BlockSpec((ch, GROUP_W), lambda t: (n - 1 - t, j))
    return pl.pallas_call(
        body, name=name, grid=(n,),
        in_specs=[pl.BlockSpec(memory_space=pltpu.SMEM), fwd(cols[0]), fwd(cols[1]), fwd(cols[2]),
                  bwd(cols[0]), bwd(cols[1]), bwd(cols[2])],
        out_specs=[fwd(0), bwd(0)],
        out_shape=[jax.ShapeDtypeStruct((s, GROUP_W), F32)] * 2,
        scratch_shapes=[pltpu.VMEM((2, RET_HEADS, hd, hd), F32)],
        compiler_params=_cparams("arbitrary"),
    )(lg, a, b, c, a, b, c)


def _post(os_, zg, gcol, g, *, norm, name, tm=256):
    s = zg.shape[0]
    tm = min(tm, s)
    nd = len(os_)

    def body(*refs):
        o_refs, (gt_ref, g_ref, y_ref) = refs[:nd], refs[nd:]
        silu, _ = _silu_parts(gt_ref[...])
        for h in range(4):
            sl = slice(h * 128, (h + 1) * 128)
            o = o_refs[0][:, sl]
            for k in range(1, nd):
                o = o + o_refs[k][:, sl]
            if norm:
                r = lax.rsqrt(jnp.mean(o * o, axis=-1, keepdims=True) + EPS)
                o = o * r * g_ref[:, sl]
            y_ref[:, sl] = _bf(silu[:, sl] * o)

    row = pl.BlockSpec((tm, GROUP_W), lambda i: (i, 0))
    return pl.pallas_call(
        body, name=name, grid=(s // tm,),
        in_specs=[row] * nd + [pl.BlockSpec((tm, GROUP_W), lambda i: (i, gcol)),
                               pl.BlockSpec((1, GROUP_W), lambda i: (0, 0))],
        out_specs=row,
        out_shape=jax.ShapeDtypeStruct((s, GROUP_W), BF16),
        compiler_params=_cparams("parallel"),
    )(*os_, zg, g)


def _post_bwd(dy, ycol, os_, zg, gcol, g, *, norm, name, tm=256):
    s = zg.shape[0]
    tm = min(tm, s)
    nd = len(os_)

    def body(*refs):
        dy_ref, o_refs = refs[0], refs[1:1 + nd]
        gt_ref, g_ref, dgt_ref, do_ref, dg_ref = refs[1 + nd:]
        i = pl.program_id(0)
        silu, dsilu = _silu_parts(gt_ref[...])
        dyv = dy_ref[...]
        parts = []
        for h in range(4):
            sl = slice(h * 128, (h + 1) * 128)
            o = o_refs[0][:, sl]
            for k in range(1, nd):
                o = o + o_refs[k][:, sl]
            dn = dyv[:, sl] * silu[:, sl]
            if norm:
                r = lax.rsqrt(jnp.mean(o * o, axis=-1, keepdims=True) + EPS)
                xn = o * r
                gh = g_ref[:, sl]
                dgt_ref[:, sl] = _bf(dyv[:, sl] * (xn * gh) * dsilu[:, sl])
                parts.append(jnp.sum(dn * xn, axis=0, keepdims=True))
                dxn = dn * gh
                do_ref[:, sl] = r * (dxn - xn * jnp.mean(dxn * xn, axis=-1, keepdims=True))
            else:
                dgt_ref[:, sl] = _bf(dyv[:, sl] * o * dsilu[:, sl])
                parts.append(jnp.zeros((1, 128), F32))
                do_ref[:, sl] = dn
        part = jnp.concatenate(parts, axis=1)

        @pl.when(i == 0)
        def _():
            dg_ref[...] = part

        @pl.when(i > 0)
        def _():
            dg_ref[...] += part

    row = pl.BlockSpec((tm, GROUP_W), lambda i: (i, 0))
    vec = pl.BlockSpec((1, GROUP_W), lambda i: (0, 0))
    return pl.pallas_call(
        body, name=name, grid=(s // tm,),
        in_specs=[pl.BlockSpec((tm, GROUP_W), lambda i: (i, ycol))] + [row] * nd
        + [pl.BlockSpec((tm, GROUP_W), lambda i: (i, gcol)), vec],
        out_specs=[row, row, vec],
        out_shape=[jax.ShapeDtypeStruct((s, GROUP_W), BF16), jax.ShapeDtypeStruct((s, GROUP_W), F32),
                   jax.ShapeDtypeStruct((1, GROUP_W), F32)],
        compiler_params=_cparams("arbitrary"),
    )(dy, *os_, zg, g)


def _ret_log_gamma(swap):
    gf = 1.0 - 2.0 ** (-5.0 - jnp.arange(RET_HEADS, dtype=F32))
    lf, lb = jnp.log(gf), jnp.log(gf[::-1])
    return jnp.stack([lb, lf] if swap else [lf, lb])


def _log_sigmoid(x):
    return jnp.minimum(x, 0.0) - jnp.log(1.0 + jnp.exp(-jnp.abs(x)))


def _gla_gate(z, wa, ba, *, name, tm=256):
    s = z.shape[0]
    tm = min(tm, s)
    col = SEG["ga"][0] // 128

    def body(ga_ref, wa_ref, ba_ref, la_ref):
        pre = _dot(ga_ref[...], wa_ref[...]) + ba_ref[...]
        la_ref[...] = _log_sigmoid(pre) / GLA_TAU

    return pl.pallas_call(
        body, name=name, grid=(s // tm,),
        in_specs=[pl.BlockSpec((tm, 128), lambda i: (i, col)), pl.BlockSpec((128, 512), lambda i: (0, 0)),
                  pl.BlockSpec((1, 512), lambda i: (0, 0))],
        out_specs=pl.BlockSpec((tm, 512), lambda i: (i, 0)),
        out_shape=jax.ShapeDtypeStruct((s, 512), F32),
        compiler_params=_cparams("parallel"),
    )(z, wa, ba)


def _gla_gate_bwd(dla, z, wa, ba, *, name, tm=256):
    s = z.shape[0]
    tm = min(tm, s)
    col = SEG["ga"][0] // 128

    def body(dla_ref, ga_ref, wa_ref, ba_ref, dga_ref, dwa_ref, dba_ref):
        i = pl.program_id(0)
        gav = ga_ref[...]
        pre = _dot(gav, wa_ref[...]) + ba_ref[...]
        dpre = dla_ref[...] * (1.0 - _sigmoid(pre)) * (1.0 / GLA_TAU)
        dga_ref[...] = _bf(_dot(dpre, wa_ref[...], 1, 1))
        pw = _dot(gav, dpre, 0, 0)
        pb = jnp.sum(dpre, axis=0, keepdims=True)

        @pl.when(i == 0)
        def _():
            dwa_ref[...] = pw
            dba_ref[...] = pb

        @pl.when(i > 0)
        def _():
            dwa_ref[...] += pw
            dba_ref[...] += pb

    return pl.pallas_call(
        body, name=name, grid=(s // tm,),
        in_specs=[pl.BlockSpec((tm, 512), lambda i: (i, 0)), pl.BlockSpec((tm, 128), lambda i: (i, col)),
                  pl.BlockSpec((128, 512), lambda i: (0, 0)), pl.BlockSpec((1, 512), lambda i: (0, 0))],
        out_specs=[pl.BlockSpec((tm, 128), lambda i: (i, 0)), pl.BlockSpec((128, 512), lambda i: (0, 0)),
                   pl.BlockSpec((1, 512), lambda i: (0, 0))],
        out_shape=[jax.ShapeDtypeStruct((s, 128), BF16), jax.ShapeDtypeStruct((128, 512), F32),
                   jax.ShapeDtypeStruct((1, 512), F32)],
        compiler_params=_cparams("arbitrary"),
    )(dla, z, wa, ba)


def _gla_masks(ch):
    ii = lax.broadcasted_iota(jnp.int32, (ch, ch), 0)
    tt = lax.broadcasted_iota(jnp.int32, (ch, ch), 1)
    return jnp.where(tt <= ii, 1.0, 0.0), jnp.where(tt >= ii, 1.0, 0.0)


def _running_sum(x, up):
    n = x.shape[0]
    rows = lax.broadcasted_iota(jnp.int32, x.shape, 0)
    k = 1
    while k < n:
        if up:
            x = x + jnp.where(rows < n - k, pltpu.roll(x, n - k, 0), 0.0)
        else:
            x = x + jnp.where(rows >= k, pltpu.roll(x, k, 0), 0.0)
        k *= 2
    return x


def _gla_chunk(d, tmat, qv, kv, lav, ch):
    c = _running_sum(lav, up=(d == 1))
    big_l = c[ch - 1:ch, :] if d == 0 else c[0:1, :]
    qt = qv * (GLA_DK ** -0.5) * jnp.exp(c)
    kt = kv * jnp.exp(-c)
    kh = kv * jnp.exp(big_l - c)
    return c, big_l, qt, kt, kh


def _gla_fwd(qh, kh_, z, la, *, name):
    s = z.shape[0]
    ch = min(GLA_CHUNK, s)
    n = s // ch
    vcol = SEG["gv"][0] // GROUP_W

    def body(q0, k0, v0, la0, q1, k1, v1, la1, o0, o1, zs0, zs1, st):
        t = pl.program_id(0)

        @pl.when(t == 0)
        def _():
            st[...] = jnp.zeros_like(st)

        masks = _gla_masks(ch)
        for d, (q_ref, k_ref, v_ref, la_ref, o_ref, zs_ref) in enumerate(
                ((q0, k0, v0, la0, o0, zs0), (q1, k1, v1, la1, o1, zs1))):
            for h in range(GLA_HEADS):
                c, big_l, qt, kt, kh = _gla_chunk(d, masks[d], q_ref[h], k_ref[h], la_ref[0, h], ch)
                vv = v_ref[:, h * GLA_DV:(h + 1) * GLA_DV]
                p = _dot(qt, kt, 1, 1) * masks[d]
                zst = st[d, h]
                o_ref[:, h * GLA_DV:(h + 1) * GLA_DV] = _dot(p, vv) + _dot(qt, zst, 1, 1)
                zs_ref[h, 0] = zst
                st[d, h] = zst * jnp.exp(big_l) + _dot(vv, kh, 0, 0)

    cidx = (lambda t: t), (lambda t: n - 1 - t)
    hs = lambda d: pl.BlockSpec((GLA_HEADS, ch, GLA_DK), lambda t: (0, cidx[d](t), 0))
    vs = lambda d: pl.BlockSpec((ch, GROUP_W), lambda t: (cidx[d](t), vcol))
    las = lambda d: pl.BlockSpec((1, GLA_HEADS, ch, GLA_DK), lambda t: (d, 0, cidx[d](t), 0))
    os_ = lambda d: pl.BlockSpec((ch, GROUP_W), lambda t: (cidx[d](t), 0))
    zss = lambda d: pl.BlockSpec((GLA_HEADS, 1, GLA_DV, GLA_DK), lambda t: (0, cidx[d](t), 0, 0))
    o0, o1, zs0, zs1 = pl.pallas_call(
        body, name=name, grid=(n,),
        in_specs=[hs(0), hs(0), vs(0), las(0), hs(1), hs(1), vs(1), las(1)],
        out_specs=[os_(0), os_(1), zss(0), zss(1)],
        out_shape=[jax.ShapeDtypeStruct((s, GROUP_W), F32)] * 2
        + [jax.ShapeDtypeStruct((GLA_HEADS, n, GLA_DV, GLA_DK), F32)] * 2,
        scratch_shapes=[pltpu.VMEM((2, GLA_HEADS, GLA_DV, GLA_DK), F32)],
        compiler_params=_cparams("arbitrary"),
    )(qh, kh_, z, la, qh, kh_, z, la)
    return (o0, o1), (zs0, zs1)


def _gla_bwd(qh, kh_, z, la, do, zs, *, name):
    s = z.shape[0]
    ch = min(GLA_CHUNK, s)
    n = s // ch
    vcol = SEG["gv"][0] // GROUP_W

    def body(q0, k0, v0, la0, do0, zs0, q1, k1, v1, la1, do1, zs1,
             dq0, dk0, dla0, dv0, dq1, dk1, dla1, dv1, gz):
        t = pl.program_id(0)

        @pl.when(t == 0)
        def _():
            gz[...] = jnp.zeros_like(gz)

        masks = _gla_masks(ch)
        rows = lax.broadcasted_iota(jnp.int32, (ch, 1), 0)
        for d, (q_ref, k_ref, v_ref, la_ref, do_ref, zs_ref, dq_ref, dk_ref, dla_ref, dv_ref) in enumerate(
                ((q0, k0, v0, la0, do0, zs0, dq0, dk0, dla0, dv0), (q1, k1, v1, la1, do1, zs1, dq1, dk1, dla1, dv1))):
            tmat = masks[d]
            end = ch - 1 if d == 0 else 0
            for h in range(GLA_HEADS):
                c, big_l, qt, kt, kh = _gla_chunk(d, tmat, q_ref[h], k_ref[h], la_ref[0, h], ch)
                vsl = slice(h * GLA_DV, (h + 1) * GLA_DV)
                vv, dov, zst, gzv = v_ref[:, vsl], do_ref[:, vsl], zs_ref[h, 0], gz[d, h]
                p = _dot(qt, kt, 1, 1) * tmat
                dp = _dot(dov, vv, 1, 1) * tmat
                dqt = _dot(dp, kt) + _dot(dov, zst)
                dkt = _dot(dp, qt, 0, 0)
                dkh = _dot(vv, gzv)
                dv_ref[:, vsl] = _dot(p, dov, 0, 0) + _dot(kh, gzv, 1, 1)
                dq_ref[h] = dqt * jnp.exp(c) * (GLA_DK ** -0.5)
                dk_ref[h] = dkt * jnp.exp(-c) + dkh * jnp.exp(big_l - c)
                e_l = jnp.exp(big_l)
                d_l = jnp.sum(dkh * kh, axis=0, keepdims=True) + e_l * jnp.sum(zst * gzv, axis=0, keepdims=True)
                dc = dqt * qt - dkt * kt - dkh * kh + jnp.where(rows == end, d_l, 0.0)
                dla_ref[h] = _running_sum(dc, up=(d == 0))
                gz[d, h] = gzv * e_l + _dot(dov, qt, 0, 0)

    cidx = (lambda t: n - 1 - t), (lambda t: t)
    hs = lambda d: pl.BlockSpec((GLA_HEADS, ch, GLA_DK), lambda t: (0, cidx[d](t), 0))
    vs = lambda d: pl.BlockSpec((ch, GROUP_W), lambda t: (cidx[d](t), vcol))
    las = lambda d: pl.BlockSpec((1, GLA_HEADS, ch, GLA_DK), lambda t: (d, 0, cidx[d](t), 0))
    row = lambda d: pl.BlockSpec((ch, GROUP_W), lambda t: (cidx[d](t), 0))
    zss = lambda d: pl.BlockSpec((GLA_HEADS, 1, GLA_DV, GLA_DK), lambda t: (0, cidx[d](t), 0, 0))
    hshape = jax.ShapeDtypeStruct((GLA_HEADS, s, GLA_DK), F32)
    wide = jax.ShapeDtypeStruct((s, GROUP_W), F32)
    outs = pl.pallas_call(
        body, name=name, grid=(n,),
        in_specs=[hs(0), hs(0), vs(0), las(0), row(0), zss(0), hs(1), hs(1), vs(1), las(1), row(1), zss(1)],
        out_specs=[hs(0), hs(0), hs(0), row(0), hs(1), hs(1), hs(1), row(1)],
        out_shape=[hshape, hshape, hshape, wide, hshape, hshape, hshape, wide],
        scratch_shapes=[pltpu.VMEM((2, GLA_HEADS, GLA_DV, GLA_DK), F32)],
        compiler_params=_cparams("arbitrary"),
    )(qh, kh_, z, la, do, zs[0], qh, kh_, z, la, do, zs[1])
    dq0, dk0, dla0, dv0, dq1, dk1, dla1, dv1 = outs
    return (dq0, dq1), (dk0, dk1), (dla0, dla1), (dv0, dv1)


def _band(lo, hi, rows, width):
    r = lax.broadcasted_iota(jnp.int32, (rows, width), 0)
    j = lax.broadcasted_iota(jnp.int32, (rows, width), 1)
    k = j - POOL_HALO - r
    return jnp.where((k >= lo) & (k <= hi), 1.0, 0.0)


def _pool_cnt(t0, half, rows, s):
    t = t0 + lax.broadcasted_iota(jnp.int32, (rows, 1), 0)
    return (jnp.minimum(t + half, s) - jnp.maximum(t - half, 0)).astype(F32)


def _pool_fwd(z, pw, scale, *, name):
    s = z.shape[0]
    tl = min(POOL_TILE, s)
    nt = s // tl
    ucol, gcol = SEG["pv"][0] // 128, SEG["pg"][0] // 128

    def body(u_ref, gt_ref, pw_ref, sc_ref, y_ref, pad):
        g = pl.program_id(0)
        half = jnp.left_shift(1, g)
        pad[0:POOL_HALO, :] = jnp.zeros((POOL_HALO, POOL_GW), F32)
        pad[POOL_HALO + s:POOL_HALO + s + POOL_HALO, :] = jnp.zeros((POOL_HALO, POOL_GW), F32)
        pad[POOL_HALO:POOL_HALO + s, :] = u_ref[...]
        band = _band(-half, half - 1, tl, tl + 2 * POOL_HALO)
        pwv, scv = pw_ref[0], sc_ref[...]

        def tile(i, carry):
            t0 = pl.multiple_of(i * tl, tl)
            win = pad[pl.ds(t0, tl + 2 * POOL_HALO), :]
            u = win[POOL_HALO:POOL_HALO + tl, :]
            pooled = _split_dot(band, win) / _pool_cnt(t0, half, tl, s) - u
            mixed = _dot(pooled, pwv)
            silu, _ = _silu_parts(gt_ref[pl.ds(t0, tl), :])
            y_ref[pl.ds(t0, tl), :] = _bf(silu * (mixed * scv))
            return carry

        lax.fori_loop(0, nt, tile, 0)

    return pl.pallas_call(
        body, name=name, grid=(POOL_GROUPS,),
        in_specs=[pl.BlockSpec((s, POOL_GW), lambda g: (0, ucol + g)),
                  pl.BlockSpec((s, POOL_GW), lambda g: (0, gcol + g)),
                  pl.BlockSpec((1, POOL_GW, POOL_GW), lambda g: (g, 0, 0)),
                  pl.BlockSpec((1, POOL_GW), lambda g: (0, g))],
        out_specs=pl.BlockSpec((s, POOL_GW), lambda g: (0, g)),
        out_shape=jax.ShapeDtypeStruct((s, GROUP_W), BF16),
        scratch_shapes=[pltpu.VMEM((s + 2 * POOL_HALO, POOL_GW), F32)],
        compiler_params=_cparams("parallel"),
    )(z, z, pw, scale)


def _pool_bwd(dy, z, pw, scale, *, name):
    s = z.shape[0]
    tl = min(POOL_TILE, s)
    nt = s // tl
    ucol, gcol, ycol = SEG["pv"][0] // 128, SEG["pg"][0] // 128, 2 * GROUP_W // 128

    def body(dy_ref, u_ref, gt_ref, pw_ref, sc_ref, du_ref, dgt_ref, dpw_ref, dsc_ref, pad, epad, dpo):
        g = pl.program_id(0)
        half = jnp.left_shift(1, g)
        zeros = jnp.zeros((POOL_HALO, POOL_GW), F32)
        for buf in (pad, epad):
            buf[0:POOL_HALO, :] = zeros
            buf[POOL_HALO + s:POOL_HALO + s + POOL_HALO, :] = zeros
        pad[POOL_HALO:POOL_HALO + s, :] = u_ref[...]
        band = _band(-half, half - 1, tl, tl + 2 * POOL_HALO)
        band_t = _band(1 - half, half, tl, tl + 2 * POOL_HALO)
        pwv, scv = pw_ref[0], sc_ref[...]
        dpw_ref[0] = jnp.zeros((POOL_GW, POOL_GW), F32)
        dsc_ref[...] = jnp.zeros((1, POOL_GW), F32)

        def tile(i, carry):
            t0 = pl.multiple_of(i * tl, tl)
            win = pad[pl.ds(t0, tl + 2 * POOL_HALO), :]
            u = win[POOL_HALO:POOL_HALO + tl, :]
            cnt = _pool_cnt(t0, half, tl, s)
            pooled = _split_dot(band, win) / cnt - u
            mixed = _dot(pooled, pwv)
            silu, dsilu = _silu_parts(gt_ref[pl.ds(t0, tl), :])
            dyv = dy_ref[pl.ds(t0, tl), :]
            dgt_ref[pl.ds(t0, tl), :] = _bf(dyv * (mixed * scv) * dsilu)
            dsc_ref[...] += jnp.sum(dyv * silu * mixed, axis=0, keepdims=True)
            dm = dyv * silu * scv
            dpw_ref[0] += _dot(pooled, dm, 0, 0)
            dpooled = _dot(dm, pwv, 1, 1)
            dpo[pl.ds(t0, tl), :] = dpooled
            epad[pl.ds(POOL_HALO + t0, tl), :] = dpooled / cnt
            return carry

        lax.fori_loop(0, nt, tile, 0)

        def tile2(i, carry):
            t0 = pl.multiple_of(i * tl, tl)
            ewin = epad[pl.ds(t0, tl + 2 * POOL_HALO), :]
            du_ref[pl.ds(t0, tl), :] = _bf(_split_dot(band_t, ewin) - dpo[pl.ds(t0, tl), :])
            return carry

        lax.fori_loop(0, nt, tile2, 0)

    col = lambda c0: pl.BlockSpec((s, POOL_GW), lambda g: (0, c0 + g))
    return pl.pallas_call(
        body, name=name, grid=(POOL_GROUPS,),
        in_specs=[col(ycol), col(ucol), col(gcol), pl.BlockSpec((1, POOL_GW, POOL_GW), lambda g: (g, 0, 0)),
                  pl.BlockSpec((1, POOL_GW), lambda g: (0, g))],
        out_specs=[col(0), col(0), pl.BlockSpec((1, POOL_GW, POOL_GW), lambda g: (g, 0, 0)),
                   pl.BlockSpec((1, POOL_GW), lambda g: (0, g))],
        out_shape=[jax.ShapeDtypeStruct((s, GROUP_W), BF16), jax.ShapeDtypeStruct((s, GROUP_W), BF16),
                   jax.ShapeDtypeStruct((POOL_GROUPS, POOL_GW, POOL_GW), F32),
                   jax.ShapeDtypeStruct((1, GROUP_W), F32)],
        scratch_shapes=[pltpu.VMEM((s + 2 * POOL_HALO, POOL_GW), F32), pltpu.VMEM((s + 2 * POOL_HALO, POOL_GW), F32),
                        pltpu.VMEM((s, POOL_GW), F32)],
        compiler_params=_cparams("parallel"),
    )(dy, z, z, pw, scale)


def _mla_specs(tm):
    zq = pl.BlockSpec((tm, 512), lambda i: (i, SEG["mq"][0] // 512))
    zkv = pl.BlockSpec((tm, 256), lambda i: (i, SEG["mkv"][0] // 256))
    zkr = pl.BlockSpec((tm, 128), lambda i: (i, SEG["mkr"][0] // 128))
    full = lambda r, c: pl.BlockSpec((r, c), lambda i: (0, 0))
    tab = pl.BlockSpec((tm, 128), lambda i: (i, 0))
    weights = [full(1, 512), full(512, 1024), full(1, 256), full(256, 1024), full(1, 256), full(1, 256)]
    return [zq, zkv, zkr] + weights + [tab, tab, tab]


def _mla_project(xq_ref, xkv_ref, qg_ref, wq_ref, kvg_ref, wkv_ref):
    xq = xq_ref[...]
    r1 = lax.rsqrt(jnp.mean(xq * xq, axis=-1, keepdims=True) + EPS)
    xn1 = xq * r1
    qn = _bf(xn1 * qg_ref[...])
    qraw = _dot(qn, wq_ref[...])
    xkv = xkv_ref[...]
    r2 = lax.rsqrt(jnp.mean(xkv * xkv, axis=-1, keepdims=True) + EPS)
    xn2 = xkv * r2
    kvn = _bf(xn2 * kvg_ref[...])
    kvraw = _dot(kvn, wkv_ref[...])
    return r1, xn1, qn, qraw, r2, xn2, kvn, kvraw


def _mla_pre(z, qg, wq, kvg, wkv, qng, kng, cos, sp, sn, *, name, tm=256):
    s = z.shape[0]
    tm = min(tm, s)

    def body(xq_ref, xkv_ref, pe_ref, qg_ref, wq_ref, kvg_ref, wkv_ref, qng_ref, kng_ref, c_ref, sp_ref, sn_ref,
             q_ref, k_ref, v_ref):
        _, _, _, qraw, _, _, _, kvraw = _mla_project(xq_ref, xkv_ref, qg_ref, wq_ref, kvg_ref, wkv_ref)
        c, spv, snv = c_ref[...], sp_ref[...], sn_ref[...]
        pe = pe_ref[...]
        pe_ss = jnp.sum(pe * pe, axis=-1, keepdims=True)
        qngv, kngv = qng_ref[...], kng_ref[...]
        for h in range(MLA_HEADS):
            b = h * MLA_QKP
            qh = qraw[:, b:b + MLA_QKP]
            r = lax.rsqrt(jnp.sum(qh * qh, axis=-1, keepdims=True) * (1.0 / MLA_QK) + EPS)
            qn_h = qh * r * qngv
            q_ref[:, b:b + 128] = _bf(qn_h[:, :128] * MLA_SCALE)
            q_ref[:, b + 128:b + 256] = _bf(_rope64(qn_h[:, 128:], c, spv, snv) * MLA_SCALE)
            kn = kvraw[:, b:b + 128]
            rk = lax.rsqrt((jnp.sum(kn * kn, axis=-1, keepdims=True) + pe_ss) * (1.0 / MLA_QK) + EPS)
            k_ref[:, b:b + 128] = _bf(kn * rk * kngv[:, :128])
            k_ref[:, b + 128:b + 256] = _bf(_rope64(pe * rk * kngv[:, 128:], c, spv, snv))
            v_ref[:, h * MLA_V:(h + 1) * MLA_V] = _bf(kvraw[:, b + 128:b + 256])

    row = lambda w: pl.BlockSpec((tm, w), lambda i: (i, 0))
    return pl.pallas_call(
        body, name=name, grid=(s // tm,), in_specs=_mla_specs(tm),
        out_specs=[row(1024), row(1024), row(512)],
        out_shape=[jax.ShapeDtypeStruct((s, 1024), BF16), jax.ShapeDtypeStruct((s, 1024), BF16),
                   jax.ShapeDtypeStruct((s, 512), BF16)],
        compiler_params=_cparams("parallel"),
    )(z, z, z, qg, wq, kvg, wkv, qng, kng, cos, sp, sn)


def _mla_pre_bwd(dq, dk, dv, z, qg, wq, kvg, wkv, qng, kng, cos, sp, sn, *, name, tm=256):
    s = z.shape[0]
    tm = min(tm, s)

    def body(dq_ref, dk_ref, dv_ref, xq_ref, xkv_ref, pe_ref, qg_ref, wq_ref, kvg_ref, wkv_ref, qng_ref, kng_ref,
             c_ref, sp_ref, sn_ref, dxq_ref, dxkv_ref, dpe_ref, dwq_ref, dwkv_ref, dqg_ref, dkvg_ref, dqng_ref,
             dkng_ref, dqraw, dkvraw):
        i = pl.program_id(0)
        r1, xn1, qn, qraw, r2, xn2, kvn, kvraw = _mla_project(xq_ref, xkv_ref, qg_ref, wq_ref, kvg_ref, wkv_ref)
        c, spv, snv = c_ref[...], sp_ref[...], sn_ref[...]
        pe = pe_ref[...]
        pe_ss = jnp.sum(pe * pe, axis=-1, keepdims=True)
        qngv, kngv = qng_ref[...], kng_ref[...]
        dqng = jnp.zeros((1, MLA_QKP), F32)
        dkng = jnp.zeros((1, MLA_QKP), F32)
        dpe = jnp.zeros_like(pe)
        for h in range(MLA_HEADS):
            b = h * MLA_QKP
            qh = qraw[:, b:b + MLA_QKP]
            r = lax.rsqrt(jnp.sum(qh * qh, axis=-1, keepdims=True) * (1.0 / MLA_QK) + EPS)
            xn = qh * r
            d_n = jnp.concatenate(
                [dq_ref[:, b:b + 128], _unrope64(dq_ref[:, b + 128:b + 256], c, spv, snv)], axis=1) * MLA_SCALE
            dqng = dqng + jnp.sum(d_n * xn, axis=0, keepdims=True)
            dxn = d_n * qngv
            dqraw[:, b:b + MLA_QKP] = _bf(r * (dxn - xn * (jnp.sum(dxn * xn, axis=-1, keepdims=True) * (1.0 / MLA_QK))))
            kn = kvraw[:, b:b + 128]
            rk = lax.rsqrt((jnp.sum(kn * kn, axis=-1, keepdims=True) + pe_ss) * (1.0 / MLA_QK) + EPS)
            xk = jnp.concatenate([kn, pe], axis=1) * rk
            d_k = jnp.concatenate(
                [dk_ref[:, b:b + 128], _unrope64(dk_ref[:, b + 128:b + 256], c, spv, snv)], axis=1)
            dkng = dkng + jnp.sum(d_k * xk, axis=0, keepdims=True)
            dxk = d_k * kngv
            dfull = rk * (dxk - xk * (jnp.sum(dxk * xk, axis=-1, keepdims=True) * (1.0 / MLA_QK)))
            dkvraw[:, b:b + 128] = _bf(dfull[:, :128])
            dkvraw[:, b + 128:b + 256] = _bf(dv_ref[:, h * MLA_V:(h + 1) * MLA_V])
            dpe = dpe + dfull[:, 128:]
        dpe_ref[...] = _bf(dpe)
        dqr, dkvr = dqraw[...], dkvraw[...]
        dqn = _dot(dqr, wq_ref[...], 1, 1)
        dxn1 = dqn * qg_ref[...]
        dxq_ref[...] = _bf(r1 * (dxn1 - xn1 * jnp.mean(dxn1 * xn1, axis=-1, keepdims=True)))
        dkvn = _dot(dkvr, wkv_ref[...], 1, 1)
        dxn2 = dkvn * kvg_ref[...]
        dxkv_ref[...] = _bf(r2 * (dxn2 - xn2 * jnp.mean(dxn2 * xn2, axis=-1, keepdims=True)))
        parts = (_dot(qn, dqr, 0, 0), _dot(kvn, dkvr, 0, 0), jnp.sum(dqn * xn1, axis=0, keepdims=True),
                 jnp.sum(dkvn * xn2, axis=0, keepdims=True), dqng, dkng)
        accs = (dwq_ref, dwkv_ref, dqg_ref, dkvg_ref, dqng_ref, dkng_ref)

        @pl.when(i == 0)
        def _():
            for a, p in zip(accs, parts):
                a[...] = p

        @pl.when(i > 0)
        def _():
            for a, p in zip(accs, parts):
                a[...] += p

    row = lambda w: pl.BlockSpec((tm, w), lambda i: (i, 0))
    full = lambda r, c: pl.BlockSpec((r, c), lambda i: (0, 0))
    return pl.pallas_call(
        body, name=name, grid=(s // tm,),
        in_specs=[row(1024), row(1024), row(512)] + _mla_specs(tm),
        out_specs=[row(512), row(256), row(128), full(512, 1024), full(256, 1024), full(1, 512), full(1, 256),
                   full(1, 256), full(1, 256)],
        out_shape=[jax.ShapeDtypeStruct((s, 512), BF16), jax.ShapeDtypeStruct((s, 256), BF16),
                   jax.ShapeDtypeStruct((s, 128), BF16), jax.ShapeDtypeStruct((512, 1024), F32),
                   jax.ShapeDtypeStruct((256, 1024), F32), jax.ShapeDtypeStruct((1, 512), F32),
                   jax.ShapeDtypeStruct((1, 256), F32), jax.ShapeDtypeStruct((1, 256), F32),
                   jax.ShapeDtypeStruct((1, 256), F32)],
        scratch_shapes=[pltpu.VMEM((tm, 1024), BF16), pltpu.VMEM((tm, 1024), BF16)],
        compiler_params=_cparams("arbitrary"),
    )(dq, dk, dv, z, z, z, qg, wq, kvg, wkv, qng, kng, cos, sp, sn)


def _flash_fwd(q, k, v, *, name, tq=1024, tk=1024, rider=None):
    s = q.shape[0]
    tq, tk = min(tq, s), min(tk, s)
    nk = s // tk

    def body(q_ref, k_ref, v_ref, o_ref, lse_ref, m_s, l_s, acc):
        j = pl.program_id(2)

        @pl.when(j == 0)
        def _():
            m_s[...] = jnp.full_like(m_s, -jnp.inf)
            l_s[...] = jnp.zeros_like(l_s)
            acc[...] = jnp.zeros_like(acc)

        sc = _dot(q_ref[...], k_ref[...], 1, 1)
        m_prev = m_s[...]
        m_new = jnp.maximum(m_prev, jnp.max(sc, axis=-1, keepdims=True))
        p = jnp.exp(sc - m_new[:, 0:1])
        alpha = jnp.exp(m_prev - m_new)
        l_s[...] = alpha * l_s[...] + jnp.sum(p, axis=-1, keepdims=True)
        acc[...] = alpha * acc[...] + _dot(p, v_ref[...])
        m_s[...] = m_new

        @pl.when(j == nk - 1)
        def _():
            o_ref[...] = acc[...] / l_s[...]
            lse_ref[...] = m_s[...] + jnp.log(l_s[...])

    (o, lse), rode = _ride_call(
        body, rider, name=name, grid=(MLA_HEADS, s // tq, nk),
        in_specs=[pl.BlockSpec((tq, MLA_QKP), lambda h, i, j: (i, h)),
                  pl.BlockSpec((tk, MLA_QKP), lambda h, i, j: (j, h)),
                  pl.BlockSpec((tk, MLA_V), lambda h, i, j: (j, h))],
        out_specs=[pl.BlockSpec((tq, MLA_V), lambda h, i, j: (i, h))] * 2,
        out_shape=[jax.ShapeDtypeStruct((s, GROUP_W), F32)] * 2,
        scratch_shapes=[pltpu.VMEM((tq, MLA_V), F32), pltpu.VMEM((tq, MLA_V), F32), pltpu.VMEM((tq, MLA_V), F32)],
        args=(q, k, v), sem=("parallel", "parallel", "arbitrary"))
    return (o, lse) if rider is None else (o, lse, rode)


def _flash_bwd(q, k, v, do, o, lse, *, name, tq=1024, tk=1024, rider=None):
    s = q.shape[0]
    tq, tk = min(tq, s), min(tk, s)
    nq, nk = s // tq, s // tk

    def body(q_ref, k_ref, v_ref, do_ref, o_ref, lse_ref, dq_ref, dk_ref, dv_ref, dk_acc, dv_acc):
        j, i = pl.program_id(1), pl.program_id(2)
        dov = do_ref[...]
        delta = jnp.sum(dov * o_ref[...], axis=-1, keepdims=True)
        p = jnp.exp(_dot(q_ref[...], k_ref[...], 1, 1) - lse_ref[:, 0:1])
        ds = p * (_dot(dov, v_ref[...], 1, 1) - delta)
        pv = _dot(p, dov, 0, 0)
        pk = _dot(ds, q_ref[...], 0, 0)
        pq = _dot(ds, k_ref[...])
        rows = pl.ds(pl.multiple_of(i * tq, tq), tq)

        @pl.when(j == 0)
        def _():
            dq_ref[rows, :] = pq

        @pl.when(j > 0)
        def _():
            dq_ref[rows, :] += pq

        @pl.when(i == 0)
        def _():
            dv_acc[...] = pv
            dk_acc[...] = pk

        @pl.when(i > 0)
        def _():
            dv_acc[...] += pv
            dk_acc[...] += pk

        @pl.when(i == nq - 1)
        def _():
            dk_ref[...] = dk_acc[...]
            dv_ref[...] = dv_acc[...]

    qb = pl.BlockSpec((tq, MLA_QKP), lambda h, j, i: (i, h))
    kb = pl.BlockSpec((tk, MLA_QKP), lambda h, j, i: (j, h))
    vb = pl.BlockSpec((tk, MLA_V), lambda h, j, i: (j, h))
    ob = pl.BlockSpec((tq, MLA_V), lambda h, j, i: (i, h))
    (dq, dk, dv), rode = _ride_call(
        body, rider, name=name, grid=(MLA_HEADS, nk, nq),
        in_specs=[qb, kb, vb, ob, ob, ob],
        out_specs=[pl.BlockSpec((s, MLA_QKP), lambda h, j, i: (0, h)), kb, vb],
        out_shape=[jax.ShapeDtypeStruct((s, MLA_HEADS * MLA_QKP), F32),
                   jax.ShapeDtypeStruct((s, MLA_HEADS * MLA_QKP), F32), jax.ShapeDtypeStruct((s, GROUP_W), F32)],
        scratch_shapes=[pltpu.VMEM((tk, MLA_QKP), F32), pltpu.VMEM((tk, MLA_V), F32)],
        args=(q, k, v, do, o, lse), sem=("arbitrary", "arbitrary", "arbitrary"))
    return (dq, dk, dv) if rider is None else (dq, dk, dv, rode)


def _rows_tile(r, c, itemsize=4, budget=2 * 1024 * 1024):
    if r * c * itemsize <= budget:
        return r
    best = None
    for t in range(8, r, 8):
        if r % t == 0 and t * c * itemsize <= budget:
            best = t
    return best if best is not None else r


def _add_n(arrs, *, out_dtype=F32, name):
    shape = arrs[0].shape
    c = shape[-1]
    flat = [a.reshape(-1, c) for a in arrs]
    r = flat[0].shape[0]
    t = _rows_tile(r, c)

    def body(*refs):
        acc = refs[0][...].astype(F32)
        for ref in refs[1:-1]:
            acc = acc + ref[...].astype(F32)
        refs[-1][...] = acc.astype(out_dtype)

    blk = pl.BlockSpec((t, c), lambda i: (i, 0))
    out = pl.pallas_call(
        body, name=name, grid=(r // t,), in_specs=[blk] * len(flat), out_specs=blk,
        out_shape=jax.ShapeDtypeStruct((r, c), out_dtype), compiler_params=_cparams("parallel"),
    )(*flat)
    return out.reshape(shape)


def _adamw(w, g, m, v, *, name):
    shape = w.shape
    c = shape[-1]
    flat = [a.reshape(-1, c) for a in (w, g, m, v)]
    r = flat[0].shape[0]
    t = _rows_tile(r, c, budget=1024 * 1024)

    def body(w_ref, g_ref, m_ref, v_ref, d_ref, mo_ref, vo_ref):
        gv = g_ref[...]
        m2 = ADAM_B1 * m_ref[...] + (1.0 - ADAM_B1) * gv
        v2 = ADAM_B2 * v_ref[...] + (1.0 - ADAM_B2) * (gv * gv)
        m_hat = m2 / (1.0 - ADAM_B1 ** ADAM_STEP)
        v_hat = v2 / (1.0 - ADAM_B2 ** ADAM_STEP)
        d_ref[...] = -ADAM_LR * (m_hat / (jnp.sqrt(v_hat) + ADAM_EPS) + ADAM_WD * w_ref[...])
        mo_ref[...] = m2
        vo_ref[...] = v2

    blk = pl.BlockSpec((t, c), lambda i: (i, 0))
    outs = pl.pallas_call(
        body, name=name, grid=(r // t,), in_specs=[blk] * 4, out_specs=[blk] * 3,
        out_shape=[jax.ShapeDtypeStruct((r, c), F32)] * 3, compiler_params=_cparams("parallel"),
    )(*flat)
    return tuple(o.reshape(shape) for o in outs)


def _place():
    x, y, c = lax.axis_index("x"), lax.axis_index("y"), lax.axis_index("c")
    chips = [(1 - x, y), (x, 1 - y), (1 - x, 1 - y)]
    return x, y, c, chips


ANY = pl.BlockSpec(memory_space=pl.ANY)


def _half(ref, axis, hc, lead=()):
    n = ref.shape[len(lead) + axis] // 2
    return ref.at[tuple(lead) + (slice(None),) * axis + (pl.ds(hc * n, n),)]


def _gather_shards(shards, axes, *, name):
    nt = len(shards)

    def body(*refs):
        src, dst = refs[:nt], refs[nt:2 * nt]
        send, recv, fsend, frecv, lsem = refs[2 * nt:]
        x, y, c, chips = _place()
        me = 2 * x + y
        local = [pltpu.make_async_copy(src[t], dst[t].at[me], lsem.at[t]) for t in range(nt)]
        for cp in local:
            cp.start()

        def half(t, slot, hc):
            return _half(dst[t], axes[t], hc, lead=(slot,))

        def first(t, k):
            return pltpu.make_async_remote_copy(
                src_ref=_half(src[t], axes[t], c), dst_ref=half(t, me, c),
                send_sem=send.at[t, k], recv_sem=recv.at[t, k],
                device_id=(chips[k][0], chips[k][1], c), device_id_type=MESH)

        def landed(t, k):
            slot = 2 * chips[k][0] + chips[k][1]
            return pltpu.make_async_remote_copy(
                src_ref=half(t, slot, c), dst_ref=half(t, slot, c),
                send_sem=send.at[t, k], recv_sem=recv.at[t, k],
                device_id=(chips[k][0], chips[k][1], c), device_id_type=MESH)

        def forward(t, k, hc):
            slot = 2 * chips[k][0] + chips[k][1]
            return pltpu.make_async_remote_copy(
                src_ref=half(t, slot, hc), dst_ref=half(t, slot, hc),
                send_sem=fsend.at[t, k], recv_sem=frecv.at[t, k],
                device_id=(x, y, 1 - c), device_id_type=MESH)

        for t in range(nt):
            for k in range(3):
                first(t, k).start()
        for t in range(nt):
            for k in range(3):
                landed(t, k).wait_recv()
                forward(t, k, c).start()
        for t in range(nt):
            for k in range(3):
                forward(t, k, 1 - c).wait_recv()
        for t in range(nt):
            for k in range(3):
                first(t, k).wait_send()
                forward(t, k, c).wait_send()
        for cp in local:
            cp.wait()

    return pl.pallas_call(
        body, name=name, in_specs=[ANY] * nt, out_specs=[ANY] * nt,
        out_shape=[jax.ShapeDtypeStruct((N_CHIP,) + a.shape, a.dtype) for a in shards],
        scratch_shapes=[pltpu.SemaphoreType.DMA((nt, 3)), pltpu.SemaphoreType.DMA((nt, 3)),
                        pltpu.SemaphoreType.DMA((nt, 3)), pltpu.SemaphoreType.DMA((nt, 3)),
                        pltpu.SemaphoreType.DMA((nt,))],
    )(*shards)


def _comm_rows(hr, c, budget=2 * 1024 * 1024):
    if hr * c * 4 <= budget:
        return hr
    best = None
    for t in range(16, hr, 16):
        if hr % t == 0 and t * c * 4 <= budget:
            best = t
    return best if best is not None else hr


def _comm_cols(r, hc, budget=2 * 1024 * 1024):
    best = 128
    for t in range(128, hc + 1, 128):
        if hc % t == 0 and r * t * 4 <= budget:
            best = t
    return best


def _comm_chunks(shape, axis):
    r, cdim = shape
    if axis == 0:
        rc = _comm_rows(r // 2, cdim)
        nt = (r // 2) // rc
        return (rc, cdim), nt, (lambda h, t: (h * nt + t, 0))
    cc = _comm_cols(r, cdim // 2)
    nt = (cdim // 2) // cc
    return (r, cc), nt, (lambda h, t: (0, h * nt + t))


def _pair_reduce(g, where, axis, *, out_dtype, name):
    n_slot, r, cdim = g.shape
    blk_shape, nr, at = _comm_chunks((r, cdim), axis)
    steps = n_slot * nr
    half_shape = (r // 2, cdim) if axis == 0 else (r, cdim // 2)

    def body(w_ref, a_ref, b_ref, o_ref, land, send, recv, credit):
        x, y, c, _ = _place()
        sib = (x, y, 1 - c)
        i = pl.program_id(0) * nr + pl.program_id(1)
        s = lax.rem(i, 2)

        @pl.when(i >= 2)
        def _():
            pl.semaphore_wait(credit.at[s], 1)

        cp = pltpu.make_async_remote_copy(src_ref=b_ref.at[0], dst_ref=land.at[s], send_sem=send.at[s],
                                          recv_sem=recv.at[s], device_id=sib, device_id_type=MESH)
        cp.start()
        cp.wait_recv()
        o_ref[0] = (a_ref[0] + land[s]).astype(out_dtype)
        cp.wait_send()

        @pl.when(i + 2 < steps)
        def _():
            pl.semaphore_signal(credit.at[s], inc=1, device_id=sib, device_id_type=MESH)

    blk = lambda half: pl.BlockSpec((1,) + blk_shape, lambda j, t, w: (j,) + at(half(w), t))
    grid_spec = pltpu.PrefetchScalarGridSpec(
        num_scalar_prefetch=1, grid=(n_slot, nr),
        in_specs=[blk(lambda w: w[0]), blk(lambda w: 1 - w[0])],
        out_specs=pl.BlockSpec((1,) + blk_shape, lambda j, t, w: (j,) + at(0, t)),
        scratch_shapes=[pltpu.VMEM((2,) + blk_shape, F32), pltpu.SemaphoreType.DMA((2,)),
                        pltpu.SemaphoreType.DMA((2,)), pltpu.SemaphoreType.REGULAR((2,))])
    return pl.pallas_call(
        body, name=name, grid_spec=grid_spec, out_shape=jax.ShapeDtypeStruct((n_slot,) + half_shape, out_dtype),
        compiler_params=_cparams("arbitrary", "arbitrary"),
    )(where, g, g)


def _chip_exchange(parts, *, name):
    nt = len(parts)

    def body(*refs):
        src, got = refs[:nt], refs[nt:2 * nt]
        send, recv = refs[2 * nt:]
        x, y, c, chips = _place()
        remote = []
        for t in range(nt):
            for k in range(3):
                remote.append(pltpu.make_async_remote_copy(
                    src_ref=src[t].at[2 * chips[k][0] + chips[k][1]], dst_ref=got[t].at[k],
                    send_sem=send.at[t, k], recv_sem=recv.at[t, k],
                    device_id=(chips[k][0], chips[k][1], c), device_id_type=MESH))
        for cp in remote:
            cp.start()
        for cp in remote:
            cp.wait_recv()
        for cp in remote:
            cp.wait_send()

    return pl.pallas_call(
        body, name=name, in_specs=[ANY] * nt, out_specs=[ANY] * nt,
        out_shape=[jax.ShapeDtypeStruct((3,) + a.shape[1:], a.dtype) for a in parts],
        scratch_shapes=[pltpu.SemaphoreType.DMA((nt, 3)), pltpu.SemaphoreType.DMA((nt, 3))],
    )(*parts)


def _sum_join(p, got, where, axis, *, name):
    _, hr, cdim = p.shape
    full = (2 * hr, cdim) if axis == 0 else (hr, 2 * cdim)
    blk_shape, n, at = _comm_chunks(full, axis)
    step_len = blk_shape[axis]
    half_len = full[axis] // 2

    def body(w_ref, p_ref, g_ref, out, buf, lsem, ssem, rsem):
        x, y, c, _ = _place()
        sib = (x, y, 1 - c)
        r = pl.program_id(0)

        def part(start, size):
            return out.at[(slice(None),) * axis + (pl.ds(start, size),)]

        def copies(step, slot):
            rows = part(pl.multiple_of(c * half_len + step * step_len, 8 if axis == 0 else 128), step_len)
            return (pltpu.make_async_copy(buf.at[slot], rows, lsem.at[slot]),
                    pltpu.make_async_remote_copy(src_ref=buf.at[slot], dst_ref=rows, send_sem=ssem.at[slot],
                                                 recv_sem=rsem, device_id=sib, device_id_type=MESH))

        s = lax.rem(r, 2)

        @pl.when(r >= 2)
        def _():
            lc, rm = copies(r - 2, s)
            lc.wait()
            rm.wait_send()

        buf[s] = p_ref[0].astype(F32) + g_ref[0].astype(F32) + g_ref[1].astype(F32) + g_ref[2].astype(F32)
        lc, rm = copies(r, s)
        lc.start()
        rm.start()

        @pl.when(r == n - 1)
        def _():
            for step in range(max(0, n - 2), n):
                lc, rm = copies(step, step % 2)
                lc.wait()
                rm.wait_send()
            whole = part(0, half_len)
            pltpu.make_async_remote_copy(src_ref=whole, dst_ref=whole, send_sem=ssem.at[0], recv_sem=rsem,
                                         device_id=sib, device_id_type=MESH).wait_recv()

    grid_spec = pltpu.PrefetchScalarGridSpec(
        num_scalar_prefetch=1, grid=(n,),
        in_specs=[pl.BlockSpec((1,) + blk_shape, lambda t, w: (w[1],) + at(0, t)),
                  pl.BlockSpec((3,) + blk_shape, lambda t, w: (0,) + at(0, t))],
        out_specs=ANY,
        scratch_shapes=[pltpu.VMEM((2,) + blk_shape, F32), pltpu.SemaphoreType.DMA((2,)),
                        pltpu.SemaphoreType.DMA((2,)), pltpu.SemaphoreType.DMA])
    return pl.pallas_call(
        body, name=name, grid_spec=grid_spec, out_shape=jax.ShapeDtypeStruct(full, F32),
        compiler_params=_cparams("arbitrary"),
    )(where, p, got)


def _rider_gather_send(shards, axes):
    nt = len(shards)

    def copies(src, dst, send, recv, lsem):
        x, y, c, chips = _place()
        me = 2 * x + y
        local = [pltpu.make_async_copy(src[t], dst[t].at[me], lsem.at[t]) for t in range(nt)]
        out, landed = [], []
        for t in range(nt):
            for k in range(3):
                peer = (chips[k][0], chips[k][1], c)
                out.append(pltpu.make_async_remote_copy(
                    src_ref=_half(src[t], axes[t], c), dst_ref=_half(dst[t], axes[t], c, lead=(me,)),
                    send_sem=send.at[t, k], recv_sem=recv.at[t, k], device_id=peer, device_id_type=MESH))
                theirs = _half(dst[t], axes[t], c, lead=(2 * chips[k][0] + chips[k][1],))
                landed.append(pltpu.make_async_remote_copy(
                    src_ref=theirs, dst_ref=theirs, send_sem=send.at[t, k], recv_sem=recv.at[t, k],
                    device_id=peer, device_id_type=MESH))
        return local, out, landed

    def start(src, dst, sems):
        local, out, _ = copies(src, dst, *sems)
        for cp in local + out:
            cp.start()

    def finish(src, dst, sems):
        local, out, landed = copies(src, dst, *sems)
        for cp in landed:
            cp.wait_recv()
        for cp in out:
            cp.wait_send()
        for cp in local:
            cp.wait()

    return _Rider(shards, [jax.ShapeDtypeStruct((N_CHIP,) + a.shape, a.dtype) for a in shards],
                  [pltpu.SemaphoreType.DMA((nt, 3)), pltpu.SemaphoreType.DMA((nt, 3)), pltpu.SemaphoreType.DMA((nt,))],
                  start, finish)


def _rider_gather_forward(bufs, axes):
    nt = len(bufs)

    def copies(src, dst, send, recv):
        x, y, c, chips = _place()
        mine, theirs = [], []
        for t in range(nt):
            for k in range(3):
                slot = 2 * chips[k][0] + chips[k][1]
                for hc, into in ((c, mine), (1 - c, theirs)):
                    into.append(pltpu.make_async_remote_copy(
                        src_ref=_half(src[t], axes[t], hc, lead=(slot,)),
                        dst_ref=_half(dst[t], axes[t], hc, lead=(slot,)),
                        send_sem=send.at[t, k], recv_sem=recv.at[t, k], device_id=(x, y, 1 - c), device_id_type=MESH))
        return mine, theirs

    def start(src, dst, sems):
        for cp in copies(src, dst, *sems)[0]:
            cp.start()

    def finish(src, dst, sems):
        mine, theirs = copies(src, dst, *sems)
        for cp in theirs:
            cp.wait_recv()
        for cp in mine:
            cp.wait_send()

    return _Rider(bufs, [jax.ShapeDtypeStruct(a.shape, a.dtype) for a in bufs],
                  [pltpu.SemaphoreType.DMA((nt, 3)), pltpu.SemaphoreType.DMA((nt, 3))], start, finish,
                  aliases={t: t for t in range(nt)})


def _rider_chip_exchange(parts):
    nt = len(parts)

    def copies(src, got, send, recv):
        x, y, c, chips = _place()
        return [pltpu.make_async_remote_copy(
            src_ref=src[t].at[2 * chips[k][0] + chips[k][1]], dst_ref=got[t].at[k], send_sem=send.at[t, k],
            recv_sem=recv.at[t, k], device_id=(chips[k][0], chips[k][1], c), device_id_type=MESH)
            for t in range(nt) for k in range(3)]

    def start(src, got, sems):
        for cp in copies(src, got, *sems):
            cp.start()

    def finish(src, got, sems):
        remote = copies(src, got, *sems)
        for cp in remote:
            cp.wait_recv()
        for cp in remote:
            cp.wait_send()

    return _Rider(parts, [jax.ShapeDtypeStruct((3,) + a.shape[1:], a.dtype) for a in parts],
                  [pltpu.SemaphoreType.DMA((nt, 3)), pltpu.SemaphoreType.DMA((nt, 3))], start, finish)


def _gather_all(block, *, name):
    m_per, n = block.shape

    def body(x_ref, out_ref, send_sems, recv_sems, local_sem):
        x, y, c, chips = _place()
        me, sibling = (x, y, c), (x, y, 1 - c)

        def rows(px, py, pc):
            return out_ref.at[4 * px + 2 * py + pc]

        def copy(k, blk, to, src=None):
            return pltpu.make_async_remote_copy(
                src_ref=rows(*blk) if src is None else src, dst_ref=rows(*blk),
                send_sem=send_sems.at[k], recv_sem=recv_sems.at[k], device_id=to, device_id_type=MESH)

        mine = pltpu.make_async_copy(x_ref, rows(*me), local_sem)
        mine.start()
        first = [copy(0, me, sibling, src=x_ref)]
        first += [copy(1 + j, me, (*chip, c), src=x_ref) for j, chip in enumerate(chips)]
        for cp in first:
            cp.start()
        passed = [copy(4 + j, (*chip, c), sibling) for j, chip in enumerate(chips)]
        for j, chip in enumerate(chips):
            copy(1 + j, (*chip, c), me).wait_recv()
            passed[j].start()
        copy(0, sibling, me).wait_recv()
        for j, chip in enumerate(chips):
            copy(4 + j, (*chip, 1 - c), me).wait_recv()
        for cp in first + passed:
            cp.wait_send()
        mine.wait()

    return pl.pallas_call(
        body, name=name,
        out_shape=jax.ShapeDtypeStruct((N_DEV, m_per, n), block.dtype),
        in_specs=[pl.BlockSpec(memory_space=pltpu.VMEM)], out_specs=pl.BlockSpec(memory_space=pltpu.VMEM),
        scratch_shapes=[pltpu.SemaphoreType.DMA((7,)), pltpu.SemaphoreType.DMA((7,)), pltpu.SemaphoreType.DMA],
        compiler_params=pltpu.CompilerParams(vmem_limit_bytes=VMEM_LIMIT),
    )(block)


def _sum_slots(slots, *, name):
    n, m, c = slots.shape
    t = _rows_tile(m, c * n)

    def body(s_ref, o_ref):
        acc = s_ref[0]
        for k in range(1, n):
            acc = acc + s_ref[k]
        o_ref[...] = acc

    return pl.pallas_call(
        body, name=name, grid=(m // t,), in_specs=[pl.BlockSpec((n, t, c), lambda i: (0, i, 0))],
        out_specs=pl.BlockSpec((t, c), lambda i: (i, 0)), out_shape=jax.ShapeDtypeStruct((m, c), F32),
        compiler_params=_cparams("parallel"),
    )(slots)


def _pad_rows(a, rows):
    return a if a.shape[0] == rows else jnp.pad(a, ((0, rows - a.shape[0]), (0, 0)))


def _w_in_padded(shards):
    full = shards.reshape(IN_COLS, shards.shape[2])
    return jnp.concatenate([_pad_rows(full[SEG[n][2]:SEG[n][2] + SEG[n][3]], SEG[n][1]) for n in SEG_ORDER], axis=0)


def _w_in_unpadded(gp):
    full = jnp.concatenate([gp[SEG[n][0]:SEG[n][0] + SEG[n][3]] for n in ORIG_ORDER], axis=0)
    return full.reshape(N_CHIP, IN_COLS // N_CHIP, gp.shape[1])


def _pad_heads(w, true_w, pad_w):
    r = w.shape[0]
    h = w.shape[1] // true_w
    return jnp.pad(w.reshape(r, h, true_w), ((0, 0), (0, 0), (0, pad_w - true_w))).reshape(r, h * pad_w)


def _unpad_heads(w, true_w, pad_w):
    r = w.shape[0]
    h = w.shape[1] // pad_w
    return w.reshape(r, h, pad_w)[:, :, :true_w].reshape(r, h * true_w)


def _cols_to_slots(a):
    w = a.shape[1] // N_CHIP
    return jnp.stack([a[:, j * w:(j + 1) * w] for j in range(N_CHIP)])


def _slots_to_cols(a):
    return jnp.concatenate([a[j] for j in range(N_CHIP)], axis=1)


def _to_heads(a, h, d):
    return a.reshape(a.shape[0], h, d).transpose(1, 0, 2)


def _from_heads(a):
    return a.transpose(1, 0, 2).reshape(a.shape[1], -1)


SMALL = [("norm_g", 2048), ("ret_norm_g", 512), ("gla_ba_f", 256), ("gla_ba_b", 256), ("gla_norm_g", 512),
         ("pool_w", 4 * 128 * 128), ("pool_scale", 512), ("mla_q_norm_g", 512), ("mla_kv_norm_g", 256),
         ("mla_qk_norm_q", 192), ("mla_qk_norm_k", 192)]


def _pack_small(vals):
    parts = []
    for name, n in SMALL:
        v = vals[name].reshape(-1)
        parts.append(jnp.pad(v, (0, (-v.shape[0]) % 1024)))
    parts.append(jnp.pad(vals["loss"].reshape(-1), (0, 1023)))
    return jnp.concatenate(parts).reshape(-1, 128)


def _unpack_small(block):
    flat = block.reshape(-1)
    out, off = {}, 0
    for name, n in SMALL:
        out[name] = flat[off:off + DEPTH * n]
        off += DEPTH * n + (-(DEPTH * n)) % 1024
    out["loss"] = flat[off]
    return out


def _layer_weights(l, p, g):
    wa = jnp.zeros((128, 512), F32)
    wa = wa.at[0:GLA_RANK, 0:256].set(_slots_to_cols(g["gla_wa2_f"]))
    wa = wa.at[GLA_RANK:2 * GLA_RANK, 256:512].set(_slots_to_cols(g["gla_wa2_b"]))
    return dict(
        norm_g=p["norm_g"][l][None, :],
        w_in=_w_in_padded(g["w_in"]),
        w_out=g["w_out"].reshape(4 * g["w_out"].shape[1], -1),
        ret_norm_g=p["ret_norm_g"][l][None, :],
        wa=_bf(wa),
        ba=jnp.concatenate([p["gla_ba_f"][l], p["gla_ba_b"][l]])[None, :],
        gla_norm_g=p["gla_norm_g"][l][None, :],
        pool_w=_bf(p["pool_w"][l]),
        pool_scale=p["pool_scale"][l][None, :],
        qg=p["mla_q_norm_g"][l][None, :],
        wq=_pad_heads(_slots_to_cols(g["mla_wq_b"]), MLA_QK, MLA_QKP),
        kvg=p["mla_kv_norm_g"][l][None, :],
        wkv=_slots_to_cols(g["mla_wkv_b"]),
        qng=jnp.pad(p["mla_qk_norm_q"][l], (0, MLA_QKP - MLA_QK))[None, :],
        kng=jnp.pad(p["mla_qk_norm_k"][l], (0, MLA_QKP - MLA_QK))[None, :],
    )


def _layer_fwd(l, x, w, tabs, next_shards=None):
    ret_cos, ret_sin, mla_cos, mla_sp, mla_sn = tabs
    nm = lambda s: f"l{l}_{s}"
    h = _rmsnorm_fwd(x, w["norm_g"], name=nm("norm"))
    if next_shards is None:
        z = _matmul(h, w["w_in"], tb=True, name=nm("in_proj"))
    else:
        z, landed = _matmul(h, w["w_in"], tb=True, rider=_rider_gather_send(next_shards, SHARD_AXES),
                            name=nm("in_proj"))
    qr, kr = _ret_pre(z, ret_cos, ret_sin, name=nm("ret_pre"))
    ret_o = _bla(qr, kr, z, _ret_log_gamma(False), (0, 0, SEG["rv"][0] // 512), name=nm("ret_scan"))
    y_a = _post(ret_o, z, SEG["rg"][0] // 512, w["ret_norm_g"], norm=True, name=nm("ret_post"))
    la = _gla_gate(z, w["wa"], w["ba"], name=nm("gla_gate"))
    la_h = jnp.stack([_to_heads(la[:, :256], GLA_HEADS, GLA_DK), _to_heads(la[:, 256:], GLA_HEADS, GLA_DK)])
    gq = _to_heads(z[:, SEG["gq"][0]:SEG["gq"][0] + 256], GLA_HEADS, GLA_DK)
    gk = _to_heads(z[:, SEG["gk"][0]:SEG["gk"][0] + 256], GLA_HEADS, GLA_DK)
    gla_o, gla_st = _gla_fwd(gq, gk, z, la_h, name=nm("gla_scan"))
    y_b = _post(gla_o, z, SEG["gg"][0] // 512, w["gla_norm_g"], norm=True, name=nm("gla_post"))
    y_c = _pool_fwd(z, w["pool_w"], w["pool_scale"], name=nm("pool"))
    q, k, v = _mla_pre(z, w["qg"], w["wq"], w["kvg"], w["wkv"], w["qng"], w["kng"], mla_cos, mla_sp, mla_sn,
                       name=nm("mla_pre"))
    if next_shards is None:
        (att_o, lse), gathered = _flash_fwd(q, k, v, name=nm("attn")), None
    else:
        att_o, lse, gathered = _flash_fwd(q, k, v, rider=_rider_gather_forward(landed, SHARD_AXES), name=nm("attn"))
    y_d = _post([att_o], z, SEG["mg"][0] // 512, w["qg"], norm=False, name=nm("mla_post"))
    y = jnp.concatenate([y_a, y_b, y_c, y_d], axis=1)
    x_next = _matmul(y, w["w_out"], add=x, name=nm("out_proj"))
    saved = dict(x=x, h=h, z=z, y=y, qr=qr, kr=kr, ret_o=ret_o, la_h=la_h, gq=gq, gk=gk, gla_o=gla_o, gla_st=gla_st,
                 q=q, k=k, v=v, att_o=att_o, lse=lse)
    return x_next, saved, gathered


def _layer_bwd(l, dx_next, w, sv, tabs, riding_parts=None):
    ret_cos, ret_sin, mla_cos, mla_sp, mla_sn = tabs
    nm = lambda s: f"l{l}_{s}"
    z = sv["z"]
    dy = _matmul(dx_next, w["w_out"], tb=True, name=nm("out_proj_dy"))
    d_w_out = _matmul(sv["y"].T, dx_next, tn=512, name=nm("out_proj_dw"))
    d_rg, d_ret_o, d_ret_g = _post_bwd(dy, 0, sv["ret_o"], z, SEG["rg"][0] // 512, w["ret_norm_g"], norm=True,
                                       name=nm("ret_post_bwd"))
    vcol = SEG["rv"][0] // 512
    dqr = _bla(d_ret_o, z, sv["kr"], _ret_log_gamma(False), (0, vcol, 0), name=nm("ret_scan_dq"))
    dkr = _bla(z, d_ret_o, sv["qr"], _ret_log_gamma(True), (vcol, 0, 0), name=nm("ret_scan_dk"))
    drv = _bla(sv["kr"], sv["qr"], d_ret_o, _ret_log_gamma(True), (0, 0, 0), name=nm("ret_scan_dv"))
    d_rq, d_rk = _ret_pre_bwd(dqr, dkr, ret_cos, ret_sin, name=nm("ret_pre_bwd"))
    d_rv = _add_n([drv[0], drv[1]], out_dtype=BF16, name=nm("ret_dv_sum"))
    d_gg, d_gla_o, d_gla_g = _post_bwd(dy, 1, sv["gla_o"], z, SEG["gg"][0] // 512, w["gla_norm_g"], norm=True,
                                       name=nm("gla_post_bwd"))
    dq2, dk2, dla2, dv2 = _gla_bwd(sv["gq"], sv["gk"], z, sv["la_h"], d_gla_o, sv["gla_st"], name=nm("gla_scan_bwd"))
    d_gq = _bf(_from_heads(dq2[0] + dq2[1]))
    d_gk = _bf(_from_heads(dk2[0] + dk2[1]))
    d_gv = _add_n([dv2[0], dv2[1]], out_dtype=BF16, name=nm("gla_dv_sum"))
    dla = jnp.concatenate([_from_heads(dla2[0]), _from_heads(dla2[1])], axis=1)
    d_ga, d_wa, d_ba = _gla_gate_bwd(dla, z, w["wa"], w["ba"], name=nm("gla_gate_bwd"))
    d_pv, d_pg, d_pool_w, d_pool_scale = _pool_bwd(dy, z, w["pool_w"], w["pool_scale"], name=nm("pool_bwd"))
    d_mg, d_att_o, _ = _post_bwd(dy, 3, [sv["att_o"]], z, SEG["mg"][0] // 512, w["qg"], norm=False,
                                 name=nm("mla_post_bwd"))
    if riding_parts is None:
        (dq, dk, dv), rode = _flash_bwd(sv["q"], sv["k"], sv["v"], d_att_o, sv["att_o"], sv["lse"],
                                        name=nm("attn_bwd")), None
    else:
        dq, dk, dv, rode = _flash_bwd(sv["q"], sv["k"], sv["v"], d_att_o, sv["att_o"], sv["lse"],
                                      rider=_rider_chip_exchange(riding_parts), name=nm("attn_bwd"))
    d_mq, d_mkv, d_mkr, d_wq, d_wkv, d_qg, d_kvg, d_qng, d_kng = _mla_pre_bwd(
        dq, dk, dv, z, w["qg"], w["wq"], w["kvg"], w["wkv"], w["qng"], w["kng"], mla_cos, mla_sp, mla_sn,
        name=nm("mla_pre_bwd"))
    segs = dict(rq=d_rq, rk=d_rk, rv=d_rv, rg=d_rg, gv=d_gv, gg=d_gg, pv=d_pv, pg=d_pg, mq=d_mq, mg=d_mg,
                gq=d_gq, gk=d_gk, mkv=d_mkv, ga=d_ga, mkr=d_mkr)
    dz = jnp.concatenate([segs[n] for n in SEG_ORDER], axis=1)
    dh = _matmul(dz, w["w_in"], tn=512, name=nm("in_proj_dh"))
    dz_t = jnp.concatenate([segs[n].T for n in SEG_ORDER], axis=0)
    d_w_in = _matmul(dz_t, sv["h"], name=nm("in_proj_dw"))
    dx, d_norm_g = _rmsnorm_bwd(sv["x"], dh, w["norm_g"], dx_next, name=nm("norm_bwd"))
    sharded = dict(
        w_in=_w_in_unpadded(d_w_in),
        w_out=d_w_out.reshape(N_CHIP, d_w_out.shape[0] // N_CHIP, d_w_out.shape[1]),
        mla_wq_b=_cols_to_slots(_unpad_heads(d_wq, MLA_QK, MLA_QKP)),
        mla_wkv_b=_cols_to_slots(d_wkv),
        gla_wa2_f=_cols_to_slots(d_wa[0:GLA_RANK, 0:256]),
        gla_wa2_b=_cols_to_slots(d_wa[GLA_RANK:2 * GLA_RANK, 256:512]),
    )
    small = dict(
        norm_g=d_norm_g[0], ret_norm_g=d_ret_g[0], gla_ba_f=d_ba[0, :256], gla_ba_b=d_ba[0, 256:],
        gla_norm_g=d_gla_g[0], pool_w=d_pool_w.reshape(-1), pool_scale=d_pool_scale[0], mla_q_norm_g=d_qg[0],
        mla_kv_norm_g=d_kvg[0], mla_qk_norm_q=d_qng[0, :MLA_QK], mla_qk_norm_k=d_kng[0, :MLA_QK],
    )
    return dx, sharded, small, rode


SHARDED = ["w_in", "w_out", "mla_wq_b", "mla_wkv_b", "gla_wa2_f", "gla_wa2_b"]
WEIGHTS = ["norm_g", "w_in", "ret_norm_g", "gla_wa2_f", "gla_ba_f", "gla_wa2_b", "gla_ba_b", "gla_norm_g", "pool_w",
           "pool_scale", "mla_q_norm_g", "mla_wq_b", "mla_kv_norm_g", "mla_wkv_b", "mla_qk_norm_q", "mla_qk_norm_k",
           "w_out"]


SHARD_AXES = [1, 0, 0, 0, 0, 0]


def _layer_shards(p, l):
    return [jnp.swapaxes(p["w_in"], 1, 2)[l].astype(BF16), p["w_out"][l].astype(BF16), p["mla_wq_b"][l].astype(BF16),
            p["mla_wkv_b"][l].astype(BF16), p["gla_wa2_f"][l], p["gla_wa2_b"][l]]


def _step(p, where):
    x = p["x"][0]
    tabs = _rope_tables(x.shape[0])
    got0 = _gather_shards(_layer_shards(p, 0), SHARD_AXES, name="l0_gather_weights")
    w0 = _layer_weights(0, p, dict(zip(SHARDED, got0)))
    x1, sv0, got1 = _layer_fwd(0, x, w0, tabs, next_shards=_layer_shards(p, 1))
    w1 = _layer_weights(1, p, dict(zip(SHARDED, got1)))
    x2, sv1, _ = _layer_fwd(1, x1, w1, tabs)
    dx, loss = _loss_head(x2, p["loss_target"][0], name="loss_head")

    def pair_sums(l, sharded):
        return [_pair_reduce(sharded[n], where, ax, out_dtype=BF16, name=f"l{l}_pair_reduce_{n}")
                for n, ax in zip(SHARDED, SHARD_AXES)]

    def joined(l, pair, others):
        return [_sum_join(a, b, where, ax, name=f"l{l}_sum_join_{n}")
                for n, a, b, ax in zip(SHARDED, pair, others, SHARD_AXES)]

    dx, sharded1, small1, _ = _layer_bwd(1, dx, w1, sv1, tabs)
    pair1 = pair_sums(1, sharded1)
    dx, sharded0, small0, others1 = _layer_bwd(0, dx, w0, sv0, tabs, riding_parts=pair1)
    grads1 = joined(1, pair1, others1)
    pair0 = pair_sums(0, sharded0)
    grads0 = joined(0, pair0, _chip_exchange(pair0, name="l0_chip_exchange"))
    grads = {n: jnp.stack([g0, g1]) for n, g0, g1 in zip(SHARDED, grads0, grads1)}
    small = {n: jnp.stack([small0[n], small1[n]]) for n, _ in SMALL}
    small["loss"] = loss
    return dx[None], grads, small


def kernel(x, norm_g, w_in, ret_norm_g, gla_wa2_f, gla_ba_f, gla_wa2_b, gla_ba_b, gla_norm_g, pool_w, pool_scale, mla_q_norm_g, mla_wq_b, mla_kv_norm_g, mla_wkv_b, mla_qk_norm_q, mla_qk_norm_k, w_out, loss_target, m_norm_g, m_w_in, m_ret_norm_g, m_gla_wa2_f, m_gla_ba_f, m_gla_wa2_b, m_gla_ba_b, m_gla_norm_g, m_pool_w, m_pool_scale, m_mla_q_norm_g, m_mla_wq_b, m_mla_kv_norm_g, m_mla_wkv_b, m_mla_qk_norm_q, m_mla_qk_norm_k, m_w_out, v_norm_g, v_w_in, v_ret_norm_g, v_gla_wa2_f, v_gla_ba_f, v_gla_wa2_b, v_gla_ba_b, v_gla_norm_g, v_pool_w, v_pool_scale, v_mla_q_norm_g, v_mla_wq_b, v_mla_kv_norm_g, v_mla_wkv_b, v_mla_qk_norm_q, v_mla_qk_norm_k, v_w_out):
    p = dict(x=x, norm_g=norm_g, w_in=w_in, ret_norm_g=ret_norm_g, gla_wa2_f=gla_wa2_f, gla_ba_f=gla_ba_f,
             gla_wa2_b=gla_wa2_b, gla_ba_b=gla_ba_b, gla_norm_g=gla_norm_g, pool_w=pool_w, pool_scale=pool_scale,
             mla_q_norm_g=mla_q_norm_g, mla_wq_b=mla_wq_b, mla_kv_norm_g=mla_kv_norm_g, mla_wkv_b=mla_wkv_b,
             mla_qk_norm_q=mla_qk_norm_q, mla_qk_norm_k=mla_qk_norm_k, w_out=w_out, loss_target=loss_target)
    moments = dict(
        m=dict(norm_g=m_norm_g, w_in=m_w_in, ret_norm_g=m_ret_norm_g, gla_wa2_f=m_gla_wa2_f, gla_ba_f=m_gla_ba_f,
               gla_wa2_b=m_gla_wa2_b, gla_ba_b=m_gla_ba_b, gla_norm_g=m_gla_norm_g, pool_w=m_pool_w,
               pool_scale=m_pool_scale, mla_q_norm_g=m_mla_q_norm_g, mla_wq_b=m_mla_wq_b,
               mla_kv_norm_g=m_mla_kv_norm_g, mla_wkv_b=m_mla_wkv_b, mla_qk_norm_q=m_mla_qk_norm_q,
               mla_qk_norm_k=m_mla_qk_norm_k, w_out=m_w_out),
        v=dict(norm_g=v_norm_g, w_in=v_w_in, ret_norm_g=v_ret_norm_g, gla_wa2_f=v_gla_wa2_f, gla_ba_f=v_gla_ba_f,
               gla_wa2_b=v_gla_wa2_b, gla_ba_b=v_gla_ba_b, gla_norm_g=v_gla_norm_g, pool_w=v_pool_w,
               pool_scale=v_pool_scale, mla_q_norm_g=v_mla_q_norm_g, mla_wq_b=v_mla_wq_b,
               mla_kv_norm_g=v_mla_kv_norm_g, mla_wkv_b=v_mla_wkv_b, mla_qk_norm_q=v_mla_qk_norm_q,
               mla_qk_norm_k=v_mla_qk_norm_k, w_out=v_w_out))

    where = jnp.stack([lax.axis_index("c"), 2 * lax.axis_index("x") + lax.axis_index("y")]).astype(jnp.int32)
    grad_x, grads, small = _step(p, where)

    slots = _gather_all(_pack_small(small), name="gather_small")
    total = _unpack_small(_sum_slots(slots, name="sum_small"))
    for n, _ in SMALL:
        grads[n] = total[n].reshape(p[n].shape)
    loss = total["loss"]

    delta, new_m, new_v = {}, {}, {}
    for n in WEIGHTS:
        turn = (lambda a: jnp.swapaxes(a, 1, 2)) if n == "w_in" else (lambda a: a)
        outs = _adamw(turn(p[n]), grads[n], turn(moments["m"][n]), turn(moments["v"][n]), name=f"adamw_{n}")
        grads[n] = turn(grads[n])
        delta[n], new_m[n], new_v[n] = (turn(o) for o in outs)
    return (loss, grad_x, *[grads[n] for n in WEIGHTS], *[delta[n] for n in WEIGHTS],
            *[new_m[n] for n in WEIGHTS], *[new_v[n] for n in WEIGHTS])
```

```python
import functools
import math

import jax
import jax.numpy as jnp
from jax import lax
from jax.experimental import pallas as pl
from jax.experimental.pallas import tpu as pltpu

F32 = jnp.float32
BF16 = jnp.bfloat16
MESH = pl.DeviceIdType.MESH

EPS = 1e-6
ROPE_THETA = 10000.0
DEPTH = 2
N_DEV = 8
N_CHIP = 4

GROUP_W = 512
RET_HEADS = 4
RET_HD = 128
RET_CHUNK = 128
GLA_HEADS = 4
GLA_DK = 64
GLA_DV = 128
GLA_RANK = 16
GLA_TAU = 16.0
GLA_CHUNK = 64
POOL_GROUPS = 4
POOL_GW = 128
POOL_HALO = 8
POOL_TILE = 256
MLA_HEADS = 4
MLA_NOPE = 128
MLA_ROPE = 64
MLA_QK = MLA_NOPE + MLA_ROPE
MLA_QKP = 256
MLA_V = 128
MLA_Q_RANK = 512
MLA_KV_RANK = 256
MLA_SCALE = MLA_QK ** -0.5
FLASH_STRIP = 1024

ADAM_LR = 0.001
ADAM_B1 = 0.9
ADAM_B2 = 0.999
ADAM_EPS = 1e-08
ADAM_WD = 0.01
ADAM_STEP = 10

VMEM_LIMIT = 56 * 1024 * 1024

SEG = {
    "rq": (0, 512, 0, 512), "rk": (512, 512, 512, 512), "rv": (1024, 512, 1024, 512), "rg": (1536, 512, 1536, 512),
    "gv": (2048, 512, 2560, 512), "gg": (2560, 512, 3072, 512),
    "pv": (3072, 512, 3616, 512), "pg": (3584, 512, 4128, 512),
    "mq": (4096, 512, 4640, 512), "mg": (4608, 512, 5472, 512),
    "gq": (5120, 256, 2048, 256), "gk": (5376, 256, 2304, 256), "mkv": (5632, 256, 5152, 256),
    "ga": (5888, 128, 3584, 32), "mkr": (6016, 128, 5408, 64),
}
SEG_ORDER = ["rq", "rk", "rv", "rg", "gv", "gg", "pv", "pg", "mq", "mg", "gq", "gk", "mkv", "ga", "mkr"]
IN_COLS = 5984
IN_PAD = 6144
ORIG_ORDER = ["rq", "rk", "rv", "rg", "gq", "gk", "gv", "gg", "ga", "pv", "pg", "mq", "mkv", "mkr", "mg"]


def _cparams(*sem):
    return pltpu.CompilerParams(dimension_semantics=tuple(sem), vmem_limit_bytes=VMEM_LIMIT)


def _bf(v):
    return v.astype(BF16)


def _dot(a, b, ca=1, cb=0):
    return lax.dot_general(_bf(a), _bf(b), (((ca,), (cb,)), ((), ())), preferred_element_type=F32)


def _split_dot(a01, x, ca=1, cb=0):
    hi = _bf(x)
    r1 = x - hi.astype(F32)
    mid = _bf(r1)
    lo = _bf(r1 - mid.astype(F32))
    dn = (((ca,), (cb,)), ((), ()))
    a = _bf(a01)
    return (lax.dot_general(a, hi, dn, preferred_element_type=F32)
            + lax.dot_general(a, mid, dn, preferred_element_type=F32)
            + lax.dot_general(a, lo, dn, preferred_element_type=F32))


def _sigmoid(x):
    return 1.0 / (1.0 + jnp.exp(-x))


def _silu_parts(g):
    sg = _sigmoid(g)
    return g * sg, sg * (1.0 + g * (1.0 - sg))


class _Rider:
    def __init__(self, ins, outs, sems, start, finish, aliases=None):
        self.ins, self.outs, self.sems, self.start, self.finish = list(ins), list(outs), list(sems), start, finish
        self.aliases = dict(aliases or {})


def _ride(body, rider, n_in, n_out, grid):
    if rider is None:
        return body
    ri, ro, rs = len(rider.ins), len(rider.outs), len(rider.sems)

    def wrapped(*refs):
        ins, refs = refs[:n_in], refs[n_in:]
        rin, refs = refs[:ri], refs[ri:]
        outs, refs = refs[:n_out], refs[n_out:]
        rout, refs = refs[:ro], refs[ro:]
        scratch, sems = refs[:len(refs) - rs], refs[len(refs) - rs:]
        first = pl.program_id(0) == 0
        last = pl.program_id(0) == grid[0] - 1
        for ax in range(1, len(grid)):
            first = jnp.logical_and(first, pl.program_id(ax) == 0)
            last = jnp.logical_and(last, pl.program_id(ax) == grid[ax] - 1)

        @pl.when(first)
        def _():
            rider.start(rin, rout, sems)

        body(*ins, *outs, *scratch)

        @pl.when(last)
        def _():
            rider.finish(rin, rout, sems)

    return wrapped


def _ride_call(body, rider, *, name, grid, in_specs, out_specs, out_shape, scratch_shapes, args, sem):
    n_in, n_out = len(in_specs), len(out_specs)
    if rider is None:
        return pl.pallas_call(body, name=name, grid=grid, in_specs=in_specs, out_specs=out_specs, out_shape=out_shape,
                              scratch_shapes=scratch_shapes, compiler_params=_cparams(*sem))(*args), []
    outs = pl.pallas_call(
        _ride(body, rider, n_in, n_out, grid), name=name, grid=grid,
        in_specs=list(in_specs) + [ANY] * len(rider.ins), out_specs=list(out_specs) + [ANY] * len(rider.outs),
        out_shape=list(out_shape) + rider.outs, scratch_shapes=list(scratch_shapes) + rider.sems,
        input_output_aliases={n_in + i: n_out + o for i, o in rider.aliases.items()},
        compiler_params=_cparams(*(["arbitrary"] * len(grid))),
    )(*args, *rider.ins)
    return outs[:n_out], outs[n_out:]


def _matmul(a, b, *, ta=False, tb=False, out_dtype=F32, tm=512, tn=1024, tk=None, add=None, n_outer=True, rider=None,
            name):
    m, kdim = (a.shape[1], a.shape[0]) if ta else a.shape
    n = b.shape[0] if tb else b.shape[1]
    tm, tn = min(tm, m), min(tn, n)
    tk = kdim if tk is None else min(tk, kdim)
    assert m % tm == 0 and n % tn == 0 and kdim % tk == 0
    nk = kdim // tk
    ca, cb = (0 if ta else 1), (1 if tb else 0)

    def body(*refs):
        if add is None:
            a_ref, b_ref, o_ref = refs[:3]
            add_ref = None
        else:
            a_ref, b_ref, add_ref, o_ref = refs[:4]
        p = _dot(a_ref[...], b_ref[...], ca, cb)

        def finish(r):
            if add_ref is not None:
                r = r + add_ref[...]
            o_ref[...] = r.astype(out_dtype)

        if nk == 1:
            finish(p)
        else:
            acc = refs[-1]
            k = pl.program_id(2)

            @pl.when(k == 0)
            def _():
                acc[...] = p

            @pl.when(k > 0)
            def _():
                acc[...] += p

            @pl.when(k == nk - 1)
            def _():
                finish(acc[...])

    def ij(g0, g1):
        return (g1, g0) if n_outer else (g0, g1)

    a_spec = (pl.BlockSpec((tk, tm), lambda g0, g1, k: (k, ij(g0, g1)[0])) if ta
              else pl.BlockSpec((tm, tk), lambda g0, g1, k: (ij(g0, g1)[0], k)))
    b_spec = (pl.BlockSpec((tn, tk), lambda g0, g1, k: (ij(g0, g1)[1], k)) if tb
              else pl.BlockSpec((tk, tn), lambda g0, g1, k: (k, ij(g0, g1)[1])))
    o_spec = pl.BlockSpec((tm, tn), lambda g0, g1, k: ij(g0, g1))
    in_specs = [a_spec, b_spec] + ([o_spec] if add is not None else [])
    args = (a, b) + ((add,) if add is not None else ())
    grid = (n // tn, m // tm, nk) if n_outer else (m // tm, n // tn, nk)
    (out,), rode = _ride_call(
        body, rider, name=name, grid=grid, in_specs=in_specs, out_specs=[o_spec],
        out_shape=[jax.ShapeDtypeStruct((m, n), out_dtype)],
        scratch_shapes=[] if nk == 1 else [pltpu.VMEM((tm, tn), F32)], args=args,
        sem=("parallel", "parallel", "arbitrary"))
    return out if rider is None else (out, rode)


def _rmsnorm_fwd(x, g, *, name, tm=256):
    s, d = x.shape
    tm = min(tm, s)

    def body(x_ref, g_ref, h_ref):
        xv = x_ref[...]
        r = lax.rsqrt(jnp.mean(xv * xv, axis=-1, keepdims=True) + EPS)
        h_ref[...] = _bf(xv * r * g_ref[...])

    return pl.pallas_call(
        body, name=name, grid=(s // tm,),
        in_specs=[pl.BlockSpec((tm, d), lambda i: (i, 0)), pl.BlockSpec((1, d), lambda i: (0, 0))],
        out_specs=pl.BlockSpec((tm, d), lambda i: (i, 0)),
        out_shape=jax.ShapeDtypeStruct((s, d), BF16),
        compiler_params=_cparams("parallel"),
    )(x, g)


def _rmsnorm_bwd(x, dh, g, dres, *, name, tm=256):
    s, d = x.shape
    tm = min(tm, s)

    def body(x_ref, dh_ref, g_ref, dres_ref, dx_ref, dg_ref):
        i = pl.program_id(0)
        xv = x_ref[...]
        r = lax.rsqrt(jnp.mean(xv * xv, axis=-1, keepdims=True) + EPS)
        xn = xv * r
        dv = dh_ref[...]
        part = jnp.sum(dv * xn, axis=0, keepdims=True)

        @pl.when(i == 0)
        def _():
            dg_ref[...] = part

        @pl.when(i > 0)
        def _():
            dg_ref[...] += part

        dxn = dv * g_ref[...]
        dx_ref[...] = dres_ref[...] + r * (dxn - xn * jnp.mean(dxn * xn, axis=-1, keepdims=True))

    row = pl.BlockSpec((tm, d), lambda i: (i, 0))
    vec = pl.BlockSpec((1, d), lambda i: (0, 0))
    return pl.pallas_call(
        body, name=name, grid=(s // tm,), in_specs=[row, row, vec, row], out_specs=[row, vec],
        out_shape=[jax.ShapeDtypeStruct((s, d), F32), jax.ShapeDtypeStruct((1, d), F32)],
        compiler_params=_cparams("arbitrary"),
    )(x, dh, g, dres)


def _loss_head(xf, target, *, name, tm=256):
    s, d = xf.shape
    tm = min(tm, s)

    def body(x_ref, t_ref, dx_ref, l_ref):
        i = pl.program_id(0)
        e = x_ref[...] - t_ref[...]
        dx_ref[...] = e * (1.0 / d)
        rows = jnp.mean(e * e, axis=-1, keepdims=True)
        part = 0.5 * jnp.sum(rows, axis=0, keepdims=True)

        @pl.when(i == 0)
        def _():
            l_ref[...] = part

        @pl.when(i > 0)
        def _():
            l_ref[...] += part

    row = pl.BlockSpec((tm, d), lambda i: (i, 0))
    return pl.pallas_call(
        body, name=name, grid=(s // tm,), in_specs=[row, row],
        out_specs=[row, pl.BlockSpec((1, 1), lambda i: (0, 0))],
        out_shape=[jax.ShapeDtypeStruct((s, d), F32), jax.ShapeDtypeStruct((1, 1), F32)],
        compiler_params=_cparams("arbitrary"),
    )(xf, target)


def _rope_tables(s):
    pos = jnp.arange(s, dtype=F32)[:, None]
    inv_r = 1.0 / (ROPE_THETA ** (jnp.arange(0, RET_HD, 2, dtype=F32) / RET_HD))
    ang = pos * inv_r[None, :]
    ret_cos = jnp.concatenate([jnp.cos(ang), jnp.cos(ang)], axis=1)
    ret_sin = jnp.concatenate([-jnp.sin(ang), jnp.sin(ang)], axis=1)
    inv_m = 1.0 / (ROPE_THETA ** (jnp.arange(0, MLA_ROPE, 2, dtype=F32) / MLA_ROPE))
    am = pos * inv_m[None, :]
    z32, z64 = jnp.zeros((s, 32), F32), jnp.zeros((s, 64), F32)
    mla_cos = jnp.concatenate([jnp.cos(am), jnp.cos(am), z64], axis=1)
    mla_sp = jnp.concatenate([z32, jnp.sin(am), z64], axis=1)
    mla_sn = jnp.concatenate([-jnp.sin(am), z32, z64], axis=1)
    return ret_cos, ret_sin, mla_cos, mla_sp, mla_sn


def _rope128(x, c, sg):
    return x * c + pltpu.roll(x, 64, 1) * sg


def _unrope128(d, c, sg):
    return d * c + pltpu.roll(d * sg, 64, 1)


def _rope64(t, c, sp, sn):
    return t * c + pltpu.roll(t, 96, 1) * sn + pltpu.roll(t, 32, 1) * sp


def _unrope64(d, c, sp, sn):
    return d * c + pltpu.roll(d * sn, 32, 1) + pltpu.roll(d * sp, 96, 1)


def _ret_pre(z, cos, sin, *, name, tm=256):
    s = z.shape[0]
    tm = min(tm, s)
    scale = RET_HD ** -0.5

    def body(q_ref, k_ref, c_ref, s_ref, qo_ref, ko_ref):
        c, sg = c_ref[...], s_ref[...]
        for h in range(RET_HEADS):
            sl = slice(h * RET_HD, (h + 1) * RET_HD)
            qo_ref[:, sl] = _rope128(q_ref[:, sl], c, sg)
            ko_ref[:, sl] = _rope128(k_ref[:, sl], c, sg) * scale

    seg = lambda j: pl.BlockSpec((tm, GROUP_W), lambda i: (i, j))
    tab = pl.BlockSpec((tm, RET_HD), lambda i: (i, 0))
    return pl.pallas_call(
        body, name=name, grid=(s // tm,), in_specs=[seg(0), seg(1), tab, tab],
        out_specs=[seg(0), seg(0)],
        out_shape=[jax.ShapeDtypeStruct((s, GROUP_W), F32)] * 2,
        compiler_params=_cparams("parallel"),
    )(z, z, cos, sin)


def _ret_pre_bwd(dqr, dkr, cos, sin, *, name, tm=256):
    s = dqr[0].shape[0]
    tm = min(tm, s)
    scale = RET_HD ** -0.5

    def body(dq0_ref, dq1_ref, dk0_ref, dk1_ref, c_ref, s_ref, qo_ref, ko_ref):
        c, sg = c_ref[...], s_ref[...]
        for h in range(RET_HEADS):
            sl = slice(h * RET_HD, (h + 1) * RET_HD)
            qo_ref[:, sl] = _bf(_unrope128(dq0_ref[:, sl] + dq1_ref[:, sl], c, sg))
            ko_ref[:, sl] = _bf(_unrope128(dk0_ref[:, sl] + dk1_ref[:, sl], c, sg) * scale)

    row = pl.BlockSpec((tm, GROUP_W), lambda i: (i, 0))
    tab = pl.BlockSpec((tm, RET_HD), lambda i: (i, 0))
    return pl.pallas_call(
        body, name=name, grid=(s // tm,), in_specs=[row, row, row, row, tab, tab], out_specs=[row, row],
        out_shape=[jax.ShapeDtypeStruct((s, GROUP_W), BF16)] * 2,
        compiler_params=_cparams("parallel"),
    )(dqr[0], dqr[1], dkr[0], dkr[1], cos, sin)


def _bla(a, b, c, lg, cols, *, name):
    s = a.shape[0]
    ch = min(RET_CHUNK, s)
    n = s // ch
    hd = RET_HD

    def body(lg_ref, a0, b0, c0, a1, b1, c1, o0, o1, st):
        t = pl.program_id(0)

        @pl.when(t == 0)
        def _():
            st[...] = jnp.zeros_like(st)

        ii = lax.broadcasted_iota(jnp.int32, (ch, ch), 0)
        jj = lax.broadcasted_iota(jnp.int32, (ch, ch), 1)
        idx = lax.broadcasted_iota(jnp.int32, (ch, 1), 0).astype(F32)
        for d, (a_ref, b_ref, c_ref, o_ref) in enumerate(((a0, b0, c0, o0), (a1, b1, c1, o1))):
            diff = ((ii - jj) if d == 0 else (jj - ii)).astype(F32)
            keep = diff >= 0
            dpos = jnp.maximum(diff, 0.0)
            pq = (idx + 1.0) if d == 0 else (ch - idx)
            pk = (ch - 1.0 - idx) if d == 0 else idx
            for h in range(RET_HEADS):
                g = lg_ref[d, h]
                sl = slice(h * hd, (h + 1) * hd)
                av, bv, cv = a_ref[:, sl], b_ref[:, sl], c_ref[:, sl]
                sc = _dot(av, bv, 1, 1) * jnp.where(keep, jnp.exp(dpos * g), 0.0)
                stv = st[d, h]
                o_ref[:, sl] = _dot(sc, cv) + _dot(av * jnp.exp(pq * g), stv)
                st[d, h] = jnp.exp(ch * g) * stv + _dot(bv * jnp.exp(pk * g), cv, 0, 0)

    fwd = lambda j: pl.BlockSpec((ch, GROUP_W), lambda t: (t, j))
    bwd = lambda j: pl.BlockSpec((ch, GROUP_W), lambda t: (n - 1 - t, j))
    return pl.pallas_call(
        body, name=name, grid=(n,),
        in_specs=[pl.BlockSpec(memory_space=pltpu.SMEM), fwd(cols[0]), fwd(cols[1]), fwd(cols[2]),
                  bwd(cols[0]), bwd(cols[1]), bwd(cols[2])],
        out_specs=[fwd(0), bwd(0)],
        out_shape=[jax.ShapeDtypeStruct((s, GROUP_W), F32)] * 2,
        scratch_shapes=[pltpu.VMEM((2, RET_HEADS, hd, hd), F32)],
        compiler_params=_cparams("arbitrary"),
    )(lg, a, b, c, a, b, c)


def _post(os_, zg, gcol, g, *, norm, name, tm=256):
    s = zg.shape[0]
    tm = min(tm, s)
    nd = len(os_)

    def body(*refs):
        o_refs, (gt_ref, g_ref, y_ref) = refs[:nd], refs[nd:]
        silu, _ = _silu_parts(gt_ref[...])
        for h in range(4):
            sl = slice(h * 128, (h + 1) * 128)
            o = o_refs[0][:, sl]
            for k in range(1, nd):
                o = o + o_refs[k][:, sl]
            if norm:
                r = lax.rsqrt(jnp.mean(o * o, axis=-1, keepdims=True) + EPS)
                o = o * r * g_ref[:, sl]
            y_ref[:, sl] = _bf(silu[:, sl] * o)

    row = pl.BlockSpec((tm, GROUP_W), lambda i: (i, 0))
    return pl.pallas_call(
        body, name=name, grid=(s // tm,),
        in_specs=[row] * nd + [pl.BlockSpec((tm, GROUP_W), lambda i: (i, gcol)),
                               pl.BlockSpec((1, GROUP_W), lambda i: (0, 0))],
        out_specs=row,
        out_shape=jax.ShapeDtypeStruct((s, GROUP_W), BF16),
        compiler_params=_cparams("parallel"),
    )(*os_, zg, g)


def _post_bwd(dy, ycol, os_, zg, gcol, g, *, norm, name, tm=256):
    s = zg.shape[0]
    tm = min(tm, s)
    nd = len(os_)

    def body(*refs):
        dy_ref, o_refs = refs[0], refs[1:1 + nd]
        gt_ref, g_ref, dgt_ref, do_ref, dg_ref = refs[1 + nd:]
        i = pl.program_id(0)
        silu, dsilu = _silu_parts(gt_ref[...])
        dyv = dy_ref[...]
        parts = []
        for h in range(4):
            sl = slice(h * 128, (h + 1) * 128)
            o = o_refs[0][:, sl]
            for k in range(1, nd):
                o = o + o_refs[k][:, sl]
            dn = dyv[:, sl] * silu[:, sl]
            if norm:
                r = lax.rsqrt(jnp.mean(o * o, axis=-1, keepdims=True) + EPS)
                xn = o * r
                gh = g_ref[:, sl]
                dgt_ref[:, sl] = _bf(dyv[:, sl] * (xn * gh) * dsilu[:, sl])
                parts.append(jnp.sum(dn * xn, axis=0, keepdims=True))
                dxn = dn * gh
                do_ref[:, sl] = r * (dxn - xn * jnp.mean(dxn * xn, axis=-1, keepdims=True))
            else:
                dgt_ref[:, sl] = _bf(dyv[:, sl] * o * dsilu[:, sl])
                parts.append(jnp.zeros((1, 128), F32))
                do_ref[:, sl] = dn
        part = jnp.concatenate(parts, axis=1)

        @pl.when(i == 0)
        def _():
            dg_ref[...] = part

        @pl.when(i > 0)
        def _():
            dg_ref[...] += part

    row = pl.BlockSpec((tm, GROUP_W), lambda i: (i, 0))
    vec = pl.BlockSpec((1, GROUP_W), lambda i: (0, 0))
    return pl.pallas_call(
        body, name=name, grid=(s // tm,),
        in_specs=[pl.BlockSpec((tm, GROUP_W), lambda i: (i, ycol))] + [row] * nd
        + [pl.BlockSpec((tm, GROUP_W), lambda i: (i, gcol)), vec],
        out_specs=[row, row, vec],
        out_shape=[jax.ShapeDtypeStruct((s, GROUP_W), BF16), jax.ShapeDtypeStruct((s, GROUP_W), F32),
                   jax.ShapeDtypeStruct((1, GROUP_W), F32)],
        compiler_params=_cparams("arbitrary"),
    )(dy, *os_, zg, g)


def _ret_log_gamma(swap):
    gf = 1.0 - 2.0 ** (-5.0 - jnp.arange(RET_HEADS, dtype=F32))
    lf, lb = jnp.log(gf), jnp.log(gf[::-1])
    return jnp.stack([lb, lf] if swap else [lf, lb])


def _log_sigmoid(x):
    return jnp.minimum(x, 0.0) - jnp.log(1.0 + jnp.exp(-jnp.abs(x)))


def _gla_gate(z, wa, ba, *, name, tm=256):
    s = z.shape[0]
    tm = min(tm, s)
    col = SEG["ga"][0] // 128

    def body(ga_ref, wa_ref, ba_ref, la_ref):
        pre = _dot(ga_ref[...], wa_ref[...]) + ba_ref[...]
        la_ref[...] = _log_sigmoid(pre) / GLA_TAU

    return pl.pallas_call(
        body, name=name, grid=(s // tm,),
        in_specs=[pl.BlockSpec((tm, 128), lambda i: (i, col)), pl.BlockSpec((128, 512), lambda i: (0, 0)),
                  pl.BlockSpec((1, 512), lambda i: (0, 0))],
        out_specs=pl.BlockSpec((tm, 512), lambda i: (i, 0)),
        out_shape=jax.ShapeDtypeStruct((s, 512), F32),
        compiler_params=_cparams("parallel"),
    )(z, wa, ba)


def _gla_gate_bwd(dla, z, wa, ba, *, name, tm=256):
    s = z.shape[0]
    tm = min(tm, s)
    col = SEG["ga"][0] // 128

    def body(dla_ref, ga_ref, wa_ref, ba_ref, dga_ref, dwa_ref, dba_ref):
        i = pl.program_id(0)
        gav = ga_ref[...]
        pre = _dot(gav, wa_ref[...]) + ba_ref[...]
        dpre = dla_ref[...] * (1.0 - _sigmoid(pre)) * (1.0 / GLA_TAU)
        dga_ref[...] = _bf(_dot(dpre, wa_ref[...], 1, 1))
        pw = _dot(gav, dpre, 0, 0)
        pb = jnp.sum(dpre, axis=0, keepdims=True)

        @pl.when(i == 0)
        def _():
            dwa_ref[...] = pw
            dba_ref[...] = pb

        @pl.when(i > 0)
        def _():
            dwa_ref[...] += pw
            dba_ref[...] += pb

    return pl.pallas_call(
        body, name=name, grid=(s // tm,),
        in_specs=[pl.BlockSpec((tm, 512), lambda i: (i, 0)), pl.BlockSpec((tm, 128), lambda i: (i, col)),
                  pl.BlockSpec((128, 512), lambda i: (0, 0)), pl.BlockSpec((1, 512), lambda i: (0, 0))],
        out_specs=[pl.BlockSpec((tm, 128), lambda i: (i, 0)), pl.BlockSpec((128, 512), lambda i: (0, 0)),
                   pl.BlockSpec((1, 512), lambda i: (0, 0))],
        out_shape=[jax.ShapeDtypeStruct((s, 128), BF16), jax.ShapeDtypeStruct((128, 512), F32),
                   jax.ShapeDtypeStruct((1, 512), F32)],
        compiler_params=_cparams("arbitrary"),
    )(dla, z, wa, ba)


def _gla_masks(ch):
    ii = lax.broadcasted_iota(jnp.int32, (ch, ch), 0)
    tt = lax.broadcasted_iota(jnp.int32, (ch, ch), 1)
    return jnp.where(tt <= ii, 1.0, 0.0), jnp.where(tt >= ii, 1.0, 0.0)


def _running_sum(x, up):
    n = x.shape[0]
    rows = lax.broadcasted_iota(jnp.int32, x.shape, 0)
    k = 1
    while k < n:
        if up:
            x = x + jnp.where(rows < n - k, pltpu.roll(x, n - k, 0), 0.0)
        else:
            x = x + jnp.where(rows >= k, pltpu.roll(x, k, 0), 0.0)
        k *= 2
    return x


def _gla_chunk(d, tmat, qv, kv, lav, ch):
    c = _running_sum(lav, up=(d == 1))
    big_l = c[ch - 1:ch, :] if d == 0 else c[0:1, :]
    qt = qv * (GLA_DK ** -0.5) * jnp.exp(c)
    kt = kv * jnp.exp(-c)
    kh = kv * jnp.exp(big_l - c)
    return c, big_l, qt, kt, kh


def _gla_fwd(qh, kh_, z, la, *, name):
    s = z.shape[0]
    ch = min(GLA_CHUNK, s)
    n = s // ch
    vcol = SEG["gv"][0] // GROUP_W

    def body(q0, k0, v0, la0, q1, k1, v1, la1, o0, o1, zs0, zs1, st):
        t = pl.program_id(0)

        @pl.when(t == 0)
        def _():
            st[...] = jnp.zeros_like(st)

        masks = _gla_masks(ch)
        for d, (q_ref, k_ref, v_ref, la_ref, o_ref, zs_ref) in enumerate(
                ((q0, k0, v0, la0, o0, zs0), (q1, k1, v1, la1, o1, zs1))):
            for h in range(GLA_HEADS):
                c, big_l, qt, kt, kh = _gla_chunk(d, masks[d], q_ref[h], k_ref[h], la_ref[0, h], ch)
                vv = v_ref[:, h * GLA_DV:(h + 1) * GLA_DV]
                p = _dot(qt, kt, 1, 1) * masks[d]
                zst = st[d, h]
                o_ref[:, h * GLA_DV:(h + 1) * GLA_DV] = _dot(p, vv) + _dot(qt, zst, 1, 1)
                zs_ref[h, 0] = zst
                st[d, h] = zst * jnp.exp(big_l) + _dot(vv, kh, 0, 0)

    cidx = (lambda t: t), (lambda t: n - 1 - t)
    hs = lambda d: pl.BlockSpec((GLA_HEADS, ch, GLA_DK), lambda t: (0, cidx[d](t), 0))
    vs = lambda d: pl.BlockSpec((ch, GROUP_W), lambda t: (cidx[d](t), vcol))
    las = lambda d: pl.BlockSpec((1, GLA_HEADS, ch, GLA_DK), lambda t: (d, 0, cidx[d](t), 0))
    os_ = lambda d: pl.BlockSpec((ch, GROUP_W), lambda t: (cidx[d](t), 0))
    zss = lambda d: pl.BlockSpec((GLA_HEADS, 1, GLA_DV, GLA_DK), lambda t: (0, cidx[d](t), 0, 0))
    o0, o1, zs0, zs1 = pl.pallas_call(
        body, name=name, grid=(n,),
        in_specs=[hs(0), hs(0), vs(0), las(0), hs(1), hs(1), vs(1), las(1)],
        out_specs=[os_(0), os_(1), zss(0), zss(1)],
        out_shape=[jax.ShapeDtypeStruct((s, GROUP_W), F32)] * 2
        + [jax.ShapeDtypeStruct((GLA_HEADS, n, GLA_DV, GLA_DK), F32)] * 2,
        scratch_shapes=[pltpu.VMEM((2, GLA_HEADS, GLA_DV, GLA_DK), F32)],
        compiler_params=_cparams("arbitrary"),
    )(qh, kh_, z, la, qh, kh_, z, la)
    return (o0, o1), (zs0, zs1)


def _gla_bwd(qh, kh_, z, la, do, zs, *, name):
    s = z.shape[0]
    ch = min(GLA_CHUNK, s)
    n = s // ch
    vcol = SEG["gv"][0] // GROUP_W

    def body(q0, k0, v0, la0, do0, zs0, q1, k1, v1, la1, do1, zs1,
             dq0, dk0, dla0, dv0, dq1, dk1, dla1, dv1, gz):
        t = pl.program_id(0)

        @pl.when(t == 0)
        def _():
            gz[...] = jnp.zeros_like(gz)

        masks = _gla_masks(ch)
        rows = lax.broadcasted_iota(jnp.int32, (ch, 1), 0)
        for d, (q_ref, k_ref, v_ref, la_ref, do_ref, zs_ref, dq_ref, dk_ref, dla_ref, dv_ref) in enumerate(
                ((q0, k0, v0, la0, do0, zs0, dq0, dk0, dla0, dv0), (q1, k1, v1, la1, do1, zs1, dq1, dk1, dla1, dv1))):
            tmat = masks[d]
            end = ch - 1 if d == 0 else 0
            for h in range(GLA_HEADS):
                c, big_l, qt, kt, kh = _gla_chunk(d, tmat, q_ref[h], k_ref[h], la_ref[0, h], ch)
                vsl = slice(h * GLA_DV, (h + 1) * GLA_DV)
                vv, dov, zst, gzv = v_ref[:, vsl], do_ref[:, vsl], zs_ref[h, 0], gz[d, h]
                p = _dot(qt, kt, 1, 1) * tmat
                dp = _dot(dov, vv, 1, 1) * tmat
                dqt = _dot(dp, kt) + _dot(dov, zst)
                dkt = _dot(dp, qt, 0, 0)
                dkh = _dot(vv, gzv)
                dv_ref[:, vsl] = _dot(p, dov, 0, 0) + _dot(kh, gzv, 1, 1)
                dq_ref[h] = dqt * jnp.exp(c) * (GLA_DK ** -0.5)
                dk_ref[h] = dkt * jnp.exp(-c) + dkh * jnp.exp(big_l - c)
                e_l = jnp.exp(big_l)
                d_l = jnp.sum(dkh * kh, axis=0, keepdims=True) + e_l * jnp.sum(zst * gzv, axis=0, keepdims=True)
                dc = dqt * qt - dkt * kt - dkh * kh + jnp.where(rows == end, d_l, 0.0)
                dla_ref[h] = _running_sum(dc, up=(d == 0))
                gz[d, h] = gzv * e_l + _dot(dov, qt, 0, 0)

    cidx = (lambda t: n - 1 - t), (lambda t: t)
    hs = lambda d: pl.BlockSpec((GLA_HEADS, ch, GLA_DK), lambda t: (0, cidx[d](t), 0))
    vs = lambda d: pl.BlockSpec((ch, GROUP_W), lambda t: (cidx[d](t), vcol))
    las = lambda d: pl.BlockSpec((1, GLA_HEADS, ch, GLA_DK), lambda t: (d, 0, cidx[d](t), 0))
    row = lambda d: pl.BlockSpec((ch, GROUP_W), lambda t: (cidx[d](t), 0))
    zss = lambda d: pl.BlockSpec((GLA_HEADS, 1, GLA_DV, GLA_DK), lambda t: (0, cidx[d](t), 0, 0))
    hshape = jax.ShapeDtypeStruct((GLA_HEADS, s, GLA_DK), F32)
    wide = jax.ShapeDtypeStruct((s, GROUP_W), F32)
    outs = pl.pallas_call(
        body, name=name, grid=(n,),
        in_specs=[hs(0), hs(0), vs(0), las(0), row(0), zss(0), hs(1), hs(1), vs(1), las(1), row(1), zss(1)],
        out_specs=[hs(0), hs(0), hs(0), row(0), hs(1), hs(1), hs(1), row(1)],
        out_shape=[hshape, hshape, hshape, wide, hshape, hshape, hshape, wide],
        scratch_shapes=[pltpu.VMEM((2, GLA_HEADS, GLA_DV, GLA_DK), F32)],
        compiler_params=_cparams("arbitrary"),
    )(qh, kh_, z, la, do, zs[0], qh, kh_, z, la, do, zs[1])
    dq0, dk0, dla0, dv0, dq1, dk1, dla1, dv1 = outs
    return (dq0, dq1), (dk0, dk1), (dla0, dla1), (dv0, dv1)


def _band(lo, hi, rows, width):
    r = lax.broadcasted_iota(jnp.int32, (rows, width), 0)
    j = lax.broadcasted_iota(jnp.int32, (rows, width), 1)
    k = j - POOL_HALO - r
    return jnp.where((k >= lo) & (k <= hi), 1.0, 0.0)


def _pool_cnt(t0, half, rows, s):
    t = t0 + lax.broadcasted_iota(jnp.int32, (rows, 1), 0)
    return (jnp.minimum(t + half, s) - jnp.maximum(t - half, 0)).astype(F32)


def _pool_fwd(z, pw, scale, *, name):
    s = z.shape[0]
    tl = min(POOL_TILE, s)
    nt = s // tl
    ucol, gcol = SEG["pv"][0] // 128, SEG["pg"][0] // 128

    def body(u_ref, gt_ref, pw_ref, sc_ref, y_ref, pad):
        g = pl.program_id(0)
        half = jnp.left_shift(1, g)
        pad[0:POOL_HALO, :] = jnp.zeros((POOL_HALO, POOL_GW), F32)
        pad[POOL_HALO + s:POOL_HALO + s + POOL_HALO, :] = jnp.zeros((POOL_HALO, POOL_GW), F32)
        pad[POOL_HALO:POOL_HALO + s, :] = u_ref[...]
        band = _band(-half, half - 1, tl, tl + 2 * POOL_HALO)
        pwv, scv = pw_ref[0], sc_ref[...]

        def tile(i, carry):
            t0 = pl.multiple_of(i * tl, tl)
            win = pad[pl.ds(t0, tl + 2 * POOL_HALO), :]
            u = win[POOL_HALO:POOL_HALO + tl, :]
            pooled = _split_dot(band, win) / _pool_cnt(t0, half, tl, s) - u
            mixed = _dot(pooled, pwv)
            silu, _ = _silu_parts(gt_ref[pl.ds(t0, tl), :])
            y_ref[pl.ds(t0, tl), :] = _bf(silu * (mixed * scv))
            return carry

        lax.fori_loop(0, nt, tile, 0)

    return pl.pallas_call(
        body, name=name, grid=(POOL_GROUPS,),
        in_specs=[pl.BlockSpec((s, POOL_GW), lambda g: (0, ucol + g)),
                  pl.BlockSpec((s, POOL_GW), lambda g: (0, gcol + g)),
                  pl.BlockSpec((1, POOL_GW, POOL_GW), lambda g: (g, 0, 0)),
                  pl.BlockSpec((1, POOL_GW), lambda g: (0, g))],
        out_specs=pl.BlockSpec((s, POOL_GW), lambda g: (0, g)),
        out_shape=jax.ShapeDtypeStruct((s, GROUP_W), BF16),
        scratch_shapes=[pltpu.VMEM((s + 2 * POOL_HALO, POOL_GW), F32)],
        compiler_params=_cparams("parallel"),
    )(z, z, pw, scale)


def _pool_bwd(dy, z, pw, scale, *, name):
    s = z.shape[0]
    tl = min(POOL_TILE, s)
    nt = s // tl
    ucol, gcol, ycol = SEG["pv"][0] // 128, SEG["pg"][0] // 128, 2 * GROUP_W // 128

    def body(dy_ref, u_ref, gt_ref, pw_ref, sc_ref, du_ref, dgt_ref, dpw_ref, dsc_ref, pad, epad, dpo):
        g = pl.program_id(0)
        half = jnp.left_shift(1, g)
        zeros = jnp.zeros((POOL_HALO, POOL_GW), F32)
        for buf in (pad, epad):
            buf[0:POOL_HALO, :] = zeros
            buf[POOL_HALO + s:POOL_HALO + s + POOL_HALO, :] = zeros
        pad[POOL_HALO:POOL_HALO + s, :] = u_ref[...]
        band = _band(-half, half - 1, tl, tl + 2 * POOL_HALO)
        band_t = _band(1 - half, half, tl, tl + 2 * POOL_HALO)
        pwv, scv = pw_ref[0], sc_ref[...]
        dpw_ref[0] = jnp.zeros((POOL_GW, POOL_GW), F32)
        dsc_ref[...] = jnp.zeros((1, POOL_GW), F32)

        def tile(i, carry):
            t0 = pl.multiple_of(i * tl, tl)
            win = pad[pl.ds(t0, tl + 2 * POOL_HALO), :]
            u = win[POOL_HALO:POOL_HALO + tl, :]
            cnt = _pool_cnt(t0, half, tl, s)
            pooled = _split_dot(band, win) / cnt - u
            mixed = _dot(pooled, pwv)
            silu, dsilu = _silu_parts(gt_ref[pl.ds(t0, tl), :])
            dyv = dy_ref[pl.ds(t0, tl), :]
            dgt_ref[pl.ds(t0, tl), :] = _bf(dyv * (mixed * scv) * dsilu)
            dsc_ref[...] += jnp.sum(dyv * silu * mixed, axis=0, keepdims=True)
            dm = dyv * silu * scv
            dpw_ref[0] += _dot(pooled, dm, 0, 0)
            dpooled = _dot(dm, pwv, 1, 1)
            dpo[pl.ds(t0, tl), :] = dpooled
            epad[pl.ds(POOL_HALO + t0, tl), :] = dpooled / cnt
            return carry

        lax.fori_loop(0, nt, tile, 0)

        def tile2(i, carry):
            t0 = pl.multiple_of(i * tl, tl)
            ewin = epad[pl.ds(t0, tl + 2 * POOL_HALO), :]
            du_ref[pl.ds(t0, tl), :] = _bf(_split_dot(band_t, ewin) - dpo[pl.ds(t0, tl), :])
            return carry

        lax.fori_loop(0, nt, tile2, 0)

    col = lambda c0: pl.BlockSpec((s, POOL_GW), lambda g: (0, c0 + g))
    return pl.pallas_call(
        body, name=name, grid=(POOL_GROUPS,),
        in_specs=[col(ycol), col(ucol), col(gcol), pl.BlockSpec((1, POOL_GW, POOL_GW), lambda g: (g, 0, 0)),
                  pl.BlockSpec((1, POOL_GW), lambda g: (0, g))],
        out_specs=[col(0), col(0), pl.BlockSpec((1, POOL_GW, POOL_GW), lambda g: (g, 0, 0)),
                   pl.BlockSpec((1, POOL_GW), lambda g: (0, g))],
        out_shape=[jax.ShapeDtypeStruct((s, GROUP_W), BF16), jax.ShapeDtypeStruct((s, GROUP_W), BF16),
                   jax.ShapeDtypeStruct((POOL_GROUPS, POOL_GW, POOL_GW), F32),
                   jax.ShapeDtypeStruct((1, GROUP_W), F32)],
        scratch_shapes=[pltpu.VMEM((s + 2 * POOL_HALO, POOL_GW), F32), pltpu.VMEM((s + 2 * POOL_HALO, POOL_GW), F32),
                        pltpu.VMEM((s, POOL_GW), F32)],
        compiler_params=_cparams("parallel"),
    )(dy, z, z, pw, scale)


def _mla_specs(tm):
    zq = pl.BlockSpec((tm, 512), lambda i: (i, SEG["mq"][0] // 512))
    zkv = pl.BlockSpec((tm, 256), lambda i: (i, SEG["mkv"][0] // 256))
    zkr = pl.BlockSpec((tm, 128), lambda i: (i, SEG["mkr"][0] // 128))
    full = lambda r, c: pl.BlockSpec((r, c), lambda i: (0, 0))
    tab = pl.BlockSpec((tm, 128), lambda i: (i, 0))
    weights = [full(1, 512), full(512, 1024), full(1, 256), full(256, 1024), full(1, 256), full(1, 256)]
    return [zq, zkv, zkr] + weights + [tab, tab, tab]


def _mla_project(xq_ref, xkv_ref, qg_ref, wq_ref, kvg_ref, wkv_ref):
    xq = xq_ref[...]
    r1 = lax.rsqrt(jnp.mean(xq * xq, axis=-1, keepdims=True) + EPS)
    xn1 = xq * r1
    qn = _bf(xn1 * qg_ref[...])
    qraw = _dot(qn, wq_ref[...])
    xkv = xkv_ref[...]
    r2 = lax.rsqrt(jnp.mean(xkv * xkv, axis=-1, keepdims=True) + EPS)
    xn2 = xkv * r2
    kvn = _bf(xn2 * kvg_ref[...])
    kvraw = _dot(kvn, wkv_ref[...])
    return r1, xn1, qn, qraw, r2, xn2, kvn, kvraw


def _mla_pre(z, qg, wq, kvg, wkv, qng, kng, cos, sp, sn, *, name, tm=256):
    s = z.shape[0]
    tm = min(tm, s)

    def body(xq_ref, xkv_ref, pe_ref, qg_ref, wq_ref, kvg_ref, wkv_ref, qng_ref, kng_ref, c_ref, sp_ref, sn_ref,
             q_ref, k_ref, v_ref):
        _, _, _, qraw, _, _, _, kvraw = _mla_project(xq_ref, xkv_ref, qg_ref, wq_ref, kvg_ref, wkv_ref)
        c, spv, snv = c_ref[...], sp_ref[...], sn_ref[...]
        pe = pe_ref[...]
        pe_ss = jnp.sum(pe * pe, axis=-1, keepdims=True)
        qngv, kngv = qng_ref[...], kng_ref[...]
        for h in range(MLA_HEADS):
            b = h * MLA_QKP
            qh = qraw[:, b:b + MLA_QKP]
            r = lax.rsqrt(jnp.sum(qh * qh, axis=-1, keepdims=True) * (1.0 / MLA_QK) + EPS)
            qn_h = qh * r * qngv
            q_ref[:, b:b + 128] = _bf(qn_h[:, :128] * MLA_SCALE)
            q_ref[:, b + 128:b + 256] = _bf(_rope64(qn_h[:, 128:], c, spv, snv) * MLA_SCALE)
            kn = kvraw[:, b:b + 128]
            rk = lax.rsqrt((jnp.sum(kn * kn, axis=-1, keepdims=True) + pe_ss) * (1.0 / MLA_QK) + EPS)
            k_ref[:, b:b + 128] = _bf(kn * rk * kngv[:, :128])
            k_ref[:, b + 128:b + 256] = _bf(_rope64(pe * rk * kngv[:, 128:], c, spv, snv))
            v_ref[:, h * MLA_V:(h + 1) * MLA_V] = _bf(kvraw[:, b + 128:b + 256])

    row = lambda w: pl.BlockSpec((tm, w), lambda i: (i, 0))
    return pl.pallas_call(
        body, name=name, grid=(s // tm,), in_specs=_mla_specs(tm),
        out_specs=[row(1024), row(1024), row(512)],
        out_shape=[jax.ShapeDtypeStruct((s, 1024), BF16), jax.ShapeDtypeStruct((s, 1024), BF16),
                   jax.ShapeDtypeStruct((s, 512), BF16)],
        compiler_params=_cparams("parallel"),
    )(z, z, z, qg, wq, kvg, wkv, qng, kng, cos, sp, sn)


def _mla_pre_bwd(dq, dk, dv, z, qg, wq, kvg, wkv, qng, kng, cos, sp, sn, *, name, tm=256):
    s = z.shape[0]
    tm = min(tm, s)

    def body(dq_ref, dk_ref, dv_ref, xq_ref, xkv_ref, pe_ref, qg_ref, wq_ref, kvg_ref, wkv_ref, qng_ref, kng_ref,
             c_ref, sp_ref, sn_ref, dxq_ref, dxkv_ref, dpe_ref, dwq_ref, dwkv_ref, dqg_ref, dkvg_ref, dqng_ref,
             dkng_ref, dqraw, dkvraw):
        i = pl.program_id(0)
        r1, xn1, qn, qraw, r2, xn2, kvn, kvraw = _mla_project(xq_ref, xkv_ref, qg_ref, wq_ref, kvg_ref, wkv_ref)
        c, spv, snv = c_ref[...], sp_ref[...], sn_ref[...]
        pe = pe_ref[...]
        pe_ss = jnp.sum(pe * pe, axis=-1, keepdims=True)
        qngv, kngv = qng_ref[...], kng_ref[...]
        dqng = jnp.zeros((1, MLA_QKP), F32)
        dkng = jnp.zeros((1, MLA_QKP), F32)
        dpe = jnp.zeros_like(pe)
        for h in range(MLA_HEADS):
            b = h * MLA_QKP
            qh = qraw[:, b:b + MLA_QKP]
            r = lax.rsqrt(jnp.sum(qh * qh, axis=-1, keepdims=True) * (1.0 / MLA_QK) + EPS)
            xn = qh * r
            d_n = jnp.concatenate(
                [dq_ref[:, b:b + 128], _unrope64(dq_ref[:, b + 128:b + 256], c, spv, snv)], axis=1) * MLA_SCALE
            dqng = dqng + jnp.sum(d_n * xn, axis=0, keepdims=True)
            dxn = d_n * qngv
            dqraw[:, b:b + MLA_QKP] = _bf(r * (dxn - xn * (jnp.sum(dxn * xn, axis=-1, keepdims=True) * (1.0 / MLA_QK))))
            kn = kvraw[:, b:b + 128]
            rk = lax.rsqrt((jnp.sum(kn * kn, axis=-1, keepdims=True) + pe_ss) * (1.0 / MLA_QK) + EPS)
            xk = jnp.concatenate([kn, pe], axis=1) * rk
            d_k = jnp.concatenate(
                [dk_ref[:, b:b + 128], _unrope64(dk_ref[:, b + 128:b + 256], c, spv, snv)], axis=1)
            dkng = dkng + jnp.sum(d_k * xk, axis=0, keepdims=True)
            dxk = d_k * kngv
            dfull = rk * (dxk - xk * (jnp.sum(dxk * xk, axis=-1, keepdims=True) * (1.0 / MLA_QK)))
            dkvraw[:, b:b + 128] = _bf(dfull[:, :128])
            dkvraw[:, b + 128:b + 256] = _bf(dv_ref[:, h * MLA_V:(h + 1) * MLA_V])
            dpe = dpe + dfull[:, 128:]
        dpe_ref[...] = _bf(dpe)
        dqr, dkvr = dqraw[...], dkvraw[...]
        dqn = _dot(dqr, wq_ref[...], 1, 1)
        dxn1 = dqn * qg_ref[...]
        dxq_ref[...] = _bf(r1 * (dxn1 - xn1 * jnp.mean(dxn1 * xn1, axis=-1, keepdims=True)))
        dkvn = _dot(dkvr, wkv_ref[...], 1, 1)
        dxn2 = dkvn * kvg_ref[...]
        dxkv_ref[...] = _bf(r2 * (dxn2 - xn2 * jnp.mean(dxn2 * xn2, axis=-1, keepdims=True)))
        parts = (_dot(qn, dqr, 0, 0), _dot(kvn, dkvr, 0, 0), jnp.sum(dqn * xn1, axis=0, keepdims=True),
                 jnp.sum(dkvn * xn2, axis=0, keepdims=True), dqng, dkng)
        accs = (dwq_ref, dwkv_ref, dqg_ref, dkvg_ref, dqng_ref, dkng_ref)

        @pl.when(i == 0)
        def _():
            for a, p in zip(accs, parts):
                a[...] = p

        @pl.when(i > 0)
        def _():
            for a, p in zip(accs, parts):
                a[...] += p

    row = lambda w: pl.BlockSpec((tm, w), lambda i: (i, 0))
    full = lambda r, c: pl.BlockSpec((r, c), lambda i: (0, 0))
    return pl.pallas_call(
        body, name=name, grid=(s // tm,),
        in_specs=[row(1024), row(1024), row(512)] + _mla_specs(tm),
        out_specs=[row(512), row(256), row(128), full(512, 1024), full(256, 1024), full(1, 512), full(1, 256),
                   full(1, 256), full(1, 256)],
        out_shape=[jax.ShapeDtypeStruct((s, 512), BF16), jax.ShapeDtypeStruct((s, 256), BF16),
                   jax.ShapeDtypeStruct((s, 128), BF16), jax.ShapeDtypeStruct((512, 1024), F32),
                   jax.ShapeDtypeStruct((256, 1024), F32), jax.ShapeDtypeStruct((1, 512), F32),
                   jax.ShapeDtypeStruct((1, 256), F32), jax.ShapeDtypeStruct((1, 256), F32),
                   jax.ShapeDtypeStruct((1, 256), F32)],
        scratch_shapes=[pltpu.VMEM((tm, 1024), BF16), pltpu.VMEM((tm, 1024), BF16)],
        compiler_params=_cparams("arbitrary"),
    )(dq, dk, dv, z, z, z, qg, wq, kvg, wkv, qng, kng, cos, sp, sn)


def _flash_fwd(q, k, v, *, name, tq=1024, tk=1024, rider=None):
    s = q.shape[0]
    tq, tk = min(tq, s), min(tk, s)
    nk = s // tk
    strip = min(FLASH_STRIP, tq)

    def body(q_ref, k_ref, v_ref, o_ref, lse_ref, m_s, l_s, acc):
        j = pl.program_id(2)

        @pl.when(j == 0)
        def _():
            m_s[...] = jnp.full_like(m_s, -jnp.inf)
            l_s[...] = jnp.zeros_like(l_s)
            acc[...] = jnp.zeros_like(acc)

        for r in range(tq // strip):
            rows = slice(r * strip, (r + 1) * strip)
            sc = _dot(q_ref[rows, :], k_ref[...], 1, 1)
            m_prev = m_s[rows, :]
            m_new = jnp.maximum(m_prev, jnp.max(sc, axis=-1, keepdims=True))
            p = jnp.exp(sc - m_new[:, 0:1])
            alpha = jnp.exp(m_prev - m_new)
            l_s[rows, :] = alpha * l_s[rows, :] + jnp.sum(p, axis=-1, keepdims=True)
            acc[rows, :] = alpha * acc[rows, :] + _dot(p, v_ref[...])
            m_s[rows, :] = m_new

        @pl.when(j == nk - 1)
        def _():
            o_ref[...] = acc[...] / l_s[...]
            lse_ref[...] = m_s[...] + jnp.log(l_s[...])

    (o, lse), rode = _ride_call(
        body, rider, name=name, grid=(MLA_HEADS, s // tq, nk),
        in_specs=[pl.BlockSpec((tq, MLA_QKP), lambda h, i, j: (i, h)),
                  pl.BlockSpec((tk, MLA_QKP), lambda h, i, j: (j, h)),
                  pl.BlockSpec((tk, MLA_V), lambda h, i, j: (j, h))],
        out_specs=[pl.BlockSpec((tq, MLA_V), lambda h, i, j: (i, h))] * 2,
        out_shape=[jax.ShapeDtypeStruct((s, GROUP_W), F32)] * 2,
        scratch_shapes=[pltpu.VMEM((tq, MLA_V), F32), pltpu.VMEM((tq, MLA_V), F32), pltpu.VMEM((tq, MLA_V), F32)],
        args=(q, k, v), sem=("parallel", "parallel", "arbitrary"))
    return (o, lse) if rider is None else (o, lse, rode)


def _flash_bwd(q, k, v, do, o, lse, *, name, tq=1024, tk=1024, rider=None):
    s = q.shape[0]
    tq, tk = min(tq, s), min(tk, s)
    nq, nk = s // tq, s // tk

    def body(q_ref, k_ref, v_ref, do_ref, o_ref, lse_ref, dq_ref, dk_ref, dv_ref, dk_acc, dv_acc):
        j, i = pl.program_id(1), pl.program_id(2)
        dov = do_ref[...]
        delta = jnp.sum(dov * o_ref[...], axis=-1, keepdims=True)
        p = jnp.exp(_dot(q_ref[...], k_ref[...], 1, 1) - lse_ref[:, 0:1])
        ds = p * (_dot(dov, v_ref[...], 1, 1) - delta)
        pv = _dot(p, dov, 0, 0)
        pk = _dot(ds, q_ref[...], 0, 0)
        pq = _dot(ds, k_ref[...])
        rows = pl.ds(pl.multiple_of(i * tq, tq), tq)

        @pl.when(j == 0)
        def _():
            dq_ref[rows, :] = pq

        @pl.when(j > 0)
        def _():
            dq_ref[rows, :] += pq

        @pl.when(i == 0)
        def _():
            dv_acc[...] = pv
            dk_acc[...] = pk

        @pl.when(i > 0)
        def _():
            dv_acc[...] += pv
            dk_acc[...] += pk

        @pl.when(i == nq - 1)
        def _():
            dk_ref[...] = dk_acc[...]
            dv_ref[...] = dv_acc[...]

    qb = pl.BlockSpec((tq, MLA_QKP), lambda h, j, i: (i, h))
    kb = pl.BlockSpec((tk, MLA_QKP), lambda h, j, i: (j, h))
    vb = pl.BlockSpec((tk, MLA_V), lambda h, j, i: (j, h))
    ob = pl.BlockSpec((tq, MLA_V), lambda h, j, i: (i, h))
    (dq, dk, dv), rode = _ride_call(
        body, rider, name=name, grid=(MLA_HEADS, nk, nq),
        in_specs=[qb, kb, vb, ob, ob, ob],
        out_specs=[pl.BlockSpec((s, MLA_QKP), lambda h, j, i: (0, h)), kb, vb],
        out_shape=[jax.ShapeDtypeStruct((s, MLA_HEADS * MLA_QKP), F32),
                   jax.ShapeDtypeStruct((s, MLA_HEADS * MLA_QKP), F32), jax.ShapeDtypeStruct((s, GROUP_W), F32)],
        scratch_shapes=[pltpu.VMEM((tk, MLA_QKP), F32), pltpu.VMEM((tk, MLA_V), F32)],
        args=(q, k, v, do, o, lse), sem=("arbitrary", "arbitrary", "arbitrary"))
    return (dq, dk, dv) if rider is None else (dq, dk, dv, rode)


def _rows_tile(r, c, itemsize=4, budget=2 * 1024 * 1024):
    if r * c * itemsize <= budget:
        return r
    best = None
    for t in range(8, r, 8):
        if r % t == 0 and t * c * itemsize <= budget:
            best = t
    return best if best is not None else r


def _add_n(arrs, *, out_dtype=F32, name):
    shape = arrs[0].shape
    c = shape[-1]
    flat = [a.reshape(-1, c) for a in arrs]
    r = flat[0].shape[0]
    t = _rows_tile(r, c)

    def body(*refs):
        acc = refs[0][...].astype(F32)
        for ref in refs[1:-1]:
            acc = acc + ref[...].astype(F32)
        refs[-1][...] = acc.astype(out_dtype)

    blk = pl.BlockSpec((t, c), lambda i: (i, 0))
    out = pl.pallas_call(
        body, name=name, grid=(r // t,), in_specs=[blk] * len(flat), out_specs=blk,
        out_shape=jax.ShapeDtypeStruct((r, c), out_dtype), compiler_params=_cparams("parallel"),
    )(*flat)
    return out.reshape(shape)


def _adamw(w, g, m, v, *, name):
    shape = w.shape
    c = shape[-1]
    flat = [a.reshape(-1, c) for a in (w, g, m, v)]
    r = flat[0].shape[0]
    t = _rows_tile(r, c, budget=1024 * 1024)

    def body(w_ref, g_ref, m_ref, v_ref, d_ref, mo_ref, vo_ref):
        gv = g_ref[...]
        m2 = ADAM_B1 * m_ref[...] + (1.0 - ADAM_B1) * gv
        v2 = ADAM_B2 * v_ref[...] + (1.0 - ADAM_B2) * (gv * gv)
        m_hat = m2 / (1.0 - ADAM_B1 ** ADAM_STEP)
        v_hat = v2 / (1.0 - ADAM_B2 ** ADAM_STEP)
        d_ref[...] = -ADAM_LR * (m_hat / (jnp.sqrt(v_hat) + ADAM_EPS) + ADAM_WD * w_ref[...])
        mo_ref[...] = m2
        vo_ref[...] = v2

    blk = pl.BlockSpec((t, c), lambda i: (i, 0))
    outs = pl.pallas_call(
        body, name=name, grid=(r // t,), in_specs=[blk] * 4, out_specs=[blk] * 3,
        out_shape=[jax.ShapeDtypeStruct((r, c), F32)] * 3, compiler_params=_cparams("parallel"),
    )(*flat)
    return tuple(o.reshape(shape) for o in outs)


def _place():
    x, y, c = lax.axis_index("x"), lax.axis_index("y"), lax.axis_index("c")
    chips = [(1 - x, y), (x, 1 - y), (1 - x, 1 - y)]
    return x, y, c, chips


ANY = pl.BlockSpec(memory_space=pl.ANY)


def _half(ref, axis, hc, lead=()):
    n = ref.shape[len(lead) + axis] // 2
    return ref.at[tuple(lead) + (slice(None),) * axis + (pl.ds(hc * n, n),)]


def _gather_shards(shards, axes, *, name):
    nt = len(shards)

    def body(*refs):
        src, dst = refs[:nt], refs[nt:2 * nt]
        send, recv, fsend, frecv, lsem = refs[2 * nt:]
        x, y, c, chips = _place()
        me = 2 * x + y
        local = [pltpu.make_async_copy(src[t], dst[t].at[me], lsem.at[t]) for t in range(nt)]
        for cp in local:
            cp.start()

        def half(t, slot, hc):
            return _half(dst[t], axes[t], hc, lead=(slot,))

        def first(t, k):
            return pltpu.make_async_remote_copy(
                src_ref=_half(src[t], axes[t], c), dst_ref=half(t, me, c),
                send_sem=send.at[t, k], recv_sem=recv.at[t, k],
                device_id=(chips[k][0], chips[k][1], c), device_id_type=MESH)

        def landed(t, k):
            slot = 2 * chips[k][0] + chips[k][1]
            return pltpu.make_async_remote_copy(
                src_ref=half(t, slot, c), dst_ref=half(t, slot, c),
                send_sem=send.at[t, k], recv_sem=recv.at[t, k],
                device_id=(chips[k][0], chips[k][1], c), device_id_type=MESH)

        def forward(t, k, hc):
            slot = 2 * chips[k][0] + chips[k][1]
            return pltpu.make_async_remote_copy(
                src_ref=half(t, slot, hc), dst_ref=half(t, slot, hc),
                send_sem=fsend.at[t, k], recv_sem=frecv.at[t, k],
                device_id=(x, y, 1 - c), device_id_type=MESH)

        for t in range(nt):
            for k in range(3):
                first(t, k).start()
        for t in range(nt):
            for k in range(3):
                landed(t, k).wait_recv()
                forward(t, k, c).start()
        for t in range(nt):
            for k in range(3):
                forward(t, k, 1 - c).wait_recv()
        for t in range(nt):
            for k in range(3):
                first(t, k).wait_send()
                forward(t, k, c).wait_send()
        for cp in local:
            cp.wait()

    return pl.pallas_call(
        body, name=name, in_specs=[ANY] * nt, out_specs=[ANY] * nt,
        out_shape=[jax.ShapeDtypeStruct((N_CHIP,) + a.shape, a.dtype) for a in shards],
        scratch_shapes=[pltpu.SemaphoreType.DMA((nt, 3)), pltpu.SemaphoreType.DMA((nt, 3)),
                        pltpu.SemaphoreType.DMA((nt, 3)), pltpu.SemaphoreType.DMA((nt, 3)),
                        pltpu.SemaphoreType.DMA((nt,))],
    )(*shards)


def _comm_rows(hr, c, budget=2 * 1024 * 1024):
    if hr * c * 4 <= budget:
        return hr
    best = None
    for t in range(16, hr, 16):
        if hr % t == 0 and t * c * 4 <= budget:
            best = t
    return best if best is not None else hr


def _comm_cols(r, hc, budget=2 * 1024 * 1024):
    best = 128
    for t in range(128, hc + 1, 128):
        if hc % t == 0 and r * t * 4 <= budget:
            best = t
    return best


def _comm_chunks(shape, axis):
    r, cdim = shape
    if axis == 0:
        rc = _comm_rows(r // 2, cdim)
        nt = (r // 2) // rc
        return (rc, cdim), nt, (lambda h, t: (h * nt + t, 0))
    cc = _comm_cols(r, cdim // 2)
    nt = (cdim // 2) // cc
    return (r, cc), nt, (lambda h, t: (0, h * nt + t))


def _pair_reduce(g, where, axis, *, out_dtype, name):
    n_slot, r, cdim = g.shape
    blk_shape, nr, at = _comm_chunks((r, cdim), axis)
    steps = n_slot * nr
    half_shape = (r // 2, cdim) if axis == 0 else (r, cdim // 2)

    def body(w_ref, a_ref, b_ref, o_ref, land, send, recv, credit):
        x, y, c, _ = _place()
        sib = (x, y, 1 - c)
        i = pl.program_id(0) * nr + pl.program_id(1)
        s = lax.rem(i, 2)

        @pl.when(i >= 2)
        def _():
            pl.semaphore_wait(credit.at[s], 1)

        cp = pltpu.make_async_remote_copy(src_ref=b_ref.at[0], dst_ref=land.at[s], send_sem=send.at[s],
                                          recv_sem=recv.at[s], device_id=sib, device_id_type=MESH)
        cp.start()
        cp.wait_recv()
        o_ref[0] = (a_ref[0] + land[s]).astype(out_dtype)
        cp.wait_send()

        @pl.when(i + 2 < steps)
        def _():
            pl.semaphore_signal(credit.at[s], inc=1, device_id=sib, device_id_type=MESH)

    blk = lambda half: pl.BlockSpec((1,) + blk_shape, lambda j, t, w: (j,) + at(half(w), t))
    grid_spec = pltpu.PrefetchScalarGridSpec(
        num_scalar_prefetch=1, grid=(n_slot, nr),
        in_specs=[blk(lambda w: w[0]), blk(lambda w: 1 - w[0])],
        out_specs=pl.BlockSpec((1,) + blk_shape, lambda j, t, w: (j,) + at(0, t)),
        scratch_shapes=[pltpu.VMEM((2,) + blk_shape, F32), pltpu.SemaphoreType.DMA((2,)),
                        pltpu.SemaphoreType.DMA((2,)), pltpu.SemaphoreType.REGULAR((2,))])
    return pl.pallas_call(
        body, name=name, grid_spec=grid_spec, out_shape=jax.ShapeDtypeStruct((n_slot,) + half_shape, out_dtype),
        compiler_params=_cparams("arbitrary", "arbitrary"),
    )(where, g, g)


def _chip_exchange(parts, *, name):
    nt = len(parts)

    def body(*refs):
        src, got = refs[:nt], refs[nt:2 * nt]
        send, recv = refs[2 * nt:]
        x, y, c, chips = _place()
        remote = []
        for t in range(nt):
            for k in range(3):
                remote.append(pltpu.make_async_remote_copy(
                    src_ref=src[t].at[2 * chips[k][0] + chips[k][1]], dst_ref=got[t].at[k],
                    send_sem=send.at[t, k], recv_sem=recv.at[t, k],
                    device_id=(chips[k][0], chips[k][1], c), device_id_type=MESH))
        for cp in remote:
            cp.start()
        for cp in remote:
            cp.wait_recv()
        for cp in remote:
            cp.wait_send()

    return pl.pallas_call(
        body, name=name, in_specs=[ANY] * nt, out_specs=[ANY] * nt,
        out_shape=[jax.ShapeDtypeStruct((3,) + a.shape[1:], a.dtype) for a in parts],
        scratch_shapes=[pltpu.SemaphoreType.DMA((nt, 3)), pltpu.SemaphoreType.DMA((nt, 3))],
    )(*parts)


def _sum_join(p, got, where, axis, *, name):
    _, hr, cdim = p.shape
    full = (2 * hr, cdim) if axis == 0 else (hr, 2 * cdim)
    blk_shape, n, at = _comm_chunks(full, axis)
    step_len = blk_shape[axis]
    half_len = full[axis] // 2

    def body(w_ref, p_ref, g_ref, out, buf, lsem, ssem, rsem):
        x, y, c, _ = _place()
        sib = (x, y, 1 - c)
        r = pl.program_id(0)

        def part(start, size):
            return out.at[(slice(None),) * axis + (pl.ds(start, size),)]

        def copies(step, slot):
            rows = part(pl.multiple_of(c * half_len + step * step_len, 8 if axis == 0 else 128), step_len)
            return (pltpu.make_async_copy(buf.at[slot], rows, lsem.at[slot]),
                    pltpu.make_async_remote_copy(src_ref=buf.at[slot], dst_ref=rows, send_sem=ssem.at[slot],
                                                 recv_sem=rsem, device_id=sib, device_id_type=MESH))

        s = lax.rem(r, 2)

        @pl.when(r >= 2)
        def _():
            lc, rm = copies(r - 2, s)
            lc.wait()
            rm.wait_send()

        buf[s] = p_ref[0].astype(F32) + g_ref[0].astype(F32) + g_ref[1].astype(F32) + g_ref[2].astype(F32)
        lc, rm = copies(r, s)
        lc.start()
        rm.start()

        @pl.when(r == n - 1)
        def _():
            for step in range(max(0, n - 2), n):
                lc, rm = copies(step, step % 2)
                lc.wait()
                rm.wait_send()
            whole = part(0, half_len)
            pltpu.make_async_remote_copy(src_ref=whole, dst_ref=whole, send_sem=ssem.at[0], recv_sem=rsem,
                                         device_id=sib, device_id_type=MESH).wait_recv()

    grid_spec = pltpu.PrefetchScalarGridSpec(
        num_scalar_prefetch=1, grid=(n,),
        in_specs=[pl.BlockSpec((1,) + blk_shape, lambda t, w: (w[1],) + at(0, t)),
                  pl.BlockSpec((3,) + blk_shape, lambda t, w: (0,) + at(0, t))],
        out_specs=ANY,
        scratch_shapes=[pltpu.VMEM((2,) + blk_shape, F32), pltpu.SemaphoreType.DMA((2,)),
                        pltpu.SemaphoreType.DMA((2,)), pltpu.SemaphoreType.DMA])
    return pl.pallas_call(
        body, name=name, grid_spec=grid_spec, out_shape=jax.ShapeDtypeStruct(full, F32),
        compiler_params=_cparams("arbitrary"),
    )(where, p, got)


def _rider_gather_send(shards, axes):
    nt = len(shards)

    def copies(src, dst, send, recv, lsem):
        x, y, c, chips = _place()
        me = 2 * x + y
        local = [pltpu.make_async_copy(src[t], dst[t].at[me], lsem.at[t]) for t in range(nt)]
        out, landed = [], []
        for t in range(nt):
            for k in range(3):
                peer = (chips[k][0], chips[k][1], c)
                out.append(pltpu.make_async_remote_copy(
                    src_ref=_half(src[t], axes[t], c), dst_ref=_half(dst[t], axes[t], c, lead=(me,)),
                    send_sem=send.at[t, k], recv_sem=recv.at[t, k], device_id=peer, device_id_type=MESH))
                theirs = _half(dst[t], axes[t], c, lead=(2 * chips[k][0] + chips[k][1],))
                landed.append(pltpu.make_async_remote_copy(
                    src_ref=theirs, dst_ref=theirs, send_sem=send.at[t, k], recv_sem=recv.at[t, k],
                    device_id=peer, device_id_type=MESH))
        return local, out, landed

    def start(src, dst, sems):
        local, out, _ = copies(src, dst, *sems)
        for cp in local + out:
            cp.start()

    def finish(src, dst, sems):
        local, out, landed = copies(src, dst, *sems)
        for cp in landed:
            cp.wait_recv()
        for cp in out:
            cp.wait_send()
        for cp in local:
            cp.wait()

    return _Rider(shards, [jax.ShapeDtypeStruct((N_CHIP,) + a.shape, a.dtype) for a in shards],
                  [pltpu.SemaphoreType.DMA((nt, 3)), pltpu.SemaphoreType.DMA((nt, 3)), pltpu.SemaphoreType.DMA((nt,))],
                  start, finish)


def _rider_gather_forward(bufs, axes):
    nt = len(bufs)

    def copies(src, dst, send, recv):
        x, y, c, chips = _place()
        mine, theirs = [], []
        for t in range(nt):
            for k in range(3):
                slot = 2 * chips[k][0] + chips[k][1]
                for hc, into in ((c, mine), (1 - c, theirs)):
                    into.append(pltpu.make_async_remote_copy(
                        src_ref=_half(src[t], axes[t], hc, lead=(slot,)),
                        dst_ref=_half(dst[t], axes[t], hc, lead=(slot,)),
                        send_sem=send.at[t, k], recv_sem=recv.at[t, k], device_id=(x, y, 1 - c), device_id_type=MESH))
        return mine, theirs

    def start(src, dst, sems):
        for cp in copies(src, dst, *sems)[0]:
            cp.start()

    def finish(src, dst, sems):
        mine, theirs = copies(src, dst, *sems)
        for cp in theirs:
            cp.wait_recv()
        for cp in mine:
            cp.wait_send()

    return _Rider(bufs, [jax.ShapeDtypeStruct(a.shape, a.dtype) for a in bufs],
                  [pltpu.SemaphoreType.DMA((nt, 3)), pltpu.SemaphoreType.DMA((nt, 3))], start, finish,
                  aliases={t: t for t in range(nt)})


def _rider_chip_exchange(parts):
    nt = len(parts)

    def copies(src, got, send, recv):
        x, y, c, chips = _place()
        return [pltpu.make_async_remote_copy(
            src_ref=src[t].at[2 * chips[k][0] + chips[k][1]], dst_ref=got[t].at[k], send_sem=send.at[t, k],
            recv_sem=recv.at[t, k], device_id=(chips[k][0], chips[k][1], c), device_id_type=MESH)
            for t in range(nt) for k in range(3)]

    def start(src, got, sems):
        for cp in copies(src, got, *sems):
            cp.start()

    def finish(src, got, sems):
        remote = copies(src, got, *sems)
        for cp in remote:
            cp.wait_recv()
        for cp in remote:
            cp.wait_send()

    return _Rider(parts, [jax.ShapeDtypeStruct((3,) + a.shape[1:], a.dtype) for a in parts],
                  [pltpu.SemaphoreType.DMA((nt, 3)), pltpu.SemaphoreType.DMA((nt, 3))], start, finish)


def _gather_all(block, *, name):
    m_per, n = block.shape

    def body(x_ref, out_ref, send_sems, recv_sems, local_sem):
        x, y, c, chips = _place()
        me, sibling = (x, y, c), (x, y, 1 - c)

        def rows(px, py, pc):
            return out_ref.at[4 * px + 2 * py + pc]

        def copy(k, blk, to, src=None):
            return pltpu.make_async_remote_copy(
                src_ref=rows(*blk) if src is None else src, dst_ref=rows(*blk),
                send_sem=send_sems.at[k], recv_sem=recv_sems.at[k], device_id=to, device_id_type=MESH)

        mine = pltpu.make_async_copy(x_ref, rows(*me), local_sem)
        mine.start()
        first = [copy(0, me, sibling, src=x_ref)]
        first += [copy(1 + j, me, (*chip, c), src=x_ref) for j, chip in enumerate(chips)]
        for cp in first:
            cp.start()
        passed = [copy(4 + j, (*chip, c), sibling) for j, chip in enumerate(chips)]
        for j, chip in enumerate(chips):
            copy(1 + j, (*chip, c), me).wait_recv()
            passed[j].start()
        copy(0, sibling, me).wait_recv()
        for j, chip in enumerate(chips):
            copy(4 + j, (*chip, 1 - c), me).wait_recv()
        for cp in first + passed:
            cp.wait_send()
        mine.wait()

    return pl.pallas_call(
        body, name=name,
        out_shape=jax.ShapeDtypeStruct((N_DEV, m_per, n), block.dtype),
        in_specs=[pl.BlockSpec(memory_space=pltpu.VMEM)], out_specs=pl.BlockSpec(memory_space=pltpu.VMEM),
        scratch_shapes=[pltpu.SemaphoreType.DMA((7,)), pltpu.SemaphoreType.DMA((7,)), pltpu.SemaphoreType.DMA],
        compiler_params=pltpu.CompilerParams(vmem_limit_bytes=VMEM_LIMIT),
    )(block)


def _sum_slots(slots, *, name):
    n, m, c = slots.shape
    t = _rows_tile(m, c * n)

    def body(s_ref, o_ref):
        acc = s_ref[0]
        for k in range(1, n):
            acc = acc + s_ref[k]
        o_ref[...] = acc

    return pl.pallas_call(
        body, name=name, grid=(m // t,), in_specs=[pl.BlockSpec((n, t, c), lambda i: (0, i, 0))],
        out_specs=pl.BlockSpec((t, c), lambda i: (i, 0)), out_shape=jax.ShapeDtypeStruct((m, c), F32),
        compiler_params=_cparams("parallel"),
    )(slots)


def _pad_rows(a, rows):
    return a if a.shape[0] == rows else jnp.pad(a, ((0, rows - a.shape[0]), (0, 0)))


def _w_in_padded(shards):
    full = shards.reshape(IN_COLS, shards.shape[2])
    return jnp.concatenate([_pad_rows(full[SEG[n][2]:SEG[n][2] + SEG[n][3]], SEG[n][1]) for n in SEG_ORDER], axis=0)


def _w_in_unpadded(gp):
    full = jnp.concatenate([gp[SEG[n][0]:SEG[n][0] + SEG[n][3]] for n in ORIG_ORDER], axis=0)
    return full.reshape(N_CHIP, IN_COLS // N_CHIP, gp.shape[1])


def _pad_heads(w, true_w, pad_w):
    r = w.shape[0]
    h = w.shape[1] // true_w
    return jnp.pad(w.reshape(r, h, true_w), ((0, 0), (0, 0), (0, pad_w - true_w))).reshape(r, h * pad_w)


def _unpad_heads(w, true_w, pad_w):
    r = w.shape[0]
    h = w.shape[1] // pad_w
    return w.reshape(r, h, pad_w)[:, :, :true_w].reshape(r, h * true_w)


def _cols_to_slots(a):
    return a.reshape(a.shape[0], N_CHIP, a.shape[1] // N_CHIP).transpose(1, 0, 2)


def _slots_to_cols(a):
    return jnp.concatenate([a[j] for j in range(N_CHIP)], axis=1)


def _to_heads(a, h, d):
    return a.reshape(a.shape[0], h, d).transpose(1, 0, 2)


def _from_heads(a):
    return a.transpose(1, 0, 2).reshape(a.shape[1], -1)


SMALL = [("norm_g", 2048), ("ret_norm_g", 512), ("gla_ba_f", 256), ("gla_ba_b", 256), ("gla_norm_g", 512),
         ("pool_w", 4 * 128 * 128), ("pool_scale", 512), ("mla_q_norm_g", 512), ("mla_kv_norm_g", 256),
         ("mla_qk_norm_q", 192), ("mla_qk_norm_k", 192)]


def _pack_small(vals):
    parts = []
    for name, n in SMALL:
        parts += [v.reshape(-1) for v in vals[name]]
        if (DEPTH * n) % 1024:
            parts.append(jnp.zeros((-(DEPTH * n)) % 1024, F32))
    parts += [vals["loss"].reshape(-1), jnp.zeros(1023, F32)]
    return jnp.concatenate(parts).reshape(-1, 128)


def _unpack_small(block):
    flat = block.reshape(-1)
    out, off = {}, 0
    for name, n in SMALL:
        out[name] = flat[off:off + DEPTH * n]
        off += DEPTH * n + (-(DEPTH * n)) % 1024
    out["loss"] = flat[off]
    return out


def _layer_weights(l, p, g):
    wa = jnp.zeros((128, 512), F32)
    wa = wa.at[0:GLA_RANK, 0:256].set(_slots_to_cols(g["gla_wa2_f"]))
    wa = wa.at[GLA_RANK:2 * GLA_RANK, 256:512].set(_slots_to_cols(g["gla_wa2_b"]))
    return dict(
        norm_g=p["norm_g"][l][None, :],
        w_in=_w_in_padded(g["w_in"]),
        w_out=g["w_out"].reshape(4 * g["w_out"].shape[1], -1),
        ret_norm_g=p["ret_norm_g"][l][None, :],
        wa=_bf(wa),
        ba=jnp.concatenate([p["gla_ba_f"][l], p["gla_ba_b"][l]])[None, :],
        gla_norm_g=p["gla_norm_g"][l][None, :],
        pool_w=_bf(p["pool_w"][l]),
        pool_scale=p["pool_scale"][l][None, :],
        qg=p["mla_q_norm_g"][l][None, :],
        wq=_pad_heads(_slots_to_cols(g["mla_wq_b"]), MLA_QK, MLA_QKP),
        kvg=p["mla_kv_norm_g"][l][None, :],
        wkv=_slots_to_cols(g["mla_wkv_b"]),
        qng=jnp.pad(p["mla_qk_norm_q"][l], (0, MLA_QKP - MLA_QK))[None, :],
        kng=jnp.pad(p["mla_qk_norm_k"][l], (0, MLA_QKP - MLA_QK))[None, :],
    )


def _layer_fwd(l, x, w, tabs, next_shards=None):
    ret_cos, ret_sin, mla_cos, mla_sp, mla_sn = tabs
    nm = lambda s: f"l{l}_{s}"
    h = _rmsnorm_fwd(x, w["norm_g"], name=nm("norm"))
    if next_shards is None:
        z = _matmul(h, w["w_in"], tb=True, name=nm("in_proj"))
    else:
        z, landed = _matmul(h, w["w_in"], tb=True, rider=_rider_gather_send(next_shards, SHARD_AXES),
                            name=nm("in_proj"))
    qr, kr = _ret_pre(z, ret_cos, ret_sin, name=nm("ret_pre"))
    ret_o = _bla(qr, kr, z, _ret_log_gamma(False), (0, 0, SEG["rv"][0] // 512), name=nm("ret_scan"))
    y_a = _post(ret_o, z, SEG["rg"][0] // 512, w["ret_norm_g"], norm=True, name=nm("ret_post"))
    la = _gla_gate(z, w["wa"], w["ba"], name=nm("gla_gate"))
    la_h = la.reshape(la.shape[0], 2, GLA_HEADS, GLA_DK).transpose(1, 2, 0, 3)
    gq = _to_heads(z[:, SEG["gq"][0]:SEG["gq"][0] + 256], GLA_HEADS, GLA_DK)
    gk = _to_heads(z[:, SEG["gk"][0]:SEG["gk"][0] + 256], GLA_HEADS, GLA_DK)
    gla_o, gla_st = _gla_fwd(gq, gk, z, la_h, name=nm("gla_scan"))
    y_b = _post(gla_o, z, SEG["gg"][0] // 512, w["gla_norm_g"], norm=True, name=nm("gla_post"))
    y_c = _pool_fwd(z, w["pool_w"], w["pool_scale"], name=nm("pool"))
    q, k, v = _mla_pre(z, w["qg"], w["wq"], w["kvg"], w["wkv"], w["qng"], w["kng"], mla_cos, mla_sp, mla_sn,
                       name=nm("mla_pre"))
    if next_shards is None:
        (att_o, lse), gathered = _flash_fwd(q, k, v, name=nm("attn")), None
    else:
        att_o, lse, gathered = _flash_fwd(q, k, v, rider=_rider_gather_forward(landed, SHARD_AXES), name=nm("attn"))
    y_d = _post([att_o], z, SEG["mg"][0] // 512, w["qg"], norm=False, name=nm("mla_post"))
    y = jnp.concatenate([y_a, y_b, y_c, y_d], axis=1)
    x_next = _matmul(y, w["w_out"], add=x, name=nm("out_proj"))
    saved = dict(x=x, h=h, z=z, y=y, qr=qr, kr=kr, ret_o=ret_o, la_h=la_h, gq=gq, gk=gk, gla_o=gla_o, gla_st=gla_st,
                 q=q, k=k, v=v, att_o=att_o, lse=lse)
    return x_next, saved, gathered


def _layer_bwd(l, dx_next, w, sv, tabs, riding_parts=None):
    ret_cos, ret_sin, mla_cos, mla_sp, mla_sn = tabs
    nm = lambda s: f"l{l}_{s}"
    z = sv["z"]
    dy = _matmul(dx_next, w["w_out"], tb=True, name=nm("out_proj_dy"))
    d_w_out = _matmul(sv["y"].T, dx_next, tn=512, name=nm("out_proj_dw"))
    d_rg, d_ret_o, d_ret_g = _post_bwd(dy, 0, sv["ret_o"], z, SEG["rg"][0] // 512, w["ret_norm_g"], norm=True,
                                       name=nm("ret_post_bwd"))
    vcol = SEG["rv"][0] // 512
    dqr = _bla(d_ret_o, z, sv["kr"], _ret_log_gamma(False), (0, vcol, 0), name=nm("ret_scan_dq"))
    dkr = _bla(z, d_ret_o, sv["qr"], _ret_log_gamma(True), (vcol, 0, 0), name=nm("ret_scan_dk"))
    drv = _bla(sv["kr"], sv["qr"], d_ret_o, _ret_log_gamma(True), (0, 0, 0), name=nm("ret_scan_dv"))
    d_rq, d_rk = _ret_pre_bwd(dqr, dkr, ret_cos, ret_sin, name=nm("ret_pre_bwd"))
    d_rv = _add_n([drv[0], drv[1]], out_dtype=BF16, name=nm("ret_dv_sum"))
    d_gg, d_gla_o, d_gla_g = _post_bwd(dy, 1, sv["gla_o"], z, SEG["gg"][0] // 512, w["gla_norm_g"], norm=True,
                                       name=nm("gla_post_bwd"))
    dq2, dk2, dla2, dv2 = _gla_bwd(sv["gq"], sv["gk"], z, sv["la_h"], d_gla_o, sv["gla_st"], name=nm("gla_scan_bwd"))
    d_gq = _bf(_from_heads(dq2[0] + dq2[1]))
    d_gk = _bf(_from_heads(dk2[0] + dk2[1]))
    d_gv = _add_n([dv2[0], dv2[1]], out_dtype=BF16, name=nm("gla_dv_sum"))
    dla = jnp.concatenate([_from_heads(dla2[0]), _from_heads(dla2[1])], axis=1)
    d_ga, d_wa, d_ba = _gla_gate_bwd(dla, z, w["wa"], w["ba"], name=nm("gla_gate_bwd"))
    d_pv, d_pg, d_pool_w, d_pool_scale = _pool_bwd(dy, z, w["pool_w"], w["pool_scale"], name=nm("pool_bwd"))
    d_mg, d_att_o, _ = _post_bwd(dy, 3, [sv["att_o"]], z, SEG["mg"][0] // 512, w["qg"], norm=False,
                                 name=nm("mla_post_bwd"))
    if riding_parts is None:
        (dq, dk, dv), rode = _flash_bwd(sv["q"], sv["k"], sv["v"], d_att_o, sv["att_o"], sv["lse"],
                                        name=nm("attn_bwd")), None
    else:
        dq, dk, dv, rode = _flash_bwd(sv["q"], sv["k"], sv["v"], d_att_o, sv["att_o"], sv["lse"],
                                      rider=_rider_chip_exchange(riding_parts), name=nm("attn_bwd"))
    d_mq, d_mkv, d_mkr, d_wq, d_wkv, d_qg, d_kvg, d_qng, d_kng = _mla_pre_bwd(
        dq, dk, dv, z, w["qg"], w["wq"], w["kvg"], w["wkv"], w["qng"], w["kng"], mla_cos, mla_sp, mla_sn,
        name=nm("mla_pre_bwd"))
    segs = dict(rq=d_rq, rk=d_rk, rv=d_rv, rg=d_rg, gv=d_gv, gg=d_gg, pv=d_pv, pg=d_pg, mq=d_mq, mg=d_mg,
                gq=d_gq, gk=d_gk, mkv=d_mkv, ga=d_ga, mkr=d_mkr)
    dz = jnp.concatenate([segs[n] for n in SEG_ORDER], axis=1)
    dh = _matmul(dz, w["w_in"], tn=512, name=nm("in_proj_dh"))
    d_w_in = _matmul(dz.T, sv["h"], name=nm("in_proj_dw"))
    dx, d_norm_g = _rmsnorm_bwd(sv["x"], dh, w["norm_g"], dx_next, name=nm("norm_bwd"))
    sharded = dict(
        w_in=_w_in_unpadded(d_w_in),
        w_out=d_w_out.reshape(N_CHIP, d_w_out.shape[0] // N_CHIP, d_w_out.shape[1]),
        mla_wq_b=_cols_to_slots(_unpad_heads(d_wq, MLA_QK, MLA_QKP)),
        mla_wkv_b=_cols_to_slots(d_wkv),
        gla_wa2_f=_cols_to_slots(d_wa[0:GLA_RANK, 0:256]),
        gla_wa2_b=_cols_to_slots(d_wa[GLA_RANK:2 * GLA_RANK, 256:512]),
    )
    small = dict(
        norm_g=d_norm_g[0], ret_norm_g=d_ret_g[0], gla_ba_f=d_ba[0, :256], gla_ba_b=d_ba[0, 256:],
        gla_norm_g=d_gla_g[0], pool_w=d_pool_w.reshape(-1), pool_scale=d_pool_scale[0], mla_q_norm_g=d_qg[0],
        mla_kv_norm_g=d_kvg[0], mla_qk_norm_q=d_qng[0, :MLA_QK], mla_qk_norm_k=d_kng[0, :MLA_QK],
    )
    return dx, sharded, small, rode


SHARDED = ["w_in", "w_out", "mla_wq_b", "mla_wkv_b", "gla_wa2_f", "gla_wa2_b"]
WEIGHTS = ["norm_g", "w_in", "ret_norm_g", "gla_wa2_f", "gla_ba_f", "gla_wa2_b", "gla_ba_b", "gla_norm_g", "pool_w",
           "pool_scale", "mla_q_norm_g", "mla_wq_b", "mla_kv_norm_g", "mla_wkv_b", "mla_qk_norm_q", "mla_qk_norm_k",
           "w_out"]


SHARD_AXES = [1, 0, 0, 0, 0, 0]


def _layer_shards(p, l):
    return [jnp.swapaxes(p["w_in"], 1, 2)[l].astype(BF16), p["w_out"][l].astype(BF16), p["mla_wq_b"][l].astype(BF16),
            p["mla_wkv_b"][l].astype(BF16), p["gla_wa2_f"][l], p["gla_wa2_b"][l]]


def _step(p, where):
    x = p["x"][0]
    tabs = _rope_tables(x.shape[0])
    got0 = _gather_shards(_layer_shards(p, 0), SHARD_AXES, name="l0_gather_weights")
    w0 = _layer_weights(0, p, dict(zip(SHARDED, got0)))
    x1, sv0, got1 = _layer_fwd(0, x, w0, tabs, next_shards=_layer_shards(p, 1))
    w1 = _layer_weights(1, p, dict(zip(SHARDED, got1)))
    x2, sv1, _ = _layer_fwd(1, x1, w1, tabs)
    dx, loss = _loss_head(x2, p["loss_target"][0], name="loss_head")

    big, big_axes = SHARDED[:2], SHARD_AXES[:2]

    def pair_sums(tag, tensors, axes, names):
        return [_pair_reduce(a, where, ax, out_dtype=BF16, name=f"{tag}_pair_reduce_{n}")
                for a, ax, n in zip(tensors, axes, names)]

    def joined(tag, pair, others, axes, names):
        return [_sum_join(a, b, where, ax, name=f"{tag}_sum_join_{n}")
                for a, b, ax, n in zip(pair, others, axes, names)]

    dx, sharded1, small1, _ = _layer_bwd(1, dx, w1, sv1, tabs)
    pair1 = pair_sums("l1", [sharded1[n] for n in big], big_axes, big)
    dx, sharded0, small0, others1 = _layer_bwd(0, dx, w0, sv0, tabs, riding_parts=pair1)
    grads1 = joined("l1", pair1, others1, big_axes, big)
    packed = jnp.concatenate([sh[n].reshape(N_CHIP, -1, 128) for sh in (sharded0, sharded1) for n in SHARDED[2:]],
                             axis=1)
    pair0 = pair_sums("l0", [sharded0[n] for n in big] + [packed], big_axes + [0], big + ["rest"])
    grads0 = joined("l0", pair0, _chip_exchange(pair0, name="l0_chip_exchange"), big_axes + [0], big + ["rest"])
    grads = {n: jnp.stack([g0, g1]) for n, g0, g1 in zip(big, grads0, grads1)}
    rest, off = grads0[2], 0
    pieces = {n: [] for n in SHARDED[2:]}
    for sh in (sharded0, sharded1):
        for n in SHARDED[2:]:
            rows = sh[n].shape[1] * sh[n].shape[2] // 128
            pieces[n].append(rest[off:off + rows].reshape(sh[n].shape[1:]))
            off += rows
    grads.update({n: jnp.stack(v) for n, v in pieces.items()})
    small = {n: [small0[n], small1[n]] for n, _ in SMALL}
    small["loss"] = loss
    return dx[None], grads, small


def kernel(x, norm_g, w_in, ret_norm_g, gla_wa2_f, gla_ba_f, gla_wa2_b, gla_ba_b, gla_norm_g, pool_w, pool_scale, mla_q_norm_g, mla_wq_b, mla_kv_norm_g, mla_wkv_b, mla_qk_norm_q, mla_qk_norm_k, w_out, loss_target, m_norm_g, m_w_in, m_ret_norm_g, m_gla_wa2_f, m_gla_ba_f, m_gla_wa2_b, m_gla_ba_b, m_gla_norm_g, m_pool_w, m_pool_scale, m_mla_q_norm_g, m_mla_wq_b, m_mla_kv_norm_g, m_mla_wkv_b, m_mla_qk_norm_q, m_mla_qk_norm_k, m_w_out, v_norm_g, v_w_in, v_ret_norm_g, v_gla_wa2_f, v_gla_ba_f, v_gla_wa2_b, v_gla_ba_b, v_gla_norm_g, v_pool_w, v_pool_scale, v_mla_q_norm_g, v_mla_wq_b, v_mla_kv_norm_g, v_mla_wkv_b, v_mla_qk_norm_q, v_mla_qk_norm_k, v_w_out):
    p = dict(x=x, norm_g=norm_g, w_in=w_in, ret_norm_g=ret_norm_g, gla_wa2_f=gla_wa2_f, gla_ba_f=gla_ba_f,
             gla_wa2_b=gla_wa2_b, gla_ba_b=gla_ba_b, gla_norm_g=gla_norm_g, pool_w=pool_w, pool_scale=pool_scale,
             mla_q_norm_g=mla_q_norm_g, mla_wq_b=mla_wq_b, mla_kv_norm_g=mla_kv_norm_g, mla_wkv_b=mla_wkv_b,
             mla_qk_norm_q=mla_qk_norm_q, mla_qk_norm_k=mla_qk_norm_k, w_out=w_out, loss_target=loss_target)
    moments = dict(
        m=dict(norm_g=m_norm_g, w_in=m_w_in, ret_norm_g=m_ret_norm_g, gla_wa2_f=m_gla_wa2_f, gla_ba_f=m_gla_ba_f,
               gla_wa2_b=m_gla_wa2_b, gla_ba_b=m_gla_ba_b, gla_norm_g=m_gla_norm_g, pool_w=m_pool_w,
               pool_scale=m_pool_scale, mla_q_norm_g=m_mla_q_norm_g, mla_wq_b=m_mla_wq_b,
               mla_kv_norm_g=m_mla_kv_norm_g, mla_wkv_b=m_mla_wkv_b, mla_qk_norm_q=m_mla_qk_norm_q,
               mla_qk_norm_k=m_mla_qk_norm_k, w_out=m_w_out),
        v=dict(norm_g=v_norm_g, w_in=v_w_in, ret_norm_g=v_ret_norm_g, gla_wa2_f=v_gla_wa2_f, gla_ba_f=v_gla_ba_f,
               gla_wa2_b=v_gla_wa2_b, gla_ba_b=v_gla_ba_b, gla_norm_g=v_gla_norm_g, pool_w=v_pool_w,
               pool_scale=v_pool_scale, mla_q_norm_g=v_mla_q_norm_g, mla_wq_b=v_mla_wq_b,
               mla_kv_norm_g=v_mla_kv_norm_g, mla_wkv_b=v_mla_wkv_b, mla_qk_norm_q=v_mla_qk_norm_q,
               mla_qk_norm_k=v_mla_qk_norm_k, w_out=v_w_out))

    where = jnp.stack([lax.axis_index("c"), 2 * lax.axis_index("x") + lax.axis_index("y")]).astype(jnp.int32)
    grad_x, grads, small = _step(p, where)

    slots = _gather_all(_pack_small(small), name="gather_small")
    total = _unpack_small(_sum_slots(slots, name="sum_small"))
    for n, _ in SMALL:
        grads[n] = total[n].reshape(p[n].shape)
    loss = total["loss"]

    delta, new_m, new_v = {}, {}, {}
    for n in WEIGHTS:
        turn = (lambda a: jnp.swapaxes(a, 1, 2)) if n == "w_in" else (lambda a: a)
        outs = _adamw(turn(p[n]), grads[n], turn(moments["m"][n]), turn(moments["v"][n]), name=f"adamw_{n}")
        grads[n] = turn(grads[n])
        delta[n], new_m[n], new_v[n] = (turn(o) for o in outs)
    return (loss, grad_x, *[grads[n] for n in WEIGHTS], *[delta[n] for n in WEIGHTS],
            *[new_m[n] for n in WEIGHTS], *[new_v[n] for n in WEIGHTS])
```

```python
import functools
import math

import jax
import jax.numpy as jnp
from jax import lax
from jax.experimental import pallas as pl
from jax.experimental.pallas import tpu as pltpu

F32 = jnp.float32
BF16 = jnp.bfloat16
MESH = pl.DeviceIdType.MESH

EPS = 1e-6
ROPE_THETA = 10000.0
DEPTH = 2
N_DEV = 8
N_CHIP = 4

GROUP_W = 512
RET_HEADS = 4
RET_HD = 128
RET_CHUNK = 128
GLA_HEADS = 4
GLA_DK = 64
GLA_DV = 128
GLA_RANK = 16
GLA_TAU = 16.0
GLA_CHUNK = 64
POOL_GROUPS = 4
POOL_GW = 128
POOL_HALO = 8
POOL_TILE = 256
MLA_HEADS = 4
MLA_NOPE = 128
MLA_ROPE = 64
MLA_QK = MLA_NOPE + MLA_ROPE
MLA_QKP = 256
MLA_V = 128
MLA_Q_RANK = 512
MLA_KV_RANK = 256
MLA_SCALE = MLA_QK ** -0.5
FLASH_STRIP = 1024

ADAM_LR = 0.001
ADAM_B1 = 0.9
ADAM_B2 = 0.999
ADAM_EPS = 1e-08
ADAM_WD = 0.01
ADAM_STEP = 10

VMEM_LIMIT = 56 * 1024 * 1024

SEG = {
    "rq": (0, 512, 0, 512), "rk": (512, 512, 512, 512), "rv": (1024, 512, 1024, 512), "rg": (1536, 512, 1536, 512),
    "gv": (2048, 512, 2560, 512), "gg": (2560, 512, 3072, 512),
    "pv": (3072, 512, 3616, 512), "pg": (3584, 512, 4128, 512),
    "mq": (4096, 512, 4640, 512), "mg": (4608, 512, 5472, 512),
    "gq": (5120, 256, 2048, 256), "gk": (5376, 256, 2304, 256), "mkv": (5632, 256, 5152, 256),
    "ga": (5888, 128, 3584, 32), "mkr": (6016, 128, 5408, 64),
}
SEG_ORDER = ["rq", "rk", "rv", "rg", "gv", "gg", "pv", "pg", "mq", "mg", "gq", "gk", "mkv", "ga", "mkr"]
IN_COLS = 5984
IN_PAD = 6144
ORIG_ORDER = ["rq", "rk", "rv", "rg", "gq", "gk", "gv", "gg", "ga", "pv", "pg", "mq", "mkv", "mkr", "mg"]


def _cparams(*sem):
    return pltpu.CompilerParams(dimension_semantics=tuple(sem), vmem_limit_bytes=VMEM_LIMIT)


def _bf(v):
    return v.astype(BF16)


def _dot(a, b, ca=1, cb=0):
    return lax.dot_general(_bf(a), _bf(b), (((ca,), (cb,)), ((), ())), preferred_element_type=F32)


def _split_dot(a01, x, ca=1, cb=0):
    hi = _bf(x)
    r1 = x - hi.astype(F32)
    mid = _bf(r1)
    lo = _bf(r1 - mid.astype(F32))
    dn = (((ca,), (cb,)), ((), ()))
    a = _bf(a01)
    return (lax.dot_general(a, hi, dn, preferred_element_type=F32)
            + lax.dot_general(a, mid, dn, preferred_element_type=F32)
            + lax.dot_general(a, lo, dn, preferred_element_type=F32))


def _sigmoid(x):
    return 1.0 / (1.0 + jnp.exp(-x))


def _silu_parts(g):
    sg = _sigmoid(g)
    return g * sg, sg * (1.0 + g * (1.0 - sg))


class _Rider:
    def __init__(self, ins, outs, sems, start, finish, aliases=None):
        self.ins, self.outs, self.sems, self.start, self.finish = list(ins), list(outs), list(sems), start, finish
        self.aliases = dict(aliases or {})


def _ride(body, rider, n_in, n_out, grid):
    if rider is None:
        return body
    ri, ro, rs = len(rider.ins), len(rider.outs), len(rider.sems)

    def wrapped(*refs):
        ins, refs = refs[:n_in], refs[n_in:]
        rin, refs = refs[:ri], refs[ri:]
        outs, refs = refs[:n_out], refs[n_out:]
        rout, refs = refs[:ro], refs[ro:]
        scratch, sems = refs[:len(refs) - rs], refs[len(refs) - rs:]
        first = pl.program_id(0) == 0
        last = pl.program_id(0) == grid[0] - 1
        for ax in range(1, len(grid)):
            first = jnp.logical_and(first, pl.program_id(ax) == 0)
            last = jnp.logical_and(last, pl.program_id(ax) == grid[ax] - 1)

        @pl.when(first)
        def _():
            rider.start(rin, rout, sems)

        body(*ins, *outs, *scratch)

        @pl.when(last)
        def _():
            rider.finish(rin, rout, sems)

    return wrapped


def _ride_call(body, rider, *, name, grid, in_specs, out_specs, out_shape, scratch_shapes, args, sem):
    n_in, n_out = len(in_specs), len(out_specs)
    if rider is None:
        return pl.pallas_call(body, name=name, grid=grid, in_specs=in_specs, out_specs=out_specs, out_shape=out_shape,
                              scratch_shapes=scratch_shapes, compiler_params=_cparams(*sem))(*args), []
    outs = pl.pallas_call(
        _ride(body, rider, n_in, n_out, grid), name=name, grid=grid,
        in_specs=list(in_specs) + [ANY] * len(rider.ins), out_specs=list(out_specs) + [ANY] * len(rider.outs),
        out_shape=list(out_shape) + rider.outs, scratch_shapes=list(scratch_shapes) + rider.sems,
        input_output_aliases={n_in + i: n_out + o for i, o in rider.aliases.items()},
        compiler_params=_cparams(*(["arbitrary"] * len(grid))),
    )(*args, *rider.ins)
    return outs[:n_out], outs[n_out:]


def _matmul(a, b, *, ta=False, tb=False, out_dtype=F32, tm=512, tn=1024, tk=None, add=None, n_outer=True, rider=None,
            name):
    m, kdim = (a.shape[1], a.shape[0]) if ta else a.shape
    n = b.shape[0] if tb else b.shape[1]
    tm, tn = min(tm, m), min(tn, n)
    tk = kdim if tk is None else min(tk, kdim)
    assert m % tm == 0 and n % tn == 0 and kdim % tk == 0
    nk = kdim // tk
    ca, cb = (0 if ta else 1), (1 if tb else 0)

    def body(*refs):
        if add is None:
            a_ref, b_ref, o_ref = refs[:3]
            add_ref = None
        else:
            a_ref, b_ref, add_ref, o_ref = refs[:4]
        p = _dot(a_ref[...], b_ref[...], ca, cb)

        def finish(r):
            if add_ref is not None:
                r = r + add_ref[...]
            o_ref[...] = r.astype(out_dtype)

        if nk == 1:
            finish(p)
        else:
            acc = refs[-1]
            k = pl.program_id(2)

            @pl.when(k == 0)
            def _():
                acc[...] = p

            @pl.when(k > 0)
            def _():
                acc[...] += p

            @pl.when(k == nk - 1)
            def _():
                finish(acc[...])

    def ij(g0, g1):
        return (g1, g0) if n_outer else (g0, g1)

    a_spec = (pl.BlockSpec((tk, tm), lambda g0, g1, k: (k, ij(g0, g1)[0])) if ta
              else pl.BlockSpec((tm, tk), lambda g0, g1, k: (ij(g0, g1)[0], k)))
    b_spec = (pl.BlockSpec((tn, tk), lambda g0, g1, k: (ij(g0, g1)[1], k)) if tb
              else pl.BlockSpec((tk, tn), lambda g0, g1, k: (k, ij(g0, g1)[1])))
    o_spec = pl.BlockSpec((tm, tn), lambda g0, g1, k: ij(g0, g1))
    in_specs = [a_spec, b_spec] + ([o_spec] if add is not None else [])
    args = (a, b) + ((add,) if add is not None else ())
    grid = (n // tn, m // tm, nk) if n_outer else (m // tm, n // tn, nk)
    (out,), rode = _ride_call(
        body, rider, name=name, grid=grid, in_specs=in_specs, out_specs=[o_spec],
        out_shape=[jax.ShapeDtypeStruct((m, n), out_dtype)],
        scratch_shapes=[] if nk == 1 else [pltpu.VMEM((tm, tn), F32)], args=args,
        sem=("parallel", "parallel", "arbitrary"))
    return out if rider is None else (out, rode)


def _rmsnorm_fwd(x, g, *, name, tm=256):
    s, d = x.shape
    tm = min(tm, s)

    def body(x_ref, g_ref, h_ref):
        xv = x_ref[...]
        r = lax.rsqrt(jnp.mean(xv * xv, axis=-1, keepdims=True) + EPS)
        h_ref[...] = _bf(xv * r * g_ref[...])

    return pl.pallas_call(
        body, name=name, grid=(s // tm,),
        in_specs=[pl.BlockSpec((tm, d), lambda i: (i, 0)), pl.BlockSpec((1, d), lambda i: (0, 0))],
        out_specs=pl.BlockSpec((tm, d), lambda i: (i, 0)),
        out_shape=jax.ShapeDtypeStruct((s, d), BF16),
        compiler_params=_cparams("parallel"),
    )(x, g)


def _rmsnorm_bwd(x, dh, g, dres, *, name, tm=256):
    s, d = x.shape
    tm = min(tm, s)

    def body(x_ref, dh_ref, g_ref, dres_ref, dx_ref, dg_ref):
        i = pl.program_id(0)
        xv = x_ref[...]
        r = lax.rsqrt(jnp.mean(xv * xv, axis=-1, keepdims=True) + EPS)
        xn = xv * r
        dv = dh_ref[...]
        part = jnp.sum(dv * xn, axis=0, keepdims=True)

        @pl.when(i == 0)
        def _():
            dg_ref[...] = part

        @pl.when(i > 0)
        def _():
            dg_ref[...] += part

        dxn = dv * g_ref[...]
        dx_ref[...] = dres_ref[...] + r * (dxn - xn * jnp.mean(dxn * xn, axis=-1, keepdims=True))

    row = pl.BlockSpec((tm, d), lambda i: (i, 0))
    vec = pl.BlockSpec((1, d), lambda i: (0, 0))
    return pl.pallas_call(
        body, name=name, grid=(s // tm,), in_specs=[row, row, vec, row], out_specs=[row, vec],
        out_shape=[jax.ShapeDtypeStruct((s, d), F32), jax.ShapeDtypeStruct((1, d), F32)],
        compiler_params=_cparams("arbitrary"),
    )(x, dh, g, dres)


def _loss_head(xf, target, *, name, tm=256):
    s, d = xf.shape
    tm = min(tm, s)

    def body(x_ref, t_ref, dx_ref, l_ref):
        i = pl.program_id(0)
        e = x_ref[...] - t_ref[...]
        dx_ref[...] = e * (1.0 / d)
        rows = jnp.mean(e * e, axis=-1, keepdims=True)
        part = 0.5 * jnp.sum(rows, axis=0, keepdims=True)

        @pl.when(i == 0)
        def _():
            l_ref[...] = part

        @pl.when(i > 0)
        def _():
            l_ref[...] += part

    row = pl.BlockSpec((tm, d), lambda i: (i, 0))
    return pl.pallas_call(
        body, name=name, grid=(s // tm,), in_specs=[row, row],
        out_specs=[row, pl.BlockSpec((1, 1), lambda i: (0, 0))],
        out_shape=[jax.ShapeDtypeStruct((s, d), F32), jax.ShapeDtypeStruct((1, 1), F32)],
        compiler_params=_cparams("arbitrary"),
    )(xf, target)


def _rope_tables(s):
    pos = jnp.arange(s, dtype=F32)[:, None]
    inv_r = 1.0 / (ROPE_THETA ** (jnp.arange(0, RET_HD, 2, dtype=F32) / RET_HD))
    ang = pos * inv_r[None, :]
    ret_cos = jnp.concatenate([jnp.cos(ang), jnp.cos(ang)], axis=1)
    ret_sin = jnp.concatenate([-jnp.sin(ang), jnp.sin(ang)], axis=1)
    inv_m = 1.0 / (ROPE_THETA ** (jnp.arange(0, MLA_ROPE, 2, dtype=F32) / MLA_ROPE))
    am = pos * inv_m[None, :]
    z32, z64 = jnp.zeros((s, 32), F32), jnp.zeros((s, 64), F32)
    mla_cos = jnp.concatenate([jnp.cos(am), jnp.cos(am), z64], axis=1)
    mla_sp = jnp.concatenate([z32, jnp.sin(am), z64], axis=1)
    mla_sn = jnp.concatenate([-jnp.sin(am), z32, z64], axis=1)
    return ret_cos, ret_sin, mla_cos, mla_sp, mla_sn


def _rope128(x, c, sg):
    return x * c + pltpu.roll(x, 64, 1) * sg


def _unrope128(d, c, sg):
    return d * c + pltpu.roll(d * sg, 64, 1)


def _rope64(t, c, sp, sn):
    return t * c + pltpu.roll(t, 96, 1) * sn + pltpu.roll(t, 32, 1) * sp


def _unrope64(d, c, sp, sn):
    return d * c + pltpu.roll(d * sn, 32, 1) + pltpu.roll(d * sp, 96, 1)


def _ret_pre(z, cos, sin, *, name, tm=256):
    s = z.shape[0]
    tm = min(tm, s)
    scale = RET_HD ** -0.5

    def body(q_ref, k_ref, c_ref, s_ref, qo_ref, ko_ref):
        c, sg = c_ref[...], s_ref[...]
        for h in range(RET_HEADS):
            sl = slice(h * RET_HD, (h + 1) * RET_HD)
            qo_ref[:, sl] = _rope128(q_ref[:, sl], c, sg)
            ko_ref[:, sl] = _rope128(k_ref[:, sl], c, sg) * scale

    seg = lambda j: pl.BlockSpec((tm, GROUP_W), lambda i: (i, j))
    tab = pl.BlockSpec((tm, RET_HD), lambda i: (i, 0))
    return pl.pallas_call(
        body, name=name, grid=(s // tm,), in_specs=[seg(0), seg(1), tab, tab],
        out_specs=[seg(0), seg(0)],
        out_shape=[jax.ShapeDtypeStruct((s, GROUP_W), F32)] * 2,
        compiler_params=_cparams("parallel"),
    )(z, z, cos, sin)


def _ret_pre_bwd(dqr, dkr, cos, sin, *, name, tm=256):
    s = dqr[0].shape[0]
    tm = min(tm, s)
    scale = RET_HD ** -0.5

    def body(dq0_ref, dq1_ref, dk0_ref, dk1_ref, c_ref, s_ref, qo_ref, ko_ref):
        c, sg = c_ref[...], s_ref[...]
        for h in range(RET_HEADS):
            sl = slice(h * RET_HD, (h + 1) * RET_HD)
            qo_ref[:, sl] = _bf(_unrope128(dq0_ref[:, sl] + dq1_ref[:, sl], c, sg))
            ko_ref[:, sl] = _bf(_unrope128(dk0_ref[:, sl] + dk1_ref[:, sl], c, sg) * scale)

    row = pl.BlockSpec((tm, GROUP_W), lambda i: (i, 0))
    tab = pl.BlockSpec((tm, RET_HD), lambda i: (i, 0))
    return pl.pallas_call(
        body, name=name, grid=(s // tm,), in_specs=[row, row, row, row, tab, tab], out_specs=[row, row],
        out_shape=[jax.ShapeDtypeStruct((s, GROUP_W), BF16)] * 2,
        compiler_params=_cparams("parallel"),
    )(dqr[0], dqr[1], dkr[0], dkr[1], cos, sin)


def _bla(a, b, c, lg, cols, *, name):
    s = a.shape[0]
    ch = min(RET_CHUNK, s)
    n = s // ch
    hd = RET_HD

    def body(lg_ref, a0, b0, c0, a1, b1, c1, o0, o1, st):
        t = pl.program_id(0)

        @pl.when(t == 0)
        def _():
            st[...] = jnp.zeros_like(st)

        ii = lax.broadcasted_iota(jnp.int32, (ch, ch), 0)
        jj = lax.broadcasted_iota(jnp.int32, (ch, ch), 1)
        idx = lax.broadcasted_iota(jnp.int32, (ch, 1), 0).astype(F32)
        for d, (a_ref, b_ref, c_ref, o_ref) in enumerate(((a0, b0, c0, o0), (a1, b1, c1, o1))):
            diff = ((ii - jj) if d == 0 else (jj - ii)).astype(F32)
            keep = diff >= 0
            dpos = jnp.maximum(diff, 0.0)
            pq = (idx + 1.0) if d == 0 else (ch - idx)
            pk = (ch - 1.0 - idx) if d == 0 else idx
            for h in range(RET_HEADS):
                g = lg_ref[d, h]
                sl = slice(h * hd, (h + 1) * hd)
                av, bv, cv = a_ref[:, sl], b_ref[:, sl], c_ref[:, sl]
                sc = _dot(av, bv, 1, 1) * jnp.where(keep, jnp.exp(dpos * g), 0.0)
                stv = st[d, h]
                o_ref[:, sl] = _dot(sc, cv) + _dot(av * jnp.exp(pq * g), stv)
                st[d, h] = jnp.exp(ch * g) * stv + _dot(bv * jnp.exp(pk * g), cv, 0, 0)

    fwd = lambda j: pl.BlockSpec((ch, GROUP_W), lambda t: (t, j))
    bwd = lambda j: pl.BlockSpec((ch, GROUP_W), lambda t: (n - 1 - t, j))
    return pl.pallas_call(
        body, name=name, grid=(n,),
        in_specs=[pl.BlockSpec(memory_space=pltpu.SMEM), fwd(cols[0]), fwd(cols[1]), fwd(cols[2]),
                  bwd(cols[0]), bwd(cols[1]), bwd(cols[2])],
        out_specs=[fwd(0), bwd(0)],
        out_shape=[jax.ShapeDtypeStruct((s, GROUP_W), F32)] * 2,
        scratch_shapes=[pltpu.VMEM((2, RET_HEADS, hd, hd), F32)],
        compiler_params=_cparams("arbitrary"),
    )(lg, a, b, c, a, b, c)


def _post(os_, zg, gcol, g, *, norm, name, tm=256):
    s = zg.shape[0]
    tm = min(tm, s)
    nd = len(os_)

    def body(*refs):
        o_refs, (gt_ref, g_ref, y_ref) = refs[:nd], refs[nd:]
        silu, _ = _silu_parts(gt_ref[...])
        for h in range(4):
            sl = slice(h * 128, (h + 1) * 128)
            o = o_refs[0][:, sl]
            for k in range(1, nd):
                o = o + o_refs[k][:, sl]
            if norm:
                r = lax.rsqrt(jnp.mean(o * o, axis=-1, keepdims=True) + EPS)
                o = o * r * g_ref[:, sl]
            y_ref[:, sl] = _bf(silu[:, sl] * o)

    row = pl.BlockSpec((tm, GROUP_W), lambda i: (i, 0))
    return pl.pallas_call(
        body, name=name, grid=(s // tm,),
        in_specs=[row] * nd + [pl.BlockSpec((tm, GROUP_W), lambda i: (i, gcol)),
                               pl.BlockSpec((1, GROUP_W), lambda i: (0, 0))],
        out_specs=row,
        out_shape=jax.ShapeDtypeStruct((s, GROUP_W), BF16),
        compiler_params=_cparams("parallel"),
    )(*os_, zg, g)


def _post_bwd(dy, ycol, os_, zg, gcol, g, *, norm, name, tm=256):
    s = zg.shape[0]
    tm = min(tm, s)
    nd = len(os_)

    def body(*refs):
        dy_ref, o_refs = refs[0], refs[1:1 + nd]
        gt_ref, g_ref, dgt_ref, do_ref, dg_ref = refs[1 + nd:]
        i = pl.program_id(0)
        silu, dsilu = _silu_parts(gt_ref[...])
        dyv = dy_ref[...]
        parts = []
        for h in range(4):
            sl = slice(h * 128, (h + 1) * 128)
            o = o_refs[0][:, sl]
            for k in range(1, nd):
                o = o + o_refs[k][:, sl]
            dn = dyv[:, sl] * silu[:, sl]
            if norm:
                r = lax.rsqrt(jnp.mean(o * o, axis=-1, keepdims=True) + EPS)
                xn = o * r
                gh = g_ref[:, sl]
                dgt_ref[:, sl] = _bf(dyv[:, sl] * (xn * gh) * dsilu[:, sl])
                parts.append(jnp.sum(dn * xn, axis=0, keepdims=True))
                dxn = dn * gh
                do_ref[:, sl] = r * (dxn - xn * jnp.mean(dxn * xn, axis=-1, keepdims=True))
            else:
                dgt_ref[:, sl] = _bf(dyv[:, sl] * o * dsilu[:, sl])
                parts.append(jnp.zeros((1, 128), F32))
                do_ref[:, sl] = dn
        part = jnp.concatenate(parts, axis=1)

        @pl.when(i == 0)
        def _():
            dg_ref[...] = part

        @pl.when(i > 0)
        def _():
            dg_ref[...] += part

    row = pl.BlockSpec((tm, GROUP_W), lambda i: (i, 0))
    vec = pl.BlockSpec((1, GROUP_W), lambda i: (0, 0))
    return pl.pallas_call(
        body, name=name, grid=(s // tm,),
        in_specs=[pl.BlockSpec((tm, GROUP_W), lambda i: (i, ycol))] + [row] * nd
        + [pl.BlockSpec((tm, GROUP_W), lambda i: (i, gcol)), vec],
        out_specs=[row, row, vec],
        out_shape=[jax.ShapeDtypeStruct((s, GROUP_W), BF16), jax.ShapeDtypeStruct((s, GROUP_W), F32),
                   jax.ShapeDtypeStruct((1, GROUP_W), F32)],
        compiler_params=_cparams("arbitrary"),
    )(dy, *os_, zg, g)


def _ret_log_gamma(swap):
    gf = 1.0 - 2.0 ** (-5.0 - jnp.arange(RET_HEADS, dtype=F32))
    lf, lb = jnp.log(gf), jnp.log(gf[::-1])
    return jnp.stack([lb, lf] if swap else [lf, lb])


def _log_sigmoid(x):
    return jnp.minimum(x, 0.0) - jnp.log(1.0 + jnp.exp(-jnp.abs(x)))


def _gla_gate(z, wa, ba, *, name, tm=256):
    s = z.shape[0]
    tm = min(tm, s)
    col = SEG["ga"][0] // 128

    def body(ga_ref, wa_ref, ba_ref, la_ref):
        pre = _dot(ga_ref[...], wa_ref[...]) + ba_ref[...]
        la_ref[...] = _log_sigmoid(pre) / GLA_TAU

    return pl.pallas_call(
        body, name=name, grid=(s // tm,),
        in_specs=[pl.BlockSpec((tm, 128), lambda i: (i, col)), pl.BlockSpec((128, 512), lambda i: (0, 0)),
                  pl.BlockSpec((1, 512), lambda i: (0, 0))],
        out_specs=pl.BlockSpec((tm, 512), lambda i: (i, 0)),
        out_shape=jax.ShapeDtypeStruct((s, 512), F32),
        compiler_params=_cparams("parallel"),
    )(z, wa, ba)


def _gla_gate_bwd(dla, z, wa, ba, *, name, tm=256):
    s = z.shape[0]
    tm = min(tm, s)
    col = SEG["ga"][0] // 128

    def body(dla_ref, ga_ref, wa_ref, ba_ref, dga_ref, dwa_ref, dba_ref):
        i = pl.program_id(0)
        gav = ga_ref[...]
        pre = _dot(gav, wa_ref[...]) + ba_ref[...]
        dpre = dla_ref[...] * (1.0 - _sigmoid(pre)) * (1.0 / GLA_TAU)
        dga_ref[...] = _bf(_dot(dpre, wa_ref[...], 1, 1))
        pw = _dot(gav, dpre, 0, 0)
        pb = jnp.sum(dpre, axis=0, keepdims=True)

        @pl.when(i == 0)
        def _():
            dwa_ref[...] = pw
            dba_ref[...] = pb

        @pl.when(i > 0)
        def _():
            dwa_ref[...] += pw
            dba_ref[...] += pb

    return pl.pallas_call(
        body, name=name, grid=(s // tm,),
        in_specs=[pl.BlockSpec((tm, 512), lambda i: (i, 0)), pl.BlockSpec((tm, 128), lambda i: (i, col)),
                  pl.BlockSpec((128, 512), lambda i: (0, 0)), pl.BlockSpec((1, 512), lambda i: (0, 0))],
        out_specs=[pl.BlockSpec((tm, 128), lambda i: (i, 0)), pl.BlockSpec((128, 512), lambda i: (0, 0)),
                   pl.BlockSpec((1, 512), lambda i: (0, 0))],
        out_shape=[jax.ShapeDtypeStruct((s, 128), BF16), jax.ShapeDtypeStruct((128, 512), F32),
                   jax.ShapeDtypeStruct((1, 512), F32)],
        compiler_params=_cparams("arbitrary"),
    )(dla, z, wa, ba)


def _gla_masks(ch):
    ii = lax.broadcasted_iota(jnp.int32, (ch, ch), 0)
    tt = lax.broadcasted_iota(jnp.int32, (ch, ch), 1)
    return jnp.where(tt <= ii, 1.0, 0.0), jnp.where(tt >= ii, 1.0, 0.0)


def _running_sum(x, up):
    n = x.shape[0]
    rows = lax.broadcasted_iota(jnp.int32, x.shape, 0)
    k = 1
    while k < n:
        if up:
            x = x + jnp.where(rows < n - k, pltpu.roll(x, n - k, 0), 0.0)
        else:
            x = x + jnp.where(rows >= k, pltpu.roll(x, k, 0), 0.0)
        k *= 2
    return x


def _gla_chunk(d, tmat, qv, kv, lav, ch):
    c = _running_sum(lav, up=(d == 1))
    big_l = c[ch - 1:ch, :] if d == 0 else c[0:1, :]
    qt = qv * (GLA_DK ** -0.5) * jnp.exp(c)
    kt = kv * jnp.exp(-c)
    kh = kv * jnp.exp(big_l - c)
    return c, big_l, qt, kt, kh


def _gla_fwd(qh, kh_, z, la, *, name):
    s = z.shape[0]
    ch = min(GLA_CHUNK, s)
    n = s // ch
    vcol = SEG["gv"][0] // GROUP_W

    def body(q0, k0, v0, la0, q1, k1, v1, la1, o0, o1, zs0, zs1, st):
        t = pl.program_id(0)

        @pl.when(t == 0)
        def _():
            st[...] = jnp.zeros_like(st)

        masks = _gla_masks(ch)
        for d, (q_ref, k_ref, v_ref, la_ref, o_ref, zs_ref) in enumerate(
                ((q0, k0, v0, la0, o0, zs0), (q1, k1, v1, la1, o1, zs1))):
            for h in range(GLA_HEADS):
                c, big_l, qt, kt, kh = _gla_chunk(d, masks[d], q_ref[h], k_ref[h], la_ref[0, h], ch)
                vv = v_ref[:, h * GLA_DV:(h + 1) * GLA_DV]
                p = _dot(qt, kt, 1, 1) * masks[d]
                zst = st[d, h]
                o_ref[:, h * GLA_DV:(h + 1) * GLA_DV] = _dot(p, vv) + _dot(qt, zst, 1, 1)
                zs_ref[h, 0] = zst
                st[d, h] = zst * jnp.exp(big_l) + _dot(vv, kh, 0, 0)

    cidx = (lambda t: t), (lambda t: n - 1 - t)
    hs = lambda d: pl.BlockSpec((GLA_HEADS, ch, GLA_DK), lambda t: (0, cidx[d](t), 0))
    vs = lambda d: pl.BlockSpec((ch, GROUP_W), lambda t: (cidx[d](t), vcol))
    las = lambda d: pl.BlockSpec((1, GLA_HEADS, ch, GLA_DK), lambda t: (d, 0, cidx[d](t), 0))
    os_ = lambda d: pl.BlockSpec((ch, GROUP_W), lambda t: (cidx[d](t), 0))
    zss = lambda d: pl.BlockSpec((GLA_HEADS, 1, GLA_DV, GLA_DK), lambda t: (0, cidx[d](t), 0, 0))
    o0, o1, zs0, zs1 = pl.pallas_call(
        body, name=name, grid=(n,),
        in_specs=[hs(0), hs(0), vs(0), las(0), hs(1), hs(1), vs(1), las(1)],
        out_specs=[os_(0), os_(1), zss(0), zss(1)],
        out_shape=[jax.ShapeDtypeStruct((s, GROUP_W), F32)] * 2
        + [jax.ShapeDtypeStruct((GLA_HEADS, n, GLA_DV, GLA_DK), F32)] * 2,
        scratch_shapes=[pltpu.VMEM((2, GLA_HEADS, GLA_DV, GLA_DK), F32)],
        compiler_params=_cparams("arbitrary"),
    )(qh, kh_, z, la, qh, kh_, z, la)
    return (o0, o1), (zs0, zs1)


def _gla_bwd(qh, kh_, z, la, do, zs, *, name):
    s = z.shape[0]
    ch = min(GLA_CHUNK, s)
    n = s // ch
    vcol = SEG["gv"][0] // GROUP_W

    def body(q0, k0, v0, la0, do0, zs0, q1, k1, v1, la1, do1, zs1,
             dq0, dk0, dla0, dv0, dq1, dk1, dla1, dv1, gz):
        t = pl.program_id(0)

        @pl.when(t == 0)
        def _():
            gz[...] = jnp.zeros_like(gz)

        masks = _gla_masks(ch)
        rows = lax.broadcasted_iota(jnp.int32, (ch, 1), 0)
        for d, (q_ref, k_ref, v_ref, la_ref, do_ref, zs_ref, dq_ref, dk_ref, dla_ref, dv_ref) in enumerate(
                ((q0, k0, v0, la0, do0, zs0, dq0, dk0, dla0, dv0), (q1, k1, v1, la1, do1, zs1, dq1, dk1, dla1, dv1))):
            tmat = masks[d]
            end = ch - 1 if d == 0 else 0
            for h in range(GLA_HEADS):
                c, big_l, qt, kt, kh = _gla_chunk(d, tmat, q_ref[h], k_ref[h], la_ref[0, h], ch)
                vsl = slice(h * GLA_DV, (h + 1) * GLA_DV)
                vv, dov, zst, gzv = v_ref[:, vsl], do_ref[:, vsl], zs_ref[h, 0], gz[d, h]
                p = _dot(qt, kt, 1, 1) * tmat
                dp = _dot(dov, vv, 1, 1) * tmat
                dqt = _dot(dp, kt) + _dot(dov, zst)
                dkt = _dot(dp, qt, 0, 0)
                dkh = _dot(vv, gzv)
                dv_ref[:, vsl] = _dot(p, dov, 0, 0) + _dot(kh, gzv, 1, 1)
                dq_ref[h] = dqt * jnp.exp(c) * (GLA_DK ** -0.5)
                dk_ref[h] = dkt * jnp.exp(-c) + dkh * jnp.exp(big_l - c)
                e_l = jnp.exp(big_l)
                d_l = jnp.sum(dkh * kh, axis=0, keepdims=True) + e_l * jnp.sum(zst * gzv, axis=0, keepdims=True)
                dc = dqt * qt - dkt * kt - dkh * kh + jnp.where(rows == end, d_l, 0.0)
                dla_ref[h] = _running_sum(dc, up=(d == 0))
                gz[d, h] = gzv * e_l + _dot(dov, qt, 0, 0)

    cidx = (lambda t: n - 1 - t), (lambda t: t)
    hs = lambda d: pl.BlockSpec((GLA_HEADS, ch, GLA_DK), lambda t: (0, cidx[d](t), 0))
    vs = lambda d: pl.BlockSpec((ch, GROUP_W), lambda t: (cidx[d](t), vcol))
    las = lambda d: pl.BlockSpec((1, GLA_HEADS, ch, GLA_DK), lambda t: (d, 0, cidx[d](t), 0))
    row = lambda d: pl.BlockSpec((ch, GROUP_W), lambda t: (cidx[d](t), 0))
    zss = lambda d: pl.BlockSpec((GLA_HEADS, 1, GLA_DV, GLA_DK), lambda t: (0, cidx[d](t), 0, 0))
    hshape = jax.ShapeDtypeStruct((GLA_HEADS, s, GLA_DK), F32)
    wide = jax.ShapeDtypeStruct((s, GROUP_W), F32)
    outs = pl.pallas_call(
        body, name=name, grid=(n,),
        in_specs=[hs(0), hs(0), vs(0), las(0), row(0), zss(0), hs(1), hs(1), vs(1), las(1), row(1), zss(1)],
        out_specs=[hs(0), hs(0), hs(0), row(0), hs(1), hs(1), hs(1), row(1)],
        out_shape=[hshape, hshape, hshape, wide, hshape, hshape, hshape, wide],
        scratch_shapes=[pltpu.VMEM((2, GLA_HEADS, GLA_DV, GLA_DK), F32)],
        compiler_params=_cparams("arbitrary"),
    )(qh, kh_, z, la, do, zs[0], qh, kh_, z, la, do, zs[1])
    dq0, dk0, dla0, dv0, dq1, dk1, dla1, dv1 = outs
    return (dq0, dq1), (dk0, dk1), (dla0, dla1), (dv0, dv1)


def _band(lo, hi, rows, width):
    r = lax.broadcasted_iota(jnp.int32, (rows, width), 0)
    j = lax.broadcasted_iota(jnp.int32, (rows, width), 1)
    k = j - POOL_HALO - r
    return jnp.where((k >= lo) & (k <= hi), 1.0, 0.0)


def _pool_cnt(t0, half, rows, s):
    t = t0 + lax.broadcasted_iota(jnp.int32, (rows, 1), 0)
    return (jnp.minimum(t + half, s) - jnp.maximum(t - half, 0)).astype(F32)


def _pool_fwd(z, pw, scale, *, name):
    s = z.shape[0]
    tl = min(POOL_TILE, s)
    nt = s // tl
    ucol, gcol = SEG["pv"][0] // 128, SEG["pg"][0] // 128

    def body(u_ref, gt_ref, pw_ref, sc_ref, y_ref, pad):
        g = pl.program_id(0)
        half = jnp.left_shift(1, g)
        pad[0:POOL_HALO, :] = jnp.zeros((POOL_HALO, POOL_GW), F32)
        pad[POOL_HALO + s:POOL_HALO + s + POOL_HALO, :] = jnp.zeros((POOL_HALO, POOL_GW), F32)
        pad[POOL_HALO:POOL_HALO + s, :] = u_ref[...]
        band = _band(-half, half - 1, tl, tl + 2 * POOL_HALO)
        pwv, scv = pw_ref[0], sc_ref[...]

        def tile(i, carry):
            t0 = pl.multiple_of(i * tl, tl)
            win = pad[pl.ds(t0, tl + 2 * POOL_HALO), :]
            u = win[POOL_HALO:POOL_HALO + tl, :]
            pooled = _split_dot(band, win) / _pool_cnt(t0, half, tl, s) - u
            mixed = _dot(pooled, pwv)
            silu, _ = _silu_parts(gt_ref[pl.ds(t0, tl), :])
            y_ref[pl.ds(t0, tl), :] = _bf(silu * (mixed * scv))
            return carry

        lax.fori_loop(0, nt, tile, 0)

    return pl.pallas_call(
        body, name=name, grid=(POOL_GROUPS,),
        in_specs=[pl.BlockSpec((s, POOL_GW), lambda g: (0, ucol + g)),
                  pl.BlockSpec((s, POOL_GW), lambda g: (0, gcol + g)),
                  pl.BlockSpec((1, POOL_GW, POOL_GW), lambda g: (g, 0, 0)),
                  pl.BlockSpec((1, POOL_GW), lambda g: (0, g))],
        out_specs=pl.BlockSpec((s, POOL_GW), lambda g: (0, g)),
        out_shape=jax.ShapeDtypeStruct((s, GROUP_W), BF16),
        scratch_shapes=[pltpu.VMEM((s + 2 * POOL_HALO, POOL_GW), F32)],
        compiler_params=_cparams("parallel"),
    )(z, z, pw, scale)


def _pool_bwd(dy, z, pw, scale, *, name):
    s = z.shape[0]
    tl = min(POOL_TILE, s)
    nt = s // tl
    ucol, gcol, ycol = SEG["pv"][0] // 128, SEG["pg"][0] // 128, 2 * GROUP_W // 128

    def body(dy_ref, u_ref, gt_ref, pw_ref, sc_ref, du_ref, dgt_ref, dpw_ref, dsc_ref, pad, epad, dpo):
        g = pl.program_id(0)
        half = jnp.left_shift(1, g)
        zeros = jnp.zeros((POOL_HALO, POOL_GW), F32)
        for buf in (pad, epad):
            buf[0:POOL_HALO, :] = zeros
            buf[POOL_HALO + s:POOL_HALO + s + POOL_HALO, :] = zeros
        pad[POOL_HALO:POOL_HALO + s, :] = u_ref[...]
        band = _band(-half, half - 1, tl, tl + 2 * POOL_HALO)
        band_t = _band(1 - half, half, tl, tl + 2 * POOL_HALO)
        pwv, scv = pw_ref[0], sc_ref[...]
        dpw_ref[0] = jnp.zeros((POOL_GW, POOL_GW), F32)
        dsc_ref[...] = jnp.zeros((1, POOL_GW), F32)

        def tile(i, carry):
            t0 = pl.multiple_of(i * tl, tl)
            win = pad[pl.ds(t0, tl + 2 * POOL_HALO), :]
            u = win[POOL_HALO:POOL_HALO + tl, :]
            cnt = _pool_cnt(t0, half, tl, s)
            pooled = _split_dot(band, win) / cnt - u
            mixed = _dot(pooled, pwv)
            silu, dsilu = _silu_parts(gt_ref[pl.ds(t0, tl), :])
            dyv = dy_ref[pl.ds(t0, tl), :]
            dgt_ref[pl.ds(t0, tl), :] = _bf(dyv * (mixed * scv) * dsilu)
            dsc_ref[...] += jnp.sum(dyv * silu * mixed, axis=0, keepdims=True)
            dm = dyv * silu * scv
            dpw_ref[0] += _dot(pooled, dm, 0, 0)
            dpooled = _dot(dm, pwv, 1, 1)
            dpo[pl.ds(t0, tl), :] = dpooled
            epad[pl.ds(POOL_HALO + t0, tl), :] = dpooled / cnt
            return carry

        lax.fori_loop(0, nt, tile, 0)

        def tile2(i, carry):
            t0 = pl.multiple_of(i * tl, tl)
            ewin = epad[pl.ds(t0, tl + 2 * POOL_HALO), :]
            du_ref[pl.ds(t0, tl), :] = _bf(_split_dot(band_t, ewin) - dpo[pl.ds(t0, tl), :])
            return carry

        lax.fori_loop(0, nt, tile2, 0)

    col = lambda c0: pl.BlockSpec((s, POOL_GW), lambda g: (0, c0 + g))
    return pl.pallas_call(
        body, name=name, grid=(POOL_GROUPS,),
        in_specs=[col(ycol), col(ucol), col(gcol), pl.BlockSpec((1, POOL_GW, POOL_GW), lambda g: (g, 0, 0)),
                  pl.BlockSpec((1, POOL_GW), lambda g: (0, g))],
        out_specs=[col(0), col(0), pl.BlockSpec((1, POOL_GW, POOL_GW), lambda g: (g, 0, 0)),
                   pl.BlockSpec((1, POOL_GW), lambda g: (0, g))],
        out_shape=[jax.ShapeDtypeStruct((s, GROUP_W), BF16), jax.ShapeDtypeStruct((s, GROUP_W), BF16),
                   jax.ShapeDtypeStruct((POOL_GROUPS, POOL_GW, POOL_GW), F32),
                   jax.ShapeDtypeStruct((1, GROUP_W), F32)],
        scratch_shapes=[pltpu.VMEM((s + 2 * POOL_HALO, POOL_GW), F32), pltpu.VMEM((s + 2 * POOL_HALO, POOL_GW), F32),
                        pltpu.VMEM((s, POOL_GW), F32)],
        compiler_params=_cparams("parallel"),
    )(dy, z, z, pw, scale)


def _mla_specs(tm):
    zq = pl.BlockSpec((tm, 512), lambda i: (i, SEG["mq"][0] // 512))
    zkv = pl.BlockSpec((tm, 256), lambda i: (i, SEG["mkv"][0] // 256))
    zkr = pl.BlockSpec((tm, 128), lambda i: (i, SEG["mkr"][0] // 128))
    full = lambda r, c: pl.BlockSpec((r, c), lambda i: (0, 0))
    tab = pl.BlockSpec((tm, 128), lambda i: (i, 0))
    weights = [full(1, 512), full(512, 1024), full(1, 256), full(256, 1024), full(1, 256), full(1, 256)]
    return [zq, zkv, zkr] + weights + [tab, tab, tab]


def _mla_project(xq_ref, xkv_ref, qg_ref, wq_ref, kvg_ref, wkv_ref):
    xq = xq_ref[...]
    r1 = lax.rsqrt(jnp.mean(xq * xq, axis=-1, keepdims=True) + EPS)
    xn1 = xq * r1
    qn = _bf(xn1 * qg_ref[...])
    qraw = _dot(qn, wq_ref[...])
    xkv = xkv_ref[...]
    r2 = lax.rsqrt(jnp.mean(xkv * xkv, axis=-1, keepdims=True) + EPS)
    xn2 = xkv * r2
    kvn = _bf(xn2 * kvg_ref[...])
    kvraw = _dot(kvn, wkv_ref[...])
    return r1, xn1, qn, qraw, r2, xn2, kvn, kvraw


def _mla_pre(z, qg, wq, kvg, wkv, qng, kng, cos, sp, sn, *, name, tm=256):
    s = z.shape[0]
    tm = min(tm, s)

    def body(xq_ref, xkv_ref, pe_ref, qg_ref, wq_ref, kvg_ref, wkv_ref, qng_ref, kng_ref, c_ref, sp_ref, sn_ref,
             q_ref, k_ref, v_ref):
        _, _, _, qraw, _, _, _, kvraw = _mla_project(xq_ref, xkv_ref, qg_ref, wq_ref, kvg_ref, wkv_ref)
        c, spv, snv = c_ref[...], sp_ref[...], sn_ref[...]
        pe = pe_ref[...]
        pe_ss = jnp.sum(pe * pe, axis=-1, keepdims=True)
        qngv, kngv = qng_ref[...], kng_ref[...]
        for h in range(MLA_HEADS):
            b = h * MLA_QKP
            qh = qraw[:, b:b + MLA_QKP]
            r = lax.rsqrt(jnp.sum(qh * qh, axis=-1, keepdims=True) * (1.0 / MLA_QK) + EPS)
            qn_h = qh * r * qngv
            q_ref[:, b:b + 128] = _bf(qn_h[:, :128] * MLA_SCALE)
            q_ref[:, b + 128:b + 256] = _bf(_rope64(qn_h[:, 128:], c, spv, snv) * MLA_SCALE)
            kn = kvraw[:, b:b + 128]
            rk = lax.rsqrt((jnp.sum(kn * kn, axis=-1, keepdims=True) + pe_ss) * (1.0 / MLA_QK) + EPS)
            k_ref[:, b:b + 128] = _bf(kn * rk * kngv[:, :128])
            k_ref[:, b + 128:b + 256] = _bf(_rope64(pe * rk * kngv[:, 128:], c, spv, snv))
            v_ref[:, h * MLA_V:(h + 1) * MLA_V] = _bf(kvraw[:, b + 128:b + 256])

    row = lambda w: pl.BlockSpec((tm, w), lambda i: (i, 0))
    return pl.pallas_call(
        body, name=name, grid=(s // tm,), in_specs=_mla_specs(tm),
        out_specs=[row(1024), row(1024), row(512)],
        out_shape=[jax.ShapeDtypeStruct((s, 1024), BF16), jax.ShapeDtypeStruct((s, 1024), BF16),
                   jax.ShapeDtypeStruct((s, 512), BF16)],
        compiler_params=_cparams("parallel"),
    )(z, z, z, qg, wq, kvg, wkv, qng, kng, cos, sp, sn)


def _mla_pre_bwd(dq, dk, dv, z, qg, wq, kvg, wkv, qng, kng, cos, sp, sn, *, name, tm=256):
    s = z.shape[0]
    tm = min(tm, s)

    def body(dq_ref, dk_ref, dv_ref, xq_ref, xkv_ref, pe_ref, qg_ref, wq_ref, kvg_ref, wkv_ref, qng_ref, kng_ref,
             c_ref, sp_ref, sn_ref, dxq_ref, dxkv_ref, dpe_ref, dwq_ref, dwkv_ref, dqg_ref, dkvg_ref, dqng_ref,
             dkng_ref, dqraw, dkvraw):
        i = pl.program_id(0)
        r1, xn1, qn, qraw, r2, xn2, kvn, kvraw = _mla_project(xq_ref, xkv_ref, qg_ref, wq_ref, kvg_ref, wkv_ref)
        c, spv, snv = c_ref[...], sp_ref[...], sn_ref[...]
        pe = pe_ref[...]
        pe_ss = jnp.sum(pe * pe, axis=-1, keepdims=True)
        qngv, kngv = qng_ref[...], kng_ref[...]
        dqng = jnp.zeros((1, MLA_QKP), F32)
        dkng = jnp.zeros((1, MLA_QKP), F32)
        dpe = jnp.zeros_like(pe)
        for h in range(MLA_HEADS):
            b = h * MLA_QKP
            qh = qraw[:, b:b + MLA_QKP]
            r = lax.rsqrt(jnp.sum(qh * qh, axis=-1, keepdims=True) * (1.0 / MLA_QK) + EPS)
            xn = qh * r
            d_n = jnp.concatenate(
                [dq_ref[:, b:b + 128], _unrope64(dq_ref[:, b + 128:b + 256], c, spv, snv)], axis=1) * MLA_SCALE
            dqng = dqng + jnp.sum(d_n * xn, axis=0, keepdims=True)
            dxn = d_n * qngv
            dqraw[:, b:b + MLA_QKP] = _bf(r * (dxn - xn * (jnp.sum(dxn * xn, axis=-1, keepdims=True) * (1.0 / MLA_QK))))
            kn = kvraw[:, b:b + 128]
            rk = lax.rsqrt((jnp.sum(kn * kn, axis=-1, keepdims=True) + pe_ss) * (1.0 / MLA_QK) + EPS)
            xk = jnp.concatenate([kn, pe], axis=1) * rk
            d_k = jnp.concatenate(
                [dk_ref[:, b:b + 128], _unrope64(dk_ref[:, b + 128:b + 256], c, spv, snv)], axis=1)
            dkng = dkng + jnp.sum(d_k * xk, axis=0, keepdims=True)
            dxk = d_k * kngv
            dfull = rk * (dxk - xk * (jnp.sum(dxk * xk, axis=-1, keepdims=True) * (1.0 / MLA_QK)))
            dkvraw[:, b:b + 128] = _bf(dfull[:, :128])
            dkvraw[:, b + 128:b + 256] = _bf(dv_ref[:, h * MLA_V:(h + 1) * MLA_V])
            dpe = dpe + dfull[:, 128:]
        dpe_ref[...] = _bf(dpe)
        dqr, dkvr = dqraw[...], dkvraw[...]
        dqn = _dot(dqr, wq_ref[...], 1, 1)
        dxn1 = dqn * qg_ref[...]
        dxq_ref[...] = _bf(r1 * (dxn1 - xn1 * jnp.mean(dxn1 * xn1, axis=-1, keepdims=True)))
        dkvn = _dot(dkvr, wkv_ref[...], 1, 1)
        dxn2 = dkvn * kvg_ref[...]
        dxkv_ref[...] = _bf(r2 * (dxn2 - xn2 * jnp.mean(dxn2 * xn2, axis=-1, keepdims=True)))
        parts = (_dot(qn, dqr, 0, 0), _dot(kvn, dkvr, 0, 0), jnp.sum(dqn * xn1, axis=0, keepdims=True),
                 jnp.sum(dkvn * xn2, axis=0, keepdims=True), dqng, dkng)
        accs = (dwq_ref, dwkv_ref, dqg_ref, dkvg_ref, dqng_ref, dkng_ref)

        @pl.when(i == 0)
        def _():
            for a, p in zip(accs, parts):
                a[...] = p

        @pl.when(i > 0)
        def _():
            for a, p in zip(accs, parts):
                a[...] += p

    row = lambda w: pl.BlockSpec((tm, w), lambda i: (i, 0))
    full = lambda r, c: pl.BlockSpec((r, c), lambda i: (0, 0))
    return pl.pallas_call(
        body, name=name, grid=(s // tm,),
        in_specs=[row(1024), row(1024), row(512)] + _mla_specs(tm),
        out_specs=[row(512), row(256), row(128), full(512, 1024), full(256, 1024), full(1, 512), full(1, 256),
                   full(1, 256), full(1, 256)],
        out_shape=[jax.ShapeDtypeStruct((s, 512), BF16), jax.ShapeDtypeStruct((s, 256), BF16),
                   jax.ShapeDtypeStruct((s, 128), BF16), jax.ShapeDtypeStruct((512, 1024), F32),
                   jax.ShapeDtypeStruct((256, 1024), F32), jax.ShapeDtypeStruct((1, 512), F32),
                   jax.ShapeDtypeStruct((1, 256), F32), jax.ShapeDtypeStruct((1, 256), F32),
                   jax.ShapeDtypeStruct((1, 256), F32)],
        scratch_shapes=[pltpu.VMEM((tm, 1024), BF16), pltpu.VMEM((tm, 1024), BF16)],
        compiler_params=_cparams("arbitrary"),
    )(dq, dk, dv, z, z, z, qg, wq, kvg, wkv, qng, kng, cos, sp, sn)


def _flash_fwd(q, k, v, *, name, tq=1024, tk=1024, rider=None):
    s = q.shape[0]
    tq, tk = min(tq, s), min(tk, s)
    nk = s // tk
    strip = min(FLASH_STRIP, tq)

    def body(q_ref, k_ref, v_ref, o_ref, lse_ref, m_s, l_s, acc):
        j = pl.program_id(2)

        @pl.when(j == 0)
        def _():
            m_s[...] = jnp.full_like(m_s, -jnp.inf)
            l_s[...] = jnp.zeros_like(l_s)
            acc[...] = jnp.zeros_like(acc)

        for r in range(tq // strip):
            rows = slice(r * strip, (r + 1) * strip)
            sc = _dot(q_ref[rows, :], k_ref[...], 1, 1)
            m_prev = m_s[rows, :]
            m_new = jnp.maximum(m_prev, jnp.max(sc, axis=-1, keepdims=True))
            p = jnp.exp(sc - m_new[:, 0:1])
            alpha = jnp.exp(m_prev - m_new)
            l_s[rows, :] = alpha * l_s[rows, :] + jnp.sum(p, axis=-1, keepdims=True)
            acc[rows, :] = alpha * acc[rows, :] + _dot(p, v_ref[...])
            m_s[rows, :] = m_new

        @pl.when(j == nk - 1)
        def _():
            o_ref[...] = acc[...] / l_s[...]
            lse_ref[...] = m_s[...] + jnp.log(l_s[...])

    (o, lse), rode = _ride_call(
        body, rider, name=name, grid=(MLA_HEADS, s // tq, nk),
        in_specs=[pl.BlockSpec((tq, MLA_QKP), lambda h, i, j: (i, h)),
                  pl.BlockSpec((tk, MLA_QKP), lambda h, i, j: (j, h)),
                  pl.BlockSpec((tk, MLA_V), lambda h, i, j: (j, h))],
        out_specs=[pl.BlockSpec((tq, MLA_V), lambda h, i, j: (i, h))] * 2,
        out_shape=[jax.ShapeDtypeStruct((s, GROUP_W), F32)] * 2,
        scratch_shapes=[pltpu.VMEM((tq, MLA_V), F32), pltpu.VMEM((tq, MLA_V), F32), pltpu.VMEM((tq, MLA_V), F32)],
        args=(q, k, v), sem=("parallel", "parallel", "arbitrary"))
    return (o, lse) if rider is None else (o, lse, rode)


def _flash_bwd(q, k, v, do, o, lse, *, name, tq=1024, tk=1024, rider=None):
    s = q.shape[0]
    tq, tk = min(tq, s), min(tk, s)
    nq, nk = s // tq, s // tk

    def body(q_ref, k_ref, v_ref, do_ref, o_ref, lse_ref, dq_ref, dk_ref, dv_ref, dk_acc, dv_acc):
        j, i = pl.program_id(1), pl.program_id(2)
        dov = do_ref[...]
        delta = jnp.sum(dov * o_ref[...], axis=-1, keepdims=True)
        p = jnp.exp(_dot(q_ref[...], k_ref[...], 1, 1) - lse_ref[:, 0:1])
        ds = p * (_dot(dov, v_ref[...], 1, 1) - delta)
        pv = _dot(p, dov, 0, 0)
        pk = _dot(ds, q_ref[...], 0, 0)
        pq = _dot(ds, k_ref[...])
        rows = pl.ds(pl.multiple_of(i * tq, tq), tq)

        @pl.when(j == 0)
        def _():
            dq_ref[rows, :] = pq

        @pl.when(j > 0)
        def _():
            dq_ref[rows, :] += pq

        @pl.when(i == 0)
        def _():
            dv_acc[...] = pv
            dk_acc[...] = pk

        @pl.when(i > 0)
        def _():
            dv_acc[...] += pv
            dk_acc[...] += pk

        @pl.when(i == nq - 1)
        def _():
            dk_ref[...] = dk_acc[...]
            dv_ref[...] = dv_acc[...]

    qb = pl.BlockSpec((tq, MLA_QKP), lambda h, j, i: (i, h))
    kb = pl.BlockSpec((tk, MLA_QKP), lambda h, j, i: (j, h))
    vb = pl.BlockSpec((tk, MLA_V), lambda h, j, i: (j, h))
    ob = pl.BlockSpec((tq, MLA_V), lambda h, j, i: (i, h))
    (dq, dk, dv), rode = _ride_call(
        body, rider, name=name, grid=(MLA_HEADS, nk, nq),
        in_specs=[qb, kb, vb, ob, ob, ob],
        out_specs=[pl.BlockSpec((s, MLA_QKP), lambda h, j, i: (0, h)), kb, vb],
        out_shape=[jax.ShapeDtypeStruct((s, MLA_HEADS * MLA_QKP), F32),
                   jax.ShapeDtypeStruct((s, MLA_HEADS * MLA_QKP), F32), jax.ShapeDtypeStruct((s, GROUP_W), F32)],
        scratch_shapes=[pltpu.VMEM((tk, MLA_QKP), F32), pltpu.VMEM((tk, MLA_V), F32)],
        args=(q, k, v, do, o, lse), sem=("arbitrary", "arbitrary", "arbitrary"))
    return (dq, dk, dv) if rider is None else (dq, dk, dv, rode)


def _rows_tile(r, c, itemsize=4, budget=2 * 1024 * 1024):
    if r * c * itemsize <= budget:
        return r
    best = None
    for t in range(8, r, 8):
        if r % t == 0 and t * c * itemsize <= budget:
            best = t
    return best if best is not None else r


def _add_n(arrs, *, out_dtype=F32, name):
    shape = arrs[0].shape
    c = shape[-1]
    flat = [a.reshape(-1, c) for a in arrs]
    r = flat[0].shape[0]
    t = _rows_tile(r, c)

    def body(*refs):
        acc = refs[0][...].astype(F32)
        for ref in refs[1:-1]:
            acc = acc + ref[...].astype(F32)
        refs[-1][...] = acc.astype(out_dtype)

    blk = pl.BlockSpec((t, c), lambda i: (i, 0))
    out = pl.pallas_call(
        body, name=name, grid=(r // t,), in_specs=[blk] * len(flat), out_specs=blk,
        out_shape=jax.ShapeDtypeStruct((r, c), out_dtype), compiler_params=_cparams("parallel"),
    )(*flat)
    return out.reshape(shape)


def _adamw(w, g, m, v, *, name):
    shape = w.shape
    c = shape[-1]
    flat = [a.reshape(-1, c) for a in (w, g, m, v)]
    r = flat[0].shape[0]
    t = _rows_tile(r, c, budget=1024 * 1024)

    def body(w_ref, g_ref, m_ref, v_ref, d_ref, mo_ref, vo_ref):
        gv = g_ref[...]
        m2 = ADAM_B1 * m_ref[...] + (1.0 - ADAM_B1) * gv
        v2 = ADAM_B2 * v_ref[...] + (1.0 - ADAM_B2) * (gv * gv)
        m_hat = m2 / (1.0 - ADAM_B1 ** ADAM_STEP)
        v_hat = v2 / (1.0 - ADAM_B2 ** ADAM_STEP)
        d_ref[...] = -ADAM_LR * (m_hat / (jnp.sqrt(v_hat) + ADAM_EPS) + ADAM_WD * w_ref[...])
        mo_ref[...] = m2
        vo_ref[...] = v2

    blk = pl.BlockSpec((t, c), lambda i: (i, 0))
    outs = pl.pallas_call(
        body, name=name, grid=(r // t,), in_specs=[blk] * 4, out_specs=[blk] * 3,
        out_shape=[jax.ShapeDtypeStruct((r, c), F32)] * 3, compiler_params=_cparams("parallel"),
    )(*flat)
    return tuple(o.reshape(shape) for o in outs)


def _place():
    x, y, c = lax.axis_index("x"), lax.axis_index("y"), lax.axis_index("c")
    chips = [(1 - x, y), (x, 1 - y), (1 - x, 1 - y)]
    return x, y, c, chips


ANY = pl.BlockSpec(memory_space=pl.ANY)


def _half(ref, axis, hc, lead=()):
    n = ref.shape[len(lead) + axis] // 2
    return ref.at[tuple(lead) + (slice(None),) * axis + (pl.ds(hc * n, n),)]


def _gather_shards(shards, axes, *, name):
    nt = len(shards)

    def body(*refs):
        src, dst = refs[:nt], refs[nt:2 * nt]
        send, recv, fsend, frecv, lsem = refs[2 * nt:]
        x, y, c, chips = _place()
        me = 2 * x + y
        local = [pltpu.make_async_copy(src[t], dst[t].at[me], lsem.at[t]) for t in range(nt)]
        for cp in local:
            cp.start()

        def half(t, slot, hc):
            return _half(dst[t], axes[t], hc, lead=(slot,))

        def first(t, k):
            return pltpu.make_async_remote_copy(
                src_ref=_half(src[t], axes[t], c), dst_ref=half(t, me, c),
                send_sem=send.at[t, k], recv_sem=recv.at[t, k],
                device_id=(chips[k][0], chips[k][1], c), device_id_type=MESH)

        def landed(t, k):
            slot = 2 * chips[k][0] + chips[k][1]
            return pltpu.make_async_remote_copy(
                src_ref=half(t, slot, c), dst_ref=half(t, slot, c),
                send_sem=send.at[t, k], recv_sem=recv.at[t, k],
                device_id=(chips[k][0], chips[k][1], c), device_id_type=MESH)

        def forward(t, k, hc):
            slot = 2 * chips[k][0] + chips[k][1]
            return pltpu.make_async_remote_copy(
                src_ref=half(t, slot, hc), dst_ref=half(t, slot, hc),
                send_sem=fsend.at[t, k], recv_sem=frecv.at[t, k],
                device_id=(x, y, 1 - c), device_id_type=MESH)

        for t in range(nt):
            for k in range(3):
                first(t, k).start()
        for t in range(nt):
            for k in range(3):
                landed(t, k).wait_recv()
                forward(t, k, c).start()
        for t in range(nt):
            for k in range(3):
                forward(t, k, 1 - c).wait_recv()
        for t in range(nt):
            for k in range(3):
                first(t, k).wait_send()
                forward(t, k, c).wait_send()
        for cp in local:
            cp.wait()

    return pl.pallas_call(
        body, name=name, in_specs=[ANY] * nt, out_specs=[ANY] * nt,
        out_shape=[jax.ShapeDtypeStruct((N_CHIP,) + a.shape, a.dtype) for a in shards],
        scratch_shapes=[pltpu.SemaphoreType.DMA((nt, 3)), pltpu.SemaphoreType.DMA((nt, 3)),
                        pltpu.SemaphoreType.DMA((nt, 3)), pltpu.SemaphoreType.DMA((nt, 3)),
                        pltpu.SemaphoreType.DMA((nt,))],
    )(*shards)


def _comm_rows(hr, c, budget=2 * 1024 * 1024):
    if hr * c * 4 <= budget:
        return hr
    best = None
    for t in range(16, hr, 16):
        if hr % t == 0 and t * c * 4 <= budget:
            best = t
    return best if best is not None else hr


def _comm_cols(r, hc, budget=2 * 1024 * 1024):
    best = 128
    for t in range(128, hc + 1, 128):
        if hc % t == 0 and r * t * 4 <= budget:
            best = t
    return best


def _comm_chunks(shape, axis):
    r, cdim = shape
    if axis == 0:
        rc = _comm_rows(r // 2, cdim)
        nt = (r // 2) // rc
        return (rc, cdim), nt, (lambda h, t: (h * nt + t, 0))
    cc = _comm_cols(r, cdim // 2)
    nt = (cdim // 2) // cc
    return (r, cc), nt, (lambda h, t: (0, h * nt + t))


def _pair_reduce(g, where, axis, *, out_dtype, name):
    n_slot, r, cdim = g.shape
    blk_shape, nr, at = _comm_chunks((r, cdim), axis)
    steps = n_slot * nr
    half_shape = (r // 2, cdim) if axis == 0 else (r, cdim // 2)

    def body(w_ref, a_ref, b_ref, o_ref, land, send, recv, credit):
        x, y, c, _ = _place()
        sib = (x, y, 1 - c)
        i = pl.program_id(0) * nr + pl.program_id(1)
        s = lax.rem(i, 2)

        @pl.when(i >= 2)
        def _():
            pl.semaphore_wait(credit.at[s], 1)

        cp = pltpu.make_async_remote_copy(src_ref=b_ref.at[0], dst_ref=land.at[s], send_sem=send.at[s],
                                          recv_sem=recv.at[s], device_id=sib, device_id_type=MESH)
        cp.start()
        cp.wait_recv()
        o_ref[0] = (a_ref[0] + land[s]).astype(out_dtype)
        cp.wait_send()

        @pl.when(i + 2 < steps)
        def _():
            pl.semaphore_signal(credit.at[s], inc=1, device_id=sib, device_id_type=MESH)

    blk = lambda half: pl.BlockSpec((1,) + blk_shape, lambda j, t, w: (j,) + at(half(w), t))
    grid_spec = pltpu.PrefetchScalarGridSpec(
        num_scalar_prefetch=1, grid=(n_slot, nr),
        in_specs=[blk(lambda w: w[0]), blk(lambda w: 1 - w[0])],
        out_specs=pl.BlockSpec((1,) + blk_shape, lambda j, t, w: (j,) + at(0, t)),
        scratch_shapes=[pltpu.VMEM((2,) + blk_shape, F32), pltpu.SemaphoreType.DMA((2,)),
                        pltpu.SemaphoreType.DMA((2,)), pltpu.SemaphoreType.REGULAR((2,))])
    return pl.pallas_call(
        body, name=name, grid_spec=grid_spec, out_shape=jax.ShapeDtypeStruct((n_slot,) + half_shape, out_dtype),
        compiler_params=_cparams("arbitrary", "arbitrary"),
    )(where, g, g)


def _chip_exchange(parts, *, name):
    nt = len(parts)

    def body(*refs):
        src, got = refs[:nt], refs[nt:2 * nt]
        send, recv = refs[2 * nt:]
        x, y, c, chips = _place()
        remote = []
        for t in range(nt):
            for k in range(3):
                remote.append(pltpu.make_async_remote_copy(
                    src_ref=src[t].at[2 * chips[k][0] + chips[k][1]], dst_ref=got[t].at[k],
                    send_sem=send.at[t, k], recv_sem=recv.at[t, k],
                    device_id=(chips[k][0], chips[k][1], c), device_id_type=MESH))
        for cp in remote:
            cp.start()
        for cp in remote:
            cp.wait_recv()
        for cp in remote:
            cp.wait_send()

    return pl.pallas_call(
        body, name=name, in_specs=[ANY] * nt, out_specs=[ANY] * nt,
        out_shape=[jax.ShapeDtypeStruct((3,) + a.shape[1:], a.dtype) for a in parts],
        scratch_shapes=[pltpu.SemaphoreType.DMA((nt, 3)), pltpu.SemaphoreType.DMA((nt, 3))],
    )(*parts)


def _sum_join(p, got, where, axis, *, name):
    _, hr, cdim = p.shape
    full = (2 * hr, cdim) if axis == 0 else (hr, 2 * cdim)
    blk_shape, n, at = _comm_chunks(full, axis)
    step_len = blk_shape[axis]
    half_len = full[axis] // 2

    def body(w_ref, p_ref, g_ref, out, buf, lsem, ssem, rsem):
        x, y, c, _ = _place()
        sib = (x, y, 1 - c)
        r = pl.program_id(0)

        def part(start, size):
            return out.at[(slice(None),) * axis + (pl.ds(start, size),)]

        def copies(step, slot):
            rows = part(pl.multiple_of(c * half_len + step * step_len, 8 if axis == 0 else 128), step_len)
            return (pltpu.make_async_copy(buf.at[slot], rows, lsem.at[slot]),
                    pltpu.make_async_remote_copy(src_ref=buf.at[slot], dst_ref=rows, send_sem=ssem.at[slot],
                                                 recv_sem=rsem, device_id=sib, device_id_type=MESH))

        s = lax.rem(r, 2)

        @pl.when(r >= 2)
        def _():
            lc, rm = copies(r - 2, s)
            lc.wait()
            rm.wait_send()

        buf[s] = p_ref[0].astype(F32) + g_ref[0].astype(F32) + g_ref[1].astype(F32) + g_ref[2].astype(F32)
        lc, rm = copies(r, s)
        lc.start()
        rm.start()

        @pl.when(r == n - 1)
        def _():
            for step in range(max(0, n - 2), n):
                lc, rm = copies(step, step % 2)
                lc.wait()
                rm.wait_send()
            whole = part(0, half_len)
            pltpu.make_async_remote_copy(src_ref=whole, dst_ref=whole, send_sem=ssem.at[0], recv_sem=rsem,
                                         device_id=sib, device_id_type=MESH).wait_recv()

    grid_spec = pltpu.PrefetchScalarGridSpec(
        num_scalar_prefetch=1, grid=(n,),
        in_specs=[pl.BlockSpec((1,) + blk_shape, lambda t, w: (w[1],) + at(0, t)),
                  pl.BlockSpec((3,) + blk_shape, lambda t, w: (0,) + at(0, t))],
        out_specs=ANY,
        scratch_shapes=[pltpu.VMEM((2,) + blk_shape, F32), pltpu.SemaphoreType.DMA((2,)),
                        pltpu.SemaphoreType.DMA((2,)), pltpu.SemaphoreType.DMA])
    return pl.pallas_call(
        body, name=name, grid_spec=grid_spec, out_shape=jax.ShapeDtypeStruct(full, F32),
        compiler_params=_cparams("arbitrary"),
    )(where, p, got)


def _rider_gather_send(shards, axes):
    nt = len(shards)

    def copies(src, dst, send, recv, lsem):
        x, y, c, chips = _place()
        me = 2 * x + y
        local = [pltpu.make_async_copy(src[t], dst[t].at[me], lsem.at[t]) for t in range(nt)]
        out, landed = [], []
        for t in range(nt):
            for k in range(3):
                peer = (chips[k][0], chips[k][1], c)
                out.append(pltpu.make_async_remote_copy(
                    src_ref=_half(src[t], axes[t], c), dst_ref=_half(dst[t], axes[t], c, lead=(me,)),
                    send_sem=send.at[t, k], recv_sem=recv.at[t, k], device_id=peer, device_id_type=MESH))
                theirs = _half(dst[t], axes[t], c, lead=(2 * chips[k][0] + chips[k][1],))
                landed.append(pltpu.make_async_remote_copy(
                    src_ref=theirs, dst_ref=theirs, send_sem=send.at[t, k], recv_sem=recv.at[t, k],
                    device_id=peer, device_id_type=MESH))
        return local, out, landed

    def start(src, dst, sems):
        local, out, _ = copies(src, dst, *sems)
        for cp in local + out:
            cp.start()

    def finish(src, dst, sems):
        local, out, landed = copies(src, dst, *sems)
        for cp in landed:
            cp.wait_recv()
        for cp in out:
            cp.wait_send()
        for cp in local:
            cp.wait()

    return _Rider(shards, [jax.ShapeDtypeStruct((N_CHIP,) + a.shape, a.dtype) for a in shards],
                  [pltpu.SemaphoreType.DMA((nt, 3)), pltpu.SemaphoreType.DMA((nt, 3)), pltpu.SemaphoreType.DMA((nt,))],
                  start, finish)


def _rider_gather_forward(bufs, axes):
    nt = len(bufs)

    def copies(src, dst, send, recv):
        x, y, c, chips = _place()
        mine, theirs = [], []
        for t in range(nt):
            for k in range(3):
                slot = 2 * chips[k][0] + chips[k][1]
                for hc, into in ((c, mine), (1 - c, theirs)):
                    into.append(pltpu.make_async_remote_copy(
                        src_ref=_half(src[t], axes[t], hc, lead=(slot,)),
                        dst_ref=_half(dst[t], axes[t], hc, lead=(slot,)),
                        send_sem=send.at[t, k], recv_sem=recv.at[t, k], device_id=(x, y, 1 - c), device_id_type=MESH))
        return mine, theirs

    def start(src, dst, sems):
        for cp in copies(src, dst, *sems)[0]:
            cp.start()

    def finish(src, dst, sems):
        mine, theirs = copies(src, dst, *sems)
        for cp in theirs:
            cp.wait_recv()
        for cp in mine:
            cp.wait_send()

    return _Rider(bufs, [jax.ShapeDtypeStruct(a.shape, a.dtype) for a in bufs],
                  [pltpu.SemaphoreType.DMA((nt, 3)), pltpu.SemaphoreType.DMA((nt, 3))], start, finish,
                  aliases={t: t for t in range(nt)})


def _rider_chip_exchange(parts):
    nt = len(parts)

    def copies(src, got, send, recv):
        x, y, c, chips = _place()
        return [pltpu.make_async_remote_copy(
            src_ref=src[t].at[2 * chips[k][0] + chips[k][1]], dst_ref=got[t].at[k], send_sem=send.at[t, k],
            recv_sem=recv.at[t, k], device_id=(chips[k][0], chips[k][1], c), device_id_type=MESH)
            for t in range(nt) for k in range(3)]

    def start(src, got, sems):
        for cp in copies(src, got, *sems):
            cp.start()

    def finish(src, got, sems):
        remote = copies(src, got, *sems)
        for cp in remote:
            cp.wait_recv()
        for cp in remote:
            cp.wait_send()

    return _Rider(parts, [jax.ShapeDtypeStruct((3,) + a.shape[1:], a.dtype) for a in parts],
                  [pltpu.SemaphoreType.DMA((nt, 3)), pltpu.SemaphoreType.DMA((nt, 3))], start, finish)


def _gather_all(block, *, name):
    m_per, n = block.shape

    def body(x_ref, out_ref, send_sems, recv_sems, local_sem):
        x, y, c, chips = _place()
        me, sibling = (x, y, c), (x, y, 1 - c)

        def rows(px, py, pc):
            return out_ref.at[4 * px + 2 * py + pc]

        def copy(k, blk, to, src=None):
            return pltpu.make_async_remote_copy(
                src_ref=rows(*blk) if src is None else src, dst_ref=rows(*blk),
                send_sem=send_sems.at[k], recv_sem=recv_sems.at[k], device_id=to, device_id_type=MESH)

        mine = pltpu.make_async_copy(x_ref, rows(*me), local_sem)
        mine.start()
        first = [copy(0, me, sibling, src=x_ref)]
        first += [copy(1 + j, me, (*chip, c), src=x_ref) for j, chip in enumerate(chips)]
        for cp in first:
            cp.start()
        passed = [copy(4 + j, (*chip, c), sibling) for j, chip in enumerate(chips)]
        for j, chip in enumerate(chips):
            copy(1 + j, (*chip, c), me).wait_recv()
            passed[j].start()
        copy(0, sibling, me).wait_recv()
        for j, chip in enumerate(chips):
            copy(4 + j, (*chip, 1 - c), me).wait_recv()
        for cp in first + passed:
            cp.wait_send()
        mine.wait()

    return pl.pallas_call(
        body, name=name,
        out_shape=jax.ShapeDtypeStruct((N_DEV, m_per, n), block.dtype),
        in_specs=[pl.BlockSpec(memory_space=pltpu.VMEM)], out_specs=pl.BlockSpec(memory_space=pltpu.VMEM),
        scratch_shapes=[pltpu.SemaphoreType.DMA((7,)), pltpu.SemaphoreType.DMA((7,)), pltpu.SemaphoreType.DMA],
        compiler_params=pltpu.CompilerParams(vmem_limit_bytes=VMEM_LIMIT),
    )(block)


def _sum_slots(slots, *, name):
    n, m, c = slots.shape
    t = _rows_tile(m, c * n)

    def body(s_ref, o_ref):
        acc = s_ref[0]
        for k in range(1, n):
            acc = acc + s_ref[k]
        o_ref[...] = acc

    return pl.pallas_call(
        body, name=name, grid=(m // t,), in_specs=[pl.BlockSpec((n, t, c), lambda i: (0, i, 0))],
        out_specs=pl.BlockSpec((t, c), lambda i: (i, 0)), out_shape=jax.ShapeDtypeStruct((m, c), F32),
        compiler_params=_cparams("parallel"),
    )(slots)


def _pad_rows(a, rows):
    return a if a.shape[0] == rows else jnp.pad(a, ((0, rows - a.shape[0]), (0, 0)))


def _w_in_padded(shards):
    full = shards.reshape(IN_COLS, shards.shape[2])
    return jnp.concatenate([_pad_rows(full[SEG[n][2]:SEG[n][2] + SEG[n][3]], SEG[n][1]) for n in SEG_ORDER], axis=0)


def _w_in_unpadded(gp):
    full = jnp.concatenate([gp[SEG[n][0]:SEG[n][0] + SEG[n][3]] for n in ORIG_ORDER], axis=0)
    return full.reshape(N_CHIP, IN_COLS // N_CHIP, gp.shape[1])


def _pad_heads(w, true_w, pad_w):
    r = w.shape[0]
    h = w.shape[1] // true_w
    return jnp.pad(w.reshape(r, h, true_w), ((0, 0), (0, 0), (0, pad_w - true_w))).reshape(r, h * pad_w)


def _unpad_heads(w, true_w, pad_w):
    r = w.shape[0]
    h = w.shape[1] // pad_w
    return w.reshape(r, h, pad_w)[:, :, :true_w].reshape(r, h * true_w)


def _cols_to_slots(a):
    return a.reshape(a.shape[0], N_CHIP, a.shape[1] // N_CHIP).transpose(1, 0, 2)


def _slots_to_cols(a):
    return jnp.concatenate([a[j] for j in range(N_CHIP)], axis=1)


def _to_heads(a, h, d):
    return a.reshape(a.shape[0], h, d).transpose(1, 0, 2)


def _from_heads(a):
    return a.transpose(1, 0, 2).reshape(a.shape[1], -1)


SMALL = [("norm_g", 2048), ("ret_norm_g", 512), ("gla_ba_f", 256), ("gla_ba_b", 256), ("gla_norm_g", 512),
         ("pool_w", 4 * 128 * 128), ("pool_scale", 512), ("mla_q_norm_g", 512), ("mla_kv_norm_g", 256),
         ("mla_qk_norm_q", 192), ("mla_qk_norm_k", 192)]


def _pack_small(vals):
    parts = []
    for name, n in SMALL:
        parts += [v.reshape(-1) for v in vals[name]]
        if (DEPTH * n) % 1024:
            parts.append(jnp.zeros((-(DEPTH * n)) % 1024, F32))
    parts += [vals["loss"].reshape(-1), jnp.zeros(1023, F32)]
    return jnp.concatenate(parts).reshape(-1, 128)


def _unpack_small(block):
    flat = block.reshape(-1)
    out, off = {}, 0
    for name, n in SMALL:
        out[name] = flat[off:off + DEPTH * n]
        off += DEPTH * n + (-(DEPTH * n)) % 1024
    out["loss"] = flat[off]
    return out


def _layer_weights(l, p, g):
    wa = jnp.zeros((128, 512), F32)
    wa = wa.at[0:GLA_RANK, 0:256].set(_slots_to_cols(g["gla_wa2_f"]))
    wa = wa.at[GLA_RANK:2 * GLA_RANK, 256:512].set(_slots_to_cols(g["gla_wa2_b"]))
    return dict(
        norm_g=p["norm_g"][l][None, :],
        w_in=_w_in_padded(g["w_in"]),
        w_out=g["w_out"].reshape(4 * g["w_out"].shape[1], -1),
        ret_norm_g=p["ret_norm_g"][l][None, :],
        wa=_bf(wa),
        ba=jnp.concatenate([p["gla_ba_f"][l], p["gla_ba_b"][l]])[None, :],
        gla_norm_g=p["gla_norm_g"][l][None, :],
        pool_w=_bf(p["pool_w"][l]),
        pool_scale=p["pool_scale"][l][None, :],
        qg=p["mla_q_norm_g"][l][None, :],
        wq=_pad_heads(_slots_to_cols(g["mla_wq_b"]), MLA_QK, MLA_QKP),
        kvg=p["mla_kv_norm_g"][l][None, :],
        wkv=_slots_to_cols(g["mla_wkv_b"]),
        qng=jnp.pad(p["mla_qk_norm_q"][l], (0, MLA_QKP - MLA_QK))[None, :],
        kng=jnp.pad(p["mla_qk_norm_k"][l], (0, MLA_QKP - MLA_QK))[None, :],
    )


def _layer_fwd(l, x, w, tabs, next_shards=None):
    ret_cos, ret_sin, mla_cos, mla_sp, mla_sn = tabs
    nm = lambda s: f"l{l}_{s}"
    h = _rmsnorm_fwd(x, w["norm_g"], name=nm("norm"))
    if next_shards is None:
        z = _matmul(h, w["w_in"], tb=True, name=nm("in_proj"))
    else:
        z, landed = _matmul(h, w["w_in"], tb=True, rider=_rider_gather_send(next_shards, SHARD_AXES),
                            name=nm("in_proj"))
    qr, kr = _ret_pre(z, ret_cos, ret_sin, name=nm("ret_pre"))
    ret_o = _bla(qr, kr, z, _ret_log_gamma(False), (0, 0, SEG["rv"][0] // 512), name=nm("ret_scan"))
    y_a = _post(ret_o, z, SEG["rg"][0] // 512, w["ret_norm_g"], norm=True, name=nm("ret_post"))
    la = _gla_gate(z, w["wa"], w["ba"], name=nm("gla_gate"))
    la_h = la.reshape(la.shape[0], 2, GLA_HEADS, GLA_DK).transpose(1, 2, 0, 3)
    gq = _to_heads(z[:, SEG["gq"][0]:SEG["gq"][0] + 256], GLA_HEADS, GLA_DK)
    gk = _to_heads(z[:, SEG["gk"][0]:SEG["gk"][0] + 256], GLA_HEADS, GLA_DK)
    gla_o, gla_st = _gla_fwd(gq, gk, z, la_h, name=nm("gla_scan"))
    y_b = _post(gla_o, z, SEG["gg"][0] // 512, w["gla_norm_g"], norm=True, name=nm("gla_post"))
    y_c = _pool_fwd(z, w["pool_w"], w["pool_scale"], name=nm("pool"))
    q, k, v = _mla_pre(z, w["qg"], w["wq"], w["kvg"], w["wkv"], w["qng"], w["kng"], mla_cos, mla_sp, mla_sn,
                       name=nm("mla_pre"))
    if next_shards is None:
        (att_o, lse), gathered = _flash_fwd(q, k, v, name=nm("attn")), None
    else:
        att_o, lse, gathered = _flash_fwd(q, k, v, rider=_rider_gather_forward(landed, SHARD_AXES), name=nm("attn"))
    y_d = _post([att_o], z, SEG["mg"][0] // 512, w["qg"], norm=False, name=nm("mla_post"))
    y = jnp.concatenate([y_a, y_b, y_c, y_d], axis=1)
    x_next = _matmul(y, w["w_out"], add=x, name=nm("out_proj"))
    saved = dict(x=x, h=h, z=z, y=y, qr=qr, kr=kr, ret_o=ret_o, la_h=la_h, gq=gq, gk=gk, gla_o=gla_o, gla_st=gla_st,
                 q=q, k=k, v=v, att_o=att_o, lse=lse)
    return x_next, saved, gathered


def _layer_bwd(l, dx_next, w, sv, tabs, riding_parts=None):
    ret_cos, ret_sin, mla_cos, mla_sp, mla_sn = tabs
    nm = lambda s: f"l{l}_{s}"
    z = sv["z"]
    dy = _matmul(dx_next, w["w_out"], tb=True, name=nm("out_proj_dy"))
    d_w_out = _matmul(sv["y"], dx_next, ta=True, tn=512, name=nm("out_proj_dw"))
    d_rg, d_ret_o, d_ret_g = _post_bwd(dy, 0, sv["ret_o"], z, SEG["rg"][0] // 512, w["ret_norm_g"], norm=True,
                                       name=nm("ret_post_bwd"))
    vcol = SEG["rv"][0] // 512
    dqr = _bla(d_ret_o, z, sv["kr"], _ret_log_gamma(False), (0, vcol, 0), name=nm("ret_scan_dq"))
    dkr = _bla(z, d_ret_o, sv["qr"], _ret_log_gamma(True), (vcol, 0, 0), name=nm("ret_scan_dk"))
    drv = _bla(sv["kr"], sv["qr"], d_ret_o, _ret_log_gamma(True), (0, 0, 0), name=nm("ret_scan_dv"))
    d_rq, d_rk = _ret_pre_bwd(dqr, dkr, ret_cos, ret_sin, name=nm("ret_pre_bwd"))
    d_rv = _add_n([drv[0], drv[1]], out_dtype=BF16, name=nm("ret_dv_sum"))
    d_gg, d_gla_o, d_gla_g = _post_bwd(dy, 1, sv["gla_o"], z, SEG["gg"][0] // 512, w["gla_norm_g"], norm=True,
                                       name=nm("gla_post_bwd"))
    dq2, dk2, dla2, dv2 = _gla_bwd(sv["gq"], sv["gk"], z, sv["la_h"], d_gla_o, sv["gla_st"], name=nm("gla_scan_bwd"))
    d_gq = _bf(_from_heads(dq2[0] + dq2[1]))
    d_gk = _bf(_from_heads(dk2[0] + dk2[1]))
    d_gv = _add_n([dv2[0], dv2[1]], out_dtype=BF16, name=nm("gla_dv_sum"))
    dla = jnp.concatenate([_from_heads(dla2[0]), _from_heads(dla2[1])], axis=1)
    d_ga, d_wa, d_ba = _gla_gate_bwd(dla, z, w["wa"], w["ba"], name=nm("gla_gate_bwd"))
    d_pv, d_pg, d_pool_w, d_pool_scale = _pool_bwd(dy, z, w["pool_w"], w["pool_scale"], name=nm("pool_bwd"))
    d_mg, d_att_o, _ = _post_bwd(dy, 3, [sv["att_o"]], z, SEG["mg"][0] // 512, w["qg"], norm=False,
                                 name=nm("mla_post_bwd"))
    if riding_parts is None:
        (dq, dk, dv), rode = _flash_bwd(sv["q"], sv["k"], sv["v"], d_att_o, sv["att_o"], sv["lse"],
                                        name=nm("attn_bwd")), None
    else:
        dq, dk, dv, rode = _flash_bwd(sv["q"], sv["k"], sv["v"], d_att_o, sv["att_o"], sv["lse"],
                                      rider=_rider_chip_exchange(riding_parts), name=nm("attn_bwd"))
    d_mq, d_mkv, d_mkr, d_wq, d_wkv, d_qg, d_kvg, d_qng, d_kng = _mla_pre_bwd(
        dq, dk, dv, z, w["qg"], w["wq"], w["kvg"], w["wkv"], w["qng"], w["kng"], mla_cos, mla_sp, mla_sn,
        name=nm("mla_pre_bwd"))
    segs = dict(rq=d_rq, rk=d_rk, rv=d_rv, rg=d_rg, gv=d_gv, gg=d_gg, pv=d_pv, pg=d_pg, mq=d_mq, mg=d_mg,
                gq=d_gq, gk=d_gk, mkv=d_mkv, ga=d_ga, mkr=d_mkr)
    dz = jnp.concatenate([segs[n] for n in SEG_ORDER], axis=1)
    dh = _matmul(dz, w["w_in"], tn=512, name=nm("in_proj_dh"))
    d_w_in = _matmul(dz, sv["h"], ta=True, name=nm("in_proj_dw"))
    dx, d_norm_g = _rmsnorm_bwd(sv["x"], dh, w["norm_g"], dx_next, name=nm("norm_bwd"))
    sharded = dict(
        w_in=_w_in_unpadded(d_w_in),
        w_out=d_w_out.reshape(N_CHIP, d_w_out.shape[0] // N_CHIP, d_w_out.shape[1]),
        mla_wq_b=_cols_to_slots(_unpad_heads(d_wq, MLA_QK, MLA_QKP)),
        mla_wkv_b=_cols_to_slots(d_wkv),
        gla_wa2_f=_cols_to_slots(d_wa[0:GLA_RANK, 0:256]),
        gla_wa2_b=_cols_to_slots(d_wa[GLA_RANK:2 * GLA_RANK, 256:512]),
    )
    small = dict(
        norm_g=d_norm_g[0], ret_norm_g=d_ret_g[0], gla_ba_f=d_ba[0, :256], gla_ba_b=d_ba[0, 256:],
        gla_norm_g=d_gla_g[0], pool_w=d_pool_w.reshape(-1), pool_scale=d_pool_scale[0], mla_q_norm_g=d_qg[0],
        mla_kv_norm_g=d_kvg[0], mla_qk_norm_q=d_qng[0, :MLA_QK], mla_qk_norm_k=d_kng[0, :MLA_QK],
    )
    return dx, sharded, small, rode


SHARDED = ["w_in", "w_out", "mla_wq_b", "mla_wkv_b", "gla_wa2_f", "gla_wa2_b"]
WEIGHTS = ["norm_g", "w_in", "ret_norm_g", "gla_wa2_f", "gla_ba_f", "gla_wa2_b", "gla_ba_b", "gla_norm_g", "pool_w",
           "pool_scale", "mla_q_norm_g", "mla_wq_b", "mla_kv_norm_g", "mla_wkv_b", "mla_qk_norm_q", "mla_qk_norm_k",
           "w_out"]


SHARD_AXES = [1, 0, 0, 0, 0, 0]


def _layer_shards(p, l):
    return [jnp.swapaxes(p["w_in"], 1, 2)[l].astype(BF16), p["w_out"][l].astype(BF16), p["mla_wq_b"][l].astype(BF16),
            p["mla_wkv_b"][l].astype(BF16), p["gla_wa2_f"][l], p["gla_wa2_b"][l]]


def _step(p, where):
    x = p["x"][0]
    tabs = _rope_tables(x.shape[0])
    got0 = _gather_shards(_layer_shards(p, 0), SHARD_AXES, name="l0_gather_weights")
    w0 = _layer_weights(0, p, dict(zip(SHARDED, got0)))
    x1, sv0, got1 = _layer_fwd(0, x, w0, tabs, next_shards=_layer_shards(p, 1))
    w1 = _layer_weights(1, p, dict(zip(SHARDED, got1)))
    x2, sv1, _ = _layer_fwd(1, x1, w1, tabs)
    dx, loss = _loss_head(x2, p["loss_target"][0], name="loss_head")

    big, big_axes = SHARDED[:2], SHARD_AXES[:2]

    def pair_sums(tag, tensors, axes, names):
        return [_pair_reduce(a, where, ax, out_dtype=BF16, name=f"{tag}_pair_reduce_{n}")
                for a, ax, n in zip(tensors, axes, names)]

    def joined(tag, pair, others, axes, names):
        return [_sum_join(a, b, where, ax, name=f"{tag}_sum_join_{n}")
                for a, b, ax, n in zip(pair, others, axes, names)]

    dx, sharded1, small1, _ = _layer_bwd(1, dx, w1, sv1, tabs)
    pair1 = pair_sums("l1", [sharded1[n] for n in big], big_axes, big)
    dx, sharded0, small0, others1 = _layer_bwd(0, dx, w0, sv0, tabs, riding_parts=pair1)
    grads1 = joined("l1", pair1, others1, big_axes, big)
    packed = jnp.concatenate([sh[n].reshape(N_CHIP, -1, 128) for sh in (sharded0, sharded1) for n in SHARDED[2:]],
                             axis=1)
    pair0 = pair_sums("l0", [sharded0[n] for n in big] + [packed], big_axes + [0], big + ["rest"])
    grads0 = joined("l0", pair0, _chip_exchange(pair0, name="l0_chip_exchange"), big_axes + [0], big + ["rest"])
    grads = {n: jnp.stack([g0, g1]) for n, g0, g1 in zip(big, grads0, grads1)}
    rest, off = grads0[2], 0
    pieces = {n: [] for n in SHARDED[2:]}
    for sh in (sharded0, sharded1):
        for n in SHARDED[2:]:
            rows = sh[n].shape[1] * sh[n].shape[2] // 128
            pieces[n].append(rest[off:off + rows].reshape(sh[n].shape[1:]))
            off += rows
    grads.update({n: jnp.stack(v) for n, v in pieces.items()})
    small = {n: [small0[n], small1[n]] for n, _ in SMALL}
    small["loss"] = loss
    return dx[None], grads, small


def kernel(x, norm_g, w_in, ret_norm_g, gla_wa2_f, gla_ba_f, gla_wa2_b, gla_ba_b, gla_norm_g, pool_w, pool_scale, mla_q_norm_g, mla_wq_b, mla_kv_norm_g, mla_wkv_b, mla_qk_norm_q, mla_qk_norm_k, w_out, loss_target, m_norm_g, m_w_in, m_ret_norm_g, m_gla_wa2_f, m_gla_ba_f, m_gla_wa2_b, m_gla_ba_b, m_gla_norm_g, m_pool_w, m_pool_scale, m_mla_q_norm_g, m_mla_wq_b, m_mla_kv_norm_g, m_mla_wkv_b, m_mla_qk_norm_q, m_mla_qk_norm_k, m_w_out, v_norm_g, v_w_in, v_ret_norm_g, v_gla_wa2_f, v_gla_ba_f, v_gla_wa2_b, v_gla_ba_b, v_gla_norm_g, v_pool_w, v_pool_scale, v_mla_q_norm_g, v_mla_wq_b, v_mla_kv_norm_g, v_mla_wkv_b, v_mla_qk_norm_q, v_mla_qk_norm_k, v_w_out):
    p = dict(x=x, norm_g=norm_g, w_in=w_in, ret_norm_g=ret_norm_g, gla_wa2_f=gla_wa2_f, gla_ba_f=gla_ba_f,
             gla_wa2_b=gla_wa2_b, gla_ba_b=gla_ba_b, gla_norm_g=gla_norm_g, pool_w=pool_w, pool_scale=pool_scale,
             mla_q_norm_g=mla_q_norm_g, mla_wq_b=mla_wq_b, mla_kv_norm_g=mla_kv_norm_g, mla_wkv_b=mla_wkv_b,
             mla_qk_norm_q=mla_qk_norm_q, mla_qk_norm_k=mla_qk_norm_k, w_out=w_out, loss_target=loss_target)
    moments = dict(
        m=dict(norm_g=m_norm_g, w_in=m_w_in, ret_norm_g=m_ret_norm_g, gla_wa2_f=m_gla_wa2_f, gla_ba_f=m_gla_ba_f,
               gla_wa2_b=m_gla_wa2_b, gla_ba_b=m_gla_ba_b, gla_norm_g=m_gla_norm_g, pool_w=m_pool_w,
               pool_scale=m_pool_scale, mla_q_norm_g=m_mla_q_norm_g, mla_wq_b=m_mla_wq_b,
               mla_kv_norm_g=m_mla_kv_norm_g, mla_wkv_b=m_mla_wkv_b, mla_qk_norm_q=m_mla_qk_norm_q,
               mla_qk_norm_k=m_mla_qk_norm_k, w_out=m_w_out),
        v=dict(norm_g=v_norm_g, w_in=v_w_in, ret_norm_g=v_ret_norm_g, gla_wa2_f=v_gla_wa2_f, gla_ba_f=v_gla_ba_f,
               gla_wa2_b=v_gla_wa2_b, gla_ba_b=v_gla_ba_b, gla_norm_g=v_gla_norm_g, pool_w=v_pool_w,
               pool_scale=v_pool_scale, mla_q_norm_g=v_mla_q_norm_g, mla_wq_b=v_mla_wq_b,
               mla_kv_norm_g=v_mla_kv_norm_g, mla_wkv_b=v_mla_wkv_b, mla_qk_norm_q=v_mla_qk_norm_q,
               mla_qk_norm_k=v_mla_qk_norm_k, w_out=v_w_out))

    where = jnp.stack([lax.axis_index("c"), 2 * lax.axis_index("x") + lax.axis_index("y")]).astype(jnp.int32)
    grad_x, grads, small = _step(p, where)

    slots = _gather_all(_pack_small(small), name="gather_small")
    total = _unpack_small(_sum_slots(slots, name="sum_small"))
    for n, _ in SMALL:
        grads[n] = total[n].reshape(p[n].shape)
    loss = total["loss"]

    delta, new_m, new_v = {}, {}, {}
    for n in WEIGHTS:
        turn = (lambda a: jnp.swapaxes(a, 1, 2)) if n == "w_in" else (lambda a: a)
        outs = _adamw(turn(p[n]), grads[n], turn(moments["m"][n]), turn(moments["v"][n]), name=f"adamw_{n}")
        grads[n] = turn(grads[n])
        delta[n], new_m[n], new_v[n] = (turn(o) for o in outs)
    return (loss, grad_x, *[grads[n] for n in WEIGHTS], *[delta[n] for n in WEIGHTS],
            *[new_m[n] for n in WEIGHTS], *[new_v[n] for n in WEIGHTS])
```

```python
import functools
import math

import jax
import jax.numpy as jnp
from jax import lax
from jax.experimental import pallas as pl
from jax.experimental.pallas import tpu as pltpu

F32 = jnp.float32
BF16 = jnp.bfloat16
MESH = pl.DeviceIdType.MESH

EPS = 1e-6
ROPE_THETA = 10000.0
DEPTH = 2
N_DEV = 8
N_CHIP = 4

GROUP_W = 512
RET_HEADS = 4
RET_HD = 128
RET_CHUNK = 256
GLA_HEADS = 4
GLA_DK = 64
GLA_DV = 128
GLA_RANK = 16
GLA_TAU = 16.0
GLA_CHUNK = 64
POOL_GROUPS = 4
POOL_GW = 128
POOL_HALO = 8
POOL_TILE = 256
MLA_HEADS = 4
MLA_NOPE = 128
MLA_ROPE = 64
MLA_QK = MLA_NOPE + MLA_ROPE
MLA_QKP = 256
MLA_V = 128
MLA_Q_RANK = 512
MLA_KV_RANK = 256
MLA_SCALE = MLA_QK ** -0.5
FLASH_STRIP = 1024

ADAM_LR = 0.001
ADAM_B1 = 0.9
ADAM_B2 = 0.999
ADAM_EPS = 1e-08
ADAM_WD = 0.01
ADAM_STEP = 10

VMEM_LIMIT = 56 * 1024 * 1024

SEG = {
    "rq": (0, 512, 0, 512), "rk": (512, 512, 512, 512), "rv": (1024, 512, 1024, 512), "rg": (1536, 512, 1536, 512),
    "gv": (2048, 512, 2560, 512), "gg": (2560, 512, 3072, 512),
    "pv": (3072, 512, 3616, 512), "pg": (3584, 512, 4128, 512),
    "mq": (4096, 512, 4640, 512), "mg": (4608, 512, 5472, 512),
    "gq": (5120, 256, 2048, 256), "gk": (5376, 256, 2304, 256), "mkv": (5632, 256, 5152, 256),
    "ga": (5888, 128, 3584, 32), "mkr": (6016, 128, 5408, 64),
}
SEG_ORDER = ["rq", "rk", "rv", "rg", "gv", "gg", "pv", "pg", "mq", "mg", "gq", "gk", "mkv", "ga", "mkr"]
IN_COLS = 5984
IN_PAD = 6144
ORIG_ORDER = ["rq", "rk", "rv", "rg", "gq", "gk", "gv", "gg", "ga", "pv", "pg", "mq", "mkv", "mkr", "mg"]


def _cparams(*sem):
    return pltpu.CompilerParams(dimension_semantics=tuple(sem), vmem_limit_bytes=VMEM_LIMIT)


def _bf(v):
    return v.astype(BF16)


def _dot(a, b, ca=1, cb=0):
    return lax.dot_general(_bf(a), _bf(b), (((ca,), (cb,)), ((), ())), preferred_element_type=F32)


def _split_dot(a01, x, ca=1, cb=0):
    hi = _bf(x)
    r1 = x - hi.astype(F32)
    mid = _bf(r1)
    lo = _bf(r1 - mid.astype(F32))
    dn = (((ca,), (cb,)), ((), ()))
    a = _bf(a01)
    return (lax.dot_general(a, hi, dn, preferred_element_type=F32)
            + lax.dot_general(a, mid, dn, preferred_element_type=F32)
            + lax.dot_general(a, lo, dn, preferred_element_type=F32))


def _sigmoid(x):
    return 1.0 / (1.0 + jnp.exp(-x))


def _silu_parts(g):
    sg = _sigmoid(g)
    return g * sg, sg * (1.0 + g * (1.0 - sg))


class _Rider:
    def __init__(self, ins, outs, sems, start, finish, aliases=None):
        self.ins, self.outs, self.sems, self.start, self.finish = list(ins), list(outs), list(sems), start, finish
        self.aliases = dict(aliases or {})


def _ride(body, rider, n_in, n_out, grid):
    if rider is None:
        return body
    ri, ro, rs = len(rider.ins), len(rider.outs), len(rider.sems)

    def wrapped(*refs):
        ins, refs = refs[:n_in], refs[n_in:]
        rin, refs = refs[:ri], refs[ri:]
        outs, refs = refs[:n_out], refs[n_out:]
        rout, refs = refs[:ro], refs[ro:]
        scratch, sems = refs[:len(refs) - rs], refs[len(refs) - rs:]
        first = pl.program_id(0) == 0
        last = pl.program_id(0) == grid[0] - 1
        for ax in range(1, len(grid)):
            first = jnp.logical_and(first, pl.program_id(ax) == 0)
            last = jnp.logical_and(last, pl.program_id(ax) == grid[ax] - 1)

        @pl.when(first)
        def _():
            rider.start(rin, rout, sems)

        body(*ins, *outs, *scratch)

        @pl.when(last)
        def _():
            rider.finish(rin, rout, sems)

    return wrapped


def _ride_call(body, rider, *, name, grid, in_specs, out_specs, out_shape, scratch_shapes, args, sem):
    n_in, n_out = len(in_specs), len(out_specs)
    if rider is None:
        return pl.pallas_call(body, name=name, grid=grid, in_specs=in_specs, out_specs=out_specs, out_shape=out_shape,
                              scratch_shapes=scratch_shapes, compiler_params=_cparams(*sem))(*args), []
    outs = pl.pallas_call(
        _ride(body, rider, n_in, n_out, grid), name=name, grid=grid,
        in_specs=list(in_specs) + [ANY] * len(rider.ins), out_specs=list(out_specs) + [ANY] * len(rider.outs),
        out_shape=list(out_shape) + rider.outs, scratch_shapes=list(scratch_shapes) + rider.sems,
        input_output_aliases={n_in + i: n_out + o for i, o in rider.aliases.items()},
        compiler_params=_cparams(*(["arbitrary"] * len(grid))),
    )(*args, *rider.ins)
    return outs[:n_out], outs[n_out:]


def _matmul(a, b, *, ta=False, tb=False, out_dtype=F32, tm=512, tn=1024, tk=None, add=None, n_outer=True, rider=None,
            name):
    m, kdim = (a.shape[1], a.shape[0]) if ta else a.shape
    n = b.shape[0] if tb else b.shape[1]
    tm, tn = min(tm, m), min(tn, n)
    tk = kdim if tk is None else min(tk, kdim)
    assert m % tm == 0 and n % tn == 0 and kdim % tk == 0
    nk = kdim // tk
    ca, cb = (0 if ta else 1), (1 if tb else 0)

    def body(*refs):
        if add is None:
            a_ref, b_ref, o_ref = refs[:3]
            add_ref = None
        else:
            a_ref, b_ref, add_ref, o_ref = refs[:4]
        p = _dot(a_ref[...], b_ref[...], ca, cb)

        def finish(r):
            if add_ref is not None:
                r = r + add_ref[...]
            o_ref[...] = r.astype(out_dtype)

        if nk == 1:
            finish(p)
        else:
            acc = refs[-1]
            k = pl.program_id(2)

            @pl.when(k == 0)
            def _():
                acc[...] = p

            @pl.when(k > 0)
            def _():
                acc[...] += p

            @pl.when(k == nk - 1)
            def _():
                finish(acc[...])

    def ij(g0, g1):
        return (g1, g0) if n_outer else (g0, g1)

    a_spec = (pl.BlockSpec((tk, tm), lambda g0, g1, k: (k, ij(g0, g1)[0])) if ta
              else pl.BlockSpec((tm, tk), lambda g0, g1, k: (ij(g0, g1)[0], k)))
    b_spec = (pl.BlockSpec((tn, tk), lambda g0, g1, k: (ij(g0, g1)[1], k)) if tb
              else pl.BlockSpec((tk, tn), lambda g0, g1, k: (k, ij(g0, g1)[1])))
    o_spec = pl.BlockSpec((tm, tn), lambda g0, g1, k: ij(g0, g1))
    in_specs = [a_spec, b_spec] + ([o_spec] if add is not None else [])
    args = (a, b) + ((add,) if add is not None else ())
    grid = (n // tn, m // tm, nk) if n_outer else (m // tm, n // tn, nk)
    (out,), rode = _ride_call(
        body, rider, name=name, grid=grid, in_specs=in_specs, out_specs=[o_spec],
        out_shape=[jax.ShapeDtypeStruct((m, n), out_dtype)],
        scratch_shapes=[] if nk == 1 else [pltpu.VMEM((tm, tn), F32)], args=args,
        sem=("parallel", "parallel", "arbitrary"))
    return out if rider is None else (out, rode)


def _rmsnorm_fwd(x, g, *, name, tm=256):
    s, d = x.shape
    tm = min(tm, s)

    def body(x_ref, g_ref, h_ref):
        xv = x_ref[...]
        r = lax.rsqrt(jnp.mean(xv * xv, axis=-1, keepdims=True) + EPS)
        h_ref[...] = _bf(xv * r * g_ref[...])

    return pl.pallas_call(
        body, name=name, grid=(s // tm,),
        in_specs=[pl.BlockSpec((tm, d), lambda i: (i, 0)), pl.BlockSpec((1, d), lambda i: (0, 0))],
        out_specs=pl.BlockSpec((tm, d), lambda i: (i, 0)),
        out_shape=jax.ShapeDtypeStruct((s, d), BF16),
        compiler_params=_cparams("parallel"),
    )(x, g)


def _rmsnorm_bwd(x, dh, g, dres, *, name, tm=256):
    s, d = x.shape
    tm = min(tm, s)

    def body(x_ref, dh_ref, g_ref, dres_ref, dx_ref, dg_ref):
        i = pl.program_id(0)
        xv = x_ref[...]
        r = lax.rsqrt(jnp.mean(xv * xv, axis=-1, keepdims=True) + EPS)
        xn = xv * r
        dv = dh_ref[...]
        part = jnp.sum(dv * xn, axis=0, keepdims=True)

        @pl.when(i == 0)
        def _():
            dg_ref[...] = part

        @pl.when(i > 0)
        def _():
            dg_ref[...] += part

        dxn = dv * g_ref[...]
        dx_ref[...] = dres_ref[...] + r * (dxn - xn * jnp.mean(dxn * xn, axis=-1, keepdims=True))

    row = pl.BlockSpec((tm, d), lambda i: (i, 0))
    vec = pl.BlockSpec((1, d), lambda i: (0, 0))
    return pl.pallas_call(
        body, name=name, grid=(s // tm,), in_specs=[row, row, vec, row], out_specs=[row, vec],
        out_shape=[jax.ShapeDtypeStruct((s, d), F32), jax.ShapeDtypeStruct((1, d), F32)],
        compiler_params=_cparams("arbitrary"),
    )(x, dh, g, dres)


def _loss_head(xf, target, *, name, tm=256):
    s, d = xf.shape
    tm = min(tm, s)

    def body(x_ref, t_ref, dx_ref, l_ref):
        i = pl.program_id(0)
        e = x_ref[...] - t_ref[...]
        dx_ref[...] = e * (1.0 / d)
        rows = jnp.mean(e * e, axis=-1, keepdims=True)
        part = 0.5 * jnp.sum(rows, axis=0, keepdims=True)

        @pl.when(i == 0)
        def _():
            l_ref[...] = part

        @pl.when(i > 0)
        def _():
            l_ref[...] += part

    row = pl.BlockSpec((tm, d), lambda i: (i, 0))
    return pl.pallas_call(
        body, name=name, grid=(s // tm,), in_specs=[row, row],
        out_specs=[row, pl.BlockSpec((1, 1), lambda i: (0, 0))],
        out_shape=[jax.ShapeDtypeStruct((s, d), F32), jax.ShapeDtypeStruct((1, 1), F32)],
        compiler_params=_cparams("arbitrary"),
    )(xf, target)


def _rope_tables(s):
    pos = jnp.arange(s, dtype=F32)[:, None]
    inv_r = 1.0 / (ROPE_THETA ** (jnp.arange(0, RET_HD, 2, dtype=F32) / RET_HD))
    ang = pos * inv_r[None, :]
    ret_cos = jnp.concatenate([jnp.cos(ang), jnp.cos(ang)], axis=1)
    ret_sin = jnp.concatenate([-jnp.sin(ang), jnp.sin(ang)], axis=1)
    inv_m = 1.0 / (ROPE_THETA ** (jnp.arange(0, MLA_ROPE, 2, dtype=F32) / MLA_ROPE))
    am = pos * inv_m[None, :]
    z32, z64 = jnp.zeros((s, 32), F32), jnp.zeros((s, 64), F32)
    mla_cos = jnp.concatenate([jnp.cos(am), jnp.cos(am), z64], axis=1)
    mla_sp = jnp.concatenate([z32, jnp.sin(am), z64], axis=1)
    mla_sn = jnp.concatenate([-jnp.sin(am), z32, z64], axis=1)
    return ret_cos, ret_sin, mla_cos, mla_sp, mla_sn


def _rope128(x, c, sg):
    return x * c + pltpu.roll(x, 64, 1) * sg


def _unrope128(d, c, sg):
    return d * c + pltpu.roll(d * sg, 64, 1)


def _rope64(t, c, sp, sn):
    return t * c + pltpu.roll(t, 96, 1) * sn + pltpu.roll(t, 32, 1) * sp


def _unrope64(d, c, sp, sn):
    return d * c + pltpu.roll(d * sn, 32, 1) + pltpu.roll(d * sp, 96, 1)


def _ret_pre(z, cos, sin, *, name, tm=256):
    s = z.shape[0]
    tm = min(tm, s)
    scale = RET_HD ** -0.5

    def body(q_ref, k_ref, c_ref, s_ref, qo_ref, ko_ref):
        c, sg = c_ref[...], s_ref[...]
        for h in range(RET_HEADS):
            sl = slice(h * RET_HD, (h + 1) * RET_HD)
            qo_ref[:, sl] = _rope128(q_ref[:, sl], c, sg)
            ko_ref[:, sl] = _rope128(k_ref[:, sl], c, sg) * scale

    seg = lambda j: pl.BlockSpec((tm, GROUP_W), lambda i: (i, j))
    tab = pl.BlockSpec((tm, RET_HD), lambda i: (i, 0))
    return pl.pallas_call(
        body, name=name, grid=(s // tm,), in_specs=[seg(0), seg(1), tab, tab],
        out_specs=[seg(0), seg(0)],
        out_shape=[jax.ShapeDtypeStruct((s, GROUP_W), F32)] * 2,
        compiler_params=_cparams("parallel"),
    )(z, z, cos, sin)


def _ret_pre_bwd(dqr, dkr, cos, sin, *, name, tm=256):
    s = dqr[0].shape[0]
    tm = min(tm, s)
    scale = RET_HD ** -0.5

    def body(dq0_ref, dq1_ref, dk0_ref, dk1_ref, c_ref, s_ref, qo_ref, ko_ref):
        c, sg = c_ref[...], s_ref[...]
        for h in range(RET_HEADS):
            sl = slice(h * RET_HD, (h + 1) * RET_HD)
            qo_ref[:, sl] = _bf(_unrope128(dq0_ref[:, sl] + dq1_ref[:, sl], c, sg))
            ko_ref[:, sl] = _bf(_unrope128(dk0_ref[:, sl] + dk1_ref[:, sl], c, sg) * scale)

    row = pl.BlockSpec((tm, GROUP_W), lambda i: (i, 0))
    tab = pl.BlockSpec((tm, RET_HD), lambda i: (i, 0))
    return pl.pallas_call(
        body, name=name, grid=(s // tm,), in_specs=[row, row, row, row, tab, tab], out_specs=[row, row],
        out_shape=[jax.ShapeDtypeStruct((s, GROUP_W), BF16)] * 2,
        compiler_params=_cparams("parallel"),
    )(dqr[0], dqr[1], dkr[0], dkr[1], cos, sin)


def _bla(a, b, c, lg, cols, *, name):
    s = a.shape[0]
    ch = min(RET_CHUNK, s)
    n = s // ch
    hd = RET_HD

    def body(lg_ref, a0, b0, c0, a1, b1, c1, o0, o1, st):
        t = pl.program_id(0)

        @pl.when(t == 0)
        def _():
            st[...] = jnp.zeros_like(st)

        ii = lax.broadcasted_iota(jnp.int32, (ch, ch), 0)
        jj = lax.broadcasted_iota(jnp.int32, (ch, ch), 1)
        idx = lax.broadcasted_iota(jnp.int32, (ch, 1), 0).astype(F32)
        for d, (a_ref, b_ref, c_ref, o_ref) in enumerate(((a0, b0, c0, o0), (a1, b1, c1, o1))):
            diff = ((ii - jj) if d == 0 else (jj - ii)).astype(F32)
            keep = diff >= 0
            dpos = jnp.maximum(diff, 0.0)
            pq = (idx + 1.0) if d == 0 else (ch - idx)
            pk = (ch - 1.0 - idx) if d == 0 else idx
            for h in range(RET_HEADS):
                g = lg_ref[d, h]
                sl = slice(h * hd, (h + 1) * hd)
                av, bv, cv = a_ref[:, sl], b_ref[:, sl], c_ref[:, sl]
                sc = _dot(av, bv, 1, 1) * jnp.where(keep, jnp.exp(dpos * g), 0.0)
                stv = st[d, h]
                o_ref[:, sl] = _dot(sc, cv) + _dot(av * jnp.exp(pq * g), stv)
                st[d, h] = jnp.exp(ch * g) * stv + _dot(bv * jnp.exp(pk * g), cv, 0, 0)

    fwd = lambda j: pl.BlockSpec((ch, GROUP_W), lambda t: (t, j))
    bwd = lambda j: pl.BlockSpec((ch, GROUP_W), lambda t: (n - 1 - t, j))
    return pl.pallas_call(
        body, name=name, grid=(n,),
        in_specs=[pl.BlockSpec(memory_space=pltpu.SMEM), fwd(cols[0]), fwd(cols[1]), fwd(cols[2]),
                  bwd(cols[0]), bwd(cols[1]), bwd(cols[2])],
        out_specs=[fwd(0), bwd(0)],
        out_shape=[jax.ShapeDtypeStruct((s, GROUP_W), F32)] * 2,
        scratch_shapes=[pltpu.VMEM((2, RET_HEADS, hd, hd), F32)],
        compiler_params=_cparams("arbitrary"),
    )(lg, a, b, c, a, b, c)


def _post(os_, zg, gcol, g, *, norm, name, tm=256):
    s = zg.shape[0]
    tm = min(tm, s)
    nd = len(os_)

    def body(*refs):
        o_refs, (gt_ref, g_ref, y_ref) = refs[:nd], refs[nd:]
        silu, _ = _silu_parts(gt_ref[...])
        for h in range(4):
            sl = slice(h * 128, (h + 1) * 128)
            o = o_refs[0][:, sl]
            for k in range(1, nd):
                o = o + o_refs[k][:, sl]
            if norm:
                r = lax.rsqrt(jnp.mean(o * o, axis=-1, keepdims=True) + EPS)
                o = o * r * g_ref[:, sl]
            y_ref[:, sl] = _bf(silu[:, sl] * o)

    row = pl.BlockSpec((tm, GROUP_W), lambda i: (i, 0))
    return pl.pallas_call(
        body, name=name, grid=(s // tm,),
        in_specs=[row] * nd + [pl.BlockSpec((tm, GROUP_W), lambda i: (i, gcol)),
                               pl.BlockSpec((1, GROUP_W), lambda i: (0, 0))],
        out_specs=row,
        out_shape=jax.ShapeDtypeStruct((s, GROUP_W), BF16),
        compiler_params=_cparams("parallel"),
    )(*os_, zg, g)


def _post_bwd(dy, ycol, os_, zg, gcol, g, *, norm, name, tm=256):
    s = zg.shape[0]
    tm = min(tm, s)
    nd = len(os_)

    def body(*refs):
        dy_ref, o_refs = refs[0], refs[1:1 + nd]
        gt_ref, g_ref, dgt_ref, do_ref, dg_ref = refs[1 + nd:]
        i = pl.program_id(0)
        silu, dsilu = _silu_parts(gt_ref[...])
        dyv = dy_ref[...]
        parts = []
        for h in range(4):
            sl = slice(h * 128, (h + 1) * 128)
            o = o_refs[0][:, sl]
            for k in range(1, nd):
                o = o + o_refs[k][:, sl]
            dn = dyv[:, sl] * silu[:, sl]
            if norm:
                r = lax.rsqrt(jnp.mean(o * o, axis=-1, keepdims=True) + EPS)
                xn = o * r
                gh = g_ref[:, sl]
                dgt_ref[:, sl] = _bf(dyv[:, sl] * (xn * gh) * dsilu[:, sl])
                parts.append(jnp.sum(dn * xn, axis=0, keepdims=True))
                dxn = dn * gh
                do_ref[:, sl] = r * (dxn - xn * jnp.mean(dxn * xn, axis=-1, keepdims=True))
            else:
                dgt_ref[:, sl] = _bf(dyv[:, sl] * o * dsilu[:, sl])
                parts.append(jnp.zeros((1, 128), F32))
                do_ref[:, sl] = dn
        part = jnp.concatenate(parts, axis=1)

        @pl.when(i == 0)
        def _():
            dg_ref[...] = part

        @pl.when(i > 0)
        def _():
            dg_ref[...] += part

    row = pl.BlockSpec((tm, GROUP_W), lambda i: (i, 0))
    vec = pl.BlockSpec((1, GROUP_W), lambda i: (0, 0))
    return pl.pallas_call(
        body, name=name, grid=(s // tm,),
        in_specs=[pl.BlockSpec((tm, GROUP_W), lambda i: (i, ycol))] + [row] * nd
        + [pl.BlockSpec((tm, GROUP_W), lambda i: (i, gcol)), vec],
        out_specs=[row, row, vec],
        out_shape=[jax.ShapeDtypeStruct((s, GROUP_W), BF16), jax.ShapeDtypeStruct((s, GROUP_W), F32),
                   jax.ShapeDtypeStruct((1, GROUP_W), F32)],
        compiler_params=_cparams("arbitrary"),
    )(dy, *os_, zg, g)


def _ret_log_gamma(swap):
    gf = 1.0 - 2.0 ** (-5.0 - jnp.arange(RET_HEADS, dtype=F32))
    lf, lb = jnp.log(gf), jnp.log(gf[::-1])
    return jnp.stack([lb, lf] if swap else [lf, lb])


def _log_sigmoid(x):
    return jnp.minimum(x, 0.0) - jnp.log(1.0 + jnp.exp(-jnp.abs(x)))


def _gla_gate(z, wa, ba, *, name, tm=256):
    s = z.shape[0]
    tm = min(tm, s)
    col = SEG["ga"][0] // 128

    def body(ga_ref, wa_ref, ba_ref, la_ref):
        pre = _dot(ga_ref[...], wa_ref[...]) + ba_ref[...]
        la_ref[...] = _log_sigmoid(pre) / GLA_TAU

    return pl.pallas_call(
        body, name=name, grid=(s // tm,),
        in_specs=[pl.BlockSpec((tm, 128), lambda i: (i, col)), pl.BlockSpec((128, 512), lambda i: (0, 0)),
                  pl.BlockSpec((1, 512), lambda i: (0, 0))],
        out_specs=pl.BlockSpec((tm, 512), lambda i: (i, 0)),
        out_shape=jax.ShapeDtypeStruct((s, 512), F32),
        compiler_params=_cparams("parallel"),
    )(z, wa, ba)


def _gla_gate_bwd(dla, z, wa, ba, *, name, tm=256):
    s = z.shape[0]
    tm = min(tm, s)
    col = SEG["ga"][0] // 128

    def body(dla_ref, ga_ref, wa_ref, ba_ref, dga_ref, dwa_ref, dba_ref):
        i = pl.program_id(0)
        gav = ga_ref[...]
        pre = _dot(gav, wa_ref[...]) + ba_ref[...]
        dpre = dla_ref[...] * (1.0 - _sigmoid(pre)) * (1.0 / GLA_TAU)
        dga_ref[...] = _bf(_dot(dpre, wa_ref[...], 1, 1))
        pw = _dot(gav, dpre, 0, 0)
        pb = jnp.sum(dpre, axis=0, keepdims=True)

        @pl.when(i == 0)
        def _():
            dwa_ref[...] = pw
            dba_ref[...] = pb

        @pl.when(i > 0)
        def _():
            dwa_ref[...] += pw
            dba_ref[...] += pb

    return pl.pallas_call(
        body, name=name, grid=(s // tm,),
        in_specs=[pl.BlockSpec((tm, 512), lambda i: (i, 0)), pl.BlockSpec((tm, 128), lambda i: (i, col)),
                  pl.BlockSpec((128, 512), lambda i: (0, 0)), pl.BlockSpec((1, 512), lambda i: (0, 0))],
        out_specs=[pl.BlockSpec((tm, 128), lambda i: (i, 0)), pl.BlockSpec((128, 512), lambda i: (0, 0)),
                   pl.BlockSpec((1, 512), lambda i: (0, 0))],
        out_shape=[jax.ShapeDtypeStruct((s, 128), BF16), jax.ShapeDtypeStruct((128, 512), F32),
                   jax.ShapeDtypeStruct((1, 512), F32)],
        compiler_params=_cparams("arbitrary"),
    )(dla, z, wa, ba)


def _gla_masks(ch):
    ii = lax.broadcasted_iota(jnp.int32, (ch, ch), 0)
    tt = lax.broadcasted_iota(jnp.int32, (ch, ch), 1)
    return jnp.where(tt <= ii, 1.0, 0.0), jnp.where(tt >= ii, 1.0, 0.0)


def _running_sum(x, up):
    n = x.shape[0]
    rows = lax.broadcasted_iota(jnp.int32, x.shape, 0)
    k = 1
    while k < n:
        if up:
            x = x + jnp.where(rows < n - k, pltpu.roll(x, n - k, 0), 0.0)
        else:
            x = x + jnp.where(rows >= k, pltpu.roll(x, k, 0), 0.0)
        k *= 2
    return x


def _gla_chunk(d, tmat, qv, kv, lav, ch):
    c = _running_sum(lav, up=(d == 1))
    big_l = c[ch - 1:ch, :] if d == 0 else c[0:1, :]
    qt = qv * (GLA_DK ** -0.5) * jnp.exp(c)
    kt = kv * jnp.exp(-c)
    kh = kv * jnp.exp(big_l - c)
    return c, big_l, qt, kt, kh


def _gla_fwd(qh, kh_, z, la, *, name, rider=None):
    s = z.shape[0]
    ch = min(GLA_CHUNK, s)
    n = s // ch
    vcol = SEG["gv"][0] // GROUP_W

    def body(q0, k0, v0, la0, q1, k1, v1, la1, o0, o1, zs0, zs1, st):
        t = pl.program_id(0)

        @pl.when(t == 0)
        def _():
            st[...] = jnp.zeros_like(st)

        masks = _gla_masks(ch)
        for d, (q_ref, k_ref, v_ref, la_ref, o_ref, zs_ref) in enumerate(
                ((q0, k0, v0, la0, o0, zs0), (q1, k1, v1, la1, o1, zs1))):
            for h in range(GLA_HEADS):
                c, big_l, qt, kt, kh = _gla_chunk(d, masks[d], q_ref[h], k_ref[h], la_ref[0, h], ch)
                vv = v_ref[:, h * GLA_DV:(h + 1) * GLA_DV]
                p = _dot(qt, kt, 1, 1) * masks[d]
                zst = st[d, h]
                o_ref[:, h * GLA_DV:(h + 1) * GLA_DV] = _dot(p, vv) + _dot(qt, zst, 1, 1)
                zs_ref[h, 0] = zst
                st[d, h] = zst * jnp.exp(big_l) + _dot(vv, kh, 0, 0)

    cidx = (lambda t: t), (lambda t: n - 1 - t)
    hs = lambda d: pl.BlockSpec((GLA_HEADS, ch, GLA_DK), lambda t: (0, cidx[d](t), 0))
    vs = lambda d: pl.BlockSpec((ch, GROUP_W), lambda t: (cidx[d](t), vcol))
    las = lambda d: pl.BlockSpec((1, GLA_HEADS, ch, GLA_DK), lambda t: (d, 0, cidx[d](t), 0))
    os_ = lambda d: pl.BlockSpec((ch, GROUP_W), lambda t: (cidx[d](t), 0))
    zss = lambda d: pl.BlockSpec((GLA_HEADS, 1, GLA_DV, GLA_DK), lambda t: (0, cidx[d](t), 0, 0))
    (o0, o1, zs0, zs1), rode = _ride_call(
        body, rider, name=name, grid=(n,),
        in_specs=[hs(0), hs(0), vs(0), las(0), hs(1), hs(1), vs(1), las(1)],
        out_specs=[os_(0), os_(1), zss(0), zss(1)],
        out_shape=[jax.ShapeDtypeStruct((s, GROUP_W), F32)] * 2
        + [jax.ShapeDtypeStruct((GLA_HEADS, n, GLA_DV, GLA_DK), F32)] * 2,
        scratch_shapes=[pltpu.VMEM((2, GLA_HEADS, GLA_DV, GLA_DK), F32)],
        args=(qh, kh_, z, la, qh, kh_, z, la), sem=("arbitrary",))
    return ((o0, o1), (zs0, zs1)) if rider is None else ((o0, o1), (zs0, zs1), rode)


def _gla_bwd(qh, kh_, z, la, do, zs, *, name):
    s = z.shape[0]
    ch = min(GLA_CHUNK, s)
    n = s // ch
    vcol = SEG["gv"][0] // GROUP_W

    def body(q0, k0, v0, la0, do0, zs0, q1, k1, v1, la1, do1, zs1,
             dq0, dk0, dla0, dv0, dq1, dk1, dla1, dv1, gz):
        t = pl.program_id(0)

        @pl.when(t == 0)
        def _():
            gz[...] = jnp.zeros_like(gz)

        masks = _gla_masks(ch)
        rows = lax.broadcasted_iota(jnp.int32, (ch, 1), 0)
        for d, (q_ref, k_ref, v_ref, la_ref, do_ref, zs_ref, dq_ref, dk_ref, dla_ref, dv_ref) in enumerate(
                ((q0, k0, v0, la0, do0, zs0, dq0, dk0, dla0, dv0), (q1, k1, v1, la1, do1, zs1, dq1, dk1, dla1, dv1))):
            tmat = masks[d]
            end = ch - 1 if d == 0 else 0
            for h in range(GLA_HEADS):
                c, big_l, qt, kt, kh = _gla_chunk(d, tmat, q_ref[h], k_ref[h], la_ref[0, h], ch)
                vsl = slice(h * GLA_DV, (h + 1) * GLA_DV)
                vv, dov, zst, gzv = v_ref[:, vsl], do_ref[:, vsl], zs_ref[h, 0], gz[d, h]
                p = _dot(qt, kt, 1, 1) * tmat
                dp = _dot(dov, vv, 1, 1) * tmat
                dqt = _dot(dp, kt) + _dot(dov, zst)
                dkt = _dot(dp, qt, 0, 0)
                dkh = _dot(vv, gzv)
                dv_ref[:, vsl] = _dot(p, dov, 0, 0) + _dot(kh, gzv, 1, 1)
                dq_ref[h] = dqt * jnp.exp(c) * (GLA_DK ** -0.5)
                dk_ref[h] = dkt * jnp.exp(-c) + dkh * jnp.exp(big_l - c)
                e_l = jnp.exp(big_l)
                d_l = jnp.sum(dkh * kh, axis=0, keepdims=True) + e_l * jnp.sum(zst * gzv, axis=0, keepdims=True)
                dc = dqt * qt - dkt * kt - dkh * kh + jnp.where(rows == end, d_l, 0.0)
                dla_ref[h] = _running_sum(dc, up=(d == 0))
                gz[d, h] = gzv * e_l + _dot(dov, qt, 0, 0)

    cidx = (lambda t: n - 1 - t), (lambda t: t)
    hs = lambda d: pl.BlockSpec((GLA_HEADS, ch, GLA_DK), lambda t: (0, cidx[d](t), 0))
    vs = lambda d: pl.BlockSpec((ch, GROUP_W), lambda t: (cidx[d](t), vcol))
    las = lambda d: pl.BlockSpec((1, GLA_HEADS, ch, GLA_DK), lambda t: (d, 0, cidx[d](t), 0))
    row = lambda d: pl.BlockSpec((ch, GROUP_W), lambda t: (cidx[d](t), 0))
    zss = lambda d: pl.BlockSpec((GLA_HEADS, 1, GLA_DV, GLA_DK), lambda t: (0, cidx[d](t), 0, 0))
    hshape = jax.ShapeDtypeStruct((GLA_HEADS, s, GLA_DK), F32)
    wide = jax.ShapeDtypeStruct((s, GROUP_W), F32)
    outs = pl.pallas_call(
        body, name=name, grid=(n,),
        in_specs=[hs(0), hs(0), vs(0), las(0), row(0), zss(0), hs(1), hs(1), vs(1), las(1), row(1), zss(1)],
        out_specs=[hs(0), hs(0), hs(0), row(0), hs(1), hs(1), hs(1), row(1)],
        out_shape=[hshape, hshape, hshape, wide, hshape, hshape, hshape, wide],
        scratch_shapes=[pltpu.VMEM((2, GLA_HEADS, GLA_DV, GLA_DK), F32)],
        compiler_params=_cparams("arbitrary"),
    )(qh, kh_, z, la, do, zs[0], qh, kh_, z, la, do, zs[1])
    dq0, dk0, dla0, dv0, dq1, dk1, dla1, dv1 = outs
    return (dq0, dq1), (dk0, dk1), (dla0, dla1), (dv0, dv1)


def _band(lo, hi, rows, width):
    r = lax.broadcasted_iota(jnp.int32, (rows, width), 0)
    j = lax.broadcasted_iota(jnp.int32, (rows, width), 1)
    k = j - POOL_HALO - r
    return jnp.where((k >= lo) & (k <= hi), 1.0, 0.0)


def _pool_cnt(t0, half, rows, s):
    t = t0 + lax.broadcasted_iota(jnp.int32, (rows, 1), 0)
    return (jnp.minimum(t + half, s) - jnp.maximum(t - half, 0)).astype(F32)


def _pool_fwd(z, pw, scale, *, name):
    s = z.shape[0]
    tl = min(POOL_TILE, s)
    nt = s // tl
    ucol, gcol = SEG["pv"][0] // 128, SEG["pg"][0] // 128

    def body(u_ref, gt_ref, pw_ref, sc_ref, y_ref, pad):
        g = pl.program_id(0)
        half = jnp.left_shift(1, g)
        pad[0:POOL_HALO, :] = jnp.zeros((POOL_HALO, POOL_GW), F32)
        pad[POOL_HALO + s:POOL_HALO + s + POOL_HALO, :] = jnp.zeros((POOL_HALO, POOL_GW), F32)
        pad[POOL_HALO:POOL_HALO + s, :] = u_ref[...]
        band = _band(-half, half - 1, tl, tl + 2 * POOL_HALO)
        pwv, scv = pw_ref[0], sc_ref[...]

        def tile(i, carry):
            t0 = pl.multiple_of(i * tl, tl)
            win = pad[pl.ds(t0, tl + 2 * POOL_HALO), :]
            u = win[POOL_HALO:POOL_HALO + tl, :]
            pooled = _split_dot(band, win) / _pool_cnt(t0, half, tl, s) - u
            mixed = _dot(pooled, pwv)
            silu, _ = _silu_parts(gt_ref[pl.ds(t0, tl), :])
            y_ref[pl.ds(t0, tl), :] = _bf(silu * (mixed * scv))
            return carry

        lax.fori_loop(0, nt, tile, 0)

    return pl.pallas_call(
        body, name=name, grid=(POOL_GROUPS,),
        in_specs=[pl.BlockSpec((s, POOL_GW), lambda g: (0, ucol + g)),
                  pl.BlockSpec((s, POOL_GW), lambda g: (0, gcol + g)),
                  pl.BlockSpec((1, POOL_GW, POOL_GW), lambda g: (g, 0, 0)),
                  pl.BlockSpec((1, POOL_GW), lambda g: (0, g))],
        out_specs=pl.BlockSpec((s, POOL_GW), lambda g: (0, g)),
        out_shape=jax.ShapeDtypeStruct((s, GROUP_W), BF16),
        scratch_shapes=[pltpu.VMEM((s + 2 * POOL_HALO, POOL_GW), F32)],
        compiler_params=_cparams("parallel"),
    )(z, z, pw, scale)


def _pool_bwd(dy, z, pw, scale, *, name):
    s = z.shape[0]
    tl = min(POOL_TILE, s)
    nt = s // tl
    ucol, gcol, ycol = SEG["pv"][0] // 128, SEG["pg"][0] // 128, 2 * GROUP_W // 128

    def body(dy_ref, u_ref, gt_ref, pw_ref, sc_ref, du_ref, dgt_ref, dpw_ref, dsc_ref, pad, epad, dpo):
        g = pl.program_id(0)
        half = jnp.left_shift(1, g)
        zeros = jnp.zeros((POOL_HALO, POOL_GW), F32)
        for buf in (pad, epad):
            buf[0:POOL_HALO, :] = zeros
            buf[POOL_HALO + s:POOL_HALO + s + POOL_HALO, :] = zeros
        pad[POOL_HALO:POOL_HALO + s, :] = u_ref[...]
        band = _band(-half, half - 1, tl, tl + 2 * POOL_HALO)
        band_t = _band(1 - half, half, tl, tl + 2 * POOL_HALO)
        pwv, scv = pw_ref[0], sc_ref[...]
        dpw_ref[0] = jnp.zeros((POOL_GW, POOL_GW), F32)
        dsc_ref[...] = jnp.zeros((1, POOL_GW), F32)

        def tile(i, carry):
            t0 = pl.multiple_of(i * tl, tl)
            win = pad[pl.ds(t0, tl + 2 * POOL_HALO), :]
            u = win[POOL_HALO:POOL_HALO + tl, :]
            cnt = _pool_cnt(t0, half, tl, s)
            pooled = _split_dot(band, win) / cnt - u
            mixed = _dot(pooled, pwv)
            silu, dsilu = _silu_parts(gt_ref[pl.ds(t0, tl), :])
            dyv = dy_ref[pl.ds(t0, tl), :]
            dgt_ref[pl.ds(t0, tl), :] = _bf(dyv * (mixed * scv) * dsilu)
            dsc_ref[...] += jnp.sum(dyv * silu * mixed, axis=0, keepdims=True)
            dm = dyv * silu * scv
            dpw_ref[0] += _dot(pooled, dm, 0, 0)
            dpooled = _dot(dm, pwv, 1, 1)
            dpo[pl.ds(t0, tl), :] = dpooled
            epad[pl.ds(POOL_HALO + t0, tl), :] = dpooled / cnt
            return carry

        lax.fori_loop(0, nt, tile, 0)

        def tile2(i, carry):
            t0 = pl.multiple_of(i * tl, tl)
            ewin = epad[pl.ds(t0, tl + 2 * POOL_HALO), :]
            du_ref[pl.ds(t0, tl), :] = _bf(_split_dot(band_t, ewin) - dpo[pl.ds(t0, tl), :])
            return carry

        lax.fori_loop(0, nt, tile2, 0)

    col = lambda c0: pl.BlockSpec((s, POOL_GW), lambda g: (0, c0 + g))
    return pl.pallas_call(
        body, name=name, grid=(POOL_GROUPS,),
        in_specs=[col(ycol), col(ucol), col(gcol), pl.BlockSpec((1, POOL_GW, POOL_GW), lambda g: (g, 0, 0)),
                  pl.BlockSpec((1, POOL_GW), lambda g: (0, g))],
        out_specs=[col(0), col(0), pl.BlockSpec((1, POOL_GW, POOL_GW), lambda g: (g, 0, 0)),
                   pl.BlockSpec((1, POOL_GW), lambda g: (0, g))],
        out_shape=[jax.ShapeDtypeStruct((s, GROUP_W), BF16), jax.ShapeDtypeStruct((s, GROUP_W), BF16),
                   jax.ShapeDtypeStruct((POOL_GROUPS, POOL_GW, POOL_GW), F32),
                   jax.ShapeDtypeStruct((1, GROUP_W), F32)],
        scratch_shapes=[pltpu.VMEM((s + 2 * POOL_HALO, POOL_GW), F32), pltpu.VMEM((s + 2 * POOL_HALO, POOL_GW), F32),
                        pltpu.VMEM((s, POOL_GW), F32)],
        compiler_params=_cparams("parallel"),
    )(dy, z, z, pw, scale)


def _mla_specs(tm):
    zq = pl.BlockSpec((tm, 512), lambda i: (i, SEG["mq"][0] // 512))
    zkv = pl.BlockSpec((tm, 256), lambda i: (i, SEG["mkv"][0] // 256))
    zkr = pl.BlockSpec((tm, 128), lambda i: (i, SEG["mkr"][0] // 128))
    full = lambda r, c: pl.BlockSpec((r, c), lambda i: (0, 0))
    tab = pl.BlockSpec((tm, 128), lambda i: (i, 0))
    weights = [full(1, 512), full(512, 1024), full(1, 256), full(256, 1024), full(1, 256), full(1, 256)]
    return [zq, zkv, zkr] + weights + [tab, tab, tab]


def _mla_project(xq_ref, xkv_ref, qg_ref, wq_ref, kvg_ref, wkv_ref):
    xq = xq_ref[...]
    r1 = lax.rsqrt(jnp.mean(xq * xq, axis=-1, keepdims=True) + EPS)
    xn1 = xq * r1
    qn = _bf(xn1 * qg_ref[...])
    qraw = _dot(qn, wq_ref[...])
    xkv = xkv_ref[...]
    r2 = lax.rsqrt(jnp.mean(xkv * xkv, axis=-1, keepdims=True) + EPS)
    xn2 = xkv * r2
    kvn = _bf(xn2 * kvg_ref[...])
    kvraw = _dot(kvn, wkv_ref[...])
    return r1, xn1, qn, qraw, r2, xn2, kvn, kvraw


def _mla_pre(z, qg, wq, kvg, wkv, qng, kng, cos, sp, sn, *, name, tm=256):
    s = z.shape[0]
    tm = min(tm, s)

    def body(xq_ref, xkv_ref, pe_ref, qg_ref, wq_ref, kvg_ref, wkv_ref, qng_ref, kng_ref, c_ref, sp_ref, sn_ref,
             q_ref, k_ref, v_ref):
        _, _, _, qraw, _, _, _, kvraw = _mla_project(xq_ref, xkv_ref, qg_ref, wq_ref, kvg_ref, wkv_ref)
        c, spv, snv = c_ref[...], sp_ref[...], sn_ref[...]
        pe = pe_ref[...]
        pe_ss = jnp.sum(pe * pe, axis=-1, keepdims=True)
        qngv, kngv = qng_ref[...], kng_ref[...]
        for h in range(MLA_HEADS):
            b = h * MLA_QKP
            qh = qraw[:, b:b + MLA_QKP]
            r = lax.rsqrt(jnp.sum(qh * qh, axis=-1, keepdims=True) * (1.0 / MLA_QK) + EPS)
            qn_h = qh * r * qngv
            q_ref[:, b:b + 128] = _bf(qn_h[:, :128] * MLA_SCALE)
            q_ref[:, b + 128:b + 256] = _bf(_rope64(qn_h[:, 128:], c, spv, snv) * MLA_SCALE)
            kn = kvraw[:, b:b + 128]
            rk = lax.rsqrt((jnp.sum(kn * kn, axis=-1, keepdims=True) + pe_ss) * (1.0 / MLA_QK) + EPS)
            k_ref[:, b:b + 128] = _bf(kn * rk * kngv[:, :128])
            k_ref[:, b + 128:b + 256] = _bf(_rope64(pe * rk * kngv[:, 128:], c, spv, snv))
            v_ref[:, h * MLA_V:(h + 1) * MLA_V] = _bf(kvraw[:, b + 128:b + 256])

    row = lambda w: pl.BlockSpec((tm, w), lambda i: (i, 0))
    return pl.pallas_call(
        body, name=name, grid=(s // tm,), in_specs=_mla_specs(tm),
        out_specs=[row(1024), row(1024), row(512)],
        out_shape=[jax.ShapeDtypeStruct((s, 1024), BF16), jax.ShapeDtypeStruct((s, 1024), BF16),
                   jax.ShapeDtypeStruct((s, 512), BF16)],
        compiler_params=_cparams("parallel"),
    )(z, z, z, qg, wq, kvg, wkv, qng, kng, cos, sp, sn)


def _mla_pre_bwd(dq, dk, dv, z, qg, wq, kvg, wkv, qng, kng, cos, sp, sn, *, name, tm=256):
    s = z.shape[0]
    tm = min(tm, s)

    def body(dq_ref, dk_ref, dv_ref, xq_ref, xkv_ref, pe_ref, qg_ref, wq_ref, kvg_ref, wkv_ref, qng_ref, kng_ref,
             c_ref, sp_ref, sn_ref, dxq_ref, dxkv_ref, dpe_ref, dwq_ref, dwkv_ref, dqg_ref, dkvg_ref, dqng_ref,
             dkng_ref, dqraw, dkvraw):
        i = pl.program_id(0)
        r1, xn1, qn, qraw, r2, xn2, kvn, kvraw = _mla_project(xq_ref, xkv_ref, qg_ref, wq_ref, kvg_ref, wkv_ref)
        c, spv, snv = c_ref[...], sp_ref[...], sn_ref[...]
        pe = pe_ref[...]
        pe_ss = jnp.sum(pe * pe, axis=-1, keepdims=True)
        qngv, kngv = qng_ref[...], kng_ref[...]
        dqng = jnp.zeros((1, MLA_QKP), F32)
        dkng = jnp.zeros((1, MLA_QKP), F32)
        dpe = jnp.zeros_like(pe)
        for h in range(MLA_HEADS):
            b = h * MLA_QKP
            qh = qraw[:, b:b + MLA_QKP]
            r = lax.rsqrt(jnp.sum(qh * qh, axis=-1, keepdims=True) * (1.0 / MLA_QK) + EPS)
            xn = qh * r
            d_n = jnp.concatenate(
                [dq_ref[:, b:b + 128], _unrope64(dq_ref[:, b + 128:b + 256], c, spv, snv)], axis=1) * MLA_SCALE
            dqng = dqng + jnp.sum(d_n * xn, axis=0, keepdims=True)
            dxn = d_n * qngv
            dqraw[:, b:b + MLA_QKP] = _bf(r * (dxn - xn * (jnp.sum(dxn * xn, axis=-1, keepdims=True) * (1.0 / MLA_QK))))
            kn = kvraw[:, b:b + 128]
            rk = lax.rsqrt((jnp.sum(kn * kn, axis=-1, keepdims=True) + pe_ss) * (1.0 / MLA_QK) + EPS)
            xk = jnp.concatenate([kn, pe], axis=1) * rk
            d_k = jnp.concatenate(
                [dk_ref[:, b:b + 128], _unrope64(dk_ref[:, b + 128:b + 256], c, spv, snv)], axis=1)
            dkng = dkng + jnp.sum(d_k * xk, axis=0, keepdims=True)
            dxk = d_k * kngv
            dfull = rk * (dxk - xk * (jnp.sum(dxk * xk, axis=-1, keepdims=True) * (1.0 / MLA_QK)))
            dkvraw[:, b:b + 128] = _bf(dfull[:, :128])
            dkvraw[:, b + 128:b + 256] = _bf(dv_ref[:, h * MLA_V:(h + 1) * MLA_V])
            dpe = dpe + dfull[:, 128:]
        dpe_ref[...] = _bf(dpe)
        dqr, dkvr = dqraw[...], dkvraw[...]
        dqn = _dot(dqr, wq_ref[...], 1, 1)
        dxn1 = dqn * qg_ref[...]
        dxq_ref[...] = _bf(r1 * (dxn1 - xn1 * jnp.mean(dxn1 * xn1, axis=-1, keepdims=True)))
        dkvn = _dot(dkvr, wkv_ref[...], 1, 1)
        dxn2 = dkvn * kvg_ref[...]
        dxkv_ref[...] = _bf(r2 * (dxn2 - xn2 * jnp.mean(dxn2 * xn2, axis=-1, keepdims=True)))
        parts = (_dot(qn, dqr, 0, 0), _dot(kvn, dkvr, 0, 0), jnp.sum(dqn * xn1, axis=0, keepdims=True),
                 jnp.sum(dkvn * xn2, axis=0, keepdims=True), dqng, dkng)
        accs = (dwq_ref, dwkv_ref, dqg_ref, dkvg_ref, dqng_ref, dkng_ref)

        @pl.when(i == 0)
        def _():
            for a, p in zip(accs, parts):
                a[...] = p

        @pl.when(i > 0)
        def _():
            for a, p in zip(accs, parts):
                a[...] += p

    row = lambda w: pl.BlockSpec((tm, w), lambda i: (i, 0))
    full = lambda r, c: pl.BlockSpec((r, c), lambda i: (0, 0))
    return pl.pallas_call(
        body, name=name, grid=(s // tm,),
        in_specs=[row(1024), row(1024), row(512)] + _mla_specs(tm),
        out_specs=[row(512), row(256), row(128), full(512, 1024), full(256, 1024), full(1, 512), full(1, 256),
                   full(1, 256), full(1, 256)],
        out_shape=[jax.ShapeDtypeStruct((s, 512), BF16), jax.ShapeDtypeStruct((s, 256), BF16),
                   jax.ShapeDtypeStruct((s, 128), BF16), jax.ShapeDtypeStruct((512, 1024), F32),
                   jax.ShapeDtypeStruct((256, 1024), F32), jax.ShapeDtypeStruct((1, 512), F32),
                   jax.ShapeDtypeStruct((1, 256), F32), jax.ShapeDtypeStruct((1, 256), F32),
                   jax.ShapeDtypeStruct((1, 256), F32)],
        scratch_shapes=[pltpu.VMEM((tm, 1024), BF16), pltpu.VMEM((tm, 1024), BF16)],
        compiler_params=_cparams("arbitrary"),
    )(dq, dk, dv, z, z, z, qg, wq, kvg, wkv, qng, kng, cos, sp, sn)


def _flash_fwd(q, k, v, *, name, tq=1024, tk=1024, rider=None):
    s = q.shape[0]
    tq, tk = min(tq, s), min(tk, s)
    nk = s // tk
    strip = min(FLASH_STRIP, tq)

    def body(q_ref, k_ref, v_ref, o_ref, lse_ref, m_s, l_s, acc):
        j = pl.program_id(2)

        @pl.when(j == 0)
        def _():
            m_s[...] = jnp.full_like(m_s, -jnp.inf)
            l_s[...] = jnp.zeros_like(l_s)
            acc[...] = jnp.zeros_like(acc)

        for r in range(tq // strip):
            rows = slice(r * strip, (r + 1) * strip)
            sc = _dot(q_ref[rows, :], k_ref[...], 1, 1)
            m_prev = m_s[rows, :]
            m_new = jnp.maximum(m_prev, jnp.max(sc, axis=-1, keepdims=True))
            p = jnp.exp(sc - m_new[:, 0:1])
            alpha = jnp.exp(m_prev - m_new)
            l_s[rows, :] = alpha * l_s[rows, :] + jnp.sum(p, axis=-1, keepdims=True)
            acc[rows, :] = alpha * acc[rows, :] + _dot(p, v_ref[...])
            m_s[rows, :] = m_new

        @pl.when(j == nk - 1)
        def _():
            o_ref[...] = acc[...] / l_s[...]
            lse_ref[...] = m_s[...] + jnp.log(l_s[...])

    (o, lse), rode = _ride_call(
        body, rider, name=name, grid=(MLA_HEADS, s // tq, nk),
        in_specs=[pl.BlockSpec((tq, MLA_QKP), lambda h, i, j: (i, h)),
                  pl.BlockSpec((tk, MLA_QKP), lambda h, i, j: (j, h)),
                  pl.BlockSpec((tk, MLA_V), lambda h, i, j: (j, h))],
        out_specs=[pl.BlockSpec((tq, MLA_V), lambda h, i, j: (i, h))] * 2,
        out_shape=[jax.ShapeDtypeStruct((s, GROUP_W), F32)] * 2,
        scratch_shapes=[pltpu.VMEM((tq, MLA_V), F32), pltpu.VMEM((tq, MLA_V), F32), pltpu.VMEM((tq, MLA_V), F32)],
        args=(q, k, v), sem=("parallel", "parallel", "arbitrary"))
    return (o, lse) if rider is None else (o, lse, rode)


def _flash_bwd(q, k, v, do, o, lse, *, name, tq=1024, tk=1024, rider=None):
    s = q.shape[0]
    tq, tk = min(tq, s), min(tk, s)
    nq, nk = s // tq, s // tk

    def body(q_ref, k_ref, v_ref, do_ref, o_ref, lse_ref, dq_ref, dk_ref, dv_ref, dk_acc, dv_acc):
        j, i = pl.program_id(1), pl.program_id(2)
        dov = do_ref[...]
        delta = jnp.sum(dov * o_ref[...], axis=-1, keepdims=True)
        p = jnp.exp(_dot(q_ref[...], k_ref[...], 1, 1) - lse_ref[:, 0:1])
        ds = p * (_dot(dov, v_ref[...], 1, 1) - delta)
        pv = _dot(p, dov, 0, 0)
        pk = _dot(ds, q_ref[...], 0, 0)
        pq = _dot(ds, k_ref[...])
        rows = pl.ds(pl.multiple_of(i * tq, tq), tq)

        @pl.when(j == 0)
        def _():
            dq_ref[rows, :] = pq

        @pl.when(j > 0)
        def _():
            dq_ref[rows, :] += pq

        @pl.when(i == 0)
        def _():
            dv_acc[...] = pv
            dk_acc[...] = pk

        @pl.when(i > 0)
        def _():
            dv_acc[...] += pv
            dk_acc[...] += pk

        @pl.when(i == nq - 1)
        def _():
            dk_ref[...] = dk_acc[...]
            dv_ref[...] = dv_acc[...]

    qb = pl.BlockSpec((tq, MLA_QKP), lambda h, j, i: (i, h))
    kb = pl.BlockSpec((tk, MLA_QKP), lambda h, j, i: (j, h))
    vb = pl.BlockSpec((tk, MLA_V), lambda h, j, i: (j, h))
    ob = pl.BlockSpec((tq, MLA_V), lambda h, j, i: (i, h))
    (dq, dk, dv), rode = _ride_call(
        body, rider, name=name, grid=(MLA_HEADS, nk, nq),
        in_specs=[qb, kb, vb, ob, ob, ob],
        out_specs=[pl.BlockSpec((s, MLA_QKP), lambda h, j, i: (0, h)), kb, vb],
        out_shape=[jax.ShapeDtypeStruct((s, MLA_HEADS * MLA_QKP), F32),
                   jax.ShapeDtypeStruct((s, MLA_HEADS * MLA_QKP), F32), jax.ShapeDtypeStruct((s, GROUP_W), F32)],
        scratch_shapes=[pltpu.VMEM((tk, MLA_QKP), F32), pltpu.VMEM((tk, MLA_V), F32)],
        args=(q, k, v, do, o, lse), sem=("arbitrary", "arbitrary", "arbitrary"))
    return (dq, dk, dv) if rider is None else (dq, dk, dv, rode)


def _rows_tile(r, c, itemsize=4, budget=2 * 1024 * 1024):
    if r * c * itemsize <= budget:
        return r
    best = None
    for t in range(8, r, 8):
        if r % t == 0 and t * c * itemsize <= budget:
            best = t
    return best if best is not None else r


def _add_n(arrs, *, out_dtype=F32, name):
    shape = arrs[0].shape
    c = shape[-1]
    flat = [a.reshape(-1, c) for a in arrs]
    r = flat[0].shape[0]
    t = _rows_tile(r, c)

    def body(*refs):
        acc = refs[0][...].astype(F32)
        for ref in refs[1:-1]:
            acc = acc + ref[...].astype(F32)
        refs[-1][...] = acc.astype(out_dtype)

    blk = pl.BlockSpec((t, c), lambda i: (i, 0))
    out = pl.pallas_call(
        body, name=name, grid=(r // t,), in_specs=[blk] * len(flat), out_specs=blk,
        out_shape=jax.ShapeDtypeStruct((r, c), out_dtype), compiler_params=_cparams("parallel"),
    )(*flat)
    return out.reshape(shape)


def _adamw(w, g, m, v, *, name):
    shape = w.shape
    c = shape[-1]
    flat = [a.reshape(-1, c) for a in (w, g, m, v)]
    r = flat[0].shape[0]
    t = _rows_tile(r, c, budget=1024 * 1024)

    def body(w_ref, g_ref, m_ref, v_ref, d_ref, mo_ref, vo_ref):
        gv = g_ref[...]
        m2 = ADAM_B1 * m_ref[...] + (1.0 - ADAM_B1) * gv
        v2 = ADAM_B2 * v_ref[...] + (1.0 - ADAM_B2) * (gv * gv)
        m_hat = m2 / (1.0 - ADAM_B1 ** ADAM_STEP)
        v_hat = v2 / (1.0 - ADAM_B2 ** ADAM_STEP)
        d_ref[...] = -ADAM_LR * (m_hat / (jnp.sqrt(v_hat) + ADAM_EPS) + ADAM_WD * w_ref[...])
        mo_ref[...] = m2
        vo_ref[...] = v2

    blk = pl.BlockSpec((t, c), lambda i: (i, 0))
    outs = pl.pallas_call(
        body, name=name, grid=(r // t,), in_specs=[blk] * 4, out_specs=[blk] * 3,
        out_shape=[jax.ShapeDtypeStruct((r, c), F32)] * 3, compiler_params=_cparams("parallel"),
    )(*flat)
    return tuple(o.reshape(shape) for o in outs)


def _place():
    x, y, c = lax.axis_index("x"), lax.axis_index("y"), lax.axis_index("c")
    chips = [(1 - x, y), (x, 1 - y), (1 - x, 1 - y)]
    return x, y, c, chips


ANY = pl.BlockSpec(memory_space=pl.ANY)


def _half(ref, axis, hc, lead=()):
    n = ref.shape[len(lead) + axis] // 2
    return ref.at[tuple(lead) + (slice(None),) * axis + (pl.ds(hc * n, n),)]


def _gather_shards(shards, axes, *, name):
    nt = len(shards)

    def body(*refs):
        src, dst = refs[:nt], refs[nt:2 * nt]
        send, recv, fsend, frecv, lsem = refs[2 * nt:]
        x, y, c, chips = _place()
        me = 2 * x + y
        local = [pltpu.make_async_copy(src[t], dst[t].at[me], lsem.at[t]) for t in range(nt)]
        for cp in local:
            cp.start()

        def half(t, slot, hc):
            return _half(dst[t], axes[t], hc, lead=(slot,))

        def first(t, k):
            return pltpu.make_async_remote_copy(
                src_ref=_half(src[t], axes[t], c), dst_ref=half(t, me, c),
                send_sem=send.at[t, k], recv_sem=recv.at[t, k],
                device_id=(chips[k][0], chips[k][1], c), device_id_type=MESH)

        def landed(t, k):
            slot = 2 * chips[k][0] + chips[k][1]
            return pltpu.make_async_remote_copy(
                src_ref=half(t, slot, c), dst_ref=half(t, slot, c),
                send_sem=send.at[t, k], recv_sem=recv.at[t, k],
                device_id=(chips[k][0], chips[k][1], c), device_id_type=MESH)

        def forward(t, k, hc):
            slot = 2 * chips[k][0] + chips[k][1]
            return pltpu.make_async_remote_copy(
                src_ref=half(t, slot, hc), dst_ref=half(t, slot, hc),
                send_sem=fsend.at[t, k], recv_sem=frecv.at[t, k],
                device_id=(x, y, 1 - c), device_id_type=MESH)

        for t in range(nt):
            for k in range(3):
                first(t, k).start()
        for t in range(nt):
            for k in range(3):
                landed(t, k).wait_recv()
                forward(t, k, c).start()
        for t in range(nt):
            for k in range(3):
                forward(t, k, 1 - c).wait_recv()
        for t in range(nt):
            for k in range(3):
                first(t, k).wait_send()
                forward(t, k, c).wait_send()
        for cp in local:
            cp.wait()

    return pl.pallas_call(
        body, name=name, in_specs=[ANY] * nt, out_specs=[ANY] * nt,
        out_shape=[jax.ShapeDtypeStruct((N_CHIP,) + a.shape, a.dtype) for a in shards],
        scratch_shapes=[pltpu.SemaphoreType.DMA((nt, 3)), pltpu.SemaphoreType.DMA((nt, 3)),
                        pltpu.SemaphoreType.DMA((nt, 3)), pltpu.SemaphoreType.DMA((nt, 3)),
                        pltpu.SemaphoreType.DMA((nt,))],
    )(*shards)


def _comm_rows(hr, c, budget=2 * 1024 * 1024):
    if hr * c * 4 <= budget:
        return hr
    best = None
    for t in range(16, hr, 16):
        if hr % t == 0 and t * c * 4 <= budget:
            best = t
    return best if best is not None else hr


def _comm_cols(r, hc, budget=2 * 1024 * 1024):
    best = 128
    for t in range(128, hc + 1, 128):
        if hc % t == 0 and r * t * 4 <= budget:
            best = t
    return best


def _comm_chunks(shape, axis):
    r, cdim = shape
    if axis == 0:
        rc = _comm_rows(r // 2, cdim)
        nt = (r // 2) // rc
        return (rc, cdim), nt, (lambda h, t: (h * nt + t, 0))
    cc = _comm_cols(r, cdim // 2)
    nt = (cdim // 2) // cc
    return (r, cc), nt, (lambda h, t: (0, h * nt + t))


def _pair_reduce(g, where, axis, *, out_dtype, name):
    n_slot, r, cdim = g.shape
    blk_shape, nr, at = _comm_chunks((r, cdim), axis)
    steps = n_slot * nr
    half_shape = (r // 2, cdim) if axis == 0 else (r, cdim // 2)

    def body(w_ref, a_ref, b_ref, o_ref, land, send, recv, credit):
        x, y, c, _ = _place()
        sib = (x, y, 1 - c)
        i = pl.program_id(0) * nr + pl.program_id(1)
        s = lax.rem(i, 2)

        @pl.when(i >= 2)
        def _():
            pl.semaphore_wait(credit.at[s], 1)

        cp = pltpu.make_async_remote_copy(src_ref=b_ref.at[0], dst_ref=land.at[s], send_sem=send.at[s],
                                          recv_sem=recv.at[s], device_id=sib, device_id_type=MESH)
        cp.start()
        cp.wait_recv()
        o_ref[0] = (a_ref[0] + land[s]).astype(out_dtype)
        cp.wait_send()

        @pl.when(i + 2 < steps)
        def _():
            pl.semaphore_signal(credit.at[s], inc=1, device_id=sib, device_id_type=MESH)

    blk = lambda half: pl.BlockSpec((1,) + blk_shape, lambda j, t, w: (j,) + at(half(w), t))
    grid_spec = pltpu.PrefetchScalarGridSpec(
        num_scalar_prefetch=1, grid=(n_slot, nr),
        in_specs=[blk(lambda w: w[0]), blk(lambda w: 1 - w[0])],
        out_specs=pl.BlockSpec((1,) + blk_shape, lambda j, t, w: (j,) + at(0, t)),
        scratch_shapes=[pltpu.VMEM((2,) + blk_shape, F32), pltpu.SemaphoreType.DMA((2,)),
                        pltpu.SemaphoreType.DMA((2,)), pltpu.SemaphoreType.REGULAR((2,))])
    return pl.pallas_call(
        body, name=name, grid_spec=grid_spec, out_shape=jax.ShapeDtypeStruct((n_slot,) + half_shape, out_dtype),
        compiler_params=_cparams("arbitrary", "arbitrary"),
    )(where, g, g)


def _chip_exchange(parts, *, name):
    nt = len(parts)

    def body(*refs):
        src, got = refs[:nt], refs[nt:2 * nt]
        send, recv = refs[2 * nt:]
        x, y, c, chips = _place()
        remote = []
        for t in range(nt):
            for k in range(3):
                remote.append(pltpu.make_async_remote_copy(
                    src_ref=src[t].at[2 * chips[k][0] + chips[k][1]], dst_ref=got[t].at[k],
                    send_sem=send.at[t, k], recv_sem=recv.at[t, k],
                    device_id=(chips[k][0], chips[k][1], c), device_id_type=MESH))
        for cp in remote:
            cp.start()
        for cp in remote:
            cp.wait_recv()
        for cp in remote:
            cp.wait_send()

    return pl.pallas_call(
        body, name=name, in_specs=[ANY] * nt, out_specs=[ANY] * nt,
        out_shape=[jax.ShapeDtypeStruct((3,) + a.shape[1:], a.dtype) for a in parts],
        scratch_shapes=[pltpu.SemaphoreType.DMA((nt, 3)), pltpu.SemaphoreType.DMA((nt, 3))],
    )(*parts)


def _sum_join(p, got, where, axis, *, name):
    _, hr, cdim = p.shape
    full = (2 * hr, cdim) if axis == 0 else (hr, 2 * cdim)
    blk_shape, n, at = _comm_chunks(full, axis)
    step_len = blk_shape[axis]
    half_len = full[axis] // 2

    def body(w_ref, p_ref, g_ref, out, buf, lsem, ssem, rsem):
        x, y, c, _ = _place()
        sib = (x, y, 1 - c)
        r = pl.program_id(0)

        def part(start, size):
            return out.at[(slice(None),) * axis + (pl.ds(start, size),)]

        def copies(step, slot):
            rows = part(pl.multiple_of(c * half_len + step * step_len, 8 if axis == 0 else 128), step_len)
            return (pltpu.make_async_copy(buf.at[slot], rows, lsem.at[slot]),
                    pltpu.make_async_remote_copy(src_ref=buf.at[slot], dst_ref=rows, send_sem=ssem.at[slot],
                                                 recv_sem=rsem, device_id=sib, device_id_type=MESH))

        s = lax.rem(r, 2)

        @pl.when(r >= 2)
        def _():
            lc, rm = copies(r - 2, s)
            lc.wait()
            rm.wait_send()

        buf[s] = p_ref[0].astype(F32) + g_ref[0].astype(F32) + g_ref[1].astype(F32) + g_ref[2].astype(F32)
        lc, rm = copies(r, s)
        lc.start()
        rm.start()

        @pl.when(r == n - 1)
        def _():
            for step in range(max(0, n - 2), n):
                lc, rm = copies(step, step % 2)
                lc.wait()
                rm.wait_send()
            whole = part(0, half_len)
            pltpu.make_async_remote_copy(src_ref=whole, dst_ref=whole, send_sem=ssem.at[0], recv_sem=rsem,
                                         device_id=sib, device_id_type=MESH).wait_recv()

    grid_spec = pltpu.PrefetchScalarGridSpec(
        num_scalar_prefetch=1, grid=(n,),
        in_specs=[pl.BlockSpec((1,) + blk_shape, lambda t, w: (w[1],) + at(0, t)),
                  pl.BlockSpec((3,) + blk_shape, lambda t, w: (0,) + at(0, t))],
        out_specs=ANY,
        scratch_shapes=[pltpu.VMEM((2,) + blk_shape, F32), pltpu.SemaphoreType.DMA((2,)),
                        pltpu.SemaphoreType.DMA((2,)), pltpu.SemaphoreType.DMA])
    return pl.pallas_call(
        body, name=name, grid_spec=grid_spec, out_shape=jax.ShapeDtypeStruct(full, F32),
        compiler_params=_cparams("arbitrary"),
    )(where, p, got)


def _rider_gather_send(shards, axes):
    nt = len(shards)

    def copies(src, dst, send, recv, lsem):
        x, y, c, chips = _place()
        me = 2 * x + y
        local = [pltpu.make_async_copy(src[t], dst[t].at[me], lsem.at[t]) for t in range(nt)]
        out, landed = [], []
        for t in range(nt):
            for k in range(3):
                peer = (chips[k][0], chips[k][1], c)
                out.append(pltpu.make_async_remote_copy(
                    src_ref=_half(src[t], axes[t], c), dst_ref=_half(dst[t], axes[t], c, lead=(me,)),
                    send_sem=send.at[t, k], recv_sem=recv.at[t, k], device_id=peer, device_id_type=MESH))
                theirs = _half(dst[t], axes[t], c, lead=(2 * chips[k][0] + chips[k][1],))
                landed.append(pltpu.make_async_remote_copy(
                    src_ref=theirs, dst_ref=theirs, send_sem=send.at[t, k], recv_sem=recv.at[t, k],
                    device_id=peer, device_id_type=MESH))
        return local, out, landed

    def start(src, dst, sems):
        local, out, _ = copies(src, dst, *sems)
        for cp in local + out:
            cp.start()

    def finish(src, dst, sems):
        local, out, landed = copies(src, dst, *sems)
        for cp in landed:
            cp.wait_recv()
        for cp in out:
            cp.wait_send()
        for cp in local:
            cp.wait()

    return _Rider(shards, [jax.ShapeDtypeStruct((N_CHIP,) + a.shape, a.dtype) for a in shards],
                  [pltpu.SemaphoreType.DMA((nt, 3)), pltpu.SemaphoreType.DMA((nt, 3)), pltpu.SemaphoreType.DMA((nt,))],
                  start, finish)


def _rider_gather_forward(bufs, axes):
    nt = len(bufs)

    def copies(src, dst, send, recv):
        x, y, c, chips = _place()
        mine, theirs = [], []
        for t in range(nt):
            for k in range(3):
                slot = 2 * chips[k][0] + chips[k][1]
                for hc, into in ((c, mine), (1 - c, theirs)):
                    into.append(pltpu.make_async_remote_copy(
                        src_ref=_half(src[t], axes[t], hc, lead=(slot,)),
                        dst_ref=_half(dst[t], axes[t], hc, lead=(slot,)),
                        send_sem=send.at[t, k], recv_sem=recv.at[t, k], device_id=(x, y, 1 - c), device_id_type=MESH))
        return mine, theirs

    def start(src, dst, sems):
        for cp in copies(src, dst, *sems)[0]:
            cp.start()

    def finish(src, dst, sems):
        mine, theirs = copies(src, dst, *sems)
        for cp in theirs:
            cp.wait_recv()
        for cp in mine:
            cp.wait_send()

    return _Rider(bufs, [jax.ShapeDtypeStruct(a.shape, a.dtype) for a in bufs],
                  [pltpu.SemaphoreType.DMA((nt, 3)), pltpu.SemaphoreType.DMA((nt, 3))], start, finish,
                  aliases={t: t for t in range(nt)})


def _rider_chip_exchange(parts):
    nt = len(parts)

    def copies(src, got, send, recv):
        x, y, c, chips = _place()
        return [pltpu.make_async_remote_copy(
            src_ref=src[t].at[2 * chips[k][0] + chips[k][1]], dst_ref=got[t].at[k], send_sem=send.at[t, k],
            recv_sem=recv.at[t, k], device_id=(chips[k][0], chips[k][1], c), device_id_type=MESH)
            for t in range(nt) for k in range(3)]

    def start(src, got, sems):
        for cp in copies(src, got, *sems):
            cp.start()

    def finish(src, got, sems):
        remote = copies(src, got, *sems)
        for cp in remote:
            cp.wait_recv()
        for cp in remote:
            cp.wait_send()

    return _Rider(parts, [jax.ShapeDtypeStruct((3,) + a.shape[1:], a.dtype) for a in parts],
                  [pltpu.SemaphoreType.DMA((nt, 3)), pltpu.SemaphoreType.DMA((nt, 3))], start, finish)


def _gather_all(block, *, name):
    m_per, n = block.shape

    def body(x_ref, out_ref, send_sems, recv_sems, local_sem):
        x, y, c, chips = _place()
        me, sibling = (x, y, c), (x, y, 1 - c)

        def rows(px, py, pc):
            return out_ref.at[4 * px + 2 * py + pc]

        def copy(k, blk, to, src=None):
            return pltpu.make_async_remote_copy(
                src_ref=rows(*blk) if src is None else src, dst_ref=rows(*blk),
                send_sem=send_sems.at[k], recv_sem=recv_sems.at[k], device_id=to, device_id_type=MESH)

        mine = pltpu.make_async_copy(x_ref, rows(*me), local_sem)
        mine.start()
        first = [copy(0, me, sibling, src=x_ref)]
        first += [copy(1 + j, me, (*chip, c), src=x_ref) for j, chip in enumerate(chips)]
        for cp in first:
            cp.start()
        passed = [copy(4 + j, (*chip, c), sibling) for j, chip in enumerate(chips)]
        for j, chip in enumerate(chips):
            copy(1 + j, (*chip, c), me).wait_recv()
            passed[j].start()
        copy(0, sibling, me).wait_recv()
        for j, chip in enumerate(chips):
            copy(4 + j, (*chip, 1 - c), me).wait_recv()
        for cp in first + passed:
            cp.wait_send()
        mine.wait()

    return pl.pallas_call(
        body, name=name,
        out_shape=jax.ShapeDtypeStruct((N_DEV, m_per, n), block.dtype),
        in_specs=[pl.BlockSpec(memory_space=pltpu.VMEM)], out_specs=pl.BlockSpec(memory_space=pltpu.VMEM),
        scratch_shapes=[pltpu.SemaphoreType.DMA((7,)), pltpu.SemaphoreType.DMA((7,)), pltpu.SemaphoreType.DMA],
        compiler_params=pltpu.CompilerParams(vmem_limit_bytes=VMEM_LIMIT),
    )(block)


def _sum_slots(slots, *, name):
    n, m, c = slots.shape
    t = _rows_tile(m, c * n)

    def body(s_ref, o_ref):
        acc = s_ref[0]
        for k in range(1, n):
            acc = acc + s_ref[k]
        o_ref[...] = acc

    return pl.pallas_call(
        body, name=name, grid=(m // t,), in_specs=[pl.BlockSpec((n, t, c), lambda i: (0, i, 0))],
        out_specs=pl.BlockSpec((t, c), lambda i: (i, 0)), out_shape=jax.ShapeDtypeStruct((m, c), F32),
        compiler_params=_cparams("parallel"),
    )(slots)


def _pad_rows(a, rows):
    return a if a.shape[0] == rows else jnp.pad(a, ((0, rows - a.shape[0]), (0, 0)))


def _w_in_padded(shards):
    full = shards.reshape(IN_COLS, shards.shape[2])
    return jnp.concatenate([_pad_rows(full[SEG[n][2]:SEG[n][2] + SEG[n][3]], SEG[n][1]) for n in SEG_ORDER], axis=0)


def _w_in_unpadded(gp):
    full = jnp.concatenate([gp[SEG[n][0]:SEG[n][0] + SEG[n][3]] for n in ORIG_ORDER], axis=0)
    return full.reshape(N_CHIP, IN_COLS // N_CHIP, gp.shape[1])


def _pad_heads(w, true_w, pad_w):
    r = w.shape[0]
    h = w.shape[1] // true_w
    return jnp.pad(w.reshape(r, h, true_w), ((0, 0), (0, 0), (0, pad_w - true_w))).reshape(r, h * pad_w)


def _unpad_heads(w, true_w, pad_w):
    r = w.shape[0]
    h = w.shape[1] // pad_w
    return w.reshape(r, h, pad_w)[:, :, :true_w].reshape(r, h * true_w)


def _cols_to_slots(a):
    return a.reshape(a.shape[0], N_CHIP, a.shape[1] // N_CHIP).transpose(1, 0, 2)


def _slots_to_cols(a):
    return jnp.concatenate([a[j] for j in range(N_CHIP)], axis=1)


def _to_heads(a, h, d):
    return a.reshape(a.shape[0], h, d).transpose(1, 0, 2)


def _from_heads(a):
    return a.transpose(1, 0, 2).reshape(a.shape[1], -1)


SMALL = [("norm_g", 2048), ("ret_norm_g", 512), ("gla_ba_f", 256), ("gla_ba_b", 256), ("gla_norm_g", 512),
         ("pool_w", 4 * 128 * 128), ("pool_scale", 512), ("mla_q_norm_g", 512), ("mla_kv_norm_g", 256),
         ("mla_qk_norm_q", 192), ("mla_qk_norm_k", 192)]


def _pack_small(vals):
    parts = []
    for name, n in SMALL:
        parts += [v.reshape(-1) for v in vals[name]]
        if (DEPTH * n) % 1024:
            parts.append(jnp.zeros((-(DEPTH * n)) % 1024, F32))
    parts += [vals["loss"].reshape(-1), jnp.zeros(1023, F32)]
    return jnp.concatenate(parts).reshape(-1, 128)


def _unpack_small(block):
    flat = block.reshape(-1)
    out, off = {}, 0
    for name, n in SMALL:
        out[name] = flat[off:off + DEPTH * n]
        off += DEPTH * n + (-(DEPTH * n)) % 1024
    out["loss"] = flat[off]
    return out


def _layer_weights(l, p, g):
    wa = jnp.zeros((128, 512), F32)
    wa = wa.at[0:GLA_RANK, 0:256].set(_slots_to_cols(g["gla_wa2_f"]))
    wa = wa.at[GLA_RANK:2 * GLA_RANK, 256:512].set(_slots_to_cols(g["gla_wa2_b"]))
    return dict(
        norm_g=p["norm_g"][l][None, :],
        w_in=_w_in_padded(g["w_in"]),
        w_out=g["w_out"].reshape(4 * g["w_out"].shape[1], -1),
        ret_norm_g=p["ret_norm_g"][l][None, :],
        wa=_bf(wa),
        ba=jnp.concatenate([p["gla_ba_f"][l], p["gla_ba_b"][l]])[None, :],
        gla_norm_g=p["gla_norm_g"][l][None, :],
        pool_w=_bf(p["pool_w"][l]),
        pool_scale=p["pool_scale"][l][None, :],
        qg=p["mla_q_norm_g"][l][None, :],
        wq=_pad_heads(_slots_to_cols(g["mla_wq_b"]), MLA_QK, MLA_QKP),
        kvg=p["mla_kv_norm_g"][l][None, :],
        wkv=_slots_to_cols(g["mla_wkv_b"]),
        qng=jnp.pad(p["mla_qk_norm_q"][l], (0, MLA_QKP - MLA_QK))[None, :],
        kng=jnp.pad(p["mla_qk_norm_k"][l], (0, MLA_QKP - MLA_QK))[None, :],
    )


def _layer_fwd(l, x, w, tabs, next_shards=None):
    ret_cos, ret_sin, mla_cos, mla_sp, mla_sn = tabs
    nm = lambda s: f"l{l}_{s}"
    h = _rmsnorm_fwd(x, w["norm_g"], name=nm("norm"))
    if next_shards is None:
        z = _matmul(h, w["w_in"], tb=True, name=nm("in_proj"))
    else:
        z, landed = _matmul(h, w["w_in"], tb=True, rider=_rider_gather_send(next_shards[:1], SHARD_AXES[:1]),
                            name=nm("in_proj"))
    qr, kr = _ret_pre(z, ret_cos, ret_sin, name=nm("ret_pre"))
    ret_o = _bla(qr, kr, z, _ret_log_gamma(False), (0, 0, SEG["rv"][0] // 512), name=nm("ret_scan"))
    y_a = _post(ret_o, z, SEG["rg"][0] // 512, w["ret_norm_g"], norm=True, name=nm("ret_post"))
    la = _gla_gate(z, w["wa"], w["ba"], name=nm("gla_gate"))
    la_h = la.reshape(la.shape[0], 2, GLA_HEADS, GLA_DK).transpose(1, 2, 0, 3)
    gq = _to_heads(z[:, SEG["gq"][0]:SEG["gq"][0] + 256], GLA_HEADS, GLA_DK)
    gk = _to_heads(z[:, SEG["gk"][0]:SEG["gk"][0] + 256], GLA_HEADS, GLA_DK)
    if next_shards is None:
        gla_o, gla_st = _gla_fwd(gq, gk, z, la_h, name=nm("gla_scan"))
    else:
        gla_o, gla_st, more = _gla_fwd(gq, gk, z, la_h, rider=_rider_gather_send(next_shards[1:], SHARD_AXES[1:]),
                                       name=nm("gla_scan"))
        landed = list(landed) + list(more)
    y_b = _post(gla_o, z, SEG["gg"][0] // 512, w["gla_norm_g"], norm=True, name=nm("gla_post"))
    y_c = _pool_fwd(z, w["pool_w"], w["pool_scale"], name=nm("pool"))
    q, k, v = _mla_pre(z, w["qg"], w["wq"], w["kvg"], w["wkv"], w["qng"], w["kng"], mla_cos, mla_sp, mla_sn,
                       name=nm("mla_pre"))
    if next_shards is None:
        (att_o, lse), gathered = _flash_fwd(q, k, v, name=nm("attn")), None
    else:
        att_o, lse, gathered = _flash_fwd(q, k, v, rider=_rider_gather_forward(landed, SHARD_AXES), name=nm("attn"))
    y_d = _post([att_o], z, SEG["mg"][0] // 512, w["qg"], norm=False, name=nm("mla_post"))
    y = jnp.concatenate([y_a, y_b, y_c, y_d], axis=1)
    x_next = _matmul(y, w["w_out"], add=x, name=nm("out_proj"))
    saved = dict(x=x, h=h, z=z, y=y, qr=qr, kr=kr, ret_o=ret_o, la_h=la_h, gq=gq, gk=gk, gla_o=gla_o, gla_st=gla_st,
                 q=q, k=k, v=v, att_o=att_o, lse=lse)
    return x_next, saved, gathered


def _layer_bwd(l, dx_next, w, sv, tabs, riding_parts=None):
    ret_cos, ret_sin, mla_cos, mla_sp, mla_sn = tabs
    nm = lambda s: f"l{l}_{s}"
    z = sv["z"]
    dy = _matmul(dx_next, w["w_out"], tb=True, name=nm("out_proj_dy"))
    d_w_out = _matmul(sv["y"], dx_next, ta=True, tn=512, name=nm("out_proj_dw"))
    d_rg, d_ret_o, d_ret_g = _post_bwd(dy, 0, sv["ret_o"], z, SEG["rg"][0] // 512, w["ret_norm_g"], norm=True,
                                       name=nm("ret_post_bwd"))
    vcol = SEG["rv"][0] // 512
    dqr = _bla(d_ret_o, z, sv["kr"], _ret_log_gamma(False), (0, vcol, 0), name=nm("ret_scan_dq"))
    dkr = _bla(z, d_ret_o, sv["qr"], _ret_log_gamma(True), (vcol, 0, 0), name=nm("ret_scan_dk"))
    drv = _bla(sv["kr"], sv["qr"], d_ret_o, _ret_log_gamma(True), (0, 0, 0), name=nm("ret_scan_dv"))
    d_rq, d_rk = _ret_pre_bwd(dqr, dkr, ret_cos, ret_sin, name=nm("ret_pre_bwd"))
    d_rv = _add_n([drv[0], drv[1]], out_dtype=BF16, name=nm("ret_dv_sum"))
    d_gg, d_gla_o, d_gla_g = _post_bwd(dy, 1, sv["gla_o"], z, SEG["gg"][0] // 512, w["gla_norm_g"], norm=True,
                                       name=nm("gla_post_bwd"))
    dq2, dk2, dla2, dv2 = _gla_bwd(sv["gq"], sv["gk"], z, sv["la_h"], d_gla_o, sv["gla_st"], name=nm("gla_scan_bwd"))
    d_gq = _bf(_from_heads(dq2[0] + dq2[1]))
    d_gk = _bf(_from_heads(dk2[0] + dk2[1]))
    d_gv = _add_n([dv2[0], dv2[1]], out_dtype=BF16, name=nm("gla_dv_sum"))
    dla = jnp.concatenate([_from_heads(dla2[0]), _from_heads(dla2[1])], axis=1)
    d_ga, d_wa, d_ba = _gla_gate_bwd(dla, z, w["wa"], w["ba"], name=nm("gla_gate_bwd"))
    d_pv, d_pg, d_pool_w, d_pool_scale = _pool_bwd(dy, z, w["pool_w"], w["pool_scale"], name=nm("pool_bwd"))
    d_mg, d_att_o, _ = _post_bwd(dy, 3, [sv["att_o"]], z, SEG["mg"][0] // 512, w["qg"], norm=False,
                                 name=nm("mla_post_bwd"))
    if riding_parts is None:
        (dq, dk, dv), rode = _flash_bwd(sv["q"], sv["k"], sv["v"], d_att_o, sv["att_o"], sv["lse"],
                                        name=nm("attn_bwd")), None
    else:
        dq, dk, dv, rode = _flash_bwd(sv["q"], sv["k"], sv["v"], d_att_o, sv["att_o"], sv["lse"],
                                      rider=_rider_chip_exchange(riding_parts), name=nm("attn_bwd"))
    d_mq, d_mkv, d_mkr, d_wq, d_wkv, d_qg, d_kvg, d_qng, d_kng = _mla_pre_bwd(
        dq, dk, dv, z, w["qg"], w["wq"], w["kvg"], w["wkv"], w["qng"], w["kng"], mla_cos, mla_sp, mla_sn,
        name=nm("mla_pre_bwd"))
    segs = dict(rq=d_rq, rk=d_rk, rv=d_rv, rg=d_rg, gv=d_gv, gg=d_gg, pv=d_pv, pg=d_pg, mq=d_mq, mg=d_mg,
                gq=d_gq, gk=d_gk, mkv=d_mkv, ga=d_ga, mkr=d_mkr)
    dz = jnp.concatenate([segs[n] for n in SEG_ORDER], axis=1)
    dh = _matmul(dz, w["w_in"], tn=512, name=nm("in_proj_dh"))
    d_w_in = _matmul(dz, sv["h"], ta=True, name=nm("in_proj_dw"))
    dx, d_norm_g = _rmsnorm_bwd(sv["x"], dh, w["norm_g"], dx_next, name=nm("norm_bwd"))
    sharded = dict(
        w_in=_w_in_unpadded(d_w_in),
        w_out=d_w_out.reshape(N_CHIP, d_w_out.shape[0] // N_CHIP, d_w_out.shape[1]),
        mla_wq_b=_cols_to_slots(_unpad_heads(d_wq, MLA_QK, MLA_QKP)),
        mla_wkv_b=_cols_to_slots(d_wkv),
        gla_wa2_f=_cols_to_slots(d_wa[0:GLA_RANK, 0:256]),
        gla_wa2_b=_cols_to_slots(d_wa[GLA_RANK:2 * GLA_RANK, 256:512]),
    )
    small = dict(
        norm_g=d_norm_g[0], ret_norm_g=d_ret_g[0], gla_ba_f=d_ba[0, :256], gla_ba_b=d_ba[0, 256:],
        gla_norm_g=d_gla_g[0], pool_w=d_pool_w.reshape(-1), pool_scale=d_pool_scale[0], mla_q_norm_g=d_qg[0],
        mla_kv_norm_g=d_kvg[0], mla_qk_norm_q=d_qng[0, :MLA_QK], mla_qk_norm_k=d_kng[0, :MLA_QK],
    )
    return dx, sharded, small, rode


SHARDED = ["w_in", "w_out", "mla_wq_b", "mla_wkv_b", "gla_wa2_f", "gla_wa2_b"]
WEIGHTS = ["norm_g", "w_in", "ret_norm_g", "gla_wa2_f", "gla_ba_f", "gla_wa2_b", "gla_ba_b", "gla_norm_g", "pool_w",
           "pool_scale", "mla_q_norm_g", "mla_wq_b", "mla_kv_norm_g", "mla_wkv_b", "mla_qk_norm_q", "mla_qk_norm_k",
           "w_out"]


SHARD_AXES = [1, 0, 0, 0, 0, 0]


def _layer_shards(p, l):
    return [jnp.swapaxes(p["w_in"], 1, 2)[l].astype(BF16), p["w_out"][l].astype(BF16), p["mla_wq_b"][l].astype(BF16),
            p["mla_wkv_b"][l].astype(BF16), p["gla_wa2_f"][l], p["gla_wa2_b"][l]]


def _step(p, where):
    x = p["x"][0]
    tabs = _rope_tables(x.shape[0])
    got0 = _gather_shards(_layer_shards(p, 0), SHARD_AXES, name="l0_gather_weights")
    w0 = _layer_weights(0, p, dict(zip(SHARDED, got0)))
    x1, sv0, got1 = _layer_fwd(0, x, w0, tabs, next_shards=_layer_shards(p, 1))
    w1 = _layer_weights(1, p, dict(zip(SHARDED, got1)))
    x2, sv1, _ = _layer_fwd(1, x1, w1, tabs)
    dx, loss = _loss_head(x2, p["loss_target"][0], name="loss_head")

    big, big_axes = SHARDED[:2], SHARD_AXES[:2]

    def pair_sums(tag, tensors, axes, names):
        return [_pair_reduce(a, where, ax, out_dtype=BF16, name=f"{tag}_pair_reduce_{n}")
                for a, ax, n in zip(tensors, axes, names)]

    def joined(tag, pair, others, axes, names):
        return [_sum_join(a, b, where, ax, name=f"{tag}_sum_join_{n}")
                for a, b, ax, n in zip(pair, others, axes, names)]

    dx, sharded1, small1, _ = _layer_bwd(1, dx, w1, sv1, tabs)
    pair1 = pair_sums("l1", [sharded1[n] for n in big], big_axes, big)
    dx, sharded0, small0, others1 = _layer_bwd(0, dx, w0, sv0, tabs, riding_parts=pair1)
    grads1 = joined("l1", pair1, others1, big_axes, big)
    packed = jnp.concatenate([sh[n].reshape(N_CHIP, -1, 128) for sh in (sharded0, sharded1) for n in SHARDED[2:]],
                             axis=1)
    pair0 = pair_sums("l0", [sharded0[n] for n in big] + [packed], big_axes + [0], big + ["rest"])
    grads0 = joined("l0", pair0, _chip_exchange(pair0, name="l0_chip_exchange"), big_axes + [0], big + ["rest"])
    grads = {n: jnp.stack([g0, g1]) for n, g0, g1 in zip(big, grads0, grads1)}
    rest, off = grads0[2], 0
    pieces = {n: [] for n in SHARDED[2:]}
    for sh in (sharded0, sharded1):
        for n in SHARDED[2:]:
            rows = sh[n].shape[1] * sh[n].shape[2] // 128
            pieces[n].append(rest[off:off + rows].reshape(sh[n].shape[1:]))
            off += rows
    grads.update({n: jnp.stack(v) for n, v in pieces.items()})
    small = {n: [small0[n], small1[n]] for n, _ in SMALL}
    small["loss"] = loss
    return dx[None], grads, small


def kernel(x, norm_g, w_in, ret_norm_g, gla_wa2_f, gla_ba_f, gla_wa2_b, gla_ba_b, gla_norm_g, pool_w, pool_scale, mla_q_norm_g, mla_wq_b, mla_kv_norm_g, mla_wkv_b, mla_qk_norm_q, mla_qk_norm_k, w_out, loss_target, m_norm_g, m_w_in, m_ret_norm_g, m_gla_wa2_f, m_gla_ba_f, m_gla_wa2_b, m_gla_ba_b, m_gla_norm_g, m_pool_w, m_pool_scale, m_mla_q_norm_g, m_mla_wq_b, m_mla_kv_norm_g, m_mla_wkv_b, m_mla_qk_norm_q, m_mla_qk_norm_k, m_w_out, v_norm_g, v_w_in, v_ret_norm_g, v_gla_wa2_f, v_gla_ba_f, v_gla_wa2_b, v_gla_ba_b, v_gla_norm_g, v_pool_w, v_pool_scale, v_mla_q_norm_g, v_mla_wq_b, v_mla_kv_norm_g, v_mla_wkv_b, v_mla_qk_norm_q, v_mla_qk_norm_k, v_w_out):
    p = dict(x=x, norm_g=norm_g, w_in=w_in, ret_norm_g=ret_norm_g, gla_wa2_f=gla_wa2_f, gla_ba_f=gla_ba_f,
             gla_wa2_b=gla_wa2_b, gla_ba_b=gla_ba_b, gla_norm_g=gla_norm_g, pool_w=pool_w, pool_scale=pool_scale,
             mla_q_norm_g=mla_q_norm_g, mla_wq_b=mla_wq_b, mla_kv_norm_g=mla_kv_norm_g, mla_wkv_b=mla_wkv_b,
             mla_qk_norm_q=mla_qk_norm_q, mla_qk_norm_k=mla_qk_norm_k, w_out=w_out, loss_target=loss_target)
    moments = dict(
        m=dict(norm_g=m_norm_g, w_in=m_w_in, ret_norm_g=m_ret_norm_g, gla_wa2_f=m_gla_wa2_f, gla_ba_f=m_gla_ba_f,
               gla_wa2_b=m_gla_wa2_b, gla_ba_b=m_gla_ba_b, gla_norm_g=m_gla_norm_g, pool_w=m_pool_w,
               pool_scale=m_pool_scale, mla_q_norm_g=m_mla_q_norm_g, mla_wq_b=m_mla_wq_b,
               mla_kv_norm_g=m_mla_kv_norm_g, mla_wkv_b=m_mla_wkv_b, mla_qk_norm_q=m_mla_qk_norm_q,
               mla_qk_norm_k=m_mla_qk_norm_k, w_out=m_w_out),
        v=dict(norm_g=v_norm_g, w_in=v_w_in, ret_norm_g=v_ret_norm_g, gla_wa2_f=v_gla_wa2_f, gla_ba_f=v_gla_ba_f,
               gla_wa2_b=v_gla_wa2_b, gla_ba_b=v_gla_ba_b, gla_norm_g=v_gla_norm_g, pool_w=v_pool_w,
               pool_scale=v_pool_scale, mla_q_norm_g=v_mla_q_norm_g, mla_wq_b=v_mla_wq_b,
               mla_kv_norm_g=v_mla_kv_norm_g, mla_wkv_b=v_mla_wkv_b, mla_qk_norm_q=v_mla_qk_norm_q,
               mla_qk_norm_k=v_mla_qk_norm_k, w_out=v_w_out))

    where = jnp.stack([lax.axis_index("c"), 2 * lax.axis_index("x") + lax.axis_index("y")]).astype(jnp.int32)
    grad_x, grads, small = _step(p, where)

    slots = _gather_all(_pack_small(small), name="gather_small")
    total = _unpack_small(_sum_slots(slots, name="sum_small"))
    for n, _ in SMALL:
        grads[n] = total[n].reshape(p[n].shape)
    loss = total["loss"]

    delta, new_m, new_v = {}, {}, {}
    for n in WEIGHTS:
        turn = (lambda a: jnp.swapaxes(a, 1, 2)) if n == "w_in" else (lambda a: a)
        outs = _adamw(turn(p[n]), grads[n], turn(moments["m"][n]), turn(moments["v"][n]), name=f"adamw_{n}")
        grads[n] = turn(grads[n])
        delta[n], new_m[n], new_v[n] = (turn(o) for o in outs)
    return (loss, grad_x, *[grads[n] for n in WEIGHTS], *[delta[n] for n in WEIGHTS],
            *[new_m[n] for n in WEIGHTS], *[new_v[n] for n in WEIGHTS])
```

```python
import functools
import math

import jax
import jax.numpy as jnp
from jax import lax
from jax.experimental import pallas as pl
from jax.experimental.pallas import tpu as pltpu

F32 = jnp.float32
BF16 = jnp.bfloat16
MESH = pl.DeviceIdType.MESH

EPS = 1e-6
ROPE_THETA = 10000.0
DEPTH = 2
N_DEV = 8
N_CHIP = 4

GROUP_W = 512
RET_HEADS = 4
RET_HD = 128
RET_CHUNK = 256
GLA_HEADS = 4
GLA_DK = 64
GLA_DV = 128
GLA_RANK = 16
GLA_TAU = 16.0
GLA_CHUNK = 64
POOL_GROUPS = 4
POOL_GW = 128
POOL_HALO = 8
POOL_TILE = 256
MLA_HEADS = 4
MLA_NOPE = 128
MLA_ROPE = 64
MLA_QK = MLA_NOPE + MLA_ROPE
MLA_QKP = 256
MLA_V = 128
MLA_Q_RANK = 512
MLA_KV_RANK = 256
MLA_SCALE = MLA_QK ** -0.5
FLASH_STRIP = 1024

ADAM_LR = 0.001
ADAM_B1 = 0.9
ADAM_B2 = 0.999
ADAM_EPS = 1e-08
ADAM_WD = 0.01
ADAM_STEP = 10

VMEM_LIMIT = 56 * 1024 * 1024
ROW_TILE = 512

SEG = {
    "rq": (0, 512, 0, 512), "rk": (512, 512, 512, 512), "rv": (1024, 512, 1024, 512), "rg": (1536, 512, 1536, 512),
    "gv": (2048, 512, 2560, 512), "gg": (2560, 512, 3072, 512),
    "pv": (3072, 512, 3616, 512), "pg": (3584, 512, 4128, 512),
    "mq": (4096, 512, 4640, 512), "mg": (4608, 512, 5472, 512),
    "gq": (5120, 256, 2048, 256), "gk": (5376, 256, 2304, 256), "mkv": (5632, 256, 5152, 256),
    "ga": (5888, 128, 3584, 32), "mkr": (6016, 128, 5408, 64),
}
SEG_ORDER = ["rq", "rk", "rv", "rg", "gv", "gg", "pv", "pg", "mq", "mg", "gq", "gk", "mkv", "ga", "mkr"]
IN_COLS = 5984
IN_PAD = 6144
ORIG_ORDER = ["rq", "rk", "rv", "rg", "gq", "gk", "gv", "gg", "ga", "pv", "pg", "mq", "mkv", "mkr", "mg"]


def _cparams(*sem):
    return pltpu.CompilerParams(dimension_semantics=tuple(sem), vmem_limit_bytes=VMEM_LIMIT)


def _bf(v):
    return v.astype(BF16)


def _dot(a, b, ca=1, cb=0):
    return lax.dot_general(_bf(a), _bf(b), (((ca,), (cb,)), ((), ())), preferred_element_type=F32)


def _split_dot(a01, x, ca=1, cb=0):
    hi = _bf(x)
    r1 = x - hi.astype(F32)
    mid = _bf(r1)
    lo = _bf(r1 - mid.astype(F32))
    dn = (((ca,), (cb,)), ((), ()))
    a = _bf(a01)
    return (lax.dot_general(a, hi, dn, preferred_element_type=F32)
            + lax.dot_general(a, mid, dn, preferred_element_type=F32)
            + lax.dot_general(a, lo, dn, preferred_element_type=F32))


def _sigmoid(x):
    return 1.0 / (1.0 + jnp.exp(-x))


def _silu_parts(g):
    sg = _sigmoid(g)
    return g * sg, sg * (1.0 + g * (1.0 - sg))


class _Rider:
    def __init__(self, ins, outs, sems, start, finish, aliases=None):
        self.ins, self.outs, self.sems, self.start, self.finish = list(ins), list(outs), list(sems), start, finish
        self.aliases = dict(aliases or {})


def _ride(body, rider, n_in, n_out, grid):
    if rider is None:
        return body
    ri, ro, rs = len(rider.ins), len(rider.outs), len(rider.sems)

    def wrapped(*refs):
        ins, refs = refs[:n_in], refs[n_in:]
        rin, refs = refs[:ri], refs[ri:]
        outs, refs = refs[:n_out], refs[n_out:]
        rout, refs = refs[:ro], refs[ro:]
        scratch, sems = refs[:len(refs) - rs], refs[len(refs) - rs:]
        first = pl.program_id(0) == 0
        last = pl.program_id(0) == grid[0] - 1
        for ax in range(1, len(grid)):
            first = jnp.logical_and(first, pl.program_id(ax) == 0)
            last = jnp.logical_and(last, pl.program_id(ax) == grid[ax] - 1)

        @pl.when(first)
        def _():
            rider.start(rin, rout, sems)

        body(*ins, *outs, *scratch)

        @pl.when(last)
        def _():
            rider.finish(rin, rout, sems)

    return wrapped


def _ride_call(body, rider, *, name, grid, in_specs, out_specs, out_shape, scratch_shapes, args, sem):
    n_in, n_out = len(in_specs), len(out_specs)
    if rider is None:
        return pl.pallas_call(body, name=name, grid=grid, in_specs=in_specs, out_specs=out_specs, out_shape=out_shape,
                              scratch_shapes=scratch_shapes, compiler_params=_cparams(*sem))(*args), []
    outs = pl.pallas_call(
        _ride(body, rider, n_in, n_out, grid), name=name, grid=grid,
        in_specs=list(in_specs) + [ANY] * len(rider.ins), out_specs=list(out_specs) + [ANY] * len(rider.outs),
        out_shape=list(out_shape) + rider.outs, scratch_shapes=list(scratch_shapes) + rider.sems,
        input_output_aliases={n_in + i: n_out + o for i, o in rider.aliases.items()},
        compiler_params=_cparams(*(["arbitrary"] * len(grid))),
    )(*args, *rider.ins)
    return outs[:n_out], outs[n_out:]


def _matmul(a, b, *, ta=False, tb=False, out_dtype=F32, tm=512, tn=1024, tk=None, add=None, n_outer=True, rider=None,
            name):
    m, kdim = (a.shape[1], a.shape[0]) if ta else a.shape
    n = b.shape[0] if tb else b.shape[1]
    tm, tn = min(tm, m), min(tn, n)
    tk = kdim if tk is None else min(tk, kdim)
    assert m % tm == 0 and n % tn == 0 and kdim % tk == 0
    nk = kdim // tk
    ca, cb = (0 if ta else 1), (1 if tb else 0)

    def body(*refs):
        if add is None:
            a_ref, b_ref, o_ref = refs[:3]
            add_ref = None
        else:
            a_ref, b_ref, add_ref, o_ref = refs[:4]
        p = _dot(a_ref[...], b_ref[...], ca, cb)

        def finish(r):
            if add_ref is not None:
                r = r + add_ref[...]
            o_ref[...] = r.astype(out_dtype)

        if nk == 1:
            finish(p)
        else:
            acc = refs[-1]
            k = pl.program_id(2)

            @pl.when(k == 0)
            def _():
                acc[...] = p

            @pl.when(k > 0)
            def _():
                acc[...] += p

            @pl.when(k == nk - 1)
            def _():
                finish(acc[...])

    def ij(g0, g1):
        return (g1, g0) if n_outer else (g0, g1)

    a_spec = (pl.BlockSpec((tk, tm), lambda g0, g1, k: (k, ij(g0, g1)[0])) if ta
              else pl.BlockSpec((tm, tk), lambda g0, g1, k: (ij(g0, g1)[0], k)))
    b_spec = (pl.BlockSpec((tn, tk), lambda g0, g1, k: (ij(g0, g1)[1], k)) if tb
              else pl.BlockSpec((tk, tn), lambda g0, g1, k: (k, ij(g0, g1)[1])))
    o_spec = pl.BlockSpec((tm, tn), lambda g0, g1, k: ij(g0, g1))
    in_specs = [a_spec, b_spec] + ([o_spec] if add is not None else [])
    args = (a, b) + ((add,) if add is not None else ())
    grid = (n // tn, m // tm, nk) if n_outer else (m // tm, n // tn, nk)
    (out,), rode = _ride_call(
        body, rider, name=name, grid=grid, in_specs=in_specs, out_specs=[o_spec],
        out_shape=[jax.ShapeDtypeStruct((m, n), out_dtype)],
        scratch_shapes=[] if nk == 1 else [pltpu.VMEM((tm, tn), F32)], args=args,
        sem=("parallel", "parallel", "arbitrary"))
    return out if rider is None else (out, rode)


def _rmsnorm_fwd(x, g, *, name, tm=ROW_TILE):
    s, d = x.shape
    tm = min(tm, s)

    def body(x_ref, g_ref, h_ref):
        xv = x_ref[...]
        r = lax.rsqrt(jnp.mean(xv * xv, axis=-1, keepdims=True) + EPS)
        h_ref[...] = _bf(xv * r * g_ref[...])

    return pl.pallas_call(
        body, name=name, grid=(s // tm,),
        in_specs=[pl.BlockSpec((tm, d), lambda i: (i, 0)), pl.BlockSpec((1, d), lambda i: (0, 0))],
        out_specs=pl.BlockSpec((tm, d), lambda i: (i, 0)),
        out_shape=jax.ShapeDtypeStruct((s, d), BF16),
        compiler_params=_cparams("parallel"),
    )(x, g)


def _rmsnorm_bwd(x, dh, g, dres, *, name, tm=ROW_TILE):
    s, d = x.shape
    tm = min(tm, s)

    def body(x_ref, dh_ref, g_ref, dres_ref, dx_ref, dg_ref):
        i = pl.program_id(0)
        xv = x_ref[...]
        r = lax.rsqrt(jnp.mean(xv * xv, axis=-1, keepdims=True) + EPS)
        xn = xv * r
        dv = dh_ref[...]
        part = jnp.sum(dv * xn, axis=0, keepdims=True)

        @pl.when(i == 0)
        def _():
            dg_ref[...] = part

        @pl.when(i > 0)
        def _():
            dg_ref[...] += part

        dxn = dv * g_ref[...]
        dx_ref[...] = dres_ref[...] + r * (dxn - xn * jnp.mean(dxn * xn, axis=-1, keepdims=True))

    row = pl.BlockSpec((tm, d), lambda i: (i, 0))
    vec = pl.BlockSpec((1, d), lambda i: (0, 0))
    return pl.pallas_call(
        body, name=name, grid=(s // tm,), in_specs=[row, row, vec, row], out_specs=[row, vec],
        out_shape=[jax.ShapeDtypeStruct((s, d), F32), jax.ShapeDtypeStruct((1, d), F32)],
        compiler_params=_cparams("arbitrary"),
    )(x, dh, g, dres)


def _loss_head(xf, target, *, name, tm=ROW_TILE):
    s, d = xf.shape
    tm = min(tm, s)

    def body(x_ref, t_ref, dx_ref, l_ref):
        i = pl.program_id(0)
        e = x_ref[...] - t_ref[...]
        dx_ref[...] = e * (1.0 / d)
        rows = jnp.mean(e * e, axis=-1, keepdims=True)
        part = 0.5 * jnp.sum(rows, axis=0, keepdims=True)

        @pl.when(i == 0)
        def _():
            l_ref[...] = part

        @pl.when(i > 0)
        def _():
            l_ref[...] += part

    row = pl.BlockSpec((tm, d), lambda i: (i, 0))
    return pl.pallas_call(
        body, name=name, grid=(s // tm,), in_specs=[row, row],
        out_specs=[row, pl.BlockSpec((1, 1), lambda i: (0, 0))],
        out_shape=[jax.ShapeDtypeStruct((s, d), F32), jax.ShapeDtypeStruct((1, 1), F32)],
        compiler_params=_cparams("arbitrary"),
    )(xf, target)


def _rope_tables(s):
    pos = jnp.arange(s, dtype=F32)[:, None]
    inv_r = 1.0 / (ROPE_THETA ** (jnp.arange(0, RET_HD, 2, dtype=F32) / RET_HD))
    ang = pos * inv_r[None, :]
    ret_cos = jnp.concatenate([jnp.cos(ang), jnp.cos(ang)], axis=1)
    ret_sin = jnp.concatenate([-jnp.sin(ang), jnp.sin(ang)], axis=1)
    inv_m = 1.0 / (ROPE_THETA ** (jnp.arange(0, MLA_ROPE, 2, dtype=F32) / MLA_ROPE))
    am = pos * inv_m[None, :]
    z32, z64 = jnp.zeros((s, 32), F32), jnp.zeros((s, 64), F32)
    mla_cos = jnp.concatenate([jnp.cos(am), jnp.cos(am), z64], axis=1)
    mla_sp = jnp.concatenate([z32, jnp.sin(am), z64], axis=1)
    mla_sn = jnp.concatenate([-jnp.sin(am), z32, z64], axis=1)
    return ret_cos, ret_sin, mla_cos, mla_sp, mla_sn


def _rope128(x, c, sg):
    return x * c + pltpu.roll(x, 64, 1) * sg


def _unrope128(d, c, sg):
    return d * c + pltpu.roll(d * sg, 64, 1)


def _rope64(t, c, sp, sn):
    return t * c + pltpu.roll(t, 96, 1) * sn + pltpu.roll(t, 32, 1) * sp


def _unrope64(d, c, sp, sn):
    return d * c + pltpu.roll(d * sn, 32, 1) + pltpu.roll(d * sp, 96, 1)


def _ret_pre(z, cos, sin, *, name, tm=ROW_TILE):
    s = z.shape[0]
    tm = min(tm, s)
    scale = RET_HD ** -0.5

    def body(q_ref, k_ref, c_ref, s_ref, qo_ref, ko_ref):
        c, sg = c_ref[...], s_ref[...]
        for h in range(RET_HEADS):
            sl = slice(h * RET_HD, (h + 1) * RET_HD)
            qo_ref[:, sl] = _rope128(q_ref[:, sl], c, sg)
            ko_ref[:, sl] = _rope128(k_ref[:, sl], c, sg) * scale

    seg = lambda j: pl.BlockSpec((tm, GROUP_W), lambda i: (i, j))
    tab = pl.BlockSpec((tm, RET_HD), lambda i: (i, 0))
    return pl.pallas_call(
        body, name=name, grid=(s // tm,), in_specs=[seg(0), seg(1), tab, tab],
        out_specs=[seg(0), seg(0)],
        out_shape=[jax.ShapeDtypeStruct((s, GROUP_W), F32)] * 2,
        compiler_params=_cparams("parallel"),
    )(z, z, cos, sin)


def _ret_pre_bwd(dqr, dkr, cos, sin, *, name, tm=ROW_TILE):
    s = dqr[0].shape[0]
    tm = min(tm, s)
    scale = RET_HD ** -0.5

    def body(dq0_ref, dq1_ref, dk0_ref, dk1_ref, c_ref, s_ref, qo_ref, ko_ref):
        c, sg = c_ref[...], s_ref[...]
        for h in range(RET_HEADS):
            sl = slice(h * RET_HD, (h + 1) * RET_HD)
            qo_ref[:, sl] = _bf(_unrope128(dq0_ref[:, sl] + dq1_ref[:, sl], c, sg))
            ko_ref[:, sl] = _bf(_unrope128(dk0_ref[:, sl] + dk1_ref[:, sl], c, sg) * scale)

    row = pl.BlockSpec((tm, GROUP_W), lambda i: (i, 0))
    tab = pl.BlockSpec((tm, RET_HD), lambda i: (i, 0))
    return pl.pallas_call(
        body, name=name, grid=(s // tm,), in_specs=[row, row, row, row, tab, tab], out_specs=[row, row],
        out_shape=[jax.ShapeDtypeStruct((s, GROUP_W), BF16)] * 2,
        compiler_params=_cparams("parallel"),
    )(dqr[0], dqr[1], dkr[0], dkr[1], cos, sin)


def _bla(a, b, c, lg, cols, *, name):
    s = a.shape[0]
    ch = min(RET_CHUNK, s)
    n = s // ch
    hd = RET_HD

    def body(lg_ref, a0, b0, c0, a1, b1, c1, o0, o1, st):
        t = pl.program_id(0)

        @pl.when(t == 0)
        def _():
            st[...] = jnp.zeros_like(st)

        ii = lax.broadcasted_iota(jnp.int32, (ch, ch), 0)
        jj = lax.broadcasted_iota(jnp.int32, (ch, ch), 1)
        idx = lax.broadcasted_iota(jnp.int32, (ch, 1), 0).astype(F32)
        for d, (a_ref, b_ref, c_ref, o_ref) in enumerate(((a0, b0, c0, o0), (a1, b1, c1, o1))):
            diff = ((ii - jj) if d == 0 else (jj - ii)).astype(F32)
            keep = diff >= 0
            dpos = jnp.maximum(diff, 0.0)
            pq = (idx + 1.0) if d == 0 else (ch - idx)
            pk = (ch - 1.0 - idx) if d == 0 else idx
            for h in range(RET_HEADS):
                g = lg_ref[d, h]
                sl = slice(h * hd, (h + 1) * hd)
                av, bv, cv = a_ref[:, sl], b_ref[:, sl], c_ref[:, sl]
                sc = _dot(av, bv, 1, 1) * jnp.where(keep, jnp.exp(dpos * g), 0.0)
                stv = st[d, h]
                o_ref[:, sl] = _dot(sc, cv) + _dot(av * jnp.exp(pq * g), stv)
                st[d, h] = jnp.exp(ch * g) * stv + _dot(bv * jnp.exp(pk * g), cv, 0, 0)

    fwd = lambda j: pl.BlockSpec((ch, GROUP_W), lambda t: (t, j))
    bwd = lambda j: pl.BlockSpec((ch, GROUP_W), lambda t: (n - 1 - t, j))
    return pl.pallas_call(
        body, name=name, grid=(n,),
        in_specs=[pl.BlockSpec(memory_space=pltpu.SMEM), fwd(cols[0]), fwd(cols[1]), fwd(cols[2]),
                  bwd(cols[0]), bwd(cols[1]), bwd(cols[2])],
        out_specs=[fwd(0), bwd(0)],
        out_shape=[jax.ShapeDtypeStruct((s, GROUP_W), F32)] * 2,
        scratch_shapes=[pltpu.VMEM((2, RET_HEADS, hd, hd), F32)],
        compiler_params=_cparams("arbitrary"),
    )(lg, a, b, c, a, b, c)


def _post(os_, zg, gcol, g, *, norm, name, tm=ROW_TILE):
    s = zg.shape[0]
    tm = min(tm, s)
    nd = len(os_)

    def body(*refs):
        o_refs, (gt_ref, g_ref, y_ref) = refs[:nd], refs[nd:]
        silu, _ = _silu_parts(gt_ref[...])
        for h in range(4):
            sl = slice(h * 128, (h + 1) * 128)
            o = o_refs[0][:, sl]
            for k in range(1, nd):
                o = o + o_refs[k][:, sl]
            if norm:
                r = lax.rsqrt(jnp.mean(o * o, axis=-1, keepdims=True) + EPS)
                o = o * r * g_ref[:, sl]
            y_ref[:, sl] = _bf(silu[:, sl] * o)

    row = pl.BlockSpec((tm, GROUP_W), lambda i: (i, 0))
    return pl.pallas_call(
        body, name=name, grid=(s // tm,),
        in_specs=[row] * nd + [pl.BlockSpec((tm, GROUP_W), lambda i: (i, gcol)),
                               pl.BlockSpec((1, GROUP_W), lambda i: (0, 0))],
        out_specs=row,
        out_shape=jax.ShapeDtypeStruct((s, GROUP_W), BF16),
        compiler_params=_cparams("parallel"),
    )(*os_, zg, g)


def _post_bwd(dy, ycol, os_, zg, gcol, g, *, norm, name, tm=ROW_TILE):
    s = zg.shape[0]
    tm = min(tm, s)
    nd = len(os_)

    def body(*refs):
        dy_ref, o_refs = refs[0], refs[1:1 + nd]
        gt_ref, g_ref, dgt_ref, do_ref, dg_ref = refs[1 + nd:]
        i = pl.program_id(0)
        silu, dsilu = _silu_parts(gt_ref[...])
        dyv = dy_ref[...]
        parts = []
        for h in range(4):
            sl = slice(h * 128, (h + 1) * 128)
            o = o_refs[0][:, sl]
            for k in range(1, nd):
                o = o + o_refs[k][:, sl]
            dn = dyv[:, sl] * silu[:, sl]
            if norm:
                r = lax.rsqrt(jnp.mean(o * o, axis=-1, keepdims=True) + EPS)
                xn = o * r
                gh = g_ref[:, sl]
                dgt_ref[:, sl] = _bf(dyv[:, sl] * (xn * gh) * dsilu[:, sl])
                parts.append(jnp.sum(dn * xn, axis=0, keepdims=True))
                dxn = dn * gh
                do_ref[:, sl] = r * (dxn - xn * jnp.mean(dxn * xn, axis=-1, keepdims=True))
            else:
                dgt_ref[:, sl] = _bf(dyv[:, sl] * o * dsilu[:, sl])
                parts.append(jnp.zeros((1, 128), F32))
                do_ref[:, sl] = dn
        part = jnp.concatenate(parts, axis=1)

        @pl.when(i == 0)
        def _():
            dg_ref[...] = part

        @pl.when(i > 0)
        def _():
            dg_ref[...] += part

    row = pl.BlockSpec((tm, GROUP_W), lambda i: (i, 0))
    vec = pl.BlockSpec((1, GROUP_W), lambda i: (0, 0))
    return pl.pallas_call(
        body, name=name, grid=(s // tm,),
        in_specs=[pl.BlockSpec((tm, GROUP_W), lambda i: (i, ycol))] + [row] * nd
        + [pl.BlockSpec((tm, GROUP_W), lambda i: (i, gcol)), vec],
        out_specs=[row, row, vec],
        out_shape=[jax.ShapeDtypeStruct((s, GROUP_W), BF16), jax.ShapeDtypeStruct((s, GROUP_W), F32),
                   jax.ShapeDtypeStruct((1, GROUP_W), F32)],
        compiler_params=_cparams("arbitrary"),
    )(dy, *os_, zg, g)


def _ret_log_gamma(swap):
    gf = 1.0 - 2.0 ** (-5.0 - jnp.arange(RET_HEADS, dtype=F32))
    lf, lb = jnp.log(gf), jnp.log(gf[::-1])
    return jnp.stack([lb, lf] if swap else [lf, lb])


def _log_sigmoid(x):
    return jnp.minimum(x, 0.0) - jnp.log(1.0 + jnp.exp(-jnp.abs(x)))


def _gla_gate(z, wa, ba, *, name, tm=ROW_TILE):
    s = z.shape[0]
    tm = min(tm, s)
    col = SEG["ga"][0] // 128

    def body(ga_ref, wa_ref, ba_ref, la_ref):
        pre = _dot(ga_ref[...], wa_ref[...]) + ba_ref[...]
        la_ref[...] = _log_sigmoid(pre) / GLA_TAU

    return pl.pallas_call(
        body, name=name, grid=(s // tm,),
        in_specs=[pl.BlockSpec((tm, 128), lambda i: (i, col)), pl.BlockSpec((128, 512), lambda i: (0, 0)),
                  pl.BlockSpec((1, 512), lambda i: (0, 0))],
        out_specs=pl.BlockSpec((tm, 512), lambda i: (i, 0)),
        out_shape=jax.ShapeDtypeStruct((s, 512), F32),
        compiler_params=_cparams("parallel"),
    )(z, wa, ba)


def _gla_gate_bwd(dla, z, wa, ba, *, name, tm=ROW_TILE):
    s = z.shape[0]
    tm = min(tm, s)
    col = SEG["ga"][0] // 128

    def body(dla_ref, ga_ref, wa_ref, ba_ref, dga_ref, dwa_ref, dba_ref):
        i = pl.program_id(0)
        gav = ga_ref[...]
        pre = _dot(gav, wa_ref[...]) + ba_ref[...]
        dpre = dla_ref[...] * (1.0 - _sigmoid(pre)) * (1.0 / GLA_TAU)
        dga_ref[...] = _bf(_dot(dpre, wa_ref[...], 1, 1))
        pw = _dot(gav, dpre, 0, 0)
        pb = jnp.sum(dpre, axis=0, keepdims=True)

        @pl.when(i == 0)
        def _():
            dwa_ref[...] = pw
            dba_ref[...] = pb

        @pl.when(i > 0)
        def _():
            dwa_ref[...] += pw
            dba_ref[...] += pb

    return pl.pallas_call(
        body, name=name, grid=(s // tm,),
        in_specs=[pl.BlockSpec((tm, 512), lambda i: (i, 0)), pl.BlockSpec((tm, 128), lambda i: (i, col)),
                  pl.BlockSpec((128, 512), lambda i: (0, 0)), pl.BlockSpec((1, 512), lambda i: (0, 0))],
        out_specs=[pl.BlockSpec((tm, 128), lambda i: (i, 0)), pl.BlockSpec((128, 512), lambda i: (0, 0)),
                   pl.BlockSpec((1, 512), lambda i: (0, 0))],
        out_shape=[jax.ShapeDtypeStruct((s, 128), BF16), jax.ShapeDtypeStruct((128, 512), F32),
                   jax.ShapeDtypeStruct((1, 512), F32)],
        compiler_params=_cparams("arbitrary"),
    )(dla, z, wa, ba)


def _gla_masks(ch):
    ii = lax.broadcasted_iota(jnp.int32, (ch, ch), 0)
    tt = lax.broadcasted_iota(jnp.int32, (ch, ch), 1)
    return jnp.where(tt <= ii, 1.0, 0.0), jnp.where(tt >= ii, 1.0, 0.0)


def _running_sum(x, up):
    n = x.shape[0]
    rows = lax.broadcasted_iota(jnp.int32, x.shape, 0)
    k = 1
    while k < n:
        if up:
            x = x + jnp.where(rows < n - k, pltpu.roll(x, n - k, 0), 0.0)
        else:
            x = x + jnp.where(rows >= k, pltpu.roll(x, k, 0), 0.0)
        k *= 2
    return x


def _gla_chunk(d, tmat, qv, kv, lav, ch):
    c = _running_sum(lav, up=(d == 1))
    big_l = c[ch - 1:ch, :] if d == 0 else c[0:1, :]
    qt = qv * (GLA_DK ** -0.5) * jnp.exp(c)
    kt = kv * jnp.exp(-c)
    kh = kv * jnp.exp(big_l - c)
    return c, big_l, qt, kt, kh


def _gla_fwd(qh, kh_, z, la, *, name, rider=None):
    s = z.shape[0]
    ch = min(GLA_CHUNK, s)
    n = s // ch
    vcol = SEG["gv"][0] // GROUP_W

    def body(q0, k0, v0, la0, q1, k1, v1, la1, o0, o1, zs0, zs1, st):
        t = pl.program_id(0)

        @pl.when(t == 0)
        def _():
            st[...] = jnp.zeros_like(st)

        masks = _gla_masks(ch)
        for d, (q_ref, k_ref, v_ref, la_ref, o_ref, zs_ref) in enumerate(
                ((q0, k0, v0, la0, o0, zs0), (q1, k1, v1, la1, o1, zs1))):
            for h in range(GLA_HEADS):
                c, big_l, qt, kt, kh = _gla_chunk(d, masks[d], q_ref[h], k_ref[h], la_ref[0, h], ch)
                vv = v_ref[:, h * GLA_DV:(h + 1) * GLA_DV]
                p = _dot(qt, kt, 1, 1) * masks[d]
                zst = st[d, h]
                o_ref[:, h * GLA_DV:(h + 1) * GLA_DV] = _dot(p, vv) + _dot(qt, zst, 1, 1)
                zs_ref[h, 0] = zst
                st[d, h] = zst * jnp.exp(big_l) + _dot(vv, kh, 0, 0)

    cidx = (lambda t: t), (lambda t: n - 1 - t)
    hs = lambda d: pl.BlockSpec((GLA_HEADS, ch, GLA_DK), lambda t: (0, cidx[d](t), 0))
    vs = lambda d: pl.BlockSpec((ch, GROUP_W), lambda t: (cidx[d](t), vcol))
    las = lambda d: pl.BlockSpec((1, GLA_HEADS, ch, GLA_DK), lambda t: (d, 0, cidx[d](t), 0))
    os_ = lambda d: pl.BlockSpec((ch, GROUP_W), lambda t: (cidx[d](t), 0))
    zss = lambda d: pl.BlockSpec((GLA_HEADS, 1, GLA_DV, GLA_DK), lambda t: (0, cidx[d](t), 0, 0))
    (o0, o1, zs0, zs1), rode = _ride_call(
        body, rider, name=name, grid=(n,),
        in_specs=[hs(0), hs(0), vs(0), las(0), hs(1), hs(1), vs(1), las(1)],
        out_specs=[os_(0), os_(1), zss(0), zss(1)],
        out_shape=[jax.ShapeDtypeStruct((s, GROUP_W), F32)] * 2
        + [jax.ShapeDtypeStruct((GLA_HEADS, n, GLA_DV, GLA_DK), F32)] * 2,
        scratch_shapes=[pltpu.VMEM((2, GLA_HEADS, GLA_DV, GLA_DK), F32)],
        args=(qh, kh_, z, la, qh, kh_, z, la), sem=("arbitrary",))
    return ((o0, o1), (zs0, zs1)) if rider is None else ((o0, o1), (zs0, zs1), rode)


def _gla_bwd(qh, kh_, z, la, do, zs, *, name):
    s = z.shape[0]
    ch = min(GLA_CHUNK, s)
    n = s // ch
    vcol = SEG["gv"][0] // GROUP_W

    def body(q0, k0, v0, la0, do0, zs0, q1, k1, v1, la1, do1, zs1,
             dq0, dk0, dla0, dv0, dq1, dk1, dla1, dv1, gz):
        t = pl.program_id(0)

        @pl.when(t == 0)
        def _():
            gz[...] = jnp.zeros_like(gz)

        masks = _gla_masks(ch)
        rows = lax.broadcasted_iota(jnp.int32, (ch, 1), 0)
        for d, (q_ref, k_ref, v_ref, la_ref, do_ref, zs_ref, dq_ref, dk_ref, dla_ref, dv_ref) in enumerate(
                ((q0, k0, v0, la0, do0, zs0, dq0, dk0, dla0, dv0), (q1, k1, v1, la1, do1, zs1, dq1, dk1, dla1, dv1))):
            tmat = masks[d]
            end = ch - 1 if d == 0 else 0
            for h in range(GLA_HEADS):
                c, big_l, qt, kt, kh = _gla_chunk(d, tmat, q_ref[h], k_ref[h], la_ref[0, h], ch)
                vsl = slice(h * GLA_DV, (h + 1) * GLA_DV)
                vv, dov, zst, gzv = v_ref[:, vsl], do_ref[:, vsl], zs_ref[h, 0], gz[d, h]
                p = _dot(qt, kt, 1, 1) * tmat
                dp = _dot(dov, vv, 1, 1) * tmat
                dqt = _dot(dp, kt) + _dot(dov, zst)
                dkt = _dot(dp, qt, 0, 0)
                dkh = _dot(vv, gzv)
                dv_ref[:, vsl] = _dot(p, dov, 0, 0) + _dot(kh, gzv, 1, 1)
                dq_ref[h] = dqt * jnp.exp(c) * (GLA_DK ** -0.5)
                dk_ref[h] = dkt * jnp.exp(-c) + dkh * jnp.exp(big_l - c)
                e_l = jnp.exp(big_l)
                d_l = jnp.sum(dkh * kh, axis=0, keepdims=True) + e_l * jnp.sum(zst * gzv, axis=0, keepdims=True)
                dc = dqt * qt - dkt * kt - dkh * kh + jnp.where(rows == end, d_l, 0.0)
                dla_ref[h] = _running_sum(dc, up=(d == 0))
                gz[d, h] = gzv * e_l + _dot(dov, qt, 0, 0)

    cidx = (lambda t: n - 1 - t), (lambda t: t)
    hs = lambda d: pl.BlockSpec((GLA_HEADS, ch, GLA_DK), lambda t: (0, cidx[d](t), 0))
    vs = lambda d: pl.BlockSpec((ch, GROUP_W), lambda t: (cidx[d](t), vcol))
    las = lambda d: pl.BlockSpec((1, GLA_HEADS, ch, GLA_DK), lambda t: (d, 0, cidx[d](t), 0))
    row = lambda d: pl.BlockSpec((ch, GROUP_W), lambda t: (cidx[d](t), 0))
    zss = lambda d: pl.BlockSpec((GLA_HEADS, 1, GLA_DV, GLA_DK), lambda t: (0, cidx[d](t), 0, 0))
    hshape = jax.ShapeDtypeStruct((GLA_HEADS, s, GLA_DK), F32)
    wide = jax.ShapeDtypeStruct((s, GROUP_W), F32)
    outs = pl.pallas_call(
        body, name=name, grid=(n,),
        in_specs=[hs(0), hs(0), vs(0), las(0), row(0), zss(0), hs(1), hs(1), vs(1), las(1), row(1), zss(1)],
        out_specs=[hs(0), hs(0), hs(0), row(0), hs(1), hs(1), hs(1), row(1)],
        out_shape=[hshape, hshape, hshape, wide, hshape, hshape, hshape, wide],
        scratch_shapes=[pltpu.VMEM((2, GLA_HEADS, GLA_DV, GLA_DK), F32)],
        compiler_params=_cparams("arbitrary"),
    )(qh, kh_, z, la, do, zs[0], qh, kh_, z, la, do, zs[1])
    dq0, dk0, dla0, dv0, dq1, dk1, dla1, dv1 = outs
    return (dq0, dq1), (dk0, dk1), (dla0, dla1), (dv0, dv1)


def _band(lo, hi, rows, width):
    r = lax.broadcasted_iota(jnp.int32, (rows, width), 0)
    j = lax.broadcasted_iota(jnp.int32, (rows, width), 1)
    k = j - POOL_HALO - r
    return jnp.where((k >= lo) & (k <= hi), 1.0, 0.0)


def _pool_cnt(t0, half, rows, s):
    t = t0 + lax.broadcasted_iota(jnp.int32, (rows, 1), 0)
    return (jnp.minimum(t + half, s) - jnp.maximum(t - half, 0)).astype(F32)


def _pool_fwd(z, pw, scale, *, name):
    s = z.shape[0]
    tl = min(POOL_TILE, s)
    nt = s // tl
    ucol, gcol = SEG["pv"][0] // 128, SEG["pg"][0] // 128

    def body(u_ref, gt_ref, pw_ref, sc_ref, y_ref, pad):
        g = pl.program_id(0)
        half = jnp.left_shift(1, g)
        pad[0:POOL_HALO, :] = jnp.zeros((POOL_HALO, POOL_GW), F32)
        pad[POOL_HALO + s:POOL_HALO + s + POOL_HALO, :] = jnp.zeros((POOL_HALO, POOL_GW), F32)
        pad[POOL_HALO:POOL_HALO + s, :] = u_ref[...]
        band = _band(-half, half - 1, tl, tl + 2 * POOL_HALO)
        pwv, scv = pw_ref[0], sc_ref[...]

        def tile(i, carry):
            t0 = pl.multiple_of(i * tl, tl)
            win = pad[pl.ds(t0, tl + 2 * POOL_HALO), :]
            u = win[POOL_HALO:POOL_HALO + tl, :]
            pooled = _split_dot(band, win) / _pool_cnt(t0, half, tl, s) - u
            mixed = _dot(pooled, pwv)
            silu, _ = _silu_parts(gt_ref[pl.ds(t0, tl), :])
            y_ref[pl.ds(t0, tl), :] = _bf(silu * (mixed * scv))
            return carry

        lax.fori_loop(0, nt, tile, 0)

    return pl.pallas_call(
        body, name=name, grid=(POOL_GROUPS,),
        in_specs=[pl.BlockSpec((s, POOL_GW), lambda g: (0, ucol + g)),
                  pl.BlockSpec((s, POOL_GW), lambda g: (0, gcol + g)),
                  pl.BlockSpec((1, POOL_GW, POOL_GW), lambda g: (g, 0, 0)),
                  pl.BlockSpec((1, POOL_GW), lambda g: (0, g))],
        out_specs=pl.BlockSpec((s, POOL_GW), lambda g: (0, g)),
        out_shape=jax.ShapeDtypeStruct((s, GROUP_W), BF16),
        scratch_shapes=[pltpu.VMEM((s + 2 * POOL_HALO, POOL_GW), F32)],
        compiler_params=_cparams("parallel"),
    )(z, z, pw, scale)


def _pool_bwd(dy, z, pw, scale, *, name):
    s = z.shape[0]
    tl = min(POOL_TILE, s)
    nt = s // tl
    ucol, gcol, ycol = SEG["pv"][0] // 128, SEG["pg"][0] // 128, 2 * GROUP_W // 128

    def body(dy_ref, u_ref, gt_ref, pw_ref, sc_ref, du_ref, dgt_ref, dpw_ref, dsc_ref, pad, epad, dpo):
        g = pl.program_id(0)
        half = jnp.left_shift(1, g)
        zeros = jnp.zeros((POOL_HALO, POOL_GW), F32)
        for buf in (pad, epad):
            buf[0:POOL_HALO, :] = zeros
            buf[POOL_HALO + s:POOL_HALO + s + POOL_HALO, :] = zeros
        pad[POOL_HALO:POOL_HALO + s, :] = u_ref[...]
        band = _band(-half, half - 1, tl, tl + 2 * POOL_HALO)
        band_t = _band(1 - half, half, tl, tl + 2 * POOL_HALO)
        pwv, scv = pw_ref[0], sc_ref[...]
        dpw_ref[0] = jnp.zeros((POOL_GW, POOL_GW), F32)
        dsc_ref[...] = jnp.zeros((1, POOL_GW), F32)

        def tile(i, carry):
            t0 = pl.multiple_of(i * tl, tl)
            win = pad[pl.ds(t0, tl + 2 * POOL_HALO), :]
            u = win[POOL_HALO:POOL_HALO + tl, :]
            cnt = _pool_cnt(t0, half, tl, s)
            pooled = _split_dot(band, win) / cnt - u
            mixed = _dot(pooled, pwv)
            silu, dsilu = _silu_parts(gt_ref[pl.ds(t0, tl), :])
            dyv = dy_ref[pl.ds(t0, tl), :]
            dgt_ref[pl.ds(t0, tl), :] = _bf(dyv * (mixed * scv) * dsilu)
            dsc_ref[...] += jnp.sum(dyv * silu * mixed, axis=0, keepdims=True)
            dm = dyv * silu * scv
            dpw_ref[0] += _dot(pooled, dm, 0, 0)
            dpooled = _dot(dm, pwv, 1, 1)
            dpo[pl.ds(t0, tl), :] = dpooled
            epad[pl.ds(POOL_HALO + t0, tl), :] = dpooled / cnt
            return carry

        lax.fori_loop(0, nt, tile, 0)

        def tile2(i, carry):
            t0 = pl.multiple_of(i * tl, tl)
            ewin = epad[pl.ds(t0, tl + 2 * POOL_HALO), :]
            du_ref[pl.ds(t0, tl), :] = _bf(_split_dot(band_t, ewin) - dpo[pl.ds(t0, tl), :])
            return carry

        lax.fori_loop(0, nt, tile2, 0)

    col = lambda c0: pl.BlockSpec((s, POOL_GW), lambda g: (0, c0 + g))
    return pl.pallas_call(
        body, name=name, grid=(POOL_GROUPS,),
        in_specs=[col(ycol), col(ucol), col(gcol), pl.BlockSpec((1, POOL_GW, POOL_GW), lambda g: (g, 0, 0)),
                  pl.BlockSpec((1, POOL_GW), lambda g: (0, g))],
        out_specs=[col(0), col(0), pl.BlockSpec((1, POOL_GW, POOL_GW), lambda g: (g, 0, 0)),
                   pl.BlockSpec((1, POOL_GW), lambda g: (0, g))],
        out_shape=[jax.ShapeDtypeStruct((s, GROUP_W), BF16), jax.ShapeDtypeStruct((s, GROUP_W), BF16),
                   jax.ShapeDtypeStruct((POOL_GROUPS, POOL_GW, POOL_GW), F32),
                   jax.ShapeDtypeStruct((1, GROUP_W), F32)],
        scratch_shapes=[pltpu.VMEM((s + 2 * POOL_HALO, POOL_GW), F32), pltpu.VMEM((s + 2 * POOL_HALO, POOL_GW), F32),
                        pltpu.VMEM((s, POOL_GW), F32)],
        compiler_params=_cparams("parallel"),
    )(dy, z, z, pw, scale)


def _mla_specs(tm):
    zq = pl.BlockSpec((tm, 512), lambda i: (i, SEG["mq"][0] // 512))
    zkv = pl.BlockSpec((tm, 256), lambda i: (i, SEG["mkv"][0] // 256))
    zkr = pl.BlockSpec((tm, 128), lambda i: (i, SEG["mkr"][0] // 128))
    full = lambda r, c: pl.BlockSpec((r, c), lambda i: (0, 0))
    tab = pl.BlockSpec((tm, 128), lambda i: (i, 0))
    weights = [full(1, 512), full(512, 1024), full(1, 256), full(256, 1024), full(1, 256), full(1, 256)]
    return [zq, zkv, zkr] + weights + [tab, tab, tab]


def _mla_project(xq_ref, xkv_ref, qg_ref, wq_ref, kvg_ref, wkv_ref):
    xq = xq_ref[...]
    r1 = lax.rsqrt(jnp.mean(xq * xq, axis=-1, keepdims=True) + EPS)
    xn1 = xq * r1
    qn = _bf(xn1 * qg_ref[...])
    qraw = _dot(qn, wq_ref[...])
    xkv = xkv_ref[...]
    r2 = lax.rsqrt(jnp.mean(xkv * xkv, axis=-1, keepdims=True) + EPS)
    xn2 = xkv * r2
    kvn = _bf(xn2 * kvg_ref[...])
    kvraw = _dot(kvn, wkv_ref[...])
    return r1, xn1, qn, qraw, r2, xn2, kvn, kvraw


def _mla_pre(z, qg, wq, kvg, wkv, qng, kng, cos, sp, sn, *, name, tm=ROW_TILE):
    s = z.shape[0]
    tm = min(tm, s)

    def body(xq_ref, xkv_ref, pe_ref, qg_ref, wq_ref, kvg_ref, wkv_ref, qng_ref, kng_ref, c_ref, sp_ref, sn_ref,
             q_ref, k_ref, v_ref):
        _, _, _, qraw, _, _, _, kvraw = _mla_project(xq_ref, xkv_ref, qg_ref, wq_ref, kvg_ref, wkv_ref)
        c, spv, snv = c_ref[...], sp_ref[...], sn_ref[...]
        pe = pe_ref[...]
        pe_ss = jnp.sum(pe * pe, axis=-1, keepdims=True)
        qngv, kngv = qng_ref[...], kng_ref[...]
        for h in range(MLA_HEADS):
            b = h * MLA_QKP
            qh = qraw[:, b:b + MLA_QKP]
            r = lax.rsqrt(jnp.sum(qh * qh, axis=-1, keepdims=True) * (1.0 / MLA_QK) + EPS)
            qn_h = qh * r * qngv
            q_ref[:, b:b + 128] = _bf(qn_h[:, :128] * MLA_SCALE)
            q_ref[:, b + 128:b + 256] = _bf(_rope64(qn_h[:, 128:], c, spv, snv) * MLA_SCALE)
            kn = kvraw[:, b:b + 128]
            rk = lax.rsqrt((jnp.sum(kn * kn, axis=-1, keepdims=True) + pe_ss) * (1.0 / MLA_QK) + EPS)
            k_ref[:, b:b + 128] = _bf(kn * rk * kngv[:, :128])
            k_ref[:, b + 128:b + 256] = _bf(_rope64(pe * rk * kngv[:, 128:], c, spv, snv))
            v_ref[:, h * MLA_V:(h + 1) * MLA_V] = _bf(kvraw[:, b + 128:b + 256])

    row = lambda w: pl.BlockSpec((tm, w), lambda i: (i, 0))
    return pl.pallas_call(
        body, name=name, grid=(s // tm,), in_specs=_mla_specs(tm),
        out_specs=[row(1024), row(1024), row(512)],
        out_shape=[jax.ShapeDtypeStruct((s, 1024), BF16), jax.ShapeDtypeStruct((s, 1024), BF16),
                   jax.ShapeDtypeStruct((s, 512), BF16)],
        compiler_params=_cparams("parallel"),
    )(z, z, z, qg, wq, kvg, wkv, qng, kng, cos, sp, sn)


def _mla_pre_bwd(dq, dk, dv, z, qg, wq, kvg, wkv, qng, kng, cos, sp, sn, *, name, tm=ROW_TILE):
    s = z.shape[0]
    tm = min(tm, s)

    def body(dq_ref, dk_ref, dv_ref, xq_ref, xkv_ref, pe_ref, qg_ref, wq_ref, kvg_ref, wkv_ref, qng_ref, kng_ref,
             c_ref, sp_ref, sn_ref, dxq_ref, dxkv_ref, dpe_ref, dwq_ref, dwkv_ref, dqg_ref, dkvg_ref, dqng_ref,
             dkng_ref, dqraw, dkvraw):
        i = pl.program_id(0)
        r1, xn1, qn, qraw, r2, xn2, kvn, kvraw = _mla_project(xq_ref, xkv_ref, qg_ref, wq_ref, kvg_ref, wkv_ref)
        c, spv, snv = c_ref[...], sp_ref[...], sn_ref[...]
        pe = pe_ref[...]
        pe_ss = jnp.sum(pe * pe, axis=-1, keepdims=True)
        qngv, kngv = qng_ref[...], kng_ref[...]
        dqng = jnp.zeros((1, MLA_QKP), F32)
        dkng = jnp.zeros((1, MLA_QKP), F32)
        dpe = jnp.zeros_like(pe)
        for h in range(MLA_HEADS):
            b = h * MLA_QKP
            qh = qraw[:, b:b + MLA_QKP]
            r = lax.rsqrt(jnp.sum(qh * qh, axis=-1, keepdims=True) * (1.0 / MLA_QK) + EPS)
            xn = qh * r
            d_n = jnp.concatenate(
                [dq_ref[:, b:b + 128], _unrope64(dq_ref[:, b + 128:b + 256], c, spv, snv)], axis=1) * MLA_SCALE
            dqng = dqng + jnp.sum(d_n * xn, axis=0, keepdims=True)
            dxn = d_n * qngv
            dqraw[:, b:b + MLA_QKP] = _bf(r * (dxn - xn * (jnp.sum(dxn * xn, axis=-1, keepdims=True) * (1.0 / MLA_QK))))
            kn = kvraw[:, b:b + 128]
            rk = lax.rsqrt((jnp.sum(kn * kn, axis=-1, keepdims=True) + pe_ss) * (1.0 / MLA_QK) + EPS)
            xk = jnp.concatenate([kn, pe], axis=1) * rk
            d_k = jnp.concatenate(
                [dk_ref[:, b:b + 128], _unrope64(dk_ref[:, b + 128:b + 256], c, spv, snv)], axis=1)
            dkng = dkng + jnp.sum(d_k * xk, axis=0, keepdims=True)
            dxk = d_k * kngv
            dfull = rk * (dxk - xk * (jnp.sum(dxk * xk, axis=-1, keepdims=True) * (1.0 / MLA_QK)))
            dkvraw[:, b:b + 128] = _bf(dfull[:, :128])
            dkvraw[:, b + 128:b + 256] = _bf(dv_ref[:, h * MLA_V:(h + 1) * MLA_V])
            dpe = dpe + dfull[:, 128:]
        dpe_ref[...] = _bf(dpe)
        dqr, dkvr = dqraw[...], dkvraw[...]
        dqn = _dot(dqr, wq_ref[...], 1, 1)
        dxn1 = dqn * qg_ref[...]
        dxq_ref[...] = _bf(r1 * (dxn1 - xn1 * jnp.mean(dxn1 * xn1, axis=-1, keepdims=True)))
        dkvn = _dot(dkvr, wkv_ref[...], 1, 1)
        dxn2 = dkvn * kvg_ref[...]
        dxkv_ref[...] = _bf(r2 * (dxn2 - xn2 * jnp.mean(dxn2 * xn2, axis=-1, keepdims=True)))
        parts = (_dot(qn, dqr, 0, 0), _dot(kvn, dkvr, 0, 0), jnp.sum(dqn * xn1, axis=0, keepdims=True),
                 jnp.sum(dkvn * xn2, axis=0, keepdims=True), dqng, dkng)
        accs = (dwq_ref, dwkv_ref, dqg_ref, dkvg_ref, dqng_ref, dkng_ref)

        @pl.when(i == 0)
        def _():
            for a, p in zip(accs, parts):
                a[...] = p

        @pl.when(i > 0)
        def _():
            for a, p in zip(accs, parts):
                a[...] += p

    row = lambda w: pl.BlockSpec((tm, w), lambda i: (i, 0))
    full = lambda r, c: pl.BlockSpec((r, c), lambda i: (0, 0))
    return pl.pallas_call(
        body, name=name, grid=(s // tm,),
        in_specs=[row(1024), row(1024), row(512)] + _mla_specs(tm),
        out_specs=[row(512), row(256), row(128), full(512, 1024), full(256, 1024), full(1, 512), full(1, 256),
                   full(1, 256), full(1, 256)],
        out_shape=[jax.ShapeDtypeStruct((s, 512), BF16), jax.ShapeDtypeStruct((s, 256), BF16),
                   jax.ShapeDtypeStruct((s, 128), BF16), jax.ShapeDtypeStruct((512, 1024), F32),
                   jax.ShapeDtypeStruct((256, 1024), F32), jax.ShapeDtypeStruct((1, 512), F32),
                   jax.ShapeDtypeStruct((1, 256), F32), jax.ShapeDtypeStruct((1, 256), F32),
                   jax.ShapeDtypeStruct((1, 256), F32)],
        scratch_shapes=[pltpu.VMEM((tm, 1024), BF16), pltpu.VMEM((tm, 1024), BF16)],
        compiler_params=_cparams("arbitrary"),
    )(dq, dk, dv, z, z, z, qg, wq, kvg, wkv, qng, kng, cos, sp, sn)


def _flash_fwd(q, k, v, *, name, tq=1024, tk=1024, rider=None):
    s = q.shape[0]
    tq, tk = min(tq, s), min(tk, s)
    nk = s // tk
    strip = min(FLASH_STRIP, tq)

    def body(q_ref, k_ref, v_ref, o_ref, lse_ref, m_s, l_s, acc):
        j = pl.program_id(2)

        @pl.when(j == 0)
        def _():
            m_s[...] = jnp.full_like(m_s, -jnp.inf)
            l_s[...] = jnp.zeros_like(l_s)
            acc[...] = jnp.zeros_like(acc)

        for r in range(tq // strip):
            rows = slice(r * strip, (r + 1) * strip)
            sc = _dot(q_ref[rows, :], k_ref[...], 1, 1)
            m_prev = m_s[rows, :]
            m_new = jnp.maximum(m_prev, jnp.max(sc, axis=-1, keepdims=True))
            p = jnp.exp(sc - m_new[:, 0:1])
            alpha = jnp.exp(m_prev - m_new)
            l_s[rows, :] = alpha * l_s[rows, :] + jnp.sum(p, axis=-1, keepdims=True)
            acc[rows, :] = alpha * acc[rows, :] + _dot(p, v_ref[...])
            m_s[rows, :] = m_new

        @pl.when(j == nk - 1)
        def _():
            o_ref[...] = acc[...] / l_s[...]
            lse_ref[...] = m_s[...] + jnp.log(l_s[...])

    (o, lse), rode = _ride_call(
        body, rider, name=name, grid=(MLA_HEADS, s // tq, nk),
        in_specs=[pl.BlockSpec((tq, MLA_QKP), lambda h, i, j: (i, h)),
                  pl.BlockSpec((tk, MLA_QKP), lambda h, i, j: (j, h)),
                  pl.BlockSpec((tk, MLA_V), lambda h, i, j: (j, h))],
        out_specs=[pl.BlockSpec((tq, MLA_V), lambda h, i, j: (i, h))] * 2,
        out_shape=[jax.ShapeDtypeStruct((s, GROUP_W), F32)] * 2,
        scratch_shapes=[pltpu.VMEM((tq, MLA_V), F32), pltpu.VMEM((tq, MLA_V), F32), pltpu.VMEM((tq, MLA_V), F32)],
        args=(q, k, v), sem=("parallel", "parallel", "arbitrary"))
    return (o, lse) if rider is None else (o, lse, rode)


def _flash_bwd(q, k, v, do, o, lse, *, name, tq=1024, tk=1024, rider=None):
    s = q.shape[0]
    tq, tk = min(tq, s), min(tk, s)
    nq, nk = s // tq, s // tk

    def body(q_ref, k_ref, v_ref, do_ref, o_ref, lse_ref, dq_ref, dk_ref, dv_ref, dk_acc, dv_acc):
        j, i = pl.program_id(1), pl.program_id(2)
        dov = do_ref[...]
        delta = jnp.sum(dov * o_ref[...], axis=-1, keepdims=True)
        p = jnp.exp(_dot(q_ref[...], k_ref[...], 1, 1) - lse_ref[:, 0:1])
        ds = p * (_dot(dov, v_ref[...], 1, 1) - delta)
        pv = _dot(p, dov, 0, 0)
        pk = _dot(ds, q_ref[...], 0, 0)
        pq = _dot(ds, k_ref[...])
        rows = pl.ds(pl.multiple_of(i * tq, tq), tq)

        @pl.when(j == 0)
        def _():
            dq_ref[rows, :] = pq

        @pl.when(j > 0)
        def _():
            dq_ref[rows, :] += pq

        @pl.when(i == 0)
        def _():
            dv_acc[...] = pv
            dk_acc[...] = pk

        @pl.when(i > 0)
        def _():
            dv_acc[...] += pv
            dk_acc[...] += pk

        @pl.when(i == nq - 1)
        def _():
            dk_ref[...] = dk_acc[...]
            dv_ref[...] = dv_acc[...]

    qb = pl.BlockSpec((tq, MLA_QKP), lambda h, j, i: (i, h))
    kb = pl.BlockSpec((tk, MLA_QKP), lambda h, j, i: (j, h))
    vb = pl.BlockSpec((tk, MLA_V), lambda h, j, i: (j, h))
    ob = pl.BlockSpec((tq, MLA_V), lambda h, j, i: (i, h))
    (dq, dk, dv), rode = _ride_call(
        body, rider, name=name, grid=(MLA_HEADS, nk, nq),
        in_specs=[qb, kb, vb, ob, ob, ob],
        out_specs=[pl.BlockSpec((s, MLA_QKP), lambda h, j, i: (0, h)), kb, vb],
        out_shape=[jax.ShapeDtypeStruct((s, MLA_HEADS * MLA_QKP), F32),
                   jax.ShapeDtypeStruct((s, MLA_HEADS * MLA_QKP), F32), jax.ShapeDtypeStruct((s, GROUP_W), F32)],
        scratch_shapes=[pltpu.VMEM((tk, MLA_QKP), F32), pltpu.VMEM((tk, MLA_V), F32)],
        args=(q, k, v, do, o, lse), sem=("arbitrary", "arbitrary", "arbitrary"))
    return (dq, dk, dv) if rider is None else (dq, dk, dv, rode)


def _rows_tile(r, c, itemsize=4, budget=2 * 1024 * 1024):
    if r * c * itemsize <= budget:
        return r
    best = None
    for t in range(8, r, 8):
        if r % t == 0 and t * c * itemsize <= budget:
            best = t
    return best if best is not None else r


def _add_n(arrs, *, out_dtype=F32, name):
    shape = arrs[0].shape
    c = shape[-1]
    flat = [a.reshape(-1, c) for a in arrs]
    r = flat[0].shape[0]
    t = _rows_tile(r, c)

    def body(*refs):
        acc = refs[0][...].astype(F32)
        for ref in refs[1:-1]:
            acc = acc + ref[...].astype(F32)
        refs[-1][...] = acc.astype(out_dtype)

    blk = pl.BlockSpec((t, c), lambda i: (i, 0))
    out = pl.pallas_call(
        body, name=name, grid=(r // t,), in_specs=[blk] * len(flat), out_specs=blk,
        out_shape=jax.ShapeDtypeStruct((r, c), out_dtype), compiler_params=_cparams("parallel"),
    )(*flat)
    return out.reshape(shape)


def _adamw(w, g, m, v, *, name):
    shape = w.shape
    c = shape[-1]
    flat = [a.reshape(-1, c) for a in (w, g, m, v)]
    r = flat[0].shape[0]
    t = _rows_tile(r, c, budget=1024 * 1024)

    def body(w_ref, g_ref, m_ref, v_ref, d_ref, mo_ref, vo_ref):
        gv = g_ref[...]
        m2 = ADAM_B1 * m_ref[...] + (1.0 - ADAM_B1) * gv
        v2 = ADAM_B2 * v_ref[...] + (1.0 - ADAM_B2) * (gv * gv)
        m_hat = m2 / (1.0 - ADAM_B1 ** ADAM_STEP)
        v_hat = v2 / (1.0 - ADAM_B2 ** ADAM_STEP)
        d_ref[...] = -ADAM_LR * (m_hat / (jnp.sqrt(v_hat) + ADAM_EPS) + ADAM_WD * w_ref[...])
        mo_ref[...] = m2
        vo_ref[...] = v2

    blk = pl.BlockSpec((t, c), lambda i: (i, 0))
    outs = pl.pallas_call(
        body, name=name, grid=(r // t,), in_specs=[blk] * 4, out_specs=[blk] * 3,
        out_shape=[jax.ShapeDtypeStruct((r, c), F32)] * 3, compiler_params=_cparams("parallel"),
    )(*flat)
    return tuple(o.reshape(shape) for o in outs)


def _place():
    x, y, c = lax.axis_index("x"), lax.axis_index("y"), lax.axis_index("c")
    chips = [(1 - x, y), (x, 1 - y), (1 - x, 1 - y)]
    return x, y, c, chips


ANY = pl.BlockSpec(memory_space=pl.ANY)


def _half(ref, axis, hc, lead=()):
    n = ref.shape[len(lead) + axis] // 2
    return ref.at[tuple(lead) + (slice(None),) * axis + (pl.ds(hc * n, n),)]


def _gather_shards(shards, axes, *, name):
    nt = len(shards)

    def body(*refs):
        src, dst = refs[:nt], refs[nt:2 * nt]
        send, recv, fsend, frecv, lsem = refs[2 * nt:]
        x, y, c, chips = _place()
        me = 2 * x + y
        local = [pltpu.make_async_copy(src[t], dst[t].at[me], lsem.at[t]) for t in range(nt)]
        for cp in local:
            cp.start()

        def half(t, slot, hc):
            return _half(dst[t], axes[t], hc, lead=(slot,))

        def first(t, k):
            return pltpu.make_async_remote_copy(
                src_ref=_half(src[t], axes[t], c), dst_ref=half(t, me, c),
                send_sem=send.at[t, k], recv_sem=recv.at[t, k],
                device_id=(chips[k][0], chips[k][1], c), device_id_type=MESH)

        def landed(t, k):
            slot = 2 * chips[k][0] + chips[k][1]
            return pltpu.make_async_remote_copy(
                src_ref=half(t, slot, c), dst_ref=half(t, slot, c),
                send_sem=send.at[t, k], recv_sem=recv.at[t, k],
                device_id=(chips[k][0], chips[k][1], c), device_id_type=MESH)

        def forward(t, k, hc):
            slot = 2 * chips[k][0] + chips[k][1]
            return pltpu.make_async_remote_copy(
                src_ref=half(t, slot, hc), dst_ref=half(t, slot, hc),
                send_sem=fsend.at[t, k], recv_sem=frecv.at[t, k],
                device_id=(x, y, 1 - c), device_id_type=MESH)

        for t in range(nt):
            for k in range(3):
                first(t, k).start()
        for t in range(nt):
            for k in range(3):
                landed(t, k).wait_recv()
                forward(t, k, c).start()
        for t in range(nt):
            for k in range(3):
                forward(t, k, 1 - c).wait_recv()
        for t in range(nt):
            for k in range(3):
                first(t, k).wait_send()
                forward(t, k, c).wait_send()
        for cp in local:
            cp.wait()

    return pl.pallas_call(
        body, name=name, in_specs=[ANY] * nt, out_specs=[ANY] * nt,
        out_shape=[jax.ShapeDtypeStruct((N_CHIP,) + a.shape, a.dtype) for a in shards],
        scratch_shapes=[pltpu.SemaphoreType.DMA((nt, 3)), pltpu.SemaphoreType.DMA((nt, 3)),
                        pltpu.SemaphoreType.DMA((nt, 3)), pltpu.SemaphoreType.DMA((nt, 3)),
                        pltpu.SemaphoreType.DMA((nt,))],
    )(*shards)


def _comm_rows(hr, c, budget=2 * 1024 * 1024):
    if hr * c * 4 <= budget:
        return hr
    best = None
    for t in range(16, hr, 16):
        if hr % t == 0 and t * c * 4 <= budget:
            best = t
    return best if best is not None else hr


def _comm_cols(r, hc, budget=2 * 1024 * 1024):
    best = 128
    for t in range(128, hc + 1, 128):
        if hc % t == 0 and r * t * 4 <= budget:
            best = t
    return best


def _comm_chunks(shape, axis):
    r, cdim = shape
    if axis == 0:
        rc = _comm_rows(r // 2, cdim)
        nt = (r // 2) // rc
        return (rc, cdim), nt, (lambda h, t: (h * nt + t, 0))
    cc = _comm_cols(r, cdim // 2)
    nt = (cdim // 2) // cc
    return (r, cc), nt, (lambda h, t: (0, h * nt + t))


def _pair_reduce(g, where, axis, *, out_dtype, name):
    n_slot, r, cdim = g.shape
    blk_shape, nr, at = _comm_chunks((r, cdim), axis)
    steps = n_slot * nr
    half_shape = (r // 2, cdim) if axis == 0 else (r, cdim // 2)

    def body(w_ref, a_ref, b_ref, o_ref, land, send, recv, credit):
        x, y, c, _ = _place()
        sib = (x, y, 1 - c)
        i = pl.program_id(0) * nr + pl.program_id(1)
        s = lax.rem(i, 2)

        @pl.when(i >= 2)
        def _():
            pl.semaphore_wait(credit.at[s], 1)

        cp = pltpu.make_async_remote_copy(src_ref=b_ref.at[0], dst_ref=land.at[s], send_sem=send.at[s],
                                          recv_sem=recv.at[s], device_id=sib, device_id_type=MESH)
        cp.start()
        cp.wait_recv()
        o_ref[0] = (a_ref[0] + land[s]).astype(out_dtype)
        cp.wait_send()

        @pl.when(i + 2 < steps)
        def _():
            pl.semaphore_signal(credit.at[s], inc=1, device_id=sib, device_id_type=MESH)

    blk = lambda half: pl.BlockSpec((1,) + blk_shape, lambda j, t, w: (j,) + at(half(w), t))
    grid_spec = pltpu.PrefetchScalarGridSpec(
        num_scalar_prefetch=1, grid=(n_slot, nr),
        in_specs=[blk(lambda w: w[0]), blk(lambda w: 1 - w[0])],
        out_specs=pl.BlockSpec((1,) + blk_shape, lambda j, t, w: (j,) + at(0, t)),
        scratch_shapes=[pltpu.VMEM((2,) + blk_shape, F32), pltpu.SemaphoreType.DMA((2,)),
                        pltpu.SemaphoreType.DMA((2,)), pltpu.SemaphoreType.REGULAR((2,))])
    return pl.pallas_call(
        body, name=name, grid_spec=grid_spec, out_shape=jax.ShapeDtypeStruct((n_slot,) + half_shape, out_dtype),
        compiler_params=_cparams("arbitrary", "arbitrary"),
    )(where, g, g)


def _chip_exchange(parts, *, name):
    nt = len(parts)

    def body(*refs):
        src, got = refs[:nt], refs[nt:2 * nt]
        send, recv = refs[2 * nt:]
        x, y, c, chips = _place()
        remote = []
        for t in range(nt):
            for k in range(3):
                remote.append(pltpu.make_async_remote_copy(
                    src_ref=src[t].at[2 * chips[k][0] + chips[k][1]], dst_ref=got[t].at[k],
                    send_sem=send.at[t, k], recv_sem=recv.at[t, k],
                    device_id=(chips[k][0], chips[k][1], c), device_id_type=MESH))
        for cp in remote:
            cp.start()
        for cp in remote:
            cp.wait_recv()
        for cp in remote:
            cp.wait_send()

    return pl.pallas_call(
        body, name=name, in_specs=[ANY] * nt, out_specs=[ANY] * nt,
        out_shape=[jax.ShapeDtypeStruct((3,) + a.shape[1:], a.dtype) for a in parts],
        scratch_shapes=[pltpu.SemaphoreType.DMA((nt, 3)), pltpu.SemaphoreType.DMA((nt, 3))],
    )(*parts)


def _sum_join(p, got, where, axis, *, name):
    _, hr, cdim = p.shape
    full = (2 * hr, cdim) if axis == 0 else (hr, 2 * cdim)
    blk_shape, n, at = _comm_chunks(full, axis)
    step_len = blk_shape[axis]
    half_len = full[axis] // 2

    def body(w_ref, p_ref, g_ref, out, buf, lsem, ssem, rsem):
        x, y, c, _ = _place()
        sib = (x, y, 1 - c)
        r = pl.program_id(0)

        def part(start, size):
            return out.at[(slice(None),) * axis + (pl.ds(start, size),)]

        def copies(step, slot):
            rows = part(pl.multiple_of(c * half_len + step * step_len, 8 if axis == 0 else 128), step_len)
            return (pltpu.make_async_copy(buf.at[slot], rows, lsem.at[slot]),
                    pltpu.make_async_remote_copy(src_ref=buf.at[slot], dst_ref=rows, send_sem=ssem.at[slot],
                                                 recv_sem=rsem, device_id=sib, device_id_type=MESH))

        s = lax.rem(r, 2)

        @pl.when(r >= 2)
        def _():
            lc, rm = copies(r - 2, s)
            lc.wait()
            rm.wait_send()

        buf[s] = p_ref[0].astype(F32) + g_ref[0].astype(F32) + g_ref[1].astype(F32) + g_ref[2].astype(F32)
        lc, rm = copies(r, s)
        lc.start()
        rm.start()

        @pl.when(r == n - 1)
        def _():
            for step in range(max(0, n - 2), n):
                lc, rm = copies(step, step % 2)
                lc.wait()
                rm.wait_send()
            whole = part(0, half_len)
            pltpu.make_async_remote_copy(src_ref=whole, dst_ref=whole, send_sem=ssem.at[0], recv_sem=rsem,
                                         device_id=sib, device_id_type=MESH).wait_recv()

    grid_spec = pltpu.PrefetchScalarGridSpec(
        num_scalar_prefetch=1, grid=(n,),
        in_specs=[pl.BlockSpec((1,) + blk_shape, lambda t, w: (w[1],) + at(0, t)),
                  pl.BlockSpec((3,) + blk_shape, lambda t, w: (0,) + at(0, t))],
        out_specs=ANY,
        scratch_shapes=[pltpu.VMEM((2,) + blk_shape, F32), pltpu.SemaphoreType.DMA((2,)),
                        pltpu.SemaphoreType.DMA((2,)), pltpu.SemaphoreType.DMA])
    return pl.pallas_call(
        body, name=name, grid_spec=grid_spec, out_shape=jax.ShapeDtypeStruct(full, F32),
        compiler_params=_cparams("arbitrary"),
    )(where, p, got)


def _rider_gather_send(shards, axes):
    nt = len(shards)

    def copies(src, dst, send, recv, lsem):
        x, y, c, chips = _place()
        me = 2 * x + y
        local = [pltpu.make_async_copy(src[t], dst[t].at[me], lsem.at[t]) for t in range(nt)]
        out, landed = [], []
        for t in range(nt):
            for k in range(3):
                peer = (chips[k][0], chips[k][1], c)
                out.append(pltpu.make_async_remote_copy(
                    src_ref=_half(src[t], axes[t], c), dst_ref=_half(dst[t], axes[t], c, lead=(me,)),
                    send_sem=send.at[t, k], recv_sem=recv.at[t, k], device_id=peer, device_id_type=MESH))
                theirs = _half(dst[t], axes[t], c, lead=(2 * chips[k][0] + chips[k][1],))
                landed.append(pltpu.make_async_remote_copy(
                    src_ref=theirs, dst_ref=theirs, send_sem=send.at[t, k], recv_sem=recv.at[t, k],
                    device_id=peer, device_id_type=MESH))
        return local, out, landed

    def start(src, dst, sems):
        local, out, _ = copies(src, dst, *sems)
        for cp in local + out:
            cp.start()

    def finish(src, dst, sems):
        local, out, landed = copies(src, dst, *sems)
        for cp in landed:
            cp.wait_recv()
        for cp in out:
            cp.wait_send()
        for cp in local:
            cp.wait()

    return _Rider(shards, [jax.ShapeDtypeStruct((N_CHIP,) + a.shape, a.dtype) for a in shards],
                  [pltpu.SemaphoreType.DMA((nt, 3)), pltpu.SemaphoreType.DMA((nt, 3)), pltpu.SemaphoreType.DMA((nt,))],
                  start, finish)


def _rider_gather_forward(bufs, axes):
    nt = len(bufs)

    def copies(src, dst, send, recv):
        x, y, c, chips = _place()
        mine, theirs = [], []
        for t in range(nt):
            for k in range(3):
                slot = 2 * chips[k][0] + chips[k][1]
                for hc, into in ((c, mine), (1 - c, theirs)):
                    into.append(pltpu.make_async_remote_copy(
                        src_ref=_half(src[t], axes[t], hc, lead=(slot,)),
                        dst_ref=_half(dst[t], axes[t], hc, lead=(slot,)),
                        send_sem=send.at[t, k], recv_sem=recv.at[t, k], device_id=(x, y, 1 - c), device_id_type=MESH))
        return mine, theirs

    def start(src, dst, sems):
        for cp in copies(src, dst, *sems)[0]:
            cp.start()

    def finish(src, dst, sems):
        mine, theirs = copies(src, dst, *sems)
        for cp in theirs:
            cp.wait_recv()
        for cp in mine:
            cp.wait_send()

    return _Rider(bufs, [jax.ShapeDtypeStruct(a.shape, a.dtype) for a in bufs],
                  [pltpu.SemaphoreType.DMA((nt, 3)), pltpu.SemaphoreType.DMA((nt, 3))], start, finish,
                  aliases={t: t for t in range(nt)})


def _rider_chip_exchange(parts):
    nt = len(parts)

    def copies(src, got, send, recv):
        x, y, c, chips = _place()
        return [pltpu.make_async_remote_copy(
            src_ref=src[t].at[2 * chips[k][0] + chips[k][1]], dst_ref=got[t].at[k], send_sem=send.at[t, k],
            recv_sem=recv.at[t, k], device_id=(chips[k][0], chips[k][1], c), device_id_type=MESH)
            for t in range(nt) for k in range(3)]

    def start(src, got, sems):
        for cp in copies(src, got, *sems):
            cp.start()

    def finish(src, got, sems):
        remote = copies(src, got, *sems)
        for cp in remote:
            cp.wait_recv()
        for cp in remote:
            cp.wait_send()

    return _Rider(parts, [jax.ShapeDtypeStruct((3,) + a.shape[1:], a.dtype) for a in parts],
                  [pltpu.SemaphoreType.DMA((nt, 3)), pltpu.SemaphoreType.DMA((nt, 3))], start, finish)


def _gather_all(block, *, name):
    m_per, n = block.shape

    def body(x_ref, out_ref, send_sems, recv_sems, local_sem):
        x, y, c, chips = _place()
        me, sibling = (x, y, c), (x, y, 1 - c)

        def rows(px, py, pc):
            return out_ref.at[4 * px + 2 * py + pc]

        def copy(k, blk, to, src=None):
            return pltpu.make_async_remote_copy(
                src_ref=rows(*blk) if src is None else src, dst_ref=rows(*blk),
                send_sem=send_sems.at[k], recv_sem=recv_sems.at[k], device_id=to, device_id_type=MESH)

        mine = pltpu.make_async_copy(x_ref, rows(*me), local_sem)
        mine.start()
        first = [copy(0, me, sibling, src=x_ref)]
        first += [copy(1 + j, me, (*chip, c), src=x_ref) for j, chip in enumerate(chips)]
        for cp in first:
            cp.start()
        passed = [copy(4 + j, (*chip, c), sibling) for j, chip in enumerate(chips)]
        for j, chip in enumerate(chips):
            copy(1 + j, (*chip, c), me).wait_recv()
            passed[j].start()
        copy(0, sibling, me).wait_recv()
        for j, chip in enumerate(chips):
            copy(4 + j, (*chip, 1 - c), me).wait_recv()
        for cp in first + passed:
            cp.wait_send()
        mine.wait()

    return pl.pallas_call(
        body, name=name,
        out_shape=jax.ShapeDtypeStruct((N_DEV, m_per, n), block.dtype),
        in_specs=[pl.BlockSpec(memory_space=pltpu.VMEM)], out_specs=pl.BlockSpec(memory_space=pltpu.VMEM),
        scratch_shapes=[pltpu.SemaphoreType.DMA((7,)), pltpu.SemaphoreType.DMA((7,)), pltpu.SemaphoreType.DMA],
        compiler_params=pltpu.CompilerParams(vmem_limit_bytes=VMEM_LIMIT),
    )(block)


def _sum_slots(slots, *, name):
    n, m, c = slots.shape
    t = _rows_tile(m, c * n)

    def body(s_ref, o_ref):
        acc = s_ref[0]
        for k in range(1, n):
            acc = acc + s_ref[k]
        o_ref[...] = acc

    return pl.pallas_call(
        body, name=name, grid=(m // t,), in_specs=[pl.BlockSpec((n, t, c), lambda i: (0, i, 0))],
        out_specs=pl.BlockSpec((t, c), lambda i: (i, 0)), out_shape=jax.ShapeDtypeStruct((m, c), F32),
        compiler_params=_cparams("parallel"),
    )(slots)


def _pad_rows(a, rows):
    return a if a.shape[0] == rows else jnp.pad(a, ((0, rows - a.shape[0]), (0, 0)))


def _w_in_padded(shards):
    full = shards.reshape(IN_COLS, shards.shape[2])
    return jnp.concatenate([_pad_rows(full[SEG[n][2]:SEG[n][2] + SEG[n][3]], SEG[n][1]) for n in SEG_ORDER], axis=0)


def _w_in_unpadded(gp):
    full = jnp.concatenate([gp[SEG[n][0]:SEG[n][0] + SEG[n][3]] for n in ORIG_ORDER], axis=0)
    return full.reshape(N_CHIP, IN_COLS // N_CHIP, gp.shape[1])


def _pad_heads(w, true_w, pad_w):
    r = w.shape[0]
    h = w.shape[1] // true_w
    return jnp.pad(w.reshape(r, h, true_w), ((0, 0), (0, 0), (0, pad_w - true_w))).reshape(r, h * pad_w)


def _unpad_heads(w, true_w, pad_w):
    r = w.shape[0]
    h = w.shape[1] // pad_w
    return w.reshape(r, h, pad_w)[:, :, :true_w].reshape(r, h * true_w)


def _cols_to_slots(a):
    return a.reshape(a.shape[0], N_CHIP, a.shape[1] // N_CHIP).transpose(1, 0, 2)


def _slots_to_cols(a):
    return jnp.concatenate([a[j] for j in range(N_CHIP)], axis=1)


def _to_heads(a, h, d):
    return a.reshape(a.shape[0], h, d).transpose(1, 0, 2)


def _from_heads(a):
    return a.transpose(1, 0, 2).reshape(a.shape[1], -1)


SMALL = [("norm_g", 2048), ("ret_norm_g", 512), ("gla_ba_f", 256), ("gla_ba_b", 256), ("gla_norm_g", 512),
         ("pool_w", 4 * 128 * 128), ("pool_scale", 512), ("mla_q_norm_g", 512), ("mla_kv_norm_g", 256),
         ("mla_qk_norm_q", 192), ("mla_qk_norm_k", 192)]


def _pack_small(vals):
    parts = []
    for name, n in SMALL:
        parts += [v.reshape(-1) for v in vals[name]]
        if (DEPTH * n) % 1024:
            parts.append(jnp.zeros((-(DEPTH * n)) % 1024, F32))
    parts += [vals["loss"].reshape(-1), jnp.zeros(1023, F32)]
    return jnp.concatenate(parts).reshape(-1, 128)


def _unpack_small(block):
    flat = block.reshape(-1)
    out, off = {}, 0
    for name, n in SMALL:
        out[name] = flat[off:off + DEPTH * n]
        off += DEPTH * n + (-(DEPTH * n)) % 1024
    out["loss"] = flat[off]
    return out


def _layer_weights(l, p, g):
    wa = jnp.zeros((128, 512), F32)
    wa = wa.at[0:GLA_RANK, 0:256].set(_slots_to_cols(g["gla_wa2_f"]))
    wa = wa.at[GLA_RANK:2 * GLA_RANK, 256:512].set(_slots_to_cols(g["gla_wa2_b"]))
    return dict(
        norm_g=p["norm_g"][l][None, :],
        w_in=_w_in_padded(g["w_in"]),
        w_out=g["w_out"].reshape(4 * g["w_out"].shape[1], -1),
        ret_norm_g=p["ret_norm_g"][l][None, :],
        wa=_bf(wa),
        ba=jnp.concatenate([p["gla_ba_f"][l], p["gla_ba_b"][l]])[None, :],
        gla_norm_g=p["gla_norm_g"][l][None, :],
        pool_w=_bf(p["pool_w"][l]),
        pool_scale=p["pool_scale"][l][None, :],
        qg=p["mla_q_norm_g"][l][None, :],
        wq=_pad_heads(_slots_to_cols(g["mla_wq_b"]), MLA_QK, MLA_QKP),
        kvg=p["mla_kv_norm_g"][l][None, :],
        wkv=_slots_to_cols(g["mla_wkv_b"]),
        qng=jnp.pad(p["mla_qk_norm_q"][l], (0, MLA_QKP - MLA_QK))[None, :],
        kng=jnp.pad(p["mla_qk_norm_k"][l], (0, MLA_QKP - MLA_QK))[None, :],
    )


def _layer_fwd(l, x, w, tabs, next_shards=None):
    ret_cos, ret_sin, mla_cos, mla_sp, mla_sn = tabs
    nm = lambda s: f"l{l}_{s}"
    h = _rmsnorm_fwd(x, w["norm_g"], name=nm("norm"))
    if next_shards is None:
        z = _matmul(h, w["w_in"], tb=True, name=nm("in_proj"))
    else:
        z, landed = _matmul(h, w["w_in"], tb=True, rider=_rider_gather_send(next_shards[:1], SHARD_AXES[:1]),
                            name=nm("in_proj"))
    qr, kr = _ret_pre(z, ret_cos, ret_sin, name=nm("ret_pre"))
    ret_o = _bla(qr, kr, z, _ret_log_gamma(False), (0, 0, SEG["rv"][0] // 512), name=nm("ret_scan"))
    y_a = _post(ret_o, z, SEG["rg"][0] // 512, w["ret_norm_g"], norm=True, name=nm("ret_post"))
    la = _gla_gate(z, w["wa"], w["ba"], name=nm("gla_gate"))
    la_h = la.reshape(la.shape[0], 2, GLA_HEADS, GLA_DK).transpose(1, 2, 0, 3)
    gq = _to_heads(z[:, SEG["gq"][0]:SEG["gq"][0] + 256], GLA_HEADS, GLA_DK)
    gk = _to_heads(z[:, SEG["gk"][0]:SEG["gk"][0] + 256], GLA_HEADS, GLA_DK)
    if next_shards is None:
        gla_o, gla_st = _gla_fwd(gq, gk, z, la_h, name=nm("gla_scan"))
    else:
        gla_o, gla_st, more = _gla_fwd(gq, gk, z, la_h, rider=_rider_gather_send(next_shards[1:], SHARD_AXES[1:]),
                                       name=nm("gla_scan"))
        landed = list(landed) + list(more)
    y_b = _post(gla_o, z, SEG["gg"][0] // 512, w["gla_norm_g"], norm=True, name=nm("gla_post"))
    y_c = _pool_fwd(z, w["pool_w"], w["pool_scale"], name=nm("pool"))
    q, k, v = _mla_pre(z, w["qg"], w["wq"], w["kvg"], w["wkv"], w["qng"], w["kng"], mla_cos, mla_sp, mla_sn,
                       name=nm("mla_pre"))
    if next_shards is None:
        (att_o, lse), gathered = _flash_fwd(q, k, v, name=nm("attn")), None
    else:
        att_o, lse, gathered = _flash_fwd(q, k, v, rider=_rider_gather_forward(landed, SHARD_AXES), name=nm("attn"))
    y_d = _post([att_o], z, SEG["mg"][0] // 512, w["qg"], norm=False, name=nm("mla_post"))
    y = jnp.concatenate([y_a, y_b, y_c, y_d], axis=1)
    x_next = _matmul(y, w["w_out"], add=x, name=nm("out_proj"))
    saved = dict(x=x, h=h, z=z, y=y, qr=qr, kr=kr, ret_o=ret_o, la_h=la_h, gq=gq, gk=gk, gla_o=gla_o, gla_st=gla_st,
                 q=q, k=k, v=v, att_o=att_o, lse=lse)
    return x_next, saved, gathered


def _layer_bwd(l, dx_next, w, sv, tabs, riding_parts=None):
    ret_cos, ret_sin, mla_cos, mla_sp, mla_sn = tabs
    nm = lambda s: f"l{l}_{s}"
    z = sv["z"]
    dy = _matmul(dx_next, w["w_out"], tb=True, name=nm("out_proj_dy"))
    d_w_out = _matmul(sv["y"], dx_next, ta=True, tn=512, name=nm("out_proj_dw"))
    d_rg, d_ret_o, d_ret_g = _post_bwd(dy, 0, sv["ret_o"], z, SEG["rg"][0] // 512, w["ret_norm_g"], norm=True,
                                       name=nm("ret_post_bwd"))
    vcol = SEG["rv"][0] // 512
    dqr = _bla(d_ret_o, z, sv["kr"], _ret_log_gamma(False), (0, vcol, 0), name=nm("ret_scan_dq"))
    dkr = _bla(z, d_ret_o, sv["qr"], _ret_log_gamma(True), (vcol, 0, 0), name=nm("ret_scan_dk"))
    drv = _bla(sv["kr"], sv["qr"], d_ret_o, _ret_log_gamma(True), (0, 0, 0), name=nm("ret_scan_dv"))
    d_rq, d_rk = _ret_pre_bwd(dqr, dkr, ret_cos, ret_sin, name=nm("ret_pre_bwd"))
    d_rv = _add_n([drv[0], drv[1]], out_dtype=BF16, name=nm("ret_dv_sum"))
    d_gg, d_gla_o, d_gla_g = _post_bwd(dy, 1, sv["gla_o"], z, SEG["gg"][0] // 512, w["gla_norm_g"], norm=True,
                                       name=nm("gla_post_bwd"))
    dq2, dk2, dla2, dv2 = _gla_bwd(sv["gq"], sv["gk"], z, sv["la_h"], d_gla_o, sv["gla_st"], name=nm("gla_scan_bwd"))
    d_gq = _bf(_from_heads(dq2[0] + dq2[1]))
    d_gk = _bf(_from_heads(dk2[0] + dk2[1]))
    d_gv = _add_n([dv2[0], dv2[1]], out_dtype=BF16, name=nm("gla_dv_sum"))
    dla = jnp.concatenate([_from_heads(dla2[0]), _from_heads(dla2[1])], axis=1)
    d_ga, d_wa, d_ba = _gla_gate_bwd(dla, z, w["wa"], w["ba"], name=nm("gla_gate_bwd"))
    d_pv, d_pg, d_pool_w, d_pool_scale = _pool_bwd(dy, z, w["pool_w"], w["pool_scale"], name=nm("pool_bwd"))
    d_mg, d_att_o, _ = _post_bwd(dy, 3, [sv["att_o"]], z, SEG["mg"][0] // 512, w["qg"], norm=False,
                                 name=nm("mla_post_bwd"))
    if riding_parts is None:
        (dq, dk, dv), rode = _flash_bwd(sv["q"], sv["k"], sv["v"], d_att_o, sv["att_o"], sv["lse"],
                                        name=nm("attn_bwd")), None
    else:
        dq, dk, dv, rode = _flash_bwd(sv["q"], sv["k"], sv["v"], d_att_o, sv["att_o"], sv["lse"],
                                      rider=_rider_chip_exchange(riding_parts), name=nm("attn_bwd"))
    d_mq, d_mkv, d_mkr, d_wq, d_wkv, d_qg, d_kvg, d_qng, d_kng = _mla_pre_bwd(
        dq, dk, dv, z, w["qg"], w["wq"], w["kvg"], w["wkv"], w["qng"], w["kng"], mla_cos, mla_sp, mla_sn,
        name=nm("mla_pre_bwd"))
    segs = dict(rq=d_rq, rk=d_rk, rv=d_rv, rg=d_rg, gv=d_gv, gg=d_gg, pv=d_pv, pg=d_pg, mq=d_mq, mg=d_mg,
                gq=d_gq, gk=d_gk, mkv=d_mkv, ga=d_ga, mkr=d_mkr)
    dz = jnp.concatenate([segs[n] for n in SEG_ORDER], axis=1)
    dh = _matmul(dz, w["w_in"], tn=512, name=nm("in_proj_dh"))
    d_w_in = _matmul(dz, sv["h"], ta=True, name=nm("in_proj_dw"))
    dx, d_norm_g = _rmsnorm_bwd(sv["x"], dh, w["norm_g"], dx_next, name=nm("norm_bwd"))
    sharded = dict(
        w_in=_w_in_unpadded(d_w_in),
        w_out=d_w_out.reshape(N_CHIP, d_w_out.shape[0] // N_CHIP, d_w_out.shape[1]),
        mla_wq_b=_cols_to_slots(_unpad_heads(d_wq, MLA_QK, MLA_QKP)),
        mla_wkv_b=_cols_to_slots(d_wkv),
        gla_wa2_f=_cols_to_slots(d_wa[0:GLA_RANK, 0:256]),
        gla_wa2_b=_cols_to_slots(d_wa[GLA_RANK:2 * GLA_RANK, 256:512]),
    )
    small = dict(
        norm_g=d_norm_g[0], ret_norm_g=d_ret_g[0], gla_ba_f=d_ba[0, :256], gla_ba_b=d_ba[0, 256:],
        gla_norm_g=d_gla_g[0], pool_w=d_pool_w.reshape(-1), pool_scale=d_pool_scale[0], mla_q_norm_g=d_qg[0],
        mla_kv_norm_g=d_kvg[0], mla_qk_norm_q=d_qng[0, :MLA_QK], mla_qk_norm_k=d_kng[0, :MLA_QK],
    )
    return dx, sharded, small, rode


SHARDED = ["w_in", "w_out", "mla_wq_b", "mla_wkv_b", "gla_wa2_f", "gla_wa2_b"]
WEIGHTS = ["norm_g", "w_in", "ret_norm_g", "gla_wa2_f", "gla_ba_f", "gla_wa2_b", "gla_ba_b", "gla_norm_g", "pool_w",
           "pool_scale", "mla_q_norm_g", "mla_wq_b", "mla_kv_norm_g", "mla_wkv_b", "mla_qk_norm_q", "mla_qk_norm_k",
           "w_out"]


SHARD_AXES = [1, 0, 0, 0, 0, 0]


def _layer_shards(p, l):
    return [jnp.swapaxes(p["w_in"], 1, 2)[l].astype(BF16), p["w_out"][l].astype(BF16), p["mla_wq_b"][l].astype(BF16),
            p["mla_wkv_b"][l].astype(BF16), p["gla_wa2_f"][l], p["gla_wa2_b"][l]]


def _step(p, where):
    x = p["x"][0]
    tabs = _rope_tables(x.shape[0])
    got0 = _gather_shards(_layer_shards(p, 0), SHARD_AXES, name="l0_gather_weights")
    w0 = _layer_weights(0, p, dict(zip(SHARDED, got0)))
    x1, sv0, got1 = _layer_fwd(0, x, w0, tabs, next_shards=_layer_shards(p, 1))
    w1 = _layer_weights(1, p, dict(zip(SHARDED, got1)))
    x2, sv1, _ = _layer_fwd(1, x1, w1, tabs)
    dx, loss = _loss_head(x2, p["loss_target"][0], name="loss_head")

    big, big_axes = SHARDED[:2], SHARD_AXES[:2]

    def pair_sums(tag, tensors, axes, names):
        return [_pair_reduce(a, where, ax, out_dtype=BF16, name=f"{tag}_pair_reduce_{n}")
                for a, ax, n in zip(tensors, axes, names)]

    def joined(tag, pair, others, axes, names):
        return [_sum_join(a, b, where, ax, name=f"{tag}_sum_join_{n}")
                for a, b, ax, n in zip(pair, others, axes, names)]

    dx, sharded1, small1, _ = _layer_bwd(1, dx, w1, sv1, tabs)
    pair1 = pair_sums("l1", [sharded1[n] for n in big], big_axes, big)
    dx, sharded0, small0, others1 = _layer_bwd(0, dx, w0, sv0, tabs, riding_parts=pair1)
    grads1 = joined("l1", pair1, others1, big_axes, big)
    packed = jnp.concatenate([sh[n].reshape(N_CHIP, -1, 128) for sh in (sharded0, sharded1) for n in SHARDED[2:]],
                             axis=1)
    pair0 = pair_sums("l0", [sharded0[n] for n in big] + [packed], big_axes + [0], big + ["rest"])
    grads0 = joined("l0", pair0, _chip_exchange(pair0, name="l0_chip_exchange"), big_axes + [0], big + ["rest"])
    grads = {n: jnp.stack([g0, g1]) for n, g0, g1 in zip(big, grads0, grads1)}
    rest, off = grads0[2], 0
    pieces = {n: [] for n in SHARDED[2:]}
    for sh in (sharded0, sharded1):
        for n in SHARDED[2:]:
            rows = sh[n].shape[1] * sh[n].shape[2] // 128
            pieces[n].append(rest[off:off + rows].reshape(sh[n].shape[1:]))
            off += rows
    grads.update({n: jnp.stack(v) for n, v in pieces.items()})
    small = {n: [small0[n], small1[n]] for n, _ in SMALL}
    small["loss"] = loss
    return dx[None], grads, small


def kernel(x, norm_g, w_in, ret_norm_g, gla_wa2_f, gla_ba_f, gla_wa2_b, gla_ba_b, gla_norm_g, pool_w, pool_scale, mla_q_norm_g, mla_wq_b, mla_kv_norm_g, mla_wkv_b, mla_qk_norm_q, mla_qk_norm_k, w_out, loss_target, m_norm_g, m_w_in, m_ret_norm_g, m_gla_wa2_f, m_gla_ba_f, m_gla_wa2_b, m_gla_ba_b, m_gla_norm_g, m_pool_w, m_pool_scale, m_mla_q_norm_g, m_mla_wq_b, m_mla_kv_norm_g, m_mla_wkv_b, m_mla_qk_norm_q, m_mla_qk_norm_k, m_w_out, v_norm_g, v_w_in, v_ret_norm_g, v_gla_wa2_f, v_gla_ba_f, v_gla_wa2_b, v_gla_ba_b, v_gla_norm_g, v_pool_w, v_pool_scale, v_mla_q_norm_g, v_mla_wq_b, v_mla_kv_norm_g, v_mla_wkv_b, v_mla_qk_norm_q, v_mla_qk_norm_k, v_w_out):
    p = dict(x=x, norm_g=norm_g, w_in=w_in, ret_norm_g=ret_norm_g, gla_wa2_f=gla_wa2_f, gla_ba_f=gla_ba_f,
             gla_wa2_b=gla_wa2_b, gla_ba_b=gla_ba_b, gla_norm_g=gla_norm_g, pool_w=pool_w, pool_scale=pool_scale,
             mla_q_norm_g=mla_q_norm_g, mla_wq_b=mla_wq_b, mla_kv_norm_g=mla_kv_norm_g, mla_wkv_b=mla_wkv_b,
             mla_qk_norm_q=mla_qk_norm_q, mla_qk_norm_k=mla_qk_norm_k, w_out=w_out, loss_target=loss_target)
    moments = dict(
        m=dict(norm_g=m_norm_g, w_in=m_w_in, ret_norm_g=m_ret_norm_g, gla_wa2_f=m_gla_wa2_f, gla_ba_f=m_gla_ba_f,
               gla_wa2_b=m_gla_wa2_b, gla_ba_b=m_gla_ba_b, gla_norm_g=m_gla_norm_g, pool_w=m_pool_w,
               pool_scale=m_pool_scale, mla_q_norm_g=m_mla_q_norm_g, mla_wq_b=m_mla_wq_b,
               mla_kv_norm_g=m_mla_kv_norm_g, mla_wkv_b=m_mla_wkv_b, mla_qk_norm_q=m_mla_qk_norm_q,
               mla_qk_norm_k=m_mla_qk_norm_k, w_out=m_w_out),
        v=dict(norm_g=v_norm_g, w_in=v_w_in, ret_norm_g=v_ret_norm_g, gla_wa2_f=v_gla_wa2_f, gla_ba_f=v_gla_ba_f,
               gla_wa2_b=v_gla_wa2_b, gla_ba_b=v_gla_ba_b, gla_norm_g=v_gla_norm_g, pool_w=v_pool_w,
               pool_scale=v_pool_scale, mla_q_norm_g=v_mla_q_norm_g, mla_wq_b=v_mla_wq_b,
               mla_kv_norm_g=v_mla_kv_norm_g, mla_wkv_b=v_mla_wkv_b, mla_qk_norm_q=v_mla_qk_norm_q,
               mla_qk_norm_k=v_mla_qk_norm_k, w_out=v_w_out))

    where = jnp.stack([lax.axis_index("c"), 2 * lax.axis_index("x") + lax.axis_index("y")]).astype(jnp.int32)
    grad_x, grads, small = _step(p, where)

    slots = _gather_all(_pack_small(small), name="gather_small")
    total = _unpack_small(_sum_slots(slots, name="sum_small"))
    for n, _ in SMALL:
        grads[n] = total[n].reshape(p[n].shape)
    loss = total["loss"]

    delta, new_m, new_v = {}, {}, {}
    for n in WEIGHTS:
        turn = (lambda a: jnp.swapaxes(a, 1, 2)) if n == "w_in" else (lambda a: a)
        outs = _adamw(turn(p[n]), grads[n], turn(moments["m"][n]), turn(moments["v"][n]), name=f"adamw_{n}")
        grads[n] = turn(grads[n])
        delta[n], new_m[n], new_v[n] = (turn(o) for o in outs)
    return (loss, grad_x, *[grads[n] for n in WEIGHTS], *[delta[n] for n in WEIGHTS],
            *[new_m[n] for n in WEIGHTS], *[new_v[n] for n in WEIGHTS])
```

```python
import functools
import math

import jax
import jax.numpy as jnp
from jax import lax
from jax.experimental import pallas as pl
from jax.experimental.pallas import tpu as pltpu

F32 = jnp.float32
BF16 = jnp.bfloat16
MESH = pl.DeviceIdType.MESH

EPS = 1e-6
ROPE_THETA = 10000.0
DEPTH = 2
N_DEV = 8
N_CHIP = 4

GROUP_W = 512
RET_HEADS = 4
RET_HD = 128
RET_CHUNK = 256
GLA_HEADS = 4
GLA_DK = 64
GLA_DV = 128
GLA_RANK = 16
GLA_TAU = 16.0
GLA_CHUNK = 64
POOL_GROUPS = 4
POOL_GW = 128
POOL_HALO = 8
POOL_TILE = 256
MLA_HEADS = 4
MLA_NOPE = 128
MLA_ROPE = 64
MLA_QK = MLA_NOPE + MLA_ROPE
MLA_QKP = 256
MLA_V = 128
MLA_Q_RANK = 512
MLA_KV_RANK = 256
MLA_SCALE = MLA_QK ** -0.5
FLASH_STRIP = 1024

ADAM_LR = 0.001
ADAM_B1 = 0.9
ADAM_B2 = 0.999
ADAM_EPS = 1e-08
ADAM_WD = 0.01
ADAM_STEP = 10

VMEM_LIMIT = 56 * 1024 * 1024
ROW_TILE = 512

SEG = {
    "rq": (0, 512, 0, 512), "rk": (512, 512, 512, 512), "rv": (1024, 512, 1024, 512), "rg": (1536, 512, 1536, 512),
    "gv": (2048, 512, 2560, 512), "gg": (2560, 512, 3072, 512),
    "pv": (3072, 512, 3616, 512), "pg": (3584, 512, 4128, 512),
    "mq": (4096, 512, 4640, 512), "mg": (4608, 512, 5472, 512),
    "gq": (5120, 256, 2048, 256), "gk": (5376, 256, 2304, 256), "mkv": (5632, 256, 5152, 256),
    "ga": (5888, 128, 3584, 32), "mkr": (6016, 128, 5408, 64),
}
SEG_ORDER = ["rq", "rk", "rv", "rg", "gv", "gg", "pv", "pg", "mq", "mg", "gq", "gk", "mkv", "ga", "mkr"]
IN_COLS = 5984
IN_PAD = 6144
ORIG_ORDER = ["rq", "rk", "rv", "rg", "gq", "gk", "gv", "gg", "ga", "pv", "pg", "mq", "mkv", "mkr", "mg"]


def _cparams(*sem):
    return pltpu.CompilerParams(dimension_semantics=tuple(sem), vmem_limit_bytes=VMEM_LIMIT)


def _bf(v):
    return v.astype(BF16)


def _dot(a, b, ca=1, cb=0):
    return lax.dot_general(_bf(a), _bf(b), (((ca,), (cb,)), ((), ())), preferred_element_type=F32)


def _sigmoid(x):
    return 1.0 / (1.0 + jnp.exp(-x))


def _silu_parts(g):
    sg = _sigmoid(g)
    return g * sg, sg * (1.0 + g * (1.0 - sg))


class _Rider:
    def __init__(self, ins, outs, sems, start, finish, aliases=None):
        self.ins, self.outs, self.sems, self.start, self.finish = list(ins), list(outs), list(sems), start, finish
        self.aliases = dict(aliases or {})


def _ride(body, rider, n_in, n_out, grid):
    if rider is None:
        return body
    ri, ro, rs = len(rider.ins), len(rider.outs), len(rider.sems)

    def wrapped(*refs):
        ins, refs = refs[:n_in], refs[n_in:]
        rin, refs = refs[:ri], refs[ri:]
        outs, refs = refs[:n_out], refs[n_out:]
        rout, refs = refs[:ro], refs[ro:]
        scratch, sems = refs[:len(refs) - rs], refs[len(refs) - rs:]
        first = pl.program_id(0) == 0
        last = pl.program_id(0) == grid[0] - 1
        for ax in range(1, len(grid)):
            first = jnp.logical_and(first, pl.program_id(ax) == 0)
            last = jnp.logical_and(last, pl.program_id(ax) == grid[ax] - 1)

        @pl.when(first)
        def _():
            rider.start(rin, rout, sems)

        body(*ins, *outs, *scratch)

        @pl.when(last)
        def _():
            rider.finish(rin, rout, sems)

    return wrapped


def _ride_call(body, rider, *, name, grid, in_specs, out_specs, out_shape, scratch_shapes, args, sem):
    n_in, n_out = len(in_specs), len(out_specs)
    if rider is None:
        return pl.pallas_call(body, name=name, grid=grid, in_specs=in_specs, out_specs=out_specs, out_shape=out_shape,
                              scratch_shapes=scratch_shapes, compiler_params=_cparams(*sem))(*args), []
    outs = pl.pallas_call(
        _ride(body, rider, n_in, n_out, grid), name=name, grid=grid,
        in_specs=list(in_specs) + [ANY] * len(rider.ins), out_specs=list(out_specs) + [ANY] * len(rider.outs),
        out_shape=list(out_shape) + rider.outs, scratch_shapes=list(scratch_shapes) + rider.sems,
        input_output_aliases={n_in + i: n_out + o for i, o in rider.aliases.items()},
        compiler_params=_cparams(*(["arbitrary"] * len(grid))),
    )(*args, *rider.ins)
    return outs[:n_out], outs[n_out:]


def _matmul(a, b, *, ta=False, tb=False, out_dtype=F32, tm=512, tn=1024, tk=None, add=None, n_outer=True, rider=None,
            name):
    m, kdim = (a.shape[1], a.shape[0]) if ta else a.shape
    n = b.shape[0] if tb else b.shape[1]
    tm, tn = min(tm, m), min(tn, n)
    tk = kdim if tk is None else min(tk, kdim)
    assert m % tm == 0 and n % tn == 0 and kdim % tk == 0
    nk = kdim // tk
    ca, cb = (0 if ta else 1), (1 if tb else 0)

    def body(*refs):
        if add is None:
            a_ref, b_ref, o_ref = refs[:3]
            add_ref = None
        else:
            a_ref, b_ref, add_ref, o_ref = refs[:4]
        p = _dot(a_ref[...], b_ref[...], ca, cb)

        def finish(r):
            if add_ref is not None:
                r = r + add_ref[...]
            o_ref[...] = r.astype(out_dtype)

        if nk == 1:
            finish(p)
        else:
            acc = refs[-1]
            k = pl.program_id(2)

            @pl.when(k == 0)
            def _():
                acc[...] = p

            @pl.when(k > 0)
            def _():
                acc[...] += p

            @pl.when(k == nk - 1)
            def _():
                finish(acc[...])

    def ij(g0, g1):
        return (g1, g0) if n_outer else (g0, g1)

    a_spec = (pl.BlockSpec((tk, tm), lambda g0, g1, k: (k, ij(g0, g1)[0])) if ta
              else pl.BlockSpec((tm, tk), lambda g0, g1, k: (ij(g0, g1)[0], k)))
    b_spec = (pl.BlockSpec((tn, tk), lambda g0, g1, k: (ij(g0, g1)[1], k)) if tb
              else pl.BlockSpec((tk, tn), lambda g0, g1, k: (k, ij(g0, g1)[1])))
    o_spec = pl.BlockSpec((tm, tn), lambda g0, g1, k: ij(g0, g1))
    in_specs = [a_spec, b_spec] + ([o_spec] if add is not None else [])
    args = (a, b) + ((add,) if add is not None else ())
    grid = (n // tn, m // tm, nk) if n_outer else (m // tm, n // tn, nk)
    (out,), rode = _ride_call(
        body, rider, name=name, grid=grid, in_specs=in_specs, out_specs=[o_spec],
        out_shape=[jax.ShapeDtypeStruct((m, n), out_dtype)],
        scratch_shapes=[] if nk == 1 else [pltpu.VMEM((tm, tn), F32)], args=args,
        sem=("parallel", "parallel", "arbitrary"))
    return out if rider is None else (out, rode)


def _rmsnorm_fwd(x, g, *, name, tm=ROW_TILE):
    s, d = x.shape
    tm = min(tm, s)

    def body(x_ref, g_ref, h_ref):
        xv = x_ref[...]
        r = lax.rsqrt(jnp.mean(xv * xv, axis=-1, keepdims=True) + EPS)
        h_ref[...] = _bf(xv * r * g_ref[...])

    return pl.pallas_call(
        body, name=name, grid=(s // tm,),
        in_specs=[pl.BlockSpec((tm, d), lambda i: (i, 0)), pl.BlockSpec((1, d), lambda i: (0, 0))],
        out_specs=pl.BlockSpec((tm, d), lambda i: (i, 0)),
        out_shape=jax.ShapeDtypeStruct((s, d), BF16),
        compiler_params=_cparams("parallel"),
    )(x, g)


def _rmsnorm_bwd(x, dh, g, dres, *, name, tm=ROW_TILE):
    s, d = x.shape
    tm = min(tm, s)

    def body(x_ref, dh_ref, g_ref, dres_ref, dx_ref, dg_ref):
        i = pl.program_id(0)
        xv = x_ref[...]
        r = lax.rsqrt(jnp.mean(xv * xv, axis=-1, keepdims=True) + EPS)
        xn = xv * r
        dv = dh_ref[...]
        part = jnp.sum(dv * xn, axis=0, keepdims=True)

        @pl.when(i == 0)
        def _():
            dg_ref[...] = part

        @pl.when(i > 0)
        def _():
            dg_ref[...] += part

        dxn = dv * g_ref[...]
        dx_ref[...] = dres_ref[...] + r * (dxn - xn * jnp.mean(dxn * xn, axis=-1, keepdims=True))

    row = pl.BlockSpec((tm, d), lambda i: (i, 0))
    vec = pl.BlockSpec((1, d), lambda i: (0, 0))
    return pl.pallas_call(
        body, name=name, grid=(s // tm,), in_specs=[row, row, vec, row], out_specs=[row, vec],
        out_shape=[jax.ShapeDtypeStruct((s, d), F32), jax.ShapeDtypeStruct((1, d), F32)],
        compiler_params=_cparams("arbitrary"),
    )(x, dh, g, dres)


def _loss_head(xf, target, *, name, tm=ROW_TILE):
    s, d = xf.shape
    tm = min(tm, s)

    def body(x_ref, t_ref, dx_ref, l_ref):
        i = pl.program_id(0)
        e = x_ref[...] - t_ref[...]
        dx_ref[...] = e * (1.0 / d)
        rows = jnp.mean(e * e, axis=-1, keepdims=True)
        part = 0.5 * jnp.sum(rows, axis=0, keepdims=True)

        @pl.when(i == 0)
        def _():
            l_ref[...] = part

        @pl.when(i > 0)
        def _():
            l_ref[...] += part

    row = pl.BlockSpec((tm, d), lambda i: (i, 0))
    return pl.pallas_call(
        body, name=name, grid=(s // tm,), in_specs=[row, row],
        out_specs=[row, pl.BlockSpec((1, 1), lambda i: (0, 0))],
        out_shape=[jax.ShapeDtypeStruct((s, d), F32), jax.ShapeDtypeStruct((1, 1), F32)],
        compiler_params=_cparams("arbitrary"),
    )(xf, target)


def _rope_tables(s):
    pos = jnp.arange(s, dtype=F32)[:, None]
    inv_r = 1.0 / (ROPE_THETA ** (jnp.arange(0, RET_HD, 2, dtype=F32) / RET_HD))
    ang = pos * inv_r[None, :]
    ret_cos = jnp.concatenate([jnp.cos(ang), jnp.cos(ang)], axis=1)
    ret_sin = jnp.concatenate([-jnp.sin(ang), jnp.sin(ang)], axis=1)
    inv_m = 1.0 / (ROPE_THETA ** (jnp.arange(0, MLA_ROPE, 2, dtype=F32) / MLA_ROPE))
    am = pos * inv_m[None, :]
    z32, z64 = jnp.zeros((s, 32), F32), jnp.zeros((s, 64), F32)
    mla_cos = jnp.concatenate([jnp.cos(am), jnp.cos(am), z64], axis=1)
    mla_sp = jnp.concatenate([z32, jnp.sin(am), z64], axis=1)
    mla_sn = jnp.concatenate([-jnp.sin(am), z32, z64], axis=1)
    return ret_cos, ret_sin, mla_cos, mla_sp, mla_sn


def _rope128(x, c, sg):
    return x * c + pltpu.roll(x, 64, 1) * sg


def _unrope128(d, c, sg):
    return d * c + pltpu.roll(d * sg, 64, 1)


def _rope64(t, c, sp, sn):
    return t * c + pltpu.roll(t, 96, 1) * sn + pltpu.roll(t, 32, 1) * sp


def _unrope64(d, c, sp, sn):
    return d * c + pltpu.roll(d * sn, 32, 1) + pltpu.roll(d * sp, 96, 1)


def _ret_pre(z, cos, sin, *, name, tm=ROW_TILE):
    s = z.shape[0]
    tm = min(tm, s)
    scale = RET_HD ** -0.5

    def body(q_ref, k_ref, c_ref, s_ref, qo_ref, ko_ref):
        c, sg = c_ref[...], s_ref[...]
        for h in range(RET_HEADS):
            sl = slice(h * RET_HD, (h + 1) * RET_HD)
            qo_ref[:, sl] = _rope128(q_ref[:, sl], c, sg)
            ko_ref[:, sl] = _rope128(k_ref[:, sl], c, sg) * scale

    seg = lambda j: pl.BlockSpec((tm, GROUP_W), lambda i: (i, j))
    tab = pl.BlockSpec((tm, RET_HD), lambda i: (i, 0))
    return pl.pallas_call(
        body, name=name, grid=(s // tm,), in_specs=[seg(0), seg(1), tab, tab],
        out_specs=[seg(0), seg(0)],
        out_shape=[jax.ShapeDtypeStruct((s, GROUP_W), F32)] * 2,
        compiler_params=_cparams("parallel"),
    )(z, z, cos, sin)


def _ret_pre_bwd(dqr, dkr, cos, sin, *, name, tm=ROW_TILE):
    s = dqr[0].shape[0]
    tm = min(tm, s)
    scale = RET_HD ** -0.5

    def body(dq0_ref, dq1_ref, dk0_ref, dk1_ref, c_ref, s_ref, qo_ref, ko_ref):
        c, sg = c_ref[...], s_ref[...]
        for h in range(RET_HEADS):
            sl = slice(h * RET_HD, (h + 1) * RET_HD)
            qo_ref[:, sl] = _bf(_unrope128(dq0_ref[:, sl] + dq1_ref[:, sl], c, sg))
            ko_ref[:, sl] = _bf(_unrope128(dk0_ref[:, sl] + dk1_ref[:, sl], c, sg) * scale)

    row = pl.BlockSpec((tm, GROUP_W), lambda i: (i, 0))
    tab = pl.BlockSpec((tm, RET_HD), lambda i: (i, 0))
    return pl.pallas_call(
        body, name=name, grid=(s // tm,), in_specs=[row, row, row, row, tab, tab], out_specs=[row, row],
        out_shape=[jax.ShapeDtypeStruct((s, GROUP_W), BF16)] * 2,
        compiler_params=_cparams("parallel"),
    )(dqr[0], dqr[1], dkr[0], dkr[1], cos, sin)


def _bla(a, b, c, lg, cols, *, name):
    s = a.shape[0]
    ch = min(RET_CHUNK, s)
    n = s // ch
    hd = RET_HD

    def body(lg_ref, a0, b0, c0, a1, b1, c1, o0, o1, st):
        t = pl.program_id(0)

        @pl.when(t == 0)
        def _():
            st[...] = jnp.zeros_like(st)

        ii = lax.broadcasted_iota(jnp.int32, (ch, ch), 0)
        jj = lax.broadcasted_iota(jnp.int32, (ch, ch), 1)
        idx = lax.broadcasted_iota(jnp.int32, (ch, 1), 0).astype(F32)
        for d, (a_ref, b_ref, c_ref, o_ref) in enumerate(((a0, b0, c0, o0), (a1, b1, c1, o1))):
            diff = ((ii - jj) if d == 0 else (jj - ii)).astype(F32)
            keep = diff >= 0
            dpos = jnp.maximum(diff, 0.0)
            pq = (idx + 1.0) if d == 0 else (ch - idx)
            pk = (ch - 1.0 - idx) if d == 0 else idx
            for h in range(RET_HEADS):
                g = lg_ref[d, h]
                sl = slice(h * hd, (h + 1) * hd)
                av, bv, cv = a_ref[:, sl], b_ref[:, sl], c_ref[:, sl]
                sc = _dot(av, bv, 1, 1) * jnp.where(keep, jnp.exp(dpos * g), 0.0)
                stv = st[d, h]
                o_ref[:, sl] = _dot(sc, cv) + _dot(av * jnp.exp(pq * g), stv)
                st[d, h] = jnp.exp(ch * g) * stv + _dot(bv * jnp.exp(pk * g), cv, 0, 0)

    fwd = lambda j: pl.BlockSpec((ch, GROUP_W), lambda t: (t, j))
    bwd = lambda j: pl.BlockSpec((ch, GROUP_W), lambda t: (n - 1 - t, j))
    return pl.pallas_call(
        body, name=name, grid=(n,),
        in_specs=[pl.BlockSpec(memory_space=pltpu.SMEM), fwd(cols[0]), fwd(cols[1]), fwd(cols[2]),
                  bwd(cols[0]), bwd(cols[1]), bwd(cols[2])],
        out_specs=[fwd(0), bwd(0)],
        out_shape=[jax.ShapeDtypeStruct((s, GROUP_W), F32)] * 2,
        scratch_shapes=[pltpu.VMEM((2, RET_HEADS, hd, hd), F32)],
        compiler_params=_cparams("arbitrary"),
    )(lg, a, b, c, a, b, c)


def _post(os_, zg, gcol, g, *, norm, name, tm=ROW_TILE):
    s = zg.shape[0]
    tm = min(tm, s)
    nd = len(os_)

    def body(*refs):
        o_refs, (gt_ref, g_ref, y_ref) = refs[:nd], refs[nd:]
        silu, _ = _silu_parts(gt_ref[...])
        for h in range(4):
            sl = slice(h * 128, (h + 1) * 128)
            o = o_refs[0][:, sl]
            for k in range(1, nd):
                o = o + o_refs[k][:, sl]
            if norm:
                r = lax.rsqrt(jnp.mean(o * o, axis=-1, keepdims=True) + EPS)
                o = o * r * g_ref[:, sl]
            y_ref[:, sl] = _bf(silu[:, sl] * o)

    row = pl.BlockSpec((tm, GROUP_W), lambda i: (i, 0))
    return pl.pallas_call(
        body, name=name, grid=(s // tm,),
        in_specs=[row] * nd + [pl.BlockSpec((tm, GROUP_W), lambda i: (i, gcol)),
                               pl.BlockSpec((1, GROUP_W), lambda i: (0, 0))],
        out_specs=row,
        out_shape=jax.ShapeDtypeStruct((s, GROUP_W), BF16),
        compiler_params=_cparams("parallel"),
    )(*os_, zg, g)


def _post_bwd(dy, ycol, os_, zg, gcol, g, *, norm, name, tm=ROW_TILE):
    s = zg.shape[0]
    tm = min(tm, s)
    nd = len(os_)

    def body(*refs):
        dy_ref, o_refs = refs[0], refs[1:1 + nd]
        gt_ref, g_ref, dgt_ref, do_ref, dg_ref = refs[1 + nd:]
        i = pl.program_id(0)
        silu, dsilu = _silu_parts(gt_ref[...])
        dyv = dy_ref[...]
        parts = []
        for h in range(4):
            sl = slice(h * 128, (h + 1) * 128)
            o = o_refs[0][:, sl]
            for k in range(1, nd):
                o = o + o_refs[k][:, sl]
            dn = dyv[:, sl] * silu[:, sl]
            if norm:
                r = lax.rsqrt(jnp.mean(o * o, axis=-1, keepdims=True) + EPS)
                xn = o * r
                gh = g_ref[:, sl]
                dgt_ref[:, sl] = _bf(dyv[:, sl] * (xn * gh) * dsilu[:, sl])
                parts.append(jnp.sum(dn * xn, axis=0, keepdims=True))
                dxn = dn * gh
                do_ref[:, sl] = r * (dxn - xn * jnp.mean(dxn * xn, axis=-1, keepdims=True))
            else:
                dgt_ref[:, sl] = _bf(dyv[:, sl] * o * dsilu[:, sl])
                parts.append(jnp.zeros((1, 128), F32))
                do_ref[:, sl] = dn
        part = jnp.concatenate(parts, axis=1)

        @pl.when(i == 0)
        def _():
            dg_ref[...] = part

        @pl.when(i > 0)
        def _():
            dg_ref[...] += part

    row = pl.BlockSpec((tm, GROUP_W), lambda i: (i, 0))
    vec = pl.BlockSpec((1, GROUP_W), lambda i: (0, 0))
    return pl.pallas_call(
        body, name=name, grid=(s // tm,),
        in_specs=[pl.BlockSpec((tm, GROUP_W), lambda i: (i, ycol))] + [row] * nd
        + [pl.BlockSpec((tm, GROUP_W), lambda i: (i, gcol)), vec],
        out_specs=[row, row, vec],
        out_shape=[jax.ShapeDtypeStruct((s, GROUP_W), BF16), jax.ShapeDtypeStruct((s, GROUP_W), F32),
                   jax.ShapeDtypeStruct((1, GROUP_W), F32)],
        compiler_params=_cparams("arbitrary"),
    )(dy, *os_, zg, g)


def _ret_log_gamma(swap):
    gf = 1.0 - 2.0 ** (-5.0 - jnp.arange(RET_HEADS, dtype=F32))
    lf, lb = jnp.log(gf), jnp.log(gf[::-1])
    return jnp.stack([lb, lf] if swap else [lf, lb])


def _log_sigmoid(x):
    return jnp.minimum(x, 0.0) - jnp.log(1.0 + jnp.exp(-jnp.abs(x)))


def _gla_gate(z, wa, ba, *, name, tm=ROW_TILE):
    s = z.shape[0]
    tm = min(tm, s)
    col = SEG["ga"][0] // 128

    def body(ga_ref, wa_ref, ba_ref, la_ref):
        pre = _dot(ga_ref[...], wa_ref[...]) + ba_ref[...]
        la_ref[...] = _log_sigmoid(pre) / GLA_TAU

    return pl.pallas_call(
        body, name=name, grid=(s // tm,),
        in_specs=[pl.BlockSpec((tm, 128), lambda i: (i, col)), pl.BlockSpec((128, 512), lambda i: (0, 0)),
                  pl.BlockSpec((1, 512), lambda i: (0, 0))],
        out_specs=pl.BlockSpec((tm, 512), lambda i: (i, 0)),
        out_shape=jax.ShapeDtypeStruct((s, 512), F32),
        compiler_params=_cparams("parallel"),
    )(z, wa, ba)


def _gla_gate_bwd(dla, z, wa, ba, *, name, tm=ROW_TILE):
    s = z.shape[0]
    tm = min(tm, s)
    col = SEG["ga"][0] // 128

    def body(dla_ref, ga_ref, wa_ref, ba_ref, dga_ref, dwa_ref, dba_ref):
        i = pl.program_id(0)
        gav = ga_ref[...]
        pre = _dot(gav, wa_ref[...]) + ba_ref[...]
        dpre = dla_ref[...] * (1.0 - _sigmoid(pre)) * (1.0 / GLA_TAU)
        dga_ref[...] = _bf(_dot(dpre, wa_ref[...], 1, 1))
        pw = _dot(gav, dpre, 0, 0)
        pb = jnp.sum(dpre, axis=0, keepdims=True)

        @pl.when(i == 0)
        def _():
            dwa_ref[...] = pw
            dba_ref[...] = pb

        @pl.when(i > 0)
        def _():
            dwa_ref[...] += pw
            dba_ref[...] += pb

    return pl.pallas_call(
        body, name=name, grid=(s // tm,),
        in_specs=[pl.BlockSpec((tm, 512), lambda i: (i, 0)), pl.BlockSpec((tm, 128), lambda i: (i, col)),
                  pl.BlockSpec((128, 512), lambda i: (0, 0)), pl.BlockSpec((1, 512), lambda i: (0, 0))],
        out_specs=[pl.BlockSpec((tm, 128), lambda i: (i, 0)), pl.BlockSpec((128, 512), lambda i: (0, 0)),
                   pl.BlockSpec((1, 512), lambda i: (0, 0))],
        out_shape=[jax.ShapeDtypeStruct((s, 128), BF16), jax.ShapeDtypeStruct((128, 512), F32),
                   jax.ShapeDtypeStruct((1, 512), F32)],
        compiler_params=_cparams("arbitrary"),
    )(dla, z, wa, ba)


def _gla_masks(ch):
    ii = lax.broadcasted_iota(jnp.int32, (ch, ch), 0)
    tt = lax.broadcasted_iota(jnp.int32, (ch, ch), 1)
    return jnp.where(tt <= ii, 1.0, 0.0), jnp.where(tt >= ii, 1.0, 0.0)


def _running_sum(x, up):
    n = x.shape[0]
    rows = lax.broadcasted_iota(jnp.int32, x.shape, 0)
    k = 1
    while k < n:
        if up:
            x = x + jnp.where(rows < n - k, pltpu.roll(x, n - k, 0), 0.0)
        else:
            x = x + jnp.where(rows >= k, pltpu.roll(x, k, 0), 0.0)
        k *= 2
    return x


def _gla_chunk(d, tmat, qv, kv, lav, ch):
    c = _running_sum(lav, up=(d == 1))
    big_l = c[ch - 1:ch, :] if d == 0 else c[0:1, :]
    qt = qv * (GLA_DK ** -0.5) * jnp.exp(c)
    kt = kv * jnp.exp(-c)
    kh = kv * jnp.exp(big_l - c)
    return c, big_l, qt, kt, kh


def _gla_fwd(qh, kh_, z, la, *, name, rider=None):
    s = z.shape[0]
    ch = min(GLA_CHUNK, s)
    n = s // ch
    vcol = SEG["gv"][0] // GROUP_W

    def body(q0, k0, v0, la0, q1, k1, v1, la1, o0, o1, zs0, zs1, st):
        t = pl.program_id(0)

        @pl.when(t == 0)
        def _():
            st[...] = jnp.zeros_like(st)

        masks = _gla_masks(ch)
        for d, (q_ref, k_ref, v_ref, la_ref, o_ref, zs_ref) in enumerate(
                ((q0, k0, v0, la0, o0, zs0), (q1, k1, v1, la1, o1, zs1))):
            for h in range(GLA_HEADS):
                c, big_l, qt, kt, kh = _gla_chunk(d, masks[d], q_ref[h], k_ref[h], la_ref[0, h], ch)
                vv = v_ref[:, h * GLA_DV:(h + 1) * GLA_DV]
                p = _dot(qt, kt, 1, 1) * masks[d]
                zst = st[d, h]
                o_ref[:, h * GLA_DV:(h + 1) * GLA_DV] = _dot(p, vv) + _dot(qt, zst, 1, 1)
                zs_ref[h, 0] = zst
                st[d, h] = zst * jnp.exp(big_l) + _dot(vv, kh, 0, 0)

    cidx = (lambda t: t), (lambda t: n - 1 - t)
    hs = lambda d: pl.BlockSpec((GLA_HEADS, ch, GLA_DK), lambda t: (0, cidx[d](t), 0))
    vs = lambda d: pl.BlockSpec((ch, GROUP_W), lambda t: (cidx[d](t), vcol))
    las = lambda d: pl.BlockSpec((1, GLA_HEADS, ch, GLA_DK), lambda t: (d, 0, cidx[d](t), 0))
    os_ = lambda d: pl.BlockSpec((ch, GROUP_W), lambda t: (cidx[d](t), 0))
    zss = lambda d: pl.BlockSpec((GLA_HEADS, 1, GLA_DV, GLA_DK), lambda t: (0, cidx[d](t), 0, 0))
    (o0, o1, zs0, zs1), rode = _ride_call(
        body, rider, name=name, grid=(n,),
        in_specs=[hs(0), hs(0), vs(0), las(0), hs(1), hs(1), vs(1), las(1)],
        out_specs=[os_(0), os_(1), zss(0), zss(1)],
        out_shape=[jax.ShapeDtypeStruct((s, GROUP_W), F32)] * 2
        + [jax.ShapeDtypeStruct((GLA_HEADS, n, GLA_DV, GLA_DK), F32)] * 2,
        scratch_shapes=[pltpu.VMEM((2, GLA_HEADS, GLA_DV, GLA_DK), F32)],
        args=(qh, kh_, z, la, qh, kh_, z, la), sem=("arbitrary",))
    return ((o0, o1), (zs0, zs1)) if rider is None else ((o0, o1), (zs0, zs1), rode)


def _gla_bwd(qh, kh_, z, la, do, zs, *, name):
    s = z.shape[0]
    ch = min(GLA_CHUNK, s)
    n = s // ch
    vcol = SEG["gv"][0] // GROUP_W

    def body(q0, k0, v0, la0, do0, zs0, q1, k1, v1, la1, do1, zs1,
             dq0, dk0, dla0, dv0, dq1, dk1, dla1, dv1, gz):
        t = pl.program_id(0)

        @pl.when(t == 0)
        def _():
            gz[...] = jnp.zeros_like(gz)

        masks = _gla_masks(ch)
        rows = lax.broadcasted_iota(jnp.int32, (ch, 1), 0)
        for d, (q_ref, k_ref, v_ref, la_ref, do_ref, zs_ref, dq_ref, dk_ref, dla_ref, dv_ref) in enumerate(
                ((q0, k0, v0, la0, do0, zs0, dq0, dk0, dla0, dv0), (q1, k1, v1, la1, do1, zs1, dq1, dk1, dla1, dv1))):
            tmat = masks[d]
            end = ch - 1 if d == 0 else 0
            for h in range(GLA_HEADS):
                c, big_l, qt, kt, kh = _gla_chunk(d, tmat, q_ref[h], k_ref[h], la_ref[0, h], ch)
                vsl = slice(h * GLA_DV, (h + 1) * GLA_DV)
                vv, dov, zst, gzv = v_ref[:, vsl], do_ref[:, vsl], zs_ref[h, 0], gz[d, h]
                p = _dot(qt, kt, 1, 1) * tmat
                dp = _dot(dov, vv, 1, 1) * tmat
                dqt = _dot(dp, kt) + _dot(dov, zst)
                dkt = _dot(dp, qt, 0, 0)
                dkh = _dot(vv, gzv)
                dv_ref[:, vsl] = _dot(p, dov, 0, 0) + _dot(kh, gzv, 1, 1)
                dq_ref[h] = dqt * jnp.exp(c) * (GLA_DK ** -0.5)
                dk_ref[h] = dkt * jnp.exp(-c) + dkh * jnp.exp(big_l - c)
                e_l = jnp.exp(big_l)
                d_l = jnp.sum(dkh * kh, axis=0, keepdims=True) + e_l * jnp.sum(zst * gzv, axis=0, keepdims=True)
                dc = dqt * qt - dkt * kt - dkh * kh + jnp.where(rows == end, d_l, 0.0)
                dla_ref[h] = _running_sum(dc, up=(d == 0))
                gz[d, h] = gzv * e_l + _dot(dov, qt, 0, 0)

    cidx = (lambda t: n - 1 - t), (lambda t: t)
    hs = lambda d: pl.BlockSpec((GLA_HEADS, ch, GLA_DK), lambda t: (0, cidx[d](t), 0))
    vs = lambda d: pl.BlockSpec((ch, GROUP_W), lambda t: (cidx[d](t), vcol))
    las = lambda d: pl.BlockSpec((1, GLA_HEADS, ch, GLA_DK), lambda t: (d, 0, cidx[d](t), 0))
    row = lambda d: pl.BlockSpec((ch, GROUP_W), lambda t: (cidx[d](t), 0))
    zss = lambda d: pl.BlockSpec((GLA_HEADS, 1, GLA_DV, GLA_DK), lambda t: (0, cidx[d](t), 0, 0))
    hshape = jax.ShapeDtypeStruct((GLA_HEADS, s, GLA_DK), F32)
    wide = jax.ShapeDtypeStruct((s, GROUP_W), F32)
    outs = pl.pallas_call(
        body, name=name, grid=(n,),
        in_specs=[hs(0), hs(0), vs(0), las(0), row(0), zss(0), hs(1), hs(1), vs(1), las(1), row(1), zss(1)],
        out_specs=[hs(0), hs(0), hs(0), row(0), hs(1), hs(1), hs(1), row(1)],
        out_shape=[hshape, hshape, hshape, wide, hshape, hshape, hshape, wide],
        scratch_shapes=[pltpu.VMEM((2, GLA_HEADS, GLA_DV, GLA_DK), F32)],
        compiler_params=_cparams("arbitrary"),
    )(qh, kh_, z, la, do, zs[0], qh, kh_, z, la, do, zs[1])
    dq0, dk0, dla0, dv0, dq1, dk1, dla1, dv1 = outs
    return (dq0, dq1), (dk0, dk1), (dla0, dla1), (dv0, dv1)


def _window_sums(win, g, shift):
    n = win.shape[0]
    levels, y = [], win
    for j in range(POOL_GROUPS):
        y = y + pltpu.roll(y, n - (1 << j), 0)
        levels.append(y)
    sums = levels[-1]
    for j in range(POOL_GROUPS - 2, -1, -1):
        sums = jnp.where(g == j, levels[j], sums)
    return pltpu.roll(sums, shift, 0)


def _pool_cnt(t0, half, rows, s):
    t = t0 + lax.broadcasted_iota(jnp.int32, (rows, 1), 0)
    return (jnp.minimum(t + half, s) - jnp.maximum(t - half, 0)).astype(F32)


def _pool_fwd(z, pw, scale, *, name):
    s = z.shape[0]
    tl = min(POOL_TILE, s)
    nt = s // tl
    ucol, gcol = SEG["pv"][0] // 128, SEG["pg"][0] // 128

    def body(u_ref, gt_ref, pw_ref, sc_ref, y_ref, pad):
        g = pl.program_id(0)
        half = jnp.left_shift(1, g)
        pad[0:POOL_HALO, :] = jnp.zeros((POOL_HALO, POOL_GW), F32)
        pad[POOL_HALO + s:POOL_HALO + s + POOL_HALO, :] = jnp.zeros((POOL_HALO, POOL_GW), F32)
        pad[POOL_HALO:POOL_HALO + s, :] = u_ref[...]
        pwv, scv = pw_ref[0], sc_ref[...]

        def tile(i, carry):
            t0 = pl.multiple_of(i * tl, tl)
            win = pad[pl.ds(t0, tl + 2 * POOL_HALO), :]
            u = win[POOL_HALO:POOL_HALO + tl, :]
            pooled = _window_sums(win, g, half)[POOL_HALO:POOL_HALO + tl, :] / _pool_cnt(t0, half, tl, s) - u
            mixed = _dot(pooled, pwv)
            silu, _ = _silu_parts(gt_ref[pl.ds(t0, tl), :])
            y_ref[pl.ds(t0, tl), :] = _bf(silu * (mixed * scv))
            return carry

        lax.fori_loop(0, nt, tile, 0)

    return pl.pallas_call(
        body, name=name, grid=(POOL_GROUPS,),
        in_specs=[pl.BlockSpec((s, POOL_GW), lambda g: (0, ucol + g)),
                  pl.BlockSpec((s, POOL_GW), lambda g: (0, gcol + g)),
                  pl.BlockSpec((1, POOL_GW, POOL_GW), lambda g: (g, 0, 0)),
                  pl.BlockSpec((1, POOL_GW), lambda g: (0, g))],
        out_specs=pl.BlockSpec((s, POOL_GW), lambda g: (0, g)),
        out_shape=jax.ShapeDtypeStruct((s, GROUP_W), BF16),
        scratch_shapes=[pltpu.VMEM((s + 2 * POOL_HALO, POOL_GW), F32)],
        compiler_params=_cparams("parallel"),
    )(z, z, pw, scale)


def _pool_bwd(dy, z, pw, scale, *, name):
    s = z.shape[0]
    tl = min(POOL_TILE, s)
    nt = s // tl
    ucol, gcol, ycol = SEG["pv"][0] // 128, SEG["pg"][0] // 128, 2 * GROUP_W // 128

    def body(dy_ref, u_ref, gt_ref, pw_ref, sc_ref, du_ref, dgt_ref, dpw_ref, dsc_ref, pad, epad, dpo):
        g = pl.program_id(0)
        half = jnp.left_shift(1, g)
        zeros = jnp.zeros((POOL_HALO, POOL_GW), F32)
        for buf in (pad, epad):
            buf[0:POOL_HALO, :] = zeros
            buf[POOL_HALO + s:POOL_HALO + s + POOL_HALO, :] = zeros
        pad[POOL_HALO:POOL_HALO + s, :] = u_ref[...]
        pwv, scv = pw_ref[0], sc_ref[...]
        dpw_ref[0] = jnp.zeros((POOL_GW, POOL_GW), F32)
        dsc_ref[...] = jnp.zeros((1, POOL_GW), F32)

        def tile(i, carry):
            t0 = pl.multiple_of(i * tl, tl)
            win = pad[pl.ds(t0, tl + 2 * POOL_HALO), :]
            u = win[POOL_HALO:POOL_HALO + tl, :]
            cnt = _pool_cnt(t0, half, tl, s)
            pooled = _window_sums(win, g, half)[POOL_HALO:POOL_HALO + tl, :] / cnt - u
            mixed = _dot(pooled, pwv)
            silu, dsilu = _silu_parts(gt_ref[pl.ds(t0, tl), :])
            dyv = dy_ref[pl.ds(t0, tl), :]
            dgt_ref[pl.ds(t0, tl), :] = _bf(dyv * (mixed * scv) * dsilu)
            dsc_ref[...] += jnp.sum(dyv * silu * mixed, axis=0, keepdims=True)
            dm = dyv * silu * scv
            dpw_ref[0] += _dot(pooled, dm, 0, 0)
            dpooled = _dot(dm, pwv, 1, 1)
            dpo[pl.ds(t0, tl), :] = dpooled
            epad[pl.ds(POOL_HALO + t0, tl), :] = dpooled / cnt
            return carry

        lax.fori_loop(0, nt, tile, 0)

        def tile2(i, carry):
            t0 = pl.multiple_of(i * tl, tl)
            ewin = epad[pl.ds(t0, tl + 2 * POOL_HALO), :]
            du_ref[pl.ds(t0, tl), :] = _bf(_window_sums(ewin, g, half - 1)[POOL_HALO:POOL_HALO + tl, :]
                                           - dpo[pl.ds(t0, tl), :])
            return carry

        lax.fori_loop(0, nt, tile2, 0)

    col = lambda c0: pl.BlockSpec((s, POOL_GW), lambda g: (0, c0 + g))
    return pl.pallas_call(
        body, name=name, grid=(POOL_GROUPS,),
        in_specs=[col(ycol), col(ucol), col(gcol), pl.BlockSpec((1, POOL_GW, POOL_GW), lambda g: (g, 0, 0)),
                  pl.BlockSpec((1, POOL_GW), lambda g: (0, g))],
        out_specs=[col(0), col(0), pl.BlockSpec((1, POOL_GW, POOL_GW), lambda g: (g, 0, 0)),
                   pl.BlockSpec((1, POOL_GW), lambda g: (0, g))],
        out_shape=[jax.ShapeDtypeStruct((s, GROUP_W), BF16), jax.ShapeDtypeStruct((s, GROUP_W), BF16),
                   jax.ShapeDtypeStruct((POOL_GROUPS, POOL_GW, POOL_GW), F32),
                   jax.ShapeDtypeStruct((1, GROUP_W), F32)],
        scratch_shapes=[pltpu.VMEM((s + 2 * POOL_HALO, POOL_GW), F32), pltpu.VMEM((s + 2 * POOL_HALO, POOL_GW), F32),
                        pltpu.VMEM((s, POOL_GW), F32)],
        compiler_params=_cparams("parallel"),
    )(dy, z, z, pw, scale)


def _mla_specs(tm):
    zq = pl.BlockSpec((tm, 512), lambda i: (i, SEG["mq"][0] // 512))
    zkv = pl.BlockSpec((tm, 256), lambda i: (i, SEG["mkv"][0] // 256))
    zkr = pl.BlockSpec((tm, 128), lambda i: (i, SEG["mkr"][0] // 128))
    full = lambda r, c: pl.BlockSpec((r, c), lambda i: (0, 0))
    tab = pl.BlockSpec((tm, 128), lambda i: (i, 0))
    weights = [full(1, 512), full(512, 1024), full(1, 256), full(256, 1024), full(1, 256), full(1, 256)]
    return [zq, zkv, zkr] + weights + [tab, tab, tab]


def _mla_project(xq_ref, xkv_ref, qg_ref, wq_ref, kvg_ref, wkv_ref):
    xq = xq_ref[...]
    r1 = lax.rsqrt(jnp.mean(xq * xq, axis=-1, keepdims=True) + EPS)
    xn1 = xq * r1
    qn = _bf(xn1 * qg_ref[...])
    qraw = _dot(qn, wq_ref[...])
    xkv = xkv_ref[...]
    r2 = lax.rsqrt(jnp.mean(xkv * xkv, axis=-1, keepdims=True) + EPS)
    xn2 = xkv * r2
    kvn = _bf(xn2 * kvg_ref[...])
    kvraw = _dot(kvn, wkv_ref[...])
    return r1, xn1, qn, qraw, r2, xn2, kvn, kvraw


def _mla_pre(z, qg, wq, kvg, wkv, qng, kng, cos, sp, sn, *, name, tm=ROW_TILE):
    s = z.shape[0]
    tm = min(tm, s)

    def body(xq_ref, xkv_ref, pe_ref, qg_ref, wq_ref, kvg_ref, wkv_ref, qng_ref, kng_ref, c_ref, sp_ref, sn_ref,
             q_ref, k_ref, v_ref):
        _, _, _, qraw, _, _, _, kvraw = _mla_project(xq_ref, xkv_ref, qg_ref, wq_ref, kvg_ref, wkv_ref)
        c, spv, snv = c_ref[...], sp_ref[...], sn_ref[...]
        pe = pe_ref[...]
        pe_ss = jnp.sum(pe * pe, axis=-1, keepdims=True)
        qngv, kngv = qng_ref[...], kng_ref[...]
        for h in range(MLA_HEADS):
            b = h * MLA_QKP
            qh = qraw[:, b:b + MLA_QKP]
            r = lax.rsqrt(jnp.sum(qh * qh, axis=-1, keepdims=True) * (1.0 / MLA_QK) + EPS)
            qn_h = qh * r * qngv
            q_ref[:, b:b + 128] = _bf(qn_h[:, :128] * MLA_SCALE)
            q_ref[:, b + 128:b + 256] = _bf(_rope64(qn_h[:, 128:], c, spv, snv) * MLA_SCALE)
            kn = kvraw[:, b:b + 128]
            rk = lax.rsqrt((jnp.sum(kn * kn, axis=-1, keepdims=True) + pe_ss) * (1.0 / MLA_QK) + EPS)
            k_ref[:, b:b + 128] = _bf(kn * rk * kngv[:, :128])
            k_ref[:, b + 128:b + 256] = _bf(_rope64(pe * rk * kngv[:, 128:], c, spv, snv))
            v_ref[:, h * MLA_V:(h + 1) * MLA_V] = _bf(kvraw[:, b + 128:b + 256])

    row = lambda w: pl.BlockSpec((tm, w), lambda i: (i, 0))
    return pl.pallas_call(
        body, name=name, grid=(s // tm,), in_specs=_mla_specs(tm),
        out_specs=[row(1024), row(1024), row(512)],
        out_shape=[jax.ShapeDtypeStruct((s, 1024), BF16), jax.ShapeDtypeStruct((s, 1024), BF16),
                   jax.ShapeDtypeStruct((s, 512), BF16)],
        compiler_params=_cparams("parallel"),
    )(z, z, z, qg, wq, kvg, wkv, qng, kng, cos, sp, sn)


def _mla_pre_bwd(dq, dk, dv, z, qg, wq, kvg, wkv, qng, kng, cos, sp, sn, *, name, tm=ROW_TILE):
    s = z.shape[0]
    tm = min(tm, s)

    def body(dq_ref, dk_ref, dv_ref, xq_ref, xkv_ref, pe_ref, qg_ref, wq_ref, kvg_ref, wkv_ref, qng_ref, kng_ref,
             c_ref, sp_ref, sn_ref, dxq_ref, dxkv_ref, dpe_ref, dwq_ref, dwkv_ref, dqg_ref, dkvg_ref, dqng_ref,
             dkng_ref, dqraw, dkvraw):
        i = pl.program_id(0)
        r1, xn1, qn, qraw, r2, xn2, kvn, kvraw = _mla_project(xq_ref, xkv_ref, qg_ref, wq_ref, kvg_ref, wkv_ref)
        c, spv, snv = c_ref[...], sp_ref[...], sn_ref[...]
        pe = pe_ref[...]
        pe_ss = jnp.sum(pe * pe, axis=-1, keepdims=True)
        qngv, kngv = qng_ref[...], kng_ref[...]
        dqng = jnp.zeros((1, MLA_QKP), F32)
        dkng = jnp.zeros((1, MLA_QKP), F32)
        dpe = jnp.zeros_like(pe)
        for h in range(MLA_HEADS):
            b = h * MLA_QKP
            qh = qraw[:, b:b + MLA_QKP]
            r = lax.rsqrt(jnp.sum(qh * qh, axis=-1, keepdims=True) * (1.0 / MLA_QK) + EPS)
            xn = qh * r
            d_n = jnp.concatenate(
                [dq_ref[:, b:b + 128], _unrope64(dq_ref[:, b + 128:b + 256], c, spv, snv)], axis=1) * MLA_SCALE
            dqng = dqng + jnp.sum(d_n * xn, axis=0, keepdims=True)
            dxn = d_n * qngv
            dqraw[:, b:b + MLA_QKP] = _bf(r * (dxn - xn * (jnp.sum(dxn * xn, axis=-1, keepdims=True) * (1.0 / MLA_QK))))
            kn = kvraw[:, b:b + 128]
            rk = lax.rsqrt((jnp.sum(kn * kn, axis=-1, keepdims=True) + pe_ss) * (1.0 / MLA_QK) + EPS)
            xk = jnp.concatenate([kn, pe], axis=1) * rk
            d_k = jnp.concatenate(
                [dk_ref[:, b:b + 128], _unrope64(dk_ref[:, b + 128:b + 256], c, spv, snv)], axis=1)
            dkng = dkng + jnp.sum(d_k * xk, axis=0, keepdims=True)
            dxk = d_k * kngv
            dfull = rk * (dxk - xk * (jnp.sum(dxk * xk, axis=-1, keepdims=True) * (1.0 / MLA_QK)))
            dkvraw[:, b:b + 128] = _bf(dfull[:, :128])
            dkvraw[:, b + 128:b + 256] = _bf(dv_ref[:, h * MLA_V:(h + 1) * MLA_V])
            dpe = dpe + dfull[:, 128:]
        dpe_ref[...] = _bf(dpe)
        dqr, dkvr = dqraw[...], dkvraw[...]
        dqn = _dot(dqr, wq_ref[...], 1, 1)
        dxn1 = dqn * qg_ref[...]
        dxq_ref[...] = _bf(r1 * (dxn1 - xn1 * jnp.mean(dxn1 * xn1, axis=-1, keepdims=True)))
        dkvn = _dot(dkvr, wkv_ref[...], 1, 1)
        dxn2 = dkvn * kvg_ref[...]
        dxkv_ref[...] = _bf(r2 * (dxn2 - xn2 * jnp.mean(dxn2 * xn2, axis=-1, keepdims=True)))
        parts = (_dot(qn, dqr, 0, 0), _dot(kvn, dkvr, 0, 0), jnp.sum(dqn * xn1, axis=0, keepdims=True),
                 jnp.sum(dkvn * xn2, axis=0, keepdims=True), dqng, dkng)
        accs = (dwq_ref, dwkv_ref, dqg_ref, dkvg_ref, dqng_ref, dkng_ref)

        @pl.when(i == 0)
        def _():
            for a, p in zip(accs, parts):
                a[...] = p

        @pl.when(i > 0)
        def _():
            for a, p in zip(accs, parts):
                a[...] += p

    row = lambda w: pl.BlockSpec((tm, w), lambda i: (i, 0))
    full = lambda r, c: pl.BlockSpec((r, c), lambda i: (0, 0))
    return pl.pallas_call(
        body, name=name, grid=(s // tm,),
        in_specs=[row(1024), row(1024), row(512)] + _mla_specs(tm),
        out_specs=[row(512), row(256), row(128), full(512, 1024), full(256, 1024), full(1, 512), full(1, 256),
                   full(1, 256), full(1, 256)],
        out_shape=[jax.ShapeDtypeStruct((s, 512), BF16), jax.ShapeDtypeStruct((s, 256), BF16),
                   jax.ShapeDtypeStruct((s, 128), BF16), jax.ShapeDtypeStruct((512, 1024), F32),
                   jax.ShapeDtypeStruct((256, 1024), F32), jax.ShapeDtypeStruct((1, 512), F32),
                   jax.ShapeDtypeStruct((1, 256), F32), jax.ShapeDtypeStruct((1, 256), F32),
                   jax.ShapeDtypeStruct((1, 256), F32)],
        scratch_shapes=[pltpu.VMEM((tm, 1024), BF16), pltpu.VMEM((tm, 1024), BF16)],
        compiler_params=_cparams("arbitrary"),
    )(dq, dk, dv, z, z, z, qg, wq, kvg, wkv, qng, kng, cos, sp, sn)


def _flash_fwd(q, k, v, *, name, tq=1024, tk=1024, rider=None):
    s = q.shape[0]
    tq, tk = min(tq, s), min(tk, s)
    nk = s // tk
    strip = min(FLASH_STRIP, tq)

    def body(q_ref, k_ref, v_ref, o_ref, lse_ref, m_s, l_s, acc):
        j = pl.program_id(2)

        @pl.when(j == 0)
        def _():
            m_s[...] = jnp.full_like(m_s, -jnp.inf)
            l_s[...] = jnp.zeros_like(l_s)
            acc[...] = jnp.zeros_like(acc)

        for r in range(tq // strip):
            rows = slice(r * strip, (r + 1) * strip)
            sc = _dot(q_ref[rows, :], k_ref[...], 1, 1)
            m_prev = m_s[rows, :]
            m_new = jnp.maximum(m_prev, jnp.max(sc, axis=-1, keepdims=True))
            p = jnp.exp(sc - m_new[:, 0:1])
            alpha = jnp.exp(m_prev - m_new)
            l_s[rows, :] = alpha * l_s[rows, :] + jnp.sum(p, axis=-1, keepdims=True)
            acc[rows, :] = alpha * acc[rows, :] + _dot(p, v_ref[...])
            m_s[rows, :] = m_new

        @pl.when(j == nk - 1)
        def _():
            o_ref[...] = acc[...] / l_s[...]
            lse_ref[...] = m_s[...] + jnp.log(l_s[...])

    (o, lse), rode = _ride_call(
        body, rider, name=name, grid=(MLA_HEADS, s // tq, nk),
        in_specs=[pl.BlockSpec((tq, MLA_QKP), lambda h, i, j: (i, h)),
                  pl.BlockSpec((tk, MLA_QKP), lambda h, i, j: (j, h)),
                  pl.BlockSpec((tk, MLA_V), lambda h, i, j: (j, h))],
        out_specs=[pl.BlockSpec((tq, MLA_V), lambda h, i, j: (i, h))] * 2,
        out_shape=[jax.ShapeDtypeStruct((s, GROUP_W), F32)] * 2,
        scratch_shapes=[pltpu.VMEM((tq, MLA_V), F32), pltpu.VMEM((tq, MLA_V), F32), pltpu.VMEM((tq, MLA_V), F32)],
        args=(q, k, v), sem=("parallel", "parallel", "arbitrary"))
    return (o, lse) if rider is None else (o, lse, rode)


def _flash_bwd(q, k, v, do, o, lse, *, name, tq=1024, tk=1024, rider=None):
    s = q.shape[0]
    tq, tk = min(tq, s), min(tk, s)
    nq, nk = s // tq, s // tk

    def body(q_ref, k_ref, v_ref, do_ref, o_ref, lse_ref, dq_ref, dk_ref, dv_ref, dk_acc, dv_acc):
        j, i = pl.program_id(1), pl.program_id(2)
        dov = do_ref[...]
        delta = jnp.sum(dov * o_ref[...], axis=-1, keepdims=True)
        p = jnp.exp(_dot(q_ref[...], k_ref[...], 1, 1) - lse_ref[:, 0:1])
        ds = p * (_dot(dov, v_ref[...], 1, 1) - delta)
        pv = _dot(p, dov, 0, 0)
        pk = _dot(ds, q_ref[...], 0, 0)
        pq = _dot(ds, k_ref[...])
        rows = pl.ds(pl.multiple_of(i * tq, tq), tq)

        @pl.when(j == 0)
        def _():
            dq_ref[rows, :] = pq

        @pl.when(j > 0)
        def _():
            dq_ref[rows, :] += pq

        @pl.when(i == 0)
        def _():
            dv_acc[...] = pv
            dk_acc[...] = pk

        @pl.when(i > 0)
        def _():
            dv_acc[...] += pv
            dk_acc[...] += pk

        @pl.when(i == nq - 1)
        def _():
            dk_ref[...] = dk_acc[...]
            dv_ref[...] = dv_acc[...]

    qb = pl.BlockSpec((tq, MLA_QKP), lambda h, j, i: (i, h))
    kb = pl.BlockSpec((tk, MLA_QKP), lambda h, j, i: (j, h))
    vb = pl.BlockSpec((tk, MLA_V), lambda h, j, i: (j, h))
    ob = pl.BlockSpec((tq, MLA_V), lambda h, j, i: (i, h))
    (dq, dk, dv), rode = _ride_call(
        body, rider, name=name, grid=(MLA_HEADS, nk, nq),
        in_specs=[qb, kb, vb, ob, ob, ob],
        out_specs=[pl.BlockSpec((s, MLA_QKP), lambda h, j, i: (0, h)), kb, vb],
        out_shape=[jax.ShapeDtypeStruct((s, MLA_HEADS * MLA_QKP), F32),
                   jax.ShapeDtypeStruct((s, MLA_HEADS * MLA_QKP), F32), jax.ShapeDtypeStruct((s, GROUP_W), F32)],
        scratch_shapes=[pltpu.VMEM((tk, MLA_QKP), F32), pltpu.VMEM((tk, MLA_V), F32)],
        args=(q, k, v, do, o, lse), sem=("arbitrary", "arbitrary", "arbitrary"))
    return (dq, dk, dv) if rider is None else (dq, dk, dv, rode)


def _rows_tile(r, c, itemsize=4, budget=2 * 1024 * 1024):
    if r * c * itemsize <= budget:
        return r
    best = None
    for t in range(8, r, 8):
        if r % t == 0 and t * c * itemsize <= budget:
            best = t
    return best if best is not None else r


def _add_n(arrs, *, out_dtype=F32, name):
    shape = arrs[0].shape
    c = shape[-1]
    flat = [a.reshape(-1, c) for a in arrs]
    r = flat[0].shape[0]
    t = _rows_tile(r, c)

    def body(*refs):
        acc = refs[0][...].astype(F32)
        for ref in refs[1:-1]:
            acc = acc + ref[...].astype(F32)
        refs[-1][...] = acc.astype(out_dtype)

    blk = pl.BlockSpec((t, c), lambda i: (i, 0))
    out = pl.pallas_call(
        body, name=name, grid=(r // t,), in_specs=[blk] * len(flat), out_specs=blk,
        out_shape=jax.ShapeDtypeStruct((r, c), out_dtype), compiler_params=_cparams("parallel"),
    )(*flat)
    return out.reshape(shape)


def _adamw(w, g, m, v, *, name):
    shape = w.shape
    c = shape[-1]
    flat = [a.reshape(-1, c) for a in (w, g, m, v)]
    r = flat[0].shape[0]
    t = _rows_tile(r, c, budget=1024 * 1024)

    def body(w_ref, g_ref, m_ref, v_ref, d_ref, mo_ref, vo_ref):
        gv = g_ref[...]
        m2 = ADAM_B1 * m_ref[...] + (1.0 - ADAM_B1) * gv
        v2 = ADAM_B2 * v_ref[...] + (1.0 - ADAM_B2) * (gv * gv)
        m_hat = m2 / (1.0 - ADAM_B1 ** ADAM_STEP)
        v_hat = v2 / (1.0 - ADAM_B2 ** ADAM_STEP)
        d_ref[...] = -ADAM_LR * (m_hat / (jnp.sqrt(v_hat) + ADAM_EPS) + ADAM_WD * w_ref[...])
        mo_ref[...] = m2
        vo_ref[...] = v2

    blk = pl.BlockSpec((t, c), lambda i: (i, 0))
    outs = pl.pallas_call(
        body, name=name, grid=(r // t,), in_specs=[blk] * 4, out_specs=[blk] * 3,
        out_shape=[jax.ShapeDtypeStruct((r, c), F32)] * 3, compiler_params=_cparams("parallel"),
    )(*flat)
    return tuple(o.reshape(shape) for o in outs)


def _place():
    x, y, c = lax.axis_index("x"), lax.axis_index("y"), lax.axis_index("c")
    chips = [(1 - x, y), (x, 1 - y), (1 - x, 1 - y)]
    return x, y, c, chips


ANY = pl.BlockSpec(memory_space=pl.ANY)


def _half(ref, axis, hc, lead=()):
    n = ref.shape[len(lead) + axis] // 2
    return ref.at[tuple(lead) + (slice(None),) * axis + (pl.ds(hc * n, n),)]


def _gather_shards(shards, axes, *, name):
    nt = len(shards)

    def body(*refs):
        src, dst = refs[:nt], refs[nt:2 * nt]
        send, recv, fsend, frecv, lsem = refs[2 * nt:]
        x, y, c, chips = _place()
        me = 2 * x + y
        local = [pltpu.make_async_copy(src[t], dst[t].at[me], lsem.at[t]) for t in range(nt)]
        for cp in local:
            cp.start()

        def half(t, slot, hc):
            return _half(dst[t], axes[t], hc, lead=(slot,))

        def first(t, k):
            return pltpu.make_async_remote_copy(
                src_ref=_half(src[t], axes[t], c), dst_ref=half(t, me, c),
                send_sem=send.at[t, k], recv_sem=recv.at[t, k],
                device_id=(chips[k][0], chips[k][1], c), device_id_type=MESH)

        def landed(t, k):
            slot = 2 * chips[k][0] + chips[k][1]
            return pltpu.make_async_remote_copy(
                src_ref=half(t, slot, c), dst_ref=half(t, slot, c),
                send_sem=send.at[t, k], recv_sem=recv.at[t, k],
                device_id=(chips[k][0], chips[k][1], c), device_id_type=MESH)

        def forward(t, k, hc):
            slot = 2 * chips[k][0] + chips[k][1]
            return pltpu.make_async_remote_copy(
                src_ref=half(t, slot, hc), dst_ref=half(t, slot, hc),
                send_sem=fsend.at[t, k], recv_sem=frecv.at[t, k],
                device_id=(x, y, 1 - c), device_id_type=MESH)

        for t in range(nt):
            for k in range(3):
                first(t, k).start()
        for t in range(nt):
            for k in range(3):
                landed(t, k).wait_recv()
                forward(t, k, c).start()
        for t in range(nt):
            for k in range(3):
                forward(t, k, 1 - c).wait_recv()
        for t in range(nt):
            for k in range(3):
                first(t, k).wait_send()
                forward(t, k, c).wait_send()
        for cp in local:
            cp.wait()

    return pl.pallas_call(
        body, name=name, in_specs=[ANY] * nt, out_specs=[ANY] * nt,
        out_shape=[jax.ShapeDtypeStruct((N_CHIP,) + a.shape, a.dtype) for a in shards],
        scratch_shapes=[pltpu.SemaphoreType.DMA((nt, 3)), pltpu.SemaphoreType.DMA((nt, 3)),
                        pltpu.SemaphoreType.DMA((nt, 3)), pltpu.SemaphoreType.DMA((nt, 3)),
                        pltpu.SemaphoreType.DMA((nt,))],
    )(*shards)


def _comm_rows(hr, c, budget=2 * 1024 * 1024):
    if hr * c * 4 <= budget:
        return hr
    best = None
    for t in range(16, hr, 16):
        if hr % t == 0 and t * c * 4 <= budget:
            best = t
    return best if best is not None else hr


def _comm_cols(r, hc, budget=2 * 1024 * 1024):
    best = 128
    for t in range(128, hc + 1, 128):
        if hc % t == 0 and r * t * 4 <= budget:
            best = t
    return best


def _comm_chunks(shape, axis):
    r, cdim = shape
    if axis == 0:
        rc = _comm_rows(r // 2, cdim)
        nt = (r // 2) // rc
        return (rc, cdim), nt, (lambda h, t: (h * nt + t, 0))
    cc = _comm_cols(r, cdim // 2)
    nt = (cdim // 2) // cc
    return (r, cc), nt, (lambda h, t: (0, h * nt + t))


def _pair_reduce(g, where, axis, *, out_dtype, name):
    n_slot, r, cdim = g.shape
    blk_shape, nr, at = _comm_chunks((r, cdim), axis)
    steps = n_slot * nr
    half_shape = (r // 2, cdim) if axis == 0 else (r, cdim // 2)

    def body(w_ref, a_ref, b_ref, o_ref, land, send, recv, credit):
        x, y, c, _ = _place()
        sib = (x, y, 1 - c)
        i = pl.program_id(0) * nr + pl.program_id(1)
        s = lax.rem(i, 2)

        @pl.when(i >= 2)
        def _():
            pl.semaphore_wait(credit.at[s], 1)

        cp = pltpu.make_async_remote_copy(src_ref=b_ref.at[0], dst_ref=land.at[s], send_sem=send.at[s],
                                          recv_sem=recv.at[s], device_id=sib, device_id_type=MESH)
        cp.start()
        cp.wait_recv()
        o_ref[0] = (a_ref[0] + land[s]).astype(out_dtype)
        cp.wait_send()

        @pl.when(i + 2 < steps)
        def _():
            pl.semaphore_signal(credit.at[s], inc=1, device_id=sib, device_id_type=MESH)

    blk = lambda half: pl.BlockSpec((1,) + blk_shape, lambda j, t, w: (j,) + at(half(w), t))
    grid_spec = pltpu.PrefetchScalarGridSpec(
        num_scalar_prefetch=1, grid=(n_slot, nr),
        in_specs=[blk(lambda w: w[0]), blk(lambda w: 1 - w[0])],
        out_specs=pl.BlockSpec((1,) + blk_shape, lambda j, t, w: (j,) + at(0, t)),
        scratch_shapes=[pltpu.VMEM((2,) + blk_shape, F32), pltpu.SemaphoreType.DMA((2,)),
                        pltpu.SemaphoreType.DMA((2,)), pltpu.SemaphoreType.REGULAR((2,))])
    return pl.pallas_call(
        body, name=name, grid_spec=grid_spec, out_shape=jax.ShapeDtypeStruct((n_slot,) + half_shape, out_dtype),
        compiler_params=_cparams("arbitrary", "arbitrary"),
    )(where, g, g)


def _chip_exchange(parts, *, name):
    nt = len(parts)

    def body(*refs):
        src, got = refs[:nt], refs[nt:2 * nt]
        send, recv = refs[2 * nt:]
        x, y, c, chips = _place()
        remote = []
        for t in range(nt):
            for k in range(3):
                remote.append(pltpu.make_async_remote_copy(
                    src_ref=src[t].at[2 * chips[k][0] + chips[k][1]], dst_ref=got[t].at[k],
                    send_sem=send.at[t, k], recv_sem=recv.at[t, k],
                    device_id=(chips[k][0], chips[k][1], c), device_id_type=MESH))
        for cp in remote:
            cp.start()
        for cp in remote:
            cp.wait_recv()
        for cp in remote:
            cp.wait_send()

    return pl.pallas_call(
        body, name=name, in_specs=[ANY] * nt, out_specs=[ANY] * nt,
        out_shape=[jax.ShapeDtypeStruct((3,) + a.shape[1:], a.dtype) for a in parts],
        scratch_shapes=[pltpu.SemaphoreType.DMA((nt, 3)), pltpu.SemaphoreType.DMA((nt, 3))],
    )(*parts)


def _sum_join(p, got, where, axis, *, name):
    _, hr, cdim = p.shape
    full = (2 * hr, cdim) if axis == 0 else (hr, 2 * cdim)
    blk_shape, n, at = _comm_chunks(full, axis)
    step_len = blk_shape[axis]
    half_len = full[axis] // 2

    def body(w_ref, p_ref, g_ref, out, buf, lsem, ssem, rsem):
        x, y, c, _ = _place()
        sib = (x, y, 1 - c)
        r = pl.program_id(0)

        def part(start, size):
            return out.at[(slice(None),) * axis + (pl.ds(start, size),)]

        def copies(step, slot):
            rows = part(pl.multiple_of(c * half_len + step * step_len, 8 if axis == 0 else 128), step_len)
            return (pltpu.make_async_copy(buf.at[slot], rows, lsem.at[slot]),
                    pltpu.make_async_remote_copy(src_ref=buf.at[slot], dst_ref=rows, send_sem=ssem.at[slot],
                                                 recv_sem=rsem, device_id=sib, device_id_type=MESH))

        s = lax.rem(r, 2)

        @pl.when(r >= 2)
        def _():
            lc, rm = copies(r - 2, s)
            lc.wait()
            rm.wait_send()

        buf[s] = p_ref[0].astype(F32) + g_ref[0].astype(F32) + g_ref[1].astype(F32) + g_ref[2].astype(F32)
        lc, rm = copies(r, s)
        lc.start()
        rm.start()

        @pl.when(r == n - 1)
        def _():
            for step in range(max(0, n - 2), n):
                lc, rm = copies(step, step % 2)
                lc.wait()
                rm.wait_send()
            whole = part(0, half_len)
            pltpu.make_async_remote_copy(src_ref=whole, dst_ref=whole, send_sem=ssem.at[0], recv_sem=rsem,
                                         device_id=sib, device_id_type=MESH).wait_recv()

    grid_spec = pltpu.PrefetchScalarGridSpec(
        num_scalar_prefetch=1, grid=(n,),
        in_specs=[pl.BlockSpec((1,) + blk_shape, lambda t, w: (w[1],) + at(0, t)),
                  pl.BlockSpec((3,) + blk_shape, lambda t, w: (0,) + at(0, t))],
        out_specs=ANY,
        scratch_shapes=[pltpu.VMEM((2,) + blk_shape, F32), pltpu.SemaphoreType.DMA((2,)),
                        pltpu.SemaphoreType.DMA((2,)), pltpu.SemaphoreType.DMA])
    return pl.pallas_call(
        body, name=name, grid_spec=grid_spec, out_shape=jax.ShapeDtypeStruct(full, F32),
        compiler_params=_cparams("arbitrary"),
    )(where, p, got)


def _rider_gather_send(shards, axes):
    nt = len(shards)

    def copies(src, dst, send, recv, lsem):
        x, y, c, chips = _place()
        me = 2 * x + y
        local = [pltpu.make_async_copy(src[t], dst[t].at[me], lsem.at[t]) for t in range(nt)]
        out, landed = [], []
        for t in range(nt):
            for k in range(3):
                peer = (chips[k][0], chips[k][1], c)
                out.append(pltpu.make_async_remote_copy(
                    src_ref=_half(src[t], axes[t], c), dst_ref=_half(dst[t], axes[t], c, lead=(me,)),
                    send_sem=send.at[t, k], recv_sem=recv.at[t, k], device_id=peer, device_id_type=MESH))
                theirs = _half(dst[t], axes[t], c, lead=(2 * chips[k][0] + chips[k][1],))
                landed.append(pltpu.make_async_remote_copy(
                    src_ref=theirs, dst_ref=theirs, send_sem=send.at[t, k], recv_sem=recv.at[t, k],
                    device_id=peer, device_id_type=MESH))
        return local, out, landed

    def start(src, dst, sems):
        local, out, _ = copies(src, dst, *sems)
        for cp in local + out:
            cp.start()

    def finish(src, dst, sems):
        local, out, landed = copies(src, dst, *sems)
        for cp in landed:
            cp.wait_recv()
        for cp in out:
            cp.wait_send()
        for cp in local:
            cp.wait()

    return _Rider(shards, [jax.ShapeDtypeStruct((N_CHIP,) + a.shape, a.dtype) for a in shards],
                  [pltpu.SemaphoreType.DMA((nt, 3)), pltpu.SemaphoreType.DMA((nt, 3)), pltpu.SemaphoreType.DMA((nt,))],
                  start, finish)


def _rider_gather_forward(bufs, axes):
    nt = len(bufs)

    def copies(src, dst, send, recv):
        x, y, c, chips = _place()
        mine, theirs = [], []
        for t in range(nt):
            for k in range(3):
                slot = 2 * chips[k][0] + chips[k][1]
                for hc, into in ((c, mine), (1 - c, theirs)):
                    into.append(pltpu.make_async_remote_copy(
                        src_ref=_half(src[t], axes[t], hc, lead=(slot,)),
                        dst_ref=_half(dst[t], axes[t], hc, lead=(slot,)),
                        send_sem=send.at[t, k], recv_sem=recv.at[t, k], device_id=(x, y, 1 - c), device_id_type=MESH))
        return mine, theirs

    def start(src, dst, sems):
        for cp in copies(src, dst, *sems)[0]:
            cp.start()

    def finish(src, dst, sems):
        mine, theirs = copies(src, dst, *sems)
        for cp in theirs:
            cp.wait_recv()
        for cp in mine:
            cp.wait_send()

    return _Rider(bufs, [jax.ShapeDtypeStruct(a.shape, a.dtype) for a in bufs],
                  [pltpu.SemaphoreType.DMA((nt, 3)), pltpu.SemaphoreType.DMA((nt, 3))], start, finish,
                  aliases={t: t for t in range(nt)})


def _rider_chip_exchange(parts):
    nt = len(parts)

    def copies(src, got, send, recv):
        x, y, c, chips = _place()
        return [pltpu.make_async_remote_copy(
            src_ref=src[t].at[2 * chips[k][0] + chips[k][1]], dst_ref=got[t].at[k], send_sem=send.at[t, k],
            recv_sem=recv.at[t, k], device_id=(chips[k][0], chips[k][1], c), device_id_type=MESH)
            for t in range(nt) for k in range(3)]

    def start(src, got, sems):
        for cp in copies(src, got, *sems):
            cp.start()

    def finish(src, got, sems):
        remote = copies(src, got, *sems)
        for cp in remote:
            cp.wait_recv()
        for cp in remote:
            cp.wait_send()

    return _Rider(parts, [jax.ShapeDtypeStruct((3,) + a.shape[1:], a.dtype) for a in parts],
                  [pltpu.SemaphoreType.DMA((nt, 3)), pltpu.SemaphoreType.DMA((nt, 3))], start, finish)


def _gather_all(block, *, name):
    m_per, n = block.shape

    def body(x_ref, out_ref, send_sems, recv_sems, local_sem):
        x, y, c, chips = _place()
        me, sibling = (x, y, c), (x, y, 1 - c)

        def rows(px, py, pc):
            return out_ref.at[4 * px + 2 * py + pc]

        def copy(k, blk, to, src=None):
            return pltpu.make_async_remote_copy(
                src_ref=rows(*blk) if src is None else src, dst_ref=rows(*blk),
                send_sem=send_sems.at[k], recv_sem=recv_sems.at[k], device_id=to, device_id_type=MESH)

        mine = pltpu.make_async_copy(x_ref, rows(*me), local_sem)
        mine.start()
        first = [copy(0, me, sibling, src=x_ref)]
        first += [copy(1 + j, me, (*chip, c), src=x_ref) for j, chip in enumerate(chips)]
        for cp in first:
            cp.start()
        passed = [copy(4 + j, (*chip, c), sibling) for j, chip in enumerate(chips)]
        for j, chip in enumerate(chips):
            copy(1 + j, (*chip, c), me).wait_recv()
            passed[j].start()
        copy(0, sibling, me).wait_recv()
        for j, chip in enumerate(chips):
            copy(4 + j, (*chip, 1 - c), me).wait_recv()
        for cp in first + passed:
            cp.wait_send()
        mine.wait()

    return pl.pallas_call(
        body, name=name,
        out_shape=jax.ShapeDtypeStruct((N_DEV, m_per, n), block.dtype),
        in_specs=[pl.BlockSpec(memory_space=pltpu.VMEM)], out_specs=pl.BlockSpec(memory_space=pltpu.VMEM),
        scratch_shapes=[pltpu.SemaphoreType.DMA((7,)), pltpu.SemaphoreType.DMA((7,)), pltpu.SemaphoreType.DMA],
        compiler_params=pltpu.CompilerParams(vmem_limit_bytes=VMEM_LIMIT),
    )(block)


def _sum_slots(slots, *, name):
    n, m, c = slots.shape
    t = _rows_tile(m, c * n)

    def body(s_ref, o_ref):
        acc = s_ref[0]
        for k in range(1, n):
            acc = acc + s_ref[k]
        o_ref[...] = acc

    return pl.pallas_call(
        body, name=name, grid=(m // t,), in_specs=[pl.BlockSpec((n, t, c), lambda i: (0, i, 0))],
        out_specs=pl.BlockSpec((t, c), lambda i: (i, 0)), out_shape=jax.ShapeDtypeStruct((m, c), F32),
        compiler_params=_cparams("parallel"),
    )(slots)


def _pad_rows(a, rows):
    return a if a.shape[0] == rows else jnp.pad(a, ((0, rows - a.shape[0]), (0, 0)))


def _w_in_padded(shards):
    full = shards.reshape(IN_COLS, shards.shape[2])
    return jnp.concatenate([_pad_rows(full[SEG[n][2]:SEG[n][2] + SEG[n][3]], SEG[n][1]) for n in SEG_ORDER], axis=0)


def _w_in_unpadded(gp):
    full = jnp.concatenate([gp[SEG[n][0]:SEG[n][0] + SEG[n][3]] for n in ORIG_ORDER], axis=0)
    return full.reshape(N_CHIP, IN_COLS // N_CHIP, gp.shape[1])


def _pad_heads(w, true_w, pad_w):
    r = w.shape[0]
    h = w.shape[1] // true_w
    return jnp.pad(w.reshape(r, h, true_w), ((0, 0), (0, 0), (0, pad_w - true_w))).reshape(r, h * pad_w)


def _unpad_heads(w, true_w, pad_w):
    r = w.shape[0]
    h = w.shape[1] // pad_w
    return w.reshape(r, h, pad_w)[:, :, :true_w].reshape(r, h * true_w)


def _cols_to_slots(a):
    return a.reshape(a.shape[0], N_CHIP, a.shape[1] // N_CHIP).transpose(1, 0, 2)


def _slots_to_cols(a):
    return jnp.concatenate([a[j] for j in range(N_CHIP)], axis=1)


def _to_heads(a, h, d):
    return a.reshape(a.shape[0], h, d).transpose(1, 0, 2)


def _from_heads(a):
    return a.transpose(1, 0, 2).reshape(a.shape[1], -1)


SMALL = [("norm_g", 2048), ("ret_norm_g", 512), ("gla_ba_f", 256), ("gla_ba_b", 256), ("gla_norm_g", 512),
         ("pool_w", 4 * 128 * 128), ("pool_scale", 512), ("mla_q_norm_g", 512), ("mla_kv_norm_g", 256),
         ("mla_qk_norm_q", 192), ("mla_qk_norm_k", 192)]


def _pack_small(vals):
    parts = []
    for name, n in SMALL:
        parts += [v.reshape(-1) for v in vals[name]]
        if (DEPTH * n) % 1024:
            parts.append(jnp.zeros((-(DEPTH * n)) % 1024, F32))
    parts += [vals["loss"].reshape(-1), jnp.zeros(1023, F32)]
    return jnp.concatenate(parts).reshape(-1, 128)


def _unpack_small(block):
    flat = block.reshape(-1)
    out, off = {}, 0
    for name, n in SMALL:
        out[name] = flat[off:off + DEPTH * n]
        off += DEPTH * n + (-(DEPTH * n)) % 1024
    out["loss"] = flat[off]
    return out


def _layer_weights(l, p, g):
    wa = jnp.zeros((128, 512), F32)
    wa = wa.at[0:GLA_RANK, 0:256].set(_slots_to_cols(g["gla_wa2_f"]))
    wa = wa.at[GLA_RANK:2 * GLA_RANK, 256:512].set(_slots_to_cols(g["gla_wa2_b"]))
    return dict(
        norm_g=p["norm_g"][l][None, :],
        w_in=_w_in_padded(g["w_in"]),
        w_out=g["w_out"].reshape(4 * g["w_out"].shape[1], -1),
        ret_norm_g=p["ret_norm_g"][l][None, :],
        wa=_bf(wa),
        ba=jnp.concatenate([p["gla_ba_f"][l], p["gla_ba_b"][l]])[None, :],
        gla_norm_g=p["gla_norm_g"][l][None, :],
        pool_w=_bf(p["pool_w"][l]),
        pool_scale=p["pool_scale"][l][None, :],
        qg=p["mla_q_norm_g"][l][None, :],
        wq=_pad_heads(_slots_to_cols(g["mla_wq_b"]), MLA_QK, MLA_QKP),
        kvg=p["mla_kv_norm_g"][l][None, :],
        wkv=_slots_to_cols(g["mla_wkv_b"]),
        qng=jnp.pad(p["mla_qk_norm_q"][l], (0, MLA_QKP - MLA_QK))[None, :],
        kng=jnp.pad(p["mla_qk_norm_k"][l], (0, MLA_QKP - MLA_QK))[None, :],
    )


def _layer_fwd(l, x, w, tabs, next_shards=None):
    ret_cos, ret_sin, mla_cos, mla_sp, mla_sn = tabs
    nm = lambda s: f"l{l}_{s}"
    h = _rmsnorm_fwd(x, w["norm_g"], name=nm("norm"))
    if next_shards is None:
        z = _matmul(h, w["w_in"], tb=True, name=nm("in_proj"))
    else:
        z, landed = _matmul(h, w["w_in"], tb=True, rider=_rider_gather_send(next_shards[:1], SHARD_AXES[:1]),
                            name=nm("in_proj"))
    qr, kr = _ret_pre(z, ret_cos, ret_sin, name=nm("ret_pre"))
    ret_o = _bla(qr, kr, z, _ret_log_gamma(False), (0, 0, SEG["rv"][0] // 512), name=nm("ret_scan"))
    y_a = _post(ret_o, z, SEG["rg"][0] // 512, w["ret_norm_g"], norm=True, name=nm("ret_post"))
    la = _gla_gate(z, w["wa"], w["ba"], name=nm("gla_gate"))
    la_h = la.reshape(la.shape[0], 2, GLA_HEADS, GLA_DK).transpose(1, 2, 0, 3)
    gq = _to_heads(z[:, SEG["gq"][0]:SEG["gq"][0] + 256], GLA_HEADS, GLA_DK)
    gk = _to_heads(z[:, SEG["gk"][0]:SEG["gk"][0] + 256], GLA_HEADS, GLA_DK)
    if next_shards is None:
        gla_o, gla_st = _gla_fwd(gq, gk, z, la_h, name=nm("gla_scan"))
    else:
        gla_o, gla_st, more = _gla_fwd(gq, gk, z, la_h, rider=_rider_gather_send(next_shards[1:], SHARD_AXES[1:]),
                                       name=nm("gla_scan"))
        landed = list(landed) + list(more)
    y_b = _post(gla_o, z, SEG["gg"][0] // 512, w["gla_norm_g"], norm=True, name=nm("gla_post"))
    y_c = _pool_fwd(z, w["pool_w"], w["pool_scale"], name=nm("pool"))
    q, k, v = _mla_pre(z, w["qg"], w["wq"], w["kvg"], w["wkv"], w["qng"], w["kng"], mla_cos, mla_sp, mla_sn,
                       name=nm("mla_pre"))
    if next_shards is None:
        (att_o, lse), gathered = _flash_fwd(q, k, v, name=nm("attn")), None
    else:
        att_o, lse, gathered = _flash_fwd(q, k, v, rider=_rider_gather_forward(landed, SHARD_AXES), name=nm("attn"))
    y_d = _post([att_o], z, SEG["mg"][0] // 512, w["qg"], norm=False, name=nm("mla_post"))
    y = jnp.concatenate([y_a, y_b, y_c, y_d], axis=1)
    x_next = _matmul(y, w["w_out"], add=x, name=nm("out_proj"))
    saved = dict(x=x, h=h, z=z, y=y, qr=qr, kr=kr, ret_o=ret_o, la_h=la_h, gq=gq, gk=gk, gla_o=gla_o, gla_st=gla_st,
                 q=q, k=k, v=v, att_o=att_o, lse=lse)
    return x_next, saved, gathered


def _layer_bwd(l, dx_next, w, sv, tabs, riding_parts=None):
    ret_cos, ret_sin, mla_cos, mla_sp, mla_sn = tabs
    nm = lambda s: f"l{l}_{s}"
    z = sv["z"]
    dy = _matmul(dx_next, w["w_out"], tb=True, name=nm("out_proj_dy"))
    d_w_out = _matmul(sv["y"], dx_next, ta=True, tn=512, name=nm("out_proj_dw"))
    d_rg, d_ret_o, d_ret_g = _post_bwd(dy, 0, sv["ret_o"], z, SEG["rg"][0] // 512, w["ret_norm_g"], norm=True,
                                       name=nm("ret_post_bwd"))
    vcol = SEG["rv"][0] // 512
    dqr = _bla(d_ret_o, z, sv["kr"], _ret_log_gamma(False), (0, vcol, 0), name=nm("ret_scan_dq"))
    dkr = _bla(z, d_ret_o, sv["qr"], _ret_log_gamma(True), (vcol, 0, 0), name=nm("ret_scan_dk"))
    drv = _bla(sv["kr"], sv["qr"], d_ret_o, _ret_log_gamma(True), (0, 0, 0), name=nm("ret_scan_dv"))
    d_rq, d_rk = _ret_pre_bwd(dqr, dkr, ret_cos, ret_sin, name=nm("ret_pre_bwd"))
    d_rv = _add_n([drv[0], drv[1]], out_dtype=BF16, name=nm("ret_dv_sum"))
    d_gg, d_gla_o, d_gla_g = _post_bwd(dy, 1, sv["gla_o"], z, SEG["gg"][0] // 512, w["gla_norm_g"], norm=True,
                                       name=nm("gla_post_bwd"))
    dq2, dk2, dla2, dv2 = _gla_bwd(sv["gq"], sv["gk"], z, sv["la_h"], d_gla_o, sv["gla_st"], name=nm("gla_scan_bwd"))
    d_gq = _bf(_from_heads(dq2[0] + dq2[1]))
    d_gk = _bf(_from_heads(dk2[0] + dk2[1]))
    d_gv = _add_n([dv2[0], dv2[1]], out_dtype=BF16, name=nm("gla_dv_sum"))
    dla = jnp.concatenate([_from_heads(dla2[0]), _from_heads(dla2[1])], axis=1)
    d_ga, d_wa, d_ba = _gla_gate_bwd(dla, z, w["wa"], w["ba"], name=nm("gla_gate_bwd"))
    d_pv, d_pg, d_pool_w, d_pool_scale = _pool_bwd(dy, z, w["pool_w"], w["pool_scale"], name=nm("pool_bwd"))
    d_mg, d_att_o, _ = _post_bwd(dy, 3, [sv["att_o"]], z, SEG["mg"][0] // 512, w["qg"], norm=False,
                                 name=nm("mla_post_bwd"))
    if riding_parts is None:
        (dq, dk, dv), rode = _flash_bwd(sv["q"], sv["k"], sv["v"], d_att_o, sv["att_o"], sv["lse"],
                                        name=nm("attn_bwd")), None
    else:
        dq, dk, dv, rode = _flash_bwd(sv["q"], sv["k"], sv["v"], d_att_o, sv["att_o"], sv["lse"],
                                      rider=_rider_chip_exchange(riding_parts), name=nm("attn_bwd"))
    d_mq, d_mkv, d_mkr, d_wq, d_wkv, d_qg, d_kvg, d_qng, d_kng = _mla_pre_bwd(
        dq, dk, dv, z, w["qg"], w["wq"], w["kvg"], w["wkv"], w["qng"], w["kng"], mla_cos, mla_sp, mla_sn,
        name=nm("mla_pre_bwd"))
    segs = dict(rq=d_rq, rk=d_rk, rv=d_rv, rg=d_rg, gv=d_gv, gg=d_gg, pv=d_pv, pg=d_pg, mq=d_mq, mg=d_mg,
                gq=d_gq, gk=d_gk, mkv=d_mkv, ga=d_ga, mkr=d_mkr)
    dz = jnp.concatenate([segs[n] for n in SEG_ORDER], axis=1)
    dh = _matmul(dz, w["w_in"], tn=512, name=nm("in_proj_dh"))
    d_w_in = _matmul(dz, sv["h"], ta=True, name=nm("in_proj_dw"))
    dx, d_norm_g = _rmsnorm_bwd(sv["x"], dh, w["norm_g"], dx_next, name=nm("norm_bwd"))
    sharded = dict(
        w_in=_w_in_unpadded(d_w_in),
        w_out=d_w_out.reshape(N_CHIP, d_w_out.shape[0] // N_CHIP, d_w_out.shape[1]),
        mla_wq_b=_cols_to_slots(_unpad_heads(d_wq, MLA_QK, MLA_QKP)),
        mla_wkv_b=_cols_to_slots(d_wkv),
        gla_wa2_f=_cols_to_slots(d_wa[0:GLA_RANK, 0:256]),
        gla_wa2_b=_cols_to_slots(d_wa[GLA_RANK:2 * GLA_RANK, 256:512]),
    )
    small = dict(
        norm_g=d_norm_g[0], ret_norm_g=d_ret_g[0], gla_ba_f=d_ba[0, :256], gla_ba_b=d_ba[0, 256:],
        gla_norm_g=d_gla_g[0], pool_w=d_pool_w.reshape(-1), pool_scale=d_pool_scale[0], mla_q_norm_g=d_qg[0],
        mla_kv_norm_g=d_kvg[0], mla_qk_norm_q=d_qng[0, :MLA_QK], mla_qk_norm_k=d_kng[0, :MLA_QK],
    )
    return dx, sharded, small, rode


SHARDED = ["w_in", "w_out", "mla_wq_b", "mla_wkv_b", "gla_wa2_f", "gla_wa2_b"]
WEIGHTS = ["norm_g", "w_in", "ret_norm_g", "gla_wa2_f", "gla_ba_f", "gla_wa2_b", "gla_ba_b", "gla_norm_g", "pool_w",
           "pool_scale", "mla_q_norm_g", "mla_wq_b", "mla_kv_norm_g", "mla_wkv_b", "mla_qk_norm_q", "mla_qk_norm_k",
           "w_out"]


SHARD_AXES = [1, 0, 0, 0, 0, 0]


def _layer_shards(p, l):
    return [jnp.swapaxes(p["w_in"], 1, 2)[l].astype(BF16), p["w_out"][l].astype(BF16), p["mla_wq_b"][l].astype(BF16),
            p["mla_wkv_b"][l].astype(BF16), p["gla_wa2_f"][l], p["gla_wa2_b"][l]]


def _step(p, where):
    x = p["x"][0]
    tabs = _rope_tables(x.shape[0])
    got0 = _gather_shards(_layer_shards(p, 0), SHARD_AXES, name="l0_gather_weights")
    w0 = _layer_weights(0, p, dict(zip(SHARDED, got0)))
    x1, sv0, got1 = _layer_fwd(0, x, w0, tabs, next_shards=_layer_shards(p, 1))
    w1 = _layer_weights(1, p, dict(zip(SHARDED, got1)))
    x2, sv1, _ = _layer_fwd(1, x1, w1, tabs)
    dx, loss = _loss_head(x2, p["loss_target"][0], name="loss_head")

    big, big_axes = SHARDED[:2], SHARD_AXES[:2]

    def pair_sums(tag, tensors, axes, names):
        return [_pair_reduce(a, where, ax, out_dtype=BF16, name=f"{tag}_pair_reduce_{n}")
                for a, ax, n in zip(tensors, axes, names)]

    def joined(tag, pair, others, axes, names):
        return [_sum_join(a, b, where, ax, name=f"{tag}_sum_join_{n}")
                for a, b, ax, n in zip(pair, others, axes, names)]

    dx, sharded1, small1, _ = _layer_bwd(1, dx, w1, sv1, tabs)
    pair1 = pair_sums("l1", [sharded1[n] for n in big], big_axes, big)
    dx, sharded0, small0, others1 = _layer_bwd(0, dx, w0, sv0, tabs, riding_parts=pair1)
    grads1 = joined("l1", pair1, others1, big_axes, big)
    packed = jnp.concatenate([sh[n].reshape(N_CHIP, -1, 128) for sh in (sharded0, sharded1) for n in SHARDED[2:]],
                             axis=1)
    pair0 = pair_sums("l0", [sharded0[n] for n in big] + [packed], big_axes + [0], big + ["rest"])
    grads0 = joined("l0", pair0, _chip_exchange(pair0, name="l0_chip_exchange"), big_axes + [0], big + ["rest"])
    grads = {n: jnp.stack([g0, g1]) for n, g0, g1 in zip(big, grads0, grads1)}
    rest, off = grads0[2], 0
    pieces = {n: [] for n in SHARDED[2:]}
    for sh in (sharded0, sharded1):
        for n in SHARDED[2:]:
            rows = sh[n].shape[1] * sh[n].shape[2] // 128
            pieces[n].append(rest[off:off + rows].reshape(sh[n].shape[1:]))
            off += rows
    grads.update({n: jnp.stack(v) for n, v in pieces.items()})
    small = {n: [small0[n], small1[n]] for n, _ in SMALL}
    small["loss"] = loss
    return dx[None], grads, small


def kernel(x, norm_g, w_in, ret_norm_g, gla_wa2_f, gla_ba_f, gla_wa2_b, gla_ba_b, gla_norm_g, pool_w, pool_scale, mla_q_norm_g, mla_wq_b, mla_kv_norm_g, mla_wkv_b, mla_qk_norm_q, mla_qk_norm_k, w_out, loss_target, m_norm_g, m_w_in, m_ret_norm_g, m_gla_wa2_f, m_gla_ba_f, m_gla_wa2_b, m_gla_ba_b, m_gla_norm_g, m_pool_w, m_pool_scale, m_mla_q_norm_g, m_mla_wq_b, m_mla_kv_norm_g, m_mla_wkv_b, m_mla_qk_norm_q, m_mla_qk_norm_k, m_w_out, v_norm_g, v_w_in, v_ret_norm_g, v_gla_wa2_f, v_gla_ba_f, v_gla_wa2_b, v_gla_ba_b, v_gla_norm_g, v_pool_w, v_pool_scale, v_mla_q_norm_g, v_mla_wq_b, v_mla_kv_norm_g, v_mla_wkv_b, v_mla_qk_norm_q, v_mla_qk_norm_k, v_w_out):
    p = dict(x=x, norm_g=norm_g, w_in=w_in, ret_norm_g=ret_norm_g, gla_wa2_f=gla_wa2_f, gla_ba_f=gla_ba_f,
             gla_wa2_b=gla_wa2_b, gla_ba_b=gla_ba_b, gla_norm_g=gla_norm_g, pool_w=pool_w, pool_scale=pool_scale,
             mla_q_norm_g=mla_q_norm_g, mla_wq_b=mla_wq_b, mla_kv_norm_g=mla_kv_norm_g, mla_wkv_b=mla_wkv_b,
             mla_qk_norm_q=mla_qk_norm_q, mla_qk_norm_k=mla_qk_norm_k, w_out=w_out, loss_target=loss_target)
    moments = dict(
        m=dict(norm_g=m_norm_g, w_in=m_w_in, ret_norm_g=m_ret_norm_g, gla_wa2_f=m_gla_wa2_f, gla_ba_f=m_gla_ba_f,
               gla_wa2_b=m_gla_wa2_b, gla_ba_b=m_gla_ba_b, gla_norm_g=m_gla_norm_g, pool_w=m_pool_w,
               pool_scale=m_pool_scale, mla_q_norm_g=m_mla_q_norm_g, mla_wq_b=m_mla_wq_b,
               mla_kv_norm_g=m_mla_kv_norm_g, mla_wkv_b=m_mla_wkv_b, mla_qk_norm_q=m_mla_qk_norm_q,
               mla_qk_norm_k=m_mla_qk_norm_k, w_out=m_w_out),
        v=dict(norm_g=v_norm_g, w_in=v_w_in, ret_norm_g=v_ret_norm_g, gla_wa2_f=v_gla_wa2_f, gla_ba_f=v_gla_ba_f,
               gla_wa2_b=v_gla_wa2_b, gla_ba_b=v_gla_ba_b, gla_norm_g=v_gla_norm_g, pool_w=v_pool_w,
               pool_scale=v_pool_scale, mla_q_norm_g=v_mla_q_norm_g, mla_wq_b=v_mla_wq_b,
               mla_kv_norm_g=v_mla_kv_norm_g, mla_wkv_b=v_mla_wkv_b, mla_qk_norm_q=v_mla_qk_norm_q,
               mla_qk_norm_k=v_mla_qk_norm_k, w_out=v_w_out))

    where = jnp.stack([lax.axis_index("c"), 2 * lax.axis_index("x") + lax.axis_index("y")]).astype(jnp.int32)
    grad_x, grads, small = _step(p, where)

    slots = _gather_all(_pack_small(small), name="gather_small")
    total = _unpack_small(_sum_slots(slots, name="sum_small"))
    for n, _ in SMALL:
        grads[n] = total[n].reshape(p[n].shape)
    loss = total["loss"]

    delta, new_m, new_v = {}, {}, {}
    for n in WEIGHTS:
        turn = (lambda a: jnp.swapaxes(a, 1, 2)) if n == "w_in" else (lambda a: a)
        outs = _adamw(turn(p[n]), grads[n], turn(moments["m"][n]), turn(moments["v"][n]), name=f"adamw_{n}")
        grads[n] = turn(grads[n])
        delta[n], new_m[n], new_v[n] = (turn(o) for o in outs)
    return (loss, grad_x, *[grads[n] for n in WEIGHTS], *[delta[n] for n in WEIGHTS],
            *[new_m[n] for n in WEIGHTS], *[new_v[n] for n in WEIGHTS])
```

```python
import functools
import math

import jax
import jax.numpy as jnp
from jax import lax
from jax.experimental import pallas as pl
from jax.experimental.pallas import tpu as pltpu

F32 = jnp.float32
BF16 = jnp.bfloat16
MESH = pl.DeviceIdType.MESH

EPS = 1e-6
ROPE_THETA = 10000.0
DEPTH = 2
N_DEV = 8
N_CHIP = 4

GROUP_W = 512
RET_HEADS = 4
RET_HD = 128
RET_CHUNK = 256
GLA_HEADS = 4
GLA_DK = 64
GLA_DV = 128
GLA_RANK = 16
GLA_TAU = 16.0
GLA_CHUNK = 64
POOL_GROUPS = 4
POOL_GW = 128
POOL_HALO = 8
POOL_TILE = 256
MLA_HEADS = 4
MLA_NOPE = 128
MLA_ROPE = 64
MLA_QK = MLA_NOPE + MLA_ROPE
MLA_QKP = 256
MLA_V = 128
MLA_Q_RANK = 512
MLA_KV_RANK = 256
MLA_SCALE = MLA_QK ** -0.5
FLASH_STRIP = 1024

ADAM_LR = 0.001
ADAM_B1 = 0.9
ADAM_B2 = 0.999
ADAM_EPS = 1e-08
ADAM_WD = 0.01
ADAM_STEP = 10

VMEM_LIMIT = 56 * 1024 * 1024
ROW_TILE = 512

SEG = {
    "rq": (0, 512, 0, 512), "rk": (512, 512, 512, 512), "rv": (1024, 512, 1024, 512), "rg": (1536, 512, 1536, 512),
    "gv": (2048, 512, 2560, 512), "gg": (2560, 512, 3072, 512),
    "pv": (3072, 512, 3616, 512), "pg": (3584, 512, 4128, 512),
    "mq": (4096, 512, 4640, 512), "mg": (4608, 512, 5472, 512),
    "gq": (5120, 256, 2048, 256), "gk": (5376, 256, 2304, 256), "mkv": (5632, 256, 5152, 256),
    "ga": (5888, 128, 3584, 32), "mkr": (6016, 128, 5408, 64),
}
SEG_ORDER = ["rq", "rk", "rv", "rg", "gv", "gg", "pv", "pg", "mq", "mg", "gq", "gk", "mkv", "ga", "mkr"]
IN_COLS = 5984
IN_PAD = 6144
ORIG_ORDER = ["rq", "rk", "rv", "rg", "gq", "gk", "gv", "gg", "ga", "pv", "pg", "mq", "mkv", "mkr", "mg"]


def _cparams(*sem):
    return pltpu.CompilerParams(dimension_semantics=tuple(sem), vmem_limit_bytes=VMEM_LIMIT)


def _bf(v):
    return v.astype(BF16)


def _dot(a, b, ca=1, cb=0):
    return lax.dot_general(_bf(a), _bf(b), (((ca,), (cb,)), ((), ())), preferred_element_type=F32)


def _sigmoid(x):
    return 1.0 / (1.0 + jnp.exp(-x))


def _silu_parts(g):
    sg = _sigmoid(g)
    return g * sg, sg * (1.0 + g * (1.0 - sg))


class _Rider:
    def __init__(self, ins, outs, sems, start, finish, aliases=None):
        self.ins, self.outs, self.sems, self.start, self.finish = list(ins), list(outs), list(sems), start, finish
        self.aliases = dict(aliases or {})


def _ride(body, rider, n_in, n_out, grid):
    if rider is None:
        return body
    ri, ro, rs = len(rider.ins), len(rider.outs), len(rider.sems)

    def wrapped(*refs):
        ins, refs = refs[:n_in], refs[n_in:]
        rin, refs = refs[:ri], refs[ri:]
        outs, refs = refs[:n_out], refs[n_out:]
        rout, refs = refs[:ro], refs[ro:]
        scratch, sems = refs[:len(refs) - rs], refs[len(refs) - rs:]
        first = pl.program_id(0) == 0
        last = pl.program_id(0) == grid[0] - 1
        for ax in range(1, len(grid)):
            first = jnp.logical_and(first, pl.program_id(ax) == 0)
            last = jnp.logical_and(last, pl.program_id(ax) == grid[ax] - 1)

        @pl.when(first)
        def _():
            rider.start(rin, rout, sems)

        body(*ins, *outs, *scratch)

        @pl.when(last)
        def _():
            rider.finish(rin, rout, sems)

    return wrapped


def _ride_call(body, rider, *, name, grid, in_specs, out_specs, out_shape, scratch_shapes, args, sem):
    n_in, n_out = len(in_specs), len(out_specs)
    if rider is None:
        return pl.pallas_call(body, name=name, grid=grid, in_specs=in_specs, out_specs=out_specs, out_shape=out_shape,
                              scratch_shapes=scratch_shapes, compiler_params=_cparams(*sem))(*args), []
    outs = pl.pallas_call(
        _ride(body, rider, n_in, n_out, grid), name=name, grid=grid,
        in_specs=list(in_specs) + [ANY] * len(rider.ins), out_specs=list(out_specs) + [ANY] * len(rider.outs),
        out_shape=list(out_shape) + rider.outs, scratch_shapes=list(scratch_shapes) + rider.sems,
        input_output_aliases={n_in + i: n_out + o for i, o in rider.aliases.items()},
        compiler_params=_cparams(*(["arbitrary"] * len(grid))),
    )(*args, *rider.ins)
    return outs[:n_out], outs[n_out:]


def _matmul(a, b, *, ta=False, tb=False, out_dtype=F32, tm=512, tn=1024, tk=None, add=None, n_outer=True, rider=None,
            name):
    m, kdim = (a.shape[1], a.shape[0]) if ta else a.shape
    n = b.shape[0] if tb else b.shape[1]
    tm, tn = min(tm, m), min(tn, n)
    tk = kdim if tk is None else min(tk, kdim)
    assert m % tm == 0 and n % tn == 0 and kdim % tk == 0
    nk = kdim // tk
    ca, cb = (0 if ta else 1), (1 if tb else 0)

    def body(*refs):
        if add is None:
            a_ref, b_ref, o_ref = refs[:3]
            add_ref = None
        else:
            a_ref, b_ref, add_ref, o_ref = refs[:4]
        p = _dot(a_ref[...], b_ref[...], ca, cb)

        def finish(r):
            if add_ref is not None:
                r = r + add_ref[...]
            o_ref[...] = r.astype(out_dtype)

        if nk == 1:
            finish(p)
        else:
            acc = refs[-1]
            k = pl.program_id(2)

            @pl.when(k == 0)
            def _():
                acc[...] = p

            @pl.when(k > 0)
            def _():
                acc[...] += p

            @pl.when(k == nk - 1)
            def _():
                finish(acc[...])

    def ij(g0, g1):
        return (g1, g0) if n_outer else (g0, g1)

    a_spec = (pl.BlockSpec((tk, tm), lambda g0, g1, k: (k, ij(g0, g1)[0])) if ta
              else pl.BlockSpec((tm, tk), lambda g0, g1, k: (ij(g0, g1)[0], k)))
    b_spec = (pl.BlockSpec((tn, tk), lambda g0, g1, k: (ij(g0, g1)[1], k)) if tb
              else pl.BlockSpec((tk, tn), lambda g0, g1, k: (k, ij(g0, g1)[1])))
    o_spec = pl.BlockSpec((tm, tn), lambda g0, g1, k: ij(g0, g1))
    in_specs = [a_spec, b_spec] + ([o_spec] if add is not None else [])
    args = (a, b) + ((add,) if add is not None else ())
    grid = (n // tn, m // tm, nk) if n_outer else (m // tm, n // tn, nk)
    (out,), rode = _ride_call(
        body, rider, name=name, grid=grid, in_specs=in_specs, out_specs=[o_spec],
        out_shape=[jax.ShapeDtypeStruct((m, n), out_dtype)],
        scratch_shapes=[] if nk == 1 else [pltpu.VMEM((tm, tn), F32)], args=args,
        sem=("parallel", "parallel", "arbitrary"))
    return out if rider is None else (out, rode)


def _rmsnorm_fwd(x, g, *, name, tm=ROW_TILE):
    s, d = x.shape
    tm = min(tm, s)

    def body(x_ref, g_ref, h_ref):
        xv = x_ref[...]
        r = lax.rsqrt(jnp.mean(xv * xv, axis=-1, keepdims=True) + EPS)
        h_ref[...] = _bf(xv * r * g_ref[...])

    return pl.pallas_call(
        body, name=name, grid=(s // tm,),
        in_specs=[pl.BlockSpec((tm, d), lambda i: (i, 0)), pl.BlockSpec((1, d), lambda i: (0, 0))],
        out_specs=pl.BlockSpec((tm, d), lambda i: (i, 0)),
        out_shape=jax.ShapeDtypeStruct((s, d), BF16),
        compiler_params=_cparams("parallel"),
    )(x, g)


def _rmsnorm_bwd(x, dh, g, dres, *, name, tm=ROW_TILE):
    s, d = x.shape
    tm = min(tm, s)

    def body(x_ref, dh_ref, g_ref, dres_ref, dx_ref, dg_ref):
        i = pl.program_id(0)
        xv = x_ref[...]
        r = lax.rsqrt(jnp.mean(xv * xv, axis=-1, keepdims=True) + EPS)
        xn = xv * r
        dv = dh_ref[...]
        part = jnp.sum(dv * xn, axis=0, keepdims=True)

        @pl.when(i == 0)
        def _():
            dg_ref[...] = part

        @pl.when(i > 0)
        def _():
            dg_ref[...] += part

        dxn = dv * g_ref[...]
        dx_ref[...] = dres_ref[...] + r * (dxn - xn * jnp.mean(dxn * xn, axis=-1, keepdims=True))

    row = pl.BlockSpec((tm, d), lambda i: (i, 0))
    vec = pl.BlockSpec((1, d), lambda i: (0, 0))
    return pl.pallas_call(
        body, name=name, grid=(s // tm,), in_specs=[row, row, vec, row], out_specs=[row, vec],
        out_shape=[jax.ShapeDtypeStruct((s, d), F32), jax.ShapeDtypeStruct((1, d), F32)],
        compiler_params=_cparams("arbitrary"),
    )(x, dh, g, dres)


def _loss_head(xf, target, *, name, tm=ROW_TILE):
    s, d = xf.shape
    tm = min(tm, s)

    def body(x_ref, t_ref, dx_ref, l_ref):
        i = pl.program_id(0)
        e = x_ref[...] - t_ref[...]
        dx_ref[...] = e * (1.0 / d)
        rows = jnp.mean(e * e, axis=-1, keepdims=True)
        part = 0.5 * jnp.sum(rows, axis=0, keepdims=True)

        @pl.when(i == 0)
        def _():
            l_ref[...] = part

        @pl.when(i > 0)
        def _():
            l_ref[...] += part

    row = pl.BlockSpec((tm, d), lambda i: (i, 0))
    return pl.pallas_call(
        body, name=name, grid=(s // tm,), in_specs=[row, row],
        out_specs=[row, pl.BlockSpec((1, 1), lambda i: (0, 0))],
        out_shape=[jax.ShapeDtypeStruct((s, d), F32), jax.ShapeDtypeStruct((1, 1), F32)],
        compiler_params=_cparams("arbitrary"),
    )(xf, target)


def _rope_tables(s):
    pos = jnp.arange(s, dtype=F32)[:, None]
    inv_r = 1.0 / (ROPE_THETA ** (jnp.arange(0, RET_HD, 2, dtype=F32) / RET_HD))
    ang = pos * inv_r[None, :]
    ret_cos = jnp.concatenate([jnp.cos(ang), jnp.cos(ang)], axis=1)
    ret_sin = jnp.concatenate([-jnp.sin(ang), jnp.sin(ang)], axis=1)
    inv_m = 1.0 / (ROPE_THETA ** (jnp.arange(0, MLA_ROPE, 2, dtype=F32) / MLA_ROPE))
    am = pos * inv_m[None, :]
    z32, z64 = jnp.zeros((s, 32), F32), jnp.zeros((s, 64), F32)
    mla_cos = jnp.concatenate([jnp.cos(am), jnp.cos(am), z64], axis=1)
    mla_sp = jnp.concatenate([z32, jnp.sin(am), z64], axis=1)
    mla_sn = jnp.concatenate([-jnp.sin(am), z32, z64], axis=1)
    return ret_cos, ret_sin, mla_cos, mla_sp, mla_sn


def _rope128(x, c, sg):
    return x * c + pltpu.roll(x, 64, 1) * sg


def _unrope128(d, c, sg):
    return d * c + pltpu.roll(d * sg, 64, 1)


def _rope64(t, c, sp, sn):
    return t * c + pltpu.roll(t, 96, 1) * sn + pltpu.roll(t, 32, 1) * sp


def _unrope64(d, c, sp, sn):
    return d * c + pltpu.roll(d * sn, 32, 1) + pltpu.roll(d * sp, 96, 1)


def _ret_pre(z, cos, sin, *, name, tm=ROW_TILE):
    s = z.shape[0]
    tm = min(tm, s)
    scale = RET_HD ** -0.5

    def body(q_ref, k_ref, c_ref, s_ref, qo_ref, ko_ref):
        c, sg = c_ref[...], s_ref[...]
        for h in range(RET_HEADS):
            sl = slice(h * RET_HD, (h + 1) * RET_HD)
            qo_ref[:, sl] = _rope128(q_ref[:, sl], c, sg)
            ko_ref[:, sl] = _rope128(k_ref[:, sl], c, sg) * scale

    seg = lambda j: pl.BlockSpec((tm, GROUP_W), lambda i: (i, j))
    tab = pl.BlockSpec((tm, RET_HD), lambda i: (i, 0))
    return pl.pallas_call(
        body, name=name, grid=(s // tm,), in_specs=[seg(0), seg(1), tab, tab],
        out_specs=[seg(0), seg(0)],
        out_shape=[jax.ShapeDtypeStruct((s, GROUP_W), F32)] * 2,
        compiler_params=_cparams("parallel"),
    )(z, z, cos, sin)


def _ret_pre_bwd(dqr, dkr, cos, sin, into, *, name, tm=ROW_TILE):
    s = dqr[0].shape[0]
    tm = min(tm, s)
    scale = RET_HD ** -0.5

    def body(dq0_ref, dq1_ref, dk0_ref, dk1_ref, c_ref, s_ref, _, o_ref):
        c, sg = c_ref[...], s_ref[...]
        for h in range(RET_HEADS):
            sl = slice(h * RET_HD, (h + 1) * RET_HD)
            ksl = slice(GROUP_W + h * RET_HD, GROUP_W + (h + 1) * RET_HD)
            o_ref[:, sl] = _bf(_unrope128(dq0_ref[:, sl] + dq1_ref[:, sl], c, sg))
            o_ref[:, ksl] = _bf(_unrope128(dk0_ref[:, sl] + dk1_ref[:, sl], c, sg) * scale)

    row = pl.BlockSpec((tm, GROUP_W), lambda i: (i, 0))
    tab = pl.BlockSpec((tm, RET_HD), lambda i: (i, 0))
    out_shape, out_spec, more_specs, more_args = _landing(into, tm, 2 * GROUP_W)
    return pl.pallas_call(
        body, name=name, grid=(s // tm,), in_specs=[row, row, row, row, tab, tab] + more_specs, out_specs=out_spec,
        out_shape=out_shape, input_output_aliases={6: 0},
        compiler_params=_cparams("parallel"),
    )(dqr[0], dqr[1], dkr[0], dkr[1], cos, sin, *more_args)


def _bla(a, b, c, lg, cols, *, name):
    s = a.shape[0]
    ch = min(RET_CHUNK, s)
    n = s // ch
    hd = RET_HD

    def body(lg_ref, a0, b0, c0, a1, b1, c1, o0, o1, st):
        t = pl.program_id(0)

        @pl.when(t == 0)
        def _():
            st[...] = jnp.zeros_like(st)

        ii = lax.broadcasted_iota(jnp.int32, (ch, ch), 0)
        jj = lax.broadcasted_iota(jnp.int32, (ch, ch), 1)
        idx = lax.broadcasted_iota(jnp.int32, (ch, 1), 0).astype(F32)
        for d, (a_ref, b_ref, c_ref, o_ref) in enumerate(((a0, b0, c0, o0), (a1, b1, c1, o1))):
            diff = ((ii - jj) if d == 0 else (jj - ii)).astype(F32)
            keep = diff >= 0
            dpos = jnp.maximum(diff, 0.0)
            pq = (idx + 1.0) if d == 0 else (ch - idx)
            pk = (ch - 1.0 - idx) if d == 0 else idx
            for h in range(RET_HEADS):
                g = lg_ref[d, h]
                sl = slice(h * hd, (h + 1) * hd)
                av, bv, cv = a_ref[:, sl], b_ref[:, sl], c_ref[:, sl]
                sc = _dot(av, bv, 1, 1) * jnp.where(keep, jnp.exp(dpos * g), 0.0)
                stv = st[d, h]
                o_ref[:, sl] = _dot(sc, cv) + _dot(av * jnp.exp(pq * g), stv)
                st[d, h] = jnp.exp(ch * g) * stv + _dot(bv * jnp.exp(pk * g), cv, 0, 0)

    fwd = lambda j: pl.BlockSpec((ch, GROUP_W), lambda t: (t, j))
    bwd = lambda j: pl.BlockSpec((ch, GROUP_W), lambda t: (n - 1 - t, j))
    return pl.pallas_call(
        body, name=name, grid=(n,),
        in_specs=[pl.BlockSpec(memory_space=pltpu.SMEM), fwd(cols[0]), fwd(cols[1]), fwd(cols[2]),
                  bwd(cols[0]), bwd(cols[1]), bwd(cols[2])],
        out_specs=[fwd(0), bwd(0)],
        out_shape=[jax.ShapeDtypeStruct((s, GROUP_W), F32)] * 2,
        scratch_shapes=[pltpu.VMEM((2, RET_HEADS, hd, hd), F32)],
        compiler_params=_cparams("arbitrary"),
    )(lg, a, b, c, a, b, c)


def _post(os_, zg, gcol, g, *, norm, name, tm=ROW_TILE):
    s = zg.shape[0]
    tm = min(tm, s)
    nd = len(os_)

    def body(*refs):
        o_refs, (gt_ref, g_ref, y_ref) = refs[:nd], refs[nd:]
        silu, _ = _silu_parts(gt_ref[...])
        for h in range(4):
            sl = slice(h * 128, (h + 1) * 128)
            o = o_refs[0][:, sl]
            for k in range(1, nd):
                o = o + o_refs[k][:, sl]
            if norm:
                r = lax.rsqrt(jnp.mean(o * o, axis=-1, keepdims=True) + EPS)
                o = o * r * g_ref[:, sl]
            y_ref[:, sl] = _bf(silu[:, sl] * o)

    row = pl.BlockSpec((tm, GROUP_W), lambda i: (i, 0))
    return pl.pallas_call(
        body, name=name, grid=(s // tm,),
        in_specs=[row] * nd + [pl.BlockSpec((tm, GROUP_W), lambda i: (i, gcol)),
                               pl.BlockSpec((1, GROUP_W), lambda i: (0, 0))],
        out_specs=row,
        out_shape=jax.ShapeDtypeStruct((s, GROUP_W), BF16),
        compiler_params=_cparams("parallel"),
    )(*os_, zg, g)


def _post_bwd(dy, ycol, os_, zg, gcol, g, into, *, norm, name, tm=ROW_TILE):
    s = zg.shape[0]
    tm = min(tm, s)
    nd = len(os_)

    def body(*refs):
        dy_ref, o_refs = refs[0], refs[1:1 + nd]
        gt_ref, g_ref, _, dgt_ref, do_ref, dg_ref = refs[1 + nd:]
        i = pl.program_id(0)
        silu, dsilu = _silu_parts(gt_ref[...])
        dyv = dy_ref[...]
        parts = []
        for h in range(4):
            sl = slice(h * 128, (h + 1) * 128)
            o = o_refs[0][:, sl]
            for k in range(1, nd):
                o = o + o_refs[k][:, sl]
            dn = dyv[:, sl] * silu[:, sl]
            if norm:
                r = lax.rsqrt(jnp.mean(o * o, axis=-1, keepdims=True) + EPS)
                xn = o * r
                gh = g_ref[:, sl]
                dgt_ref[:, sl] = _bf(dyv[:, sl] * (xn * gh) * dsilu[:, sl])
                parts.append(jnp.sum(dn * xn, axis=0, keepdims=True))
                dxn = dn * gh
                do_ref[:, sl] = r * (dxn - xn * jnp.mean(dxn * xn, axis=-1, keepdims=True))
            else:
                dgt_ref[:, sl] = _bf(dyv[:, sl] * o * dsilu[:, sl])
                parts.append(jnp.zeros((1, 128), F32))
                do_ref[:, sl] = dn
        part = jnp.concatenate(parts, axis=1)

        @pl.when(i == 0)
        def _():
            dg_ref[...] = part

        @pl.when(i > 0)
        def _():
            dg_ref[...] += part

    row = pl.BlockSpec((tm, GROUP_W), lambda i: (i, 0))
    vec = pl.BlockSpec((1, GROUP_W), lambda i: (0, 0))
    dgt_shape, dgt_spec, more_specs, more_args = _landing(into, tm, GROUP_W)
    n_in = nd + 3
    return pl.pallas_call(
        body, name=name, grid=(s // tm,),
        in_specs=[pl.BlockSpec((tm, GROUP_W), lambda i: (i, ycol))] + [row] * nd
        + [pl.BlockSpec((tm, GROUP_W), lambda i: (i, gcol)), vec] + more_specs,
        out_specs=[dgt_spec, row, vec],
        out_shape=[dgt_shape, jax.ShapeDtypeStruct((s, GROUP_W), F32), jax.ShapeDtypeStruct((1, GROUP_W), F32)],
        input_output_aliases={n_in: 0},
        compiler_params=_cparams("arbitrary"),
    )(dy, *os_, zg, g, *more_args)


def _ret_log_gamma(swap):
    gf = 1.0 - 2.0 ** (-5.0 - jnp.arange(RET_HEADS, dtype=F32))
    lf, lb = jnp.log(gf), jnp.log(gf[::-1])
    return jnp.stack([lb, lf] if swap else [lf, lb])


def _log_sigmoid(x):
    return jnp.minimum(x, 0.0) - jnp.log(1.0 + jnp.exp(-jnp.abs(x)))


def _gla_gate(z, wa, ba, *, name, tm=ROW_TILE):
    s = z.shape[0]
    tm = min(tm, s)
    col = SEG["ga"][0] // 128

    def body(ga_ref, wa_ref, ba_ref, la_ref):
        pre = _dot(ga_ref[...], wa_ref[...]) + ba_ref[...]
        la_ref[...] = _log_sigmoid(pre) / GLA_TAU

    return pl.pallas_call(
        body, name=name, grid=(s // tm,),
        in_specs=[pl.BlockSpec((tm, 128), lambda i: (i, col)), pl.BlockSpec((128, 512), lambda i: (0, 0)),
                  pl.BlockSpec((1, 512), lambda i: (0, 0))],
        out_specs=pl.BlockSpec((tm, 512), lambda i: (i, 0)),
        out_shape=jax.ShapeDtypeStruct((s, 512), F32),
        compiler_params=_cparams("parallel"),
    )(z, wa, ba)


def _gla_gate_bwd(dla, z, wa, ba, into, *, name, tm=ROW_TILE):
    s = z.shape[0]
    tm = min(tm, s)
    col = SEG["ga"][0] // 128

    def body(dla_ref, ga_ref, wa_ref, ba_ref, _, dga_ref, dwa_ref, dba_ref):
        i = pl.program_id(0)
        gav = ga_ref[...]
        pre = _dot(gav, wa_ref[...]) + ba_ref[...]
        dpre = dla_ref[...] * (1.0 - _sigmoid(pre)) * (1.0 / GLA_TAU)
        dga_ref[...] = _bf(_dot(dpre, wa_ref[...], 1, 1))
        pw = _dot(gav, dpre, 0, 0)
        pb = jnp.sum(dpre, axis=0, keepdims=True)

        @pl.when(i == 0)
        def _():
            dwa_ref[...] = pw
            dba_ref[...] = pb

        @pl.when(i > 0)
        def _():
            dwa_ref[...] += pw
            dba_ref[...] += pb

    dga_shape, dga_spec, more_specs, more_args = _landing(into, tm, 128)
    return pl.pallas_call(
        body, name=name, grid=(s // tm,),
        in_specs=[pl.BlockSpec((tm, 512), lambda i: (i, 0)), pl.BlockSpec((tm, 128), lambda i: (i, col)),
                  pl.BlockSpec((128, 512), lambda i: (0, 0)), pl.BlockSpec((1, 512), lambda i: (0, 0))] + more_specs,
        out_specs=[dga_spec, pl.BlockSpec((128, 512), lambda i: (0, 0)), pl.BlockSpec((1, 512), lambda i: (0, 0))],
        out_shape=[dga_shape, jax.ShapeDtypeStruct((128, 512), F32), jax.ShapeDtypeStruct((1, 512), F32)],
        input_output_aliases={4: 0},
        compiler_params=_cparams("arbitrary"),
    )(dla, z, wa, ba, *more_args)


def _gla_masks(ch):
    ii = lax.broadcasted_iota(jnp.int32, (ch, ch), 0)
    tt = lax.broadcasted_iota(jnp.int32, (ch, ch), 1)
    return jnp.where(tt <= ii, 1.0, 0.0), jnp.where(tt >= ii, 1.0, 0.0)


def _running_sum(x, up):
    n = x.shape[0]
    rows = lax.broadcasted_iota(jnp.int32, x.shape, 0)
    k = 1
    while k < n:
        if up:
            x = x + jnp.where(rows < n - k, pltpu.roll(x, n - k, 0), 0.0)
        else:
            x = x + jnp.where(rows >= k, pltpu.roll(x, k, 0), 0.0)
        k *= 2
    return x


def _gla_chunk(d, tmat, qv, kv, lav, ch):
    c = _running_sum(lav, up=(d == 1))
    big_l = c[ch - 1:ch, :] if d == 0 else c[0:1, :]
    qt = qv * (GLA_DK ** -0.5) * jnp.exp(c)
    kt = kv * jnp.exp(-c)
    kh = kv * jnp.exp(big_l - c)
    return c, big_l, qt, kt, kh


def _gla_fwd(qh, kh_, z, la, *, name, rider=None):
    s = z.shape[0]
    ch = min(GLA_CHUNK, s)
    n = s // ch
    vcol = SEG["gv"][0] // GROUP_W

    def body(q0, k0, v0, la0, q1, k1, v1, la1, o0, o1, zs0, zs1, st):
        t = pl.program_id(0)

        @pl.when(t == 0)
        def _():
            st[...] = jnp.zeros_like(st)

        masks = _gla_masks(ch)
        for d, (q_ref, k_ref, v_ref, la_ref, o_ref, zs_ref) in enumerate(
                ((q0, k0, v0, la0, o0, zs0), (q1, k1, v1, la1, o1, zs1))):
            for h in range(GLA_HEADS):
                c, big_l, qt, kt, kh = _gla_chunk(d, masks[d], q_ref[h], k_ref[h], la_ref[0, h], ch)
                vv = v_ref[:, h * GLA_DV:(h + 1) * GLA_DV]
                p = _dot(qt, kt, 1, 1) * masks[d]
                zst = st[d, h]
                o_ref[:, h * GLA_DV:(h + 1) * GLA_DV] = _dot(p, vv) + _dot(qt, zst, 1, 1)
                zs_ref[h, 0] = zst
                st[d, h] = zst * jnp.exp(big_l) + _dot(vv, kh, 0, 0)

    cidx = (lambda t: t), (lambda t: n - 1 - t)
    hs = lambda d: pl.BlockSpec((GLA_HEADS, ch, GLA_DK), lambda t: (0, cidx[d](t), 0))
    vs = lambda d: pl.BlockSpec((ch, GROUP_W), lambda t: (cidx[d](t), vcol))
    las = lambda d: pl.BlockSpec((1, GLA_HEADS, ch, GLA_DK), lambda t: (d, 0, cidx[d](t), 0))
    os_ = lambda d: pl.BlockSpec((ch, GROUP_W), lambda t: (cidx[d](t), 0))
    zss = lambda d: pl.BlockSpec((GLA_HEADS, 1, GLA_DV, GLA_DK), lambda t: (0, cidx[d](t), 0, 0))
    (o0, o1, zs0, zs1), rode = _ride_call(
        body, rider, name=name, grid=(n,),
        in_specs=[hs(0), hs(0), vs(0), las(0), hs(1), hs(1), vs(1), las(1)],
        out_specs=[os_(0), os_(1), zss(0), zss(1)],
        out_shape=[jax.ShapeDtypeStruct((s, GROUP_W), F32)] * 2
        + [jax.ShapeDtypeStruct((GLA_HEADS, n, GLA_DV, GLA_DK), F32)] * 2,
        scratch_shapes=[pltpu.VMEM((2, GLA_HEADS, GLA_DV, GLA_DK), F32)],
        args=(qh, kh_, z, la, qh, kh_, z, la), sem=("arbitrary",))
    return ((o0, o1), (zs0, zs1)) if rider is None else ((o0, o1), (zs0, zs1), rode)


def _gla_bwd(qh, kh_, z, la, do, zs, *, name):
    s = z.shape[0]
    ch = min(GLA_CHUNK, s)
    n = s // ch
    vcol = SEG["gv"][0] // GROUP_W

    def body(q0, k0, v0, la0, do0, zs0, q1, k1, v1, la1, do1, zs1,
             dq0, dk0, dla0, dv0, dq1, dk1, dla1, dv1, gz):
        t = pl.program_id(0)

        @pl.when(t == 0)
        def _():
            gz[...] = jnp.zeros_like(gz)

        masks = _gla_masks(ch)
        rows = lax.broadcasted_iota(jnp.int32, (ch, 1), 0)
        for d, (q_ref, k_ref, v_ref, la_ref, do_ref, zs_ref, dq_ref, dk_ref, dla_ref, dv_ref) in enumerate(
                ((q0, k0, v0, la0, do0, zs0, dq0, dk0, dla0, dv0), (q1, k1, v1, la1, do1, zs1, dq1, dk1, dla1, dv1))):
            tmat = masks[d]
            end = ch - 1 if d == 0 else 0
            for h in range(GLA_HEADS):
                c, big_l, qt, kt, kh = _gla_chunk(d, tmat, q_ref[h], k_ref[h], la_ref[0, h], ch)
                vsl = slice(h * GLA_DV, (h + 1) * GLA_DV)
                vv, dov, zst, gzv = v_ref[:, vsl], do_ref[:, vsl], zs_ref[h, 0], gz[d, h]
                p = _dot(qt, kt, 1, 1) * tmat
                dp = _dot(dov, vv, 1, 1) * tmat
                dqt = _dot(dp, kt) + _dot(dov, zst)
                dkt = _dot(dp, qt, 0, 0)
                dkh = _dot(vv, gzv)
                dv_ref[:, vsl] = _dot(p, dov, 0, 0) + _dot(kh, gzv, 1, 1)
                dq_ref[h] = dqt * jnp.exp(c) * (GLA_DK ** -0.5)
                dk_ref[h] = dkt * jnp.exp(-c) + dkh * jnp.exp(big_l - c)
                e_l = jnp.exp(big_l)
                d_l = jnp.sum(dkh * kh, axis=0, keepdims=True) + e_l * jnp.sum(zst * gzv, axis=0, keepdims=True)
                dc = dqt * qt - dkt * kt - dkh * kh + jnp.where(rows == end, d_l, 0.0)
                dla_ref[h] = _running_sum(dc, up=(d == 0))
                gz[d, h] = gzv * e_l + _dot(dov, qt, 0, 0)

    cidx = (lambda t: n - 1 - t), (lambda t: t)
    hs = lambda d: pl.BlockSpec((GLA_HEADS, ch, GLA_DK), lambda t: (0, cidx[d](t), 0))
    vs = lambda d: pl.BlockSpec((ch, GROUP_W), lambda t: (cidx[d](t), vcol))
    las = lambda d: pl.BlockSpec((1, GLA_HEADS, ch, GLA_DK), lambda t: (d, 0, cidx[d](t), 0))
    row = lambda d: pl.BlockSpec((ch, GROUP_W), lambda t: (cidx[d](t), 0))
    zss = lambda d: pl.BlockSpec((GLA_HEADS, 1, GLA_DV, GLA_DK), lambda t: (0, cidx[d](t), 0, 0))
    hshape = jax.ShapeDtypeStruct((GLA_HEADS, s, GLA_DK), F32)
    wide = jax.ShapeDtypeStruct((s, GROUP_W), F32)
    outs = pl.pallas_call(
        body, name=name, grid=(n,),
        in_specs=[hs(0), hs(0), vs(0), las(0), row(0), zss(0), hs(1), hs(1), vs(1), las(1), row(1), zss(1)],
        out_specs=[hs(0), hs(0), hs(0), row(0), hs(1), hs(1), hs(1), row(1)],
        out_shape=[hshape, hshape, hshape, wide, hshape, hshape, hshape, wide],
        scratch_shapes=[pltpu.VMEM((2, GLA_HEADS, GLA_DV, GLA_DK), F32)],
        compiler_params=_cparams("arbitrary"),
    )(qh, kh_, z, la, do, zs[0], qh, kh_, z, la, do, zs[1])
    dq0, dk0, dla0, dv0, dq1, dk1, dla1, dv1 = outs
    return (dq0, dq1), (dk0, dk1), (dla0, dla1), (dv0, dv1)


def _window_sums(win, g, shift):
    n = win.shape[0]
    levels, y = [], win
    for j in range(POOL_GROUPS):
        y = y + pltpu.roll(y, n - (1 << j), 0)
        levels.append(y)
    sums = levels[-1]
    for j in range(POOL_GROUPS - 2, -1, -1):
        sums = jnp.where(g == j, levels[j], sums)
    return pltpu.roll(sums, shift, 0)


def _pool_cnt(t0, half, rows, s):
    t = t0 + lax.broadcasted_iota(jnp.int32, (rows, 1), 0)
    return (jnp.minimum(t + half, s) - jnp.maximum(t - half, 0)).astype(F32)


def _pool_fwd(z, pw, scale, *, name):
    s = z.shape[0]
    tl = min(POOL_TILE, s)
    nt = s // tl
    ucol, gcol = SEG["pv"][0] // 128, SEG["pg"][0] // 128

    def body(u_ref, gt_ref, pw_ref, sc_ref, y_ref, pad):
        g = pl.program_id(0)
        half = jnp.left_shift(1, g)
        pad[0:POOL_HALO, :] = jnp.zeros((POOL_HALO, POOL_GW), F32)
        pad[POOL_HALO + s:POOL_HALO + s + POOL_HALO, :] = jnp.zeros((POOL_HALO, POOL_GW), F32)
        pad[POOL_HALO:POOL_HALO + s, :] = u_ref[...]
        pwv, scv = pw_ref[0], sc_ref[...]

        def tile(i, carry):
            t0 = pl.multiple_of(i * tl, tl)
            win = pad[pl.ds(t0, tl + 2 * POOL_HALO), :]
            u = win[POOL_HALO:POOL_HALO + tl, :]
            pooled = _window_sums(win, g, half)[POOL_HALO:POOL_HALO + tl, :] / _pool_cnt(t0, half, tl, s) - u
            mixed = _dot(pooled, pwv)
            silu, _ = _silu_parts(gt_ref[pl.ds(t0, tl), :])
            y_ref[pl.ds(t0, tl), :] = _bf(silu * (mixed * scv))
            return carry

        lax.fori_loop(0, nt, tile, 0)

    return pl.pallas_call(
        body, name=name, grid=(POOL_GROUPS,),
        in_specs=[pl.BlockSpec((s, POOL_GW), lambda g: (0, ucol + g)),
                  pl.BlockSpec((s, POOL_GW), lambda g: (0, gcol + g)),
                  pl.BlockSpec((1, POOL_GW, POOL_GW), lambda g: (g, 0, 0)),
                  pl.BlockSpec((1, POOL_GW), lambda g: (0, g))],
        out_specs=pl.BlockSpec((s, POOL_GW), lambda g: (0, g)),
        out_shape=jax.ShapeDtypeStruct((s, GROUP_W), BF16),
        scratch_shapes=[pltpu.VMEM((s + 2 * POOL_HALO, POOL_GW), F32)],
        compiler_params=_cparams("parallel"),
    )(z, z, pw, scale)


def _pool_bwd(dy, z, pw, scale, *, name):
    s = z.shape[0]
    tl = min(POOL_TILE, s)
    nt = s // tl
    ucol, gcol, ycol = SEG["pv"][0] // 128, SEG["pg"][0] // 128, 2 * GROUP_W // 128

    def body(dy_ref, u_ref, gt_ref, pw_ref, sc_ref, du_ref, dgt_ref, dpw_ref, dsc_ref, pad, epad, dpo):
        g = pl.program_id(0)
        half = jnp.left_shift(1, g)
        zeros = jnp.zeros((POOL_HALO, POOL_GW), F32)
        for buf in (pad, epad):
            buf[0:POOL_HALO, :] = zeros
            buf[POOL_HALO + s:POOL_HALO + s + POOL_HALO, :] = zeros
        pad[POOL_HALO:POOL_HALO + s, :] = u_ref[...]
        pwv, scv = pw_ref[0], sc_ref[...]
        dpw_ref[0] = jnp.zeros((POOL_GW, POOL_GW), F32)
        dsc_ref[...] = jnp.zeros((1, POOL_GW), F32)

        def tile(i, carry):
            t0 = pl.multiple_of(i * tl, tl)
            win = pad[pl.ds(t0, tl + 2 * POOL_HALO), :]
            u = win[POOL_HALO:POOL_HALO + tl, :]
            cnt = _pool_cnt(t0, half, tl, s)
            pooled = _window_sums(win, g, half)[POOL_HALO:POOL_HALO + tl, :] / cnt - u
            mixed = _dot(pooled, pwv)
            silu, dsilu = _silu_parts(gt_ref[pl.ds(t0, tl), :])
            dyv = dy_ref[pl.ds(t0, tl), :]
            dgt_ref[pl.ds(t0, tl), :] = _bf(dyv * (mixed * scv) * dsilu)
            dsc_ref[...] += jnp.sum(dyv * silu * mixed, axis=0, keepdims=True)
            dm = dyv * silu * scv
            dpw_ref[0] += _dot(pooled, dm, 0, 0)
            dpooled = _dot(dm, pwv, 1, 1)
            dpo[pl.ds(t0, tl), :] = dpooled
            epad[pl.ds(POOL_HALO + t0, tl), :] = dpooled / cnt
            return carry

        lax.fori_loop(0, nt, tile, 0)

        def tile2(i, carry):
            t0 = pl.multiple_of(i * tl, tl)
            ewin = epad[pl.ds(t0, tl + 2 * POOL_HALO), :]
            du_ref[pl.ds(t0, tl), :] = _bf(_window_sums(ewin, g, half - 1)[POOL_HALO:POOL_HALO + tl, :]
                                           - dpo[pl.ds(t0, tl), :])
            return carry

        lax.fori_loop(0, nt, tile2, 0)

    col = lambda c0: pl.BlockSpec((s, POOL_GW), lambda g: (0, c0 + g))
    return pl.pallas_call(
        body, name=name, grid=(POOL_GROUPS,),
        in_specs=[col(ycol), col(ucol), col(gcol), pl.BlockSpec((1, POOL_GW, POOL_GW), lambda g: (g, 0, 0)),
                  pl.BlockSpec((1, POOL_GW), lambda g: (0, g))],
        out_specs=[col(0), col(0), pl.BlockSpec((1, POOL_GW, POOL_GW), lambda g: (g, 0, 0)),
                   pl.BlockSpec((1, POOL_GW), lambda g: (0, g))],
        out_shape=[jax.ShapeDtypeStruct((s, GROUP_W), BF16), jax.ShapeDtypeStruct((s, GROUP_W), BF16),
                   jax.ShapeDtypeStruct((POOL_GROUPS, POOL_GW, POOL_GW), F32),
                   jax.ShapeDtypeStruct((1, GROUP_W), F32)],
        scratch_shapes=[pltpu.VMEM((s + 2 * POOL_HALO, POOL_GW), F32), pltpu.VMEM((s + 2 * POOL_HALO, POOL_GW), F32),
                        pltpu.VMEM((s, POOL_GW), F32)],
        compiler_params=_cparams("parallel"),
    )(dy, z, z, pw, scale)


def _mla_specs(tm):
    zq = pl.BlockSpec((tm, 512), lambda i: (i, SEG["mq"][0] // 512))
    zkv = pl.BlockSpec((tm, 256), lambda i: (i, SEG["mkv"][0] // 256))
    zkr = pl.BlockSpec((tm, 128), lambda i: (i, SEG["mkr"][0] // 128))
    full = lambda r, c: pl.BlockSpec((r, c), lambda i: (0, 0))
    tab = pl.BlockSpec((tm, 128), lambda i: (i, 0))
    weights = [full(1, 512), full(512, 1024), full(1, 256), full(256, 1024), full(1, 256), full(1, 256)]
    return [zq, zkv, zkr] + weights + [tab, tab, tab]


def _mla_project(xq_ref, xkv_ref, qg_ref, wq_ref, kvg_ref, wkv_ref):
    xq = xq_ref[...]
    r1 = lax.rsqrt(jnp.mean(xq * xq, axis=-1, keepdims=True) + EPS)
    xn1 = xq * r1
    qn = _bf(xn1 * qg_ref[...])
    qraw = _dot(qn, wq_ref[...])
    xkv = xkv_ref[...]
    r2 = lax.rsqrt(jnp.mean(xkv * xkv, axis=-1, keepdims=True) + EPS)
    xn2 = xkv * r2
    kvn = _bf(xn2 * kvg_ref[...])
    kvraw = _dot(kvn, wkv_ref[...])
    return r1, xn1, qn, qraw, r2, xn2, kvn, kvraw


def _mla_pre(z, qg, wq, kvg, wkv, qng, kng, cos, sp, sn, *, name, tm=ROW_TILE):
    s = z.shape[0]
    tm = min(tm, s)

    def body(xq_ref, xkv_ref, pe_ref, qg_ref, wq_ref, kvg_ref, wkv_ref, qng_ref, kng_ref, c_ref, sp_ref, sn_ref,
             q_ref, k_ref, v_ref):
        _, _, _, qraw, _, _, _, kvraw = _mla_project(xq_ref, xkv_ref, qg_ref, wq_ref, kvg_ref, wkv_ref)
        c, spv, snv = c_ref[...], sp_ref[...], sn_ref[...]
        pe = pe_ref[...]
        pe_ss = jnp.sum(pe * pe, axis=-1, keepdims=True)
        qngv, kngv = qng_ref[...], kng_ref[...]
        for h in range(MLA_HEADS):
            b = h * MLA_QKP
            qh = qraw[:, b:b + MLA_QKP]
            r = lax.rsqrt(jnp.sum(qh * qh, axis=-1, keepdims=True) * (1.0 / MLA_QK) + EPS)
            qn_h = qh * r * qngv
            q_ref[:, b:b + 128] = _bf(qn_h[:, :128] * MLA_SCALE)
            q_ref[:, b + 128:b + 256] = _bf(_rope64(qn_h[:, 128:], c, spv, snv) * MLA_SCALE)
            kn = kvraw[:, b:b + 128]
            rk = lax.rsqrt((jnp.sum(kn * kn, axis=-1, keepdims=True) + pe_ss) * (1.0 / MLA_QK) + EPS)
            k_ref[:, b:b + 128] = _bf(kn * rk * kngv[:, :128])
            k_ref[:, b + 128:b + 256] = _bf(_rope64(pe * rk * kngv[:, 128:], c, spv, snv))
            v_ref[:, h * MLA_V:(h + 1) * MLA_V] = _bf(kvraw[:, b + 128:b + 256])

    row = lambda w: pl.BlockSpec((tm, w), lambda i: (i, 0))
    return pl.pallas_call(
        body, name=name, grid=(s // tm,), in_specs=_mla_specs(tm),
        out_specs=[row(1024), row(1024), row(512)],
        out_shape=[jax.ShapeDtypeStruct((s, 1024), BF16), jax.ShapeDtypeStruct((s, 1024), BF16),
                   jax.ShapeDtypeStruct((s, 512), BF16)],
        compiler_params=_cparams("parallel"),
    )(z, z, z, qg, wq, kvg, wkv, qng, kng, cos, sp, sn)


def _mla_pre_bwd(dq, dk, dv, z, qg, wq, kvg, wkv, qng, kng, cos, sp, sn, *, name, tm=ROW_TILE):
    s = z.shape[0]
    tm = min(tm, s)

    def body(dq_ref, dk_ref, dv_ref, xq_ref, xkv_ref, pe_ref, qg_ref, wq_ref, kvg_ref, wkv_ref, qng_ref, kng_ref,
             c_ref, sp_ref, sn_ref, dxq_ref, dxkv_ref, dpe_ref, dwq_ref, dwkv_ref, dqg_ref, dkvg_ref, dqng_ref,
             dkng_ref, dqraw, dkvraw):
        i = pl.program_id(0)
        r1, xn1, qn, qraw, r2, xn2, kvn, kvraw = _mla_project(xq_ref, xkv_ref, qg_ref, wq_ref, kvg_ref, wkv_ref)
        c, spv, snv = c_ref[...], sp_ref[...], sn_ref[...]
        pe = pe_ref[...]
        pe_ss = jnp.sum(pe * pe, axis=-1, keepdims=True)
        qngv, kngv = qng_ref[...], kng_ref[...]
        dqng = jnp.zeros((1, MLA_QKP), F32)
        dkng = jnp.zeros((1, MLA_QKP), F32)
        dpe = jnp.zeros_like(pe)
        for h in range(MLA_HEADS):
            b = h * MLA_QKP
            qh = qraw[:, b:b + MLA_QKP]
            r = lax.rsqrt(jnp.sum(qh * qh, axis=-1, keepdims=True) * (1.0 / MLA_QK) + EPS)
            xn = qh * r
            d_n = jnp.concatenate(
                [dq_ref[:, b:b + 128], _unrope64(dq_ref[:, b + 128:b + 256], c, spv, snv)], axis=1) * MLA_SCALE
            dqng = dqng + jnp.sum(d_n * xn, axis=0, keepdims=True)
            dxn = d_n * qngv
            dqraw[:, b:b + MLA_QKP] = _bf(r * (dxn - xn * (jnp.sum(dxn * xn, axis=-1, keepdims=True) * (1.0 / MLA_QK))))
            kn = kvraw[:, b:b + 128]
            rk = lax.rsqrt((jnp.sum(kn * kn, axis=-1, keepdims=True) + pe_ss) * (1.0 / MLA_QK) + EPS)
            xk = jnp.concatenate([kn, pe], axis=1) * rk
            d_k = jnp.concatenate(
                [dk_ref[:, b:b + 128], _unrope64(dk_ref[:, b + 128:b + 256], c, spv, snv)], axis=1)
            dkng = dkng + jnp.sum(d_k * xk, axis=0, keepdims=True)
            dxk = d_k * kngv
            dfull = rk * (dxk - xk * (jnp.sum(dxk * xk, axis=-1, keepdims=True) * (1.0 / MLA_QK)))
            dkvraw[:, b:b + 128] = _bf(dfull[:, :128])
            dkvraw[:, b + 128:b + 256] = _bf(dv_ref[:, h * MLA_V:(h + 1) * MLA_V])
            dpe = dpe + dfull[:, 128:]
        dpe_ref[...] = _bf(dpe)
        dqr, dkvr = dqraw[...], dkvraw[...]
        dqn = _dot(dqr, wq_ref[...], 1, 1)
        dxn1 = dqn * qg_ref[...]
        dxq_ref[...] = _bf(r1 * (dxn1 - xn1 * jnp.mean(dxn1 * xn1, axis=-1, keepdims=True)))
        dkvn = _dot(dkvr, wkv_ref[...], 1, 1)
        dxn2 = dkvn * kvg_ref[...]
        dxkv_ref[...] = _bf(r2 * (dxn2 - xn2 * jnp.mean(dxn2 * xn2, axis=-1, keepdims=True)))
        parts = (_dot(qn, dqr, 0, 0), _dot(kvn, dkvr, 0, 0), jnp.sum(dqn * xn1, axis=0, keepdims=True),
                 jnp.sum(dkvn * xn2, axis=0, keepdims=True), dqng, dkng)
        accs = (dwq_ref, dwkv_ref, dqg_ref, dkvg_ref, dqng_ref, dkng_ref)

        @pl.when(i == 0)
        def _():
            for a, p in zip(accs, parts):
                a[...] = p

        @pl.when(i > 0)
        def _():
            for a, p in zip(accs, parts):
                a[...] += p

    row = lambda w: pl.BlockSpec((tm, w), lambda i: (i, 0))
    full = lambda r, c: pl.BlockSpec((r, c), lambda i: (0, 0))
    return pl.pallas_call(
        body, name=name, grid=(s // tm,),
        in_specs=[row(1024), row(1024), row(512)] + _mla_specs(tm),
        out_specs=[row(512), row(256), row(128), full(512, 1024), full(256, 1024), full(1, 512), full(1, 256),
                   full(1, 256), full(1, 256)],
        out_shape=[jax.ShapeDtypeStruct((s, 512), BF16), jax.ShapeDtypeStruct((s, 256), BF16),
                   jax.ShapeDtypeStruct((s, 128), BF16), jax.ShapeDtypeStruct((512, 1024), F32),
                   jax.ShapeDtypeStruct((256, 1024), F32), jax.ShapeDtypeStruct((1, 512), F32),
                   jax.ShapeDtypeStruct((1, 256), F32), jax.ShapeDtypeStruct((1, 256), F32),
                   jax.ShapeDtypeStruct((1, 256), F32)],
        scratch_shapes=[pltpu.VMEM((tm, 1024), BF16), pltpu.VMEM((tm, 1024), BF16)],
        compiler_params=_cparams("arbitrary"),
    )(dq, dk, dv, z, z, z, qg, wq, kvg, wkv, qng, kng, cos, sp, sn)


def _flash_fwd(q, k, v, *, name, tq=1024, tk=1024, rider=None):
    s = q.shape[0]
    tq, tk = min(tq, s), min(tk, s)
    nk = s // tk
    strip = min(FLASH_STRIP, tq)

    def body(q_ref, k_ref, v_ref, o_ref, lse_ref, m_s, l_s, acc):
        j = pl.program_id(2)

        @pl.when(j == 0)
        def _():
            m_s[...] = jnp.full_like(m_s, -jnp.inf)
            l_s[...] = jnp.zeros_like(l_s)
            acc[...] = jnp.zeros_like(acc)

        for r in range(tq // strip):
            rows = slice(r * strip, (r + 1) * strip)
            sc = _dot(q_ref[rows, :], k_ref[...], 1, 1)
            m_prev = m_s[rows, :]
            m_new = jnp.maximum(m_prev, jnp.max(sc, axis=-1, keepdims=True))
            p = jnp.exp(sc - m_new[:, 0:1])
            alpha = jnp.exp(m_prev - m_new)
            l_s[rows, :] = alpha * l_s[rows, :] + jnp.sum(p, axis=-1, keepdims=True)
            acc[rows, :] = alpha * acc[rows, :] + _dot(p, v_ref[...])
            m_s[rows, :] = m_new

        @pl.when(j == nk - 1)
        def _():
            o_ref[...] = acc[...] / l_s[...]
            lse_ref[...] = m_s[...] + jnp.log(l_s[...])

    (o, lse), rode = _ride_call(
        body, rider, name=name, grid=(MLA_HEADS, s // tq, nk),
        in_specs=[pl.BlockSpec((tq, MLA_QKP), lambda h, i, j: (i, h)),
                  pl.BlockSpec((tk, MLA_QKP), lambda h, i, j: (j, h)),
                  pl.BlockSpec((tk, MLA_V), lambda h, i, j: (j, h))],
        out_specs=[pl.BlockSpec((tq, MLA_V), lambda h, i, j: (i, h))] * 2,
        out_shape=[jax.ShapeDtypeStruct((s, GROUP_W), F32)] * 2,
        scratch_shapes=[pltpu.VMEM((tq, MLA_V), F32), pltpu.VMEM((tq, MLA_V), F32), pltpu.VMEM((tq, MLA_V), F32)],
        args=(q, k, v), sem=("parallel", "parallel", "arbitrary"))
    return (o, lse) if rider is None else (o, lse, rode)


def _flash_bwd(q, k, v, do, o, lse, *, name, tq=1024, tk=1024, rider=None):
    s = q.shape[0]
    tq, tk = min(tq, s), min(tk, s)
    nq, nk = s // tq, s // tk

    def body(q_ref, k_ref, v_ref, do_ref, o_ref, lse_ref, dq_ref, dk_ref, dv_ref, dk_acc, dv_acc):
        j, i = pl.program_id(1), pl.program_id(2)
        dov = do_ref[...]
        delta = jnp.sum(dov * o_ref[...], axis=-1, keepdims=True)
        p = jnp.exp(_dot(q_ref[...], k_ref[...], 1, 1) - lse_ref[:, 0:1])
        ds = p * (_dot(dov, v_ref[...], 1, 1) - delta)
        pv = _dot(p, dov, 0, 0)
        pk = _dot(ds, q_ref[...], 0, 0)
        pq = _dot(ds, k_ref[...])
        rows = pl.ds(pl.multiple_of(i * tq, tq), tq)

        @pl.when(j == 0)
        def _():
            dq_ref[rows, :] = pq

        @pl.when(j > 0)
        def _():
            dq_ref[rows, :] += pq

        @pl.when(i == 0)
        def _():
            dv_acc[...] = pv
            dk_acc[...] = pk

        @pl.when(i > 0)
        def _():
            dv_acc[...] += pv
            dk_acc[...] += pk

        @pl.when(i == nq - 1)
        def _():
            dk_ref[...] = dk_acc[...]
            dv_ref[...] = dv_acc[...]

    qb = pl.BlockSpec((tq, MLA_QKP), lambda h, j, i: (i, h))
    kb = pl.BlockSpec((tk, MLA_QKP), lambda h, j, i: (j, h))
    vb = pl.BlockSpec((tk, MLA_V), lambda h, j, i: (j, h))
    ob = pl.BlockSpec((tq, MLA_V), lambda h, j, i: (i, h))
    (dq, dk, dv), rode = _ride_call(
        body, rider, name=name, grid=(MLA_HEADS, nk, nq),
        in_specs=[qb, kb, vb, ob, ob, ob],
        out_specs=[pl.BlockSpec((s, MLA_QKP), lambda h, j, i: (0, h)), kb, vb],
        out_shape=[jax.ShapeDtypeStruct((s, MLA_HEADS * MLA_QKP), F32),
                   jax.ShapeDtypeStruct((s, MLA_HEADS * MLA_QKP), F32), jax.ShapeDtypeStruct((s, GROUP_W), F32)],
        scratch_shapes=[pltpu.VMEM((tk, MLA_QKP), F32), pltpu.VMEM((tk, MLA_V), F32)],
        args=(q, k, v, do, o, lse), sem=("arbitrary", "arbitrary", "arbitrary"))
    return (dq, dk, dv) if rider is None else (dq, dk, dv, rode)


def _rows_tile(r, c, itemsize=4, budget=2 * 1024 * 1024):
    if r * c * itemsize <= budget:
        return r
    best = None
    for t in range(8, r, 8):
        if r % t == 0 and t * c * itemsize <= budget:
            best = t
    return best if best is not None else r


def _landing(into, tm, width):
    buf, col = into
    assert col % width == 0
    return (jax.ShapeDtypeStruct(buf.shape, buf.dtype), pl.BlockSpec((tm, width), lambda i: (i, col // width)),
            [ANY], [buf])


def _add_n(arrs, *, out_dtype=F32, name, into=None):
    shape = arrs[0].shape
    c = shape[-1]
    flat = [a.reshape(-1, c) for a in arrs]
    r = flat[0].shape[0]
    t = _rows_tile(r, c)
    n_in = len(flat)

    def body(*refs):
        acc = refs[0][...].astype(F32)
        for ref in refs[1:n_in]:
            acc = acc + ref[...].astype(F32)
        refs[-1][...] = acc.astype(out_dtype)

    blk = pl.BlockSpec((t, c), lambda i: (i, 0))
    if into is not None:
        out_shape, out_spec, more_specs, more_args = _landing(into, t, c)
        return pl.pallas_call(
            body, name=name, grid=(r // t,), in_specs=[blk] * n_in + more_specs, out_specs=out_spec,
            out_shape=out_shape, input_output_aliases={n_in: 0}, compiler_params=_cparams("parallel"),
        )(*flat, *more_args)
    out = pl.pallas_call(
        body, name=name, grid=(r // t,), in_specs=[blk] * n_in, out_specs=blk,
        out_shape=jax.ShapeDtypeStruct((r, c), out_dtype), compiler_params=_cparams("parallel"),
    )(*flat)
    return out.reshape(shape)


def _adamw(w, g, m, v, *, name):
    shape = w.shape
    c = shape[-1]
    flat = [a.reshape(-1, c) for a in (w, g, m, v)]
    r = flat[0].shape[0]
    t = _rows_tile(r, c, budget=1024 * 1024)

    def body(w_ref, g_ref, m_ref, v_ref, d_ref, mo_ref, vo_ref):
        gv = g_ref[...]
        m2 = ADAM_B1 * m_ref[...] + (1.0 - ADAM_B1) * gv
        v2 = ADAM_B2 * v_ref[...] + (1.0 - ADAM_B2) * (gv * gv)
        m_hat = m2 / (1.0 - ADAM_B1 ** ADAM_STEP)
        v_hat = v2 / (1.0 - ADAM_B2 ** ADAM_STEP)
        d_ref[...] = -ADAM_LR * (m_hat / (jnp.sqrt(v_hat) + ADAM_EPS) + ADAM_WD * w_ref[...])
        mo_ref[...] = m2
        vo_ref[...] = v2

    blk = pl.BlockSpec((t, c), lambda i: (i, 0))
    outs = pl.pallas_call(
        body, name=name, grid=(r // t,), in_specs=[blk] * 4, out_specs=[blk] * 3,
        out_shape=[jax.ShapeDtypeStruct((r, c), F32)] * 3, compiler_params=_cparams("parallel"),
    )(*flat)
    return tuple(o.reshape(shape) for o in outs)


def _place():
    x, y, c = lax.axis_index("x"), lax.axis_index("y"), lax.axis_index("c")
    chips = [(1 - x, y), (x, 1 - y), (1 - x, 1 - y)]
    return x, y, c, chips


ANY = pl.BlockSpec(memory_space=pl.ANY)


def _half(ref, axis, hc, lead=()):
    n = ref.shape[len(lead) + axis] // 2
    return ref.at[tuple(lead) + (slice(None),) * axis + (pl.ds(hc * n, n),)]


def _gather_shards(shards, axes, *, name):
    nt = len(shards)

    def body(*refs):
        src, dst = refs[:nt], refs[nt:2 * nt]
        send, recv, fsend, frecv, lsem = refs[2 * nt:]
        x, y, c, chips = _place()
        me = 2 * x + y
        local = [pltpu.make_async_copy(src[t], dst[t].at[me], lsem.at[t]) for t in range(nt)]
        for cp in local:
            cp.start()

        def half(t, slot, hc):
            return _half(dst[t], axes[t], hc, lead=(slot,))

        def first(t, k):
            return pltpu.make_async_remote_copy(
                src_ref=_half(src[t], axes[t], c), dst_ref=half(t, me, c),
                send_sem=send.at[t, k], recv_sem=recv.at[t, k],
                device_id=(chips[k][0], chips[k][1], c), device_id_type=MESH)

        def landed(t, k):
            slot = 2 * chips[k][0] + chips[k][1]
            return pltpu.make_async_remote_copy(
                src_ref=half(t, slot, c), dst_ref=half(t, slot, c),
                send_sem=send.at[t, k], recv_sem=recv.at[t, k],
                device_id=(chips[k][0], chips[k][1], c), device_id_type=MESH)

        def forward(t, k, hc):
            slot = 2 * chips[k][0] + chips[k][1]
            return pltpu.make_async_remote_copy(
                src_ref=half(t, slot, hc), dst_ref=half(t, slot, hc),
                send_sem=fsend.at[t, k], recv_sem=frecv.at[t, k],
                device_id=(x, y, 1 - c), device_id_type=MESH)

        for t in range(nt):
            for k in range(3):
                first(t, k).start()
        for t in range(nt):
            for k in range(3):
                landed(t, k).wait_recv()
                forward(t, k, c).start()
        for t in range(nt):
            for k in range(3):
                forward(t, k, 1 - c).wait_recv()
        for t in range(nt):
            for k in range(3):
                first(t, k).wait_send()
                forward(t, k, c).wait_send()
        for cp in local:
            cp.wait()

    return pl.pallas_call(
        body, name=name, in_specs=[ANY] * nt, out_specs=[ANY] * nt,
        out_shape=[jax.ShapeDtypeStruct((N_CHIP,) + a.shape, a.dtype) for a in shards],
        scratch_shapes=[pltpu.SemaphoreType.DMA((nt, 3)), pltpu.SemaphoreType.DMA((nt, 3)),
                        pltpu.SemaphoreType.DMA((nt, 3)), pltpu.SemaphoreType.DMA((nt, 3)),
                        pltpu.SemaphoreType.DMA((nt,))],
    )(*shards)


def _comm_rows(hr, c, budget=2 * 1024 * 1024):
    if hr * c * 4 <= budget:
        return hr
    best = None
    for t in range(16, hr, 16):
        if hr % t == 0 and t * c * 4 <= budget:
            best = t
    return best if best is not None else hr


def _comm_cols(r, hc, budget=2 * 1024 * 1024):
    best = 128
    for t in range(128, hc + 1, 128):
        if hc % t == 0 and r * t * 4 <= budget:
            best = t
    return best


def _comm_chunks(shape, axis):
    r, cdim = shape
    if axis == 0:
        rc = _comm_rows(r // 2, cdim)
        nt = (r // 2) // rc
        return (rc, cdim), nt, (lambda h, t: (h * nt + t, 0))
    cc = _comm_cols(r, cdim // 2)
    nt = (cdim // 2) // cc
    return (r, cc), nt, (lambda h, t: (0, h * nt + t))


def _pair_reduce(g, where, axis, *, out_dtype, name):
    n_slot, r, cdim = g.shape
    blk_shape, nr, at = _comm_chunks((r, cdim), axis)
    steps = n_slot * nr
    half_shape = (r // 2, cdim) if axis == 0 else (r, cdim // 2)

    def body(w_ref, a_ref, b_ref, o_ref, land, send, recv, credit):
        x, y, c, _ = _place()
        sib = (x, y, 1 - c)
        i = pl.program_id(0) * nr + pl.program_id(1)
        s = lax.rem(i, 2)

        @pl.when(i >= 2)
        def _():
            pl.semaphore_wait(credit.at[s], 1)

        cp = pltpu.make_async_remote_copy(src_ref=b_ref.at[0], dst_ref=land.at[s], send_sem=send.at[s],
                                          recv_sem=recv.at[s], device_id=sib, device_id_type=MESH)
        cp.start()
        cp.wait_recv()
        o_ref[0] = (a_ref[0] + land[s]).astype(out_dtype)
        cp.wait_send()

        @pl.when(i + 2 < steps)
        def _():
            pl.semaphore_signal(credit.at[s], inc=1, device_id=sib, device_id_type=MESH)

    blk = lambda half: pl.BlockSpec((1,) + blk_shape, lambda j, t, w: (j,) + at(half(w), t))
    grid_spec = pltpu.PrefetchScalarGridSpec(
        num_scalar_prefetch=1, grid=(n_slot, nr),
        in_specs=[blk(lambda w: w[0]), blk(lambda w: 1 - w[0])],
        out_specs=pl.BlockSpec((1,) + blk_shape, lambda j, t, w: (j,) + at(0, t)),
        scratch_shapes=[pltpu.VMEM((2,) + blk_shape, F32), pltpu.SemaphoreType.DMA((2,)),
                        pltpu.SemaphoreType.DMA((2,)), pltpu.SemaphoreType.REGULAR((2,))])
    return pl.pallas_call(
        body, name=name, grid_spec=grid_spec, out_shape=jax.ShapeDtypeStruct((n_slot,) + half_shape, out_dtype),
        compiler_params=_cparams("arbitrary", "arbitrary"),
    )(where, g, g)


def _chip_exchange(parts, *, name):
    nt = len(parts)

    def body(*refs):
        src, got = refs[:nt], refs[nt:2 * nt]
        send, recv = refs[2 * nt:]
        x, y, c, chips = _place()
        remote = []
        for t in range(nt):
            for k in range(3):
                remote.append(pltpu.make_async_remote_copy(
                    src_ref=src[t].at[2 * chips[k][0] + chips[k][1]], dst_ref=got[t].at[k],
                    send_sem=send.at[t, k], recv_sem=recv.at[t, k],
                    device_id=(chips[k][0], chips[k][1], c), device_id_type=MESH))
        for cp in remote:
            cp.start()
        for cp in remote:
            cp.wait_recv()
        for cp in remote:
            cp.wait_send()

    return pl.pallas_call(
        body, name=name, in_specs=[ANY] * nt, out_specs=[ANY] * nt,
        out_shape=[jax.ShapeDtypeStruct((3,) + a.shape[1:], a.dtype) for a in parts],
        scratch_shapes=[pltpu.SemaphoreType.DMA((nt, 3)), pltpu.SemaphoreType.DMA((nt, 3))],
    )(*parts)


def _sum_join(p, got, where, axis, *, name):
    _, hr, cdim = p.shape
    full = (2 * hr, cdim) if axis == 0 else (hr, 2 * cdim)
    blk_shape, n, at = _comm_chunks(full, axis)
    step_len = blk_shape[axis]
    half_len = full[axis] // 2

    def body(w_ref, p_ref, g_ref, out, buf, lsem, ssem, rsem):
        x, y, c, _ = _place()
        sib = (x, y, 1 - c)
        r = pl.program_id(0)

        def part(start, size):
            return out.at[(slice(None),) * axis + (pl.ds(start, size),)]

        def copies(step, slot):
            rows = part(pl.multiple_of(c * half_len + step * step_len, 8 if axis == 0 else 128), step_len)
            return (pltpu.make_async_copy(buf.at[slot], rows, lsem.at[slot]),
                    pltpu.make_async_remote_copy(src_ref=buf.at[slot], dst_ref=rows, send_sem=ssem.at[slot],
                                                 recv_sem=rsem, device_id=sib, device_id_type=MESH))

        s = lax.rem(r, 2)

        @pl.when(r >= 2)
        def _():
            lc, rm = copies(r - 2, s)
            lc.wait()
            rm.wait_send()

        buf[s] = p_ref[0].astype(F32) + g_ref[0].astype(F32) + g_ref[1].astype(F32) + g_ref[2].astype(F32)
        lc, rm = copies(r, s)
        lc.start()
        rm.start()

        @pl.when(r == n - 1)
        def _():
            for step in range(max(0, n - 2), n):
                lc, rm = copies(step, step % 2)
                lc.wait()
                rm.wait_send()
            whole = part(0, half_len)
            pltpu.make_async_remote_copy(src_ref=whole, dst_ref=whole, send_sem=ssem.at[0], recv_sem=rsem,
                                         device_id=sib, device_id_type=MESH).wait_recv()

    grid_spec = pltpu.PrefetchScalarGridSpec(
        num_scalar_prefetch=1, grid=(n,),
        in_specs=[pl.BlockSpec((1,) + blk_shape, lambda t, w: (w[1],) + at(0, t)),
                  pl.BlockSpec((3,) + blk_shape, lambda t, w: (0,) + at(0, t))],
        out_specs=ANY,
        scratch_shapes=[pltpu.VMEM((2,) + blk_shape, F32), pltpu.SemaphoreType.DMA((2,)),
                        pltpu.SemaphoreType.DMA((2,)), pltpu.SemaphoreType.DMA])
    return pl.pallas_call(
        body, name=name, grid_spec=grid_spec, out_shape=jax.ShapeDtypeStruct(full, F32),
        compiler_params=_cparams("arbitrary"),
    )(where, p, got)


def _rider_gather_send(shards, axes):
    nt = len(shards)

    def copies(src, dst, send, recv, lsem):
        x, y, c, chips = _place()
        me = 2 * x + y
        local = [pltpu.make_async_copy(src[t], dst[t].at[me], lsem.at[t]) for t in range(nt)]
        out, landed = [], []
        for t in range(nt):
            for k in range(3):
                peer = (chips[k][0], chips[k][1], c)
                out.append(pltpu.make_async_remote_copy(
                    src_ref=_half(src[t], axes[t], c), dst_ref=_half(dst[t], axes[t], c, lead=(me,)),
                    send_sem=send.at[t, k], recv_sem=recv.at[t, k], device_id=peer, device_id_type=MESH))
                theirs = _half(dst[t], axes[t], c, lead=(2 * chips[k][0] + chips[k][1],))
                landed.append(pltpu.make_async_remote_copy(
                    src_ref=theirs, dst_ref=theirs, send_sem=send.at[t, k], recv_sem=recv.at[t, k],
                    device_id=peer, device_id_type=MESH))
        return local, out, landed

    def start(src, dst, sems):
        local, out, _ = copies(src, dst, *sems)
        for cp in local + out:
            cp.start()

    def finish(src, dst, sems):
        local, out, landed = copies(src, dst, *sems)
        for cp in landed:
            cp.wait_recv()
        for cp in out:
            cp.wait_send()
        for cp in local:
            cp.wait()

    return _Rider(shards, [jax.ShapeDtypeStruct((N_CHIP,) + a.shape, a.dtype) for a in shards],
                  [pltpu.SemaphoreType.DMA((nt, 3)), pltpu.SemaphoreType.DMA((nt, 3)), pltpu.SemaphoreType.DMA((nt,))],
                  start, finish)


def _rider_gather_forward(bufs, axes):
    nt = len(bufs)

    def copies(src, dst, send, recv):
        x, y, c, chips = _place()
        mine, theirs = [], []
        for t in range(nt):
            for k in range(3):
                slot = 2 * chips[k][0] + chips[k][1]
                for hc, into in ((c, mine), (1 - c, theirs)):
                    into.append(pltpu.make_async_remote_copy(
                        src_ref=_half(src[t], axes[t], hc, lead=(slot,)),
                        dst_ref=_half(dst[t], axes[t], hc, lead=(slot,)),
                        send_sem=send.at[t, k], recv_sem=recv.at[t, k], device_id=(x, y, 1 - c), device_id_type=MESH))
        return mine, theirs

    def start(src, dst, sems):
        for cp in copies(src, dst, *sems)[0]:
            cp.start()

    def finish(src, dst, sems):
        mine, theirs = copies(src, dst, *sems)
        for cp in theirs:
            cp.wait_recv()
        for cp in mine:
            cp.wait_send()

    return _Rider(bufs, [jax.ShapeDtypeStruct(a.shape, a.dtype) for a in bufs],
                  [pltpu.SemaphoreType.DMA((nt, 3)), pltpu.SemaphoreType.DMA((nt, 3))], start, finish,
                  aliases={t: t for t in range(nt)})


def _rider_chip_exchange(parts):
    nt = len(parts)

    def copies(src, got, send, recv):
        x, y, c, chips = _place()
        return [pltpu.make_async_remote_copy(
            src_ref=src[t].at[2 * chips[k][0] + chips[k][1]], dst_ref=got[t].at[k], send_sem=send.at[t, k],
            recv_sem=recv.at[t, k], device_id=(chips[k][0], chips[k][1], c), device_id_type=MESH)
            for t in range(nt) for k in range(3)]

    def start(src, got, sems):
        for cp in copies(src, got, *sems):
            cp.start()

    def finish(src, got, sems):
        remote = copies(src, got, *sems)
        for cp in remote:
            cp.wait_recv()
        for cp in remote:
            cp.wait_send()

    return _Rider(parts, [jax.ShapeDtypeStruct((3,) + a.shape[1:], a.dtype) for a in parts],
                  [pltpu.SemaphoreType.DMA((nt, 3)), pltpu.SemaphoreType.DMA((nt, 3))], start, finish)


def _gather_all(block, *, name):
    m_per, n = block.shape

    def body(x_ref, out_ref, send_sems, recv_sems, local_sem):
        x, y, c, chips = _place()
        me, sibling = (x, y, c), (x, y, 1 - c)

        def rows(px, py, pc):
            return out_ref.at[4 * px + 2 * py + pc]

        def copy(k, blk, to, src=None):
            return pltpu.make_async_remote_copy(
                src_ref=rows(*blk) if src is None else src, dst_ref=rows(*blk),
                send_sem=send_sems.at[k], recv_sem=recv_sems.at[k], device_id=to, device_id_type=MESH)

        mine = pltpu.make_async_copy(x_ref, rows(*me), local_sem)
        mine.start()
        first = [copy(0, me, sibling, src=x_ref)]
        first += [copy(1 + j, me, (*chip, c), src=x_ref) for j, chip in enumerate(chips)]
        for cp in first:
            cp.start()
        passed = [copy(4 + j, (*chip, c), sibling) for j, chip in enumerate(chips)]
        for j, chip in enumerate(chips):
            copy(1 + j, (*chip, c), me).wait_recv()
            passed[j].start()
        copy(0, sibling, me).wait_recv()
        for j, chip in enumerate(chips):
            copy(4 + j, (*chip, 1 - c), me).wait_recv()
        for cp in first + passed:
            cp.wait_send()
        mine.wait()

    return pl.pallas_call(
        body, name=name,
        out_shape=jax.ShapeDtypeStruct((N_DEV, m_per, n), block.dtype),
        in_specs=[pl.BlockSpec(memory_space=pltpu.VMEM)], out_specs=pl.BlockSpec(memory_space=pltpu.VMEM),
        scratch_shapes=[pltpu.SemaphoreType.DMA((7,)), pltpu.SemaphoreType.DMA((7,)), pltpu.SemaphoreType.DMA],
        compiler_params=pltpu.CompilerParams(vmem_limit_bytes=VMEM_LIMIT),
    )(block)


def _sum_slots(slots, *, name):
    n, m, c = slots.shape
    t = _rows_tile(m, c * n)

    def body(s_ref, o_ref):
        acc = s_ref[0]
        for k in range(1, n):
            acc = acc + s_ref[k]
        o_ref[...] = acc

    return pl.pallas_call(
        body, name=name, grid=(m // t,), in_specs=[pl.BlockSpec((n, t, c), lambda i: (0, i, 0))],
        out_specs=pl.BlockSpec((t, c), lambda i: (i, 0)), out_shape=jax.ShapeDtypeStruct((m, c), F32),
        compiler_params=_cparams("parallel"),
    )(slots)


def _pad_rows(a, rows):
    return a if a.shape[0] == rows else jnp.pad(a, ((0, rows - a.shape[0]), (0, 0)))


def _w_in_padded(shards):
    full = shards.reshape(IN_COLS, shards.shape[2])
    return jnp.concatenate([_pad_rows(full[SEG[n][2]:SEG[n][2] + SEG[n][3]], SEG[n][1]) for n in SEG_ORDER], axis=0)


def _w_in_unpadded(gp):
    full = jnp.concatenate([gp[SEG[n][0]:SEG[n][0] + SEG[n][3]] for n in ORIG_ORDER], axis=0)
    return full.reshape(N_CHIP, IN_COLS // N_CHIP, gp.shape[1])


def _pad_heads(w, true_w, pad_w):
    r = w.shape[0]
    h = w.shape[1] // true_w
    return jnp.pad(w.reshape(r, h, true_w), ((0, 0), (0, 0), (0, pad_w - true_w))).reshape(r, h * pad_w)


def _unpad_heads(w, true_w, pad_w):
    r = w.shape[0]
    h = w.shape[1] // pad_w
    return w.reshape(r, h, pad_w)[:, :, :true_w].reshape(r, h * true_w)


def _cols_to_slots(a):
    return a.reshape(a.shape[0], N_CHIP, a.shape[1] // N_CHIP).transpose(1, 0, 2)


def _slots_to_cols(a):
    return jnp.concatenate([a[j] for j in range(N_CHIP)], axis=1)


def _to_heads(a, h, d):
    return a.reshape(a.shape[0], h, d).transpose(1, 0, 2)


def _from_heads(a):
    return a.transpose(1, 0, 2).reshape(a.shape[1], -1)


SMALL = [("norm_g", 2048), ("ret_norm_g", 512), ("gla_ba_f", 256), ("gla_ba_b", 256), ("gla_norm_g", 512),
         ("pool_w", 4 * 128 * 128), ("pool_scale", 512), ("mla_q_norm_g", 512), ("mla_kv_norm_g", 256),
         ("mla_qk_norm_q", 192), ("mla_qk_norm_k", 192)]


def _pack_small(vals):
    parts = []
    for name, n in SMALL:
        parts += [v.reshape(-1) for v in vals[name]]
        if (DEPTH * n) % 1024:
            parts.append(jnp.zeros((-(DEPTH * n)) % 1024, F32))
    parts += [vals["loss"].reshape(-1), jnp.zeros(1023, F32)]
    return jnp.concatenate(parts).reshape(-1, 128)


def _unpack_small(block):
    flat = block.reshape(-1)
    out, off = {}, 0
    for name, n in SMALL:
        out[name] = flat[off:off + DEPTH * n]
        off += DEPTH * n + (-(DEPTH * n)) % 1024
    out["loss"] = flat[off]
    return out


def _layer_weights(l, p, g):
    wa = jnp.zeros((128, 512), F32)
    wa = wa.at[0:GLA_RANK, 0:256].set(_slots_to_cols(g["gla_wa2_f"]))
    wa = wa.at[GLA_RANK:2 * GLA_RANK, 256:512].set(_slots_to_cols(g["gla_wa2_b"]))
    return dict(
        norm_g=p["norm_g"][l][None, :],
        w_in=_w_in_padded(g["w_in"]),
        w_out=g["w_out"].reshape(4 * g["w_out"].shape[1], -1),
        ret_norm_g=p["ret_norm_g"][l][None, :],
        wa=_bf(wa),
        ba=jnp.concatenate([p["gla_ba_f"][l], p["gla_ba_b"][l]])[None, :],
        gla_norm_g=p["gla_norm_g"][l][None, :],
        pool_w=_bf(p["pool_w"][l]),
        pool_scale=p["pool_scale"][l][None, :],
        qg=p["mla_q_norm_g"][l][None, :],
        wq=_pad_heads(_slots_to_cols(g["mla_wq_b"]), MLA_QK, MLA_QKP),
        kvg=p["mla_kv_norm_g"][l][None, :],
        wkv=_slots_to_cols(g["mla_wkv_b"]),
        qng=jnp.pad(p["mla_qk_norm_q"][l], (0, MLA_QKP - MLA_QK))[None, :],
        kng=jnp.pad(p["mla_qk_norm_k"][l], (0, MLA_QKP - MLA_QK))[None, :],
    )


def _layer_fwd(l, x, w, tabs, next_shards=None):
    ret_cos, ret_sin, mla_cos, mla_sp, mla_sn = tabs
    nm = lambda s: f"l{l}_{s}"
    h = _rmsnorm_fwd(x, w["norm_g"], name=nm("norm"))
    if next_shards is None:
        z = _matmul(h, w["w_in"], tb=True, name=nm("in_proj"))
    else:
        z, landed = _matmul(h, w["w_in"], tb=True, rider=_rider_gather_send(next_shards[:1], SHARD_AXES[:1]),
                            name=nm("in_proj"))
    qr, kr = _ret_pre(z, ret_cos, ret_sin, name=nm("ret_pre"))
    ret_o = _bla(qr, kr, z, _ret_log_gamma(False), (0, 0, SEG["rv"][0] // 512), name=nm("ret_scan"))
    y_a = _post(ret_o, z, SEG["rg"][0] // 512, w["ret_norm_g"], norm=True, name=nm("ret_post"))
    la = _gla_gate(z, w["wa"], w["ba"], name=nm("gla_gate"))
    la_h = la.reshape(la.shape[0], 2, GLA_HEADS, GLA_DK).transpose(1, 2, 0, 3)
    gq = _to_heads(z[:, SEG["gq"][0]:SEG["gq"][0] + 256], GLA_HEADS, GLA_DK)
    gk = _to_heads(z[:, SEG["gk"][0]:SEG["gk"][0] + 256], GLA_HEADS, GLA_DK)
    if next_shards is None:
        gla_o, gla_st = _gla_fwd(gq, gk, z, la_h, name=nm("gla_scan"))
    else:
        gla_o, gla_st, more = _gla_fwd(gq, gk, z, la_h, rider=_rider_gather_send(next_shards[1:], SHARD_AXES[1:]),
                                       name=nm("gla_scan"))
        landed = list(landed) + list(more)
    y_b = _post(gla_o, z, SEG["gg"][0] // 512, w["gla_norm_g"], norm=True, name=nm("gla_post"))
    y_c = _pool_fwd(z, w["pool_w"], w["pool_scale"], name=nm("pool"))
    q, k, v = _mla_pre(z, w["qg"], w["wq"], w["kvg"], w["wkv"], w["qng"], w["kng"], mla_cos, mla_sp, mla_sn,
                       name=nm("mla_pre"))
    if next_shards is None:
        (att_o, lse), gathered = _flash_fwd(q, k, v, name=nm("attn")), None
    else:
        att_o, lse, gathered = _flash_fwd(q, k, v, rider=_rider_gather_forward(landed, SHARD_AXES), name=nm("attn"))
    y_d = _post([att_o], z, SEG["mg"][0] // 512, w["qg"], norm=False, name=nm("mla_post"))
    y = jnp.concatenate([y_a, y_b, y_c, y_d], axis=1)
    x_next = _matmul(y, w["w_out"], add=x, name=nm("out_proj"))
    saved = dict(x=x, h=h, z=z, y=y, qr=qr, kr=kr, ret_o=ret_o, la_h=la_h, gq=gq, gk=gk, gla_o=gla_o, gla_st=gla_st,
                 q=q, k=k, v=v, att_o=att_o, lse=lse)
    return x_next, saved, gathered


def _layer_bwd(l, dx_next, w, sv, tabs, riding_parts=None):
    ret_cos, ret_sin, mla_cos, mla_sp, mla_sn = tabs
    nm = lambda s: f"l{l}_{s}"
    z = sv["z"]
    dy = _matmul(dx_next, w["w_out"], tb=True, name=nm("out_proj_dy"))
    d_w_out = _matmul(sv["y"], dx_next, ta=True, tn=512, name=nm("out_proj_dw"))
    dz = lax.empty((z.shape[0], IN_PAD), BF16)
    at = lambda n: SEG[n][0]
    dz, d_ret_o, d_ret_g = _post_bwd(dy, 0, sv["ret_o"], z, SEG["rg"][0] // 512, w["ret_norm_g"], (dz, at("rg")),
                                     norm=True, name=nm("ret_post_bwd"))
    vcol = SEG["rv"][0] // 512
    dqr = _bla(d_ret_o, z, sv["kr"], _ret_log_gamma(False), (0, vcol, 0), name=nm("ret_scan_dq"))
    dkr = _bla(z, d_ret_o, sv["qr"], _ret_log_gamma(True), (vcol, 0, 0), name=nm("ret_scan_dk"))
    drv = _bla(sv["kr"], sv["qr"], d_ret_o, _ret_log_gamma(True), (0, 0, 0), name=nm("ret_scan_dv"))
    dz = _ret_pre_bwd(dqr, dkr, ret_cos, ret_sin, (dz, at("rq")), name=nm("ret_pre_bwd"))
    dz = _add_n([drv[0], drv[1]], out_dtype=BF16, into=(dz, at("rv")), name=nm("ret_dv_sum"))
    dz, d_gla_o, d_gla_g = _post_bwd(dy, 1, sv["gla_o"], z, SEG["gg"][0] // 512, w["gla_norm_g"], (dz, at("gg")),
                                     norm=True, name=nm("gla_post_bwd"))
    dq2, dk2, dla2, dv2 = _gla_bwd(sv["gq"], sv["gk"], z, sv["la_h"], d_gla_o, sv["gla_st"], name=nm("gla_scan_bwd"))
    d_gq = _bf(_from_heads(dq2[0] + dq2[1]))
    d_gk = _bf(_from_heads(dk2[0] + dk2[1]))
    dz = _add_n([dv2[0], dv2[1]], out_dtype=BF16, into=(dz, at("gv")), name=nm("gla_dv_sum"))
    dla = jnp.concatenate([_from_heads(dla2[0]), _from_heads(dla2[1])], axis=1)
    dz, d_wa, d_ba = _gla_gate_bwd(dla, z, w["wa"], w["ba"], (dz, at("ga")), name=nm("gla_gate_bwd"))
    d_pv, d_pg, d_pool_w, d_pool_scale = _pool_bwd(dy, z, w["pool_w"], w["pool_scale"], name=nm("pool_bwd"))
    dz, d_att_o, _ = _post_bwd(dy, 3, [sv["att_o"]], z, SEG["mg"][0] // 512, w["qg"], (dz, at("mg")), norm=False,
                               name=nm("mla_post_bwd"))
    if riding_parts is None:
        (dq, dk, dv), rode = _flash_bwd(sv["q"], sv["k"], sv["v"], d_att_o, sv["att_o"], sv["lse"],
                                        name=nm("attn_bwd")), None
    else:
        dq, dk, dv, rode = _flash_bwd(sv["q"], sv["k"], sv["v"], d_att_o, sv["att_o"], sv["lse"],
                                      rider=_rider_chip_exchange(riding_parts), name=nm("attn_bwd"))
    d_mq, d_mkv, d_mkr, d_wq, d_wkv, d_qg, d_kvg, d_qng, d_kng = _mla_pre_bwd(
        dq, dk, dv, z, w["qg"], w["wq"], w["kvg"], w["wkv"], w["qng"], w["kng"], mla_cos, mla_sp, mla_sn,
        name=nm("mla_pre_bwd"))
    for n, seg in dict(pv=d_pv, pg=d_pg, mq=d_mq, gq=d_gq, gk=d_gk, mkv=d_mkv, mkr=d_mkr).items():
        dz = lax.dynamic_update_slice(dz, seg, (0, at(n)))
    dh = _matmul(dz, w["w_in"], tn=512, name=nm("in_proj_dh"))
    d_w_in = _matmul(dz, sv["h"], ta=True, name=nm("in_proj_dw"))
    dx, d_norm_g = _rmsnorm_bwd(sv["x"], dh, w["norm_g"], dx_next, name=nm("norm_bwd"))
    sharded = dict(
        w_in=_w_in_unpadded(d_w_in),
        w_out=d_w_out.reshape(N_CHIP, d_w_out.shape[0] // N_CHIP, d_w_out.shape[1]),
        mla_wq_b=_cols_to_slots(_unpad_heads(d_wq, MLA_QK, MLA_QKP)),
        mla_wkv_b=_cols_to_slots(d_wkv),
        gla_wa2_f=_cols_to_slots(d_wa[0:GLA_RANK, 0:256]),
        gla_wa2_b=_cols_to_slots(d_wa[GLA_RANK:2 * GLA_RANK, 256:512]),
    )
    small = dict(
        norm_g=d_norm_g[0], ret_norm_g=d_ret_g[0], gla_ba_f=d_ba[0, :256], gla_ba_b=d_ba[0, 256:],
        gla_norm_g=d_gla_g[0], pool_w=d_pool_w.reshape(-1), pool_scale=d_pool_scale[0], mla_q_norm_g=d_qg[0],
        mla_kv_norm_g=d_kvg[0], mla_qk_norm_q=d_qng[0, :MLA_QK], mla_qk_norm_k=d_kng[0, :MLA_QK],
    )
    return dx, sharded, small, rode


SHARDED = ["w_in", "w_out", "mla_wq_b", "mla_wkv_b", "gla_wa2_f", "gla_wa2_b"]
WEIGHTS = ["norm_g", "w_in", "ret_norm_g", "gla_wa2_f", "gla_ba_f", "gla_wa2_b", "gla_ba_b", "gla_norm_g", "pool_w",
           "pool_scale", "mla_q_norm_g", "mla_wq_b", "mla_kv_norm_g", "mla_wkv_b", "mla_qk_norm_q", "mla_qk_norm_k",
           "w_out"]


SHARD_AXES = [1, 0, 0, 0, 0, 0]


def _layer_shards(p, l):
    return [jnp.swapaxes(p["w_in"], 1, 2)[l].astype(BF16), p["w_out"][l].astype(BF16), p["mla_wq_b"][l].astype(BF16),
            p["mla_wkv_b"][l].astype(BF16), p["gla_wa2_f"][l], p["gla_wa2_b"][l]]


def _step(p, where):
    x = p["x"][0]
    tabs = _rope_tables(x.shape[0])
    got0 = _gather_shards(_layer_shards(p, 0), SHARD_AXES, name="l0_gather_weights")
    w0 = _layer_weights(0, p, dict(zip(SHARDED, got0)))
    x1, sv0, got1 = _layer_fwd(0, x, w0, tabs, next_shards=_layer_shards(p, 1))
    w1 = _layer_weights(1, p, dict(zip(SHARDED, got1)))
    x2, sv1, _ = _layer_fwd(1, x1, w1, tabs)
    dx, loss = _loss_head(x2, p["loss_target"][0], name="loss_head")

    big, big_axes = SHARDED[:2], SHARD_AXES[:2]

    def pair_sums(tag, tensors, axes, names):
        return [_pair_reduce(a, where, ax, out_dtype=BF16, name=f"{tag}_pair_reduce_{n}")
                for a, ax, n in zip(tensors, axes, names)]

    def joined(tag, pair, others, axes, names):
        return [_sum_join(a, b, where, ax, name=f"{tag}_sum_join_{n}")
                for a, b, ax, n in zip(pair, others, axes, names)]

    dx, sharded1, small1, _ = _layer_bwd(1, dx, w1, sv1, tabs)
    pair1 = pair_sums("l1", [sharded1[n] for n in big], big_axes, big)
    dx, sharded0, small0, others1 = _layer_bwd(0, dx, w0, sv0, tabs, riding_parts=pair1)
    grads1 = joined("l1", pair1, others1, big_axes, big)
    packed = jnp.concatenate([sh[n].reshape(N_CHIP, -1, 128) for sh in (sharded0, sharded1) for n in SHARDED[2:]],
                             axis=1)
    pair0 = pair_sums("l0", [sharded0[n] for n in big] + [packed], big_axes + [0], big + ["rest"])
    grads0 = joined("l0", pair0, _chip_exchange(pair0, name="l0_chip_exchange"), big_axes + [0], big + ["rest"])
    grads = {n: jnp.stack([g0, g1]) for n, g0, g1 in zip(big, grads0, grads1)}
    rest, off = grads0[2], 0
    pieces = {n: [] for n in SHARDED[2:]}
    for sh in (sharded0, sharded1):
        for n in SHARDED[2:]:
            rows = sh[n].shape[1] * sh[n].shape[2] // 128
            pieces[n].append(rest[off:off + rows].reshape(sh[n].shape[1:]))
            off += rows
    grads.update({n: jnp.stack(v) for n, v in pieces.items()})
    small = {n: [small0[n], small1[n]] for n, _ in SMALL}
    small["loss"] = loss
    return dx[None], grads, small


def kernel(x, norm_g, w_in, ret_norm_g, gla_wa2_f, gla_ba_f, gla_wa2_b, gla_ba_b, gla_norm_g, pool_w, pool_scale, mla_q_norm_g, mla_wq_b, mla_kv_norm_g, mla_wkv_b, mla_qk_norm_q, mla_qk_norm_k, w_out, loss_target, m_norm_g, m_w_in, m_ret_norm_g, m_gla_wa2_f, m_gla_ba_f, m_gla_wa2_b, m_gla_ba_b, m_gla_norm_g, m_pool_w, m_pool_scale, m_mla_q_norm_g, m_mla_wq_b, m_mla_kv_norm_g, m_mla_wkv_b, m_mla_qk_norm_q, m_mla_qk_norm_k, m_w_out, v_norm_g, v_w_in, v_ret_norm_g, v_gla_wa2_f, v_gla_ba_f, v_gla_wa2_b, v_gla_ba_b, v_gla_norm_g, v_pool_w, v_pool_scale, v_mla_q_norm_g, v_mla_wq_b, v_mla_kv_norm_g, v_mla_wkv_b, v_mla_qk_norm_q, v_mla_qk_norm_k, v_w_out):
    p = dict(x=x, norm_g=norm_g, w_in=w_in, ret_norm_g=ret_norm_g, gla_wa2_f=gla_wa2_f, gla_ba_f=gla_ba_f,
             gla_wa2_b=gla_wa2_b, gla_ba_b=gla_ba_b, gla_norm_g=gla_norm_g, pool_w=pool_w, pool_scale=pool_scale,
             mla_q_norm_g=mla_q_norm_g, mla_wq_b=mla_wq_b, mla_kv_norm_g=mla_kv_norm_g, mla_wkv_b=mla_wkv_b,
             mla_qk_norm_q=mla_qk_norm_q, mla_qk_norm_k=mla_qk_norm_k, w_out=w_out, loss_target=loss_target)
    moments = dict(
        m=dict(norm_g=m_norm_g, w_in=m_w_in, ret_norm_g=m_ret_norm_g, gla_wa2_f=m_gla_wa2_f, gla_ba_f=m_gla_ba_f,
               gla_wa2_b=m_gla_wa2_b, gla_ba_b=m_gla_ba_b, gla_norm_g=m_gla_norm_g, pool_w=m_pool_w,
               pool_scale=m_pool_scale, mla_q_norm_g=m_mla_q_norm_g, mla_wq_b=m_mla_wq_b,
               mla_kv_norm_g=m_mla_kv_norm_g, mla_wkv_b=m_mla_wkv_b, mla_qk_norm_q=m_mla_qk_norm_q,
               mla_qk_norm_k=m_mla_qk_norm_k, w_out=m_w_out),
        v=dict(norm_g=v_norm_g, w_in=v_w_in, ret_norm_g=v_ret_norm_g, gla_wa2_f=v_gla_wa2_f, gla_ba_f=v_gla_ba_f,
               gla_wa2_b=v_gla_wa2_b, gla_ba_b=v_gla_ba_b, gla_norm_g=v_gla_norm_g, pool_w=v_pool_w,
               pool_scale=v_pool_scale, mla_q_norm_g=v_mla_q_norm_g, mla_wq_b=v_mla_wq_b,
               mla_kv_norm_g=v_mla_kv_norm_g, mla_wkv_b=v_mla_wkv_b, mla_qk_norm_q=v_mla_qk_norm_q,
               mla_qk_norm_k=v_mla_qk_norm_k, w_out=v_w_out))

    where = jnp.stack([lax.axis_index("c"), 2 * lax.axis_index("x") + lax.axis_index("y")]).astype(jnp.int32)
    grad_x, grads, small = _step(p, where)

    slots = _gather_all(_pack_small(small), name="gather_small")
    total = _unpack_small(_sum_slots(slots, name="sum_small"))
    for n, _ in SMALL:
        grads[n] = total[n].reshape(p[n].shape)
    loss = total["loss"]

    delta, new_m, new_v = {}, {}, {}
    for n in WEIGHTS:
        turn = (lambda a: jnp.swapaxes(a, 1, 2)) if n == "w_in" else (lambda a: a)
        outs = _adamw(turn(p[n]), grads[n], turn(moments["m"][n]), turn(moments["v"][n]), name=f"adamw_{n}")
        grads[n] = turn(grads[n])
        delta[n], new_m[n], new_v[n] = (turn(o) for o in outs)
    return (loss, grad_x, *[grads[n] for n in WEIGHTS], *[delta[n] for n in WEIGHTS],
            *[new_m[n] for n in WEIGHTS], *[new_v[n] for n in WEIGHTS])
```

```python
import jax
import jax.numpy as jnp
from jax import lax
from jax.experimental import pallas as pl
from jax.experimental.pallas import tpu as pltpu

F32 = jnp.float32
BF16 = jnp.bfloat16
MESH = pl.DeviceIdType.MESH

EPS = 1e-6
ROPE_THETA = 10000.0
DEPTH = 2
N_DEV = 8
N_CHIP = 4

GROUP_W = 512
RET_HEADS = 4
RET_HD = 128
RET_CHUNK = 256
GLA_HEADS = 4
GLA_DK = 64
GLA_DV = 128
GLA_RANK = 16
GLA_TAU = 16.0
GLA_CHUNK = 64
POOL_GROUPS = 4
POOL_GW = 128
POOL_HALO = 8
POOL_TILE = 256
MLA_HEADS = 4
MLA_NOPE = 128
MLA_ROPE = 64
MLA_QK = MLA_NOPE + MLA_ROPE
MLA_QKP = 256
MLA_V = 128
MLA_Q_RANK = 512
MLA_KV_RANK = 256
MLA_SCALE = MLA_QK ** -0.5

ADAM_LR = 0.001
ADAM_B1 = 0.9
ADAM_B2 = 0.999
ADAM_EPS = 1e-08
ADAM_WD = 0.01
ADAM_STEP = 10

VMEM_LIMIT = 56 * 1024 * 1024
ROW_TILE = 512

SEG = {
    "rq": (0, 512, 0, 512), "rk": (512, 512, 512, 512), "rv": (1024, 512, 1024, 512), "rg": (1536, 512, 1536, 512),
    "gv": (2048, 512, 2560, 512), "gg": (2560, 512, 3072, 512),
    "pv": (3072, 512, 3616, 512), "pg": (3584, 512, 4128, 512),
    "mq": (4096, 512, 4640, 512), "mg": (4608, 512, 5472, 512),
    "gq": (5120, 256, 2048, 256), "gk": (5376, 256, 2304, 256), "mkv": (5632, 256, 5152, 256),
    "ga": (5888, 128, 3584, 32), "mkr": (6016, 128, 5408, 64),
}
SEG_ORDER = ["rq", "rk", "rv", "rg", "gv", "gg", "pv", "pg", "mq", "mg", "gq", "gk", "mkv", "ga", "mkr"]
IN_COLS = 5984
IN_PAD = 6144
ORIG_ORDER = ["rq", "rk", "rv", "rg", "gq", "gk", "gv", "gg", "ga", "pv", "pg", "mq", "mkv", "mkr", "mg"]


def _cparams(*sem):
    return pltpu.CompilerParams(dimension_semantics=tuple(sem), vmem_limit_bytes=VMEM_LIMIT)


def _bf(v):
    return v.astype(BF16)


def _dot(a, b, ca=1, cb=0):
    return lax.dot_general(_bf(a), _bf(b), (((ca,), (cb,)), ((), ())), preferred_element_type=F32)


def _sigmoid(x):
    return 1.0 / (1.0 + jnp.exp(-x))


def _silu_parts(g):
    sg = _sigmoid(g)
    return g * sg, sg * (1.0 + g * (1.0 - sg))


class _Rider:
    def __init__(self, ins, outs, sems, start, finish, aliases=None):
        self.ins, self.outs, self.sems, self.start, self.finish = list(ins), list(outs), list(sems), start, finish
        self.aliases = dict(aliases or {})


def _ride(body, rider, n_in, n_out, grid):
    if rider is None:
        return body
    ri, ro, rs = len(rider.ins), len(rider.outs), len(rider.sems)

    def wrapped(*refs):
        ins, refs = refs[:n_in], refs[n_in:]
        rin, refs = refs[:ri], refs[ri:]
        outs, refs = refs[:n_out], refs[n_out:]
        rout, refs = refs[:ro], refs[ro:]
        scratch, sems = refs[:len(refs) - rs], refs[len(refs) - rs:]
        first = pl.program_id(0) == 0
        last = pl.program_id(0) == grid[0] - 1
        for ax in range(1, len(grid)):
            first = jnp.logical_and(first, pl.program_id(ax) == 0)
            last = jnp.logical_and(last, pl.program_id(ax) == grid[ax] - 1)

        @pl.when(first)
        def _():
            rider.start(rin, rout, sems)

        body(*ins, *outs, *scratch)

        @pl.when(last)
        def _():
            rider.finish(rin, rout, sems)

    return wrapped


def _ride_call(body, rider, *, name, grid, in_specs, out_specs, out_shape, scratch_shapes, args, sem):
    n_in, n_out = len(in_specs), len(out_specs)
    if rider is None:
        return pl.pallas_call(body, name=name, grid=grid, in_specs=in_specs, out_specs=out_specs, out_shape=out_shape,
                              scratch_shapes=scratch_shapes, compiler_params=_cparams(*sem))(*args), []
    outs = pl.pallas_call(
        _ride(body, rider, n_in, n_out, grid), name=name, grid=grid,
        in_specs=list(in_specs) + [ANY] * len(rider.ins), out_specs=list(out_specs) + [ANY] * len(rider.outs),
        out_shape=list(out_shape) + rider.outs, scratch_shapes=list(scratch_shapes) + rider.sems,
        input_output_aliases={n_in + i: n_out + o for i, o in rider.aliases.items()},
        compiler_params=_cparams(*(["arbitrary"] * len(grid))),
    )(*args, *rider.ins)
    return outs[:n_out], outs[n_out:]


def _matmul(a, b, *, ta=False, tb=False, out_dtype=F32, tm=512, tn=1024, tk=None, add=None, n_outer=True, rider=None,
            name):
    m, kdim = (a.shape[1], a.shape[0]) if ta else a.shape
    n = b.shape[0] if tb else b.shape[1]
    tm, tn = min(tm, m), min(tn, n)
    tk = kdim if tk is None else min(tk, kdim)
    assert m % tm == 0 and n % tn == 0 and kdim % tk == 0
    nk = kdim // tk
    ca, cb = (0 if ta else 1), (1 if tb else 0)

    def body(*refs):
        if add is None:
            a_ref, b_ref, o_ref = refs[:3]
            add_ref = None
        else:
            a_ref, b_ref, add_ref, o_ref = refs[:4]
        p = _dot(a_ref[...], b_ref[...], ca, cb)

        def finish(r):
            if add_ref is not None:
                r = r + add_ref[...]
            o_ref[...] = r.astype(out_dtype)

        if nk == 1:
            finish(p)
        else:
            acc = refs[-1]
            k = pl.program_id(2)

            @pl.when(k == 0)
            def _():
                acc[...] = p

            @pl.when(k > 0)
            def _():
                acc[...] += p

            @pl.when(k == nk - 1)
            def _():
                finish(acc[...])

    def ij(g0, g1):
        return (g1, g0) if n_outer else (g0, g1)

    a_spec = (pl.BlockSpec((tk, tm), lambda g0, g1, k: (k, ij(g0, g1)[0])) if ta
              else pl.BlockSpec((tm, tk), lambda g0, g1, k: (ij(g0, g1)[0], k)))
    b_spec = (pl.BlockSpec((tn, tk), lambda g0, g1, k: (ij(g0, g1)[1], k)) if tb
              else pl.BlockSpec((tk, tn), lambda g0, g1, k: (k, ij(g0, g1)[1])))
    o_spec = pl.BlockSpec((tm, tn), lambda g0, g1, k: ij(g0, g1))
    in_specs = [a_spec, b_spec] + ([o_spec] if add is not None else [])
    args = (a, b) + ((add,) if add is not None else ())
    grid = (n // tn, m // tm, nk) if n_outer else (m // tm, n // tn, nk)
    (out,), rode = _ride_call(
        body, rider, name=name, grid=grid, in_specs=in_specs, out_specs=[o_spec],
        out_shape=[jax.ShapeDtypeStruct((m, n), out_dtype)],
        scratch_shapes=[] if nk == 1 else [pltpu.VMEM((tm, tn), F32)], args=args,
        sem=("parallel", "parallel", "arbitrary"))
    return out if rider is None else (out, rode)


def _rmsnorm_fwd(x, g, *, name, tm=ROW_TILE):
    s, d = x.shape
    tm = min(tm, s)

    def body(x_ref, g_ref, h_ref):
        xv = x_ref[...]
        r = lax.rsqrt(jnp.mean(xv * xv, axis=-1, keepdims=True) + EPS)
        h_ref[...] = _bf(xv * r * g_ref[...])

    return pl.pallas_call(
        body, name=name, grid=(s // tm,),
        in_specs=[pl.BlockSpec((tm, d), lambda i: (i, 0)), pl.BlockSpec((1, d), lambda i: (0, 0))],
        out_specs=pl.BlockSpec((tm, d), lambda i: (i, 0)),
        out_shape=jax.ShapeDtypeStruct((s, d), BF16),
        compiler_params=_cparams("parallel"),
    )(x, g)


def _rmsnorm_bwd(x, dh, g, dres, *, name, tm=ROW_TILE):
    s, d = x.shape
    tm = min(tm, s)

    def body(x_ref, dh_ref, g_ref, dres_ref, dx_ref, dg_ref):
        i = pl.program_id(0)
        xv = x_ref[...]
        r = lax.rsqrt(jnp.mean(xv * xv, axis=-1, keepdims=True) + EPS)
        xn = xv * r
        dv = dh_ref[...]
        part = jnp.sum(dv * xn, axis=0, keepdims=True)

        @pl.when(i == 0)
        def _():
            dg_ref[...] = part

        @pl.when(i > 0)
        def _():
            dg_ref[...] += part

        dxn = dv * g_ref[...]
        dx_ref[...] = dres_ref[...] + r * (dxn - xn * jnp.mean(dxn * xn, axis=-1, keepdims=True))

    row = pl.BlockSpec((tm, d), lambda i: (i, 0))
    vec = pl.BlockSpec((1, d), lambda i: (0, 0))
    return pl.pallas_call(
        body, name=name, grid=(s // tm,), in_specs=[row, row, vec, row], out_specs=[row, vec],
        out_shape=[jax.ShapeDtypeStruct((s, d), F32), jax.ShapeDtypeStruct((1, d), F32)],
        compiler_params=_cparams("arbitrary"),
    )(x, dh, g, dres)


def _loss_head(xf, target, *, name, tm=ROW_TILE):
    s, d = xf.shape
    tm = min(tm, s)

    def body(x_ref, t_ref, dx_ref, l_ref):
        i = pl.program_id(0)
        e = x_ref[...] - t_ref[...]
        dx_ref[...] = e * (1.0 / d)
        rows = jnp.mean(e * e, axis=-1, keepdims=True)
        part = 0.5 * jnp.sum(rows, axis=0, keepdims=True)

        @pl.when(i == 0)
        def _():
            l_ref[...] = part

        @pl.when(i > 0)
        def _():
            l_ref[...] += part

    row = pl.BlockSpec((tm, d), lambda i: (i, 0))
    return pl.pallas_call(
        body, name=name, grid=(s // tm,), in_specs=[row, row],
        out_specs=[row, pl.BlockSpec((1, 1), lambda i: (0, 0))],
        out_shape=[jax.ShapeDtypeStruct((s, d), F32), jax.ShapeDtypeStruct((1, 1), F32)],
        compiler_params=_cparams("arbitrary"),
    )(xf, target)


def _rope_tables(s):
    pos = jnp.arange(s, dtype=F32)[:, None]
    inv_r = 1.0 / (ROPE_THETA ** (jnp.arange(0, RET_HD, 2, dtype=F32) / RET_HD))
    ang = pos * inv_r[None, :]
    ret_cos = jnp.concatenate([jnp.cos(ang), jnp.cos(ang)], axis=1)
    ret_sin = jnp.concatenate([-jnp.sin(ang), jnp.sin(ang)], axis=1)
    inv_m = 1.0 / (ROPE_THETA ** (jnp.arange(0, MLA_ROPE, 2, dtype=F32) / MLA_ROPE))
    am = pos * inv_m[None, :]
    z32, z64 = jnp.zeros((s, 32), F32), jnp.zeros((s, 64), F32)
    mla_cos = jnp.concatenate([jnp.cos(am), jnp.cos(am), z64], axis=1)
    mla_sp = jnp.concatenate([z32, jnp.sin(am), z64], axis=1)
    mla_sn = jnp.concatenate([-jnp.sin(am), z32, z64], axis=1)
    return ret_cos, ret_sin, mla_cos, mla_sp, mla_sn


def _rope128(x, c, sg):
    return x * c + pltpu.roll(x, 64, 1) * sg


def _unrope128(d, c, sg):
    return d * c + pltpu.roll(d * sg, 64, 1)


def _rope64(t, c, sp, sn):
    return t * c + pltpu.roll(t, 96, 1) * sn + pltpu.roll(t, 32, 1) * sp


def _unrope64(d, c, sp, sn):
    return d * c + pltpu.roll(d * sn, 32, 1) + pltpu.roll(d * sp, 96, 1)


def _ret_pre(z, cos, sin, *, name, tm=ROW_TILE):
    s = z.shape[0]
    tm = min(tm, s)
    scale = RET_HD ** -0.5

    def body(q_ref, k_ref, c_ref, s_ref, qo_ref, ko_ref):
        c, sg = c_ref[...], s_ref[...]
        for h in range(RET_HEADS):
            sl = slice(h * RET_HD, (h + 1) * RET_HD)
            qo_ref[:, sl] = _rope128(q_ref[:, sl], c, sg)
            ko_ref[:, sl] = _rope128(k_ref[:, sl], c, sg) * scale

    seg = lambda j: pl.BlockSpec((tm, GROUP_W), lambda i: (i, j))
    tab = pl.BlockSpec((tm, RET_HD), lambda i: (i, 0))
    return pl.pallas_call(
        body, name=name, grid=(s // tm,), in_specs=[seg(0), seg(1), tab, tab],
        out_specs=[seg(0), seg(0)],
        out_shape=[jax.ShapeDtypeStruct((s, GROUP_W), F32)] * 2,
        compiler_params=_cparams("parallel"),
    )(z, z, cos, sin)


def _ret_pre_bwd(dqr, dkr, cos, sin, into, *, name, tm=ROW_TILE):
    s = dqr[0].shape[0]
    tm = min(tm, s)
    scale = RET_HD ** -0.5

    def body(dq0_ref, dq1_ref, dk0_ref, dk1_ref, c_ref, s_ref, _, o_ref):
        c, sg = c_ref[...], s_ref[...]
        for h in range(RET_HEADS):
            sl = slice(h * RET_HD, (h + 1) * RET_HD)
            ksl = slice(GROUP_W + h * RET_HD, GROUP_W + (h + 1) * RET_HD)
            o_ref[:, sl] = _bf(_unrope128(dq0_ref[:, sl] + dq1_ref[:, sl], c, sg))
            o_ref[:, ksl] = _bf(_unrope128(dk0_ref[:, sl] + dk1_ref[:, sl], c, sg) * scale)

    row = pl.BlockSpec((tm, GROUP_W), lambda i: (i, 0))
    tab = pl.BlockSpec((tm, RET_HD), lambda i: (i, 0))
    out_shape, out_spec, more_specs, more_args = _landing(into, tm, 2 * GROUP_W)
    return pl.pallas_call(
        body, name=name, grid=(s // tm,), in_specs=[row, row, row, row, tab, tab] + more_specs, out_specs=out_spec,
        out_shape=out_shape, input_output_aliases={6: 0},
        compiler_params=_cparams("parallel"),
    )(dqr[0], dqr[1], dkr[0], dkr[1], cos, sin, *more_args)


def _bla(a, b, c, lg, cols, *, name):
    s = a.shape[0]
    ch = min(RET_CHUNK, s)
    n = s // ch
    hd = RET_HD

    def body(lg_ref, a0, b0, c0, a1, b1, c1, o0, o1, st):
        t = pl.program_id(0)

        @pl.when(t == 0)
        def _():
            st[...] = jnp.zeros_like(st)

        ii = lax.broadcasted_iota(jnp.int32, (ch, ch), 0)
        jj = lax.broadcasted_iota(jnp.int32, (ch, ch), 1)
        idx = lax.broadcasted_iota(jnp.int32, (ch, 1), 0).astype(F32)
        for d, (a_ref, b_ref, c_ref, o_ref) in enumerate(((a0, b0, c0, o0), (a1, b1, c1, o1))):
            diff = ((ii - jj) if d == 0 else (jj - ii)).astype(F32)
            keep = diff >= 0
            dpos = jnp.maximum(diff, 0.0)
            pq = (idx + 1.0) if d == 0 else (ch - idx)
            pk = (ch - 1.0 - idx) if d == 0 else idx
            for h in range(RET_HEADS):
                g = lg_ref[d, h]
                sl = slice(h * hd, (h + 1) * hd)
                av, bv, cv = a_ref[:, sl], b_ref[:, sl], c_ref[:, sl]
                sc = _dot(av, bv, 1, 1) * jnp.where(keep, jnp.exp(dpos * g), 0.0)
                stv = st[d, h]
                o_ref[:, sl] = _dot(sc, cv) + _dot(av * jnp.exp(pq * g), stv)
                st[d, h] = jnp.exp(ch * g) * stv + _dot(bv * jnp.exp(pk * g), cv, 0, 0)

    fwd = lambda j: pl.BlockSpec((ch, GROUP_W), lambda t: (t, j))
    bwd = lambda j: pl.BlockSpec((ch, GROUP_W), lambda t: (n - 1 - t, j))
    return pl.pallas_call(
        body, name=name, grid=(n,),
        in_specs=[pl.BlockSpec(memory_space=pltpu.SMEM), fwd(cols[0]), fwd(cols[1]), fwd(cols[2]),
                  bwd(cols[0]), bwd(cols[1]), bwd(cols[2])],
        out_specs=[fwd(0), bwd(0)],
        out_shape=[jax.ShapeDtypeStruct((s, GROUP_W), F32)] * 2,
        scratch_shapes=[pltpu.VMEM((2, RET_HEADS, hd, hd), F32)],
        compiler_params=_cparams("arbitrary"),
    )(lg, a, b, c, a, b, c)


def _post(os_, zg, gcol, g, *, norm, name, tm=ROW_TILE):
    s = zg.shape[0]
    tm = min(tm, s)
    nd = len(os_)

    def body(*refs):
        o_refs, (gt_ref, g_ref, y_ref) = refs[:nd], refs[nd:]
        silu, _ = _silu_parts(gt_ref[...])
        for h in range(4):
            sl = slice(h * 128, (h + 1) * 128)
            o = o_refs[0][:, sl]
            for k in range(1, nd):
                o = o + o_refs[k][:, sl]
            if norm:
                r = lax.rsqrt(jnp.mean(o * o, axis=-1, keepdims=True) + EPS)
                o = o * r * g_ref[:, sl]
            y_ref[:, sl] = _bf(silu[:, sl] * o)

    row = pl.BlockSpec((tm, GROUP_W), lambda i: (i, 0))
    return pl.pallas_call(
        body, name=name, grid=(s // tm,),
        in_specs=[row] * nd + [pl.BlockSpec((tm, GROUP_W), lambda i: (i, gcol)),
                               pl.BlockSpec((1, GROUP_W), lambda i: (0, 0))],
        out_specs=row,
        out_shape=jax.ShapeDtypeStruct((s, GROUP_W), BF16),
        compiler_params=_cparams("parallel"),
    )(*os_, zg, g)


def _post_bwd(dy, ycol, os_, zg, gcol, g, into, *, norm, name, tm=ROW_TILE):
    s = zg.shape[0]
    tm = min(tm, s)
    nd = len(os_)

    def body(*refs):
        dy_ref, o_refs = refs[0], refs[1:1 + nd]
        gt_ref, g_ref, _, dgt_ref, do_ref, dg_ref = refs[1 + nd:]
        i = pl.program_id(0)
        silu, dsilu = _silu_parts(gt_ref[...])
        dyv = dy_ref[...]
        parts = []
        for h in range(4):
            sl = slice(h * 128, (h + 1) * 128)
            o = o_refs[0][:, sl]
            for k in range(1, nd):
                o = o + o_refs[k][:, sl]
            dn = dyv[:, sl] * silu[:, sl]
            if norm:
                r = lax.rsqrt(jnp.mean(o * o, axis=-1, keepdims=True) + EPS)
                xn = o * r
                gh = g_ref[:, sl]
                dgt_ref[:, sl] = _bf(dyv[:, sl] * (xn * gh) * dsilu[:, sl])
                parts.append(jnp.sum(dn * xn, axis=0, keepdims=True))
                dxn = dn * gh
                do_ref[:, sl] = r * (dxn - xn * jnp.mean(dxn * xn, axis=-1, keepdims=True))
            else:
                dgt_ref[:, sl] = _bf(dyv[:, sl] * o * dsilu[:, sl])
                parts.append(jnp.zeros((1, 128), F32))
                do_ref[:, sl] = dn
        part = jnp.concatenate(parts, axis=1)

        @pl.when(i == 0)
        def _():
            dg_ref[...] = part

        @pl.when(i > 0)
        def _():
            dg_ref[...] += part

    row = pl.BlockSpec((tm, GROUP_W), lambda i: (i, 0))
    vec = pl.BlockSpec((1, GROUP_W), lambda i: (0, 0))
    dgt_shape, dgt_spec, more_specs, more_args = _landing(into, tm, GROUP_W)
    n_in = nd + 3
    return pl.pallas_call(
        body, name=name, grid=(s // tm,),
        in_specs=[pl.BlockSpec((tm, GROUP_W), lambda i: (i, ycol))] + [row] * nd
        + [pl.BlockSpec((tm, GROUP_W), lambda i: (i, gcol)), vec] + more_specs,
        out_specs=[dgt_spec, row, vec],
        out_shape=[dgt_shape, jax.ShapeDtypeStruct((s, GROUP_W), F32), jax.ShapeDtypeStruct((1, GROUP_W), F32)],
        input_output_aliases={n_in: 0},
        compiler_params=_cparams("arbitrary"),
    )(dy, *os_, zg, g, *more_args)


def _ret_log_gamma(swap):
    gf = 1.0 - 2.0 ** (-5.0 - jnp.arange(RET_HEADS, dtype=F32))
    lf, lb = jnp.log(gf), jnp.log(gf[::-1])
    return jnp.stack([lb, lf] if swap else [lf, lb])


def _log_sigmoid(x):
    return jnp.minimum(x, 0.0) - jnp.log(1.0 + jnp.exp(-jnp.abs(x)))


def _gla_gate(z, wa, ba, *, name, tm=ROW_TILE):
    s = z.shape[0]
    tm = min(tm, s)
    col = SEG["ga"][0] // 128

    def body(ga_ref, wa_ref, ba_ref, la_ref):
        pre = _dot(ga_ref[...], wa_ref[...]) + ba_ref[...]
        la_ref[...] = _log_sigmoid(pre) / GLA_TAU

    return pl.pallas_call(
        body, name=name, grid=(s // tm,),
        in_specs=[pl.BlockSpec((tm, 128), lambda i: (i, col)), pl.BlockSpec((128, 512), lambda i: (0, 0)),
                  pl.BlockSpec((1, 512), lambda i: (0, 0))],
        out_specs=pl.BlockSpec((tm, 512), lambda i: (i, 0)),
        out_shape=jax.ShapeDtypeStruct((s, 512), F32),
        compiler_params=_cparams("parallel"),
    )(z, wa, ba)


def _gla_gate_bwd(dla, z, wa, ba, into, *, name, tm=ROW_TILE):
    s = z.shape[0]
    tm = min(tm, s)
    col = SEG["ga"][0] // 128

    def body(dla_ref, ga_ref, wa_ref, ba_ref, _, dga_ref, dwa_ref, dba_ref):
        i = pl.program_id(0)
        gav = ga_ref[...]
        pre = _dot(gav, wa_ref[...]) + ba_ref[...]
        dpre = dla_ref[...] * (1.0 - _sigmoid(pre)) * (1.0 / GLA_TAU)
        dga_ref[...] = _bf(_dot(dpre, wa_ref[...], 1, 1))
        pw = _dot(gav, dpre, 0, 0)
        pb = jnp.sum(dpre, axis=0, keepdims=True)

        @pl.when(i == 0)
        def _():
            dwa_ref[...] = pw
            dba_ref[...] = pb

        @pl.when(i > 0)
        def _():
            dwa_ref[...] += pw
            dba_ref[...] += pb

    dga_shape, dga_spec, more_specs, more_args = _landing(into, tm, 128)
    return pl.pallas_call(
        body, name=name, grid=(s // tm,),
        in_specs=[pl.BlockSpec((tm, 512), lambda i: (i, 0)), pl.BlockSpec((tm, 128), lambda i: (i, col)),
                  pl.BlockSpec((128, 512), lambda i: (0, 0)), pl.BlockSpec((1, 512), lambda i: (0, 0))] + more_specs,
        out_specs=[dga_spec, pl.BlockSpec((128, 512), lambda i: (0, 0)), pl.BlockSpec((1, 512), lambda i: (0, 0))],
        out_shape=[dga_shape, jax.ShapeDtypeStruct((128, 512), F32), jax.ShapeDtypeStruct((1, 512), F32)],
        input_output_aliases={4: 0},
        compiler_params=_cparams("arbitrary"),
    )(dla, z, wa, ba, *more_args)


def _gla_masks(ch):
    ii = lax.broadcasted_iota(jnp.int32, (ch, ch), 0)
    tt = lax.broadcasted_iota(jnp.int32, (ch, ch), 1)
    return jnp.where(tt <= ii, 1.0, 0.0), jnp.where(tt >= ii, 1.0, 0.0)


def _running_sum(x, up):
    n = x.shape[0]
    rows = lax.broadcasted_iota(jnp.int32, x.shape, 0)
    k = 1
    while k < n:
        if up:
            x = x + jnp.where(rows < n - k, pltpu.roll(x, n - k, 0), 0.0)
        else:
            x = x + jnp.where(rows >= k, pltpu.roll(x, k, 0), 0.0)
        k *= 2
    return x


def _gla_chunk(d, tmat, qv, kv, lav, ch):
    c = _running_sum(lav, up=(d == 1))
    big_l = c[ch - 1:ch, :] if d == 0 else c[0:1, :]
    qt = qv * (GLA_DK ** -0.5) * jnp.exp(c)
    kt = kv * jnp.exp(-c)
    kh = kv * jnp.exp(big_l - c)
    return c, big_l, qt, kt, kh


def _gla_fwd(qh, kh_, z, la, *, name, rider=None):
    s = z.shape[0]
    ch = min(GLA_CHUNK, s)
    n = s // ch
    vcol = SEG["gv"][0] // GROUP_W

    def body(q0, k0, v0, la0, q1, k1, v1, la1, o0, o1, zs0, zs1, st):
        t = pl.program_id(0)

        @pl.when(t == 0)
        def _():
            st[...] = jnp.zeros_like(st)

        masks = _gla_masks(ch)
        for d, (q_ref, k_ref, v_ref, la_ref, o_ref, zs_ref) in enumerate(
                ((q0, k0, v0, la0, o0, zs0), (q1, k1, v1, la1, o1, zs1))):
            for h in range(GLA_HEADS):
                c, big_l, qt, kt, kh = _gla_chunk(d, masks[d], q_ref[h], k_ref[h], la_ref[0, h], ch)
                vv = v_ref[:, h * GLA_DV:(h + 1) * GLA_DV]
                p = _dot(qt, kt, 1, 1) * masks[d]
                zst = st[d, h]
                o_ref[:, h * GLA_DV:(h + 1) * GLA_DV] = _dot(p, vv) + _dot(qt, zst, 1, 1)
                zs_ref[h, 0] = zst
                st[d, h] = zst * jnp.exp(big_l) + _dot(vv, kh, 0, 0)

    cidx = (lambda t: t), (lambda t: n - 1 - t)
    hs = lambda d: pl.BlockSpec((GLA_HEADS, ch, GLA_DK), lambda t: (0, cidx[d](t), 0))
    vs = lambda d: pl.BlockSpec((ch, GROUP_W), lambda t: (cidx[d](t), vcol))
    las = lambda d: pl.BlockSpec((1, GLA_HEADS, ch, GLA_DK), lambda t: (d, 0, cidx[d](t), 0))
    os_ = lambda d: pl.BlockSpec((ch, GROUP_W), lambda t: (cidx[d](t), 0))
    zss = lambda d: pl.BlockSpec((GLA_HEADS, 1, GLA_DV, GLA_DK), lambda t: (0, cidx[d](t), 0, 0))
    (o0, o1, zs0, zs1), rode = _ride_call(
        body, rider, name=name, grid=(n,),
        in_specs=[hs(0), hs(0), vs(0), las(0), hs(1), hs(1), vs(1), las(1)],
        out_specs=[os_(0), os_(1), zss(0), zss(1)],
        out_shape=[jax.ShapeDtypeStruct((s, GROUP_W), F32)] * 2
        + [jax.ShapeDtypeStruct((GLA_HEADS, n, GLA_DV, GLA_DK), F32)] * 2,
        scratch_shapes=[pltpu.VMEM((2, GLA_HEADS, GLA_DV, GLA_DK), F32)],
        args=(qh, kh_, z, la, qh, kh_, z, la), sem=("arbitrary",))
    return ((o0, o1), (zs0, zs1)) if rider is None else ((o0, o1), (zs0, zs1), rode)


def _gla_bwd(qh, kh_, z, la, do, zs, *, name, rider=None):
    s = z.shape[0]
    ch = min(GLA_CHUNK, s)
    n = s // ch
    vcol = SEG["gv"][0] // GROUP_W

    def body(q0, k0, v0, la0, do0, zs0, q1, k1, v1, la1, do1, zs1,
             dq0, dk0, dla0, dv0, dq1, dk1, dla1, dv1, gz):
        t = pl.program_id(0)

        @pl.when(t == 0)
        def _():
            gz[...] = jnp.zeros_like(gz)

        masks = _gla_masks(ch)
        rows = lax.broadcasted_iota(jnp.int32, (ch, 1), 0)
        for d, (q_ref, k_ref, v_ref, la_ref, do_ref, zs_ref, dq_ref, dk_ref, dla_ref, dv_ref) in enumerate(
                ((q0, k0, v0, la0, do0, zs0, dq0, dk0, dla0, dv0), (q1, k1, v1, la1, do1, zs1, dq1, dk1, dla1, dv1))):
            tmat = masks[d]
            end = ch - 1 if d == 0 else 0
            for h in range(GLA_HEADS):
                c, big_l, qt, kt, kh = _gla_chunk(d, tmat, q_ref[h], k_ref[h], la_ref[0, h], ch)
                vsl = slice(h * GLA_DV, (h + 1) * GLA_DV)
                vv, dov, zst, gzv = v_ref[:, vsl], do_ref[:, vsl], zs_ref[h, 0], gz[d, h]
                p = _dot(qt, kt, 1, 1) * tmat
                dp = _dot(dov, vv, 1, 1) * tmat
                dqt = _dot(dp, kt) + _dot(dov, zst)
                dkt = _dot(dp, qt, 0, 0)
                dkh = _dot(vv, gzv)
                dv_ref[:, vsl] = _dot(p, dov, 0, 0) + _dot(kh, gzv, 1, 1)
                dq_ref[h] = dqt * jnp.exp(c) * (GLA_DK ** -0.5)
                dk_ref[h] = dkt * jnp.exp(-c) + dkh * jnp.exp(big_l - c)
                e_l = jnp.exp(big_l)
                d_l = jnp.sum(dkh * kh, axis=0, keepdims=True) + e_l * jnp.sum(zst * gzv, axis=0, keepdims=True)
                dc = dqt * qt - dkt * kt - dkh * kh + jnp.where(rows == end, d_l, 0.0)
                dla_ref[h] = _running_sum(dc, up=(d == 0))
                gz[d, h] = gzv * e_l + _dot(dov, qt, 0, 0)

    cidx = (lambda t: n - 1 - t), (lambda t: t)
    hs = lambda d: pl.BlockSpec((GLA_HEADS, ch, GLA_DK), lambda t: (0, cidx[d](t), 0))
    vs = lambda d: pl.BlockSpec((ch, GROUP_W), lambda t: (cidx[d](t), vcol))
    las = lambda d: pl.BlockSpec((1, GLA_HEADS, ch, GLA_DK), lambda t: (d, 0, cidx[d](t), 0))
    row = lambda d: pl.BlockSpec((ch, GROUP_W), lambda t: (cidx[d](t), 0))
    zss = lambda d: pl.BlockSpec((GLA_HEADS, 1, GLA_DV, GLA_DK), lambda t: (0, cidx[d](t), 0, 0))
    hshape = jax.ShapeDtypeStruct((GLA_HEADS, s, GLA_DK), F32)
    wide = jax.ShapeDtypeStruct((s, GROUP_W), F32)
    outs, rode = _ride_call(
        body, rider, name=name, grid=(n,),
        in_specs=[hs(0), hs(0), vs(0), las(0), row(0), zss(0), hs(1), hs(1), vs(1), las(1), row(1), zss(1)],
        out_specs=[hs(0), hs(0), hs(0), row(0), hs(1), hs(1), hs(1), row(1)],
        out_shape=[hshape, hshape, hshape, wide, hshape, hshape, hshape, wide],
        scratch_shapes=[pltpu.VMEM((2, GLA_HEADS, GLA_DV, GLA_DK), F32)],
        args=(qh, kh_, z, la, do, zs[0], qh, kh_, z, la, do, zs[1]), sem=("arbitrary",))
    dq0, dk0, dla0, dv0, dq1, dk1, dla1, dv1 = outs
    res = ((dq0, dq1), (dk0, dk1), (dla0, dla1), (dv0, dv1))
    return res if rider is None else res + (rode,)


def _window_sums(win, g, shift):
    n = win.shape[0]
    levels, y = [], win
    for j in range(POOL_GROUPS):
        y = y + pltpu.roll(y, n - (1 << j), 0)
        levels.append(y)
    sums = levels[-1]
    for j in range(POOL_GROUPS - 2, -1, -1):
        sums = jnp.where(g == j, levels[j], sums)
    return pltpu.roll(sums, shift, 0)


def _pool_cnt(t0, half, rows, s):
    t = t0 + lax.broadcasted_iota(jnp.int32, (rows, 1), 0)
    return (jnp.minimum(t + half, s) - jnp.maximum(t - half, 0)).astype(F32)


def _pool_fwd(z, pw, scale, *, name):
    s = z.shape[0]
    tl = min(POOL_TILE, s)
    nt = s // tl
    ucol, gcol = SEG["pv"][0] // 128, SEG["pg"][0] // 128

    def body(u_ref, gt_ref, pw_ref, sc_ref, y_ref, pad):
        g = pl.program_id(0)
        half = jnp.left_shift(1, g)
        pad[0:POOL_HALO, :] = jnp.zeros((POOL_HALO, POOL_GW), F32)
        pad[POOL_HALO + s:POOL_HALO + s + POOL_HALO, :] = jnp.zeros((POOL_HALO, POOL_GW), F32)
        pad[POOL_HALO:POOL_HALO + s, :] = u_ref[...]
        pwv, scv = pw_ref[0], sc_ref[...]

        def tile(i, carry):
            t0 = pl.multiple_of(i * tl, tl)
            win = pad[pl.ds(t0, tl + 2 * POOL_HALO), :]
            u = win[POOL_HALO:POOL_HALO + tl, :]
            pooled = _window_sums(win, g, half)[POOL_HALO:POOL_HALO + tl, :] / _pool_cnt(t0, half, tl, s) - u
            mixed = _dot(pooled, pwv)
            silu, _ = _silu_parts(gt_ref[pl.ds(t0, tl), :])
            y_ref[pl.ds(t0, tl), :] = _bf(silu * (mixed * scv))
            return carry

        lax.fori_loop(0, nt, tile, 0)

    return pl.pallas_call(
        body, name=name, grid=(POOL_GROUPS,),
        in_specs=[pl.BlockSpec((s, POOL_GW), lambda g: (0, ucol + g)),
                  pl.BlockSpec((s, POOL_GW), lambda g: (0, gcol + g)),
                  pl.BlockSpec((1, POOL_GW, POOL_GW), lambda g: (g, 0, 0)),
                  pl.BlockSpec((1, POOL_GW), lambda g: (0, g))],
        out_specs=pl.BlockSpec((s, POOL_GW), lambda g: (0, g)),
        out_shape=jax.ShapeDtypeStruct((s, GROUP_W), BF16),
        scratch_shapes=[pltpu.VMEM((s + 2 * POOL_HALO, POOL_GW), F32)],
        compiler_params=_cparams("parallel"),
    )(z, z, pw, scale)


def _pool_bwd(dy, z, pw, scale, *, name):
    s = z.shape[0]
    tl = min(POOL_TILE, s)
    nt = s // tl
    ucol, gcol, ycol = SEG["pv"][0] // 128, SEG["pg"][0] // 128, 2 * GROUP_W // 128

    def body(dy_ref, u_ref, gt_ref, pw_ref, sc_ref, du_ref, dgt_ref, dpw_ref, dsc_ref, pad, epad, dpo):
        g = pl.program_id(0)
        half = jnp.left_shift(1, g)
        zeros = jnp.zeros((POOL_HALO, POOL_GW), F32)
        for buf in (pad, epad):
            buf[0:POOL_HALO, :] = zeros
            buf[POOL_HALO + s:POOL_HALO + s + POOL_HALO, :] = zeros
        pad[POOL_HALO:POOL_HALO + s, :] = u_ref[...]
        pwv, scv = pw_ref[0], sc_ref[...]
        dpw_ref[0] = jnp.zeros((POOL_GW, POOL_GW), F32)
        dsc_ref[...] = jnp.zeros((1, POOL_GW), F32)

        def tile(i, carry):
            t0 = pl.multiple_of(i * tl, tl)
            win = pad[pl.ds(t0, tl + 2 * POOL_HALO), :]
            u = win[POOL_HALO:POOL_HALO + tl, :]
            cnt = _pool_cnt(t0, half, tl, s)
            pooled = _window_sums(win, g, half)[POOL_HALO:POOL_HALO + tl, :] / cnt - u
            mixed = _dot(pooled, pwv)
            silu, dsilu = _silu_parts(gt_ref[pl.ds(t0, tl), :])
            dyv = dy_ref[pl.ds(t0, tl), :]
            dgt_ref[pl.ds(t0, tl), :] = _bf(dyv * (mixed * scv) * dsilu)
            dsc_ref[...] += jnp.sum(dyv * silu * mixed, axis=0, keepdims=True)
            dm = dyv * silu * scv
            dpw_ref[0] += _dot(pooled, dm, 0, 0)
            dpooled = _dot(dm, pwv, 1, 1)
            dpo[pl.ds(t0, tl), :] = dpooled
            epad[pl.ds(POOL_HALO + t0, tl), :] = dpooled / cnt
            return carry

        lax.fori_loop(0, nt, tile, 0)

        def tile2(i, carry):
            t0 = pl.multiple_of(i * tl, tl)
            ewin = epad[pl.ds(t0, tl + 2 * POOL_HALO), :]
            du_ref[pl.ds(t0, tl), :] = _bf(_window_sums(ewin, g, half - 1)[POOL_HALO:POOL_HALO + tl, :]
                                           - dpo[pl.ds(t0, tl), :])
            return carry

        lax.fori_loop(0, nt, tile2, 0)

    col = lambda c0: pl.BlockSpec((s, POOL_GW), lambda g: (0, c0 + g))
    return pl.pallas_call(
        body, name=name, grid=(POOL_GROUPS,),
        in_specs=[col(ycol), col(ucol), col(gcol), pl.BlockSpec((1, POOL_GW, POOL_GW), lambda g: (g, 0, 0)),
                  pl.BlockSpec((1, POOL_GW), lambda g: (0, g))],
        out_specs=[col(0), col(0), pl.BlockSpec((1, POOL_GW, POOL_GW), lambda g: (g, 0, 0)),
                   pl.BlockSpec((1, POOL_GW), lambda g: (0, g))],
        out_shape=[jax.ShapeDtypeStruct((s, GROUP_W), BF16), jax.ShapeDtypeStruct((s, GROUP_W), BF16),
                   jax.ShapeDtypeStruct((POOL_GROUPS, POOL_GW, POOL_GW), F32),
                   jax.ShapeDtypeStruct((1, GROUP_W), F32)],
        scratch_shapes=[pltpu.VMEM((s + 2 * POOL_HALO, POOL_GW), F32), pltpu.VMEM((s + 2 * POOL_HALO, POOL_GW), F32),
                        pltpu.VMEM((s, POOL_GW), F32)],
        compiler_params=_cparams("parallel"),
    )(dy, z, z, pw, scale)


def _mla_specs(tm):
    zq = pl.BlockSpec((tm, 512), lambda i: (i, SEG["mq"][0] // 512))
    zkv = pl.BlockSpec((tm, 256), lambda i: (i, SEG["mkv"][0] // 256))
    zkr = pl.BlockSpec((tm, 128), lambda i: (i, SEG["mkr"][0] // 128))
    full = lambda r, c: pl.BlockSpec((r, c), lambda i: (0, 0))
    tab = pl.BlockSpec((tm, 128), lambda i: (i, 0))
    weights = [full(1, 512), full(512, 1024), full(1, 256), full(256, 1024), full(1, 256), full(1, 256)]
    return [zq, zkv, zkr] + weights + [tab, tab, tab]


def _mla_project(xq_ref, xkv_ref, qg_ref, wq_ref, kvg_ref, wkv_ref):
    xq = xq_ref[...]
    r1 = lax.rsqrt(jnp.mean(xq * xq, axis=-1, keepdims=True) + EPS)
    xn1 = xq * r1
    qn = _bf(xn1 * qg_ref[...])
    qraw = _dot(qn, wq_ref[...])
    xkv = xkv_ref[...]
    r2 = lax.rsqrt(jnp.mean(xkv * xkv, axis=-1, keepdims=True) + EPS)
    xn2 = xkv * r2
    kvn = _bf(xn2 * kvg_ref[...])
    kvraw = _dot(kvn, wkv_ref[...])
    return r1, xn1, qn, qraw, r2, xn2, kvn, kvraw


def _mla_pre(z, qg, wq, kvg, wkv, qng, kng, cos, sp, sn, *, name, tm=ROW_TILE):
    s = z.shape[0]
    tm = min(tm, s)

    def body(xq_ref, xkv_ref, pe_ref, qg_ref, wq_ref, kvg_ref, wkv_ref, qng_ref, kng_ref, c_ref, sp_ref, sn_ref,
             q_ref, k_ref, v_ref):
        _, _, _, qraw, _, _, _, kvraw = _mla_project(xq_ref, xkv_ref, qg_ref, wq_ref, kvg_ref, wkv_ref)
        c, spv, snv = c_ref[...], sp_ref[...], sn_ref[...]
        pe = pe_ref[...]
        pe_ss = jnp.sum(pe * pe, axis=-1, keepdims=True)
        qngv, kngv = qng_ref[...], kng_ref[...]
        for h in range(MLA_HEADS):
            b = h * MLA_QKP
            qh = qraw[:, b:b + MLA_QKP]
            r = lax.rsqrt(jnp.sum(qh * qh, axis=-1, keepdims=True) * (1.0 / MLA_QK) + EPS)
            qn_h = qh * r * qngv
            q_ref[:, b:b + 128] = _bf(qn_h[:, :128] * MLA_SCALE)
            q_ref[:, b + 128:b + 256] = _bf(_rope64(qn_h[:, 128:], c, spv, snv) * MLA_SCALE)
            kn = kvraw[:, b:b + 128]
            rk = lax.rsqrt((jnp.sum(kn * kn, axis=-1, keepdims=True) + pe_ss) * (1.0 / MLA_QK) + EPS)
            k_ref[:, b:b + 128] = _bf(kn * rk * kngv[:, :128])
            k_ref[:, b + 128:b + 256] = _bf(_rope64(pe * rk * kngv[:, 128:], c, spv, snv))
            v_ref[:, h * MLA_V:(h + 1) * MLA_V] = _bf(kvraw[:, b + 128:b + 256])

    row = lambda w: pl.BlockSpec((tm, w), lambda i: (i, 0))
    return pl.pallas_call(
        body, name=name, grid=(s // tm,), in_specs=_mla_specs(tm),
        out_specs=[row(1024), row(1024), row(512)],
        out_shape=[jax.ShapeDtypeStruct((s, 1024), BF16), jax.ShapeDtypeStruct((s, 1024), BF16),
                   jax.ShapeDtypeStruct((s, 512), BF16)],
        compiler_params=_cparams("parallel"),
    )(z, z, z, qg, wq, kvg, wkv, qng, kng, cos, sp, sn)


def _mla_pre_bwd(dq, dk, dv, z, qg, wq, kvg, wkv, qng, kng, cos, sp, sn, *, name, tm=ROW_TILE):
    s = z.shape[0]
    tm = min(tm, s)

    def body(dq_ref, dk_ref, dv_ref, xq_ref, xkv_ref, pe_ref, qg_ref, wq_ref, kvg_ref, wkv_ref, qng_ref, kng_ref,
             c_ref, sp_ref, sn_ref, dxq_ref, dxkv_ref, dpe_ref, dwq_ref, dwkv_ref, dqg_ref, dkvg_ref, dqng_ref,
             dkng_ref, dqraw, dkvraw):
        i = pl.program_id(0)
        r1, xn1, qn, qraw, r2, xn2, kvn, kvraw = _mla_project(xq_ref, xkv_ref, qg_ref, wq_ref, kvg_ref, wkv_ref)
        c, spv, snv = c_ref[...], sp_ref[...], sn_ref[...]
        pe = pe_ref[...]
        pe_ss = jnp.sum(pe * pe, axis=-1, keepdims=True)
        qngv, kngv = qng_ref[...], kng_ref[...]
        dqng = jnp.zeros((1, MLA_QKP), F32)
        dkng = jnp.zeros((1, MLA_QKP), F32)
        dpe = jnp.zeros_like(pe)
        for h in range(MLA_HEADS):
            b = h * MLA_QKP
            qh = qraw[:, b:b + MLA_QKP]
            r = lax.rsqrt(jnp.sum(qh * qh, axis=-1, keepdims=True) * (1.0 / MLA_QK) + EPS)
            xn = qh * r
            d_n = jnp.concatenate(
                [dq_ref[:, b:b + 128], _unrope64(dq_ref[:, b + 128:b + 256], c, spv, snv)], axis=1) * MLA_SCALE
            dqng = dqng + jnp.sum(d_n * xn, axis=0, keepdims=True)
            dxn = d_n * qngv
            dqraw[:, b:b + MLA_QKP] = _bf(r * (dxn - xn * (jnp.sum(dxn * xn, axis=-1, keepdims=True) * (1.0 / MLA_QK))))
            kn = kvraw[:, b:b + 128]
            rk = lax.rsqrt((jnp.sum(kn * kn, axis=-1, keepdims=True) + pe_ss) * (1.0 / MLA_QK) + EPS)
            xk = jnp.concatenate([kn, pe], axis=1) * rk
            d_k = jnp.concatenate(
                [dk_ref[:, b:b + 128], _unrope64(dk_ref[:, b + 128:b + 256], c, spv, snv)], axis=1)
            dkng = dkng + jnp.sum(d_k * xk, axis=0, keepdims=True)
            dxk = d_k * kngv
            dfull = rk * (dxk - xk * (jnp.sum(dxk * xk, axis=-1, keepdims=True) * (1.0 / MLA_QK)))
            dkvraw[:, b:b + 128] = _bf(dfull[:, :128])
            dkvraw[:, b + 128:b + 256] = _bf(dv_ref[:, h * MLA_V:(h + 1) * MLA_V])
            dpe = dpe + dfull[:, 128:]
        dpe_ref[...] = _bf(dpe)
        dqr, dkvr = dqraw[...], dkvraw[...]
        dqn = _dot(dqr, wq_ref[...], 1, 1)
        dxn1 = dqn * qg_ref[...]
        dxq_ref[...] = _bf(r1 * (dxn1 - xn1 * jnp.mean(dxn1 * xn1, axis=-1, keepdims=True)))
        dkvn = _dot(dkvr, wkv_ref[...], 1, 1)
        dxn2 = dkvn * kvg_ref[...]
        dxkv_ref[...] = _bf(r2 * (dxn2 - xn2 * jnp.mean(dxn2 * xn2, axis=-1, keepdims=True)))
        parts = (_dot(qn, dqr, 0, 0), _dot(kvn, dkvr, 0, 0), jnp.sum(dqn * xn1, axis=0, keepdims=True),
                 jnp.sum(dkvn * xn2, axis=0, keepdims=True), dqng, dkng)
        accs = (dwq_ref, dwkv_ref, dqg_ref, dkvg_ref, dqng_ref, dkng_ref)

        @pl.when(i == 0)
        def _():
            for a, p in zip(accs, parts):
                a[...] = p

        @pl.when(i > 0)
        def _():
            for a, p in zip(accs, parts):
                a[...] += p

    row = lambda w: pl.BlockSpec((tm, w), lambda i: (i, 0))
    full = lambda r, c: pl.BlockSpec((r, c), lambda i: (0, 0))
    return pl.pallas_call(
        body, name=name, grid=(s // tm,),
        in_specs=[row(1024), row(1024), row(512)] + _mla_specs(tm),
        out_specs=[row(512), row(256), row(128), full(512, 1024), full(256, 1024), full(1, 512), full(1, 256),
                   full(1, 256), full(1, 256)],
        out_shape=[jax.ShapeDtypeStruct((s, 512), BF16), jax.ShapeDtypeStruct((s, 256), BF16),
                   jax.ShapeDtypeStruct((s, 128), BF16), jax.ShapeDtypeStruct((512, 1024), F32),
                   jax.ShapeDtypeStruct((256, 1024), F32), jax.ShapeDtypeStruct((1, 512), F32),
                   jax.ShapeDtypeStruct((1, 256), F32), jax.ShapeDtypeStruct((1, 256), F32),
                   jax.ShapeDtypeStruct((1, 256), F32)],
        scratch_shapes=[pltpu.VMEM((tm, 1024), BF16), pltpu.VMEM((tm, 1024), BF16)],
        compiler_params=_cparams("arbitrary"),
    )(dq, dk, dv, z, z, z, qg, wq, kvg, wkv, qng, kng, cos, sp, sn)


def _flash_fwd(q, k, v, *, name, tq=1024, tk=1024, rider=None):
    s = q.shape[0]
    tq, tk = min(tq, s), min(tk, s)
    nk = s // tk

    def body(q_ref, k_ref, v_ref, o_ref, lse_ref, m_s, l_s, acc):
        j = pl.program_id(2)

        @pl.when(j == 0)
        def _():
            m_s[...] = jnp.full_like(m_s, -jnp.inf)
            l_s[...] = jnp.zeros_like(l_s)
            acc[...] = jnp.zeros_like(acc)

        sc = _dot(q_ref[...], k_ref[...], 1, 1)
        m_prev = m_s[...]
        m_new = jnp.maximum(m_prev, jnp.max(sc, axis=-1, keepdims=True))
        p = jnp.exp(sc - m_new[:, 0:1])
        alpha = jnp.exp(m_prev - m_new)
        l_s[...] = alpha * l_s[...] + jnp.sum(p, axis=-1, keepdims=True)
        acc[...] = alpha * acc[...] + _dot(p, v_ref[...])
        m_s[...] = m_new

        @pl.when(j == nk - 1)
        def _():
            o_ref[...] = acc[...] / l_s[...]
            lse_ref[...] = m_s[...] + jnp.log(l_s[...])

    (o, lse), rode = _ride_call(
        body, rider, name=name, grid=(MLA_HEADS, s // tq, nk),
        in_specs=[pl.BlockSpec((tq, MLA_QKP), lambda h, i, j: (i, h)),
                  pl.BlockSpec((tk, MLA_QKP), lambda h, i, j: (j, h)),
                  pl.BlockSpec((tk, MLA_V), lambda h, i, j: (j, h))],
        out_specs=[pl.BlockSpec((tq, MLA_V), lambda h, i, j: (i, h))] * 2,
        out_shape=[jax.ShapeDtypeStruct((s, GROUP_W), F32)] * 2,
        scratch_shapes=[pltpu.VMEM((tq, MLA_V), F32), pltpu.VMEM((tq, MLA_V), F32), pltpu.VMEM((tq, MLA_V), F32)],
        args=(q, k, v), sem=("parallel", "parallel", "arbitrary"))
    return (o, lse) if rider is None else (o, lse, rode)


def _flash_bwd(q, k, v, do, o, lse, *, name, tq=1024, tk=1024, rider=None):
    s = q.shape[0]
    tq, tk = min(tq, s), min(tk, s)
    nq, nk = s // tq, s // tk

    def body(q_ref, k_ref, v_ref, do_ref, o_ref, lse_ref, dq_ref, dk_ref, dv_ref, dk_acc, dv_acc):
        j, i = pl.program_id(1), pl.program_id(2)
        dov = do_ref[...]
        delta = jnp.sum(dov * o_ref[...], axis=-1, keepdims=True)
        p = jnp.exp(_dot(q_ref[...], k_ref[...], 1, 1) - lse_ref[:, 0:1])
        ds = p * (_dot(dov, v_ref[...], 1, 1) - delta)
        pv = _dot(p, dov, 0, 0)
        pk = _dot(ds, q_ref[...], 0, 0)
        pq = _dot(ds, k_ref[...])
        rows = pl.ds(pl.multiple_of(i * tq, tq), tq)

        @pl.when(j == 0)
        def _():
            dq_ref[rows, :] = pq

        @pl.when(j > 0)
        def _():
            dq_ref[rows, :] += pq

        @pl.when(i == 0)
        def _():
            dv_acc[...] = pv
            dk_acc[...] = pk

        @pl.when(i > 0)
        def _():
            dv_acc[...] += pv
            dk_acc[...] += pk

        @pl.when(i == nq - 1)
        def _():
            dk_ref[...] = dk_acc[...]
            dv_ref[...] = dv_acc[...]

    qb = pl.BlockSpec((tq, MLA_QKP), lambda h, j, i: (i, h))
    kb = pl.BlockSpec((tk, MLA_QKP), lambda h, j, i: (j, h))
    vb = pl.BlockSpec((tk, MLA_V), lambda h, j, i: (j, h))
    ob = pl.BlockSpec((tq, MLA_V), lambda h, j, i: (i, h))
    (dq, dk, dv), rode = _ride_call(
        body, rider, name=name, grid=(MLA_HEADS, nk, nq),
        in_specs=[qb, kb, vb, ob, ob, ob],
        out_specs=[pl.BlockSpec((s, MLA_QKP), lambda h, j, i: (0, h)), kb, vb],
        out_shape=[jax.ShapeDtypeStruct((s, MLA_HEADS * MLA_QKP), F32),
                   jax.ShapeDtypeStruct((s, MLA_HEADS * MLA_QKP), F32), jax.ShapeDtypeStruct((s, GROUP_W), F32)],
        scratch_shapes=[pltpu.VMEM((tk, MLA_QKP), F32), pltpu.VMEM((tk, MLA_V), F32)],
        args=(q, k, v, do, o, lse), sem=("arbitrary", "arbitrary", "arbitrary"))
    return (dq, dk, dv) if rider is None else (dq, dk, dv, rode)


def _rows_tile(r, c, itemsize=4, budget=2 * 1024 * 1024):
    if r * c * itemsize <= budget:
        return r
    best = None
    for t in range(8, r, 8):
        if r % t == 0 and t * c * itemsize <= budget:
            best = t
    return best if best is not None else r


def _landing(into, tm, width):
    buf, col = into
    assert col % width == 0
    return (jax.ShapeDtypeStruct(buf.shape, buf.dtype), pl.BlockSpec((tm, width), lambda i: (i, col // width)),
            [ANY], [buf])


def _add_n(arrs, *, out_dtype=F32, name, into=None):
    shape = arrs[0].shape
    c = shape[-1]
    flat = [a.reshape(-1, c) for a in arrs]
    r = flat[0].shape[0]
    t = _rows_tile(r, c)
    n_in = len(flat)

    def body(*refs):
        acc = refs[0][...].astype(F32)
        for ref in refs[1:n_in]:
            acc = acc + ref[...].astype(F32)
        refs[-1][...] = acc.astype(out_dtype)

    blk = pl.BlockSpec((t, c), lambda i: (i, 0))
    if into is not None:
        out_shape, out_spec, more_specs, more_args = _landing(into, t, c)
        return pl.pallas_call(
            body, name=name, grid=(r // t,), in_specs=[blk] * n_in + more_specs, out_specs=out_spec,
            out_shape=out_shape, input_output_aliases={n_in: 0}, compiler_params=_cparams("parallel"),
        )(*flat, *more_args)
    out = pl.pallas_call(
        body, name=name, grid=(r // t,), in_specs=[blk] * n_in, out_specs=blk,
        out_shape=jax.ShapeDtypeStruct((r, c), out_dtype), compiler_params=_cparams("parallel"),
    )(*flat)
    return out.reshape(shape)


def _adamw(w, g, m, v, *, name):
    shape = w.shape
    c = shape[-1]
    flat = [a.reshape(-1, c) for a in (w, g, m, v)]
    r = flat[0].shape[0]
    t = _rows_tile(r, c, budget=1024 * 1024)

    def body(w_ref, g_ref, m_ref, v_ref, d_ref, mo_ref, vo_ref):
        gv = g_ref[...]
        m2 = ADAM_B1 * m_ref[...] + (1.0 - ADAM_B1) * gv
        v2 = ADAM_B2 * v_ref[...] + (1.0 - ADAM_B2) * (gv * gv)
        m_hat = m2 / (1.0 - ADAM_B1 ** ADAM_STEP)
        v_hat = v2 / (1.0 - ADAM_B2 ** ADAM_STEP)
        d_ref[...] = -ADAM_LR * (m_hat / (jnp.sqrt(v_hat) + ADAM_EPS) + ADAM_WD * w_ref[...])
        mo_ref[...] = m2
        vo_ref[...] = v2

    blk = pl.BlockSpec((t, c), lambda i: (i, 0))
    outs = pl.pallas_call(
        body, name=name, grid=(r // t,), in_specs=[blk] * 4, out_specs=[blk] * 3,
        out_shape=[jax.ShapeDtypeStruct((r, c), F32)] * 3, compiler_params=_cparams("parallel"),
    )(*flat)
    return tuple(o.reshape(shape) for o in outs)


def _place():
    x, y, c = lax.axis_index("x"), lax.axis_index("y"), lax.axis_index("c")
    chips = [(1 - x, y), (x, 1 - y), (1 - x, 1 - y)]
    return x, y, c, chips


ANY = pl.BlockSpec(memory_space=pl.ANY)


def _half(ref, axis, hc, lead=()):
    n = ref.shape[len(lead) + axis] // 2
    return ref.at[tuple(lead) + (slice(None),) * axis + (pl.ds(hc * n, n),)]


def _gather_shards(shards, axes, *, name):
    nt = len(shards)

    def body(*refs):
        src, dst = refs[:nt], refs[nt:2 * nt]
        send, recv, fsend, frecv, lsem = refs[2 * nt:]
        x, y, c, chips = _place()
        me = 2 * x + y
        local = [pltpu.make_async_copy(src[t], dst[t].at[me], lsem.at[t]) for t in range(nt)]
        for cp in local:
            cp.start()

        def half(t, slot, hc):
            return _half(dst[t], axes[t], hc, lead=(slot,))

        def first(t, k):
            return pltpu.make_async_remote_copy(
                src_ref=_half(src[t], axes[t], c), dst_ref=half(t, me, c),
                send_sem=send.at[t, k], recv_sem=recv.at[t, k],
                device_id=(chips[k][0], chips[k][1], c), device_id_type=MESH)

        def landed(t, k):
            slot = 2 * chips[k][0] + chips[k][1]
            return pltpu.make_async_remote_copy(
                src_ref=half(t, slot, c), dst_ref=half(t, slot, c),
                send_sem=send.at[t, k], recv_sem=recv.at[t, k],
                device_id=(chips[k][0], chips[k][1], c), device_id_type=MESH)

        def forward(t, k, hc):
            slot = 2 * chips[k][0] + chips[k][1]
            return pltpu.make_async_remote_copy(
                src_ref=half(t, slot, hc), dst_ref=half(t, slot, hc),
                send_sem=fsend.at[t, k], recv_sem=frecv.at[t, k],
                device_id=(x, y, 1 - c), device_id_type=MESH)

        for t in range(nt):
            for k in range(3):
                first(t, k).start()
        for t in range(nt):
            for k in range(3):
                landed(t, k).wait_recv()
                forward(t, k, c).start()
        for t in range(nt):
            for k in range(3):
                forward(t, k, 1 - c).wait_recv()
        for t in range(nt):
            for k in range(3):
                first(t, k).wait_send()
                forward(t, k, c).wait_send()
        for cp in local:
            cp.wait()

    return pl.pallas_call(
        body, name=name, in_specs=[ANY] * nt, out_specs=[ANY] * nt,
        out_shape=[jax.ShapeDtypeStruct((N_CHIP,) + a.shape, a.dtype) for a in shards],
        scratch_shapes=[pltpu.SemaphoreType.DMA((nt, 3)), pltpu.SemaphoreType.DMA((nt, 3)),
                        pltpu.SemaphoreType.DMA((nt, 3)), pltpu.SemaphoreType.DMA((nt, 3)),
                        pltpu.SemaphoreType.DMA((nt,))],
    )(*shards)


def _comm_rows(hr, c, budget=2 * 1024 * 1024):
    if hr * c * 4 <= budget:
        return hr
    best = None
    for t in range(16, hr, 16):
        if hr % t == 0 and t * c * 4 <= budget:
            best = t
    return best if best is not None else hr


def _comm_cols(r, hc, budget=2 * 1024 * 1024):
    best = 128
    for t in range(128, hc + 1, 128):
        if hc % t == 0 and r * t * 4 <= budget:
            best = t
    return best


def _comm_chunks(shape, axis):
    r, cdim = shape
    if axis == 0:
        rc = _comm_rows(r // 2, cdim)
        nt = (r // 2) // rc
        return (rc, cdim), nt, (lambda h, t: (h * nt + t, 0))
    cc = _comm_cols(r, cdim // 2)
    nt = (cdim // 2) // cc
    return (r, cc), nt, (lambda h, t: (0, h * nt + t))


def _pair_reduce(g, where, axis, *, out_dtype, name):
    n_slot, r, cdim = g.shape
    blk_shape, nr, at = _comm_chunks((r, cdim), axis)
    steps = n_slot * nr
    half_shape = (r // 2, cdim) if axis == 0 else (r, cdim // 2)

    def body(w_ref, a_ref, b_ref, o_ref, land, send, recv, credit):
        x, y, c, _ = _place()
        sib = (x, y, 1 - c)
        i = pl.program_id(0) * nr + pl.program_id(1)
        s = lax.rem(i, 2)

        @pl.when(i >= 2)
        def _():
            pl.semaphore_wait(credit.at[s], 1)

        cp = pltpu.make_async_remote_copy(src_ref=b_ref.at[0], dst_ref=land.at[s], send_sem=send.at[s],
                                          recv_sem=recv.at[s], device_id=sib, device_id_type=MESH)
        cp.start()
        cp.wait_recv()
        o_ref[0] = (a_ref[0] + land[s]).astype(out_dtype)
        cp.wait_send()

        @pl.when(i + 2 < steps)
        def _():
            pl.semaphore_signal(credit.at[s], inc=1, device_id=sib, device_id_type=MESH)

    blk = lambda half: pl.BlockSpec((1,) + blk_shape, lambda j, t, w: (j,) + at(half(w), t))
    grid_spec = pltpu.PrefetchScalarGridSpec(
        num_scalar_prefetch=1, grid=(n_slot, nr),
        in_specs=[blk(lambda w: w[0]), blk(lambda w: 1 - w[0])],
        out_specs=pl.BlockSpec((1,) + blk_shape, lambda j, t, w: (j,) + at(0, t)),
        scratch_shapes=[pltpu.VMEM((2,) + blk_shape, F32), pltpu.SemaphoreType.DMA((2,)),
                        pltpu.SemaphoreType.DMA((2,)), pltpu.SemaphoreType.REGULAR((2,))])
    return pl.pallas_call(
        body, name=name, grid_spec=grid_spec, out_shape=jax.ShapeDtypeStruct((n_slot,) + half_shape, out_dtype),
        compiler_params=_cparams("arbitrary", "arbitrary"),
    )(where, g, g)


def _chip_exchange(parts, *, name):
    nt = len(parts)

    def body(*refs):
        src, got = refs[:nt], refs[nt:2 * nt]
        send, recv = refs[2 * nt:]
        x, y, c, chips = _place()
        remote = []
        for t in range(nt):
            for k in range(3):
                remote.append(pltpu.make_async_remote_copy(
                    src_ref=src[t].at[2 * chips[k][0] + chips[k][1]], dst_ref=got[t].at[k],
                    send_sem=send.at[t, k], recv_sem=recv.at[t, k],
                    device_id=(chips[k][0], chips[k][1], c), device_id_type=MESH))
        for cp in remote:
            cp.start()
        for cp in remote:
            cp.wait_recv()
        for cp in remote:
            cp.wait_send()

    return pl.pallas_call(
        body, name=name, in_specs=[ANY] * nt, out_specs=[ANY] * nt,
        out_shape=[jax.ShapeDtypeStruct((3,) + a.shape[1:], a.dtype) for a in parts],
        scratch_shapes=[pltpu.SemaphoreType.DMA((nt, 3)), pltpu.SemaphoreType.DMA((nt, 3))],
    )(*parts)


def _sum_join(p, got, where, axis, *, name):
    _, hr, cdim = p.shape
    full = (2 * hr, cdim) if axis == 0 else (hr, 2 * cdim)
    blk_shape, n, at = _comm_chunks(full, axis)
    step_len = blk_shape[axis]
    half_len = full[axis] // 2

    def body(w_ref, p_ref, g_ref, out, buf, lsem, ssem, rsem):
        x, y, c, _ = _place()
        sib = (x, y, 1 - c)
        r = pl.program_id(0)

        def part(start, size):
            return out.at[(slice(None),) * axis + (pl.ds(start, size),)]

        def copies(step, slot):
            rows = part(pl.multiple_of(c * half_len + step * step_len, 8 if axis == 0 else 128), step_len)
            return (pltpu.make_async_copy(buf.at[slot], rows, lsem.at[slot]),
                    pltpu.make_async_remote_copy(src_ref=buf.at[slot], dst_ref=rows, send_sem=ssem.at[slot],
                                                 recv_sem=rsem, device_id=sib, device_id_type=MESH))

        s = lax.rem(r, 2)

        @pl.when(r >= 2)
        def _():
            lc, rm = copies(r - 2, s)
            lc.wait()
            rm.wait_send()

        buf[s] = p_ref[0].astype(F32) + g_ref[0].astype(F32) + g_ref[1].astype(F32) + g_ref[2].astype(F32)
        lc, rm = copies(r, s)
        lc.start()
        rm.start()

        @pl.when(r == n - 1)
        def _():
            for step in range(max(0, n - 2), n):
                lc, rm = copies(step, step % 2)
                lc.wait()
                rm.wait_send()
            whole = part(0, half_len)
            pltpu.make_async_remote_copy(src_ref=whole, dst_ref=whole, send_sem=ssem.at[0], recv_sem=rsem,
                                         device_id=sib, device_id_type=MESH).wait_recv()

    grid_spec = pltpu.PrefetchScalarGridSpec(
        num_scalar_prefetch=1, grid=(n,),
        in_specs=[pl.BlockSpec((1,) + blk_shape, lambda t, w: (w[1],) + at(0, t)),
                  pl.BlockSpec((3,) + blk_shape, lambda t, w: (0,) + at(0, t))],
        out_specs=ANY,
        scratch_shapes=[pltpu.VMEM((2,) + blk_shape, F32), pltpu.SemaphoreType.DMA((2,)),
                        pltpu.SemaphoreType.DMA((2,)), pltpu.SemaphoreType.DMA])
    return pl.pallas_call(
        body, name=name, grid_spec=grid_spec, out_shape=jax.ShapeDtypeStruct(full, F32),
        compiler_params=_cparams("arbitrary"),
    )(where, p, got)


def _rider_gather_send(shards, axes):
    nt = len(shards)

    def copies(src, dst, send, recv, lsem):
        x, y, c, chips = _place()
        me = 2 * x + y
        local = [pltpu.make_async_copy(src[t], dst[t].at[me], lsem.at[t]) for t in range(nt)]
        out, landed = [], []
        for t in range(nt):
            for k in range(3):
                peer = (chips[k][0], chips[k][1], c)
                out.append(pltpu.make_async_remote_copy(
                    src_ref=_half(src[t], axes[t], c), dst_ref=_half(dst[t], axes[t], c, lead=(me,)),
                    send_sem=send.at[t, k], recv_sem=recv.at[t, k], device_id=peer, device_id_type=MESH))
                theirs = _half(dst[t], axes[t], c, lead=(2 * chips[k][0] + chips[k][1],))
                landed.append(pltpu.make_async_remote_copy(
                    src_ref=theirs, dst_ref=theirs, send_sem=send.at[t, k], recv_sem=recv.at[t, k],
                    device_id=peer, device_id_type=MESH))
        return local, out, landed

    def start(src, dst, sems):
        local, out, _ = copies(src, dst, *sems)
        for cp in local + out:
            cp.start()

    def finish(src, dst, sems):
        local, out, landed = copies(src, dst, *sems)
        for cp in landed:
            cp.wait_recv()
        for cp in out:
            cp.wait_send()
        for cp in local:
            cp.wait()

    return _Rider(shards, [jax.ShapeDtypeStruct((N_CHIP,) + a.shape, a.dtype) for a in shards],
                  [pltpu.SemaphoreType.DMA((nt, 3)), pltpu.SemaphoreType.DMA((nt, 3)), pltpu.SemaphoreType.DMA((nt,))],
                  start, finish)


def _rider_gather_forward(bufs, axes):
    nt = len(bufs)

    def copies(src, dst, send, recv):
        x, y, c, chips = _place()
        mine, theirs = [], []
        for t in range(nt):
            for k in range(3):
                slot = 2 * chips[k][0] + chips[k][1]
                for hc, into in ((c, mine), (1 - c, theirs)):
                    into.append(pltpu.make_async_remote_copy(
                        src_ref=_half(src[t], axes[t], hc, lead=(slot,)),
                        dst_ref=_half(dst[t], axes[t], hc, lead=(slot,)),
                        send_sem=send.at[t, k], recv_sem=recv.at[t, k], device_id=(x, y, 1 - c), device_id_type=MESH))
        return mine, theirs

    def start(src, dst, sems):
        for cp in copies(src, dst, *sems)[0]:
            cp.start()

    def finish(src, dst, sems):
        mine, theirs = copies(src, dst, *sems)
        for cp in theirs:
            cp.wait_recv()
        for cp in mine:
            cp.wait_send()

    return _Rider(bufs, [jax.ShapeDtypeStruct(a.shape, a.dtype) for a in bufs],
                  [pltpu.SemaphoreType.DMA((nt, 3)), pltpu.SemaphoreType.DMA((nt, 3))], start, finish,
                  aliases={t: t for t in range(nt)})


def _rider_chip_exchange(parts):
    nt = len(parts)

    def copies(src, got, send, recv):
        x, y, c, chips = _place()
        return [pltpu.make_async_remote_copy(
            src_ref=src[t].at[2 * chips[k][0] + chips[k][1]], dst_ref=got[t].at[k], send_sem=send.at[t, k],
            recv_sem=recv.at[t, k], device_id=(chips[k][0], chips[k][1], c), device_id_type=MESH)
            for t in range(nt) for k in range(3)]

    def start(src, got, sems):
        for cp in copies(src, got, *sems):
            cp.start()

    def finish(src, got, sems):
        remote = copies(src, got, *sems)
        for cp in remote:
            cp.wait_recv()
        for cp in remote:
            cp.wait_send()

    return _Rider(parts, [jax.ShapeDtypeStruct((3,) + a.shape[1:], a.dtype) for a in parts],
                  [pltpu.SemaphoreType.DMA((nt, 3)), pltpu.SemaphoreType.DMA((nt, 3))], start, finish)


def _gather_all(block, *, name):
    m_per, n = block.shape

    def body(x_ref, out_ref, send_sems, recv_sems, local_sem):
        x, y, c, chips = _place()
        me, sibling = (x, y, c), (x, y, 1 - c)

        def rows(px, py, pc):
            return out_ref.at[4 * px + 2 * py + pc]

        def copy(k, blk, to, src=None):
            return pltpu.make_async_remote_copy(
                src_ref=rows(*blk) if src is None else src, dst_ref=rows(*blk),
                send_sem=send_sems.at[k], recv_sem=recv_sems.at[k], device_id=to, device_id_type=MESH)

        mine = pltpu.make_async_copy(x_ref, rows(*me), local_sem)
        mine.start()
        first = [copy(0, me, sibling, src=x_ref)]
        first += [copy(1 + j, me, (*chip, c), src=x_ref) for j, chip in enumerate(chips)]
        for cp in first:
            cp.start()
        passed = [copy(4 + j, (*chip, c), sibling) for j, chip in enumerate(chips)]
        for j, chip in enumerate(chips):
            copy(1 + j, (*chip, c), me).wait_recv()
            passed[j].start()
        copy(0, sibling, me).wait_recv()
        for j, chip in enumerate(chips):
            copy(4 + j, (*chip, 1 - c), me).wait_recv()
        for cp in first + passed:
            cp.wait_send()
        mine.wait()

    return pl.pallas_call(
        body, name=name,
        out_shape=jax.ShapeDtypeStruct((N_DEV, m_per, n), block.dtype),
        in_specs=[pl.BlockSpec(memory_space=pltpu.VMEM)], out_specs=pl.BlockSpec(memory_space=pltpu.VMEM),
        scratch_shapes=[pltpu.SemaphoreType.DMA((7,)), pltpu.SemaphoreType.DMA((7,)), pltpu.SemaphoreType.DMA],
        compiler_params=pltpu.CompilerParams(vmem_limit_bytes=VMEM_LIMIT),
    )(block)


def _sum_slots(slots, *, name):
    n, m, c = slots.shape
    t = _rows_tile(m, c * n)

    def body(s_ref, o_ref):
        acc = s_ref[0]
        for k in range(1, n):
            acc = acc + s_ref[k]
        o_ref[...] = acc

    return pl.pallas_call(
        body, name=name, grid=(m // t,), in_specs=[pl.BlockSpec((n, t, c), lambda i: (0, i, 0))],
        out_specs=pl.BlockSpec((t, c), lambda i: (i, 0)), out_shape=jax.ShapeDtypeStruct((m, c), F32),
        compiler_params=_cparams("parallel"),
    )(slots)


def _pad_rows(a, rows):
    return a if a.shape[0] == rows else jnp.pad(a, ((0, rows - a.shape[0]), (0, 0)))


def _w_in_padded(shards):
    full = shards.reshape(IN_COLS, shards.shape[2])
    return jnp.concatenate([_pad_rows(full[SEG[n][2]:SEG[n][2] + SEG[n][3]], SEG[n][1]) for n in SEG_ORDER], axis=0)


def _w_in_unpadded(gp):
    full = jnp.concatenate([gp[SEG[n][0]:SEG[n][0] + SEG[n][3]] for n in ORIG_ORDER], axis=0)
    return full.reshape(N_CHIP, IN_COLS // N_CHIP, gp.shape[1])


def _pad_heads(w, true_w, pad_w):
    r = w.shape[0]
    h = w.shape[1] // true_w
    return jnp.pad(w.reshape(r, h, true_w), ((0, 0), (0, 0), (0, pad_w - true_w))).reshape(r, h * pad_w)


def _unpad_heads(w, true_w, pad_w):
    r = w.shape[0]
    h = w.shape[1] // pad_w
    return w.reshape(r, h, pad_w)[:, :, :true_w].reshape(r, h * true_w)


def _cols_to_slots(a):
    return a.reshape(a.shape[0], N_CHIP, a.shape[1] // N_CHIP).transpose(1, 0, 2)


def _slots_to_cols(a):
    return jnp.concatenate([a[j] for j in range(N_CHIP)], axis=1)


def _to_heads(a, h, d):
    return a.reshape(a.shape[0], h, d).transpose(1, 0, 2)


def _from_heads(a):
    return a.transpose(1, 0, 2).reshape(a.shape[1], -1)


SMALL = [("norm_g", 2048), ("ret_norm_g", 512), ("gla_ba_f", 256), ("gla_ba_b", 256), ("gla_norm_g", 512),
         ("pool_w", 4 * 128 * 128), ("pool_scale", 512), ("mla_q_norm_g", 512), ("mla_kv_norm_g", 256),
         ("mla_qk_norm_q", 192), ("mla_qk_norm_k", 192)]


def _pack_small(vals):
    parts = []
    for name, n in SMALL:
        parts += [v.reshape(-1) for v in vals[name]]
        if (DEPTH * n) % 1024:
            parts.append(jnp.zeros((-(DEPTH * n)) % 1024, F32))
    parts += [vals["loss"].reshape(-1), jnp.zeros(1023, F32)]
    return jnp.concatenate(parts).reshape(-1, 128)


def _unpack_small(block):
    flat = block.reshape(-1)
    out, off = {}, 0
    for name, n in SMALL:
        out[name] = flat[off:off + DEPTH * n]
        off += DEPTH * n + (-(DEPTH * n)) % 1024
    out["loss"] = flat[off]
    return out


def _layer_weights(l, p, g):
    wa = jnp.zeros((128, 512), F32)
    wa = wa.at[0:GLA_RANK, 0:256].set(_slots_to_cols(g["gla_wa2_f"]))
    wa = wa.at[GLA_RANK:2 * GLA_RANK, 256:512].set(_slots_to_cols(g["gla_wa2_b"]))
    return dict(
        norm_g=p["norm_g"][l][None, :],
        w_in=_w_in_padded(g["w_in"]),
        w_out=g["w_out"].reshape(4 * g["w_out"].shape[1], -1),
        ret_norm_g=p["ret_norm_g"][l][None, :],
        wa=_bf(wa),
        ba=jnp.concatenate([p["gla_ba_f"][l], p["gla_ba_b"][l]])[None, :],
        gla_norm_g=p["gla_norm_g"][l][None, :],
        pool_w=_bf(p["pool_w"][l]),
        pool_scale=p["pool_scale"][l][None, :],
        qg=p["mla_q_norm_g"][l][None, :],
        wq=_pad_heads(_slots_to_cols(g["mla_wq_b"]), MLA_QK, MLA_QKP),
        kvg=p["mla_kv_norm_g"][l][None, :],
        wkv=_slots_to_cols(g["mla_wkv_b"]),
        qng=jnp.pad(p["mla_qk_norm_q"][l], (0, MLA_QKP - MLA_QK))[None, :],
        kng=jnp.pad(p["mla_qk_norm_k"][l], (0, MLA_QKP - MLA_QK))[None, :],
    )


def _layer_fwd(l, x, w, tabs, next_shards=None):
    ret_cos, ret_sin, mla_cos, mla_sp, mla_sn = tabs
    nm = lambda s: f"l{l}_{s}"
    h = _rmsnorm_fwd(x, w["norm_g"], name=nm("norm"))
    if next_shards is None:
        z = _matmul(h, w["w_in"], tb=True, name=nm("in_proj"))
    else:
        z, landed = _matmul(h, w["w_in"], tb=True, rider=_rider_gather_send(next_shards[:1], SHARD_AXES[:1]),
                            name=nm("in_proj"))
    qr, kr = _ret_pre(z, ret_cos, ret_sin, name=nm("ret_pre"))
    ret_o = _bla(qr, kr, z, _ret_log_gamma(False), (0, 0, SEG["rv"][0] // 512), name=nm("ret_scan"))
    y_a = _post(ret_o, z, SEG["rg"][0] // 512, w["ret_norm_g"], norm=True, name=nm("ret_post"))
    la = _gla_gate(z, w["wa"], w["ba"], name=nm("gla_gate"))
    la_h = la.reshape(la.shape[0], 2, GLA_HEADS, GLA_DK).transpose(1, 2, 0, 3)
    gq = _to_heads(z[:, SEG["gq"][0]:SEG["gq"][0] + 256], GLA_HEADS, GLA_DK)
    gk = _to_heads(z[:, SEG["gk"][0]:SEG["gk"][0] + 256], GLA_HEADS, GLA_DK)
    if next_shards is None:
        gla_o, gla_st = _gla_fwd(gq, gk, z, la_h, name=nm("gla_scan"))
    else:
        gla_o, gla_st, more = _gla_fwd(gq, gk, z, la_h, rider=_rider_gather_send(next_shards[1:], SHARD_AXES[1:]),
                                       name=nm("gla_scan"))
        landed = list(landed) + list(more)
    y_b = _post(gla_o, z, SEG["gg"][0] // 512, w["gla_norm_g"], norm=True, name=nm("gla_post"))
    y_c = _pool_fwd(z, w["pool_w"], w["pool_scale"], name=nm("pool"))
    q, k, v = _mla_pre(z, w["qg"], w["wq"], w["kvg"], w["wkv"], w["qng"], w["kng"], mla_cos, mla_sp, mla_sn,
                       name=nm("mla_pre"))
    if next_shards is None:
        (att_o, lse), gathered = _flash_fwd(q, k, v, name=nm("attn")), None
    else:
        att_o, lse, gathered = _flash_fwd(q, k, v, rider=_rider_gather_forward(landed, SHARD_AXES), name=nm("attn"))
    y_d = _post([att_o], z, SEG["mg"][0] // 512, w["qg"], norm=False, name=nm("mla_post"))
    y = jnp.concatenate([y_a, y_b, y_c, y_d], axis=1)
    x_next = _matmul(y, w["w_out"], add=x, name=nm("out_proj"))
    saved = dict(x=x, h=h, z=z, y=y, qr=qr, kr=kr, ret_o=ret_o, la_h=la_h, gq=gq, gk=gk, gla_o=gla_o, gla_st=gla_st,
                 q=q, k=k, v=v, att_o=att_o, lse=lse)
    return x_next, saved, gathered


def _layer_bwd(l, dx_next, w, sv, tabs, riding_parts=None, where=None):
    ret_cos, ret_sin, mla_cos, mla_sp, mla_sn = tabs
    nm = lambda s: f"l{l}_{s}"
    z = sv["z"]
    dy = _matmul(dx_next, w["w_out"], tb=True, name=nm("out_proj_dy"))
    d_w_out = _matmul(sv["y"], dx_next, ta=True, tn=512, name=nm("out_proj_dw"))
    d_w_out = d_w_out.reshape(N_CHIP, d_w_out.shape[0] // N_CHIP, d_w_out.shape[1])
    if where is not None:
        pair_w_out = _pair_reduce(d_w_out, where, 0, out_dtype=BF16, name=nm("pair_reduce_w_out"))
    dz = lax.empty((z.shape[0], IN_PAD), BF16)
    at = lambda n: SEG[n][0]
    dz, d_ret_o, d_ret_g = _post_bwd(dy, 0, sv["ret_o"], z, SEG["rg"][0] // 512, w["ret_norm_g"], (dz, at("rg")),
                                     norm=True, name=nm("ret_post_bwd"))
    vcol = SEG["rv"][0] // 512
    dqr = _bla(d_ret_o, z, sv["kr"], _ret_log_gamma(False), (0, vcol, 0), name=nm("ret_scan_dq"))
    dkr = _bla(z, d_ret_o, sv["qr"], _ret_log_gamma(True), (vcol, 0, 0), name=nm("ret_scan_dk"))
    drv = _bla(sv["kr"], sv["qr"], d_ret_o, _ret_log_gamma(True), (0, 0, 0), name=nm("ret_scan_dv"))
    dz = _ret_pre_bwd(dqr, dkr, ret_cos, ret_sin, (dz, at("rq")), name=nm("ret_pre_bwd"))
    dz = _add_n([drv[0], drv[1]], out_dtype=BF16, into=(dz, at("rv")), name=nm("ret_dv_sum"))
    dz, d_gla_o, d_gla_g = _post_bwd(dy, 1, sv["gla_o"], z, SEG["gg"][0] // 512, w["gla_norm_g"], (dz, at("gg")),
                                     norm=True, name=nm("gla_post_bwd"))
    if where is None:
        dq2, dk2, dla2, dv2 = _gla_bwd(sv["gq"], sv["gk"], z, sv["la_h"], d_gla_o, sv["gla_st"],
                                       name=nm("gla_scan_bwd"))
    else:
        dq2, dk2, dla2, dv2, (others_w_out,) = _gla_bwd(
            sv["gq"], sv["gk"], z, sv["la_h"], d_gla_o, sv["gla_st"], rider=_rider_chip_exchange([pair_w_out]),
            name=nm("gla_scan_bwd"))
        d_w_out = (pair_w_out, others_w_out)
    d_gq = _bf(_from_heads(dq2[0] + dq2[1]))
    d_gk = _bf(_from_heads(dk2[0] + dk2[1]))
    dz = _add_n([dv2[0], dv2[1]], out_dtype=BF16, into=(dz, at("gv")), name=nm("gla_dv_sum"))
    dla = jnp.concatenate([_from_heads(dla2[0]), _from_heads(dla2[1])], axis=1)
    dz, d_wa, d_ba = _gla_gate_bwd(dla, z, w["wa"], w["ba"], (dz, at("ga")), name=nm("gla_gate_bwd"))
    d_pv, d_pg, d_pool_w, d_pool_scale = _pool_bwd(dy, z, w["pool_w"], w["pool_scale"], name=nm("pool_bwd"))
    dz, d_att_o, _ = _post_bwd(dy, 3, [sv["att_o"]], z, SEG["mg"][0] // 512, w["qg"], (dz, at("mg")), norm=False,
                               name=nm("mla_post_bwd"))
    if riding_parts is None:
        (dq, dk, dv), rode = _flash_bwd(sv["q"], sv["k"], sv["v"], d_att_o, sv["att_o"], sv["lse"],
                                        name=nm("attn_bwd")), None
    else:
        dq, dk, dv, rode = _flash_bwd(sv["q"], sv["k"], sv["v"], d_att_o, sv["att_o"], sv["lse"],
                                      rider=_rider_chip_exchange(riding_parts), name=nm("attn_bwd"))
    d_mq, d_mkv, d_mkr, d_wq, d_wkv, d_qg, d_kvg, d_qng, d_kng = _mla_pre_bwd(
        dq, dk, dv, z, w["qg"], w["wq"], w["kvg"], w["wkv"], w["qng"], w["kng"], mla_cos, mla_sp, mla_sn,
        name=nm("mla_pre_bwd"))
    for n, seg in dict(pv=d_pv, pg=d_pg, mq=d_mq, gq=d_gq, gk=d_gk, mkv=d_mkv, mkr=d_mkr).items():
        dz = lax.dynamic_update_slice(dz, seg, (0, at(n)))
    dh = _matmul(dz, w["w_in"], tn=512, name=nm("in_proj_dh"))
    d_w_in = _matmul(dz, sv["h"], ta=True, name=nm("in_proj_dw"))
    dx, d_norm_g = _rmsnorm_bwd(sv["x"], dh, w["norm_g"], dx_next, name=nm("norm_bwd"))
    sharded = dict(
        w_in=_w_in_unpadded(d_w_in),
        w_out=d_w_out,
        mla_wq_b=_cols_to_slots(_unpad_heads(d_wq, MLA_QK, MLA_QKP)),
        mla_wkv_b=_cols_to_slots(d_wkv),
        gla_wa2_f=_cols_to_slots(d_wa[0:GLA_RANK, 0:256]),
        gla_wa2_b=_cols_to_slots(d_wa[GLA_RANK:2 * GLA_RANK, 256:512]),
    )
    small = dict(
        norm_g=d_norm_g[0], ret_norm_g=d_ret_g[0], gla_ba_f=d_ba[0, :256], gla_ba_b=d_ba[0, 256:],
        gla_norm_g=d_gla_g[0], pool_w=d_pool_w.reshape(-1), pool_scale=d_pool_scale[0], mla_q_norm_g=d_qg[0],
        mla_kv_norm_g=d_kvg[0], mla_qk_norm_q=d_qng[0, :MLA_QK], mla_qk_norm_k=d_kng[0, :MLA_QK],
    )
    return dx, sharded, small, rode


SHARDED = ["w_in", "w_out", "mla_wq_b", "mla_wkv_b", "gla_wa2_f", "gla_wa2_b"]
WEIGHTS = ["norm_g", "w_in", "ret_norm_g", "gla_wa2_f", "gla_ba_f", "gla_wa2_b", "gla_ba_b", "gla_norm_g", "pool_w",
           "pool_scale", "mla_q_norm_g", "mla_wq_b", "mla_kv_norm_g", "mla_wkv_b", "mla_qk_norm_q", "mla_qk_norm_k",
           "w_out"]


SHARD_AXES = [1, 0, 0, 0, 0, 0]


def _layer_shards(p, l):
    return [jnp.swapaxes(p["w_in"], 1, 2)[l].astype(BF16), p["w_out"][l].astype(BF16), p["mla_wq_b"][l].astype(BF16),
            p["mla_wkv_b"][l].astype(BF16), p["gla_wa2_f"][l], p["gla_wa2_b"][l]]


def _step(p, where):
    x = p["x"][0]
    tabs = _rope_tables(x.shape[0])
    got0 = _gather_shards(_layer_shards(p, 0), SHARD_AXES, name="l0_gather_weights")
    w0 = _layer_weights(0, p, dict(zip(SHARDED, got0)))
    x1, sv0, got1 = _layer_fwd(0, x, w0, tabs, next_shards=_layer_shards(p, 1))
    w1 = _layer_weights(1, p, dict(zip(SHARDED, got1)))
    x2, sv1, _ = _layer_fwd(1, x1, w1, tabs)
    dx, loss = _loss_head(x2, p["loss_target"][0], name="loss_head")

    big, big_axes = SHARDED[:2], SHARD_AXES[:2]

    def pair_sums(tag, tensors, axes, names):
        return [_pair_reduce(a, where, ax, out_dtype=BF16, name=f"{tag}_pair_reduce_{n}")
                for a, ax, n in zip(tensors, axes, names)]

    def joined(tag, pair, others, axes, names):
        return [_sum_join(a, b, where, ax, name=f"{tag}_sum_join_{n}")
                for a, b, ax, n in zip(pair, others, axes, names)]

    dx, sharded1, small1, _ = _layer_bwd(1, dx, w1, sv1, tabs)
    pair1 = pair_sums("l1", [sharded1[n] for n in big], big_axes, big)
    dx, sharded0, small0, others1 = _layer_bwd(0, dx, w0, sv0, tabs, riding_parts=pair1, where=where)
    grads1 = joined("l1", pair1, others1, big_axes, big)
    packed = jnp.concatenate([sh[n].reshape(N_CHIP, -1, 128) for sh in (sharded0, sharded1) for n in SHARDED[2:]],
                             axis=1)
    last, last_axes, last_names = [sharded0["w_in"], packed], [SHARD_AXES[0], 0], ["w_in", "rest"]
    pair0 = pair_sums("l0", last, last_axes, last_names)
    g_w_in0, rest = joined("l0", pair0, _chip_exchange(pair0, name="l0_chip_exchange"), last_axes, last_names)
    (g_w_out0,) = joined("l0", [sharded0["w_out"][0]], [sharded0["w_out"][1]], [SHARD_AXES[1]], ["w_out"])
    grads = {n: jnp.stack([g0, g1]) for n, g0, g1 in zip(big, (g_w_in0, g_w_out0), grads1)}
    off = 0
    pieces = {n: [] for n in SHARDED[2:]}
    for sh in (sharded0, sharded1):
        for n in SHARDED[2:]:
            rows = sh[n].shape[1] * sh[n].shape[2] // 128
            pieces[n].append(rest[off:off + rows].reshape(sh[n].shape[1:]))
            off += rows
    grads.update({n: jnp.stack(v) for n, v in pieces.items()})
    small = {n: [small0[n], small1[n]] for n, _ in SMALL}
    small["loss"] = loss
    return dx[None], grads, small


def kernel(x, norm_g, w_in, ret_norm_g, gla_wa2_f, gla_ba_f, gla_wa2_b, gla_ba_b, gla_norm_g, pool_w, pool_scale, mla_q_norm_g, mla_wq_b, mla_kv_norm_g, mla_wkv_b, mla_qk_norm_q, mla_qk_norm_k, w_out, loss_target, m_norm_g, m_w_in, m_ret_norm_g, m_gla_wa2_f, m_gla_ba_f, m_gla_wa2_b, m_gla_ba_b, m_gla_norm_g, m_pool_w, m_pool_scale, m_mla_q_norm_g, m_mla_wq_b, m_mla_kv_norm_g, m_mla_wkv_b, m_mla_qk_norm_q, m_mla_qk_norm_k, m_w_out, v_norm_g, v_w_in, v_ret_norm_g, v_gla_wa2_f, v_gla_ba_f, v_gla_wa2_b, v_gla_ba_b, v_gla_norm_g, v_pool_w, v_pool_scale, v_mla_q_norm_g, v_mla_wq_b, v_mla_kv_norm_g, v_mla_wkv_b, v_mla_qk_norm_q, v_mla_qk_norm_k, v_w_out):
    p = dict(x=x, norm_g=norm_g, w_in=w_in, ret_norm_g=ret_norm_g, gla_wa2_f=gla_wa2_f, gla_ba_f=gla_ba_f,
             gla_wa2_b=gla_wa2_b, gla_ba_b=gla_ba_b, gla_norm_g=gla_norm_g, pool_w=pool_w, pool_scale=pool_scale,
             mla_q_norm_g=mla_q_norm_g, mla_wq_b=mla_wq_b, mla_kv_norm_g=mla_kv_norm_g, mla_wkv_b=mla_wkv_b,
             mla_qk_norm_q=mla_qk_norm_q, mla_qk_norm_k=mla_qk_norm_k, w_out=w_out, loss_target=loss_target)
    moments = dict(
        m=dict(norm_g=m_norm_g, w_in=m_w_in, ret_norm_g=m_ret_norm_g, gla_wa2_f=m_gla_wa2_f, gla_ba_f=m_gla_ba_f,
               gla_wa2_b=m_gla_wa2_b, gla_ba_b=m_gla_ba_b, gla_norm_g=m_gla_norm_g, pool_w=m_pool_w,
               pool_scale=m_pool_scale, mla_q_norm_g=m_mla_q_norm_g, mla_wq_b=m_mla_wq_b,
               mla_kv_norm_g=m_mla_kv_norm_g, mla_wkv_b=m_mla_wkv_b, mla_qk_norm_q=m_mla_qk_norm_q,
               mla_qk_norm_k=m_mla_qk_norm_k, w_out=m_w_out),
        v=dict(norm_g=v_norm_g, w_in=v_w_in, ret_norm_g=v_ret_norm_g, gla_wa2_f=v_gla_wa2_f, gla_ba_f=v_gla_ba_f,
               gla_wa2_b=v_gla_wa2_b, gla_ba_b=v_gla_ba_b, gla_norm_g=v_gla_norm_g, pool_w=v_pool_w,
               pool_scale=v_pool_scale, mla_q_norm_g=v_mla_q_norm_g, mla_wq_b=v_mla_wq_b,
               mla_kv_norm_g=v_mla_kv_norm_g, mla_wkv_b=v_mla_wkv_b, mla_qk_norm_q=v_mla_qk_norm_q,
               mla_qk_norm_k=v_mla_qk_norm_k, w_out=v_w_out))

    where = jnp.stack([lax.axis_index("c"), 2 * lax.axis_index("x") + lax.axis_index("y")]).astype(jnp.int32)
    grad_x, grads, small = _step(p, where)

    slots = _gather_all(_pack_small(small), name="gather_small")
    total = _unpack_small(_sum_slots(slots, name="sum_small"))
    for n, _ in SMALL:
        grads[n] = total[n].reshape(p[n].shape)
    loss = total["loss"]

    delta, new_m, new_v = {}, {}, {}
    for n in WEIGHTS:
        turn = (lambda a: jnp.swapaxes(a, 1, 2)) if n == "w_in" else (lambda a: a)
        outs = _adamw(turn(p[n]), grads[n], turn(moments["m"][n]), turn(moments["v"][n]), name=f"adamw_{n}")
        grads[n] = turn(grads[n])
        delta[n], new_m[n], new_v[n] = (turn(o) for o in outs)
    return (loss, grad_x, *[grads[n] for n in WEIGHTS], *[delta[n] for n in WEIGHTS],
            *[new_m[n] for n in WEIGHTS], *[new_v[n] for n in WEIGHTS])
```

```python
import jax
import jax.numpy as jnp
from jax import lax
from jax.experimental import pallas as pl
from jax.experimental.pallas import tpu as pltpu

F32 = jnp.float32
BF16 = jnp.bfloat16
MESH = pl.DeviceIdType.MESH

EPS = 1e-6
ROPE_THETA = 10000.0
DEPTH = 2
N_DEV = 8
N_CHIP = 4

GROUP_W = 512
RET_HEADS = 4
RET_HD = 128
RET_CHUNK = 256
GLA_HEADS = 4
GLA_DK = 64
GLA_DV = 128
GLA_RANK = 16
GLA_TAU = 16.0
GLA_CHUNK = 64
POOL_GROUPS = 4
POOL_GW = 128
POOL_HALO = 8
POOL_TILE = 256
MLA_HEADS = 4
MLA_NOPE = 128
MLA_ROPE = 64
MLA_QK = MLA_NOPE + MLA_ROPE
MLA_QKP = 256
MLA_V = 128
MLA_Q_RANK = 512
MLA_KV_RANK = 256
MLA_SCALE = MLA_QK ** -0.5

ADAM_LR = 0.001
ADAM_B1 = 0.9
ADAM_B2 = 0.999
ADAM_EPS = 1e-08
ADAM_WD = 0.01
ADAM_STEP = 10

VMEM_LIMIT = 56 * 1024 * 1024
ROW_TILE = 512

SEG = {
    "rq": (0, 512, 0, 512), "rk": (512, 512, 512, 512), "rv": (1024, 512, 1024, 512), "rg": (1536, 512, 1536, 512),
    "gv": (2048, 512, 2560, 512), "gg": (2560, 512, 3072, 512),
    "pv": (3072, 512, 3616, 512), "pg": (3584, 512, 4128, 512),
    "mq": (4096, 512, 4640, 512), "mg": (4608, 512, 5472, 512),
    "gq": (5120, 256, 2048, 256), "gk": (5376, 256, 2304, 256), "mkv": (5632, 256, 5152, 256),
    "ga": (5888, 128, 3584, 32), "mkr": (6016, 128, 5408, 64),
}
SEG_ORDER = ["rq", "rk", "rv", "rg", "gv", "gg", "pv", "pg", "mq", "mg", "gq", "gk", "mkv", "ga", "mkr"]
IN_COLS = 5984
IN_PAD = 6144
ORIG_ORDER = ["rq", "rk", "rv", "rg", "gq", "gk", "gv", "gg", "ga", "pv", "pg", "mq", "mkv", "mkr", "mg"]


def _cparams(*sem):
    return pltpu.CompilerParams(dimension_semantics=tuple(sem), vmem_limit_bytes=VMEM_LIMIT)


def _bf(v):
    return v.astype(BF16)


def _dot(a, b, ca=1, cb=0):
    return lax.dot_general(_bf(a), _bf(b), (((ca,), (cb,)), ((), ())), preferred_element_type=F32)


def _sigmoid(x):
    return 1.0 / (1.0 + jnp.exp(-x))


def _silu_parts(g):
    sg = _sigmoid(g)
    return g * sg, sg * (1.0 + g * (1.0 - sg))


class _Rider:
    def __init__(self, ins, outs, sems, start, finish, aliases=None):
        self.ins, self.outs, self.sems, self.start, self.finish = list(ins), list(outs), list(sems), start, finish
        self.aliases = dict(aliases or {})


def _ride(body, rider, n_in, n_out, grid):
    if rider is None:
        return body
    ri, ro, rs = len(rider.ins), len(rider.outs), len(rider.sems)

    def wrapped(*refs):
        ins, refs = refs[:n_in], refs[n_in:]
        rin, refs = refs[:ri], refs[ri:]
        outs, refs = refs[:n_out], refs[n_out:]
        rout, refs = refs[:ro], refs[ro:]
        scratch, sems = refs[:len(refs) - rs], refs[len(refs) - rs:]
        first = pl.program_id(0) == 0
        last = pl.program_id(0) == grid[0] - 1
        for ax in range(1, len(grid)):
            first = jnp.logical_and(first, pl.program_id(ax) == 0)
            last = jnp.logical_and(last, pl.program_id(ax) == grid[ax] - 1)

        @pl.when(first)
        def _():
            rider.start(rin, rout, sems)

        body(*ins, *outs, *scratch)

        @pl.when(last)
        def _():
            rider.finish(rin, rout, sems)

    return wrapped


def _ride_call(body, rider, *, name, grid, in_specs, out_specs, out_shape, scratch_shapes, args, sem):
    n_in, n_out = len(in_specs), len(out_specs)
    if rider is None:
        return pl.pallas_call(body, name=name, grid=grid, in_specs=in_specs, out_specs=out_specs, out_shape=out_shape,
                              scratch_shapes=scratch_shapes, compiler_params=_cparams(*sem))(*args), []
    outs = pl.pallas_call(
        _ride(body, rider, n_in, n_out, grid), name=name, grid=grid,
        in_specs=list(in_specs) + [ANY] * len(rider.ins), out_specs=list(out_specs) + [ANY] * len(rider.outs),
        out_shape=list(out_shape) + rider.outs, scratch_shapes=list(scratch_shapes) + rider.sems,
        input_output_aliases={n_in + i: n_out + o for i, o in rider.aliases.items()},
        compiler_params=_cparams(*(["arbitrary"] * len(grid))),
    )(*args, *rider.ins)
    return outs[:n_out], outs[n_out:]


def _matmul(a, b, *, ta=False, tb=False, out_dtype=F32, tm=512, tn=1024, tk=None, add=None, n_outer=True, rider=None,
            name):
    m, kdim = (a.shape[1], a.shape[0]) if ta else a.shape
    n = b.shape[0] if tb else b.shape[1]
    tm, tn = min(tm, m), min(tn, n)
    tk = kdim if tk is None else min(tk, kdim)
    assert m % tm == 0 and n % tn == 0 and kdim % tk == 0
    nk = kdim // tk
    ca, cb = (0 if ta else 1), (1 if tb else 0)

    def body(*refs):
        if add is None:
            a_ref, b_ref, o_ref = refs[:3]
            add_ref = None
        else:
            a_ref, b_ref, add_ref, o_ref = refs[:4]
        p = _dot(a_ref[...], b_ref[...], ca, cb)

        def finish(r):
            if add_ref is not None:
                r = r + add_ref[...]
            o_ref[...] = r.astype(out_dtype)

        if nk == 1:
            finish(p)
        else:
            acc = refs[-1]
            k = pl.program_id(2)

            @pl.when(k == 0)
            def _():
                acc[...] = p

            @pl.when(k > 0)
            def _():
                acc[...] += p

            @pl.when(k == nk - 1)
            def _():
                finish(acc[...])

    def ij(g0, g1):
        return (g1, g0) if n_outer else (g0, g1)

    a_spec = (pl.BlockSpec((tk, tm), lambda g0, g1, k: (k, ij(g0, g1)[0])) if ta
              else pl.BlockSpec((tm, tk), lambda g0, g1, k: (ij(g0, g1)[0], k)))
    b_spec = (pl.BlockSpec((tn, tk), lambda g0, g1, k: (ij(g0, g1)[1], k)) if tb
              else pl.BlockSpec((tk, tn), lambda g0, g1, k: (k, ij(g0, g1)[1])))
    o_spec = pl.BlockSpec((tm, tn), lambda g0, g1, k: ij(g0, g1))
    in_specs = [a_spec, b_spec] + ([o_spec] if add is not None else [])
    args = (a, b) + ((add,) if add is not None else ())
    grid = (n // tn, m // tm, nk) if n_outer else (m // tm, n // tn, nk)
    (out,), rode = _ride_call(
        body, rider, name=name, grid=grid, in_specs=in_specs, out_specs=[o_spec],
        out_shape=[jax.ShapeDtypeStruct((m, n), out_dtype)],
        scratch_shapes=[] if nk == 1 else [pltpu.VMEM((tm, tn), F32)], args=args,
        sem=("parallel", "parallel", "arbitrary"))
    return out if rider is None else (out, rode)


def _rmsnorm_fwd(x, g, *, name, tm=ROW_TILE):
    s, d = x.shape
    tm = min(tm, s)

    def body(x_ref, g_ref, h_ref):
        xv = x_ref[...]
        r = lax.rsqrt(jnp.mean(xv * xv, axis=-1, keepdims=True) + EPS)
        h_ref[...] = _bf(xv * r * g_ref[...])

    return pl.pallas_call(
        body, name=name, grid=(s // tm,),
        in_specs=[pl.BlockSpec((tm, d), lambda i: (i, 0)), pl.BlockSpec((1, d), lambda i: (0, 0))],
        out_specs=pl.BlockSpec((tm, d), lambda i: (i, 0)),
        out_shape=jax.ShapeDtypeStruct((s, d), BF16),
        compiler_params=_cparams("parallel"),
    )(x, g)


def _rmsnorm_bwd(x, dh, g, dres, *, name, tm=ROW_TILE):
    s, d = x.shape
    tm = min(tm, s)

    def body(x_ref, dh_ref, g_ref, dres_ref, dx_ref, dg_ref):
        i = pl.program_id(0)
        xv = x_ref[...]
        r = lax.rsqrt(jnp.mean(xv * xv, axis=-1, keepdims=True) + EPS)
        xn = xv * r
        dv = dh_ref[...]
        part = jnp.sum(dv * xn, axis=0, keepdims=True)

        @pl.when(i == 0)
        def _():
            dg_ref[...] = part

        @pl.when(i > 0)
        def _():
            dg_ref[...] += part

        dxn = dv * g_ref[...]
        dx_ref[...] = dres_ref[...] + r * (dxn - xn * jnp.mean(dxn * xn, axis=-1, keepdims=True))

    row = pl.BlockSpec((tm, d), lambda i: (i, 0))
    vec = pl.BlockSpec((1, d), lambda i: (0, 0))
    return pl.pallas_call(
        body, name=name, grid=(s // tm,), in_specs=[row, row, vec, row], out_specs=[row, vec],
        out_shape=[jax.ShapeDtypeStruct((s, d), F32), jax.ShapeDtypeStruct((1, d), F32)],
        compiler_params=_cparams("arbitrary"),
    )(x, dh, g, dres)


def _loss_head(xf, target, *, name, tm=ROW_TILE):
    s, d = xf.shape
    tm = min(tm, s)

    def body(x_ref, t_ref, dx_ref, l_ref):
        i = pl.program_id(0)
        e = x_ref[...] - t_ref[...]
        dx_ref[...] = e * (1.0 / d)
        rows = jnp.mean(e * e, axis=-1, keepdims=True)
        part = 0.5 * jnp.sum(rows, axis=0, keepdims=True)

        @pl.when(i == 0)
        def _():
            l_ref[...] = part

        @pl.when(i > 0)
        def _():
            l_ref[...] += part

    row = pl.BlockSpec((tm, d), lambda i: (i, 0))
    return pl.pallas_call(
        body, name=name, grid=(s // tm,), in_specs=[row, row],
        out_specs=[row, pl.BlockSpec((1, 1), lambda i: (0, 0))],
        out_shape=[jax.ShapeDtypeStruct((s, d), F32), jax.ShapeDtypeStruct((1, 1), F32)],
        compiler_params=_cparams("arbitrary"),
    )(xf, target)


def _rope_tables(s):
    pos = jnp.arange(s, dtype=F32)[:, None]
    inv_r = 1.0 / (ROPE_THETA ** (jnp.arange(0, RET_HD, 2, dtype=F32) / RET_HD))
    ang = pos * inv_r[None, :]
    ret_cos = jnp.concatenate([jnp.cos(ang), jnp.cos(ang)], axis=1)
    ret_sin = jnp.concatenate([-jnp.sin(ang), jnp.sin(ang)], axis=1)
    inv_m = 1.0 / (ROPE_THETA ** (jnp.arange(0, MLA_ROPE, 2, dtype=F32) / MLA_ROPE))
    am = pos * inv_m[None, :]
    z32, z64 = jnp.zeros((s, 32), F32), jnp.zeros((s, 64), F32)
    mla_cos = jnp.concatenate([jnp.cos(am), jnp.cos(am), z64], axis=1)
    mla_sp = jnp.concatenate([z32, jnp.sin(am), z64], axis=1)
    mla_sn = jnp.concatenate([-jnp.sin(am), z32, z64], axis=1)
    return ret_cos, ret_sin, mla_cos, mla_sp, mla_sn


def _rope128(x, c, sg):
    return x * c + pltpu.roll(x, 64, 1) * sg


def _unrope128(d, c, sg):
    return d * c + pltpu.roll(d * sg, 64, 1)


def _rope64(t, c, sp, sn):
    return t * c + pltpu.roll(t, 96, 1) * sn + pltpu.roll(t, 32, 1) * sp


def _unrope64(d, c, sp, sn):
    return d * c + pltpu.roll(d * sn, 32, 1) + pltpu.roll(d * sp, 96, 1)


def _ret_pre(z, cos, sin, *, name, tm=ROW_TILE):
    s = z.shape[0]
    tm = min(tm, s)
    scale = RET_HD ** -0.5

    def body(q_ref, k_ref, c_ref, s_ref, qo_ref, ko_ref):
        c, sg = c_ref[...], s_ref[...]
        for h in range(RET_HEADS):
            sl = slice(h * RET_HD, (h + 1) * RET_HD)
            qo_ref[:, sl] = _rope128(q_ref[:, sl], c, sg)
            ko_ref[:, sl] = _rope128(k_ref[:, sl], c, sg) * scale

    seg = lambda j: pl.BlockSpec((tm, GROUP_W), lambda i: (i, j))
    tab = pl.BlockSpec((tm, RET_HD), lambda i: (i, 0))
    return pl.pallas_call(
        body, name=name, grid=(s // tm,), in_specs=[seg(0), seg(1), tab, tab],
        out_specs=[seg(0), seg(0)],
        out_shape=[jax.ShapeDtypeStruct((s, GROUP_W), F32)] * 2,
        compiler_params=_cparams("parallel"),
    )(z, z, cos, sin)


def _ret_pre_bwd(dqr, dkr, cos, sin, into, *, name, tm=ROW_TILE):
    s = dqr[0].shape[0]
    tm = min(tm, s)
    scale = RET_HD ** -0.5

    def body(dq0_ref, dq1_ref, dk0_ref, dk1_ref, c_ref, s_ref, _, o_ref):
        c, sg = c_ref[...], s_ref[...]
        for h in range(RET_HEADS):
            sl = slice(h * RET_HD, (h + 1) * RET_HD)
            ksl = slice(GROUP_W + h * RET_HD, GROUP_W + (h + 1) * RET_HD)
            o_ref[:, sl] = _bf(_unrope128(dq0_ref[:, sl] + dq1_ref[:, sl], c, sg))
            o_ref[:, ksl] = _bf(_unrope128(dk0_ref[:, sl] + dk1_ref[:, sl], c, sg) * scale)

    row = pl.BlockSpec((tm, GROUP_W), lambda i: (i, 0))
    tab = pl.BlockSpec((tm, RET_HD), lambda i: (i, 0))
    out_shape, out_spec, more_specs, more_args = _landing(into, tm, 2 * GROUP_W)
    return pl.pallas_call(
        body, name=name, grid=(s // tm,), in_specs=[row, row, row, row, tab, tab] + more_specs, out_specs=out_spec,
        out_shape=out_shape, input_output_aliases={6: 0},
        compiler_params=_cparams("parallel"),
    )(dqr[0], dqr[1], dkr[0], dkr[1], cos, sin, *more_args)


def _bla(a, b, c, lg, cols, *, name):
    s = a.shape[0]
    ch = min(RET_CHUNK, s)
    n = s // ch
    hd = RET_HD

    def body(lg_ref, a0, b0, c0, a1, b1, c1, o0, o1, st):
        t = pl.program_id(0)

        @pl.when(t == 0)
        def _():
            st[...] = jnp.zeros_like(st)

        ii = lax.broadcasted_iota(jnp.int32, (ch, ch), 0)
        jj = lax.broadcasted_iota(jnp.int32, (ch, ch), 1)
        idx = lax.broadcasted_iota(jnp.int32, (ch, 1), 0).astype(F32)
        for d, (a_ref, b_ref, c_ref, o_ref) in enumerate(((a0, b0, c0, o0), (a1, b1, c1, o1))):
            diff = ((ii - jj) if d == 0 else (jj - ii)).astype(F32)
            keep = diff >= 0
            dpos = jnp.maximum(diff, 0.0)
            pq = (idx + 1.0) if d == 0 else (ch - idx)
            pk = (ch - 1.0 - idx) if d == 0 else idx
            for h in range(RET_HEADS):
                g = lg_ref[d, h]
                sl = slice(h * hd, (h + 1) * hd)
                av, bv, cv = a_ref[:, sl], b_ref[:, sl], c_ref[:, sl]
                sc = _dot(av, bv, 1, 1) * jnp.where(keep, jnp.exp(dpos * g), 0.0)
                stv = st[d, h]
                o_ref[:, sl] = _dot(sc, cv) + _dot(av * jnp.exp(pq * g), stv)
                st[d, h] = jnp.exp(ch * g) * stv + _dot(bv * jnp.exp(pk * g), cv, 0, 0)

    fwd = lambda j: pl.BlockSpec((ch, GROUP_W), lambda t: (t, j))
    bwd = lambda j: pl.BlockSpec((ch, GROUP_W), lambda t: (n - 1 - t, j))
    return pl.pallas_call(
        body, name=name, grid=(n,),
        in_specs=[pl.BlockSpec(memory_space=pltpu.SMEM), fwd(cols[0]), fwd(cols[1]), fwd(cols[2]),
                  bwd(cols[0]), bwd(cols[1]), bwd(cols[2])],
        out_specs=[fwd(0), bwd(0)],
        out_shape=[jax.ShapeDtypeStruct((s, GROUP_W), F32)] * 2,
        scratch_shapes=[pltpu.VMEM((2, RET_HEADS, hd, hd), F32)],
        compiler_params=_cparams("arbitrary"),
    )(lg, a, b, c, a, b, c)


def _post(os_, zg, gcol, g, *, norm, name, tm=ROW_TILE):
    s = zg.shape[0]
    tm = min(tm, s)
    nd = len(os_)

    def body(*refs):
        o_refs, (gt_ref, g_ref, y_ref) = refs[:nd], refs[nd:]
        silu, _ = _silu_parts(gt_ref[...])
        for h in range(4):
            sl = slice(h * 128, (h + 1) * 128)
            o = o_refs[0][:, sl]
            for k in range(1, nd):
                o = o + o_refs[k][:, sl]
            if norm:
                r = lax.rsqrt(jnp.mean(o * o, axis=-1, keepdims=True) + EPS)
                o = o * r * g_ref[:, sl]
            y_ref[:, sl] = _bf(silu[:, sl] * o)

    row = pl.BlockSpec((tm, GROUP_W), lambda i: (i, 0))
    return pl.pallas_call(
        body, name=name, grid=(s // tm,),
        in_specs=[row] * nd + [pl.BlockSpec((tm, GROUP_W), lambda i: (i, gcol)),
                               pl.BlockSpec((1, GROUP_W), lambda i: (0, 0))],
        out_specs=row,
        out_shape=jax.ShapeDtypeStruct((s, GROUP_W), BF16),
        compiler_params=_cparams("parallel"),
    )(*os_, zg, g)


def _post_bwd(dy, ycol, os_, zg, gcol, g, into, *, norm, name, tm=ROW_TILE):
    s = zg.shape[0]
    tm = min(tm, s)
    nd = len(os_)

    def body(*refs):
        dy_ref, o_refs = refs[0], refs[1:1 + nd]
        gt_ref, g_ref, _, dgt_ref, do_ref, dg_ref = refs[1 + nd:]
        i = pl.program_id(0)
        silu, dsilu = _silu_parts(gt_ref[...])
        dyv = dy_ref[...]
        parts = []
        for h in range(4):
            sl = slice(h * 128, (h + 1) * 128)
            o = o_refs[0][:, sl]
            for k in range(1, nd):
                o = o + o_refs[k][:, sl]
            dn = dyv[:, sl] * silu[:, sl]
            if norm:
                r = lax.rsqrt(jnp.mean(o * o, axis=-1, keepdims=True) + EPS)
                xn = o * r
                gh = g_ref[:, sl]
                dgt_ref[:, sl] = _bf(dyv[:, sl] * (xn * gh) * dsilu[:, sl])
                parts.append(jnp.sum(dn * xn, axis=0, keepdims=True))
                dxn = dn * gh
                do_ref[:, sl] = r * (dxn - xn * jnp.mean(dxn * xn, axis=-1, keepdims=True))
            else:
                dgt_ref[:, sl] = _bf(dyv[:, sl] * o * dsilu[:, sl])
                parts.append(jnp.zeros((1, 128), F32))
                do_ref[:, sl] = dn
        part = jnp.concatenate(parts, axis=1)

        @pl.when(i == 0)
        def _():
            dg_ref[...] = part

        @pl.when(i > 0)
        def _():
            dg_ref[...] += part

    row = pl.BlockSpec((tm, GROUP_W), lambda i: (i, 0))
    vec = pl.BlockSpec((1, GROUP_W), lambda i: (0, 0))
    dgt_shape, dgt_spec, more_specs, more_args = _landing(into, tm, GROUP_W)
    n_in = nd + 3
    return pl.pallas_call(
        body, name=name, grid=(s // tm,),
        in_specs=[pl.BlockSpec((tm, GROUP_W), lambda i: (i, ycol))] + [row] * nd
        + [pl.BlockSpec((tm, GROUP_W), lambda i: (i, gcol)), vec] + more_specs,
        out_specs=[dgt_spec, row, vec],
        out_shape=[dgt_shape, jax.ShapeDtypeStruct((s, GROUP_W), F32), jax.ShapeDtypeStruct((1, GROUP_W), F32)],
        input_output_aliases={n_in: 0},
        compiler_params=_cparams("arbitrary"),
    )(dy, *os_, zg, g, *more_args)


def _ret_log_gamma(swap):
    gf = 1.0 - 2.0 ** (-5.0 - jnp.arange(RET_HEADS, dtype=F32))
    lf, lb = jnp.log(gf), jnp.log(gf[::-1])
    return jnp.stack([lb, lf] if swap else [lf, lb])


def _log_sigmoid(x):
    return jnp.minimum(x, 0.0) - jnp.log(1.0 + jnp.exp(-jnp.abs(x)))


def _gla_gate(z, wa, ba, *, name, tm=ROW_TILE):
    s = z.shape[0]
    tm = min(tm, s)
    col = SEG["ga"][0] // 128

    def body(ga_ref, wa_ref, ba_ref, la_ref):
        pre = _dot(ga_ref[...], wa_ref[...]) + ba_ref[...]
        la_ref[...] = _log_sigmoid(pre) / GLA_TAU

    return pl.pallas_call(
        body, name=name, grid=(s // tm,),
        in_specs=[pl.BlockSpec((tm, 128), lambda i: (i, col)), pl.BlockSpec((128, 512), lambda i: (0, 0)),
                  pl.BlockSpec((1, 512), lambda i: (0, 0))],
        out_specs=pl.BlockSpec((tm, 512), lambda i: (i, 0)),
        out_shape=jax.ShapeDtypeStruct((s, 512), F32),
        compiler_params=_cparams("parallel"),
    )(z, wa, ba)


def _gla_gate_bwd(dla, z, wa, ba, into, *, name, tm=ROW_TILE):
    s = z.shape[0]
    tm = min(tm, s)
    col = SEG["ga"][0] // 128

    def body(dla_ref, ga_ref, wa_ref, ba_ref, _, dga_ref, dwa_ref, dba_ref):
        i = pl.program_id(0)
        gav = ga_ref[...]
        pre = _dot(gav, wa_ref[...]) + ba_ref[...]
        dpre = dla_ref[...] * (1.0 - _sigmoid(pre)) * (1.0 / GLA_TAU)
        dga_ref[...] = _bf(_dot(dpre, wa_ref[...], 1, 1))
        pw = _dot(gav, dpre, 0, 0)
        pb = jnp.sum(dpre, axis=0, keepdims=True)

        @pl.when(i == 0)
        def _():
            dwa_ref[...] = pw
            dba_ref[...] = pb

        @pl.when(i > 0)
        def _():
            dwa_ref[...] += pw
            dba_ref[...] += pb

    dga_shape, dga_spec, more_specs, more_args = _landing(into, tm, 128)
    return pl.pallas_call(
        body, name=name, grid=(s // tm,),
        in_specs=[pl.BlockSpec((tm, 512), lambda i: (i, 0)), pl.BlockSpec((tm, 128), lambda i: (i, col)),
                  pl.BlockSpec((128, 512), lambda i: (0, 0)), pl.BlockSpec((1, 512), lambda i: (0, 0))] + more_specs,
        out_specs=[dga_spec, pl.BlockSpec((128, 512), lambda i: (0, 0)), pl.BlockSpec((1, 512), lambda i: (0, 0))],
        out_shape=[dga_shape, jax.ShapeDtypeStruct((128, 512), F32), jax.ShapeDtypeStruct((1, 512), F32)],
        input_output_aliases={4: 0},
        compiler_params=_cparams("arbitrary"),
    )(dla, z, wa, ba, *more_args)


def _gla_masks(ch):
    ii = lax.broadcasted_iota(jnp.int32, (ch, ch), 0)
    tt = lax.broadcasted_iota(jnp.int32, (ch, ch), 1)
    return jnp.where(tt <= ii, 1.0, 0.0), jnp.where(tt >= ii, 1.0, 0.0)


def _running_sum(x, up):
    n = x.shape[0]
    rows = lax.broadcasted_iota(jnp.int32, x.shape, 0)
    k = 1
    while k < n:
        if up:
            x = x + jnp.where(rows < n - k, pltpu.roll(x, n - k, 0), 0.0)
        else:
            x = x + jnp.where(rows >= k, pltpu.roll(x, k, 0), 0.0)
        k *= 2
    return x


def _gla_chunk(d, tmat, qv, kv, lav, ch):
    c = _running_sum(lav, up=(d == 1))
    big_l = c[ch - 1:ch, :] if d == 0 else c[0:1, :]
    qt = qv * (GLA_DK ** -0.5) * jnp.exp(c)
    kt = kv * jnp.exp(-c)
    kh = kv * jnp.exp(big_l - c)
    return c, big_l, qt, kt, kh


def _gla_fwd(qh, kh_, z, la, *, name, rider=None):
    s = z.shape[0]
    ch = min(GLA_CHUNK, s)
    n = s // ch
    vcol = SEG["gv"][0] // GROUP_W

    def body(q0, k0, v0, la0, q1, k1, v1, la1, o0, o1, zs0, zs1, st):
        t = pl.program_id(0)

        @pl.when(t == 0)
        def _():
            st[...] = jnp.zeros_like(st)

        masks = _gla_masks(ch)
        for d, (q_ref, k_ref, v_ref, la_ref, o_ref, zs_ref) in enumerate(
                ((q0, k0, v0, la0, o0, zs0), (q1, k1, v1, la1, o1, zs1))):
            for h in range(GLA_HEADS):
                c, big_l, qt, kt, kh = _gla_chunk(d, masks[d], q_ref[h], k_ref[h], la_ref[0, h], ch)
                vv = v_ref[:, h * GLA_DV:(h + 1) * GLA_DV]
                p = _dot(qt, kt, 1, 1) * masks[d]
                zst = st[d, h]
                o_ref[:, h * GLA_DV:(h + 1) * GLA_DV] = _dot(p, vv) + _dot(qt, zst, 1, 1)
                zs_ref[h, 0] = zst
                st[d, h] = zst * jnp.exp(big_l) + _dot(vv, kh, 0, 0)

    cidx = (lambda t: t), (lambda t: n - 1 - t)
    hs = lambda d: pl.BlockSpec((GLA_HEADS, ch, GLA_DK), lambda t: (0, cidx[d](t), 0))
    vs = lambda d: pl.BlockSpec((ch, GROUP_W), lambda t: (cidx[d](t), vcol))
    las = lambda d: pl.BlockSpec((1, GLA_HEADS, ch, GLA_DK), lambda t: (d, 0, cidx[d](t), 0))
    os_ = lambda d: pl.BlockSpec((ch, GROUP_W), lambda t: (cidx[d](t), 0))
    zss = lambda d: pl.BlockSpec((GLA_HEADS, 1, GLA_DV, GLA_DK), lambda t: (0, cidx[d](t), 0, 0))
    (o0, o1, zs0, zs1), rode = _ride_call(
        body, rider, name=name, grid=(n,),
        in_specs=[hs(0), hs(0), vs(0), las(0), hs(1), hs(1), vs(1), las(1)],
        out_specs=[os_(0), os_(1), zss(0), zss(1)],
        out_shape=[jax.ShapeDtypeStruct((s, GROUP_W), F32)] * 2
        + [jax.ShapeDtypeStruct((GLA_HEADS, n, GLA_DV, GLA_DK), F32)] * 2,
        scratch_shapes=[pltpu.VMEM((2, GLA_HEADS, GLA_DV, GLA_DK), F32)],
        args=(qh, kh_, z, la, qh, kh_, z, la), sem=("arbitrary",))
    return ((o0, o1), (zs0, zs1)) if rider is None else ((o0, o1), (zs0, zs1), rode)


def _gla_bwd(qh, kh_, z, la, do, zs, *, name, rider=None):
    s = z.shape[0]
    ch = min(GLA_CHUNK, s)
    n = s // ch
    vcol = SEG["gv"][0] // GROUP_W

    def body(q0, k0, v0, la0, do0, zs0, q1, k1, v1, la1, do1, zs1,
             dq0, dk0, dla0, dv0, dq1, dk1, dla1, dv1, gz):
        t = pl.program_id(0)

        @pl.when(t == 0)
        def _():
            gz[...] = jnp.zeros_like(gz)

        masks = _gla_masks(ch)
        rows = lax.broadcasted_iota(jnp.int32, (ch, 1), 0)
        for d, (q_ref, k_ref, v_ref, la_ref, do_ref, zs_ref, dq_ref, dk_ref, dla_ref, dv_ref) in enumerate(
                ((q0, k0, v0, la0, do0, zs0, dq0, dk0, dla0, dv0), (q1, k1, v1, la1, do1, zs1, dq1, dk1, dla1, dv1))):
            tmat = masks[d]
            end = ch - 1 if d == 0 else 0
            for h in range(GLA_HEADS):
                ksl = slice(h * GLA_DK, (h + 1) * GLA_DK)
                c, big_l, qt, kt, kh = _gla_chunk(d, tmat, q_ref[h], k_ref[h], la_ref[0, h], ch)
                vsl = slice(h * GLA_DV, (h + 1) * GLA_DV)
                vv, dov, zst, gzv = v_ref[:, vsl], do_ref[:, vsl], zs_ref[h, 0], gz[d, h]
                p = _dot(qt, kt, 1, 1) * tmat
                dp = _dot(dov, vv, 1, 1) * tmat
                dqt = _dot(dp, kt) + _dot(dov, zst)
                dkt = _dot(dp, qt, 0, 0)
                dkh = _dot(vv, gzv)
                dv_ref[:, vsl] = _dot(p, dov, 0, 0) + _dot(kh, gzv, 1, 1)
                dq_ref[:, ksl] = dqt * jnp.exp(c) * (GLA_DK ** -0.5)
                dk_ref[:, ksl] = dkt * jnp.exp(-c) + dkh * jnp.exp(big_l - c)
                e_l = jnp.exp(big_l)
                d_l = jnp.sum(dkh * kh, axis=0, keepdims=True) + e_l * jnp.sum(zst * gzv, axis=0, keepdims=True)
                dc = dqt * qt - dkt * kt - dkh * kh + jnp.where(rows == end, d_l, 0.0)
                dla_ref[:, ksl] = _running_sum(dc, up=(d == 0))
                gz[d, h] = gzv * e_l + _dot(dov, qt, 0, 0)

    cidx = (lambda t: n - 1 - t), (lambda t: t)
    hs = lambda d: pl.BlockSpec((GLA_HEADS, ch, GLA_DK), lambda t: (0, cidx[d](t), 0))
    vs = lambda d: pl.BlockSpec((ch, GROUP_W), lambda t: (cidx[d](t), vcol))
    las = lambda d: pl.BlockSpec((1, GLA_HEADS, ch, GLA_DK), lambda t: (d, 0, cidx[d](t), 0))
    row = lambda d: pl.BlockSpec((ch, GROUP_W), lambda t: (cidx[d](t), 0))
    zss = lambda d: pl.BlockSpec((GLA_HEADS, 1, GLA_DV, GLA_DK), lambda t: (0, cidx[d](t), 0, 0))
    kw = GLA_HEADS * GLA_DK
    ks = lambda d: pl.BlockSpec((ch, kw), lambda t: (cidx[d](t), 0))
    hshape = jax.ShapeDtypeStruct((s, kw), F32)
    wide = jax.ShapeDtypeStruct((s, GROUP_W), F32)
    outs, rode = _ride_call(
        body, rider, name=name, grid=(n,),
        in_specs=[hs(0), hs(0), vs(0), las(0), row(0), zss(0), hs(1), hs(1), vs(1), las(1), row(1), zss(1)],
        out_specs=[ks(0), ks(0), ks(0), row(0), ks(1), ks(1), ks(1), row(1)],
        out_shape=[hshape, hshape, hshape, wide, hshape, hshape, hshape, wide],
        scratch_shapes=[pltpu.VMEM((2, GLA_HEADS, GLA_DV, GLA_DK), F32)],
        args=(qh, kh_, z, la, do, zs[0], qh, kh_, z, la, do, zs[1]), sem=("arbitrary",))
    dq0, dk0, dla0, dv0, dq1, dk1, dla1, dv1 = outs
    res = ((dq0, dq1), (dk0, dk1), (dla0, dla1), (dv0, dv1))
    return res if rider is None else res + (rode,)


def _window_sums(win, g, shift):
    n = win.shape[0]
    levels, y = [], win
    for j in range(POOL_GROUPS):
        y = y + pltpu.roll(y, n - (1 << j), 0)
        levels.append(y)
    sums = levels[-1]
    for j in range(POOL_GROUPS - 2, -1, -1):
        sums = jnp.where(g == j, levels[j], sums)
    return pltpu.roll(sums, shift, 0)


def _pool_cnt(t0, half, rows, s):
    t = t0 + lax.broadcasted_iota(jnp.int32, (rows, 1), 0)
    return (jnp.minimum(t + half, s) - jnp.maximum(t - half, 0)).astype(F32)


def _pool_fwd(z, pw, scale, *, name):
    s = z.shape[0]
    tl = min(POOL_TILE, s)
    nt = s // tl
    ucol, gcol = SEG["pv"][0] // 128, SEG["pg"][0] // 128

    def body(u_ref, gt_ref, pw_ref, sc_ref, y_ref, pad):
        g = pl.program_id(0)
        half = jnp.left_shift(1, g)
        pad[0:POOL_HALO, :] = jnp.zeros((POOL_HALO, POOL_GW), F32)
        pad[POOL_HALO + s:POOL_HALO + s + POOL_HALO, :] = jnp.zeros((POOL_HALO, POOL_GW), F32)
        pad[POOL_HALO:POOL_HALO + s, :] = u_ref[...]
        pwv, scv = pw_ref[0], sc_ref[...]

        def tile(i, carry):
            t0 = pl.multiple_of(i * tl, tl)
            win = pad[pl.ds(t0, tl + 2 * POOL_HALO), :]
            u = win[POOL_HALO:POOL_HALO + tl, :]
            pooled = _window_sums(win, g, half)[POOL_HALO:POOL_HALO + tl, :] / _pool_cnt(t0, half, tl, s) - u
            mixed = _dot(pooled, pwv)
            silu, _ = _silu_parts(gt_ref[pl.ds(t0, tl), :])
            y_ref[pl.ds(t0, tl), :] = _bf(silu * (mixed * scv))
            return carry

        lax.fori_loop(0, nt, tile, 0)

    return pl.pallas_call(
        body, name=name, grid=(POOL_GROUPS,),
        in_specs=[pl.BlockSpec((s, POOL_GW), lambda g: (0, ucol + g)),
                  pl.BlockSpec((s, POOL_GW), lambda g: (0, gcol + g)),
                  pl.BlockSpec((1, POOL_GW, POOL_GW), lambda g: (g, 0, 0)),
                  pl.BlockSpec((1, POOL_GW), lambda g: (0, g))],
        out_specs=pl.BlockSpec((s, POOL_GW), lambda g: (0, g)),
        out_shape=jax.ShapeDtypeStruct((s, GROUP_W), BF16),
        scratch_shapes=[pltpu.VMEM((s + 2 * POOL_HALO, POOL_GW), F32)],
        compiler_params=_cparams("parallel"),
    )(z, z, pw, scale)


def _pool_bwd(dy, z, pw, scale, *, name):
    s = z.shape[0]
    tl = min(POOL_TILE, s)
    nt = s // tl
    ucol, gcol, ycol = SEG["pv"][0] // 128, SEG["pg"][0] // 128, 2 * GROUP_W // 128

    def body(dy_ref, u_ref, gt_ref, pw_ref, sc_ref, du_ref, dgt_ref, dpw_ref, dsc_ref, pad, epad, dpo):
        g = pl.program_id(0)
        half = jnp.left_shift(1, g)
        zeros = jnp.zeros((POOL_HALO, POOL_GW), F32)
        for buf in (pad, epad):
            buf[0:POOL_HALO, :] = zeros
            buf[POOL_HALO + s:POOL_HALO + s + POOL_HALO, :] = zeros
        pad[POOL_HALO:POOL_HALO + s, :] = u_ref[...]
        pwv, scv = pw_ref[0], sc_ref[...]
        dpw_ref[0] = jnp.zeros((POOL_GW, POOL_GW), F32)
        dsc_ref[...] = jnp.zeros((1, POOL_GW), F32)

        def tile(i, carry):
            t0 = pl.multiple_of(i * tl, tl)
            win = pad[pl.ds(t0, tl + 2 * POOL_HALO), :]
            u = win[POOL_HALO:POOL_HALO + tl, :]
            cnt = _pool_cnt(t0, half, tl, s)
            pooled = _window_sums(win, g, half)[POOL_HALO:POOL_HALO + tl, :] / cnt - u
            mixed = _dot(pooled, pwv)
            silu, dsilu = _silu_parts(gt_ref[pl.ds(t0, tl), :])
            dyv = dy_ref[pl.ds(t0, tl), :]
            dgt_ref[pl.ds(t0, tl), :] = _bf(dyv * (mixed * scv) * dsilu)
            dsc_ref[...] += jnp.sum(dyv * silu * mixed, axis=0, keepdims=True)
            dm = dyv * silu * scv
            dpw_ref[0] += _dot(pooled, dm, 0, 0)
            dpooled = _dot(dm, pwv, 1, 1)
            dpo[pl.ds(t0, tl), :] = dpooled
            epad[pl.ds(POOL_HALO + t0, tl), :] = dpooled / cnt
            return carry

        lax.fori_loop(0, nt, tile, 0)

        def tile2(i, carry):
            t0 = pl.multiple_of(i * tl, tl)
            ewin = epad[pl.ds(t0, tl + 2 * POOL_HALO), :]
            du_ref[pl.ds(t0, tl), :] = _bf(_window_sums(ewin, g, half - 1)[POOL_HALO:POOL_HALO + tl, :]
                                           - dpo[pl.ds(t0, tl), :])
            return carry

        lax.fori_loop(0, nt, tile2, 0)

    col = lambda c0: pl.BlockSpec((s, POOL_GW), lambda g: (0, c0 + g))
    return pl.pallas_call(
        body, name=name, grid=(POOL_GROUPS,),
        in_specs=[col(ycol), col(ucol), col(gcol), pl.BlockSpec((1, POOL_GW, POOL_GW), lambda g: (g, 0, 0)),
                  pl.BlockSpec((1, POOL_GW), lambda g: (0, g))],
        out_specs=[col(0), col(0), pl.BlockSpec((1, POOL_GW, POOL_GW), lambda g: (g, 0, 0)),
                   pl.BlockSpec((1, POOL_GW), lambda g: (0, g))],
        out_shape=[jax.ShapeDtypeStruct((s, GROUP_W), BF16), jax.ShapeDtypeStruct((s, GROUP_W), BF16),
                   jax.ShapeDtypeStruct((POOL_GROUPS, POOL_GW, POOL_GW), F32),
                   jax.ShapeDtypeStruct((1, GROUP_W), F32)],
        scratch_shapes=[pltpu.VMEM((s + 2 * POOL_HALO, POOL_GW), F32), pltpu.VMEM((s + 2 * POOL_HALO, POOL_GW), F32),
                        pltpu.VMEM((s, POOL_GW), F32)],
        compiler_params=_cparams("parallel"),
    )(dy, z, z, pw, scale)


def _mla_specs(tm):
    zq = pl.BlockSpec((tm, 512), lambda i: (i, SEG["mq"][0] // 512))
    zkv = pl.BlockSpec((tm, 256), lambda i: (i, SEG["mkv"][0] // 256))
    zkr = pl.BlockSpec((tm, 128), lambda i: (i, SEG["mkr"][0] // 128))
    full = lambda r, c: pl.BlockSpec((r, c), lambda i: (0, 0))
    tab = pl.BlockSpec((tm, 128), lambda i: (i, 0))
    weights = [full(1, 512), full(512, 1024), full(1, 256), full(256, 1024), full(1, 256), full(1, 256)]
    return [zq, zkv, zkr] + weights + [tab, tab, tab]


def _mla_project(xq_ref, xkv_ref, qg_ref, wq_ref, kvg_ref, wkv_ref):
    xq = xq_ref[...]
    r1 = lax.rsqrt(jnp.mean(xq * xq, axis=-1, keepdims=True) + EPS)
    xn1 = xq * r1
    qn = _bf(xn1 * qg_ref[...])
    qraw = _dot(qn, wq_ref[...])
    xkv = xkv_ref[...]
    r2 = lax.rsqrt(jnp.mean(xkv * xkv, axis=-1, keepdims=True) + EPS)
    xn2 = xkv * r2
    kvn = _bf(xn2 * kvg_ref[...])
    kvraw = _dot(kvn, wkv_ref[...])
    return r1, xn1, qn, qraw, r2, xn2, kvn, kvraw


def _mla_pre(z, qg, wq, kvg, wkv, qng, kng, cos, sp, sn, *, name, tm=ROW_TILE):
    s = z.shape[0]
    tm = min(tm, s)

    def body(xq_ref, xkv_ref, pe_ref, qg_ref, wq_ref, kvg_ref, wkv_ref, qng_ref, kng_ref, c_ref, sp_ref, sn_ref,
             q_ref, k_ref, v_ref):
        _, _, _, qraw, _, _, _, kvraw = _mla_project(xq_ref, xkv_ref, qg_ref, wq_ref, kvg_ref, wkv_ref)
        c, spv, snv = c_ref[...], sp_ref[...], sn_ref[...]
        pe = pe_ref[...]
        pe_ss = jnp.sum(pe * pe, axis=-1, keepdims=True)
        qngv, kngv = qng_ref[...], kng_ref[...]
        for h in range(MLA_HEADS):
            b = h * MLA_QKP
            qh = qraw[:, b:b + MLA_QKP]
            r = lax.rsqrt(jnp.sum(qh * qh, axis=-1, keepdims=True) * (1.0 / MLA_QK) + EPS)
            qn_h = qh * r * qngv
            q_ref[:, b:b + 128] = _bf(qn_h[:, :128] * MLA_SCALE)
            q_ref[:, b + 128:b + 256] = _bf(_rope64(qn_h[:, 128:], c, spv, snv) * MLA_SCALE)
            kn = kvraw[:, b:b + 128]
            rk = lax.rsqrt((jnp.sum(kn * kn, axis=-1, keepdims=True) + pe_ss) * (1.0 / MLA_QK) + EPS)
            k_ref[:, b:b + 128] = _bf(kn * rk * kngv[:, :128])
            k_ref[:, b + 128:b + 256] = _bf(_rope64(pe * rk * kngv[:, 128:], c, spv, snv))
            v_ref[:, h * MLA_V:(h + 1) * MLA_V] = _bf(kvraw[:, b + 128:b + 256])

    row = lambda w: pl.BlockSpec((tm, w), lambda i: (i, 0))
    return pl.pallas_call(
        body, name=name, grid=(s // tm,), in_specs=_mla_specs(tm),
        out_specs=[row(1024), row(1024), row(512)],
        out_shape=[jax.ShapeDtypeStruct((s, 1024), BF16), jax.ShapeDtypeStruct((s, 1024), BF16),
                   jax.ShapeDtypeStruct((s, 512), BF16)],
        compiler_params=_cparams("parallel"),
    )(z, z, z, qg, wq, kvg, wkv, qng, kng, cos, sp, sn)


def _mla_pre_bwd(dq, dk, dv, z, qg, wq, kvg, wkv, qng, kng, cos, sp, sn, *, name, tm=ROW_TILE):
    s = z.shape[0]
    tm = min(tm, s)

    def body(dq_ref, dk_ref, dv_ref, xq_ref, xkv_ref, pe_ref, qg_ref, wq_ref, kvg_ref, wkv_ref, qng_ref, kng_ref,
             c_ref, sp_ref, sn_ref, dxq_ref, dxkv_ref, dpe_ref, dwq_ref, dwkv_ref, dqg_ref, dkvg_ref, dqng_ref,
             dkng_ref, dqraw, dkvraw):
        i = pl.program_id(0)
        r1, xn1, qn, qraw, r2, xn2, kvn, kvraw = _mla_project(xq_ref, xkv_ref, qg_ref, wq_ref, kvg_ref, wkv_ref)
        c, spv, snv = c_ref[...], sp_ref[...], sn_ref[...]
        pe = pe_ref[...]
        pe_ss = jnp.sum(pe * pe, axis=-1, keepdims=True)
        qngv, kngv = qng_ref[...], kng_ref[...]
        dqng = jnp.zeros((1, MLA_QKP), F32)
        dkng = jnp.zeros((1, MLA_QKP), F32)
        dpe = jnp.zeros_like(pe)
        for h in range(MLA_HEADS):
            b = h * MLA_QKP
            qh = qraw[:, b:b + MLA_QKP]
            r = lax.rsqrt(jnp.sum(qh * qh, axis=-1, keepdims=True) * (1.0 / MLA_QK) + EPS)
            xn = qh * r
            d_n = jnp.concatenate(
                [dq_ref[:, b:b + 128], _unrope64(dq_ref[:, b + 128:b + 256], c, spv, snv)], axis=1) * MLA_SCALE
            dqng = dqng + jnp.sum(d_n * xn, axis=0, keepdims=True)
            dxn = d_n * qngv
            dqraw[:, b:b + MLA_QKP] = _bf(r * (dxn - xn * (jnp.sum(dxn * xn, axis=-1, keepdims=True) * (1.0 / MLA_QK))))
            kn = kvraw[:, b:b + 128]
            rk = lax.rsqrt((jnp.sum(kn * kn, axis=-1, keepdims=True) + pe_ss) * (1.0 / MLA_QK) + EPS)
            xk = jnp.concatenate([kn, pe], axis=1) * rk
            d_k = jnp.concatenate(
                [dk_ref[:, b:b + 128], _unrope64(dk_ref[:, b + 128:b + 256], c, spv, snv)], axis=1)
            dkng = dkng + jnp.sum(d_k * xk, axis=0, keepdims=True)
            dxk = d_k * kngv
            dfull = rk * (dxk - xk * (jnp.sum(dxk * xk, axis=-1, keepdims=True) * (1.0 / MLA_QK)))
            dkvraw[:, b:b + 128] = _bf(dfull[:, :128])
            dkvraw[:, b + 128:b + 256] = _bf(dv_ref[:, h * MLA_V:(h + 1) * MLA_V])
            dpe = dpe + dfull[:, 128:]
        dpe_ref[...] = _bf(dpe)
        dqr, dkvr = dqraw[...], dkvraw[...]
        dqn = _dot(dqr, wq_ref[...], 1, 1)
        dxn1 = dqn * qg_ref[...]
        dxq_ref[...] = _bf(r1 * (dxn1 - xn1 * jnp.mean(dxn1 * xn1, axis=-1, keepdims=True)))
        dkvn = _dot(dkvr, wkv_ref[...], 1, 1)
        dxn2 = dkvn * kvg_ref[...]
        dxkv_ref[...] = _bf(r2 * (dxn2 - xn2 * jnp.mean(dxn2 * xn2, axis=-1, keepdims=True)))
        parts = (_dot(qn, dqr, 0, 0), _dot(kvn, dkvr, 0, 0), jnp.sum(dqn * xn1, axis=0, keepdims=True),
                 jnp.sum(dkvn * xn2, axis=0, keepdims=True), dqng, dkng)
        accs = (dwq_ref, dwkv_ref, dqg_ref, dkvg_ref, dqng_ref, dkng_ref)

        @pl.when(i == 0)
        def _():
            for a, p in zip(accs, parts):
                a[...] = p

        @pl.when(i > 0)
        def _():
            for a, p in zip(accs, parts):
                a[...] += p

    row = lambda w: pl.BlockSpec((tm, w), lambda i: (i, 0))
    full = lambda r, c: pl.BlockSpec((r, c), lambda i: (0, 0))
    return pl.pallas_call(
        body, name=name, grid=(s // tm,),
        in_specs=[row(1024), row(1024), row(512)] + _mla_specs(tm),
        out_specs=[row(512), row(256), row(128), full(512, 1024), full(256, 1024), full(1, 512), full(1, 256),
                   full(1, 256), full(1, 256)],
        out_shape=[jax.ShapeDtypeStruct((s, 512), BF16), jax.ShapeDtypeStruct((s, 256), BF16),
                   jax.ShapeDtypeStruct((s, 128), BF16), jax.ShapeDtypeStruct((512, 1024), F32),
                   jax.ShapeDtypeStruct((256, 1024), F32), jax.ShapeDtypeStruct((1, 512), F32),
                   jax.ShapeDtypeStruct((1, 256), F32), jax.ShapeDtypeStruct((1, 256), F32),
                   jax.ShapeDtypeStruct((1, 256), F32)],
        scratch_shapes=[pltpu.VMEM((tm, 1024), BF16), pltpu.VMEM((tm, 1024), BF16)],
        compiler_params=_cparams("arbitrary"),
    )(dq, dk, dv, z, z, z, qg, wq, kvg, wkv, qng, kng, cos, sp, sn)


def _flash_fwd(q, k, v, *, name, tq=1024, tk=1024, rider=None):
    s = q.shape[0]
    tq, tk = min(tq, s), min(tk, s)
    nk = s // tk

    def body(q_ref, k_ref, v_ref, o_ref, lse_ref, m_s, l_s, acc):
        j = pl.program_id(2)

        @pl.when(j == 0)
        def _():
            m_s[...] = jnp.full_like(m_s, -jnp.inf)
            l_s[...] = jnp.zeros_like(l_s)
            acc[...] = jnp.zeros_like(acc)

        sc = _dot(q_ref[...], k_ref[...], 1, 1)
        m_prev = m_s[...]
        m_new = jnp.maximum(m_prev, jnp.max(sc, axis=-1, keepdims=True))
        p = jnp.exp(sc - m_new[:, 0:1])
        alpha = jnp.exp(m_prev - m_new)
        l_s[...] = alpha * l_s[...] + jnp.sum(p, axis=-1, keepdims=True)
        acc[...] = alpha * acc[...] + _dot(p, v_ref[...])
        m_s[...] = m_new

        @pl.when(j == nk - 1)
        def _():
            o_ref[...] = acc[...] / l_s[...]
            lse_ref[...] = m_s[...] + jnp.log(l_s[...])

    (o, lse), rode = _ride_call(
        body, rider, name=name, grid=(MLA_HEADS, s // tq, nk),
        in_specs=[pl.BlockSpec((tq, MLA_QKP), lambda h, i, j: (i, h)),
                  pl.BlockSpec((tk, MLA_QKP), lambda h, i, j: (j, h)),
                  pl.BlockSpec((tk, MLA_V), lambda h, i, j: (j, h))],
        out_specs=[pl.BlockSpec((tq, MLA_V), lambda h, i, j: (i, h))] * 2,
        out_shape=[jax.ShapeDtypeStruct((s, GROUP_W), F32)] * 2,
        scratch_shapes=[pltpu.VMEM((tq, MLA_V), F32), pltpu.VMEM((tq, MLA_V), F32), pltpu.VMEM((tq, MLA_V), F32)],
        args=(q, k, v), sem=("parallel", "parallel", "arbitrary"))
    return (o, lse) if rider is None else (o, lse, rode)


def _flash_bwd(q, k, v, do, o, lse, *, name, tq=1024, tk=1024, rider=None):
    s = q.shape[0]
    tq, tk = min(tq, s), min(tk, s)
    nq, nk = s // tq, s // tk

    def body(q_ref, k_ref, v_ref, do_ref, o_ref, lse_ref, dq_ref, dk_ref, dv_ref, dk_acc, dv_acc):
        j, i = pl.program_id(1), pl.program_id(2)
        dov = do_ref[...]
        delta = jnp.sum(dov * o_ref[...], axis=-1, keepdims=True)
        p = jnp.exp(_dot(q_ref[...], k_ref[...], 1, 1) - lse_ref[:, 0:1])
        ds = p * (_dot(dov, v_ref[...], 1, 1) - delta)
        pv = _dot(p, dov, 0, 0)
        pk = _dot(ds, q_ref[...], 0, 0)
        pq = _dot(ds, k_ref[...])
        rows = pl.ds(pl.multiple_of(i * tq, tq), tq)

        @pl.when(j == 0)
        def _():
            dq_ref[rows, :] = pq

        @pl.when(j > 0)
        def _():
            dq_ref[rows, :] += pq

        @pl.when(i == 0)
        def _():
            dv_acc[...] = pv
            dk_acc[...] = pk

        @pl.when(i > 0)
        def _():
            dv_acc[...] += pv
            dk_acc[...] += pk

        @pl.when(i == nq - 1)
        def _():
            dk_ref[...] = dk_acc[...]
            dv_ref[...] = dv_acc[...]

    qb = pl.BlockSpec((tq, MLA_QKP), lambda h, j, i: (i, h))
    kb = pl.BlockSpec((tk, MLA_QKP), lambda h, j, i: (j, h))
    vb = pl.BlockSpec((tk, MLA_V), lambda h, j, i: (j, h))
    ob = pl.BlockSpec((tq, MLA_V), lambda h, j, i: (i, h))
    (dq, dk, dv), rode = _ride_call(
        body, rider, name=name, grid=(MLA_HEADS, nk, nq),
        in_specs=[qb, kb, vb, ob, ob, ob],
        out_specs=[pl.BlockSpec((s, MLA_QKP), lambda h, j, i: (0, h)), kb, vb],
        out_shape=[jax.ShapeDtypeStruct((s, MLA_HEADS * MLA_QKP), F32),
                   jax.ShapeDtypeStruct((s, MLA_HEADS * MLA_QKP), F32), jax.ShapeDtypeStruct((s, GROUP_W), F32)],
        scratch_shapes=[pltpu.VMEM((tk, MLA_QKP), F32), pltpu.VMEM((tk, MLA_V), F32)],
        args=(q, k, v, do, o, lse), sem=("arbitrary", "arbitrary", "arbitrary"))
    return (dq, dk, dv) if rider is None else (dq, dk, dv, rode)


def _rows_tile(r, c, itemsize=4, budget=2 * 1024 * 1024):
    if r * c * itemsize <= budget:
        return r
    best = None
    for t in range(8, r, 8):
        if r % t == 0 and t * c * itemsize <= budget:
            best = t
    return best if best is not None else r


def _landing(into, tm, width):
    buf, col = into
    assert col % width == 0
    return (jax.ShapeDtypeStruct(buf.shape, buf.dtype), pl.BlockSpec((tm, width), lambda i: (i, col // width)),
            [ANY], [buf])


def _add_n(arrs, *, out_dtype=F32, name, into=None):
    shape = arrs[0].shape
    c = shape[-1]
    flat = [a.reshape(-1, c) for a in arrs]
    r = flat[0].shape[0]
    t = _rows_tile(r, c)
    n_in = len(flat)

    def body(*refs):
        acc = refs[0][...].astype(F32)
        for ref in refs[1:n_in]:
            acc = acc + ref[...].astype(F32)
        refs[-1][...] = acc.astype(out_dtype)

    blk = pl.BlockSpec((t, c), lambda i: (i, 0))
    if into is not None:
        out_shape, out_spec, more_specs, more_args = _landing(into, t, c)
        return pl.pallas_call(
            body, name=name, grid=(r // t,), in_specs=[blk] * n_in + more_specs, out_specs=out_spec,
            out_shape=out_shape, input_output_aliases={n_in: 0}, compiler_params=_cparams("parallel"),
        )(*flat, *more_args)
    out = pl.pallas_call(
        body, name=name, grid=(r // t,), in_specs=[blk] * n_in, out_specs=blk,
        out_shape=jax.ShapeDtypeStruct((r, c), out_dtype), compiler_params=_cparams("parallel"),
    )(*flat)
    return out.reshape(shape)


def _adamw(w, g, m, v, *, name):
    shape = w.shape
    c = shape[-1]
    flat = [a.reshape(-1, c) for a in (w, g, m, v)]
    r = flat[0].shape[0]
    t = _rows_tile(r, c, budget=1024 * 1024)

    def body(w_ref, g_ref, m_ref, v_ref, d_ref, mo_ref, vo_ref):
        gv = g_ref[...]
        m2 = ADAM_B1 * m_ref[...] + (1.0 - ADAM_B1) * gv
        v2 = ADAM_B2 * v_ref[...] + (1.0 - ADAM_B2) * (gv * gv)
        m_hat = m2 / (1.0 - ADAM_B1 ** ADAM_STEP)
        v_hat = v2 / (1.0 - ADAM_B2 ** ADAM_STEP)
        d_ref[...] = -ADAM_LR * (m_hat / (jnp.sqrt(v_hat) + ADAM_EPS) + ADAM_WD * w_ref[...])
        mo_ref[...] = m2
        vo_ref[...] = v2

    blk = pl.BlockSpec((t, c), lambda i: (i, 0))
    outs = pl.pallas_call(
        body, name=name, grid=(r // t,), in_specs=[blk] * 4, out_specs=[blk] * 3,
        out_shape=[jax.ShapeDtypeStruct((r, c), F32)] * 3, compiler_params=_cparams("parallel"),
    )(*flat)
    return tuple(o.reshape(shape) for o in outs)


def _place():
    x, y, c = lax.axis_index("x"), lax.axis_index("y"), lax.axis_index("c")
    chips = [(1 - x, y), (x, 1 - y), (1 - x, 1 - y)]
    return x, y, c, chips


ANY = pl.BlockSpec(memory_space=pl.ANY)


def _half(ref, axis, hc, lead=()):
    n = ref.shape[len(lead) + axis] // 2
    return ref.at[tuple(lead) + (slice(None),) * axis + (pl.ds(hc * n, n),)]


def _gather_shards(shards, axes, *, name):
    nt = len(shards)

    def body(*refs):
        src, dst = refs[:nt], refs[nt:2 * nt]
        send, recv, fsend, frecv, lsem = refs[2 * nt:]
        x, y, c, chips = _place()
        me = 2 * x + y
        local = [pltpu.make_async_copy(src[t], dst[t].at[me], lsem.at[t]) for t in range(nt)]
        for cp in local:
            cp.start()

        def half(t, slot, hc):
            return _half(dst[t], axes[t], hc, lead=(slot,))

        def first(t, k):
            return pltpu.make_async_remote_copy(
                src_ref=_half(src[t], axes[t], c), dst_ref=half(t, me, c),
                send_sem=send.at[t, k], recv_sem=recv.at[t, k],
                device_id=(chips[k][0], chips[k][1], c), device_id_type=MESH)

        def landed(t, k):
            slot = 2 * chips[k][0] + chips[k][1]
            return pltpu.make_async_remote_copy(
                src_ref=half(t, slot, c), dst_ref=half(t, slot, c),
                send_sem=send.at[t, k], recv_sem=recv.at[t, k],
                device_id=(chips[k][0], chips[k][1], c), device_id_type=MESH)

        def forward(t, k, hc):
            slot = 2 * chips[k][0] + chips[k][1]
            return pltpu.make_async_remote_copy(
                src_ref=half(t, slot, hc), dst_ref=half(t, slot, hc),
                send_sem=fsend.at[t, k], recv_sem=frecv.at[t, k],
                device_id=(x, y, 1 - c), device_id_type=MESH)

        for t in range(nt):
            for k in range(3):
                first(t, k).start()
        for t in range(nt):
            for k in range(3):
                landed(t, k).wait_recv()
                forward(t, k, c).start()
        for t in range(nt):
            for k in range(3):
                forward(t, k, 1 - c).wait_recv()
        for t in range(nt):
            for k in range(3):
                first(t, k).wait_send()
                forward(t, k, c).wait_send()
        for cp in local:
            cp.wait()

    return pl.pallas_call(
        body, name=name, in_specs=[ANY] * nt, out_specs=[ANY] * nt,
        out_shape=[jax.ShapeDtypeStruct((N_CHIP,) + a.shape, a.dtype) for a in shards],
        scratch_shapes=[pltpu.SemaphoreType.DMA((nt, 3)), pltpu.SemaphoreType.DMA((nt, 3)),
                        pltpu.SemaphoreType.DMA((nt, 3)), pltpu.SemaphoreType.DMA((nt, 3)),
                        pltpu.SemaphoreType.DMA((nt,))],
    )(*shards)


def _comm_rows(hr, c, budget=2 * 1024 * 1024):
    if hr * c * 4 <= budget:
        return hr
    best = None
    for t in range(16, hr, 16):
        if hr % t == 0 and t * c * 4 <= budget:
            best = t
    return best if best is not None else hr


def _comm_cols(r, hc, budget=2 * 1024 * 1024):
    best = 128
    for t in range(128, hc + 1, 128):
        if hc % t == 0 and r * t * 4 <= budget:
            best = t
    return best


def _comm_chunks(shape, axis):
    r, cdim = shape
    if axis == 0:
        rc = _comm_rows(r // 2, cdim)
        nt = (r // 2) // rc
        return (rc, cdim), nt, (lambda h, t: (h * nt + t, 0))
    cc = _comm_cols(r, cdim // 2)
    nt = (cdim // 2) // cc
    return (r, cc), nt, (lambda h, t: (0, h * nt + t))


def _pair_reduce(g, where, axis, *, out_dtype, name):
    n_slot, r, cdim = g.shape
    blk_shape, nr, at = _comm_chunks((r, cdim), axis)
    steps = n_slot * nr
    half_shape = (r // 2, cdim) if axis == 0 else (r, cdim // 2)

    def body(w_ref, a_ref, b_ref, o_ref, land, send, recv, credit):
        x, y, c, _ = _place()
        sib = (x, y, 1 - c)
        i = pl.program_id(0) * nr + pl.program_id(1)
        s = lax.rem(i, 2)

        @pl.when(i >= 2)
        def _():
            pl.semaphore_wait(credit.at[s], 1)

        cp = pltpu.make_async_remote_copy(src_ref=b_ref.at[0], dst_ref=land.at[s], send_sem=send.at[s],
                                          recv_sem=recv.at[s], device_id=sib, device_id_type=MESH)
        cp.start()
        cp.wait_recv()
        o_ref[0] = (a_ref[0] + land[s]).astype(out_dtype)
        cp.wait_send()

        @pl.when(i + 2 < steps)
        def _():
            pl.semaphore_signal(credit.at[s], inc=1, device_id=sib, device_id_type=MESH)

    blk = lambda half: pl.BlockSpec((1,) + blk_shape, lambda j, t, w: (j,) + at(half(w), t))
    grid_spec = pltpu.PrefetchScalarGridSpec(
        num_scalar_prefetch=1, grid=(n_slot, nr),
        in_specs=[blk(lambda w: w[0]), blk(lambda w: 1 - w[0])],
        out_specs=pl.BlockSpec((1,) + blk_shape, lambda j, t, w: (j,) + at(0, t)),
        scratch_shapes=[pltpu.VMEM((2,) + blk_shape, F32), pltpu.SemaphoreType.DMA((2,)),
                        pltpu.SemaphoreType.DMA((2,)), pltpu.SemaphoreType.REGULAR((2,))])
    return pl.pallas_call(
        body, name=name, grid_spec=grid_spec, out_shape=jax.ShapeDtypeStruct((n_slot,) + half_shape, out_dtype),
        compiler_params=_cparams("arbitrary", "arbitrary"),
    )(where, g, g)


def _chip_exchange(parts, *, name):
    nt = len(parts)

    def body(*refs):
        src, got = refs[:nt], refs[nt:2 * nt]
        send, recv = refs[2 * nt:]
        x, y, c, chips = _place()
        remote = []
        for t in range(nt):
            for k in range(3):
                remote.append(pltpu.make_async_remote_copy(
                    src_ref=src[t].at[2 * chips[k][0] + chips[k][1]], dst_ref=got[t].at[k],
                    send_sem=send.at[t, k], recv_sem=recv.at[t, k],
                    device_id=(chips[k][0], chips[k][1], c), device_id_type=MESH))
        for cp in remote:
            cp.start()
        for cp in remote:
            cp.wait_recv()
        for cp in remote:
            cp.wait_send()

    return pl.pallas_call(
        body, name=name, in_specs=[ANY] * nt, out_specs=[ANY] * nt,
        out_shape=[jax.ShapeDtypeStruct((3,) + a.shape[1:], a.dtype) for a in parts],
        scratch_shapes=[pltpu.SemaphoreType.DMA((nt, 3)), pltpu.SemaphoreType.DMA((nt, 3))],
    )(*parts)


def _sum_join(p, got, where, axis, *, name):
    _, hr, cdim = p.shape
    full = (2 * hr, cdim) if axis == 0 else (hr, 2 * cdim)
    blk_shape, n, at = _comm_chunks(full, axis)
    step_len = blk_shape[axis]
    half_len = full[axis] // 2

    def body(w_ref, p_ref, g_ref, out, buf, lsem, ssem, rsem):
        x, y, c, _ = _place()
        sib = (x, y, 1 - c)
        r = pl.program_id(0)

        def part(start, size):
            return out.at[(slice(None),) * axis + (pl.ds(start, size),)]

        def copies(step, slot):
            rows = part(pl.multiple_of(c * half_len + step * step_len, 8 if axis == 0 else 128), step_len)
            return (pltpu.make_async_copy(buf.at[slot], rows, lsem.at[slot]),
                    pltpu.make_async_remote_copy(src_ref=buf.at[slot], dst_ref=rows, send_sem=ssem.at[slot],
                                                 recv_sem=rsem, device_id=sib, device_id_type=MESH))

        s = lax.rem(r, 2)

        @pl.when(r >= 2)
        def _():
            lc, rm = copies(r - 2, s)
            lc.wait()
            rm.wait_send()

        buf[s] = p_ref[0].astype(F32) + g_ref[0].astype(F32) + g_ref[1].astype(F32) + g_ref[2].astype(F32)
        lc, rm = copies(r, s)
        lc.start()
        rm.start()

        @pl.when(r == n - 1)
        def _():
            for step in range(max(0, n - 2), n):
                lc, rm = copies(step, step % 2)
                lc.wait()
                rm.wait_send()
            whole = part(0, half_len)
            pltpu.make_async_remote_copy(src_ref=whole, dst_ref=whole, send_sem=ssem.at[0], recv_sem=rsem,
                                         device_id=sib, device_id_type=MESH).wait_recv()

    grid_spec = pltpu.PrefetchScalarGridSpec(
        num_scalar_prefetch=1, grid=(n,),
        in_specs=[pl.BlockSpec((1,) + blk_shape, lambda t, w: (w[1],) + at(0, t)),
                  pl.BlockSpec((3,) + blk_shape, lambda t, w: (0,) + at(0, t))],
        out_specs=ANY,
        scratch_shapes=[pltpu.VMEM((2,) + blk_shape, F32), pltpu.SemaphoreType.DMA((2,)),
                        pltpu.SemaphoreType.DMA((2,)), pltpu.SemaphoreType.DMA])
    return pl.pallas_call(
        body, name=name, grid_spec=grid_spec, out_shape=jax.ShapeDtypeStruct(full, F32),
        compiler_params=_cparams("arbitrary"),
    )(where, p, got)


def _rider_gather_send(shards, axes):
    nt = len(shards)

    def copies(src, dst, send, recv, lsem):
        x, y, c, chips = _place()
        me = 2 * x + y
        local = [pltpu.make_async_copy(src[t], dst[t].at[me], lsem.at[t]) for t in range(nt)]
        out, landed = [], []
        for t in range(nt):
            for k in range(3):
                peer = (chips[k][0], chips[k][1], c)
                out.append(pltpu.make_async_remote_copy(
                    src_ref=_half(src[t], axes[t], c), dst_ref=_half(dst[t], axes[t], c, lead=(me,)),
                    send_sem=send.at[t, k], recv_sem=recv.at[t, k], device_id=peer, device_id_type=MESH))
                theirs = _half(dst[t], axes[t], c, lead=(2 * chips[k][0] + chips[k][1],))
                landed.append(pltpu.make_async_remote_copy(
                    src_ref=theirs, dst_ref=theirs, send_sem=send.at[t, k], recv_sem=recv.at[t, k],
                    device_id=peer, device_id_type=MESH))
        return local, out, landed

    def start(src, dst, sems):
        local, out, _ = copies(src, dst, *sems)
        for cp in local + out:
            cp.start()

    def finish(src, dst, sems):
        local, out, landed = copies(src, dst, *sems)
        for cp in landed:
            cp.wait_recv()
        for cp in out:
            cp.wait_send()
        for cp in local:
            cp.wait()

    return _Rider(shards, [jax.ShapeDtypeStruct((N_CHIP,) + a.shape, a.dtype) for a in shards],
                  [pltpu.SemaphoreType.DMA((nt, 3)), pltpu.SemaphoreType.DMA((nt, 3)), pltpu.SemaphoreType.DMA((nt,))],
                  start, finish)


def _rider_gather_forward(bufs, axes):
    nt = len(bufs)

    def copies(src, dst, send, recv):
        x, y, c, chips = _place()
        mine, theirs = [], []
        for t in range(nt):
            for k in range(3):
                slot = 2 * chips[k][0] + chips[k][1]
                for hc, into in ((c, mine), (1 - c, theirs)):
                    into.append(pltpu.make_async_remote_copy(
                        src_ref=_half(src[t], axes[t], hc, lead=(slot,)),
                        dst_ref=_half(dst[t], axes[t], hc, lead=(slot,)),
                        send_sem=send.at[t, k], recv_sem=recv.at[t, k], device_id=(x, y, 1 - c), device_id_type=MESH))
        return mine, theirs

    def start(src, dst, sems):
        for cp in copies(src, dst, *sems)[0]:
            cp.start()

    def finish(src, dst, sems):
        mine, theirs = copies(src, dst, *sems)
        for cp in theirs:
            cp.wait_recv()
        for cp in mine:
            cp.wait_send()

    return _Rider(bufs, [jax.ShapeDtypeStruct(a.shape, a.dtype) for a in bufs],
                  [pltpu.SemaphoreType.DMA((nt, 3)), pltpu.SemaphoreType.DMA((nt, 3))], start, finish,
                  aliases={t: t for t in range(nt)})


def _rider_chip_exchange(parts):
    nt = len(parts)

    def copies(src, got, send, recv):
        x, y, c, chips = _place()
        return [pltpu.make_async_remote_copy(
            src_ref=src[t].at[2 * chips[k][0] + chips[k][1]], dst_ref=got[t].at[k], send_sem=send.at[t, k],
            recv_sem=recv.at[t, k], device_id=(chips[k][0], chips[k][1], c), device_id_type=MESH)
            for t in range(nt) for k in range(3)]

    def start(src, got, sems):
        for cp in copies(src, got, *sems):
            cp.start()

    def finish(src, got, sems):
        remote = copies(src, got, *sems)
        for cp in remote:
            cp.wait_recv()
        for cp in remote:
            cp.wait_send()

    return _Rider(parts, [jax.ShapeDtypeStruct((3,) + a.shape[1:], a.dtype) for a in parts],
                  [pltpu.SemaphoreType.DMA((nt, 3)), pltpu.SemaphoreType.DMA((nt, 3))], start, finish)


def _gather_all(block, *, name):
    m_per, n = block.shape

    def body(x_ref, out_ref, send_sems, recv_sems, local_sem):
        x, y, c, chips = _place()
        me, sibling = (x, y, c), (x, y, 1 - c)

        def rows(px, py, pc):
            return out_ref.at[4 * px + 2 * py + pc]

        def copy(k, blk, to, src=None):
            return pltpu.make_async_remote_copy(
                src_ref=rows(*blk) if src is None else src, dst_ref=rows(*blk),
                send_sem=send_sems.at[k], recv_sem=recv_sems.at[k], device_id=to, device_id_type=MESH)

        mine = pltpu.make_async_copy(x_ref, rows(*me), local_sem)
        mine.start()
        first = [copy(0, me, sibling, src=x_ref)]
        first += [copy(1 + j, me, (*chip, c), src=x_ref) for j, chip in enumerate(chips)]
        for cp in first:
            cp.start()
        passed = [copy(4 + j, (*chip, c), sibling) for j, chip in enumerate(chips)]
        for j, chip in enumerate(chips):
            copy(1 + j, (*chip, c), me).wait_recv()
            passed[j].start()
        copy(0, sibling, me).wait_recv()
        for j, chip in enumerate(chips):
            copy(4 + j, (*chip, 1 - c), me).wait_recv()
        for cp in first + passed:
            cp.wait_send()
        mine.wait()

    return pl.pallas_call(
        body, name=name,
        out_shape=jax.ShapeDtypeStruct((N_DEV, m_per, n), block.dtype),
        in_specs=[pl.BlockSpec(memory_space=pltpu.VMEM)], out_specs=pl.BlockSpec(memory_space=pltpu.VMEM),
        scratch_shapes=[pltpu.SemaphoreType.DMA((7,)), pltpu.SemaphoreType.DMA((7,)), pltpu.SemaphoreType.DMA],
        compiler_params=pltpu.CompilerParams(vmem_limit_bytes=VMEM_LIMIT),
    )(block)


def _sum_slots(slots, *, name):
    n, m, c = slots.shape
    t = _rows_tile(m, c * n)

    def body(s_ref, o_ref):
        acc = s_ref[0]
        for k in range(1, n):
            acc = acc + s_ref[k]
        o_ref[...] = acc

    return pl.pallas_call(
        body, name=name, grid=(m // t,), in_specs=[pl.BlockSpec((n, t, c), lambda i: (0, i, 0))],
        out_specs=pl.BlockSpec((t, c), lambda i: (i, 0)), out_shape=jax.ShapeDtypeStruct((m, c), F32),
        compiler_params=_cparams("parallel"),
    )(slots)


def _pad_rows(a, rows):
    return a if a.shape[0] == rows else jnp.pad(a, ((0, rows - a.shape[0]), (0, 0)))


def _w_in_padded(shards):
    full = shards.reshape(IN_COLS, shards.shape[2])
    return jnp.concatenate([_pad_rows(full[SEG[n][2]:SEG[n][2] + SEG[n][3]], SEG[n][1]) for n in SEG_ORDER], axis=0)


def _w_in_unpadded(gp):
    full = jnp.concatenate([gp[SEG[n][0]:SEG[n][0] + SEG[n][3]] for n in ORIG_ORDER], axis=0)
    return full.reshape(N_CHIP, IN_COLS // N_CHIP, gp.shape[1])


def _pad_heads(w, true_w, pad_w):
    r = w.shape[0]
    h = w.shape[1] // true_w
    return jnp.pad(w.reshape(r, h, true_w), ((0, 0), (0, 0), (0, pad_w - true_w))).reshape(r, h * pad_w)


def _unpad_heads(w, true_w, pad_w):
    r = w.shape[0]
    h = w.shape[1] // pad_w
    return w.reshape(r, h, pad_w)[:, :, :true_w].reshape(r, h * true_w)


def _cols_to_slots(a):
    return a.reshape(a.shape[0], N_CHIP, a.shape[1] // N_CHIP).transpose(1, 0, 2)


def _to_heads(a, h, d):
    return a.reshape(a.shape[0], h, d).transpose(1, 0, 2)


def _slots_to_cols(a):
    return jnp.concatenate([a[j] for j in range(N_CHIP)], axis=1)


SMALL = [("norm_g", 2048), ("ret_norm_g", 512), ("gla_ba_f", 256), ("gla_ba_b", 256), ("gla_norm_g", 512),
         ("pool_w", 4 * 128 * 128), ("pool_scale", 512), ("mla_q_norm_g", 512), ("mla_kv_norm_g", 256),
         ("mla_qk_norm_q", 192), ("mla_qk_norm_k", 192)]


def _pack_small(vals):
    parts = []
    for name, n in SMALL:
        parts += [v.reshape(-1) for v in vals[name]]
        if (DEPTH * n) % 1024:
            parts.append(jnp.zeros((-(DEPTH * n)) % 1024, F32))
    parts += [vals["loss"].reshape(-1), jnp.zeros(1023, F32)]
    return jnp.concatenate(parts).reshape(-1, 128)


def _unpack_small(block):
    flat = block.reshape(-1)
    out, off = {}, 0
    for name, n in SMALL:
        out[name] = flat[off:off + DEPTH * n]
        off += DEPTH * n + (-(DEPTH * n)) % 1024
    out["loss"] = flat[off]
    return out


def _layer_weights(l, p, g):
    wa = jnp.zeros((128, 512), F32)
    wa = wa.at[0:GLA_RANK, 0:256].set(_slots_to_cols(g["gla_wa2_f"]))
    wa = wa.at[GLA_RANK:2 * GLA_RANK, 256:512].set(_slots_to_cols(g["gla_wa2_b"]))
    return dict(
        norm_g=p["norm_g"][l][None, :],
        w_in=_w_in_padded(g["w_in"]),
        w_out=g["w_out"].reshape(4 * g["w_out"].shape[1], -1),
        ret_norm_g=p["ret_norm_g"][l][None, :],
        wa=_bf(wa),
        ba=jnp.concatenate([p["gla_ba_f"][l], p["gla_ba_b"][l]])[None, :],
        gla_norm_g=p["gla_norm_g"][l][None, :],
        pool_w=_bf(p["pool_w"][l]),
        pool_scale=p["pool_scale"][l][None, :],
        qg=p["mla_q_norm_g"][l][None, :],
        wq=_pad_heads(_slots_to_cols(g["mla_wq_b"]), MLA_QK, MLA_QKP),
        kvg=p["mla_kv_norm_g"][l][None, :],
        wkv=_slots_to_cols(g["mla_wkv_b"]),
        qng=jnp.pad(p["mla_qk_norm_q"][l], (0, MLA_QKP - MLA_QK))[None, :],
        kng=jnp.pad(p["mla_qk_norm_k"][l], (0, MLA_QKP - MLA_QK))[None, :],
    )


def _layer_fwd(l, x, w, tabs, next_shards=None):
    ret_cos, ret_sin, mla_cos, mla_sp, mla_sn = tabs
    nm = lambda s: f"l{l}_{s}"
    h = _rmsnorm_fwd(x, w["norm_g"], name=nm("norm"))
    if next_shards is None:
        z = _matmul(h, w["w_in"], tb=True, name=nm("in_proj"))
    else:
        z, landed = _matmul(h, w["w_in"], tb=True, rider=_rider_gather_send(next_shards[:1], SHARD_AXES[:1]),
                            name=nm("in_proj"))
    qr, kr = _ret_pre(z, ret_cos, ret_sin, name=nm("ret_pre"))
    ret_o = _bla(qr, kr, z, _ret_log_gamma(False), (0, 0, SEG["rv"][0] // 512), name=nm("ret_scan"))
    y_a = _post(ret_o, z, SEG["rg"][0] // 512, w["ret_norm_g"], norm=True, name=nm("ret_post"))
    la = _gla_gate(z, w["wa"], w["ba"], name=nm("gla_gate"))
    la_h = la.reshape(la.shape[0], 2, GLA_HEADS, GLA_DK).transpose(1, 2, 0, 3)
    gq = _to_heads(z[:, SEG["gq"][0]:SEG["gq"][0] + 256], GLA_HEADS, GLA_DK)
    gk = _to_heads(z[:, SEG["gk"][0]:SEG["gk"][0] + 256], GLA_HEADS, GLA_DK)
    if next_shards is None:
        gla_o, gla_st = _gla_fwd(gq, gk, z, la_h, name=nm("gla_scan"))
    else:
        gla_o, gla_st, more = _gla_fwd(gq, gk, z, la_h, rider=_rider_gather_send(next_shards[1:], SHARD_AXES[1:]),
                                       name=nm("gla_scan"))
        landed = list(landed) + list(more)
    y_b = _post(gla_o, z, SEG["gg"][0] // 512, w["gla_norm_g"], norm=True, name=nm("gla_post"))
    y_c = _pool_fwd(z, w["pool_w"], w["pool_scale"], name=nm("pool"))
    q, k, v = _mla_pre(z, w["qg"], w["wq"], w["kvg"], w["wkv"], w["qng"], w["kng"], mla_cos, mla_sp, mla_sn,
                       name=nm("mla_pre"))
    if next_shards is None:
        (att_o, lse), gathered = _flash_fwd(q, k, v, name=nm("attn")), None
    else:
        att_o, lse, gathered = _flash_fwd(q, k, v, rider=_rider_gather_forward(landed, SHARD_AXES), name=nm("attn"))
    y_d = _post([att_o], z, SEG["mg"][0] // 512, w["qg"], norm=False, name=nm("mla_post"))
    y = jnp.concatenate([y_a, y_b, y_c, y_d], axis=1)
    x_next = _matmul(y, w["w_out"], add=x, name=nm("out_proj"))
    saved = dict(x=x, h=h, z=z, y=y, qr=qr, kr=kr, ret_o=ret_o, la_h=la_h, gq=gq, gk=gk, gla_o=gla_o, gla_st=gla_st,
                 q=q, k=k, v=v, att_o=att_o, lse=lse)
    return x_next, saved, gathered


def _layer_bwd(l, dx_next, w, sv, tabs, riding_parts=None, where=None):
    ret_cos, ret_sin, mla_cos, mla_sp, mla_sn = tabs
    nm = lambda s: f"l{l}_{s}"
    z = sv["z"]
    dy = _matmul(dx_next, w["w_out"], tb=True, name=nm("out_proj_dy"))
    d_w_out = _matmul(sv["y"], dx_next, ta=True, tn=512, name=nm("out_proj_dw"))
    d_w_out = d_w_out.reshape(N_CHIP, d_w_out.shape[0] // N_CHIP, d_w_out.shape[1])
    if where is not None:
        pair_w_out = _pair_reduce(d_w_out, where, 0, out_dtype=BF16, name=nm("pair_reduce_w_out"))
    dz = lax.empty((z.shape[0], IN_PAD), BF16)
    at = lambda n: SEG[n][0]
    dz, d_ret_o, d_ret_g = _post_bwd(dy, 0, sv["ret_o"], z, SEG["rg"][0] // 512, w["ret_norm_g"], (dz, at("rg")),
                                     norm=True, name=nm("ret_post_bwd"))
    vcol = SEG["rv"][0] // 512
    dqr = _bla(d_ret_o, z, sv["kr"], _ret_log_gamma(False), (0, vcol, 0), name=nm("ret_scan_dq"))
    dkr = _bla(z, d_ret_o, sv["qr"], _ret_log_gamma(True), (vcol, 0, 0), name=nm("ret_scan_dk"))
    drv = _bla(sv["kr"], sv["qr"], d_ret_o, _ret_log_gamma(True), (0, 0, 0), name=nm("ret_scan_dv"))
    dz = _ret_pre_bwd(dqr, dkr, ret_cos, ret_sin, (dz, at("rq")), name=nm("ret_pre_bwd"))
    dz = _add_n([drv[0], drv[1]], out_dtype=BF16, into=(dz, at("rv")), name=nm("ret_dv_sum"))
    dz, d_gla_o, d_gla_g = _post_bwd(dy, 1, sv["gla_o"], z, SEG["gg"][0] // 512, w["gla_norm_g"], (dz, at("gg")),
                                     norm=True, name=nm("gla_post_bwd"))
    if where is None:
        dq2, dk2, dla2, dv2 = _gla_bwd(sv["gq"], sv["gk"], z, sv["la_h"], d_gla_o, sv["gla_st"],
                                       name=nm("gla_scan_bwd"))
    else:
        dq2, dk2, dla2, dv2, (others_w_out,) = _gla_bwd(
            sv["gq"], sv["gk"], z, sv["la_h"], d_gla_o, sv["gla_st"], rider=_rider_chip_exchange([pair_w_out]),
            name=nm("gla_scan_bwd"))
        d_w_out = (pair_w_out, others_w_out)
    d_gq = _bf(dq2[0] + dq2[1])
    d_gk = _bf(dk2[0] + dk2[1])
    dz = _add_n([dv2[0], dv2[1]], out_dtype=BF16, into=(dz, at("gv")), name=nm("gla_dv_sum"))
    dla = jnp.concatenate([dla2[0], dla2[1]], axis=1)
    dz, d_wa, d_ba = _gla_gate_bwd(dla, z, w["wa"], w["ba"], (dz, at("ga")), name=nm("gla_gate_bwd"))
    d_pv, d_pg, d_pool_w, d_pool_scale = _pool_bwd(dy, z, w["pool_w"], w["pool_scale"], name=nm("pool_bwd"))
    dz, d_att_o, _ = _post_bwd(dy, 3, [sv["att_o"]], z, SEG["mg"][0] // 512, w["qg"], (dz, at("mg")), norm=False,
                               name=nm("mla_post_bwd"))
    if riding_parts is None:
        (dq, dk, dv), rode = _flash_bwd(sv["q"], sv["k"], sv["v"], d_att_o, sv["att_o"], sv["lse"],
                                        name=nm("attn_bwd")), None
    else:
        dq, dk, dv, rode = _flash_bwd(sv["q"], sv["k"], sv["v"], d_att_o, sv["att_o"], sv["lse"],
                                      rider=_rider_chip_exchange(riding_parts), name=nm("attn_bwd"))
    d_mq, d_mkv, d_mkr, d_wq, d_wkv, d_qg, d_kvg, d_qng, d_kng = _mla_pre_bwd(
        dq, dk, dv, z, w["qg"], w["wq"], w["kvg"], w["wkv"], w["qng"], w["kng"], mla_cos, mla_sp, mla_sn,
        name=nm("mla_pre_bwd"))
    for n, seg in dict(pv=d_pv, pg=d_pg, mq=d_mq, gq=d_gq, gk=d_gk, mkv=d_mkv, mkr=d_mkr).items():
        dz = lax.dynamic_update_slice(dz, seg, (0, at(n)))
    dh = _matmul(dz, w["w_in"], tn=512, name=nm("in_proj_dh"))
    d_w_in = _matmul(dz, sv["h"], ta=True, name=nm("in_proj_dw"))
    dx, d_norm_g = _rmsnorm_bwd(sv["x"], dh, w["norm_g"], dx_next, name=nm("norm_bwd"))
    sharded = dict(
        w_in=_w_in_unpadded(d_w_in),
        w_out=d_w_out,
        mla_wq_b=_cols_to_slots(_unpad_heads(d_wq, MLA_QK, MLA_QKP)),
        mla_wkv_b=_cols_to_slots(d_wkv),
        gla_wa2_f=_cols_to_slots(d_wa[0:GLA_RANK, 0:256]),
        gla_wa2_b=_cols_to_slots(d_wa[GLA_RANK:2 * GLA_RANK, 256:512]),
    )
    small = dict(
        norm_g=d_norm_g[0], ret_norm_g=d_ret_g[0], gla_ba_f=d_ba[0, :256], gla_ba_b=d_ba[0, 256:],
        gla_norm_g=d_gla_g[0], pool_w=d_pool_w.reshape(-1), pool_scale=d_pool_scale[0], mla_q_norm_g=d_qg[0],
        mla_kv_norm_g=d_kvg[0], mla_qk_norm_q=d_qng[0, :MLA_QK], mla_qk_norm_k=d_kng[0, :MLA_QK],
    )
    return dx, sharded, small, rode


SHARDED = ["w_in", "w_out", "mla_wq_b", "mla_wkv_b", "gla_wa2_f", "gla_wa2_b"]
WEIGHTS = ["norm_g", "w_in", "ret_norm_g", "gla_wa2_f", "gla_ba_f", "gla_wa2_b", "gla_ba_b", "gla_norm_g", "pool_w",
           "pool_scale", "mla_q_norm_g", "mla_wq_b", "mla_kv_norm_g", "mla_wkv_b", "mla_qk_norm_q", "mla_qk_norm_k",
           "w_out"]


SHARD_AXES = [1, 0, 0, 0, 0, 0]


def _layer_shards(p, l):
    return [jnp.swapaxes(p["w_in"], 1, 2)[l].astype(BF16), p["w_out"][l].astype(BF16), p["mla_wq_b"][l].astype(BF16),
            p["mla_wkv_b"][l].astype(BF16), p["gla_wa2_f"][l], p["gla_wa2_b"][l]]


def _step(p, where):
    x = p["x"][0]
    tabs = _rope_tables(x.shape[0])
    got0 = _gather_shards(_layer_shards(p, 0), SHARD_AXES, name="l0_gather_weights")
    w0 = _layer_weights(0, p, dict(zip(SHARDED, got0)))
    x1, sv0, got1 = _layer_fwd(0, x, w0, tabs, next_shards=_layer_shards(p, 1))
    w1 = _layer_weights(1, p, dict(zip(SHARDED, got1)))
    x2, sv1, _ = _layer_fwd(1, x1, w1, tabs)
    dx, loss = _loss_head(x2, p["loss_target"][0], name="loss_head")

    big, big_axes = SHARDED[:2], SHARD_AXES[:2]

    def pair_sums(tag, tensors, axes, names):
        return [_pair_reduce(a, where, ax, out_dtype=BF16, name=f"{tag}_pair_reduce_{n}")
                for a, ax, n in zip(tensors, axes, names)]

    def joined(tag, pair, others, axes, names):
        return [_sum_join(a, b, where, ax, name=f"{tag}_sum_join_{n}")
                for a, b, ax, n in zip(pair, others, axes, names)]

    dx, sharded1, small1, _ = _layer_bwd(1, dx, w1, sv1, tabs)
    pair1 = pair_sums("l1", [sharded1[n] for n in big], big_axes, big)
    dx, sharded0, small0, others1 = _layer_bwd(0, dx, w0, sv0, tabs, riding_parts=pair1, where=where)
    grads1 = joined("l1", pair1, others1, big_axes, big)
    packed = jnp.concatenate([sh[n].reshape(N_CHIP, -1, 128) for sh in (sharded0, sharded1) for n in SHARDED[2:]],
                             axis=1)
    last, last_axes, last_names = [sharded0["w_in"], packed], [SHARD_AXES[0], 0], ["w_in", "rest"]
    pair0 = pair_sums("l0", last, last_axes, last_names)
    g_w_in0, rest = joined("l0", pair0, _chip_exchange(pair0, name="l0_chip_exchange"), last_axes, last_names)
    (g_w_out0,) = joined("l0", [sharded0["w_out"][0]], [sharded0["w_out"][1]], [SHARD_AXES[1]], ["w_out"])
    grads = {n: jnp.stack([g0, g1]) for n, g0, g1 in zip(big, (g_w_in0, g_w_out0), grads1)}
    off = 0
    pieces = {n: [] for n in SHARDED[2:]}
    for sh in (sharded0, sharded1):
        for n in SHARDED[2:]:
            rows = sh[n].shape[1] * sh[n].shape[2] // 128
            pieces[n].append(rest[off:off + rows].reshape(sh[n].shape[1:]))
            off += rows
    grads.update({n: jnp.stack(v) for n, v in pieces.items()})
    small = {n: [small0[n], small1[n]] for n, _ in SMALL}
    small["loss"] = loss
    return dx[None], grads, small


def kernel(x, norm_g, w_in, ret_norm_g, gla_wa2_f, gla_ba_f, gla_wa2_b, gla_ba_b, gla_norm_g, pool_w, pool_scale, mla_q_norm_g, mla_wq_b, mla_kv_norm_g, mla_wkv_b, mla_qk_norm_q, mla_qk_norm_k, w_out, loss_target, m_norm_g, m_w_in, m_ret_norm_g, m_gla_wa2_f, m_gla_ba_f, m_gla_wa2_b, m_gla_ba_b, m_gla_norm_g, m_pool_w, m_pool_scale, m_mla_q_norm_g, m_mla_wq_b, m_mla_kv_norm_g, m_mla_wkv_b, m_mla_qk_norm_q, m_mla_qk_norm_k, m_w_out, v_norm_g, v_w_in, v_ret_norm_g, v_gla_wa2_f, v_gla_ba_f, v_gla_wa2_b, v_gla_ba_b, v_gla_norm_g, v_pool_w, v_pool_scale, v_mla_q_norm_g, v_mla_wq_b, v_mla_kv_norm_g, v_mla_wkv_b, v_mla_qk_norm_q, v_mla_qk_norm_k, v_w_out):
    p = dict(x=x, norm_g=norm_g, w_in=w_in, ret_norm_g=ret_norm_g, gla_wa2_f=gla_wa2_f, gla_ba_f=gla_ba_f,
             gla_wa2_b=gla_wa2_b, gla_ba_b=gla_ba_b, gla_norm_g=gla_norm_g, pool_w=pool_w, pool_scale=pool_scale,
             mla_q_norm_g=mla_q_norm_g, mla_wq_b=mla_wq_b, mla_kv_norm_g=mla_kv_norm_g, mla_wkv_b=mla_wkv_b,
             mla_qk_norm_q=mla_qk_norm_q, mla_qk_norm_k=mla_qk_norm_k, w_out=w_out, loss_target=loss_target)
    moments = dict(
        m=dict(norm_g=m_norm_g, w_in=m_w_in, ret_norm_g=m_ret_norm_g, gla_wa2_f=m_gla_wa2_f, gla_ba_f=m_gla_ba_f,
               gla_wa2_b=m_gla_wa2_b, gla_ba_b=m_gla_ba_b, gla_norm_g=m_gla_norm_g, pool_w=m_pool_w,
               pool_scale=m_pool_scale, mla_q_norm_g=m_mla_q_norm_g, mla_wq_b=m_mla_wq_b,
               mla_kv_norm_g=m_mla_kv_norm_g, mla_wkv_b=m_mla_wkv_b, mla_qk_norm_q=m_mla_qk_norm_q,
               mla_qk_norm_k=m_mla_qk_norm_k, w_out=m_w_out),
        v=dict(norm_g=v_norm_g, w_in=v_w_in, ret_norm_g=v_ret_norm_g, gla_wa2_f=v_gla_wa2_f, gla_ba_f=v_gla_ba_f,
               gla_wa2_b=v_gla_wa2_b, gla_ba_b=v_gla_ba_b, gla_norm_g=v_gla_norm_g, pool_w=v_pool_w,
               pool_scale=v_pool_scale, mla_q_norm_g=v_mla_q_norm_g, mla_wq_b=v_mla_wq_b,
               mla_kv_norm_g=v_mla_kv_norm_g, mla_wkv_b=v_mla_wkv_b, mla_qk_norm_q=v_mla_qk_norm_q,
               mla_qk_norm_k=v_mla_qk_norm_k, w_out=v_w_out))

    where = jnp.stack([lax.axis_index("c"), 2 * lax.axis_index("x") + lax.axis_index("y")]).astype(jnp.int32)
    grad_x, grads, small = _step(p, where)

    slots = _gather_all(_pack_small(small), name="gather_small")
    total = _unpack_small(_sum_slots(slots, name="sum_small"))
    for n, _ in SMALL:
        grads[n] = total[n].reshape(p[n].shape)
    loss = total["loss"]

    delta, new_m, new_v = {}, {}, {}
    for n in WEIGHTS:
        turn = (lambda a: jnp.swapaxes(a, 1, 2)) if n == "w_in" else (lambda a: a)
        outs = _adamw(turn(p[n]), grads[n], turn(moments["m"][n]), turn(moments["v"][n]), name=f"adamw_{n}")
        grads[n] = turn(grads[n])
        delta[n], new_m[n], new_v[n] = (turn(o) for o in outs)
    return (loss, grad_x, *[grads[n] for n in WEIGHTS], *[delta[n] for n in WEIGHTS],
            *[new_m[n] for n in WEIGHTS], *[new_v[n] for n in WEIGHTS])
```

```python
import jax
import jax.numpy as jnp
from jax import lax
from jax.experimental import pallas as pl
from jax.experimental.pallas import tpu as pltpu

F32 = jnp.float32
BF16 = jnp.bfloat16
MESH = pl.DeviceIdType.MESH

EPS = 1e-6
ROPE_THETA = 10000.0
DEPTH = 2
N_DEV = 8
N_CHIP = 4

GROUP_W = 512
RET_HEADS = 4
RET_HD = 128
RET_CHUNK = 256
GLA_HEADS = 4
GLA_DK = 64
GLA_DV = 128
GLA_RANK = 16
GLA_TAU = 16.0
GLA_CHUNK = 64
POOL_GROUPS = 4
POOL_GW = 128
POOL_HALO = 8
POOL_TILE = 256
MLA_HEADS = 4
MLA_NOPE = 128
MLA_ROPE = 64
MLA_QK = MLA_NOPE + MLA_ROPE
MLA_QKP = 256
MLA_V = 128
MLA_Q_RANK = 512
MLA_KV_RANK = 256
MLA_SCALE = MLA_QK ** -0.5

ADAM_LR = 0.001
ADAM_B1 = 0.9
ADAM_B2 = 0.999
ADAM_EPS = 1e-08
ADAM_WD = 0.01
ADAM_STEP = 10

VMEM_LIMIT = 56 * 1024 * 1024
ROW_TILE = 512

SEG = {
    "rq": (0, 512, 0, 512), "rk": (512, 512, 512, 512), "rv": (1024, 512, 1024, 512), "rg": (1536, 512, 1536, 512),
    "gv": (2048, 512, 2560, 512), "gg": (2560, 512, 3072, 512),
    "pv": (3072, 512, 3616, 512), "pg": (3584, 512, 4128, 512),
    "mq": (4096, 512, 4640, 512), "mg": (4608, 512, 5472, 512),
    "gq": (5120, 256, 2048, 256), "gk": (5376, 256, 2304, 256), "mkv": (5632, 256, 5152, 256),
    "ga": (5888, 128, 3584, 32), "mkr": (6016, 128, 5408, 64),
}
SEG_ORDER = ["rq", "rk", "rv", "rg", "gv", "gg", "pv", "pg", "mq", "mg", "gq", "gk", "mkv", "ga", "mkr"]
IN_COLS = 5984
IN_PAD = 6144
ORIG_ORDER = ["rq", "rk", "rv", "rg", "gq", "gk", "gv", "gg", "ga", "pv", "pg", "mq", "mkv", "mkr", "mg"]


def _cparams(*sem):
    return pltpu.CompilerParams(dimension_semantics=tuple(sem), vmem_limit_bytes=VMEM_LIMIT)


def _bf(v):
    return v.astype(BF16)


def _dot(a, b, ca=1, cb=0):
    return lax.dot_general(_bf(a), _bf(b), (((ca,), (cb,)), ((), ())), preferred_element_type=F32)


def _sigmoid(x):
    return 1.0 / (1.0 + jnp.exp(-x))


def _silu_parts(g):
    sg = _sigmoid(g)
    return g * sg, sg * (1.0 + g * (1.0 - sg))


class _Rider:
    def __init__(self, ins, outs, sems, start, finish, aliases=None):
        self.ins, self.outs, self.sems, self.start, self.finish = list(ins), list(outs), list(sems), start, finish
        self.aliases = dict(aliases or {})


def _ride(body, rider, n_in, n_out, grid):
    if rider is None:
        return body
    ri, ro, rs = len(rider.ins), len(rider.outs), len(rider.sems)

    def wrapped(*refs):
        ins, refs = refs[:n_in], refs[n_in:]
        rin, refs = refs[:ri], refs[ri:]
        outs, refs = refs[:n_out], refs[n_out:]
        rout, refs = refs[:ro], refs[ro:]
        scratch, sems = refs[:len(refs) - rs], refs[len(refs) - rs:]
        first = pl.program_id(0) == 0
        last = pl.program_id(0) == grid[0] - 1
        for ax in range(1, len(grid)):
            first = jnp.logical_and(first, pl.program_id(ax) == 0)
            last = jnp.logical_and(last, pl.program_id(ax) == grid[ax] - 1)

        @pl.when(first)
        def _():
            rider.start(rin, rout, sems)

        body(*ins, *outs, *scratch)

        @pl.when(last)
        def _():
            rider.finish(rin, rout, sems)

    return wrapped


def _ride_call(body, rider, *, name, grid, in_specs, out_specs, out_shape, scratch_shapes, args, sem):
    n_in, n_out = len(in_specs), len(out_specs)
    if rider is None:
        return pl.pallas_call(body, name=name, grid=grid, in_specs=in_specs, out_specs=out_specs, out_shape=out_shape,
                              scratch_shapes=scratch_shapes, compiler_params=_cparams(*sem))(*args), []
    outs = pl.pallas_call(
        _ride(body, rider, n_in, n_out, grid), name=name, grid=grid,
        in_specs=list(in_specs) + [ANY] * len(rider.ins), out_specs=list(out_specs) + [ANY] * len(rider.outs),
        out_shape=list(out_shape) + rider.outs, scratch_shapes=list(scratch_shapes) + rider.sems,
        input_output_aliases={n_in + i: n_out + o for i, o in rider.aliases.items()},
        compiler_params=_cparams(*(["arbitrary"] * len(grid))),
    )(*args, *rider.ins)
    return outs[:n_out], outs[n_out:]


def _matmul(a, b, *, ta=False, tb=False, out_dtype=F32, tm=512, tn=1024, tk=None, add=None, n_outer=True, rider=None,
            name):
    m, kdim = (a.shape[1], a.shape[0]) if ta else a.shape
    n = b.shape[0] if tb else b.shape[1]
    tm, tn = min(tm, m), min(tn, n)
    tk = kdim if tk is None else min(tk, kdim)
    assert m % tm == 0 and n % tn == 0 and kdim % tk == 0
    nk = kdim // tk
    ca, cb = (0 if ta else 1), (1 if tb else 0)

    def body(*refs):
        if add is None:
            a_ref, b_ref, o_ref = refs[:3]
            add_ref = None
        else:
            a_ref, b_ref, add_ref, o_ref = refs[:4]
        p = _dot(a_ref[...], b_ref[...], ca, cb)

        def finish(r):
            if add_ref is not None:
                r = r + add_ref[...]
            o_ref[...] = r.astype(out_dtype)

        if nk == 1:
            finish(p)
        else:
            acc = refs[-1]
            k = pl.program_id(2)

            @pl.when(k == 0)
            def _():
                acc[...] = p

            @pl.when(k > 0)
            def _():
                acc[...] += p

            @pl.when(k == nk - 1)
            def _():
                finish(acc[...])

    def ij(g0, g1):
        return (g1, g0) if n_outer else (g0, g1)

    a_spec = (pl.BlockSpec((tk, tm), lambda g0, g1, k: (k, ij(g0, g1)[0])) if ta
              else pl.BlockSpec((tm, tk), lambda g0, g1, k: (ij(g0, g1)[0], k)))
    b_spec = (pl.BlockSpec((tn, tk), lambda g0, g1, k: (ij(g0, g1)[1], k)) if tb
              else pl.BlockSpec((tk, tn), lambda g0, g1, k: (k, ij(g0, g1)[1])))
    o_spec = pl.BlockSpec((tm, tn), lambda g0, g1, k: ij(g0, g1))
    in_specs = [a_spec, b_spec] + ([o_spec] if add is not None else [])
    args = (a, b) + ((add,) if add is not None else ())
    grid = (n // tn, m // tm, nk) if n_outer else (m // tm, n // tn, nk)
    (out,), rode = _ride_call(
        body, rider, name=name, grid=grid, in_specs=in_specs, out_specs=[o_spec],
        out_shape=[jax.ShapeDtypeStruct((m, n), out_dtype)],
        scratch_shapes=[] if nk == 1 else [pltpu.VMEM((tm, tn), F32)], args=args,
        sem=("parallel", "parallel", "arbitrary"))
    return out if rider is None else (out, rode)


def _rmsnorm_fwd(x, g, *, name, tm=ROW_TILE):
    s, d = x.shape
    tm = min(tm, s)

    def body(x_ref, g_ref, h_ref):
        xv = x_ref[...]
        r = lax.rsqrt(jnp.mean(xv * xv, axis=-1, keepdims=True) + EPS)
        h_ref[...] = _bf(xv * r * g_ref[...])

    return pl.pallas_call(
        body, name=name, grid=(s // tm,),
        in_specs=[pl.BlockSpec((tm, d), lambda i: (i, 0)), pl.BlockSpec((1, d), lambda i: (0, 0))],
        out_specs=pl.BlockSpec((tm, d), lambda i: (i, 0)),
        out_shape=jax.ShapeDtypeStruct((s, d), BF16),
        compiler_params=_cparams("parallel"),
    )(x, g)


def _rmsnorm_bwd(x, dh, g, dres, *, name, tm=ROW_TILE):
    s, d = x.shape
    tm = min(tm, s)

    def body(x_ref, dh_ref, g_ref, dres_ref, dx_ref, dg_ref):
        i = pl.program_id(0)
        xv = x_ref[...]
        r = lax.rsqrt(jnp.mean(xv * xv, axis=-1, keepdims=True) + EPS)
        xn = xv * r
        dv = dh_ref[...]
        part = jnp.sum(dv * xn, axis=0, keepdims=True)

        @pl.when(i == 0)
        def _():
            dg_ref[...] = part

        @pl.when(i > 0)
        def _():
            dg_ref[...] += part

        dxn = dv * g_ref[...]
        dx_ref[...] = dres_ref[...] + r * (dxn - xn * jnp.mean(dxn * xn, axis=-1, keepdims=True))

    row = pl.BlockSpec((tm, d), lambda i: (i, 0))
    vec = pl.BlockSpec((1, d), lambda i: (0, 0))
    return pl.pallas_call(
        body, name=name, grid=(s // tm,), in_specs=[row, row, vec, row], out_specs=[row, vec],
        out_shape=[jax.ShapeDtypeStruct((s, d), F32), jax.ShapeDtypeStruct((1, d), F32)],
        compiler_params=_cparams("arbitrary"),
    )(x, dh, g, dres)


def _loss_head(xf, target, *, name, tm=ROW_TILE):
    s, d = xf.shape
    tm = min(tm, s)

    def body(x_ref, t_ref, dx_ref, l_ref):
        i = pl.program_id(0)
        e = x_ref[...] - t_ref[...]
        dx_ref[...] = e * (1.0 / d)
        rows = jnp.mean(e * e, axis=-1, keepdims=True)
        part = 0.5 * jnp.sum(rows, axis=0, keepdims=True)

        @pl.when(i == 0)
        def _():
            l_ref[...] = part

        @pl.when(i > 0)
        def _():
            l_ref[...] += part

    row = pl.BlockSpec((tm, d), lambda i: (i, 0))
    return pl.pallas_call(
        body, name=name, grid=(s // tm,), in_specs=[row, row],
        out_specs=[row, pl.BlockSpec((1, 1), lambda i: (0, 0))],
        out_shape=[jax.ShapeDtypeStruct((s, d), F32), jax.ShapeDtypeStruct((1, 1), F32)],
        compiler_params=_cparams("arbitrary"),
    )(xf, target)


def _rope_tables(s):
    pos = jnp.arange(s, dtype=F32)[:, None]
    inv_r = 1.0 / (ROPE_THETA ** (jnp.arange(0, RET_HD, 2, dtype=F32) / RET_HD))
    ang = pos * inv_r[None, :]
    ret_cos = jnp.concatenate([jnp.cos(ang), jnp.cos(ang)], axis=1)
    ret_sin = jnp.concatenate([-jnp.sin(ang), jnp.sin(ang)], axis=1)
    inv_m = 1.0 / (ROPE_THETA ** (jnp.arange(0, MLA_ROPE, 2, dtype=F32) / MLA_ROPE))
    am = pos * inv_m[None, :]
    z32, z64 = jnp.zeros((s, 32), F32), jnp.zeros((s, 64), F32)
    mla_cos = jnp.concatenate([jnp.cos(am), jnp.cos(am), z64], axis=1)
    mla_sp = jnp.concatenate([z32, jnp.sin(am), z64], axis=1)
    mla_sn = jnp.concatenate([-jnp.sin(am), z32, z64], axis=1)
    return ret_cos, ret_sin, mla_cos, mla_sp, mla_sn


def _rope128(x, c, sg):
    return x * c + pltpu.roll(x, 64, 1) * sg


def _unrope128(d, c, sg):
    return d * c + pltpu.roll(d * sg, 64, 1)


def _rope64(t, c, sp, sn):
    return t * c + pltpu.roll(t, 96, 1) * sn + pltpu.roll(t, 32, 1) * sp


def _unrope64(d, c, sp, sn):
    return d * c + pltpu.roll(d * sn, 32, 1) + pltpu.roll(d * sp, 96, 1)


def _ret_pre(z, cos, sin, *, name, tm=ROW_TILE):
    s = z.shape[0]
    tm = min(tm, s)
    scale = RET_HD ** -0.5

    def body(q_ref, k_ref, c_ref, s_ref, qo_ref, ko_ref):
        c, sg = c_ref[...], s_ref[...]
        for h in range(RET_HEADS):
            sl = slice(h * RET_HD, (h + 1) * RET_HD)
            qo_ref[:, sl] = _rope128(q_ref[:, sl], c, sg)
            ko_ref[:, sl] = _rope128(k_ref[:, sl], c, sg) * scale

    seg = lambda j: pl.BlockSpec((tm, GROUP_W), lambda i: (i, j))
    tab = pl.BlockSpec((tm, RET_HD), lambda i: (i, 0))
    return pl.pallas_call(
        body, name=name, grid=(s // tm,), in_specs=[seg(0), seg(1), tab, tab],
        out_specs=[seg(0), seg(0)],
        out_shape=[jax.ShapeDtypeStruct((s, GROUP_W), F32)] * 2,
        compiler_params=_cparams("parallel"),
    )(z, z, cos, sin)


def _ret_pre_bwd(dqr, dkr, cos, sin, into, *, name, tm=ROW_TILE):
    s = dqr[0].shape[0]
    tm = min(tm, s)
    scale = RET_HD ** -0.5

    def body(dq0_ref, dq1_ref, dk0_ref, dk1_ref, c_ref, s_ref, _, o_ref):
        c, sg = c_ref[...], s_ref[...]
        for h in range(RET_HEADS):
            sl = slice(h * RET_HD, (h + 1) * RET_HD)
            ksl = slice(GROUP_W + h * RET_HD, GROUP_W + (h + 1) * RET_HD)
            o_ref[:, sl] = _bf(_unrope128(dq0_ref[:, sl] + dq1_ref[:, sl], c, sg))
            o_ref[:, ksl] = _bf(_unrope128(dk0_ref[:, sl] + dk1_ref[:, sl], c, sg) * scale)

    row = pl.BlockSpec((tm, GROUP_W), lambda i: (i, 0))
    tab = pl.BlockSpec((tm, RET_HD), lambda i: (i, 0))
    out_shape, out_spec, more_specs, more_args = _landing(into, tm, 2 * GROUP_W)
    return pl.pallas_call(
        body, name=name, grid=(s // tm,), in_specs=[row, row, row, row, tab, tab] + more_specs, out_specs=out_spec,
        out_shape=out_shape, input_output_aliases={6: 0},
        compiler_params=_cparams("parallel"),
    )(dqr[0], dqr[1], dkr[0], dkr[1], cos, sin, *more_args)


def _bla(a, b, c, lg, cols, *, name):
    s = a.shape[0]
    ch = min(RET_CHUNK, s)
    n = s // ch
    hd = RET_HD

    def body(lg_ref, a0, b0, c0, a1, b1, c1, o0, o1, st):
        t = pl.program_id(0)

        @pl.when(t == 0)
        def _():
            st[...] = jnp.zeros_like(st)

        ii = lax.broadcasted_iota(jnp.int32, (ch, ch), 0)
        jj = lax.broadcasted_iota(jnp.int32, (ch, ch), 1)
        idx = lax.broadcasted_iota(jnp.int32, (ch, 1), 0).astype(F32)
        for d, (a_ref, b_ref, c_ref, o_ref) in enumerate(((a0, b0, c0, o0), (a1, b1, c1, o1))):
            diff = ((ii - jj) if d == 0 else (jj - ii)).astype(F32)
            keep = diff >= 0
            dpos = jnp.maximum(diff, 0.0)
            pq = (idx + 1.0) if d == 0 else (ch - idx)
            pk = (ch - 1.0 - idx) if d == 0 else idx
            for h in range(RET_HEADS):
                g = lg_ref[d, h]
                sl = slice(h * hd, (h + 1) * hd)
                av, bv, cv = a_ref[:, sl], b_ref[:, sl], c_ref[:, sl]
                sc = _dot(av, bv, 1, 1) * jnp.where(keep, jnp.exp(dpos * g), 0.0)
                stv = st[d, h]
                o_ref[:, sl] = _dot(sc, cv) + _dot(av * jnp.exp(pq * g), stv)
                st[d, h] = jnp.exp(ch * g) * stv + _dot(bv * jnp.exp(pk * g), cv, 0, 0)

    fwd = lambda j: pl.BlockSpec((ch, GROUP_W), lambda t: (t, j))
    bwd = lambda j: pl.BlockSpec((ch, GROUP_W), lambda t: (n - 1 - t, j))
    return pl.pallas_call(
        body, name=name, grid=(n,),
        in_specs=[pl.BlockSpec(memory_space=pltpu.SMEM), fwd(cols[0]), fwd(cols[1]), fwd(cols[2]),
                  bwd(cols[0]), bwd(cols[1]), bwd(cols[2])],
        out_specs=[fwd(0), bwd(0)],
        out_shape=[jax.ShapeDtypeStruct((s, GROUP_W), F32)] * 2,
        scratch_shapes=[pltpu.VMEM((2, RET_HEADS, hd, hd), F32)],
        compiler_params=_cparams("arbitrary"),
    )(lg, a, b, c, a, b, c)


def _post(os_, zg, gcol, g, *, norm, name, tm=ROW_TILE):
    s = zg.shape[0]
    tm = min(tm, s)
    nd = len(os_)

    def body(*refs):
        o_refs, (gt_ref, g_ref, y_ref) = refs[:nd], refs[nd:]
        silu, _ = _silu_parts(gt_ref[...])
        for h in range(4):
            sl = slice(h * 128, (h + 1) * 128)
            o = o_refs[0][:, sl]
            for k in range(1, nd):
                o = o + o_refs[k][:, sl]
            if norm:
                r = lax.rsqrt(jnp.mean(o * o, axis=-1, keepdims=True) + EPS)
                o = o * r * g_ref[:, sl]
            y_ref[:, sl] = _bf(silu[:, sl] * o)

    row = pl.BlockSpec((tm, GROUP_W), lambda i: (i, 0))
    return pl.pallas_call(
        body, name=name, grid=(s // tm,),
        in_specs=[row] * nd + [pl.BlockSpec((tm, GROUP_W), lambda i: (i, gcol)),
                               pl.BlockSpec((1, GROUP_W), lambda i: (0, 0))],
        out_specs=row,
        out_shape=jax.ShapeDtypeStruct((s, GROUP_W), BF16),
        compiler_params=_cparams("parallel"),
    )(*os_, zg, g)


def _post_bwd(dy, ycol, os_, zg, gcol, g, into, *, norm, name, tm=ROW_TILE):
    s = zg.shape[0]
    tm = min(tm, s)
    nd = len(os_)

    def body(*refs):
        dy_ref, o_refs = refs[0], refs[1:1 + nd]
        gt_ref, g_ref, _, dgt_ref, do_ref, dg_ref = refs[1 + nd:]
        i = pl.program_id(0)
        silu, dsilu = _silu_parts(gt_ref[...])
        dyv = dy_ref[...]
        parts = []
        for h in range(4):
            sl = slice(h * 128, (h + 1) * 128)
            o = o_refs[0][:, sl]
            for k in range(1, nd):
                o = o + o_refs[k][:, sl]
            dn = dyv[:, sl] * silu[:, sl]
            if norm:
                r = lax.rsqrt(jnp.mean(o * o, axis=-1, keepdims=True) + EPS)
                xn = o * r
                gh = g_ref[:, sl]
                dgt_ref[:, sl] = _bf(dyv[:, sl] * (xn * gh) * dsilu[:, sl])
                parts.append(jnp.sum(dn * xn, axis=0, keepdims=True))
                dxn = dn * gh
                do_ref[:, sl] = r * (dxn - xn * jnp.mean(dxn * xn, axis=-1, keepdims=True))
            else:
                dgt_ref[:, sl] = _bf(dyv[:, sl] * o * dsilu[:, sl])
                parts.append(jnp.zeros((1, 128), F32))
                do_ref[:, sl] = dn
        part = jnp.concatenate(parts, axis=1)

        @pl.when(i == 0)
        def _():
            dg_ref[...] = part

        @pl.when(i > 0)
        def _():
            dg_ref[...] += part

    row = pl.BlockSpec((tm, GROUP_W), lambda i: (i, 0))
    vec = pl.BlockSpec((1, GROUP_W), lambda i: (0, 0))
    dgt_shape, dgt_spec, more_specs, more_args = _landing(into, tm, GROUP_W)
    n_in = nd + 3
    return pl.pallas_call(
        body, name=name, grid=(s // tm,),
        in_specs=[pl.BlockSpec((tm, GROUP_W), lambda i: (i, ycol))] + [row] * nd
        + [pl.BlockSpec((tm, GROUP_W), lambda i: (i, gcol)), vec] + more_specs,
        out_specs=[dgt_spec, row, vec],
        out_shape=[dgt_shape, jax.ShapeDtypeStruct((s, GROUP_W), F32), jax.ShapeDtypeStruct((1, GROUP_W), F32)],
        input_output_aliases={n_in: 0},
        compiler_params=_cparams("arbitrary"),
    )(dy, *os_, zg, g, *more_args)


def _ret_log_gamma(swap):
    gf = 1.0 - 2.0 ** (-5.0 - jnp.arange(RET_HEADS, dtype=F32))
    lf, lb = jnp.log(gf), jnp.log(gf[::-1])
    return jnp.stack([lb, lf] if swap else [lf, lb])


def _log_sigmoid(x):
    return jnp.minimum(x, 0.0) - jnp.log(1.0 + jnp.exp(-jnp.abs(x)))


def _gla_gate(z, wa, ba, *, name, tm=ROW_TILE):
    s = z.shape[0]
    tm = min(tm, s)
    col = SEG["ga"][0] // 128

    def body(ga_ref, wa_ref, ba_ref, la_ref):
        pre = _dot(ga_ref[...], wa_ref[...]) + ba_ref[...]
        la_ref[...] = _log_sigmoid(pre) / GLA_TAU

    return pl.pallas_call(
        body, name=name, grid=(s // tm,),
        in_specs=[pl.BlockSpec((tm, 128), lambda i: (i, col)), pl.BlockSpec((128, 512), lambda i: (0, 0)),
                  pl.BlockSpec((1, 512), lambda i: (0, 0))],
        out_specs=pl.BlockSpec((tm, 512), lambda i: (i, 0)),
        out_shape=jax.ShapeDtypeStruct((s, 512), F32),
        compiler_params=_cparams("parallel"),
    )(z, wa, ba)


def _gla_gate_bwd(dla, z, wa, ba, into, *, name, tm=ROW_TILE):
    s = z.shape[0]
    tm = min(tm, s)
    col = SEG["ga"][0] // 128

    def body(dla0_ref, dla1_ref, ga_ref, wa_ref, ba_ref, _, dga_ref, dwa_ref, dba_ref):
        i = pl.program_id(0)
        gav = ga_ref[...]
        pre = _dot(gav, wa_ref[...]) + ba_ref[...]
        dla_v = jnp.concatenate([dla0_ref[...], dla1_ref[...]], axis=1)
        dpre = dla_v * (1.0 - _sigmoid(pre)) * (1.0 / GLA_TAU)
        dga_ref[...] = _bf(_dot(dpre, wa_ref[...], 1, 1))
        pw = _dot(gav, dpre, 0, 0)
        pb = jnp.sum(dpre, axis=0, keepdims=True)

        @pl.when(i == 0)
        def _():
            dwa_ref[...] = pw
            dba_ref[...] = pb

        @pl.when(i > 0)
        def _():
            dwa_ref[...] += pw
            dba_ref[...] += pb

    dga_shape, dga_spec, more_specs, more_args = _landing(into, tm, 128)
    return pl.pallas_call(
        body, name=name, grid=(s // tm,),
        in_specs=[pl.BlockSpec((tm, 256), lambda i: (i, 0)), pl.BlockSpec((tm, 256), lambda i: (i, 0)),
                  pl.BlockSpec((tm, 128), lambda i: (i, col)),
                  pl.BlockSpec((128, 512), lambda i: (0, 0)), pl.BlockSpec((1, 512), lambda i: (0, 0))] + more_specs,
        out_specs=[dga_spec, pl.BlockSpec((128, 512), lambda i: (0, 0)), pl.BlockSpec((1, 512), lambda i: (0, 0))],
        out_shape=[dga_shape, jax.ShapeDtypeStruct((128, 512), F32), jax.ShapeDtypeStruct((1, 512), F32)],
        input_output_aliases={5: 0},
        compiler_params=_cparams("arbitrary"),
    )(dla[0], dla[1], z, wa, ba, *more_args)


def _gla_masks(ch):
    ii = lax.broadcasted_iota(jnp.int32, (ch, ch), 0)
    tt = lax.broadcasted_iota(jnp.int32, (ch, ch), 1)
    return jnp.where(tt <= ii, 1.0, 0.0), jnp.where(tt >= ii, 1.0, 0.0)


def _running_sum(x, up):
    n = x.shape[0]
    rows = lax.broadcasted_iota(jnp.int32, x.shape, 0)
    k = 1
    while k < n:
        if up:
            x = x + jnp.where(rows < n - k, pltpu.roll(x, n - k, 0), 0.0)
        else:
            x = x + jnp.where(rows >= k, pltpu.roll(x, k, 0), 0.0)
        k *= 2
    return x


def _gla_chunk(d, tmat, qv, kv, lav, ch):
    c = _running_sum(lav, up=(d == 1))
    big_l = c[ch - 1:ch, :] if d == 0 else c[0:1, :]
    qt = qv * (GLA_DK ** -0.5) * jnp.exp(c)
    kt = kv * jnp.exp(-c)
    kh = kv * jnp.exp(big_l - c)
    return c, big_l, qt, kt, kh


def _gla_fwd(qh, kh_, z, la, *, name, rider=None):
    s = z.shape[0]
    ch = min(GLA_CHUNK, s)
    n = s // ch
    vcol = SEG["gv"][0] // GROUP_W

    def body(q0, k0, v0, la0, q1, k1, v1, la1, o0, o1, zs0, zs1, st):
        t = pl.program_id(0)

        @pl.when(t == 0)
        def _():
            st[...] = jnp.zeros_like(st)

        masks = _gla_masks(ch)
        for d, (q_ref, k_ref, v_ref, la_ref, o_ref, zs_ref) in enumerate(
                ((q0, k0, v0, la0, o0, zs0), (q1, k1, v1, la1, o1, zs1))):
            for h in range(GLA_HEADS):
                c, big_l, qt, kt, kh = _gla_chunk(d, masks[d], q_ref[h], k_ref[h], la_ref[0, h], ch)
                vv = v_ref[:, h * GLA_DV:(h + 1) * GLA_DV]
                p = _dot(qt, kt, 1, 1) * masks[d]
                zst = st[d, h]
                o_ref[:, h * GLA_DV:(h + 1) * GLA_DV] = _dot(p, vv) + _dot(qt, zst, 1, 1)
                zs_ref[h, 0] = zst
                st[d, h] = zst * jnp.exp(big_l) + _dot(vv, kh, 0, 0)

    cidx = (lambda t: t), (lambda t: n - 1 - t)
    hs = lambda d: pl.BlockSpec((GLA_HEADS, ch, GLA_DK), lambda t: (0, cidx[d](t), 0))
    vs = lambda d: pl.BlockSpec((ch, GROUP_W), lambda t: (cidx[d](t), vcol))
    las = lambda d: pl.BlockSpec((1, GLA_HEADS, ch, GLA_DK), lambda t: (d, 0, cidx[d](t), 0))
    os_ = lambda d: pl.BlockSpec((ch, GROUP_W), lambda t: (cidx[d](t), 0))
    zss = lambda d: pl.BlockSpec((GLA_HEADS, 1, GLA_DV, GLA_DK), lambda t: (0, cidx[d](t), 0, 0))
    (o0, o1, zs0, zs1), rode = _ride_call(
        body, rider, name=name, grid=(n,),
        in_specs=[hs(0), hs(0), vs(0), las(0), hs(1), hs(1), vs(1), las(1)],
        out_specs=[os_(0), os_(1), zss(0), zss(1)],
        out_shape=[jax.ShapeDtypeStruct((s, GROUP_W), F32)] * 2
        + [jax.ShapeDtypeStruct((GLA_HEADS, n, GLA_DV, GLA_DK), F32)] * 2,
        scratch_shapes=[pltpu.VMEM((2, GLA_HEADS, GLA_DV, GLA_DK), F32)],
        args=(qh, kh_, z, la, qh, kh_, z, la), sem=("arbitrary",))
    return ((o0, o1), (zs0, zs1)) if rider is None else ((o0, o1), (zs0, zs1), rode)


def _gla_bwd(qh, kh_, z, la, do, zs, *, name, rider=None):
    s = z.shape[0]
    ch = min(GLA_CHUNK, s)
    n = s // ch
    vcol = SEG["gv"][0] // GROUP_W

    def body(q0, k0, v0, la0, do0, zs0, q1, k1, v1, la1, do1, zs1,
             dq0, dk0, dla0, dv0, dq1, dk1, dla1, dv1, gz):
        t = pl.program_id(0)

        @pl.when(t == 0)
        def _():
            gz[...] = jnp.zeros_like(gz)

        masks = _gla_masks(ch)
        rows = lax.broadcasted_iota(jnp.int32, (ch, 1), 0)
        for d, (q_ref, k_ref, v_ref, la_ref, do_ref, zs_ref, dq_ref, dk_ref, dla_ref, dv_ref) in enumerate(
                ((q0, k0, v0, la0, do0, zs0, dq0, dk0, dla0, dv0), (q1, k1, v1, la1, do1, zs1, dq1, dk1, dla1, dv1))):
            tmat = masks[d]
            end = ch - 1 if d == 0 else 0
            for h in range(GLA_HEADS):
                ksl = slice(h * GLA_DK, (h + 1) * GLA_DK)
                c, big_l, qt, kt, kh = _gla_chunk(d, tmat, q_ref[h], k_ref[h], la_ref[0, h], ch)
                vsl = slice(h * GLA_DV, (h + 1) * GLA_DV)
                vv, dov, zst, gzv = v_ref[:, vsl], do_ref[:, vsl], zs_ref[h, 0], gz[d, h]
                p = _dot(qt, kt, 1, 1) * tmat
                dp = _dot(dov, vv, 1, 1) * tmat
                dqt = _dot(dp, kt) + _dot(dov, zst)
                dkt = _dot(dp, qt, 0, 0)
                dkh = _dot(vv, gzv)
                dv_ref[:, vsl] = _dot(p, dov, 0, 0) + _dot(kh, gzv, 1, 1)
                dq_ref[:, ksl] = dqt * jnp.exp(c) * (GLA_DK ** -0.5)
                dk_ref[:, ksl] = dkt * jnp.exp(-c) + dkh * jnp.exp(big_l - c)
                e_l = jnp.exp(big_l)
                d_l = jnp.sum(dkh * kh, axis=0, keepdims=True) + e_l * jnp.sum(zst * gzv, axis=0, keepdims=True)
                dc = dqt * qt - dkt * kt - dkh * kh + jnp.where(rows == end, d_l, 0.0)
                dla_ref[:, ksl] = _running_sum(dc, up=(d == 0))
                gz[d, h] = gzv * e_l + _dot(dov, qt, 0, 0)

    cidx = (lambda t: n - 1 - t), (lambda t: t)
    hs = lambda d: pl.BlockSpec((GLA_HEADS, ch, GLA_DK), lambda t: (0, cidx[d](t), 0))
    vs = lambda d: pl.BlockSpec((ch, GROUP_W), lambda t: (cidx[d](t), vcol))
    las = lambda d: pl.BlockSpec((1, GLA_HEADS, ch, GLA_DK), lambda t: (d, 0, cidx[d](t), 0))
    row = lambda d: pl.BlockSpec((ch, GROUP_W), lambda t: (cidx[d](t), 0))
    zss = lambda d: pl.BlockSpec((GLA_HEADS, 1, GLA_DV, GLA_DK), lambda t: (0, cidx[d](t), 0, 0))
    kw = GLA_HEADS * GLA_DK
    ks = lambda d: pl.BlockSpec((ch, kw), lambda t: (cidx[d](t), 0))
    hshape = jax.ShapeDtypeStruct((s, kw), F32)
    wide = jax.ShapeDtypeStruct((s, GROUP_W), F32)
    outs, rode = _ride_call(
        body, rider, name=name, grid=(n,),
        in_specs=[hs(0), hs(0), vs(0), las(0), row(0), zss(0), hs(1), hs(1), vs(1), las(1), row(1), zss(1)],
        out_specs=[ks(0), ks(0), ks(0), row(0), ks(1), ks(1), ks(1), row(1)],
        out_shape=[hshape, hshape, hshape, wide, hshape, hshape, hshape, wide],
        scratch_shapes=[pltpu.VMEM((2, GLA_HEADS, GLA_DV, GLA_DK), F32)],
        args=(qh, kh_, z, la, do, zs[0], qh, kh_, z, la, do, zs[1]), sem=("arbitrary",))
    dq0, dk0, dla0, dv0, dq1, dk1, dla1, dv1 = outs
    res = ((dq0, dq1), (dk0, dk1), (dla0, dla1), (dv0, dv1))
    return res if rider is None else res + (rode,)


def _window_sums(win, g, shift):
    n = win.shape[0]
    levels, y = [], win
    for j in range(POOL_GROUPS):
        y = y + pltpu.roll(y, n - (1 << j), 0)
        levels.append(y)
    sums = levels[-1]
    for j in range(POOL_GROUPS - 2, -1, -1):
        sums = jnp.where(g == j, levels[j], sums)
    return pltpu.roll(sums, shift, 0)


def _pool_cnt(t0, half, rows, s):
    t = t0 + lax.broadcasted_iota(jnp.int32, (rows, 1), 0)
    return (jnp.minimum(t + half, s) - jnp.maximum(t - half, 0)).astype(F32)


def _pool_fwd(z, pw, scale, *, name):
    s = z.shape[0]
    tl = min(POOL_TILE, s)
    nt = s // tl
    ucol, gcol = SEG["pv"][0] // 128, SEG["pg"][0] // 128

    def body(u_ref, gt_ref, pw_ref, sc_ref, y_ref, pad):
        g = pl.program_id(0)
        half = jnp.left_shift(1, g)
        pad[0:POOL_HALO, :] = jnp.zeros((POOL_HALO, POOL_GW), F32)
        pad[POOL_HALO + s:POOL_HALO + s + POOL_HALO, :] = jnp.zeros((POOL_HALO, POOL_GW), F32)
        pad[POOL_HALO:POOL_HALO + s, :] = u_ref[...]
        pwv, scv = pw_ref[0], sc_ref[...]

        def tile(i, carry):
            t0 = pl.multiple_of(i * tl, tl)
            win = pad[pl.ds(t0, tl + 2 * POOL_HALO), :]
            u = win[POOL_HALO:POOL_HALO + tl, :]
            pooled = _window_sums(win, g, half)[POOL_HALO:POOL_HALO + tl, :] / _pool_cnt(t0, half, tl, s) - u
            mixed = _dot(pooled, pwv)
            silu, _ = _silu_parts(gt_ref[pl.ds(t0, tl), :])
            y_ref[pl.ds(t0, tl), :] = _bf(silu * (mixed * scv))
            return carry

        lax.fori_loop(0, nt, tile, 0)

    return pl.pallas_call(
        body, name=name, grid=(POOL_GROUPS,),
        in_specs=[pl.BlockSpec((s, POOL_GW), lambda g: (0, ucol + g)),
                  pl.BlockSpec((s, POOL_GW), lambda g: (0, gcol + g)),
                  pl.BlockSpec((1, POOL_GW, POOL_GW), lambda g: (g, 0, 0)),
                  pl.BlockSpec((1, POOL_GW), lambda g: (0, g))],
        out_specs=pl.BlockSpec((s, POOL_GW), lambda g: (0, g)),
        out_shape=jax.ShapeDtypeStruct((s, GROUP_W), BF16),
        scratch_shapes=[pltpu.VMEM((s + 2 * POOL_HALO, POOL_GW), F32)],
        compiler_params=_cparams("parallel"),
    )(z, z, pw, scale)


def _pool_bwd(dy, z, pw, scale, *, name):
    s = z.shape[0]
    tl = min(POOL_TILE, s)
    nt = s // tl
    ucol, gcol, ycol = SEG["pv"][0] // 128, SEG["pg"][0] // 128, 2 * GROUP_W // 128

    def body(dy_ref, u_ref, gt_ref, pw_ref, sc_ref, du_ref, dgt_ref, dpw_ref, dsc_ref, pad, epad, dpo):
        g = pl.program_id(0)
        half = jnp.left_shift(1, g)
        zeros = jnp.zeros((POOL_HALO, POOL_GW), F32)
        for buf in (pad, epad):
            buf[0:POOL_HALO, :] = zeros
            buf[POOL_HALO + s:POOL_HALO + s + POOL_HALO, :] = zeros
        pad[POOL_HALO:POOL_HALO + s, :] = u_ref[...]
        pwv, scv = pw_ref[0], sc_ref[...]
        dpw_ref[0] = jnp.zeros((POOL_GW, POOL_GW), F32)
        dsc_ref[...] = jnp.zeros((1, POOL_GW), F32)

        def tile(i, carry):
            t0 = pl.multiple_of(i * tl, tl)
            win = pad[pl.ds(t0, tl + 2 * POOL_HALO), :]
            u = win[POOL_HALO:POOL_HALO + tl, :]
            cnt = _pool_cnt(t0, half, tl, s)
            pooled = _window_sums(win, g, half)[POOL_HALO:POOL_HALO + tl, :] / cnt - u
            mixed = _dot(pooled, pwv)
            silu, dsilu = _silu_parts(gt_ref[pl.ds(t0, tl), :])
            dyv = dy_ref[pl.ds(t0, tl), :]
            dgt_ref[pl.ds(t0, tl), :] = _bf(dyv * (mixed * scv) * dsilu)
            dsc_ref[...] += jnp.sum(dyv * silu * mixed, axis=0, keepdims=True)
            dm = dyv * silu * scv
            dpw_ref[0] += _dot(pooled, dm, 0, 0)
            dpooled = _dot(dm, pwv, 1, 1)
            dpo[pl.ds(t0, tl), :] = dpooled
            epad[pl.ds(POOL_HALO + t0, tl), :] = dpooled / cnt
            return carry

        lax.fori_loop(0, nt, tile, 0)

        def tile2(i, carry):
            t0 = pl.multiple_of(i * tl, tl)
            ewin = epad[pl.ds(t0, tl + 2 * POOL_HALO), :]
            du_ref[pl.ds(t0, tl), :] = _bf(_window_sums(ewin, g, half - 1)[POOL_HALO:POOL_HALO + tl, :]
                                           - dpo[pl.ds(t0, tl), :])
            return carry

        lax.fori_loop(0, nt, tile2, 0)

    col = lambda c0: pl.BlockSpec((s, POOL_GW), lambda g: (0, c0 + g))
    return pl.pallas_call(
        body, name=name, grid=(POOL_GROUPS,),
        in_specs=[col(ycol), col(ucol), col(gcol), pl.BlockSpec((1, POOL_GW, POOL_GW), lambda g: (g, 0, 0)),
                  pl.BlockSpec((1, POOL_GW), lambda g: (0, g))],
        out_specs=[col(0), col(0), pl.BlockSpec((1, POOL_GW, POOL_GW), lambda g: (g, 0, 0)),
                   pl.BlockSpec((1, POOL_GW), lambda g: (0, g))],
        out_shape=[jax.ShapeDtypeStruct((s, GROUP_W), BF16), jax.ShapeDtypeStruct((s, GROUP_W), BF16),
                   jax.ShapeDtypeStruct((POOL_GROUPS, POOL_GW, POOL_GW), F32),
                   jax.ShapeDtypeStruct((1, GROUP_W), F32)],
        scratch_shapes=[pltpu.VMEM((s + 2 * POOL_HALO, POOL_GW), F32), pltpu.VMEM((s + 2 * POOL_HALO, POOL_GW), F32),
                        pltpu.VMEM((s, POOL_GW), F32)],
        compiler_params=_cparams("parallel"),
    )(dy, z, z, pw, scale)


def _mla_specs(tm):
    zq = pl.BlockSpec((tm, 512), lambda i: (i, SEG["mq"][0] // 512))
    zkv = pl.BlockSpec((tm, 256), lambda i: (i, SEG["mkv"][0] // 256))
    zkr = pl.BlockSpec((tm, 128), lambda i: (i, SEG["mkr"][0] // 128))
    full = lambda r, c: pl.BlockSpec((r, c), lambda i: (0, 0))
    tab = pl.BlockSpec((tm, 128), lambda i: (i, 0))
    weights = [full(1, 512), full(512, 1024), full(1, 256), full(256, 1024), full(1, 256), full(1, 256)]
    return [zq, zkv, zkr] + weights + [tab, tab, tab]


def _mla_project(xq_ref, xkv_ref, qg_ref, wq_ref, kvg_ref, wkv_ref):
    xq = xq_ref[...]
    r1 = lax.rsqrt(jnp.mean(xq * xq, axis=-1, keepdims=True) + EPS)
    xn1 = xq * r1
    qn = _bf(xn1 * qg_ref[...])
    qraw = _dot(qn, wq_ref[...])
    xkv = xkv_ref[...]
    r2 = lax.rsqrt(jnp.mean(xkv * xkv, axis=-1, keepdims=True) + EPS)
    xn2 = xkv * r2
    kvn = _bf(xn2 * kvg_ref[...])
    kvraw = _dot(kvn, wkv_ref[...])
    return r1, xn1, qn, qraw, r2, xn2, kvn, kvraw


def _mla_pre(z, qg, wq, kvg, wkv, qng, kng, cos, sp, sn, *, name, tm=ROW_TILE):
    s = z.shape[0]
    tm = min(tm, s)

    def body(xq_ref, xkv_ref, pe_ref, qg_ref, wq_ref, kvg_ref, wkv_ref, qng_ref, kng_ref, c_ref, sp_ref, sn_ref,
             q_ref, k_ref, v_ref):
        _, _, _, qraw, _, _, _, kvraw = _mla_project(xq_ref, xkv_ref, qg_ref, wq_ref, kvg_ref, wkv_ref)
        c, spv, snv = c_ref[...], sp_ref[...], sn_ref[...]
        pe = pe_ref[...]
        pe_ss = jnp.sum(pe * pe, axis=-1, keepdims=True)
        qngv, kngv = qng_ref[...], kng_ref[...]
        for h in range(MLA_HEADS):
            b = h * MLA_QKP
            qh = qraw[:, b:b + MLA_QKP]
            r = lax.rsqrt(jnp.sum(qh * qh, axis=-1, keepdims=True) * (1.0 / MLA_QK) + EPS)
            qn_h = qh * r * qngv
            q_ref[:, b:b + 128] = _bf(qn_h[:, :128] * MLA_SCALE)
            q_ref[:, b + 128:b + 256] = _bf(_rope64(qn_h[:, 128:], c, spv, snv) * MLA_SCALE)
            kn = kvraw[:, b:b + 128]
            rk = lax.rsqrt((jnp.sum(kn * kn, axis=-1, keepdims=True) + pe_ss) * (1.0 / MLA_QK) + EPS)
            k_ref[:, b:b + 128] = _bf(kn * rk * kngv[:, :128])
            k_ref[:, b + 128:b + 256] = _bf(_rope64(pe * rk * kngv[:, 128:], c, spv, snv))
            v_ref[:, h * MLA_V:(h + 1) * MLA_V] = _bf(kvraw[:, b + 128:b + 256])

    row = lambda w: pl.BlockSpec((tm, w), lambda i: (i, 0))
    return pl.pallas_call(
        body, name=name, grid=(s // tm,), in_specs=_mla_specs(tm),
        out_specs=[row(1024), row(1024), row(512)],
        out_shape=[jax.ShapeDtypeStruct((s, 1024), BF16), jax.ShapeDtypeStruct((s, 1024), BF16),
                   jax.ShapeDtypeStruct((s, 512), BF16)],
        compiler_params=_cparams("parallel"),
    )(z, z, z, qg, wq, kvg, wkv, qng, kng, cos, sp, sn)


def _mla_pre_bwd(dq, dk, dv, z, qg, wq, kvg, wkv, qng, kng, cos, sp, sn, *, name, tm=ROW_TILE):
    s = z.shape[0]
    tm = min(tm, s)

    def body(dq_ref, dk_ref, dv_ref, xq_ref, xkv_ref, pe_ref, qg_ref, wq_ref, kvg_ref, wkv_ref, qng_ref, kng_ref,
             c_ref, sp_ref, sn_ref, dxq_ref, dxkv_ref, dpe_ref, dwq_ref, dwkv_ref, dqg_ref, dkvg_ref, dqng_ref,
             dkng_ref, dqraw, dkvraw):
        i = pl.program_id(0)
        r1, xn1, qn, qraw, r2, xn2, kvn, kvraw = _mla_project(xq_ref, xkv_ref, qg_ref, wq_ref, kvg_ref, wkv_ref)
        c, spv, snv = c_ref[...], sp_ref[...], sn_ref[...]
        pe = pe_ref[...]
        pe_ss = jnp.sum(pe * pe, axis=-1, keepdims=True)
        qngv, kngv = qng_ref[...], kng_ref[...]
        dqng = jnp.zeros((1, MLA_QKP), F32)
        dkng = jnp.zeros((1, MLA_QKP), F32)
        dpe = jnp.zeros_like(pe)
        for h in range(MLA_HEADS):
            b = h * MLA_QKP
            qh = qraw[:, b:b + MLA_QKP]
            r = lax.rsqrt(jnp.sum(qh * qh, axis=-1, keepdims=True) * (1.0 / MLA_QK) + EPS)
            xn = qh * r
            d_n = jnp.concatenate(
                [dq_ref[:, b:b + 128], _unrope64(dq_ref[:, b + 128:b + 256], c, spv, snv)], axis=1) * MLA_SCALE
            dqng = dqng + jnp.sum(d_n * xn, axis=0, keepdims=True)
            dxn = d_n * qngv
            dqraw[:, b:b + MLA_QKP] = _bf(r * (dxn - xn * (jnp.sum(dxn * xn, axis=-1, keepdims=True) * (1.0 / MLA_QK))))
            kn = kvraw[:, b:b + 128]
            rk = lax.rsqrt((jnp.sum(kn * kn, axis=-1, keepdims=True) + pe_ss) * (1.0 / MLA_QK) + EPS)
            xk = jnp.concatenate([kn, pe], axis=1) * rk
            d_k = jnp.concatenate(
                [dk_ref[:, b:b + 128], _unrope64(dk_ref[:, b + 128:b + 256], c, spv, snv)], axis=1)
            dkng = dkng + jnp.sum(d_k * xk, axis=0, keepdims=True)
            dxk = d_k * kngv
            dfull = rk * (dxk - xk * (jnp.sum(dxk * xk, axis=-1, keepdims=True) * (1.0 / MLA_QK)))
            dkvraw[:, b:b + 128] = _bf(dfull[:, :128])
            dkvraw[:, b + 128:b + 256] = _bf(dv_ref[:, h * MLA_V:(h + 1) * MLA_V])
            dpe = dpe + dfull[:, 128:]
        dpe_ref[...] = _bf(dpe)
        dqr, dkvr = dqraw[...], dkvraw[...]
        dqn = _dot(dqr, wq_ref[...], 1, 1)
        dxn1 = dqn * qg_ref[...]
        dxq_ref[...] = _bf(r1 * (dxn1 - xn1 * jnp.mean(dxn1 * xn1, axis=-1, keepdims=True)))
        dkvn = _dot(dkvr, wkv_ref[...], 1, 1)
        dxn2 = dkvn * kvg_ref[...]
        dxkv_ref[...] = _bf(r2 * (dxn2 - xn2 * jnp.mean(dxn2 * xn2, axis=-1, keepdims=True)))
        parts = (_dot(qn, dqr, 0, 0), _dot(kvn, dkvr, 0, 0), jnp.sum(dqn * xn1, axis=0, keepdims=True),
                 jnp.sum(dkvn * xn2, axis=0, keepdims=True), dqng, dkng)
        accs = (dwq_ref, dwkv_ref, dqg_ref, dkvg_ref, dqng_ref, dkng_ref)

        @pl.when(i == 0)
        def _():
            for a, p in zip(accs, parts):
                a[...] = p

        @pl.when(i > 0)
        def _():
            for a, p in zip(accs, parts):
                a[...] += p

    row = lambda w: pl.BlockSpec((tm, w), lambda i: (i, 0))
    full = lambda r, c: pl.BlockSpec((r, c), lambda i: (0, 0))
    return pl.pallas_call(
        body, name=name, grid=(s // tm,),
        in_specs=[row(1024), row(1024), row(512)] + _mla_specs(tm),
        out_specs=[row(512), row(256), row(128), full(512, 1024), full(256, 1024), full(1, 512), full(1, 256),
                   full(1, 256), full(1, 256)],
        out_shape=[jax.ShapeDtypeStruct((s, 512), BF16), jax.ShapeDtypeStruct((s, 256), BF16),
                   jax.ShapeDtypeStruct((s, 128), BF16), jax.ShapeDtypeStruct((512, 1024), F32),
                   jax.ShapeDtypeStruct((256, 1024), F32), jax.ShapeDtypeStruct((1, 512), F32),
                   jax.ShapeDtypeStruct((1, 256), F32), jax.ShapeDtypeStruct((1, 256), F32),
                   jax.ShapeDtypeStruct((1, 256), F32)],
        scratch_shapes=[pltpu.VMEM((tm, 1024), BF16), pltpu.VMEM((tm, 1024), BF16)],
        compiler_params=_cparams("arbitrary"),
    )(dq, dk, dv, z, z, z, qg, wq, kvg, wkv, qng, kng, cos, sp, sn)


def _flash_fwd(q, k, v, *, name, tq=1024, tk=1024, rider=None):
    s = q.shape[0]
    tq, tk = min(tq, s), min(tk, s)
    nk = s // tk

    def body(q_ref, k_ref, v_ref, o_ref, lse_ref, m_s, l_s, acc):
        j = pl.program_id(2)

        @pl.when(j == 0)
        def _():
            m_s[...] = jnp.full_like(m_s, -jnp.inf)
            l_s[...] = jnp.zeros_like(l_s)
            acc[...] = jnp.zeros_like(acc)

        sc = _dot(q_ref[...], k_ref[...], 1, 1)
        m_prev = m_s[...]
        m_new = jnp.maximum(m_prev, jnp.max(sc, axis=-1, keepdims=True))
        p = jnp.exp(sc - m_new[:, 0:1])
        alpha = jnp.exp(m_prev - m_new)
        l_s[...] = alpha * l_s[...] + jnp.sum(p, axis=-1, keepdims=True)
        acc[...] = alpha * acc[...] + _dot(p, v_ref[...])
        m_s[...] = m_new

        @pl.when(j == nk - 1)
        def _():
            o_ref[...] = acc[...] / l_s[...]
            lse_ref[...] = m_s[...] + jnp.log(l_s[...])

    (o, lse), rode = _ride_call(
        body, rider, name=name, grid=(MLA_HEADS, s // tq, nk),
        in_specs=[pl.BlockSpec((tq, MLA_QKP), lambda h, i, j: (i, h)),
                  pl.BlockSpec((tk, MLA_QKP), lambda h, i, j: (j, h)),
                  pl.BlockSpec((tk, MLA_V), lambda h, i, j: (j, h))],
        out_specs=[pl.BlockSpec((tq, MLA_V), lambda h, i, j: (i, h))] * 2,
        out_shape=[jax.ShapeDtypeStruct((s, GROUP_W), F32)] * 2,
        scratch_shapes=[pltpu.VMEM((tq, MLA_V), F32), pltpu.VMEM((tq, MLA_V), F32), pltpu.VMEM((tq, MLA_V), F32)],
        args=(q, k, v), sem=("parallel", "parallel", "arbitrary"))
    return (o, lse) if rider is None else (o, lse, rode)


def _flash_bwd(q, k, v, do, o, lse, *, name, tq=1024, tk=1024, rider=None):
    s = q.shape[0]
    tq, tk = min(tq, s), min(tk, s)
    nq, nk = s // tq, s // tk

    def body(q_ref, k_ref, v_ref, do_ref, o_ref, lse_ref, dq_ref, dk_ref, dv_ref, dk_acc, dv_acc):
        j, i = pl.program_id(1), pl.program_id(2)
        dov = do_ref[...]
        delta = jnp.sum(dov * o_ref[...], axis=-1, keepdims=True)
        p = jnp.exp(_dot(q_ref[...], k_ref[...], 1, 1) - lse_ref[:, 0:1])
        ds = p * (_dot(dov, v_ref[...], 1, 1) - delta)
        pv = _dot(p, dov, 0, 0)
        pk = _dot(ds, q_ref[...], 0, 0)
        pq = _dot(ds, k_ref[...])
        rows = pl.ds(pl.multiple_of(i * tq, tq), tq)

        @pl.when(j == 0)
        def _():
            dq_ref[rows, :] = pq

        @pl.when(j > 0)
        def _():
            dq_ref[rows, :] += pq

        @pl.when(i == 0)
        def _():
            dv_acc[...] = pv
            dk_acc[...] = pk

        @pl.when(i > 0)
        def _():
            dv_acc[...] += pv
            dk_acc[...] += pk

        @pl.when(i == nq - 1)
        def _():
            dk_ref[...] = dk_acc[...]
            dv_ref[...] = dv_acc[...]

    qb = pl.BlockSpec((tq, MLA_QKP), lambda h, j, i: (i, h))
    kb = pl.BlockSpec((tk, MLA_QKP), lambda h, j, i: (j, h))
    vb = pl.BlockSpec((tk, MLA_V), lambda h, j, i: (j, h))
    ob = pl.BlockSpec((tq, MLA_V), lambda h, j, i: (i, h))
    (dq, dk, dv), rode = _ride_call(
        body, rider, name=name, grid=(MLA_HEADS, nk, nq),
        in_specs=[qb, kb, vb, ob, ob, ob],
        out_specs=[pl.BlockSpec((s, MLA_QKP), lambda h, j, i: (0, h)), kb, vb],
        out_shape=[jax.ShapeDtypeStruct((s, MLA_HEADS * MLA_QKP), F32),
                   jax.ShapeDtypeStruct((s, MLA_HEADS * MLA_QKP), F32), jax.ShapeDtypeStruct((s, GROUP_W), F32)],
        scratch_shapes=[pltpu.VMEM((tk, MLA_QKP), F32), pltpu.VMEM((tk, MLA_V), F32)],
        args=(q, k, v, do, o, lse), sem=("arbitrary", "arbitrary", "arbitrary"))
    return (dq, dk, dv) if rider is None else (dq, dk, dv, rode)


def _rows_tile(r, c, itemsize=4, budget=2 * 1024 * 1024):
    if r * c * itemsize <= budget:
        return r
    best = None
    for t in range(8, r, 8):
        if r % t == 0 and t * c * itemsize <= budget:
            best = t
    return best if best is not None else r


def _landing(into, tm, width):
    buf, col = into
    assert col % width == 0
    return (jax.ShapeDtypeStruct(buf.shape, buf.dtype), pl.BlockSpec((tm, width), lambda i: (i, col // width)),
            [ANY], [buf])


def _add_n(arrs, *, out_dtype=F32, name, into=None):
    shape = arrs[0].shape
    c = shape[-1]
    flat = [a.reshape(-1, c) for a in arrs]
    r = flat[0].shape[0]
    t = _rows_tile(r, c)
    n_in = len(flat)

    def body(*refs):
        acc = refs[0][...].astype(F32)
        for ref in refs[1:n_in]:
            acc = acc + ref[...].astype(F32)
        refs[-1][...] = acc.astype(out_dtype)

    blk = pl.BlockSpec((t, c), lambda i: (i, 0))
    if into is not None:
        out_shape, out_spec, more_specs, more_args = _landing(into, t, c)
        return pl.pallas_call(
            body, name=name, grid=(r // t,), in_specs=[blk] * n_in + more_specs, out_specs=out_spec,
            out_shape=out_shape, input_output_aliases={n_in: 0}, compiler_params=_cparams("parallel"),
        )(*flat, *more_args)
    out = pl.pallas_call(
        body, name=name, grid=(r // t,), in_specs=[blk] * n_in, out_specs=blk,
        out_shape=jax.ShapeDtypeStruct((r, c), out_dtype), compiler_params=_cparams("parallel"),
    )(*flat)
    return out.reshape(shape)


def _adamw(w, g, m, v, *, name):
    shape = w.shape
    c = shape[-1]
    flat = [a.reshape(-1, c) for a in (w, g, m, v)]
    r = flat[0].shape[0]
    t = _rows_tile(r, c, budget=1024 * 1024)

    def body(w_ref, g_ref, m_ref, v_ref, d_ref, mo_ref, vo_ref):
        gv = g_ref[...]
        m2 = ADAM_B1 * m_ref[...] + (1.0 - ADAM_B1) * gv
        v2 = ADAM_B2 * v_ref[...] + (1.0 - ADAM_B2) * (gv * gv)
        m_hat = m2 / (1.0 - ADAM_B1 ** ADAM_STEP)
        v_hat = v2 / (1.0 - ADAM_B2 ** ADAM_STEP)
        d_ref[...] = -ADAM_LR * (m_hat / (jnp.sqrt(v_hat) + ADAM_EPS) + ADAM_WD * w_ref[...])
        mo_ref[...] = m2
        vo_ref[...] = v2

    blk = pl.BlockSpec((t, c), lambda i: (i, 0))
    outs = pl.pallas_call(
        body, name=name, grid=(r // t,), in_specs=[blk] * 4, out_specs=[blk] * 3,
        out_shape=[jax.ShapeDtypeStruct((r, c), F32)] * 3, compiler_params=_cparams("parallel"),
    )(*flat)
    return tuple(o.reshape(shape) for o in outs)


def _place():
    x, y, c = lax.axis_index("x"), lax.axis_index("y"), lax.axis_index("c")
    chips = [(1 - x, y), (x, 1 - y), (1 - x, 1 - y)]
    return x, y, c, chips


ANY = pl.BlockSpec(memory_space=pl.ANY)


def _half(ref, axis, hc, lead=()):
    n = ref.shape[len(lead) + axis] // 2
    return ref.at[tuple(lead) + (slice(None),) * axis + (pl.ds(hc * n, n),)]


def _gather_shards(shards, axes, *, name):
    nt = len(shards)

    def body(*refs):
        src, dst = refs[:nt], refs[nt:2 * nt]
        send, recv, fsend, frecv, lsem = refs[2 * nt:]
        x, y, c, chips = _place()
        me = 2 * x + y
        local = [pltpu.make_async_copy(src[t], dst[t].at[me], lsem.at[t]) for t in range(nt)]
        for cp in local:
            cp.start()

        def half(t, slot, hc):
            return _half(dst[t], axes[t], hc, lead=(slot,))

        def first(t, k):
            return pltpu.make_async_remote_copy(
                src_ref=_half(src[t], axes[t], c), dst_ref=half(t, me, c),
                send_sem=send.at[t, k], recv_sem=recv.at[t, k],
                device_id=(chips[k][0], chips[k][1], c), device_id_type=MESH)

        def landed(t, k):
            slot = 2 * chips[k][0] + chips[k][1]
            return pltpu.make_async_remote_copy(
                src_ref=half(t, slot, c), dst_ref=half(t, slot, c),
                send_sem=send.at[t, k], recv_sem=recv.at[t, k],
                device_id=(chips[k][0], chips[k][1], c), device_id_type=MESH)

        def forward(t, k, hc):
            slot = 2 * chips[k][0] + chips[k][1]
            return pltpu.make_async_remote_copy(
                src_ref=half(t, slot, hc), dst_ref=half(t, slot, hc),
                send_sem=fsend.at[t, k], recv_sem=frecv.at[t, k],
                device_id=(x, y, 1 - c), device_id_type=MESH)

        for t in range(nt):
            for k in range(3):
                first(t, k).start()
        for t in range(nt):
            for k in range(3):
                landed(t, k).wait_recv()
                forward(t, k, c).start()
        for t in range(nt):
            for k in range(3):
                forward(t, k, 1 - c).wait_recv()
        for t in range(nt):
            for k in range(3):
                first(t, k).wait_send()
                forward(t, k, c).wait_send()
        for cp in local:
            cp.wait()

    return pl.pallas_call(
        body, name=name, in_specs=[ANY] * nt, out_specs=[ANY] * nt,
        out_shape=[jax.ShapeDtypeStruct((N_CHIP,) + a.shape, a.dtype) for a in shards],
        scratch_shapes=[pltpu.SemaphoreType.DMA((nt, 3)), pltpu.SemaphoreType.DMA((nt, 3)),
                        pltpu.SemaphoreType.DMA((nt, 3)), pltpu.SemaphoreType.DMA((nt, 3)),
                        pltpu.SemaphoreType.DMA((nt,))],
    )(*shards)


def _comm_rows(hr, c, budget=2 * 1024 * 1024):
    if hr * c * 4 <= budget:
        return hr
    best = None
    for t in range(16, hr, 16):
        if hr % t == 0 and t * c * 4 <= budget:
            best = t
    return best if best is not None else hr


def _comm_cols(r, hc, budget=2 * 1024 * 1024):
    best = 128
    for t in range(128, hc + 1, 128):
        if hc % t == 0 and r * t * 4 <= budget:
            best = t
    return best


def _comm_chunks(shape, axis):
    r, cdim = shape
    if axis == 0:
        rc = _comm_rows(r // 2, cdim)
        nt = (r // 2) // rc
        return (rc, cdim), nt, (lambda h, t: (h * nt + t, 0))
    cc = _comm_cols(r, cdim // 2)
    nt = (cdim // 2) // cc
    return (r, cc), nt, (lambda h, t: (0, h * nt + t))


def _pair_reduce(g, where, axis, *, out_dtype, name):
    n_slot, r, cdim = g.shape
    blk_shape, nr, at = _comm_chunks((r, cdim), axis)
    steps = n_slot * nr
    half_shape = (r // 2, cdim) if axis == 0 else (r, cdim // 2)

    def body(w_ref, a_ref, b_ref, o_ref, land, send, recv, credit):
        x, y, c, _ = _place()
        sib = (x, y, 1 - c)
        i = pl.program_id(0) * nr + pl.program_id(1)
        s = lax.rem(i, 2)

        @pl.when(i >= 2)
        def _():
            pl.semaphore_wait(credit.at[s], 1)

        cp = pltpu.make_async_remote_copy(src_ref=b_ref.at[0], dst_ref=land.at[s], send_sem=send.at[s],
                                          recv_sem=recv.at[s], device_id=sib, device_id_type=MESH)
        cp.start()
        cp.wait_recv()
        o_ref[0] = (a_ref[0] + land[s]).astype(out_dtype)
        cp.wait_send()

        @pl.when(i + 2 < steps)
        def _():
            pl.semaphore_signal(credit.at[s], inc=1, device_id=sib, device_id_type=MESH)

    blk = lambda half: pl.BlockSpec((1,) + blk_shape, lambda j, t, w: (j,) + at(half(w), t))
    grid_spec = pltpu.PrefetchScalarGridSpec(
        num_scalar_prefetch=1, grid=(n_slot, nr),
        in_specs=[blk(lambda w: w[0]), blk(lambda w: 1 - w[0])],
        out_specs=pl.BlockSpec((1,) + blk_shape, lambda j, t, w: (j,) + at(0, t)),
        scratch_shapes=[pltpu.VMEM((2,) + blk_shape, F32), pltpu.SemaphoreType.DMA((2,)),
                        pltpu.SemaphoreType.DMA((2,)), pltpu.SemaphoreType.REGULAR((2,))])
    return pl.pallas_call(
        body, name=name, grid_spec=grid_spec, out_shape=jax.ShapeDtypeStruct((n_slot,) + half_shape, out_dtype),
        compiler_params=_cparams("arbitrary", "arbitrary"),
    )(where, g, g)


def _chip_exchange(parts, *, name):
    nt = len(parts)

    def body(*refs):
        src, got = refs[:nt], refs[nt:2 * nt]
        send, recv = refs[2 * nt:]
        x, y, c, chips = _place()
        remote = []
        for t in range(nt):
            for k in range(3):
                remote.append(pltpu.make_async_remote_copy(
                    src_ref=src[t].at[2 * chips[k][0] + chips[k][1]], dst_ref=got[t].at[k],
                    send_sem=send.at[t, k], recv_sem=recv.at[t, k],
                    device_id=(chips[k][0], chips[k][1], c), device_id_type=MESH))
        for cp in remote:
            cp.start()
        for cp in remote:
            cp.wait_recv()
        for cp in remote:
            cp.wait_send()

    return pl.pallas_call(
        body, name=name, in_specs=[ANY] * nt, out_specs=[ANY] * nt,
        out_shape=[jax.ShapeDtypeStruct((3,) + a.shape[1:], a.dtype) for a in parts],
        scratch_shapes=[pltpu.SemaphoreType.DMA((nt, 3)), pltpu.SemaphoreType.DMA((nt, 3))],
    )(*parts)


def _sum_join(p, got, where, axis, *, name):
    _, hr, cdim = p.shape
    full = (2 * hr, cdim) if axis == 0 else (hr, 2 * cdim)
    blk_shape, n, at = _comm_chunks(full, axis)
    step_len = blk_shape[axis]
    half_len = full[axis] // 2

    def body(w_ref, p_ref, g_ref, out, buf, lsem, ssem, rsem):
        x, y, c, _ = _place()
        sib = (x, y, 1 - c)
        r = pl.program_id(0)

        def part(start, size):
            return out.at[(slice(None),) * axis + (pl.ds(start, size),)]

        def copies(step, slot):
            rows = part(pl.multiple_of(c * half_len + step * step_len, 8 if axis == 0 else 128), step_len)
            return (pltpu.make_async_copy(buf.at[slot], rows, lsem.at[slot]),
                    pltpu.make_async_remote_copy(src_ref=buf.at[slot], dst_ref=rows, send_sem=ssem.at[slot],
                                                 recv_sem=rsem, device_id=sib, device_id_type=MESH))

        s = lax.rem(r, 2)

        @pl.when(r >= 2)
        def _():
            lc, rm = copies(r - 2, s)
            lc.wait()
            rm.wait_send()

        buf[s] = p_ref[0].astype(F32) + g_ref[0].astype(F32) + g_ref[1].astype(F32) + g_ref[2].astype(F32)
        lc, rm = copies(r, s)
        lc.start()
        rm.start()

        @pl.when(r == n - 1)
        def _():
            for step in range(max(0, n - 2), n):
                lc, rm = copies(step, step % 2)
                lc.wait()
                rm.wait_send()
            whole = part(0, half_len)
            pltpu.make_async_remote_copy(src_ref=whole, dst_ref=whole, send_sem=ssem.at[0], recv_sem=rsem,
                                         device_id=sib, device_id_type=MESH).wait_recv()

    grid_spec = pltpu.PrefetchScalarGridSpec(
        num_scalar_prefetch=1, grid=(n,),
        in_specs=[pl.BlockSpec((1,) + blk_shape, lambda t, w: (w[1],) + at(0, t)),
                  pl.BlockSpec((3,) + blk_shape, lambda t, w: (0,) + at(0, t))],
        out_specs=ANY,
        scratch_shapes=[pltpu.VMEM((2,) + blk_shape, F32), pltpu.SemaphoreType.DMA((2,)),
                        pltpu.SemaphoreType.DMA((2,)), pltpu.SemaphoreType.DMA])
    return pl.pallas_call(
        body, name=name, grid_spec=grid_spec, out_shape=jax.ShapeDtypeStruct(full, F32),
        compiler_params=_cparams("arbitrary"),
    )(where, p, got)


def _rider_gather_send(shards, axes):
    nt = len(shards)

    def copies(src, dst, send, recv, lsem):
        x, y, c, chips = _place()
        me = 2 * x + y
        local = [pltpu.make_async_copy(src[t], dst[t].at[me], lsem.at[t]) for t in range(nt)]
        out, landed = [], []
        for t in range(nt):
            for k in range(3):
                peer = (chips[k][0], chips[k][1], c)
                out.append(pltpu.make_async_remote_copy(
                    src_ref=_half(src[t], axes[t], c), dst_ref=_half(dst[t], axes[t], c, lead=(me,)),
                    send_sem=send.at[t, k], recv_sem=recv.at[t, k], device_id=peer, device_id_type=MESH))
                theirs = _half(dst[t], axes[t], c, lead=(2 * chips[k][0] + chips[k][1],))
                landed.append(pltpu.make_async_remote_copy(
                    src_ref=theirs, dst_ref=theirs, send_sem=send.at[t, k], recv_sem=recv.at[t, k],
                    device_id=peer, device_id_type=MESH))
        return local, out, landed

    def start(src, dst, sems):
        local, out, _ = copies(src, dst, *sems)
        for cp in local + out:
            cp.start()

    def finish(src, dst, sems):
        local, out, landed = copies(src, dst, *sems)
        for cp in landed:
            cp.wait_recv()
        for cp in out:
            cp.wait_send()
        for cp in local:
            cp.wait()

    return _Rider(shards, [jax.ShapeDtypeStruct((N_CHIP,) + a.shape, a.dtype) for a in shards],
                  [pltpu.SemaphoreType.DMA((nt, 3)), pltpu.SemaphoreType.DMA((nt, 3)), pltpu.SemaphoreType.DMA((nt,))],
                  start, finish)


def _rider_gather_forward(bufs, axes):
    nt = len(bufs)

    def copies(src, dst, send, recv):
        x, y, c, chips = _place()
        mine, theirs = [], []
        for t in range(nt):
            for k in range(3):
                slot = 2 * chips[k][0] + chips[k][1]
                for hc, into in ((c, mine), (1 - c, theirs)):
                    into.append(pltpu.make_async_remote_copy(
                        src_ref=_half(src[t], axes[t], hc, lead=(slot,)),
                        dst_ref=_half(dst[t], axes[t], hc, lead=(slot,)),
                        send_sem=send.at[t, k], recv_sem=recv.at[t, k], device_id=(x, y, 1 - c), device_id_type=MESH))
        return mine, theirs

    def start(src, dst, sems):
        for cp in copies(src, dst, *sems)[0]:
            cp.start()

    def finish(src, dst, sems):
        mine, theirs = copies(src, dst, *sems)
        for cp in theirs:
            cp.wait_recv()
        for cp in mine:
            cp.wait_send()

    return _Rider(bufs, [jax.ShapeDtypeStruct(a.shape, a.dtype) for a in bufs],
                  [pltpu.SemaphoreType.DMA((nt, 3)), pltpu.SemaphoreType.DMA((nt, 3))], start, finish,
                  aliases={t: t for t in range(nt)})


def _rider_chip_exchange(parts):
    nt = len(parts)

    def copies(src, got, send, recv):
        x, y, c, chips = _place()
        return [pltpu.make_async_remote_copy(
            src_ref=src[t].at[2 * chips[k][0] + chips[k][1]], dst_ref=got[t].at[k], send_sem=send.at[t, k],
            recv_sem=recv.at[t, k], device_id=(chips[k][0], chips[k][1], c), device_id_type=MESH)
            for t in range(nt) for k in range(3)]

    def start(src, got, sems):
        for cp in copies(src, got, *sems):
            cp.start()

    def finish(src, got, sems):
        remote = copies(src, got, *sems)
        for cp in remote:
            cp.wait_recv()
        for cp in remote:
            cp.wait_send()

    return _Rider(parts, [jax.ShapeDtypeStruct((3,) + a.shape[1:], a.dtype) for a in parts],
                  [pltpu.SemaphoreType.DMA((nt, 3)), pltpu.SemaphoreType.DMA((nt, 3))], start, finish)


def _gather_all(block, *, name):
    m_per, n = block.shape

    def body(x_ref, out_ref, send_sems, recv_sems, local_sem):
        x, y, c, chips = _place()
        me, sibling = (x, y, c), (x, y, 1 - c)

        def rows(px, py, pc):
            return out_ref.at[4 * px + 2 * py + pc]

        def copy(k, blk, to, src=None):
            return pltpu.make_async_remote_copy(
                src_ref=rows(*blk) if src is None else src, dst_ref=rows(*blk),
                send_sem=send_sems.at[k], recv_sem=recv_sems.at[k], device_id=to, device_id_type=MESH)

        mine = pltpu.make_async_copy(x_ref, rows(*me), local_sem)
        mine.start()
        first = [copy(0, me, sibling, src=x_ref)]
        first += [copy(1 + j, me, (*chip, c), src=x_ref) for j, chip in enumerate(chips)]
        for cp in first:
            cp.start()
        passed = [copy(4 + j, (*chip, c), sibling) for j, chip in enumerate(chips)]
        for j, chip in enumerate(chips):
            copy(1 + j, (*chip, c), me).wait_recv()
            passed[j].start()
        copy(0, sibling, me).wait_recv()
        for j, chip in enumerate(chips):
            copy(4 + j, (*chip, 1 - c), me).wait_recv()
        for cp in first + passed:
            cp.wait_send()
        mine.wait()

    return pl.pallas_call(
        body, name=name,
        out_shape=jax.ShapeDtypeStruct((N_DEV, m_per, n), block.dtype),
        in_specs=[pl.BlockSpec(memory_space=pltpu.VMEM)], out_specs=pl.BlockSpec(memory_space=pltpu.VMEM),
        scratch_shapes=[pltpu.SemaphoreType.DMA((7,)), pltpu.SemaphoreType.DMA((7,)), pltpu.SemaphoreType.DMA],
        compiler_params=pltpu.CompilerParams(vmem_limit_bytes=VMEM_LIMIT),
    )(block)


def _sum_slots(slots, *, name):
    n, m, c = slots.shape
    t = _rows_tile(m, c * n)

    def body(s_ref, o_ref):
        acc = s_ref[0]
        for k in range(1, n):
            acc = acc + s_ref[k]
        o_ref[...] = acc

    return pl.pallas_call(
        body, name=name, grid=(m // t,), in_specs=[pl.BlockSpec((n, t, c), lambda i: (0, i, 0))],
        out_specs=pl.BlockSpec((t, c), lambda i: (i, 0)), out_shape=jax.ShapeDtypeStruct((m, c), F32),
        compiler_params=_cparams("parallel"),
    )(slots)


def _pad_rows(a, rows):
    return a if a.shape[0] == rows else jnp.pad(a, ((0, rows - a.shape[0]), (0, 0)))


def _w_in_padded(shards):
    full = shards.reshape(IN_COLS, shards.shape[2])
    return jnp.concatenate([_pad_rows(full[SEG[n][2]:SEG[n][2] + SEG[n][3]], SEG[n][1]) for n in SEG_ORDER], axis=0)


def _w_in_unpadded(gp):
    full = jnp.concatenate([gp[SEG[n][0]:SEG[n][0] + SEG[n][3]] for n in ORIG_ORDER], axis=0)
    return full.reshape(N_CHIP, IN_COLS // N_CHIP, gp.shape[1])


def _pad_heads(w, true_w, pad_w):
    r = w.shape[0]
    h = w.shape[1] // true_w
    return jnp.pad(w.reshape(r, h, true_w), ((0, 0), (0, 0), (0, pad_w - true_w))).reshape(r, h * pad_w)


def _unpad_heads(w, true_w, pad_w):
    r = w.shape[0]
    h = w.shape[1] // pad_w
    return w.reshape(r, h, pad_w)[:, :, :true_w].reshape(r, h * true_w)


def _cols_to_slots(a):
    return a.reshape(a.shape[0], N_CHIP, a.shape[1] // N_CHIP).transpose(1, 0, 2)


def _to_heads(a, h, d):
    return a.reshape(a.shape[0], h, d).transpose(1, 0, 2)


def _slots_to_cols(a):
    return jnp.concatenate([a[j] for j in range(N_CHIP)], axis=1)


SMALL = [("norm_g", 2048), ("ret_norm_g", 512), ("gla_ba_f", 256), ("gla_ba_b", 256), ("gla_norm_g", 512),
         ("pool_w", 4 * 128 * 128), ("pool_scale", 512), ("mla_q_norm_g", 512), ("mla_kv_norm_g", 256),
         ("mla_qk_norm_q", 192), ("mla_qk_norm_k", 192)]


def _pack_small(vals):
    parts = []
    for name, n in SMALL:
        parts += [v.reshape(-1) for v in vals[name]]
        if (DEPTH * n) % 1024:
            parts.append(jnp.zeros((-(DEPTH * n)) % 1024, F32))
    parts += [vals["loss"].reshape(-1), jnp.zeros(1023, F32)]
    return jnp.concatenate(parts).reshape(-1, 128)


def _unpack_small(block):
    flat = block.reshape(-1)
    out, off = {}, 0
    for name, n in SMALL:
        out[name] = flat[off:off + DEPTH * n]
        off += DEPTH * n + (-(DEPTH * n)) % 1024
    out["loss"] = flat[off]
    return out


def _layer_weights(l, p, g):
    wa = jnp.zeros((128, 512), F32)
    wa = wa.at[0:GLA_RANK, 0:256].set(_slots_to_cols(g["gla_wa2_f"]))
    wa = wa.at[GLA_RANK:2 * GLA_RANK, 256:512].set(_slots_to_cols(g["gla_wa2_b"]))
    return dict(
        norm_g=p["norm_g"][l][None, :],
        w_in=_w_in_padded(g["w_in"]),
        w_out=g["w_out"].reshape(4 * g["w_out"].shape[1], -1),
        ret_norm_g=p["ret_norm_g"][l][None, :],
        wa=_bf(wa),
        ba=jnp.concatenate([p["gla_ba_f"][l], p["gla_ba_b"][l]])[None, :],
        gla_norm_g=p["gla_norm_g"][l][None, :],
        pool_w=_bf(p["pool_w"][l]),
        pool_scale=p["pool_scale"][l][None, :],
        qg=p["mla_q_norm_g"][l][None, :],
        wq=_pad_heads(_slots_to_cols(g["mla_wq_b"]), MLA_QK, MLA_QKP),
        kvg=p["mla_kv_norm_g"][l][None, :],
        wkv=_slots_to_cols(g["mla_wkv_b"]),
        qng=jnp.pad(p["mla_qk_norm_q"][l], (0, MLA_QKP - MLA_QK))[None, :],
        kng=jnp.pad(p["mla_qk_norm_k"][l], (0, MLA_QKP - MLA_QK))[None, :],
    )


def _layer_fwd(l, x, w, tabs, next_shards=None):
    ret_cos, ret_sin, mla_cos, mla_sp, mla_sn = tabs
    nm = lambda s: f"l{l}_{s}"
    h = _rmsnorm_fwd(x, w["norm_g"], name=nm("norm"))
    if next_shards is None:
        z = _matmul(h, w["w_in"], tb=True, name=nm("in_proj"))
    else:
        z, landed = _matmul(h, w["w_in"], tb=True, rider=_rider_gather_send(next_shards[:1], SHARD_AXES[:1]),
                            name=nm("in_proj"))
    qr, kr = _ret_pre(z, ret_cos, ret_sin, name=nm("ret_pre"))
    ret_o = _bla(qr, kr, z, _ret_log_gamma(False), (0, 0, SEG["rv"][0] // 512), name=nm("ret_scan"))
    y_a = _post(ret_o, z, SEG["rg"][0] // 512, w["ret_norm_g"], norm=True, name=nm("ret_post"))
    la = _gla_gate(z, w["wa"], w["ba"], name=nm("gla_gate"))
    la_h = la.reshape(la.shape[0], 2, GLA_HEADS, GLA_DK).transpose(1, 2, 0, 3)
    gq = _to_heads(z[:, SEG["gq"][0]:SEG["gq"][0] + 256], GLA_HEADS, GLA_DK)
    gk = _to_heads(z[:, SEG["gk"][0]:SEG["gk"][0] + 256], GLA_HEADS, GLA_DK)
    if next_shards is None:
        gla_o, gla_st = _gla_fwd(gq, gk, z, la_h, name=nm("gla_scan"))
    else:
        gla_o, gla_st, more = _gla_fwd(gq, gk, z, la_h, rider=_rider_gather_send(next_shards[1:], SHARD_AXES[1:]),
                                       name=nm("gla_scan"))
        landed = list(landed) + list(more)
    y_b = _post(gla_o, z, SEG["gg"][0] // 512, w["gla_norm_g"], norm=True, name=nm("gla_post"))
    y_c = _pool_fwd(z, w["pool_w"], w["pool_scale"], name=nm("pool"))
    q, k, v = _mla_pre(z, w["qg"], w["wq"], w["kvg"], w["wkv"], w["qng"], w["kng"], mla_cos, mla_sp, mla_sn,
                       name=nm("mla_pre"))
    if next_shards is None:
        (att_o, lse), gathered = _flash_fwd(q, k, v, name=nm("attn")), None
    else:
        att_o, lse, gathered = _flash_fwd(q, k, v, rider=_rider_gather_forward(landed, SHARD_AXES), name=nm("attn"))
    y_d = _post([att_o], z, SEG["mg"][0] // 512, w["qg"], norm=False, name=nm("mla_post"))
    y = jnp.concatenate([y_a, y_b, y_c, y_d], axis=1)
    x_next = _matmul(y, w["w_out"], add=x, name=nm("out_proj"))
    saved = dict(x=x, h=h, z=z, y=y, qr=qr, kr=kr, ret_o=ret_o, la_h=la_h, gq=gq, gk=gk, gla_o=gla_o, gla_st=gla_st,
                 q=q, k=k, v=v, att_o=att_o, lse=lse)
    return x_next, saved, gathered


def _layer_bwd(l, dx_next, w, sv, tabs, riding_parts=None, where=None):
    ret_cos, ret_sin, mla_cos, mla_sp, mla_sn = tabs
    nm = lambda s: f"l{l}_{s}"
    z = sv["z"]
    dy = _matmul(dx_next, w["w_out"], tb=True, name=nm("out_proj_dy"))
    d_w_out = _matmul(sv["y"], dx_next, ta=True, tn=512, name=nm("out_proj_dw"))
    d_w_out = d_w_out.reshape(N_CHIP, d_w_out.shape[0] // N_CHIP, d_w_out.shape[1])
    if where is not None:
        pair_w_out = _pair_reduce(d_w_out, where, 0, out_dtype=BF16, name=nm("pair_reduce_w_out"))
    dz = lax.empty((z.shape[0], IN_PAD), BF16)
    at = lambda n: SEG[n][0]
    dz, d_ret_o, d_ret_g = _post_bwd(dy, 0, sv["ret_o"], z, SEG["rg"][0] // 512, w["ret_norm_g"], (dz, at("rg")),
                                     norm=True, name=nm("ret_post_bwd"))
    vcol = SEG["rv"][0] // 512
    dqr = _bla(d_ret_o, z, sv["kr"], _ret_log_gamma(False), (0, vcol, 0), name=nm("ret_scan_dq"))
    dkr = _bla(z, d_ret_o, sv["qr"], _ret_log_gamma(True), (vcol, 0, 0), name=nm("ret_scan_dk"))
    drv = _bla(sv["kr"], sv["qr"], d_ret_o, _ret_log_gamma(True), (0, 0, 0), name=nm("ret_scan_dv"))
    dz = _ret_pre_bwd(dqr, dkr, ret_cos, ret_sin, (dz, at("rq")), name=nm("ret_pre_bwd"))
    dz = _add_n([drv[0], drv[1]], out_dtype=BF16, into=(dz, at("rv")), name=nm("ret_dv_sum"))
    dz, d_gla_o, d_gla_g = _post_bwd(dy, 1, sv["gla_o"], z, SEG["gg"][0] // 512, w["gla_norm_g"], (dz, at("gg")),
                                     norm=True, name=nm("gla_post_bwd"))
    if where is None:
        dq2, dk2, dla2, dv2 = _gla_bwd(sv["gq"], sv["gk"], z, sv["la_h"], d_gla_o, sv["gla_st"],
                                       name=nm("gla_scan_bwd"))
    else:
        dq2, dk2, dla2, dv2, (others_w_out,) = _gla_bwd(
            sv["gq"], sv["gk"], z, sv["la_h"], d_gla_o, sv["gla_st"], rider=_rider_chip_exchange([pair_w_out]),
            name=nm("gla_scan_bwd"))
        d_w_out = (pair_w_out, others_w_out)
    d_gq = _bf(dq2[0] + dq2[1])
    d_gk = _bf(dk2[0] + dk2[1])
    dz = _add_n([dv2[0], dv2[1]], out_dtype=BF16, into=(dz, at("gv")), name=nm("gla_dv_sum"))
    dz, d_wa, d_ba = _gla_gate_bwd(dla2, z, w["wa"], w["ba"], (dz, at("ga")), name=nm("gla_gate_bwd"))
    d_pv, d_pg, d_pool_w, d_pool_scale = _pool_bwd(dy, z, w["pool_w"], w["pool_scale"], name=nm("pool_bwd"))
    dz, d_att_o, _ = _post_bwd(dy, 3, [sv["att_o"]], z, SEG["mg"][0] // 512, w["qg"], (dz, at("mg")), norm=False,
                               name=nm("mla_post_bwd"))
    if riding_parts is None:
        (dq, dk, dv), rode = _flash_bwd(sv["q"], sv["k"], sv["v"], d_att_o, sv["att_o"], sv["lse"],
                                        name=nm("attn_bwd")), None
    else:
        dq, dk, dv, rode = _flash_bwd(sv["q"], sv["k"], sv["v"], d_att_o, sv["att_o"], sv["lse"],
                                      rider=_rider_chip_exchange(riding_parts), name=nm("attn_bwd"))
    d_mq, d_mkv, d_mkr, d_wq, d_wkv, d_qg, d_kvg, d_qng, d_kng = _mla_pre_bwd(
        dq, dk, dv, z, w["qg"], w["wq"], w["kvg"], w["wkv"], w["qng"], w["kng"], mla_cos, mla_sp, mla_sn,
        name=nm("mla_pre_bwd"))
    for n, seg in dict(pv=d_pv, pg=d_pg, mq=d_mq, gq=d_gq, gk=d_gk, mkv=d_mkv, mkr=d_mkr).items():
        dz = lax.dynamic_update_slice(dz, seg, (0, at(n)))
    dh = _matmul(dz, w["w_in"], tn=512, name=nm("in_proj_dh"))
    d_w_in = _matmul(dz, sv["h"], ta=True, name=nm("in_proj_dw"))
    dx, d_norm_g = _rmsnorm_bwd(sv["x"], dh, w["norm_g"], dx_next, name=nm("norm_bwd"))
    sharded = dict(
        w_in=_w_in_unpadded(d_w_in),
        w_out=d_w_out,
        mla_wq_b=_cols_to_slots(_unpad_heads(d_wq, MLA_QK, MLA_QKP)),
        mla_wkv_b=_cols_to_slots(d_wkv),
        gla_wa2_f=_cols_to_slots(d_wa[0:GLA_RANK, 0:256]),
        gla_wa2_b=_cols_to_slots(d_wa[GLA_RANK:2 * GLA_RANK, 256:512]),
    )
    small = dict(
        norm_g=d_norm_g[0], ret_norm_g=d_ret_g[0], gla_ba_f=d_ba[0, :256], gla_ba_b=d_ba[0, 256:],
        gla_norm_g=d_gla_g[0], pool_w=d_pool_w.reshape(-1), pool_scale=d_pool_scale[0], mla_q_norm_g=d_qg[0],
        mla_kv_norm_g=d_kvg[0], mla_qk_norm_q=d_qng[0, :MLA_QK], mla_qk_norm_k=d_kng[0, :MLA_QK],
    )
    return dx, sharded, small, rode


SHARDED = ["w_in", "w_out", "mla_wq_b", "mla_wkv_b", "gla_wa2_f", "gla_wa2_b"]
WEIGHTS = ["norm_g", "w_in", "ret_norm_g", "gla_wa2_f", "gla_ba_f", "gla_wa2_b", "gla_ba_b", "gla_norm_g", "pool_w",
           "pool_scale", "mla_q_norm_g", "mla_wq_b", "mla_kv_norm_g", "mla_wkv_b", "mla_qk_norm_q", "mla_qk_norm_k",
           "w_out"]


SHARD_AXES = [1, 0, 0, 0, 0, 0]


def _layer_shards(p, l):
    return [jnp.swapaxes(p["w_in"], 1, 2)[l].astype(BF16), p["w_out"][l].astype(BF16), p["mla_wq_b"][l].astype(BF16),
            p["mla_wkv_b"][l].astype(BF16), p["gla_wa2_f"][l], p["gla_wa2_b"][l]]


def _step(p, where):
    x = p["x"][0]
    tabs = _rope_tables(x.shape[0])
    got0 = _gather_shards(_layer_shards(p, 0), SHARD_AXES, name="l0_gather_weights")
    w0 = _layer_weights(0, p, dict(zip(SHARDED, got0)))
    x1, sv0, got1 = _layer_fwd(0, x, w0, tabs, next_shards=_layer_shards(p, 1))
    w1 = _layer_weights(1, p, dict(zip(SHARDED, got1)))
    x2, sv1, _ = _layer_fwd(1, x1, w1, tabs)
    dx, loss = _loss_head(x2, p["loss_target"][0], name="loss_head")

    big, big_axes = SHARDED[:2], SHARD_AXES[:2]

    def pair_sums(tag, tensors, axes, names):
        return [_pair_reduce(a, where, ax, out_dtype=BF16, name=f"{tag}_pair_reduce_{n}")
                for a, ax, n in zip(tensors, axes, names)]

    def joined(tag, pair, others, axes, names):
        return [_sum_join(a, b, where, ax, name=f"{tag}_sum_join_{n}")
                for a, b, ax, n in zip(pair, others, axes, names)]

    dx, sharded1, small1, _ = _layer_bwd(1, dx, w1, sv1, tabs)
    pair1 = pair_sums("l1", [sharded1[n] for n in big], big_axes, big)
    dx, sharded0, small0, others1 = _layer_bwd(0, dx, w0, sv0, tabs, riding_parts=pair1, where=where)
    grads1 = joined("l1", pair1, others1, big_axes, big)
    packed = jnp.concatenate([sh[n].reshape(N_CHIP, -1, 128) for sh in (sharded0, sharded1) for n in SHARDED[2:]],
                             axis=1)
    last, last_axes, last_names = [sharded0["w_in"], packed], [SHARD_AXES[0], 0], ["w_in", "rest"]
    pair0 = pair_sums("l0", last, last_axes, last_names)
    g_w_in0, rest = joined("l0", pair0, _chip_exchange(pair0, name="l0_chip_exchange"), last_axes, last_names)
    (g_w_out0,) = joined("l0", [sharded0["w_out"][0]], [sharded0["w_out"][1]], [SHARD_AXES[1]], ["w_out"])
    grads = {n: jnp.stack([g0, g1]) for n, g0, g1 in zip(big, (g_w_in0, g_w_out0), grads1)}
    off = 0
    pieces = {n: [] for n in SHARDED[2:]}
    for sh in (sharded0, sharded1):
        for n in SHARDED[2:]:
            rows = sh[n].shape[1] * sh[n].shape[2] // 128
            pieces[n].append(rest[off:off + rows].reshape(sh[n].shape[1:]))
            off += rows
    grads.update({n: jnp.stack(v) for n, v in pieces.items()})
    small = {n: [small0[n], small1[n]] for n, _ in SMALL}
    small["loss"] = loss
    return dx[None], grads, small


def kernel(x, norm_g, w_in, ret_norm_g, gla_wa2_f, gla_ba_f, gla_wa2_b, gla_ba_b, gla_norm_g, pool_w, pool_scale, mla_q_norm_g, mla_wq_b, mla_kv_norm_g, mla_wkv_b, mla_qk_norm_q, mla_qk_norm_k, w_out, loss_target, m_norm_g, m_w_in, m_ret_norm_g, m_gla_wa2_f, m_gla_ba_f, m_gla_wa2_b, m_gla_ba_b, m_gla_norm_g, m_pool_w, m_pool_scale, m_mla_q_norm_g, m_mla_wq_b, m_mla_kv_norm_g, m_mla_wkv_b, m_mla_qk_norm_q, m_mla_qk_norm_k, m_w_out, v_norm_g, v_w_in, v_ret_norm_g, v_gla_wa2_f, v_gla_ba_f, v_gla_wa2_b, v_gla_ba_b, v_gla_norm_g, v_pool_w, v_pool_scale, v_mla_q_norm_g, v_mla_wq_b, v_mla_kv_norm_g, v_mla_wkv_b, v_mla_qk_norm_q, v_mla_qk_norm_k, v_w_out):
    p = dict(x=x, norm_g=norm_g, w_in=w_in, ret_norm_g=ret_norm_g, gla_wa2_f=gla_wa2_f, gla_ba_f=gla_ba_f,
             gla_wa2_b=gla_wa2_b, gla_ba_b=gla_ba_b, gla_norm_g=gla_norm_g, pool_w=pool_w, pool_scale=pool_scale,
             mla_q_norm_g=mla_q_norm_g, mla_wq_b=mla_wq_b, mla_kv_norm_g=mla_kv_norm_g, mla_wkv_b=mla_wkv_b,
             mla_qk_norm_q=mla_qk_norm_q, mla_qk_norm_k=mla_qk_norm_k, w_out=w_out, loss_target=loss_target)
    moments = dict(
        m=dict(norm_g=m_norm_g, w_in=m_w_in, ret_norm_g=m_ret_norm_g, gla_wa2_f=m_gla_wa2_f, gla_ba_f=m_gla_ba_f,
               gla_wa2_b=m_gla_wa2_b, gla_ba_b=m_gla_ba_b, gla_norm_g=m_gla_norm_g, pool_w=m_pool_w,
               pool_scale=m_pool_scale, mla_q_norm_g=m_mla_q_norm_g, mla_wq_b=m_mla_wq_b,
               mla_kv_norm_g=m_mla_kv_norm_g, mla_wkv_b=m_mla_wkv_b, mla_qk_norm_q=m_mla_qk_norm_q,
               mla_qk_norm_k=m_mla_qk_norm_k, w_out=m_w_out),
        v=dict(norm_g=v_norm_g, w_in=v_w_in, ret_norm_g=v_ret_norm_g, gla_wa2_f=v_gla_wa2_f, gla_ba_f=v_gla_ba_f,
               gla_wa2_b=v_gla_wa2_b, gla_ba_b=v_gla_ba_b, gla_norm_g=v_gla_norm_g, pool_w=v_pool_w,
               pool_scale=v_pool_scale, mla_q_norm_g=v_mla_q_norm_g, mla_wq_b=v_mla_wq_b,
               mla_kv_norm_g=v_mla_kv_norm_g, mla_wkv_b=v_mla_wkv_b, mla_qk_norm_q=v_mla_qk_norm_q,
               mla_qk_norm_k=v_mla_qk_norm_k, w_out=v_w_out))

    where = jnp.stack([lax.axis_index("c"), 2 * lax.axis_index("x") + lax.axis_index("y")]).astype(jnp.int32)
    grad_x, grads, small = _step(p, where)

    slots = _gather_all(_pack_small(small), name="gather_small")
    total = _unpack_small(_sum_slots(slots, name="sum_small"))
    for n, _ in SMALL:
        grads[n] = total[n].reshape(p[n].shape)
    loss = total["loss"]

    delta, new_m, new_v = {}, {}, {}
    for n in WEIGHTS:
        turn = (lambda a: jnp.swapaxes(a, 1, 2)) if n == "w_in" else (lambda a: a)
        outs = _adamw(turn(p[n]), grads[n], turn(moments["m"][n]), turn(moments["v"][n]), name=f"adamw_{n}")
        grads[n] = turn(grads[n])
        delta[n], new_m[n], new_v[n] = (turn(o) for o in outs)
    return (loss, grad_x, *[grads[n] for n in WEIGHTS], *[delta[n] for n in WEIGHTS],
            *[new_m[n] for n in WEIGHTS], *[new_v[n] for n in WEIGHTS])
```

```python
import jax
import jax.numpy as jnp
from jax import lax
from jax.experimental import pallas as pl
from jax.experimental.pallas import tpu as pltpu

F32 = jnp.float32
BF16 = jnp.bfloat16
MESH = pl.DeviceIdType.MESH

EPS = 1e-6
ROPE_THETA = 10000.0
DEPTH = 2
N_DEV = 8
N_CHIP = 4

GROUP_W = 512
RET_HEADS = 4
RET_HD = 128
RET_CHUNK = 256
GLA_HEADS = 4
GLA_DK = 64
GLA_DV = 128
GLA_RANK = 16
GLA_TAU = 16.0
GLA_CHUNK = 64
POOL_GROUPS = 4
POOL_GW = 128
POOL_HALO = 8
POOL_TILE = 256
MLA_HEADS = 4
MLA_NOPE = 128
MLA_ROPE = 64
MLA_QK = MLA_NOPE + MLA_ROPE
MLA_QKP = 256
MLA_V = 128
MLA_Q_RANK = 512
MLA_KV_RANK = 256
MLA_SCALE = MLA_QK ** -0.5

ADAM_LR = 0.001
ADAM_B1 = 0.9
ADAM_B2 = 0.999
ADAM_EPS = 1e-08
ADAM_WD = 0.01
ADAM_STEP = 10

VMEM_LIMIT = 56 * 1024 * 1024
ROW_TILE = 512

SEG = {
    "rq": (0, 512, 0, 512), "rk": (512, 512, 512, 512), "rv": (1024, 512, 1024, 512), "rg": (1536, 512, 1536, 512),
    "gv": (2048, 512, 2560, 512), "gg": (2560, 512, 3072, 512),
    "pv": (3072, 512, 3616, 512), "pg": (3584, 512, 4128, 512),
    "mq": (4096, 512, 4640, 512), "mg": (4608, 512, 5472, 512),
    "gq": (5120, 256, 2048, 256), "gk": (5376, 256, 2304, 256), "mkv": (5632, 256, 5152, 256),
    "ga": (5888, 128, 3584, 32), "mkr": (6016, 128, 5408, 64),
}
SEG_ORDER = ["rq", "rk", "rv", "rg", "gv", "gg", "pv", "pg", "mq", "mg", "gq", "gk", "mkv", "ga", "mkr"]
IN_COLS = 5984
IN_PAD = 6144
ORIG_ORDER = ["rq", "rk", "rv", "rg", "gq", "gk", "gv", "gg", "ga", "pv", "pg", "mq", "mkv", "mkr", "mg"]


def _cparams(*sem):
    return pltpu.CompilerParams(dimension_semantics=tuple(sem), vmem_limit_bytes=VMEM_LIMIT)


def _bf(v):
    return v.astype(BF16)


def _dot(a, b, ca=1, cb=0):
    return lax.dot_general(_bf(a), _bf(b), (((ca,), (cb,)), ((), ())), preferred_element_type=F32)


def _sigmoid(x):
    return 1.0 / (1.0 + jnp.exp(-x))


def _silu_parts(g):
    sg = _sigmoid(g)
    return g * sg, sg * (1.0 + g * (1.0 - sg))


class _Rider:
    def __init__(self, ins, outs, sems, start, finish, aliases=None):
        self.ins, self.outs, self.sems, self.start, self.finish = list(ins), list(outs), list(sems), start, finish
        self.aliases = dict(aliases or {})


def _ride(body, rider, n_in, n_out, grid):
    if rider is None:
        return body
    ri, ro, rs = len(rider.ins), len(rider.outs), len(rider.sems)

    def wrapped(*refs):
        ins, refs = refs[:n_in], refs[n_in:]
        rin, refs = refs[:ri], refs[ri:]
        outs, refs = refs[:n_out], refs[n_out:]
        rout, refs = refs[:ro], refs[ro:]
        scratch, sems = refs[:len(refs) - rs], refs[len(refs) - rs:]
        first = pl.program_id(0) == 0
        last = pl.program_id(0) == grid[0] - 1
        for ax in range(1, len(grid)):
            first = jnp.logical_and(first, pl.program_id(ax) == 0)
            last = jnp.logical_and(last, pl.program_id(ax) == grid[ax] - 1)

        @pl.when(first)
        def _():
            rider.start(rin, rout, sems)

        body(*ins, *outs, *scratch)

        @pl.when(last)
        def _():
            rider.finish(rin, rout, sems)

    return wrapped


def _ride_call(body, rider, *, name, grid, in_specs, out_specs, out_shape, scratch_shapes, args, sem):
    n_in, n_out = len(in_specs), len(out_specs)
    if rider is None:
        return pl.pallas_call(body, name=name, grid=grid, in_specs=in_specs, out_specs=out_specs, out_shape=out_shape,
                              scratch_shapes=scratch_shapes, compiler_params=_cparams(*sem))(*args), []
    outs = pl.pallas_call(
        _ride(body, rider, n_in, n_out, grid), name=name, grid=grid,
        in_specs=list(in_specs) + [ANY] * len(rider.ins), out_specs=list(out_specs) + [ANY] * len(rider.outs),
        out_shape=list(out_shape) + rider.outs, scratch_shapes=list(scratch_shapes) + rider.sems,
        input_output_aliases={n_in + i: n_out + o for i, o in rider.aliases.items()},
        compiler_params=_cparams(*(["arbitrary"] * len(grid))),
    )(*args, *rider.ins)
    return outs[:n_out], outs[n_out:]


def _matmul(a, b, *, ta=False, tb=False, out_dtype=F32, tm=512, tn=1024, tk=None, add=None, n_outer=True, rider=None,
            name):
    m, kdim = (a.shape[1], a.shape[0]) if ta else a.shape
    n = b.shape[0] if tb else b.shape[1]
    tm, tn = min(tm, m), min(tn, n)
    tk = kdim if tk is None else min(tk, kdim)
    assert m % tm == 0 and n % tn == 0 and kdim % tk == 0
    nk = kdim // tk
    ca, cb = (0 if ta else 1), (1 if tb else 0)

    def body(*refs):
        if add is None:
            a_ref, b_ref, o_ref = refs[:3]
            add_ref = None
        else:
            a_ref, b_ref, add_ref, o_ref = refs[:4]
        p = _dot(a_ref[...], b_ref[...], ca, cb)

        def finish(r):
            if add_ref is not None:
                r = r + add_ref[...]
            o_ref[...] = r.astype(out_dtype)

        if nk == 1:
            finish(p)
        else:
            acc = refs[-1]
            k = pl.program_id(2)

            @pl.when(k == 0)
            def _():
                acc[...] = p

            @pl.when(k > 0)
            def _():
                acc[...] += p

            @pl.when(k == nk - 1)
            def _():
                finish(acc[...])

    def ij(g0, g1):
        return (g1, g0) if n_outer else (g0, g1)

    a_spec = (pl.BlockSpec((tk, tm), lambda g0, g1, k: (k, ij(g0, g1)[0])) if ta
              else pl.BlockSpec((tm, tk), lambda g0, g1, k: (ij(g0, g1)[0], k)))
    b_spec = (pl.BlockSpec((tn, tk), lambda g0, g1, k: (ij(g0, g1)[1], k)) if tb
              else pl.BlockSpec((tk, tn), lambda g0, g1, k: (k, ij(g0, g1)[1])))
    o_spec = pl.BlockSpec((tm, tn), lambda g0, g1, k: ij(g0, g1))
    in_specs = [a_spec, b_spec] + ([o_spec] if add is not None else [])
    args = (a, b) + ((add,) if add is not None else ())
    grid = (n // tn, m // tm, nk) if n_outer else (m // tm, n // tn, nk)
    (out,), rode = _ride_call(
        body, rider, name=name, grid=grid, in_specs=in_specs, out_specs=[o_spec],
        out_shape=[jax.ShapeDtypeStruct((m, n), out_dtype)],
        scratch_shapes=[] if nk == 1 else [pltpu.VMEM((tm, tn), F32)], args=args,
        sem=("parallel", "parallel", "arbitrary"))
    return out if rider is None else (out, rode)


def _rmsnorm_fwd(x, g, *, name, tm=ROW_TILE):
    s, d = x.shape
    tm = min(tm, s)

    def body(x_ref, g_ref, h_ref):
        xv = x_ref[...]
        r = lax.rsqrt(jnp.mean(xv * xv, axis=-1, keepdims=True) + EPS)
        h_ref[...] = _bf(xv * r * g_ref[...])

    return pl.pallas_call(
        body, name=name, grid=(s // tm,),
        in_specs=[pl.BlockSpec((tm, d), lambda i: (i, 0)), pl.BlockSpec((1, d), lambda i: (0, 0))],
        out_specs=pl.BlockSpec((tm, d), lambda i: (i, 0)),
        out_shape=jax.ShapeDtypeStruct((s, d), BF16),
        compiler_params=_cparams("parallel"),
    )(x, g)


def _rmsnorm_bwd(x, dh, g, dres, *, name, tm=ROW_TILE):
    s, d = x.shape
    tm = min(tm, s)

    def body(x_ref, dh_ref, g_ref, dres_ref, dx_ref, dg_ref):
        i = pl.program_id(0)
        xv = x_ref[...]
        r = lax.rsqrt(jnp.mean(xv * xv, axis=-1, keepdims=True) + EPS)
        xn = xv * r
        dv = dh_ref[...]
        part = jnp.sum(dv * xn, axis=0, keepdims=True)

        @pl.when(i == 0)
        def _():
            dg_ref[...] = part

        @pl.when(i > 0)
        def _():
            dg_ref[...] += part

        dxn = dv * g_ref[...]
        dx_ref[...] = dres_ref[...] + r * (dxn - xn * jnp.mean(dxn * xn, axis=-1, keepdims=True))

    row = pl.BlockSpec((tm, d), lambda i: (i, 0))
    vec = pl.BlockSpec((1, d), lambda i: (0, 0))
    return pl.pallas_call(
        body, name=name, grid=(s // tm,), in_specs=[row, row, vec, row], out_specs=[row, vec],
        out_shape=[jax.ShapeDtypeStruct((s, d), F32), jax.ShapeDtypeStruct((1, d), F32)],
        compiler_params=_cparams("arbitrary"),
    )(x, dh, g, dres)


def _out_proj_loss(y, w, x, target, *, name, tm=512, tn=1024):
    m, kdim = y.shape
    n = w.shape[1]
    tm, tn = min(tm, m), min(tn, n)

    def body(y_ref, w_ref, x_ref, t_ref, dx_ref, l_ref):
        first = jnp.logical_and(pl.program_id(0) == 0, pl.program_id(1) == 0)
        e = _dot(y_ref[...], w_ref[...]) + x_ref[...] - t_ref[...]
        dx_ref[...] = e * (1.0 / n)
        part = (0.5 / n) * jnp.sum(jnp.sum(e * e, axis=-1, keepdims=True), axis=0, keepdims=True)

        @pl.when(first)
        def _():
            l_ref[...] = part

        @pl.when(jnp.logical_not(first))
        def _():
            l_ref[...] += part

    tile = pl.BlockSpec((tm, tn), lambda j, i: (i, j))
    return pl.pallas_call(
        body, name=name, grid=(n // tn, m // tm),
        in_specs=[pl.BlockSpec((tm, kdim), lambda j, i: (i, 0)), pl.BlockSpec((kdim, tn), lambda j, i: (0, j)),
                  tile, tile],
        out_specs=[tile, pl.BlockSpec((1, 1), lambda j, i: (0, 0))],
        out_shape=[jax.ShapeDtypeStruct((m, n), F32), jax.ShapeDtypeStruct((1, 1), F32)],
        compiler_params=_cparams("arbitrary", "arbitrary"),
    )(y, w, x, target)


def _rope_tables(s):
    pos = jnp.arange(s, dtype=F32)[:, None]
    inv_r = 1.0 / (ROPE_THETA ** (jnp.arange(0, RET_HD, 2, dtype=F32) / RET_HD))
    ang = pos * inv_r[None, :]
    ret_cos = jnp.concatenate([jnp.cos(ang), jnp.cos(ang)], axis=1)
    ret_sin = jnp.concatenate([-jnp.sin(ang), jnp.sin(ang)], axis=1)
    inv_m = 1.0 / (ROPE_THETA ** (jnp.arange(0, MLA_ROPE, 2, dtype=F32) / MLA_ROPE))
    am = pos * inv_m[None, :]
    z32, z64 = jnp.zeros((s, 32), F32), jnp.zeros((s, 64), F32)
    mla_cos = jnp.concatenate([jnp.cos(am), jnp.cos(am), z64], axis=1)
    mla_sp = jnp.concatenate([z32, jnp.sin(am), z64], axis=1)
    mla_sn = jnp.concatenate([-jnp.sin(am), z32, z64], axis=1)
    return ret_cos, ret_sin, mla_cos, mla_sp, mla_sn


def _rope128(x, c, sg):
    return x * c + pltpu.roll(x, 64, 1) * sg


def _unrope128(d, c, sg):
    return d * c + pltpu.roll(d * sg, 64, 1)


def _rope64(t, c, sp, sn):
    return t * c + pltpu.roll(t, 96, 1) * sn + pltpu.roll(t, 32, 1) * sp


def _unrope64(d, c, sp, sn):
    return d * c + pltpu.roll(d * sn, 32, 1) + pltpu.roll(d * sp, 96, 1)


def _ret_pre(z, cos, sin, *, name, tm=ROW_TILE):
    s = z.shape[0]
    tm = min(tm, s)
    scale = RET_HD ** -0.5

    def body(q_ref, k_ref, c_ref, s_ref, qo_ref, ko_ref):
        c, sg = c_ref[...], s_ref[...]
        for h in range(RET_HEADS):
            sl = slice(h * RET_HD, (h + 1) * RET_HD)
            qo_ref[:, sl] = _rope128(q_ref[:, sl], c, sg)
            ko_ref[:, sl] = _rope128(k_ref[:, sl], c, sg) * scale

    seg = lambda j: pl.BlockSpec((tm, GROUP_W), lambda i: (i, j))
    tab = pl.BlockSpec((tm, RET_HD), lambda i: (i, 0))
    return pl.pallas_call(
        body, name=name, grid=(s // tm,), in_specs=[seg(0), seg(1), tab, tab],
        out_specs=[seg(0), seg(0)],
        out_shape=[jax.ShapeDtypeStruct((s, GROUP_W), F32)] * 2,
        compiler_params=_cparams("parallel"),
    )(z, z, cos, sin)


def _ret_pre_bwd(dqr, dkr, cos, sin, into, *, name, tm=ROW_TILE):
    s = dqr[0].shape[0]
    tm = min(tm, s)
    scale = RET_HD ** -0.5

    def body(dq0_ref, dq1_ref, dk0_ref, dk1_ref, c_ref, s_ref, _, o_ref):
        c, sg = c_ref[...], s_ref[...]
        for h in range(RET_HEADS):
            sl = slice(h * RET_HD, (h + 1) * RET_HD)
            ksl = slice(GROUP_W + h * RET_HD, GROUP_W + (h + 1) * RET_HD)
            o_ref[:, sl] = _bf(_unrope128(dq0_ref[:, sl] + dq1_ref[:, sl], c, sg))
            o_ref[:, ksl] = _bf(_unrope128(dk0_ref[:, sl] + dk1_ref[:, sl], c, sg) * scale)

    row = pl.BlockSpec((tm, GROUP_W), lambda i: (i, 0))
    tab = pl.BlockSpec((tm, RET_HD), lambda i: (i, 0))
    out_shape, out_spec, more_specs, more_args = _landing(into, tm, 2 * GROUP_W)
    return pl.pallas_call(
        body, name=name, grid=(s // tm,), in_specs=[row, row, row, row, tab, tab] + more_specs, out_specs=out_spec,
        out_shape=out_shape, input_output_aliases={6: 0},
        compiler_params=_cparams("parallel"),
    )(dqr[0], dqr[1], dkr[0], dkr[1], cos, sin, *more_args)


def _bla(a, b, c, lg, cols, *, name):
    s = a.shape[0]
    ch = min(RET_CHUNK, s)
    n = s // ch
    hd = RET_HD

    def body(lg_ref, a0, b0, c0, a1, b1, c1, o0, o1, st):
        t = pl.program_id(0)

        @pl.when(t == 0)
        def _():
            st[...] = jnp.zeros_like(st)

        ii = lax.broadcasted_iota(jnp.int32, (ch, ch), 0)
        jj = lax.broadcasted_iota(jnp.int32, (ch, ch), 1)
        idx = lax.broadcasted_iota(jnp.int32, (ch, 1), 0).astype(F32)
        for d, (a_ref, b_ref, c_ref, o_ref) in enumerate(((a0, b0, c0, o0), (a1, b1, c1, o1))):
            diff = ((ii - jj) if d == 0 else (jj - ii)).astype(F32)
            keep = diff >= 0
            dpos = jnp.maximum(diff, 0.0)
            pq = (idx + 1.0) if d == 0 else (ch - idx)
            pk = (ch - 1.0 - idx) if d == 0 else idx
            for h in range(RET_HEADS):
                g = lg_ref[d, h]
                sl = slice(h * hd, (h + 1) * hd)
                av, bv, cv = a_ref[:, sl], b_ref[:, sl], c_ref[:, sl]
                sc = _dot(av, bv, 1, 1) * jnp.where(keep, jnp.exp(dpos * g), 0.0)
                stv = st[d, h]
                o_ref[:, sl] = _dot(sc, cv) + _dot(av * jnp.exp(pq * g), stv)
                st[d, h] = jnp.exp(ch * g) * stv + _dot(bv * jnp.exp(pk * g), cv, 0, 0)

    fwd = lambda j: pl.BlockSpec((ch, GROUP_W), lambda t: (t, j))
    bwd = lambda j: pl.BlockSpec((ch, GROUP_W), lambda t: (n - 1 - t, j))
    return pl.pallas_call(
        body, name=name, grid=(n,),
        in_specs=[pl.BlockSpec(memory_space=pltpu.SMEM), fwd(cols[0]), fwd(cols[1]), fwd(cols[2]),
                  bwd(cols[0]), bwd(cols[1]), bwd(cols[2])],
        out_specs=[fwd(0), bwd(0)],
        out_shape=[jax.ShapeDtypeStruct((s, GROUP_W), F32)] * 2,
        scratch_shapes=[pltpu.VMEM((2, RET_HEADS, hd, hd), F32)],
        compiler_params=_cparams("arbitrary"),
    )(lg, a, b, c, a, b, c)


def _post(os_, zg, gcol, g, *, norm, name, tm=ROW_TILE):
    s = zg.shape[0]
    tm = min(tm, s)
    nd = len(os_)

    def body(*refs):
        o_refs, (gt_ref, g_ref, y_ref) = refs[:nd], refs[nd:]
        silu, _ = _silu_parts(gt_ref[...])
        for h in range(4):
            sl = slice(h * 128, (h + 1) * 128)
            o = o_refs[0][:, sl]
            for k in range(1, nd):
                o = o + o_refs[k][:, sl]
            if norm:
                r = lax.rsqrt(jnp.mean(o * o, axis=-1, keepdims=True) + EPS)
                o = o * r * g_ref[:, sl]
            y_ref[:, sl] = _bf(silu[:, sl] * o)

    row = pl.BlockSpec((tm, GROUP_W), lambda i: (i, 0))
    return pl.pallas_call(
        body, name=name, grid=(s // tm,),
        in_specs=[row] * nd + [pl.BlockSpec((tm, GROUP_W), lambda i: (i, gcol)),
                               pl.BlockSpec((1, GROUP_W), lambda i: (0, 0))],
        out_specs=row,
        out_shape=jax.ShapeDtypeStruct((s, GROUP_W), BF16),
        compiler_params=_cparams("parallel"),
    )(*os_, zg, g)


def _post_bwd(dy, ycol, os_, zg, gcol, g, into, *, norm, name, tm=ROW_TILE):
    s = zg.shape[0]
    tm = min(tm, s)
    nd = len(os_)

    def body(*refs):
        dy_ref, o_refs = refs[0], refs[1:1 + nd]
        gt_ref, g_ref, _, dgt_ref, do_ref, dg_ref = refs[1 + nd:]
        i = pl.program_id(0)
        silu, dsilu = _silu_parts(gt_ref[...])
        dyv = dy_ref[...]
        parts = []
        for h in range(4):
            sl = slice(h * 128, (h + 1) * 128)
            o = o_refs[0][:, sl]
            for k in range(1, nd):
                o = o + o_refs[k][:, sl]
            dn = dyv[:, sl] * silu[:, sl]
            if norm:
                r = lax.rsqrt(jnp.mean(o * o, axis=-1, keepdims=True) + EPS)
                xn = o * r
                gh = g_ref[:, sl]
                dgt_ref[:, sl] = _bf(dyv[:, sl] * (xn * gh) * dsilu[:, sl])
                parts.append(jnp.sum(dn * xn, axis=0, keepdims=True))
                dxn = dn * gh
                do_ref[:, sl] = r * (dxn - xn * jnp.mean(dxn * xn, axis=-1, keepdims=True))
            else:
                dgt_ref[:, sl] = _bf(dyv[:, sl] * o * dsilu[:, sl])
                parts.append(jnp.zeros((1, 128), F32))
                do_ref[:, sl] = dn
        part = jnp.concatenate(parts, axis=1)

        @pl.when(i == 0)
        def _():
            dg_ref[...] = part

        @pl.when(i > 0)
        def _():
            dg_ref[...] += part

    row = pl.BlockSpec((tm, GROUP_W), lambda i: (i, 0))
    vec = pl.BlockSpec((1, GROUP_W), lambda i: (0, 0))
    dgt_shape, dgt_spec, more_specs, more_args = _landing(into, tm, GROUP_W)
    n_in = nd + 3
    return pl.pallas_call(
        body, name=name, grid=(s // tm,),
        in_specs=[pl.BlockSpec((tm, GROUP_W), lambda i: (i, ycol))] + [row] * nd
        + [pl.BlockSpec((tm, GROUP_W), lambda i: (i, gcol)), vec] + more_specs,
        out_specs=[dgt_spec, row, vec],
        out_shape=[dgt_shape, jax.ShapeDtypeStruct((s, GROUP_W), F32), jax.ShapeDtypeStruct((1, GROUP_W), F32)],
        input_output_aliases={n_in: 0},
        compiler_params=_cparams("arbitrary"),
    )(dy, *os_, zg, g, *more_args)


def _ret_log_gamma(swap):
    gf = 1.0 - 2.0 ** (-5.0 - jnp.arange(RET_HEADS, dtype=F32))
    lf, lb = jnp.log(gf), jnp.log(gf[::-1])
    return jnp.stack([lb, lf] if swap else [lf, lb])


def _log_sigmoid(x):
    return jnp.minimum(x, 0.0) - jnp.log(1.0 + jnp.exp(-jnp.abs(x)))


def _gla_gate(z, wa, ba, *, name, tm=ROW_TILE):
    s = z.shape[0]
    tm = min(tm, s)
    col = SEG["ga"][0] // 128

    def body(ga_ref, wa_ref, ba_ref, la_ref):
        pre = _dot(ga_ref[...], wa_ref[...]) + ba_ref[...]
        la_ref[...] = _log_sigmoid(pre) / GLA_TAU

    return pl.pallas_call(
        body, name=name, grid=(s // tm,),
        in_specs=[pl.BlockSpec((tm, 128), lambda i: (i, col)), pl.BlockSpec((128, 512), lambda i: (0, 0)),
                  pl.BlockSpec((1, 512), lambda i: (0, 0))],
        out_specs=pl.BlockSpec((tm, 512), lambda i: (i, 0)),
        out_shape=jax.ShapeDtypeStruct((s, 512), F32),
        compiler_params=_cparams("parallel"),
    )(z, wa, ba)


def _gla_gate_bwd(dla, z, wa, ba, into, *, name, tm=ROW_TILE):
    s = z.shape[0]
    tm = min(tm, s)
    col = SEG["ga"][0] // 128

    def body(dla0_ref, dla1_ref, ga_ref, wa_ref, ba_ref, _, dga_ref, dwa_ref, dba_ref):
        i = pl.program_id(0)
        gav = ga_ref[...]
        pre = _dot(gav, wa_ref[...]) + ba_ref[...]
        dla_v = jnp.concatenate([dla0_ref[...], dla1_ref[...]], axis=1)
        dpre = dla_v * (1.0 - _sigmoid(pre)) * (1.0 / GLA_TAU)
        dga_ref[...] = _bf(_dot(dpre, wa_ref[...], 1, 1))
        pw = _dot(gav, dpre, 0, 0)
        pb = jnp.sum(dpre, axis=0, keepdims=True)

        @pl.when(i == 0)
        def _():
            dwa_ref[...] = pw
            dba_ref[...] = pb

        @pl.when(i > 0)
        def _():
            dwa_ref[...] += pw
            dba_ref[...] += pb

    dga_shape, dga_spec, more_specs, more_args = _landing(into, tm, 128)
    return pl.pallas_call(
        body, name=name, grid=(s // tm,),
        in_specs=[pl.BlockSpec((tm, 256), lambda i: (i, 0)), pl.BlockSpec((tm, 256), lambda i: (i, 0)),
                  pl.BlockSpec((tm, 128), lambda i: (i, col)),
                  pl.BlockSpec((128, 512), lambda i: (0, 0)), pl.BlockSpec((1, 512), lambda i: (0, 0))] + more_specs,
        out_specs=[dga_spec, pl.BlockSpec((128, 512), lambda i: (0, 0)), pl.BlockSpec((1, 512), lambda i: (0, 0))],
        out_shape=[dga_shape, jax.ShapeDtypeStruct((128, 512), F32), jax.ShapeDtypeStruct((1, 512), F32)],
        input_output_aliases={5: 0},
        compiler_params=_cparams("arbitrary"),
    )(dla[0], dla[1], z, wa, ba, *more_args)


def _gla_masks(ch):
    ii = lax.broadcasted_iota(jnp.int32, (ch, ch), 0)
    tt = lax.broadcasted_iota(jnp.int32, (ch, ch), 1)
    return jnp.where(tt <= ii, 1.0, 0.0), jnp.where(tt >= ii, 1.0, 0.0)


def _running_sum(x, up):
    n = x.shape[0]
    rows = lax.broadcasted_iota(jnp.int32, x.shape, 0)
    k = 1
    while k < n:
        if up:
            x = x + jnp.where(rows < n - k, pltpu.roll(x, n - k, 0), 0.0)
        else:
            x = x + jnp.where(rows >= k, pltpu.roll(x, k, 0), 0.0)
        k *= 2
    return x


def _gla_chunk(d, tmat, qv, kv, lav, ch):
    c = _running_sum(lav, up=(d == 1))
    big_l = c[ch - 1:ch, :] if d == 0 else c[0:1, :]
    qt = qv * (GLA_DK ** -0.5) * jnp.exp(c)
    kt = kv * jnp.exp(-c)
    kh = kv * jnp.exp(big_l - c)
    return c, big_l, qt, kt, kh


def _gla_fwd(qh, kh_, z, la, *, name, rider=None):
    s = z.shape[0]
    ch = min(GLA_CHUNK, s)
    n = s // ch
    vcol = SEG["gv"][0] // GROUP_W

    def body(q0, k0, v0, la0, q1, k1, v1, la1, o0, o1, zs0, zs1, st):
        t = pl.program_id(0)

        @pl.when(t == 0)
        def _():
            st[...] = jnp.zeros_like(st)

        masks = _gla_masks(ch)
        for d, (q_ref, k_ref, v_ref, la_ref, o_ref, zs_ref) in enumerate(
                ((q0, k0, v0, la0, o0, zs0), (q1, k1, v1, la1, o1, zs1))):
            for h in range(GLA_HEADS):
                c, big_l, qt, kt, kh = _gla_chunk(d, masks[d], q_ref[h], k_ref[h], la_ref[0, h], ch)
                vv = v_ref[:, h * GLA_DV:(h + 1) * GLA_DV]
                p = _dot(qt, kt, 1, 1) * masks[d]
                zst = st[d, h]
                o_ref[:, h * GLA_DV:(h + 1) * GLA_DV] = _dot(p, vv) + _dot(qt, zst, 1, 1)
                zs_ref[h, 0] = zst
                st[d, h] = zst * jnp.exp(big_l) + _dot(vv, kh, 0, 0)

    cidx = (lambda t: t), (lambda t: n - 1 - t)
    hs = lambda d: pl.BlockSpec((GLA_HEADS, ch, GLA_DK), lambda t: (0, cidx[d](t), 0))
    vs = lambda d: pl.BlockSpec((ch, GROUP_W), lambda t: (cidx[d](t), vcol))
    las = lambda d: pl.BlockSpec((1, GLA_HEADS, ch, GLA_DK), lambda t: (d, 0, cidx[d](t), 0))
    os_ = lambda d: pl.BlockSpec((ch, GROUP_W), lambda t: (cidx[d](t), 0))
    zss = lambda d: pl.BlockSpec((GLA_HEADS, 1, GLA_DV, GLA_DK), lambda t: (0, cidx[d](t), 0, 0))
    (o0, o1, zs0, zs1), rode = _ride_call(
        body, rider, name=name, grid=(n,),
        in_specs=[hs(0), hs(0), vs(0), las(0), hs(1), hs(1), vs(1), las(1)],
        out_specs=[os_(0), os_(1), zss(0), zss(1)],
        out_shape=[jax.ShapeDtypeStruct((s, GROUP_W), F32)] * 2
        + [jax.ShapeDtypeStruct((GLA_HEADS, n, GLA_DV, GLA_DK), F32)] * 2,
        scratch_shapes=[pltpu.VMEM((2, GLA_HEADS, GLA_DV, GLA_DK), F32)],
        args=(qh, kh_, z, la, qh, kh_, z, la), sem=("arbitrary",))
    return ((o0, o1), (zs0, zs1)) if rider is None else ((o0, o1), (zs0, zs1), rode)


def _gla_bwd(qh, kh_, z, la, do, zs, *, name, rider=None):
    s = z.shape[0]
    ch = min(GLA_CHUNK, s)
    n = s // ch
    vcol = SEG["gv"][0] // GROUP_W

    def body(q0, k0, v0, la0, do0, zs0, q1, k1, v1, la1, do1, zs1,
             dq0, dk0, dla0, dv0, dq1, dk1, dla1, dv1, gz):
        t = pl.program_id(0)

        @pl.when(t == 0)
        def _():
            gz[...] = jnp.zeros_like(gz)

        masks = _gla_masks(ch)
        rows = lax.broadcasted_iota(jnp.int32, (ch, 1), 0)
        for d, (q_ref, k_ref, v_ref, la_ref, do_ref, zs_ref, dq_ref, dk_ref, dla_ref, dv_ref) in enumerate(
                ((q0, k0, v0, la0, do0, zs0, dq0, dk0, dla0, dv0), (q1, k1, v1, la1, do1, zs1, dq1, dk1, dla1, dv1))):
            tmat = masks[d]
            end = ch - 1 if d == 0 else 0
            for h in range(GLA_HEADS):
                ksl = slice(h * GLA_DK, (h + 1) * GLA_DK)
                c, big_l, qt, kt, kh = _gla_chunk(d, tmat, q_ref[h], k_ref[h], la_ref[0, h], ch)
                vsl = slice(h * GLA_DV, (h + 1) * GLA_DV)
                vv, dov, zst, gzv = v_ref[:, vsl], do_ref[:, vsl], zs_ref[h, 0], gz[d, h]
                p = _dot(qt, kt, 1, 1) * tmat
                dp = _dot(dov, vv, 1, 1) * tmat
                dqt = _dot(dp, kt) + _dot(dov, zst)
                dkt = _dot(dp, qt, 0, 0)
                dkh = _dot(vv, gzv)
                dv_ref[:, vsl] = _dot(p, dov, 0, 0) + _dot(kh, gzv, 1, 1)
                dq_ref[:, ksl] = dqt * jnp.exp(c) * (GLA_DK ** -0.5)
                dk_ref[:, ksl] = dkt * jnp.exp(-c) + dkh * jnp.exp(big_l - c)
                e_l = jnp.exp(big_l)
                d_l = jnp.sum(dkh * kh, axis=0, keepdims=True) + e_l * jnp.sum(zst * gzv, axis=0, keepdims=True)
                dc = dqt * qt - dkt * kt - dkh * kh + jnp.where(rows == end, d_l, 0.0)
                dla_ref[:, ksl] = _running_sum(dc, up=(d == 0))
                gz[d, h] = gzv * e_l + _dot(dov, qt, 0, 0)

    cidx = (lambda t: n - 1 - t), (lambda t: t)
    hs = lambda d: pl.BlockSpec((GLA_HEADS, ch, GLA_DK), lambda t: (0, cidx[d](t), 0))
    vs = lambda d: pl.BlockSpec((ch, GROUP_W), lambda t: (cidx[d](t), vcol))
    las = lambda d: pl.BlockSpec((1, GLA_HEADS, ch, GLA_DK), lambda t: (d, 0, cidx[d](t), 0))
    row = lambda d: pl.BlockSpec((ch, GROUP_W), lambda t: (cidx[d](t), 0))
    zss = lambda d: pl.BlockSpec((GLA_HEADS, 1, GLA_DV, GLA_DK), lambda t: (0, cidx[d](t), 0, 0))
    kw = GLA_HEADS * GLA_DK
    ks = lambda d: pl.BlockSpec((ch, kw), lambda t: (cidx[d](t), 0))
    hshape = jax.ShapeDtypeStruct((s, kw), F32)
    wide = jax.ShapeDtypeStruct((s, GROUP_W), F32)
    outs, rode = _ride_call(
        body, rider, name=name, grid=(n,),
        in_specs=[hs(0), hs(0), vs(0), las(0), row(0), zss(0), hs(1), hs(1), vs(1), las(1), row(1), zss(1)],
        out_specs=[ks(0), ks(0), ks(0), row(0), ks(1), ks(1), ks(1), row(1)],
        out_shape=[hshape, hshape, hshape, wide, hshape, hshape, hshape, wide],
        scratch_shapes=[pltpu.VMEM((2, GLA_HEADS, GLA_DV, GLA_DK), F32)],
        args=(qh, kh_, z, la, do, zs[0], qh, kh_, z, la, do, zs[1]), sem=("arbitrary",))
    dq0, dk0, dla0, dv0, dq1, dk1, dla1, dv1 = outs
    res = ((dq0, dq1), (dk0, dk1), (dla0, dla1), (dv0, dv1))
    return res if rider is None else res + (rode,)


def _window_sums(win, g, shift):
    n = win.shape[0]
    levels, y = [], win
    for j in range(POOL_GROUPS):
        y = y + pltpu.roll(y, n - (1 << j), 0)
        levels.append(y)
    sums = levels[-1]
    for j in range(POOL_GROUPS - 2, -1, -1):
        sums = jnp.where(g == j, levels[j], sums)
    return pltpu.roll(sums, shift, 0)


def _pool_cnt(t0, half, rows, s):
    t = t0 + lax.broadcasted_iota(jnp.int32, (rows, 1), 0)
    return (jnp.minimum(t + half, s) - jnp.maximum(t - half, 0)).astype(F32)


def _pool_fwd(z, pw, scale, *, name):
    s = z.shape[0]
    tl = min(POOL_TILE, s)
    nt = s // tl
    ucol, gcol = SEG["pv"][0] // 128, SEG["pg"][0] // 128

    def body(u_ref, gt_ref, pw_ref, sc_ref, y_ref, pad):
        g = pl.program_id(0)
        half = jnp.left_shift(1, g)
        pad[0:POOL_HALO, :] = jnp.zeros((POOL_HALO, POOL_GW), F32)
        pad[POOL_HALO + s:POOL_HALO + s + POOL_HALO, :] = jnp.zeros((POOL_HALO, POOL_GW), F32)
        pad[POOL_HALO:POOL_HALO + s, :] = u_ref[...]
        pwv, scv = pw_ref[0], sc_ref[...]

        def tile(i, carry):
            t0 = pl.multiple_of(i * tl, tl)
            win = pad[pl.ds(t0, tl + 2 * POOL_HALO), :]
            u = win[POOL_HALO:POOL_HALO + tl, :]
            pooled = _window_sums(win, g, half)[POOL_HALO:POOL_HALO + tl, :] / _pool_cnt(t0, half, tl, s) - u
            mixed = _dot(pooled, pwv)
            silu, _ = _silu_parts(gt_ref[pl.ds(t0, tl), :])
            y_ref[pl.ds(t0, tl), :] = _bf(silu * (mixed * scv))
            return carry

        lax.fori_loop(0, nt, tile, 0)

    return pl.pallas_call(
        body, name=name, grid=(POOL_GROUPS,),
        in_specs=[pl.BlockSpec((s, POOL_GW), lambda g: (0, ucol + g)),
                  pl.BlockSpec((s, POOL_GW), lambda g: (0, gcol + g)),
                  pl.BlockSpec((1, POOL_GW, POOL_GW), lambda g: (g, 0, 0)),
                  pl.BlockSpec((1, POOL_GW), lambda g: (0, g))],
        out_specs=pl.BlockSpec((s, POOL_GW), lambda g: (0, g)),
        out_shape=jax.ShapeDtypeStruct((s, GROUP_W), BF16),
        scratch_shapes=[pltpu.VMEM((s + 2 * POOL_HALO, POOL_GW), F32)],
        compiler_params=_cparams("parallel"),
    )(z, z, pw, scale)


def _pool_bwd(dy, z, pw, scale, *, name):
    s = z.shape[0]
    tl = min(POOL_TILE, s)
    nt = s // tl
    ucol, gcol, ycol = SEG["pv"][0] // 128, SEG["pg"][0] // 128, 2 * GROUP_W // 128

    def body(dy_ref, u_ref, gt_ref, pw_ref, sc_ref, du_ref, dgt_ref, dpw_ref, dsc_ref, pad, epad, dpo):
        g = pl.program_id(0)
        half = jnp.left_shift(1, g)
        zeros = jnp.zeros((POOL_HALO, POOL_GW), F32)
        for buf in (pad, epad):
            buf[0:POOL_HALO, :] = zeros
            buf[POOL_HALO + s:POOL_HALO + s + POOL_HALO, :] = zeros
        pad[POOL_HALO:POOL_HALO + s, :] = u_ref[...]
        pwv, scv = pw_ref[0], sc_ref[...]
        dpw_ref[0] = jnp.zeros((POOL_GW, POOL_GW), F32)
        dsc_ref[...] = jnp.zeros((1, POOL_GW), F32)

        def tile(i, carry):
            t0 = pl.multiple_of(i * tl, tl)
            win = pad[pl.ds(t0, tl + 2 * POOL_HALO), :]
            u = win[POOL_HALO:POOL_HALO + tl, :]
            cnt = _pool_cnt(t0, half, tl, s)
            pooled = _window_sums(win, g, half)[POOL_HALO:POOL_HALO + tl, :] / cnt - u
            mixed = _dot(pooled, pwv)
            silu, dsilu = _silu_parts(gt_ref[pl.ds(t0, tl), :])
            dyv = dy_ref[pl.ds(t0, tl), :]
            dgt_ref[pl.ds(t0, tl), :] = _bf(dyv * (mixed * scv) * dsilu)
            dsc_ref[...] += jnp.sum(dyv * silu * mixed, axis=0, keepdims=True)
            dm = dyv * silu * scv
            dpw_ref[0] += _dot(pooled, dm, 0, 0)
            dpooled = _dot(dm, pwv, 1, 1)
            dpo[pl.ds(t0, tl), :] = dpooled
            epad[pl.ds(POOL_HALO + t0, tl), :] = dpooled / cnt
            return carry

        lax.fori_loop(0, nt, tile, 0)

        def tile2(i, carry):
            t0 = pl.multiple_of(i * tl, tl)
            ewin = epad[pl.ds(t0, tl + 2 * POOL_HALO), :]
            du_ref[pl.ds(t0, tl), :] = _bf(_window_sums(ewin, g, half - 1)[POOL_HALO:POOL_HALO + tl, :]
                                           - dpo[pl.ds(t0, tl), :])
            return carry

        lax.fori_loop(0, nt, tile2, 0)

    col = lambda c0: pl.BlockSpec((s, POOL_GW), lambda g: (0, c0 + g))
    return pl.pallas_call(
        body, name=name, grid=(POOL_GROUPS,),
        in_specs=[col(ycol), col(ucol), col(gcol), pl.BlockSpec((1, POOL_GW, POOL_GW), lambda g: (g, 0, 0)),
                  pl.BlockSpec((1, POOL_GW), lambda g: (0, g))],
        out_specs=[col(0), col(0), pl.BlockSpec((1, POOL_GW, POOL_GW), lambda g: (g, 0, 0)),
                   pl.BlockSpec((1, POOL_GW), lambda g: (0, g))],
        out_shape=[jax.ShapeDtypeStruct((s, GROUP_W), BF16), jax.ShapeDtypeStruct((s, GROUP_W), BF16),
                   jax.ShapeDtypeStruct((POOL_GROUPS, POOL_GW, POOL_GW), F32),
                   jax.ShapeDtypeStruct((1, GROUP_W), F32)],
        scratch_shapes=[pltpu.VMEM((s + 2 * POOL_HALO, POOL_GW), F32), pltpu.VMEM((s + 2 * POOL_HALO, POOL_GW), F32),
                        pltpu.VMEM((s, POOL_GW), F32)],
        compiler_params=_cparams("parallel"),
    )(dy, z, z, pw, scale)


def _mla_specs(tm):
    zq = pl.BlockSpec((tm, 512), lambda i: (i, SEG["mq"][0] // 512))
    zkv = pl.BlockSpec((tm, 256), lambda i: (i, SEG["mkv"][0] // 256))
    zkr = pl.BlockSpec((tm, 128), lambda i: (i, SEG["mkr"][0] // 128))
    full = lambda r, c: pl.BlockSpec((r, c), lambda i: (0, 0))
    tab = pl.BlockSpec((tm, 128), lambda i: (i, 0))
    weights = [full(1, 512), full(512, 1024), full(1, 256), full(256, 1024), full(1, 256), full(1, 256)]
    return [zq, zkv, zkr] + weights + [tab, tab, tab]


def _mla_project(xq_ref, xkv_ref, qg_ref, wq_ref, kvg_ref, wkv_ref):
    xq = xq_ref[...]
    r1 = lax.rsqrt(jnp.mean(xq * xq, axis=-1, keepdims=True) + EPS)
    xn1 = xq * r1
    qn = _bf(xn1 * qg_ref[...])
    qraw = _dot(qn, wq_ref[...])
    xkv = xkv_ref[...]
    r2 = lax.rsqrt(jnp.mean(xkv * xkv, axis=-1, keepdims=True) + EPS)
    xn2 = xkv * r2
    kvn = _bf(xn2 * kvg_ref[...])
    kvraw = _dot(kvn, wkv_ref[...])
    return r1, xn1, qn, qraw, r2, xn2, kvn, kvraw


def _mla_pre(z, qg, wq, kvg, wkv, qng, kng, cos, sp, sn, *, name, tm=ROW_TILE):
    s = z.shape[0]
    tm = min(tm, s)

    def body(xq_ref, xkv_ref, pe_ref, qg_ref, wq_ref, kvg_ref, wkv_ref, qng_ref, kng_ref, c_ref, sp_ref, sn_ref,
             q_ref, k_ref, v_ref):
        _, _, _, qraw, _, _, _, kvraw = _mla_project(xq_ref, xkv_ref, qg_ref, wq_ref, kvg_ref, wkv_ref)
        c, spv, snv = c_ref[...], sp_ref[...], sn_ref[...]
        pe = pe_ref[...]
        pe_ss = jnp.sum(pe * pe, axis=-1, keepdims=True)
        qngv, kngv = qng_ref[...], kng_ref[...]
        for h in range(MLA_HEADS):
            b = h * MLA_QKP
            qh = qraw[:, b:b + MLA_QKP]
            r = lax.rsqrt(jnp.sum(qh * qh, axis=-1, keepdims=True) * (1.0 / MLA_QK) + EPS)
            qn_h = qh * r * qngv
            q_ref[:, b:b + 128] = _bf(qn_h[:, :128] * MLA_SCALE)
            q_ref[:, b + 128:b + 256] = _bf(_rope64(qn_h[:, 128:], c, spv, snv) * MLA_SCALE)
            kn = kvraw[:, b:b + 128]
            rk = lax.rsqrt((jnp.sum(kn * kn, axis=-1, keepdims=True) + pe_ss) * (1.0 / MLA_QK) + EPS)
            k_ref[:, b:b + 128] = _bf(kn * rk * kngv[:, :128])
            k_ref[:, b + 128:b + 256] = _bf(_rope64(pe * rk * kngv[:, 128:], c, spv, snv))
            v_ref[:, h * MLA_V:(h + 1) * MLA_V] = _bf(kvraw[:, b + 128:b + 256])

    row = lambda w: pl.BlockSpec((tm, w), lambda i: (i, 0))
    return pl.pallas_call(
        body, name=name, grid=(s // tm,), in_specs=_mla_specs(tm),
        out_specs=[row(1024), row(1024), row(512)],
        out_shape=[jax.ShapeDtypeStruct((s, 1024), BF16), jax.ShapeDtypeStruct((s, 1024), BF16),
                   jax.ShapeDtypeStruct((s, 512), BF16)],
        compiler_params=_cparams("parallel"),
    )(z, z, z, qg, wq, kvg, wkv, qng, kng, cos, sp, sn)


def _mla_pre_bwd(dq, dk, dv, z, qg, wq, kvg, wkv, qng, kng, cos, sp, sn, *, name, tm=ROW_TILE):
    s = z.shape[0]
    tm = min(tm, s)

    def body(dq_ref, dk_ref, dv_ref, xq_ref, xkv_ref, pe_ref, qg_ref, wq_ref, kvg_ref, wkv_ref, qng_ref, kng_ref,
             c_ref, sp_ref, sn_ref, dxq_ref, dxkv_ref, dpe_ref, dwq_ref, dwkv_ref, dqg_ref, dkvg_ref, dqng_ref,
             dkng_ref, dqraw, dkvraw):
        i = pl.program_id(0)
        r1, xn1, qn, qraw, r2, xn2, kvn, kvraw = _mla_project(xq_ref, xkv_ref, qg_ref, wq_ref, kvg_ref, wkv_ref)
        c, spv, snv = c_ref[...], sp_ref[...], sn_ref[...]
        pe = pe_ref[...]
        pe_ss = jnp.sum(pe * pe, axis=-1, keepdims=True)
        qngv, kngv = qng_ref[...], kng_ref[...]
        dqng = jnp.zeros((1, MLA_QKP), F32)
        dkng = jnp.zeros((1, MLA_QKP), F32)
        dpe = jnp.zeros_like(pe)
        for h in range(MLA_HEADS):
            b = h * MLA_QKP
            qh = qraw[:, b:b + MLA_QKP]
            r = lax.rsqrt(jnp.sum(qh * qh, axis=-1, keepdims=True) * (1.0 / MLA_QK) + EPS)
            xn = qh * r
            d_n = jnp.concatenate(
                [dq_ref[:, b:b + 128], _unrope64(dq_ref[:, b + 128:b + 256], c, spv, snv)], axis=1) * MLA_SCALE
            dqng = dqng + jnp.sum(d_n * xn, axis=0, keepdims=True)
            dxn = d_n * qngv
            dqraw[:, b:b + MLA_QKP] = _bf(r * (dxn - xn * (jnp.sum(dxn * xn, axis=-1, keepdims=True) * (1.0 / MLA_QK))))
            kn = kvraw[:, b:b + 128]
            rk = lax.rsqrt((jnp.sum(kn * kn, axis=-1, keepdims=True) + pe_ss) * (1.0 / MLA_QK) + EPS)
            xk = jnp.concatenate([kn, pe], axis=1) * rk
            d_k = jnp.concatenate(
                [dk_ref[:, b:b + 128], _unrope64(dk_ref[:, b + 128:b + 256], c, spv, snv)], axis=1)
            dkng = dkng + jnp.sum(d_k * xk, axis=0, keepdims=True)
            dxk = d_k * kngv
            dfull = rk * (dxk - xk * (jnp.sum(dxk * xk, axis=-1, keepdims=True) * (1.0 / MLA_QK)))
            dkvraw[:, b:b + 128] = _bf(dfull[:, :128])
            dkvraw[:, b + 128:b + 256] = _bf(dv_ref[:, h * MLA_V:(h + 1) * MLA_V])
            dpe = dpe + dfull[:, 128:]
        dpe_ref[...] = _bf(dpe)
        dqr, dkvr = dqraw[...], dkvraw[...]
        dqn = _dot(dqr, wq_ref[...], 1, 1)
        dxn1 = dqn * qg_ref[...]
        dxq_ref[...] = _bf(r1 * (dxn1 - xn1 * jnp.mean(dxn1 * xn1, axis=-1, keepdims=True)))
        dkvn = _dot(dkvr, wkv_ref[...], 1, 1)
        dxn2 = dkvn * kvg_ref[...]
        dxkv_ref[...] = _bf(r2 * (dxn2 - xn2 * jnp.mean(dxn2 * xn2, axis=-1, keepdims=True)))
        parts = (_dot(qn, dqr, 0, 0), _dot(kvn, dkvr, 0, 0), jnp.sum(dqn * xn1, axis=0, keepdims=True),
                 jnp.sum(dkvn * xn2, axis=0, keepdims=True), dqng, dkng)
        accs = (dwq_ref, dwkv_ref, dqg_ref, dkvg_ref, dqng_ref, dkng_ref)

        @pl.when(i == 0)
        def _():
            for a, p in zip(accs, parts):
                a[...] = p

        @pl.when(i > 0)
        def _():
            for a, p in zip(accs, parts):
                a[...] += p

    row = lambda w: pl.BlockSpec((tm, w), lambda i: (i, 0))
    full = lambda r, c: pl.BlockSpec((r, c), lambda i: (0, 0))
    return pl.pallas_call(
        body, name=name, grid=(s // tm,),
        in_specs=[row(1024), row(1024), row(512)] + _mla_specs(tm),
        out_specs=[row(512), row(256), row(128), full(512, 1024), full(256, 1024), full(1, 512), full(1, 256),
                   full(1, 256), full(1, 256)],
        out_shape=[jax.ShapeDtypeStruct((s, 512), BF16), jax.ShapeDtypeStruct((s, 256), BF16),
                   jax.ShapeDtypeStruct((s, 128), BF16), jax.ShapeDtypeStruct((512, 1024), F32),
                   jax.ShapeDtypeStruct((256, 1024), F32), jax.ShapeDtypeStruct((1, 512), F32),
                   jax.ShapeDtypeStruct((1, 256), F32), jax.ShapeDtypeStruct((1, 256), F32),
                   jax.ShapeDtypeStruct((1, 256), F32)],
        scratch_shapes=[pltpu.VMEM((tm, 1024), BF16), pltpu.VMEM((tm, 1024), BF16)],
        compiler_params=_cparams("arbitrary"),
    )(dq, dk, dv, z, z, z, qg, wq, kvg, wkv, qng, kng, cos, sp, sn)


def _flash_fwd(q, k, v, *, name, tq=1024, tk=1024, rider=None):
    s = q.shape[0]
    tq, tk = min(tq, s), min(tk, s)
    nk = s // tk

    def body(q_ref, k_ref, v_ref, o_ref, lse_ref, m_s, l_s, acc):
        j = pl.program_id(2)

        @pl.when(j == 0)
        def _():
            m_s[...] = jnp.full_like(m_s, -jnp.inf)
            l_s[...] = jnp.zeros_like(l_s)
            acc[...] = jnp.zeros_like(acc)

        sc = _dot(q_ref[...], k_ref[...], 1, 1)
        m_prev = m_s[...]
        m_new = jnp.maximum(m_prev, jnp.max(sc, axis=-1, keepdims=True))
        p = jnp.exp(sc - m_new[:, 0:1])
        alpha = jnp.exp(m_prev - m_new)
        l_s[...] = alpha * l_s[...] + jnp.sum(p, axis=-1, keepdims=True)
        acc[...] = alpha * acc[...] + _dot(p, v_ref[...])
        m_s[...] = m_new

        @pl.when(j == nk - 1)
        def _():
            o_ref[...] = acc[...] / l_s[...]
            lse_ref[...] = m_s[...] + jnp.log(l_s[...])

    (o, lse), rode = _ride_call(
        body, rider, name=name, grid=(MLA_HEADS, s // tq, nk),
        in_specs=[pl.BlockSpec((tq, MLA_QKP), lambda h, i, j: (i, h)),
                  pl.BlockSpec((tk, MLA_QKP), lambda h, i, j: (j, h)),
                  pl.BlockSpec((tk, MLA_V), lambda h, i, j: (j, h))],
        out_specs=[pl.BlockSpec((tq, MLA_V), lambda h, i, j: (i, h))] * 2,
        out_shape=[jax.ShapeDtypeStruct((s, GROUP_W), F32)] * 2,
        scratch_shapes=[pltpu.VMEM((tq, MLA_V), F32), pltpu.VMEM((tq, MLA_V), F32), pltpu.VMEM((tq, MLA_V), F32)],
        args=(q, k, v), sem=("parallel", "parallel", "arbitrary"))
    return (o, lse) if rider is None else (o, lse, rode)


def _flash_bwd(q, k, v, do, o, lse, *, name, tq=1024, tk=1024, rider=None):
    s = q.shape[0]
    tq, tk = min(tq, s), min(tk, s)
    nq, nk = s // tq, s // tk

    def body(q_ref, k_ref, v_ref, do_ref, o_ref, lse_ref, dq_ref, dk_ref, dv_ref, dk_acc, dv_acc):
        j, i = pl.program_id(1), pl.program_id(2)
        dov = do_ref[...]
        delta = jnp.sum(dov * o_ref[...], axis=-1, keepdims=True)
        p = jnp.exp(_dot(q_ref[...], k_ref[...], 1, 1) - lse_ref[:, 0:1])
        ds = p * (_dot(dov, v_ref[...], 1, 1) - delta)
        pv = _dot(p, dov, 0, 0)
        pk = _dot(ds, q_ref[...], 0, 0)
        pq = _dot(ds, k_ref[...])
        rows = pl.ds(pl.multiple_of(i * tq, tq), tq)

        @pl.when(j == 0)
        def _():
            dq_ref[rows, :] = pq

        @pl.when(j > 0)
        def _():
            dq_ref[rows, :] += pq

        @pl.when(i == 0)
        def _():
            dv_acc[...] = pv
            dk_acc[...] = pk

        @pl.when(i > 0)
        def _():
            dv_acc[...] += pv
            dk_acc[...] += pk

        @pl.when(i == nq - 1)
        def _():
            dk_ref[...] = dk_acc[...]
            dv_ref[...] = dv_acc[...]

    qb = pl.BlockSpec((tq, MLA_QKP), lambda h, j, i: (i, h))
    kb = pl.BlockSpec((tk, MLA_QKP), lambda h, j, i: (j, h))
    vb = pl.BlockSpec((tk, MLA_V), lambda h, j, i: (j, h))
    ob = pl.BlockSpec((tq, MLA_V), lambda h, j, i: (i, h))
    (dq, dk, dv), rode = _ride_call(
        body, rider, name=name, grid=(MLA_HEADS, nk, nq),
        in_specs=[qb, kb, vb, ob, ob, ob],
        out_specs=[pl.BlockSpec((s, MLA_QKP), lambda h, j, i: (0, h)), kb, vb],
        out_shape=[jax.ShapeDtypeStruct((s, MLA_HEADS * MLA_QKP), F32),
                   jax.ShapeDtypeStruct((s, MLA_HEADS * MLA_QKP), F32), jax.ShapeDtypeStruct((s, GROUP_W), F32)],
        scratch_shapes=[pltpu.VMEM((tk, MLA_QKP), F32), pltpu.VMEM((tk, MLA_V), F32)],
        args=(q, k, v, do, o, lse), sem=("arbitrary", "arbitrary", "arbitrary"))
    return (dq, dk, dv) if rider is None else (dq, dk, dv, rode)


def _rows_tile(r, c, itemsize=4, budget=2 * 1024 * 1024):
    if r * c * itemsize <= budget:
        return r
    best = None
    for t in range(8, r, 8):
        if r % t == 0 and t * c * itemsize <= budget:
            best = t
    return best if best is not None else r


def _landing(into, tm, width):
    buf, col = into
    assert col % width == 0
    return (jax.ShapeDtypeStruct(buf.shape, buf.dtype), pl.BlockSpec((tm, width), lambda i: (i, col // width)),
            [ANY], [buf])


def _add_n(arrs, *, out_dtype=F32, name, into=None):
    shape = arrs[0].shape
    c = shape[-1]
    flat = [a.reshape(-1, c) for a in arrs]
    r = flat[0].shape[0]
    t = _rows_tile(r, c)
    n_in = len(flat)

    def body(*refs):
        acc = refs[0][...].astype(F32)
        for ref in refs[1:n_in]:
            acc = acc + ref[...].astype(F32)
        refs[-1][...] = acc.astype(out_dtype)

    blk = pl.BlockSpec((t, c), lambda i: (i, 0))
    if into is not None:
        out_shape, out_spec, more_specs, more_args = _landing(into, t, c)
        return pl.pallas_call(
            body, name=name, grid=(r // t,), in_specs=[blk] * n_in + more_specs, out_specs=out_spec,
            out_shape=out_shape, input_output_aliases={n_in: 0}, compiler_params=_cparams("parallel"),
        )(*flat, *more_args)
    out = pl.pallas_call(
        body, name=name, grid=(r // t,), in_specs=[blk] * n_in, out_specs=blk,
        out_shape=jax.ShapeDtypeStruct((r, c), out_dtype), compiler_params=_cparams("parallel"),
    )(*flat)
    return out.reshape(shape)


def _adamw(w, g, m, v, *, name):
    shape = w.shape
    c = shape[-1]
    flat = [a.reshape(-1, c) for a in (w, g, m, v)]
    r = flat[0].shape[0]
    t = _rows_tile(r, c, budget=1024 * 1024)

    def body(w_ref, g_ref, m_ref, v_ref, d_ref, mo_ref, vo_ref):
        gv = g_ref[...]
        m2 = ADAM_B1 * m_ref[...] + (1.0 - ADAM_B1) * gv
        v2 = ADAM_B2 * v_ref[...] + (1.0 - ADAM_B2) * (gv * gv)
        m_hat = m2 / (1.0 - ADAM_B1 ** ADAM_STEP)
        v_hat = v2 / (1.0 - ADAM_B2 ** ADAM_STEP)
        d_ref[...] = -ADAM_LR * (m_hat / (jnp.sqrt(v_hat) + ADAM_EPS) + ADAM_WD * w_ref[...])
        mo_ref[...] = m2
        vo_ref[...] = v2

    blk = pl.BlockSpec((t, c), lambda i: (i, 0))
    outs = pl.pallas_call(
        body, name=name, grid=(r // t,), in_specs=[blk] * 4, out_specs=[blk] * 3,
        out_shape=[jax.ShapeDtypeStruct((r, c), F32)] * 3, compiler_params=_cparams("parallel"),
    )(*flat)
    return tuple(o.reshape(shape) for o in outs)


def _place():
    x, y, c = lax.axis_index("x"), lax.axis_index("y"), lax.axis_index("c")
    chips = [(1 - x, y), (x, 1 - y), (1 - x, 1 - y)]
    return x, y, c, chips


ANY = pl.BlockSpec(memory_space=pl.ANY)


def _half(ref, axis, hc, lead=()):
    n = ref.shape[len(lead) + axis] // 2
    return ref.at[tuple(lead) + (slice(None),) * axis + (pl.ds(hc * n, n),)]


def _gather_shards(shards, axes, *, name):
    nt = len(shards)

    def body(*refs):
        src, dst = refs[:nt], refs[nt:2 * nt]
        send, recv, fsend, frecv, lsem = refs[2 * nt:]
        x, y, c, chips = _place()
        me = 2 * x + y
        local = [pltpu.make_async_copy(src[t], dst[t].at[me], lsem.at[t]) for t in range(nt)]
        for cp in local:
            cp.start()

        def half(t, slot, hc):
            return _half(dst[t], axes[t], hc, lead=(slot,))

        def first(t, k):
            return pltpu.make_async_remote_copy(
                src_ref=_half(src[t], axes[t], c), dst_ref=half(t, me, c),
                send_sem=send.at[t, k], recv_sem=recv.at[t, k],
                device_id=(chips[k][0], chips[k][1], c), device_id_type=MESH)

        def landed(t, k):
            slot = 2 * chips[k][0] + chips[k][1]
            return pltpu.make_async_remote_copy(
                src_ref=half(t, slot, c), dst_ref=half(t, slot, c),
                send_sem=send.at[t, k], recv_sem=recv.at[t, k],
                device_id=(chips[k][0], chips[k][1], c), device_id_type=MESH)

        def forward(t, k, hc):
            slot = 2 * chips[k][0] + chips[k][1]
            return pltpu.make_async_remote_copy(
                src_ref=half(t, slot, hc), dst_ref=half(t, slot, hc),
                send_sem=fsend.at[t, k], recv_sem=frecv.at[t, k],
                device_id=(x, y, 1 - c), device_id_type=MESH)

        for t in range(nt):
            for k in range(3):
                first(t, k).start()
        for t in range(nt):
            for k in range(3):
                landed(t, k).wait_recv()
                forward(t, k, c).start()
        for t in range(nt):
            for k in range(3):
                forward(t, k, 1 - c).wait_recv()
        for t in range(nt):
            for k in range(3):
                first(t, k).wait_send()
                forward(t, k, c).wait_send()
        for cp in local:
            cp.wait()

    return pl.pallas_call(
        body, name=name, in_specs=[ANY] * nt, out_specs=[ANY] * nt,
        out_shape=[jax.ShapeDtypeStruct((N_CHIP,) + a.shape, a.dtype) for a in shards],
        scratch_shapes=[pltpu.SemaphoreType.DMA((nt, 3)), pltpu.SemaphoreType.DMA((nt, 3)),
                        pltpu.SemaphoreType.DMA((nt, 3)), pltpu.SemaphoreType.DMA((nt, 3)),
                        pltpu.SemaphoreType.DMA((nt,))],
    )(*shards)


def _comm_rows(hr, c, budget=2 * 1024 * 1024):
    if hr * c * 4 <= budget:
        return hr
    best = None
    for t in range(16, hr, 16):
        if hr % t == 0 and t * c * 4 <= budget:
            best = t
    return best if best is not None else hr


def _comm_cols(r, hc, budget=2 * 1024 * 1024):
    best = 128
    for t in range(128, hc + 1, 128):
        if hc % t == 0 and r * t * 4 <= budget:
            best = t
    return best


def _comm_chunks(shape, axis):
    r, cdim = shape
    if axis == 0:
        rc = _comm_rows(r // 2, cdim)
        nt = (r // 2) // rc
        return (rc, cdim), nt, (lambda h, t: (h * nt + t, 0))
    cc = _comm_cols(r, cdim // 2)
    nt = (cdim // 2) // cc
    return (r, cc), nt, (lambda h, t: (0, h * nt + t))


def _pair_reduce(g, where, axis, *, out_dtype, name):
    n_slot, r, cdim = g.shape
    blk_shape, nr, at = _comm_chunks((r, cdim), axis)
    steps = n_slot * nr
    half_shape = (r // 2, cdim) if axis == 0 else (r, cdim // 2)

    def body(w_ref, a_ref, b_ref, o_ref, land, send, recv, credit):
        x, y, c, _ = _place()
        sib = (x, y, 1 - c)
        i = pl.program_id(0) * nr + pl.program_id(1)
        s = lax.rem(i, 2)

        @pl.when(i >= 2)
        def _():
            pl.semaphore_wait(credit.at[s], 1)

        cp = pltpu.make_async_remote_copy(src_ref=b_ref.at[0], dst_ref=land.at[s], send_sem=send.at[s],
                                          recv_sem=recv.at[s], device_id=sib, device_id_type=MESH)
        cp.start()
        cp.wait_recv()
        o_ref[0] = (a_ref[0] + land[s]).astype(out_dtype)
        cp.wait_send()

        @pl.when(i + 2 < steps)
        def _():
            pl.semaphore_signal(credit.at[s], inc=1, device_id=sib, device_id_type=MESH)

    blk = lambda half: pl.BlockSpec((1,) + blk_shape, lambda j, t, w: (j,) + at(half(w), t))
    grid_spec = pltpu.PrefetchScalarGridSpec(
        num_scalar_prefetch=1, grid=(n_slot, nr),
        in_specs=[blk(lambda w: w[0]), blk(lambda w: 1 - w[0])],
        out_specs=pl.BlockSpec((1,) + blk_shape, lambda j, t, w: (j,) + at(0, t)),
        scratch_shapes=[pltpu.VMEM((2,) + blk_shape, F32), pltpu.SemaphoreType.DMA((2,)),
                        pltpu.SemaphoreType.DMA((2,)), pltpu.SemaphoreType.REGULAR((2,))])
    return pl.pallas_call(
        body, name=name, grid_spec=grid_spec, out_shape=jax.ShapeDtypeStruct((n_slot,) + half_shape, out_dtype),
        compiler_params=_cparams("arbitrary", "arbitrary"),
    )(where, g, g)


def _chip_exchange(parts, *, name):
    nt = len(parts)

    def body(*refs):
        src, got = refs[:nt], refs[nt:2 * nt]
        send, recv = refs[2 * nt:]
        x, y, c, chips = _place()
        remote = []
        for t in range(nt):
            for k in range(3):
                remote.append(pltpu.make_async_remote_copy(
                    src_ref=src[t].at[2 * chips[k][0] + chips[k][1]], dst_ref=got[t].at[k],
                    send_sem=send.at[t, k], recv_sem=recv.at[t, k],
                    device_id=(chips[k][0], chips[k][1], c), device_id_type=MESH))
        for cp in remote:
            cp.start()
        for cp in remote:
            cp.wait_recv()
        for cp in remote:
            cp.wait_send()

    return pl.pallas_call(
        body, name=name, in_specs=[ANY] * nt, out_specs=[ANY] * nt,
        out_shape=[jax.ShapeDtypeStruct((3,) + a.shape[1:], a.dtype) for a in parts],
        scratch_shapes=[pltpu.SemaphoreType.DMA((nt, 3)), pltpu.SemaphoreType.DMA((nt, 3))],
    )(*parts)


def _sum_join(p, got, where, axis, *, name):
    _, hr, cdim = p.shape
    full = (2 * hr, cdim) if axis == 0 else (hr, 2 * cdim)
    blk_shape, n, at = _comm_chunks(full, axis)
    step_len = blk_shape[axis]
    half_len = full[axis] // 2

    def body(w_ref, p_ref, g_ref, out, buf, lsem, ssem, rsem):
        x, y, c, _ = _place()
        sib = (x, y, 1 - c)
        r = pl.program_id(0)

        def part(start, size):
            return out.at[(slice(None),) * axis + (pl.ds(start, size),)]

        def copies(step, slot):
            rows = part(pl.multiple_of(c * half_len + step * step_len, 8 if axis == 0 else 128), step_len)
            return (pltpu.make_async_copy(buf.at[slot], rows, lsem.at[slot]),
                    pltpu.make_async_remote_copy(src_ref=buf.at[slot], dst_ref=rows, send_sem=ssem.at[slot],
                                                 recv_sem=rsem, device_id=sib, device_id_type=MESH))

        s = lax.rem(r, 2)

        @pl.when(r >= 2)
        def _():
            lc, rm = copies(r - 2, s)
            lc.wait()
            rm.wait_send()

        buf[s] = p_ref[0].astype(F32) + g_ref[0].astype(F32) + g_ref[1].astype(F32) + g_ref[2].astype(F32)
        lc, rm = copies(r, s)
        lc.start()
        rm.start()

        @pl.when(r == n - 1)
        def _():
            for step in range(max(0, n - 2), n):
                lc, rm = copies(step, step % 2)
                lc.wait()
                rm.wait_send()
            whole = part(0, half_len)
            pltpu.make_async_remote_copy(src_ref=whole, dst_ref=whole, send_sem=ssem.at[0], recv_sem=rsem,
                                         device_id=sib, device_id_type=MESH).wait_recv()

    grid_spec = pltpu.PrefetchScalarGridSpec(
        num_scalar_prefetch=1, grid=(n,),
        in_specs=[pl.BlockSpec((1,) + blk_shape, lambda t, w: (w[1],) + at(0, t)),
                  pl.BlockSpec((3,) + blk_shape, lambda t, w: (0,) + at(0, t))],
        out_specs=ANY,
        scratch_shapes=[pltpu.VMEM((2,) + blk_shape, F32), pltpu.SemaphoreType.DMA((2,)),
                        pltpu.SemaphoreType.DMA((2,)), pltpu.SemaphoreType.DMA])
    return pl.pallas_call(
        body, name=name, grid_spec=grid_spec, out_shape=jax.ShapeDtypeStruct(full, F32),
        compiler_params=_cparams("arbitrary"),
    )(where, p, got)


def _rider_gather_send(shards, axes):
    nt = len(shards)

    def copies(src, dst, send, recv, lsem):
        x, y, c, chips = _place()
        me = 2 * x + y
        local = [pltpu.make_async_copy(src[t], dst[t].at[me], lsem.at[t]) for t in range(nt)]
        out, landed = [], []
        for t in range(nt):
            for k in range(3):
                peer = (chips[k][0], chips[k][1], c)
                out.append(pltpu.make_async_remote_copy(
                    src_ref=_half(src[t], axes[t], c), dst_ref=_half(dst[t], axes[t], c, lead=(me,)),
                    send_sem=send.at[t, k], recv_sem=recv.at[t, k], device_id=peer, device_id_type=MESH))
                theirs = _half(dst[t], axes[t], c, lead=(2 * chips[k][0] + chips[k][1],))
                landed.append(pltpu.make_async_remote_copy(
                    src_ref=theirs, dst_ref=theirs, send_sem=send.at[t, k], recv_sem=recv.at[t, k],
                    device_id=peer, device_id_type=MESH))
        return local, out, landed

    def start(src, dst, sems):
        local, out, _ = copies(src, dst, *sems)
        for cp in local + out:
            cp.start()

    def finish(src, dst, sems):
        local, out, landed = copies(src, dst, *sems)
        for cp in landed:
            cp.wait_recv()
        for cp in out:
            cp.wait_send()
        for cp in local:
            cp.wait()

    return _Rider(shards, [jax.ShapeDtypeStruct((N_CHIP,) + a.shape, a.dtype) for a in shards],
                  [pltpu.SemaphoreType.DMA((nt, 3)), pltpu.SemaphoreType.DMA((nt, 3)), pltpu.SemaphoreType.DMA((nt,))],
                  start, finish)


def _rider_gather_forward(bufs, axes):
    nt = len(bufs)

    def copies(src, dst, send, recv):
        x, y, c, chips = _place()
        mine, theirs = [], []
        for t in range(nt):
            for k in range(3):
                slot = 2 * chips[k][0] + chips[k][1]
                for hc, into in ((c, mine), (1 - c, theirs)):
                    into.append(pltpu.make_async_remote_copy(
                        src_ref=_half(src[t], axes[t], hc, lead=(slot,)),
                        dst_ref=_half(dst[t], axes[t], hc, lead=(slot,)),
                        send_sem=send.at[t, k], recv_sem=recv.at[t, k], device_id=(x, y, 1 - c), device_id_type=MESH))
        return mine, theirs

    def start(src, dst, sems):
        for cp in copies(src, dst, *sems)[0]:
            cp.start()

    def finish(src, dst, sems):
        mine, theirs = copies(src, dst, *sems)
        for cp in theirs:
            cp.wait_recv()
        for cp in mine:
            cp.wait_send()

    return _Rider(bufs, [jax.ShapeDtypeStruct(a.shape, a.dtype) for a in bufs],
                  [pltpu.SemaphoreType.DMA((nt, 3)), pltpu.SemaphoreType.DMA((nt, 3))], start, finish,
                  aliases={t: t for t in range(nt)})


def _rider_chip_exchange(parts):
    nt = len(parts)

    def copies(src, got, send, recv):
        x, y, c, chips = _place()
        return [pltpu.make_async_remote_copy(
            src_ref=src[t].at[2 * chips[k][0] + chips[k][1]], dst_ref=got[t].at[k], send_sem=send.at[t, k],
            recv_sem=recv.at[t, k], device_id=(chips[k][0], chips[k][1], c), device_id_type=MESH)
            for t in range(nt) for k in range(3)]

    def start(src, got, sems):
        for cp in copies(src, got, *sems):
            cp.start()

    def finish(src, got, sems):
        remote = copies(src, got, *sems)
        for cp in remote:
            cp.wait_recv()
        for cp in remote:
            cp.wait_send()

    return _Rider(parts, [jax.ShapeDtypeStruct((3,) + a.shape[1:], a.dtype) for a in parts],
                  [pltpu.SemaphoreType.DMA((nt, 3)), pltpu.SemaphoreType.DMA((nt, 3))], start, finish)


def _gather_all(block, *, name):
    m_per, n = block.shape

    def body(x_ref, out_ref, send_sems, recv_sems, local_sem):
        x, y, c, chips = _place()
        me, sibling = (x, y, c), (x, y, 1 - c)

        def rows(px, py, pc):
            return out_ref.at[4 * px + 2 * py + pc]

        def copy(k, blk, to, src=None):
            return pltpu.make_async_remote_copy(
                src_ref=rows(*blk) if src is None else src, dst_ref=rows(*blk),
                send_sem=send_sems.at[k], recv_sem=recv_sems.at[k], device_id=to, device_id_type=MESH)

        mine = pltpu.make_async_copy(x_ref, rows(*me), local_sem)
        mine.start()
        first = [copy(0, me, sibling, src=x_ref)]
        first += [copy(1 + j, me, (*chip, c), src=x_ref) for j, chip in enumerate(chips)]
        for cp in first:
            cp.start()
        passed = [copy(4 + j, (*chip, c), sibling) for j, chip in enumerate(chips)]
        for j, chip in enumerate(chips):
            copy(1 + j, (*chip, c), me).wait_recv()
            passed[j].start()
        copy(0, sibling, me).wait_recv()
        for j, chip in enumerate(chips):
            copy(4 + j, (*chip, 1 - c), me).wait_recv()
        for cp in first + passed:
            cp.wait_send()
        mine.wait()

    return pl.pallas_call(
        body, name=name,
        out_shape=jax.ShapeDtypeStruct((N_DEV, m_per, n), block.dtype),
        in_specs=[pl.BlockSpec(memory_space=pltpu.VMEM)], out_specs=pl.BlockSpec(memory_space=pltpu.VMEM),
        scratch_shapes=[pltpu.SemaphoreType.DMA((7,)), pltpu.SemaphoreType.DMA((7,)), pltpu.SemaphoreType.DMA],
        compiler_params=pltpu.CompilerParams(vmem_limit_bytes=VMEM_LIMIT),
    )(block)


def _sum_slots(slots, *, name):
    n, m, c = slots.shape
    t = _rows_tile(m, c * n)

    def body(s_ref, o_ref):
        acc = s_ref[0]
        for k in range(1, n):
            acc = acc + s_ref[k]
        o_ref[...] = acc

    return pl.pallas_call(
        body, name=name, grid=(m // t,), in_specs=[pl.BlockSpec((n, t, c), lambda i: (0, i, 0))],
        out_specs=pl.BlockSpec((t, c), lambda i: (i, 0)), out_shape=jax.ShapeDtypeStruct((m, c), F32),
        compiler_params=_cparams("parallel"),
    )(slots)


def _pad_rows(a, rows):
    return a if a.shape[0] == rows else jnp.pad(a, ((0, rows - a.shape[0]), (0, 0)))


def _w_in_padded(shards):
    full = shards.reshape(IN_COLS, shards.shape[2])
    return jnp.concatenate([_pad_rows(full[SEG[n][2]:SEG[n][2] + SEG[n][3]], SEG[n][1]) for n in SEG_ORDER], axis=0)


def _w_in_unpadded(gp):
    full = jnp.concatenate([gp[SEG[n][0]:SEG[n][0] + SEG[n][3]] for n in ORIG_ORDER], axis=0)
    return full.reshape(N_CHIP, IN_COLS // N_CHIP, gp.shape[1])


def _pad_heads(w, true_w, pad_w):
    r = w.shape[0]
    h = w.shape[1] // true_w
    return jnp.pad(w.reshape(r, h, true_w), ((0, 0), (0, 0), (0, pad_w - true_w))).reshape(r, h * pad_w)


def _unpad_heads(w, true_w, pad_w):
    r = w.shape[0]
    h = w.shape[1] // pad_w
    return w.reshape(r, h, pad_w)[:, :, :true_w].reshape(r, h * true_w)


def _cols_to_slots(a):
    return a.reshape(a.shape[0], N_CHIP, a.shape[1] // N_CHIP).transpose(1, 0, 2)


def _to_heads(a, h, d):
    return a.reshape(a.shape[0], h, d).transpose(1, 0, 2)


def _slots_to_cols(a):
    return jnp.concatenate([a[j] for j in range(N_CHIP)], axis=1)


SMALL = [("norm_g", 2048), ("ret_norm_g", 512), ("gla_ba_f", 256), ("gla_ba_b", 256), ("gla_norm_g", 512),
         ("pool_w", 4 * 128 * 128), ("pool_scale", 512), ("mla_q_norm_g", 512), ("mla_kv_norm_g", 256),
         ("mla_qk_norm_q", 192), ("mla_qk_norm_k", 192)]


def _pack_small(vals):
    parts = []
    for name, n in SMALL:
        parts += [v.reshape(-1) for v in vals[name]]
        if (DEPTH * n) % 1024:
            parts.append(jnp.zeros((-(DEPTH * n)) % 1024, F32))
    parts += [vals["loss"].reshape(-1), jnp.zeros(1023, F32)]
    return jnp.concatenate(parts).reshape(-1, 128)


def _unpack_small(block):
    flat = block.reshape(-1)
    out, off = {}, 0
    for name, n in SMALL:
        out[name] = flat[off:off + DEPTH * n]
        off += DEPTH * n + (-(DEPTH * n)) % 1024
    out["loss"] = flat[off]
    return out


def _layer_weights(l, p, g):
    wa = jnp.zeros((128, 512), F32)
    wa = wa.at[0:GLA_RANK, 0:256].set(_slots_to_cols(g["gla_wa2_f"]))
    wa = wa.at[GLA_RANK:2 * GLA_RANK, 256:512].set(_slots_to_cols(g["gla_wa2_b"]))
    return dict(
        norm_g=p["norm_g"][l][None, :],
        w_in=_w_in_padded(g["w_in"]),
        w_out=g["w_out"].reshape(4 * g["w_out"].shape[1], -1),
        ret_norm_g=p["ret_norm_g"][l][None, :],
        wa=_bf(wa),
        ba=jnp.concatenate([p["gla_ba_f"][l], p["gla_ba_b"][l]])[None, :],
        gla_norm_g=p["gla_norm_g"][l][None, :],
        pool_w=_bf(p["pool_w"][l]),
        pool_scale=p["pool_scale"][l][None, :],
        qg=p["mla_q_norm_g"][l][None, :],
        wq=_pad_heads(_slots_to_cols(g["mla_wq_b"]), MLA_QK, MLA_QKP),
        kvg=p["mla_kv_norm_g"][l][None, :],
        wkv=_slots_to_cols(g["mla_wkv_b"]),
        qng=jnp.pad(p["mla_qk_norm_q"][l], (0, MLA_QKP - MLA_QK))[None, :],
        kng=jnp.pad(p["mla_qk_norm_k"][l], (0, MLA_QKP - MLA_QK))[None, :],
    )


def _layer_fwd(l, x, w, tabs, next_shards=None, loss_target=None):
    ret_cos, ret_sin, mla_cos, mla_sp, mla_sn = tabs
    nm = lambda s: f"l{l}_{s}"
    h = _rmsnorm_fwd(x, w["norm_g"], name=nm("norm"))
    if next_shards is None:
        z = _matmul(h, w["w_in"], tb=True, name=nm("in_proj"))
    else:
        z, landed = _matmul(h, w["w_in"], tb=True, rider=_rider_gather_send(next_shards[:1], SHARD_AXES[:1]),
                            name=nm("in_proj"))
    qr, kr = _ret_pre(z, ret_cos, ret_sin, name=nm("ret_pre"))
    ret_o = _bla(qr, kr, z, _ret_log_gamma(False), (0, 0, SEG["rv"][0] // 512), name=nm("ret_scan"))
    y_a = _post(ret_o, z, SEG["rg"][0] // 512, w["ret_norm_g"], norm=True, name=nm("ret_post"))
    la = _gla_gate(z, w["wa"], w["ba"], name=nm("gla_gate"))
    la_h = la.reshape(la.shape[0], 2, GLA_HEADS, GLA_DK).transpose(1, 2, 0, 3)
    gq = _to_heads(z[:, SEG["gq"][0]:SEG["gq"][0] + 256], GLA_HEADS, GLA_DK)
    gk = _to_heads(z[:, SEG["gk"][0]:SEG["gk"][0] + 256], GLA_HEADS, GLA_DK)
    if next_shards is None:
        gla_o, gla_st = _gla_fwd(gq, gk, z, la_h, name=nm("gla_scan"))
    else:
        gla_o, gla_st, more = _gla_fwd(gq, gk, z, la_h, rider=_rider_gather_send(next_shards[1:], SHARD_AXES[1:]),
                                       name=nm("gla_scan"))
        landed = list(landed) + list(more)
    y_b = _post(gla_o, z, SEG["gg"][0] // 512, w["gla_norm_g"], norm=True, name=nm("gla_post"))
    y_c = _pool_fwd(z, w["pool_w"], w["pool_scale"], name=nm("pool"))
    q, k, v = _mla_pre(z, w["qg"], w["wq"], w["kvg"], w["wkv"], w["qng"], w["kng"], mla_cos, mla_sp, mla_sn,
                       name=nm("mla_pre"))
    if next_shards is None:
        (att_o, lse), gathered = _flash_fwd(q, k, v, name=nm("attn")), None
    else:
        att_o, lse, gathered = _flash_fwd(q, k, v, rider=_rider_gather_forward(landed, SHARD_AXES), name=nm("attn"))
    y_d = _post([att_o], z, SEG["mg"][0] // 512, w["qg"], norm=False, name=nm("mla_post"))
    y = jnp.concatenate([y_a, y_b, y_c, y_d], axis=1)
    if loss_target is None:
        x_next = _matmul(y, w["w_out"], add=x, name=nm("out_proj"))
    else:
        x_next = _out_proj_loss(y, w["w_out"], x, loss_target, name=nm("out_proj"))
    saved = dict(x=x, h=h, z=z, y=y, qr=qr, kr=kr, ret_o=ret_o, la_h=la_h, gq=gq, gk=gk, gla_o=gla_o, gla_st=gla_st,
                 q=q, k=k, v=v, att_o=att_o, lse=lse)
    return x_next, saved, gathered


def _layer_bwd(l, dx_next, w, sv, tabs, riding_parts=None, where=None):
    ret_cos, ret_sin, mla_cos, mla_sp, mla_sn = tabs
    nm = lambda s: f"l{l}_{s}"
    z = sv["z"]
    dy = _matmul(dx_next, w["w_out"], tb=True, name=nm("out_proj_dy"))
    d_w_out = _matmul(sv["y"], dx_next, ta=True, tn=512, name=nm("out_proj_dw"))
    d_w_out = d_w_out.reshape(N_CHIP, d_w_out.shape[0] // N_CHIP, d_w_out.shape[1])
    if where is not None:
        pair_w_out = _pair_reduce(d_w_out, where, 0, out_dtype=BF16, name=nm("pair_reduce_w_out"))
    dz = lax.empty((z.shape[0], IN_PAD), BF16)
    at = lambda n: SEG[n][0]
    dz, d_ret_o, d_ret_g = _post_bwd(dy, 0, sv["ret_o"], z, SEG["rg"][0] // 512, w["ret_norm_g"], (dz, at("rg")),
                                     norm=True, name=nm("ret_post_bwd"))
    vcol = SEG["rv"][0] // 512
    dqr = _bla(d_ret_o, z, sv["kr"], _ret_log_gamma(False), (0, vcol, 0), name=nm("ret_scan_dq"))
    dkr = _bla(z, d_ret_o, sv["qr"], _ret_log_gamma(True), (vcol, 0, 0), name=nm("ret_scan_dk"))
    drv = _bla(sv["kr"], sv["qr"], d_ret_o, _ret_log_gamma(True), (0, 0, 0), name=nm("ret_scan_dv"))
    dz = _ret_pre_bwd(dqr, dkr, ret_cos, ret_sin, (dz, at("rq")), name=nm("ret_pre_bwd"))
    dz = _add_n([drv[0], drv[1]], out_dtype=BF16, into=(dz, at("rv")), name=nm("ret_dv_sum"))
    dz, d_gla_o, d_gla_g = _post_bwd(dy, 1, sv["gla_o"], z, SEG["gg"][0] // 512, w["gla_norm_g"], (dz, at("gg")),
                                     norm=True, name=nm("gla_post_bwd"))
    if where is None:
        dq2, dk2, dla2, dv2 = _gla_bwd(sv["gq"], sv["gk"], z, sv["la_h"], d_gla_o, sv["gla_st"],
                                       name=nm("gla_scan_bwd"))
    else:
        dq2, dk2, dla2, dv2, (others_w_out,) = _gla_bwd(
            sv["gq"], sv["gk"], z, sv["la_h"], d_gla_o, sv["gla_st"], rider=_rider_chip_exchange([pair_w_out]),
            name=nm("gla_scan_bwd"))
        d_w_out = (pair_w_out, others_w_out)
    d_gq = _bf(dq2[0] + dq2[1])
    d_gk = _bf(dk2[0] + dk2[1])
    dz = _add_n([dv2[0], dv2[1]], out_dtype=BF16, into=(dz, at("gv")), name=nm("gla_dv_sum"))
    dz, d_wa, d_ba = _gla_gate_bwd(dla2, z, w["wa"], w["ba"], (dz, at("ga")), name=nm("gla_gate_bwd"))
    d_pv, d_pg, d_pool_w, d_pool_scale = _pool_bwd(dy, z, w["pool_w"], w["pool_scale"], name=nm("pool_bwd"))
    dz, d_att_o, _ = _post_bwd(dy, 3, [sv["att_o"]], z, SEG["mg"][0] // 512, w["qg"], (dz, at("mg")), norm=False,
                               name=nm("mla_post_bwd"))
    if riding_parts is None:
        (dq, dk, dv), rode = _flash_bwd(sv["q"], sv["k"], sv["v"], d_att_o, sv["att_o"], sv["lse"],
                                        name=nm("attn_bwd")), None
    else:
        dq, dk, dv, rode = _flash_bwd(sv["q"], sv["k"], sv["v"], d_att_o, sv["att_o"], sv["lse"],
                                      rider=_rider_chip_exchange(riding_parts), name=nm("attn_bwd"))
    d_mq, d_mkv, d_mkr, d_wq, d_wkv, d_qg, d_kvg, d_qng, d_kng = _mla_pre_bwd(
        dq, dk, dv, z, w["qg"], w["wq"], w["kvg"], w["wkv"], w["qng"], w["kng"], mla_cos, mla_sp, mla_sn,
        name=nm("mla_pre_bwd"))
    for n, seg in dict(pv=d_pv, pg=d_pg, mq=d_mq, gq=d_gq, gk=d_gk, mkv=d_mkv, mkr=d_mkr).items():
        dz = lax.dynamic_update_slice(dz, seg, (0, at(n)))
    dh = _matmul(dz, w["w_in"], tn=512, name=nm("in_proj_dh"))
    d_w_in = _matmul(dz, sv["h"], ta=True, name=nm("in_proj_dw"))
    dx, d_norm_g = _rmsnorm_bwd(sv["x"], dh, w["norm_g"], dx_next, name=nm("norm_bwd"))
    sharded = dict(
        w_in=_w_in_unpadded(d_w_in),
        w_out=d_w_out,
        mla_wq_b=_cols_to_slots(_unpad_heads(d_wq, MLA_QK, MLA_QKP)),
        mla_wkv_b=_cols_to_slots(d_wkv),
        gla_wa2_f=_cols_to_slots(d_wa[0:GLA_RANK, 0:256]),
        gla_wa2_b=_cols_to_slots(d_wa[GLA_RANK:2 * GLA_RANK, 256:512]),
    )
    small = dict(
        norm_g=d_norm_g[0], ret_norm_g=d_ret_g[0], gla_ba_f=d_ba[0, :256], gla_ba_b=d_ba[0, 256:],
        gla_norm_g=d_gla_g[0], pool_w=d_pool_w.reshape(-1), pool_scale=d_pool_scale[0], mla_q_norm_g=d_qg[0],
        mla_kv_norm_g=d_kvg[0], mla_qk_norm_q=d_qng[0, :MLA_QK], mla_qk_norm_k=d_kng[0, :MLA_QK],
    )
    return dx, sharded, small, rode


SHARDED = ["w_in", "w_out", "mla_wq_b", "mla_wkv_b", "gla_wa2_f", "gla_wa2_b"]
WEIGHTS = ["norm_g", "w_in", "ret_norm_g", "gla_wa2_f", "gla_ba_f", "gla_wa2_b", "gla_ba_b", "gla_norm_g", "pool_w",
           "pool_scale", "mla_q_norm_g", "mla_wq_b", "mla_kv_norm_g", "mla_wkv_b", "mla_qk_norm_q", "mla_qk_norm_k",
           "w_out"]


SHARD_AXES = [1, 0, 0, 0, 0, 0]


def _layer_shards(p, l):
    return [jnp.swapaxes(p["w_in"], 1, 2)[l].astype(BF16), p["w_out"][l].astype(BF16), p["mla_wq_b"][l].astype(BF16),
            p["mla_wkv_b"][l].astype(BF16), p["gla_wa2_f"][l], p["gla_wa2_b"][l]]


def _step(p, where):
    x = p["x"][0]
    tabs = _rope_tables(x.shape[0])
    got0 = _gather_shards(_layer_shards(p, 0), SHARD_AXES, name="l0_gather_weights")
    w0 = _layer_weights(0, p, dict(zip(SHARDED, got0)))
    x1, sv0, got1 = _layer_fwd(0, x, w0, tabs, next_shards=_layer_shards(p, 1))
    w1 = _layer_weights(1, p, dict(zip(SHARDED, got1)))
    (dx, loss), sv1, _ = _layer_fwd(1, x1, w1, tabs, loss_target=p["loss_target"][0])

    big, big_axes = SHARDED[:2], SHARD_AXES[:2]

    def pair_sums(tag, tensors, axes, names):
        return [_pair_reduce(a, where, ax, out_dtype=BF16, name=f"{tag}_pair_reduce_{n}")
                for a, ax, n in zip(tensors, axes, names)]

    def joined(tag, pair, others, axes, names):
        return [_sum_join(a, b, where, ax, name=f"{tag}_sum_join_{n}")
                for a, b, ax, n in zip(pair, others, axes, names)]

    dx, sharded1, small1, _ = _layer_bwd(1, dx, w1, sv1, tabs)
    pair1 = pair_sums("l1", [sharded1[n] for n in big], big_axes, big)
    dx, sharded0, small0, others1 = _layer_bwd(0, dx, w0, sv0, tabs, riding_parts=pair1, where=where)
    grads1 = joined("l1", pair1, others1, big_axes, big)
    packed = jnp.concatenate([sh[n].reshape(N_CHIP, -1, 128) for sh in (sharded0, sharded1) for n in SHARDED[2:]],
                             axis=1)
    last, last_axes, last_names = [sharded0["w_in"], packed], [SHARD_AXES[0], 0], ["w_in", "rest"]
    pair0 = pair_sums("l0", last, last_axes, last_names)
    g_w_in0, rest = joined("l0", pair0, _chip_exchange(pair0, name="l0_chip_exchange"), last_axes, last_names)
    (g_w_out0,) = joined("l0", [sharded0["w_out"][0]], [sharded0["w_out"][1]], [SHARD_AXES[1]], ["w_out"])
    grads = {n: jnp.stack([g0, g1]) for n, g0, g1 in zip(big, (g_w_in0, g_w_out0), grads1)}
    off = 0
    pieces = {n: [] for n in SHARDED[2:]}
    for sh in (sharded0, sharded1):
        for n in SHARDED[2:]:
            rows = sh[n].shape[1] * sh[n].shape[2] // 128
            pieces[n].append(rest[off:off + rows].reshape(sh[n].shape[1:]))
            off += rows
    grads.update({n: jnp.stack(v) for n, v in pieces.items()})
    small = {n: [small0[n], small1[n]] for n, _ in SMALL}
    small["loss"] = loss
    return dx[None], grads, small


def kernel(x, norm_g, w_in, ret_norm_g, gla_wa2_f, gla_ba_f, gla_wa2_b, gla_ba_b, gla_norm_g, pool_w, pool_scale, mla_q_norm_g, mla_wq_b, mla_kv_norm_g, mla_wkv_b, mla_qk_norm_q, mla_qk_norm_k, w_out, loss_target, m_norm_g, m_w_in, m_ret_norm_g, m_gla_wa2_f, m_gla_ba_f, m_gla_wa2_b, m_gla_ba_b, m_gla_norm_g, m_pool_w, m_pool_scale, m_mla_q_norm_g, m_mla_wq_b, m_mla_kv_norm_g, m_mla_wkv_b, m_mla_qk_norm_q, m_mla_qk_norm_k, m_w_out, v_norm_g, v_w_in, v_ret_norm_g, v_gla_wa2_f, v_gla_ba_f, v_gla_wa2_b, v_gla_ba_b, v_gla_norm_g, v_pool_w, v_pool_scale, v_mla_q_norm_g, v_mla_wq_b, v_mla_kv_norm_g, v_mla_wkv_b, v_mla_qk_norm_q, v_mla_qk_norm_k, v_w_out):
    p = dict(x=x, norm_g=norm_g, w_in=w_in, ret_norm_g=ret_norm_g, gla_wa2_f=gla_wa2_f, gla_ba_f=gla_ba_f,
             gla_wa2_b=gla_wa2_b, gla_ba_b=gla_ba_b, gla_norm_g=gla_norm_g, pool_w=pool_w, pool_scale=pool_scale,
             mla_q_norm_g=mla_q_norm_g, mla_wq_b=mla_wq_b, mla_kv_norm_g=mla_kv_norm_g, mla_wkv_b=mla_wkv_b,
             mla_qk_norm_q=mla_qk_norm_q, mla_qk_norm_k=mla_qk_norm_k, w_out=w_out, loss_target=loss_target)
    moments = dict(
        m=dict(norm_g=m_norm_g, w_in=m_w_in, ret_norm_g=m_ret_norm_g, gla_wa2_f=m_gla_wa2_f, gla_ba_f=m_gla_ba_f,
               gla_wa2_b=m_gla_wa2_b, gla_ba_b=m_gla_ba_b, gla_norm_g=m_gla_norm_g, pool_w=m_pool_w,
               pool_scale=m_pool_scale, mla_q_norm_g=m_mla_q_norm_g, mla_wq_b=m_mla_wq_b,
               mla_kv_norm_g=m_mla_kv_norm_g, mla_wkv_b=m_mla_wkv_b, mla_qk_norm_q=m_mla_qk_norm_q,
               mla_qk_norm_k=m_mla_qk_norm_k, w_out=m_w_out),
        v=dict(norm_g=v_norm_g, w_in=v_w_in, ret_norm_g=v_ret_norm_g, gla_wa2_f=v_gla_wa2_f, gla_ba_f=v_gla_ba_f,
               gla_wa2_b=v_gla_wa2_b, gla_ba_b=v_gla_ba_b, gla_norm_g=v_gla_norm_g, pool_w=v_pool_w,
               pool_scale=v_pool_scale, mla_q_norm_g=v_mla_q_norm_g, mla_wq_b=v_mla_wq_b,
               mla_kv_norm_g=v_mla_kv_norm_g, mla_wkv_b=v_mla_wkv_b, mla_qk_norm_q=v_mla_qk_norm_q,
               mla_qk_norm_k=v_mla_qk_norm_k, w_out=v_w_out))

    where = jnp.stack([lax.axis_index("c"), 2 * lax.axis_index("x") + lax.axis_index("y")]).astype(jnp.int32)
    grad_x, grads, small = _step(p, where)

    slots = _gather_all(_pack_small(small), name="gather_small")
    total = _unpack_small(_sum_slots(slots, name="sum_small"))
    for n, _ in SMALL:
        grads[n] = total[n].reshape(p[n].shape)
    loss = total["loss"]

    delta, new_m, new_v = {}, {}, {}
    for n in WEIGHTS:
        turn = (lambda a: jnp.swapaxes(a, 1, 2)) if n == "w_in" else (lambda a: a)
        outs = _adamw(turn(p[n]), grads[n], turn(moments["m"][n]), turn(moments["v"][n]), name=f"adamw_{n}")
        grads[n] = turn(grads[n])
        delta[n], new_m[n], new_v[n] = (turn(o) for o in outs)
    return (loss, grad_x, *[grads[n] for n in WEIGHTS], *[delta[n] for n in WEIGHTS],
            *[new_m[n] for n in WEIGHTS], *[new_v[n] for n in WEIGHTS])
```

```python
import jax
import jax.numpy as jnp
from jax import lax
from jax.experimental import pallas as pl
from jax.experimental.pallas import tpu as pltpu

F32 = jnp.float32
BF16 = jnp.bfloat16
MESH = pl.DeviceIdType.MESH

EPS = 1e-6
ROPE_THETA = 10000.0
DEPTH = 2
N_DEV = 8
N_CHIP = 4

GROUP_W = 512
RET_HEADS = 4
RET_HD = 128
RET_CHUNK = 256
GLA_HEADS = 4
GLA_DK = 64
GLA_DV = 128
GLA_RANK = 16
GLA_TAU = 16.0
GLA_CHUNK = 64
POOL_GROUPS = 4
POOL_GW = 128
POOL_HALO = 8
POOL_TILE = 256
MLA_HEADS = 4
MLA_NOPE = 128
MLA_ROPE = 64
MLA_QK = MLA_NOPE + MLA_ROPE
MLA_QKP = 256
MLA_V = 128
MLA_Q_RANK = 512
MLA_KV_RANK = 256
MLA_SCALE = MLA_QK ** -0.5

ADAM_LR = 0.001
ADAM_B1 = 0.9
ADAM_B2 = 0.999
ADAM_EPS = 1e-08
ADAM_WD = 0.01
ADAM_STEP = 10

VMEM_LIMIT = 56 * 1024 * 1024
ROW_TILE = 512
GROUP_ROW_TILE = 1024

SEG = {
    "rq": (0, 512, 0, 512), "rk": (512, 512, 512, 512), "rv": (1024, 512, 1024, 512), "rg": (1536, 512, 1536, 512),
    "gv": (2048, 512, 2560, 512), "gg": (2560, 512, 3072, 512),
    "pv": (3072, 512, 3616, 512), "pg": (3584, 512, 4128, 512),
    "mq": (4096, 512, 4640, 512), "mg": (4608, 512, 5472, 512),
    "gq": (5120, 256, 2048, 256), "gk": (5376, 256, 2304, 256), "mkv": (5632, 256, 5152, 256),
    "ga": (5888, 128, 3584, 32), "mkr": (6016, 128, 5408, 64),
}
SEG_ORDER = ["rq", "rk", "rv", "rg", "gv", "gg", "pv", "pg", "mq", "mg", "gq", "gk", "mkv", "ga", "mkr"]
IN_COLS = 5984
IN_PAD = 6144
ORIG_ORDER = ["rq", "rk", "rv", "rg", "gq", "gk", "gv", "gg", "ga", "pv", "pg", "mq", "mkv", "mkr", "mg"]


def _cparams(*sem):
    return pltpu.CompilerParams(dimension_semantics=tuple(sem), vmem_limit_bytes=VMEM_LIMIT)


def _bf(v):
    return v.astype(BF16)


def _dot(a, b, ca=1, cb=0):
    return lax.dot_general(_bf(a), _bf(b), (((ca,), (cb,)), ((), ())), preferred_element_type=F32)


def _sigmoid(x):
    return 1.0 / (1.0 + jnp.exp(-x))


def _silu_parts(g):
    sg = _sigmoid(g)
    return g * sg, sg * (1.0 + g * (1.0 - sg))


class _Rider:
    def __init__(self, ins, outs, sems, start, finish, aliases=None):
        self.ins, self.outs, self.sems, self.start, self.finish = list(ins), list(outs), list(sems), start, finish
        self.aliases = dict(aliases or {})


def _ride(body, rider, n_in, n_out, grid):
    if rider is None:
        return body
    ri, ro, rs = len(rider.ins), len(rider.outs), len(rider.sems)

    def wrapped(*refs):
        ins, refs = refs[:n_in], refs[n_in:]
        rin, refs = refs[:ri], refs[ri:]
        outs, refs = refs[:n_out], refs[n_out:]
        rout, refs = refs[:ro], refs[ro:]
        scratch, sems = refs[:len(refs) - rs], refs[len(refs) - rs:]
        first = pl.program_id(0) == 0
        last = pl.program_id(0) == grid[0] - 1
        for ax in range(1, len(grid)):
            first = jnp.logical_and(first, pl.program_id(ax) == 0)
            last = jnp.logical_and(last, pl.program_id(ax) == grid[ax] - 1)

        @pl.when(first)
        def _():
            rider.start(rin, rout, sems)

        body(*ins, *outs, *scratch)

        @pl.when(last)
        def _():
            rider.finish(rin, rout, sems)

    return wrapped


def _ride_call(body, rider, *, name, grid, in_specs, out_specs, out_shape, scratch_shapes, args, sem):
    n_in, n_out = len(in_specs), len(out_specs)
    if rider is None:
        return pl.pallas_call(body, name=name, grid=grid, in_specs=in_specs, out_specs=out_specs, out_shape=out_shape,
                              scratch_shapes=scratch_shapes, compiler_params=_cparams(*sem))(*args), []
    outs = pl.pallas_call(
        _ride(body, rider, n_in, n_out, grid), name=name, grid=grid,
        in_specs=list(in_specs) + [ANY] * len(rider.ins), out_specs=list(out_specs) + [ANY] * len(rider.outs),
        out_shape=list(out_shape) + rider.outs, scratch_shapes=list(scratch_shapes) + rider.sems,
        input_output_aliases={n_in + i: n_out + o for i, o in rider.aliases.items()},
        compiler_params=_cparams(*(["arbitrary"] * len(grid))),
    )(*args, *rider.ins)
    return outs[:n_out], outs[n_out:]


def _matmul(a, b, *, ta=False, tb=False, out_dtype=F32, tm=512, tn=1024, tk=None, add=None, n_outer=True, rider=None,
            name):
    m, kdim = (a.shape[1], a.shape[0]) if ta else a.shape
    n = b.shape[0] if tb else b.shape[1]
    tm, tn = min(tm, m), min(tn, n)
    tk = kdim if tk is None else min(tk, kdim)
    assert m % tm == 0 and n % tn == 0 and kdim % tk == 0
    nk = kdim // tk
    ca, cb = (0 if ta else 1), (1 if tb else 0)

    def body(*refs):
        if add is None:
            a_ref, b_ref, o_ref = refs[:3]
            add_ref = None
        else:
            a_ref, b_ref, add_ref, o_ref = refs[:4]
        p = _dot(a_ref[...], b_ref[...], ca, cb)

        def finish(r):
            if add_ref is not None:
                r = r + add_ref[...]
            o_ref[...] = r.astype(out_dtype)

        if nk == 1:
            finish(p)
        else:
            acc = refs[-1]
            k = pl.program_id(2)

            @pl.when(k == 0)
            def _():
                acc[...] = p

            @pl.when(k > 0)
            def _():
                acc[...] += p

            @pl.when(k == nk - 1)
            def _():
                finish(acc[...])

    def ij(g0, g1):
        return (g1, g0) if n_outer else (g0, g1)

    a_spec = (pl.BlockSpec((tk, tm), lambda g0, g1, k: (k, ij(g0, g1)[0])) if ta
              else pl.BlockSpec((tm, tk), lambda g0, g1, k: (ij(g0, g1)[0], k)))
    b_spec = (pl.BlockSpec((tn, tk), lambda g0, g1, k: (ij(g0, g1)[1], k)) if tb
              else pl.BlockSpec((tk, tn), lambda g0, g1, k: (k, ij(g0, g1)[1])))
    o_spec = pl.BlockSpec((tm, tn), lambda g0, g1, k: ij(g0, g1))
    in_specs = [a_spec, b_spec] + ([o_spec] if add is not None else [])
    args = (a, b) + ((add,) if add is not None else ())
    grid = (n // tn, m // tm, nk) if n_outer else (m // tm, n // tn, nk)
    (out,), rode = _ride_call(
        body, rider, name=name, grid=grid, in_specs=in_specs, out_specs=[o_spec],
        out_shape=[jax.ShapeDtypeStruct((m, n), out_dtype)],
        scratch_shapes=[] if nk == 1 else [pltpu.VMEM((tm, tn), F32)], args=args,
        sem=("parallel", "parallel", "arbitrary"))
    return out if rider is None else (out, rode)


def _rmsnorm_fwd(x, g, *, name, tm=ROW_TILE):
    s, d = x.shape
    tm = min(tm, s)

    def body(x_ref, g_ref, h_ref):
        xv = x_ref[...]
        r = lax.rsqrt(jnp.mean(xv * xv, axis=-1, keepdims=True) + EPS)
        h_ref[...] = _bf(xv * r * g_ref[...])

    return pl.pallas_call(
        body, name=name, grid=(s // tm,),
        in_specs=[pl.BlockSpec((tm, d), lambda i: (i, 0)), pl.BlockSpec((1, d), lambda i: (0, 0))],
        out_specs=pl.BlockSpec((tm, d), lambda i: (i, 0)),
        out_shape=jax.ShapeDtypeStruct((s, d), BF16),
        compiler_params=_cparams("parallel"),
    )(x, g)


def _rmsnorm_bwd(x, dh, g, dres, *, name, tm=ROW_TILE):
    s, d = x.shape
    tm = min(tm, s)

    def body(x_ref, dh_ref, g_ref, dres_ref, dx_ref, dg_ref):
        i = pl.program_id(0)
        xv = x_ref[...]
        r = lax.rsqrt(jnp.mean(xv * xv, axis=-1, keepdims=True) + EPS)
        xn = xv * r
        dv = dh_ref[...]
        part = jnp.sum(dv * xn, axis=0, keepdims=True)

        @pl.when(i == 0)
        def _():
            dg_ref[...] = part

        @pl.when(i > 0)
        def _():
            dg_ref[...] += part

        dxn = dv * g_ref[...]
        dx_ref[...] = dres_ref[...] + r * (dxn - xn * jnp.mean(dxn * xn, axis=-1, keepdims=True))

    row = pl.BlockSpec((tm, d), lambda i: (i, 0))
    vec = pl.BlockSpec((1, d), lambda i: (0, 0))
    return pl.pallas_call(
        body, name=name, grid=(s // tm,), in_specs=[row, row, vec, row], out_specs=[row, vec],
        out_shape=[jax.ShapeDtypeStruct((s, d), F32), jax.ShapeDtypeStruct((1, d), F32)],
        compiler_params=_cparams("arbitrary"),
    )(x, dh, g, dres)


def _out_proj_loss(y, w, x, target, *, name, tm=512, tn=1024):
    m, kdim = y.shape
    n = w.shape[1]
    tm, tn = min(tm, m), min(tn, n)

    def body(y_ref, w_ref, x_ref, t_ref, dx_ref, l_ref):
        first = jnp.logical_and(pl.program_id(0) == 0, pl.program_id(1) == 0)
        e = _dot(y_ref[...], w_ref[...]) + x_ref[...] - t_ref[...]
        dx_ref[...] = e * (1.0 / n)
        part = (0.5 / n) * jnp.sum(jnp.sum(e * e, axis=-1, keepdims=True), axis=0, keepdims=True)

        @pl.when(first)
        def _():
            l_ref[...] = part

        @pl.when(jnp.logical_not(first))
        def _():
            l_ref[...] += part

    tile = pl.BlockSpec((tm, tn), lambda j, i: (i, j))
    return pl.pallas_call(
        body, name=name, grid=(n // tn, m // tm),
        in_specs=[pl.BlockSpec((tm, kdim), lambda j, i: (i, 0)), pl.BlockSpec((kdim, tn), lambda j, i: (0, j)),
                  tile, tile],
        out_specs=[tile, pl.BlockSpec((1, 1), lambda j, i: (0, 0))],
        out_shape=[jax.ShapeDtypeStruct((m, n), F32), jax.ShapeDtypeStruct((1, 1), F32)],
        compiler_params=_cparams("arbitrary", "arbitrary"),
    )(y, w, x, target)


def _rope_tables(s):
    pos = jnp.arange(s, dtype=F32)[:, None]
    inv_r = 1.0 / (ROPE_THETA ** (jnp.arange(0, RET_HD, 2, dtype=F32) / RET_HD))
    ang = pos * inv_r[None, :]
    ret_cos = jnp.concatenate([jnp.cos(ang), jnp.cos(ang)], axis=1)
    ret_sin = jnp.concatenate([-jnp.sin(ang), jnp.sin(ang)], axis=1)
    inv_m = 1.0 / (ROPE_THETA ** (jnp.arange(0, MLA_ROPE, 2, dtype=F32) / MLA_ROPE))
    am = pos * inv_m[None, :]
    z32, z64 = jnp.zeros((s, 32), F32), jnp.zeros((s, 64), F32)
    mla_cos = jnp.concatenate([jnp.cos(am), jnp.cos(am), z64], axis=1)
    mla_sp = jnp.concatenate([z32, jnp.sin(am), z64], axis=1)
    mla_sn = jnp.concatenate([-jnp.sin(am), z32, z64], axis=1)
    return ret_cos, ret_sin, mla_cos, mla_sp, mla_sn


def _rope128(x, c, sg):
    return x * c + pltpu.roll(x, 64, 1) * sg


def _unrope128(d, c, sg):
    return d * c + pltpu.roll(d * sg, 64, 1)


def _rope64(t, c, sp, sn):
    return t * c + pltpu.roll(t, 96, 1) * sn + pltpu.roll(t, 32, 1) * sp


def _unrope64(d, c, sp, sn):
    return d * c + pltpu.roll(d * sn, 32, 1) + pltpu.roll(d * sp, 96, 1)


def _ret_pre(z, cos, sin, *, name, tm=GROUP_ROW_TILE):
    s = z.shape[0]
    tm = min(tm, s)
    scale = RET_HD ** -0.5

    def body(q_ref, k_ref, c_ref, s_ref, qo_ref, ko_ref):
        c, sg = c_ref[...], s_ref[...]
        for h in range(RET_HEADS):
            sl = slice(h * RET_HD, (h + 1) * RET_HD)
            qo_ref[:, sl] = _rope128(q_ref[:, sl], c, sg)
            ko_ref[:, sl] = _rope128(k_ref[:, sl], c, sg) * scale

    seg = lambda j: pl.BlockSpec((tm, GROUP_W), lambda i: (i, j))
    tab = pl.BlockSpec((tm, RET_HD), lambda i: (i, 0))
    return pl.pallas_call(
        body, name=name, grid=(s // tm,), in_specs=[seg(0), seg(1), tab, tab],
        out_specs=[seg(0), seg(0)],
        out_shape=[jax.ShapeDtypeStruct((s, GROUP_W), F32)] * 2,
        compiler_params=_cparams("parallel"),
    )(z, z, cos, sin)


def _ret_pre_bwd(dqr, dkr, cos, sin, into, *, name, tm=GROUP_ROW_TILE):
    s = dqr[0].shape[0]
    tm = min(tm, s)
    scale = RET_HD ** -0.5

    def body(dq0_ref, dq1_ref, dk0_ref, dk1_ref, c_ref, s_ref, _, o_ref):
        c, sg = c_ref[...], s_ref[...]
        for h in range(RET_HEADS):
            sl = slice(h * RET_HD, (h + 1) * RET_HD)
            ksl = slice(GROUP_W + h * RET_HD, GROUP_W + (h + 1) * RET_HD)
            o_ref[:, sl] = _bf(_unrope128(dq0_ref[:, sl] + dq1_ref[:, sl], c, sg))
            o_ref[:, ksl] = _bf(_unrope128(dk0_ref[:, sl] + dk1_ref[:, sl], c, sg) * scale)

    row = pl.BlockSpec((tm, GROUP_W), lambda i: (i, 0))
    tab = pl.BlockSpec((tm, RET_HD), lambda i: (i, 0))
    out_shape, out_spec, more_specs, more_args = _landing(into, tm, 2 * GROUP_W)
    return pl.pallas_call(
        body, name=name, grid=(s // tm,), in_specs=[row, row, row, row, tab, tab] + more_specs, out_specs=out_spec,
        out_shape=out_shape, input_output_aliases={6: 0},
        compiler_params=_cparams("parallel"),
    )(dqr[0], dqr[1], dkr[0], dkr[1], cos, sin, *more_args)


def _bla(a, b, c, lg, cols, *, name):
    s = a.shape[0]
    ch = min(RET_CHUNK, s)
    n = s // ch
    hd = RET_HD

    def body(lg_ref, a0, b0, c0, a1, b1, c1, o0, o1, st):
        t = pl.program_id(0)

        @pl.when(t == 0)
        def _():
            st[...] = jnp.zeros_like(st)

        ii = lax.broadcasted_iota(jnp.int32, (ch, ch), 0)
        jj = lax.broadcasted_iota(jnp.int32, (ch, ch), 1)
        idx = lax.broadcasted_iota(jnp.int32, (ch, 1), 0).astype(F32)
        for d, (a_ref, b_ref, c_ref, o_ref) in enumerate(((a0, b0, c0, o0), (a1, b1, c1, o1))):
            diff = ((ii - jj) if d == 0 else (jj - ii)).astype(F32)
            keep = diff >= 0
            dpos = jnp.maximum(diff, 0.0)
            pq = (idx + 1.0) if d == 0 else (ch - idx)
            pk = (ch - 1.0 - idx) if d == 0 else idx
            for h in range(RET_HEADS):
                g = lg_ref[d, h]
                sl = slice(h * hd, (h + 1) * hd)
                av, bv, cv = a_ref[:, sl], b_ref[:, sl], c_ref[:, sl]
                sc = _dot(av, bv, 1, 1) * jnp.where(keep, jnp.exp(dpos * g), 0.0)
                stv = st[d, h]
                o_ref[:, sl] = _dot(sc, cv) + _dot(av * jnp.exp(pq * g), stv)
                st[d, h] = jnp.exp(ch * g) * stv + _dot(bv * jnp.exp(pk * g), cv, 0, 0)

    fwd = lambda j: pl.BlockSpec((ch, GROUP_W), lambda t: (t, j))
    bwd = lambda j: pl.BlockSpec((ch, GROUP_W), lambda t: (n - 1 - t, j))
    return pl.pallas_call(
        body, name=name, grid=(n,),
        in_specs=[pl.BlockSpec(memory_space=pltpu.SMEM), fwd(cols[0]), fwd(cols[1]), fwd(cols[2]),
                  bwd(cols[0]), bwd(cols[1]), bwd(cols[2])],
        out_specs=[fwd(0), bwd(0)],
        out_shape=[jax.ShapeDtypeStruct((s, GROUP_W), F32)] * 2,
        scratch_shapes=[pltpu.VMEM((2, RET_HEADS, hd, hd), F32)],
        compiler_params=_cparams("arbitrary"),
    )(lg, a, b, c, a, b, c)


def _post(os_, zg, gcol, g, *, norm, name, tm=GROUP_ROW_TILE):
    s = zg.shape[0]
    tm = min(tm, s)
    nd = len(os_)

    def body(*refs):
        o_refs, (gt_ref, g_ref, y_ref) = refs[:nd], refs[nd:]
        silu, _ = _silu_parts(gt_ref[...])
        for h in range(4):
            sl = slice(h * 128, (h + 1) * 128)
            o = o_refs[0][:, sl]
            for k in range(1, nd):
                o = o + o_refs[k][:, sl]
            if norm:
                r = lax.rsqrt(jnp.mean(o * o, axis=-1, keepdims=True) + EPS)
                o = o * r * g_ref[:, sl]
            y_ref[:, sl] = _bf(silu[:, sl] * o)

    row = pl.BlockSpec((tm, GROUP_W), lambda i: (i, 0))
    return pl.pallas_call(
        body, name=name, grid=(s // tm,),
        in_specs=[row] * nd + [pl.BlockSpec((tm, GROUP_W), lambda i: (i, gcol)),
                               pl.BlockSpec((1, GROUP_W), lambda i: (0, 0))],
        out_specs=row,
        out_shape=jax.ShapeDtypeStruct((s, GROUP_W), BF16),
        compiler_params=_cparams("parallel"),
    )(*os_, zg, g)


def _post_bwd(dy, ycol, os_, zg, gcol, g, into, *, norm, name, tm=GROUP_ROW_TILE):
    s = zg.shape[0]
    tm = min(tm, s)
    nd = len(os_)

    def body(*refs):
        dy_ref, o_refs = refs[0], refs[1:1 + nd]
        gt_ref, g_ref, _, dgt_ref, do_ref, dg_ref = refs[1 + nd:]
        i = pl.program_id(0)
        silu, dsilu = _silu_parts(gt_ref[...])
        dyv = dy_ref[...]
        parts = []
        for h in range(4):
            sl = slice(h * 128, (h + 1) * 128)
            o = o_refs[0][:, sl]
            for k in range(1, nd):
                o = o + o_refs[k][:, sl]
            dn = dyv[:, sl] * silu[:, sl]
            if norm:
                r = lax.rsqrt(jnp.mean(o * o, axis=-1, keepdims=True) + EPS)
                xn = o * r
                gh = g_ref[:, sl]
                dgt_ref[:, sl] = _bf(dyv[:, sl] * (xn * gh) * dsilu[:, sl])
                parts.append(jnp.sum(dn * xn, axis=0, keepdims=True))
                dxn = dn * gh
                do_ref[:, sl] = r * (dxn - xn * jnp.mean(dxn * xn, axis=-1, keepdims=True))
            else:
                dgt_ref[:, sl] = _bf(dyv[:, sl] * o * dsilu[:, sl])
                parts.append(jnp.zeros((1, 128), F32))
                do_ref[:, sl] = dn
        part = jnp.concatenate(parts, axis=1)

        @pl.when(i == 0)
        def _():
            dg_ref[...] = part

        @pl.when(i > 0)
        def _():
            dg_ref[...] += part

    row = pl.BlockSpec((tm, GROUP_W), lambda i: (i, 0))
    vec = pl.BlockSpec((1, GROUP_W), lambda i: (0, 0))
    dgt_shape, dgt_spec, more_specs, more_args = _landing(into, tm, GROUP_W)
    n_in = nd + 3
    return pl.pallas_call(
        body, name=name, grid=(s // tm,),
        in_specs=[pl.BlockSpec((tm, GROUP_W), lambda i: (i, ycol))] + [row] * nd
        + [pl.BlockSpec((tm, GROUP_W), lambda i: (i, gcol)), vec] + more_specs,
        out_specs=[dgt_spec, row, vec],
        out_shape=[dgt_shape, jax.ShapeDtypeStruct((s, GROUP_W), F32), jax.ShapeDtypeStruct((1, GROUP_W), F32)],
        input_output_aliases={n_in: 0},
        compiler_params=_cparams("arbitrary"),
    )(dy, *os_, zg, g, *more_args)


def _ret_log_gamma(swap):
    gf = 1.0 - 2.0 ** (-5.0 - jnp.arange(RET_HEADS, dtype=F32))
    lf, lb = jnp.log(gf), jnp.log(gf[::-1])
    return jnp.stack([lb, lf] if swap else [lf, lb])


def _log_sigmoid(x):
    return jnp.minimum(x, 0.0) - jnp.log(1.0 + jnp.exp(-jnp.abs(x)))


def _gla_gate(z, wa, ba, *, name, tm=GROUP_ROW_TILE):
    s = z.shape[0]
    tm = min(tm, s)
    col = SEG["ga"][0] // 128

    def body(ga_ref, wa_ref, ba_ref, la_ref):
        pre = _dot(ga_ref[...], wa_ref[...]) + ba_ref[...]
        la_ref[...] = _log_sigmoid(pre) / GLA_TAU

    return pl.pallas_call(
        body, name=name, grid=(s // tm,),
        in_specs=[pl.BlockSpec((tm, 128), lambda i: (i, col)), pl.BlockSpec((128, 512), lambda i: (0, 0)),
                  pl.BlockSpec((1, 512), lambda i: (0, 0))],
        out_specs=pl.BlockSpec((tm, 512), lambda i: (i, 0)),
        out_shape=jax.ShapeDtypeStruct((s, 512), F32),
        compiler_params=_cparams("parallel"),
    )(z, wa, ba)


def _gla_gate_bwd(dla, z, wa, ba, into, *, name, tm=GROUP_ROW_TILE):
    s = z.shape[0]
    tm = min(tm, s)
    col = SEG["ga"][0] // 128

    def body(dla0_ref, dla1_ref, ga_ref, wa_ref, ba_ref, _, dga_ref, dwa_ref, dba_ref):
        i = pl.program_id(0)
        gav = ga_ref[...]
        pre = _dot(gav, wa_ref[...]) + ba_ref[...]
        dla_v = jnp.concatenate([dla0_ref[...], dla1_ref[...]], axis=1)
        dpre = dla_v * (1.0 - _sigmoid(pre)) * (1.0 / GLA_TAU)
        dga_ref[...] = _bf(_dot(dpre, wa_ref[...], 1, 1))
        pw = _dot(gav, dpre, 0, 0)
        pb = jnp.sum(dpre, axis=0, keepdims=True)

        @pl.when(i == 0)
        def _():
            dwa_ref[...] = pw
            dba_ref[...] = pb

        @pl.when(i > 0)
        def _():
            dwa_ref[...] += pw
            dba_ref[...] += pb

    dga_shape, dga_spec, more_specs, more_args = _landing(into, tm, 128)
    return pl.pallas_call(
        body, name=name, grid=(s // tm,),
        in_specs=[pl.BlockSpec((tm, 256), lambda i: (i, 0)), pl.BlockSpec((tm, 256), lambda i: (i, 0)),
                  pl.BlockSpec((tm, 128), lambda i: (i, col)),
                  pl.BlockSpec((128, 512), lambda i: (0, 0)), pl.BlockSpec((1, 512), lambda i: (0, 0))] + more_specs,
        out_specs=[dga_spec, pl.BlockSpec((128, 512), lambda i: (0, 0)), pl.BlockSpec((1, 512), lambda i: (0, 0))],
        out_shape=[dga_shape, jax.ShapeDtypeStruct((128, 512), F32), jax.ShapeDtypeStruct((1, 512), F32)],
        input_output_aliases={5: 0},
        compiler_params=_cparams("arbitrary"),
    )(dla[0], dla[1], z, wa, ba, *more_args)


def _gla_masks(ch):
    ii = lax.broadcasted_iota(jnp.int32, (ch, ch), 0)
    tt = lax.broadcasted_iota(jnp.int32, (ch, ch), 1)
    return jnp.where(tt <= ii, 1.0, 0.0), jnp.where(tt >= ii, 1.0, 0.0)


def _running_sum(x, up):
    n = x.shape[0]
    rows = lax.broadcasted_iota(jnp.int32, x.shape, 0)
    k = 1
    while k < n:
        if up:
            x = x + jnp.where(rows < n - k, pltpu.roll(x, n - k, 0), 0.0)
        else:
            x = x + jnp.where(rows >= k, pltpu.roll(x, k, 0), 0.0)
        k *= 2
    return x


def _gla_chunk(d, tmat, qv, kv, lav, ch):
    c = _running_sum(lav, up=(d == 1))
    big_l = c[ch - 1:ch, :] if d == 0 else c[0:1, :]
    qt = qv * (GLA_DK ** -0.5) * jnp.exp(c)
    kt = kv * jnp.exp(-c)
    kh = kv * jnp.exp(big_l - c)
    return c, big_l, qt, kt, kh


def _gla_fwd(qh, kh_, z, la, *, name, rider=None):
    s = z.shape[0]
    ch = min(GLA_CHUNK, s)
    n = s // ch
    vcol = SEG["gv"][0] // GROUP_W

    def body(q0, k0, v0, la0, q1, k1, v1, la1, o0, o1, zs0, zs1, st):
        t = pl.program_id(0)

        @pl.when(t == 0)
        def _():
            st[...] = jnp.zeros_like(st)

        masks = _gla_masks(ch)
        for d, (q_ref, k_ref, v_ref, la_ref, o_ref, zs_ref) in enumerate(
                ((q0, k0, v0, la0, o0, zs0), (q1, k1, v1, la1, o1, zs1))):
            for h in range(GLA_HEADS):
                c, big_l, qt, kt, kh = _gla_chunk(d, masks[d], q_ref[h], k_ref[h], la_ref[0, h], ch)
                vv = v_ref[:, h * GLA_DV:(h + 1) * GLA_DV]
                p = _dot(qt, kt, 1, 1) * masks[d]
                zst = st[d, h]
                o_ref[:, h * GLA_DV:(h + 1) * GLA_DV] = _dot(p, vv) + _dot(qt, zst, 1, 1)
                zs_ref[h, 0] = zst
                st[d, h] = zst * jnp.exp(big_l) + _dot(vv, kh, 0, 0)

    cidx = (lambda t: t), (lambda t: n - 1 - t)
    hs = lambda d: pl.BlockSpec((GLA_HEADS, ch, GLA_DK), lambda t: (0, cidx[d](t), 0))
    vs = lambda d: pl.BlockSpec((ch, GROUP_W), lambda t: (cidx[d](t), vcol))
    las = lambda d: pl.BlockSpec((1, GLA_HEADS, ch, GLA_DK), lambda t: (d, 0, cidx[d](t), 0))
    os_ = lambda d: pl.BlockSpec((ch, GROUP_W), lambda t: (cidx[d](t), 0))
    zss = lambda d: pl.BlockSpec((GLA_HEADS, 1, GLA_DV, GLA_DK), lambda t: (0, cidx[d](t), 0, 0))
    (o0, o1, zs0, zs1), rode = _ride_call(
        body, rider, name=name, grid=(n,),
        in_specs=[hs(0), hs(0), vs(0), las(0), hs(1), hs(1), vs(1), las(1)],
        out_specs=[os_(0), os_(1), zss(0), zss(1)],
        out_shape=[jax.ShapeDtypeStruct((s, GROUP_W), F32)] * 2
        + [jax.ShapeDtypeStruct((GLA_HEADS, n, GLA_DV, GLA_DK), F32)] * 2,
        scratch_shapes=[pltpu.VMEM((2, GLA_HEADS, GLA_DV, GLA_DK), F32)],
        args=(qh, kh_, z, la, qh, kh_, z, la), sem=("arbitrary",))
    return ((o0, o1), (zs0, zs1)) if rider is None else ((o0, o1), (zs0, zs1), rode)


def _gla_bwd(qh, kh_, z, la, do, zs, *, name, rider=None):
    s = z.shape[0]
    ch = min(GLA_CHUNK, s)
    n = s // ch
    vcol = SEG["gv"][0] // GROUP_W

    def body(q0, k0, v0, la0, do0, zs0, q1, k1, v1, la1, do1, zs1,
             dq0, dk0, dla0, dv0, dq1, dk1, dla1, dv1, gz):
        t = pl.program_id(0)

        @pl.when(t == 0)
        def _():
            gz[...] = jnp.zeros_like(gz)

        masks = _gla_masks(ch)
        rows = lax.broadcasted_iota(jnp.int32, (ch, 1), 0)
        for d, (q_ref, k_ref, v_ref, la_ref, do_ref, zs_ref, dq_ref, dk_ref, dla_ref, dv_ref) in enumerate(
                ((q0, k0, v0, la0, do0, zs0, dq0, dk0, dla0, dv0), (q1, k1, v1, la1, do1, zs1, dq1, dk1, dla1, dv1))):
            tmat = masks[d]
            end = ch - 1 if d == 0 else 0
            for h in range(GLA_HEADS):
                ksl = slice(h * GLA_DK, (h + 1) * GLA_DK)
                c, big_l, qt, kt, kh = _gla_chunk(d, tmat, q_ref[h], k_ref[h], la_ref[0, h], ch)
                vsl = slice(h * GLA_DV, (h + 1) * GLA_DV)
                vv, dov, zst, gzv = v_ref[:, vsl], do_ref[:, vsl], zs_ref[h, 0], gz[d, h]
                p = _dot(qt, kt, 1, 1) * tmat
                dp = _dot(dov, vv, 1, 1) * tmat
                dqt = _dot(dp, kt) + _dot(dov, zst)
                dkt = _dot(dp, qt, 0, 0)
                dkh = _dot(vv, gzv)
                dv_ref[:, vsl] = _dot(p, dov, 0, 0) + _dot(kh, gzv, 1, 1)
                dq_ref[:, ksl] = dqt * jnp.exp(c) * (GLA_DK ** -0.5)
                dk_ref[:, ksl] = dkt * jnp.exp(-c) + dkh * jnp.exp(big_l - c)
                e_l = jnp.exp(big_l)
                d_l = jnp.sum(dkh * kh, axis=0, keepdims=True) + e_l * jnp.sum(zst * gzv, axis=0, keepdims=True)
                dc = dqt * qt - dkt * kt - dkh * kh + jnp.where(rows == end, d_l, 0.0)
                dla_ref[:, ksl] = _running_sum(dc, up=(d == 0))
                gz[d, h] = gzv * e_l + _dot(dov, qt, 0, 0)

    cidx = (lambda t: n - 1 - t), (lambda t: t)
    hs = lambda d: pl.BlockSpec((GLA_HEADS, ch, GLA_DK), lambda t: (0, cidx[d](t), 0))
    vs = lambda d: pl.BlockSpec((ch, GROUP_W), lambda t: (cidx[d](t), vcol))
    las = lambda d: pl.BlockSpec((1, GLA_HEADS, ch, GLA_DK), lambda t: (d, 0, cidx[d](t), 0))
    row = lambda d: pl.BlockSpec((ch, GROUP_W), lambda t: (cidx[d](t), 0))
    zss = lambda d: pl.BlockSpec((GLA_HEADS, 1, GLA_DV, GLA_DK), lambda t: (0, cidx[d](t), 0, 0))
    kw = GLA_HEADS * GLA_DK
    ks = lambda d: pl.BlockSpec((ch, kw), lambda t: (cidx[d](t), 0))
    hshape = jax.ShapeDtypeStruct((s, kw), F32)
    wide = jax.ShapeDtypeStruct((s, GROUP_W), F32)
    outs, rode = _ride_call(
        body, rider, name=name, grid=(n,),
        in_specs=[hs(0), hs(0), vs(0), las(0), row(0), zss(0), hs(1), hs(1), vs(1), las(1), row(1), zss(1)],
        out_specs=[ks(0), ks(0), ks(0), row(0), ks(1), ks(1), ks(1), row(1)],
        out_shape=[hshape, hshape, hshape, wide, hshape, hshape, hshape, wide],
        scratch_shapes=[pltpu.VMEM((2, GLA_HEADS, GLA_DV, GLA_DK), F32)],
        args=(qh, kh_, z, la, do, zs[0], qh, kh_, z, la, do, zs[1]), sem=("arbitrary",))
    dq0, dk0, dla0, dv0, dq1, dk1, dla1, dv1 = outs
    res = ((dq0, dq1), (dk0, dk1), (dla0, dla1), (dv0, dv1))
    return res if rider is None else res + (rode,)


def _window_sums(win, g, shift):
    n = win.shape[0]
    levels, y = [], win
    for j in range(POOL_GROUPS):
        y = y + pltpu.roll(y, n - (1 << j), 0)
        levels.append(y)
    sums = levels[-1]
    for j in range(POOL_GROUPS - 2, -1, -1):
        sums = jnp.where(g == j, levels[j], sums)
    return pltpu.roll(sums, shift, 0)


def _pool_cnt(t0, half, rows, s):
    t = t0 + lax.broadcasted_iota(jnp.int32, (rows, 1), 0)
    return (jnp.minimum(t + half, s) - jnp.maximum(t - half, 0)).astype(F32)


def _pool_fwd(z, pw, scale, *, name):
    s = z.shape[0]
    tl = min(POOL_TILE, s)
    nt = s // tl
    ucol, gcol = SEG["pv"][0] // 128, SEG["pg"][0] // 128

    def body(u_ref, gt_ref, pw_ref, sc_ref, y_ref, pad):
        g = pl.program_id(0)
        half = jnp.left_shift(1, g)
        pad[0:POOL_HALO, :] = jnp.zeros((POOL_HALO, POOL_GW), F32)
        pad[POOL_HALO + s:POOL_HALO + s + POOL_HALO, :] = jnp.zeros((POOL_HALO, POOL_GW), F32)
        pad[POOL_HALO:POOL_HALO + s, :] = u_ref[...]
        pwv, scv = pw_ref[0], sc_ref[...]

        def tile(i, carry):
            t0 = pl.multiple_of(i * tl, tl)
            win = pad[pl.ds(t0, tl + 2 * POOL_HALO), :]
            u = win[POOL_HALO:POOL_HALO + tl, :]
            pooled = _window_sums(win, g, half)[POOL_HALO:POOL_HALO + tl, :] / _pool_cnt(t0, half, tl, s) - u
            mixed = _dot(pooled, pwv)
            silu, _ = _silu_parts(gt_ref[pl.ds(t0, tl), :])
            y_ref[pl.ds(t0, tl), :] = _bf(silu * (mixed * scv))
            return carry

        lax.fori_loop(0, nt, tile, 0)

    return pl.pallas_call(
        body, name=name, grid=(POOL_GROUPS,),
        in_specs=[pl.BlockSpec((s, POOL_GW), lambda g: (0, ucol + g)),
                  pl.BlockSpec((s, POOL_GW), lambda g: (0, gcol + g)),
                  pl.BlockSpec((1, POOL_GW, POOL_GW), lambda g: (g, 0, 0)),
                  pl.BlockSpec((1, POOL_GW), lambda g: (0, g))],
        out_specs=pl.BlockSpec((s, POOL_GW), lambda g: (0, g)),
        out_shape=jax.ShapeDtypeStruct((s, GROUP_W), BF16),
        scratch_shapes=[pltpu.VMEM((s + 2 * POOL_HALO, POOL_GW), F32)],
        compiler_params=_cparams("parallel"),
    )(z, z, pw, scale)


def _pool_bwd(dy, z, pw, scale, *, name):
    s = z.shape[0]
    tl = min(POOL_TILE, s)
    nt = s // tl
    ucol, gcol, ycol = SEG["pv"][0] // 128, SEG["pg"][0] // 128, 2 * GROUP_W // 128

    def body(dy_ref, u_ref, gt_ref, pw_ref, sc_ref, du_ref, dgt_ref, dpw_ref, dsc_ref, pad, epad, dpo):
        g = pl.program_id(0)
        half = jnp.left_shift(1, g)
        zeros = jnp.zeros((POOL_HALO, POOL_GW), F32)
        for buf in (pad, epad):
            buf[0:POOL_HALO, :] = zeros
            buf[POOL_HALO + s:POOL_HALO + s + POOL_HALO, :] = zeros
        pad[POOL_HALO:POOL_HALO + s, :] = u_ref[...]
        pwv, scv = pw_ref[0], sc_ref[...]
        dpw_ref[0] = jnp.zeros((POOL_GW, POOL_GW), F32)
        dsc_ref[...] = jnp.zeros((1, POOL_GW), F32)

        def tile(i, carry):
            t0 = pl.multiple_of(i * tl, tl)
            win = pad[pl.ds(t0, tl + 2 * POOL_HALO), :]
            u = win[POOL_HALO:POOL_HALO + tl, :]
            cnt = _pool_cnt(t0, half, tl, s)
            pooled = _window_sums(win, g, half)[POOL_HALO:POOL_HALO + tl, :] / cnt - u
            mixed = _dot(pooled, pwv)
            silu, dsilu = _silu_parts(gt_ref[pl.ds(t0, tl), :])
            dyv = dy_ref[pl.ds(t0, tl), :]
            dgt_ref[pl.ds(t0, tl), :] = _bf(dyv * (mixed * scv) * dsilu)
            dsc_ref[...] += jnp.sum(dyv * silu * mixed, axis=0, keepdims=True)
            dm = dyv * silu * scv
            dpw_ref[0] += _dot(pooled, dm, 0, 0)
            dpooled = _dot(dm, pwv, 1, 1)
            dpo[pl.ds(t0, tl), :] = dpooled
            epad[pl.ds(POOL_HALO + t0, tl), :] = dpooled / cnt
            return carry

        lax.fori_loop(0, nt, tile, 0)

        def tile2(i, carry):
            t0 = pl.multiple_of(i * tl, tl)
            ewin = epad[pl.ds(t0, tl + 2 * POOL_HALO), :]
            du_ref[pl.ds(t0, tl), :] = _bf(_window_sums(ewin, g, half - 1)[POOL_HALO:POOL_HALO + tl, :]
                                           - dpo[pl.ds(t0, tl), :])
            return carry

        lax.fori_loop(0, nt, tile2, 0)

    col = lambda c0: pl.BlockSpec((s, POOL_GW), lambda g: (0, c0 + g))
    return pl.pallas_call(
        body, name=name, grid=(POOL_GROUPS,),
        in_specs=[col(ycol), col(ucol), col(gcol), pl.BlockSpec((1, POOL_GW, POOL_GW), lambda g: (g, 0, 0)),
                  pl.BlockSpec((1, POOL_GW), lambda g: (0, g))],
        out_specs=[col(0), col(0), pl.BlockSpec((1, POOL_GW, POOL_GW), lambda g: (g, 0, 0)),
                   pl.BlockSpec((1, POOL_GW), lambda g: (0, g))],
        out_shape=[jax.ShapeDtypeStruct((s, GROUP_W), BF16), jax.ShapeDtypeStruct((s, GROUP_W), BF16),
                   jax.ShapeDtypeStruct((POOL_GROUPS, POOL_GW, POOL_GW), F32),
                   jax.ShapeDtypeStruct((1, GROUP_W), F32)],
        scratch_shapes=[pltpu.VMEM((s + 2 * POOL_HALO, POOL_GW), F32), pltpu.VMEM((s + 2 * POOL_HALO, POOL_GW), F32),
                        pltpu.VMEM((s, POOL_GW), F32)],
        compiler_params=_cparams("parallel"),
    )(dy, z, z, pw, scale)


def _mla_specs(tm):
    zq = pl.BlockSpec((tm, 512), lambda i: (i, SEG["mq"][0] // 512))
    zkv = pl.BlockSpec((tm, 256), lambda i: (i, SEG["mkv"][0] // 256))
    zkr = pl.BlockSpec((tm, 128), lambda i: (i, SEG["mkr"][0] // 128))
    full = lambda r, c: pl.BlockSpec((r, c), lambda i: (0, 0))
    tab = pl.BlockSpec((tm, 128), lambda i: (i, 0))
    weights = [full(1, 512), full(512, 1024), full(1, 256), full(256, 1024), full(1, 256), full(1, 256)]
    return [zq, zkv, zkr] + weights + [tab, tab, tab]


def _mla_project(xq_ref, xkv_ref, qg_ref, wq_ref, kvg_ref, wkv_ref):
    xq = xq_ref[...]
    r1 = lax.rsqrt(jnp.mean(xq * xq, axis=-1, keepdims=True) + EPS)
    xn1 = xq * r1
    qn = _bf(xn1 * qg_ref[...])
    qraw = _dot(qn, wq_ref[...])
    xkv = xkv_ref[...]
    r2 = lax.rsqrt(jnp.mean(xkv * xkv, axis=-1, keepdims=True) + EPS)
    xn2 = xkv * r2
    kvn = _bf(xn2 * kvg_ref[...])
    kvraw = _dot(kvn, wkv_ref[...])
    return r1, xn1, qn, qraw, r2, xn2, kvn, kvraw


def _mla_pre(z, qg, wq, kvg, wkv, qng, kng, cos, sp, sn, *, name, tm=ROW_TILE):
    s = z.shape[0]
    tm = min(tm, s)

    def body(xq_ref, xkv_ref, pe_ref, qg_ref, wq_ref, kvg_ref, wkv_ref, qng_ref, kng_ref, c_ref, sp_ref, sn_ref,
             q_ref, k_ref, v_ref):
        _, _, _, qraw, _, _, _, kvraw = _mla_project(xq_ref, xkv_ref, qg_ref, wq_ref, kvg_ref, wkv_ref)
        c, spv, snv = c_ref[...], sp_ref[...], sn_ref[...]
        pe = pe_ref[...]
        pe_ss = jnp.sum(pe * pe, axis=-1, keepdims=True)
        qngv, kngv = qng_ref[...], kng_ref[...]
        for h in range(MLA_HEADS):
            b = h * MLA_QKP
            qh = qraw[:, b:b + MLA_QKP]
            r = lax.rsqrt(jnp.sum(qh * qh, axis=-1, keepdims=True) * (1.0 / MLA_QK) + EPS)
            qn_h = qh * r * qngv
            q_ref[:, b:b + 128] = _bf(qn_h[:, :128] * MLA_SCALE)
            q_ref[:, b + 128:b + 256] = _bf(_rope64(qn_h[:, 128:], c, spv, snv) * MLA_SCALE)
            kn = kvraw[:, b:b + 128]
            rk = lax.rsqrt((jnp.sum(kn * kn, axis=-1, keepdims=True) + pe_ss) * (1.0 / MLA_QK) + EPS)
            k_ref[:, b:b + 128] = _bf(kn * rk * kngv[:, :128])
            k_ref[:, b + 128:b + 256] = _bf(_rope64(pe * rk * kngv[:, 128:], c, spv, snv))
            v_ref[:, h * MLA_V:(h + 1) * MLA_V] = _bf(kvraw[:, b + 128:b + 256])

    row = lambda w: pl.BlockSpec((tm, w), lambda i: (i, 0))
    return pl.pallas_call(
        body, name=name, grid=(s // tm,), in_specs=_mla_specs(tm),
        out_specs=[row(1024), row(1024), row(512)],
        out_shape=[jax.ShapeDtypeStruct((s, 1024), BF16), jax.ShapeDtypeStruct((s, 1024), BF16),
                   jax.ShapeDtypeStruct((s, 512), BF16)],
        compiler_params=_cparams("parallel"),
    )(z, z, z, qg, wq, kvg, wkv, qng, kng, cos, sp, sn)


def _mla_pre_bwd(dq, dk, dv, z, qg, wq, kvg, wkv, qng, kng, cos, sp, sn, *, name, tm=ROW_TILE):
    s = z.shape[0]
    tm = min(tm, s)

    def body(dq_ref, dk_ref, dv_ref, xq_ref, xkv_ref, pe_ref, qg_ref, wq_ref, kvg_ref, wkv_ref, qng_ref, kng_ref,
             c_ref, sp_ref, sn_ref, dxq_ref, dxkv_ref, dpe_ref, dwq_ref, dwkv_ref, dqg_ref, dkvg_ref, dqng_ref,
             dkng_ref, dqraw, dkvraw):
        i = pl.program_id(0)
        r1, xn1, qn, qraw, r2, xn2, kvn, kvraw = _mla_project(xq_ref, xkv_ref, qg_ref, wq_ref, kvg_ref, wkv_ref)
        c, spv, snv = c_ref[...], sp_ref[...], sn_ref[...]
        pe = pe_ref[...]
        pe_ss = jnp.sum(pe * pe, axis=-1, keepdims=True)
        qngv, kngv = qng_ref[...], kng_ref[...]
        dqng = jnp.zeros((1, MLA_QKP), F32)
        dkng = jnp.zeros((1, MLA_QKP), F32)
        dpe = jnp.zeros_like(pe)
        for h in range(MLA_HEADS):
            b = h * MLA_QKP
            qh = qraw[:, b:b + MLA_QKP]
            r = lax.rsqrt(jnp.sum(qh * qh, axis=-1, keepdims=True) * (1.0 / MLA_QK) + EPS)
            xn = qh * r
            d_n = jnp.concatenate(
                [dq_ref[:, b:b + 128], _unrope64(dq_ref[:, b + 128:b + 256], c, spv, snv)], axis=1) * MLA_SCALE
            dqng = dqng + jnp.sum(d_n * xn, axis=0, keepdims=True)
            dxn = d_n * qngv
            dqraw[:, b:b + MLA_QKP] = _bf(r * (dxn - xn * (jnp.sum(dxn * xn, axis=-1, keepdims=True) * (1.0 / MLA_QK))))
            kn = kvraw[:, b:b + 128]
            rk = lax.rsqrt((jnp.sum(kn * kn, axis=-1, keepdims=True) + pe_ss) * (1.0 / MLA_QK) + EPS)
            xk = jnp.concatenate([kn, pe], axis=1) * rk
            d_k = jnp.concatenate(
                [dk_ref[:, b:b + 128], _unrope64(dk_ref[:, b + 128:b + 256], c, spv, snv)], axis=1)
            dkng = dkng + jnp.sum(d_k * xk, axis=0, keepdims=True)
            dxk = d_k * kngv
            dfull = rk * (dxk - xk * (jnp.sum(dxk * xk, axis=-1, keepdims=True) * (1.0 / MLA_QK)))
            dkvraw[:, b:b + 128] = _bf(dfull[:, :128])
            dkvraw[:, b + 128:b + 256] = _bf(dv_ref[:, h * MLA_V:(h + 1) * MLA_V])
            dpe = dpe + dfull[:, 128:]
        dpe_ref[...] = _bf(dpe)
        dqr, dkvr = dqraw[...], dkvraw[...]
        dqn = _dot(dqr, wq_ref[...], 1, 1)
        dxn1 = dqn * qg_ref[...]
        dxq_ref[...] = _bf(r1 * (dxn1 - xn1 * jnp.mean(dxn1 * xn1, axis=-1, keepdims=True)))
        dkvn = _dot(dkvr, wkv_ref[...], 1, 1)
        dxn2 = dkvn * kvg_ref[...]
        dxkv_ref[...] = _bf(r2 * (dxn2 - xn2 * jnp.mean(dxn2 * xn2, axis=-1, keepdims=True)))
        parts = (_dot(qn, dqr, 0, 0), _dot(kvn, dkvr, 0, 0), jnp.sum(dqn * xn1, axis=0, keepdims=True),
                 jnp.sum(dkvn * xn2, axis=0, keepdims=True), dqng, dkng)
        accs = (dwq_ref, dwkv_ref, dqg_ref, dkvg_ref, dqng_ref, dkng_ref)

        @pl.when(i == 0)
        def _():
            for a, p in zip(accs, parts):
                a[...] = p

        @pl.when(i > 0)
        def _():
            for a, p in zip(accs, parts):
                a[...] += p

    row = lambda w: pl.BlockSpec((tm, w), lambda i: (i, 0))
    full = lambda r, c: pl.BlockSpec((r, c), lambda i: (0, 0))
    return pl.pallas_call(
        body, name=name, grid=(s // tm,),
        in_specs=[row(1024), row(1024), row(512)] + _mla_specs(tm),
        out_specs=[row(512), row(256), row(128), full(512, 1024), full(256, 1024), full(1, 512), full(1, 256),
                   full(1, 256), full(1, 256)],
        out_shape=[jax.ShapeDtypeStruct((s, 512), BF16), jax.ShapeDtypeStruct((s, 256), BF16),
                   jax.ShapeDtypeStruct((s, 128), BF16), jax.ShapeDtypeStruct((512, 1024), F32),
                   jax.ShapeDtypeStruct((256, 1024), F32), jax.ShapeDtypeStruct((1, 512), F32),
                   jax.ShapeDtypeStruct((1, 256), F32), jax.ShapeDtypeStruct((1, 256), F32),
                   jax.ShapeDtypeStruct((1, 256), F32)],
        scratch_shapes=[pltpu.VMEM((tm, 1024), BF16), pltpu.VMEM((tm, 1024), BF16)],
        compiler_params=_cparams("arbitrary"),
    )(dq, dk, dv, z, z, z, qg, wq, kvg, wkv, qng, kng, cos, sp, sn)


def _flash_fwd(q, k, v, *, name, tq=1024, tk=1024, rider=None):
    s = q.shape[0]
    tq, tk = min(tq, s), min(tk, s)
    nk = s // tk

    def body(q_ref, k_ref, v_ref, o_ref, lse_ref, m_s, l_s, acc):
        j = pl.program_id(2)

        @pl.when(j == 0)
        def _():
            m_s[...] = jnp.full_like(m_s, -jnp.inf)
            l_s[...] = jnp.zeros_like(l_s)
            acc[...] = jnp.zeros_like(acc)

        sc = _dot(q_ref[...], k_ref[...], 1, 1)
        m_prev = m_s[...]
        m_new = jnp.maximum(m_prev, jnp.max(sc, axis=-1, keepdims=True))
        p = jnp.exp(sc - m_new[:, 0:1])
        alpha = jnp.exp(m_prev - m_new)
        l_s[...] = alpha * l_s[...] + jnp.sum(p, axis=-1, keepdims=True)
        acc[...] = alpha * acc[...] + _dot(p, v_ref[...])
        m_s[...] = m_new

        @pl.when(j == nk - 1)
        def _():
            o_ref[...] = acc[...] / l_s[...]
            lse_ref[...] = m_s[...] + jnp.log(l_s[...])

    (o, lse), rode = _ride_call(
        body, rider, name=name, grid=(MLA_HEADS, s // tq, nk),
        in_specs=[pl.BlockSpec((tq, MLA_QKP), lambda h, i, j: (i, h)),
                  pl.BlockSpec((tk, MLA_QKP), lambda h, i, j: (j, h)),
                  pl.BlockSpec((tk, MLA_V), lambda h, i, j: (j, h))],
        out_specs=[pl.BlockSpec((tq, MLA_V), lambda h, i, j: (i, h))] * 2,
        out_shape=[jax.ShapeDtypeStruct((s, GROUP_W), F32)] * 2,
        scratch_shapes=[pltpu.VMEM((tq, MLA_V), F32), pltpu.VMEM((tq, MLA_V), F32), pltpu.VMEM((tq, MLA_V), F32)],
        args=(q, k, v), sem=("parallel", "parallel", "arbitrary"))
    return (o, lse) if rider is None else (o, lse, rode)


def _flash_bwd(q, k, v, do, o, lse, *, name, tq=1024, tk=1024, rider=None):
    s = q.shape[0]
    tq, tk = min(tq, s), min(tk, s)
    nq, nk = s // tq, s // tk

    def body(q_ref, k_ref, v_ref, do_ref, o_ref, lse_ref, dq_ref, dk_ref, dv_ref, dk_acc, dv_acc):
        j, i = pl.program_id(1), pl.program_id(2)
        dov = do_ref[...]
        delta = jnp.sum(dov * o_ref[...], axis=-1, keepdims=True)
        p = jnp.exp(_dot(q_ref[...], k_ref[...], 1, 1) - lse_ref[:, 0:1])
        ds = p * (_dot(dov, v_ref[...], 1, 1) - delta)
        pv = _dot(p, dov, 0, 0)
        pk = _dot(ds, q_ref[...], 0, 0)
        pq = _dot(ds, k_ref[...])
        rows = pl.ds(pl.multiple_of(i * tq, tq), tq)

        @pl.when(j == 0)
        def _():
            dq_ref[rows, :] = pq

        @pl.when(j > 0)
        def _():
            dq_ref[rows, :] += pq

        @pl.when(i == 0)
        def _():
            dv_acc[...] = pv
            dk_acc[...] = pk

        @pl.when(i > 0)
        def _():
            dv_acc[...] += pv
            dk_acc[...] += pk

        @pl.when(i == nq - 1)
        def _():
            dk_ref[...] = dk_acc[...]
            dv_ref[...] = dv_acc[...]

    qb = pl.BlockSpec((tq, MLA_QKP), lambda h, j, i: (i, h))
    kb = pl.BlockSpec((tk, MLA_QKP), lambda h, j, i: (j, h))
    vb = pl.BlockSpec((tk, MLA_V), lambda h, j, i: (j, h))
    ob = pl.BlockSpec((tq, MLA_V), lambda h, j, i: (i, h))
    (dq, dk, dv), rode = _ride_call(
        body, rider, name=name, grid=(MLA_HEADS, nk, nq),
        in_specs=[qb, kb, vb, ob, ob, ob],
        out_specs=[pl.BlockSpec((s, MLA_QKP), lambda h, j, i: (0, h)), kb, vb],
        out_shape=[jax.ShapeDtypeStruct((s, MLA_HEADS * MLA_QKP), F32),
                   jax.ShapeDtypeStruct((s, MLA_HEADS * MLA_QKP), F32), jax.ShapeDtypeStruct((s, GROUP_W), F32)],
        scratch_shapes=[pltpu.VMEM((tk, MLA_QKP), F32), pltpu.VMEM((tk, MLA_V), F32)],
        args=(q, k, v, do, o, lse), sem=("arbitrary", "arbitrary", "arbitrary"))
    return (dq, dk, dv) if rider is None else (dq, dk, dv, rode)


def _rows_tile(r, c, itemsize=4, budget=2 * 1024 * 1024):
    if r * c * itemsize <= budget:
        return r
    best = None
    for t in range(8, r, 8):
        if r % t == 0 and t * c * itemsize <= budget:
            best = t
    return best if best is not None else r


def _landing(into, tm, width):
    buf, col = into
    assert col % width == 0
    return (jax.ShapeDtypeStruct(buf.shape, buf.dtype), pl.BlockSpec((tm, width), lambda i: (i, col // width)),
            [ANY], [buf])


def _add_n(arrs, *, out_dtype=F32, name, into=None):
    shape = arrs[0].shape
    c = shape[-1]
    flat = [a.reshape(-1, c) for a in arrs]
    r = flat[0].shape[0]
    t = _rows_tile(r, c)
    n_in = len(flat)

    def body(*refs):
        acc = refs[0][...].astype(F32)
        for ref in refs[1:n_in]:
            acc = acc + ref[...].astype(F32)
        refs[-1][...] = acc.astype(out_dtype)

    blk = pl.BlockSpec((t, c), lambda i: (i, 0))
    if into is not None:
        out_shape, out_spec, more_specs, more_args = _landing(into, t, c)
        return pl.pallas_call(
            body, name=name, grid=(r // t,), in_specs=[blk] * n_in + more_specs, out_specs=out_spec,
            out_shape=out_shape, input_output_aliases={n_in: 0}, compiler_params=_cparams("parallel"),
        )(*flat, *more_args)
    out = pl.pallas_call(
        body, name=name, grid=(r // t,), in_specs=[blk] * n_in, out_specs=blk,
        out_shape=jax.ShapeDtypeStruct((r, c), out_dtype), compiler_params=_cparams("parallel"),
    )(*flat)
    return out.reshape(shape)


def _adamw(w, g, m, v, *, name):
    shape = w.shape
    c = shape[-1]
    flat = [a.reshape(-1, c) for a in (w, g, m, v)]
    r = flat[0].shape[0]
    t = _rows_tile(r, c, budget=1024 * 1024)

    def body(w_ref, g_ref, m_ref, v_ref, d_ref, mo_ref, vo_ref):
        gv = g_ref[...]
        m2 = ADAM_B1 * m_ref[...] + (1.0 - ADAM_B1) * gv
        v2 = ADAM_B2 * v_ref[...] + (1.0 - ADAM_B2) * (gv * gv)
        m_hat = m2 / (1.0 - ADAM_B1 ** ADAM_STEP)
        v_hat = v2 / (1.0 - ADAM_B2 ** ADAM_STEP)
        d_ref[...] = -ADAM_LR * (m_hat / (jnp.sqrt(v_hat) + ADAM_EPS) + ADAM_WD * w_ref[...])
        mo_ref[...] = m2
        vo_ref[...] = v2

    blk = pl.BlockSpec((t, c), lambda i: (i, 0))
    outs = pl.pallas_call(
        body, name=name, grid=(r // t,), in_specs=[blk] * 4, out_specs=[blk] * 3,
        out_shape=[jax.ShapeDtypeStruct((r, c), F32)] * 3, compiler_params=_cparams("parallel"),
    )(*flat)
    return tuple(o.reshape(shape) for o in outs)


def _place():
    x, y, c = lax.axis_index("x"), lax.axis_index("y"), lax.axis_index("c")
    chips = [(1 - x, y), (x, 1 - y), (1 - x, 1 - y)]
    return x, y, c, chips


ANY = pl.BlockSpec(memory_space=pl.ANY)


def _half(ref, axis, hc, lead=()):
    n = ref.shape[len(lead) + axis] // 2
    return ref.at[tuple(lead) + (slice(None),) * axis + (pl.ds(hc * n, n),)]


def _gather_shards(shards, axes, *, name):
    nt = len(shards)

    def body(*refs):
        src, dst = refs[:nt], refs[nt:2 * nt]
        send, recv, fsend, frecv, lsem = refs[2 * nt:]
        x, y, c, chips = _place()
        me = 2 * x + y
        local = [pltpu.make_async_copy(src[t], dst[t].at[me], lsem.at[t]) for t in range(nt)]
        for cp in local:
            cp.start()

        def half(t, slot, hc):
            return _half(dst[t], axes[t], hc, lead=(slot,))

        def first(t, k):
            return pltpu.make_async_remote_copy(
                src_ref=_half(src[t], axes[t], c), dst_ref=half(t, me, c),
                send_sem=send.at[t, k], recv_sem=recv.at[t, k],
                device_id=(chips[k][0], chips[k][1], c), device_id_type=MESH)

        def landed(t, k):
            slot = 2 * chips[k][0] + chips[k][1]
            return pltpu.make_async_remote_copy(
                src_ref=half(t, slot, c), dst_ref=half(t, slot, c),
                send_sem=send.at[t, k], recv_sem=recv.at[t, k],
                device_id=(chips[k][0], chips[k][1], c), device_id_type=MESH)

        def forward(t, k, hc):
            slot = 2 * chips[k][0] + chips[k][1]
            return pltpu.make_async_remote_copy(
                src_ref=half(t, slot, hc), dst_ref=half(t, slot, hc),
                send_sem=fsend.at[t, k], recv_sem=frecv.at[t, k],
                device_id=(x, y, 1 - c), device_id_type=MESH)

        for t in range(nt):
            for k in range(3):
                first(t, k).start()
        for t in range(nt):
            for k in range(3):
                landed(t, k).wait_recv()
                forward(t, k, c).start()
        for t in range(nt):
            for k in range(3):
                forward(t, k, 1 - c).wait_recv()
        for t in range(nt):
            for k in range(3):
                first(t, k).wait_send()
                forward(t, k, c).wait_send()
        for cp in local:
            cp.wait()

    return pl.pallas_call(
        body, name=name, in_specs=[ANY] * nt, out_specs=[ANY] * nt,
        out_shape=[jax.ShapeDtypeStruct((N_CHIP,) + a.shape, a.dtype) for a in shards],
        scratch_shapes=[pltpu.SemaphoreType.DMA((nt, 3)), pltpu.SemaphoreType.DMA((nt, 3)),
                        pltpu.SemaphoreType.DMA((nt, 3)), pltpu.SemaphoreType.DMA((nt, 3)),
                        pltpu.SemaphoreType.DMA((nt,))],
    )(*shards)


def _comm_rows(hr, c, budget=2 * 1024 * 1024):
    if hr * c * 4 <= budget:
        return hr
    best = None
    for t in range(16, hr, 16):
        if hr % t == 0 and t * c * 4 <= budget:
            best = t
    return best if best is not None else hr


def _comm_cols(r, hc, budget=2 * 1024 * 1024):
    best = 128
    for t in range(128, hc + 1, 128):
        if hc % t == 0 and r * t * 4 <= budget:
            best = t
    return best


def _comm_chunks(shape, axis):
    r, cdim = shape
    if axis == 0:
        rc = _comm_rows(r // 2, cdim)
        nt = (r // 2) // rc
        return (rc, cdim), nt, (lambda h, t: (h * nt + t, 0))
    cc = _comm_cols(r, cdim // 2)
    nt = (cdim // 2) // cc
    return (r, cc), nt, (lambda h, t: (0, h * nt + t))


def _pair_reduce(g, where, axis, *, out_dtype, name):
    n_slot, r, cdim = g.shape
    blk_shape, nr, at = _comm_chunks((r, cdim), axis)
    steps = n_slot * nr
    half_shape = (r // 2, cdim) if axis == 0 else (r, cdim // 2)

    def body(w_ref, a_ref, b_ref, o_ref, land, send, recv, credit):
        x, y, c, _ = _place()
        sib = (x, y, 1 - c)
        i = pl.program_id(0) * nr + pl.program_id(1)
        s = lax.rem(i, 2)

        @pl.when(i >= 2)
        def _():
            pl.semaphore_wait(credit.at[s], 1)

        cp = pltpu.make_async_remote_copy(src_ref=b_ref.at[0], dst_ref=land.at[s], send_sem=send.at[s],
                                          recv_sem=recv.at[s], device_id=sib, device_id_type=MESH)
        cp.start()
        cp.wait_recv()
        o_ref[0] = (a_ref[0] + land[s]).astype(out_dtype)
        cp.wait_send()

        @pl.when(i + 2 < steps)
        def _():
            pl.semaphore_signal(credit.at[s], inc=1, device_id=sib, device_id_type=MESH)

    blk = lambda half: pl.BlockSpec((1,) + blk_shape, lambda j, t, w: (j,) + at(half(w), t))
    grid_spec = pltpu.PrefetchScalarGridSpec(
        num_scalar_prefetch=1, grid=(n_slot, nr),
        in_specs=[blk(lambda w: w[0]), blk(lambda w: 1 - w[0])],
        out_specs=pl.BlockSpec((1,) + blk_shape, lambda j, t, w: (j,) + at(0, t)),
        scratch_shapes=[pltpu.VMEM((2,) + blk_shape, F32), pltpu.SemaphoreType.DMA((2,)),
                        pltpu.SemaphoreType.DMA((2,)), pltpu.SemaphoreType.REGULAR((2,))])
    return pl.pallas_call(
        body, name=name, grid_spec=grid_spec, out_shape=jax.ShapeDtypeStruct((n_slot,) + half_shape, out_dtype),
        compiler_params=_cparams("arbitrary", "arbitrary"),
    )(where, g, g)


def _chip_exchange(parts, *, name):
    nt = len(parts)

    def body(*refs):
        src, got = refs[:nt], refs[nt:2 * nt]
        send, recv = refs[2 * nt:]
        x, y, c, chips = _place()
        remote = []
        for t in range(nt):
            for k in range(3):
                remote.append(pltpu.make_async_remote_copy(
                    src_ref=src[t].at[2 * chips[k][0] + chips[k][1]], dst_ref=got[t].at[k],
                    send_sem=send.at[t, k], recv_sem=recv.at[t, k],
                    device_id=(chips[k][0], chips[k][1], c), device_id_type=MESH))
        for cp in remote:
            cp.start()
        for cp in remote:
            cp.wait_recv()
        for cp in remote:
            cp.wait_send()

    return pl.pallas_call(
        body, name=name, in_specs=[ANY] * nt, out_specs=[ANY] * nt,
        out_shape=[jax.ShapeDtypeStruct((3,) + a.shape[1:], a.dtype) for a in parts],
        scratch_shapes=[pltpu.SemaphoreType.DMA((nt, 3)), pltpu.SemaphoreType.DMA((nt, 3))],
    )(*parts)


def _sum_join(p, got, where, axis, *, name):
    _, hr, cdim = p.shape
    full = (2 * hr, cdim) if axis == 0 else (hr, 2 * cdim)
    blk_shape, n, at = _comm_chunks(full, axis)
    step_len = blk_shape[axis]
    half_len = full[axis] // 2

    def body(w_ref, p_ref, g_ref, out, buf, lsem, ssem, rsem):
        x, y, c, _ = _place()
        sib = (x, y, 1 - c)
        r = pl.program_id(0)

        def part(start, size):
            return out.at[(slice(None),) * axis + (pl.ds(start, size),)]

        def copies(step, slot):
            rows = part(pl.multiple_of(c * half_len + step * step_len, 8 if axis == 0 else 128), step_len)
            return (pltpu.make_async_copy(buf.at[slot], rows, lsem.at[slot]),
                    pltpu.make_async_remote_copy(src_ref=buf.at[slot], dst_ref=rows, send_sem=ssem.at[slot],
                                                 recv_sem=rsem, device_id=sib, device_id_type=MESH))

        s = lax.rem(r, 2)

        @pl.when(r >= 2)
        def _():
            lc, rm = copies(r - 2, s)
            lc.wait()
            rm.wait_send()

        buf[s] = p_ref[0].astype(F32) + g_ref[0].astype(F32) + g_ref[1].astype(F32) + g_ref[2].astype(F32)
        lc, rm = copies(r, s)
        lc.start()
        rm.start()

        @pl.when(r == n - 1)
        def _():
            for step in range(max(0, n - 2), n):
                lc, rm = copies(step, step % 2)
                lc.wait()
                rm.wait_send()
            whole = part(0, half_len)
            pltpu.make_async_remote_copy(src_ref=whole, dst_ref=whole, send_sem=ssem.at[0], recv_sem=rsem,
                                         device_id=sib, device_id_type=MESH).wait_recv()

    grid_spec = pltpu.PrefetchScalarGridSpec(
        num_scalar_prefetch=1, grid=(n,),
        in_specs=[pl.BlockSpec((1,) + blk_shape, lambda t, w: (w[1],) + at(0, t)),
                  pl.BlockSpec((3,) + blk_shape, lambda t, w: (0,) + at(0, t))],
        out_specs=ANY,
        scratch_shapes=[pltpu.VMEM((2,) + blk_shape, F32), pltpu.SemaphoreType.DMA((2,)),
                        pltpu.SemaphoreType.DMA((2,)), pltpu.SemaphoreType.DMA])
    return pl.pallas_call(
        body, name=name, grid_spec=grid_spec, out_shape=jax.ShapeDtypeStruct(full, F32),
        compiler_params=_cparams("arbitrary"),
    )(where, p, got)


def _rider_gather_send(shards, axes):
    nt = len(shards)

    def copies(src, dst, send, recv, lsem):
        x, y, c, chips = _place()
        me = 2 * x + y
        local = [pltpu.make_async_copy(src[t], dst[t].at[me], lsem.at[t]) for t in range(nt)]
        out, landed = [], []
        for t in range(nt):
            for k in range(3):
                peer = (chips[k][0], chips[k][1], c)
                out.append(pltpu.make_async_remote_copy(
                    src_ref=_half(src[t], axes[t], c), dst_ref=_half(dst[t], axes[t], c, lead=(me,)),
                    send_sem=send.at[t, k], recv_sem=recv.at[t, k], device_id=peer, device_id_type=MESH))
                theirs = _half(dst[t], axes[t], c, lead=(2 * chips[k][0] + chips[k][1],))
                landed.append(pltpu.make_async_remote_copy(
                    src_ref=theirs, dst_ref=theirs, send_sem=send.at[t, k], recv_sem=recv.at[t, k],
                    device_id=peer, device_id_type=MESH))
        return local, out, landed

    def start(src, dst, sems):
        local, out, _ = copies(src, dst, *sems)
        for cp in local + out:
            cp.start()

    def finish(src, dst, sems):
        local, out, landed = copies(src, dst, *sems)
        for cp in landed:
            cp.wait_recv()
        for cp in out:
            cp.wait_send()
        for cp in local:
            cp.wait()

    return _Rider(shards, [jax.ShapeDtypeStruct((N_CHIP,) + a.shape, a.dtype) for a in shards],
                  [pltpu.SemaphoreType.DMA((nt, 3)), pltpu.SemaphoreType.DMA((nt, 3)), pltpu.SemaphoreType.DMA((nt,))],
                  start, finish)


def _rider_gather_forward(bufs, axes):
    nt = len(bufs)

    def copies(src, dst, send, recv):
        x, y, c, chips = _place()
        mine, theirs = [], []
        for t in range(nt):
            for k in range(3):
                slot = 2 * chips[k][0] + chips[k][1]
                for hc, into in ((c, mine), (1 - c, theirs)):
                    into.append(pltpu.make_async_remote_copy(
                        src_ref=_half(src[t], axes[t], hc, lead=(slot,)),
                        dst_ref=_half(dst[t], axes[t], hc, lead=(slot,)),
                        send_sem=send.at[t, k], recv_sem=recv.at[t, k], device_id=(x, y, 1 - c), device_id_type=MESH))
        return mine, theirs

    def start(src, dst, sems):
        for cp in copies(src, dst, *sems)[0]:
            cp.start()

    def finish(src, dst, sems):
        mine, theirs = copies(src, dst, *sems)
        for cp in theirs:
            cp.wait_recv()
        for cp in mine:
            cp.wait_send()

    return _Rider(bufs, [jax.ShapeDtypeStruct(a.shape, a.dtype) for a in bufs],
                  [pltpu.SemaphoreType.DMA((nt, 3)), pltpu.SemaphoreType.DMA((nt, 3))], start, finish,
                  aliases={t: t for t in range(nt)})


def _rider_chip_exchange(parts):
    nt = len(parts)

    def copies(src, got, send, recv):
        x, y, c, chips = _place()
        return [pltpu.make_async_remote_copy(
            src_ref=src[t].at[2 * chips[k][0] + chips[k][1]], dst_ref=got[t].at[k], send_sem=send.at[t, k],
            recv_sem=recv.at[t, k], device_id=(chips[k][0], chips[k][1], c), device_id_type=MESH)
            for t in range(nt) for k in range(3)]

    def start(src, got, sems):
        for cp in copies(src, got, *sems):
            cp.start()

    def finish(src, got, sems):
        remote = copies(src, got, *sems)
        for cp in remote:
            cp.wait_recv()
        for cp in remote:
            cp.wait_send()

    return _Rider(parts, [jax.ShapeDtypeStruct((3,) + a.shape[1:], a.dtype) for a in parts],
                  [pltpu.SemaphoreType.DMA((nt, 3)), pltpu.SemaphoreType.DMA((nt, 3))], start, finish)


def _gather_all(block, *, name):
    m_per, n = block.shape

    def body(x_ref, out_ref, send_sems, recv_sems, local_sem):
        x, y, c, chips = _place()
        me, sibling = (x, y, c), (x, y, 1 - c)

        def rows(px, py, pc):
            return out_ref.at[4 * px + 2 * py + pc]

        def copy(k, blk, to, src=None):
            return pltpu.make_async_remote_copy(
                src_ref=rows(*blk) if src is None else src, dst_ref=rows(*blk),
                send_sem=send_sems.at[k], recv_sem=recv_sems.at[k], device_id=to, device_id_type=MESH)

        mine = pltpu.make_async_copy(x_ref, rows(*me), local_sem)
        mine.start()
        first = [copy(0, me, sibling, src=x_ref)]
        first += [copy(1 + j, me, (*chip, c), src=x_ref) for j, chip in enumerate(chips)]
        for cp in first:
            cp.start()
        passed = [copy(4 + j, (*chip, c), sibling) for j, chip in enumerate(chips)]
        for j, chip in enumerate(chips):
            copy(1 + j, (*chip, c), me).wait_recv()
            passed[j].start()
        copy(0, sibling, me).wait_recv()
        for j, chip in enumerate(chips):
            copy(4 + j, (*chip, 1 - c), me).wait_recv()
        for cp in first + passed:
            cp.wait_send()
        mine.wait()

    return pl.pallas_call(
        body, name=name,
        out_shape=jax.ShapeDtypeStruct((N_DEV, m_per, n), block.dtype),
        in_specs=[pl.BlockSpec(memory_space=pltpu.VMEM)], out_specs=pl.BlockSpec(memory_space=pltpu.VMEM),
        scratch_shapes=[pltpu.SemaphoreType.DMA((7,)), pltpu.SemaphoreType.DMA((7,)), pltpu.SemaphoreType.DMA],
        compiler_params=pltpu.CompilerParams(vmem_limit_bytes=VMEM_LIMIT),
    )(block)


def _sum_slots(slots, *, name):
    n, m, c = slots.shape
    t = _rows_tile(m, c * n)

    def body(s_ref, o_ref):
        acc = s_ref[0]
        for k in range(1, n):
            acc = acc + s_ref[k]
        o_ref[...] = acc

    return pl.pallas_call(
        body, name=name, grid=(m // t,), in_specs=[pl.BlockSpec((n, t, c), lambda i: (0, i, 0))],
        out_specs=pl.BlockSpec((t, c), lambda i: (i, 0)), out_shape=jax.ShapeDtypeStruct((m, c), F32),
        compiler_params=_cparams("parallel"),
    )(slots)


def _pad_rows(a, rows):
    return a if a.shape[0] == rows else jnp.pad(a, ((0, rows - a.shape[0]), (0, 0)))


def _w_in_padded(shards):
    full = shards.reshape(IN_COLS, shards.shape[2])
    return jnp.concatenate([_pad_rows(full[SEG[n][2]:SEG[n][2] + SEG[n][3]], SEG[n][1]) for n in SEG_ORDER], axis=0)


def _w_in_unpadded(gp):
    full = jnp.concatenate([gp[SEG[n][0]:SEG[n][0] + SEG[n][3]] for n in ORIG_ORDER], axis=0)
    return full.reshape(N_CHIP, IN_COLS // N_CHIP, gp.shape[1])


def _pad_heads(w, true_w, pad_w):
    r = w.shape[0]
    h = w.shape[1] // true_w
    return jnp.pad(w.reshape(r, h, true_w), ((0, 0), (0, 0), (0, pad_w - true_w))).reshape(r, h * pad_w)


def _unpad_heads(w, true_w, pad_w):
    r = w.shape[0]
    h = w.shape[1] // pad_w
    return w.reshape(r, h, pad_w)[:, :, :true_w].reshape(r, h * true_w)


def _cols_to_slots(a):
    return a.reshape(a.shape[0], N_CHIP, a.shape[1] // N_CHIP).transpose(1, 0, 2)


def _to_heads(a, h, d):
    return a.reshape(a.shape[0], h, d).transpose(1, 0, 2)


def _slots_to_cols(a):
    return jnp.concatenate([a[j] for j in range(N_CHIP)], axis=1)


SMALL = [("norm_g", 2048), ("ret_norm_g", 512), ("gla_ba_f", 256), ("gla_ba_b", 256), ("gla_norm_g", 512),
         ("pool_w", 4 * 128 * 128), ("pool_scale", 512), ("mla_q_norm_g", 512), ("mla_kv_norm_g", 256),
         ("mla_qk_norm_q", 192), ("mla_qk_norm_k", 192)]


def _pack_small(vals):
    parts = []
    for name, n in SMALL:
        parts += [v.reshape(-1) for v in vals[name]]
        if (DEPTH * n) % 1024:
            parts.append(jnp.zeros((-(DEPTH * n)) % 1024, F32))
    parts += [vals["loss"].reshape(-1), jnp.zeros(1023, F32)]
    return jnp.concatenate(parts).reshape(-1, 128)


def _unpack_small(block):
    flat = block.reshape(-1)
    out, off = {}, 0
    for name, n in SMALL:
        out[name] = flat[off:off + DEPTH * n]
        off += DEPTH * n + (-(DEPTH * n)) % 1024
    out["loss"] = flat[off]
    return out


def _layer_weights(l, p, g):
    wa = jnp.zeros((128, 512), F32)
    wa = wa.at[0:GLA_RANK, 0:256].set(_slots_to_cols(g["gla_wa2_f"]))
    wa = wa.at[GLA_RANK:2 * GLA_RANK, 256:512].set(_slots_to_cols(g["gla_wa2_b"]))
    return dict(
        norm_g=p["norm_g"][l][None, :],
        w_in=_w_in_padded(g["w_in"]),
        w_out=g["w_out"].reshape(4 * g["w_out"].shape[1], -1),
        ret_norm_g=p["ret_norm_g"][l][None, :],
        wa=_bf(wa),
        ba=jnp.concatenate([p["gla_ba_f"][l], p["gla_ba_b"][l]])[None, :],
        gla_norm_g=p["gla_norm_g"][l][None, :],
        pool_w=_bf(p["pool_w"][l]),
        pool_scale=p["pool_scale"][l][None, :],
        qg=p["mla_q_norm_g"][l][None, :],
        wq=_pad_heads(_slots_to_cols(g["mla_wq_b"]), MLA_QK, MLA_QKP),
        kvg=p["mla_kv_norm_g"][l][None, :],
        wkv=_slots_to_cols(g["mla_wkv_b"]),
        qng=jnp.pad(p["mla_qk_norm_q"][l], (0, MLA_QKP - MLA_QK))[None, :],
        kng=jnp.pad(p["mla_qk_norm_k"][l], (0, MLA_QKP - MLA_QK))[None, :],
    )


def _layer_fwd(l, x, w, tabs, next_shards=None, loss_target=None):
    ret_cos, ret_sin, mla_cos, mla_sp, mla_sn = tabs
    nm = lambda s: f"l{l}_{s}"
    h = _rmsnorm_fwd(x, w["norm_g"], name=nm("norm"))
    if next_shards is None:
        z = _matmul(h, w["w_in"], tb=True, name=nm("in_proj"))
    else:
        z, landed = _matmul(h, w["w_in"], tb=True, rider=_rider_gather_send(next_shards[:1], SHARD_AXES[:1]),
                            name=nm("in_proj"))
    qr, kr = _ret_pre(z, ret_cos, ret_sin, name=nm("ret_pre"))
    ret_o = _bla(qr, kr, z, _ret_log_gamma(False), (0, 0, SEG["rv"][0] // 512), name=nm("ret_scan"))
    y_a = _post(ret_o, z, SEG["rg"][0] // 512, w["ret_norm_g"], norm=True, name=nm("ret_post"))
    la = _gla_gate(z, w["wa"], w["ba"], name=nm("gla_gate"))
    la_h = la.reshape(la.shape[0], 2, GLA_HEADS, GLA_DK).transpose(1, 2, 0, 3)
    gq = _to_heads(z[:, SEG["gq"][0]:SEG["gq"][0] + 256], GLA_HEADS, GLA_DK)
    gk = _to_heads(z[:, SEG["gk"][0]:SEG["gk"][0] + 256], GLA_HEADS, GLA_DK)
    if next_shards is None:
        gla_o, gla_st = _gla_fwd(gq, gk, z, la_h, name=nm("gla_scan"))
    else:
        gla_o, gla_st, more = _gla_fwd(gq, gk, z, la_h, rider=_rider_gather_send(next_shards[1:], SHARD_AXES[1:]),
                                       name=nm("gla_scan"))
        landed = list(landed) + list(more)
    y_b = _post(gla_o, z, SEG["gg"][0] // 512, w["gla_norm_g"], norm=True, name=nm("gla_post"))
    y_c = _pool_fwd(z, w["pool_w"], w["pool_scale"], name=nm("pool"))
    q, k, v = _mla_pre(z, w["qg"], w["wq"], w["kvg"], w["wkv"], w["qng"], w["kng"], mla_cos, mla_sp, mla_sn,
                       name=nm("mla_pre"))
    if next_shards is None:
        (att_o, lse), gathered = _flash_fwd(q, k, v, name=nm("attn")), None
    else:
        att_o, lse, gathered = _flash_fwd(q, k, v, rider=_rider_gather_forward(landed, SHARD_AXES), name=nm("attn"))
    y_d = _post([att_o], z, SEG["mg"][0] // 512, w["qg"], norm=False, name=nm("mla_post"))
    y = jnp.concatenate([y_a, y_b, y_c, y_d], axis=1)
    if loss_target is None:
        x_next = _matmul(y, w["w_out"], add=x, name=nm("out_proj"))
    else:
        x_next = _out_proj_loss(y, w["w_out"], x, loss_target, name=nm("out_proj"))
    saved = dict(x=x, h=h, z=z, y=y, qr=qr, kr=kr, ret_o=ret_o, la_h=la_h, gq=gq, gk=gk, gla_o=gla_o, gla_st=gla_st,
                 q=q, k=k, v=v, att_o=att_o, lse=lse)
    return x_next, saved, gathered


def _layer_bwd(l, dx_next, w, sv, tabs, riding_parts=None, where=None):
    ret_cos, ret_sin, mla_cos, mla_sp, mla_sn = tabs
    nm = lambda s: f"l{l}_{s}"
    z = sv["z"]
    dy = _matmul(dx_next, w["w_out"], tb=True, name=nm("out_proj_dy"))
    d_w_out = _matmul(sv["y"], dx_next, ta=True, tn=512, name=nm("out_proj_dw"))
    d_w_out = d_w_out.reshape(N_CHIP, d_w_out.shape[0] // N_CHIP, d_w_out.shape[1])
    if where is not None:
        pair_w_out = _pair_reduce(d_w_out, where, 0, out_dtype=BF16, name=nm("pair_reduce_w_out"))
    dz = lax.empty((z.shape[0], IN_PAD), BF16)
    at = lambda n: SEG[n][0]
    dz, d_ret_o, d_ret_g = _post_bwd(dy, 0, sv["ret_o"], z, SEG["rg"][0] // 512, w["ret_norm_g"], (dz, at("rg")),
                                     norm=True, name=nm("ret_post_bwd"))
    vcol = SEG["rv"][0] // 512
    dqr = _bla(d_ret_o, z, sv["kr"], _ret_log_gamma(False), (0, vcol, 0), name=nm("ret_scan_dq"))
    dkr = _bla(z, d_ret_o, sv["qr"], _ret_log_gamma(True), (vcol, 0, 0), name=nm("ret_scan_dk"))
    drv = _bla(sv["kr"], sv["qr"], d_ret_o, _ret_log_gamma(True), (0, 0, 0), name=nm("ret_scan_dv"))
    dz = _ret_pre_bwd(dqr, dkr, ret_cos, ret_sin, (dz, at("rq")), name=nm("ret_pre_bwd"))
    dz = _add_n([drv[0], drv[1]], out_dtype=BF16, into=(dz, at("rv")), name=nm("ret_dv_sum"))
    dz, d_gla_o, d_gla_g = _post_bwd(dy, 1, sv["gla_o"], z, SEG["gg"][0] // 512, w["gla_norm_g"], (dz, at("gg")),
                                     norm=True, name=nm("gla_post_bwd"))
    if where is None:
        dq2, dk2, dla2, dv2 = _gla_bwd(sv["gq"], sv["gk"], z, sv["la_h"], d_gla_o, sv["gla_st"],
                                       name=nm("gla_scan_bwd"))
    else:
        dq2, dk2, dla2, dv2, (others_w_out,) = _gla_bwd(
            sv["gq"], sv["gk"], z, sv["la_h"], d_gla_o, sv["gla_st"], rider=_rider_chip_exchange([pair_w_out]),
            name=nm("gla_scan_bwd"))
        d_w_out = (pair_w_out, others_w_out)
    d_gq = _bf(dq2[0] + dq2[1])
    d_gk = _bf(dk2[0] + dk2[1])
    dz = _add_n([dv2[0], dv2[1]], out_dtype=BF16, into=(dz, at("gv")), name=nm("gla_dv_sum"))
    dz, d_wa, d_ba = _gla_gate_bwd(dla2, z, w["wa"], w["ba"], (dz, at("ga")), name=nm("gla_gate_bwd"))
    d_pv, d_pg, d_pool_w, d_pool_scale = _pool_bwd(dy, z, w["pool_w"], w["pool_scale"], name=nm("pool_bwd"))
    dz, d_att_o, _ = _post_bwd(dy, 3, [sv["att_o"]], z, SEG["mg"][0] // 512, w["qg"], (dz, at("mg")), norm=False,
                               name=nm("mla_post_bwd"))
    if riding_parts is None:
        (dq, dk, dv), rode = _flash_bwd(sv["q"], sv["k"], sv["v"], d_att_o, sv["att_o"], sv["lse"],
                                        name=nm("attn_bwd")), None
    else:
        dq, dk, dv, rode = _flash_bwd(sv["q"], sv["k"], sv["v"], d_att_o, sv["att_o"], sv["lse"],
                                      rider=_rider_chip_exchange(riding_parts), name=nm("attn_bwd"))
    d_mq, d_mkv, d_mkr, d_wq, d_wkv, d_qg, d_kvg, d_qng, d_kng = _mla_pre_bwd(
        dq, dk, dv, z, w["qg"], w["wq"], w["kvg"], w["wkv"], w["qng"], w["kng"], mla_cos, mla_sp, mla_sn,
        name=nm("mla_pre_bwd"))
    for n, seg in dict(pv=d_pv, pg=d_pg, mq=d_mq, gq=d_gq, gk=d_gk, mkv=d_mkv, mkr=d_mkr).items():
        dz = lax.dynamic_update_slice(dz, seg, (0, at(n)))
    dh = _matmul(dz, w["w_in"], tn=512, name=nm("in_proj_dh"))
    d_w_in = _matmul(dz, sv["h"], ta=True, name=nm("in_proj_dw"))
    dx, d_norm_g = _rmsnorm_bwd(sv["x"], dh, w["norm_g"], dx_next, name=nm("norm_bwd"))
    sharded = dict(
        w_in=_w_in_unpadded(d_w_in),
        w_out=d_w_out,
        mla_wq_b=_cols_to_slots(_unpad_heads(d_wq, MLA_QK, MLA_QKP)),
        mla_wkv_b=_cols_to_slots(d_wkv),
        gla_wa2_f=_cols_to_slots(d_wa[0:GLA_RANK, 0:256]),
        gla_wa2_b=_cols_to_slots(d_wa[GLA_RANK:2 * GLA_RANK, 256:512]),
    )
    small = dict(
        norm_g=d_norm_g[0], ret_norm_g=d_ret_g[0], gla_ba_f=d_ba[0, :256], gla_ba_b=d_ba[0, 256:],
        gla_norm_g=d_gla_g[0], pool_w=d_pool_w.reshape(-1), pool_scale=d_pool_scale[0], mla_q_norm_g=d_qg[0],
        mla_kv_norm_g=d_kvg[0], mla_qk_norm_q=d_qng[0, :MLA_QK], mla_qk_norm_k=d_kng[0, :MLA_QK],
    )
    return dx, sharded, small, rode


SHARDED = ["w_in", "w_out", "mla_wq_b", "mla_wkv_b", "gla_wa2_f", "gla_wa2_b"]
WEIGHTS = ["norm_g", "w_in", "ret_norm_g", "gla_wa2_f", "gla_ba_f", "gla_wa2_b", "gla_ba_b", "gla_norm_g", "pool_w",
           "pool_scale", "mla_q_norm_g", "mla_wq_b", "mla_kv_norm_g", "mla_wkv_b", "mla_qk_norm_q", "mla_qk_norm_k",
           "w_out"]


SHARD_AXES = [1, 0, 0, 0, 0, 0]


def _layer_shards(p, l):
    return [jnp.swapaxes(p["w_in"], 1, 2)[l].astype(BF16), p["w_out"][l].astype(BF16), p["mla_wq_b"][l].astype(BF16),
            p["mla_wkv_b"][l].astype(BF16), p["gla_wa2_f"][l], p["gla_wa2_b"][l]]


def _step(p, where):
    x = p["x"][0]
    tabs = _rope_tables(x.shape[0])
    got0 = _gather_shards(_layer_shards(p, 0), SHARD_AXES, name="l0_gather_weights")
    w0 = _layer_weights(0, p, dict(zip(SHARDED, got0)))
    x1, sv0, got1 = _layer_fwd(0, x, w0, tabs, next_shards=_layer_shards(p, 1))
    w1 = _layer_weights(1, p, dict(zip(SHARDED, got1)))
    (dx, loss), sv1, _ = _layer_fwd(1, x1, w1, tabs, loss_target=p["loss_target"][0])

    big, big_axes = SHARDED[:2], SHARD_AXES[:2]

    def pair_sums(tag, tensors, axes, names):
        return [_pair_reduce(a, where, ax, out_dtype=BF16, name=f"{tag}_pair_reduce_{n}")
                for a, ax, n in zip(tensors, axes, names)]

    def joined(tag, pair, others, axes, names):
        return [_sum_join(a, b, where, ax, name=f"{tag}_sum_join_{n}")
                for a, b, ax, n in zip(pair, others, axes, names)]

    dx, sharded1, small1, _ = _layer_bwd(1, dx, w1, sv1, tabs)
    pair1 = pair_sums("l1", [sharded1[n] for n in big], big_axes, big)
    dx, sharded0, small0, others1 = _layer_bwd(0, dx, w0, sv0, tabs, riding_parts=pair1, where=where)
    grads1 = joined("l1", pair1, others1, big_axes, big)
    packed = jnp.concatenate([sh[n].reshape(N_CHIP, -1, 128) for sh in (sharded0, sharded1) for n in SHARDED[2:]],
                             axis=1)
    last, last_axes, last_names = [sharded0["w_in"], packed], [SHARD_AXES[0], 0], ["w_in", "rest"]
    pair0 = pair_sums("l0", last, last_axes, last_names)
    g_w_in0, rest = joined("l0", pair0, _chip_exchange(pair0, name="l0_chip_exchange"), last_axes, last_names)
    (g_w_out0,) = joined("l0", [sharded0["w_out"][0]], [sharded0["w_out"][1]], [SHARD_AXES[1]], ["w_out"])
    grads = {n: jnp.stack([g0, g1]) for n, g0, g1 in zip(big, (g_w_in0, g_w_out0), grads1)}
    off = 0
    pieces = {n: [] for n in SHARDED[2:]}
    for sh in (sharded0, sharded1):
        for n in SHARDED[2:]:
            rows = sh[n].shape[1] * sh[n].shape[2] // 128
            pieces[n].append(rest[off:off + rows].reshape(sh[n].shape[1:]))
            off += rows
    grads.update({n: jnp.stack(v) for n, v in pieces.items()})
    small = {n: [small0[n], small1[n]] for n, _ in SMALL}
    small["loss"] = loss
    return dx[None], grads, small


def kernel(x, norm_g, w_in, ret_norm_g, gla_wa2_f, gla_ba_f, gla_wa2_b, gla_ba_b, gla_norm_g, pool_w, pool_scale, mla_q_norm_g, mla_wq_b, mla_kv_norm_g, mla_wkv_b, mla_qk_norm_q, mla_qk_norm_k, w_out, loss_target, m_norm_g, m_w_in, m_ret_norm_g, m_gla_wa2_f, m_gla_ba_f, m_gla_wa2_b, m_gla_ba_b, m_gla_norm_g, m_pool_w, m_pool_scale, m_mla_q_norm_g, m_mla_wq_b, m_mla_kv_norm_g, m_mla_wkv_b, m_mla_qk_norm_q, m_mla_qk_norm_k, m_w_out, v_norm_g, v_w_in, v_ret_norm_g, v_gla_wa2_f, v_gla_ba_f, v_gla_wa2_b, v_gla_ba_b, v_gla_norm_g, v_pool_w, v_pool_scale, v_mla_q_norm_g, v_mla_wq_b, v_mla_kv_norm_g, v_mla_wkv_b, v_mla_qk_norm_q, v_mla_qk_norm_k, v_w_out):
    p = dict(x=x, norm_g=norm_g, w_in=w_in, ret_norm_g=ret_norm_g, gla_wa2_f=gla_wa2_f, gla_ba_f=gla_ba_f,
             gla_wa2_b=gla_wa2_b, gla_ba_b=gla_ba_b, gla_norm_g=gla_norm_g, pool_w=pool_w, pool_scale=pool_scale,
             mla_q_norm_g=mla_q_norm_g, mla_wq_b=mla_wq_b, mla_kv_norm_g=mla_kv_norm_g, mla_wkv_b=mla_wkv_b,
             mla_qk_norm_q=mla_qk_norm_q, mla_qk_norm_k=mla_qk_norm_k, w_out=w_out, loss_target=loss_target)
    moments = dict(
        m=dict(norm_g=m_norm_g, w_in=m_w_in, ret_norm_g=m_ret_norm_g, gla_wa2_f=m_gla_wa2_f, gla_ba_f=m_gla_ba_f,
               gla_wa2_b=m_gla_wa2_b, gla_ba_b=m_gla_ba_b, gla_norm_g=m_gla_norm_g, pool_w=m_pool_w,
               pool_scale=m_pool_scale, mla_q_norm_g=m_mla_q_norm_g, mla_wq_b=m_mla_wq_b,
               mla_kv_norm_g=m_mla_kv_norm_g, mla_wkv_b=m_mla_wkv_b, mla_qk_norm_q=m_mla_qk_norm_q,
               mla_qk_norm_k=m_mla_qk_norm_k, w_out=m_w_out),
        v=dict(norm_g=v_norm_g, w_in=v_w_in, ret_norm_g=v_ret_norm_g, gla_wa2_f=v_gla_wa2_f, gla_ba_f=v_gla_ba_f,
               gla_wa2_b=v_gla_wa2_b, gla_ba_b=v_gla_ba_b, gla_norm_g=v_gla_norm_g, pool_w=v_pool_w,
               pool_scale=v_pool_scale, mla_q_norm_g=v_mla_q_norm_g, mla_wq_b=v_mla_wq_b,
               mla_kv_norm_g=v_mla_kv_norm_g, mla_wkv_b=v_mla_wkv_b, mla_qk_norm_q=v_mla_qk_norm_q,
               mla_qk_norm_k=v_mla_qk_norm_k, w_out=v_w_out))

    where = jnp.stack([lax.axis_index("c"), 2 * lax.axis_index("x") + lax.axis_index("y")]).astype(jnp.int32)
    grad_x, grads, small = _step(p, where)

    slots = _gather_all(_pack_small(small), name="gather_small")
    total = _unpack_small(_sum_slots(slots, name="sum_small"))
    for n, _ in SMALL:
        grads[n] = total[n].reshape(p[n].shape)
    loss = total["loss"]

    delta, new_m, new_v = {}, {}, {}
    for n in WEIGHTS:
        turn = (lambda a: jnp.swapaxes(a, 1, 2)) if n == "w_in" else (lambda a: a)
        outs = _adamw(turn(p[n]), grads[n], turn(moments["m"][n]), turn(moments["v"][n]), name=f"adamw_{n}")
        grads[n] = turn(grads[n])
        delta[n], new_m[n], new_v[n] = (turn(o) for o in outs)
    return (loss, grad_x, *[grads[n] for n in WEIGHTS], *[delta[n] for n in WEIGHTS],
            *[new_m[n] for n in WEIGHTS], *[new_v[n] for n in WEIGHTS])
```

```python
import jax
import jax.numpy as jnp
from jax import lax
from jax.experimental import pallas as pl
from jax.experimental.pallas import tpu as pltpu

F32 = jnp.float32
BF16 = jnp.bfloat16
MESH = pl.DeviceIdType.MESH

EPS = 1e-6
ROPE_THETA = 10000.0
DEPTH = 2
N_DEV = 8
N_CHIP = 4

GROUP_W = 512
RET_HEADS = 4
RET_HD = 128
RET_CHUNK = 256
GLA_HEADS = 4
GLA_DK = 64
GLA_DV = 128
GLA_RANK = 16
GLA_TAU = 16.0
GLA_CHUNK = 64
POOL_GROUPS = 4
POOL_GW = 128
POOL_HALO = 8
POOL_TILE = 256
MLA_HEADS = 4
MLA_NOPE = 128
MLA_ROPE = 64
MLA_QK = MLA_NOPE + MLA_ROPE
MLA_QKP = 256
MLA_V = 128
MLA_Q_RANK = 512
MLA_KV_RANK = 256
MLA_SCALE = MLA_QK ** -0.5

ADAM_LR = 0.001
ADAM_B1 = 0.9
ADAM_B2 = 0.999
ADAM_EPS = 1e-08
ADAM_WD = 0.01
ADAM_STEP = 10

VMEM_LIMIT = 56 * 1024 * 1024
ROW_TILE = 512
GROUP_ROW_TILE = 1024

SEG = {
    "rq": (0, 512, 0, 512), "rk": (512, 512, 512, 512), "rv": (1024, 512, 1024, 512), "rg": (1536, 512, 1536, 512),
    "gv": (2048, 512, 2560, 512), "gg": (2560, 512, 3072, 512),
    "pv": (3072, 512, 3616, 512), "pg": (3584, 512, 4128, 512),
    "mq": (4096, 512, 4640, 512), "mg": (4608, 512, 5472, 512),
    "gq": (5120, 256, 2048, 256), "gk": (5376, 256, 2304, 256), "mkv": (5632, 256, 5152, 256),
    "ga": (5888, 128, 3584, 32), "mkr": (6016, 128, 5408, 64),
}
SEG_ORDER = ["rq", "rk", "rv", "rg", "gv", "gg", "pv", "pg", "mq", "mg", "gq", "gk", "mkv", "ga", "mkr"]
IN_COLS = 5984
IN_PAD = 6144
ORIG_ORDER = ["rq", "rk", "rv", "rg", "gq", "gk", "gv", "gg", "ga", "pv", "pg", "mq", "mkv", "mkr", "mg"]


def _cparams(*sem):
    return pltpu.CompilerParams(dimension_semantics=tuple(sem), vmem_limit_bytes=VMEM_LIMIT)


def _bf(v):
    return v.astype(BF16)


def _dot(a, b, ca=1, cb=0):
    return lax.dot_general(_bf(a), _bf(b), (((ca,), (cb,)), ((), ())), preferred_element_type=F32)


def _sigmoid(x):
    return 1.0 / (1.0 + jnp.exp(-x))


def _silu_parts(g):
    sg = _sigmoid(g)
    return g * sg, sg * (1.0 + g * (1.0 - sg))


class _Rider:
    def __init__(self, ins, outs, sems, start, finish, aliases=None):
        self.ins, self.outs, self.sems, self.start, self.finish = list(ins), list(outs), list(sems), start, finish
        self.aliases = dict(aliases or {})


def _ride(body, rider, n_in, n_out, grid):
    if rider is None:
        return body
    ri, ro, rs = len(rider.ins), len(rider.outs), len(rider.sems)

    def wrapped(*refs):
        ins, refs = refs[:n_in], refs[n_in:]
        rin, refs = refs[:ri], refs[ri:]
        outs, refs = refs[:n_out], refs[n_out:]
        rout, refs = refs[:ro], refs[ro:]
        scratch, sems = refs[:len(refs) - rs], refs[len(refs) - rs:]
        first = pl.program_id(0) == 0
        last = pl.program_id(0) == grid[0] - 1
        for ax in range(1, len(grid)):
            first = jnp.logical_and(first, pl.program_id(ax) == 0)
            last = jnp.logical_and(last, pl.program_id(ax) == grid[ax] - 1)

        @pl.when(first)
        def _():
            rider.start(rin, rout, sems)

        body(*ins, *outs, *scratch)

        @pl.when(last)
        def _():
            rider.finish(rin, rout, sems)

    return wrapped


def _ride_call(body, rider, *, name, grid, in_specs, out_specs, out_shape, scratch_shapes, args, sem):
    n_in, n_out = len(in_specs), len(out_specs)
    if rider is None:
        return pl.pallas_call(body, name=name, grid=grid, in_specs=in_specs, out_specs=out_specs, out_shape=out_shape,
                              scratch_shapes=scratch_shapes, compiler_params=_cparams(*sem))(*args), []
    outs = pl.pallas_call(
        _ride(body, rider, n_in, n_out, grid), name=name, grid=grid,
        in_specs=list(in_specs) + [ANY] * len(rider.ins), out_specs=list(out_specs) + [ANY] * len(rider.outs),
        out_shape=list(out_shape) + rider.outs, scratch_shapes=list(scratch_shapes) + rider.sems,
        input_output_aliases={n_in + i: n_out + o for i, o in rider.aliases.items()},
        compiler_params=_cparams(*(["arbitrary"] * len(grid))),
    )(*args, *rider.ins)
    return outs[:n_out], outs[n_out:]


def _matmul(a, b, *, ta=False, tb=False, out_dtype=F32, tm=512, tn=1024, tk=None, add=None, n_outer=True, rider=None,
            name):
    m, kdim = (a.shape[1], a.shape[0]) if ta else a.shape
    n = b.shape[0] if tb else b.shape[1]
    tm, tn = min(tm, m), min(tn, n)
    tk = kdim if tk is None else min(tk, kdim)
    assert m % tm == 0 and n % tn == 0 and kdim % tk == 0
    nk = kdim // tk
    ca, cb = (0 if ta else 1), (1 if tb else 0)

    def body(*refs):
        if add is None:
            a_ref, b_ref, o_ref = refs[:3]
            add_ref = None
        else:
            a_ref, b_ref, add_ref, o_ref = refs[:4]
        p = _dot(a_ref[...], b_ref[...], ca, cb)

        def finish(r):
            if add_ref is not None:
                r = r + add_ref[...]
            o_ref[...] = r.astype(out_dtype)

        if nk == 1:
            finish(p)
        else:
            acc = refs[-1]
            k = pl.program_id(2)

            @pl.when(k == 0)
            def _():
                acc[...] = p

            @pl.when(k > 0)
            def _():
                acc[...] += p

            @pl.when(k == nk - 1)
            def _():
                finish(acc[...])

    def ij(g0, g1):
        return (g1, g0) if n_outer else (g0, g1)

    a_spec = (pl.BlockSpec((tk, tm), lambda g0, g1, k: (k, ij(g0, g1)[0])) if ta
              else pl.BlockSpec((tm, tk), lambda g0, g1, k: (ij(g0, g1)[0], k)))
    b_spec = (pl.BlockSpec((tn, tk), lambda g0, g1, k: (ij(g0, g1)[1], k)) if tb
              else pl.BlockSpec((tk, tn), lambda g0, g1, k: (k, ij(g0, g1)[1])))
    o_spec = pl.BlockSpec((tm, tn), lambda g0, g1, k: ij(g0, g1))
    in_specs = [a_spec, b_spec] + ([o_spec] if add is not None else [])
    args = (a, b) + ((add,) if add is not None else ())
    grid = (n // tn, m // tm, nk) if n_outer else (m // tm, n // tn, nk)
    (out,), rode = _ride_call(
        body, rider, name=name, grid=grid, in_specs=in_specs, out_specs=[o_spec],
        out_shape=[jax.ShapeDtypeStruct((m, n), out_dtype)],
        scratch_shapes=[] if nk == 1 else [pltpu.VMEM((tm, tn), F32)], args=args,
        sem=("parallel", "parallel", "arbitrary"))
    return out if rider is None else (out, rode)


def _rmsnorm_fwd(x, g, *, name, tm=ROW_TILE):
    s, d = x.shape
    tm = min(tm, s)

    def body(x_ref, g_ref, h_ref):
        xv = x_ref[...]
        r = lax.rsqrt(jnp.mean(xv * xv, axis=-1, keepdims=True) + EPS)
        h_ref[...] = _bf(xv * r * g_ref[...])

    return pl.pallas_call(
        body, name=name, grid=(s // tm,),
        in_specs=[pl.BlockSpec((tm, d), lambda i: (i, 0)), pl.BlockSpec((1, d), lambda i: (0, 0))],
        out_specs=pl.BlockSpec((tm, d), lambda i: (i, 0)),
        out_shape=jax.ShapeDtypeStruct((s, d), BF16),
        compiler_params=_cparams("parallel"),
    )(x, g)


def _rmsnorm_bwd(x, dh, g, dres, *, name, tm=ROW_TILE):
    s, d = x.shape
    tm = min(tm, s)

    def body(x_ref, dh_ref, g_ref, dres_ref, dx_ref, dg_ref):
        i = pl.program_id(0)
        xv = x_ref[...]
        r = lax.rsqrt(jnp.mean(xv * xv, axis=-1, keepdims=True) + EPS)
        xn = xv * r
        dv = dh_ref[...]
        part = jnp.sum(dv * xn, axis=0, keepdims=True)

        @pl.when(i == 0)
        def _():
            dg_ref[...] = part

        @pl.when(i > 0)
        def _():
            dg_ref[...] += part

        dxn = dv * g_ref[...]
        dx_ref[...] = dres_ref[...] + r * (dxn - xn * jnp.mean(dxn * xn, axis=-1, keepdims=True))

    row = pl.BlockSpec((tm, d), lambda i: (i, 0))
    vec = pl.BlockSpec((1, d), lambda i: (0, 0))
    return pl.pallas_call(
        body, name=name, grid=(s // tm,), in_specs=[row, row, vec, row], out_specs=[row, vec],
        out_shape=[jax.ShapeDtypeStruct((s, d), F32), jax.ShapeDtypeStruct((1, d), F32)],
        compiler_params=_cparams("arbitrary"),
    )(x, dh, g, dres)


def _out_proj_loss(y, w, x, target, *, name, tm=512, tn=1024):
    m, kdim = y.shape
    n = w.shape[1]
    tm, tn = min(tm, m), min(tn, n)

    def body(y_ref, w_ref, x_ref, t_ref, dx_ref, l_ref):
        first = jnp.logical_and(pl.program_id(0) == 0, pl.program_id(1) == 0)
        e = _dot(y_ref[...], w_ref[...]) + x_ref[...] - t_ref[...]
        dx_ref[...] = e * (1.0 / n)
        part = (0.5 / n) * jnp.sum(jnp.sum(e * e, axis=-1, keepdims=True), axis=0, keepdims=True)

        @pl.when(first)
        def _():
            l_ref[...] = part

        @pl.when(jnp.logical_not(first))
        def _():
            l_ref[...] += part

    tile = pl.BlockSpec((tm, tn), lambda j, i: (i, j))
    return pl.pallas_call(
        body, name=name, grid=(n // tn, m // tm),
        in_specs=[pl.BlockSpec((tm, kdim), lambda j, i: (i, 0)), pl.BlockSpec((kdim, tn), lambda j, i: (0, j)),
                  tile, tile],
        out_specs=[tile, pl.BlockSpec((1, 1), lambda j, i: (0, 0))],
        out_shape=[jax.ShapeDtypeStruct((m, n), F32), jax.ShapeDtypeStruct((1, 1), F32)],
        compiler_params=_cparams("arbitrary", "arbitrary"),
    )(y, w, x, target)


def _rope_tables(s):
    pos = jnp.arange(s, dtype=F32)[:, None]
    inv_r = 1.0 / (ROPE_THETA ** (jnp.arange(0, RET_HD, 2, dtype=F32) / RET_HD))
    ang = pos * inv_r[None, :]
    ret_cos = jnp.concatenate([jnp.cos(ang), jnp.cos(ang)], axis=1)
    ret_sin = jnp.concatenate([-jnp.sin(ang), jnp.sin(ang)], axis=1)
    inv_m = 1.0 / (ROPE_THETA ** (jnp.arange(0, MLA_ROPE, 2, dtype=F32) / MLA_ROPE))
    am = pos * inv_m[None, :]
    z32, z64 = jnp.zeros((s, 32), F32), jnp.zeros((s, 64), F32)
    mla_cos = jnp.concatenate([jnp.cos(am), jnp.cos(am), z64], axis=1)
    mla_sp = jnp.concatenate([z32, jnp.sin(am), z64], axis=1)
    mla_sn = jnp.concatenate([-jnp.sin(am), z32, z64], axis=1)
    return ret_cos, ret_sin, mla_cos, mla_sp, mla_sn


def _rope128(x, c, sg):
    return x * c + pltpu.roll(x, 64, 1) * sg


def _unrope128(d, c, sg):
    return d * c + pltpu.roll(d * sg, 64, 1)


def _rope64(t, c, sp, sn):
    return t * c + pltpu.roll(t, 96, 1) * sn + pltpu.roll(t, 32, 1) * sp


def _unrope64(d, c, sp, sn):
    return d * c + pltpu.roll(d * sn, 32, 1) + pltpu.roll(d * sp, 96, 1)


def _ret_pre(z, cos, sin, *, name, tm=GROUP_ROW_TILE):
    s = z.shape[0]
    tm = min(tm, s)
    scale = RET_HD ** -0.5

    def body(q_ref, k_ref, c_ref, s_ref, qo_ref, ko_ref):
        c, sg = c_ref[...], s_ref[...]
        for h in range(RET_HEADS):
            sl = slice(h * RET_HD, (h + 1) * RET_HD)
            qo_ref[:, sl] = _rope128(q_ref[:, sl], c, sg)
            ko_ref[:, sl] = _rope128(k_ref[:, sl], c, sg) * scale

    seg = lambda j: pl.BlockSpec((tm, GROUP_W), lambda i: (i, j))
    tab = pl.BlockSpec((tm, RET_HD), lambda i: (i, 0))
    return pl.pallas_call(
        body, name=name, grid=(s // tm,), in_specs=[seg(0), seg(1), tab, tab],
        out_specs=[seg(0), seg(0)],
        out_shape=[jax.ShapeDtypeStruct((s, GROUP_W), F32)] * 2,
        compiler_params=_cparams("parallel"),
    )(z, z, cos, sin)


def _ret_pre_bwd(dqr, dkr, cos, sin, into, *, name, tm=GROUP_ROW_TILE):
    s = dqr[0].shape[0]
    tm = min(tm, s)
    scale = RET_HD ** -0.5

    def body(dq0_ref, dq1_ref, dk0_ref, dk1_ref, c_ref, s_ref, _, o_ref):
        c, sg = c_ref[...], s_ref[...]
        for h in range(RET_HEADS):
            sl = slice(h * RET_HD, (h + 1) * RET_HD)
            ksl = slice(GROUP_W + h * RET_HD, GROUP_W + (h + 1) * RET_HD)
            o_ref[:, sl] = _bf(_unrope128(dq0_ref[:, sl] + dq1_ref[:, sl], c, sg))
            o_ref[:, ksl] = _bf(_unrope128(dk0_ref[:, sl] + dk1_ref[:, sl], c, sg) * scale)

    row = pl.BlockSpec((tm, GROUP_W), lambda i: (i, 0))
    tab = pl.BlockSpec((tm, RET_HD), lambda i: (i, 0))
    out_shape, out_spec, more_specs, more_args = _landing(into, tm, 2 * GROUP_W)
    return pl.pallas_call(
        body, name=name, grid=(s // tm,), in_specs=[row, row, row, row, tab, tab] + more_specs, out_specs=out_spec,
        out_shape=out_shape, input_output_aliases={6: 0},
        compiler_params=_cparams("parallel"),
    )(dqr[0], dqr[1], dkr[0], dkr[1], cos, sin, *more_args)


def _bla(a, b, c, lg, cols, *, name):
    s = a.shape[0]
    ch = min(RET_CHUNK, s)
    n = s // ch
    hd = RET_HD

    def body(lg_ref, a0, b0, c0, a1, b1, c1, o0, o1, st):
        t = pl.program_id(0)

        @pl.when(t == 0)
        def _():
            st[...] = jnp.zeros_like(st)

        ii = lax.broadcasted_iota(jnp.int32, (ch, ch), 0)
        jj = lax.broadcasted_iota(jnp.int32, (ch, ch), 1)
        idx = lax.broadcasted_iota(jnp.int32, (ch, 1), 0).astype(F32)
        for d, (a_ref, b_ref, c_ref, o_ref) in enumerate(((a0, b0, c0, o0), (a1, b1, c1, o1))):
            diff = ((ii - jj) if d == 0 else (jj - ii)).astype(F32)
            keep = diff >= 0
            dpos = jnp.maximum(diff, 0.0)
            pq = (idx + 1.0) if d == 0 else (ch - idx)
            pk = (ch - 1.0 - idx) if d == 0 else idx
            for h in range(RET_HEADS):
                g = lg_ref[d, h]
                sl = slice(h * hd, (h + 1) * hd)
                av, bv, cv = a_ref[:, sl], b_ref[:, sl], c_ref[:, sl]
                sc = _dot(av, bv, 1, 1) * jnp.where(keep, jnp.exp(dpos * g), 0.0)
                stv = st[d, h]
                o_ref[:, sl] = _dot(sc, cv) + _dot(av * jnp.exp(pq * g), stv)
                st[d, h] = jnp.exp(ch * g) * stv + _dot(bv * jnp.exp(pk * g), cv, 0, 0)

    fwd = lambda j: pl.BlockSpec((ch, GROUP_W), lambda t: (t, j))
    bwd = lambda j: pl.BlockSpec((ch, GROUP_W), lambda t: (n - 1 - t, j))
    return pl.pallas_call(
        body, name=name, grid=(n,),
        in_specs=[pl.BlockSpec(memory_space=pltpu.SMEM), fwd(cols[0]), fwd(cols[1]), fwd(cols[2]),
                  bwd(cols[0]), bwd(cols[1]), bwd(cols[2])],
        out_specs=[fwd(0), bwd(0)],
        out_shape=[jax.ShapeDtypeStruct((s, GROUP_W), F32)] * 2,
        scratch_shapes=[pltpu.VMEM((2, RET_HEADS, hd, hd), F32)],
        compiler_params=_cparams("arbitrary"),
    )(lg, a, b, c, a, b, c)


def _bla_bwd(do, z, qr, kr, vcol, *, name):
    s = do.shape[0]
    ch = min(RET_CHUNK, s)
    n = s // ch
    hd = RET_HD
    lg = jnp.stack([_ret_log_gamma(False), _ret_log_gamma(True)])
    uses = ((0, 1, 3, 0), (1, 0, 2, 1), (3, 2, 0, 1))

    def body(lg_ref, do0, v0, q0, k0, do1, v1, q1, k1, dq0, dk0, dv0, dq1, dk1, dv1, st):
        t = pl.program_id(0)

        @pl.when(t == 0)
        def _():
            st[...] = jnp.zeros_like(st)

        ii = lax.broadcasted_iota(jnp.int32, (ch, ch), 0)
        jj = lax.broadcasted_iota(jnp.int32, (ch, ch), 1)
        idx = lax.broadcasted_iota(jnp.int32, (ch, 1), 0).astype(F32)
        for d, (ins, outs) in enumerate((((do0, v0, q0, k0), (dq0, dk0, dv0)), ((do1, v1, q1, k1), (dq1, dk1, dv1)))):
            diff = ((ii - jj) if d == 0 else (jj - ii)).astype(F32)
            keep = diff >= 0
            dpos = jnp.maximum(diff, 0.0)
            pq = (idx + 1.0) if d == 0 else (ch - idx)
            pk = (ch - 1.0 - idx) if d == 0 else idx
            for u, (ai, bi, ci, sw) in enumerate(uses):
                for h in range(RET_HEADS):
                    g = lg_ref[sw, d, h]
                    sl = slice(h * hd, (h + 1) * hd)
                    av, bv, cv = ins[ai][:, sl], ins[bi][:, sl], ins[ci][:, sl]
                    sc = _dot(av, bv, 1, 1) * jnp.where(keep, jnp.exp(dpos * g), 0.0)
                    stv = st[u, d, h]
                    outs[u][:, sl] = _dot(sc, cv) + _dot(av * jnp.exp(pq * g), stv)
                    st[u, d, h] = jnp.exp(ch * g) * stv + _dot(bv * jnp.exp(pk * g), cv, 0, 0)

    fwd = lambda j: pl.BlockSpec((ch, GROUP_W), lambda t: (t, j))
    bwd = lambda j: pl.BlockSpec((ch, GROUP_W), lambda t: (n - 1 - t, j))
    outs = pl.pallas_call(
        body, name=name, grid=(n,),
        in_specs=[pl.BlockSpec(memory_space=pltpu.SMEM), fwd(0), fwd(vcol), fwd(0), fwd(0),
                  bwd(0), bwd(vcol), bwd(0), bwd(0)],
        out_specs=[fwd(0)] * 3 + [bwd(0)] * 3,
        out_shape=[jax.ShapeDtypeStruct((s, GROUP_W), F32)] * 6,
        scratch_shapes=[pltpu.VMEM((3, 2, RET_HEADS, hd, hd), F32)],
        compiler_params=_cparams("arbitrary"),
    )(lg, do, z, qr, kr, do, z, qr, kr)
    return (outs[0], outs[3]), (outs[1], outs[4]), (outs[2], outs[5])


def _post(os_, zg, gcol, g, *, norm, name, tm=GROUP_ROW_TILE):
    s = zg.shape[0]
    tm = min(tm, s)
    nd = len(os_)

    def body(*refs):
        o_refs, (gt_ref, g_ref, y_ref) = refs[:nd], refs[nd:]
        silu, _ = _silu_parts(gt_ref[...])
        for h in range(4):
            sl = slice(h * 128, (h + 1) * 128)
            o = o_refs[0][:, sl]
            for k in range(1, nd):
                o = o + o_refs[k][:, sl]
            if norm:
                r = lax.rsqrt(jnp.mean(o * o, axis=-1, keepdims=True) + EPS)
                o = o * r * g_ref[:, sl]
            y_ref[:, sl] = _bf(silu[:, sl] * o)

    row = pl.BlockSpec((tm, GROUP_W), lambda i: (i, 0))
    return pl.pallas_call(
        body, name=name, grid=(s // tm,),
        in_specs=[row] * nd + [pl.BlockSpec((tm, GROUP_W), lambda i: (i, gcol)),
                               pl.BlockSpec((1, GROUP_W), lambda i: (0, 0))],
        out_specs=row,
        out_shape=jax.ShapeDtypeStruct((s, GROUP_W), BF16),
        compiler_params=_cparams("parallel"),
    )(*os_, zg, g)


def _post_bwd(dy, ycol, os_, zg, gcol, g, into, *, norm, name, tm=GROUP_ROW_TILE):
    s = zg.shape[0]
    tm = min(tm, s)
    nd = len(os_)

    def body(*refs):
        dy_ref, o_refs = refs[0], refs[1:1 + nd]
        gt_ref, g_ref, _, dgt_ref, do_ref, dg_ref = refs[1 + nd:]
        i = pl.program_id(0)
        silu, dsilu = _silu_parts(gt_ref[...])
        dyv = dy_ref[...]
        parts = []
        for h in range(4):
            sl = slice(h * 128, (h + 1) * 128)
            o = o_refs[0][:, sl]
            for k in range(1, nd):
                o = o + o_refs[k][:, sl]
            dn = dyv[:, sl] * silu[:, sl]
            if norm:
                r = lax.rsqrt(jnp.mean(o * o, axis=-1, keepdims=True) + EPS)
                xn = o * r
                gh = g_ref[:, sl]
                dgt_ref[:, sl] = _bf(dyv[:, sl] * (xn * gh) * dsilu[:, sl])
                parts.append(jnp.sum(dn * xn, axis=0, keepdims=True))
                dxn = dn * gh
                do_ref[:, sl] = r * (dxn - xn * jnp.mean(dxn * xn, axis=-1, keepdims=True))
            else:
                dgt_ref[:, sl] = _bf(dyv[:, sl] * o * dsilu[:, sl])
                parts.append(jnp.zeros((1, 128), F32))
                do_ref[:, sl] = dn
        part = jnp.concatenate(parts, axis=1)

        @pl.when(i == 0)
        def _():
            dg_ref[...] = part

        @pl.when(i > 0)
        def _():
            dg_ref[...] += part

    row = pl.BlockSpec((tm, GROUP_W), lambda i: (i, 0))
    vec = pl.BlockSpec((1, GROUP_W), lambda i: (0, 0))
    dgt_shape, dgt_spec, more_specs, more_args = _landing(into, tm, GROUP_W)
    n_in = nd + 3
    return pl.pallas_call(
        body, name=name, grid=(s // tm,),
        in_specs=[pl.BlockSpec((tm, GROUP_W), lambda i: (i, ycol))] + [row] * nd
        + [pl.BlockSpec((tm, GROUP_W), lambda i: (i, gcol)), vec] + more_specs,
        out_specs=[dgt_spec, row, vec],
        out_shape=[dgt_shape, jax.ShapeDtypeStruct((s, GROUP_W), F32), jax.ShapeDtypeStruct((1, GROUP_W), F32)],
        input_output_aliases={n_in: 0},
        compiler_params=_cparams("arbitrary"),
    )(dy, *os_, zg, g, *more_args)


def _ret_log_gamma(swap):
    gf = 1.0 - 2.0 ** (-5.0 - jnp.arange(RET_HEADS, dtype=F32))
    lf, lb = jnp.log(gf), jnp.log(gf[::-1])
    return jnp.stack([lb, lf] if swap else [lf, lb])


def _log_sigmoid(x):
    return jnp.minimum(x, 0.0) - jnp.log(1.0 + jnp.exp(-jnp.abs(x)))


def _gla_gate(z, wa, ba, *, name, tm=GROUP_ROW_TILE):
    s = z.shape[0]
    tm = min(tm, s)
    col = SEG["ga"][0] // 128

    def body(ga_ref, wa_ref, ba_ref, la_ref):
        pre = _dot(ga_ref[...], wa_ref[...]) + ba_ref[...]
        la_ref[...] = _log_sigmoid(pre) / GLA_TAU

    return pl.pallas_call(
        body, name=name, grid=(s // tm,),
        in_specs=[pl.BlockSpec((tm, 128), lambda i: (i, col)), pl.BlockSpec((128, 512), lambda i: (0, 0)),
                  pl.BlockSpec((1, 512), lambda i: (0, 0))],
        out_specs=pl.BlockSpec((tm, 512), lambda i: (i, 0)),
        out_shape=jax.ShapeDtypeStruct((s, 512), F32),
        compiler_params=_cparams("parallel"),
    )(z, wa, ba)


def _gla_gate_bwd(dla, z, wa, ba, into, *, name, tm=GROUP_ROW_TILE):
    s = z.shape[0]
    tm = min(tm, s)
    col = SEG["ga"][0] // 128

    def body(dla0_ref, dla1_ref, ga_ref, wa_ref, ba_ref, _, dga_ref, dwa_ref, dba_ref):
        i = pl.program_id(0)
        gav = ga_ref[...]
        pre = _dot(gav, wa_ref[...]) + ba_ref[...]
        dla_v = jnp.concatenate([dla0_ref[...], dla1_ref[...]], axis=1)
        dpre = dla_v * (1.0 - _sigmoid(pre)) * (1.0 / GLA_TAU)
        dga_ref[...] = _bf(_dot(dpre, wa_ref[...], 1, 1))
        pw = _dot(gav, dpre, 0, 0)
        pb = jnp.sum(dpre, axis=0, keepdims=True)

        @pl.when(i == 0)
        def _():
            dwa_ref[...] = pw
            dba_ref[...] = pb

        @pl.when(i > 0)
        def _():
            dwa_ref[...] += pw
            dba_ref[...] += pb

    dga_shape, dga_spec, more_specs, more_args = _landing(into, tm, 128)
    return pl.pallas_call(
        body, name=name, grid=(s // tm,),
        in_specs=[pl.BlockSpec((tm, 256), lambda i: (i, 0)), pl.BlockSpec((tm, 256), lambda i: (i, 0)),
                  pl.BlockSpec((tm, 128), lambda i: (i, col)),
                  pl.BlockSpec((128, 512), lambda i: (0, 0)), pl.BlockSpec((1, 512), lambda i: (0, 0))] + more_specs,
        out_specs=[dga_spec, pl.BlockSpec((128, 512), lambda i: (0, 0)), pl.BlockSpec((1, 512), lambda i: (0, 0))],
        out_shape=[dga_shape, jax.ShapeDtypeStruct((128, 512), F32), jax.ShapeDtypeStruct((1, 512), F32)],
        input_output_aliases={5: 0},
        compiler_params=_cparams("arbitrary"),
    )(dla[0], dla[1], z, wa, ba, *more_args)


def _gla_masks(ch):
    ii = lax.broadcasted_iota(jnp.int32, (ch, ch), 0)
    tt = lax.broadcasted_iota(jnp.int32, (ch, ch), 1)
    return jnp.where(tt <= ii, 1.0, 0.0), jnp.where(tt >= ii, 1.0, 0.0)


def _running_sum(x, up):
    n = x.shape[0]
    rows = lax.broadcasted_iota(jnp.int32, x.shape, 0)
    k = 1
    while k < n:
        if up:
            x = x + jnp.where(rows < n - k, pltpu.roll(x, n - k, 0), 0.0)
        else:
            x = x + jnp.where(rows >= k, pltpu.roll(x, k, 0), 0.0)
        k *= 2
    return x


def _gla_chunk(d, tmat, qv, kv, lav, ch):
    c = _running_sum(lav, up=(d == 1))
    big_l = c[ch - 1:ch, :] if d == 0 else c[0:1, :]
    qt = qv * (GLA_DK ** -0.5) * jnp.exp(c)
    kt = kv * jnp.exp(-c)
    kh = kv * jnp.exp(big_l - c)
    return c, big_l, qt, kt, kh


def _gla_fwd(qh, kh_, z, la, *, name, rider=None):
    s = z.shape[0]
    ch = min(GLA_CHUNK, s)
    n = s // ch
    vcol = SEG["gv"][0] // GROUP_W

    def body(q0, k0, v0, la0, q1, k1, v1, la1, o0, o1, zs0, zs1, st):
        t = pl.program_id(0)

        @pl.when(t == 0)
        def _():
            st[...] = jnp.zeros_like(st)

        masks = _gla_masks(ch)
        for d, (q_ref, k_ref, v_ref, la_ref, o_ref, zs_ref) in enumerate(
                ((q0, k0, v0, la0, o0, zs0), (q1, k1, v1, la1, o1, zs1))):
            for h in range(GLA_HEADS):
                c, big_l, qt, kt, kh = _gla_chunk(d, masks[d], q_ref[h], k_ref[h], la_ref[0, h], ch)
                vv = v_ref[:, h * GLA_DV:(h + 1) * GLA_DV]
                p = _dot(qt, kt, 1, 1) * masks[d]
                zst = st[d, h]
                o_ref[:, h * GLA_DV:(h + 1) * GLA_DV] = _dot(p, vv) + _dot(qt, zst, 1, 1)
                zs_ref[h, 0] = zst
                st[d, h] = zst * jnp.exp(big_l) + _dot(vv, kh, 0, 0)

    cidx = (lambda t: t), (lambda t: n - 1 - t)
    hs = lambda d: pl.BlockSpec((GLA_HEADS, ch, GLA_DK), lambda t: (0, cidx[d](t), 0))
    vs = lambda d: pl.BlockSpec((ch, GROUP_W), lambda t: (cidx[d](t), vcol))
    las = lambda d: pl.BlockSpec((1, GLA_HEADS, ch, GLA_DK), lambda t: (d, 0, cidx[d](t), 0))
    os_ = lambda d: pl.BlockSpec((ch, GROUP_W), lambda t: (cidx[d](t), 0))
    zss = lambda d: pl.BlockSpec((GLA_HEADS, 1, GLA_DV, GLA_DK), lambda t: (0, cidx[d](t), 0, 0))
    (o0, o1, zs0, zs1), rode = _ride_call(
        body, rider, name=name, grid=(n,),
        in_specs=[hs(0), hs(0), vs(0), las(0), hs(1), hs(1), vs(1), las(1)],
        out_specs=[os_(0), os_(1), zss(0), zss(1)],
        out_shape=[jax.ShapeDtypeStruct((s, GROUP_W), F32)] * 2
        + [jax.ShapeDtypeStruct((GLA_HEADS, n, GLA_DV, GLA_DK), F32)] * 2,
        scratch_shapes=[pltpu.VMEM((2, GLA_HEADS, GLA_DV, GLA_DK), F32)],
        args=(qh, kh_, z, la, qh, kh_, z, la), sem=("arbitrary",))
    return ((o0, o1), (zs0, zs1)) if rider is None else ((o0, o1), (zs0, zs1), rode)


def _gla_bwd(qh, kh_, z, la, do, zs, *, name, rider=None):
    s = z.shape[0]
    ch = min(GLA_CHUNK, s)
    n = s // ch
    vcol = SEG["gv"][0] // GROUP_W

    def body(q0, k0, v0, la0, do0, zs0, q1, k1, v1, la1, do1, zs1,
             dq0, dk0, dla0, dv0, dq1, dk1, dla1, dv1, gz):
        t = pl.program_id(0)

        @pl.when(t == 0)
        def _():
            gz[...] = jnp.zeros_like(gz)

        masks = _gla_masks(ch)
        rows = lax.broadcasted_iota(jnp.int32, (ch, 1), 0)
        for d, (q_ref, k_ref, v_ref, la_ref, do_ref, zs_ref, dq_ref, dk_ref, dla_ref, dv_ref) in enumerate(
                ((q0, k0, v0, la0, do0, zs0, dq0, dk0, dla0, dv0), (q1, k1, v1, la1, do1, zs1, dq1, dk1, dla1, dv1))):
            tmat = masks[d]
            end = ch - 1 if d == 0 else 0
            for h in range(GLA_HEADS):
                ksl = slice(h * GLA_DK, (h + 1) * GLA_DK)
                c, big_l, qt, kt, kh = _gla_chunk(d, tmat, q_ref[h], k_ref[h], la_ref[0, h], ch)
                vsl = slice(h * GLA_DV, (h + 1) * GLA_DV)
                vv, dov, zst, gzv = v_ref[:, vsl], do_ref[:, vsl], zs_ref[h, 0], gz[d, h]
                p = _dot(qt, kt, 1, 1) * tmat
                dp = _dot(dov, vv, 1, 1) * tmat
                dqt = _dot(dp, kt) + _dot(dov, zst)
                dkt = _dot(dp, qt, 0, 0)
                dkh = _dot(vv, gzv)
                dv_ref[:, vsl] = _dot(p, dov, 0, 0) + _dot(kh, gzv, 1, 1)
                dq_ref[:, ksl] = dqt * jnp.exp(c) * (GLA_DK ** -0.5)
                dk_ref[:, ksl] = dkt * jnp.exp(-c) + dkh * jnp.exp(big_l - c)
                e_l = jnp.exp(big_l)
                d_l = jnp.sum(dkh * kh, axis=0, keepdims=True) + e_l * jnp.sum(zst * gzv, axis=0, keepdims=True)
                dc = dqt * qt - dkt * kt - dkh * kh + jnp.where(rows == end, d_l, 0.0)
                dla_ref[:, ksl] = _running_sum(dc, up=(d == 0))
                gz[d, h] = gzv * e_l + _dot(dov, qt, 0, 0)

    cidx = (lambda t: n - 1 - t), (lambda t: t)
    hs = lambda d: pl.BlockSpec((GLA_HEADS, ch, GLA_DK), lambda t: (0, cidx[d](t), 0))
    vs = lambda d: pl.BlockSpec((ch, GROUP_W), lambda t: (cidx[d](t), vcol))
    las = lambda d: pl.BlockSpec((1, GLA_HEADS, ch, GLA_DK), lambda t: (d, 0, cidx[d](t), 0))
    row = lambda d: pl.BlockSpec((ch, GROUP_W), lambda t: (cidx[d](t), 0))
    zss = lambda d: pl.BlockSpec((GLA_HEADS, 1, GLA_DV, GLA_DK), lambda t: (0, cidx[d](t), 0, 0))
    kw = GLA_HEADS * GLA_DK
    ks = lambda d: pl.BlockSpec((ch, kw), lambda t: (cidx[d](t), 0))
    hshape = jax.ShapeDtypeStruct((s, kw), F32)
    wide = jax.ShapeDtypeStruct((s, GROUP_W), F32)
    outs, rode = _ride_call(
        body, rider, name=name, grid=(n,),
        in_specs=[hs(0), hs(0), vs(0), las(0), row(0), zss(0), hs(1), hs(1), vs(1), las(1), row(1), zss(1)],
        out_specs=[ks(0), ks(0), ks(0), row(0), ks(1), ks(1), ks(1), row(1)],
        out_shape=[hshape, hshape, hshape, wide, hshape, hshape, hshape, wide],
        scratch_shapes=[pltpu.VMEM((2, GLA_HEADS, GLA_DV, GLA_DK), F32)],
        args=(qh, kh_, z, la, do, zs[0], qh, kh_, z, la, do, zs[1]), sem=("arbitrary",))
    dq0, dk0, dla0, dv0, dq1, dk1, dla1, dv1 = outs
    res = ((dq0, dq1), (dk0, dk1), (dla0, dla1), (dv0, dv1))
    return res if rider is None else res + (rode,)


def _window_sums(win, g, shift):
    n = win.shape[0]
    levels, y = [], win
    for j in range(POOL_GROUPS):
        y = y + pltpu.roll(y, n - (1 << j), 0)
        levels.append(y)
    sums = levels[-1]
    for j in range(POOL_GROUPS - 2, -1, -1):
        sums = jnp.where(g == j, levels[j], sums)
    return pltpu.roll(sums, shift, 0)


def _pool_cnt(t0, half, rows, s):
    t = t0 + lax.broadcasted_iota(jnp.int32, (rows, 1), 0)
    return (jnp.minimum(t + half, s) - jnp.maximum(t - half, 0)).astype(F32)


def _pool_fwd(z, pw, scale, *, name):
    s = z.shape[0]
    tl = min(POOL_TILE, s)
    nt = s // tl
    ucol, gcol = SEG["pv"][0] // 128, SEG["pg"][0] // 128

    def body(u_ref, gt_ref, pw_ref, sc_ref, y_ref, pad):
        g = pl.program_id(0)
        half = jnp.left_shift(1, g)
        pad[0:POOL_HALO, :] = jnp.zeros((POOL_HALO, POOL_GW), F32)
        pad[POOL_HALO + s:POOL_HALO + s + POOL_HALO, :] = jnp.zeros((POOL_HALO, POOL_GW), F32)
        pad[POOL_HALO:POOL_HALO + s, :] = u_ref[...]
        pwv, scv = pw_ref[0], sc_ref[...]

        def tile(i, carry):
            t0 = pl.multiple_of(i * tl, tl)
            win = pad[pl.ds(t0, tl + 2 * POOL_HALO), :]
            u = win[POOL_HALO:POOL_HALO + tl, :]
            pooled = _window_sums(win, g, half)[POOL_HALO:POOL_HALO + tl, :] / _pool_cnt(t0, half, tl, s) - u
            mixed = _dot(pooled, pwv)
            silu, _ = _silu_parts(gt_ref[pl.ds(t0, tl), :])
            y_ref[pl.ds(t0, tl), :] = _bf(silu * (mixed * scv))
            return carry

        lax.fori_loop(0, nt, tile, 0)

    return pl.pallas_call(
        body, name=name, grid=(POOL_GROUPS,),
        in_specs=[pl.BlockSpec((s, POOL_GW), lambda g: (0, ucol + g)),
                  pl.BlockSpec((s, POOL_GW), lambda g: (0, gcol + g)),
                  pl.BlockSpec((1, POOL_GW, POOL_GW), lambda g: (g, 0, 0)),
                  pl.BlockSpec((1, POOL_GW), lambda g: (0, g))],
        out_specs=pl.BlockSpec((s, POOL_GW), lambda g: (0, g)),
        out_shape=jax.ShapeDtypeStruct((s, GROUP_W), BF16),
        scratch_shapes=[pltpu.VMEM((s + 2 * POOL_HALO, POOL_GW), F32)],
        compiler_params=_cparams("parallel"),
    )(z, z, pw, scale)


def _pool_bwd(dy, z, pw, scale, *, name):
    s = z.shape[0]
    tl = min(POOL_TILE, s)
    nt = s // tl
    ucol, gcol, ycol = SEG["pv"][0] // 128, SEG["pg"][0] // 128, 2 * GROUP_W // 128

    def body(dy_ref, u_ref, gt_ref, pw_ref, sc_ref, du_ref, dgt_ref, dpw_ref, dsc_ref, pad, epad, dpo):
        g = pl.program_id(0)
        half = jnp.left_shift(1, g)
        zeros = jnp.zeros((POOL_HALO, POOL_GW), F32)
        for buf in (pad, epad):
            buf[0:POOL_HALO, :] = zeros
            buf[POOL_HALO + s:POOL_HALO + s + POOL_HALO, :] = zeros
        pad[POOL_HALO:POOL_HALO + s, :] = u_ref[...]
        pwv, scv = pw_ref[0], sc_ref[...]
        dpw_ref[0] = jnp.zeros((POOL_GW, POOL_GW), F32)
        dsc_ref[...] = jnp.zeros((1, POOL_GW), F32)

        def tile(i, carry):
            t0 = pl.multiple_of(i * tl, tl)
            win = pad[pl.ds(t0, tl + 2 * POOL_HALO), :]
            u = win[POOL_HALO:POOL_HALO + tl, :]
            cnt = _pool_cnt(t0, half, tl, s)
            pooled = _window_sums(win, g, half)[POOL_HALO:POOL_HALO + tl, :] / cnt - u
            mixed = _dot(pooled, pwv)
            silu, dsilu = _silu_parts(gt_ref[pl.ds(t0, tl), :])
            dyv = dy_ref[pl.ds(t0, tl), :]
            dgt_ref[pl.ds(t0, tl), :] = _bf(dyv * (mixed * scv) * dsilu)
            dsc_ref[...] += jnp.sum(dyv * silu * mixed, axis=0, keepdims=True)
            dm = dyv * silu * scv
            dpw_ref[0] += _dot(pooled, dm, 0, 0)
            dpooled = _dot(dm, pwv, 1, 1)
            dpo[pl.ds(t0, tl), :] = dpooled
            epad[pl.ds(POOL_HALO + t0, tl), :] = dpooled / cnt
            return carry

        lax.fori_loop(0, nt, tile, 0)

        def tile2(i, carry):
            t0 = pl.multiple_of(i * tl, tl)
            ewin = epad[pl.ds(t0, tl + 2 * POOL_HALO), :]
            du_ref[pl.ds(t0, tl), :] = _bf(_window_sums(ewin, g, half - 1)[POOL_HALO:POOL_HALO + tl, :]
                                           - dpo[pl.ds(t0, tl), :])
            return carry

        lax.fori_loop(0, nt, tile2, 0)

    col = lambda c0: pl.BlockSpec((s, POOL_GW), lambda g: (0, c0 + g))
    return pl.pallas_call(
        body, name=name, grid=(POOL_GROUPS,),
        in_specs=[col(ycol), col(ucol), col(gcol), pl.BlockSpec((1, POOL_GW, POOL_GW), lambda g: (g, 0, 0)),
                  pl.BlockSpec((1, POOL_GW), lambda g: (0, g))],
        out_specs=[col(0), col(0), pl.BlockSpec((1, POOL_GW, POOL_GW), lambda g: (g, 0, 0)),
                   pl.BlockSpec((1, POOL_GW), lambda g: (0, g))],
        out_shape=[jax.ShapeDtypeStruct((s, GROUP_W), BF16), jax.ShapeDtypeStruct((s, GROUP_W), BF16),
                   jax.ShapeDtypeStruct((POOL_GROUPS, POOL_GW, POOL_GW), F32),
                   jax.ShapeDtypeStruct((1, GROUP_W), F32)],
        scratch_shapes=[pltpu.VMEM((s + 2 * POOL_HALO, POOL_GW), F32), pltpu.VMEM((s + 2 * POOL_HALO, POOL_GW), F32),
                        pltpu.VMEM((s, POOL_GW), F32)],
        compiler_params=_cparams("parallel"),
    )(dy, z, z, pw, scale)


def _mla_specs(tm):
    zq = pl.BlockSpec((tm, 512), lambda i: (i, SEG["mq"][0] // 512))
    zkv = pl.BlockSpec((tm, 256), lambda i: (i, SEG["mkv"][0] // 256))
    zkr = pl.BlockSpec((tm, 128), lambda i: (i, SEG["mkr"][0] // 128))
    full = lambda r, c: pl.BlockSpec((r, c), lambda i: (0, 0))
    tab = pl.BlockSpec((tm, 128), lambda i: (i, 0))
    weights = [full(1, 512), full(512, 1024), full(1, 256), full(256, 1024), full(1, 256), full(1, 256)]
    return [zq, zkv, zkr] + weights + [tab, tab, tab]


def _mla_project(xq_ref, xkv_ref, qg_ref, wq_ref, kvg_ref, wkv_ref):
    xq = xq_ref[...]
    r1 = lax.rsqrt(jnp.mean(xq * xq, axis=-1, keepdims=True) + EPS)
    xn1 = xq * r1
    qn = _bf(xn1 * qg_ref[...])
    qraw = _dot(qn, wq_ref[...])
    xkv = xkv_ref[...]
    r2 = lax.rsqrt(jnp.mean(xkv * xkv, axis=-1, keepdims=True) + EPS)
    xn2 = xkv * r2
    kvn = _bf(xn2 * kvg_ref[...])
    kvraw = _dot(kvn, wkv_ref[...])
    return r1, xn1, qn, qraw, r2, xn2, kvn, kvraw


def _mla_pre(z, qg, wq, kvg, wkv, qng, kng, cos, sp, sn, *, name, tm=ROW_TILE):
    s = z.shape[0]
    tm = min(tm, s)

    def body(xq_ref, xkv_ref, pe_ref, qg_ref, wq_ref, kvg_ref, wkv_ref, qng_ref, kng_ref, c_ref, sp_ref, sn_ref,
             q_ref, k_ref, v_ref):
        _, _, _, qraw, _, _, _, kvraw = _mla_project(xq_ref, xkv_ref, qg_ref, wq_ref, kvg_ref, wkv_ref)
        c, spv, snv = c_ref[...], sp_ref[...], sn_ref[...]
        pe = pe_ref[...]
        pe_ss = jnp.sum(pe * pe, axis=-1, keepdims=True)
        qngv, kngv = qng_ref[...], kng_ref[...]
        for h in range(MLA_HEADS):
            b = h * MLA_QKP
            qh = qraw[:, b:b + MLA_QKP]
            r = lax.rsqrt(jnp.sum(qh * qh, axis=-1, keepdims=True) * (1.0 / MLA_QK) + EPS)
            qn_h = qh * r * qngv
            q_ref[:, b:b + 128] = _bf(qn_h[:, :128] * MLA_SCALE)
            q_ref[:, b + 128:b + 256] = _bf(_rope64(qn_h[:, 128:], c, spv, snv) * MLA_SCALE)
            kn = kvraw[:, b:b + 128]
            rk = lax.rsqrt((jnp.sum(kn * kn, axis=-1, keepdims=True) + pe_ss) * (1.0 / MLA_QK) + EPS)
            k_ref[:, b:b + 128] = _bf(kn * rk * kngv[:, :128])
            k_ref[:, b + 128:b + 256] = _bf(_rope64(pe * rk * kngv[:, 128:], c, spv, snv))
            v_ref[:, h * MLA_V:(h + 1) * MLA_V] = _bf(kvraw[:, b + 128:b + 256])

    row = lambda w: pl.BlockSpec((tm, w), lambda i: (i, 0))
    return pl.pallas_call(
        body, name=name, grid=(s // tm,), in_specs=_mla_specs(tm),
        out_specs=[row(1024), row(1024), row(512)],
        out_shape=[jax.ShapeDtypeStruct((s, 1024), BF16), jax.ShapeDtypeStruct((s, 1024), BF16),
                   jax.ShapeDtypeStruct((s, 512), BF16)],
        compiler_params=_cparams("parallel"),
    )(z, z, z, qg, wq, kvg, wkv, qng, kng, cos, sp, sn)


def _mla_pre_bwd(dq, dk, dv, z, qg, wq, kvg, wkv, qng, kng, cos, sp, sn, *, name, tm=ROW_TILE):
    s = z.shape[0]
    tm = min(tm, s)

    def body(dq_ref, dk_ref, dv_ref, xq_ref, xkv_ref, pe_ref, qg_ref, wq_ref, kvg_ref, wkv_ref, qng_ref, kng_ref,
             c_ref, sp_ref, sn_ref, dxq_ref, dxkv_ref, dpe_ref, dwq_ref, dwkv_ref, dqg_ref, dkvg_ref, dqng_ref,
             dkng_ref, dqraw, dkvraw):
        i = pl.program_id(0)
        r1, xn1, qn, qraw, r2, xn2, kvn, kvraw = _mla_project(xq_ref, xkv_ref, qg_ref, wq_ref, kvg_ref, wkv_ref)
        c, spv, snv = c_ref[...], sp_ref[...], sn_ref[...]
        pe = pe_ref[...]
        pe_ss = jnp.sum(pe * pe, axis=-1, keepdims=True)
        qngv, kngv = qng_ref[...], kng_ref[...]
        dqng = jnp.zeros((1, MLA_QKP), F32)
        dkng = jnp.zeros((1, MLA_QKP), F32)
        dpe = jnp.zeros_like(pe)
        for h in range(MLA_HEADS):
            b = h * MLA_QKP
            qh = qraw[:, b:b + MLA_QKP]
            r = lax.rsqrt(jnp.sum(qh * qh, axis=-1, keepdims=True) * (1.0 / MLA_QK) + EPS)
            xn = qh * r
            d_n = jnp.concatenate(
                [dq_ref[:, b:b + 128], _unrope64(dq_ref[:, b + 128:b + 256], c, spv, snv)], axis=1) * MLA_SCALE
            dqng = dqng + jnp.sum(d_n * xn, axis=0, keepdims=True)
            dxn = d_n * qngv
            dqraw[:, b:b + MLA_QKP] = _bf(r * (dxn - xn * (jnp.sum(dxn * xn, axis=-1, keepdims=True) * (1.0 / MLA_QK))))
            kn = kvraw[:, b:b + 128]
            rk = lax.rsqrt((jnp.sum(kn * kn, axis=-1, keepdims=True) + pe_ss) * (1.0 / MLA_QK) + EPS)
            xk = jnp.concatenate([kn, pe], axis=1) * rk
            d_k = jnp.concatenate(
                [dk_ref[:, b:b + 128], _unrope64(dk_ref[:, b + 128:b + 256], c, spv, snv)], axis=1)
            dkng = dkng + jnp.sum(d_k * xk, axis=0, keepdims=True)
            dxk = d_k * kngv
            dfull = rk * (dxk - xk * (jnp.sum(dxk * xk, axis=-1, keepdims=True) * (1.0 / MLA_QK)))
            dkvraw[:, b:b + 128] = _bf(dfull[:, :128])
            dkvraw[:, b + 128:b + 256] = _bf(dv_ref[:, h * MLA_V:(h + 1) * MLA_V])
            dpe = dpe + dfull[:, 128:]
        dpe_ref[...] = _bf(dpe)
        dqr, dkvr = dqraw[...], dkvraw[...]
        dqn = _dot(dqr, wq_ref[...], 1, 1)
        dxn1 = dqn * qg_ref[...]
        dxq_ref[...] = _bf(r1 * (dxn1 - xn1 * jnp.mean(dxn1 * xn1, axis=-1, keepdims=True)))
        dkvn = _dot(dkvr, wkv_ref[...], 1, 1)
        dxn2 = dkvn * kvg_ref[...]
        dxkv_ref[...] = _bf(r2 * (dxn2 - xn2 * jnp.mean(dxn2 * xn2, axis=-1, keepdims=True)))
        parts = (_dot(qn, dqr, 0, 0), _dot(kvn, dkvr, 0, 0), jnp.sum(dqn * xn1, axis=0, keepdims=True),
                 jnp.sum(dkvn * xn2, axis=0, keepdims=True), dqng, dkng)
        accs = (dwq_ref, dwkv_ref, dqg_ref, dkvg_ref, dqng_ref, dkng_ref)

        @pl.when(i == 0)
        def _():
            for a, p in zip(accs, parts):
                a[...] = p

        @pl.when(i > 0)
        def _():
            for a, p in zip(accs, parts):
                a[...] += p

    row = lambda w: pl.BlockSpec((tm, w), lambda i: (i, 0))
    full = lambda r, c: pl.BlockSpec((r, c), lambda i: (0, 0))
    return pl.pallas_call(
        body, name=name, grid=(s // tm,),
        in_specs=[row(1024), row(1024), row(512)] + _mla_specs(tm),
        out_specs=[row(512), row(256), row(128), full(512, 1024), full(256, 1024), full(1, 512), full(1, 256),
                   full(1, 256), full(1, 256)],
        out_shape=[jax.ShapeDtypeStruct((s, 512), BF16), jax.ShapeDtypeStruct((s, 256), BF16),
                   jax.ShapeDtypeStruct((s, 128), BF16), jax.ShapeDtypeStruct((512, 1024), F32),
                   jax.ShapeDtypeStruct((256, 1024), F32), jax.ShapeDtypeStruct((1, 512), F32),
                   jax.ShapeDtypeStruct((1, 256), F32), jax.ShapeDtypeStruct((1, 256), F32),
                   jax.ShapeDtypeStruct((1, 256), F32)],
        scratch_shapes=[pltpu.VMEM((tm, 1024), BF16), pltpu.VMEM((tm, 1024), BF16)],
        compiler_params=_cparams("arbitrary"),
    )(dq, dk, dv, z, z, z, qg, wq, kvg, wkv, qng, kng, cos, sp, sn)


def _flash_fwd(q, k, v, *, name, tq=1024, tk=1024, rider=None):
    s = q.shape[0]
    tq, tk = min(tq, s), min(tk, s)
    nk = s // tk

    def body(q_ref, k_ref, v_ref, o_ref, lse_ref, m_s, l_s, acc):
        j = pl.program_id(2)

        @pl.when(j == 0)
        def _():
            m_s[...] = jnp.full_like(m_s, -jnp.inf)
            l_s[...] = jnp.zeros_like(l_s)
            acc[...] = jnp.zeros_like(acc)

        sc = _dot(q_ref[...], k_ref[...], 1, 1)
        m_prev = m_s[...]
        m_new = jnp.maximum(m_prev, jnp.max(sc, axis=-1, keepdims=True))
        p = jnp.exp(sc - m_new[:, 0:1])
        alpha = jnp.exp(m_prev - m_new)
        l_s[...] = alpha * l_s[...] + jnp.sum(p, axis=-1, keepdims=True)
        acc[...] = alpha * acc[...] + _dot(p, v_ref[...])
        m_s[...] = m_new

        @pl.when(j == nk - 1)
        def _():
            o_ref[...] = acc[...] / l_s[...]
            lse_ref[...] = m_s[...] + jnp.log(l_s[...])

    (o, lse), rode = _ride_call(
        body, rider, name=name, grid=(MLA_HEADS, s // tq, nk),
        in_specs=[pl.BlockSpec((tq, MLA_QKP), lambda h, i, j: (i, h)),
                  pl.BlockSpec((tk, MLA_QKP), lambda h, i, j: (j, h)),
                  pl.BlockSpec((tk, MLA_V), lambda h, i, j: (j, h))],
        out_specs=[pl.BlockSpec((tq, MLA_V), lambda h, i, j: (i, h))] * 2,
        out_shape=[jax.ShapeDtypeStruct((s, GROUP_W), F32)] * 2,
        scratch_shapes=[pltpu.VMEM((tq, MLA_V), F32), pltpu.VMEM((tq, MLA_V), F32), pltpu.VMEM((tq, MLA_V), F32)],
        args=(q, k, v), sem=("parallel", "parallel", "arbitrary"))
    return (o, lse) if rider is None else (o, lse, rode)


def _flash_bwd(q, k, v, do, o, lse, *, name, tq=1024, tk=1024, rider=None):
    s = q.shape[0]
    tq, tk = min(tq, s), min(tk, s)
    nq, nk = s // tq, s // tk

    def body(q_ref, k_ref, v_ref, do_ref, o_ref, lse_ref, dq_ref, dk_ref, dv_ref, dk_acc, dv_acc):
        j, i = pl.program_id(1), pl.program_id(2)
        dov = do_ref[...]
        delta = jnp.sum(dov * o_ref[...], axis=-1, keepdims=True)
        p = jnp.exp(_dot(q_ref[...], k_ref[...], 1, 1) - lse_ref[:, 0:1])
        ds = p * (_dot(dov, v_ref[...], 1, 1) - delta)
        pv = _dot(p, dov, 0, 0)
        pk = _dot(ds, q_ref[...], 0, 0)
        pq = _dot(ds, k_ref[...])
        rows = pl.ds(pl.multiple_of(i * tq, tq), tq)

        @pl.when(j == 0)
        def _():
            dq_ref[rows, :] = pq

        @pl.when(j > 0)
        def _():
            dq_ref[rows, :] += pq

        @pl.when(i == 0)
        def _():
            dv_acc[...] = pv
            dk_acc[...] = pk

        @pl.when(i > 0)
        def _():
            dv_acc[...] += pv
            dk_acc[...] += pk

        @pl.when(i == nq - 1)
        def _():
            dk_ref[...] = dk_acc[...]
            dv_ref[...] = dv_acc[...]

    qb = pl.BlockSpec((tq, MLA_QKP), lambda h, j, i: (i, h))
    kb = pl.BlockSpec((tk, MLA_QKP), lambda h, j, i: (j, h))
    vb = pl.BlockSpec((tk, MLA_V), lambda h, j, i: (j, h))
    ob = pl.BlockSpec((tq, MLA_V), lambda h, j, i: (i, h))
    (dq, dk, dv), rode = _ride_call(
        body, rider, name=name, grid=(MLA_HEADS, nk, nq),
        in_specs=[qb, kb, vb, ob, ob, ob],
        out_specs=[pl.BlockSpec((s, MLA_QKP), lambda h, j, i: (0, h)), kb, vb],
        out_shape=[jax.ShapeDtypeStruct((s, MLA_HEADS * MLA_QKP), F32),
                   jax.ShapeDtypeStruct((s, MLA_HEADS * MLA_QKP), F32), jax.ShapeDtypeStruct((s, GROUP_W), F32)],
        scratch_shapes=[pltpu.VMEM((tk, MLA_QKP), F32), pltpu.VMEM((tk, MLA_V), F32)],
        args=(q, k, v, do, o, lse), sem=("arbitrary", "arbitrary", "arbitrary"))
    return (dq, dk, dv) if rider is None else (dq, dk, dv, rode)


def _rows_tile(r, c, itemsize=4, budget=2 * 1024 * 1024):
    if r * c * itemsize <= budget:
        return r
    best = None
    for t in range(8, r, 8):
        if r % t == 0 and t * c * itemsize <= budget:
            best = t
    return best if best is not None else r


def _landing(into, tm, width):
    buf, col = into
    assert col % width == 0
    return (jax.ShapeDtypeStruct(buf.shape, buf.dtype), pl.BlockSpec((tm, width), lambda i: (i, col // width)),
            [ANY], [buf])


def _add_n(arrs, *, out_dtype=F32, name, into=None):
    shape = arrs[0].shape
    c = shape[-1]
    flat = [a.reshape(-1, c) for a in arrs]
    r = flat[0].shape[0]
    t = _rows_tile(r, c)
    n_in = len(flat)

    def body(*refs):
        acc = refs[0][...].astype(F32)
        for ref in refs[1:n_in]:
            acc = acc + ref[...].astype(F32)
        refs[-1][...] = acc.astype(out_dtype)

    blk = pl.BlockSpec((t, c), lambda i: (i, 0))
    if into is not None:
        out_shape, out_spec, more_specs, more_args = _landing(into, t, c)
        return pl.pallas_call(
            body, name=name, grid=(r // t,), in_specs=[blk] * n_in + more_specs, out_specs=out_spec,
            out_shape=out_shape, input_output_aliases={n_in: 0}, compiler_params=_cparams("parallel"),
        )(*flat, *more_args)
    out = pl.pallas_call(
        body, name=name, grid=(r // t,), in_specs=[blk] * n_in, out_specs=blk,
        out_shape=jax.ShapeDtypeStruct((r, c), out_dtype), compiler_params=_cparams("parallel"),
    )(*flat)
    return out.reshape(shape)


def _adamw(w, g, m, v, *, name):
    shape = w.shape
    c = shape[-1]
    flat = [a.reshape(-1, c) for a in (w, g, m, v)]
    r = flat[0].shape[0]
    t = _rows_tile(r, c, budget=1024 * 1024)

    def body(w_ref, g_ref, m_ref, v_ref, d_ref, mo_ref, vo_ref):
        gv = g_ref[...]
        m2 = ADAM_B1 * m_ref[...] + (1.0 - ADAM_B1) * gv
        v2 = ADAM_B2 * v_ref[...] + (1.0 - ADAM_B2) * (gv * gv)
        m_hat = m2 / (1.0 - ADAM_B1 ** ADAM_STEP)
        v_hat = v2 / (1.0 - ADAM_B2 ** ADAM_STEP)
        d_ref[...] = -ADAM_LR * (m_hat / (jnp.sqrt(v_hat) + ADAM_EPS) + ADAM_WD * w_ref[...])
        mo_ref[...] = m2
        vo_ref[...] = v2

    blk = pl.BlockSpec((t, c), lambda i: (i, 0))
    outs = pl.pallas_call(
        body, name=name, grid=(r // t,), in_specs=[blk] * 4, out_specs=[blk] * 3,
        out_shape=[jax.ShapeDtypeStruct((r, c), F32)] * 3, compiler_params=_cparams("parallel"),
    )(*flat)
    return tuple(o.reshape(shape) for o in outs)


def _place():
    x, y, c = lax.axis_index("x"), lax.axis_index("y"), lax.axis_index("c")
    chips = [(1 - x, y), (x, 1 - y), (1 - x, 1 - y)]
    return x, y, c, chips


ANY = pl.BlockSpec(memory_space=pl.ANY)


def _half(ref, axis, hc, lead=()):
    n = ref.shape[len(lead) + axis] // 2
    return ref.at[tuple(lead) + (slice(None),) * axis + (pl.ds(hc * n, n),)]


def _gather_shards(shards, axes, *, name):
    nt = len(shards)

    def body(*refs):
        src, dst = refs[:nt], refs[nt:2 * nt]
        send, recv, fsend, frecv, lsem = refs[2 * nt:]
        x, y, c, chips = _place()
        me = 2 * x + y
        local = [pltpu.make_async_copy(src[t], dst[t].at[me], lsem.at[t]) for t in range(nt)]
        for cp in local:
            cp.start()

        def half(t, slot, hc):
            return _half(dst[t], axes[t], hc, lead=(slot,))

        def first(t, k):
            return pltpu.make_async_remote_copy(
                src_ref=_half(src[t], axes[t], c), dst_ref=half(t, me, c),
                send_sem=send.at[t, k], recv_sem=recv.at[t, k],
                device_id=(chips[k][0], chips[k][1], c), device_id_type=MESH)

        def landed(t, k):
            slot = 2 * chips[k][0] + chips[k][1]
            return pltpu.make_async_remote_copy(
                src_ref=half(t, slot, c), dst_ref=half(t, slot, c),
                send_sem=send.at[t, k], recv_sem=recv.at[t, k],
                device_id=(chips[k][0], chips[k][1], c), device_id_type=MESH)

        def forward(t, k, hc):
            slot = 2 * chips[k][0] + chips[k][1]
            return pltpu.make_async_remote_copy(
                src_ref=half(t, slot, hc), dst_ref=half(t, slot, hc),
                send_sem=fsend.at[t, k], recv_sem=frecv.at[t, k],
                device_id=(x, y, 1 - c), device_id_type=MESH)

        for t in range(nt):
            for k in range(3):
                first(t, k).start()
        for t in range(nt):
            for k in range(3):
                landed(t, k).wait_recv()
                forward(t, k, c).start()
        for t in range(nt):
            for k in range(3):
                forward(t, k, 1 - c).wait_recv()
        for t in range(nt):
            for k in range(3):
                first(t, k).wait_send()
                forward(t, k, c).wait_send()
        for cp in local:
            cp.wait()

    return pl.pallas_call(
        body, name=name, in_specs=[ANY] * nt, out_specs=[ANY] * nt,
        out_shape=[jax.ShapeDtypeStruct((N_CHIP,) + a.shape, a.dtype) for a in shards],
        scratch_shapes=[pltpu.SemaphoreType.DMA((nt, 3)), pltpu.SemaphoreType.DMA((nt, 3)),
                        pltpu.SemaphoreType.DMA((nt, 3)), pltpu.SemaphoreType.DMA((nt, 3)),
                        pltpu.SemaphoreType.DMA((nt,))],
    )(*shards)


def _comm_rows(hr, c, budget=2 * 1024 * 1024):
    if hr * c * 4 <= budget:
        return hr
    best = None
    for t in range(16, hr, 16):
        if hr % t == 0 and t * c * 4 <= budget:
            best = t
    return best if best is not None else hr


def _comm_cols(r, hc, budget=2 * 1024 * 1024):
    best = 128
    for t in range(128, hc + 1, 128):
        if hc % t == 0 and r * t * 4 <= budget:
            best = t
    return best


def _comm_chunks(shape, axis):
    r, cdim = shape
    if axis == 0:
        rc = _comm_rows(r // 2, cdim)
        nt = (r // 2) // rc
        return (rc, cdim), nt, (lambda h, t: (h * nt + t, 0))
    cc = _comm_cols(r, cdim // 2)
    nt = (cdim // 2) // cc
    return (r, cc), nt, (lambda h, t: (0, h * nt + t))


def _pair_reduce(g, where, axis, *, out_dtype, name):
    n_slot, r, cdim = g.shape
    blk_shape, nr, at = _comm_chunks((r, cdim), axis)
    steps = n_slot * nr
    half_shape = (r // 2, cdim) if axis == 0 else (r, cdim // 2)

    def body(w_ref, a_ref, b_ref, o_ref, land, send, recv, credit):
        x, y, c, _ = _place()
        sib = (x, y, 1 - c)
        i = pl.program_id(0) * nr + pl.program_id(1)
        s = lax.rem(i, 2)

        @pl.when(i >= 2)
        def _():
            pl.semaphore_wait(credit.at[s], 1)

        cp = pltpu.make_async_remote_copy(src_ref=b_ref.at[0], dst_ref=land.at[s], send_sem=send.at[s],
                                          recv_sem=recv.at[s], device_id=sib, device_id_type=MESH)
        cp.start()
        cp.wait_recv()
        o_ref[0] = (a_ref[0] + land[s]).astype(out_dtype)
        cp.wait_send()

        @pl.when(i + 2 < steps)
        def _():
            pl.semaphore_signal(credit.at[s], inc=1, device_id=sib, device_id_type=MESH)

    blk = lambda half: pl.BlockSpec((1,) + blk_shape, lambda j, t, w: (j,) + at(half(w), t))
    grid_spec = pltpu.PrefetchScalarGridSpec(
        num_scalar_prefetch=1, grid=(n_slot, nr),
        in_specs=[blk(lambda w: w[0]), blk(lambda w: 1 - w[0])],
        out_specs=pl.BlockSpec((1,) + blk_shape, lambda j, t, w: (j,) + at(0, t)),
        scratch_shapes=[pltpu.VMEM((2,) + blk_shape, F32), pltpu.SemaphoreType.DMA((2,)),
                        pltpu.SemaphoreType.DMA((2,)), pltpu.SemaphoreType.REGULAR((2,))])
    return pl.pallas_call(
        body, name=name, grid_spec=grid_spec, out_shape=jax.ShapeDtypeStruct((n_slot,) + half_shape, out_dtype),
        compiler_params=_cparams("arbitrary", "arbitrary"),
    )(where, g, g)


def _chip_exchange(parts, *, name):
    nt = len(parts)

    def body(*refs):
        src, got = refs[:nt], refs[nt:2 * nt]
        send, recv = refs[2 * nt:]
        x, y, c, chips = _place()
        remote = []
        for t in range(nt):
            for k in range(3):
                remote.append(pltpu.make_async_remote_copy(
                    src_ref=src[t].at[2 * chips[k][0] + chips[k][1]], dst_ref=got[t].at[k],
                    send_sem=send.at[t, k], recv_sem=recv.at[t, k],
                    device_id=(chips[k][0], chips[k][1], c), device_id_type=MESH))
        for cp in remote:
            cp.start()
        for cp in remote:
            cp.wait_recv()
        for cp in remote:
            cp.wait_send()

    return pl.pallas_call(
        body, name=name, in_specs=[ANY] * nt, out_specs=[ANY] * nt,
        out_shape=[jax.ShapeDtypeStruct((3,) + a.shape[1:], a.dtype) for a in parts],
        scratch_shapes=[pltpu.SemaphoreType.DMA((nt, 3)), pltpu.SemaphoreType.DMA((nt, 3))],
    )(*parts)


def _sum_join(p, got, where, axis, *, name):
    _, hr, cdim = p.shape
    full = (2 * hr, cdim) if axis == 0 else (hr, 2 * cdim)
    blk_shape, n, at = _comm_chunks(full, axis)
    step_len = blk_shape[axis]
    half_len = full[axis] // 2

    def body(w_ref, p_ref, g_ref, out, buf, lsem, ssem, rsem):
        x, y, c, _ = _place()
        sib = (x, y, 1 - c)
        r = pl.program_id(0)

        def part(start, size):
            return out.at[(slice(None),) * axis + (pl.ds(start, size),)]

        def copies(step, slot):
            rows = part(pl.multiple_of(c * half_len + step * step_len, 8 if axis == 0 else 128), step_len)
            return (pltpu.make_async_copy(buf.at[slot], rows, lsem.at[slot]),
                    pltpu.make_async_remote_copy(src_ref=buf.at[slot], dst_ref=rows, send_sem=ssem.at[slot],
                                                 recv_sem=rsem, device_id=sib, device_id_type=MESH))

        s = lax.rem(r, 2)

        @pl.when(r >= 2)
        def _():
            lc, rm = copies(r - 2, s)
            lc.wait()
            rm.wait_send()

        buf[s] = p_ref[0].astype(F32) + g_ref[0].astype(F32) + g_ref[1].astype(F32) + g_ref[2].astype(F32)
        lc, rm = copies(r, s)
        lc.start()
        rm.start()

        @pl.when(r == n - 1)
        def _():
            for step in range(max(0, n - 2), n):
                lc, rm = copies(step, step % 2)
                lc.wait()
                rm.wait_send()
            whole = part(0, half_len)
            pltpu.make_async_remote_copy(src_ref=whole, dst_ref=whole, send_sem=ssem.at[0], recv_sem=rsem,
                                         device_id=sib, device_id_type=MESH).wait_recv()

    grid_spec = pltpu.PrefetchScalarGridSpec(
        num_scalar_prefetch=1, grid=(n,),
        in_specs=[pl.BlockSpec((1,) + blk_shape, lambda t, w: (w[1],) + at(0, t)),
                  pl.BlockSpec((3,) + blk_shape, lambda t, w: (0,) + at(0, t))],
        out_specs=ANY,
        scratch_shapes=[pltpu.VMEM((2,) + blk_shape, F32), pltpu.SemaphoreType.DMA((2,)),
                        pltpu.SemaphoreType.DMA((2,)), pltpu.SemaphoreType.DMA])
    return pl.pallas_call(
        body, name=name, grid_spec=grid_spec, out_shape=jax.ShapeDtypeStruct(full, F32),
        compiler_params=_cparams("arbitrary"),
    )(where, p, got)


def _rider_gather_send(shards, axes):
    nt = len(shards)

    def copies(src, dst, send, recv, lsem):
        x, y, c, chips = _place()
        me = 2 * x + y
        local = [pltpu.make_async_copy(src[t], dst[t].at[me], lsem.at[t]) for t in range(nt)]
        out, landed = [], []
        for t in range(nt):
            for k in range(3):
                peer = (chips[k][0], chips[k][1], c)
                out.append(pltpu.make_async_remote_copy(
                    src_ref=_half(src[t], axes[t], c), dst_ref=_half(dst[t], axes[t], c, lead=(me,)),
                    send_sem=send.at[t, k], recv_sem=recv.at[t, k], device_id=peer, device_id_type=MESH))
                theirs = _half(dst[t], axes[t], c, lead=(2 * chips[k][0] + chips[k][1],))
                landed.append(pltpu.make_async_remote_copy(
                    src_ref=theirs, dst_ref=theirs, send_sem=send.at[t, k], recv_sem=recv.at[t, k],
                    device_id=peer, device_id_type=MESH))
        return local, out, landed

    def start(src, dst, sems):
        local, out, _ = copies(src, dst, *sems)
        for cp in local + out:
            cp.start()

    def finish(src, dst, sems):
        local, out, landed = copies(src, dst, *sems)
        for cp in landed:
            cp.wait_recv()
        for cp in out:
            cp.wait_send()
        for cp in local:
            cp.wait()

    return _Rider(shards, [jax.ShapeDtypeStruct((N_CHIP,) + a.shape, a.dtype) for a in shards],
                  [pltpu.SemaphoreType.DMA((nt, 3)), pltpu.SemaphoreType.DMA((nt, 3)), pltpu.SemaphoreType.DMA((nt,))],
                  start, finish)


def _rider_gather_forward(bufs, axes):
    nt = len(bufs)

    def copies(src, dst, send, recv):
        x, y, c, chips = _place()
        mine, theirs = [], []
        for t in range(nt):
            for k in range(3):
                slot = 2 * chips[k][0] + chips[k][1]
                for hc, into in ((c, mine), (1 - c, theirs)):
                    into.append(pltpu.make_async_remote_copy(
                        src_ref=_half(src[t], axes[t], hc, lead=(slot,)),
                        dst_ref=_half(dst[t], axes[t], hc, lead=(slot,)),
                        send_sem=send.at[t, k], recv_sem=recv.at[t, k], device_id=(x, y, 1 - c), device_id_type=MESH))
        return mine, theirs

    def start(src, dst, sems):
        for cp in copies(src, dst, *sems)[0]:
            cp.start()

    def finish(src, dst, sems):
        mine, theirs = copies(src, dst, *sems)
        for cp in theirs:
            cp.wait_recv()
        for cp in mine:
            cp.wait_send()

    return _Rider(bufs, [jax.ShapeDtypeStruct(a.shape, a.dtype) for a in bufs],
                  [pltpu.SemaphoreType.DMA((nt, 3)), pltpu.SemaphoreType.DMA((nt, 3))], start, finish,
                  aliases={t: t for t in range(nt)})


def _rider_chip_exchange(parts):
    nt = len(parts)

    def copies(src, got, send, recv):
        x, y, c, chips = _place()
        return [pltpu.make_async_remote_copy(
            src_ref=src[t].at[2 * chips[k][0] + chips[k][1]], dst_ref=got[t].at[k], send_sem=send.at[t, k],
            recv_sem=recv.at[t, k], device_id=(chips[k][0], chips[k][1], c), device_id_type=MESH)
            for t in range(nt) for k in range(3)]

    def start(src, got, sems):
        for cp in copies(src, got, *sems):
            cp.start()

    def finish(src, got, sems):
        remote = copies(src, got, *sems)
        for cp in remote:
            cp.wait_recv()
        for cp in remote:
            cp.wait_send()

    return _Rider(parts, [jax.ShapeDtypeStruct((3,) + a.shape[1:], a.dtype) for a in parts],
                  [pltpu.SemaphoreType.DMA((nt, 3)), pltpu.SemaphoreType.DMA((nt, 3))], start, finish)


def _gather_all(block, *, name):
    m_per, n = block.shape

    def body(x_ref, out_ref, send_sems, recv_sems, local_sem):
        x, y, c, chips = _place()
        me, sibling = (x, y, c), (x, y, 1 - c)

        def rows(px, py, pc):
            return out_ref.at[4 * px + 2 * py + pc]

        def copy(k, blk, to, src=None):
            return pltpu.make_async_remote_copy(
                src_ref=rows(*blk) if src is None else src, dst_ref=rows(*blk),
                send_sem=send_sems.at[k], recv_sem=recv_sems.at[k], device_id=to, device_id_type=MESH)

        mine = pltpu.make_async_copy(x_ref, rows(*me), local_sem)
        mine.start()
        first = [copy(0, me, sibling, src=x_ref)]
        first += [copy(1 + j, me, (*chip, c), src=x_ref) for j, chip in enumerate(chips)]
        for cp in first:
            cp.start()
        passed = [copy(4 + j, (*chip, c), sibling) for j, chip in enumerate(chips)]
        for j, chip in enumerate(chips):
            copy(1 + j, (*chip, c), me).wait_recv()
            passed[j].start()
        copy(0, sibling, me).wait_recv()
        for j, chip in enumerate(chips):
            copy(4 + j, (*chip, 1 - c), me).wait_recv()
        for cp in first + passed:
            cp.wait_send()
        mine.wait()

    return pl.pallas_call(
        body, name=name,
        out_shape=jax.ShapeDtypeStruct((N_DEV, m_per, n), block.dtype),
        in_specs=[pl.BlockSpec(memory_space=pltpu.VMEM)], out_specs=pl.BlockSpec(memory_space=pltpu.VMEM),
        scratch_shapes=[pltpu.SemaphoreType.DMA((7,)), pltpu.SemaphoreType.DMA((7,)), pltpu.SemaphoreType.DMA],
        compiler_params=pltpu.CompilerParams(vmem_limit_bytes=VMEM_LIMIT),
    )(block)


def _sum_slots(slots, *, name):
    n, m, c = slots.shape
    t = _rows_tile(m, c * n)

    def body(s_ref, o_ref):
        acc = s_ref[0]
        for k in range(1, n):
            acc = acc + s_ref[k]
        o_ref[...] = acc

    return pl.pallas_call(
        body, name=name, grid=(m // t,), in_specs=[pl.BlockSpec((n, t, c), lambda i: (0, i, 0))],
        out_specs=pl.BlockSpec((t, c), lambda i: (i, 0)), out_shape=jax.ShapeDtypeStruct((m, c), F32),
        compiler_params=_cparams("parallel"),
    )(slots)


def _pad_rows(a, rows):
    return a if a.shape[0] == rows else jnp.pad(a, ((0, rows - a.shape[0]), (0, 0)))


def _w_in_padded(shards):
    full = shards.reshape(IN_COLS, shards.shape[2])
    return jnp.concatenate([_pad_rows(full[SEG[n][2]:SEG[n][2] + SEG[n][3]], SEG[n][1]) for n in SEG_ORDER], axis=0)


def _w_in_unpadded(gp):
    full = jnp.concatenate([gp[SEG[n][0]:SEG[n][0] + SEG[n][3]] for n in ORIG_ORDER], axis=0)
    return full.reshape(N_CHIP, IN_COLS // N_CHIP, gp.shape[1])


def _pad_heads(w, true_w, pad_w):
    r = w.shape[0]
    h = w.shape[1] // true_w
    return jnp.pad(w.reshape(r, h, true_w), ((0, 0), (0, 0), (0, pad_w - true_w))).reshape(r, h * pad_w)


def _unpad_heads(w, true_w, pad_w):
    r = w.shape[0]
    h = w.shape[1] // pad_w
    return w.reshape(r, h, pad_w)[:, :, :true_w].reshape(r, h * true_w)


def _cols_to_slots(a):
    return a.reshape(a.shape[0], N_CHIP, a.shape[1] // N_CHIP).transpose(1, 0, 2)


def _to_heads(a, h, d):
    return a.reshape(a.shape[0], h, d).transpose(1, 0, 2)


def _slots_to_cols(a):
    return jnp.concatenate([a[j] for j in range(N_CHIP)], axis=1)


SMALL = [("norm_g", 2048), ("ret_norm_g", 512), ("gla_ba_f", 256), ("gla_ba_b", 256), ("gla_norm_g", 512),
         ("pool_w", 4 * 128 * 128), ("pool_scale", 512), ("mla_q_norm_g", 512), ("mla_kv_norm_g", 256),
         ("mla_qk_norm_q", 192), ("mla_qk_norm_k", 192)]


def _pack_small(vals):
    parts = []
    for name, n in SMALL:
        parts += [v.reshape(-1) for v in vals[name]]
        if (DEPTH * n) % 1024:
            parts.append(jnp.zeros((-(DEPTH * n)) % 1024, F32))
    parts += [vals["loss"].reshape(-1), jnp.zeros(1023, F32)]
    return jnp.concatenate(parts).reshape(-1, 128)


def _unpack_small(block):
    flat = block.reshape(-1)
    out, off = {}, 0
    for name, n in SMALL:
        out[name] = flat[off:off + DEPTH * n]
        off += DEPTH * n + (-(DEPTH * n)) % 1024
    out["loss"] = flat[off]
    return out


def _layer_weights(l, p, g):
    wa = jnp.zeros((128, 512), F32)
    wa = wa.at[0:GLA_RANK, 0:256].set(_slots_to_cols(g["gla_wa2_f"]))
    wa = wa.at[GLA_RANK:2 * GLA_RANK, 256:512].set(_slots_to_cols(g["gla_wa2_b"]))
    return dict(
        norm_g=p["norm_g"][l][None, :],
        w_in=_w_in_padded(g["w_in"]),
        w_out=g["w_out"].reshape(4 * g["w_out"].shape[1], -1),
        ret_norm_g=p["ret_norm_g"][l][None, :],
        wa=_bf(wa),
        ba=jnp.concatenate([p["gla_ba_f"][l], p["gla_ba_b"][l]])[None, :],
        gla_norm_g=p["gla_norm_g"][l][None, :],
        pool_w=_bf(p["pool_w"][l]),
        pool_scale=p["pool_scale"][l][None, :],
        qg=p["mla_q_norm_g"][l][None, :],
        wq=_pad_heads(_slots_to_cols(g["mla_wq_b"]), MLA_QK, MLA_QKP),
        kvg=p["mla_kv_norm_g"][l][None, :],
        wkv=_slots_to_cols(g["mla_wkv_b"]),
        qng=jnp.pad(p["mla_qk_norm_q"][l], (0, MLA_QKP - MLA_QK))[None, :],
        kng=jnp.pad(p["mla_qk_norm_k"][l], (0, MLA_QKP - MLA_QK))[None, :],
    )


def _layer_fwd(l, x, w, tabs, next_shards=None, loss_target=None):
    ret_cos, ret_sin, mla_cos, mla_sp, mla_sn = tabs
    nm = lambda s: f"l{l}_{s}"
    h = _rmsnorm_fwd(x, w["norm_g"], name=nm("norm"))
    if next_shards is None:
        z = _matmul(h, w["w_in"], tb=True, name=nm("in_proj"))
    else:
        z, landed = _matmul(h, w["w_in"], tb=True, rider=_rider_gather_send(next_shards[:1], SHARD_AXES[:1]),
                            name=nm("in_proj"))
    qr, kr = _ret_pre(z, ret_cos, ret_sin, name=nm("ret_pre"))
    ret_o = _bla(qr, kr, z, _ret_log_gamma(False), (0, 0, SEG["rv"][0] // 512), name=nm("ret_scan"))
    y_a = _post(ret_o, z, SEG["rg"][0] // 512, w["ret_norm_g"], norm=True, name=nm("ret_post"))
    la = _gla_gate(z, w["wa"], w["ba"], name=nm("gla_gate"))
    la_h = la.reshape(la.shape[0], 2, GLA_HEADS, GLA_DK).transpose(1, 2, 0, 3)
    gq = _to_heads(z[:, SEG["gq"][0]:SEG["gq"][0] + 256], GLA_HEADS, GLA_DK)
    gk = _to_heads(z[:, SEG["gk"][0]:SEG["gk"][0] + 256], GLA_HEADS, GLA_DK)
    if next_shards is None:
        gla_o, gla_st = _gla_fwd(gq, gk, z, la_h, name=nm("gla_scan"))
    else:
        gla_o, gla_st, more = _gla_fwd(gq, gk, z, la_h, rider=_rider_gather_send(next_shards[1:], SHARD_AXES[1:]),
                                       name=nm("gla_scan"))
        landed = list(landed) + list(more)
    y_b = _post(gla_o, z, SEG["gg"][0] // 512, w["gla_norm_g"], norm=True, name=nm("gla_post"))
    y_c = _pool_fwd(z, w["pool_w"], w["pool_scale"], name=nm("pool"))
    q, k, v = _mla_pre(z, w["qg"], w["wq"], w["kvg"], w["wkv"], w["qng"], w["kng"], mla_cos, mla_sp, mla_sn,
                       name=nm("mla_pre"))
    if next_shards is None:
        (att_o, lse), gathered = _flash_fwd(q, k, v, name=nm("attn")), None
    else:
        att_o, lse, gathered = _flash_fwd(q, k, v, rider=_rider_gather_forward(landed, SHARD_AXES), name=nm("attn"))
    y_d = _post([att_o], z, SEG["mg"][0] // 512, w["qg"], norm=False, name=nm("mla_post"))
    y = jnp.concatenate([y_a, y_b, y_c, y_d], axis=1)
    if loss_target is None:
        x_next = _matmul(y, w["w_out"], add=x, name=nm("out_proj"))
    else:
        x_next = _out_proj_loss(y, w["w_out"], x, loss_target, name=nm("out_proj"))
    saved = dict(x=x, h=h, z=z, y=y, qr=qr, kr=kr, ret_o=ret_o, la_h=la_h, gq=gq, gk=gk, gla_o=gla_o, gla_st=gla_st,
                 q=q, k=k, v=v, att_o=att_o, lse=lse)
    return x_next, saved, gathered


def _layer_bwd(l, dx_next, w, sv, tabs, riding_parts=None, where=None):
    ret_cos, ret_sin, mla_cos, mla_sp, mla_sn = tabs
    nm = lambda s: f"l{l}_{s}"
    z = sv["z"]
    dy = _matmul(dx_next, w["w_out"], tb=True, name=nm("out_proj_dy"))
    d_w_out = _matmul(sv["y"], dx_next, ta=True, tn=512, name=nm("out_proj_dw"))
    d_w_out = d_w_out.reshape(N_CHIP, d_w_out.shape[0] // N_CHIP, d_w_out.shape[1])
    if where is not None:
        pair_w_out = _pair_reduce(d_w_out, where, 0, out_dtype=BF16, name=nm("pair_reduce_w_out"))
    dz = lax.empty((z.shape[0], IN_PAD), BF16)
    at = lambda n: SEG[n][0]
    dz, d_ret_o, d_ret_g = _post_bwd(dy, 0, sv["ret_o"], z, SEG["rg"][0] // 512, w["ret_norm_g"], (dz, at("rg")),
                                     norm=True, name=nm("ret_post_bwd"))
    vcol = SEG["rv"][0] // 512
    dqr, dkr, drv = _bla_bwd(d_ret_o, z, sv["qr"], sv["kr"], vcol, name=nm("ret_scan_bwd"))
    dz = _ret_pre_bwd(dqr, dkr, ret_cos, ret_sin, (dz, at("rq")), name=nm("ret_pre_bwd"))
    dz = _add_n([drv[0], drv[1]], out_dtype=BF16, into=(dz, at("rv")), name=nm("ret_dv_sum"))
    dz, d_gla_o, d_gla_g = _post_bwd(dy, 1, sv["gla_o"], z, SEG["gg"][0] // 512, w["gla_norm_g"], (dz, at("gg")),
                                     norm=True, name=nm("gla_post_bwd"))
    if where is None:
        dq2, dk2, dla2, dv2 = _gla_bwd(sv["gq"], sv["gk"], z, sv["la_h"], d_gla_o, sv["gla_st"],
                                       name=nm("gla_scan_bwd"))
    else:
        dq2, dk2, dla2, dv2, (others_w_out,) = _gla_bwd(
            sv["gq"], sv["gk"], z, sv["la_h"], d_gla_o, sv["gla_st"], rider=_rider_chip_exchange([pair_w_out]),
            name=nm("gla_scan_bwd"))
        d_w_out = (pair_w_out, others_w_out)
    d_gq = _bf(dq2[0] + dq2[1])
    d_gk = _bf(dk2[0] + dk2[1])
    dz = _add_n([dv2[0], dv2[1]], out_dtype=BF16, into=(dz, at("gv")), name=nm("gla_dv_sum"))
    dz, d_wa, d_ba = _gla_gate_bwd(dla2, z, w["wa"], w["ba"], (dz, at("ga")), name=nm("gla_gate_bwd"))
    d_pv, d_pg, d_pool_w, d_pool_scale = _pool_bwd(dy, z, w["pool_w"], w["pool_scale"], name=nm("pool_bwd"))
    dz, d_att_o, _ = _post_bwd(dy, 3, [sv["att_o"]], z, SEG["mg"][0] // 512, w["qg"], (dz, at("mg")), norm=False,
                               name=nm("mla_post_bwd"))
    if riding_parts is None:
        (dq, dk, dv), rode = _flash_bwd(sv["q"], sv["k"], sv["v"], d_att_o, sv["att_o"], sv["lse"],
                                        name=nm("attn_bwd")), None
    else:
        dq, dk, dv, rode = _flash_bwd(sv["q"], sv["k"], sv["v"], d_att_o, sv["att_o"], sv["lse"],
                                      rider=_rider_chip_exchange(riding_parts), name=nm("attn_bwd"))
    d_mq, d_mkv, d_mkr, d_wq, d_wkv, d_qg, d_kvg, d_qng, d_kng = _mla_pre_bwd(
        dq, dk, dv, z, w["qg"], w["wq"], w["kvg"], w["wkv"], w["qng"], w["kng"], mla_cos, mla_sp, mla_sn,
        name=nm("mla_pre_bwd"))
    for n, seg in dict(pv=d_pv, pg=d_pg, mq=d_mq, gq=d_gq, gk=d_gk, mkv=d_mkv, mkr=d_mkr).items():
        dz = lax.dynamic_update_slice(dz, seg, (0, at(n)))
    dh = _matmul(dz, w["w_in"], tn=512, name=nm("in_proj_dh"))
    d_w_in = _matmul(dz, sv["h"], ta=True, name=nm("in_proj_dw"))
    dx, d_norm_g = _rmsnorm_bwd(sv["x"], dh, w["norm_g"], dx_next, name=nm("norm_bwd"))
    sharded = dict(
        w_in=_w_in_unpadded(d_w_in),
        w_out=d_w_out,
        mla_wq_b=_cols_to_slots(_unpad_heads(d_wq, MLA_QK, MLA_QKP)),
        mla_wkv_b=_cols_to_slots(d_wkv),
        gla_wa2_f=_cols_to_slots(d_wa[0:GLA_RANK, 0:256]),
        gla_wa2_b=_cols_to_slots(d_wa[GLA_RANK:2 * GLA_RANK, 256:512]),
    )
    small = dict(
        norm_g=d_norm_g[0], ret_norm_g=d_ret_g[0], gla_ba_f=d_ba[0, :256], gla_ba_b=d_ba[0, 256:],
        gla_norm_g=d_gla_g[0], pool_w=d_pool_w.reshape(-1), pool_scale=d_pool_scale[0], mla_q_norm_g=d_qg[0],
        mla_kv_norm_g=d_kvg[0], mla_qk_norm_q=d_qng[0, :MLA_QK], mla_qk_norm_k=d_kng[0, :MLA_QK],
    )
    return dx, sharded, small, rode


SHARDED = ["w_in", "w_out", "mla_wq_b", "mla_wkv_b", "gla_wa2_f", "gla_wa2_b"]
WEIGHTS = ["norm_g", "w_in", "ret_norm_g", "gla_wa2_f", "gla_ba_f", "gla_wa2_b", "gla_ba_b", "gla_norm_g", "pool_w",
           "pool_scale", "mla_q_norm_g", "mla_wq_b", "mla_kv_norm_g", "mla_wkv_b", "mla_qk_norm_q", "mla_qk_norm_k",
           "w_out"]


SHARD_AXES = [1, 0, 0, 0, 0, 0]


def _layer_shards(p, l):
    return [jnp.swapaxes(p["w_in"], 1, 2)[l].astype(BF16), p["w_out"][l].astype(BF16), p["mla_wq_b"][l].astype(BF16),
            p["mla_wkv_b"][l].astype(BF16), p["gla_wa2_f"][l], p["gla_wa2_b"][l]]


def _step(p, where):
    x = p["x"][0]
    tabs = _rope_tables(x.shape[0])
    got0 = _gather_shards(_layer_shards(p, 0), SHARD_AXES, name="l0_gather_weights")
    w0 = _layer_weights(0, p, dict(zip(SHARDED, got0)))
    x1, sv0, got1 = _layer_fwd(0, x, w0, tabs, next_shards=_layer_shards(p, 1))
    w1 = _layer_weights(1, p, dict(zip(SHARDED, got1)))
    (dx, loss), sv1, _ = _layer_fwd(1, x1, w1, tabs, loss_target=p["loss_target"][0])

    big, big_axes = SHARDED[:2], SHARD_AXES[:2]

    def pair_sums(tag, tensors, axes, names):
        return [_pair_reduce(a, where, ax, out_dtype=BF16, name=f"{tag}_pair_reduce_{n}")
                for a, ax, n in zip(tensors, axes, names)]

    def joined(tag, pair, others, axes, names):
        return [_sum_join(a, b, where, ax, name=f"{tag}_sum_join_{n}")
                for a, b, ax, n in zip(pair, others, axes, names)]

    dx, sharded1, small1, _ = _layer_bwd(1, dx, w1, sv1, tabs)
    pair1 = pair_sums("l1", [sharded1[n] for n in big], big_axes, big)
    dx, sharded0, small0, others1 = _layer_bwd(0, dx, w0, sv0, tabs, riding_parts=pair1, where=where)
    grads1 = joined("l1", pair1, others1, big_axes, big)
    packed = jnp.concatenate([sh[n].reshape(N_CHIP, -1, 128) for sh in (sharded0, sharded1) for n in SHARDED[2:]],
                             axis=1)
    last, last_axes, last_names = [sharded0["w_in"], packed], [SHARD_AXES[0], 0], ["w_in", "rest"]
    pair0 = pair_sums("l0", last, last_axes, last_names)
    g_w_in0, rest = joined("l0", pair0, _chip_exchange(pair0, name="l0_chip_exchange"), last_axes, last_names)
    (g_w_out0,) = joined("l0", [sharded0["w_out"][0]], [sharded0["w_out"][1]], [SHARD_AXES[1]], ["w_out"])
    grads = {n: jnp.stack([g0, g1]) for n, g0, g1 in zip(big, (g_w_in0, g_w_out0), grads1)}
    off = 0
    pieces = {n: [] for n in SHARDED[2:]}
    for sh in (sharded0, sharded1):
        for n in SHARDED[2:]:
            rows = sh[n].shape[1] * sh[n].shape[2] // 128
            pieces[n].append(rest[off:off + rows].reshape(sh[n].shape[1:]))
            off += rows
    grads.update({n: jnp.stack(v) for n, v in pieces.items()})
    small = {n: [small0[n], small1[n]] for n, _ in SMALL}
    small["loss"] = loss
    return dx[None], grads, small


def kernel(x, norm_g, w_in, ret_norm_g, gla_wa2_f, gla_ba_f, gla_wa2_b, gla_ba_b, gla_norm_g, pool_w, pool_scale, mla_q_norm_g, mla_wq_b, mla_kv_norm_g, mla_wkv_b, mla_qk_norm_q, mla_qk_norm_k, w_out, loss_target, m_norm_g, m_w_in, m_ret_norm_g, m_gla_wa2_f, m_gla_ba_f, m_gla_wa2_b, m_gla_ba_b, m_gla_norm_g, m_pool_w, m_pool_scale, m_mla_q_norm_g, m_mla_wq_b, m_mla_kv_norm_g, m_mla_wkv_b, m_mla_qk_norm_q, m_mla_qk_norm_k, m_w_out, v_norm_g, v_w_in, v_ret_norm_g, v_gla_wa2_f, v_gla_ba_f, v_gla_wa2_b, v_gla_ba_b, v_gla_norm_g, v_pool_w, v_pool_scale, v_mla_q_norm_g, v_mla_wq_b, v_mla_kv_norm_g, v_mla_wkv_b, v_mla_qk_norm_q, v_mla_qk_norm_k, v_w_out):
    p = dict(x=x, norm_g=norm_g, w_in=w_in, ret_norm_g=ret_norm_g, gla_wa2_f=gla_wa2_f, gla_ba_f=gla_ba_f,
             gla_wa2_b=gla_wa2_b, gla_ba_b=gla_ba_b, gla_norm_g=gla_norm_g, pool_w=pool_w, pool_scale=pool_scale,
             mla_q_norm_g=mla_q_norm_g, mla_wq_b=mla_wq_b, mla_kv_norm_g=mla_kv_norm_g, mla_wkv_b=mla_wkv_b,
             mla_qk_norm_q=mla_qk_norm_q, mla_qk_norm_k=mla_qk_norm_k, w_out=w_out, loss_target=loss_target)
    moments = dict(
        m=dict(norm_g=m_norm_g, w_in=m_w_in, ret_norm_g=m_ret_norm_g, gla_wa2_f=m_gla_wa2_f, gla_ba_f=m_gla_ba_f,
               gla_wa2_b=m_gla_wa2_b, gla_ba_b=m_gla_ba_b, gla_norm_g=m_gla_norm_g, pool_w=m_pool_w,
               pool_scale=m_pool_scale, mla_q_norm_g=m_mla_q_norm_g, mla_wq_b=m_mla_wq_b,
               mla_kv_norm_g=m_mla_kv_norm_g, mla_wkv_b=m_mla_wkv_b, mla_qk_norm_q=m_mla_qk_norm_q,
               mla_qk_norm_k=m_mla_qk_norm_k, w_out=m_w_out),
        v=dict(norm_g=v_norm_g, w_in=v_w_in, ret_norm_g=v_ret_norm_g, gla_wa2_f=v_gla_wa2_f, gla_ba_f=v_gla_ba_f,
               gla_wa2_b=v_gla_wa2_b, gla_ba_b=v_gla_ba_b, gla_norm_g=v_gla_norm_g, pool_w=v_pool_w,
               pool_scale=v_pool_scale, mla_q_norm_g=v_mla_q_norm_g, mla_wq_b=v_mla_wq_b,
               mla_kv_norm_g=v_mla_kv_norm_g, mla_wkv_b=v_mla_wkv_b, mla_qk_norm_q=v_mla_qk_norm_q,
               mla_qk_norm_k=v_mla_qk_norm_k, w_out=v_w_out))

    where = jnp.stack([lax.axis_index("c"), 2 * lax.axis_index("x") + lax.axis_index("y")]).astype(jnp.int32)
    grad_x, grads, small = _step(p, where)

    slots = _gather_all(_pack_small(small), name="gather_small")
    total = _unpack_small(_sum_slots(slots, name="sum_small"))
    for n, _ in SMALL:
        grads[n] = total[n].reshape(p[n].shape)
    loss = total["loss"]

    delta, new_m, new_v = {}, {}, {}
    for n in WEIGHTS:
        turn = (lambda a: jnp.swapaxes(a, 1, 2)) if n == "w_in" else (lambda a: a)
        outs = _adamw(turn(p[n]), grads[n], turn(moments["m"][n]), turn(moments["v"][n]), name=f"adamw_{n}")
        grads[n] = turn(grads[n])
        delta[n], new_m[n], new_v[n] = (turn(o) for o in outs)
    return (loss, grad_x, *[grads[n] for n in WEIGHTS], *[delta[n] for n in WEIGHTS],
            *[new_m[n] for n in WEIGHTS], *[new_v[n] for n in WEIGHTS])
```
